```python
import jax, jax.numpy as jnp
from jax import lax
import numpy as np

D_MODEL = 1024
BATCH = 8
SEQ = 2048
DEPTH = 1

HGRN_HEADS = 8
HGRN_EXPAND = 128
HGRN_FWD = HGRN_HEADS * HGRN_EXPAND
HGRN_IN = D_MODEL
HGRN_VDIM = HGRN_IN // HGRN_HEADS
HGRN_SCALE = HGRN_EXPAND ** -0.5
CHUNK = 32
RWKV_HEAD = 64
RWKV_DIM = D_MODEL
RWKV_HEADS = RWKV_DIM // RWKV_HEAD
W_LORA = 64
A_LORA = 64
G_LORA = 128
GN_EPS = 1e-5 * RWKV_HEAD
D_FF = 2816
CONV_W = 3
EPS = 1e-6

HGRN_COLS = 2 * HGRN_FWD + 2 * HGRN_IN
RWKV_COLS = 3 * RWKV_DIM + W_LORA + A_LORA + G_LORA
GATE_COLS = 2 * D_MODEL
IN_COLS = HGRN_COLS + RWKV_COLS + GATE_COLS

kernel_name = "hgrn2_rwkv7_gated_hybrid_convffn"

F32 = jnp.float32


def _rmsnorm(x, g):
    xf = x.astype(F32)
    y = xf * lax.rsqrt(jnp.mean(xf * xf, axis=-1, keepdims=True) + EPS)
    return (y * g.astype(F32)).astype(x.dtype)


def _split(z, sizes):
    outs, off = [], 0
    for s in sizes:
        outs.append(z[..., off:off + s])
        off += s
    return outs


def _shift1(z):
    return jnp.pad(z[:, :-1], ((0, 0), (1, 0), (0, 0)))


def _causal_dwconv(h, w, b):
    S = h.shape[1]
    hp = jnp.pad(h, ((0, 0), (CONV_W - 1, 0), (0, 0)))
    out = b
    for j in range(CONV_W):
        out = out + w[j] * hp[:, j:j + S]
    return out


def _hgrn2_chunkwise(q, f_log, k, v):
    B, S, H, K = q.shape
    V = v.shape[-1]
    N = S // CHUNK

    def chunk(t):
        return t.reshape(B, N, CHUNK, H, t.shape[-1]).transpose(0, 3, 1, 2, 4)

    qc, gc, kc, vc = chunk(q), chunk(f_log), chunk(k), chunk(v)
    b = jnp.cumsum(gc, axis=3)
    b_ref = b[:, :, :, CHUNK // 2 - 1:CHUNK // 2, :]
    q_in = qc * jnp.exp(b - b_ref)
    k_in = kc * jnp.exp(b_ref - b)
    scores = jnp.einsum('bhnck,bhndk->bhncd', q_in, k_in)
    mask = jnp.tril(jnp.ones((CHUNK, CHUNK), dtype=bool))
    scores = jnp.where(mask, scores, 0.0)
    o_intra = jnp.einsum('bhncd,bhndv->bhncv', scores, vc)

    b_last = b[:, :, :, -1, :]
    u = jnp.einsum('bhnck,bhncv->bhnkv', kc * jnp.exp(b_last[:, :, :, None, :] - b), vc)
    decay = jnp.exp(b_last)

    def step(state, inp):
        d, u_n = inp
        return d[..., None] * state + u_n, state

    s0 = jnp.zeros((B, H, K, V), F32)
    _, s_prev = lax.scan(step, s0, (decay.transpose(2, 0, 1, 3), u.transpose(2, 0, 1, 3, 4)))
    s_prev = s_prev.transpose(1, 2, 0, 3, 4)
    o_inter = jnp.einsum('bhnck,bhnkv->bhncv', qc * jnp.exp(b), s_prev)
    o = o_intra + o_inter
    return o.transpose(0, 2, 3, 1, 4).reshape(B, S, H, V)


def _rwkv7_scan(r, w, k, v, a_vec, b_vec):
    B, S, H, N = r.shape

    def step(state, inp):
        r_t, w_t, k_t, v_t, a_t, b_t = inp
        sa = jnp.einsum('bhvk,bhk->bhv', state, a_t)
        state = (state * w_t[:, :, None, :] + sa[..., None] * b_t[:, :, None, :]
                 + v_t[..., None] * k_t[:, :, None, :])
        y = jnp.einsum('bhvk,bhk->bhv', state, r_t)
        return state, y

    xs = tuple(t.astype(F32).transpose(1, 0, 2, 3) for t in (r, w, k, v, a_vec, b_vec))
    _, y = lax.scan(step, jnp.zeros((B, H, N, N), F32), xs)
    return y.transpose(1, 0, 2, 3)


def _token_mixer(xn, lb, w_in, hgrn_gnorm, w_branch_a, rwkv_mu, rwkv_w0, rwkv_w2, rwkv_a0,
                 rwkv_a2, rwkv_g2, rwkv_k_k, rwkv_k_a, rwkv_r_k, rwkv_ln_w, rwkv_ln_b,
                 w_branch_b, w_out):
    B, S, _ = xn.shape
    z = xn @ w_in
    z_h, z_r, z_g = _split(z, [HGRN_COLS, RWKV_COLS, GATE_COLS])

    hq, hf, hi, hg = _split(z_h, [HGRN_FWD, HGRN_FWD, HGRN_IN, HGRN_IN])
    q = jax.nn.silu(hq.astype(F32)).reshape(B, S, HGRN_HEADS, HGRN_EXPAND) * HGRN_SCALE
    f = lb + (1.0 - lb) * jax.nn.sigmoid(hf.astype(F32))
    f = f.reshape(B, S, HGRN_HEADS, HGRN_EXPAND)
    k_h = 1.0 - f
    vi = hi.astype(F32).reshape(B, S, HGRN_HEADS, HGRN_VDIM)
    o_a = _hgrn2_chunkwise(q, jnp.log(f), k_h, vi)
    o_a = o_a * lax.rsqrt(jnp.mean(o_a * o_a, axis=-1, keepdims=True) + EPS)
    o_a = o_a * hgrn_gnorm.astype(F32).reshape(HGRN_HEADS, HGRN_VDIM)
    o_a = o_a.reshape(B, S, HGRN_IN) * jax.nn.silu(hg.astype(F32))
    y_a = o_a.astype(xn.dtype) @ w_branch_a

    z_r = z_r + rwkv_mu * (_shift1(z_r) - z_r)
    rr, kr, vr, wz, az, gz = _split(z_r, [RWKV_DIM, RWKV_DIM, RWKV_DIM, W_LORA, A_LORA, G_LORA])
    w_log = -jax.nn.softplus(-(rwkv_w0 + jnp.tanh(wz) @ rwkv_w2).astype(F32)) - 0.5
    decay = jnp.exp(-jnp.exp(w_log))
    a = jax.nn.sigmoid((rwkv_a0 + az @ rwkv_a2).astype(F32))
    g = jax.nn.sigmoid(gz) @ rwkv_g2
    kr = kr.astype(F32)
    kk = (kr * rwkv_k_k).reshape(B, S, RWKV_HEADS, RWKV_HEAD)
    kk = kk / jnp.maximum(jnp.linalg.norm(kk, axis=-1, keepdims=True), 1e-12)
    kr = kr * (1.0 + (a - 1.0) * rwkv_k_a)

    def heads(t):
        return t.astype(F32).reshape(B, S, RWKV_HEADS, RWKV_HEAD)

    r_h, k_r, v_r, w_h, a_h = heads(rr), heads(kr), heads(vr), heads(decay), heads(a)
    y = _rwkv7_scan(r_h, w_h, k_r, v_r, -kk, kk * a_h)
    mu_y = jnp.mean(y, axis=-1, keepdims=True)
    var_y = jnp.mean(jnp.square(y - mu_y), axis=-1, keepdims=True)
    y = ((y - mu_y) * lax.rsqrt(var_y + GN_EPS)).reshape(B, S, RWKV_DIM)
    y = y * rwkv_ln_w.astype(F32) + rwkv_ln_b.astype(F32)
    bonus = jnp.sum(r_h * k_r * rwkv_r_k.astype(F32), axis=-1, keepdims=True) * v_r
    o_b = (y + bonus.reshape(B, S, RWKV_DIM)) * g.astype(F32)
    y_b = o_b.astype(xn.dtype) @ w_branch_b

    ga, gb = _split(z_g, [D_MODEL, D_MODEL])
    merged = jax.nn.sigmoid(ga) * y_a + jax.nn.sigmoid(gb) * y_b
    return merged @ w_out


def _conv_ffn(xn, w_up, conv_w, conv_b, w_down):
    hu = xn @ w_up
    hc = _causal_dwconv(hu, conv_w, conv_b)
    gate, val = _split(hc, [D_FF, D_FF])
    return (jax.nn.silu(gate) * val) @ w_down


def _fwd_setup_inputs(seed: int = 0) -> dict:
    key = jax.random.key(seed)
    ks = jax.random.split(key, 32)
    L = DEPTH

    def nrm(k, shape, scale):
        return jax.random.normal(k, shape, F32) * scale

    def gain(k, shape):
        return 1.0 + 0.02 * jax.random.normal(k, shape, F32)

    return {
        "x": jax.random.normal(ks[0], (BATCH, SEQ, D_MODEL), F32),
        "attn_pre_norm": gain(ks[1], (L, D_MODEL)),
        "w_in": nrm(ks[2], (L, D_MODEL, IN_COLS), D_MODEL ** -0.5),
        "hgrn_lb": nrm(ks[3], (DEPTH + 1, HGRN_FWD), 0.1),
        "hgrn_gnorm": gain(ks[4], (L, HGRN_IN)),
        "w_branch_a": nrm(ks[5], (L, HGRN_IN, D_MODEL), HGRN_IN ** -0.5),
        "rwkv_mu": jax.random.uniform(ks[6], (L, RWKV_COLS), F32),
        "rwkv_w0": jax.random.uniform(ks[7], (L, RWKV_DIM), F32, minval=-6.0, maxval=0.0),
        "rwkv_w2": nrm(ks[8], (L, W_LORA, RWKV_DIM), 0.1 * W_LORA ** -0.5),
        "rwkv_a0": nrm(ks[9], (L, RWKV_DIM), 0.1),
        "rwkv_a2": nrm(ks[10], (L, A_LORA, RWKV_DIM), 0.5 * A_LORA ** -0.5),
        "rwkv_g2": nrm(ks[11], (L, G_LORA, RWKV_DIM), G_LORA ** -0.5),
        "rwkv_k_k": 0.85 + 0.02 * jax.random.normal(ks[12], (L, RWKV_DIM), F32),
        "rwkv_k_a": gain(ks[13], (L, RWKV_DIM)),
        "rwkv_r_k": nrm(ks[14], (L, RWKV_HEADS, RWKV_HEAD), 0.1),
        "rwkv_ln_w": gain(ks[15], (L, RWKV_DIM)),
        "rwkv_ln_b": nrm(ks[16], (L, RWKV_DIM), 0.01),
        "w_branch_b": nrm(ks[17], (L, RWKV_DIM, D_MODEL), RWKV_DIM ** -0.5),
        "w_out": nrm(ks[18], (L, D_MODEL, D_MODEL), D_MODEL ** -0.5),
        "attn_post_norm": gain(ks[19], (L, D_MODEL)),
        "ffn_pre_norm": gain(ks[20], (L, D_MODEL)),
        "w_up": nrm(ks[21], (L, D_MODEL, 2 * D_FF), D_MODEL ** -0.5),
        "conv_w": nrm(ks[22], (L, CONV_W, 2 * D_FF), CONV_W ** -0.5),
        "conv_b": nrm(ks[23], (L, 2 * D_FF), 0.01),
        "w_down": nrm(ks[24], (L, D_FF, D_MODEL), D_FF ** -0.5),
        "ffn_post_norm": gain(ks[25], (L, D_MODEL)),
    }


def _fwd_reference(x, attn_pre_norm, w_in, hgrn_lb, hgrn_gnorm, w_branch_a, rwkv_mu, rwkv_w0,
              rwkv_w2, rwkv_a0, rwkv_a2, rwkv_g2, rwkv_k_k, rwkv_k_a, rwkv_r_k, rwkv_ln_w,
              rwkv_ln_b, w_branch_b, w_out, attn_post_norm, ffn_pre_norm, w_up, conv_w,
              conv_b, w_down, ffn_post_norm):
    lb_table = jnp.cumsum(jax.nn.softmax(hgrn_lb.astype(F32), axis=0), axis=0)
    h = x
    for l in range(DEPTH):
        xn = _rmsnorm(h, attn_pre_norm[l])
        mix = _token_mixer(xn, lb_table[l], w_in[l], hgrn_gnorm[l], w_branch_a[l], rwkv_mu[l],
                           rwkv_w0[l], rwkv_w2[l], rwkv_a0[l], rwkv_a2[l], rwkv_g2[l],
                           rwkv_k_k[l], rwkv_k_a[l], rwkv_r_k[l], rwkv_ln_w[l], rwkv_ln_b[l],
                           w_branch_b[l], w_out[l])
        h = h + _rmsnorm(mix, attn_post_norm[l])
        xn = _rmsnorm(h, ffn_pre_norm[l])
        ff = _conv_ffn(xn, w_up[l], conv_w[l], conv_b[l], w_down[l])
        h = h + _rmsnorm(ff, ffn_post_norm[l])
    return h


import jax as _jax
import jax.numpy as _jnp

TWIN_FORMAT = 'train_step'
FWD_PARAMS = ['x', 'attn_pre_norm', 'w_in', 'hgrn_lb', 'hgrn_gnorm', 'w_branch_a', 'rwkv_mu', 'rwkv_w0', 'rwkv_w2', 'rwkv_a0', 'rwkv_a2', 'rwkv_g2', 'rwkv_k_k', 'rwkv_k_a', 'rwkv_r_k', 'rwkv_ln_w', 'rwkv_ln_b', 'w_branch_b', 'w_out', 'attn_post_norm', 'ffn_pre_norm', 'w_up', 'conv_w', 'conv_b', 'w_down', 'ffn_post_norm']
TWIN_WEIGHTS = ['attn_pre_norm', 'w_in', 'hgrn_lb', 'hgrn_gnorm', 'w_branch_a', 'rwkv_mu', 'rwkv_w0', 'rwkv_w2', 'rwkv_a0', 'rwkv_a2', 'rwkv_g2', 'rwkv_k_k', 'rwkv_k_a', 'rwkv_r_k', 'rwkv_ln_w', 'rwkv_ln_b', 'w_branch_b', 'w_out', 'attn_post_norm', 'ffn_pre_norm', 'w_up', 'conv_w', 'conv_b', 'w_down', 'ffn_post_norm']
TWIN_DIFF_INPUT = 'x'
TWIN_INPUTS = ['x', 'attn_pre_norm', 'w_in', 'hgrn_lb', 'hgrn_gnorm', 'w_branch_a', 'rwkv_mu', 'rwkv_w0', 'rwkv_w2', 'rwkv_a0', 'rwkv_a2', 'rwkv_g2', 'rwkv_k_k', 'rwkv_k_a', 'rwkv_r_k', 'rwkv_ln_w', 'rwkv_ln_b', 'w_branch_b', 'w_out', 'attn_post_norm', 'ffn_pre_norm', 'w_up', 'conv_w', 'conv_b', 'w_down', 'ffn_post_norm', 'loss_target', 'm_attn_pre_norm', 'm_w_in', 'm_hgrn_lb', 'm_hgrn_gnorm', 'm_w_branch_a', 'm_rwkv_mu', 'm_rwkv_w0', 'm_rwkv_w2', 'm_rwkv_a0', 'm_rwkv_a2', 'm_rwkv_g2', 'm_rwkv_k_k', 'm_rwkv_k_a', 'm_rwkv_r_k', 'm_rwkv_ln_w', 'm_rwkv_ln_b', 'm_w_branch_b', 'm_w_out', 'm_attn_post_norm', 'm_ffn_pre_norm', 'm_w_up', 'm_conv_w', 'm_conv_b', 'm_w_down', 'm_ffn_post_norm', 'v_attn_pre_norm', 'v_w_in', 'v_hgrn_lb', 'v_hgrn_gnorm', 'v_w_branch_a', 'v_rwkv_mu', 'v_rwkv_w0', 'v_rwkv_w2', 'v_rwkv_a0', 'v_rwkv_a2', 'v_rwkv_g2', 'v_rwkv_k_k', 'v_rwkv_k_a', 'v_rwkv_r_k', 'v_rwkv_ln_w', 'v_rwkv_ln_b', 'v_w_branch_b', 'v_w_out', 'v_attn_post_norm', 'v_ffn_pre_norm', 'v_w_up', 'v_conv_w', 'v_conv_b', 'v_w_down', 'v_ffn_post_norm']
TWIN_OUTPUTS = ['loss', 'grad_x', 'grad_attn_pre_norm', 'grad_w_in', 'grad_hgrn_lb', 'grad_hgrn_gnorm', 'grad_w_branch_a', 'grad_rwkv_mu', 'grad_rwkv_w0', 'grad_rwkv_w2', 'grad_rwkv_a0', 'grad_rwkv_a2', 'grad_rwkv_g2', 'grad_rwkv_k_k', 'grad_rwkv_k_a', 'grad_rwkv_r_k', 'grad_rwkv_ln_w', 'grad_rwkv_ln_b', 'grad_w_branch_b', 'grad_w_out', 'grad_attn_post_norm', 'grad_ffn_pre_norm', 'grad_w_up', 'grad_conv_w', 'grad_conv_b', 'grad_w_down', 'grad_ffn_post_norm', 'delta_attn_pre_norm', 'delta_w_in', 'delta_hgrn_lb', 'delta_hgrn_gnorm', 'delta_w_branch_a', 'delta_rwkv_mu', 'delta_rwkv_w0', 'delta_rwkv_w2', 'delta_rwkv_a0', 'delta_rwkv_a2', 'delta_rwkv_g2', 'delta_rwkv_k_k', 'delta_rwkv_k_a', 'delta_rwkv_r_k', 'delta_rwkv_ln_w', 'delta_rwkv_ln_b', 'delta_w_branch_b', 'delta_w_out', 'delta_attn_post_norm', 'delta_ffn_pre_norm', 'delta_w_up', 'delta_conv_w', 'delta_conv_b', 'delta_w_down', 'delta_ffn_post_norm', 'new_m_attn_pre_norm', 'new_m_w_in', 'new_m_hgrn_lb', 'new_m_hgrn_gnorm', 'new_m_w_branch_a', 'new_m_rwkv_mu', 'new_m_rwkv_w0', 'new_m_rwkv_w2', 'new_m_rwkv_a0', 'new_m_rwkv_a2', 'new_m_rwkv_g2', 'new_m_rwkv_k_k', 'new_m_rwkv_k_a', 'new_m_rwkv_r_k', 'new_m_rwkv_ln_w', 'new_m_rwkv_ln_b', 'new_m_w_branch_b', 'new_m_w_out', 'new_m_attn_post_norm', 'new_m_ffn_pre_norm', 'new_m_w_up', 'new_m_conv_w', 'new_m_conv_b', 'new_m_w_down', 'new_m_ffn_post_norm', 'new_v_attn_pre_norm', 'new_v_w_in', 'new_v_hgrn_lb', 'new_v_hgrn_gnorm', 'new_v_w_branch_a', 'new_v_rwkv_mu', 'new_v_rwkv_w0', 'new_v_rwkv_w2', 'new_v_rwkv_a0', 'new_v_rwkv_a2', 'new_v_rwkv_g2', 'new_v_rwkv_k_k', 'new_v_rwkv_k_a', 'new_v_rwkv_r_k', 'new_v_rwkv_ln_w', 'new_v_rwkv_ln_b', 'new_v_w_branch_b', 'new_v_w_out', 'new_v_attn_post_norm', 'new_v_ffn_pre_norm', 'new_v_w_up', 'new_v_conv_w', 'new_v_conv_b', 'new_v_w_down', 'new_v_ffn_post_norm']
TWIN_LEAF_KINDS = {'loss': 'loss', 'grad_x': 'grad_x', 'grad_attn_pre_norm': 'grad_w', 'grad_w_in': 'grad_w', 'grad_hgrn_lb': 'grad_w', 'grad_hgrn_gnorm': 'grad_w', 'grad_w_branch_a': 'grad_w', 'grad_rwkv_mu': 'grad_w', 'grad_rwkv_w0': 'grad_w', 'grad_rwkv_w2': 'grad_w', 'grad_rwkv_a0': 'grad_w', 'grad_rwkv_a2': 'grad_w', 'grad_rwkv_g2': 'grad_w', 'grad_rwkv_k_k': 'grad_w', 'grad_rwkv_k_a': 'grad_w', 'grad_rwkv_r_k': 'grad_w', 'grad_rwkv_ln_w': 'grad_w', 'grad_rwkv_ln_b': 'grad_w', 'grad_w_branch_b': 'grad_w', 'grad_w_out': 'grad_w', 'grad_attn_post_norm': 'grad_w', 'grad_ffn_pre_norm': 'grad_w', 'grad_w_up': 'grad_w', 'grad_conv_w': 'grad_w', 'grad_conv_b': 'grad_w', 'grad_w_down': 'grad_w', 'grad_ffn_post_norm': 'grad_w', 'delta_attn_pre_norm': 'delta_w', 'delta_w_in': 'delta_w', 'delta_hgrn_lb': 'delta_w', 'delta_hgrn_gnorm': 'delta_w', 'delta_w_branch_a': 'delta_w', 'delta_rwkv_mu': 'delta_w', 'delta_rwkv_w0': 'delta_w', 'delta_rwkv_w2': 'delta_w', 'delta_rwkv_a0': 'delta_w', 'delta_rwkv_a2': 'delta_w', 'delta_rwkv_g2': 'delta_w', 'delta_rwkv_k_k': 'delta_w', 'delta_rwkv_k_a': 'delta_w', 'delta_rwkv_r_k': 'delta_w', 'delta_rwkv_ln_w': 'delta_w', 'delta_rwkv_ln_b': 'delta_w', 'delta_w_branch_b': 'delta_w', 'delta_w_out': 'delta_w', 'delta_attn_post_norm': 'delta_w', 'delta_ffn_pre_norm': 'delta_w', 'delta_w_up': 'delta_w', 'delta_conv_w': 'delta_w', 'delta_conv_b': 'delta_w', 'delta_w_down': 'delta_w', 'delta_ffn_post_norm': 'delta_w', 'new_m_attn_pre_norm': 'new_m', 'new_m_w_in': 'new_m', 'new_m_hgrn_lb': 'new_m', 'new_m_hgrn_gnorm': 'new_m', 'new_m_w_branch_a': 'new_m', 'new_m_rwkv_mu': 'new_m', 'new_m_rwkv_w0': 'new_m', 'new_m_rwkv_w2': 'new_m', 'new_m_rwkv_a0': 'new_m', 'new_m_rwkv_a2': 'new_m', 'new_m_rwkv_g2': 'new_m', 'new_m_rwkv_k_k': 'new_m', 'new_m_rwkv_k_a': 'new_m', 'new_m_rwkv_r_k': 'new_m', 'new_m_rwkv_ln_w': 'new_m', 'new_m_rwkv_ln_b': 'new_m', 'new_m_w_branch_b': 'new_m', 'new_m_w_out': 'new_m', 'new_m_attn_post_norm': 'new_m', 'new_m_ffn_pre_norm': 'new_m', 'new_m_w_up': 'new_m', 'new_m_conv_w': 'new_m', 'new_m_conv_b': 'new_m', 'new_m_w_down': 'new_m', 'new_m_ffn_post_norm': 'new_m', 'new_v_attn_pre_norm': 'new_v', 'new_v_w_in': 'new_v', 'new_v_hgrn_lb': 'new_v', 'new_v_hgrn_gnorm': 'new_v', 'new_v_w_branch_a': 'new_v', 'new_v_rwkv_mu': 'new_v', 'new_v_rwkv_w0': 'new_v', 'new_v_rwkv_w2': 'new_v', 'new_v_rwkv_a0': 'new_v', 'new_v_rwkv_a2': 'new_v', 'new_v_rwkv_g2': 'new_v', 'new_v_rwkv_k_k': 'new_v', 'new_v_rwkv_k_a': 'new_v', 'new_v_rwkv_r_k': 'new_v', 'new_v_rwkv_ln_w': 'new_v', 'new_v_rwkv_ln_b': 'new_v', 'new_v_w_branch_b': 'new_v', 'new_v_w_out': 'new_v', 'new_v_attn_post_norm': 'new_v', 'new_v_ffn_pre_norm': 'new_v', 'new_v_w_up': 'new_v', 'new_v_conv_w': 'new_v', 'new_v_conv_b': 'new_v', 'new_v_w_down': 'new_v', 'new_v_ffn_post_norm': 'new_v'}


def _forward(args):
    return _fwd_reference(*[args[k] for k in FWD_PARAMS])


def _output_shape():
    out = _jax.eval_shape(lambda: _forward(_fwd_setup_inputs(0)))
    return out.shape, out.dtype

N_MICROBATCH = 1
ADAM_LR = 0.001
ADAM_B1 = 0.9
ADAM_B2 = 0.999
ADAM_EPS = 1e-08
ADAM_WD = 0.01
ADAM_STEP = 10
PER_EXAMPLE_BATCH_AXIS = {'x': 0, 'loss_target': 0}
SHARED_INPUTS = []
_WEIGHT_DTYPES = {'attn_pre_norm': _jnp.float32, 'w_in': _jnp.float32, 'hgrn_lb': _jnp.float32, 'hgrn_gnorm': _jnp.float32, 'w_branch_a': _jnp.float32, 'rwkv_mu': _jnp.float32, 'rwkv_w0': _jnp.float32, 'rwkv_w2': _jnp.float32, 'rwkv_a0': _jnp.float32, 'rwkv_a2': _jnp.float32, 'rwkv_g2': _jnp.float32, 'rwkv_k_k': _jnp.float32, 'rwkv_k_a': _jnp.float32, 'rwkv_r_k': _jnp.float32, 'rwkv_ln_w': _jnp.float32, 'rwkv_ln_b': _jnp.float32, 'w_branch_b': _jnp.float32, 'w_out': _jnp.float32, 'attn_post_norm': _jnp.float32, 'ffn_pre_norm': _jnp.float32, 'w_up': _jnp.float32, 'conv_w': _jnp.float32, 'conv_b': _jnp.float32, 'w_down': _jnp.float32, 'ffn_post_norm': _jnp.float32}
MOMENT_SCALE = {'attn_pre_norm': 5.518030e-01, 'w_in': 1.812108e-01, 'hgrn_lb': 2.133818e-02, 'hgrn_gnorm': 2.369759e-01, 'w_branch_a': 2.390996e-01, 'rwkv_mu': 3.783520e-01, 'rwkv_w0': 8.433140e-02, 'rwkv_w2': 9.824761e-03, 'rwkv_a0': 8.590817e-02, 'rwkv_a2': 8.446533e-02, 'rwkv_g2': 2.168145e-01, 'rwkv_k_k': 2.695518e-01, 'rwkv_k_a': 2.306311e-01, 'rwkv_r_k': 4.656280e-01, 'rwkv_ln_w': 2.179504e-01, 'rwkv_ln_b': 2.753943e-01, 'w_branch_b': 2.200581e-01, 'w_out': 3.336786e-01, 'attn_post_norm': 1.602002e+01, 'ffn_pre_norm': 3.412231e-01, 'w_up': 1.444947e-01, 'conv_w': 1.475964e-01, 'conv_b': 2.119296e-01, 'w_down': 2.518338e-01, 'ffn_post_norm': 1.602715e+01}


def _to_microbatches(a, axis):
    t = _jnp.moveaxis(a, axis, 0)
    t = t.reshape((N_MICROBATCH, t.shape[0] // N_MICROBATCH) + t.shape[1:])
    return _jnp.moveaxis(t, 1, axis + 1)


def setup_inputs(seed: int = 0) -> dict:
    inp = _fwd_setup_inputs(seed)
    key = _jax.random.fold_in(_jax.random.key(seed), 7919)
    shape, _ = _output_shape()
    out = dict(inp)
    out["loss_target"] = _jax.random.normal(_jax.random.fold_in(key, 0), shape, _jnp.float32)
    for i, name in enumerate(TWIN_WEIGHTS):
        w = inp[name].astype(_jnp.float32)
        if MOMENT_SCALE is None:
            s = _jnp.sqrt(_jnp.mean(_jnp.square(w)) + 1e-30)
        else:
            s = MOMENT_SCALE[name]
        km, kv = _jax.random.split(_jax.random.fold_in(key, i + 1))
        out[name] = w
        out["m_" + name] = s * _jax.random.normal(km, w.shape, _jnp.float32)
        out["v_" + name] = (s * s) * _jax.random.uniform(kv, w.shape, _jnp.float32, 0.5, 1.5)
    if N_MICROBATCH > 1:
        for name, axis in PER_EXAMPLE_BATCH_AXIS.items():
            out[name] = _to_microbatches(out[name], axis)
    return {'x': out['x'], 'attn_pre_norm': out['attn_pre_norm'], 'w_in': out['w_in'], 'hgrn_lb': out['hgrn_lb'], 'hgrn_gnorm': out['hgrn_gnorm'], 'w_branch_a': out['w_branch_a'], 'rwkv_mu': out['rwkv_mu'], 'rwkv_w0': out['rwkv_w0'], 'rwkv_w2': out['rwkv_w2'], 'rwkv_a0': out['rwkv_a0'], 'rwkv_a2': out['rwkv_a2'], 'rwkv_g2': out['rwkv_g2'], 'rwkv_k_k': out['rwkv_k_k'], 'rwkv_k_a': out['rwkv_k_a'], 'rwkv_r_k': out['rwkv_r_k'], 'rwkv_ln_w': out['rwkv_ln_w'], 'rwkv_ln_b': out['rwkv_ln_b'], 'w_branch_b': out['w_branch_b'], 'w_out': out['w_out'], 'attn_post_norm': out['attn_post_norm'], 'ffn_pre_norm': out['ffn_pre_norm'], 'w_up': out['w_up'], 'conv_w': out['conv_w'], 'conv_b': out['conv_b'], 'w_down': out['w_down'], 'ffn_post_norm': out['ffn_post_norm'], 'loss_target': out['loss_target'], 'm_attn_pre_norm': out['m_attn_pre_norm'], 'm_w_in': out['m_w_in'], 'm_hgrn_lb': out['m_hgrn_lb'], 'm_hgrn_gnorm': out['m_hgrn_gnorm'], 'm_w_branch_a': out['m_w_branch_a'], 'm_rwkv_mu': out['m_rwkv_mu'], 'm_rwkv_w0': out['m_rwkv_w0'], 'm_rwkv_w2': out['m_rwkv_w2'], 'm_rwkv_a0': out['m_rwkv_a0'], 'm_rwkv_a2': out['m_rwkv_a2'], 'm_rwkv_g2': out['m_rwkv_g2'], 'm_rwkv_k_k': out['m_rwkv_k_k'], 'm_rwkv_k_a': out['m_rwkv_k_a'], 'm_rwkv_r_k': out['m_rwkv_r_k'], 'm_rwkv_ln_w': out['m_rwkv_ln_w'], 'm_rwkv_ln_b': out['m_rwkv_ln_b'], 'm_w_branch_b': out['m_w_branch_b'], 'm_w_out': out['m_w_out'], 'm_attn_post_norm': out['m_attn_post_norm'], 'm_ffn_pre_norm': out['m_ffn_pre_norm'], 'm_w_up': out['m_w_up'], 'm_conv_w': out['m_conv_w'], 'm_conv_b': out['m_conv_b'], 'm_w_down': out['m_w_down'], 'm_ffn_post_norm': out['m_ffn_post_norm'], 'v_attn_pre_norm': out['v_attn_pre_norm'], 'v_w_in': out['v_w_in'], 'v_hgrn_lb': out['v_hgrn_lb'], 'v_hgrn_gnorm': out['v_hgrn_gnorm'], 'v_w_branch_a': out['v_w_branch_a'], 'v_rwkv_mu': out['v_rwkv_mu'], 'v_rwkv_w0': out['v_rwkv_w0'], 'v_rwkv_w2': out['v_rwkv_w2'], 'v_rwkv_a0': out['v_rwkv_a0'], 'v_rwkv_a2': out['v_rwkv_a2'], 'v_rwkv_g2': out['v_rwkv_g2'], 'v_rwkv_k_k': out['v_rwkv_k_k'], 'v_rwkv_k_a': out['v_rwkv_k_a'], 'v_rwkv_r_k': out['v_rwkv_r_k'], 'v_rwkv_ln_w': out['v_rwkv_ln_w'], 'v_rwkv_ln_b': out['v_rwkv_ln_b'], 'v_w_branch_b': out['v_w_branch_b'], 'v_w_out': out['v_w_out'], 'v_attn_post_norm': out['v_attn_post_norm'], 'v_ffn_pre_norm': out['v_ffn_pre_norm'], 'v_w_up': out['v_w_up'], 'v_conv_w': out['v_conv_w'], 'v_conv_b': out['v_conv_b'], 'v_w_down': out['v_w_down'], 'v_ffn_post_norm': out['v_ffn_post_norm']}


def _loss(weights, diff, rest, loss_target):
    with _jax.named_scope("forward"):
        args = {**rest, TWIN_DIFF_INPUT: diff, **{k: w.astype(_WEIGHT_DTYPES[k]) for k, w in weights.items()}}
        y = _forward(args)
    with _jax.named_scope("loss_head"):
        err = _jnp.square(y.astype(_jnp.float32) - loss_target)
        return 0.5 * _jnp.sum(_jnp.mean(err, axis=-1)) if err.ndim else 0.5 * err


def _adamw(w, g, m, v):
    m = ADAM_B1 * m + (1.0 - ADAM_B1) * g
    v = ADAM_B2 * v + (1.0 - ADAM_B2) * _jnp.square(g)
    m_hat = m / (1.0 - ADAM_B1 ** ADAM_STEP)
    v_hat = v / (1.0 - ADAM_B2 ** ADAM_STEP)
    delta = -ADAM_LR * (m_hat / (_jnp.sqrt(v_hat) + ADAM_EPS) + ADAM_WD * w)
    return delta, m, v


def reference(x, attn_pre_norm, w_in, hgrn_lb, hgrn_gnorm, w_branch_a, rwkv_mu, rwkv_w0, rwkv_w2, rwkv_a0, rwkv_a2, rwkv_g2, rwkv_k_k, rwkv_k_a, rwkv_r_k, rwkv_ln_w, rwkv_ln_b, w_branch_b, w_out, attn_post_norm, ffn_pre_norm, w_up, conv_w, conv_b, w_down, ffn_post_norm, loss_target, m_attn_pre_norm, m_w_in, m_hgrn_lb, m_hgrn_gnorm, m_w_branch_a, m_rwkv_mu, m_rwkv_w0, m_rwkv_w2, m_rwkv_a0, m_rwkv_a2, m_rwkv_g2, m_rwkv_k_k, m_rwkv_k_a, m_rwkv_r_k, m_rwkv_ln_w, m_rwkv_ln_b, m_w_branch_b, m_w_out, m_attn_post_norm, m_ffn_pre_norm, m_w_up, m_conv_w, m_conv_b, m_w_down, m_ffn_post_norm, v_attn_pre_norm, v_w_in, v_hgrn_lb, v_hgrn_gnorm, v_w_branch_a, v_rwkv_mu, v_rwkv_w0, v_rwkv_w2, v_rwkv_a0, v_rwkv_a2, v_rwkv_g2, v_rwkv_k_k, v_rwkv_k_a, v_rwkv_r_k, v_rwkv_ln_w, v_rwkv_ln_b, v_w_branch_b, v_w_out, v_attn_post_norm, v_ffn_pre_norm, v_w_up, v_conv_w, v_conv_b, v_w_down, v_ffn_post_norm):
    given = dict(x=x, attn_pre_norm=attn_pre_norm, w_in=w_in, hgrn_lb=hgrn_lb, hgrn_gnorm=hgrn_gnorm, w_branch_a=w_branch_a, rwkv_mu=rwkv_mu, rwkv_w0=rwkv_w0, rwkv_w2=rwkv_w2, rwkv_a0=rwkv_a0, rwkv_a2=rwkv_a2, rwkv_g2=rwkv_g2, rwkv_k_k=rwkv_k_k, rwkv_k_a=rwkv_k_a, rwkv_r_k=rwkv_r_k, rwkv_ln_w=rwkv_ln_w, rwkv_ln_b=rwkv_ln_b, w_branch_b=w_branch_b, w_out=w_out, attn_post_norm=attn_post_norm, ffn_pre_norm=ffn_pre_norm, w_up=w_up, conv_w=conv_w, conv_b=conv_b, w_down=w_down, ffn_post_norm=ffn_post_norm, loss_target=loss_target, m_attn_pre_norm=m_attn_pre_norm, m_w_in=m_w_in, m_hgrn_lb=m_hgrn_lb, m_hgrn_gnorm=m_hgrn_gnorm, m_w_branch_a=m_w_branch_a, m_rwkv_mu=m_rwkv_mu, m_rwkv_w0=m_rwkv_w0, m_rwkv_w2=m_rwkv_w2, m_rwkv_a0=m_rwkv_a0, m_rwkv_a2=m_rwkv_a2, m_rwkv_g2=m_rwkv_g2, m_rwkv_k_k=m_rwkv_k_k, m_rwkv_k_a=m_rwkv_k_a, m_rwkv_r_k=m_rwkv_r_k, m_rwkv_ln_w=m_rwkv_ln_w, m_rwkv_ln_b=m_rwkv_ln_b, m_w_branch_b=m_w_branch_b, m_w_out=m_w_out, m_attn_post_norm=m_attn_post_norm, m_ffn_pre_norm=m_ffn_pre_norm, m_w_up=m_w_up, m_conv_w=m_conv_w, m_conv_b=m_conv_b, m_w_down=m_w_down, m_ffn_post_norm=m_ffn_post_norm, v_attn_pre_norm=v_attn_pre_norm, v_w_in=v_w_in, v_hgrn_lb=v_hgrn_lb, v_hgrn_gnorm=v_hgrn_gnorm, v_w_branch_a=v_w_branch_a, v_rwkv_mu=v_rwkv_mu, v_rwkv_w0=v_rwkv_w0, v_rwkv_w2=v_rwkv_w2, v_rwkv_a0=v_rwkv_a0, v_rwkv_a2=v_rwkv_a2, v_rwkv_g2=v_rwkv_g2, v_rwkv_k_k=v_rwkv_k_k, v_rwkv_k_a=v_rwkv_k_a, v_rwkv_r_k=v_rwkv_r_k, v_rwkv_ln_w=v_rwkv_ln_w, v_rwkv_ln_b=v_rwkv_ln_b, v_w_branch_b=v_w_branch_b, v_w_out=v_w_out, v_attn_post_norm=v_attn_post_norm, v_ffn_pre_norm=v_ffn_pre_norm, v_w_up=v_w_up, v_conv_w=v_conv_w, v_conv_b=v_conv_b, v_w_down=v_w_down, v_ffn_post_norm=v_ffn_post_norm)
    weights = {n: given[n] for n in TWIN_WEIGHTS}
    shared = {n: given[n] for n in SHARED_INPUTS}
    per_example = {n: given[n] for n in ['x']}
    grad_fn = _jax.value_and_grad(_loss, argnums=(0, 1))

    def one_microbatch(ex, loss_target):
        ex = dict(ex)
        diff = ex.pop(TWIN_DIFF_INPUT)
        return grad_fn(weights, diff, {**shared, **ex}, loss_target)

    if N_MICROBATCH == 1:
        loss, (grad_w, grad_x) = one_microbatch(per_example, given["loss_target"])
    else:
        def body(carry, xs):
            loss_sum, grad_sum = carry
            l_k, (gw_k, gx_k) = one_microbatch(xs[0], xs[1])
            with _jax.named_scope("update"):
                return (loss_sum + l_k, _jax.tree.map(_jnp.add, grad_sum, gw_k)), gx_k

        init = (_jnp.zeros((), _jnp.float32), _jax.tree.map(_jnp.zeros_like, weights))
        (loss, grad_w), grad_x = _jax.lax.scan(body, init, (per_example, given["loss_target"]))
    with _jax.named_scope("update"):
        delta_w, new_m, new_v = {}, {}, {}
        for n in TWIN_WEIGHTS:
            delta_w[n], new_m[n], new_v[n] = _adamw(weights[n], grad_w[n], given["m_" + n], given["v_" + n])
    return (loss, grad_x, *[grad_w[n] for n in TWIN_WEIGHTS], *[delta_w[n] for n in TWIN_WEIGHTS],
            *[new_m[n] for n in TWIN_WEIGHTS], *[new_v[n] for n in TWIN_WEIGHTS])
```

```python
import functools

import jax
import jax.numpy as jnp
from jax import lax
from jax.experimental import pallas as pl
from jax.experimental.pallas import tpu as pltpu

F32 = jnp.float32
BF = jnp.bfloat16
MESH = pl.DeviceIdType.MESH

D = 1024
HG_HEADS = 8
HG_K = 128
HG_CHUNK = 32
HG_SCALE = HG_K ** -0.5
RW_HEADS = 16
RW_N = 64
RW_CHUNK = 64
DFF = 2816
IN_COLS = 9472
HG_COLS = 4096
RW_COLS = 3328
EPS = 1e-6
GN_EPS = 1e-5 * RW_N
ADAM_LR = 0.001
ADAM_B1 = 0.9
ADAM_B2 = 0.999
ADAM_EPS = 1e-08
ADAM_WD = 0.01
ADAM_STEP = 10
N_DEV = 8
LANES = 128
VMEM_LIMIT = 56 * 1024 * 1024

SHARD_ROWS = (("w_in", 1184), ("w_up", 704), ("w_down", 352), ("w_branch_a", 128), ("w_branch_b", 128),
              ("w_out", 128), ("rwkv_w2", 16), ("rwkv_a2", 16), ("rwkv_g2", 16), ("conv_w", 16))
SHARD_TOTAL = sum(r for _, r in SHARD_ROWS)
REPL_ROWS = (("attn_pre_norm", 1), ("hgrn_lb", 2), ("hgrn_gnorm", 1), ("rwkv_mu", 4), ("rwkv_w0", 1),
             ("rwkv_a0", 1), ("rwkv_k_k", 1), ("rwkv_k_a", 1), ("rwkv_r_k", 1), ("rwkv_ln_w", 1),
             ("rwkv_ln_b", 1), ("attn_post_norm", 1), ("ffn_pre_norm", 1), ("conv_b", 6), ("ffn_post_norm", 1))
REPL_TOTAL = 32


def _cparams(sem=None, **kw):
    return pltpu.CompilerParams(dimension_semantics=sem, vmem_limit_bytes=VMEM_LIMIT, **kw)


_DN = {"nn": ((1,), (0,)), "nt": ((1,), (1,)), "tn": ((0,), (0,))}


def _raw_dot(a, b, mode, hi):
    dn = (_DN[mode], ((), ()))
    if hi:
        return lax.dot_general(a, b, dn, precision=lax.Precision.HIGHEST, preferred_element_type=F32)
    return lax.dot_general(a.astype(BF), b.astype(BF), dn, preferred_element_type=F32)


@functools.partial(jax.custom_vjp, nondiff_argnums=(2, 3))
def _dot(a, b, mode, hi):
    return _raw_dot(a, b, mode, hi)


def _dot_fwd(a, b, mode, hi):
    return _raw_dot(a, b, mode, hi), (a, b)


def _dot_bwd(mode, hi, res, g):
    a, b = res
    if mode == "nn":
        return _dot(g, b, "nt", hi), _dot(a, g, "tn", hi)
    if mode == "nt":
        return _dot(g, b, "nn", hi), _dot(g, a, "tn", hi)
    return _dot(b, g, "nt", hi), _dot(a, g, "nn", hi)


_dot.defvjp(_dot_fwd, _dot_bwd)


def _row(x, i):
    r = lax.broadcasted_iota(jnp.int32, x.shape, 0)
    return jnp.sum(jnp.where(r == i, x, 0.0), axis=0, keepdims=True)


def _shift_down(x, prev):
    t = x.shape[0]

    @jax.custom_vjp
    def sh(x, prev):
        r = lax.broadcasted_iota(jnp.int32, x.shape, 0)
        return jnp.where(r == 0, prev, pltpu.roll(x, 1, 0))

    def fwd(x, prev):
        return sh(x, prev), None

    def bwd(_, g):
        r = lax.broadcasted_iota(jnp.int32, g.shape, 0)
        dx = jnp.where(r == t - 1, 0.0, pltpu.roll(g, t - 1, 0))
        return dx, jnp.sum(jnp.where(r == 0, g, 0.0), axis=0, keepdims=True)

    sh.defvjp(fwd, bwd)
    return sh(x, prev)


def _sigmoid(x):
    return jax.nn.sigmoid(x)


def _silu(x):
    return x * jax.nn.sigmoid(x)


def _softplus(x):
    return jnp.maximum(x, 0.0) + jnp.log(1.0 + jnp.exp(-jnp.abs(x)))


def _rms(x, g):
    return (x * lax.rsqrt(jnp.mean(x * x, axis=-1, keepdims=True) + EPS)) * g


def _headmat():
    j = lax.broadcasted_iota(jnp.int32, (D, LANES), 0)
    h = lax.broadcasted_iota(jnp.int32, (D, LANES), 1)
    e = jnp.where(lax.shift_right_logical(j, 6) == h, 1.0, 0.0).astype(F32)
    pad = jnp.where(lax.broadcasted_iota(jnp.int32, (1, LANES), 1) >= RW_HEADS, 1.0, 0.0).astype(F32)
    return e, pad


def _f_pre1(ps, xs, cs):
    return [_rms(xs[0], ps[0])], []


def _f_hgrn(ps, xs, cs):
    lbraw, gn = ps
    hq, hf, hi, hg = xs
    (st,) = cs
    l0, l1 = _row(lbraw, 0), _row(lbraw, 1)
    m = jnp.maximum(l0, l1)
    e0, e1 = jnp.exp(l0 - m), jnp.exp(l1 - m)
    lb = e0 / (e0 + e1)
    q = _silu(hq) * HG_SCALE
    f = lb + (1.0 - lb) * _sigmoid(hf)
    kh = 1.0 - f
    gl = jnp.log(f)
    c = HG_CHUNK
    r = lax.broadcasted_iota(jnp.int32, (c, c), 0)
    cc = lax.broadcasted_iota(jnp.int32, (c, c), 1)
    low = cc <= r
    tri = jnp.where(low, 1.0, 0.0).astype(F32)
    outs = []
    for i in range(hq.shape[0] // c):
        sl = slice(i * c, (i + 1) * c)
        qc, kc, vc = q[sl], kh[sl], hi[sl]
        b = _dot(tri, gl[sl], "nn", True)
        bref = _row(b, c // 2 - 1)
        blast = _row(b, c - 1)
        sc = _dot(qc * jnp.exp(b - bref), kc * jnp.exp(bref - b), "nt", False)
        sc = jnp.where(low, sc, 0.0)
        o = _dot(sc, vc, "nn", False) + _dot(qc * jnp.exp(b), st, "nt", False)
        u = _dot(vc, kc * jnp.exp(blast - b), "tn", False)
        st = jnp.exp(blast) * st + u
        outs.append(o)
    o = outs[0] if len(outs) == 1 else jnp.concatenate(outs, axis=0)
    o = o * lax.rsqrt(jnp.mean(o * o, axis=-1, keepdims=True) + EPS)
    o = o * gn
    return [o * _silu(hg)], [st]


_RW_OFFS = (0, 1024, 2048, 3072, 3200, 3328)


def _f_rwpre(ps, xs, cs):
    mu, w0, w2p, a0, a2p, g2, k_k, k_a = ps
    (prev,) = cs
    t = xs[0].shape[0]
    zs = []
    for i, z in enumerate(xs):
        lo, hi = _RW_OFFS[i], _RW_OFFS[i + 1]
        zs.append(z + mu[:, lo:hi] * (_shift_down(z, prev[:, lo:hi]) - z))
    rr, kr, vr, wa, gz = zs
    w_log = -_softplus(-(w0 + _dot(jnp.tanh(wa), w2p, "nn", False))) - 0.5
    lw = -jnp.exp(w_log)
    a = _sigmoid(a0 + _dot(wa, a2p, "nn", False))
    g = _dot(_sigmoid(gz), g2, "nn", False)
    e, pad = _headmat()
    kkr = kr * k_k
    nrm = jnp.sqrt(_dot(kkr * kkr, e, "nn", True) + pad)
    kk = kkr / _dot(jnp.maximum(nrm, 1e-12), e, "nt", True)
    k2 = kr * (1.0 + (a - 1.0) * k_a)
    newprev = jnp.concatenate([_row(z, t - 1) for z in xs], axis=1)
    return [rr, lw, k2, vr, -kk, kk * a, g], [newprev]


def _f_rwscan(hi_inv):
    def f(ps, xs, cs):
        r, lw, k, v, av, bv = xs
        (sv,) = cs
        c = RW_CHUNK
        n = 2 * c
        ri = lax.broadcasted_iota(jnp.int32, (c, c), 0)
        ci = lax.broadcasted_iota(jnp.int32, (c, c), 1)
        tri = jnp.where(ci <= ri, 1.0, 0.0).astype(F32)
        cl = _dot(tri, lw, "nn", True)
        cl_last = _row(cl, c - 1)
        lane = lax.broadcasted_iota(jnp.int32, (c, LANES), 1)
        h0 = lane < RW_N

        def stack(x):
            return jnp.concatenate([jnp.where(h0, x, 0.0), jnp.where(h0, 0.0, x)], axis=0)

        am = stack(av * jnp.exp(cl - lw))
        bm = stack(bv * jnp.exp(-cl))
        km = stack(k * jnp.exp(-cl))
        rm = stack(r * jnp.exp(cl))
        vm = stack(v)
        rn = lax.broadcasted_iota(jnp.int32, (n, n), 0)
        cn = lax.broadcasted_iota(jnp.int32, (n, n), 1)
        blk = (rn >= c) == (cn >= c)
        strict = blk & (cn < rn)
        incl = blk & (cn <= rn)
        lab = jnp.where(strict, _dot(am, bm, "nt", False), 0.0)
        lak = jnp.where(strict, _dot(am, km, "nt", False), 0.0)
        wrb = jnp.where(incl, _dot(rm, bm, "nt", False), 0.0)
        wrk = jnp.where(incl, _dot(rm, km, "nt", False), 0.0)
        eye = jnp.where(rn == cn, 1.0, 0.0).astype(F32)
        tinv = eye + lab
        p = lab
        for _ in range(5):
            p = _dot(p, p, "nn", hi_inv)
            tinv = tinv + _dot(tinv, p, "nn", hi_inv)
        um = _dot(tinv, _dot(am, sv, "nt", False) + _dot(lak, vm, "nn", False), "nn", False)
        ym = _dot(rm, sv, "nt", False) + _dot(wrb, um, "nn", False) + _dot(wrk, vm, "nn", False)
        y = ym[:c] + ym[c:]
        sn = (sv + _dot(um, bm, "tn", False) + _dot(vm, km, "tn", False)) * jnp.exp(cl_last)
        return [y], [sn]
    return f


def _f_rwpost(ps, xs, cs):
    ln_w, ln_b, r_k = ps
    y, r, k, v, g = xs
    e, _ = _headmat()
    inv_n = 1.0 / RW_N
    mu = _dot(y, e, "nn", True) * inv_n
    yc = y - _dot(mu, e, "nt", True)
    var = _dot(yc * yc, e, "nn", True) * inv_n
    yn = yc * _dot(lax.rsqrt(var + GN_EPS), e, "nt", True)
    yn = yn * ln_w + ln_b
    bonus = _dot(_dot(r * k * r_k, e, "nn", True), e, "nt", True) * v
    return [(yn + bonus) * g], []


def _f_merge(ps, xs, cs):
    ga, gb, ya, yb = xs
    return [_sigmoid(ga) * ya + _sigmoid(gb) * yb], []


def _f_post1(ps, xs, cs):
    x, mix = xs
    h1 = x + _rms(mix, ps[0])
    return [h1, _rms(h1, ps[1])], []


def _f_conv(ps, xs, cs):
    cw, cb = ps
    p1, p2 = cs
    w0, w1, w2 = _row(cw, 0), _row(cw, 1), _row(cw, 2)
    t = xs[0].shape[0]
    hc = []
    for i, x in enumerate(xs):
        sl = slice(i * DFF, (i + 1) * DFF)
        s1 = _shift_down(x, p1[:, sl])
        s2 = _shift_down(s1, p2[:, sl])
        hc.append(cb[:, sl] + w0[:, sl] * s2 + w1[:, sl] * s1 + w2[:, sl] * x)
    n1 = jnp.concatenate([_row(x, t - 1) for x in xs], axis=1)
    n2 = jnp.concatenate([_row(x, t - 2) for x in xs], axis=1)
    return [_silu(hc[0]) * hc[1]], [n1, n2]


class _Stage:
    def __init__(self, name, f, g, tm, par_per_g, in_pieces, in_offs, carry_shapes, out_pieces, out_dtypes):
        self.name, self.f, self.g, self.tm = name, f, g, tm
        self.par_per_g, self.in_pieces, self.in_offs = par_per_g, in_pieces, in_offs
        self.carry_shapes, self.out_pieces, self.out_dtypes = carry_shapes, out_pieces, out_dtypes


def _par_spec(arr, per_g, g):
    r, c = arr.shape
    if per_g:
        return pl.BlockSpec((r, c // g), lambda gi, ni: (0, gi))
    return pl.BlockSpec((r, c), lambda gi, ni: (0, 0))


def _row_spec(tm, width, off, n, rev):
    if rev:
        return pl.BlockSpec((tm, width), lambda gi, ni: (n - 1 - ni, off + gi))
    return pl.BlockSpec((tm, width), lambda gi, ni: (ni, off + gi))


def _carry_spec(shape, n, rev):
    if rev:
        return pl.BlockSpec((None, None) + shape, lambda gi, ni: (gi, n - 1 - ni, 0, 0))
    return pl.BlockSpec((None, None) + shape, lambda gi, ni: (gi, ni, 0, 0))


def _load_pieces(refs, pieces_list):
    out = []
    for ref, pieces in zip(refs, pieces_list):
        o = 0
        for w in pieces:
            out.append(ref[:, o:o + w].astype(F32))
            o += w
    return out


def _store_pieces(refs, pieces_list, vals):
    k = 0
    for ref, pieces in zip(refs, pieces_list):
        o = 0
        for w in pieces:
            ref[:, o:o + w] = vals[k].astype(ref.dtype)
            k += 1
            o += w


def _stage_fwd(st, t, params, inputs):
    g, tm = st.g, min(st.tm, t)
    n = t // tm
    npar, nin, ncar, nout = len(params), len(inputs), len(st.carry_shapes), len(st.out_pieces)

    def body(*refs):
        p_refs = refs[:npar]
        x_refs = refs[npar:npar + nin]
        o_refs = refs[npar + nin:npar + nin + nout]
        s_refs = refs[npar + nin + nout:npar + nin + nout + ncar]
        c_scr = refs[npar + nin + nout + ncar:]
        ni = pl.program_id(1)

        @pl.when(ni == 0)
        def _():
            for c in c_scr:
                c[...] = jnp.zeros(c.shape, F32)

        ps = [r[...].astype(F32) for r in p_refs]
        xs = _load_pieces(x_refs, st.in_pieces)
        cs = [c[...] for c in c_scr]
        for s, c in zip(s_refs, cs):
            s[...] = c
        outs, ncs = st.f(ps, xs, cs)
        _store_pieces(o_refs, st.out_pieces, outs)
        for c, v in zip(c_scr, ncs):
            c[...] = v

    in_specs = [_par_spec(p, pg, g) for p, pg in zip(params, st.par_per_g)]
    in_specs += [_row_spec(tm, sum(pc), off, n, False) for pc, off in zip(st.in_pieces, st.in_offs)]
    out_specs = [_row_spec(tm, sum(pc), 0, n, False) for pc in st.out_pieces]
    out_specs += [_carry_spec(s, n, False) for s in st.carry_shapes]
    out_shape = [jax.ShapeDtypeStruct((t, g * sum(pc)), dt) for pc, dt in zip(st.out_pieces, st.out_dtypes)]
    out_shape += [jax.ShapeDtypeStruct((g, n) + s, F32) for s in st.carry_shapes]
    res = pl.pallas_call(
        body, name=st.name + "_fwd", grid=(g, n), in_specs=in_specs, out_specs=out_specs, out_shape=out_shape,
        scratch_shapes=[pltpu.VMEM(s, F32) for s in st.carry_shapes],
        compiler_params=_cparams(("arbitrary", "arbitrary")),
    )(*params, *inputs)
    return list(res[:nout]), list(res[nout:])


def _stage_bwd(st, t, params, inputs, saved, douts, dx_dtypes):
    g, tm = st.g, min(st.tm, t)
    n = t // tm
    npar, nin, ncar, nout = len(params), len(inputs), len(st.carry_shapes), len(st.out_pieces)
    flat_d = [d for ds in douts for d in ds]
    nd = len(flat_d)
    dx_idx = [i for i, dt in enumerate(dx_dtypes) if dt is not None]

    def body(*refs):
        p_refs = refs[:npar]
        x_refs = refs[npar:npar + nin]
        s_refs = refs[npar + nin:npar + nin + ncar]
        d_refs = refs[npar + nin + ncar:npar + nin + ncar + nd]
        o = npar + nin + ncar + nd
        dp_refs = refs[o:o + npar]
        dx_refs = refs[o + npar:o + npar + len(dx_idx)]
        dc_scr = refs[o + npar + len(dx_idx):]
        gi, ni = pl.program_id(0), pl.program_id(1)

        @pl.when(ni == 0)
        def _():
            for c in dc_scr:
                c[...] = jnp.zeros(c.shape, F32)

        ps = [r[...].astype(F32) for r in p_refs]
        xs = _load_pieces(x_refs, st.in_pieces)
        cs = [s[...] for s in s_refs]
        dys = []
        k = 0
        for ds, pieces in zip(douts, st.out_pieces):
            acc = _load_pieces([d_refs[k]], [pieces])
            for j in range(1, len(ds)):
                more = _load_pieces([d_refs[k + j]], [pieces])
                acc = [a + b for a, b in zip(acc, more)]
            dys += acc
            k += len(ds)
        _, vjp = jax.vjp(st.f, ps, xs, cs)
        dps, dxs, dcs = vjp((dys, [c[...] for c in dc_scr]))
        k = 0
        per_in = []
        for pieces in st.in_pieces:
            per_in.append(dxs[k:k + len(pieces)])
            k += len(pieces)
        for ref, i in zip(dx_refs, dx_idx):
            _store_pieces([ref], [st.in_pieces[i]], per_in[i])
        for c, v in zip(dc_scr, dcs):
            c[...] = v
        for ref, dp, pg in zip(dp_refs, dps, st.par_per_g):
            first = (ni == 0) if pg else ((ni == 0) & (gi == 0))

            @pl.when(first)
            def _():
                ref[...] = jnp.zeros(ref.shape, F32)

            ref[...] += dp

    in_specs = [_par_spec(p, pg, g) for p, pg in zip(params, st.par_per_g)]
    in_specs += [_row_spec(tm, sum(pc), off, n, True) for pc, off in zip(st.in_pieces, st.in_offs)]
    in_specs += [_carry_spec(s, n, True) for s in st.carry_shapes]
    for ds, pc in zip(douts, st.out_pieces):
        in_specs += [_row_spec(tm, sum(pc), 0, n, True) for _ in ds]
    out_specs = [_par_spec(p, pg, g) for p, pg in zip(params, st.par_per_g)]
    out_specs += [_row_spec(tm, sum(st.in_pieces[i]), 0, n, True) for i in dx_idx]
    out_shape = [jax.ShapeDtypeStruct(p.shape, F32) for p in params]
    out_shape += [jax.ShapeDtypeStruct((t, g * sum(st.in_pieces[i])), dx_dtypes[i]) for i in dx_idx]
    res = pl.pallas_call(
        body, name=st.name + "_bwd", grid=(g, n), in_specs=in_specs, out_specs=out_specs, out_shape=out_shape,
        scratch_shapes=[pltpu.VMEM(s, F32) for s in st.carry_shapes],
        compiler_params=_cparams(("arbitrary", "arbitrary")),
    )(*params, *inputs, *saved, *flat_d)
    return list(res[:npar]), list(res[npar:])


def _pick(n, cap):
    if n <= cap:
        return n
    best = LANES
    for k in range(1, n // LANES + 1):
        if (n // LANES) % k == 0 and k * LANES <= cap:
            best = k * LANES
    return best


def _mm(name, a, b, mode, out_dtype=F32, tm=512, tn=512, b_outer=False):
    m = a.shape[1] if mode == "tn" else a.shape[0]
    k = a.shape[0] if mode == "tn" else a.shape[1]
    n = b.shape[0] if mode == "nt" else b.shape[1]
    tm, tn = _pick(m, tm), _pick(n, tn)

    def body(a_ref, b_ref, o_ref):
        o_ref[...] = _raw_dot(a_ref[...], b_ref[...], mode, False).astype(o_ref.dtype)

    if b_outer:
        grid = (n // tn, m // tm)
        ij = lambda p, q: (q, p)
    else:
        grid = (m // tm, n // tn)
        ij = lambda p, q: (p, q)
    if mode == "tn":
        a_spec = pl.BlockSpec((k, tm), lambda p, q: (0, ij(p, q)[0]))
    else:
        a_spec = pl.BlockSpec((tm, k), lambda p, q: (ij(p, q)[0], 0))
    if mode == "nt":
        b_spec = pl.BlockSpec((tn, k), lambda p, q: (ij(p, q)[1], 0))
    else:
        b_spec = pl.BlockSpec((k, tn), lambda p, q: (0, ij(p, q)[1]))
    return pl.pallas_call(
        body, name=name, grid=grid, in_specs=[a_spec, b_spec],
        out_specs=pl.BlockSpec((tm, tn), lambda p, q: ij(p, q)),
        out_shape=jax.ShapeDtypeStruct((m, n), out_dtype),
        compiler_params=_cparams(("arbitrary", "arbitrary")),
    )(a, b)


def _loss_stage(t, g_post, h1, ff, tgt):
    tm = min(256, t)
    n = t // tm

    def body(g_ref, h_ref, f_ref, t_ref, loss_ref, dg_ref, dh_ref, df_ref):
        ni = pl.program_id(0)
        target = t_ref[...]

        def lossf(g, h1, ff):
            e = h1 + _rms(ff, g) - target
            return 0.5 * jnp.sum(jnp.mean(e * e, axis=-1))

        l, (dg, dh, df) = jax.value_and_grad(lossf, argnums=(0, 1, 2))(g_ref[...], h_ref[...], f_ref[...])

        @pl.when(ni == 0)
        def _():
            loss_ref[...] = jnp.zeros(loss_ref.shape, F32)
            dg_ref[...] = jnp.zeros(dg_ref.shape, F32)

        loss_ref[...] += jnp.full(loss_ref.shape, l, F32)
        dg_ref[...] += dg
        dh_ref[...] = dh
        df_ref[...] = df.astype(df_ref.dtype)

    row = pl.BlockSpec((tm, D), lambda ni: (ni, 0))
    one = pl.BlockSpec((1, D), lambda ni: (0, 0))
    return pl.pallas_call(
        body, name="loss_head", grid=(n,), in_specs=[one, row, row, row],
        out_specs=[pl.BlockSpec((1, LANES), lambda ni: (0, 0)), one, row, row],
        out_shape=[jax.ShapeDtypeStruct((1, LANES), F32), jax.ShapeDtypeStruct((1, D), F32),
                   jax.ShapeDtypeStruct((t, D), F32), jax.ShapeDtypeStruct((t, D), BF)],
        compiler_params=_cparams(("arbitrary",)),
    )(g_post, h1, ff, tgt)


_ANY = pl.BlockSpec(memory_space=pl.ANY)


def _all_gather(name, blk):
    r, c = blk.shape

    def body(x_ref, out_ref, send_sems, recv_sems, local_sem):
        x, y, cc = lax.axis_index("x"), lax.axis_index("y"), lax.axis_index("c")
        me, sibling = (x, y, cc), (x, y, 1 - cc)
        chips = [(1 - x, y), (x, 1 - y), (1 - x, 1 - y)]

        def rows(px, py, pc):
            return out_ref.at[4 * px + 2 * py + pc]

        def copy(k, block, to, src=None):
            return pltpu.make_async_remote_copy(
                src_ref=rows(*block) if src is None else src, dst_ref=rows(*block),
                send_sem=send_sems.at[k], recv_sem=recv_sems.at[k], device_id=to, device_id_type=MESH)

        mine = pltpu.make_async_copy(x_ref, rows(*me), local_sem)
        mine.start()
        first = [copy(0, me, sibling, src=x_ref)]
        first += [copy(1 + j, me, (*chip, cc), src=x_ref) for j, chip in enumerate(chips)]
        for cp in first:
            cp.start()
        passed = [copy(4 + j, (*chip, cc), sibling) for j, chip in enumerate(chips)]
        for j, chip in enumerate(chips):
            copy(1 + j, (*chip, cc), me).wait_recv()
            passed[j].start()
        copy(0, sibling, me).wait_recv()
        for j, chip in enumerate(chips):
            copy(4 + j, (*chip, 1 - cc), me).wait_recv()
        for cp in first + passed:
            cp.wait_send()
        mine.wait()

    return pl.pallas_call(
        body, name=name, in_specs=[_ANY], out_specs=_ANY,
        out_shape=jax.ShapeDtypeStruct((N_DEV, r, c), blk.dtype),
        scratch_shapes=[pltpu.SemaphoreType.DMA((7,)), pltpu.SemaphoreType.DMA((7,)), pltpu.SemaphoreType.DMA],
    )(blk)


def _swap_sibling(name, send):
    def body(s_ref, r_ref, ssem, rsem):
        x, y, cc = lax.axis_index("x"), lax.axis_index("y"), lax.axis_index("c")
        cp = pltpu.make_async_remote_copy(src_ref=s_ref, dst_ref=r_ref, send_sem=ssem, recv_sem=rsem,
                                          device_id=(x, y, 1 - cc), device_id_type=MESH)
        cp.start()
        cp.wait()

    return pl.pallas_call(
        body, name=name, in_specs=[_ANY], out_specs=_ANY, out_shape=jax.ShapeDtypeStruct(send.shape, send.dtype),
        scratch_shapes=[pltpu.SemaphoreType.DMA, pltpu.SemaphoreType.DMA],
    )(send)


def _swap_chips(name, send3):
    def body(s_ref, r_ref, ssems, rsems):
        x, y, cc = lax.axis_index("x"), lax.axis_index("y"), lax.axis_index("c")
        targets = [(1 - x, y, cc), (x, 1 - y, cc), (1 - x, 1 - y, cc)]
        cps = [pltpu.make_async_remote_copy(src_ref=s_ref.at[k], dst_ref=r_ref.at[k], send_sem=ssems.at[k],
                                            recv_sem=rsems.at[k], device_id=targets[k], device_id_type=MESH)
               for k in range(3)]
        for cp in cps:
            cp.start()
        for cp in cps:
            cp.wait()

    return pl.pallas_call(
        body, name=name, in_specs=[_ANY], out_specs=_ANY, out_shape=jax.ShapeDtypeStruct(send3.shape, send3.dtype),
        scratch_shapes=[pltpu.SemaphoreType.DMA((3,)), pltpu.SemaphoreType.DMA((3,))],
    )(send3)


def _pair_sum(own4, recv4):
    _, r, c = own4.shape
    tr = _pick_rows(r)

    def body0(a_ref, b_ref, o_ref):
        o_ref[...] = a_ref[...] + b_ref[...].astype(F32)

    def body3(a_ref, b_ref, o_ref):
        o_ref[...] = (a_ref[...] + b_ref[...].astype(F32)).astype(BF)

    blk0 = pl.BlockSpec((None, tr, c), lambda i: (0, i, 0))
    p0 = pl.pallas_call(
        body0, name="pair_sum_own", grid=(r // tr,), in_specs=[blk0, blk0],
        out_specs=pl.BlockSpec((tr, c), lambda i: (i, 0)), out_shape=jax.ShapeDtypeStruct((r, c), F32),
        compiler_params=_cparams(("arbitrary",)),
    )(own4, recv4)
    blk3 = pl.BlockSpec((None, tr, c), lambda k, i: (k + 1, i, 0))
    p3 = pl.pallas_call(
        body3, name="pair_sum_send", grid=(3, r // tr), in_specs=[blk3, blk3],
        out_specs=pl.BlockSpec((None, tr, c), lambda k, i: (k, i, 0)),
        out_shape=jax.ShapeDtypeStruct((3, r, c), BF),
        compiler_params=_cparams(("arbitrary", "arbitrary")),
    )(own4, recv4)
    return p0, p3


def _pick_rows(r):
    for tr in (384, 336, 256, 192, 128, 64, 32, 16, 8):
        if r % tr == 0:
            return tr
    return r


def _adamw(w, g, m, v):
    m = ADAM_B1 * m + (1.0 - ADAM_B1) * g
    v = ADAM_B2 * v + (1.0 - ADAM_B2) * jnp.square(g)
    m_hat = m / (1.0 - ADAM_B1 ** ADAM_STEP)
    v_hat = v / (1.0 - ADAM_B2 ** ADAM_STEP)
    delta = -ADAM_LR * (m_hat / (jnp.sqrt(v_hat) + ADAM_EPS) + ADAM_WD * w)
    return delta, m, v


def _adam_sharded(p0, recv3, w, m, v):
    r, c = w.shape
    tr = _pick_rows(r)

    def body(p_ref, r_ref, w_ref, m_ref, v_ref, g_out, d_out, m_out, v_out):
        g = p_ref[...]
        for k in range(3):
            g = g + r_ref[k].astype(F32)
        d, mn, vn = _adamw(w_ref[...], g, m_ref[...], v_ref[...])
        g_out[...] = g
        d_out[...] = d
        m_out[...] = mn
        v_out[...] = vn

    row = pl.BlockSpec((tr, c), lambda i: (i, 0))
    return pl.pallas_call(
        body, name="adam_sharded", grid=(r // tr,),
        in_specs=[row, pl.BlockSpec((3, tr, c), lambda i: (0, i, 0)), row, row, row],
        out_specs=[row] * 4, out_shape=[jax.ShapeDtypeStruct((r, c), F32)] * 4,
        compiler_params=_cparams(("arbitrary",)),
    )(p0, recv3, w, m, v)


def _adam_replicated(g8, w, m, v):
    def body(g_ref, w_ref, m_ref, v_ref, g_out, d_out, m_out, v_out):
        g = g_ref[0]
        for k in range(1, N_DEV):
            g = g + g_ref[k]
        d, mn, vn = _adamw(w_ref[...], g, m_ref[...], v_ref[...])
        g_out[...] = g
        d_out[...] = d
        m_out[...] = mn
        v_out[...] = vn

    return pl.pallas_call(
        body, name="adam_replicated", out_shape=[jax.ShapeDtypeStruct(w.shape, F32)] * 4,
        compiler_params=_cparams(),
    )(g8, w, m, v)


def _pad_rows(a, rows):
    return jnp.pad(a, ((0, rows - a.shape[0]), (0, 0)))


def _flat_rows(a, rows):
    f = a.reshape(-1)
    return jnp.pad(f, (0, rows * D - f.shape[0])).reshape(rows, D)


def _pack_shard(d, conv_bits):
    parts = []
    for name, rows in SHARD_ROWS:
        a = d[name]
        if name == "conv_w" and conv_bits:
            a = lax.bitcast_convert_type(a.astype(F32), BF)
        parts.append(_flat_rows(a, rows))
    return jnp.concatenate(parts, axis=0)


def _shard_slices(p):
    out, o = {}, 0
    for name, rows in SHARD_ROWS:
        out[name] = p[..., o:o + rows, :]
        o += rows
    return out


_SHARD_SHAPES = {"w_in": (1, 1024, 1184), "w_up": (1, 1024, 704), "w_down": (1, 352, 1024),
                 "w_branch_a": (1, 128, 1024), "w_branch_b": (1, 128, 1024), "w_out": (1, 128, 1024),
                 "rwkv_w2": (1, 64, 128), "rwkv_a2": (1, 64, 128), "rwkv_g2": (1, 128, 128), "conv_w": (1, 3, 704)}


def _unpack_shard(p):
    out = {}
    for name, a in _shard_slices(p).items():
        shp = _SHARD_SHAPES[name]
        cnt = shp[0] * shp[1] * shp[2]
        out[name] = a.reshape(-1)[:cnt].reshape(shp)
    return out


def _perm_hg_cols(w):
    r = w.shape[0]
    return w.reshape(r, 4, HG_HEADS, HG_K).transpose(0, 2, 1, 3).reshape(r, HG_COLS)


def _unperm_hg_cols(w):
    r = w.shape[0]
    return w.reshape(r, HG_HEADS, 4, HG_K).transpose(0, 2, 1, 3).reshape(r, HG_COLS)


def _unpack_gathered(gw):
    s = _shard_slices(gw)
    w_in = s["w_in"].reshape(N_DEV, D, 1184).transpose(1, 0, 2).reshape(D, IN_COLS)
    w_hp = _perm_hg_cols(w_in[:, :HG_COLS])
    w_r = w_in[:, HG_COLS:HG_COLS + RW_COLS]
    w_g = w_in[:, HG_COLS + RW_COLS:]
    w_up = s["w_up"].reshape(N_DEV, D, 704).transpose(1, 0, 2).reshape(D, 2 * DFF)
    w_down = s["w_down"].reshape(DFF, D)
    w2 = s["rwkv_w2"][:, :8].reshape(N_DEV, 64, 128).transpose(1, 0, 2).reshape(64, D)
    a2 = s["rwkv_a2"][:, :8].reshape(N_DEV, 64, 128).transpose(1, 0, 2).reshape(64, D)
    g2 = s["rwkv_g2"].reshape(N_DEV, 128, 128).transpose(1, 0, 2).reshape(128, D)
    z64 = jnp.zeros((64, D), BF)
    cw = s["conv_w"].reshape(N_DEV, -1)[:, :2 * 3 * 704].reshape(N_DEV, 3, 704, 2)
    cw = lax.bitcast_convert_type(cw, F32).transpose(1, 0, 2).reshape(3, 2 * DFF)
    return dict(w_hp=w_hp, w_r=w_r, w_g=w_g, w_all=jnp.concatenate([w_hp, w_g, w_r], axis=1), w_up=w_up,
                w_down=w_down, w_a=s["w_branch_a"].reshape(D, D), w_b=s["w_branch_b"].reshape(D, D),
                w_out=s["w_out"].reshape(D, D), w2p=jnp.concatenate([w2, z64], axis=0),
                a2p=jnp.concatenate([z64, a2], axis=0), g2=g2, conv_w=cw)


def _pack_grads(gr):
    def cols(w, per):
        r = w.shape[0]
        return w.reshape(r, N_DEV, per).transpose(1, 0, 2).reshape(N_DEV, -1, D)

    def padr(a, rows):
        return jnp.pad(a, ((0, 0), (0, rows - a.shape[1]), (0, 0)))

    parts = {
        "w_in": cols(gr["w_in"], 1184), "w_up": cols(gr["w_up"], 704), "w_down": gr["w_down"].reshape(N_DEV, 352, D),
        "w_branch_a": gr["w_branch_a"].reshape(N_DEV, 128, D), "w_branch_b": gr["w_branch_b"].reshape(N_DEV, 128, D),
        "w_out": gr["w_out"].reshape(N_DEV, 128, D), "rwkv_w2": padr(cols(gr["rwkv_w2"], 128), 16),
        "rwkv_a2": padr(cols(gr["rwkv_a2"], 128), 16), "rwkv_g2": cols(gr["rwkv_g2"], 128),
    }
    cw = gr["conv_w"].reshape(3, N_DEV, 704).transpose(1, 0, 2).reshape(N_DEV, 3 * 704)
    parts["conv_w"] = jnp.pad(cw, ((0, 0), (0, 16 * D - 3 * 704))).reshape(N_DEV, 16, D)
    return jnp.concatenate([parts[name] for name, _ in SHARD_ROWS], axis=1)


def _pack_repl(d):
    parts = [_flat_rows(d[name], rows) for name, rows in REPL_ROWS]
    return _pad_rows(jnp.concatenate(parts, axis=0), REPL_TOTAL)


def _unpack_repl(p, shapes):
    out, o = {}, 0
    for name, rows in REPL_ROWS:
        shp = shapes[name]
        cnt = 1
        for s in shp:
            cnt *= s
        out[name] = p[o:o + rows].reshape(-1)[:cnt].reshape(shp)
        o += rows
    return out


_WEIGHTS = ("attn_pre_norm", "w_in", "hgrn_lb", "hgrn_gnorm", "w_branch_a", "rwkv_mu", "rwkv_w0", "rwkv_w2",
            "rwkv_a0", "rwkv_a2", "rwkv_g2", "rwkv_k_k", "rwkv_k_a", "rwkv_r_k", "rwkv_ln_w", "rwkv_ln_b",
            "w_branch_b", "w_out", "attn_post_norm", "ffn_pre_norm", "w_up", "conv_w", "conv_b", "w_down",
            "ffn_post_norm")
_SHARDED = tuple(name for name, _ in SHARD_ROWS)


def _stages():
    one = [D]
    return dict(
        pre1=_Stage("pre1", _f_pre1, 1, 256, [False], [one], [0], [], [one], [BF]),
        hgrn=_Stage("hgrn", _f_hgrn, HG_HEADS, 128, [True, True], [[HG_K] * 4], [0], [(HG_K, HG_K)], [[HG_K]], [BF]),
        rwpre=_Stage("rwkv_pre", _f_rwpre, 1, 128, [False] * 8, [[D], [D], [D], [LANES], [LANES]], [6, 7, 8, 72, 73],
                     [(1, RW_COLS)], [one] * 7, [F32] * 7),
        rwscan=_Stage("rwkv_scan", _f_rwscan(True), RW_HEADS // 2, RW_CHUNK, [], [[LANES]] * 6, [0] * 6,
                      [(LANES, LANES)], [[LANES]], [F32]),
        rwpost=_Stage("rwkv_post", _f_rwpost, 1, 128, [False] * 3, [one] * 5, [0] * 5, [], [one], [BF]),
        merge=_Stage("merge", _f_merge, 1, 256, [], [one] * 4, [4, 5, 0, 0], [], [one], [BF]),
        post1=_Stage("post1", _f_post1, 1, 256, [False, False], [one, one], [0, 0], [], [one, one], [F32, BF]),
        conv=_Stage("conv", _f_conv, 1, 128, [False, False], [[DFF], [DFF]], [0, 1], [(1, 2 * DFF), (1, 2 * DFF)],
                    [[DFF]], [BF]),
    )


def kernel(x, attn_pre_norm, w_in, hgrn_lb, hgrn_gnorm, w_branch_a, rwkv_mu, rwkv_w0, rwkv_w2, rwkv_a0, rwkv_a2, rwkv_g2, rwkv_k_k, rwkv_k_a, rwkv_r_k, rwkv_ln_w, rwkv_ln_b, w_branch_b, w_out, attn_post_norm, ffn_pre_norm, w_up, conv_w, conv_b, w_down, ffn_post_norm, loss_target, m_attn_pre_norm, m_w_in, m_hgrn_lb, m_hgrn_gnorm, m_w_branch_a, m_rwkv_mu, m_rwkv_w0, m_rwkv_w2, m_rwkv_a0, m_rwkv_a2, m_rwkv_g2, m_rwkv_k_k, m_rwkv_k_a, m_rwkv_r_k, m_rwkv_ln_w, m_rwkv_ln_b, m_w_branch_b, m_w_out, m_attn_post_norm, m_ffn_pre_norm, m_w_up, m_conv_w, m_conv_b, m_w_down, m_ffn_post_norm, v_attn_pre_norm, v_w_in, v_hgrn_lb, v_hgrn_gnorm, v_w_branch_a, v_rwkv_mu, v_rwkv_w0, v_rwkv_w2, v_rwkv_a0, v_rwkv_a2, v_rwkv_g2, v_rwkv_k_k, v_rwkv_k_a, v_rwkv_r_k, v_rwkv_ln_w, v_rwkv_ln_b, v_w_branch_b, v_w_out, v_attn_post_norm, v_ffn_pre_norm, v_w_up, v_conv_w, v_conv_b, v_w_down, v_ffn_post_norm):
    w = dict(attn_pre_norm=attn_pre_norm, w_in=w_in, hgrn_lb=hgrn_lb, hgrn_gnorm=hgrn_gnorm, w_branch_a=w_branch_a, rwkv_mu=rwkv_mu, rwkv_w0=rwkv_w0, rwkv_w2=rwkv_w2, rwkv_a0=rwkv_a0, rwkv_a2=rwkv_a2, rwkv_g2=rwkv_g2, rwkv_k_k=rwkv_k_k, rwkv_k_a=rwkv_k_a, rwkv_r_k=rwkv_r_k, rwkv_ln_w=rwkv_ln_w, rwkv_ln_b=rwkv_ln_b, w_branch_b=w_branch_b, w_out=w_out, attn_post_norm=attn_post_norm, ffn_pre_norm=ffn_pre_norm, w_up=w_up, conv_w=conv_w, conv_b=conv_b, w_down=w_down, ffn_post_norm=ffn_post_norm)
    mo = dict(attn_pre_norm=m_attn_pre_norm, w_in=m_w_in, hgrn_lb=m_hgrn_lb, hgrn_gnorm=m_hgrn_gnorm, w_branch_a=m_w_branch_a, rwkv_mu=m_rwkv_mu, rwkv_w0=m_rwkv_w0, rwkv_w2=m_rwkv_w2, rwkv_a0=m_rwkv_a0, rwkv_a2=m_rwkv_a2, rwkv_g2=m_rwkv_g2, rwkv_k_k=m_rwkv_k_k, rwkv_k_a=m_rwkv_k_a, rwkv_r_k=m_rwkv_r_k, rwkv_ln_w=m_rwkv_ln_w, rwkv_ln_b=m_rwkv_ln_b, w_branch_b=m_w_branch_b, w_out=m_w_out, attn_post_norm=m_attn_post_norm, ffn_pre_norm=m_ffn_pre_norm, w_up=m_w_up, conv_w=m_conv_w, conv_b=m_conv_b, w_down=m_w_down, ffn_post_norm=m_ffn_post_norm)
    vo = dict(attn_pre_norm=v_attn_pre_norm, w_in=v_w_in, hgrn_lb=v_hgrn_lb, hgrn_gnorm=v_hgrn_gnorm, w_branch_a=v_w_branch_a, rwkv_mu=v_rwkv_mu, rwkv_w0=v_rwkv_w0, rwkv_w2=v_rwkv_w2, rwkv_a0=v_rwkv_a0, rwkv_a2=v_rwkv_a2, rwkv_g2=v_rwkv_g2, rwkv_k_k=v_rwkv_k_k, rwkv_k_a=v_rwkv_k_a, rwkv_r_k=v_rwkv_r_k, rwkv_ln_w=v_rwkv_ln_w, rwkv_ln_b=v_rwkv_ln_b, w_branch_b=v_w_branch_b, w_out=v_w_out, attn_post_norm=v_attn_post_norm, ffn_pre_norm=v_ffn_pre_norm, w_up=v_w_up, conv_w=v_conv_w, conv_b=v_conv_b, w_down=v_w_down, ffn_post_norm=v_ffn_post_norm)

    t = x.shape[1]
    x2 = x.reshape(t, D)
    tgt = loss_target.reshape(t, D)
    st = _stages()
    ax, ay, ac = lax.axis_index("x"), lax.axis_index("y"), lax.axis_index("c")

    gw = _all_gather("gather_weights", _pack_shard({k: w[k].astype(BF) if k != "conv_w" else w[k] for k in _SHARDED}, True))
    fw = _unpack_gathered(gw)
    r_k = rwkv_r_k.reshape(1, D)

    (xn,), _ = _stage_fwd(st["pre1"], t, [attn_pre_norm], [x2])
    z = _mm("in_proj", xn, fw["w_all"], "nn", F32, tm=256, tn=4736, b_outer=True)
    hg_par = [hgrn_lb, hgrn_gnorm]
    (o_a,), hg_saved = _stage_fwd(st["hgrn"], t, hg_par, [z])
    rwpre_par = [rwkv_mu, rwkv_w0, fw["w2p"], rwkv_a0, fw["a2p"], fw["g2"], rwkv_k_k, rwkv_k_a]
    rw_in, rwpre_saved = _stage_fwd(st["rwpre"], t, rwpre_par, [z] * 5)
    r_, lw_, k_, v_, av_, bv_, g_ = rw_in
    (y_,), rws_saved = _stage_fwd(st["rwscan"], t, [], [r_, lw_, k_, v_, av_, bv_])
    rwpost_par = [rwkv_ln_w, rwkv_ln_b, r_k]
    (o_b,), _ = _stage_fwd(st["rwpost"], t, rwpost_par, [y_, r_, k_, v_, g_])
    y_a = _mm("branch_a", o_a, fw["w_a"], "nn")
    y_b = _mm("branch_b", o_b, fw["w_b"], "nn")
    (merged,), _ = _stage_fwd(st["merge"], t, [], [z, z, y_a, y_b])
    mix = _mm("out_proj", merged, fw["w_out"], "nn")
    (h1, xn2), _ = _stage_fwd(st["post1"], t, [attn_post_norm, ffn_pre_norm], [x2, mix])
    hu = _mm("up_proj", xn2, fw["w_up"], "nn", F32, tm=512, tn=1408)
    conv_par = [fw["conv_w"], conv_b]
    (act,), conv_saved = _stage_fwd(st["conv"], t, conv_par, [hu, hu])
    ff = _mm("down_proj", act, fw["w_down"], "nn")

    loss_acc, d_ffn_post, dh1, dff = _loss_stage(t, ffn_post_norm, h1, ff, tgt)
    dact = _mm("d_act", dff, fw["w_down"], "nt", F32, tm=512, tn=1408)
    dw_down = _mm("dw_down", act, dff, "tn", F32, tm=1408, tn=512)
    (dcw, dcb), (dhu_g, dhu_v) = _stage_bwd(st["conv"], t, conv_par, [hu, hu], conv_saved, [[dact]], [BF, BF])
    dhu = jnp.concatenate([dhu_g, dhu_v], axis=1)
    dxn2 = _mm("d_xn2", dhu, fw["w_up"], "nt", F32, tm=256, tn=512)
    dw_up = _mm("dw_up", xn2, dhu, "tn", F32, tm=512, tn=1408)
    (d_post, d_pre2), (dx_a, dmix) = _stage_bwd(st["post1"], t, [attn_post_norm, ffn_pre_norm], [x2, mix], [],
                                                 [[dh1], [dxn2]], [F32, BF])
    dmerged = _mm("d_merged", dmix, fw["w_out"], "nt")
    dw_out = _mm("dw_out", merged, dmix, "tn")
    _, (dga, dgb, dy_a, dy_b) = _stage_bwd(st["merge"], t, [], [z, z, y_a, y_b], [], [[dmerged]], [BF, BF, BF, BF])
    do_a = _mm("d_oa", dy_a, fw["w_a"], "nt")
    dw_a = _mm("dw_a", o_a, dy_a, "tn")
    do_b = _mm("d_ob", dy_b, fw["w_b"], "nt")
    dw_b = _mm("dw_b", o_b, dy_b, "tn")
    (d_lnw, d_lnb, d_rk), (dy_, dr1, dk1, dv1, dg_) = _stage_bwd(
        st["rwpost"], t, rwpost_par, [y_, r_, k_, v_, g_], [], [[do_b]], [F32] * 5)
    _, (dr2, dlw, dk2, dv2, dav, dbv) = _stage_bwd(
        st["rwscan"], t, [], [r_, lw_, k_, v_, av_, bv_], rws_saved, [[dy_]], [F32] * 6)
    rwpre_dp, dz_r_parts = _stage_bwd(
        st["rwpre"], t, rwpre_par, [z] * 5, rwpre_saved,
        [[dr1, dr2], [dlw], [dk1, dk2], [dv1, dv2], [dav], [dbv], [dg_]], [BF] * 5)
    d_mu, d_w0, d_w2p, d_a0, d_a2p, d_g2, d_kk, d_ka = rwpre_dp
    dz_r = jnp.concatenate(dz_r_parts, axis=1)
    (d_lb, d_gn), (dz_h,) = _stage_bwd(st["hgrn"], t, hg_par, [z], hg_saved, [[do_a]], [BF])
    dz_g = jnp.concatenate([dga, dgb], axis=1)
    dxn_h = _mm("d_xn_h", dz_h, fw["w_hp"], "nt")
    dxn_r = _mm("d_xn_r", dz_r, fw["w_r"], "nt")
    dxn_g = _mm("d_xn_g", dz_g, fw["w_g"], "nt")
    dw_hp = _mm("dw_h", xn, dz_h, "tn", F32, tm=512, tn=1024)
    dw_r = _mm("dw_r", xn, dz_r, "tn", F32, tm=512, tn=1664)
    dw_g = _mm("dw_g", xn, dz_g, "tn", F32, tm=512, tn=1024)
    (d_pre1,), (dx_b,) = _stage_bwd(st["pre1"], t, [attn_pre_norm], [x2], [], [[dxn_h, dxn_r, dxn_g]], [F32])
    grad_x = (dx_a + dx_b).reshape(x.shape)
    loss = lax.psum(loss_acc[0, 0], ("x", "y", "c"))

    full = dict(w_in=jnp.concatenate([_unperm_hg_cols(dw_hp), dw_r, dw_g], axis=1), w_up=dw_up, w_down=dw_down,
                w_branch_a=dw_a, w_branch_b=dw_b, w_out=dw_out, rwkv_w2=d_w2p[:64], rwkv_a2=d_a2p[64:], rwkv_g2=d_g2,
                conv_w=dcw)
    g5 = _pack_grads(full).reshape(2, 2, 2, SHARD_TOTAL, D)
    chips = [(ax, ay), (1 - ax, ay), (ax, 1 - ay), (1 - ax, 1 - ay)]
    own4 = jnp.stack([g5[cx, cy, ac] for cx, cy in chips])
    send4 = jnp.stack([g5[cx, cy, 1 - ac] for cx, cy in chips]).astype(BF)
    recv4 = _swap_sibling("reduce_pair", send4)
    p0, p3 = _pair_sum(own4, recv4)
    recv3 = _swap_chips("reduce_chips", p3)
    packs = [_pack_shard({k: src[k] for k in _SHARDED}, False) for src in (w, mo, vo)]
    sh_out = [_unpack_shard(o) for o in _adam_sharded(p0, recv3, *packs)]

    rg = dict(attn_pre_norm=d_pre1, hgrn_lb=d_lb, hgrn_gnorm=d_gn, rwkv_mu=d_mu, rwkv_w0=d_w0, rwkv_a0=d_a0,
              rwkv_k_k=d_kk, rwkv_k_a=d_ka, rwkv_r_k=d_rk, rwkv_ln_w=d_lnw, rwkv_ln_b=d_lnb, attn_post_norm=d_post,
              ffn_pre_norm=d_pre2, conv_b=dcb, ffn_post_norm=d_ffn_post)
    g8 = _all_gather("gather_small_grads", _pack_repl(rg))
    shapes = {name: w[name].shape for name, _ in REPL_ROWS}
    rp_out = [_unpack_repl(o, shapes) for o in _adam_replicated(g8, *[_pack_repl(src) for src in (w, mo, vo)])]

    outs = [loss, grad_x]
    for kind in range(4):
        for name in _WEIGHTS:
            outs.append(sh_out[kind][name] if name in _SHARDED else rp_out[kind][name])
    return tuple(outs)
```

```python
import functools

import jax
import jax.numpy as jnp
from jax import lax
from jax.experimental import pallas as pl
from jax.experimental.pallas import tpu as pltpu

F32 = jnp.float32
BF = jnp.bfloat16
MESH = pl.DeviceIdType.MESH

D = 1024
HG_HEADS = 8
HG_K = 128
HG_CHUNK = 32
HG_SCALE = HG_K ** -0.5
HG_PER_STEP = 2
RW_HEADS = 16
RW_N = 64
RW_CHUNK = 64
RW_PAIRS_PER_STEP = 4
DFF = 2816
IN_COLS = 9472
RW_COLS = 3328
EPS = 1e-6
GN_EPS = 1e-5 * RW_N
ADAM_LR = 0.001
ADAM_B1 = 0.9
ADAM_B2 = 0.999
ADAM_EPS = 1e-08
ADAM_WD = 0.01
ADAM_STEP = 10
N_DEV = 8
LANES = 128
VMEM_LIMIT = 56 * 1024 * 1024
TILE_BYTES = 1280 * 1024

REPL = (("attn_pre_norm", 1024), ("hgrn_lb", 1024), ("hgrn_gnorm", 1024), ("rwkv_mu", 3328), ("rwkv_w0", 1024),
        ("rwkv_a0", 1024), ("rwkv_k_k", 1024), ("rwkv_k_a", 1024), ("rwkv_r_k", 1024), ("rwkv_ln_w", 1024),
        ("rwkv_ln_b", 1024), ("attn_post_norm", 1024), ("ffn_pre_norm", 1024), ("conv_b", 5632), ("ffn_post_norm", 1024))
REPL_ROWS = {"hgrn_lb": 2}
REPL_TOTAL = 32


def _cparams(sem=None, **kw):
    return pltpu.CompilerParams(dimension_semantics=sem, vmem_limit_bytes=VMEM_LIMIT, **kw)


_DN = {"nn": ((1,), (0,)), "nt": ((1,), (1,)), "tn": ((0,), (0,))}


def _raw_dot(a, b, mode):
    return lax.dot_general(a.astype(BF), b.astype(BF), (_DN[mode], ((), ())), preferred_element_type=F32)


@functools.partial(jax.custom_vjp, nondiff_argnums=(2,))
def _dot(a, b, mode):
    return _raw_dot(a, b, mode)


def _dot_fwd(a, b, mode):
    return _raw_dot(a, b, mode), (a, b)


def _dot_bwd(mode, res, g):
    a, b = res
    if mode == "nn":
        return _dot(g, b, "nt"), _dot(a, g, "tn")
    if mode == "nt":
        return _dot(g, b, "nn"), _dot(g, a, "tn")
    return _dot(b, g, "nt"), _dot(a, g, "nn")


_dot.defvjp(_dot_fwd, _dot_bwd)


def _bf_pieces(x, n):
    out, r = [], x
    for i in range(n):
        p = r.astype(BF)
        out.append(p)
        if i + 1 < n:
            r = r - p.astype(F32)
    return out


def _raw_split_dot(x, e, mode, n, x_left):
    eb = e.astype(BF)
    acc = None
    for p in _bf_pieces(x, n):
        ops = (p, eb) if x_left else (eb, p)
        t = lax.dot_general(*ops, (_DN[mode], ((), ())), preferred_element_type=F32)
        acc = t if acc is None else acc + t
    return acc


@functools.partial(jax.custom_vjp, nondiff_argnums=(2, 3))
def _edot(x, e, mode, n):
    return _raw_split_dot(x, e, mode, n, True)


def _edot_fwd(x, e, mode, n):
    return _raw_split_dot(x, e, mode, n, True), e


def _edot_bwd(mode, n, e, g):
    return _raw_split_dot(g, e, "nt" if mode == "nn" else "nn", n, True), jnp.zeros_like(e)


_edot.defvjp(_edot_fwd, _edot_bwd)


@functools.partial(jax.custom_vjp, nondiff_argnums=(2,))
def _tdot(tri, x, n):
    return _raw_split_dot(x, tri, "nn", n, False)


def _tdot_fwd(tri, x, n):
    return _raw_split_dot(x, tri, "nn", n, False), tri


def _tdot_bwd(n, tri, g):
    return jnp.zeros_like(tri), _raw_split_dot(g, tri, "tn", n, False)


_tdot.defvjp(_tdot_fwd, _tdot_bwd)


def _row(x, i):
    r = lax.broadcasted_iota(jnp.int32, x.shape, 0)
    return jnp.sum(jnp.where(r == i, x, 0.0), axis=0, keepdims=True)


def _shift_down(x, prev):
    t = x.shape[0]

    @jax.custom_vjp
    def sh(x, prev):
        r = lax.broadcasted_iota(jnp.int32, x.shape, 0)
        return jnp.where(r == 0, prev, pltpu.roll(x, 1, 0))

    def fwd(x, prev):
        return sh(x, prev), None

    def bwd(_, g):
        r = lax.broadcasted_iota(jnp.int32, g.shape, 0)
        dx = jnp.where(r == t - 1, 0.0, pltpu.roll(g, t - 1, 0))
        return dx, jnp.sum(jnp.where(r == 0, g, 0.0), axis=0, keepdims=True)

    sh.defvjp(fwd, bwd)
    return sh(x, prev)


def _sigmoid(x):
    return jax.nn.sigmoid(x)


def _silu(x):
    return x * jax.nn.sigmoid(x)


def _softplus(x):
    return jnp.maximum(x, 0.0) + jnp.log(1.0 + jnp.exp(-jnp.abs(x)))


def _rms(x, g):
    return (x * lax.rsqrt(jnp.mean(x * x, axis=-1, keepdims=True) + EPS)) * g


def _headmat():
    j = lax.broadcasted_iota(jnp.int32, (D, LANES), 0)
    h = lax.broadcasted_iota(jnp.int32, (D, LANES), 1)
    e = jnp.where(lax.shift_right_logical(j, 6) == h, 1.0, 0.0).astype(F32)
    pad = jnp.where(lax.broadcasted_iota(jnp.int32, (1, LANES), 1) >= RW_HEADS, 1.0, 0.0).astype(F32)
    return e, pad


def _tril(c):
    r = lax.broadcasted_iota(jnp.int32, (c, c), 0)
    cc = lax.broadcasted_iota(jnp.int32, (c, c), 1)
    return cc <= r


def _f_pre1(ps, xs, cs):
    return [_rms(xs[0], ps[0])], []


def _hgrn_head(lbraw, gn, hq, hf, hi, hg, st):
    l0, l1 = _row(lbraw, 0), _row(lbraw, 1)
    m = jnp.maximum(l0, l1)
    e0, e1 = jnp.exp(l0 - m), jnp.exp(l1 - m)
    lb = e0 / (e0 + e1)
    q = _silu(hq) * HG_SCALE
    f = lb + (1.0 - lb) * _sigmoid(hf)
    kh = 1.0 - f
    gl = jnp.log(f)
    c = HG_CHUNK
    low = _tril(c)
    tri = jnp.where(low, 1.0, 0.0).astype(F32)
    outs = []
    for i in range(hq.shape[0] // c):
        sl = slice(i * c, (i + 1) * c)
        qc, kc, vc = q[sl], kh[sl], hi[sl]
        b = _tdot(tri, gl[sl], 3)
        bref = _row(b, c // 2 - 1)
        blast = _row(b, c - 1)
        sc = _dot(qc * jnp.exp(b - bref), kc * jnp.exp(bref - b), "nt")
        sc = jnp.where(low, sc, 0.0)
        o = _dot(sc, vc, "nn") + _dot(qc * jnp.exp(b), st, "nt")
        u = _dot(vc, kc * jnp.exp(blast - b), "tn")
        st = jnp.exp(blast) * st + u
        outs.append(o)
    o = outs[0] if len(outs) == 1 else jnp.concatenate(outs, axis=0)
    o = o * lax.rsqrt(jnp.mean(o * o, axis=-1, keepdims=True) + EPS)
    o = o * gn
    return o * _silu(hg), st


def _f_hgrn(ps, xs, cs):
    lbraw, gn = ps
    hq, hf, hi, hg = xs
    (st,) = cs
    outs, sts = [], []
    for p in range(HG_PER_STEP):
        sl = slice(p * HG_K, (p + 1) * HG_K)
        o, s = _hgrn_head(lbraw[:, sl], gn[:, sl], hq[:, sl], hf[:, sl], hi[:, sl], hg[:, sl], st[sl])
        outs.append(o)
        sts.append(s)
    return [jnp.concatenate(outs, axis=1)], [jnp.concatenate(sts, axis=0)]


_RW_OFFS = (0, 1024, 2048, 3072, 3200, 3328)


def _f_rwpre(ps, xs, cs):
    mu, w0, w2p, a0, a2p, g2, k_k, k_a = ps
    (prev,) = cs
    t = xs[0].shape[0]
    zs = []
    for i, z in enumerate(xs):
        lo, hi = _RW_OFFS[i], _RW_OFFS[i + 1]
        zs.append(z + mu[:, lo:hi] * (_shift_down(z, prev[:, lo:hi]) - z))
    rr, kr, vr, wa, gz = zs
    w_log = -_softplus(-(w0 + _dot(jnp.tanh(wa), w2p, "nn"))) - 0.5
    lw = -jnp.exp(w_log)
    a = _sigmoid(a0 + _dot(wa, a2p, "nn"))
    g = _dot(_sigmoid(gz), g2, "nn")
    e, pad = _headmat()
    kkr = kr * k_k
    nrm = jnp.sqrt(_edot(kkr * kkr, e, "nn", 2) + pad)
    kk = kkr / _edot(jnp.maximum(nrm, 1e-12), e, "nt", 2)
    k2 = kr * (1.0 + (a - 1.0) * k_a)
    newprev = jnp.concatenate([_row(z, t - 1) for z in xs], axis=1)
    return [rr, lw, k2, vr, -kk, kk * a, g], [newprev]


def _rwkv_pair(r, lw, k, v, av, bv, sv):
    c = RW_CHUNK
    n = 2 * c
    tri = jnp.where(_tril(c), 1.0, 0.0).astype(F32)
    cl = _tdot(tri, lw, 3)
    cl_last = _row(cl, c - 1)
    lane = lax.broadcasted_iota(jnp.int32, (c, LANES), 1)
    h0 = lane < RW_N

    def stack(x):
        return jnp.concatenate([jnp.where(h0, x, 0.0), jnp.where(h0, 0.0, x)], axis=0)

    am = stack(av * jnp.exp(cl - lw))
    bm = stack(bv * jnp.exp(-cl))
    km = stack(k * jnp.exp(-cl))
    rm = stack(r * jnp.exp(cl))
    vm = stack(v)
    rn = lax.broadcasted_iota(jnp.int32, (n, n), 0)
    cn = lax.broadcasted_iota(jnp.int32, (n, n), 1)
    blk = (rn >= c) == (cn >= c)
    strict = blk & (cn < rn)
    incl = blk & (cn <= rn)
    lab = jnp.where(strict, _dot(am, bm, "nt"), 0.0)
    lak = jnp.where(strict, _dot(am, km, "nt"), 0.0)
    wrb = jnp.where(incl, _dot(rm, bm, "nt"), 0.0)
    wrk = jnp.where(incl, _dot(rm, km, "nt"), 0.0)
    tinv = jnp.where(rn == cn, 1.0, 0.0).astype(F32) + lab
    p = lab
    for _ in range(5):
        p = _dot(p, p, "nn")
        tinv = tinv + _dot(tinv, p, "nn")
    um = _dot(tinv, _dot(am, sv, "nt") + _dot(lak, vm, "nn"), "nn")
    ym = _dot(rm, sv, "nt") + _dot(wrb, um, "nn") + _dot(wrk, vm, "nn")
    sn = (sv + _dot(um, bm, "tn") + _dot(vm, km, "tn")) * jnp.exp(cl_last)
    return ym[:c] + ym[c:], sn


def _f_rwscan(ps, xs, cs):
    (sv,) = cs
    ys, svs = [], []
    for p in range(RW_PAIRS_PER_STEP):
        sl = slice(p * LANES, (p + 1) * LANES)
        y, s = _rwkv_pair(*[x[:, sl] for x in xs], sv[sl])
        ys.append(y)
        svs.append(s)
    return [jnp.concatenate(ys, axis=1)], [jnp.concatenate(svs, axis=0)]


def _f_rwpost(ps, xs, cs):
    ln_w, ln_b, r_k = ps
    y, r, k, v, g = xs
    e, _ = _headmat()
    inv_n = 1.0 / RW_N
    mu = _edot(y, e, "nn", 2) * inv_n
    yc = y - _edot(mu, e, "nt", 2)
    var = _edot(yc * yc, e, "nn", 2) * inv_n
    yn = yc * _edot(lax.rsqrt(var + GN_EPS), e, "nt", 2)
    yn = yn * ln_w + ln_b
    bonus = _edot(_edot(r * k * r_k, e, "nn", 2), e, "nt", 2) * v
    return [(yn + bonus) * g], []


def _f_merge(ps, xs, cs):
    ga, gb, ya, yb = xs
    return [_sigmoid(ga) * ya + _sigmoid(gb) * yb], []


def _f_post1(ps, xs, cs):
    x, mix = xs
    h1 = x + _rms(mix, ps[0])
    return [h1, _rms(h1, ps[1])], []


def _f_conv(ps, xs, cs):
    cw, cb = ps
    p1, p2 = cs
    w0, w1, w2 = _row(cw, 0), _row(cw, 1), _row(cw, 2)
    t = xs[0].shape[0]
    hc = []
    for i, x in enumerate(xs):
        sl = slice(i * DFF, (i + 1) * DFF)
        s1 = _shift_down(x, p1[:, sl])
        s2 = _shift_down(s1, p2[:, sl])
        hc.append(cb[:, sl] + w0[:, sl] * s2 + w1[:, sl] * s1 + w2[:, sl] * x)
    n1 = jnp.concatenate([_row(x, t - 1) for x in xs], axis=1)
    n2 = jnp.concatenate([_row(x, t - 2) for x in xs], axis=1)
    return [_silu(hc[0]) * hc[1]], [n1, n2]


class _Stage:
    def __init__(self, name, f, g, tm, par_per_g, in_pieces, in_offs, carry_shapes, out_pieces, out_dtypes):
        self.name, self.f, self.g, self.tm = name, f, g, tm
        self.par_per_g, self.in_pieces, self.in_offs = par_per_g, in_pieces, in_offs
        self.carry_shapes, self.out_pieces, self.out_dtypes = carry_shapes, out_pieces, out_dtypes


def _par_spec(arr, per_g, g):
    r, c = arr.shape
    if per_g:
        return pl.BlockSpec((r, c // g), lambda gi, ni: (0, gi))
    return pl.BlockSpec((r, c), lambda gi, ni: (0, 0))


def _row_spec(tm, width, off, n, rev):
    if rev:
        return pl.BlockSpec((tm, width), lambda gi, ni: (n - 1 - ni, off + gi))
    return pl.BlockSpec((tm, width), lambda gi, ni: (ni, off + gi))


def _carry_spec(shape, n, rev):
    if rev:
        return pl.BlockSpec((None, None) + shape, lambda gi, ni: (gi, n - 1 - ni, 0, 0))
    return pl.BlockSpec((None, None) + shape, lambda gi, ni: (gi, ni, 0, 0))


def _load_pieces(refs, pieces_list):
    out = []
    for ref, pieces in zip(refs, pieces_list):
        o = 0
        for w in pieces:
            out.append(ref[:, o:o + w].astype(F32))
            o += w
    return out


def _store_pieces(refs, pieces_list, vals):
    k = 0
    for ref, pieces in zip(refs, pieces_list):
        o = 0
        for w in pieces:
            ref[:, o:o + w] = vals[k].astype(ref.dtype)
            k += 1
            o += w


def _stage_fwd(st, t, params, inputs):
    g, tm = st.g, min(st.tm, t)
    n = t // tm
    npar, nin, ncar, nout = len(params), len(inputs), len(st.carry_shapes), len(st.out_pieces)

    def body(*refs):
        p_refs = refs[:npar]
        x_refs = refs[npar:npar + nin]
        o_refs = refs[npar + nin:npar + nin + nout]
        s_refs = refs[npar + nin + nout:npar + nin + nout + ncar]
        c_scr = refs[npar + nin + nout + ncar:]
        ni = pl.program_id(1)

        @pl.when(ni == 0)
        def _():
            for c in c_scr:
                c[...] = jnp.zeros(c.shape, F32)

        ps = [r[...].astype(F32) for r in p_refs]
        xs = _load_pieces(x_refs, st.in_pieces)
        cs = [c[...] for c in c_scr]
        for s, c in zip(s_refs, cs):
            s[...] = c
        outs, ncs = st.f(ps, xs, cs)
        _store_pieces(o_refs, st.out_pieces, outs)
        for c, v in zip(c_scr, ncs):
            c[...] = v

    in_specs = [_par_spec(p, pg, g) for p, pg in zip(params, st.par_per_g)]
    in_specs += [_row_spec(tm, sum(pc), off, n, False) for pc, off in zip(st.in_pieces, st.in_offs)]
    out_specs = [_row_spec(tm, sum(pc), 0, n, False) for pc in st.out_pieces]
    out_specs += [_carry_spec(s, n, False) for s in st.carry_shapes]
    out_shape = [jax.ShapeDtypeStruct((t, g * sum(pc)), dt) for pc, dt in zip(st.out_pieces, st.out_dtypes)]
    out_shape += [jax.ShapeDtypeStruct((g, n) + s, F32) for s in st.carry_shapes]
    res = pl.pallas_call(
        body, name=st.name + "_fwd", grid=(g, n), in_specs=in_specs, out_specs=out_specs, out_shape=out_shape,
        scratch_shapes=[pltpu.VMEM(s, F32) for s in st.carry_shapes],
        compiler_params=_cparams(("arbitrary", "arbitrary")),
    )(*params, *inputs)
    return list(res[:nout]), list(res[nout:])


def _stage_bwd(st, t, params, inputs, saved, douts, dx_dtypes):
    g, tm = st.g, min(st.tm, t)
    n = t // tm
    npar, nin, ncar = len(params), len(inputs), len(st.carry_shapes)
    flat_d = [d for ds in douts for d in ds]
    nd = len(flat_d)
    dx_idx = [i for i, dt in enumerate(dx_dtypes) if dt is not None]

    def body(*refs):
        p_refs = refs[:npar]
        x_refs = refs[npar:npar + nin]
        s_refs = refs[npar + nin:npar + nin + ncar]
        d_refs = refs[npar + nin + ncar:npar + nin + ncar + nd]
        o = npar + nin + ncar + nd
        dp_refs = refs[o:o + npar]
        dx_refs = refs[o + npar:o + npar + len(dx_idx)]
        dc_scr = refs[o + npar + len(dx_idx):]
        gi, ni = pl.program_id(0), pl.program_id(1)

        @pl.when(ni == 0)
        def _():
            for c in dc_scr:
                c[...] = jnp.zeros(c.shape, F32)

        ps = [r[...].astype(F32) for r in p_refs]
        xs = _load_pieces(x_refs, st.in_pieces)
        cs = [s[...] for s in s_refs]
        dys = []
        k = 0
        for ds, pieces in zip(douts, st.out_pieces):
            acc = _load_pieces([d_refs[k]], [pieces])
            for j in range(1, len(ds)):
                more = _load_pieces([d_refs[k + j]], [pieces])
                acc = [a + b for a, b in zip(acc, more)]
            dys += acc
            k += len(ds)
        _, vjp = jax.vjp(st.f, ps, xs, cs)
        dps, dxs, dcs = vjp((dys, [c[...] for c in dc_scr]))
        k = 0
        per_in = []
        for pieces in st.in_pieces:
            per_in.append(dxs[k:k + len(pieces)])
            k += len(pieces)
        for ref, i in zip(dx_refs, dx_idx):
            _store_pieces([ref], [st.in_pieces[i]], per_in[i])
        for c, v in zip(dc_scr, dcs):
            c[...] = v
        for ref, dp, pg in zip(dp_refs, dps, st.par_per_g):
            first = (ni == 0) if pg else ((ni == 0) & (gi == 0))

            @pl.when(first)
            def _():
                ref[...] = jnp.zeros(ref.shape, F32)

            ref[...] += dp

    in_specs = [_par_spec(p, pg, g) for p, pg in zip(params, st.par_per_g)]
    in_specs += [_row_spec(tm, sum(pc), off, n, True) for pc, off in zip(st.in_pieces, st.in_offs)]
    in_specs += [_carry_spec(s, n, True) for s in st.carry_shapes]
    for ds, pc in zip(douts, st.out_pieces):
        in_specs += [_row_spec(tm, sum(pc), 0, n, True) for _ in ds]
    out_specs = [_par_spec(p, pg, g) for p, pg in zip(params, st.par_per_g)]
    out_specs += [_row_spec(tm, sum(st.in_pieces[i]), 0, n, True) for i in dx_idx]
    out_shape = [jax.ShapeDtypeStruct(p.shape, F32) for p in params]
    out_shape += [jax.ShapeDtypeStruct((t, g * sum(st.in_pieces[i])), dx_dtypes[i]) for i in dx_idx]
    res = pl.pallas_call(
        body, name=st.name + "_bwd", grid=(g, n), in_specs=in_specs, out_specs=out_specs, out_shape=out_shape,
        scratch_shapes=[pltpu.VMEM(s, F32) for s in st.carry_shapes],
        compiler_params=_cparams(("arbitrary", "arbitrary")),
    )(*params, *inputs, *saved, *flat_d)
    return list(res[:npar]), list(res[npar:])


def _pick(n, cap):
    if n <= cap:
        return n
    best = LANES
    for k in range(1, n // LANES + 1):
        if (n // LANES) % k == 0 and k * LANES <= cap:
            best = k * LANES
    return best


def _mm(name, a, b, mode, out_dtype=F32, tm=512, tn=512, b_outer=False):
    m = a.shape[1] if mode == "tn" else a.shape[0]
    k = a.shape[0] if mode == "tn" else a.shape[1]
    n = b.shape[0] if mode == "nt" else b.shape[1]
    tm, tn = _pick(m, tm), _pick(n, tn)

    def body(a_ref, b_ref, o_ref):
        o_ref[...] = _raw_dot(a_ref[...], b_ref[...], mode).astype(o_ref.dtype)

    if b_outer:
        grid = (n // tn, m // tm)
        ij = lambda p, q: (q, p)
    else:
        grid = (m // tm, n // tn)
        ij = lambda p, q: (p, q)
    if mode == "tn":
        a_spec = pl.BlockSpec((k, tm), lambda p, q: (0, ij(p, q)[0]))
    else:
        a_spec = pl.BlockSpec((tm, k), lambda p, q: (ij(p, q)[0], 0))
    if mode == "nt":
        b_spec = pl.BlockSpec((tn, k), lambda p, q: (ij(p, q)[1], 0))
    else:
        b_spec = pl.BlockSpec((k, tn), lambda p, q: (0, ij(p, q)[1]))
    return pl.pallas_call(
        body, name=name, grid=grid, in_specs=[a_spec, b_spec],
        out_specs=pl.BlockSpec((tm, tn), lambda p, q: ij(p, q)),
        out_shape=jax.ShapeDtypeStruct((m, n), out_dtype),
        compiler_params=_cparams(("arbitrary", "arbitrary")),
    )(a, b)


def _loss_stage(t, g_post, h1, ff, tgt):
    tm = min(256, t)
    n = t // tm

    def body(g_ref, h_ref, f_ref, t_ref, loss_ref, dg_ref, dh_ref, df_ref):
        ni = pl.program_id(0)
        target = t_ref[...]

        def lossf(g, h1, ff):
            e = h1 + _rms(ff, g) - target
            return 0.5 * jnp.sum(jnp.mean(e * e, axis=-1))

        l, (dg, dh, df) = jax.value_and_grad(lossf, argnums=(0, 1, 2))(g_ref[...], h_ref[...], f_ref[...])

        @pl.when(ni == 0)
        def _():
            loss_ref[...] = jnp.zeros(loss_ref.shape, F32)
            dg_ref[...] = jnp.zeros(dg_ref.shape, F32)

        loss_ref[...] += jnp.full(loss_ref.shape, l, F32)
        dg_ref[...] += dg
        dh_ref[...] = dh
        df_ref[...] = df.astype(df_ref.dtype)

    row = pl.BlockSpec((tm, D), lambda ni: (ni, 0))
    one = pl.BlockSpec((1, D), lambda ni: (0, 0))
    return pl.pallas_call(
        body, name="loss_head", grid=(n,), in_specs=[one, row, row, row],
        out_specs=[pl.BlockSpec((1, LANES), lambda ni: (0, 0)), one, row, row],
        out_shape=[jax.ShapeDtypeStruct((1, LANES), F32), jax.ShapeDtypeStruct((1, D), F32),
                   jax.ShapeDtypeStruct((t, D), F32), jax.ShapeDtypeStruct((t, D), BF)],
        compiler_params=_cparams(("arbitrary",)),
    )(g_post, h1, ff, tgt)


_ANY = pl.BlockSpec(memory_space=pl.ANY)


def _all_gather(name, blks):
    na = len(blks)

    def body(*refs):
        x_refs, out_refs = refs[:na], refs[na:2 * na]
        send_sems, recv_sems, local_sems = refs[2 * na:]
        x, y, cc = lax.axis_index("x"), lax.axis_index("y"), lax.axis_index("c")
        me, sibling = (x, y, cc), (x, y, 1 - cc)
        chips = [(1 - x, y), (x, 1 - y), (1 - x, 1 - y)]

        def copy(a, k, block, to, src=None):
            dst = out_refs[a].at[4 * block[0] + 2 * block[1] + block[2]]
            return pltpu.make_async_remote_copy(
                src_ref=dst if src is None else src, dst_ref=dst, send_sem=send_sems.at[7 * a + k],
                recv_sem=recv_sems.at[7 * a + k], device_id=to, device_id_type=MESH)

        mine, first, passed = [], [], []
        for a in range(na):
            m = pltpu.make_async_copy(x_refs[a], out_refs[a].at[4 * x + 2 * y + cc], local_sems.at[a])
            m.start()
            mine.append(m)
            cps = [copy(a, 0, me, sibling, src=x_refs[a])]
            cps += [copy(a, 1 + j, me, (*chip, cc), src=x_refs[a]) for j, chip in enumerate(chips)]
            for cp in cps:
                cp.start()
            first += cps
        for j, chip in enumerate(chips):
            for a in range(na):
                copy(a, 1 + j, (*chip, cc), me).wait_recv()
                fw = copy(a, 4 + j, (*chip, cc), sibling)
                fw.start()
                passed.append(fw)
        for a in range(na):
            copy(a, 0, sibling, me).wait_recv()
            for j, chip in enumerate(chips):
                copy(a, 4 + j, (*chip, 1 - cc), me).wait_recv()
        for cp in first + passed:
            cp.wait_send()
        for m in mine:
            m.wait()

    res = pl.pallas_call(
        body, name=name, in_specs=[_ANY] * na, out_specs=[_ANY] * na,
        out_shape=[jax.ShapeDtypeStruct((N_DEV,) + b.shape, b.dtype) for b in blks],
        scratch_shapes=[pltpu.SemaphoreType.DMA((7 * na,)), pltpu.SemaphoreType.DMA((7 * na,)),
                        pltpu.SemaphoreType.DMA((na,))],
    )(*blks)
    return list(res)


def _reduce_pair(g8s):
    na = len(g8s)

    def body(*refs):
        g_refs, own_refs, recv_refs = refs[:na], refs[na:2 * na], refs[2 * na:3 * na]
        ssem, rsem, lsem = refs[3 * na:]
        x, y, cc = lax.axis_index("x"), lax.axis_index("y"), lax.axis_index("c")
        chips = [(x, y), (1 - x, y), (x, 1 - y), (1 - x, 1 - y)]
        sib = (x, y, 1 - cc)
        for a in range(na):
            for k, (cx, cy) in enumerate(chips):
                pltpu.make_async_remote_copy(
                    src_ref=g_refs[a].at[4 * cx + 2 * cy + 1 - cc], dst_ref=recv_refs[a].at[k],
                    send_sem=ssem.at[a], recv_sem=rsem.at[a], device_id=sib, device_id_type=MESH).start()
                pltpu.make_async_copy(g_refs[a].at[4 * cx + 2 * cy + cc], own_refs[a].at[k], lsem.at[a]).start()
        for a in range(na):
            pltpu.make_async_remote_copy(src_ref=recv_refs[a], dst_ref=recv_refs[a], send_sem=ssem.at[a],
                                         recv_sem=rsem.at[a], device_id=sib, device_id_type=MESH).wait()
            pltpu.make_async_copy(own_refs[a], own_refs[a], lsem.at[a]).wait()

    shp = [jax.ShapeDtypeStruct((4,) + g.shape[1:], g.dtype) for g in g8s]
    res = pl.pallas_call(
        body, name="reduce_pair", in_specs=[_ANY] * na, out_specs=[_ANY] * (2 * na), out_shape=shp + shp,
        scratch_shapes=[pltpu.SemaphoreType.DMA((na,)), pltpu.SemaphoreType.DMA((na,)), pltpu.SemaphoreType.DMA((na,))],
    )(*g8s)
    return list(res[:na]), list(res[na:])


def _swap_chips(name, sends):
    na = len(sends)

    def body(*refs):
        s_refs, r_refs = refs[:na], refs[na:2 * na]
        ssems, rsems = refs[2 * na:]
        x, y, cc = lax.axis_index("x"), lax.axis_index("y"), lax.axis_index("c")
        targets = [(1 - x, y, cc), (x, 1 - y, cc), (1 - x, 1 - y, cc)]
        cps = [pltpu.make_async_remote_copy(src_ref=s_refs[a].at[k], dst_ref=r_refs[a].at[k],
                                            send_sem=ssems.at[3 * a + k], recv_sem=rsems.at[3 * a + k],
                                            device_id=targets[k], device_id_type=MESH)
               for a in range(na) for k in range(3)]
        for cp in cps:
            cp.start()
        for cp in cps:
            cp.wait()

    res = pl.pallas_call(
        body, name=name, in_specs=[_ANY] * na, out_specs=[_ANY] * na,
        out_shape=[jax.ShapeDtypeStruct(s.shape, s.dtype) for s in sends],
        scratch_shapes=[pltpu.SemaphoreType.DMA((3 * na,)), pltpu.SemaphoreType.DMA((3 * na,))],
    )(*sends)
    return list(res)


def _pick_rows(r, c):
    if r * c * 4 <= TILE_BYTES or r % 16:
        return r
    best = 16
    for tr in range(16, r, 16):
        if r % tr == 0 and tr * c * 4 <= TILE_BYTES:
            best = tr
    return best


def _pair_sum(name, own4, recv4):
    _, r, c = own4.shape
    tr = _pick_rows(r, c)

    def body(a_ref, b_ref, o0_ref, o3_ref):
        k = pl.program_id(1)
        s = a_ref[...].astype(F32) + b_ref[...].astype(F32)

        @pl.when(k == 0)
        def _():
            o0_ref[...] = s

        @pl.when(k > 0)
        def _():
            o3_ref[...] = s.astype(BF)

    blk = pl.BlockSpec((None, tr, c), lambda i, k: (k, i, 0))
    return pl.pallas_call(
        body, name=name, grid=(r // tr, 4), in_specs=[blk, blk],
        out_specs=[pl.BlockSpec((tr, c), lambda i, k: (i, 0)),
                   pl.BlockSpec((None, tr, c), lambda i, k: (jnp.maximum(k - 1, 0), i, 0))],
        out_shape=[jax.ShapeDtypeStruct((r, c), F32), jax.ShapeDtypeStruct((3, r, c), BF)],
        compiler_params=_cparams(("arbitrary", "arbitrary")),
    )(own4, recv4)


def _adamw(w, g, m, v):
    m = ADAM_B1 * m + (1.0 - ADAM_B1) * g
    v = ADAM_B2 * v + (1.0 - ADAM_B2) * jnp.square(g)
    m_hat = m / (1.0 - ADAM_B1 ** ADAM_STEP)
    v_hat = v / (1.0 - ADAM_B2 ** ADAM_STEP)
    delta = -ADAM_LR * (m_hat / (jnp.sqrt(v_hat) + ADAM_EPS) + ADAM_WD * w)
    return delta, m, v


def _adam_sharded(name, p0, recv3, w, m, v):
    r, c = w.shape
    tr = _pick_rows(r, c)

    def body(p_ref, r_ref, w_ref, m_ref, v_ref, g_out, d_out, m_out, v_out):
        g = p_ref[...]
        for k in range(3):
            g = g + r_ref[k].astype(F32)
        d, mn, vn = _adamw(w_ref[...], g, m_ref[...], v_ref[...])
        g_out[...] = g
        d_out[...] = d
        m_out[...] = mn
        v_out[...] = vn

    row = pl.BlockSpec((tr, c), lambda i: (i, 0))
    return pl.pallas_call(
        body, name=name, grid=(r // tr,),
        in_specs=[row, pl.BlockSpec((3, tr, c), lambda i: (0, i, 0)), row, row, row],
        out_specs=[row] * 4, out_shape=[jax.ShapeDtypeStruct((r, c), F32)] * 4,
        compiler_params=_cparams(("arbitrary",)),
    )(p0, recv3, w, m, v)


def _repl_rows():
    rows, r = {}, 0
    for name, cols in REPL:
        rows[name] = r
        r += REPL_ROWS.get(name, 1) * ((cols + D - 1) // D)
    return rows


def _pack_replicated(grads):
    rows = _repl_rows()
    names = [n for n, _ in REPL]

    def body(*refs):
        o_ref = refs[-1]
        o_ref[...] = jnp.zeros(o_ref.shape, F32)
        for name, ref in zip(names, refs[:-1]):
            r0 = rows[name]
            nr, nc = ref.shape
            if nc <= D:
                o_ref[r0:r0 + nr, 0:nc] = ref[...]
            else:
                for j in range((nc + D - 1) // D):
                    lo, hi = j * D, min(nc, (j + 1) * D)
                    o_ref[r0 + j:r0 + j + 1, 0:hi - lo] = ref[:, lo:hi]

    return pl.pallas_call(body, name="pack_replicated", out_shape=jax.ShapeDtypeStruct((REPL_TOTAL, D), F32),
                          compiler_params=_cparams())(*[grads[n] for n in names])


def _adam_replicated(g8, ws, ms, vs):
    rows = _repl_rows()
    names = [n for n, _ in REPL]
    np_ = len(names)

    def body(*refs):
        g_ref = refs[0]
        w_refs, m_refs, v_refs = refs[1:1 + np_], refs[1 + np_:1 + 2 * np_], refs[1 + 2 * np_:1 + 3 * np_]
        outs = refs[1 + 3 * np_:1 + 7 * np_]
        scr = refs[-1]
        g = g_ref[0]
        for k in range(1, N_DEV):
            g = g + g_ref[k]
        scr[...] = g
        for i, name in enumerate(names):
            r0 = rows[name]
            nr, nc = w_refs[i].shape
            if nc <= D:
                gi = scr[r0:r0 + nr, 0:nc]
            else:
                parts = []
                for j in range((nc + D - 1) // D):
                    lo, hi = j * D, min(nc, (j + 1) * D)
                    parts.append(scr[r0 + j:r0 + j + 1, 0:hi - lo])
                gi = jnp.concatenate(parts, axis=1)
            d, mn, vn = _adamw(w_refs[i][...], gi, m_refs[i][...], v_refs[i][...])
            outs[i][...] = gi
            outs[np_ + i][...] = d
            outs[2 * np_ + i][...] = mn
            outs[3 * np_ + i][...] = vn

    shp = [jax.ShapeDtypeStruct(w.shape, F32) for w in ws]
    res = pl.pallas_call(body, name="adam_replicated", out_shape=shp * 4,
                         scratch_shapes=[pltpu.VMEM((REPL_TOTAL, D), F32)], compiler_params=_cparams(),
                         )(g8, *ws, *ms, *vs)
    return [dict(zip(names, res[k * np_:(k + 1) * np_])) for k in range(4)]


_WEIGHTS = ("attn_pre_norm", "w_in", "hgrn_lb", "hgrn_gnorm", "w_branch_a", "rwkv_mu", "rwkv_w0", "rwkv_w2",
            "rwkv_a0", "rwkv_a2", "rwkv_g2", "rwkv_k_k", "rwkv_k_a", "rwkv_r_k", "rwkv_ln_w", "rwkv_ln_b",
            "w_branch_b", "w_out", "attn_post_norm", "ffn_pre_norm", "w_up", "conv_w", "conv_b", "w_down",
            "ffn_post_norm")
_BIG = ("w_in", "w_up", "w_down", "w_branch_a", "w_branch_b", "w_out")


def _stages():
    one = [D]
    hw = HG_K * HG_PER_STEP
    rw = LANES * RW_PAIRS_PER_STEP
    return dict(
        pre1=_Stage("pre1", _f_pre1, 1, 256, [False], [one], [0], [], [one], [BF]),
        hgrn=_Stage("hgrn", _f_hgrn, HG_HEADS // HG_PER_STEP, 128, [True, True], [[hw]] * 4,
                    [i * D // hw for i in range(4)], [(hw, HG_K)], [[hw]], [BF]),
        rwpre=_Stage("rwkv_pre", _f_rwpre, 1, 128, [False] * 8, [[D], [D], [D], [LANES], [LANES]], [4, 5, 6, 56, 57],
                     [(1, RW_COLS)], [one] * 7, [F32] * 7),
        rwscan=_Stage("rwkv_scan", _f_rwscan, RW_HEADS // 2 // RW_PAIRS_PER_STEP, RW_CHUNK, [], [[rw]] * 6, [0] * 6,
                      [(rw, LANES)], [[rw]], [F32]),
        rwpost=_Stage("rwkv_post", _f_rwpost, 1, 128, [False] * 3, [one] * 5, [0] * 5, [], [one], [BF]),
        merge=_Stage("merge", _f_merge, 4, 512, [], [[256]] * 4, [29, 33, 0, 0], [], [[256]], [BF]),
        post1=_Stage("post1", _f_post1, 1, 256, [False, False], [one, one], [0, 0], [], [one, one], [F32, BF]),
        conv=_Stage("conv", _f_conv, 1, 128, [False, False], [[DFF], [DFF]], [0, 1], [(1, 2 * DFF), (1, 2 * DFF)],
                    [[DFF]], [BF]),
    )


def _cols_to_blocks(w, per):
    return w.reshape(w.shape[0], N_DEV, per).transpose(1, 0, 2)


def _blocks_to_cols(g):
    return g.transpose(1, 0, 2).reshape(g.shape[1], N_DEV * g.shape[2])


def kernel(x, attn_pre_norm, w_in, hgrn_lb, hgrn_gnorm, w_branch_a, rwkv_mu, rwkv_w0, rwkv_w2, rwkv_a0, rwkv_a2, rwkv_g2, rwkv_k_k, rwkv_k_a, rwkv_r_k, rwkv_ln_w, rwkv_ln_b, w_branch_b, w_out, attn_post_norm, ffn_pre_norm, w_up, conv_w, conv_b, w_down, ffn_post_norm, loss_target, m_attn_pre_norm, m_w_in, m_hgrn_lb, m_hgrn_gnorm, m_w_branch_a, m_rwkv_mu, m_rwkv_w0, m_rwkv_w2, m_rwkv_a0, m_rwkv_a2, m_rwkv_g2, m_rwkv_k_k, m_rwkv_k_a, m_rwkv_r_k, m_rwkv_ln_w, m_rwkv_ln_b, m_w_branch_b, m_w_out, m_attn_post_norm, m_ffn_pre_norm, m_w_up, m_conv_w, m_conv_b, m_w_down, m_ffn_post_norm, v_attn_pre_norm, v_w_in, v_hgrn_lb, v_hgrn_gnorm, v_w_branch_a, v_rwkv_mu, v_rwkv_w0, v_rwkv_w2, v_rwkv_a0, v_rwkv_a2, v_rwkv_g2, v_rwkv_k_k, v_rwkv_k_a, v_rwkv_r_k, v_rwkv_ln_w, v_rwkv_ln_b, v_w_branch_b, v_w_out, v_attn_post_norm, v_ffn_pre_norm, v_w_up, v_conv_w, v_conv_b, v_w_down, v_ffn_post_norm):
    w = dict(attn_pre_norm=attn_pre_norm, w_in=w_in, hgrn_lb=hgrn_lb, hgrn_gnorm=hgrn_gnorm, w_branch_a=w_branch_a, rwkv_mu=rwkv_mu, rwkv_w0=rwkv_w0, rwkv_w2=rwkv_w2, rwkv_a0=rwkv_a0, rwkv_a2=rwkv_a2, rwkv_g2=rwkv_g2, rwkv_k_k=rwkv_k_k, rwkv_k_a=rwkv_k_a, rwkv_r_k=rwkv_r_k, rwkv_ln_w=rwkv_ln_w, rwkv_ln_b=rwkv_ln_b, w_branch_b=w_branch_b, w_out=w_out, attn_post_norm=attn_post_norm, ffn_pre_norm=ffn_pre_norm, w_up=w_up, conv_w=conv_w, conv_b=conv_b, w_down=w_down, ffn_post_norm=ffn_post_norm)
    mo = dict(attn_pre_norm=m_attn_pre_norm, w_in=m_w_in, hgrn_lb=m_hgrn_lb, hgrn_gnorm=m_hgrn_gnorm, w_branch_a=m_w_branch_a, rwkv_mu=m_rwkv_mu, rwkv_w0=m_rwkv_w0, rwkv_w2=m_rwkv_w2, rwkv_a0=m_rwkv_a0, rwkv_a2=m_rwkv_a2, rwkv_g2=m_rwkv_g2, rwkv_k_k=m_rwkv_k_k, rwkv_k_a=m_rwkv_k_a, rwkv_r_k=m_rwkv_r_k, rwkv_ln_w=m_rwkv_ln_w, rwkv_ln_b=m_rwkv_ln_b, w_branch_b=m_w_branch_b, w_out=m_w_out, attn_post_norm=m_attn_post_norm, ffn_pre_norm=m_ffn_pre_norm, w_up=m_w_up, conv_w=m_conv_w, conv_b=m_conv_b, w_down=m_w_down, ffn_post_norm=m_ffn_post_norm)
    vo = dict(attn_pre_norm=v_attn_pre_norm, w_in=v_w_in, hgrn_lb=v_hgrn_lb, hgrn_gnorm=v_hgrn_gnorm, w_branch_a=v_w_branch_a, rwkv_mu=v_rwkv_mu, rwkv_w0=v_rwkv_w0, rwkv_w2=v_rwkv_w2, rwkv_a0=v_rwkv_a0, rwkv_a2=v_rwkv_a2, rwkv_g2=v_rwkv_g2, rwkv_k_k=v_rwkv_k_k, rwkv_k_a=v_rwkv_k_a, rwkv_r_k=v_rwkv_r_k, rwkv_ln_w=v_rwkv_ln_w, rwkv_ln_b=v_rwkv_ln_b, w_branch_b=v_w_branch_b, w_out=v_w_out, attn_post_norm=v_attn_post_norm, ffn_pre_norm=v_ffn_pre_norm, w_up=v_w_up, conv_w=v_conv_w, conv_b=v_conv_b, w_down=v_w_down, ffn_post_norm=v_ffn_post_norm)

    t = x.shape[1]
    x2 = x.reshape(t, D)
    tgt = loss_target.reshape(t, D)
    st = _stages()

    small = jnp.concatenate([rwkv_w2[0], rwkv_a2[0], rwkv_g2[0]], axis=0).astype(BF)
    conv_bits = lax.bitcast_convert_type(conv_w[0], BF).reshape(3, 2 * 704)
    gw = _all_gather("gather_weights", [w[k][0].astype(BF) for k in _BIG] + [small, conv_bits])
    fw_in = _blocks_to_cols(gw[0])
    fw_up = _blocks_to_cols(gw[1])
    fw_down = gw[2].reshape(DFF, D)
    fw_a, fw_b, fw_out = (g.reshape(D, D) for g in gw[3:6])
    z64 = jnp.zeros((64, D), BF)
    w2p = jnp.concatenate([_blocks_to_cols(gw[6][:, 0:64]), z64], axis=0)
    a2p = jnp.concatenate([z64, _blocks_to_cols(gw[6][:, 64:128])], axis=0)
    g2f = _blocks_to_cols(gw[6][:, 128:256])
    conv_full = _blocks_to_cols(lax.bitcast_convert_type(gw[7].reshape(N_DEV, 3, 704, 2), F32))
    r_k = rwkv_r_k.reshape(1, D)

    (xn,), _ = _stage_fwd(st["pre1"], t, [attn_pre_norm], [x2])
    z = _mm("in_proj", xn, fw_in, "nn", F32, tm=256, tn=4736, b_outer=True)
    hg_par = [hgrn_lb, hgrn_gnorm]
    (o_a,), hg_saved = _stage_fwd(st["hgrn"], t, hg_par, [z] * 4)
    rwpre_par = [rwkv_mu, rwkv_w0, w2p, rwkv_a0, a2p, g2f, rwkv_k_k, rwkv_k_a]
    rw_in, rwpre_saved = _stage_fwd(st["rwpre"], t, rwpre_par, [z] * 5)
    r_, lw_, k_, v_, av_, bv_, g_ = rw_in
    (y_,), rws_saved = _stage_fwd(st["rwscan"], t, [], [r_, lw_, k_, v_, av_, bv_])
    rwpost_par = [rwkv_ln_w, rwkv_ln_b, r_k]
    (o_b,), _ = _stage_fwd(st["rwpost"], t, rwpost_par, [y_, r_, k_, v_, g_])
    y_a = _mm("branch_a", o_a, fw_a, "nn")
    y_b = _mm("branch_b", o_b, fw_b, "nn")
    (merged,), _ = _stage_fwd(st["merge"], t, [], [z, z, y_a, y_b])
    mix = _mm("out_proj", merged, fw_out, "nn")
    (h1, xn2), _ = _stage_fwd(st["post1"], t, [attn_post_norm, ffn_pre_norm], [x2, mix])
    hu = _mm("up_proj", xn2, fw_up, "nn", F32, tm=512, tn=1408)
    conv_par = [conv_full, conv_b]
    (act,), conv_saved = _stage_fwd(st["conv"], t, conv_par, [hu, hu])
    ff = _mm("down_proj", act, fw_down, "nn")

    loss_acc, d_ffn_post, dh1, dff = _loss_stage(t, ffn_post_norm, h1, ff, tgt)
    dact = _mm("d_act", dff, fw_down, "nt", F32, tm=512, tn=1408)
    dw_down = _mm("dw_down", act, dff, "tn", BF, tm=1408, tn=512)
    (dcw, dcb), (dhu_g, dhu_v) = _stage_bwd(st["conv"], t, conv_par, [hu, hu], conv_saved, [[dact]], [BF, BF])
    dhu = jnp.concatenate([dhu_g, dhu_v], axis=1)
    dxn2 = _mm("d_xn2", dhu, fw_up, "nt", F32, tm=256, tn=512)
    dw_up = _mm("dw_up", xn2, dhu, "tn", BF, tm=512, tn=1408)
    (d_post, d_pre2), (dx_a, dmix) = _stage_bwd(st["post1"], t, [attn_post_norm, ffn_pre_norm], [x2, mix], [],
                                                 [[dh1], [dxn2]], [F32, BF])
    dmerged = _mm("d_merged", dmix, fw_out, "nt")
    dw_out = _mm("dw_out", merged, dmix, "tn", BF)
    _, (dga, dgb, dy_a, dy_b) = _stage_bwd(st["merge"], t, [], [z, z, y_a, y_b], [], [[dmerged]], [BF, BF, BF, BF])
    do_a = _mm("d_oa", dy_a, fw_a, "nt")
    dw_a = _mm("dw_a", o_a, dy_a, "tn", BF)
    do_b = _mm("d_ob", dy_b, fw_b, "nt")
    dw_b = _mm("dw_b", o_b, dy_b, "tn", BF)
    (d_lnw, d_lnb, d_rk), (dy_, dr1, dk1, dv1, dg_) = _stage_bwd(
        st["rwpost"], t, rwpost_par, [y_, r_, k_, v_, g_], [], [[do_b]], [F32] * 5)
    _, (dr2, dlw, dk2, dv2, dav, dbv) = _stage_bwd(
        st["rwscan"], t, [], [r_, lw_, k_, v_, av_, bv_], rws_saved, [[dy_]], [F32] * 6)
    rwpre_dp, dz_r = _stage_bwd(
        st["rwpre"], t, rwpre_par, [z] * 5, rwpre_saved,
        [[dr1, dr2], [dlw], [dk1, dk2], [dv1, dv2], [dav], [dbv], [dg_]], [BF] * 5)
    d_mu, d_w0, d_w2p, d_a0, d_a2p, d_g2, d_kk, d_ka = rwpre_dp
    (d_lb, d_gn), dz_h = _stage_bwd(st["hgrn"], t, hg_par, [z] * 4, hg_saved, [[do_a]], [BF] * 4)
    dz = jnp.concatenate(dz_h + dz_r + [dga, dgb], axis=1)
    dxn = _mm("d_xn", dz, fw_in, "nt", F32, tm=256, tn=512)
    dw_in = _mm("dw_in", xn, dz, "tn", BF, tm=1024, tn=256)
    (d_pre1,), (dx_b,) = _stage_bwd(st["pre1"], t, [attn_pre_norm], [x2], [], [[dxn]], [F32])
    grad_x = (dx_a + dx_b).reshape(x.shape)
    loss = lax.psum(loss_acc[0, 0], ("x", "y", "c"))

    d_small = jnp.concatenate([d_w2p[:64], d_a2p[64:], d_g2], axis=0).astype(BF)
    g8s = [_cols_to_blocks(dw_in, 1184), _cols_to_blocks(dw_up, 704), dw_down.reshape(N_DEV, 352, D),
           dw_a.reshape(N_DEV, 128, D), dw_b.reshape(N_DEV, 128, D), dw_out.reshape(N_DEV, 128, D),
           _cols_to_blocks(d_small, LANES), _cols_to_blocks(dcw.astype(BF), 704)]
    own4s, recv4s = _reduce_pair(g8s)
    names = list(_BIG) + ["small", "conv_w"]
    sums = [_pair_sum("pair_sum_" + n, o, r) for n, o, r in zip(names, own4s, recv4s)]
    recv3s = _swap_chips("reduce_chips", [s[1] for s in sums])

    def small_of(src):
        return jnp.concatenate([src["rwkv_w2"][0], src["rwkv_a2"][0], src["rwkv_g2"][0]], axis=0)

    sh_out = [dict() for _ in range(4)]
    for i, n in enumerate(names):
        if n == "small":
            packs = [small_of(src) for src in (w, mo, vo)]
        else:
            packs = [src[n][0] for src in (w, mo, vo)]
        res = _adam_sharded("adam_" + n, sums[i][0], recv3s[i], *packs)
        for kind in range(4):
            if n == "small":
                sh_out[kind]["rwkv_w2"] = res[kind][0:64][None]
                sh_out[kind]["rwkv_a2"] = res[kind][64:128][None]
                sh_out[kind]["rwkv_g2"] = res[kind][128:256][None]
            else:
                sh_out[kind][n] = res[kind][None]

    rg = dict(attn_pre_norm=d_pre1, hgrn_lb=d_lb, hgrn_gnorm=d_gn, rwkv_mu=d_mu, rwkv_w0=d_w0, rwkv_a0=d_a0,
              rwkv_k_k=d_kk, rwkv_k_a=d_ka, rwkv_r_k=d_rk, rwkv_ln_w=d_lnw, rwkv_ln_b=d_lnb, attn_post_norm=d_post,
              ffn_pre_norm=d_pre2, conv_b=dcb, ffn_post_norm=d_ffn_post)
    (g8,) = _all_gather("gather_small_grads", [_pack_replicated(rg)])
    rnames = [n for n, _ in REPL]
    flat = lambda src: [src[n].reshape(1, D) if n == "rwkv_r_k" else src[n] for n in rnames]
    rp_out = _adam_replicated(g8, flat(w), flat(mo), flat(vo))
    for kind in range(4):
        rp_out[kind]["rwkv_r_k"] = rp_out[kind]["rwkv_r_k"].reshape(rwkv_r_k.shape)

    outs = [loss, grad_x]
    for kind in range(4):
        for name in _WEIGHTS:
            outs.append(sh_out[kind][name] if name in sh_out[kind] else rp_out[kind][name])
    return tuple(outs)
```

```python
import functools

import jax
import jax.numpy as jnp
from jax import lax
from jax.experimental import pallas as pl
from jax.experimental.pallas import tpu as pltpu

F32 = jnp.float32
BF = jnp.bfloat16
MESH = pl.DeviceIdType.MESH

D = 1024
HG_HEADS = 8
HG_K = 128
HG_CHUNK = 32
HG_SCALE = HG_K ** -0.5
HG_PER_STEP = 2
RW_HEADS = 16
RW_N = 64
RW_CHUNK = 64
RW_PAIRS_PER_STEP = 4
DFF = 2816
IN_COLS = 9472
RW_COLS = 3328
EPS = 1e-6
GN_EPS = 1e-5 * RW_N
ADAM_LR = 0.001
ADAM_B1 = 0.9
ADAM_B2 = 0.999
ADAM_EPS = 1e-08
ADAM_WD = 0.01
ADAM_STEP = 10
N_DEV = 8
LANES = 128
VMEM_LIMIT = 56 * 1024 * 1024
TILE_BYTES = 1280 * 1024

REPL = (("attn_pre_norm", 1024), ("hgrn_lb", 1024), ("hgrn_gnorm", 1024), ("rwkv_mu", 3328), ("rwkv_w0", 1024),
        ("rwkv_a0", 1024), ("rwkv_k_k", 1024), ("rwkv_k_a", 1024), ("rwkv_r_k", 1024), ("rwkv_ln_w", 1024),
        ("rwkv_ln_b", 1024), ("attn_post_norm", 1024), ("ffn_pre_norm", 1024), ("conv_b", 5632), ("ffn_post_norm", 1024))
REPL_ROWS = {"hgrn_lb": 2}
REPL_TOTAL = 32


def _cparams(sem=None, **kw):
    return pltpu.CompilerParams(dimension_semantics=sem, vmem_limit_bytes=VMEM_LIMIT, **kw)


_DN = {"nn": ((1,), (0,)), "nt": ((1,), (1,)), "tn": ((0,), (0,))}


def _raw_dot(a, b, mode):
    return lax.dot_general(a.astype(BF), b.astype(BF), (_DN[mode], ((), ())), preferred_element_type=F32)


@functools.partial(jax.custom_vjp, nondiff_argnums=(2,))
def _dot(a, b, mode):
    return _raw_dot(a, b, mode)


def _dot_fwd(a, b, mode):
    return _raw_dot(a, b, mode), (a, b)


def _dot_bwd(mode, res, g):
    a, b = res
    if mode == "nn":
        return _dot(g, b, "nt"), _dot(a, g, "tn")
    if mode == "nt":
        return _dot(g, b, "nn"), _dot(g, a, "tn")
    return _dot(b, g, "nt"), _dot(a, g, "nn")


_dot.defvjp(_dot_fwd, _dot_bwd)


def _bf_pieces(x, n):
    out, r = [], x
    for i in range(n):
        p = r.astype(BF)
        out.append(p)
        if i + 1 < n:
            r = r - p.astype(F32)
    return out


def _raw_split_dot(x, e, mode, n, x_left):
    eb = e.astype(BF)
    acc = None
    for p in _bf_pieces(x, n):
        ops = (p, eb) if x_left else (eb, p)
        t = lax.dot_general(*ops, (_DN[mode], ((), ())), preferred_element_type=F32)
        acc = t if acc is None else acc + t
    return acc


@functools.partial(jax.custom_vjp, nondiff_argnums=(2, 3))
def _edot(x, e, mode, n):
    return _raw_split_dot(x, e, mode, n, True)


def _edot_fwd(x, e, mode, n):
    return _raw_split_dot(x, e, mode, n, True), e


def _edot_bwd(mode, n, e, g):
    return _raw_split_dot(g, e, "nt" if mode == "nn" else "nn", n, True), jnp.zeros_like(e)


_edot.defvjp(_edot_fwd, _edot_bwd)


@functools.partial(jax.custom_vjp, nondiff_argnums=(2,))
def _tdot(tri, x, n):
    return _raw_split_dot(x, tri, "nn", n, False)


def _tdot_fwd(tri, x, n):
    return _raw_split_dot(x, tri, "nn", n, False), tri


def _tdot_bwd(n, tri, g):
    return jnp.zeros_like(tri), _raw_split_dot(g, tri, "tn", n, False)


_tdot.defvjp(_tdot_fwd, _tdot_bwd)


def _row(x, i):
    r = lax.broadcasted_iota(jnp.int32, x.shape, 0)
    return jnp.sum(jnp.where(r == i, x, 0.0), axis=0, keepdims=True)


def _shift_down(x, prev):
    t = x.shape[0]

    @jax.custom_vjp
    def sh(x, prev):
        r = lax.broadcasted_iota(jnp.int32, x.shape, 0)
        return jnp.where(r == 0, prev, pltpu.roll(x, 1, 0))

    def fwd(x, prev):
        return sh(x, prev), None

    def bwd(_, g):
        r = lax.broadcasted_iota(jnp.int32, g.shape, 0)
        dx = jnp.where(r == t - 1, 0.0, pltpu.roll(g, t - 1, 0))
        return dx, jnp.sum(jnp.where(r == 0, g, 0.0), axis=0, keepdims=True)

    sh.defvjp(fwd, bwd)
    return sh(x, prev)


def _sigmoid(x):
    return jax.nn.sigmoid(x)


def _silu(x):
    return x * jax.nn.sigmoid(x)


def _softplus(x):
    return jnp.maximum(x, 0.0) + jnp.log(1.0 + jnp.exp(-jnp.abs(x)))


def _rms(x, g):
    return (x * lax.rsqrt(jnp.mean(x * x, axis=-1, keepdims=True) + EPS)) * g


def _headmat():
    j = lax.broadcasted_iota(jnp.int32, (D, LANES), 0)
    h = lax.broadcasted_iota(jnp.int32, (D, LANES), 1)
    e = jnp.where(lax.shift_right_logical(j, 6) == h, 1.0, 0.0).astype(F32)
    pad = jnp.where(lax.broadcasted_iota(jnp.int32, (1, LANES), 1) >= RW_HEADS, 1.0, 0.0).astype(F32)
    return e, pad


def _tril(c):
    r = lax.broadcasted_iota(jnp.int32, (c, c), 0)
    cc = lax.broadcasted_iota(jnp.int32, (c, c), 1)
    return cc <= r


def _f_pre1(ps, xs, cs):
    return [_rms(xs[0], ps[0])], []


def _hgrn_head(lbraw, gn, hq, hf, hi, hg, st):
    l0, l1 = _row(lbraw, 0), _row(lbraw, 1)
    m = jnp.maximum(l0, l1)
    e0, e1 = jnp.exp(l0 - m), jnp.exp(l1 - m)
    lb = e0 / (e0 + e1)
    q = _silu(hq) * HG_SCALE
    f = lb + (1.0 - lb) * _sigmoid(hf)
    kh = 1.0 - f
    gl = jnp.log(f)
    c = HG_CHUNK
    low = _tril(c)
    tri = jnp.where(low, 1.0, 0.0).astype(F32)
    outs = []
    for i in range(hq.shape[0] // c):
        sl = slice(i * c, (i + 1) * c)
        qc, kc, vc = q[sl], kh[sl], hi[sl]
        b = _tdot(tri, gl[sl], 3)
        bref = _row(b, c // 2 - 1)
        blast = _row(b, c - 1)
        sc = _dot(qc * jnp.exp(b - bref), kc * jnp.exp(bref - b), "nt")
        sc = jnp.where(low, sc, 0.0)
        o = _dot(sc, vc, "nn") + _dot(qc * jnp.exp(b), st, "nt")
        u = _dot(vc, kc * jnp.exp(blast - b), "tn")
        st = jnp.exp(blast) * st + u
        outs.append(o)
    o = outs[0] if len(outs) == 1 else jnp.concatenate(outs, axis=0)
    o = o * lax.rsqrt(jnp.mean(o * o, axis=-1, keepdims=True) + EPS)
    o = o * gn
    return o * _silu(hg), st


def _f_hgrn(ps, xs, cs):
    lbraw, gn = ps
    hq, hf, hi, hg = xs
    (st,) = cs
    outs, sts = [], []
    for p in range(HG_PER_STEP):
        sl = slice(p * HG_K, (p + 1) * HG_K)
        o, s = _hgrn_head(lbraw[:, sl], gn[:, sl], hq[:, sl], hf[:, sl], hi[:, sl], hg[:, sl], st[sl])
        outs.append(o)
        sts.append(s)
    return [jnp.concatenate(outs, axis=1)], [jnp.concatenate(sts, axis=0)]


_RW_OFFS = (0, 1024, 2048, 3072, 3200, 3328)


def _f_rwpre(ps, xs, cs):
    mu, w0, w2p, a0, a2p, g2, k_k, k_a = ps
    (prev,) = cs
    t = xs[0].shape[0]
    zs = []
    for i, z in enumerate(xs):
        lo, hi = _RW_OFFS[i], _RW_OFFS[i + 1]
        zs.append(z + mu[:, lo:hi] * (_shift_down(z, prev[:, lo:hi]) - z))
    rr, kr, vr, wa, gz = zs
    w_log = -_softplus(-(w0 + _dot(jnp.tanh(wa), w2p, "nn"))) - 0.5
    lw = -jnp.exp(w_log)
    a = _sigmoid(a0 + _dot(wa, a2p, "nn"))
    g = _dot(_sigmoid(gz), g2, "nn")
    e, pad = _headmat()
    kkr = kr * k_k
    nrm = jnp.sqrt(_edot(kkr * kkr, e, "nn", 2) + pad)
    kk = kkr / _edot(jnp.maximum(nrm, 1e-12), e, "nt", 2)
    k2 = kr * (1.0 + (a - 1.0) * k_a)
    newprev = jnp.concatenate([_row(z, t - 1) for z in xs], axis=1)
    return [rr, lw, k2, vr, -kk, kk * a, g], [newprev]


def _rwkv_pair(r, lw, k, v, av, bv, sv):
    c = RW_CHUNK
    n = 2 * c
    tri = jnp.where(_tril(c), 1.0, 0.0).astype(F32)
    cl = _tdot(tri, lw, 3)
    cl_last = _row(cl, c - 1)
    lane = lax.broadcasted_iota(jnp.int32, (c, LANES), 1)
    h0 = lane < RW_N

    def stack(x):
        return jnp.concatenate([jnp.where(h0, x, 0.0), jnp.where(h0, 0.0, x)], axis=0)

    am = stack(av * jnp.exp(cl - lw))
    bm = stack(bv * jnp.exp(-cl))
    km = stack(k * jnp.exp(-cl))
    rm = stack(r * jnp.exp(cl))
    vm = stack(v)
    rn = lax.broadcasted_iota(jnp.int32, (n, n), 0)
    cn = lax.broadcasted_iota(jnp.int32, (n, n), 1)
    blk = (rn >= c) == (cn >= c)
    strict = blk & (cn < rn)
    incl = blk & (cn <= rn)
    lab = jnp.where(strict, _dot(am, bm, "nt"), 0.0)
    lak = jnp.where(strict, _dot(am, km, "nt"), 0.0)
    wrb = jnp.where(incl, _dot(rm, bm, "nt"), 0.0)
    wrk = jnp.where(incl, _dot(rm, km, "nt"), 0.0)
    tinv = jnp.where(rn == cn, 1.0, 0.0).astype(F32) + lab
    p = lab
    for _ in range(5):
        p = _dot(p, p, "nn")
        tinv = tinv + _dot(tinv, p, "nn")
    um = _dot(tinv, _dot(am, sv, "nt") + _dot(lak, vm, "nn"), "nn")
    ym = _dot(rm, sv, "nt") + _dot(wrb, um, "nn") + _dot(wrk, vm, "nn")
    sn = (sv + _dot(um, bm, "tn") + _dot(vm, km, "tn")) * jnp.exp(cl_last)
    return ym[:c] + ym[c:], sn


def _f_rwscan(ps, xs, cs):
    (sv,) = cs
    ys, svs = [], []
    for p in range(RW_PAIRS_PER_STEP):
        sl = slice(p * LANES, (p + 1) * LANES)
        y, s = _rwkv_pair(*[x[:, sl] for x in xs], sv[sl])
        ys.append(y)
        svs.append(s)
    return [jnp.concatenate(ys, axis=1)], [jnp.concatenate(svs, axis=0)]


def _f_rwpost(ps, xs, cs):
    ln_w, ln_b, r_k = ps
    y, r, k, v, g = xs
    e, _ = _headmat()
    inv_n = 1.0 / RW_N
    mu = _edot(y, e, "nn", 2) * inv_n
    yc = y - _edot(mu, e, "nt", 2)
    var = _edot(yc * yc, e, "nn", 2) * inv_n
    yn = yc * _edot(lax.rsqrt(var + GN_EPS), e, "nt", 2)
    yn = yn * ln_w + ln_b
    bonus = _edot(_edot(r * k * r_k, e, "nn", 2), e, "nt", 2) * v
    return [(yn + bonus) * g], []


def _f_merge(ps, xs, cs):
    ga, gb, ya, yb = xs
    return [_sigmoid(ga) * ya + _sigmoid(gb) * yb], []


def _f_post1(ps, xs, cs):
    x, mix = xs
    h1 = x + _rms(mix, ps[0])
    return [h1, _rms(h1, ps[1])], []


def _f_conv(ps, xs, cs):
    cw, cb = ps
    p1, p2 = cs
    w0, w1, w2 = _row(cw, 0), _row(cw, 1), _row(cw, 2)
    t = xs[0].shape[0]
    hc = []
    for i, x in enumerate(xs):
        sl = slice(i * DFF, (i + 1) * DFF)
        s1 = _shift_down(x, p1[:, sl])
        s2 = _shift_down(s1, p2[:, sl])
        hc.append(cb[:, sl] + w0[:, sl] * s2 + w1[:, sl] * s1 + w2[:, sl] * x)
    n1 = jnp.concatenate([_row(x, t - 1) for x in xs], axis=1)
    n2 = jnp.concatenate([_row(x, t - 2) for x in xs], axis=1)
    return [_silu(hc[0]) * hc[1]], [n1, n2]


class _Stage:
    def __init__(self, name, f, g, tm, par_per_g, in_pieces, in_offs, carry_shapes, out_pieces, out_dtypes):
        self.name, self.f, self.g, self.tm = name, f, g, tm
        self.par_per_g, self.in_pieces, self.in_offs = par_per_g, in_pieces, in_offs
        self.carry_shapes, self.out_pieces, self.out_dtypes = carry_shapes, out_pieces, out_dtypes


def _par_spec(arr, per_g, g):
    r, c = arr.shape
    if per_g:
        return pl.BlockSpec((r, c // g), lambda gi, ni: (0, gi))
    return pl.BlockSpec((r, c), lambda gi, ni: (0, 0))


def _row_spec(tm, width, off, n, rev):
    if rev:
        return pl.BlockSpec((tm, width), lambda gi, ni: (n - 1 - ni, off + gi))
    return pl.BlockSpec((tm, width), lambda gi, ni: (ni, off + gi))


def _carry_spec(shape, n, rev):
    if rev:
        return pl.BlockSpec((None, None) + shape, lambda gi, ni: (gi, n - 1 - ni, 0, 0))
    return pl.BlockSpec((None, None) + shape, lambda gi, ni: (gi, ni, 0, 0))


def _load_pieces(refs, pieces_list):
    out = []
    for ref, pieces in zip(refs, pieces_list):
        o = 0
        for w in pieces:
            out.append(ref[:, o:o + w].astype(F32))
            o += w
    return out


def _store_pieces(refs, pieces_list, vals):
    k = 0
    for ref, pieces in zip(refs, pieces_list):
        o = 0
        for w in pieces:
            ref[:, o:o + w] = vals[k].astype(ref.dtype)
            k += 1
            o += w


_ANY = pl.BlockSpec(memory_space=pl.ANY)


class _Exchange:
    def __init__(self, kind, arrs):
        self.kind, self.arrs, self.results = kind, list(arrs), None
        lead = N_DEV if kind == "gather" else N_DEV - 1
        self.out_shape = [jax.ShapeDtypeStruct((lead,) + (a.shape if kind == "gather" else a.shape[1:]), a.dtype)
                          for a in self.arrs]
        self.nsem = (N_DEV - 1) * len(self.arrs)

    def copies(self, in_refs, out_refs, ssem, rsem):
        x, y, c = lax.axis_index("x"), lax.axis_index("y"), lax.axis_index("c")
        me = 4 * x + 2 * y + c
        cps = []
        for a, (i_ref, o_ref) in enumerate(zip(in_refs, out_refs)):
            for j in range(1, N_DEV):
                px = 1 - x if j & 4 else x
                py = 1 - y if j & 2 else y
                pc = 1 - c if j & 1 else c
                if self.kind == "gather":
                    src, dst = i_ref, o_ref.at[me]
                else:
                    src, dst = i_ref.at[4 * px + 2 * py + pc], o_ref.at[j - 1]
                s = (N_DEV - 1) * a + j - 1
                cps.append(pltpu.make_async_remote_copy(src_ref=src, dst_ref=dst, send_sem=ssem.at[s],
                                                        recv_sem=rsem.at[s], device_id=(px, py, pc),
                                                        device_id_type=MESH))
        return cps


def _run_exchange(hook, first, last, in_refs, out_refs, ssem, rsem):
    @pl.when(first)
    def _():
        for cp in hook.copies(in_refs, out_refs, ssem, rsem):
            cp.start()

    @pl.when(last)
    def _():
        for cp in hook.copies(in_refs, out_refs, ssem, rsem):
            cp.wait()


def _hook_specs(hook):
    if hook is None:
        return [], [], [], []
    na = len(hook.arrs)
    sems = [pltpu.SemaphoreType.DMA((hook.nsem,)), pltpu.SemaphoreType.DMA((hook.nsem,))]
    return [_ANY] * na, [_ANY] * na, hook.out_shape, sems


def _stage_fwd(st, t, params, inputs, hook=None):
    g, tm = st.g, min(st.tm, t)
    n = t // tm
    npar, nin, ncar, nout = len(params), len(inputs), len(st.carry_shapes), len(st.out_pieces)
    h_in, h_out, h_shape, h_sems = _hook_specs(hook)
    nh = len(h_in)

    def body(*refs):
        p_refs = refs[:npar]
        x_refs = refs[npar:npar + nin]
        hi_refs = refs[npar + nin:npar + nin + nh]
        o = npar + nin + nh
        o_refs = refs[o:o + nout]
        s_refs = refs[o + nout:o + nout + ncar]
        ho_refs = refs[o + nout + ncar:o + nout + ncar + nh]
        c_scr = refs[o + nout + ncar + nh:o + nout + ncar + nh + ncar]
        gi, ni = pl.program_id(0), pl.program_id(1)
        if hook is not None:
            _run_exchange(hook, (gi == 0) & (ni == 0), (gi == g - 1) & (ni == n - 1), hi_refs, ho_refs, *refs[-2:])

        @pl.when(ni == 0)
        def _():
            for c in c_scr:
                c[...] = jnp.zeros(c.shape, F32)

        ps = [r[...].astype(F32) for r in p_refs]
        xs = _load_pieces(x_refs, st.in_pieces)
        cs = [c[...] for c in c_scr]
        for s, c in zip(s_refs, cs):
            s[...] = c
        outs, ncs = st.f(ps, xs, cs)
        _store_pieces(o_refs, st.out_pieces, outs)
        for c, v in zip(c_scr, ncs):
            c[...] = v

    in_specs = [_par_spec(p, pg, g) for p, pg in zip(params, st.par_per_g)]
    in_specs += [_row_spec(tm, sum(pc), off, n, False) for pc, off in zip(st.in_pieces, st.in_offs)]
    out_specs = [_row_spec(tm, sum(pc), 0, n, False) for pc in st.out_pieces]
    out_specs += [_carry_spec(s, n, False) for s in st.carry_shapes]
    out_shape = [jax.ShapeDtypeStruct((t, g * sum(pc)), dt) for pc, dt in zip(st.out_pieces, st.out_dtypes)]
    out_shape += [jax.ShapeDtypeStruct((g, n) + s, F32) for s in st.carry_shapes]
    res = pl.pallas_call(
        body, name=st.name + "_fwd", grid=(g, n), in_specs=in_specs + h_in, out_specs=out_specs + h_out,
        out_shape=out_shape + h_shape,
        scratch_shapes=[pltpu.VMEM(s, F32) for s in st.carry_shapes] + h_sems,
        compiler_params=_cparams(("arbitrary", "arbitrary")),
    )(*params, *inputs, *(hook.arrs if hook else []))
    if hook is not None:
        hook.results = list(res[nout + ncar:])
    return list(res[:nout]), list(res[nout:nout + ncar])


def _stage_bwd(st, t, params, inputs, saved, douts, dx_dtypes, hook=None):
    g, tm = st.g, min(st.tm, t)
    n = t // tm
    npar, nin, ncar = len(params), len(inputs), len(st.carry_shapes)
    flat_d = [d for ds in douts for d in ds]
    nd = len(flat_d)
    dx_idx = [i for i, dt in enumerate(dx_dtypes) if dt is not None]
    h_in, h_out, h_shape, h_sems = _hook_specs(hook)
    nh = len(h_in)

    def body(*refs):
        p_refs = refs[:npar]
        x_refs = refs[npar:npar + nin]
        s_refs = refs[npar + nin:npar + nin + ncar]
        d_refs = refs[npar + nin + ncar:npar + nin + ncar + nd]
        hi_refs = refs[npar + nin + ncar + nd:npar + nin + ncar + nd + nh]
        o = npar + nin + ncar + nd + nh
        dp_refs = refs[o:o + npar]
        dx_refs = refs[o + npar:o + npar + len(dx_idx)]
        ho_refs = refs[o + npar + len(dx_idx):o + npar + len(dx_idx) + nh]
        dc_scr = refs[o + npar + len(dx_idx) + nh:o + npar + len(dx_idx) + nh + ncar]
        gi, ni = pl.program_id(0), pl.program_id(1)
        if hook is not None:
            _run_exchange(hook, (gi == 0) & (ni == 0), (gi == g - 1) & (ni == n - 1), hi_refs, ho_refs, *refs[-2:])

        @pl.when(ni == 0)
        def _():
            for c in dc_scr:
                c[...] = jnp.zeros(c.shape, F32)

        ps = [r[...].astype(F32) for r in p_refs]
        xs = _load_pieces(x_refs, st.in_pieces)
        cs = [s[...] for s in s_refs]
        dys = []
        k = 0
        for ds, pieces in zip(douts, st.out_pieces):
            acc = _load_pieces([d_refs[k]], [pieces])
            for j in range(1, len(ds)):
                more = _load_pieces([d_refs[k + j]], [pieces])
                acc = [a + b for a, b in zip(acc, more)]
            dys += acc
            k += len(ds)
        _, vjp = jax.vjp(st.f, ps, xs, cs)
        dps, dxs, dcs = vjp((dys, [c[...] for c in dc_scr]))
        k = 0
        per_in = []
        for pieces in st.in_pieces:
            per_in.append(dxs[k:k + len(pieces)])
            k += len(pieces)
        for ref, i in zip(dx_refs, dx_idx):
            _store_pieces([ref], [st.in_pieces[i]], per_in[i])
        for c, v in zip(dc_scr, dcs):
            c[...] = v
        for ref, dp, pg in zip(dp_refs, dps, st.par_per_g):
            first = (ni == 0) if pg else ((ni == 0) & (gi == 0))

            @pl.when(first)
            def _():
                ref[...] = jnp.zeros(ref.shape, F32)

            ref[...] += dp

    in_specs = [_par_spec(p, pg, g) for p, pg in zip(params, st.par_per_g)]
    in_specs += [_row_spec(tm, sum(pc), off, n, True) for pc, off in zip(st.in_pieces, st.in_offs)]
    in_specs += [_carry_spec(s, n, True) for s in st.carry_shapes]
    for ds, pc in zip(douts, st.out_pieces):
        in_specs += [_row_spec(tm, sum(pc), 0, n, True) for _ in ds]
    out_specs = [_par_spec(p, pg, g) for p, pg in zip(params, st.par_per_g)]
    out_specs += [_row_spec(tm, sum(st.in_pieces[i]), 0, n, True) for i in dx_idx]
    out_shape = [jax.ShapeDtypeStruct(p.shape, F32) for p in params]
    out_shape += [jax.ShapeDtypeStruct((t, g * sum(st.in_pieces[i])), dx_dtypes[i]) for i in dx_idx]
    res = pl.pallas_call(
        body, name=st.name + "_bwd", grid=(g, n), in_specs=in_specs + h_in, out_specs=out_specs + h_out,
        out_shape=out_shape + h_shape,
        scratch_shapes=[pltpu.VMEM(s, F32) for s in st.carry_shapes] + h_sems,
        compiler_params=_cparams(("arbitrary", "arbitrary")),
    )(*params, *inputs, *saved, *flat_d, *(hook.arrs if hook else []))
    if hook is not None:
        hook.results = list(res[npar + len(dx_idx):])
    return list(res[:npar]), list(res[npar:npar + len(dx_idx)])


def _pick(n, cap):
    if n <= cap:
        return n
    best = LANES
    for k in range(1, n // LANES + 1):
        if (n // LANES) % k == 0 and k * LANES <= cap:
            best = k * LANES
    return best


def _mm(name, a, b, mode, out_dtype=F32, tm=512, tn=512, b_outer=False):
    m = a.shape[1] if mode == "tn" else a.shape[0]
    k = a.shape[0] if mode == "tn" else a.shape[1]
    n = b.shape[0] if mode == "nt" else b.shape[1]
    tm, tn = _pick(m, tm), _pick(n, tn)

    def body(a_ref, b_ref, o_ref):
        o_ref[...] = _raw_dot(a_ref[...], b_ref[...], mode).astype(o_ref.dtype)

    if b_outer:
        grid = (n // tn, m // tm)
        ij = lambda p, q: (q, p)
    else:
        grid = (m // tm, n // tn)
        ij = lambda p, q: (p, q)
    if mode == "tn":
        a_spec = pl.BlockSpec((k, tm), lambda p, q: (0, ij(p, q)[0]))
    else:
        a_spec = pl.BlockSpec((tm, k), lambda p, q: (ij(p, q)[0], 0))
    if mode == "nt":
        b_spec = pl.BlockSpec((tn, k), lambda p, q: (ij(p, q)[1], 0))
    else:
        b_spec = pl.BlockSpec((k, tn), lambda p, q: (0, ij(p, q)[1]))
    return pl.pallas_call(
        body, name=name, grid=grid, in_specs=[a_spec, b_spec],
        out_specs=pl.BlockSpec((tm, tn), lambda p, q: ij(p, q)),
        out_shape=jax.ShapeDtypeStruct((m, n), out_dtype),
        compiler_params=_cparams(("arbitrary", "arbitrary")),
    )(a, b)


def _loss_stage(t, g_post, h1, ff, tgt):
    tm = min(256, t)
    n = t // tm

    def body(g_ref, h_ref, f_ref, t_ref, loss_ref, dg_ref, dh_ref, df_ref):
        ni = pl.program_id(0)
        target = t_ref[...]

        def lossf(g, h1, ff):
            e = h1 + _rms(ff, g) - target
            return 0.5 * jnp.sum(jnp.mean(e * e, axis=-1))

        l, (dg, dh, df) = jax.value_and_grad(lossf, argnums=(0, 1, 2))(g_ref[...], h_ref[...], f_ref[...])

        @pl.when(ni == 0)
        def _():
            loss_ref[...] = jnp.zeros(loss_ref.shape, F32)
            dg_ref[...] = jnp.zeros(dg_ref.shape, F32)

        loss_ref[...] += jnp.full(loss_ref.shape, l, F32)
        dg_ref[...] += dg
        dh_ref[...] = dh
        df_ref[...] = df.astype(df_ref.dtype)

    row = pl.BlockSpec((tm, D), lambda ni: (ni, 0))
    one = pl.BlockSpec((1, D), lambda ni: (0, 0))
    return pl.pallas_call(
        body, name="loss_head", grid=(n,), in_specs=[one, row, row, row],
        out_specs=[pl.BlockSpec((1, LANES), lambda ni: (0, 0)), one, row, row],
        out_shape=[jax.ShapeDtypeStruct((1, LANES), F32), jax.ShapeDtypeStruct((1, D), F32),
                   jax.ShapeDtypeStruct((t, D), F32), jax.ShapeDtypeStruct((t, D), BF)],
        compiler_params=_cparams(("arbitrary",)),
    )(g_post, h1, ff, tgt)


_ANY = pl.BlockSpec(memory_space=pl.ANY)


def _all_gather(name, blks):
    na = len(blks)

    def body(*refs):
        x_refs, out_refs = refs[:na], refs[na:2 * na]
        send_sems, recv_sems, local_sems = refs[2 * na:]
        x, y, cc = lax.axis_index("x"), lax.axis_index("y"), lax.axis_index("c")
        me, sibling = (x, y, cc), (x, y, 1 - cc)
        chips = [(1 - x, y), (x, 1 - y), (1 - x, 1 - y)]

        def copy(a, k, block, to, src=None):
            dst = out_refs[a].at[4 * block[0] + 2 * block[1] + block[2]]
            return pltpu.make_async_remote_copy(
                src_ref=dst if src is None else src, dst_ref=dst, send_sem=send_sems.at[7 * a + k],
                recv_sem=recv_sems.at[7 * a + k], device_id=to, device_id_type=MESH)

        mine, first, passed = [], [], []
        for a in range(na):
            m = pltpu.make_async_copy(x_refs[a], out_refs[a].at[4 * x + 2 * y + cc], local_sems.at[a])
            m.start()
            mine.append(m)
            cps = [copy(a, 0, me, sibling, src=x_refs[a])]
            cps += [copy(a, 1 + j, me, (*chip, cc), src=x_refs[a]) for j, chip in enumerate(chips)]
            for cp in cps:
                cp.start()
            first += cps
        for j, chip in enumerate(chips):
            for a in range(na):
                copy(a, 1 + j, (*chip, cc), me).wait_recv()
                fw = copy(a, 4 + j, (*chip, cc), sibling)
                fw.start()
                passed.append(fw)
        for a in range(na):
            copy(a, 0, sibling, me).wait_recv()
            for j, chip in enumerate(chips):
                copy(a, 4 + j, (*chip, 1 - cc), me).wait_recv()
        for cp in first + passed:
            cp.wait_send()
        for m in mine:
            m.wait()

    res = pl.pallas_call(
        body, name=name, in_specs=[_ANY] * na, out_specs=[_ANY] * na,
        out_shape=[jax.ShapeDtypeStruct((N_DEV,) + b.shape, b.dtype) for b in blks],
        scratch_shapes=[pltpu.SemaphoreType.DMA((7 * na,)), pltpu.SemaphoreType.DMA((7 * na,)),
                        pltpu.SemaphoreType.DMA((na,))],
    )(*blks)
    return list(res)


def _reduce_pair(g8s):
    na = len(g8s)

    def body(*refs):
        g_refs, recv_refs = refs[:na], refs[na:2 * na]
        ssem, rsem = refs[2 * na:]
        x, y, cc = lax.axis_index("x"), lax.axis_index("y"), lax.axis_index("c")
        chips = [(x, y), (1 - x, y), (x, 1 - y), (1 - x, 1 - y)]
        sib = (x, y, 1 - cc)
        for a in range(na):
            for k, (cx, cy) in enumerate(chips):
                pltpu.make_async_remote_copy(
                    src_ref=g_refs[a].at[4 * cx + 2 * cy + 1 - cc], dst_ref=recv_refs[a].at[k],
                    send_sem=ssem.at[a], recv_sem=rsem.at[a], device_id=sib, device_id_type=MESH).start()
        for a in range(na):
            pltpu.make_async_remote_copy(src_ref=recv_refs[a], dst_ref=recv_refs[a], send_sem=ssem.at[a],
                                         recv_sem=rsem.at[a], device_id=sib, device_id_type=MESH).wait()

    res = pl.pallas_call(
        body, name="reduce_pair", in_specs=[_ANY] * na, out_specs=[_ANY] * na,
        out_shape=[jax.ShapeDtypeStruct((4,) + g.shape[1:], g.dtype) for g in g8s],
        scratch_shapes=[pltpu.SemaphoreType.DMA((na,)), pltpu.SemaphoreType.DMA((na,))],
    )(*g8s)
    return list(res)


def _swap_chips(name, sends):
    na = len(sends)

    def body(*refs):
        s_refs, r_refs = refs[:na], refs[na:2 * na]
        ssems, rsems = refs[2 * na:]
        x, y, cc = lax.axis_index("x"), lax.axis_index("y"), lax.axis_index("c")
        targets = [(1 - x, y, cc), (x, 1 - y, cc), (1 - x, 1 - y, cc)]
        cps = [pltpu.make_async_remote_copy(src_ref=s_refs[a].at[k], dst_ref=r_refs[a].at[k],
                                            send_sem=ssems.at[3 * a + k], recv_sem=rsems.at[3 * a + k],
                                            device_id=targets[k], device_id_type=MESH)
               for a in range(na) for k in range(3)]
        for cp in cps:
            cp.start()
        for cp in cps:
            cp.wait()

    res = pl.pallas_call(
        body, name=name, in_specs=[_ANY] * na, out_specs=[_ANY] * na,
        out_shape=[jax.ShapeDtypeStruct(s.shape, s.dtype) for s in sends],
        scratch_shapes=[pltpu.SemaphoreType.DMA((3 * na,)), pltpu.SemaphoreType.DMA((3 * na,))],
    )(*sends)
    return list(res)


def _pick_rows(r, c):
    if r * c * 4 <= TILE_BYTES or r % 16:
        return r
    best = 16
    for tr in range(16, r, 16):
        if r % tr == 0 and tr * c * 4 <= TILE_BYTES:
            best = tr
    return best


def _pair_sum(name, idx4, g8, recv4):
    _, r, c = g8.shape
    tr = _pick_rows(r, c)

    def body(idx_ref, a_ref, b_ref, o0_ref, o3_ref):
        k = pl.program_id(1)
        s = a_ref[...].astype(F32) + b_ref[...].astype(F32)

        @pl.when(k == 0)
        def _():
            o0_ref[...] = s

        @pl.when(k > 0)
        def _():
            o3_ref[...] = s.astype(BF)

    spec = pltpu.PrefetchScalarGridSpec(
        num_scalar_prefetch=1, grid=(r // tr, 4),
        in_specs=[pl.BlockSpec((None, tr, c), lambda i, k, idx: (idx[k], i, 0)),
                  pl.BlockSpec((None, tr, c), lambda i, k, idx: (k, i, 0))],
        out_specs=[pl.BlockSpec((tr, c), lambda i, k, idx: (i, 0)),
                   pl.BlockSpec((None, tr, c), lambda i, k, idx: (jnp.maximum(k - 1, 0), i, 0))])
    return pl.pallas_call(
        body, name=name, grid_spec=spec,
        out_shape=[jax.ShapeDtypeStruct((r, c), F32), jax.ShapeDtypeStruct((3, r, c), BF)],
        compiler_params=_cparams(("arbitrary", "arbitrary")),
    )(idx4, g8, recv4)


def _adamw(w, g, m, v):
    m = ADAM_B1 * m + (1.0 - ADAM_B1) * g
    v = ADAM_B2 * v + (1.0 - ADAM_B2) * jnp.square(g)
    m_hat = m / (1.0 - ADAM_B1 ** ADAM_STEP)
    v_hat = v / (1.0 - ADAM_B2 ** ADAM_STEP)
    delta = -ADAM_LR * (m_hat / (jnp.sqrt(v_hat) + ADAM_EPS) + ADAM_WD * w)
    return delta, m, v


def _adam_sharded(name, idx1, own, recv, w, m, v):
    r, c = w.shape
    tr = _pick_rows(r, c)
    nj = recv.shape[0]

    def body(idx_ref, p_ref, r_ref, w_ref, m_ref, v_ref, g_out, d_out, m_out, v_out):
        g = p_ref[...].astype(F32)
        for k in range(nj):
            g = g + r_ref[k].astype(F32)
        d, mn, vn = _adamw(w_ref[...], g, m_ref[...], v_ref[...])
        g_out[...] = g
        d_out[...] = d
        m_out[...] = mn
        v_out[...] = vn

    row = pl.BlockSpec((tr, c), lambda i, idx: (i, 0))
    spec = pltpu.PrefetchScalarGridSpec(
        num_scalar_prefetch=1, grid=(r // tr,),
        in_specs=[pl.BlockSpec((None, tr, c), lambda i, idx: (idx[0], i, 0)),
                  pl.BlockSpec((nj, tr, c), lambda i, idx: (0, i, 0)), row, row, row],
        out_specs=[row] * 4)
    return pl.pallas_call(
        body, name=name, grid_spec=spec, out_shape=[jax.ShapeDtypeStruct((r, c), F32)] * 4,
        compiler_params=_cparams(("arbitrary",)),
    )(idx1, own, recv, w, m, v)


def _repl_rows():
    rows, r = {}, 0
    for name, cols in REPL:
        rows[name] = r
        r += REPL_ROWS.get(name, 1) * ((cols + D - 1) // D)
    return rows


def _pack_replicated(grads):
    rows = _repl_rows()
    names = [n for n, _ in REPL]

    def body(*refs):
        o_ref = refs[-1]
        o_ref[...] = jnp.zeros(o_ref.shape, F32)
        for name, ref in zip(names, refs[:-1]):
            r0 = rows[name]
            nr, nc = ref.shape
            if nc <= D:
                o_ref[r0:r0 + nr, 0:nc] = ref[...]
            else:
                for j in range((nc + D - 1) // D):
                    lo, hi = j * D, min(nc, (j + 1) * D)
                    o_ref[r0 + j:r0 + j + 1, 0:hi - lo] = ref[:, lo:hi]

    return pl.pallas_call(body, name="pack_replicated", out_shape=jax.ShapeDtypeStruct((REPL_TOTAL, D), F32),
                          compiler_params=_cparams())(*[grads[n] for n in names])


def _adam_replicated(g8, ws, ms, vs):
    rows = _repl_rows()
    names = [n for n, _ in REPL]
    np_ = len(names)

    def body(*refs):
        g_ref = refs[0]
        w_refs, m_refs, v_refs = refs[1:1 + np_], refs[1 + np_:1 + 2 * np_], refs[1 + 2 * np_:1 + 3 * np_]
        outs = refs[1 + 3 * np_:1 + 7 * np_]
        scr = refs[-1]
        g = g_ref[0]
        for k in range(1, N_DEV):
            g = g + g_ref[k]
        scr[...] = g
        for i, name in enumerate(names):
            r0 = rows[name]
            nr, nc = w_refs[i].shape
            if nc <= D:
                gi = scr[r0:r0 + nr, 0:nc]
            else:
                parts = []
                for j in range((nc + D - 1) // D):
                    lo, hi = j * D, min(nc, (j + 1) * D)
                    parts.append(scr[r0 + j:r0 + j + 1, 0:hi - lo])
                gi = jnp.concatenate(parts, axis=1)
            d, mn, vn = _adamw(w_refs[i][...], gi, m_refs[i][...], v_refs[i][...])
            outs[i][...] = gi
            outs[np_ + i][...] = d
            outs[2 * np_ + i][...] = mn
            outs[3 * np_ + i][...] = vn

    shp = [jax.ShapeDtypeStruct(w.shape, F32) for w in ws]
    res = pl.pallas_call(body, name="adam_replicated", out_shape=shp * 4,
                         scratch_shapes=[pltpu.VMEM((REPL_TOTAL, D), F32)], compiler_params=_cparams(),
                         )(g8, *ws, *ms, *vs)
    return [dict(zip(names, res[k * np_:(k + 1) * np_])) for k in range(4)]


_WEIGHTS = ("attn_pre_norm", "w_in", "hgrn_lb", "hgrn_gnorm", "w_branch_a", "rwkv_mu", "rwkv_w0", "rwkv_w2",
            "rwkv_a0", "rwkv_a2", "rwkv_g2", "rwkv_k_k", "rwkv_k_a", "rwkv_r_k", "rwkv_ln_w", "rwkv_ln_b",
            "w_branch_b", "w_out", "attn_post_norm", "ffn_pre_norm", "w_up", "conv_w", "conv_b", "w_down",
            "ffn_post_norm")
_BIG = ("w_in", "w_up", "w_down", "w_branch_a", "w_branch_b", "w_out")


def _stages():
    one = [D]
    hw = HG_K * HG_PER_STEP
    rw = LANES * RW_PAIRS_PER_STEP
    return dict(
        pre1=_Stage("pre1", _f_pre1, 1, 256, [False], [one], [0], [], [one], [BF]),
        hgrn=_Stage("hgrn", _f_hgrn, HG_HEADS // HG_PER_STEP, 128, [True, True], [[hw]] * 4,
                    [i * D // hw for i in range(4)], [(hw, HG_K)], [[hw]], [BF]),
        rwpre=_Stage("rwkv_pre", _f_rwpre, 1, 128, [False] * 8, [[D], [D], [D], [LANES], [LANES]], [4, 5, 6, 56, 57],
                     [(1, RW_COLS)], [one] * 7, [F32] * 7),
        rwscan=_Stage("rwkv_scan", _f_rwscan, RW_HEADS // 2 // RW_PAIRS_PER_STEP, RW_CHUNK, [], [[rw]] * 6, [0] * 6,
                      [(rw, LANES)], [[rw]], [F32]),
        rwpost=_Stage("rwkv_post", _f_rwpost, 1, 128, [False] * 3, [one] * 5, [0] * 5, [], [one], [BF]),
        merge=_Stage("merge", _f_merge, 4, 512, [], [[256]] * 4, [29, 33, 0, 0], [], [[256]], [BF]),
        post1=_Stage("post1", _f_post1, 1, 256, [False, False], [one, one], [0, 0], [], [one, one], [F32, BF]),
        conv=_Stage("conv", _f_conv, 1, 128, [False, False], [[DFF], [DFF]], [0, 1], [(1, 2 * DFF), (1, 2 * DFF)],
                    [[DFF]], [BF]),
    )


def _cols_to_blocks(w, per):
    return w.reshape(w.shape[0], N_DEV, per).transpose(1, 0, 2)


def _blocks_to_cols(g):
    return g.transpose(1, 0, 2).reshape(g.shape[1], N_DEV * g.shape[2])


def kernel(x, attn_pre_norm, w_in, hgrn_lb, hgrn_gnorm, w_branch_a, rwkv_mu, rwkv_w0, rwkv_w2, rwkv_a0, rwkv_a2, rwkv_g2, rwkv_k_k, rwkv_k_a, rwkv_r_k, rwkv_ln_w, rwkv_ln_b, w_branch_b, w_out, attn_post_norm, ffn_pre_norm, w_up, conv_w, conv_b, w_down, ffn_post_norm, loss_target, m_attn_pre_norm, m_w_in, m_hgrn_lb, m_hgrn_gnorm, m_w_branch_a, m_rwkv_mu, m_rwkv_w0, m_rwkv_w2, m_rwkv_a0, m_rwkv_a2, m_rwkv_g2, m_rwkv_k_k, m_rwkv_k_a, m_rwkv_r_k, m_rwkv_ln_w, m_rwkv_ln_b, m_w_branch_b, m_w_out, m_attn_post_norm, m_ffn_pre_norm, m_w_up, m_conv_w, m_conv_b, m_w_down, m_ffn_post_norm, v_attn_pre_norm, v_w_in, v_hgrn_lb, v_hgrn_gnorm, v_w_branch_a, v_rwkv_mu, v_rwkv_w0, v_rwkv_w2, v_rwkv_a0, v_rwkv_a2, v_rwkv_g2, v_rwkv_k_k, v_rwkv_k_a, v_rwkv_r_k, v_rwkv_ln_w, v_rwkv_ln_b, v_w_branch_b, v_w_out, v_attn_post_norm, v_ffn_pre_norm, v_w_up, v_conv_w, v_conv_b, v_w_down, v_ffn_post_norm):
    w = dict(attn_pre_norm=attn_pre_norm, w_in=w_in, hgrn_lb=hgrn_lb, hgrn_gnorm=hgrn_gnorm, w_branch_a=w_branch_a, rwkv_mu=rwkv_mu, rwkv_w0=rwkv_w0, rwkv_w2=rwkv_w2, rwkv_a0=rwkv_a0, rwkv_a2=rwkv_a2, rwkv_g2=rwkv_g2, rwkv_k_k=rwkv_k_k, rwkv_k_a=rwkv_k_a, rwkv_r_k=rwkv_r_k, rwkv_ln_w=rwkv_ln_w, rwkv_ln_b=rwkv_ln_b, w_branch_b=w_branch_b, w_out=w_out, attn_post_norm=attn_post_norm, ffn_pre_norm=ffn_pre_norm, w_up=w_up, conv_w=conv_w, conv_b=conv_b, w_down=w_down, ffn_post_norm=ffn_post_norm)
    mo = dict(attn_pre_norm=m_attn_pre_norm, w_in=m_w_in, hgrn_lb=m_hgrn_lb, hgrn_gnorm=m_hgrn_gnorm, w_branch_a=m_w_branch_a, rwkv_mu=m_rwkv_mu, rwkv_w0=m_rwkv_w0, rwkv_w2=m_rwkv_w2, rwkv_a0=m_rwkv_a0, rwkv_a2=m_rwkv_a2, rwkv_g2=m_rwkv_g2, rwkv_k_k=m_rwkv_k_k, rwkv_k_a=m_rwkv_k_a, rwkv_r_k=m_rwkv_r_k, rwkv_ln_w=m_rwkv_ln_w, rwkv_ln_b=m_rwkv_ln_b, w_branch_b=m_w_branch_b, w_out=m_w_out, attn_post_norm=m_attn_post_norm, ffn_pre_norm=m_ffn_pre_norm, w_up=m_w_up, conv_w=m_conv_w, conv_b=m_conv_b, w_down=m_w_down, ffn_post_norm=m_ffn_post_norm)
    vo = dict(attn_pre_norm=v_attn_pre_norm, w_in=v_w_in, hgrn_lb=v_hgrn_lb, hgrn_gnorm=v_hgrn_gnorm, w_branch_a=v_w_branch_a, rwkv_mu=v_rwkv_mu, rwkv_w0=v_rwkv_w0, rwkv_w2=v_rwkv_w2, rwkv_a0=v_rwkv_a0, rwkv_a2=v_rwkv_a2, rwkv_g2=v_rwkv_g2, rwkv_k_k=v_rwkv_k_k, rwkv_k_a=v_rwkv_k_a, rwkv_r_k=v_rwkv_r_k, rwkv_ln_w=v_rwkv_ln_w, rwkv_ln_b=v_rwkv_ln_b, w_branch_b=v_w_branch_b, w_out=v_w_out, attn_post_norm=v_attn_post_norm, ffn_pre_norm=v_ffn_pre_norm, w_up=v_w_up, conv_w=v_conv_w, conv_b=v_conv_b, w_down=v_w_down, ffn_post_norm=v_ffn_post_norm)

    t = x.shape[1]
    x2 = x.reshape(t, D)
    tgt = loss_target.reshape(t, D)
    st = _stages()

    me = 4 * lax.axis_index("x") + 2 * lax.axis_index("y") + lax.axis_index("c")
    small = jnp.concatenate([rwkv_w2[0], rwkv_a2[0], rwkv_g2[0]], axis=0).astype(BF)
    g_in, g_small = _all_gather("gather_weights", [w_in[0].astype(BF), small])
    fw_in = _blocks_to_cols(g_in)
    z64 = jnp.zeros((64, D), BF)
    w2p = jnp.concatenate([_blocks_to_cols(g_small[:, 0:64]), z64], axis=0)
    a2p = jnp.concatenate([z64, _blocks_to_cols(g_small[:, 64:128])], axis=0)
    g2f = _blocks_to_cols(g_small[:, 128:256])
    conv_bits = lax.bitcast_convert_type(conv_w[0], BF).reshape(3, 2 * 704)
    late = [w[k][0].astype(BF) for k in _BIG[1:]] + [conv_bits]
    late_gather = _Exchange("gather", late)
    r_k = rwkv_r_k.reshape(1, D)

    (xn,), _ = _stage_fwd(st["pre1"], t, [attn_pre_norm], [x2])
    z = _mm("in_proj", xn, fw_in, "nn", F32, tm=256, tn=4736, b_outer=True)
    rwpre_par = [rwkv_mu, rwkv_w0, w2p, rwkv_a0, a2p, g2f, rwkv_k_k, rwkv_k_a]
    rw_in, rwpre_saved = _stage_fwd(st["rwpre"], t, rwpre_par, [z] * 5)
    r_, lw_, k_, v_, av_, bv_, g_ = rw_in
    (y_,), rws_saved = _stage_fwd(st["rwscan"], t, [], [r_, lw_, k_, v_, av_, bv_], hook=late_gather)
    gl = [lax.dynamic_update_slice(g, own[None], (me, 0, 0)) for g, own in zip(late_gather.results, late)]
    fw_up = _blocks_to_cols(gl[0])
    fw_down = gl[1].reshape(DFF, D)
    fw_a, fw_b, fw_out = (g.reshape(D, D) for g in gl[2:5])
    conv_full = _blocks_to_cols(lax.bitcast_convert_type(gl[5].reshape(N_DEV, 3, 704, 2), F32))
    hg_par = [hgrn_lb, hgrn_gnorm]
    (o_a,), hg_saved = _stage_fwd(st["hgrn"], t, hg_par, [z] * 4)
    rwpost_par = [rwkv_ln_w, rwkv_ln_b, r_k]
    (o_b,), _ = _stage_fwd(st["rwpost"], t, rwpost_par, [y_, r_, k_, v_, g_])
    y_a = _mm("branch_a", o_a, fw_a, "nn")
    y_b = _mm("branch_b", o_b, fw_b, "nn")
    (merged,), _ = _stage_fwd(st["merge"], t, [], [z, z, y_a, y_b])
    mix = _mm("out_proj", merged, fw_out, "nn")
    (h1, xn2), _ = _stage_fwd(st["post1"], t, [attn_post_norm, ffn_pre_norm], [x2, mix])
    hu = _mm("up_proj", xn2, fw_up, "nn", F32, tm=512, tn=1408)
    conv_par = [conv_full, conv_b]
    (act,), conv_saved = _stage_fwd(st["conv"], t, conv_par, [hu, hu])
    ff = _mm("down_proj", act, fw_down, "nn")

    loss_acc, d_ffn_post, dh1, dff = _loss_stage(t, ffn_post_norm, h1, ff, tgt)
    dact = _mm("d_act", dff, fw_down, "nt", F32, tm=512, tn=1408)
    dw_down = _mm("dw_down", act, dff, "tn", BF, tm=1408, tn=512)
    (dcw, dcb), (dhu_g, dhu_v) = _stage_bwd(st["conv"], t, conv_par, [hu, hu], conv_saved, [[dact]], [BF, BF])
    dhu = jnp.concatenate([dhu_g, dhu_v], axis=1)
    dxn2 = _mm("d_xn2", dhu, fw_up, "nt", F32, tm=256, tn=512)
    dw_up = _mm("dw_up", xn2, dhu, "tn", BF, tm=512, tn=1408)
    (d_post, d_pre2), (dx_a, dmix) = _stage_bwd(st["post1"], t, [attn_post_norm, ffn_pre_norm], [x2, mix], [],
                                                 [[dh1], [dxn2]], [F32, BF])
    dmerged = _mm("d_merged", dmix, fw_out, "nt")
    dw_out = _mm("dw_out", merged, dmix, "tn", BF)
    _, (dga, dgb, dy_a, dy_b) = _stage_bwd(st["merge"], t, [], [z, z, y_a, y_b], [], [[dmerged]], [BF, BF, BF, BF])
    do_a = _mm("d_oa", dy_a, fw_a, "nt")
    dw_a = _mm("dw_a", o_a, dy_a, "tn", BF)
    do_b = _mm("d_ob", dy_b, fw_b, "nt")
    dw_b = _mm("dw_b", o_b, dy_b, "tn", BF)
    (d_lnw, d_lnb, d_rk), (dy_, dr1, dk1, dv1, dg_) = _stage_bwd(
        st["rwpost"], t, rwpost_par, [y_, r_, k_, v_, g_], [], [[do_b]], [F32] * 5)
    early = [_cols_to_blocks(dw_up, 704), dw_down.reshape(N_DEV, 352, D), dw_a.reshape(N_DEV, 128, D),
             dw_b.reshape(N_DEV, 128, D), dw_out.reshape(N_DEV, 128, D), _cols_to_blocks(dcw.astype(BF), 704)]
    early_scatter = _Exchange("scatter", early)
    _, (dr2, dlw, dk2, dv2, dav, dbv) = _stage_bwd(
        st["rwscan"], t, [], [r_, lw_, k_, v_, av_, bv_], rws_saved, [[dy_]], [F32] * 6, hook=early_scatter)
    rwpre_dp, dz_r = _stage_bwd(
        st["rwpre"], t, rwpre_par, [z] * 5, rwpre_saved,
        [[dr1, dr2], [dlw], [dk1, dk2], [dv1, dv2], [dav], [dbv], [dg_]], [BF] * 5)
    d_mu, d_w0, d_w2p, d_a0, d_a2p, d_g2, d_kk, d_ka = rwpre_dp
    (d_lb, d_gn), dz_h = _stage_bwd(st["hgrn"], t, hg_par, [z] * 4, hg_saved, [[do_a]], [BF] * 4)
    dz = jnp.concatenate(dz_h + dz_r + [dga, dgb], axis=1)
    dxn = _mm("d_xn", dz, fw_in, "nt", F32, tm=256, tn=512)
    dw_in = _mm("dw_in", xn, dz, "tn", BF, tm=1024, tn=256)
    (d_pre1,), (dx_b,) = _stage_bwd(st["pre1"], t, [attn_pre_norm], [x2], [], [[dxn]], [F32])
    grad_x = (dx_a + dx_b).reshape(x.shape)
    loss = lax.psum(loss_acc[0, 0], ("x", "y", "c"))

    ax, ay, ac = lax.axis_index("x"), lax.axis_index("y"), lax.axis_index("c")
    idx4 = jnp.stack([4 * cx + 2 * cy + ac for cx, cy in ((ax, ay), (1 - ax, ay), (ax, 1 - ay), (1 - ax, 1 - ay))])
    idx4 = idx4.astype(jnp.int32)
    idx_me, idx_0 = idx4[0:1], jnp.zeros((1,), jnp.int32)
    d_small = jnp.concatenate([d_w2p[:64], d_a2p[64:], d_g2], axis=0).astype(BF)
    g8s = [_cols_to_blocks(dw_in, 1184), _cols_to_blocks(d_small, LANES)]
    recv4s = _reduce_pair(g8s)
    sums = [_pair_sum("pair_sum_" + n, idx4, g, r) for n, g, r in zip(("w_in", "small"), g8s, recv4s)]
    recv3s = _swap_chips("reduce_chips", [s[1] for s in sums])

    def small_of(src):
        return jnp.concatenate([src["rwkv_w2"][0], src["rwkv_a2"][0], src["rwkv_g2"][0]], axis=0)

    sh_out = [dict() for _ in range(4)]
    res = _adam_sharded("adam_w_in", idx_0, sums[0][0][None], recv3s[0], *[src["w_in"][0] for src in (w, mo, vo)])
    res_s = _adam_sharded("adam_small", idx_0, sums[1][0][None], recv3s[1], *[small_of(src) for src in (w, mo, vo)])
    for kind in range(4):
        sh_out[kind]["w_in"] = res[kind][None]
        sh_out[kind]["rwkv_w2"] = res_s[kind][0:64][None]
        sh_out[kind]["rwkv_a2"] = res_s[kind][64:128][None]
        sh_out[kind]["rwkv_g2"] = res_s[kind][128:256][None]
    for n, own, recv in zip(_BIG[1:] + ("conv_w",), early, early_scatter.results):
        res = _adam_sharded("adam_" + n, idx_me, own, recv, *[src[n][0] for src in (w, mo, vo)])
        for kind in range(4):
            sh_out[kind][n] = res[kind][None]

    rg = dict(attn_pre_norm=d_pre1, hgrn_lb=d_lb, hgrn_gnorm=d_gn, rwkv_mu=d_mu, rwkv_w0=d_w0, rwkv_a0=d_a0,
              rwkv_k_k=d_kk, rwkv_k_a=d_ka, rwkv_r_k=d_rk, rwkv_ln_w=d_lnw, rwkv_ln_b=d_lnb, attn_post_norm=d_post,
              ffn_pre_norm=d_pre2, conv_b=dcb, ffn_post_norm=d_ffn_post)
    (g8,) = _all_gather("gather_small_grads", [_pack_replicated(rg)])
    rnames = [n for n, _ in REPL]
    flat = lambda src: [src[n].reshape(1, D) if n == "rwkv_r_k" else src[n] for n in rnames]
    rp_out = _adam_replicated(g8, flat(w), flat(mo), flat(vo))
    for kind in range(4):
        rp_out[kind]["rwkv_r_k"] = rp_out[kind]["rwkv_r_k"].reshape(rwkv_r_k.shape)

    outs = [loss, grad_x]
    for kind in range(4):
        for name in _WEIGHTS:
            outs.append(sh_out[kind][name] if name in sh_out[kind] else rp_out[kind][name])
    return tuple(outs)
```

```python
import functools

import jax
import jax.numpy as jnp
from jax import lax
from jax.experimental import pallas as pl
from jax.experimental.pallas import tpu as pltpu

F32 = jnp.float32
BF = jnp.bfloat16
MESH = pl.DeviceIdType.MESH

D = 1024
HG_HEADS = 8
HG_K = 128
HG_CHUNK = 32
HG_SCALE = HG_K ** -0.5
HG_PER_STEP = 8
RW_HEADS = 16
RW_N = 64
RW_CHUNK = 64
RW_PAIRS_PER_STEP = 8
DFF = 2816
IN_COLS = 9472
RW_COLS = 3328
EPS = 1e-6
GN_EPS = 1e-5 * RW_N
ADAM_LR = 0.001
ADAM_B1 = 0.9
ADAM_B2 = 0.999
ADAM_EPS = 1e-08
ADAM_WD = 0.01
ADAM_STEP = 10
N_DEV = 8
LANES = 128
VMEM_LIMIT = 56 * 1024 * 1024
TILE_BYTES = 1280 * 1024

REPL = (("attn_pre_norm", 1024), ("hgrn_lb", 1024), ("hgrn_gnorm", 1024), ("rwkv_mu", 3328), ("rwkv_w0", 1024),
        ("rwkv_a0", 1024), ("rwkv_k_k", 1024), ("rwkv_k_a", 1024), ("rwkv_r_k", 1024), ("rwkv_ln_w", 1024),
        ("rwkv_ln_b", 1024), ("attn_post_norm", 1024), ("ffn_pre_norm", 1024), ("conv_b", 5632), ("ffn_post_norm", 1024))
REPL_ROWS = {"hgrn_lb": 2}
REPL_TOTAL = 32


def _cparams(sem=None, **kw):
    return pltpu.CompilerParams(dimension_semantics=sem, vmem_limit_bytes=VMEM_LIMIT, **kw)


_DN = {"nn": ((1,), (0,)), "nt": ((1,), (1,)), "tn": ((0,), (0,))}


def _raw_dot(a, b, mode):
    return lax.dot_general(a.astype(BF), b.astype(BF), (_DN[mode], ((), ())), preferred_element_type=F32)


@functools.partial(jax.custom_vjp, nondiff_argnums=(2,))
def _dot(a, b, mode):
    return _raw_dot(a, b, mode)


def _dot_fwd(a, b, mode):
    return _raw_dot(a, b, mode), (a, b)


def _dot_bwd(mode, res, g):
    a, b = res
    if mode == "nn":
        return _dot(g, b, "nt"), _dot(a, g, "tn")
    if mode == "nt":
        return _dot(g, b, "nn"), _dot(g, a, "tn")
    return _dot(b, g, "nt"), _dot(a, g, "nn")


_dot.defvjp(_dot_fwd, _dot_bwd)


def _bf_pieces(x, n):
    out, r = [], x
    for i in range(n):
        p = r.astype(BF)
        out.append(p)
        if i + 1 < n:
            r = r - p.astype(F32)
    return out


def _raw_split_dot(x, e, mode, n, x_left):
    eb = e.astype(BF)
    acc = None
    for p in _bf_pieces(x, n):
        ops = (p, eb) if x_left else (eb, p)
        t = lax.dot_general(*ops, (_DN[mode], ((), ())), preferred_element_type=F32)
        acc = t if acc is None else acc + t
    return acc


@functools.partial(jax.custom_vjp, nondiff_argnums=(2, 3))
def _edot(x, e, mode, n):
    return _raw_split_dot(x, e, mode, n, True)


def _edot_fwd(x, e, mode, n):
    return _raw_split_dot(x, e, mode, n, True), e


def _edot_bwd(mode, n, e, g):
    return _raw_split_dot(g, e, "nt" if mode == "nn" else "nn", n, True), jnp.zeros_like(e)


_edot.defvjp(_edot_fwd, _edot_bwd)


@functools.partial(jax.custom_vjp, nondiff_argnums=(2,))
def _tdot(tri, x, n):
    return _raw_split_dot(x, tri, "nn", n, False)


def _tdot_fwd(tri, x, n):
    return _raw_split_dot(x, tri, "nn", n, False), tri


def _tdot_bwd(n, tri, g):
    return jnp.zeros_like(tri), _raw_split_dot(g, tri, "tn", n, False)


_tdot.defvjp(_tdot_fwd, _tdot_bwd)


def _row(x, i):
    r = lax.broadcasted_iota(jnp.int32, x.shape, 0)
    return jnp.sum(jnp.where(r == i, x, 0.0), axis=0, keepdims=True)


def _shift_down(x, prev):
    t = x.shape[0]

    @jax.custom_vjp
    def sh(x, prev):
        r = lax.broadcasted_iota(jnp.int32, x.shape, 0)
        return jnp.where(r == 0, prev, pltpu.roll(x, 1, 0))

    def fwd(x, prev):
        return sh(x, prev), None

    def bwd(_, g):
        r = lax.broadcasted_iota(jnp.int32, g.shape, 0)
        dx = jnp.where(r == t - 1, 0.0, pltpu.roll(g, t - 1, 0))
        return dx, jnp.sum(jnp.where(r == 0, g, 0.0), axis=0, keepdims=True)

    sh.defvjp(fwd, bwd)
    return sh(x, prev)


def _sigmoid(x):
    return jax.nn.sigmoid(x)


def _silu(x):
    return x * jax.nn.sigmoid(x)


def _softplus(x):
    return jnp.maximum(x, 0.0) + jnp.log(1.0 + jnp.exp(-jnp.abs(x)))


def _rms(x, g):
    return (x * lax.rsqrt(jnp.mean(x * x, axis=-1, keepdims=True) + EPS)) * g


def _headmat():
    j = lax.broadcasted_iota(jnp.int32, (D, LANES), 0)
    h = lax.broadcasted_iota(jnp.int32, (D, LANES), 1)
    e = jnp.where(lax.shift_right_logical(j, 6) == h, 1.0, 0.0).astype(F32)
    pad = jnp.where(lax.broadcasted_iota(jnp.int32, (1, LANES), 1) >= RW_HEADS, 1.0, 0.0).astype(F32)
    return e, pad


def _tril(c):
    r = lax.broadcasted_iota(jnp.int32, (c, c), 0)
    cc = lax.broadcasted_iota(jnp.int32, (c, c), 1)
    return cc <= r


def _f_pre1(ps, xs, cs):
    return [_rms(xs[0], ps[0])], []


def _f_hgrn(ps, xs, cs):
    lbraw, gn = ps
    hq, hf, hi, hg = xs
    hd = range(HG_PER_STEP)
    st = [cs[0][p * HG_K:(p + 1) * HG_K] for p in hd]
    l0, l1 = _row(lbraw, 0), _row(lbraw, 1)
    m = jnp.maximum(l0, l1)
    e0, e1 = jnp.exp(l0 - m), jnp.exp(l1 - m)
    lb = e0 / (e0 + e1)
    q = _silu(hq) * HG_SCALE
    f = lb + (1.0 - lb) * _sigmoid(hf)
    kh = 1.0 - f
    gl = jnp.log(f)
    c = HG_CHUNK
    low = _tril(c)
    tri = jnp.where(low, 1.0, 0.0).astype(F32)
    outs = []
    for i in range(hq.shape[0] // c):
        rows = slice(i * c, (i + 1) * c)
        b = _tdot(tri, gl[rows], 3)
        bref = _row(b, c // 2 - 1)
        blast = _row(b, c - 1)
        qi = q[rows] * jnp.exp(b - bref)
        ki = kh[rows] * jnp.exp(bref - b)
        qd = q[rows] * jnp.exp(b)
        kd = kh[rows] * jnp.exp(blast - b)
        dec = jnp.exp(blast)
        sl = [slice(p * HG_K, (p + 1) * HG_K) for p in hd]
        sc = [jnp.where(low, _dot(qi[:, sl[p]], ki[:, sl[p]], "nt"), 0.0) for p in hd]
        o = [_dot(sc[p], hi[rows, sl[p]], "nn") + _dot(qd[:, sl[p]], st[p], "nt") for p in hd]
        u = [_dot(hi[rows, sl[p]], kd[:, sl[p]], "tn") for p in hd]
        st = [dec[:, sl[p]] * st[p] + u[p] for p in hd]
        outs.append(jnp.concatenate(o, axis=1) if len(o) > 1 else o[0])
    o = outs[0] if len(outs) == 1 else jnp.concatenate(outs, axis=0)
    on = []
    for p in hd:
        op = o[:, p * HG_K:(p + 1) * HG_K]
        on.append(op * lax.rsqrt(jnp.mean(op * op, axis=-1, keepdims=True) + EPS))
    o = jnp.concatenate(on, axis=1) if len(on) > 1 else on[0]
    o = o * gn
    return [o * _silu(hg)], [jnp.concatenate(st, axis=0) if len(st) > 1 else st[0]]


_RW_OFFS = (0, 1024, 2048, 3072, 3200, 3328)


def _f_rwpre(ps, xs, cs):
    mu, w0, w2p, a0, a2p, g2, k_k, k_a = ps
    (prev,) = cs
    t = xs[0].shape[0]
    zs = []
    for i, z in enumerate(xs):
        lo, hi = _RW_OFFS[i], _RW_OFFS[i + 1]
        zs.append(z + mu[:, lo:hi] * (_shift_down(z, prev[:, lo:hi]) - z))
    rr, kr, vr, wa, gz = zs
    w_log = -_softplus(-(w0 + _dot(jnp.tanh(wa), w2p, "nn"))) - 0.5
    lw = -jnp.exp(w_log)
    a = _sigmoid(a0 + _dot(wa, a2p, "nn"))
    g = _dot(_sigmoid(gz), g2, "nn")
    e, pad = _headmat()
    kkr = kr * k_k
    nrm = jnp.sqrt(_edot(kkr * kkr, e, "nn", 2) + pad)
    kk = kkr / _edot(jnp.maximum(nrm, 1e-12), e, "nt", 2)
    k2 = kr * (1.0 + (a - 1.0) * k_a)
    newprev = jnp.concatenate([_row(z, t - 1) for z in xs], axis=1)
    return [rr, lw, k2, vr, -kk, kk * a, g], [newprev]


def _f_rwscan(ps, xs, cs):
    npair = RW_PAIRS_PER_STEP
    pr = range(npair)
    r, lw, k, v, av, bv = [[x[:, p * LANES:(p + 1) * LANES] for p in pr] for x in xs]
    sv = [cs[0][p * LANES:(p + 1) * LANES] for p in pr]
    c = RW_CHUNK
    n = 2 * c
    tri = jnp.where(_tril(c), 1.0, 0.0).astype(F32)
    cl = [_tdot(tri, lw[p], 3) for p in pr]
    cl_last = [_row(cl[p], c - 1) for p in pr]
    lane = lax.broadcasted_iota(jnp.int32, (c, LANES), 1)
    h0 = lane < RW_N

    def stack(x):
        return jnp.concatenate([jnp.where(h0, x, 0.0), jnp.where(h0, 0.0, x)], axis=0)

    am = [stack(av[p] * jnp.exp(cl[p] - lw[p])) for p in pr]
    bm = [stack(bv[p] * jnp.exp(-cl[p])) for p in pr]
    km = [stack(k[p] * jnp.exp(-cl[p])) for p in pr]
    rm = [stack(r[p] * jnp.exp(cl[p])) for p in pr]
    vm = [stack(v[p]) for p in pr]
    rn = lax.broadcasted_iota(jnp.int32, (n, n), 0)
    cn = lax.broadcasted_iota(jnp.int32, (n, n), 1)
    blk = (rn >= c) == (cn >= c)
    strict = blk & (cn < rn)
    incl = blk & (cn <= rn)
    lab = [jnp.where(strict, _dot(am[p], bm[p], "nt"), 0.0) for p in pr]
    lak = [jnp.where(strict, _dot(am[p], km[p], "nt"), 0.0) for p in pr]
    wrb = [jnp.where(incl, _dot(rm[p], bm[p], "nt"), 0.0) for p in pr]
    wrk = [jnp.where(incl, _dot(rm[p], km[p], "nt"), 0.0) for p in pr]
    eye = jnp.where(rn == cn, 1.0, 0.0).astype(F32)
    tinv = [eye + lab[p] for p in pr]
    pw = lab
    for _ in range(5):
        pw = [_dot(pw[p], pw[p], "nn") for p in pr]
        tinv = [tinv[p] + _dot(tinv[p], pw[p], "nn") for p in pr]
    rhs = [_dot(am[p], sv[p], "nt") + _dot(lak[p], vm[p], "nn") for p in pr]
    um = [_dot(tinv[p], rhs[p], "nn") for p in pr]
    ym = [_dot(rm[p], sv[p], "nt") + _dot(wrb[p], um[p], "nn") + _dot(wrk[p], vm[p], "nn") for p in pr]
    sn = [(sv[p] + _dot(um[p], bm[p], "tn") + _dot(vm[p], km[p], "tn")) * jnp.exp(cl_last[p]) for p in pr]
    ys = [ym[p][:c] + ym[p][c:] for p in pr]
    return [jnp.concatenate(ys, axis=1)], [jnp.concatenate(sn, axis=0)]


def _f_mixers(ps, xs, cs):
    oa, st = _f_hgrn(ps, xs[:4], cs[:1])
    y, sv = _f_rwscan([], xs[4:], cs[1:])
    return oa + y, st + sv


def _f_rwpost(ps, xs, cs):
    ln_w, ln_b, r_k = ps
    y, r, k, v, g = xs
    e, _ = _headmat()
    inv_n = 1.0 / RW_N
    mu = _edot(y, e, "nn", 2) * inv_n
    yc = y - _edot(mu, e, "nt", 2)
    var = _edot(yc * yc, e, "nn", 2) * inv_n
    yn = yc * _edot(lax.rsqrt(var + GN_EPS), e, "nt", 2)
    yn = yn * ln_w + ln_b
    bonus = _edot(_edot(r * k * r_k, e, "nn", 2), e, "nt", 2) * v
    return [(yn + bonus) * g], []


def _f_merge(ps, xs, cs):
    ga, gb, ya, yb = xs
    return [_sigmoid(ga) * ya + _sigmoid(gb) * yb], []


def _f_post1(ps, xs, cs):
    x, mix = xs
    h1 = x + _rms(mix, ps[0])
    return [h1, _rms(h1, ps[1])], []


def _f_conv(ps, xs, cs):
    cw, cb = ps
    p1, p2 = cs
    w0, w1, w2 = _row(cw, 0), _row(cw, 1), _row(cw, 2)
    t = xs[0].shape[0]
    hc = []
    for i, x in enumerate(xs):
        sl = slice(i * DFF, (i + 1) * DFF)
        s1 = _shift_down(x, p1[:, sl])
        s2 = _shift_down(s1, p2[:, sl])
        hc.append(cb[:, sl] + w0[:, sl] * s2 + w1[:, sl] * s1 + w2[:, sl] * x)
    n1 = jnp.concatenate([_row(x, t - 1) for x in xs], axis=1)
    n2 = jnp.concatenate([_row(x, t - 2) for x in xs], axis=1)
    return [_silu(hc[0]) * hc[1]], [n1, n2]


class _Stage:
    def __init__(self, name, f, g, tm, par_per_g, in_pieces, in_offs, carry_shapes, out_pieces, out_dtypes):
        self.name, self.f, self.g, self.tm = name, f, g, tm
        self.par_per_g, self.in_pieces, self.in_offs = par_per_g, in_pieces, in_offs
        self.carry_shapes, self.out_pieces, self.out_dtypes = carry_shapes, out_pieces, out_dtypes


def _par_spec(arr, per_g, g):
    r, c = arr.shape
    if per_g:
        return pl.BlockSpec((r, c // g), lambda gi, ni: (0, gi))
    return pl.BlockSpec((r, c), lambda gi, ni: (0, 0))


def _row_spec(tm, width, off, n, rev):
    if rev:
        return pl.BlockSpec((tm, width), lambda gi, ni: (n - 1 - ni, off + gi))
    return pl.BlockSpec((tm, width), lambda gi, ni: (ni, off + gi))


def _carry_spec(shape, n, rev):
    if rev:
        return pl.BlockSpec((None, None) + shape, lambda gi, ni: (gi, n - 1 - ni, 0, 0))
    return pl.BlockSpec((None, None) + shape, lambda gi, ni: (gi, ni, 0, 0))


def _load_pieces(refs, pieces_list):
    out = []
    for ref, pieces in zip(refs, pieces_list):
        o = 0
        for w in pieces:
            out.append(ref[:, o:o + w].astype(F32))
            o += w
    return out


def _store_pieces(refs, pieces_list, vals):
    k = 0
    for ref, pieces in zip(refs, pieces_list):
        o = 0
        for w in pieces:
            ref[:, o:o + w] = vals[k].astype(ref.dtype)
            k += 1
            o += w


_ANY = pl.BlockSpec(memory_space=pl.ANY)


class _Exchange:
    def __init__(self, kind, arrs):
        self.kind, self.arrs, self.results = kind, list(arrs), None
        if kind == "scatter":
            self.out_shape = [jax.ShapeDtypeStruct((N_DEV - 1,) + a.shape[1:], a.dtype) for a in self.arrs]
        else:
            self.out_shape = [jax.ShapeDtypeStruct((N_DEV,) + a.shape, a.dtype) for a in self.arrs]
        self.nsem = (N_DEV - 1) * len(self.arrs)

    def copies(self, in_refs, out_refs, ssem, rsem):
        x, y, c = lax.axis_index("x"), lax.axis_index("y"), lax.axis_index("c")
        me = 4 * x + 2 * y + c
        cps = []
        for a, (i_ref, o_ref) in enumerate(zip(in_refs, out_refs)):
            for j in range(1, N_DEV):
                px = 1 - x if j & 4 else x
                py = 1 - y if j & 2 else y
                pc = 1 - c if j & 1 else c
                if self.kind == "gather":
                    src, dst = i_ref, o_ref.at[me]
                else:
                    src, dst = i_ref.at[4 * px + 2 * py + pc], o_ref.at[j - 1]
                s = (N_DEV - 1) * a + j - 1
                cps.append(pltpu.make_async_remote_copy(src_ref=src, dst_ref=dst, send_sem=ssem.at[s],
                                                        recv_sem=rsem.at[s], device_id=(px, py, pc),
                                                        device_id_type=MESH))
        return cps

    def run(self, first, mid, last, in_refs, out_refs, ssem, rsem):
        if self.kind == "gather2":
            return self.run_two_level(first, mid, last, in_refs, out_refs, ssem, rsem)

        @pl.when(first)
        def _():
            for cp in self.copies(in_refs, out_refs, ssem, rsem):
                cp.start()

        @pl.when(last)
        def _():
            for cp in self.copies(in_refs, out_refs, ssem, rsem):
                cp.wait()

    def run_two_level(self, first, mid, last, in_refs, out_refs, ssem, rsem):
        x, y, c = lax.axis_index("x"), lax.axis_index("y"), lax.axis_index("c")
        me, sibling = (x, y, c), (x, y, 1 - c)
        chips = [(1 - x, y), (x, 1 - y), (1 - x, 1 - y)]
        arrs = range(len(in_refs))

        def copy(a, k, block, to, src=None):
            dst = out_refs[a].at[4 * block[0] + 2 * block[1] + block[2]]
            return pltpu.make_async_remote_copy(
                src_ref=dst if src is None else src, dst_ref=dst, send_sem=ssem.at[7 * a + k],
                recv_sem=rsem.at[7 * a + k], device_id=to, device_id_type=MESH)

        def firsts(a):
            return [copy(a, 0, me, sibling, src=in_refs[a])] + [
                copy(a, 1 + j, me, (*chip, c), src=in_refs[a]) for j, chip in enumerate(chips)]

        def passed(a):
            return [copy(a, 4 + j, (*chip, c), sibling) for j, chip in enumerate(chips)]

        @pl.when(first)
        def _():
            for a in arrs:
                for cp in firsts(a):
                    cp.start()

        @pl.when(mid)
        def _():
            for j, chip in enumerate(chips):
                for a in arrs:
                    copy(a, 1 + j, (*chip, c), me).wait_recv()
                    passed(a)[j].start()

        @pl.when(last)
        def _():
            for a in arrs:
                copy(a, 0, sibling, me).wait_recv()
                for j, chip in enumerate(chips):
                    copy(a, 4 + j, (*chip, 1 - c), me).wait_recv()
                for cp in firsts(a) + passed(a):
                    cp.wait_send()


def _hook_specs(hook):
    if hook is None:
        return [], [], [], []
    na = len(hook.arrs)
    sems = [pltpu.SemaphoreType.DMA((hook.nsem,)), pltpu.SemaphoreType.DMA((hook.nsem,))]
    return [_ANY] * na, [_ANY] * na, hook.out_shape, sems


def _stage_fwd(st, t, params, inputs, hook=None):
    g, tm = st.g, min(st.tm, t)
    n = t // tm
    npar, nin, ncar, nout = len(params), len(inputs), len(st.carry_shapes), len(st.out_pieces)
    h_in, h_out, h_shape, h_sems = _hook_specs(hook)
    nh = len(h_in)

    def body(*refs):
        p_refs = refs[:npar]
        x_refs = refs[npar:npar + nin]
        hi_refs = refs[npar + nin:npar + nin + nh]
        o = npar + nin + nh
        o_refs = refs[o:o + nout]
        s_refs = refs[o + nout:o + nout + ncar]
        ho_refs = refs[o + nout + ncar:o + nout + ncar + nh]
        c_scr = refs[o + nout + ncar + nh:o + nout + ncar + nh + ncar]
        gi, ni = pl.program_id(0), pl.program_id(1)
        if hook is not None:
            step = gi * n + ni
            hook.run(step == 0, step == (4 * g * n) // 5, step == g * n - 1, hi_refs, ho_refs, *refs[-2:])

        @pl.when(ni == 0)
        def _():
            for c in c_scr:
                c[...] = jnp.zeros(c.shape, F32)

        ps = [r[...].astype(F32) for r in p_refs]
        xs = _load_pieces(x_refs, st.in_pieces)
        cs = [c[...] for c in c_scr]
        for s, c in zip(s_refs, cs):
            s[...] = c
        outs, ncs = st.f(ps, xs, cs)
        _store_pieces(o_refs, st.out_pieces, outs)
        for c, v in zip(c_scr, ncs):
            c[...] = v

    in_specs = [_par_spec(p, pg, g) for p, pg in zip(params, st.par_per_g)]
    in_specs += [_row_spec(tm, sum(pc), off, n, False) for pc, off in zip(st.in_pieces, st.in_offs)]
    out_specs = [_row_spec(tm, sum(pc), 0, n, False) for pc in st.out_pieces]
    out_specs += [_carry_spec(s, n, False) for s in st.carry_shapes]
    out_shape = [jax.ShapeDtypeStruct((t, g * sum(pc)), dt) for pc, dt in zip(st.out_pieces, st.out_dtypes)]
    out_shape += [jax.ShapeDtypeStruct((g, n) + s, F32) for s in st.carry_shapes]
    res = pl.pallas_call(
        body, name=st.name + "_fwd", grid=(g, n), in_specs=in_specs + h_in, out_specs=out_specs + h_out,
        out_shape=out_shape + h_shape,
        scratch_shapes=[pltpu.VMEM(s, F32) for s in st.carry_shapes] + h_sems,
        compiler_params=_cparams(("arbitrary", "arbitrary")),
    )(*params, *inputs, *(hook.arrs if hook else []))
    if hook is not None:
        hook.results = list(res[nout + ncar:])
    return list(res[:nout]), list(res[nout:nout + ncar])


def _stage_bwd(st, t, params, inputs, saved, douts, dx_dtypes, hook=None):
    g, tm = st.g, min(st.tm, t)
    n = t // tm
    npar, nin, ncar = len(params), len(inputs), len(st.carry_shapes)
    flat_d = [d for ds in douts for d in ds]
    nd = len(flat_d)
    dx_idx = [i for i, dt in enumerate(dx_dtypes) if dt is not None]
    h_in, h_out, h_shape, h_sems = _hook_specs(hook)
    nh = len(h_in)

    def body(*refs):
        p_refs = refs[:npar]
        x_refs = refs[npar:npar + nin]
        s_refs = refs[npar + nin:npar + nin + ncar]
        d_refs = refs[npar + nin + ncar:npar + nin + ncar + nd]
        hi_refs = refs[npar + nin + ncar + nd:npar + nin + ncar + nd + nh]
        o = npar + nin + ncar + nd + nh
        dp_refs = refs[o:o + npar]
        dx_refs = refs[o + npar:o + npar + len(dx_idx)]
        ho_refs = refs[o + npar + len(dx_idx):o + npar + len(dx_idx) + nh]
        dc_scr = refs[o + npar + len(dx_idx) + nh:o + npar + len(dx_idx) + nh + ncar]
        gi, ni = pl.program_id(0), pl.program_id(1)
        if hook is not None:
            step = gi * n + ni
            hook.run(step == 0, step == (4 * g * n) // 5, step == g * n - 1, hi_refs, ho_refs, *refs[-2:])

        @pl.when(ni == 0)
        def _():
            for c in dc_scr:
                c[...] = jnp.zeros(c.shape, F32)

        ps = [r[...].astype(F32) for r in p_refs]
        xs = _load_pieces(x_refs, st.in_pieces)
        cs = [s[...] for s in s_refs]
        dys = []
        k = 0
        for ds, pieces in zip(douts, st.out_pieces):
            acc = _load_pieces([d_refs[k]], [pieces])
            for j in range(1, len(ds)):
                more = _load_pieces([d_refs[k + j]], [pieces])
                acc = [a + b for a, b in zip(acc, more)]
            dys += acc
            k += len(ds)
        _, vjp = jax.vjp(st.f, ps, xs, cs)
        dps, dxs, dcs = vjp((dys, [c[...] for c in dc_scr]))
        k = 0
        per_in = []
        for pieces in st.in_pieces:
            per_in.append(dxs[k:k + len(pieces)])
            k += len(pieces)
        for ref, i in zip(dx_refs, dx_idx):
            _store_pieces([ref], [st.in_pieces[i]], per_in[i])
        for c, v in zip(dc_scr, dcs):
            c[...] = v
        for ref, dp, pg in zip(dp_refs, dps, st.par_per_g):
            first = (ni == 0) if pg else ((ni == 0) & (gi == 0))

            @pl.when(first)
            def _():
                ref[...] = jnp.zeros(ref.shape, F32)

            ref[...] += dp

    in_specs = [_par_spec(p, pg, g) for p, pg in zip(params, st.par_per_g)]
    in_specs += [_row_spec(tm, sum(pc), off, n, True) for pc, off in zip(st.in_pieces, st.in_offs)]
    in_specs += [_carry_spec(s, n, True) for s in st.carry_shapes]
    for ds, pc in zip(douts, st.out_pieces):
        in_specs += [_row_spec(tm, sum(pc), 0, n, True) for _ in ds]
    out_specs = [_par_spec(p, pg, g) for p, pg in zip(params, st.par_per_g)]
    out_specs += [_row_spec(tm, sum(st.in_pieces[i]), 0, n, True) for i in dx_idx]
    out_shape = [jax.ShapeDtypeStruct(p.shape, F32) for p in params]
    out_shape += [jax.ShapeDtypeStruct((t, g * sum(st.in_pieces[i])), dx_dtypes[i]) for i in dx_idx]
    res = pl.pallas_call(
        body, name=st.name + "_bwd", grid=(g, n), in_specs=in_specs + h_in, out_specs=out_specs + h_out,
        out_shape=out_shape + h_shape,
        scratch_shapes=[pltpu.VMEM(s, F32) for s in st.carry_shapes] + h_sems,
        compiler_params=_cparams(("arbitrary", "arbitrary")),
    )(*params, *inputs, *saved, *flat_d, *(hook.arrs if hook else []))
    if hook is not None:
        hook.results = list(res[npar + len(dx_idx):])
    return list(res[:npar]), list(res[npar:npar + len(dx_idx)])


def _pick(n, cap):
    if n <= cap:
        return n
    best = LANES
    for k in range(1, n // LANES + 1):
        if (n // LANES) % k == 0 and k * LANES <= cap:
            best = k * LANES
    return best


def _mm(name, a, b, mode, out_dtype=F32, tm=512, tn=512, b_outer=False):
    m = a.shape[1] if mode == "tn" else a.shape[0]
    k = a.shape[0] if mode == "tn" else a.shape[1]
    n = b.shape[0] if mode == "nt" else b.shape[1]
    tm, tn = _pick(m, tm), _pick(n, tn)

    def body(a_ref, b_ref, o_ref):
        o_ref[...] = _raw_dot(a_ref[...], b_ref[...], mode).astype(o_ref.dtype)

    if b_outer:
        grid = (n // tn, m // tm)
        ij = lambda p, q: (q, p)
    else:
        grid = (m // tm, n // tn)
        ij = lambda p, q: (p, q)
    if mode == "tn":
        a_spec = pl.BlockSpec((k, tm), lambda p, q: (0, ij(p, q)[0]))
    else:
        a_spec = pl.BlockSpec((tm, k), lambda p, q: (ij(p, q)[0], 0))
    if mode == "nt":
        b_spec = pl.BlockSpec((tn, k), lambda p, q: (ij(p, q)[1], 0))
    else:
        b_spec = pl.BlockSpec((k, tn), lambda p, q: (0, ij(p, q)[1]))
    return pl.pallas_call(
        body, name=name, grid=grid, in_specs=[a_spec, b_spec],
        out_specs=pl.BlockSpec((tm, tn), lambda p, q: ij(p, q)),
        out_shape=jax.ShapeDtypeStruct((m, n), out_dtype),
        compiler_params=_cparams(("arbitrary", "arbitrary")),
    )(a, b)


def _loss_stage(t, g_post, h1, ff, tgt):
    tm = min(256, t)
    n = t // tm

    def body(g_ref, h_ref, f_ref, t_ref, loss_ref, dg_ref, dh_ref, df_ref):
        ni = pl.program_id(0)
        target = t_ref[...]

        def lossf(g, h1, ff):
            e = h1 + _rms(ff, g) - target
            return 0.5 * jnp.sum(jnp.mean(e * e, axis=-1))

        l, (dg, dh, df) = jax.value_and_grad(lossf, argnums=(0, 1, 2))(g_ref[...], h_ref[...], f_ref[...])

        @pl.when(ni == 0)
        def _():
            loss_ref[...] = jnp.zeros(loss_ref.shape, F32)
            dg_ref[...] = jnp.zeros(dg_ref.shape, F32)

        loss_ref[...] += jnp.full(loss_ref.shape, l, F32)
        dg_ref[...] += dg
        dh_ref[...] = dh
        df_ref[...] = df.astype(df_ref.dtype)

    row = pl.BlockSpec((tm, D), lambda ni: (ni, 0))
    one = pl.BlockSpec((1, D), lambda ni: (0, 0))
    return pl.pallas_call(
        body, name="loss_head", grid=(n,), in_specs=[one, row, row, row],
        out_specs=[pl.BlockSpec((1, LANES), lambda ni: (0, 0)), one, row, row],
        out_shape=[jax.ShapeDtypeStruct((1, LANES), F32), jax.ShapeDtypeStruct((1, D), F32),
                   jax.ShapeDtypeStruct((t, D), F32), jax.ShapeDtypeStruct((t, D), BF)],
        compiler_params=_cparams(("arbitrary",)),
    )(g_post, h1, ff, tgt)


_ANY = pl.BlockSpec(memory_space=pl.ANY)


def _all_gather(name, blks):
    na = len(blks)

    def body(*refs):
        x_refs, out_refs = refs[:na], refs[na:2 * na]
        send_sems, recv_sems, local_sems = refs[2 * na:]
        x, y, cc = lax.axis_index("x"), lax.axis_index("y"), lax.axis_index("c")
        me, sibling = (x, y, cc), (x, y, 1 - cc)
        chips = [(1 - x, y), (x, 1 - y), (1 - x, 1 - y)]

        def copy(a, k, block, to, src=None):
            dst = out_refs[a].at[4 * block[0] + 2 * block[1] + block[2]]
            return pltpu.make_async_remote_copy(
                src_ref=dst if src is None else src, dst_ref=dst, send_sem=send_sems.at[7 * a + k],
                recv_sem=recv_sems.at[7 * a + k], device_id=to, device_id_type=MESH)

        mine, first, passed = [], [], []
        for a in range(na):
            m = pltpu.make_async_copy(x_refs[a], out_refs[a].at[4 * x + 2 * y + cc], local_sems.at[a])
            m.start()
            mine.append(m)
            cps = [copy(a, 0, me, sibling, src=x_refs[a])]
            cps += [copy(a, 1 + j, me, (*chip, cc), src=x_refs[a]) for j, chip in enumerate(chips)]
            for cp in cps:
                cp.start()
            first += cps
        for j, chip in enumerate(chips):
            for a in range(na):
                copy(a, 1 + j, (*chip, cc), me).wait_recv()
                fw = copy(a, 4 + j, (*chip, cc), sibling)
                fw.start()
                passed.append(fw)
        for a in range(na):
            copy(a, 0, sibling, me).wait_recv()
            for j, chip in enumerate(chips):
                copy(a, 4 + j, (*chip, 1 - cc), me).wait_recv()
        for cp in first + passed:
            cp.wait_send()
        for m in mine:
            m.wait()

    res = pl.pallas_call(
        body, name=name, in_specs=[_ANY] * na, out_specs=[_ANY] * na,
        out_shape=[jax.ShapeDtypeStruct((N_DEV,) + b.shape, b.dtype) for b in blks],
        scratch_shapes=[pltpu.SemaphoreType.DMA((7 * na,)), pltpu.SemaphoreType.DMA((7 * na,)),
                        pltpu.SemaphoreType.DMA((na,))],
    )(*blks)
    return list(res)


def _reduce_pair(g8s):
    na = len(g8s)

    def body(*refs):
        g_refs, recv_refs = refs[:na], refs[na:2 * na]
        ssem, rsem = refs[2 * na:]
        x, y, cc = lax.axis_index("x"), lax.axis_index("y"), lax.axis_index("c")
        chips = [(x, y), (1 - x, y), (x, 1 - y), (1 - x, 1 - y)]
        sib = (x, y, 1 - cc)
        for a in range(na):
            for k, (cx, cy) in enumerate(chips):
                pltpu.make_async_remote_copy(
                    src_ref=g_refs[a].at[4 * cx + 2 * cy + 1 - cc], dst_ref=recv_refs[a].at[k],
                    send_sem=ssem.at[a], recv_sem=rsem.at[a], device_id=sib, device_id_type=MESH).start()
        for a in range(na):
            pltpu.make_async_remote_copy(src_ref=recv_refs[a], dst_ref=recv_refs[a], send_sem=ssem.at[a],
                                         recv_sem=rsem.at[a], device_id=sib, device_id_type=MESH).wait()

    res = pl.pallas_call(
        body, name="reduce_pair", in_specs=[_ANY] * na, out_specs=[_ANY] * na,
        out_shape=[jax.ShapeDtypeStruct((4,) + g.shape[1:], g.dtype) for g in g8s],
        scratch_shapes=[pltpu.SemaphoreType.DMA((na,)), pltpu.SemaphoreType.DMA((na,))],
    )(*g8s)
    return list(res)


def _swap_chips(name, sends):
    na = len(sends)

    def body(*refs):
        s_refs, r_refs = refs[:na], refs[na:2 * na]
        ssems, rsems = refs[2 * na:]
        x, y, cc = lax.axis_index("x"), lax.axis_index("y"), lax.axis_index("c")
        targets = [(1 - x, y, cc), (x, 1 - y, cc), (1 - x, 1 - y, cc)]
        cps = [pltpu.make_async_remote_copy(src_ref=s_refs[a].at[k], dst_ref=r_refs[a].at[k],
                                            send_sem=ssems.at[3 * a + k], recv_sem=rsems.at[3 * a + k],
                                            device_id=targets[k], device_id_type=MESH)
               for a in range(na) for k in range(3)]
        for cp in cps:
            cp.start()
        for cp in cps:
            cp.wait()

    res = pl.pallas_call(
        body, name=name, in_specs=[_ANY] * na, out_specs=[_ANY] * na,
        out_shape=[jax.ShapeDtypeStruct(s.shape, s.dtype) for s in sends],
        scratch_shapes=[pltpu.SemaphoreType.DMA((3 * na,)), pltpu.SemaphoreType.DMA((3 * na,))],
    )(*sends)
    return list(res)


def _pick_rows(r, c):
    if r * c * 4 <= TILE_BYTES or r % 16:
        return r
    best = 16
    for tr in range(16, r, 16):
        if r % tr == 0 and tr * c * 4 <= TILE_BYTES:
            best = tr
    return best


def _pair_sum(name, idx4, g8, recv4):
    _, r, c = g8.shape
    tr = _pick_rows(r, c)

    def body(idx_ref, a_ref, b_ref, o0_ref, o3_ref):
        k = pl.program_id(1)
        s = a_ref[...].astype(F32) + b_ref[...].astype(F32)

        @pl.when(k == 0)
        def _():
            o0_ref[...] = s

        @pl.when(k > 0)
        def _():
            o3_ref[...] = s.astype(BF)

    spec = pltpu.PrefetchScalarGridSpec(
        num_scalar_prefetch=1, grid=(r // tr, 4),
        in_specs=[pl.BlockSpec((None, tr, c), lambda i, k, idx: (idx[k], i, 0)),
                  pl.BlockSpec((None, tr, c), lambda i, k, idx: (k, i, 0))],
        out_specs=[pl.BlockSpec((tr, c), lambda i, k, idx: (i, 0)),
                   pl.BlockSpec((None, tr, c), lambda i, k, idx: (jnp.maximum(k - 1, 0), i, 0))])
    return pl.pallas_call(
        body, name=name, grid_spec=spec,
        out_shape=[jax.ShapeDtypeStruct((r, c), F32), jax.ShapeDtypeStruct((3, r, c), BF)],
        compiler_params=_cparams(("arbitrary", "arbitrary")),
    )(idx4, g8, recv4)


def _adamw(w, g, m, v):
    m = ADAM_B1 * m + (1.0 - ADAM_B1) * g
    v = ADAM_B2 * v + (1.0 - ADAM_B2) * jnp.square(g)
    m_hat = m / (1.0 - ADAM_B1 ** ADAM_STEP)
    v_hat = v / (1.0 - ADAM_B2 ** ADAM_STEP)
    delta = -ADAM_LR * (m_hat / (jnp.sqrt(v_hat) + ADAM_EPS) + ADAM_WD * w)
    return delta, m, v


def _adam_sharded(name, idx1, own, recv, w, m, v):
    r, c = w.shape
    tr = _pick_rows(r, c)
    nj = recv.shape[0]

    def body(idx_ref, p_ref, r_ref, w_ref, m_ref, v_ref, g_out, d_out, m_out, v_out):
        g = p_ref[...].astype(F32)
        for k in range(nj):
            g = g + r_ref[k].astype(F32)
        d, mn, vn = _adamw(w_ref[...], g, m_ref[...], v_ref[...])
        g_out[...] = g
        d_out[...] = d
        m_out[...] = mn
        v_out[...] = vn

    row = pl.BlockSpec((tr, c), lambda i, idx: (i, 0))
    spec = pltpu.PrefetchScalarGridSpec(
        num_scalar_prefetch=1, grid=(r // tr,),
        in_specs=[pl.BlockSpec((None, tr, c), lambda i, idx: (idx[0], i, 0)),
                  pl.BlockSpec((nj, tr, c), lambda i, idx: (0, i, 0)), row, row, row],
        out_specs=[row] * 4)
    return pl.pallas_call(
        body, name=name, grid_spec=spec, out_shape=[jax.ShapeDtypeStruct((r, c), F32)] * 4,
        compiler_params=_cparams(("arbitrary",)),
    )(idx1, own, recv, w, m, v)


def _repl_rows():
    rows, r = {}, 0
    for name, cols in REPL:
        rows[name] = r
        r += REPL_ROWS.get(name, 1) * ((cols + D - 1) // D)
    return rows


def _pack_replicated(grads):
    rows = _repl_rows()
    names = [n for n, _ in REPL]

    def body(*refs):
        o_ref = refs[-1]
        o_ref[...] = jnp.zeros(o_ref.shape, F32)
        for name, ref in zip(names, refs[:-1]):
            r0 = rows[name]
            nr, nc = ref.shape
            if nc <= D:
                o_ref[r0:r0 + nr, 0:nc] = ref[...]
            else:
                for j in range((nc + D - 1) // D):
                    lo, hi = j * D, min(nc, (j + 1) * D)
                    o_ref[r0 + j:r0 + j + 1, 0:hi - lo] = ref[:, lo:hi]

    return pl.pallas_call(body, name="pack_replicated", out_shape=jax.ShapeDtypeStruct((REPL_TOTAL, D), F32),
                          compiler_params=_cparams())(*[grads[n] for n in names])


def _adam_replicated(g8, ws, ms, vs):
    rows = _repl_rows()
    names = [n for n, _ in REPL]
    np_ = len(names)

    def body(*refs):
        g_ref = refs[0]
        w_refs, m_refs, v_refs = refs[1:1 + np_], refs[1 + np_:1 + 2 * np_], refs[1 + 2 * np_:1 + 3 * np_]
        outs = refs[1 + 3 * np_:1 + 7 * np_]
        scr = refs[-1]
        g = g_ref[0]
        for k in range(1, N_DEV):
            g = g + g_ref[k]
        scr[...] = g
        for i, name in enumerate(names):
            r0 = rows[name]
            nr, nc = w_refs[i].shape
            if nc <= D:
                gi = scr[r0:r0 + nr, 0:nc]
            else:
                parts = []
                for j in range((nc + D - 1) // D):
                    lo, hi = j * D, min(nc, (j + 1) * D)
                    parts.append(scr[r0 + j:r0 + j + 1, 0:hi - lo])
                gi = jnp.concatenate(parts, axis=1)
            d, mn, vn = _adamw(w_refs[i][...], gi, m_refs[i][...], v_refs[i][...])
            outs[i][...] = gi
            outs[np_ + i][...] = d
            outs[2 * np_ + i][...] = mn
            outs[3 * np_ + i][...] = vn

    shp = [jax.ShapeDtypeStruct(w.shape, F32) for w in ws]
    res = pl.pallas_call(body, name="adam_replicated", out_shape=shp * 4,
                         scratch_shapes=[pltpu.VMEM((REPL_TOTAL, D), F32)], compiler_params=_cparams(),
                         )(g8, *ws, *ms, *vs)
    return [dict(zip(names, res[k * np_:(k + 1) * np_])) for k in range(4)]


_WEIGHTS = ("attn_pre_norm", "w_in", "hgrn_lb", "hgrn_gnorm", "w_branch_a", "rwkv_mu", "rwkv_w0", "rwkv_w2",
            "rwkv_a0", "rwkv_a2", "rwkv_g2", "rwkv_k_k", "rwkv_k_a", "rwkv_r_k", "rwkv_ln_w", "rwkv_ln_b",
            "w_branch_b", "w_out", "attn_post_norm", "ffn_pre_norm", "w_up", "conv_w", "conv_b", "w_down",
            "ffn_post_norm")
_BIG = ("w_in", "w_up", "w_down", "w_branch_a", "w_branch_b", "w_out")


def _stages():
    one = [D]
    hw = HG_K * HG_PER_STEP
    rw = LANES * RW_PAIRS_PER_STEP
    return dict(
        pre1=_Stage("pre1", _f_pre1, 1, 256, [False], [one], [0], [], [one], [BF]),
        mixers=_Stage("mixers", _f_mixers, 1, RW_CHUNK, [False, False], [one] * 10, [0, 1, 2, 3] + [0] * 6,
                      [(hw, HG_K), (rw, LANES)], [one, one], [BF, F32]),
        rwpre=_Stage("rwkv_pre", _f_rwpre, 1, 128, [False] * 8, [[D], [D], [D], [LANES], [LANES]], [4, 5, 6, 56, 57],
                     [(1, RW_COLS)], [one] * 7, [F32] * 7),
        rwpost=_Stage("rwkv_post", _f_rwpost, 1, 128, [False] * 3, [one] * 5, [0] * 5, [], [one], [BF]),
        merge=_Stage("merge", _f_merge, 4, 512, [], [[256]] * 4, [29, 33, 0, 0], [], [[256]], [BF]),
        post1=_Stage("post1", _f_post1, 1, 256, [False, False], [one, one], [0, 0], [], [one, one], [F32, BF]),
        conv=_Stage("conv", _f_conv, 1, 128, [False, False], [[DFF], [DFF]], [0, 1], [(1, 2 * DFF), (1, 2 * DFF)],
                    [[DFF]], [BF]),
    )


def _cols_to_blocks(w, per):
    return w.reshape(w.shape[0], N_DEV, per).transpose(1, 0, 2)


def _blocks_to_cols(g):
    return g.transpose(1, 0, 2).reshape(g.shape[1], N_DEV * g.shape[2])


def kernel(x, attn_pre_norm, w_in, hgrn_lb, hgrn_gnorm, w_branch_a, rwkv_mu, rwkv_w0, rwkv_w2, rwkv_a0, rwkv_a2, rwkv_g2, rwkv_k_k, rwkv_k_a, rwkv_r_k, rwkv_ln_w, rwkv_ln_b, w_branch_b, w_out, attn_post_norm, ffn_pre_norm, w_up, conv_w, conv_b, w_down, ffn_post_norm, loss_target, m_attn_pre_norm, m_w_in, m_hgrn_lb, m_hgrn_gnorm, m_w_branch_a, m_rwkv_mu, m_rwkv_w0, m_rwkv_w2, m_rwkv_a0, m_rwkv_a2, m_rwkv_g2, m_rwkv_k_k, m_rwkv_k_a, m_rwkv_r_k, m_rwkv_ln_w, m_rwkv_ln_b, m_w_branch_b, m_w_out, m_attn_post_norm, m_ffn_pre_norm, m_w_up, m_conv_w, m_conv_b, m_w_down, m_ffn_post_norm, v_attn_pre_norm, v_w_in, v_hgrn_lb, v_hgrn_gnorm, v_w_branch_a, v_rwkv_mu, v_rwkv_w0, v_rwkv_w2, v_rwkv_a0, v_rwkv_a2, v_rwkv_g2, v_rwkv_k_k, v_rwkv_k_a, v_rwkv_r_k, v_rwkv_ln_w, v_rwkv_ln_b, v_w_branch_b, v_w_out, v_attn_post_norm, v_ffn_pre_norm, v_w_up, v_conv_w, v_conv_b, v_w_down, v_ffn_post_norm):
    w = dict(attn_pre_norm=attn_pre_norm, w_in=w_in, hgrn_lb=hgrn_lb, hgrn_gnorm=hgrn_gnorm, w_branch_a=w_branch_a, rwkv_mu=rwkv_mu, rwkv_w0=rwkv_w0, rwkv_w2=rwkv_w2, rwkv_a0=rwkv_a0, rwkv_a2=rwkv_a2, rwkv_g2=rwkv_g2, rwkv_k_k=rwkv_k_k, rwkv_k_a=rwkv_k_a, rwkv_r_k=rwkv_r_k, rwkv_ln_w=rwkv_ln_w, rwkv_ln_b=rwkv_ln_b, w_branch_b=w_branch_b, w_out=w_out, attn_post_norm=attn_post_norm, ffn_pre_norm=ffn_pre_norm, w_up=w_up, conv_w=conv_w, conv_b=conv_b, w_down=w_down, ffn_post_norm=ffn_post_norm)
    mo = dict(attn_pre_norm=m_attn_pre_norm, w_in=m_w_in, hgrn_lb=m_hgrn_lb, hgrn_gnorm=m_hgrn_gnorm, w_branch_a=m_w_branch_a, rwkv_mu=m_rwkv_mu, rwkv_w0=m_rwkv_w0, rwkv_w2=m_rwkv_w2, rwkv_a0=m_rwkv_a0, rwkv_a2=m_rwkv_a2, rwkv_g2=m_rwkv_g2, rwkv_k_k=m_rwkv_k_k, rwkv_k_a=m_rwkv_k_a, rwkv_r_k=m_rwkv_r_k, rwkv_ln_w=m_rwkv_ln_w, rwkv_ln_b=m_rwkv_ln_b, w_branch_b=m_w_branch_b, w_out=m_w_out, attn_post_norm=m_attn_post_norm, ffn_pre_norm=m_ffn_pre_norm, w_up=m_w_up, conv_w=m_conv_w, conv_b=m_conv_b, w_down=m_w_down, ffn_post_norm=m_ffn_post_norm)
    vo = dict(attn_pre_norm=v_attn_pre_norm, w_in=v_w_in, hgrn_lb=v_hgrn_lb, hgrn_gnorm=v_hgrn_gnorm, w_branch_a=v_w_branch_a, rwkv_mu=v_rwkv_mu, rwkv_w0=v_rwkv_w0, rwkv_w2=v_rwkv_w2, rwkv_a0=v_rwkv_a0, rwkv_a2=v_rwkv_a2, rwkv_g2=v_rwkv_g2, rwkv_k_k=v_rwkv_k_k, rwkv_k_a=v_rwkv_k_a, rwkv_r_k=v_rwkv_r_k, rwkv_ln_w=v_rwkv_ln_w, rwkv_ln_b=v_rwkv_ln_b, w_branch_b=v_w_branch_b, w_out=v_w_out, attn_post_norm=v_attn_post_norm, ffn_pre_norm=v_ffn_pre_norm, w_up=v_w_up, conv_w=v_conv_w, conv_b=v_conv_b, w_down=v_w_down, ffn_post_norm=v_ffn_post_norm)

    t = x.shape[1]
    x2 = x.reshape(t, D)
    tgt = loss_target.reshape(t, D)
    st = _stages()

    me = 4 * lax.axis_index("x") + 2 * lax.axis_index("y") + lax.axis_index("c")
    small = jnp.concatenate([rwkv_w2[0], rwkv_a2[0], rwkv_g2[0]], axis=0).astype(BF)
    g_in, g_small = _all_gather("gather_weights", [w_in[0].astype(BF), small])
    fw_in = _blocks_to_cols(g_in)
    z64 = jnp.zeros((64, D), BF)
    w2p = jnp.concatenate([_blocks_to_cols(g_small[:, 0:64]), z64], axis=0)
    a2p = jnp.concatenate([z64, _blocks_to_cols(g_small[:, 64:128])], axis=0)
    g2f = _blocks_to_cols(g_small[:, 128:256])
    conv_bits = lax.bitcast_convert_type(conv_w[0], BF).reshape(3, 2 * 704)
    late = [w[k][0].astype(BF) for k in _BIG[1:]] + [conv_bits]
    late_gather = _Exchange("gather2", late)
    r_k = rwkv_r_k.reshape(1, D)

    (xn,), _ = _stage_fwd(st["pre1"], t, [attn_pre_norm], [x2])
    z = _mm("in_proj", xn, fw_in, "nn", F32, tm=256, tn=4736, b_outer=True)
    rwpre_par = [rwkv_mu, rwkv_w0, w2p, rwkv_a0, a2p, g2f, rwkv_k_k, rwkv_k_a]
    rw_in, rwpre_saved = _stage_fwd(st["rwpre"], t, rwpre_par, [z] * 5)
    r_, lw_, k_, v_, av_, bv_, g_ = rw_in
    hg_par = [hgrn_lb, hgrn_gnorm]
    mix_in = [z] * 4 + [r_, lw_, k_, v_, av_, bv_]
    (o_a, y_), mix_saved = _stage_fwd(st["mixers"], t, hg_par, mix_in, hook=late_gather)
    gl = [lax.dynamic_update_slice(g, own[None], (me, 0, 0)) for g, own in zip(late_gather.results, late)]
    fw_up = _blocks_to_cols(gl[0])
    fw_down = gl[1].reshape(DFF, D)
    fw_a, fw_b, fw_out = (g.reshape(D, D) for g in gl[2:5])
    conv_full = _blocks_to_cols(lax.bitcast_convert_type(gl[5].reshape(N_DEV, 3, 704, 2), F32))
    rwpost_par = [rwkv_ln_w, rwkv_ln_b, r_k]
    (o_b,), _ = _stage_fwd(st["rwpost"], t, rwpost_par, [y_, r_, k_, v_, g_])
    y_a = _mm("branch_a", o_a, fw_a, "nn")
    y_b = _mm("branch_b", o_b, fw_b, "nn")
    (merged,), _ = _stage_fwd(st["merge"], t, [], [z, z, y_a, y_b])
    mix = _mm("out_proj", merged, fw_out, "nn")
    (h1, xn2), _ = _stage_fwd(st["post1"], t, [attn_post_norm, ffn_pre_norm], [x2, mix])
    hu = _mm("up_proj", xn2, fw_up, "nn", F32, tm=512, tn=1408)
    conv_par = [conv_full, conv_b]
    (act,), conv_saved = _stage_fwd(st["conv"], t, conv_par, [hu, hu])
    ff = _mm("down_proj", act, fw_down, "nn")

    loss_acc, d_ffn_post, dh1, dff = _loss_stage(t, ffn_post_norm, h1, ff, tgt)
    dact = _mm("d_act", dff, fw_down, "nt", F32, tm=512, tn=1408)
    dw_down = _mm("dw_down", act, dff, "tn", BF, tm=1408, tn=512)
    (dcw, dcb), (dhu_g, dhu_v) = _stage_bwd(st["conv"], t, conv_par, [hu, hu], conv_saved, [[dact]], [BF, BF])
    dhu = jnp.concatenate([dhu_g, dhu_v], axis=1)
    dxn2 = _mm("d_xn2", dhu, fw_up, "nt", F32, tm=256, tn=512)
    dw_up = _mm("dw_up", xn2, dhu, "tn", BF, tm=512, tn=1408)
    (d_post, d_pre2), (dx_a, dmix) = _stage_bwd(st["post1"], t, [attn_post_norm, ffn_pre_norm], [x2, mix], [],
                                                 [[dh1], [dxn2]], [F32, BF])
    dmerged = _mm("d_merged", dmix, fw_out, "nt")
    dw_out = _mm("dw_out", merged, dmix, "tn", BF)
    _, (dga, dgb, dy_a, dy_b) = _stage_bwd(st["merge"], t, [], [z, z, y_a, y_b], [], [[dmerged]], [BF, BF, BF, BF])
    do_a = _mm("d_oa", dy_a, fw_a, "nt")
    dw_a = _mm("dw_a", o_a, dy_a, "tn", BF)
    do_b = _mm("d_ob", dy_b, fw_b, "nt")
    dw_b = _mm("dw_b", o_b, dy_b, "tn", BF)
    (d_lnw, d_lnb, d_rk), (dy_, dr1, dk1, dv1, dg_) = _stage_bwd(
        st["rwpost"], t, rwpost_par, [y_, r_, k_, v_, g_], [], [[do_b]], [F32] * 5)
    early = [_cols_to_blocks(dw_up, 704), dw_down.reshape(N_DEV, 352, D), dw_a.reshape(N_DEV, 128, D),
             dw_b.reshape(N_DEV, 128, D), dw_out.reshape(N_DEV, 128, D), _cols_to_blocks(dcw.astype(BF), 704)]
    early_scatter = _Exchange("scatter", early)
    (d_lb, d_gn), mix_dx = _stage_bwd(st["mixers"], t, hg_par, mix_in, mix_saved, [[do_a], [dy_]],
                                      [BF] * 4 + [F32] * 6, hook=early_scatter)
    dz_h = mix_dx[:4]
    dr2, dlw, dk2, dv2, dav, dbv = mix_dx[4:]
    rwpre_dp, dz_r = _stage_bwd(
        st["rwpre"], t, rwpre_par, [z] * 5, rwpre_saved,
        [[dr1, dr2], [dlw], [dk1, dk2], [dv1, dv2], [dav], [dbv], [dg_]], [BF] * 5)
    d_mu, d_w0, d_w2p, d_a0, d_a2p, d_g2, d_kk, d_ka = rwpre_dp
    dz = jnp.concatenate(dz_h + dz_r + [dga, dgb], axis=1)
    dxn = _mm("d_xn", dz, fw_in, "nt", F32, tm=256, tn=512)
    dw_in = _mm("dw_in", xn, dz, "tn", BF, tm=1024, tn=256)
    (d_pre1,), (dx_b,) = _stage_bwd(st["pre1"], t, [attn_pre_norm], [x2], [], [[dxn]], [F32])
    grad_x = (dx_a + dx_b).reshape(x.shape)
    loss = lax.psum(loss_acc[0, 0], ("x", "y", "c"))

    ax, ay, ac = lax.axis_index("x"), lax.axis_index("y"), lax.axis_index("c")
    idx4 = jnp.stack([4 * cx + 2 * cy + ac for cx, cy in ((ax, ay), (1 - ax, ay), (ax, 1 - ay), (1 - ax, 1 - ay))])
    idx4 = idx4.astype(jnp.int32)
    idx_me, idx_0 = idx4[0:1], jnp.zeros((1,), jnp.int32)
    d_small = jnp.concatenate([d_w2p[:64], d_a2p[64:], d_g2], axis=0).astype(BF)
    g8s = [_cols_to_blocks(dw_in, 1184), _cols_to_blocks(d_small, LANES)]
    recv4s = _reduce_pair(g8s)
    sums = [_pair_sum("pair_sum_" + n, idx4, g, r) for n, g, r in zip(("w_in", "small"), g8s, recv4s)]
    recv3s = _swap_chips("reduce_chips", [s[1] for s in sums])

    def small_of(src):
        return jnp.concatenate([src["rwkv_w2"][0], src["rwkv_a2"][0], src["rwkv_g2"][0]], axis=0)

    sh_out = [dict() for _ in range(4)]
    res = _adam_sharded("adam_w_in", idx_0, sums[0][0][None], recv3s[0], *[src["w_in"][0] for src in (w, mo, vo)])
    res_s = _adam_sharded("adam_small", idx_0, sums[1][0][None], recv3s[1], *[small_of(src) for src in (w, mo, vo)])
    for kind in range(4):
        sh_out[kind]["w_in"] = res[kind][None]
        sh_out[kind]["rwkv_w2"] = res_s[kind][0:64][None]
        sh_out[kind]["rwkv_a2"] = res_s[kind][64:128][None]
        sh_out[kind]["rwkv_g2"] = res_s[kind][128:256][None]
    for n, own, recv in zip(_BIG[1:] + ("conv_w",), early, early_scatter.results):
        res = _adam_sharded("adam_" + n, idx_me, own, recv, *[src[n][0] for src in (w, mo, vo)])
        for kind in range(4):
            sh_out[kind][n] = res[kind][None]

    rg = dict(attn_pre_norm=d_pre1, hgrn_lb=d_lb, hgrn_gnorm=d_gn, rwkv_mu=d_mu, rwkv_w0=d_w0, rwkv_a0=d_a0,
              rwkv_k_k=d_kk, rwkv_k_a=d_ka, rwkv_r_k=d_rk, rwkv_ln_w=d_lnw, rwkv_ln_b=d_lnb, attn_post_norm=d_post,
              ffn_pre_norm=d_pre2, conv_b=dcb, ffn_post_norm=d_ffn_post)
    (g8,) = _all_gather("gather_small_grads", [_pack_replicated(rg)])
    rnames = [n for n, _ in REPL]
    flat = lambda src: [src[n].reshape(1, D) if n == "rwkv_r_k" else src[n] for n in rnames]
    rp_out = _adam_replicated(g8, flat(w), flat(mo), flat(vo))
    for kind in range(4):
        rp_out[kind]["rwkv_r_k"] = rp_out[kind]["rwkv_r_k"].reshape(rwkv_r_k.shape)

    outs = [loss, grad_x]
    for kind in range(4):
        for name in _WEIGHTS:
            outs.append(sh_out[kind][name] if name in sh_out[kind] else rp_out[kind][name])
    return tuple(outs)
```

```python
import functools

import jax
import jax.numpy as jnp
from jax import lax
from jax.experimental import pallas as pl
from jax.experimental.pallas import tpu as pltpu

F32 = jnp.float32
BF = jnp.bfloat16
MESH = pl.DeviceIdType.MESH

D = 1024
HG_HEADS = 8
HG_K = 128
HG_CHUNK = 32
HG_SCALE = HG_K ** -0.5
HG_PER_STEP = 8
RW_HEADS = 16
RW_N = 64
RW_CHUNK = 64
RW_PAIRS_PER_STEP = 8
DFF = 2816
IN_COLS = 9472
RW_COLS = 3328
EPS = 1e-6
GN_EPS = 1e-5 * RW_N
ADAM_LR = 0.001
ADAM_B1 = 0.9
ADAM_B2 = 0.999
ADAM_EPS = 1e-08
ADAM_WD = 0.01
ADAM_STEP = 10
N_DEV = 8
LANES = 128
VMEM_LIMIT = 56 * 1024 * 1024
TILE_BYTES = 1280 * 1024

REPL = (("attn_pre_norm", 1024), ("hgrn_lb", 1024), ("hgrn_gnorm", 1024), ("rwkv_mu", 3328), ("rwkv_w0", 1024),
        ("rwkv_a0", 1024), ("rwkv_k_k", 1024), ("rwkv_k_a", 1024), ("rwkv_r_k", 1024), ("rwkv_ln_w", 1024),
        ("rwkv_ln_b", 1024), ("attn_post_norm", 1024), ("ffn_pre_norm", 1024), ("conv_b", 5632), ("ffn_post_norm", 1024))
REPL_ROWS = {"hgrn_lb": 2}
REPL_TOTAL = 32


def _cparams(sem=None, **kw):
    return pltpu.CompilerParams(dimension_semantics=sem, vmem_limit_bytes=VMEM_LIMIT, **kw)


_DN = {"nn": ((1,), (0,)), "nt": ((1,), (1,)), "tn": ((0,), (0,))}


def _raw_dot(a, b, mode):
    return lax.dot_general(a.astype(BF), b.astype(BF), (_DN[mode], ((), ())), preferred_element_type=F32)


@functools.partial(jax.custom_vjp, nondiff_argnums=(2,))
def _dot(a, b, mode):
    return _raw_dot(a, b, mode)


def _dot_fwd(a, b, mode):
    return _raw_dot(a, b, mode), (a, b)


def _dot_bwd(mode, res, g):
    a, b = res
    if mode == "nn":
        return _dot(g, b, "nt"), _dot(a, g, "tn")
    if mode == "nt":
        return _dot(g, b, "nn"), _dot(g, a, "tn")
    return _dot(b, g, "nt"), _dot(a, g, "nn")


_dot.defvjp(_dot_fwd, _dot_bwd)


def _bf_pieces(x, n):
    out, r = [], x
    for i in range(n):
        p = r.astype(BF)
        out.append(p)
        if i + 1 < n:
            r = r - p.astype(F32)
    return out


def _raw_split_dot(x, e, mode, n, x_left):
    eb = e.astype(BF)
    acc = None
    for p in _bf_pieces(x, n):
        ops = (p, eb) if x_left else (eb, p)
        t = lax.dot_general(*ops, (_DN[mode], ((), ())), preferred_element_type=F32)
        acc = t if acc is None else acc + t
    return acc


@functools.partial(jax.custom_vjp, nondiff_argnums=(2, 3))
def _edot(x, e, mode, n):
    return _raw_split_dot(x, e, mode, n, True)


def _edot_fwd(x, e, mode, n):
    return _raw_split_dot(x, e, mode, n, True), e


def _edot_bwd(mode, n, e, g):
    return _raw_split_dot(g, e, "nt" if mode == "nn" else "nn", n, True), jnp.zeros_like(e)


_edot.defvjp(_edot_fwd, _edot_bwd)


@functools.partial(jax.custom_vjp, nondiff_argnums=(2,))
def _tdot(tri, x, n):
    return _raw_split_dot(x, tri, "nn", n, False)


def _tdot_fwd(tri, x, n):
    return _raw_split_dot(x, tri, "nn", n, False), tri


def _tdot_bwd(n, tri, g):
    return jnp.zeros_like(tri), _raw_split_dot(g, tri, "tn", n, False)


_tdot.defvjp(_tdot_fwd, _tdot_bwd)


def _row(x, i):
    r = lax.broadcasted_iota(jnp.int32, x.shape, 0)
    return jnp.sum(jnp.where(r == i, x, 0.0), axis=0, keepdims=True)


def _shift_down(x, prev):
    t = x.shape[0]

    @jax.custom_vjp
    def sh(x, prev):
        r = lax.broadcasted_iota(jnp.int32, x.shape, 0)
        return jnp.where(r == 0, prev, pltpu.roll(x, 1, 0))

    def fwd(x, prev):
        return sh(x, prev), None

    def bwd(_, g):
        r = lax.broadcasted_iota(jnp.int32, g.shape, 0)
        dx = jnp.where(r == t - 1, 0.0, pltpu.roll(g, t - 1, 0))
        return dx, jnp.sum(jnp.where(r == 0, g, 0.0), axis=0, keepdims=True)

    sh.defvjp(fwd, bwd)
    return sh(x, prev)


def _sigmoid(x):
    return jax.nn.sigmoid(x)


def _silu(x):
    return x * jax.nn.sigmoid(x)


def _softplus(x):
    return jnp.maximum(x, 0.0) + jnp.log(1.0 + jnp.exp(-jnp.abs(x)))


def _rms(x, g):
    return (x * lax.rsqrt(jnp.mean(x * x, axis=-1, keepdims=True) + EPS)) * g


def _headmat():
    j = lax.broadcasted_iota(jnp.int32, (D, LANES), 0)
    h = lax.broadcasted_iota(jnp.int32, (D, LANES), 1)
    e = jnp.where(lax.shift_right_logical(j, 6) == h, 1.0, 0.0).astype(F32)
    pad = jnp.where(lax.broadcasted_iota(jnp.int32, (1, LANES), 1) >= RW_HEADS, 1.0, 0.0).astype(F32)
    return e, pad


def _tril(c):
    r = lax.broadcasted_iota(jnp.int32, (c, c), 0)
    cc = lax.broadcasted_iota(jnp.int32, (c, c), 1)
    return cc <= r


def _f_pre1(ps, xs, cs):
    return [_rms(xs[0], ps[0])], []


def _f_hgrn(ps, xs, cs):
    lbraw, gn = ps
    hq, hf, hi, hg = xs
    hd = range(HG_PER_STEP)
    st = [cs[0][p * HG_K:(p + 1) * HG_K] for p in hd]
    l0, l1 = _row(lbraw, 0), _row(lbraw, 1)
    m = jnp.maximum(l0, l1)
    e0, e1 = jnp.exp(l0 - m), jnp.exp(l1 - m)
    lb = e0 / (e0 + e1)
    q = _silu(hq) * HG_SCALE
    f = lb + (1.0 - lb) * _sigmoid(hf)
    kh = 1.0 - f
    gl = jnp.log(f)
    c = HG_CHUNK
    low = _tril(c)
    tri = jnp.where(low, 1.0, 0.0).astype(F32)
    outs = []
    for i in range(hq.shape[0] // c):
        rows = slice(i * c, (i + 1) * c)
        b = _tdot(tri, gl[rows], 3)
        bref = _row(b, c // 2 - 1)
        blast = _row(b, c - 1)
        qi = q[rows] * jnp.exp(b - bref)
        ki = kh[rows] * jnp.exp(bref - b)
        qd = q[rows] * jnp.exp(b)
        kd = kh[rows] * jnp.exp(blast - b)
        dec = jnp.exp(blast)
        sl = [slice(p * HG_K, (p + 1) * HG_K) for p in hd]
        sc = [jnp.where(low, _dot(qi[:, sl[p]], ki[:, sl[p]], "nt"), 0.0) for p in hd]
        o = [_dot(sc[p], hi[rows, sl[p]], "nn") + _dot(qd[:, sl[p]], st[p], "nt") for p in hd]
        u = [_dot(hi[rows, sl[p]], kd[:, sl[p]], "tn") for p in hd]
        st = [dec[:, sl[p]] * st[p] + u[p] for p in hd]
        outs.append(jnp.concatenate(o, axis=1) if len(o) > 1 else o[0])
    o = outs[0] if len(outs) == 1 else jnp.concatenate(outs, axis=0)
    on = []
    for p in hd:
        op = o[:, p * HG_K:(p + 1) * HG_K]
        on.append(op * lax.rsqrt(jnp.mean(op * op, axis=-1, keepdims=True) + EPS))
    o = jnp.concatenate(on, axis=1) if len(on) > 1 else on[0]
    o = o * gn
    return [o * _silu(hg)], [jnp.concatenate(st, axis=0) if len(st) > 1 else st[0]]


_RW_OFFS = (0, 1024, 2048, 3072, 3200, 3328)


def _f_rwpre(ps, xs, cs):
    mu, w0, w2p, a0, a2p, g2, k_k, k_a = ps
    (prev,) = cs
    t = xs[0].shape[0]
    zs = []
    for i, z in enumerate(xs):
        lo, hi = _RW_OFFS[i], _RW_OFFS[i + 1]
        zs.append(z + mu[:, lo:hi] * (_shift_down(z, prev[:, lo:hi]) - z))
    rr, kr, vr, wa, gz = zs
    w_log = -_softplus(-(w0 + _dot(jnp.tanh(wa), w2p, "nn"))) - 0.5
    lw = -jnp.exp(w_log)
    a = _sigmoid(a0 + _dot(wa, a2p, "nn"))
    g = _dot(_sigmoid(gz), g2, "nn")
    e, pad = _headmat()
    kkr = kr * k_k
    nrm = jnp.sqrt(_edot(kkr * kkr, e, "nn", 2) + pad)
    kk = kkr / _edot(jnp.maximum(nrm, 1e-12), e, "nt", 2)
    k2 = kr * (1.0 + (a - 1.0) * k_a)
    newprev = jnp.concatenate([_row(z, t - 1) for z in xs], axis=1)
    return [rr, lw, k2, vr, -kk, kk * a, g], [newprev]


def _f_rwscan(ps, xs, cs):
    npair = RW_PAIRS_PER_STEP
    pr = range(npair)
    r, lw, k, v, av, bv = [[x[:, p * LANES:(p + 1) * LANES] for p in pr] for x in xs]
    sv = [cs[0][p * LANES:(p + 1) * LANES] for p in pr]
    c = RW_CHUNK
    n = 2 * c
    tri = jnp.where(_tril(c), 1.0, 0.0).astype(F32)
    cl = [_tdot(tri, lw[p], 3) for p in pr]
    cl_last = [_row(cl[p], c - 1) for p in pr]
    lane = lax.broadcasted_iota(jnp.int32, (c, LANES), 1)
    h0 = lane < RW_N

    def stack(x):
        return jnp.concatenate([jnp.where(h0, x, 0.0), jnp.where(h0, 0.0, x)], axis=0)

    am = [stack(av[p] * jnp.exp(cl[p] - lw[p])) for p in pr]
    bm = [stack(bv[p] * jnp.exp(-cl[p])) for p in pr]
    km = [stack(k[p] * jnp.exp(-cl[p])) for p in pr]
    rm = [stack(r[p] * jnp.exp(cl[p])) for p in pr]
    vm = [stack(v[p]) for p in pr]
    rn = lax.broadcasted_iota(jnp.int32, (n, n), 0)
    cn = lax.broadcasted_iota(jnp.int32, (n, n), 1)
    blk = (rn >= c) == (cn >= c)
    strict = blk & (cn < rn)
    incl = blk & (cn <= rn)
    lab = [jnp.where(strict, _dot(am[p], bm[p], "nt"), 0.0) for p in pr]
    lak = [jnp.where(strict, _dot(am[p], km[p], "nt"), 0.0) for p in pr]
    wrb = [jnp.where(incl, _dot(rm[p], bm[p], "nt"), 0.0) for p in pr]
    wrk = [jnp.where(incl, _dot(rm[p], km[p], "nt"), 0.0) for p in pr]
    eye = jnp.where(rn == cn, 1.0, 0.0).astype(F32)
    tinv = [eye + lab[p] for p in pr]
    pw = lab
    for _ in range(5):
        pw = [_dot(pw[p], pw[p], "nn") for p in pr]
        tinv = [tinv[p] + _dot(tinv[p], pw[p], "nn") for p in pr]
    rhs = [_dot(am[p], sv[p], "nt") + _dot(lak[p], vm[p], "nn") for p in pr]
    um = [_dot(tinv[p], rhs[p], "nn") for p in pr]
    ym = [_dot(rm[p], sv[p], "nt") + _dot(wrb[p], um[p], "nn") + _dot(wrk[p], vm[p], "nn") for p in pr]
    sn = [(sv[p] + _dot(um[p], bm[p], "tn") + _dot(vm[p], km[p], "tn")) * jnp.exp(cl_last[p]) for p in pr]
    ys = [ym[p][:c] + ym[p][c:] for p in pr]
    return [jnp.concatenate(ys, axis=1)], [jnp.concatenate(sn, axis=0)]


def _f_mixers(ps, xs, cs):
    oa, st = _f_hgrn(ps, xs[:4], cs[:1])
    y, sv = _f_rwscan([], xs[4:], cs[1:])
    return oa + y, st + sv


def _f_rwpost(ps, xs, cs):
    ln_w, ln_b, r_k = ps
    y, r, k, v, g = xs
    e, _ = _headmat()
    inv_n = 1.0 / RW_N
    mu = _edot(y, e, "nn", 2) * inv_n
    yc = y - _edot(mu, e, "nt", 2)
    var = _edot(yc * yc, e, "nn", 2) * inv_n
    yn = yc * _edot(lax.rsqrt(var + GN_EPS), e, "nt", 2)
    yn = yn * ln_w + ln_b
    bonus = _edot(_edot(r * k * r_k, e, "nn", 2), e, "nt", 2) * v
    return [(yn + bonus) * g], []


def _f_merge(ps, xs, cs):
    ga, gb, ya, yb = xs
    return [_sigmoid(ga) * ya + _sigmoid(gb) * yb], []


def _f_post1(ps, xs, cs):
    x, mix = xs
    h1 = x + _rms(mix, ps[0])
    return [h1, _rms(h1, ps[1])], []


def _f_conv(ps, xs, cs):
    cw, cb = ps
    p1, p2 = cs
    w0, w1, w2 = _row(cw, 0), _row(cw, 1), _row(cw, 2)
    t = xs[0].shape[0]
    hc = []
    for i, x in enumerate(xs):
        sl = slice(i * DFF, (i + 1) * DFF)
        s1 = _shift_down(x, p1[:, sl])
        s2 = _shift_down(s1, p2[:, sl])
        hc.append(cb[:, sl] + w0[:, sl] * s2 + w1[:, sl] * s1 + w2[:, sl] * x)
    n1 = jnp.concatenate([_row(x, t - 1) for x in xs], axis=1)
    n2 = jnp.concatenate([_row(x, t - 2) for x in xs], axis=1)
    return [_silu(hc[0]) * hc[1]], [n1, n2]


class _Stage:
    def __init__(self, name, f, g, tm, par_per_g, in_pieces, in_offs, carry_shapes, out_pieces, out_dtypes):
        self.name, self.f, self.g, self.tm = name, f, g, tm
        self.par_per_g, self.in_pieces, self.in_offs = par_per_g, in_pieces, in_offs
        self.carry_shapes, self.out_pieces, self.out_dtypes = carry_shapes, out_pieces, out_dtypes


def _par_spec(arr, per_g, g):
    r, c = arr.shape
    if per_g:
        return pl.BlockSpec((r, c // g), lambda gi, ni: (0, gi))
    return pl.BlockSpec((r, c), lambda gi, ni: (0, 0))


def _row_spec(tm, width, off, n, rev):
    if rev:
        return pl.BlockSpec((tm, width), lambda gi, ni: (n - 1 - ni, off + gi))
    return pl.BlockSpec((tm, width), lambda gi, ni: (ni, off + gi))


def _carry_spec(shape, n, rev):
    if rev:
        return pl.BlockSpec((None, None) + shape, lambda gi, ni: (gi, n - 1 - ni, 0, 0))
    return pl.BlockSpec((None, None) + shape, lambda gi, ni: (gi, ni, 0, 0))


def _load_pieces(refs, pieces_list):
    out = []
    for ref, pieces in zip(refs, pieces_list):
        o = 0
        for w in pieces:
            out.append(ref[:, o:o + w].astype(F32))
            o += w
    return out


def _store_pieces(refs, pieces_list, vals):
    k = 0
    for ref, pieces in zip(refs, pieces_list):
        o = 0
        for w in pieces:
            ref[:, o:o + w] = vals[k].astype(ref.dtype)
            k += 1
            o += w


_ANY = pl.BlockSpec(memory_space=pl.ANY)


class _Exchange:
    def __init__(self, kind, arrs):
        self.kind, self.arrs, self.results = kind, list(arrs), None
        if kind == "scatter":
            self.out_shape = [jax.ShapeDtypeStruct((N_DEV - 1,) + a.shape[1:], a.dtype) for a in self.arrs]
        elif kind == "swap3":
            self.out_shape = [jax.ShapeDtypeStruct(a.shape, a.dtype) for a in self.arrs]
        else:
            self.out_shape = [jax.ShapeDtypeStruct((N_DEV,) + a.shape, a.dtype) for a in self.arrs]
        self.nsem = (3 if kind == "swap3" else N_DEV - 1) * len(self.arrs)

    def copies(self, in_refs, out_refs, ssem, rsem):
        x, y, c = lax.axis_index("x"), lax.axis_index("y"), lax.axis_index("c")
        me = 4 * x + 2 * y + c
        cps = []
        if self.kind == "swap3":
            targets = [(1 - x, y, c), (x, 1 - y, c), (1 - x, 1 - y, c)]
            return [pltpu.make_async_remote_copy(src_ref=i_ref.at[k], dst_ref=o_ref.at[k], send_sem=ssem.at[3 * a + k],
                                                 recv_sem=rsem.at[3 * a + k], device_id=targets[k],
                                                 device_id_type=MESH)
                    for a, (i_ref, o_ref) in enumerate(zip(in_refs, out_refs)) for k in range(3)]
        for a, (i_ref, o_ref) in enumerate(zip(in_refs, out_refs)):
            for j in range(1, N_DEV):
                px = 1 - x if j & 4 else x
                py = 1 - y if j & 2 else y
                pc = 1 - c if j & 1 else c
                if self.kind == "gather":
                    src, dst = i_ref, o_ref.at[me]
                else:
                    src, dst = i_ref.at[4 * px + 2 * py + pc], o_ref.at[j - 1]
                s = (N_DEV - 1) * a + j - 1
                cps.append(pltpu.make_async_remote_copy(src_ref=src, dst_ref=dst, send_sem=ssem.at[s],
                                                        recv_sem=rsem.at[s], device_id=(px, py, pc),
                                                        device_id_type=MESH))
        return cps

    def run(self, first, mid, last, in_refs, out_refs, ssem, rsem):
        if self.kind == "gather2":
            return self.run_two_level(first, mid, last, in_refs, out_refs, ssem, rsem)

        @pl.when(first)
        def _():
            for cp in self.copies(in_refs, out_refs, ssem, rsem):
                cp.start()

        @pl.when(last)
        def _():
            for cp in self.copies(in_refs, out_refs, ssem, rsem):
                cp.wait()

    def run_two_level(self, first, mid, last, in_refs, out_refs, ssem, rsem):
        x, y, c = lax.axis_index("x"), lax.axis_index("y"), lax.axis_index("c")
        me, sibling = (x, y, c), (x, y, 1 - c)
        chips = [(1 - x, y), (x, 1 - y), (1 - x, 1 - y)]
        arrs = range(len(in_refs))

        def copy(a, k, block, to, src=None):
            dst = out_refs[a].at[4 * block[0] + 2 * block[1] + block[2]]
            return pltpu.make_async_remote_copy(
                src_ref=dst if src is None else src, dst_ref=dst, send_sem=ssem.at[7 * a + k],
                recv_sem=rsem.at[7 * a + k], device_id=to, device_id_type=MESH)

        def firsts(a):
            return [copy(a, 0, me, sibling, src=in_refs[a])] + [
                copy(a, 1 + j, me, (*chip, c), src=in_refs[a]) for j, chip in enumerate(chips)]

        def passed(a):
            return [copy(a, 4 + j, (*chip, c), sibling) for j, chip in enumerate(chips)]

        @pl.when(first)
        def _():
            for a in arrs:
                for cp in firsts(a):
                    cp.start()

        @pl.when(mid)
        def _():
            for j, chip in enumerate(chips):
                for a in arrs:
                    copy(a, 1 + j, (*chip, c), me).wait_recv()
                    passed(a)[j].start()

        @pl.when(last)
        def _():
            for a in arrs:
                copy(a, 0, sibling, me).wait_recv()
                for j, chip in enumerate(chips):
                    copy(a, 4 + j, (*chip, 1 - c), me).wait_recv()
                for cp in firsts(a) + passed(a):
                    cp.wait_send()


def _hook_specs(hook):
    if hook is None:
        return [], [], [], []
    na = len(hook.arrs)
    sems = [pltpu.SemaphoreType.DMA((hook.nsem,)), pltpu.SemaphoreType.DMA((hook.nsem,))]
    return [_ANY] * na, [_ANY] * na, hook.out_shape, sems


def _stage_fwd(st, t, params, inputs, hook=None):
    g, tm = st.g, min(st.tm, t)
    n = t // tm
    npar, nin, ncar, nout = len(params), len(inputs), len(st.carry_shapes), len(st.out_pieces)
    h_in, h_out, h_shape, h_sems = _hook_specs(hook)
    nh = len(h_in)

    def body(*refs):
        p_refs = refs[:npar]
        x_refs = refs[npar:npar + nin]
        hi_refs = refs[npar + nin:npar + nin + nh]
        o = npar + nin + nh
        o_refs = refs[o:o + nout]
        s_refs = refs[o + nout:o + nout + ncar]
        ho_refs = refs[o + nout + ncar:o + nout + ncar + nh]
        c_scr = refs[o + nout + ncar + nh:o + nout + ncar + nh + ncar]
        gi, ni = pl.program_id(0), pl.program_id(1)
        if hook is not None:
            step = gi * n + ni
            hook.run(step == 0, step == (4 * g * n) // 5, step == g * n - 1, hi_refs, ho_refs, *refs[-2:])

        @pl.when(ni == 0)
        def _():
            for c in c_scr:
                c[...] = jnp.zeros(c.shape, F32)

        ps = [r[...].astype(F32) for r in p_refs]
        xs = _load_pieces(x_refs, st.in_pieces)
        cs = [c[...] for c in c_scr]
        for s, c in zip(s_refs, cs):
            s[...] = c
        outs, ncs = st.f(ps, xs, cs)
        _store_pieces(o_refs, st.out_pieces, outs)
        for c, v in zip(c_scr, ncs):
            c[...] = v

    in_specs = [_par_spec(p, pg, g) for p, pg in zip(params, st.par_per_g)]
    in_specs += [_row_spec(tm, sum(pc), off, n, False) for pc, off in zip(st.in_pieces, st.in_offs)]
    out_specs = [_row_spec(tm, sum(pc), 0, n, False) for pc in st.out_pieces]
    out_specs += [_carry_spec(s, n, False) for s in st.carry_shapes]
    out_shape = [jax.ShapeDtypeStruct((t, g * sum(pc)), dt) for pc, dt in zip(st.out_pieces, st.out_dtypes)]
    out_shape += [jax.ShapeDtypeStruct((g, n) + s, F32) for s in st.carry_shapes]
    res = pl.pallas_call(
        body, name=st.name + "_fwd", grid=(g, n), in_specs=in_specs + h_in, out_specs=out_specs + h_out,
        out_shape=out_shape + h_shape,
        scratch_shapes=[pltpu.VMEM(s, F32) for s in st.carry_shapes] + h_sems,
        compiler_params=_cparams(("arbitrary", "arbitrary")),
    )(*params, *inputs, *(hook.arrs if hook else []))
    if hook is not None:
        hook.results = list(res[nout + ncar:])
    return list(res[:nout]), list(res[nout:nout + ncar])


def _stage_bwd(st, t, params, inputs, saved, douts, dx_dtypes, hook=None):
    g, tm = st.g, min(st.tm, t)
    n = t // tm
    npar, nin, ncar = len(params), len(inputs), len(st.carry_shapes)
    flat_d = [d for ds in douts for d in ds]
    nd = len(flat_d)
    dx_idx = [i for i, dt in enumerate(dx_dtypes) if dt is not None]
    h_in, h_out, h_shape, h_sems = _hook_specs(hook)
    nh = len(h_in)

    def body(*refs):
        p_refs = refs[:npar]
        x_refs = refs[npar:npar + nin]
        s_refs = refs[npar + nin:npar + nin + ncar]
        d_refs = refs[npar + nin + ncar:npar + nin + ncar + nd]
        hi_refs = refs[npar + nin + ncar + nd:npar + nin + ncar + nd + nh]
        o = npar + nin + ncar + nd + nh
        dp_refs = refs[o:o + npar]
        dx_refs = refs[o + npar:o + npar + len(dx_idx)]
        ho_refs = refs[o + npar + len(dx_idx):o + npar + len(dx_idx) + nh]
        dc_scr = refs[o + npar + len(dx_idx) + nh:o + npar + len(dx_idx) + nh + ncar]
        gi, ni = pl.program_id(0), pl.program_id(1)
        if hook is not None:
            step = gi * n + ni
            hook.run(step == 0, step == (4 * g * n) // 5, step == g * n - 1, hi_refs, ho_refs, *refs[-2:])

        @pl.when(ni == 0)
        def _():
            for c in dc_scr:
                c[...] = jnp.zeros(c.shape, F32)

        ps = [r[...].astype(F32) for r in p_refs]
        xs = _load_pieces(x_refs, st.in_pieces)
        cs = [s[...] for s in s_refs]
        dys = []
        k = 0
        for ds, pieces in zip(douts, st.out_pieces):
            acc = _load_pieces([d_refs[k]], [pieces])
            for j in range(1, len(ds)):
                more = _load_pieces([d_refs[k + j]], [pieces])
                acc = [a + b for a, b in zip(acc, more)]
            dys += acc
            k += len(ds)
        _, vjp = jax.vjp(st.f, ps, xs, cs)
        dps, dxs, dcs = vjp((dys, [c[...] for c in dc_scr]))
        k = 0
        per_in = []
        for pieces in st.in_pieces:
            per_in.append(dxs[k:k + len(pieces)])
            k += len(pieces)
        for ref, i in zip(dx_refs, dx_idx):
            _store_pieces([ref], [st.in_pieces[i]], per_in[i])
        for c, v in zip(dc_scr, dcs):
            c[...] = v
        for ref, dp, pg in zip(dp_refs, dps, st.par_per_g):
            first = (ni == 0) if pg else ((ni == 0) & (gi == 0))

            @pl.when(first)
            def _():
                ref[...] = jnp.zeros(ref.shape, F32)

            ref[...] += dp

    in_specs = [_par_spec(p, pg, g) for p, pg in zip(params, st.par_per_g)]
    in_specs += [_row_spec(tm, sum(pc), off, n, True) for pc, off in zip(st.in_pieces, st.in_offs)]
    in_specs += [_carry_spec(s, n, True) for s in st.carry_shapes]
    for ds, pc in zip(douts, st.out_pieces):
        in_specs += [_row_spec(tm, sum(pc), 0, n, True) for _ in ds]
    out_specs = [_par_spec(p, pg, g) for p, pg in zip(params, st.par_per_g)]
    out_specs += [_row_spec(tm, sum(st.in_pieces[i]), 0, n, True) for i in dx_idx]
    out_shape = [jax.ShapeDtypeStruct(p.shape, F32) for p in params]
    out_shape += [jax.ShapeDtypeStruct((t, g * sum(st.in_pieces[i])), dx_dtypes[i]) for i in dx_idx]
    res = pl.pallas_call(
        body, name=st.name + "_bwd", grid=(g, n), in_specs=in_specs + h_in, out_specs=out_specs + h_out,
        out_shape=out_shape + h_shape,
        scratch_shapes=[pltpu.VMEM(s, F32) for s in st.carry_shapes] + h_sems,
        compiler_params=_cparams(("arbitrary", "arbitrary")),
    )(*params, *inputs, *saved, *flat_d, *(hook.arrs if hook else []))
    if hook is not None:
        hook.results = list(res[npar + len(dx_idx):])
    return list(res[:npar]), list(res[npar:npar + len(dx_idx)])


def _pick(n, cap):
    if n <= cap:
        return n
    best = LANES
    for k in range(1, n // LANES + 1):
        if (n // LANES) % k == 0 and k * LANES <= cap:
            best = k * LANES
    return best


def _mm(name, a, b, mode, out_dtype=F32, tm=1024, tn=512, b_outer=False, hook=None):
    m = a.shape[1] if mode == "tn" else a.shape[0]
    k = a.shape[0] if mode == "tn" else a.shape[1]
    n = b.shape[0] if mode == "nt" else b.shape[1]
    tm, tn = _pick(m, tm), _pick(n, tn)
    h_in, h_out, h_shape, h_sems = _hook_specs(hook)
    nh = len(h_in)
    if b_outer:
        grid = (n // tn, m // tm)
        ij = lambda p, q: (q, p)
    else:
        grid = (m // tm, n // tn)
        ij = lambda p, q: (p, q)

    def body(*refs):
        a_ref, b_ref, o_ref = refs[0], refs[1], refs[2 + nh]
        if hook is not None:
            step = pl.program_id(0) * grid[1] + pl.program_id(1)
            total = grid[0] * grid[1]
            hook.run(step == 0, step == (4 * total) // 5, step == total - 1, refs[2:2 + nh],
                     refs[3 + nh:3 + 2 * nh], *refs[-2:])
        o_ref[...] = _raw_dot(a_ref[...], b_ref[...], mode).astype(o_ref.dtype)

    if mode == "tn":
        a_spec = pl.BlockSpec((k, tm), lambda p, q: (0, ij(p, q)[0]))
    else:
        a_spec = pl.BlockSpec((tm, k), lambda p, q: (ij(p, q)[0], 0))
    if mode == "nt":
        b_spec = pl.BlockSpec((tn, k), lambda p, q: (ij(p, q)[1], 0))
    else:
        b_spec = pl.BlockSpec((k, tn), lambda p, q: (0, ij(p, q)[1]))
    res = pl.pallas_call(
        body, name=name, grid=grid, in_specs=[a_spec, b_spec] + h_in,
        out_specs=[pl.BlockSpec((tm, tn), lambda p, q: ij(p, q))] + h_out,
        out_shape=[jax.ShapeDtypeStruct((m, n), out_dtype)] + h_shape, scratch_shapes=h_sems,
        compiler_params=_cparams(("arbitrary", "arbitrary")),
    )(a, b, *(hook.arrs if hook else []))
    if hook is not None:
        hook.results = list(res[1:])
    return res[0]


def _loss_stage(t, g_post, h1, ff, tgt):
    tm = min(256, t)
    n = t // tm

    def body(g_ref, h_ref, f_ref, t_ref, loss_ref, dg_ref, dh_ref, df_ref):
        ni = pl.program_id(0)
        target = t_ref[...]

        def lossf(g, h1, ff):
            e = h1 + _rms(ff, g) - target
            return 0.5 * jnp.sum(jnp.mean(e * e, axis=-1))

        l, (dg, dh, df) = jax.value_and_grad(lossf, argnums=(0, 1, 2))(g_ref[...], h_ref[...], f_ref[...])

        @pl.when(ni == 0)
        def _():
            loss_ref[...] = jnp.zeros(loss_ref.shape, F32)
            dg_ref[...] = jnp.zeros(dg_ref.shape, F32)

        loss_ref[...] += jnp.full(loss_ref.shape, l, F32)
        dg_ref[...] += dg
        dh_ref[...] = dh
        df_ref[...] = df.astype(df_ref.dtype)

    row = pl.BlockSpec((tm, D), lambda ni: (ni, 0))
    one = pl.BlockSpec((1, D), lambda ni: (0, 0))
    return pl.pallas_call(
        body, name="loss_head", grid=(n,), in_specs=[one, row, row, row],
        out_specs=[pl.BlockSpec((1, LANES), lambda ni: (0, 0)), one, row, row],
        out_shape=[jax.ShapeDtypeStruct((1, LANES), F32), jax.ShapeDtypeStruct((1, D), F32),
                   jax.ShapeDtypeStruct((t, D), F32), jax.ShapeDtypeStruct((t, D), BF)],
        compiler_params=_cparams(("arbitrary",)),
    )(g_post, h1, ff, tgt)


_ANY = pl.BlockSpec(memory_space=pl.ANY)


def _all_gather(name, blks):
    na = len(blks)

    def body(*refs):
        x_refs, out_refs = refs[:na], refs[na:2 * na]
        send_sems, recv_sems, local_sems = refs[2 * na:]
        x, y, cc = lax.axis_index("x"), lax.axis_index("y"), lax.axis_index("c")
        me, sibling = (x, y, cc), (x, y, 1 - cc)
        chips = [(1 - x, y), (x, 1 - y), (1 - x, 1 - y)]

        def copy(a, k, block, to, src=None):
            dst = out_refs[a].at[4 * block[0] + 2 * block[1] + block[2]]
            return pltpu.make_async_remote_copy(
                src_ref=dst if src is None else src, dst_ref=dst, send_sem=send_sems.at[7 * a + k],
                recv_sem=recv_sems.at[7 * a + k], device_id=to, device_id_type=MESH)

        mine, first, passed = [], [], []
        for a in range(na):
            m = pltpu.make_async_copy(x_refs[a], out_refs[a].at[4 * x + 2 * y + cc], local_sems.at[a])
            m.start()
            mine.append(m)
            cps = [copy(a, 0, me, sibling, src=x_refs[a])]
            cps += [copy(a, 1 + j, me, (*chip, cc), src=x_refs[a]) for j, chip in enumerate(chips)]
            for cp in cps:
                cp.start()
            first += cps
        for j, chip in enumerate(chips):
            for a in range(na):
                copy(a, 1 + j, (*chip, cc), me).wait_recv()
                fw = copy(a, 4 + j, (*chip, cc), sibling)
                fw.start()
                passed.append(fw)
        for a in range(na):
            copy(a, 0, sibling, me).wait_recv()
            for j, chip in enumerate(chips):
                copy(a, 4 + j, (*chip, 1 - cc), me).wait_recv()
        for cp in first + passed:
            cp.wait_send()
        for m in mine:
            m.wait()

    res = pl.pallas_call(
        body, name=name, in_specs=[_ANY] * na, out_specs=[_ANY] * na,
        out_shape=[jax.ShapeDtypeStruct((N_DEV,) + b.shape, b.dtype) for b in blks],
        scratch_shapes=[pltpu.SemaphoreType.DMA((7 * na,)), pltpu.SemaphoreType.DMA((7 * na,)),
                        pltpu.SemaphoreType.DMA((na,))],
    )(*blks)
    return list(res)


def _reduce_pair(g8s):
    na = len(g8s)

    def body(*refs):
        g_refs, recv_refs = refs[:na], refs[na:2 * na]
        ssem, rsem = refs[2 * na:]
        x, y, cc = lax.axis_index("x"), lax.axis_index("y"), lax.axis_index("c")
        chips = [(x, y), (1 - x, y), (x, 1 - y), (1 - x, 1 - y)]
        sib = (x, y, 1 - cc)
        for a in range(na):
            for k, (cx, cy) in enumerate(chips):
                pltpu.make_async_remote_copy(
                    src_ref=g_refs[a].at[4 * cx + 2 * cy + 1 - cc], dst_ref=recv_refs[a].at[k],
                    send_sem=ssem.at[a], recv_sem=rsem.at[a], device_id=sib, device_id_type=MESH).start()
        for a in range(na):
            pltpu.make_async_remote_copy(src_ref=recv_refs[a], dst_ref=recv_refs[a], send_sem=ssem.at[a],
                                         recv_sem=rsem.at[a], device_id=sib, device_id_type=MESH).wait()

    res = pl.pallas_call(
        body, name="reduce_pair", in_specs=[_ANY] * na, out_specs=[_ANY] * na,
        out_shape=[jax.ShapeDtypeStruct((4,) + g.shape[1:], g.dtype) for g in g8s],
        scratch_shapes=[pltpu.SemaphoreType.DMA((na,)), pltpu.SemaphoreType.DMA((na,))],
    )(*g8s)
    return list(res)


def _pick_rows(r, c):
    if r * c * 4 <= TILE_BYTES or r % 16:
        return r
    best = 16
    for tr in range(16, r, 16):
        if r % tr == 0 and tr * c * 4 <= TILE_BYTES:
            best = tr
    return best


def _pair_sum(name, idx4, g8, recv4):
    _, r, c = g8.shape
    tr = _pick_rows(r, c)

    def body(idx_ref, a_ref, b_ref, o0_ref, o3_ref):
        k = pl.program_id(1)
        s = a_ref[...].astype(F32) + b_ref[...].astype(F32)

        @pl.when(k == 0)
        def _():
            o0_ref[...] = s

        @pl.when(k > 0)
        def _():
            o3_ref[...] = s.astype(BF)

    spec = pltpu.PrefetchScalarGridSpec(
        num_scalar_prefetch=1, grid=(r // tr, 4),
        in_specs=[pl.BlockSpec((None, tr, c), lambda i, k, idx: (idx[k], i, 0)),
                  pl.BlockSpec((None, tr, c), lambda i, k, idx: (k, i, 0))],
        out_specs=[pl.BlockSpec((tr, c), lambda i, k, idx: (i, 0)),
                   pl.BlockSpec((None, tr, c), lambda i, k, idx: (jnp.maximum(k - 1, 0), i, 0))])
    return pl.pallas_call(
        body, name=name, grid_spec=spec,
        out_shape=[jax.ShapeDtypeStruct((r, c), F32), jax.ShapeDtypeStruct((3, r, c), BF)],
        compiler_params=_cparams(("arbitrary", "arbitrary")),
    )(idx4, g8, recv4)


def _adamw(w, g, m, v):
    m = ADAM_B1 * m + (1.0 - ADAM_B1) * g
    v = ADAM_B2 * v + (1.0 - ADAM_B2) * jnp.square(g)
    m_hat = m / (1.0 - ADAM_B1 ** ADAM_STEP)
    v_hat = v / (1.0 - ADAM_B2 ** ADAM_STEP)
    delta = -ADAM_LR * (m_hat / (jnp.sqrt(v_hat) + ADAM_EPS) + ADAM_WD * w)
    return delta, m, v


def _adam_sharded(name, idx1, own, recv, w, m, v):
    r, c = w.shape
    tr = _pick_rows(r, c)
    nj = recv.shape[0]

    def body(idx_ref, p_ref, r_ref, w_ref, m_ref, v_ref, g_out, d_out, m_out, v_out):
        g = p_ref[...].astype(F32)
        for k in range(nj):
            g = g + r_ref[k].astype(F32)
        d, mn, vn = _adamw(w_ref[...], g, m_ref[...], v_ref[...])
        g_out[...] = g
        d_out[...] = d
        m_out[...] = mn
        v_out[...] = vn

    row = pl.BlockSpec((tr, c), lambda i, idx: (i, 0))
    spec = pltpu.PrefetchScalarGridSpec(
        num_scalar_prefetch=1, grid=(r // tr,),
        in_specs=[pl.BlockSpec((None, tr, c), lambda i, idx: (idx[0], i, 0)),
                  pl.BlockSpec((nj, tr, c), lambda i, idx: (0, i, 0)), row, row, row],
        out_specs=[row] * 4)
    return pl.pallas_call(
        body, name=name, grid_spec=spec, out_shape=[jax.ShapeDtypeStruct((r, c), F32)] * 4,
        compiler_params=_cparams(("arbitrary",)),
    )(idx1, own, recv, w, m, v)


def _repl_rows():
    rows, r = {}, 0
    for name, cols in REPL:
        rows[name] = r
        r += REPL_ROWS.get(name, 1) * ((cols + D - 1) // D)
    return rows


def _pack_replicated(grads):
    rows = _repl_rows()
    names = [n for n, _ in REPL]

    def body(*refs):
        o_ref = refs[-1]
        o_ref[...] = jnp.zeros(o_ref.shape, F32)
        for name, ref in zip(names, refs[:-1]):
            r0 = rows[name]
            nr, nc = ref.shape
            if nc <= D:
                o_ref[r0:r0 + nr, 0:nc] = ref[...]
            else:
                for j in range((nc + D - 1) // D):
                    lo, hi = j * D, min(nc, (j + 1) * D)
                    o_ref[r0 + j:r0 + j + 1, 0:hi - lo] = ref[:, lo:hi]

    return pl.pallas_call(body, name="pack_replicated", out_shape=jax.ShapeDtypeStruct((REPL_TOTAL, D), F32),
                          compiler_params=_cparams())(*[grads[n] for n in names])


def _adam_replicated(g8, ws, ms, vs):
    rows = _repl_rows()
    names = [n for n, _ in REPL]
    np_ = len(names)

    def body(*refs):
        g_ref = refs[0]
        w_refs, m_refs, v_refs = refs[1:1 + np_], refs[1 + np_:1 + 2 * np_], refs[1 + 2 * np_:1 + 3 * np_]
        outs = refs[1 + 3 * np_:1 + 7 * np_]
        scr = refs[-1]
        g = g_ref[0]
        for k in range(1, N_DEV):
            g = g + g_ref[k]
        scr[...] = g
        for i, name in enumerate(names):
            r0 = rows[name]
            nr, nc = w_refs[i].shape
            if nc <= D:
                gi = scr[r0:r0 + nr, 0:nc]
            else:
                parts = []
                for j in range((nc + D - 1) // D):
                    lo, hi = j * D, min(nc, (j + 1) * D)
                    parts.append(scr[r0 + j:r0 + j + 1, 0:hi - lo])
                gi = jnp.concatenate(parts, axis=1)
            d, mn, vn = _adamw(w_refs[i][...], gi, m_refs[i][...], v_refs[i][...])
            outs[i][...] = gi
            outs[np_ + i][...] = d
            outs[2 * np_ + i][...] = mn
            outs[3 * np_ + i][...] = vn

    shp = [jax.ShapeDtypeStruct(w.shape, F32) for w in ws]
    res = pl.pallas_call(body, name="adam_replicated", out_shape=shp * 4,
                         scratch_shapes=[pltpu.VMEM((REPL_TOTAL, D), F32)], compiler_params=_cparams(),
                         )(g8, *ws, *ms, *vs)
    return [dict(zip(names, res[k * np_:(k + 1) * np_])) for k in range(4)]


_WEIGHTS = ("attn_pre_norm", "w_in", "hgrn_lb", "hgrn_gnorm", "w_branch_a", "rwkv_mu", "rwkv_w0", "rwkv_w2",
            "rwkv_a0", "rwkv_a2", "rwkv_g2", "rwkv_k_k", "rwkv_k_a", "rwkv_r_k", "rwkv_ln_w", "rwkv_ln_b",
            "w_branch_b", "w_out", "attn_post_norm", "ffn_pre_norm", "w_up", "conv_w", "conv_b", "w_down",
            "ffn_post_norm")
_BIG = ("w_in", "w_up", "w_down", "w_branch_a", "w_branch_b", "w_out")


def _stages():
    one = [D]
    hw = HG_K * HG_PER_STEP
    rw = LANES * RW_PAIRS_PER_STEP
    return dict(
        pre1=_Stage("pre1", _f_pre1, 1, 256, [False], [one], [0], [], [one], [BF]),
        mixers=_Stage("mixers", _f_mixers, 1, RW_CHUNK, [False, False], [one] * 10, [0, 1, 2, 3] + [0] * 6,
                      [(hw, HG_K), (rw, LANES)], [one, one], [BF, F32]),
        rwpre=_Stage("rwkv_pre", _f_rwpre, 1, 128, [False] * 8, [[D], [D], [D], [LANES], [LANES]], [4, 5, 6, 56, 57],
                     [(1, RW_COLS)], [one] * 7, [F32] * 7),
        rwpost=_Stage("rwkv_post", _f_rwpost, 1, 128, [False] * 3, [one] * 5, [0] * 5, [], [one], [BF]),
        merge=_Stage("merge", _f_merge, 4, 512, [], [[256]] * 4, [29, 33, 0, 0], [], [[256]], [BF]),
        post1=_Stage("post1", _f_post1, 1, 256, [False, False], [one, one], [0, 0], [], [one, one], [F32, BF]),
        conv=_Stage("conv", _f_conv, 1, 128, [False, False], [[DFF], [DFF]], [0, 1], [(1, 2 * DFF), (1, 2 * DFF)],
                    [[DFF]], [BF]),
    )


def _cols_to_blocks(w, per):
    return w.reshape(w.shape[0], N_DEV, per).transpose(1, 0, 2)


def _blocks_to_cols(g):
    return g.transpose(1, 0, 2).reshape(g.shape[1], N_DEV * g.shape[2])


def kernel(x, attn_pre_norm, w_in, hgrn_lb, hgrn_gnorm, w_branch_a, rwkv_mu, rwkv_w0, rwkv_w2, rwkv_a0, rwkv_a2, rwkv_g2, rwkv_k_k, rwkv_k_a, rwkv_r_k, rwkv_ln_w, rwkv_ln_b, w_branch_b, w_out, attn_post_norm, ffn_pre_norm, w_up, conv_w, conv_b, w_down, ffn_post_norm, loss_target, m_attn_pre_norm, m_w_in, m_hgrn_lb, m_hgrn_gnorm, m_w_branch_a, m_rwkv_mu, m_rwkv_w0, m_rwkv_w2, m_rwkv_a0, m_rwkv_a2, m_rwkv_g2, m_rwkv_k_k, m_rwkv_k_a, m_rwkv_r_k, m_rwkv_ln_w, m_rwkv_ln_b, m_w_branch_b, m_w_out, m_attn_post_norm, m_ffn_pre_norm, m_w_up, m_conv_w, m_conv_b, m_w_down, m_ffn_post_norm, v_attn_pre_norm, v_w_in, v_hgrn_lb, v_hgrn_gnorm, v_w_branch_a, v_rwkv_mu, v_rwkv_w0, v_rwkv_w2, v_rwkv_a0, v_rwkv_a2, v_rwkv_g2, v_rwkv_k_k, v_rwkv_k_a, v_rwkv_r_k, v_rwkv_ln_w, v_rwkv_ln_b, v_w_branch_b, v_w_out, v_attn_post_norm, v_ffn_pre_norm, v_w_up, v_conv_w, v_conv_b, v_w_down, v_ffn_post_norm):
    w = dict(attn_pre_norm=attn_pre_norm, w_in=w_in, hgrn_lb=hgrn_lb, hgrn_gnorm=hgrn_gnorm, w_branch_a=w_branch_a, rwkv_mu=rwkv_mu, rwkv_w0=rwkv_w0, rwkv_w2=rwkv_w2, rwkv_a0=rwkv_a0, rwkv_a2=rwkv_a2, rwkv_g2=rwkv_g2, rwkv_k_k=rwkv_k_k, rwkv_k_a=rwkv_k_a, rwkv_r_k=rwkv_r_k, rwkv_ln_w=rwkv_ln_w, rwkv_ln_b=rwkv_ln_b, w_branch_b=w_branch_b, w_out=w_out, attn_post_norm=attn_post_norm, ffn_pre_norm=ffn_pre_norm, w_up=w_up, conv_w=conv_w, conv_b=conv_b, w_down=w_down, ffn_post_norm=ffn_post_norm)
    mo = dict(attn_pre_norm=m_attn_pre_norm, w_in=m_w_in, hgrn_lb=m_hgrn_lb, hgrn_gnorm=m_hgrn_gnorm, w_branch_a=m_w_branch_a, rwkv_mu=m_rwkv_mu, rwkv_w0=m_rwkv_w0, rwkv_w2=m_rwkv_w2, rwkv_a0=m_rwkv_a0, rwkv_a2=m_rwkv_a2, rwkv_g2=m_rwkv_g2, rwkv_k_k=m_rwkv_k_k, rwkv_k_a=m_rwkv_k_a, rwkv_r_k=m_rwkv_r_k, rwkv_ln_w=m_rwkv_ln_w, rwkv_ln_b=m_rwkv_ln_b, w_branch_b=m_w_branch_b, w_out=m_w_out, attn_post_norm=m_attn_post_norm, ffn_pre_norm=m_ffn_pre_norm, w_up=m_w_up, conv_w=m_conv_w, conv_b=m_conv_b, w_down=m_w_down, ffn_post_norm=m_ffn_post_norm)
    vo = dict(attn_pre_norm=v_attn_pre_norm, w_in=v_w_in, hgrn_lb=v_hgrn_lb, hgrn_gnorm=v_hgrn_gnorm, w_branch_a=v_w_branch_a, rwkv_mu=v_rwkv_mu, rwkv_w0=v_rwkv_w0, rwkv_w2=v_rwkv_w2, rwkv_a0=v_rwkv_a0, rwkv_a2=v_rwkv_a2, rwkv_g2=v_rwkv_g2, rwkv_k_k=v_rwkv_k_k, rwkv_k_a=v_rwkv_k_a, rwkv_r_k=v_rwkv_r_k, rwkv_ln_w=v_rwkv_ln_w, rwkv_ln_b=v_rwkv_ln_b, w_branch_b=v_w_branch_b, w_out=v_w_out, attn_post_norm=v_attn_post_norm, ffn_pre_norm=v_ffn_pre_norm, w_up=v_w_up, conv_w=v_conv_w, conv_b=v_conv_b, w_down=v_w_down, ffn_post_norm=v_ffn_post_norm)

    t = x.shape[1]
    x2 = x.reshape(t, D)
    tgt = loss_target.reshape(t, D)
    st = _stages()

    me = 4 * lax.axis_index("x") + 2 * lax.axis_index("y") + lax.axis_index("c")
    small = jnp.concatenate([rwkv_w2[0], rwkv_a2[0], rwkv_g2[0]], axis=0).astype(BF)
    g_in, g_small = _all_gather("gather_weights", [w_in[0].astype(BF), small])
    fw_in = _blocks_to_cols(g_in)
    z64 = jnp.zeros((64, D), BF)
    w2p = jnp.concatenate([_blocks_to_cols(g_small[:, 0:64]), z64], axis=0)
    a2p = jnp.concatenate([z64, _blocks_to_cols(g_small[:, 64:128])], axis=0)
    g2f = _blocks_to_cols(g_small[:, 128:256])
    conv_bits = lax.bitcast_convert_type(conv_w[0], BF).reshape(3, 2 * 704)
    late = [w[k][0].astype(BF) for k in _BIG[1:]] + [conv_bits]
    late_gather = _Exchange("gather2", late)
    r_k = rwkv_r_k.reshape(1, D)

    (xn,), _ = _stage_fwd(st["pre1"], t, [attn_pre_norm], [x2])
    z = _mm("in_proj", xn, fw_in, "nn", F32, tm=256, tn=4736, b_outer=True)
    rwpre_par = [rwkv_mu, rwkv_w0, w2p, rwkv_a0, a2p, g2f, rwkv_k_k, rwkv_k_a]
    rw_in, rwpre_saved = _stage_fwd(st["rwpre"], t, rwpre_par, [z] * 5)
    r_, lw_, k_, v_, av_, bv_, g_ = rw_in
    hg_par = [hgrn_lb, hgrn_gnorm]
    mix_in = [z] * 4 + [r_, lw_, k_, v_, av_, bv_]
    (o_a, y_), mix_saved = _stage_fwd(st["mixers"], t, hg_par, mix_in, hook=late_gather)
    gl = [lax.dynamic_update_slice(g, own[None], (me, 0, 0)) for g, own in zip(late_gather.results, late)]
    fw_up = _blocks_to_cols(gl[0])
    fw_down = gl[1].reshape(DFF, D)
    fw_a, fw_b, fw_out = (g.reshape(D, D) for g in gl[2:5])
    conv_full = _blocks_to_cols(lax.bitcast_convert_type(gl[5].reshape(N_DEV, 3, 704, 2), F32))
    rwpost_par = [rwkv_ln_w, rwkv_ln_b, r_k]
    (o_b,), _ = _stage_fwd(st["rwpost"], t, rwpost_par, [y_, r_, k_, v_, g_])
    y_a = _mm("branch_a", o_a, fw_a, "nn")
    y_b = _mm("branch_b", o_b, fw_b, "nn")
    (merged,), _ = _stage_fwd(st["merge"], t, [], [z, z, y_a, y_b])
    mix = _mm("out_proj", merged, fw_out, "nn")
    (h1, xn2), _ = _stage_fwd(st["post1"], t, [attn_post_norm, ffn_pre_norm], [x2, mix])
    hu = _mm("up_proj", xn2, fw_up, "nn", F32, tm=512, tn=1408)
    conv_par = [conv_full, conv_b]
    (act,), conv_saved = _stage_fwd(st["conv"], t, conv_par, [hu, hu])
    ff = _mm("down_proj", act, fw_down, "nn")

    loss_acc, d_ffn_post, dh1, dff = _loss_stage(t, ffn_post_norm, h1, ff, tgt)
    dact = _mm("d_act", dff, fw_down, "nt", F32, tm=512, tn=1408)
    dw_down = _mm("dw_down", act, dff, "tn", BF, tm=1408, tn=512)
    (dcw, dcb), (dhu_g, dhu_v) = _stage_bwd(st["conv"], t, conv_par, [hu, hu], conv_saved, [[dact]], [BF, BF])
    dhu = jnp.concatenate([dhu_g, dhu_v], axis=1)
    dxn2 = _mm("d_xn2", dhu, fw_up, "nt", F32, tm=256, tn=512)
    dw_up = _mm("dw_up", xn2, dhu, "tn", BF, tm=512, tn=1408)
    (d_post, d_pre2), (dx_a, dmix) = _stage_bwd(st["post1"], t, [attn_post_norm, ffn_pre_norm], [x2, mix], [],
                                                 [[dh1], [dxn2]], [F32, BF])
    dmerged = _mm("d_merged", dmix, fw_out, "nt")
    dw_out = _mm("dw_out", merged, dmix, "tn", BF)
    _, (dga, dgb, dy_a, dy_b) = _stage_bwd(st["merge"], t, [], [z, z, y_a, y_b], [], [[dmerged]], [BF, BF, BF, BF])
    do_a = _mm("d_oa", dy_a, fw_a, "nt")
    dw_a = _mm("dw_a", o_a, dy_a, "tn", BF)
    do_b = _mm("d_ob", dy_b, fw_b, "nt")
    dw_b = _mm("dw_b", o_b, dy_b, "tn", BF)
    (d_lnw, d_lnb, d_rk), (dy_, dr1, dk1, dv1, dg_) = _stage_bwd(
        st["rwpost"], t, rwpost_par, [y_, r_, k_, v_, g_], [], [[do_b]], [F32] * 5)
    early = [_cols_to_blocks(dw_up, 704), dw_down.reshape(N_DEV, 352, D), dw_a.reshape(N_DEV, 128, D),
             dw_b.reshape(N_DEV, 128, D), dw_out.reshape(N_DEV, 128, D), _cols_to_blocks(dcw.astype(BF), 704)]
    early_scatter = _Exchange("scatter", early)
    (d_lb, d_gn), mix_dx = _stage_bwd(st["mixers"], t, hg_par, mix_in, mix_saved, [[do_a], [dy_]],
                                      [BF] * 4 + [F32] * 6, hook=early_scatter)
    dz_h = mix_dx[:4]
    dr2, dlw, dk2, dv2, dav, dbv = mix_dx[4:]
    rwpre_dp, dz_r = _stage_bwd(
        st["rwpre"], t, rwpre_par, [z] * 5, rwpre_saved,
        [[dr1, dr2], [dlw], [dk1, dk2], [dv1, dv2], [dav], [dbv], [dg_]], [BF] * 5)
    d_mu, d_w0, d_w2p, d_a0, d_a2p, d_g2, d_kk, d_ka = rwpre_dp
    dz = jnp.concatenate(dz_h + dz_r + [dga, dgb], axis=1)
    dw_in = _mm("dw_in", xn, dz, "tn", BF, tm=1024, tn=256)

    ax, ay, ac = lax.axis_index("x"), lax.axis_index("y"), lax.axis_index("c")
    idx4 = jnp.stack([4 * cx + 2 * cy + ac for cx, cy in ((ax, ay), (1 - ax, ay), (ax, 1 - ay), (1 - ax, 1 - ay))])
    idx4 = idx4.astype(jnp.int32)
    idx_me, idx_0 = idx4[0:1], jnp.zeros((1,), jnp.int32)
    d_small = jnp.concatenate([d_w2p[:64], d_a2p[64:], d_g2], axis=0).astype(BF)
    g8s = [_cols_to_blocks(dw_in, 1184), _cols_to_blocks(d_small, LANES)]
    recv4s = _reduce_pair(g8s)
    sums = [_pair_sum("pair_sum_" + n, idx4, g, r) for n, g, r in zip(("w_in", "small"), g8s, recv4s)]
    chip_swap = _Exchange("swap3", [s[1] for s in sums])
    dxn = _mm("d_xn", dz, fw_in, "nt", F32, tm=512, tn=512, b_outer=True, hook=chip_swap)
    recv3s = chip_swap.results
    (d_pre1,), (dx_b,) = _stage_bwd(st["pre1"], t, [attn_pre_norm], [x2], [], [[dxn]], [F32])
    grad_x = (dx_a + dx_b).reshape(x.shape)
    loss = lax.psum(loss_acc[0, 0], ("x", "y", "c"))

    def small_of(src):
        return jnp.concatenate([src["rwkv_w2"][0], src["rwkv_a2"][0], src["rwkv_g2"][0]], axis=0)

    sh_out = [dict() for _ in range(4)]
    res = _adam_sharded("adam_w_in", idx_0, sums[0][0][None], recv3s[0], *[src["w_in"][0] for src in (w, mo, vo)])
    res_s = _adam_sharded("adam_small", idx_0, sums[1][0][None], recv3s[1], *[small_of(src) for src in (w, mo, vo)])
    for kind in range(4):
        sh_out[kind]["w_in"] = res[kind][None]
        sh_out[kind]["rwkv_w2"] = res_s[kind][0:64][None]
        sh_out[kind]["rwkv_a2"] = res_s[kind][64:128][None]
        sh_out[kind]["rwkv_g2"] = res_s[kind][128:256][None]
    for n, own, recv in zip(_BIG[1:] + ("conv_w",), early, early_scatter.results):
        res = _adam_sharded("adam_" + n, idx_me, own, recv, *[src[n][0] for src in (w, mo, vo)])
        for kind in range(4):
            sh_out[kind][n] = res[kind][None]

    rg = dict(attn_pre_norm=d_pre1, hgrn_lb=d_lb, hgrn_gnorm=d_gn, rwkv_mu=d_mu, rwkv_w0=d_w0, rwkv_a0=d_a0,
              rwkv_k_k=d_kk, rwkv_k_a=d_ka, rwkv_r_k=d_rk, rwkv_ln_w=d_lnw, rwkv_ln_b=d_lnb, attn_post_norm=d_post,
              ffn_pre_norm=d_pre2, conv_b=dcb, ffn_post_norm=d_ffn_post)
    (g8,) = _all_gather("gather_small_grads", [_pack_replicated(rg)])
    rnames = [n for n, _ in REPL]
    flat = lambda src: [src[n].reshape(1, D) if n == "rwkv_r_k" else src[n] for n in rnames]
    rp_out = _adam_replicated(g8, flat(w), flat(mo), flat(vo))
    for kind in range(4):
        rp_out[kind]["rwkv_r_k"] = rp_out[kind]["rwkv_r_k"].reshape(rwkv_r_k.shape)

    outs = [loss, grad_x]
    for kind in range(4):
        for name in _WEIGHTS:
            outs.append(sh_out[kind][name] if name in sh_out[kind] else rp_out[kind][name])
    return tuple(outs)
```

```python
import functools

import jax
import jax.numpy as jnp
from jax import lax
from jax.experimental import pallas as pl
from jax.experimental.pallas import tpu as pltpu

F32 = jnp.float32
BF = jnp.bfloat16
MESH = pl.DeviceIdType.MESH

D = 1024
HG_HEADS = 8
HG_K = 128
HG_CHUNK = 32
HG_SCALE = HG_K ** -0.5
HG_PER_STEP = 8
RW_HEADS = 16
RW_N = 64
RW_CHUNK = 64
RW_PAIRS_PER_STEP = 8
DFF = 2816
IN_COLS = 9472
RW_COLS = 3328
EPS = 1e-6
GN_EPS = 1e-5 * RW_N
ADAM_LR = 0.001
ADAM_B1 = 0.9
ADAM_B2 = 0.999
ADAM_EPS = 1e-08
ADAM_WD = 0.01
ADAM_STEP = 10
N_DEV = 8
LANES = 128
VMEM_LIMIT = 56 * 1024 * 1024
TILE_BYTES = 1280 * 1024

REPL = (("attn_pre_norm", 1024), ("hgrn_lb", 1024), ("hgrn_gnorm", 1024), ("rwkv_mu", 3328), ("rwkv_w0", 1024),
        ("rwkv_a0", 1024), ("rwkv_k_k", 1024), ("rwkv_k_a", 1024), ("rwkv_r_k", 1024), ("rwkv_ln_w", 1024),
        ("rwkv_ln_b", 1024), ("attn_post_norm", 1024), ("ffn_pre_norm", 1024), ("conv_b", 5632), ("ffn_post_norm", 1024))
REPL_ROWS = {"hgrn_lb": 2}
REPL_TOTAL = 32


def _cparams(sem=None, **kw):
    return pltpu.CompilerParams(dimension_semantics=sem, vmem_limit_bytes=VMEM_LIMIT, **kw)


_DN = {"nn": ((1,), (0,)), "nt": ((1,), (1,)), "tn": ((0,), (0,))}


def _raw_dot(a, b, mode):
    return lax.dot_general(a.astype(BF), b.astype(BF), (_DN[mode], ((), ())), preferred_element_type=F32)


@functools.partial(jax.custom_vjp, nondiff_argnums=(2,))
def _dot(a, b, mode):
    return _raw_dot(a, b, mode)


def _dot_fwd(a, b, mode):
    return _raw_dot(a, b, mode), (a, b)


def _dot_bwd(mode, res, g):
    a, b = res
    if mode == "nn":
        return _dot(g, b, "nt"), _dot(a, g, "tn")
    if mode == "nt":
        return _dot(g, b, "nn"), _dot(g, a, "tn")
    return _dot(b, g, "nt"), _dot(a, g, "nn")


_dot.defvjp(_dot_fwd, _dot_bwd)


def _bf_pieces(x, n):
    out, r = [], x
    for i in range(n):
        p = r.astype(BF)
        out.append(p)
        if i + 1 < n:
            r = r - p.astype(F32)
    return out


def _raw_split_dot(x, e, mode, n, x_left):
    eb = e.astype(BF)
    acc = None
    for p in _bf_pieces(x, n):
        ops = (p, eb) if x_left else (eb, p)
        t = lax.dot_general(*ops, (_DN[mode], ((), ())), preferred_element_type=F32)
        acc = t if acc is None else acc + t
    return acc


@functools.partial(jax.custom_vjp, nondiff_argnums=(2, 3))
def _edot(x, e, mode, n):
    return _raw_split_dot(x, e, mode, n, True)


def _edot_fwd(x, e, mode, n):
    return _raw_split_dot(x, e, mode, n, True), e


def _edot_bwd(mode, n, e, g):
    return _raw_split_dot(g, e, "nt" if mode == "nn" else "nn", n, True), jnp.zeros_like(e)


_edot.defvjp(_edot_fwd, _edot_bwd)


@functools.partial(jax.custom_vjp, nondiff_argnums=(2,))
def _tdot(tri, x, n):
    return _raw_split_dot(x, tri, "nn", n, False)


def _tdot_fwd(tri, x, n):
    return _raw_split_dot(x, tri, "nn", n, False), tri


def _tdot_bwd(n, tri, g):
    return jnp.zeros_like(tri), _raw_split_dot(g, tri, "tn", n, False)


_tdot.defvjp(_tdot_fwd, _tdot_bwd)


def _row(x, i):
    r = lax.broadcasted_iota(jnp.int32, x.shape, 0)
    return jnp.sum(jnp.where(r == i, x, 0.0), axis=0, keepdims=True)


def _shift_down(x, prev):
    t = x.shape[0]

    @jax.custom_vjp
    def sh(x, prev):
        r = lax.broadcasted_iota(jnp.int32, x.shape, 0)
        return jnp.where(r == 0, prev, pltpu.roll(x, 1, 0))

    def fwd(x, prev):
        return sh(x, prev), None

    def bwd(_, g):
        r = lax.broadcasted_iota(jnp.int32, g.shape, 0)
        dx = jnp.where(r == t - 1, 0.0, pltpu.roll(g, t - 1, 0))
        return dx, jnp.sum(jnp.where(r == 0, g, 0.0), axis=0, keepdims=True)

    sh.defvjp(fwd, bwd)
    return sh(x, prev)


def _sigmoid(x):
    return jax.nn.sigmoid(x)


def _silu(x):
    return x * jax.nn.sigmoid(x)


def _softplus(x):
    return jnp.maximum(x, 0.0) + jnp.log(1.0 + jnp.exp(-jnp.abs(x)))


def _rms(x, g):
    return (x * lax.rsqrt(jnp.mean(x * x, axis=-1, keepdims=True) + EPS)) * g


def _headmat():
    j = lax.broadcasted_iota(jnp.int32, (D, LANES), 0)
    h = lax.broadcasted_iota(jnp.int32, (D, LANES), 1)
    e = jnp.where(lax.shift_right_logical(j, 6) == h, 1.0, 0.0).astype(F32)
    pad = jnp.where(lax.broadcasted_iota(jnp.int32, (1, LANES), 1) >= RW_HEADS, 1.0, 0.0).astype(F32)
    return e, pad


def _tril(c):
    r = lax.broadcasted_iota(jnp.int32, (c, c), 0)
    cc = lax.broadcasted_iota(jnp.int32, (c, c), 1)
    return cc <= r


def _f_pre1(ps, xs, cs):
    return [_rms(xs[0], ps[0])], []


def _f_hgrn(ps, xs, cs):
    lbraw, gn = ps
    hq, hf, hi, hg = xs
    hd = range(HG_PER_STEP)
    st = [cs[0][p * HG_K:(p + 1) * HG_K] for p in hd]
    l0, l1 = _row(lbraw, 0), _row(lbraw, 1)
    m = jnp.maximum(l0, l1)
    e0, e1 = jnp.exp(l0 - m), jnp.exp(l1 - m)
    lb = e0 / (e0 + e1)
    q = _silu(hq) * HG_SCALE
    f = lb + (1.0 - lb) * _sigmoid(hf)
    kh = 1.0 - f
    gl = jnp.log(f)
    c = HG_CHUNK
    low = _tril(c)
    tri = jnp.where(low, 1.0, 0.0).astype(F32)
    outs = []
    for i in range(hq.shape[0] // c):
        rows = slice(i * c, (i + 1) * c)
        b = _tdot(tri, gl[rows], 3)
        bref = _row(b, c // 2 - 1)
        blast = _row(b, c - 1)
        qi = q[rows] * jnp.exp(b - bref)
        ki = kh[rows] * jnp.exp(bref - b)
        qd = q[rows] * jnp.exp(b)
        kd = kh[rows] * jnp.exp(blast - b)
        dec = jnp.exp(blast)
        sl = [slice(p * HG_K, (p + 1) * HG_K) for p in hd]
        sc = [jnp.where(low, _dot(qi[:, sl[p]], ki[:, sl[p]], "nt"), 0.0) for p in hd]
        o = [_dot(sc[p], hi[rows, sl[p]], "nn") + _dot(qd[:, sl[p]], st[p], "nt") for p in hd]
        u = [_dot(hi[rows, sl[p]], kd[:, sl[p]], "tn") for p in hd]
        st = [dec[:, sl[p]] * st[p] + u[p] for p in hd]
        outs.append(jnp.concatenate(o, axis=1) if len(o) > 1 else o[0])
    o = outs[0] if len(outs) == 1 else jnp.concatenate(outs, axis=0)
    on = []
    for p in hd:
        op = o[:, p * HG_K:(p + 1) * HG_K]
        on.append(op * lax.rsqrt(jnp.mean(op * op, axis=-1, keepdims=True) + EPS))
    o = jnp.concatenate(on, axis=1) if len(on) > 1 else on[0]
    o = o * gn
    return [o * _silu(hg)], [jnp.concatenate(st, axis=0) if len(st) > 1 else st[0]]


_RW_OFFS = (0, 1024, 2048, 3072, 3200, 3328)


def _f_rwpre(ps, xs, cs):
    mu, w0, w2p, a0, a2p, g2, k_k, k_a = ps
    (prev,) = cs
    t = xs[0].shape[0]
    zs = []
    for i, z in enumerate(xs):
        lo, hi = _RW_OFFS[i], _RW_OFFS[i + 1]
        zs.append(z + mu[:, lo:hi] * (_shift_down(z, prev[:, lo:hi]) - z))
    rr, kr, vr, wa, gz = zs
    w_log = -_softplus(-(w0 + _dot(jnp.tanh(wa), w2p, "nn"))) - 0.5
    lw = -jnp.exp(w_log)
    a = _sigmoid(a0 + _dot(wa, a2p, "nn"))
    g = _dot(_sigmoid(gz), g2, "nn")
    e, pad = _headmat()
    kkr = kr * k_k
    nrm = jnp.sqrt(_edot(kkr * kkr, e, "nn", 2) + pad)
    kk = kkr / _edot(jnp.maximum(nrm, 1e-12), e, "nt", 2)
    k2 = kr * (1.0 + (a - 1.0) * k_a)
    newprev = jnp.concatenate([_row(z, t - 1) for z in xs], axis=1)
    return [rr, lw, k2, vr, -kk, kk * a, g], [newprev]


def _f_rwscan(ps, xs, cs):
    npair = RW_PAIRS_PER_STEP
    pr = range(npair)
    r, lw, k, v, av, bv = [[x[:, p * LANES:(p + 1) * LANES] for p in pr] for x in xs]
    sv = [cs[0][p * LANES:(p + 1) * LANES] for p in pr]
    c = RW_CHUNK
    n = 2 * c
    tri = jnp.where(_tril(c), 1.0, 0.0).astype(F32)
    cl = [_tdot(tri, lw[p], 3) for p in pr]
    cl_last = [_row(cl[p], c - 1) for p in pr]
    lane = lax.broadcasted_iota(jnp.int32, (c, LANES), 1)
    h0 = lane < RW_N

    def stack(x):
        return jnp.concatenate([jnp.where(h0, x, 0.0), jnp.where(h0, 0.0, x)], axis=0)

    am = [stack(av[p] * jnp.exp(cl[p] - lw[p])) for p in pr]
    bm = [stack(bv[p] * jnp.exp(-cl[p])) for p in pr]
    km = [stack(k[p] * jnp.exp(-cl[p])) for p in pr]
    rm = [stack(r[p] * jnp.exp(cl[p])) for p in pr]
    vm = [stack(v[p]) for p in pr]
    rn = lax.broadcasted_iota(jnp.int32, (n, n), 0)
    cn = lax.broadcasted_iota(jnp.int32, (n, n), 1)
    blk = (rn >= c) == (cn >= c)
    strict = blk & (cn < rn)
    incl = blk & (cn <= rn)
    lab = [jnp.where(strict, _dot(am[p], bm[p], "nt"), 0.0) for p in pr]
    lak = [jnp.where(strict, _dot(am[p], km[p], "nt"), 0.0) for p in pr]
    wrb = [jnp.where(incl, _dot(rm[p], bm[p], "nt"), 0.0) for p in pr]
    wrk = [jnp.where(incl, _dot(rm[p], km[p], "nt"), 0.0) for p in pr]
    eye = jnp.where(rn == cn, 1.0, 0.0).astype(F32)
    tinv = [eye + lab[p] for p in pr]
    pw = lab
    for _ in range(5):
        pw = [_dot(pw[p], pw[p], "nn") for p in pr]
        tinv = [tinv[p] + _dot(tinv[p], pw[p], "nn") for p in pr]
    rhs = [_dot(am[p], sv[p], "nt") + _dot(lak[p], vm[p], "nn") for p in pr]
    um = [_dot(tinv[p], rhs[p], "nn") for p in pr]
    ym = [_dot(rm[p], sv[p], "nt") + _dot(wrb[p], um[p], "nn") + _dot(wrk[p], vm[p], "nn") for p in pr]
    sn = [(sv[p] + _dot(um[p], bm[p], "tn") + _dot(vm[p], km[p], "tn")) * jnp.exp(cl_last[p]) for p in pr]
    ys = [ym[p][:c] + ym[p][c:] for p in pr]
    return [jnp.concatenate(ys, axis=1)], [jnp.concatenate(sn, axis=0)]


def _f_mixers(ps, xs, cs):
    oa, st = _f_hgrn(ps[:2], xs[:4], cs[:1])
    (r, lw, k, v, av, bv, g), prev = _f_rwpre(ps[2:10], xs[4:], cs[1:2])
    y, sv = _f_rwscan([], [r, lw, k, v, av, bv], cs[2:])
    ob, _ = _f_rwpost(ps[10:], y + [r, k, v, g], [])
    return oa + ob, st + prev + sv


def _f_rwpost(ps, xs, cs):
    ln_w, ln_b, r_k = ps
    y, r, k, v, g = xs
    e, _ = _headmat()
    inv_n = 1.0 / RW_N
    mu = _edot(y, e, "nn", 2) * inv_n
    yc = y - _edot(mu, e, "nt", 2)
    var = _edot(yc * yc, e, "nn", 2) * inv_n
    yn = yc * _edot(lax.rsqrt(var + GN_EPS), e, "nt", 2)
    yn = yn * ln_w + ln_b
    bonus = _edot(_edot(r * k * r_k, e, "nn", 2), e, "nt", 2) * v
    return [(yn + bonus) * g], []


def _f_merge(ps, xs, cs):
    ga, gb, ya, yb = xs
    return [_sigmoid(ga) * ya + _sigmoid(gb) * yb], []


def _f_post1(ps, xs, cs):
    x, mix = xs
    h1 = x + _rms(mix, ps[0])
    return [h1, _rms(h1, ps[1])], []


def _f_conv(ps, xs, cs):
    cw, cb = ps
    p1, p2 = cs
    w0, w1, w2 = _row(cw, 0), _row(cw, 1), _row(cw, 2)
    t = xs[0].shape[0]
    hc = []
    for i, x in enumerate(xs):
        sl = slice(i * DFF, (i + 1) * DFF)
        s1 = _shift_down(x, p1[:, sl])
        s2 = _shift_down(s1, p2[:, sl])
        hc.append(cb[:, sl] + w0[:, sl] * s2 + w1[:, sl] * s1 + w2[:, sl] * x)
    n1 = jnp.concatenate([_row(x, t - 1) for x in xs], axis=1)
    n2 = jnp.concatenate([_row(x, t - 2) for x in xs], axis=1)
    return [_silu(hc[0]) * hc[1]], [n1, n2]


class _Stage:
    def __init__(self, name, f, g, tm, par_per_g, in_pieces, in_offs, carry_shapes, out_pieces, out_dtypes):
        self.name, self.f, self.g, self.tm = name, f, g, tm
        self.par_per_g, self.in_pieces, self.in_offs = par_per_g, in_pieces, in_offs
        self.carry_shapes, self.out_pieces, self.out_dtypes = carry_shapes, out_pieces, out_dtypes


def _par_spec(arr, per_g, g):
    r, c = arr.shape
    if per_g:
        return pl.BlockSpec((r, c // g), lambda gi, ni: (0, gi))
    return pl.BlockSpec((r, c), lambda gi, ni: (0, 0))


def _row_spec(tm, width, off, n, rev):
    if rev:
        return pl.BlockSpec((tm, width), lambda gi, ni: (n - 1 - ni, off + gi))
    return pl.BlockSpec((tm, width), lambda gi, ni: (ni, off + gi))


def _carry_spec(shape, n, rev):
    if rev:
        return pl.BlockSpec((None, None) + shape, lambda gi, ni: (gi, n - 1 - ni, 0, 0))
    return pl.BlockSpec((None, None) + shape, lambda gi, ni: (gi, ni, 0, 0))


def _load_pieces(refs, pieces_list):
    out = []
    for ref, pieces in zip(refs, pieces_list):
        o = 0
        for w in pieces:
            out.append(ref[:, o:o + w].astype(F32))
            o += w
    return out


def _store_pieces(refs, pieces_list, vals):
    k = 0
    for ref, pieces in zip(refs, pieces_list):
        o = 0
        for w in pieces:
            ref[:, o:o + w] = vals[k].astype(ref.dtype)
            k += 1
            o += w


_ANY = pl.BlockSpec(memory_space=pl.ANY)


class _Exchange:
    def __init__(self, kind, arrs):
        self.kind, self.arrs, self.results = kind, list(arrs), None
        if kind == "scatter":
            self.out_shape = [jax.ShapeDtypeStruct((N_DEV - 1,) + a.shape[1:], a.dtype) for a in self.arrs]
        elif kind == "swap3":
            self.out_shape = [jax.ShapeDtypeStruct(a.shape, a.dtype) for a in self.arrs]
        else:
            self.out_shape = [jax.ShapeDtypeStruct((N_DEV,) + a.shape, a.dtype) for a in self.arrs]
        self.nsem = (3 if kind == "swap3" else N_DEV - 1) * len(self.arrs)

    def copies(self, in_refs, out_refs, ssem, rsem):
        x, y, c = lax.axis_index("x"), lax.axis_index("y"), lax.axis_index("c")
        me = 4 * x + 2 * y + c
        cps = []
        if self.kind == "swap3":
            targets = [(1 - x, y, c), (x, 1 - y, c), (1 - x, 1 - y, c)]
            return [pltpu.make_async_remote_copy(src_ref=i_ref.at[k], dst_ref=o_ref.at[k], send_sem=ssem.at[3 * a + k],
                                                 recv_sem=rsem.at[3 * a + k], device_id=targets[k],
                                                 device_id_type=MESH)
                    for a, (i_ref, o_ref) in enumerate(zip(in_refs, out_refs)) for k in range(3)]
        for a, (i_ref, o_ref) in enumerate(zip(in_refs, out_refs)):
            for j in range(1, N_DEV):
                px = 1 - x if j & 4 else x
                py = 1 - y if j & 2 else y
                pc = 1 - c if j & 1 else c
                if self.kind == "gather":
                    src, dst = i_ref, o_ref.at[me]
                else:
                    src, dst = i_ref.at[4 * px + 2 * py + pc], o_ref.at[j - 1]
                s = (N_DEV - 1) * a + j - 1
                cps.append(pltpu.make_async_remote_copy(src_ref=src, dst_ref=dst, send_sem=ssem.at[s],
                                                        recv_sem=rsem.at[s], device_id=(px, py, pc),
                                                        device_id_type=MESH))
        return cps

    def run(self, first, mid, last, in_refs, out_refs, ssem, rsem):
        if self.kind == "gather2":
            return self.run_two_level(first, mid, last, in_refs, out_refs, ssem, rsem)

        @pl.when(first)
        def _():
            for cp in self.copies(in_refs, out_refs, ssem, rsem):
                cp.start()

        @pl.when(last)
        def _():
            for cp in self.copies(in_refs, out_refs, ssem, rsem):
                cp.wait()

    def run_two_level(self, first, mid, last, in_refs, out_refs, ssem, rsem):
        x, y, c = lax.axis_index("x"), lax.axis_index("y"), lax.axis_index("c")
        me, sibling = (x, y, c), (x, y, 1 - c)
        chips = [(1 - x, y), (x, 1 - y), (1 - x, 1 - y)]
        arrs = range(len(in_refs))

        def copy(a, k, block, to, src=None):
            dst = out_refs[a].at[4 * block[0] + 2 * block[1] + block[2]]
            return pltpu.make_async_remote_copy(
                src_ref=dst if src is None else src, dst_ref=dst, send_sem=ssem.at[7 * a + k],
                recv_sem=rsem.at[7 * a + k], device_id=to, device_id_type=MESH)

        def firsts(a):
            return [copy(a, 0, me, sibling, src=in_refs[a])] + [
                copy(a, 1 + j, me, (*chip, c), src=in_refs[a]) for j, chip in enumerate(chips)]

        def passed(a):
            return [copy(a, 4 + j, (*chip, c), sibling) for j, chip in enumerate(chips)]

        @pl.when(first)
        def _():
            for a in arrs:
                for cp in firsts(a):
                    cp.start()

        @pl.when(mid)
        def _():
            for j, chip in enumerate(chips):
                for a in arrs:
                    copy(a, 1 + j, (*chip, c), me).wait_recv()
                    passed(a)[j].start()

        @pl.when(last)
        def _():
            for a in arrs:
                copy(a, 0, sibling, me).wait_recv()
                for j, chip in enumerate(chips):
                    copy(a, 4 + j, (*chip, 1 - c), me).wait_recv()
                for cp in firsts(a) + passed(a):
                    cp.wait_send()


def _hook_specs(hook):
    if hook is None:
        return [], [], [], []
    na = len(hook.arrs)
    sems = [pltpu.SemaphoreType.DMA((hook.nsem,)), pltpu.SemaphoreType.DMA((hook.nsem,))]
    return [_ANY] * na, [_ANY] * na, hook.out_shape, sems


def _stage_fwd(st, t, params, inputs, hook=None):
    g, tm = st.g, min(st.tm, t)
    n = t // tm
    npar, nin, ncar, nout = len(params), len(inputs), len(st.carry_shapes), len(st.out_pieces)
    h_in, h_out, h_shape, h_sems = _hook_specs(hook)
    nh = len(h_in)

    def body(*refs):
        p_refs = refs[:npar]
        x_refs = refs[npar:npar + nin]
        hi_refs = refs[npar + nin:npar + nin + nh]
        o = npar + nin + nh
        o_refs = refs[o:o + nout]
        s_refs = refs[o + nout:o + nout + ncar]
        ho_refs = refs[o + nout + ncar:o + nout + ncar + nh]
        c_scr = refs[o + nout + ncar + nh:o + nout + ncar + nh + ncar]
        gi, ni = pl.program_id(0), pl.program_id(1)
        if hook is not None:
            step = gi * n + ni
            hook.run(step == 0, step == (4 * g * n) // 5, step == g * n - 1, hi_refs, ho_refs, *refs[-2:])

        @pl.when(ni == 0)
        def _():
            for c in c_scr:
                c[...] = jnp.zeros(c.shape, F32)

        ps = [r[...].astype(F32) for r in p_refs]
        xs = _load_pieces(x_refs, st.in_pieces)
        cs = [c[...] for c in c_scr]
        for s, c in zip(s_refs, cs):
            s[...] = c
        outs, ncs = st.f(ps, xs, cs)
        _store_pieces(o_refs, st.out_pieces, outs)
        for c, v in zip(c_scr, ncs):
            c[...] = v

    in_specs = [_par_spec(p, pg, g) for p, pg in zip(params, st.par_per_g)]
    in_specs += [_row_spec(tm, sum(pc), off, n, False) for pc, off in zip(st.in_pieces, st.in_offs)]
    out_specs = [_row_spec(tm, sum(pc), 0, n, False) for pc in st.out_pieces]
    out_specs += [_carry_spec(s, n, False) for s in st.carry_shapes]
    out_shape = [jax.ShapeDtypeStruct((t, g * sum(pc)), dt) for pc, dt in zip(st.out_pieces, st.out_dtypes)]
    out_shape += [jax.ShapeDtypeStruct((g, n) + s, F32) for s in st.carry_shapes]
    res = pl.pallas_call(
        body, name=st.name + "_fwd", grid=(g, n), in_specs=in_specs + h_in, out_specs=out_specs + h_out,
        out_shape=out_shape + h_shape,
        scratch_shapes=[pltpu.VMEM(s, F32) for s in st.carry_shapes] + h_sems,
        compiler_params=_cparams(("arbitrary", "arbitrary")),
    )(*params, *inputs, *(hook.arrs if hook else []))
    if hook is not None:
        hook.results = list(res[nout + ncar:])
    return list(res[:nout]), list(res[nout:nout + ncar])


def _stage_bwd(st, t, params, inputs, saved, douts, dx_dtypes, hook=None):
    g, tm = st.g, min(st.tm, t)
    n = t // tm
    npar, nin, ncar = len(params), len(inputs), len(st.carry_shapes)
    flat_d = [d for ds in douts for d in ds]
    nd = len(flat_d)
    dx_idx = [i for i, dt in enumerate(dx_dtypes) if dt is not None]
    h_in, h_out, h_shape, h_sems = _hook_specs(hook)
    nh = len(h_in)

    def body(*refs):
        p_refs = refs[:npar]
        x_refs = refs[npar:npar + nin]
        s_refs = refs[npar + nin:npar + nin + ncar]
        d_refs = refs[npar + nin + ncar:npar + nin + ncar + nd]
        hi_refs = refs[npar + nin + ncar + nd:npar + nin + ncar + nd + nh]
        o = npar + nin + ncar + nd + nh
        dp_refs = refs[o:o + npar]
        dx_refs = refs[o + npar:o + npar + len(dx_idx)]
        ho_refs = refs[o + npar + len(dx_idx):o + npar + len(dx_idx) + nh]
        dc_scr = refs[o + npar + len(dx_idx) + nh:o + npar + len(dx_idx) + nh + ncar]
        gi, ni = pl.program_id(0), pl.program_id(1)
        if hook is not None:
            step = gi * n + ni
            hook.run(step == 0, step == (4 * g * n) // 5, step == g * n - 1, hi_refs, ho_refs, *refs[-2:])

        @pl.when(ni == 0)
        def _():
            for c in dc_scr:
                c[...] = jnp.zeros(c.shape, F32)

        ps = [r[...].astype(F32) for r in p_refs]
        xs = _load_pieces(x_refs, st.in_pieces)
        cs = [s[...] for s in s_refs]
        dys = []
        k = 0
        for ds, pieces in zip(douts, st.out_pieces):
            acc = _load_pieces([d_refs[k]], [pieces])
            for j in range(1, len(ds)):
                more = _load_pieces([d_refs[k + j]], [pieces])
                acc = [a + b for a, b in zip(acc, more)]
            dys += acc
            k += len(ds)
        _, vjp = jax.vjp(st.f, ps, xs, cs)
        dps, dxs, dcs = vjp((dys, [c[...] for c in dc_scr]))
        k = 0
        per_in = []
        for pieces in st.in_pieces:
            per_in.append(dxs[k:k + len(pieces)])
            k += len(pieces)
        for ref, i in zip(dx_refs, dx_idx):
            _store_pieces([ref], [st.in_pieces[i]], per_in[i])
        for c, v in zip(dc_scr, dcs):
            c[...] = v
        for ref, dp, pg in zip(dp_refs, dps, st.par_per_g):
            first = (ni == 0) if pg else ((ni == 0) & (gi == 0))

            @pl.when(first)
            def _():
                ref[...] = jnp.zeros(ref.shape, F32)

            ref[...] += dp

    in_specs = [_par_spec(p, pg, g) for p, pg in zip(params, st.par_per_g)]
    in_specs += [_row_spec(tm, sum(pc), off, n, True) for pc, off in zip(st.in_pieces, st.in_offs)]
    in_specs += [_carry_spec(s, n, True) for s in st.carry_shapes]
    for ds, pc in zip(douts, st.out_pieces):
        in_specs += [_row_spec(tm, sum(pc), 0, n, True) for _ in ds]
    out_specs = [_par_spec(p, pg, g) for p, pg in zip(params, st.par_per_g)]
    out_specs += [_row_spec(tm, sum(st.in_pieces[i]), 0, n, True) for i in dx_idx]
    out_shape = [jax.ShapeDtypeStruct(p.shape, F32) for p in params]
    out_shape += [jax.ShapeDtypeStruct((t, g * sum(st.in_pieces[i])), dx_dtypes[i]) for i in dx_idx]
    res = pl.pallas_call(
        body, name=st.name + "_bwd", grid=(g, n), in_specs=in_specs + h_in, out_specs=out_specs + h_out,
        out_shape=out_shape + h_shape,
        scratch_shapes=[pltpu.VMEM(s, F32) for s in st.carry_shapes] + h_sems,
        compiler_params=_cparams(("arbitrary", "arbitrary")),
    )(*params, *inputs, *saved, *flat_d, *(hook.arrs if hook else []))
    if hook is not None:
        hook.results = list(res[npar + len(dx_idx):])
    return list(res[:npar]), list(res[npar:npar + len(dx_idx)])


def _pick(n, cap):
    if n <= cap:
        return n
    best = LANES
    for k in range(1, n // LANES + 1):
        if (n // LANES) % k == 0 and k * LANES <= cap:
            best = k * LANES
    return best


def _mm(name, a, b, mode, out_dtype=F32, tm=1024, tn=512, b_outer=False, hook=None):
    m = a.shape[1] if mode == "tn" else a.shape[0]
    k = a.shape[0] if mode == "tn" else a.shape[1]
    n = b.shape[0] if mode == "nt" else b.shape[1]
    tm, tn = _pick(m, tm), _pick(n, tn)
    h_in, h_out, h_shape, h_sems = _hook_specs(hook)
    nh = len(h_in)
    if b_outer:
        grid = (n // tn, m // tm)
        ij = lambda p, q: (q, p)
    else:
        grid = (m // tm, n // tn)
        ij = lambda p, q: (p, q)

    def body(*refs):
        a_ref, b_ref, o_ref = refs[0], refs[1], refs[2 + nh]
        if hook is not None:
            step = pl.program_id(0) * grid[1] + pl.program_id(1)
            total = grid[0] * grid[1]
            hook.run(step == 0, step == (4 * total) // 5, step == total - 1, refs[2:2 + nh],
                     refs[3 + nh:3 + 2 * nh], *refs[-2:])
        o_ref[...] = _raw_dot(a_ref[...], b_ref[...], mode).astype(o_ref.dtype)

    if mode == "tn":
        a_spec = pl.BlockSpec((k, tm), lambda p, q: (0, ij(p, q)[0]))
    else:
        a_spec = pl.BlockSpec((tm, k), lambda p, q: (ij(p, q)[0], 0))
    if mode == "nt":
        b_spec = pl.BlockSpec((tn, k), lambda p, q: (ij(p, q)[1], 0))
    else:
        b_spec = pl.BlockSpec((k, tn), lambda p, q: (0, ij(p, q)[1]))
    res = pl.pallas_call(
        body, name=name, grid=grid, in_specs=[a_spec, b_spec] + h_in,
        out_specs=[pl.BlockSpec((tm, tn), lambda p, q: ij(p, q))] + h_out,
        out_shape=[jax.ShapeDtypeStruct((m, n), out_dtype)] + h_shape, scratch_shapes=h_sems,
        compiler_params=_cparams(("arbitrary", "arbitrary")),
    )(a, b, *(hook.arrs if hook else []))
    if hook is not None:
        hook.results = list(res[1:])
    return res[0]


def _loss_stage(t, g_post, h1, ff, tgt):
    tm = min(256, t)
    n = t // tm

    def body(g_ref, h_ref, f_ref, t_ref, loss_ref, dg_ref, dh_ref, df_ref):
        ni = pl.program_id(0)
        target = t_ref[...]

        def lossf(g, h1, ff):
            e = h1 + _rms(ff, g) - target
            return 0.5 * jnp.sum(jnp.mean(e * e, axis=-1))

        l, (dg, dh, df) = jax.value_and_grad(lossf, argnums=(0, 1, 2))(g_ref[...], h_ref[...], f_ref[...])

        @pl.when(ni == 0)
        def _():
            loss_ref[...] = jnp.zeros(loss_ref.shape, F32)
            dg_ref[...] = jnp.zeros(dg_ref.shape, F32)

        loss_ref[...] += jnp.full(loss_ref.shape, l, F32)
        dg_ref[...] += dg
        dh_ref[...] = dh
        df_ref[...] = df.astype(df_ref.dtype)

    row = pl.BlockSpec((tm, D), lambda ni: (ni, 0))
    one = pl.BlockSpec((1, D), lambda ni: (0, 0))
    return pl.pallas_call(
        body, name="loss_head", grid=(n,), in_specs=[one, row, row, row],
        out_specs=[pl.BlockSpec((1, LANES), lambda ni: (0, 0)), one, row, row],
        out_shape=[jax.ShapeDtypeStruct((1, LANES), F32), jax.ShapeDtypeStruct((1, D), F32),
                   jax.ShapeDtypeStruct((t, D), F32), jax.ShapeDtypeStruct((t, D), BF)],
        compiler_params=_cparams(("arbitrary",)),
    )(g_post, h1, ff, tgt)


_ANY = pl.BlockSpec(memory_space=pl.ANY)


def _all_gather(name, blks):
    na = len(blks)

    def body(*refs):
        x_refs, out_refs = refs[:na], refs[na:2 * na]
        send_sems, recv_sems, local_sems = refs[2 * na:]
        x, y, cc = lax.axis_index("x"), lax.axis_index("y"), lax.axis_index("c")
        me, sibling = (x, y, cc), (x, y, 1 - cc)
        chips = [(1 - x, y), (x, 1 - y), (1 - x, 1 - y)]

        def copy(a, k, block, to, src=None):
            dst = out_refs[a].at[4 * block[0] + 2 * block[1] + block[2]]
            return pltpu.make_async_remote_copy(
                src_ref=dst if src is None else src, dst_ref=dst, send_sem=send_sems.at[7 * a + k],
                recv_sem=recv_sems.at[7 * a + k], device_id=to, device_id_type=MESH)

        mine, first, passed = [], [], []
        for a in range(na):
            m = pltpu.make_async_copy(x_refs[a], out_refs[a].at[4 * x + 2 * y + cc], local_sems.at[a])
            m.start()
            mine.append(m)
            cps = [copy(a, 0, me, sibling, src=x_refs[a])]
            cps += [copy(a, 1 + j, me, (*chip, cc), src=x_refs[a]) for j, chip in enumerate(chips)]
            for cp in cps:
                cp.start()
            first += cps
        for j, chip in enumerate(chips):
            for a in range(na):
                copy(a, 1 + j, (*chip, cc), me).wait_recv()
                fw = copy(a, 4 + j, (*chip, cc), sibling)
                fw.start()
                passed.append(fw)
        for a in range(na):
            copy(a, 0, sibling, me).wait_recv()
            for j, chip in enumerate(chips):
                copy(a, 4 + j, (*chip, 1 - cc), me).wait_recv()
        for cp in first + passed:
            cp.wait_send()
        for m in mine:
            m.wait()

    res = pl.pallas_call(
        body, name=name, in_specs=[_ANY] * na, out_specs=[_ANY] * na,
        out_shape=[jax.ShapeDtypeStruct((N_DEV,) + b.shape, b.dtype) for b in blks],
        scratch_shapes=[pltpu.SemaphoreType.DMA((7 * na,)), pltpu.SemaphoreType.DMA((7 * na,)),
                        pltpu.SemaphoreType.DMA((na,))],
    )(*blks)
    return list(res)


def _reduce_pair(g8s):
    na = len(g8s)

    def body(*refs):
        g_refs, recv_refs = refs[:na], refs[na:2 * na]
        ssem, rsem = refs[2 * na:]
        x, y, cc = lax.axis_index("x"), lax.axis_index("y"), lax.axis_index("c")
        chips = [(x, y), (1 - x, y), (x, 1 - y), (1 - x, 1 - y)]
        sib = (x, y, 1 - cc)
        for a in range(na):
            for k, (cx, cy) in enumerate(chips):
                pltpu.make_async_remote_copy(
                    src_ref=g_refs[a].at[4 * cx + 2 * cy + 1 - cc], dst_ref=recv_refs[a].at[k],
                    send_sem=ssem.at[a], recv_sem=rsem.at[a], device_id=sib, device_id_type=MESH).start()
        for a in range(na):
            pltpu.make_async_remote_copy(src_ref=recv_refs[a], dst_ref=recv_refs[a], send_sem=ssem.at[a],
                                         recv_sem=rsem.at[a], device_id=sib, device_id_type=MESH).wait()

    res = pl.pallas_call(
        body, name="reduce_pair", in_specs=[_ANY] * na, out_specs=[_ANY] * na,
        out_shape=[jax.ShapeDtypeStruct((4,) + g.shape[1:], g.dtype) for g in g8s],
        scratch_shapes=[pltpu.SemaphoreType.DMA((na,)), pltpu.SemaphoreType.DMA((na,))],
    )(*g8s)
    return list(res)


def _pick_rows(r, c):
    if r * c * 4 <= TILE_BYTES or r % 16:
        return r
    best = 16
    for tr in range(16, r, 16):
        if r % tr == 0 and tr * c * 4 <= TILE_BYTES:
            best = tr
    return best


def _pair_sum(name, idx4, g8, recv4):
    _, r, c = g8.shape
    tr = _pick_rows(r, c)

    def body(idx_ref, a_ref, b_ref, o0_ref, o3_ref):
        k = pl.program_id(1)
        s = a_ref[...].astype(F32) + b_ref[...].astype(F32)

        @pl.when(k == 0)
        def _():
            o0_ref[...] = s

        @pl.when(k > 0)
        def _():
            o3_ref[...] = s.astype(BF)

    spec = pltpu.PrefetchScalarGridSpec(
        num_scalar_prefetch=1, grid=(r // tr, 4),
        in_specs=[pl.BlockSpec((None, tr, c), lambda i, k, idx: (idx[k], i, 0)),
                  pl.BlockSpec((None, tr, c), lambda i, k, idx: (k, i, 0))],
        out_specs=[pl.BlockSpec((tr, c), lambda i, k, idx: (i, 0)),
                   pl.BlockSpec((None, tr, c), lambda i, k, idx: (jnp.maximum(k - 1, 0), i, 0))])
    return pl.pallas_call(
        body, name=name, grid_spec=spec,
        out_shape=[jax.ShapeDtypeStruct((r, c), F32), jax.ShapeDtypeStruct((3, r, c), BF)],
        compiler_params=_cparams(("arbitrary", "arbitrary")),
    )(idx4, g8, recv4)


def _adamw(w, g, m, v):
    m = ADAM_B1 * m + (1.0 - ADAM_B1) * g
    v = ADAM_B2 * v + (1.0 - ADAM_B2) * jnp.square(g)
    m_hat = m / (1.0 - ADAM_B1 ** ADAM_STEP)
    v_hat = v / (1.0 - ADAM_B2 ** ADAM_STEP)
    delta = -ADAM_LR * (m_hat / (jnp.sqrt(v_hat) + ADAM_EPS) + ADAM_WD * w)
    return delta, m, v


def _adam_sharded(name, idx1, own, recv, w, m, v):
    r, c = w.shape
    tr = _pick_rows(r, c)
    nj = recv.shape[0]

    def body(idx_ref, p_ref, r_ref, w_ref, m_ref, v_ref, g_out, d_out, m_out, v_out):
        g = p_ref[...].astype(F32)
        for k in range(nj):
            g = g + r_ref[k].astype(F32)
        d, mn, vn = _adamw(w_ref[...], g, m_ref[...], v_ref[...])
        g_out[...] = g
        d_out[...] = d
        m_out[...] = mn
        v_out[...] = vn

    row = pl.BlockSpec((tr, c), lambda i, idx: (i, 0))
    spec = pltpu.PrefetchScalarGridSpec(
        num_scalar_prefetch=1, grid=(r // tr,),
        in_specs=[pl.BlockSpec((None, tr, c), lambda i, idx: (idx[0], i, 0)),
                  pl.BlockSpec((nj, tr, c), lambda i, idx: (0, i, 0)), row, row, row],
        out_specs=[row] * 4)
    return pl.pallas_call(
        body, name=name, grid_spec=spec, out_shape=[jax.ShapeDtypeStruct((r, c), F32)] * 4,
        compiler_params=_cparams(("arbitrary",)),
    )(idx1, own, recv, w, m, v)


def _repl_rows():
    rows, r = {}, 0
    for name, cols in REPL:
        rows[name] = r
        r += REPL_ROWS.get(name, 1) * ((cols + D - 1) // D)
    return rows


def _pack_replicated(grads):
    rows = _repl_rows()
    names = [n for n, _ in REPL]

    def body(*refs):
        o_ref = refs[-1]
        o_ref[...] = jnp.zeros(o_ref.shape, F32)
        for name, ref in zip(names, refs[:-1]):
            r0 = rows[name]
            nr, nc = ref.shape
            if nc <= D:
                o_ref[r0:r0 + nr, 0:nc] = ref[...]
            else:
                for j in range((nc + D - 1) // D):
                    lo, hi = j * D, min(nc, (j + 1) * D)
                    o_ref[r0 + j:r0 + j + 1, 0:hi - lo] = ref[:, lo:hi]

    return pl.pallas_call(body, name="pack_replicated", out_shape=jax.ShapeDtypeStruct((REPL_TOTAL, D), F32),
                          compiler_params=_cparams())(*[grads[n] for n in names])


def _adam_replicated(g8, ws, ms, vs):
    rows = _repl_rows()
    names = [n for n, _ in REPL]
    np_ = len(names)

    def body(*refs):
        g_ref = refs[0]
        w_refs, m_refs, v_refs = refs[1:1 + np_], refs[1 + np_:1 + 2 * np_], refs[1 + 2 * np_:1 + 3 * np_]
        outs = refs[1 + 3 * np_:1 + 7 * np_]
        scr = refs[-1]
        g = g_ref[0]
        for k in range(1, N_DEV):
            g = g + g_ref[k]
        scr[...] = g
        for i, name in enumerate(names):
            r0 = rows[name]
            nr, nc = w_refs[i].shape
            if nc <= D:
                gi = scr[r0:r0 + nr, 0:nc]
            else:
                parts = []
                for j in range((nc + D - 1) // D):
                    lo, hi = j * D, min(nc, (j + 1) * D)
                    parts.append(scr[r0 + j:r0 + j + 1, 0:hi - lo])
                gi = jnp.concatenate(parts, axis=1)
            d, mn, vn = _adamw(w_refs[i][...], gi, m_refs[i][...], v_refs[i][...])
            outs[i][...] = gi
            outs[np_ + i][...] = d
            outs[2 * np_ + i][...] = mn
            outs[3 * np_ + i][...] = vn

    shp = [jax.ShapeDtypeStruct(w.shape, F32) for w in ws]
    res = pl.pallas_call(body, name="adam_replicated", out_shape=shp * 4,
                         scratch_shapes=[pltpu.VMEM((REPL_TOTAL, D), F32)], compiler_params=_cparams(),
                         )(g8, *ws, *ms, *vs)
    return [dict(zip(names, res[k * np_:(k + 1) * np_])) for k in range(4)]


_WEIGHTS = ("attn_pre_norm", "w_in", "hgrn_lb", "hgrn_gnorm", "w_branch_a", "rwkv_mu", "rwkv_w0", "rwkv_w2",
            "rwkv_a0", "rwkv_a2", "rwkv_g2", "rwkv_k_k", "rwkv_k_a", "rwkv_r_k", "rwkv_ln_w", "rwkv_ln_b",
            "w_branch_b", "w_out", "attn_post_norm", "ffn_pre_norm", "w_up", "conv_w", "conv_b", "w_down",
            "ffn_post_norm")
_BIG = ("w_in", "w_up", "w_down", "w_branch_a", "w_branch_b", "w_out")


def _stages():
    one = [D]
    hw = HG_K * HG_PER_STEP
    rw = LANES * RW_PAIRS_PER_STEP
    return dict(
        pre1=_Stage("pre1", _f_pre1, 1, 256, [False], [one], [0], [], [one], [BF]),
        mixers=_Stage("mixers", _f_mixers, 1, RW_CHUNK, [False] * 13, [one] * 7 + [[LANES], [LANES]],
                      [0, 1, 2, 3, 4, 5, 6, 56, 57], [(hw, HG_K), (1, RW_COLS), (rw, LANES)], [one, one], [BF, BF]),
        merge=_Stage("merge", _f_merge, 4, 512, [], [[256]] * 4, [29, 33, 0, 0], [], [[256]], [BF]),
        post1=_Stage("post1", _f_post1, 1, 256, [False, False], [one, one], [0, 0], [], [one, one], [F32, BF]),
        conv=_Stage("conv", _f_conv, 1, 128, [False, False], [[DFF], [DFF]], [0, 1], [(1, 2 * DFF), (1, 2 * DFF)],
                    [[DFF]], [BF]),
    )


def _cols_to_blocks(w, per):
    return w.reshape(w.shape[0], N_DEV, per).transpose(1, 0, 2)


def _blocks_to_cols(g):
    return g.transpose(1, 0, 2).reshape(g.shape[1], N_DEV * g.shape[2])


def kernel(x, attn_pre_norm, w_in, hgrn_lb, hgrn_gnorm, w_branch_a, rwkv_mu, rwkv_w0, rwkv_w2, rwkv_a0, rwkv_a2, rwkv_g2, rwkv_k_k, rwkv_k_a, rwkv_r_k, rwkv_ln_w, rwkv_ln_b, w_branch_b, w_out, attn_post_norm, ffn_pre_norm, w_up, conv_w, conv_b, w_down, ffn_post_norm, loss_target, m_attn_pre_norm, m_w_in, m_hgrn_lb, m_hgrn_gnorm, m_w_branch_a, m_rwkv_mu, m_rwkv_w0, m_rwkv_w2, m_rwkv_a0, m_rwkv_a2, m_rwkv_g2, m_rwkv_k_k, m_rwkv_k_a, m_rwkv_r_k, m_rwkv_ln_w, m_rwkv_ln_b, m_w_branch_b, m_w_out, m_attn_post_norm, m_ffn_pre_norm, m_w_up, m_conv_w, m_conv_b, m_w_down, m_ffn_post_norm, v_attn_pre_norm, v_w_in, v_hgrn_lb, v_hgrn_gnorm, v_w_branch_a, v_rwkv_mu, v_rwkv_w0, v_rwkv_w2, v_rwkv_a0, v_rwkv_a2, v_rwkv_g2, v_rwkv_k_k, v_rwkv_k_a, v_rwkv_r_k, v_rwkv_ln_w, v_rwkv_ln_b, v_w_branch_b, v_w_out, v_attn_post_norm, v_ffn_pre_norm, v_w_up, v_conv_w, v_conv_b, v_w_down, v_ffn_post_norm):
    w = dict(attn_pre_norm=attn_pre_norm, w_in=w_in, hgrn_lb=hgrn_lb, hgrn_gnorm=hgrn_gnorm, w_branch_a=w_branch_a, rwkv_mu=rwkv_mu, rwkv_w0=rwkv_w0, rwkv_w2=rwkv_w2, rwkv_a0=rwkv_a0, rwkv_a2=rwkv_a2, rwkv_g2=rwkv_g2, rwkv_k_k=rwkv_k_k, rwkv_k_a=rwkv_k_a, rwkv_r_k=rwkv_r_k, rwkv_ln_w=rwkv_ln_w, rwkv_ln_b=rwkv_ln_b, w_branch_b=w_branch_b, w_out=w_out, attn_post_norm=attn_post_norm, ffn_pre_norm=ffn_pre_norm, w_up=w_up, conv_w=conv_w, conv_b=conv_b, w_down=w_down, ffn_post_norm=ffn_post_norm)
    mo = dict(attn_pre_norm=m_attn_pre_norm, w_in=m_w_in, hgrn_lb=m_hgrn_lb, hgrn_gnorm=m_hgrn_gnorm, w_branch_a=m_w_branch_a, rwkv_mu=m_rwkv_mu, rwkv_w0=m_rwkv_w0, rwkv_w2=m_rwkv_w2, rwkv_a0=m_rwkv_a0, rwkv_a2=m_rwkv_a2, rwkv_g2=m_rwkv_g2, rwkv_k_k=m_rwkv_k_k, rwkv_k_a=m_rwkv_k_a, rwkv_r_k=m_rwkv_r_k, rwkv_ln_w=m_rwkv_ln_w, rwkv_ln_b=m_rwkv_ln_b, w_branch_b=m_w_branch_b, w_out=m_w_out, attn_post_norm=m_attn_post_norm, ffn_pre_norm=m_ffn_pre_norm, w_up=m_w_up, conv_w=m_conv_w, conv_b=m_conv_b, w_down=m_w_down, ffn_post_norm=m_ffn_post_norm)
    vo = dict(attn_pre_norm=v_attn_pre_norm, w_in=v_w_in, hgrn_lb=v_hgrn_lb, hgrn_gnorm=v_hgrn_gnorm, w_branch_a=v_w_branch_a, rwkv_mu=v_rwkv_mu, rwkv_w0=v_rwkv_w0, rwkv_w2=v_rwkv_w2, rwkv_a0=v_rwkv_a0, rwkv_a2=v_rwkv_a2, rwkv_g2=v_rwkv_g2, rwkv_k_k=v_rwkv_k_k, rwkv_k_a=v_rwkv_k_a, rwkv_r_k=v_rwkv_r_k, rwkv_ln_w=v_rwkv_ln_w, rwkv_ln_b=v_rwkv_ln_b, w_branch_b=v_w_branch_b, w_out=v_w_out, attn_post_norm=v_attn_post_norm, ffn_pre_norm=v_ffn_pre_norm, w_up=v_w_up, conv_w=v_conv_w, conv_b=v_conv_b, w_down=v_w_down, ffn_post_norm=v_ffn_post_norm)

    t = x.shape[1]
    x2 = x.reshape(t, D)
    tgt = loss_target.reshape(t, D)
    st = _stages()

    me = 4 * lax.axis_index("x") + 2 * lax.axis_index("y") + lax.axis_index("c")
    small = jnp.concatenate([rwkv_w2[0], rwkv_a2[0], rwkv_g2[0]], axis=0).astype(BF)
    g_in, g_small = _all_gather("gather_weights", [w_in[0].astype(BF), small])
    fw_in = _blocks_to_cols(g_in)
    z64 = jnp.zeros((64, D), BF)
    w2p = jnp.concatenate([_blocks_to_cols(g_small[:, 0:64]), z64], axis=0)
    a2p = jnp.concatenate([z64, _blocks_to_cols(g_small[:, 64:128])], axis=0)
    g2f = _blocks_to_cols(g_small[:, 128:256])
    conv_bits = lax.bitcast_convert_type(conv_w[0], BF).reshape(3, 2 * 704)
    late = [w[k][0].astype(BF) for k in _BIG[1:]] + [conv_bits]
    late_gather = _Exchange("gather2", late)
    r_k = rwkv_r_k.reshape(1, D)

    (xn,), _ = _stage_fwd(st["pre1"], t, [attn_pre_norm], [x2])
    z = _mm("in_proj", xn, fw_in, "nn", F32, tm=512, tn=4736, b_outer=True)
    mix_par = [hgrn_lb, hgrn_gnorm, rwkv_mu, rwkv_w0, w2p, rwkv_a0, a2p, g2f, rwkv_k_k, rwkv_k_a,
               rwkv_ln_w, rwkv_ln_b, r_k]
    mix_in = [z] * 9
    (o_a, o_b), mix_saved = _stage_fwd(st["mixers"], t, mix_par, mix_in, hook=late_gather)
    gl = [lax.dynamic_update_slice(g, own[None], (me, 0, 0)) for g, own in zip(late_gather.results, late)]
    fw_up = _blocks_to_cols(gl[0])
    fw_down = gl[1].reshape(DFF, D)
    fw_a, fw_b, fw_out = (g.reshape(D, D) for g in gl[2:5])
    conv_full = _blocks_to_cols(lax.bitcast_convert_type(gl[5].reshape(N_DEV, 3, 704, 2), F32))
    y_a = _mm("branch_a", o_a, fw_a, "nn")
    y_b = _mm("branch_b", o_b, fw_b, "nn")
    (merged,), _ = _stage_fwd(st["merge"], t, [], [z, z, y_a, y_b])
    mix = _mm("out_proj", merged, fw_out, "nn")
    (h1, xn2), _ = _stage_fwd(st["post1"], t, [attn_post_norm, ffn_pre_norm], [x2, mix])
    hu = _mm("up_proj", xn2, fw_up, "nn", F32, tm=1024, tn=1408)
    conv_par = [conv_full, conv_b]
    (act,), conv_saved = _stage_fwd(st["conv"], t, conv_par, [hu, hu])
    ff = _mm("down_proj", act, fw_down, "nn")

    loss_acc, d_ffn_post, dh1, dff = _loss_stage(t, ffn_post_norm, h1, ff, tgt)
    dact = _mm("d_act", dff, fw_down, "nt", F32, tm=1024, tn=1408)
    dw_down = _mm("dw_down", act, dff, "tn", BF, tm=1408, tn=512)
    (dcw, dcb), (dhu_g, dhu_v) = _stage_bwd(st["conv"], t, conv_par, [hu, hu], conv_saved, [[dact]], [BF, BF])
    dhu = jnp.concatenate([dhu_g, dhu_v], axis=1)
    dxn2 = _mm("d_xn2", dhu, fw_up, "nt", F32, tm=1024, tn=256)
    dw_up = _mm("dw_up", xn2, dhu, "tn", BF, tm=1024, tn=1408)
    (d_post, d_pre2), (dx_a, dmix) = _stage_bwd(st["post1"], t, [attn_post_norm, ffn_pre_norm], [x2, mix], [],
                                                 [[dh1], [dxn2]], [F32, BF])
    dmerged = _mm("d_merged", dmix, fw_out, "nt")
    dw_out = _mm("dw_out", merged, dmix, "tn", BF)
    _, (dga, dgb, dy_a, dy_b) = _stage_bwd(st["merge"], t, [], [z, z, y_a, y_b], [], [[dmerged]], [BF, BF, BF, BF])
    do_a = _mm("d_oa", dy_a, fw_a, "nt")
    dw_a = _mm("dw_a", o_a, dy_a, "tn", BF)
    do_b = _mm("d_ob", dy_b, fw_b, "nt")
    dw_b = _mm("dw_b", o_b, dy_b, "tn", BF)
    early = [_cols_to_blocks(dw_up, 704), dw_down.reshape(N_DEV, 352, D), dw_a.reshape(N_DEV, 128, D),
             dw_b.reshape(N_DEV, 128, D), dw_out.reshape(N_DEV, 128, D), _cols_to_blocks(dcw.astype(BF), 704)]
    early_scatter = _Exchange("scatter", early)
    mix_dp, dz_hr = _stage_bwd(st["mixers"], t, mix_par, mix_in, mix_saved, [[do_a], [do_b]], [BF] * 9,
                               hook=early_scatter)
    d_lb, d_gn, d_mu, d_w0, d_w2p, d_a0, d_a2p, d_g2, d_kk, d_ka, d_lnw, d_lnb, d_rk = mix_dp
    dz = jnp.concatenate(dz_hr + [dga, dgb], axis=1)
    dw_in = _mm("dw_in", xn, dz, "tn", BF, tm=1024, tn=256)

    ax, ay, ac = lax.axis_index("x"), lax.axis_index("y"), lax.axis_index("c")
    idx4 = jnp.stack([4 * cx + 2 * cy + ac for cx, cy in ((ax, ay), (1 - ax, ay), (ax, 1 - ay), (1 - ax, 1 - ay))])
    idx4 = idx4.astype(jnp.int32)
    idx_me, idx_0 = idx4[0:1], jnp.zeros((1,), jnp.int32)
    d_small = jnp.concatenate([d_w2p[:64], d_a2p[64:], d_g2], axis=0).astype(BF)
    g8s = [_cols_to_blocks(dw_in, 1184), _cols_to_blocks(d_small, LANES)]
    recv4s = _reduce_pair(g8s)
    sums = [_pair_sum("pair_sum_" + n, idx4, g, r) for n, g, r in zip(("w_in", "small"), g8s, recv4s)]
    chip_swap = _Exchange("swap3", [s[1] for s in sums])
    dxn = _mm("d_xn", dz, fw_in, "nt", F32, tm=512, tn=512, b_outer=True, hook=chip_swap)
    recv3s = chip_swap.results
    (d_pre1,), (dx_b,) = _stage_bwd(st["pre1"], t, [attn_pre_norm], [x2], [], [[dxn]], [F32])
    grad_x = (dx_a + dx_b).reshape(x.shape)
    loss = lax.psum(loss_acc[0, 0], ("x", "y", "c"))

    def small_of(src):
        return jnp.concatenate([src["rwkv_w2"][0], src["rwkv_a2"][0], src["rwkv_g2"][0]], axis=0)

    sh_out = [dict() for _ in range(4)]
    res = _adam_sharded("adam_w_in", idx_0, sums[0][0][None], recv3s[0], *[src["w_in"][0] for src in (w, mo, vo)])
    res_s = _adam_sharded("adam_small", idx_0, sums[1][0][None], recv3s[1], *[small_of(src) for src in (w, mo, vo)])
    for kind in range(4):
        sh_out[kind]["w_in"] = res[kind][None]
        sh_out[kind]["rwkv_w2"] = res_s[kind][0:64][None]
        sh_out[kind]["rwkv_a2"] = res_s[kind][64:128][None]
        sh_out[kind]["rwkv_g2"] = res_s[kind][128:256][None]
    for n, own, recv in zip(_BIG[1:] + ("conv_w",), early, early_scatter.results):
        res = _adam_sharded("adam_" + n, idx_me, own, recv, *[src[n][0] for src in (w, mo, vo)])
        for kind in range(4):
            sh_out[kind][n] = res[kind][None]

    rg = dict(attn_pre_norm=d_pre1, hgrn_lb=d_lb, hgrn_gnorm=d_gn, rwkv_mu=d_mu, rwkv_w0=d_w0, rwkv_a0=d_a0,
              rwkv_k_k=d_kk, rwkv_k_a=d_ka, rwkv_r_k=d_rk, rwkv_ln_w=d_lnw, rwkv_ln_b=d_lnb, attn_post_norm=d_post,
              ffn_pre_norm=d_pre2, conv_b=dcb, ffn_post_norm=d_ffn_post)
    (g8,) = _all_gather("gather_small_grads", [_pack_replicated(rg)])
    rnames = [n for n, _ in REPL]
    flat = lambda src: [src[n].reshape(1, D) if n == "rwkv_r_k" else src[n] for n in rnames]
    rp_out = _adam_replicated(g8, flat(w), flat(mo), flat(vo))
    for kind in range(4):
        rp_out[kind]["rwkv_r_k"] = rp_out[kind]["rwkv_r_k"].reshape(rwkv_r_k.shape)

    outs = [loss, grad_x]
    for kind in range(4):
        for name in _WEIGHTS:
            outs.append(sh_out[kind][name] if name in sh_out[kind] else rp_out[kind][name])
    return tuple(outs)
```

```python
import functools

import jax
import jax.numpy as jnp
from jax import lax
from jax.experimental import pallas as pl
from jax.experimental.pallas import tpu as pltpu

F32 = jnp.float32
BF = jnp.bfloat16
MESH = pl.DeviceIdType.MESH

D = 1024
HG_HEADS = 8
HG_K = 128
HG_CHUNK = 32
HG_SCALE = HG_K ** -0.5
HG_PER_STEP = 8
RW_HEADS = 16
RW_N = 64
RW_CHUNK = 64
RW_PAIRS_PER_STEP = 8
DFF = 2816
IN_COLS = 9472
RW_COLS = 3328
EPS = 1e-6
GN_EPS = 1e-5 * RW_N
ADAM_LR = 0.001
ADAM_B1 = 0.9
ADAM_B2 = 0.999
ADAM_EPS = 1e-08
ADAM_WD = 0.01
ADAM_STEP = 10
N_DEV = 8
LANES = 128
VMEM_LIMIT = 56 * 1024 * 1024
TILE_BYTES = 1280 * 1024

REPL = (("attn_pre_norm", 1024), ("hgrn_lb", 1024), ("hgrn_gnorm", 1024), ("rwkv_mu", 3328), ("rwkv_w0", 1024),
        ("rwkv_a0", 1024), ("rwkv_k_k", 1024), ("rwkv_k_a", 1024), ("rwkv_r_k", 1024), ("rwkv_ln_w", 1024),
        ("rwkv_ln_b", 1024), ("attn_post_norm", 1024), ("ffn_pre_norm", 1024), ("conv_b", 5632), ("ffn_post_norm", 1024))
REPL_ROWS = {"hgrn_lb": 2}
REPL_TOTAL = 32


def _cparams(sem=None, **kw):
    return pltpu.CompilerParams(dimension_semantics=sem, vmem_limit_bytes=VMEM_LIMIT, **kw)


_DN = {"nn": ((1,), (0,)), "nt": ((1,), (1,)), "tn": ((0,), (0,))}


def _raw_dot(a, b, mode):
    return lax.dot_general(a.astype(BF), b.astype(BF), (_DN[mode], ((), ())), preferred_element_type=F32)


@functools.partial(jax.custom_vjp, nondiff_argnums=(2,))
def _dot(a, b, mode):
    return _raw_dot(a, b, mode)


def _dot_fwd(a, b, mode):
    return _raw_dot(a, b, mode), (a, b)


def _dot_bwd(mode, res, g):
    a, b = res
    if mode == "nn":
        return _dot(g, b, "nt"), _dot(a, g, "tn")
    if mode == "nt":
        return _dot(g, b, "nn"), _dot(g, a, "tn")
    return _dot(b, g, "nt"), _dot(a, g, "nn")


_dot.defvjp(_dot_fwd, _dot_bwd)


def _bf_pieces(x, n):
    out, r = [], x
    for i in range(n):
        p = r.astype(BF)
        out.append(p)
        if i + 1 < n:
            r = r - p.astype(F32)
    return out


def _raw_split_dot(x, e, mode, n, x_left):
    eb = e.astype(BF)
    acc = None
    for p in _bf_pieces(x, n):
        ops = (p, eb) if x_left else (eb, p)
        t = lax.dot_general(*ops, (_DN[mode], ((), ())), preferred_element_type=F32)
        acc = t if acc is None else acc + t
    return acc


def _raw_headsum(x):
    t = x.shape[0]
    i = lax.broadcasted_iota(jnp.int32, (LANES, LANES), 0)
    j = lax.broadcasted_iota(jnp.int32, (LANES, LANES), 1)
    same = jnp.where((i >= RW_N) == (j >= RW_N), 1.0, 0.0).astype(F32)
    groups = x.shape[1] // LANES
    rows = jnp.concatenate([x[:, q * LANES:(q + 1) * LANES] for q in range(groups)], axis=0)
    s = _raw_split_dot(rows, same, "nn", 2, True)
    return jnp.concatenate([s[q * t:(q + 1) * t] for q in range(groups)], axis=1)


@jax.custom_vjp
def _headsum(x):
    return _raw_headsum(x)


def _headsum_fwd(x):
    return _raw_headsum(x), None


def _headsum_bwd(_, g):
    return (_raw_headsum(g),)


_headsum.defvjp(_headsum_fwd, _headsum_bwd)


@functools.partial(jax.custom_vjp, nondiff_argnums=(2,))
def _tdot(tri, x, n):
    return _raw_split_dot(x, tri, "nn", n, False)


def _tdot_fwd(tri, x, n):
    return _raw_split_dot(x, tri, "nn", n, False), tri


def _tdot_bwd(n, tri, g):
    return jnp.zeros_like(tri), _raw_split_dot(g, tri, "tn", n, False)


_tdot.defvjp(_tdot_fwd, _tdot_bwd)


def _row(x, i):
    r = lax.broadcasted_iota(jnp.int32, x.shape, 0)
    return jnp.sum(jnp.where(r == i, x, 0.0), axis=0, keepdims=True)


def _shift_down(x, prev):
    t = x.shape[0]

    @jax.custom_vjp
    def sh(x, prev):
        r = lax.broadcasted_iota(jnp.int32, x.shape, 0)
        return jnp.where(r == 0, prev, pltpu.roll(x, 1, 0))

    def fwd(x, prev):
        return sh(x, prev), None

    def bwd(_, g):
        r = lax.broadcasted_iota(jnp.int32, g.shape, 0)
        dx = jnp.where(r == t - 1, 0.0, pltpu.roll(g, t - 1, 0))
        return dx, jnp.sum(jnp.where(r == 0, g, 0.0), axis=0, keepdims=True)

    sh.defvjp(fwd, bwd)
    return sh(x, prev)


def _sigmoid(x):
    return jax.nn.sigmoid(x)


def _silu(x):
    return x * jax.nn.sigmoid(x)


def _softplus(x):
    return jnp.maximum(x, 0.0) + jnp.log(1.0 + jnp.exp(-jnp.abs(x)))


def _rms(x, g):
    return (x * lax.rsqrt(jnp.mean(x * x, axis=-1, keepdims=True) + EPS)) * g


def _tril(c):
    r = lax.broadcasted_iota(jnp.int32, (c, c), 0)
    cc = lax.broadcasted_iota(jnp.int32, (c, c), 1)
    return cc <= r


def _f_pre1(ps, xs, cs):
    return [_rms(xs[0], ps[0])], []


def _f_hgrn(ps, xs, cs):
    lbraw, gn = ps
    hq, hf, hi, hg = xs
    hd = range(HG_PER_STEP)
    st = [cs[0][p * HG_K:(p + 1) * HG_K] for p in hd]
    l0, l1 = _row(lbraw, 0), _row(lbraw, 1)
    m = jnp.maximum(l0, l1)
    e0, e1 = jnp.exp(l0 - m), jnp.exp(l1 - m)
    lb = e0 / (e0 + e1)
    q = _silu(hq) * HG_SCALE
    f = lb + (1.0 - lb) * _sigmoid(hf)
    kh = 1.0 - f
    gl = jnp.log(f)
    c = HG_CHUNK
    low = _tril(c)
    tri = jnp.where(low, 1.0, 0.0).astype(F32)
    outs = []
    for i in range(hq.shape[0] // c):
        rows = slice(i * c, (i + 1) * c)
        b = _tdot(tri, gl[rows], 3)
        bref = _row(b, c // 2 - 1)
        blast = _row(b, c - 1)
        qi = q[rows] * jnp.exp(b - bref)
        ki = kh[rows] * jnp.exp(bref - b)
        qd = q[rows] * jnp.exp(b)
        kd = kh[rows] * jnp.exp(blast - b)
        dec = jnp.exp(blast)
        sl = [slice(p * HG_K, (p + 1) * HG_K) for p in hd]
        sc = [jnp.where(low, _dot(qi[:, sl[p]], ki[:, sl[p]], "nt"), 0.0) for p in hd]
        o = [_dot(sc[p], hi[rows, sl[p]], "nn") + _dot(qd[:, sl[p]], st[p], "nt") for p in hd]
        u = [_dot(hi[rows, sl[p]], kd[:, sl[p]], "tn") for p in hd]
        st = [dec[:, sl[p]] * st[p] + u[p] for p in hd]
        outs.append(jnp.concatenate(o, axis=1) if len(o) > 1 else o[0])
    o = outs[0] if len(outs) == 1 else jnp.concatenate(outs, axis=0)
    on = []
    for p in hd:
        op = o[:, p * HG_K:(p + 1) * HG_K]
        on.append(op * lax.rsqrt(jnp.mean(op * op, axis=-1, keepdims=True) + EPS))
    o = jnp.concatenate(on, axis=1) if len(on) > 1 else on[0]
    o = o * gn
    return [o * _silu(hg)], [jnp.concatenate(st, axis=0) if len(st) > 1 else st[0]]


_RW_OFFS = (0, 1024, 2048, 3072, 3200, 3328)


def _f_rwpre(ps, xs, cs):
    mu, w0, w2p, a0, a2p, g2, k_k, k_a = ps
    (prev,) = cs
    t = xs[0].shape[0]
    zs = []
    for i, z in enumerate(xs):
        lo, hi = _RW_OFFS[i], _RW_OFFS[i + 1]
        zs.append(z + mu[:, lo:hi] * (_shift_down(z, prev[:, lo:hi]) - z))
    rr, kr, vr, wa, gz = zs
    w_log = -_softplus(-(w0 + _dot(jnp.tanh(wa), w2p, "nn"))) - 0.5
    lw = -jnp.exp(w_log)
    a = _sigmoid(a0 + _dot(wa, a2p, "nn"))
    g = _dot(_sigmoid(gz), g2, "nn")
    kkr = kr * k_k
    kk = kkr / jnp.maximum(jnp.sqrt(_headsum(kkr * kkr)), 1e-12)
    k2 = kr * (1.0 + (a - 1.0) * k_a)
    newprev = jnp.concatenate([_row(z, t - 1) for z in xs], axis=1)
    return [rr, lw, k2, vr, -kk, kk * a, g], [newprev]


def _f_rwscan(ps, xs, cs):
    npair = RW_PAIRS_PER_STEP
    pr = range(npair)
    r, lw, k, v, av, bv = [[x[:, p * LANES:(p + 1) * LANES] for p in pr] for x in xs]
    sv = [cs[0][p * LANES:(p + 1) * LANES] for p in pr]
    c = RW_CHUNK
    n = 2 * c
    tri = jnp.where(_tril(c), 1.0, 0.0).astype(F32)
    cl = [_tdot(tri, lw[p], 3) for p in pr]
    cl_last = [_row(cl[p], c - 1) for p in pr]
    lane = lax.broadcasted_iota(jnp.int32, (c, LANES), 1)
    h0 = lane < RW_N

    def stack(x):
        return jnp.concatenate([jnp.where(h0, x, 0.0), jnp.where(h0, 0.0, x)], axis=0)

    am = [stack(av[p] * jnp.exp(cl[p] - lw[p])) for p in pr]
    bm = [stack(bv[p] * jnp.exp(-cl[p])) for p in pr]
    km = [stack(k[p] * jnp.exp(-cl[p])) for p in pr]
    rm = [stack(r[p] * jnp.exp(cl[p])) for p in pr]
    vm = [stack(v[p]) for p in pr]
    rn = lax.broadcasted_iota(jnp.int32, (n, n), 0)
    cn = lax.broadcasted_iota(jnp.int32, (n, n), 1)
    blk = (rn >= c) == (cn >= c)
    strict = blk & (cn < rn)
    incl = blk & (cn <= rn)
    lab = [jnp.where(strict, _dot(am[p], bm[p], "nt"), 0.0) for p in pr]
    lak = [jnp.where(strict, _dot(am[p], km[p], "nt"), 0.0) for p in pr]
    wrb = [jnp.where(incl, _dot(rm[p], bm[p], "nt"), 0.0) for p in pr]
    wrk = [jnp.where(incl, _dot(rm[p], km[p], "nt"), 0.0) for p in pr]
    eye = jnp.where(rn == cn, 1.0, 0.0).astype(F32)
    tinv = [eye + lab[p] for p in pr]
    pw = lab
    for _ in range(5):
        pw = [_dot(pw[p], pw[p], "nn") for p in pr]
        tinv = [tinv[p] + _dot(tinv[p], pw[p], "nn") for p in pr]
    rhs = [_dot(am[p], sv[p], "nt") + _dot(lak[p], vm[p], "nn") for p in pr]
    um = [_dot(tinv[p], rhs[p], "nn") for p in pr]
    ym = [_dot(rm[p], sv[p], "nt") + _dot(wrb[p], um[p], "nn") + _dot(wrk[p], vm[p], "nn") for p in pr]
    sn = [(sv[p] + _dot(um[p], bm[p], "tn") + _dot(vm[p], km[p], "tn")) * jnp.exp(cl_last[p]) for p in pr]
    ys = [ym[p][:c] + ym[p][c:] for p in pr]
    return [jnp.concatenate(ys, axis=1)], [jnp.concatenate(sn, axis=0)]


def _f_mixers(ps, xs, cs):
    oa, st = _f_hgrn(ps[:2], xs[:4], cs[:1])
    (r, lw, k, v, av, bv, g), prev = _f_rwpre(ps[2:10], xs[4:], cs[1:2])
    y, sv = _f_rwscan([], [r, lw, k, v, av, bv], cs[2:])
    ob, _ = _f_rwpost(ps[10:], y + [r, k, v, g], [])
    return oa + ob, st + prev + sv


def _f_rwpost(ps, xs, cs):
    ln_w, ln_b, r_k = ps
    y, r, k, v, g = xs
    inv_n = 1.0 / RW_N
    yc = y - _headsum(y) * inv_n
    var = _headsum(yc * yc) * inv_n
    yn = yc * lax.rsqrt(var + GN_EPS)
    yn = yn * ln_w + ln_b
    bonus = _headsum(r * k * r_k) * v
    return [(yn + bonus) * g], []


def _f_merge(ps, xs, cs):
    ga, gb, ya, yb = xs
    return [_sigmoid(ga) * ya + _sigmoid(gb) * yb], []


def _f_post1(ps, xs, cs):
    x, mix = xs
    h1 = x + _rms(mix, ps[0])
    return [h1, _rms(h1, ps[1])], []


def _f_conv(ps, xs, cs):
    cw, cb = ps
    p1, p2 = cs
    w0, w1, w2 = _row(cw, 0), _row(cw, 1), _row(cw, 2)
    t = xs[0].shape[0]
    hc = []
    for i, x in enumerate(xs):
        sl = slice(i * DFF, (i + 1) * DFF)
        s1 = _shift_down(x, p1[:, sl])
        s2 = _shift_down(s1, p2[:, sl])
        hc.append(cb[:, sl] + w0[:, sl] * s2 + w1[:, sl] * s1 + w2[:, sl] * x)
    n1 = jnp.concatenate([_row(x, t - 1) for x in xs], axis=1)
    n2 = jnp.concatenate([_row(x, t - 2) for x in xs], axis=1)
    return [_silu(hc[0]) * hc[1]], [n1, n2]


class _Stage:
    def __init__(self, name, f, g, tm, par_per_g, in_pieces, in_offs, carry_shapes, out_pieces, out_dtypes):
        self.name, self.f, self.g, self.tm = name, f, g, tm
        self.par_per_g, self.in_pieces, self.in_offs = par_per_g, in_pieces, in_offs
        self.carry_shapes, self.out_pieces, self.out_dtypes = carry_shapes, out_pieces, out_dtypes


def _par_spec(arr, per_g, g):
    r, c = arr.shape
    if per_g:
        return pl.BlockSpec((r, c // g), lambda gi, ni: (0, gi))
    return pl.BlockSpec((r, c), lambda gi, ni: (0, 0))


def _row_spec(tm, width, off, n, rev):
    if rev:
        return pl.BlockSpec((tm, width), lambda gi, ni: (n - 1 - ni, off + gi))
    return pl.BlockSpec((tm, width), lambda gi, ni: (ni, off + gi))


def _carry_spec(shape, n, rev):
    if rev:
        return pl.BlockSpec((None, None) + shape, lambda gi, ni: (gi, n - 1 - ni, 0, 0))
    return pl.BlockSpec((None, None) + shape, lambda gi, ni: (gi, ni, 0, 0))


def _load_pieces(refs, pieces_list):
    out = []
    for ref, pieces in zip(refs, pieces_list):
        o = 0
        for w in pieces:
            out.append(ref[:, o:o + w].astype(F32))
            o += w
    return out


def _store_pieces(refs, pieces_list, vals):
    k = 0
    for ref, pieces in zip(refs, pieces_list):
        o = 0
        for w in pieces:
            ref[:, o:o + w] = vals[k].astype(ref.dtype)
            k += 1
            o += w


_ANY = pl.BlockSpec(memory_space=pl.ANY)


class _Exchange:
    def __init__(self, kind, arrs):
        self.kind, self.arrs, self.results = kind, list(arrs), None
        if kind == "scatter":
            self.out_shape = [jax.ShapeDtypeStruct((N_DEV - 1,) + a.shape[1:], a.dtype) for a in self.arrs]
        else:
            self.out_shape = [jax.ShapeDtypeStruct((N_DEV,) + a.shape, a.dtype) for a in self.arrs]
        self.nsem = (N_DEV - 1) * len(self.arrs)

    def copies(self, in_refs, out_refs, ssem, rsem):
        x, y, c = lax.axis_index("x"), lax.axis_index("y"), lax.axis_index("c")
        me = 4 * x + 2 * y + c
        cps = []
        for a, (i_ref, o_ref) in enumerate(zip(in_refs, out_refs)):
            for j in range(1, N_DEV):
                px = 1 - x if j & 4 else x
                py = 1 - y if j & 2 else y
                pc = 1 - c if j & 1 else c
                if self.kind == "gather":
                    src, dst = i_ref, o_ref.at[me]
                else:
                    src, dst = i_ref.at[4 * px + 2 * py + pc], o_ref.at[j - 1]
                s = (N_DEV - 1) * a + j - 1
                cps.append(pltpu.make_async_remote_copy(src_ref=src, dst_ref=dst, send_sem=ssem.at[s],
                                                        recv_sem=rsem.at[s], device_id=(px, py, pc),
                                                        device_id_type=MESH))
        return cps

    def run(self, first, mid, last, in_refs, out_refs, ssem, rsem):
        if self.kind == "gather2":
            return self.run_two_level(first, mid, last, in_refs, out_refs, ssem, rsem)

        @pl.when(first)
        def _():
            for cp in self.copies(in_refs, out_refs, ssem, rsem):
                cp.start()

        @pl.when(last)
        def _():
            for cp in self.copies(in_refs, out_refs, ssem, rsem):
                cp.wait()

    def run_two_level(self, first, mid, last, in_refs, out_refs, ssem, rsem):
        x, y, c = lax.axis_index("x"), lax.axis_index("y"), lax.axis_index("c")
        me, sibling = (x, y, c), (x, y, 1 - c)
        chips = [(1 - x, y), (x, 1 - y), (1 - x, 1 - y)]
        arrs = range(len(in_refs))

        def copy(a, k, block, to, src=None):
            dst = out_refs[a].at[4 * block[0] + 2 * block[1] + block[2]]
            return pltpu.make_async_remote_copy(
                src_ref=dst if src is None else src, dst_ref=dst, send_sem=ssem.at[7 * a + k],
                recv_sem=rsem.at[7 * a + k], device_id=to, device_id_type=MESH)

        def firsts(a):
            return [copy(a, 0, me, sibling, src=in_refs[a])] + [
                copy(a, 1 + j, me, (*chip, c), src=in_refs[a]) for j, chip in enumerate(chips)]

        def passed(a):
            return [copy(a, 4 + j, (*chip, c), sibling) for j, chip in enumerate(chips)]

        @pl.when(first)
        def _():
            for a in arrs:
                for cp in firsts(a):
                    cp.start()

        @pl.when(mid)
        def _():
            for j, chip in enumerate(chips):
                for a in arrs:
                    copy(a, 1 + j, (*chip, c), me).wait_recv()
                    passed(a)[j].start()

        @pl.when(last)
        def _():
            for a in arrs:
                copy(a, 0, sibling, me).wait_recv()
                for j, chip in enumerate(chips):
                    copy(a, 4 + j, (*chip, 1 - c), me).wait_recv()
                for cp in firsts(a) + passed(a):
                    cp.wait_send()


def _hook_specs(hook):
    if hook is None:
        return [], [], [], []
    na = len(hook.arrs)
    sems = [pltpu.SemaphoreType.DMA((hook.nsem,)), pltpu.SemaphoreType.DMA((hook.nsem,))]
    return [_ANY] * na, [_ANY] * na, hook.out_shape, sems


def _stage_fwd(st, t, params, inputs, hook=None):
    g, tm = st.g, min(st.tm, t)
    n = t // tm
    npar, nin, ncar, nout = len(params), len(inputs), len(st.carry_shapes), len(st.out_pieces)
    h_in, h_out, h_shape, h_sems = _hook_specs(hook)
    nh = len(h_in)

    def body(*refs):
        p_refs = refs[:npar]
        x_refs = refs[npar:npar + nin]
        hi_refs = refs[npar + nin:npar + nin + nh]
        o = npar + nin + nh
        o_refs = refs[o:o + nout]
        s_refs = refs[o + nout:o + nout + ncar]
        ho_refs = refs[o + nout + ncar:o + nout + ncar + nh]
        c_scr = refs[o + nout + ncar + nh:o + nout + ncar + nh + ncar]
        gi, ni = pl.program_id(0), pl.program_id(1)
        if hook is not None:
            step = gi * n + ni
            hook.run(step == 0, step == (4 * g * n) // 5, step == g * n - 1, hi_refs, ho_refs, *refs[-2:])

        @pl.when(ni == 0)
        def _():
            for c in c_scr:
                c[...] = jnp.zeros(c.shape, F32)

        ps = [r[...].astype(F32) for r in p_refs]
        xs = _load_pieces(x_refs, st.in_pieces)
        cs = [c[...] for c in c_scr]
        for s, c in zip(s_refs, cs):
            s[...] = c
        outs, ncs = st.f(ps, xs, cs)
        _store_pieces(o_refs, st.out_pieces, outs)
        for c, v in zip(c_scr, ncs):
            c[...] = v

    in_specs = [_par_spec(p, pg, g) for p, pg in zip(params, st.par_per_g)]
    in_specs += [_row_spec(tm, sum(pc), off, n, False) for pc, off in zip(st.in_pieces, st.in_offs)]
    out_specs = [_row_spec(tm, sum(pc), 0, n, False) for pc in st.out_pieces]
    out_specs += [_carry_spec(s, n, False) for s in st.carry_shapes]
    out_shape = [jax.ShapeDtypeStruct((t, g * sum(pc)), dt) for pc, dt in zip(st.out_pieces, st.out_dtypes)]
    out_shape += [jax.ShapeDtypeStruct((g, n) + s, F32) for s in st.carry_shapes]
    res = pl.pallas_call(
        body, name=st.name + "_fwd", grid=(g, n), in_specs=in_specs + h_in, out_specs=out_specs + h_out,
        out_shape=out_shape + h_shape,
        scratch_shapes=[pltpu.VMEM(s, F32) for s in st.carry_shapes] + h_sems,
        compiler_params=_cparams(("arbitrary", "arbitrary")),
    )(*params, *inputs, *(hook.arrs if hook else []))
    if hook is not None:
        hook.results = list(res[nout + ncar:])
    return list(res[:nout]), list(res[nout:nout + ncar])


def _stage_bwd(st, t, params, inputs, saved, douts, dx_dtypes, hook=None):
    g, tm = st.g, min(st.tm, t)
    n = t // tm
    npar, nin, ncar = len(params), len(inputs), len(st.carry_shapes)
    flat_d = [d for ds in douts for d in ds]
    nd = len(flat_d)
    dx_idx = [i for i, dt in enumerate(dx_dtypes) if dt is not None]
    h_in, h_out, h_shape, h_sems = _hook_specs(hook)
    nh = len(h_in)

    def body(*refs):
        p_refs = refs[:npar]
        x_refs = refs[npar:npar + nin]
        s_refs = refs[npar + nin:npar + nin + ncar]
        d_refs = refs[npar + nin + ncar:npar + nin + ncar + nd]
        hi_refs = refs[npar + nin + ncar + nd:npar + nin + ncar + nd + nh]
        o = npar + nin + ncar + nd + nh
        dp_refs = refs[o:o + npar]
        dx_refs = refs[o + npar:o + npar + len(dx_idx)]
        ho_refs = refs[o + npar + len(dx_idx):o + npar + len(dx_idx) + nh]
        dc_scr = refs[o + npar + len(dx_idx) + nh:o + npar + len(dx_idx) + nh + ncar]
        gi, ni = pl.program_id(0), pl.program_id(1)
        if hook is not None:
            step = gi * n + ni
            hook.run(step == 0, step == (4 * g * n) // 5, step == g * n - 1, hi_refs, ho_refs, *refs[-2:])

        @pl.when(ni == 0)
        def _():
            for c in dc_scr:
                c[...] = jnp.zeros(c.shape, F32)

        ps = [r[...].astype(F32) for r in p_refs]
        xs = _load_pieces(x_refs, st.in_pieces)
        cs = [s[...] for s in s_refs]
        dys = []
        k = 0
        for ds, pieces in zip(douts, st.out_pieces):
            acc = _load_pieces([d_refs[k]], [pieces])
            for j in range(1, len(ds)):
                more = _load_pieces([d_refs[k + j]], [pieces])
                acc = [a + b for a, b in zip(acc, more)]
            dys += acc
            k += len(ds)
        _, vjp = jax.vjp(st.f, ps, xs, cs)
        dps, dxs, dcs = vjp((dys, [c[...] for c in dc_scr]))
        k = 0
        per_in = []
        for pieces in st.in_pieces:
            per_in.append(dxs[k:k + len(pieces)])
            k += len(pieces)
        for ref, i in zip(dx_refs, dx_idx):
            _store_pieces([ref], [st.in_pieces[i]], per_in[i])
        for c, v in zip(dc_scr, dcs):
            c[...] = v
        for ref, dp, pg in zip(dp_refs, dps, st.par_per_g):
            first = (ni == 0) if pg else ((ni == 0) & (gi == 0))

            @pl.when(first)
            def _():
                ref[...] = jnp.zeros(ref.shape, F32)

            ref[...] += dp

    in_specs = [_par_spec(p, pg, g) for p, pg in zip(params, st.par_per_g)]
    in_specs += [_row_spec(tm, sum(pc), off, n, True) for pc, off in zip(st.in_pieces, st.in_offs)]
    in_specs += [_carry_spec(s, n, True) for s in st.carry_shapes]
    for ds, pc in zip(douts, st.out_pieces):
        in_specs += [_row_spec(tm, sum(pc), 0, n, True) for _ in ds]
    out_specs = [_par_spec(p, pg, g) for p, pg in zip(params, st.par_per_g)]
    out_specs += [_row_spec(tm, sum(st.in_pieces[i]), 0, n, True) for i in dx_idx]
    out_shape = [jax.ShapeDtypeStruct(p.shape, F32) for p in params]
    out_shape += [jax.ShapeDtypeStruct((t, g * sum(st.in_pieces[i])), dx_dtypes[i]) for i in dx_idx]
    res = pl.pallas_call(
        body, name=st.name + "_bwd", grid=(g, n), in_specs=in_specs + h_in, out_specs=out_specs + h_out,
        out_shape=out_shape + h_shape,
        scratch_shapes=[pltpu.VMEM(s, F32) for s in st.carry_shapes] + h_sems,
        compiler_params=_cparams(("arbitrary", "arbitrary")),
    )(*params, *inputs, *saved, *flat_d, *(hook.arrs if hook else []))
    if hook is not None:
        hook.results = list(res[npar + len(dx_idx):])
    return list(res[:npar]), list(res[npar:npar + len(dx_idx)])


def _pick(n, cap):
    if n <= cap:
        return n
    best = LANES
    for k in range(1, n // LANES + 1):
        if (n // LANES) % k == 0 and k * LANES <= cap:
            best = k * LANES
    return best


def _mm(name, a, b, mode, out_dtype=F32, tm=1024, tn=512, b_outer=False, token=None):
    m = a.shape[1] if mode == "tn" else a.shape[0]
    k = a.shape[0] if mode == "tn" else a.shape[1]
    n = b.shape[0] if mode == "nt" else b.shape[1]
    tm, tn = _pick(m, tm), _pick(n, tn)
    if b_outer:
        grid = (n // tn, m // tm)
        ij = lambda p, q: (q, p)
    else:
        grid = (m // tm, n // tn)
        ij = lambda p, q: (p, q)
    extra = [] if token is None else [token]

    def body(*refs):
        a_ref, b_ref, o_ref = refs[0], refs[1], refs[-1]
        o_ref[...] = _raw_dot(a_ref[...], b_ref[...], mode).astype(o_ref.dtype)

    if mode == "tn":
        a_spec = pl.BlockSpec((k, tm), lambda p, q: (0, ij(p, q)[0]))
    else:
        a_spec = pl.BlockSpec((tm, k), lambda p, q: (ij(p, q)[0], 0))
    if mode == "nt":
        b_spec = pl.BlockSpec((tn, k), lambda p, q: (ij(p, q)[1], 0))
    else:
        b_spec = pl.BlockSpec((k, tn), lambda p, q: (0, ij(p, q)[1]))
    return pl.pallas_call(
        body, name=name, grid=grid,
        in_specs=[a_spec, b_spec] + [pl.BlockSpec(e.shape, lambda p, q: (0, 0)) for e in extra],
        out_specs=pl.BlockSpec((tm, tn), lambda p, q: ij(p, q)),
        out_shape=jax.ShapeDtypeStruct((m, n), out_dtype),
        compiler_params=_cparams(("arbitrary", "arbitrary")),
    )(a, b, *extra)


def _loss_stage(t, g_post, h1, ff, tgt):
    tm = min(256, t)
    n = t // tm

    def body(g_ref, h_ref, f_ref, t_ref, loss_ref, dg_ref, dh_ref, df_ref):
        ni = pl.program_id(0)
        target = t_ref[...]

        def lossf(g, h1, ff):
            e = h1 + _rms(ff, g) - target
            return 0.5 * jnp.sum(jnp.mean(e * e, axis=-1))

        l, (dg, dh, df) = jax.value_and_grad(lossf, argnums=(0, 1, 2))(g_ref[...], h_ref[...], f_ref[...])

        @pl.when(ni == 0)
        def _():
            loss_ref[...] = jnp.zeros(loss_ref.shape, F32)
            dg_ref[...] = jnp.zeros(dg_ref.shape, F32)

        loss_ref[...] += jnp.full(loss_ref.shape, l, F32)
        dg_ref[...] += dg
        dh_ref[...] = dh
        df_ref[...] = df.astype(df_ref.dtype)

    row = pl.BlockSpec((tm, D), lambda ni: (ni, 0))
    one = pl.BlockSpec((1, D), lambda ni: (0, 0))
    return pl.pallas_call(
        body, name="loss_head", grid=(n,), in_specs=[one, row, row, row],
        out_specs=[pl.BlockSpec((1, LANES), lambda ni: (0, 0)), one, row, row],
        out_shape=[jax.ShapeDtypeStruct((1, LANES), F32), jax.ShapeDtypeStruct((1, D), F32),
                   jax.ShapeDtypeStruct((t, D), F32), jax.ShapeDtypeStruct((t, D), BF)],
        compiler_params=_cparams(("arbitrary",)),
    )(g_post, h1, ff, tgt)


_ANY = pl.BlockSpec(memory_space=pl.ANY)


def _all_gather(name, blks):
    na = len(blks)

    def body(*refs):
        x_refs, out_refs = refs[:na], refs[na:2 * na]
        send_sems, recv_sems, local_sems = refs[2 * na:]
        x, y, cc = lax.axis_index("x"), lax.axis_index("y"), lax.axis_index("c")
        me, sibling = (x, y, cc), (x, y, 1 - cc)
        chips = [(1 - x, y), (x, 1 - y), (1 - x, 1 - y)]

        def copy(a, k, block, to, src=None):
            dst = out_refs[a].at[4 * block[0] + 2 * block[1] + block[2]]
            return pltpu.make_async_remote_copy(
                src_ref=dst if src is None else src, dst_ref=dst, send_sem=send_sems.at[7 * a + k],
                recv_sem=recv_sems.at[7 * a + k], device_id=to, device_id_type=MESH)

        mine, first, passed = [], [], []
        for a in range(na):
            m = pltpu.make_async_copy(x_refs[a], out_refs[a].at[4 * x + 2 * y + cc], local_sems.at[a])
            m.start()
            mine.append(m)
            cps = [copy(a, 0, me, sibling, src=x_refs[a])]
            cps += [copy(a, 1 + j, me, (*chip, cc), src=x_refs[a]) for j, chip in enumerate(chips)]
            for cp in cps:
                cp.start()
            first += cps
        for j, chip in enumerate(chips):
            for a in range(na):
                copy(a, 1 + j, (*chip, cc), me).wait_recv()
                fw = copy(a, 4 + j, (*chip, cc), sibling)
                fw.start()
                passed.append(fw)
        for a in range(na):
            copy(a, 0, sibling, me).wait_recv()
            for j, chip in enumerate(chips):
                copy(a, 4 + j, (*chip, 1 - cc), me).wait_recv()
        for cp in first + passed:
            cp.wait_send()
        for m in mine:
            m.wait()

    res = pl.pallas_call(
        body, name=name, in_specs=[_ANY] * na, out_specs=[_ANY] * na,
        out_shape=[jax.ShapeDtypeStruct((N_DEV,) + b.shape, b.dtype) for b in blks],
        scratch_shapes=[pltpu.SemaphoreType.DMA((7 * na,)), pltpu.SemaphoreType.DMA((7 * na,)),
                        pltpu.SemaphoreType.DMA((na,))],
    )(*blks)
    return list(res)


def _reduce_pair(g8s):
    na = len(g8s)

    def body(*refs):
        g_refs, recv_refs = refs[:na], refs[na:2 * na]
        ssem, rsem = refs[2 * na:]
        x, y, cc = lax.axis_index("x"), lax.axis_index("y"), lax.axis_index("c")
        chips = [(x, y), (1 - x, y), (x, 1 - y), (1 - x, 1 - y)]
        sib = (x, y, 1 - cc)
        for a in range(na):
            for k, (cx, cy) in enumerate(chips):
                pltpu.make_async_remote_copy(
                    src_ref=g_refs[a].at[4 * cx + 2 * cy + 1 - cc], dst_ref=recv_refs[a].at[k],
                    send_sem=ssem.at[a], recv_sem=rsem.at[a], device_id=sib, device_id_type=MESH).start()
        for a in range(na):
            pltpu.make_async_remote_copy(src_ref=recv_refs[a], dst_ref=recv_refs[a], send_sem=ssem.at[a],
                                         recv_sem=rsem.at[a], device_id=sib, device_id_type=MESH).wait()

    res = pl.pallas_call(
        body, name="reduce_pair", in_specs=[_ANY] * na, out_specs=[_ANY] * na,
        out_shape=[jax.ShapeDtypeStruct((4,) + g.shape[1:], g.dtype) for g in g8s],
        scratch_shapes=[pltpu.SemaphoreType.DMA((na,)), pltpu.SemaphoreType.DMA((na,))],
    )(*g8s)
    return list(res)


_HBM = pl.BlockSpec(memory_space=pltpu.HBM)
_SEM = pl.BlockSpec(memory_space=pltpu.SEMAPHORE)
_EFFECT = pltpu.SideEffectType.DATAFLOW_SIDE_EFFECTING


def _chip_swap_copies(s_refs, land_refs, ssem, rsem):
    x, y, c = lax.axis_index("x"), lax.axis_index("y"), lax.axis_index("c")
    targets = [(1 - x, y, c), (x, 1 - y, c), (1 - x, 1 - y, c)]
    return [pltpu.make_async_remote_copy(src_ref=s.at[k], dst_ref=d.at[k], send_sem=ssem.at[3 * a + k],
                                         recv_sem=rsem.at[3 * a + k], device_id=targets[k], device_id_type=MESH)
            for a, (s, d) in enumerate(zip(s_refs, land_refs)) for k in range(3)]


def _chip_swap_start(sends):
    na = len(sends)

    def body(*refs):
        cps = _chip_swap_copies(refs[:na], refs[na:2 * na], refs[2 * na], refs[2 * na + 1])
        for cp in cps:
            cp.start()
        token = refs[-1]
        token[...] = jnp.zeros(token.shape, token.dtype)

    bufs = [pltpu.HBM(s.shape, s.dtype) for s in sends]
    res = pl.pallas_call(
        body, name="chip_swap_start",
        out_shape=[pltpu.SemaphoreType.DMA((3 * na,)), pltpu.SemaphoreType.DMA((3 * na,))] + bufs + bufs
        + [jax.ShapeDtypeStruct((8, LANES), F32)],
        in_specs=[_HBM] * (2 * na), out_specs=[_SEM, _SEM] + [_HBM] * (2 * na) + [pl.BlockSpec(memory_space=pltpu.VMEM)],
        input_output_aliases={i: 2 + i for i in range(2 * na)},
        compiler_params=pltpu.CompilerParams(has_side_effects=_EFFECT),
    )(*[pltpu.with_memory_space_constraint(s, pltpu.HBM) for s in sends],
      *[pltpu.with_memory_space_constraint(lax.empty(s.shape, s.dtype), pltpu.HBM) for s in sends])
    return res[0], res[1], list(res[2:2 + na]), list(res[2 + na:2 + 2 * na]), res[-1]


def _chip_swap_wait(ssem, rsem, srcs, lands, after):
    na = len(srcs)

    def body(*refs):
        cps = _chip_swap_copies(refs[:na], refs[na:2 * na], refs[2 * na], refs[2 * na + 1])
        for cp in cps:
            cp.wait_send()
            cp.wait_recv()

    bufs = [pltpu.HBM(s.shape, s.dtype) for s in srcs]
    res = pl.pallas_call(
        body, name="chip_swap_wait", out_shape=bufs + bufs,
        in_specs=[_HBM] * (2 * na) + [_SEM, _SEM, _ANY], out_specs=[_HBM] * (2 * na),
        input_output_aliases={i: i for i in range(2 * na)},
        compiler_params=pltpu.CompilerParams(has_side_effects=_EFFECT),
    )(*srcs, *lands, ssem, rsem, after)
    return list(res[na:])


def _pick_rows(r, c):
    if r * c * 4 <= TILE_BYTES or r % 16:
        return r
    best = 16
    for tr in range(16, r, 16):
        if r % tr == 0 and tr * c * 4 <= TILE_BYTES:
            best = tr
    return best


def _pair_sum(name, idx4, g8, recv4):
    _, r, c = g8.shape
    tr = _pick_rows(r, c)

    def body(idx_ref, a_ref, b_ref, o0_ref, o3_ref):
        k = pl.program_id(1)
        s = a_ref[...].astype(F32) + b_ref[...].astype(F32)

        @pl.when(k == 0)
        def _():
            o0_ref[...] = s

        @pl.when(k > 0)
        def _():
            o3_ref[...] = s.astype(BF)

    spec = pltpu.PrefetchScalarGridSpec(
        num_scalar_prefetch=1, grid=(r // tr, 4),
        in_specs=[pl.BlockSpec((None, tr, c), lambda i, k, idx: (idx[k], i, 0)),
                  pl.BlockSpec((None, tr, c), lambda i, k, idx: (k, i, 0))],
        out_specs=[pl.BlockSpec((tr, c), lambda i, k, idx: (i, 0)),
                   pl.BlockSpec((None, tr, c), lambda i, k, idx: (jnp.maximum(k - 1, 0), i, 0))])
    return pl.pallas_call(
        body, name=name, grid_spec=spec,
        out_shape=[jax.ShapeDtypeStruct((r, c), F32), jax.ShapeDtypeStruct((3, r, c), BF)],
        compiler_params=_cparams(("arbitrary", "arbitrary")),
    )(idx4, g8, recv4)


def _adamw(w, g, m, v):
    m = ADAM_B1 * m + (1.0 - ADAM_B1) * g
    v = ADAM_B2 * v + (1.0 - ADAM_B2) * jnp.square(g)
    m_hat = m / (1.0 - ADAM_B1 ** ADAM_STEP)
    v_hat = v / (1.0 - ADAM_B2 ** ADAM_STEP)
    delta = -ADAM_LR * (m_hat / (jnp.sqrt(v_hat) + ADAM_EPS) + ADAM_WD * w)
    return delta, m, v


def _adam_sharded(name, idx1, own, recv, w, m, v):
    r, c = w.shape
    tr = _pick_rows(r, c)
    nj = recv.shape[0]

    def body(idx_ref, p_ref, r_ref, w_ref, m_ref, v_ref, g_out, d_out, m_out, v_out):
        g = p_ref[...].astype(F32)
        for k in range(nj):
            g = g + r_ref[k].astype(F32)
        d, mn, vn = _adamw(w_ref[...], g, m_ref[...], v_ref[...])
        g_out[...] = g
        d_out[...] = d
        m_out[...] = mn
        v_out[...] = vn

    row = pl.BlockSpec((tr, c), lambda i, idx: (i, 0))
    spec = pltpu.PrefetchScalarGridSpec(
        num_scalar_prefetch=1, grid=(r // tr,),
        in_specs=[pl.BlockSpec((None, tr, c), lambda i, idx: (idx[0], i, 0)),
                  pl.BlockSpec((nj, tr, c), lambda i, idx: (0, i, 0)), row, row, row],
        out_specs=[row] * 4)
    return pl.pallas_call(
        body, name=name, grid_spec=spec, out_shape=[jax.ShapeDtypeStruct((r, c), F32)] * 4,
        compiler_params=_cparams(("arbitrary",)),
    )(idx1, own, recv, w, m, v)


def _repl_rows():
    rows, r = {}, 0
    for name, cols in REPL:
        rows[name] = r
        r += REPL_ROWS.get(name, 1) * ((cols + D - 1) // D)
    return rows


def _pack_replicated(grads):
    rows = _repl_rows()
    names = [n for n, _ in REPL]

    def body(*refs):
        o_ref = refs[-1]
        o_ref[...] = jnp.zeros(o_ref.shape, F32)
        for name, ref in zip(names, refs[:-1]):
            r0 = rows[name]
            nr, nc = ref.shape
            if nc <= D:
                o_ref[r0:r0 + nr, 0:nc] = ref[...]
            else:
                for j in range((nc + D - 1) // D):
                    lo, hi = j * D, min(nc, (j + 1) * D)
                    o_ref[r0 + j:r0 + j + 1, 0:hi - lo] = ref[:, lo:hi]

    return pl.pallas_call(body, name="pack_replicated", out_shape=jax.ShapeDtypeStruct((REPL_TOTAL, D), F32),
                          compiler_params=_cparams())(*[grads[n] for n in names])


def _adam_replicated(g8, ws, ms, vs):
    rows = _repl_rows()
    names = [n for n, _ in REPL]
    np_ = len(names)

    def body(*refs):
        g_ref = refs[0]
        w_refs, m_refs, v_refs = refs[1:1 + np_], refs[1 + np_:1 + 2 * np_], refs[1 + 2 * np_:1 + 3 * np_]
        outs = refs[1 + 3 * np_:1 + 7 * np_]
        scr = refs[-1]
        g = g_ref[0]
        for k in range(1, N_DEV):
            g = g + g_ref[k]
        scr[...] = g
        for i, name in enumerate(names):
            r0 = rows[name]
            nr, nc = w_refs[i].shape
            if nc <= D:
                gi = scr[r0:r0 + nr, 0:nc]
            else:
                parts = []
                for j in range((nc + D - 1) // D):
                    lo, hi = j * D, min(nc, (j + 1) * D)
                    parts.append(scr[r0 + j:r0 + j + 1, 0:hi - lo])
                gi = jnp.concatenate(parts, axis=1)
            d, mn, vn = _adamw(w_refs[i][...], gi, m_refs[i][...], v_refs[i][...])
            outs[i][...] = gi
            outs[np_ + i][...] = d
            outs[2 * np_ + i][...] = mn
            outs[3 * np_ + i][...] = vn

    shp = [jax.ShapeDtypeStruct(w.shape, F32) for w in ws]
    res = pl.pallas_call(body, name="adam_replicated", out_shape=shp * 4,
                         scratch_shapes=[pltpu.VMEM((REPL_TOTAL, D), F32)], compiler_params=_cparams(),
                         )(g8, *ws, *ms, *vs)
    return [dict(zip(names, res[k * np_:(k + 1) * np_])) for k in range(4)]


_WEIGHTS = ("attn_pre_norm", "w_in", "hgrn_lb", "hgrn_gnorm", "w_branch_a", "rwkv_mu", "rwkv_w0", "rwkv_w2",
            "rwkv_a0", "rwkv_a2", "rwkv_g2", "rwkv_k_k", "rwkv_k_a", "rwkv_r_k", "rwkv_ln_w", "rwkv_ln_b",
            "w_branch_b", "w_out", "attn_post_norm", "ffn_pre_norm", "w_up", "conv_w", "conv_b", "w_down",
            "ffn_post_norm")
_BIG = ("w_in", "w_up", "w_down", "w_branch_a", "w_branch_b", "w_out")


def _stages():
    one = [D]
    hw = HG_K * HG_PER_STEP
    rw = LANES * RW_PAIRS_PER_STEP
    return dict(
        pre1=_Stage("pre1", _f_pre1, 1, 256, [False], [one], [0], [], [one], [BF]),
        mixers=_Stage("mixers", _f_mixers, 1, RW_CHUNK, [False] * 13, [[D] * 7 + [LANES, LANES]], [0],
                      [(hw, HG_K), (1, RW_COLS), (rw, LANES)], [one, one], [BF, BF]),
        merge=_Stage("merge", _f_merge, 4, 512, [], [[256]] * 4, [29, 33, 0, 0], [], [[256]], [BF]),
        post1=_Stage("post1", _f_post1, 1, 256, [False, False], [one, one], [0, 0], [], [one, one], [F32, BF]),
        conv=_Stage("conv", _f_conv, 1, 128, [False, False], [[DFF, DFF]], [0], [(1, 2 * DFF), (1, 2 * DFF)],
                    [[DFF]], [BF]),
    )


def _cols_to_blocks(w, per):
    return w.reshape(w.shape[0], N_DEV, per).transpose(1, 0, 2)


def _blocks_to_cols(g):
    return g.transpose(1, 0, 2).reshape(g.shape[1], N_DEV * g.shape[2])


def kernel(x, attn_pre_norm, w_in, hgrn_lb, hgrn_gnorm, w_branch_a, rwkv_mu, rwkv_w0, rwkv_w2, rwkv_a0, rwkv_a2, rwkv_g2, rwkv_k_k, rwkv_k_a, rwkv_r_k, rwkv_ln_w, rwkv_ln_b, w_branch_b, w_out, attn_post_norm, ffn_pre_norm, w_up, conv_w, conv_b, w_down, ffn_post_norm, loss_target, m_attn_pre_norm, m_w_in, m_hgrn_lb, m_hgrn_gnorm, m_w_branch_a, m_rwkv_mu, m_rwkv_w0, m_rwkv_w2, m_rwkv_a0, m_rwkv_a2, m_rwkv_g2, m_rwkv_k_k, m_rwkv_k_a, m_rwkv_r_k, m_rwkv_ln_w, m_rwkv_ln_b, m_w_branch_b, m_w_out, m_attn_post_norm, m_ffn_pre_norm, m_w_up, m_conv_w, m_conv_b, m_w_down, m_ffn_post_norm, v_attn_pre_norm, v_w_in, v_hgrn_lb, v_hgrn_gnorm, v_w_branch_a, v_rwkv_mu, v_rwkv_w0, v_rwkv_w2, v_rwkv_a0, v_rwkv_a2, v_rwkv_g2, v_rwkv_k_k, v_rwkv_k_a, v_rwkv_r_k, v_rwkv_ln_w, v_rwkv_ln_b, v_w_branch_b, v_w_out, v_attn_post_norm, v_ffn_pre_norm, v_w_up, v_conv_w, v_conv_b, v_w_down, v_ffn_post_norm):
    w = dict(attn_pre_norm=attn_pre_norm, w_in=w_in, hgrn_lb=hgrn_lb, hgrn_gnorm=hgrn_gnorm, w_branch_a=w_branch_a, rwkv_mu=rwkv_mu, rwkv_w0=rwkv_w0, rwkv_w2=rwkv_w2, rwkv_a0=rwkv_a0, rwkv_a2=rwkv_a2, rwkv_g2=rwkv_g2, rwkv_k_k=rwkv_k_k, rwkv_k_a=rwkv_k_a, rwkv_r_k=rwkv_r_k, rwkv_ln_w=rwkv_ln_w, rwkv_ln_b=rwkv_ln_b, w_branch_b=w_branch_b, w_out=w_out, attn_post_norm=attn_post_norm, ffn_pre_norm=ffn_pre_norm, w_up=w_up, conv_w=conv_w, conv_b=conv_b, w_down=w_down, ffn_post_norm=ffn_post_norm)
    mo = dict(attn_pre_norm=m_attn_pre_norm, w_in=m_w_in, hgrn_lb=m_hgrn_lb, hgrn_gnorm=m_hgrn_gnorm, w_branch_a=m_w_branch_a, rwkv_mu=m_rwkv_mu, rwkv_w0=m_rwkv_w0, rwkv_w2=m_rwkv_w2, rwkv_a0=m_rwkv_a0, rwkv_a2=m_rwkv_a2, rwkv_g2=m_rwkv_g2, rwkv_k_k=m_rwkv_k_k, rwkv_k_a=m_rwkv_k_a, rwkv_r_k=m_rwkv_r_k, rwkv_ln_w=m_rwkv_ln_w, rwkv_ln_b=m_rwkv_ln_b, w_branch_b=m_w_branch_b, w_out=m_w_out, attn_post_norm=m_attn_post_norm, ffn_pre_norm=m_ffn_pre_norm, w_up=m_w_up, conv_w=m_conv_w, conv_b=m_conv_b, w_down=m_w_down, ffn_post_norm=m_ffn_post_norm)
    vo = dict(attn_pre_norm=v_attn_pre_norm, w_in=v_w_in, hgrn_lb=v_hgrn_lb, hgrn_gnorm=v_hgrn_gnorm, w_branch_a=v_w_branch_a, rwkv_mu=v_rwkv_mu, rwkv_w0=v_rwkv_w0, rwkv_w2=v_rwkv_w2, rwkv_a0=v_rwkv_a0, rwkv_a2=v_rwkv_a2, rwkv_g2=v_rwkv_g2, rwkv_k_k=v_rwkv_k_k, rwkv_k_a=v_rwkv_k_a, rwkv_r_k=v_rwkv_r_k, rwkv_ln_w=v_rwkv_ln_w, rwkv_ln_b=v_rwkv_ln_b, w_branch_b=v_w_branch_b, w_out=v_w_out, attn_post_norm=v_attn_post_norm, ffn_pre_norm=v_ffn_pre_norm, w_up=v_w_up, conv_w=v_conv_w, conv_b=v_conv_b, w_down=v_w_down, ffn_post_norm=v_ffn_post_norm)

    t = x.shape[1]
    x2 = x.reshape(t, D)
    tgt = loss_target.reshape(t, D)
    st = _stages()

    me = 4 * lax.axis_index("x") + 2 * lax.axis_index("y") + lax.axis_index("c")
    small = jnp.concatenate([rwkv_w2[0], rwkv_a2[0], rwkv_g2[0]], axis=0).astype(BF)
    g_in, g_small = _all_gather("gather_weights", [w_in[0].astype(BF), small])
    fw_in = _blocks_to_cols(g_in)
    z64 = jnp.zeros((64, D), BF)
    w2p = jnp.concatenate([_blocks_to_cols(g_small[:, 0:64]), z64], axis=0)
    a2p = jnp.concatenate([z64, _blocks_to_cols(g_small[:, 64:128])], axis=0)
    g2f = _blocks_to_cols(g_small[:, 128:256])
    conv_bits = lax.bitcast_convert_type(conv_w[0], BF).reshape(3, 2 * 704)
    late = [w[k][0].astype(BF) for k in _BIG[1:]] + [conv_bits]
    late_gather = _Exchange("gather2", late)
    r_k = rwkv_r_k.reshape(1, D)

    (xn,), _ = _stage_fwd(st["pre1"], t, [attn_pre_norm], [x2])
    z = _mm("in_proj", xn, fw_in, "nn", F32, tm=512, tn=4736, b_outer=True)
    mix_par = [hgrn_lb, hgrn_gnorm, rwkv_mu, rwkv_w0, w2p, rwkv_a0, a2p, g2f, rwkv_k_k, rwkv_k_a,
               rwkv_ln_w, rwkv_ln_b, r_k]
    mix_in = [z]
    (o_a, o_b), mix_saved = _stage_fwd(st["mixers"], t, mix_par, mix_in, hook=late_gather)
    gl = [lax.dynamic_update_slice(g, own[None], (me, 0, 0)) for g, own in zip(late_gather.results, late)]
    fw_up = _blocks_to_cols(gl[0])
    fw_down = gl[1].reshape(DFF, D)
    fw_a, fw_b, fw_out = (g.reshape(D, D) for g in gl[2:5])
    conv_full = _blocks_to_cols(lax.bitcast_convert_type(gl[5].reshape(N_DEV, 3, 704, 2), F32))
    y_a = _mm("branch_a", o_a, fw_a, "nn")
    y_b = _mm("branch_b", o_b, fw_b, "nn")
    (merged,), _ = _stage_fwd(st["merge"], t, [], [z, z, y_a, y_b])
    mix = _mm("out_proj", merged, fw_out, "nn")
    (h1, xn2), _ = _stage_fwd(st["post1"], t, [attn_post_norm, ffn_pre_norm], [x2, mix])
    hu = _mm("up_proj", xn2, fw_up, "nn", F32, tm=1024, tn=1408)
    conv_par = [conv_full, conv_b]
    (act,), conv_saved = _stage_fwd(st["conv"], t, conv_par, [hu])
    ff = _mm("down_proj", act, fw_down, "nn")

    loss_acc, d_ffn_post, dh1, dff = _loss_stage(t, ffn_post_norm, h1, ff, tgt)
    dact = _mm("d_act", dff, fw_down, "nt", F32, tm=1024, tn=1408)
    dw_down = _mm("dw_down", act, dff, "tn", BF, tm=1408, tn=512)
    (dcw, dcb), (dhu,) = _stage_bwd(st["conv"], t, conv_par, [hu], conv_saved, [[dact]], [BF])
    dxn2 = _mm("d_xn2", dhu, fw_up, "nt", F32, tm=1024, tn=256)
    dw_up = _mm("dw_up", xn2, dhu, "tn", BF, tm=1024, tn=1408)
    (d_post, d_pre2), (dx_a, dmix) = _stage_bwd(st["post1"], t, [attn_post_norm, ffn_pre_norm], [x2, mix], [],
                                                 [[dh1], [dxn2]], [F32, BF])
    dmerged = _mm("d_merged", dmix, fw_out, "nt")
    dw_out = _mm("dw_out", merged, dmix, "tn", BF)
    _, (dga, dgb, dy_a, dy_b) = _stage_bwd(st["merge"], t, [], [z, z, y_a, y_b], [], [[dmerged]], [BF, BF, BF, BF])
    do_a = _mm("d_oa", dy_a, fw_a, "nt")
    dw_a = _mm("dw_a", o_a, dy_a, "tn", BF)
    do_b = _mm("d_ob", dy_b, fw_b, "nt")
    dw_b = _mm("dw_b", o_b, dy_b, "tn", BF)
    early = [_cols_to_blocks(dw_up, 704), dw_down.reshape(N_DEV, 352, D), dw_a.reshape(N_DEV, 128, D),
             dw_b.reshape(N_DEV, 128, D), dw_out.reshape(N_DEV, 128, D), _cols_to_blocks(dcw.astype(BF), 704)]
    early_scatter = _Exchange("scatter", early)
    mix_dp, dz_hr = _stage_bwd(st["mixers"], t, mix_par, mix_in, mix_saved, [[do_a], [do_b]], [BF],
                               hook=early_scatter)
    d_lb, d_gn, d_mu, d_w0, d_w2p, d_a0, d_a2p, d_g2, d_kk, d_ka, d_lnw, d_lnb, d_rk = mix_dp
    dz = jnp.concatenate(dz_hr + [dga, dgb], axis=1)
    dw_in = _mm("dw_in", xn, dz, "tn", BF, tm=1024, tn=256)

    ax, ay, ac = lax.axis_index("x"), lax.axis_index("y"), lax.axis_index("c")
    idx4 = jnp.stack([4 * cx + 2 * cy + ac for cx, cy in ((ax, ay), (1 - ax, ay), (ax, 1 - ay), (1 - ax, 1 - ay))])
    idx4 = idx4.astype(jnp.int32)
    idx_me, idx_0 = idx4[0:1], jnp.zeros((1,), jnp.int32)
    d_small = jnp.concatenate([d_w2p[:64], d_a2p[64:], d_g2], axis=0).astype(BF)
    g8s = [_cols_to_blocks(dw_in, 1184), _cols_to_blocks(d_small, LANES)]
    recv4s = _reduce_pair(g8s)
    sums = [_pair_sum("pair_sum_" + n, idx4, g, r) for n, g, r in zip(("w_in", "small"), g8s, recv4s)]
    swap_ssem, swap_rsem, swap_srcs, swap_lands, token = _chip_swap_start([s[1] for s in sums])
    dxn = _mm("d_xn", dz, fw_in, "nt", F32, tm=512, tn=512, b_outer=True, token=token)
    (d_pre1,), (dx_b,) = _stage_bwd(st["pre1"], t, [attn_pre_norm], [x2], [], [[dxn]], [F32])
    grad_x = (dx_a + dx_b).reshape(x.shape)
    loss = lax.psum(loss_acc[0, 0], ("x", "y", "c"))

    rg = dict(attn_pre_norm=d_pre1, hgrn_lb=d_lb, hgrn_gnorm=d_gn, rwkv_mu=d_mu, rwkv_w0=d_w0, rwkv_a0=d_a0,
              rwkv_k_k=d_kk, rwkv_k_a=d_ka, rwkv_r_k=d_rk, rwkv_ln_w=d_lnw, rwkv_ln_b=d_lnb, attn_post_norm=d_post,
              ffn_pre_norm=d_pre2, conv_b=dcb, ffn_post_norm=d_ffn_post)
    (g8,) = _all_gather("gather_small_grads", [_pack_replicated(rg)])
    rnames = [n for n, _ in REPL]
    flat = lambda src: [src[n].reshape(1, D) if n == "rwkv_r_k" else src[n] for n in rnames]
    rp_out = _adam_replicated(g8, flat(w), flat(mo), flat(vo))
    recv3s = _chip_swap_wait(swap_ssem, swap_rsem, swap_srcs, swap_lands, rp_out[0]["attn_pre_norm"])
    for kind in range(4):
        rp_out[kind]["rwkv_r_k"] = rp_out[kind]["rwkv_r_k"].reshape(rwkv_r_k.shape)

    def small_of(src):
        return jnp.concatenate([src["rwkv_w2"][0], src["rwkv_a2"][0], src["rwkv_g2"][0]], axis=0)

    sh_out = [dict() for _ in range(4)]
    res = _adam_sharded("adam_w_in", idx_0, sums[0][0][None], recv3s[0], *[src["w_in"][0] for src in (w, mo, vo)])
    res_s = _adam_sharded("adam_small", idx_0, sums[1][0][None], recv3s[1], *[small_of(src) for src in (w, mo, vo)])
    for kind in range(4):
        sh_out[kind]["w_in"] = res[kind][None]
        sh_out[kind]["rwkv_w2"] = res_s[kind][0:64][None]
        sh_out[kind]["rwkv_a2"] = res_s[kind][64:128][None]
        sh_out[kind]["rwkv_g2"] = res_s[kind][128:256][None]
    for n, own, recv in zip(_BIG[1:] + ("conv_w",), early, early_scatter.results):
        res = _adam_sharded("adam_" + n, idx_me, own, recv, *[src[n][0] for src in (w, mo, vo)])
        for kind in range(4):
            sh_out[kind][n] = res[kind][None]

    outs = [loss, grad_x]
    for kind in range(4):
        for name in _WEIGHTS:
            outs.append(sh_out[kind][name] if name in sh_out[kind] else rp_out[kind][name])
    return tuple(outs)
```

```python
import functools

import jax
import jax.numpy as jnp
from jax import lax
from jax.experimental import pallas as pl
from jax.experimental.pallas import tpu as pltpu

F32 = jnp.float32
BF = jnp.bfloat16
MESH = pl.DeviceIdType.MESH

D = 1024
HG_HEADS = 8
HG_K = 128
HG_CHUNK = 32
HG_SCALE = HG_K ** -0.5
HG_PER_STEP = 8
RW_HEADS = 16
RW_N = 64
RW_CHUNK = 64
RW_PAIRS_PER_STEP = 8
DFF = 2816
IN_COLS = 9472
RW_COLS = 3328
EPS = 1e-6
GN_EPS = 1e-5 * RW_N
ADAM_LR = 0.001
ADAM_B1 = 0.9
ADAM_B2 = 0.999
ADAM_EPS = 1e-08
ADAM_WD = 0.01
ADAM_STEP = 10
N_DEV = 8
LANES = 128
VMEM_LIMIT = 56 * 1024 * 1024
TILE_BYTES = 1280 * 1024

REPL = (("attn_pre_norm", 1024), ("hgrn_lb", 1024), ("hgrn_gnorm", 1024), ("rwkv_mu", 3328), ("rwkv_w0", 1024),
        ("rwkv_a0", 1024), ("rwkv_k_k", 1024), ("rwkv_k_a", 1024), ("rwkv_r_k", 1024), ("rwkv_ln_w", 1024),
        ("rwkv_ln_b", 1024), ("attn_post_norm", 1024), ("ffn_pre_norm", 1024), ("conv_b", 5632), ("ffn_post_norm", 1024))
REPL_ROWS = {"hgrn_lb": 2}
REPL_TOTAL = 32


def _cparams(sem=None, **kw):
    return pltpu.CompilerParams(dimension_semantics=sem, vmem_limit_bytes=VMEM_LIMIT, **kw)


_DN = {"nn": ((1,), (0,)), "nt": ((1,), (1,)), "tn": ((0,), (0,))}


def _raw_dot(a, b, mode):
    return lax.dot_general(a.astype(BF), b.astype(BF), (_DN[mode], ((), ())), preferred_element_type=F32)


@functools.partial(jax.custom_vjp, nondiff_argnums=(2,))
def _dot(a, b, mode):
    return _raw_dot(a, b, mode)


def _dot_fwd(a, b, mode):
    return _raw_dot(a, b, mode), (a, b)


def _dot_bwd(mode, res, g):
    a, b = res
    if mode == "nn":
        return _dot(g, b, "nt"), _dot(a, g, "tn")
    if mode == "nt":
        return _dot(g, b, "nn"), _dot(g, a, "tn")
    return _dot(b, g, "nt"), _dot(a, g, "nn")


_dot.defvjp(_dot_fwd, _dot_bwd)


def _bf_pieces(x, n):
    out, r = [], x
    for i in range(n):
        p = r.astype(BF)
        out.append(p)
        if i + 1 < n:
            r = r - p.astype(F32)
    return out


def _raw_split_dot(x, e, mode, n, x_left):
    eb = e.astype(BF)
    acc = None
    for p in _bf_pieces(x, n):
        ops = (p, eb) if x_left else (eb, p)
        t = lax.dot_general(*ops, (_DN[mode], ((), ())), preferred_element_type=F32)
        acc = t if acc is None else acc + t
    return acc


def _raw_headsum(x):
    t = x.shape[0]
    i = lax.broadcasted_iota(jnp.int32, (LANES, LANES), 0)
    j = lax.broadcasted_iota(jnp.int32, (LANES, LANES), 1)
    same = jnp.where((i >= RW_N) == (j >= RW_N), 1.0, 0.0).astype(F32)
    groups = x.shape[1] // LANES
    rows = jnp.concatenate([x[:, q * LANES:(q + 1) * LANES] for q in range(groups)], axis=0)
    s = _raw_split_dot(rows, same, "nn", 2, True)
    return jnp.concatenate([s[q * t:(q + 1) * t] for q in range(groups)], axis=1)


@jax.custom_vjp
def _headsum(x):
    return _raw_headsum(x)


def _headsum_fwd(x):
    return _raw_headsum(x), None


def _headsum_bwd(_, g):
    return (_raw_headsum(g),)


_headsum.defvjp(_headsum_fwd, _headsum_bwd)


@functools.partial(jax.custom_vjp, nondiff_argnums=(2,))
def _tdot(tri, x, n):
    return _raw_split_dot(x, tri, "nn", n, False)


def _tdot_fwd(tri, x, n):
    return _raw_split_dot(x, tri, "nn", n, False), tri


def _tdot_bwd(n, tri, g):
    return jnp.zeros_like(tri), _raw_split_dot(g, tri, "tn", n, False)


_tdot.defvjp(_tdot_fwd, _tdot_bwd)


def _row(x, i):
    r = lax.broadcasted_iota(jnp.int32, x.shape, 0)
    return jnp.sum(jnp.where(r == i, x, 0.0), axis=0, keepdims=True)


def _shift_down(x, prev):
    t = x.shape[0]

    @jax.custom_vjp
    def sh(x, prev):
        r = lax.broadcasted_iota(jnp.int32, x.shape, 0)
        return jnp.where(r == 0, prev, pltpu.roll(x, 1, 0))

    def fwd(x, prev):
        return sh(x, prev), None

    def bwd(_, g):
        r = lax.broadcasted_iota(jnp.int32, g.shape, 0)
        dx = jnp.where(r == t - 1, 0.0, pltpu.roll(g, t - 1, 0))
        return dx, jnp.sum(jnp.where(r == 0, g, 0.0), axis=0, keepdims=True)

    sh.defvjp(fwd, bwd)
    return sh(x, prev)


def _sigmoid(x):
    return jax.nn.sigmoid(x)


def _silu(x):
    return x * jax.nn.sigmoid(x)


def _softplus(x):
    return jnp.maximum(x, 0.0) + jnp.log(1.0 + jnp.exp(-jnp.abs(x)))


def _rms(x, g):
    return (x * lax.rsqrt(jnp.mean(x * x, axis=-1, keepdims=True) + EPS)) * g


def _tril(c):
    r = lax.broadcasted_iota(jnp.int32, (c, c), 0)
    cc = lax.broadcasted_iota(jnp.int32, (c, c), 1)
    return cc <= r


def _f_pre1(ps, xs, cs):
    return [_rms(xs[0], ps[0])], []


def _f_hgrn(ps, xs, cs):
    lbraw, gn = ps
    hq, hf, hi, hg = xs
    hd = range(HG_PER_STEP)
    st = [cs[0][p * HG_K:(p + 1) * HG_K] for p in hd]
    l0, l1 = _row(lbraw, 0), _row(lbraw, 1)
    m = jnp.maximum(l0, l1)
    e0, e1 = jnp.exp(l0 - m), jnp.exp(l1 - m)
    lb = e0 / (e0 + e1)
    q = _silu(hq) * HG_SCALE
    f = lb + (1.0 - lb) * _sigmoid(hf)
    kh = 1.0 - f
    gl = jnp.log(f)
    c = HG_CHUNK
    low = _tril(c)
    tri = jnp.where(low, 1.0, 0.0).astype(F32)
    outs = []
    for i in range(hq.shape[0] // c):
        rows = slice(i * c, (i + 1) * c)
        b = _tdot(tri, gl[rows], 3)
        bref = _row(b, c // 2 - 1)
        blast = _row(b, c - 1)
        qi = q[rows] * jnp.exp(b - bref)
        ki = kh[rows] * jnp.exp(bref - b)
        qd = q[rows] * jnp.exp(b)
        kd = kh[rows] * jnp.exp(blast - b)
        dec = jnp.exp(blast)
        sl = [slice(p * HG_K, (p + 1) * HG_K) for p in hd]
        sc = [jnp.where(low, _dot(qi[:, sl[p]], ki[:, sl[p]], "nt"), 0.0) for p in hd]
        o = [_dot(sc[p], hi[rows, sl[p]], "nn") + _dot(qd[:, sl[p]], st[p], "nt") for p in hd]
        u = [_dot(hi[rows, sl[p]], kd[:, sl[p]], "tn") for p in hd]
        st = [dec[:, sl[p]] * st[p] + u[p] for p in hd]
        outs.append(jnp.concatenate(o, axis=1) if len(o) > 1 else o[0])
    o = outs[0] if len(outs) == 1 else jnp.concatenate(outs, axis=0)
    on = []
    for p in hd:
        op = o[:, p * HG_K:(p + 1) * HG_K]
        on.append(op * lax.rsqrt(jnp.mean(op * op, axis=-1, keepdims=True) + EPS))
    o = jnp.concatenate(on, axis=1) if len(on) > 1 else on[0]
    o = o * gn
    return [o * _silu(hg)], [jnp.concatenate(st, axis=0) if len(st) > 1 else st[0]]


_RW_OFFS = (0, 1024, 2048, 3072, 3200, 3328)


def _f_rwpre(ps, xs, cs):
    mu, w0, w2p, a0, a2p, g2, k_k, k_a = ps
    (prev,) = cs
    t = xs[0].shape[0]
    zs = []
    for i, z in enumerate(xs):
        lo, hi = _RW_OFFS[i], _RW_OFFS[i + 1]
        zs.append(z + mu[:, lo:hi] * (_shift_down(z, prev[:, lo:hi]) - z))
    rr, kr, vr, wa, gz = zs
    w_log = -_softplus(-(w0 + _dot(jnp.tanh(wa), w2p, "nn"))) - 0.5
    lw = -jnp.exp(w_log)
    a = _sigmoid(a0 + _dot(wa, a2p, "nn"))
    g = _dot(_sigmoid(gz), g2, "nn")
    kkr = kr * k_k
    kk = kkr / jnp.maximum(jnp.sqrt(_headsum(kkr * kkr)), 1e-12)
    k2 = kr * (1.0 + (a - 1.0) * k_a)
    newprev = jnp.concatenate([_row(z, t - 1) for z in xs], axis=1)
    return [rr, lw, k2, vr, -kk, kk * a, g], [newprev]


def _raw_inverses(ls):
    n = ls[0].shape[0]
    r = lax.broadcasted_iota(jnp.int32, (n, n), 0)
    c = lax.broadcasted_iota(jnp.int32, (n, n), 1)
    eye = jnp.where(r == c, 1.0, 0.0).astype(F32)
    tinv = [eye + l for l in ls]
    pw = ls
    for _ in range(5):
        pw = [_raw_dot(p, p, "nn") for p in pw]
        tinv = [t + _raw_dot(t, p, "nn") for t, p in zip(tinv, pw)]
    return tinv


@jax.custom_vjp
def _unit_lower_inverses(ls):
    return _raw_inverses(ls)


def _inverses_fwd(ls):
    tinv = _raw_inverses(ls)
    return tinv, tinv


def _inverses_bwd(tinv, gs):
    return ([_raw_dot(_raw_dot(t, g, "tn"), t, "nt") for t, g in zip(tinv, gs)],)


_unit_lower_inverses.defvjp(_inverses_fwd, _inverses_bwd)


def _f_rwscan(ps, xs, cs):
    npair = RW_PAIRS_PER_STEP
    pr = range(npair)
    r, lw, k, v, av, bv = [[x[:, p * LANES:(p + 1) * LANES] for p in pr] for x in xs]
    sv = [cs[0][p * LANES:(p + 1) * LANES] for p in pr]
    c = RW_CHUNK
    n = 2 * c
    tri = jnp.where(_tril(c), 1.0, 0.0).astype(F32)
    cl = [_tdot(tri, lw[p], 3) for p in pr]
    cl_last = [_row(cl[p], c - 1) for p in pr]
    lane = lax.broadcasted_iota(jnp.int32, (c, LANES), 1)
    h0 = lane < RW_N

    def stack(x):
        return jnp.concatenate([jnp.where(h0, x, 0.0), jnp.where(h0, 0.0, x)], axis=0)

    am = [stack(av[p] * jnp.exp(cl[p] - lw[p])) for p in pr]
    bm = [stack(bv[p] * jnp.exp(-cl[p])) for p in pr]
    km = [stack(k[p] * jnp.exp(-cl[p])) for p in pr]
    rm = [stack(r[p] * jnp.exp(cl[p])) for p in pr]
    vm = [stack(v[p]) for p in pr]
    rn = lax.broadcasted_iota(jnp.int32, (n, n), 0)
    cn = lax.broadcasted_iota(jnp.int32, (n, n), 1)
    blk = (rn >= c) == (cn >= c)
    strict = blk & (cn < rn)
    incl = blk & (cn <= rn)
    lab = [jnp.where(strict, _dot(am[p], bm[p], "nt"), 0.0) for p in pr]
    lak = [jnp.where(strict, _dot(am[p], km[p], "nt"), 0.0) for p in pr]
    wrb = [jnp.where(incl, _dot(rm[p], bm[p], "nt"), 0.0) for p in pr]
    wrk = [jnp.where(incl, _dot(rm[p], km[p], "nt"), 0.0) for p in pr]
    tinv = _unit_lower_inverses(lab)
    rhs = [_dot(am[p], sv[p], "nt") + _dot(lak[p], vm[p], "nn") for p in pr]
    um = [_dot(tinv[p], rhs[p], "nn") for p in pr]
    ym = [_dot(rm[p], sv[p], "nt") + _dot(wrb[p], um[p], "nn") + _dot(wrk[p], vm[p], "nn") for p in pr]
    sn = [(sv[p] + _dot(um[p], bm[p], "tn") + _dot(vm[p], km[p], "tn")) * jnp.exp(cl_last[p]) for p in pr]
    ys = [ym[p][:c] + ym[p][c:] for p in pr]
    return [jnp.concatenate(ys, axis=1)], [jnp.concatenate(sn, axis=0)]


def _f_mixers(ps, xs, cs):
    oa, st = _f_hgrn(ps[:2], xs[:4], cs[:1])
    (r, lw, k, v, av, bv, g), prev = _f_rwpre(ps[2:10], xs[4:], cs[1:2])
    y, sv = _f_rwscan([], [r, lw, k, v, av, bv], cs[2:])
    ob, _ = _f_rwpost(ps[10:], y + [r, k, v, g], [])
    return oa + ob, st + prev + sv


def _f_rwpost(ps, xs, cs):
    ln_w, ln_b, r_k = ps
    y, r, k, v, g = xs
    inv_n = 1.0 / RW_N
    yc = y - _headsum(y) * inv_n
    var = _headsum(yc * yc) * inv_n
    yn = yc * lax.rsqrt(var + GN_EPS)
    yn = yn * ln_w + ln_b
    bonus = _headsum(r * k * r_k) * v
    return [(yn + bonus) * g], []


def _f_merge(ps, xs, cs):
    ga, gb, ya, yb = xs
    return [_sigmoid(ga) * ya + _sigmoid(gb) * yb], []


def _f_post1(ps, xs, cs):
    x, mix = xs
    h1 = x + _rms(mix, ps[0])
    return [h1, _rms(h1, ps[1])], []


def _f_conv(ps, xs, cs):
    cw, cb = ps
    p1, p2 = cs
    w0, w1, w2 = _row(cw, 0), _row(cw, 1), _row(cw, 2)
    t = xs[0].shape[0]
    hc = []
    for i, x in enumerate(xs):
        sl = slice(i * DFF, (i + 1) * DFF)
        s1 = _shift_down(x, p1[:, sl])
        s2 = _shift_down(s1, p2[:, sl])
        hc.append(cb[:, sl] + w0[:, sl] * s2 + w1[:, sl] * s1 + w2[:, sl] * x)
    n1 = jnp.concatenate([_row(x, t - 1) for x in xs], axis=1)
    n2 = jnp.concatenate([_row(x, t - 2) for x in xs], axis=1)
    return [_silu(hc[0]) * hc[1]], [n1, n2]


class _Stage:
    def __init__(self, name, f, g, tm, par_per_g, in_pieces, in_offs, carry_shapes, out_pieces, out_dtypes):
        self.name, self.f, self.g, self.tm = name, f, g, tm
        self.par_per_g, self.in_pieces, self.in_offs = par_per_g, in_pieces, in_offs
        self.carry_shapes, self.out_pieces, self.out_dtypes = carry_shapes, out_pieces, out_dtypes


def _par_spec(arr, per_g, g):
    r, c = arr.shape
    if per_g:
        return pl.BlockSpec((r, c // g), lambda gi, ni: (0, gi))
    return pl.BlockSpec((r, c), lambda gi, ni: (0, 0))


def _row_spec(tm, width, off, n, rev):
    if rev:
        return pl.BlockSpec((tm, width), lambda gi, ni: (n - 1 - ni, off + gi))
    return pl.BlockSpec((tm, width), lambda gi, ni: (ni, off + gi))


def _carry_spec(shape, n, rev):
    if rev:
        return pl.BlockSpec((None, None) + shape, lambda gi, ni: (gi, n - 1 - ni, 0, 0))
    return pl.BlockSpec((None, None) + shape, lambda gi, ni: (gi, ni, 0, 0))


def _load_pieces(refs, pieces_list):
    out = []
    for ref, pieces in zip(refs, pieces_list):
        o = 0
        for w in pieces:
            out.append(ref[:, o:o + w].astype(F32))
            o += w
    return out


def _store_pieces(refs, pieces_list, vals):
    k = 0
    for ref, pieces in zip(refs, pieces_list):
        o = 0
        for w in pieces:
            ref[:, o:o + w] = vals[k].astype(ref.dtype)
            k += 1
            o += w


_ANY = pl.BlockSpec(memory_space=pl.ANY)


class _Exchange:
    def __init__(self, kind, arrs):
        self.kind, self.arrs, self.results = kind, list(arrs), None
        if kind == "scatter":
            self.out_shape = [jax.ShapeDtypeStruct((N_DEV - 1,) + a.shape[1:], a.dtype) for a in self.arrs]
        else:
            self.out_shape = [jax.ShapeDtypeStruct((N_DEV,) + a.shape, a.dtype) for a in self.arrs]
        self.nsem = (N_DEV - 1) * len(self.arrs)

    def copies(self, in_refs, out_refs, ssem, rsem):
        x, y, c = lax.axis_index("x"), lax.axis_index("y"), lax.axis_index("c")
        me = 4 * x + 2 * y + c
        cps = []
        for a, (i_ref, o_ref) in enumerate(zip(in_refs, out_refs)):
            for j in range(1, N_DEV):
                px = 1 - x if j & 4 else x
                py = 1 - y if j & 2 else y
                pc = 1 - c if j & 1 else c
                if self.kind == "gather":
                    src, dst = i_ref, o_ref.at[me]
                else:
                    src, dst = i_ref.at[4 * px + 2 * py + pc], o_ref.at[j - 1]
                s = (N_DEV - 1) * a + j - 1
                cps.append(pltpu.make_async_remote_copy(src_ref=src, dst_ref=dst, send_sem=ssem.at[s],
                                                        recv_sem=rsem.at[s], device_id=(px, py, pc),
                                                        device_id_type=MESH))
        return cps

    def run(self, first, mid, last, in_refs, out_refs, ssem, rsem):
        if self.kind == "gather2":
            return self.run_two_level(first, mid, last, in_refs, out_refs, ssem, rsem)

        @pl.when(first)
        def _():
            for cp in self.copies(in_refs, out_refs, ssem, rsem):
                cp.start()

        @pl.when(last)
        def _():
            for cp in self.copies(in_refs, out_refs, ssem, rsem):
                cp.wait()

    def run_two_level(self, first, mid, last, in_refs, out_refs, ssem, rsem):
        x, y, c = lax.axis_index("x"), lax.axis_index("y"), lax.axis_index("c")
        me, sibling = (x, y, c), (x, y, 1 - c)
        chips = [(1 - x, y), (x, 1 - y), (1 - x, 1 - y)]
        arrs = range(len(in_refs))

        def copy(a, k, block, to, src=None):
            dst = out_refs[a].at[4 * block[0] + 2 * block[1] + block[2]]
            return pltpu.make_async_remote_copy(
                src_ref=dst if src is None else src, dst_ref=dst, send_sem=ssem.at[7 * a + k],
                recv_sem=rsem.at[7 * a + k], device_id=to, device_id_type=MESH)

        def firsts(a):
            return [copy(a, 0, me, sibling, src=in_refs[a])] + [
                copy(a, 1 + j, me, (*chip, c), src=in_refs[a]) for j, chip in enumerate(chips)]

        def passed(a):
            return [copy(a, 4 + j, (*chip, c), sibling) for j, chip in enumerate(chips)]

        @pl.when(first)
        def _():
            for a in arrs:
                for cp in firsts(a):
                    cp.start()

        @pl.when(mid)
        def _():
            for j, chip in enumerate(chips):
                for a in arrs:
                    copy(a, 1 + j, (*chip, c), me).wait_recv()
                    passed(a)[j].start()

        @pl.when(last)
        def _():
            for a in arrs:
                copy(a, 0, sibling, me).wait_recv()
                for j, chip in enumerate(chips):
                    copy(a, 4 + j, (*chip, 1 - c), me).wait_recv()
                for cp in firsts(a) + passed(a):
                    cp.wait_send()


def _hook_specs(hook):
    if hook is None:
        return [], [], [], []
    na = len(hook.arrs)
    sems = [pltpu.SemaphoreType.DMA((hook.nsem,)), pltpu.SemaphoreType.DMA((hook.nsem,))]
    return [_ANY] * na, [_ANY] * na, hook.out_shape, sems


def _stage_fwd(st, t, params, inputs, hook=None):
    g, tm = st.g, min(st.tm, t)
    n = t // tm
    npar, nin, ncar, nout = len(params), len(inputs), len(st.carry_shapes), len(st.out_pieces)
    h_in, h_out, h_shape, h_sems = _hook_specs(hook)
    nh = len(h_in)

    def body(*refs):
        p_refs = refs[:npar]
        x_refs = refs[npar:npar + nin]
        hi_refs = refs[npar + nin:npar + nin + nh]
        o = npar + nin + nh
        o_refs = refs[o:o + nout]
        s_refs = refs[o + nout:o + nout + ncar]
        ho_refs = refs[o + nout + ncar:o + nout + ncar + nh]
        c_scr = refs[o + nout + ncar + nh:o + nout + ncar + nh + ncar]
        gi, ni = pl.program_id(0), pl.program_id(1)
        if hook is not None:
            step = gi * n + ni
            hook.run(step == 0, step == (4 * g * n) // 5, step == g * n - 1, hi_refs, ho_refs, *refs[-2:])

        @pl.when(ni == 0)
        def _():
            for c in c_scr:
                c[...] = jnp.zeros(c.shape, F32)

        ps = [r[...].astype(F32) for r in p_refs]
        xs = _load_pieces(x_refs, st.in_pieces)
        cs = [c[...] for c in c_scr]
        for s, c in zip(s_refs, cs):
            s[...] = c
        outs, ncs = st.f(ps, xs, cs)
        _store_pieces(o_refs, st.out_pieces, outs)
        for c, v in zip(c_scr, ncs):
            c[...] = v

    in_specs = [_par_spec(p, pg, g) for p, pg in zip(params, st.par_per_g)]
    in_specs += [_row_spec(tm, sum(pc), off, n, False) for pc, off in zip(st.in_pieces, st.in_offs)]
    out_specs = [_row_spec(tm, sum(pc), 0, n, False) for pc in st.out_pieces]
    out_specs += [_carry_spec(s, n, False) for s in st.carry_shapes]
    out_shape = [jax.ShapeDtypeStruct((t, g * sum(pc)), dt) for pc, dt in zip(st.out_pieces, st.out_dtypes)]
    out_shape += [jax.ShapeDtypeStruct((g, n) + s, F32) for s in st.carry_shapes]
    res = pl.pallas_call(
        body, name=st.name + "_fwd", grid=(g, n), in_specs=in_specs + h_in, out_specs=out_specs + h_out,
        out_shape=out_shape + h_shape,
        scratch_shapes=[pltpu.VMEM(s, F32) for s in st.carry_shapes] + h_sems,
        compiler_params=_cparams(("arbitrary", "arbitrary")),
    )(*params, *inputs, *(hook.arrs if hook else []))
    if hook is not None:
        hook.results = list(res[nout + ncar:])
    return list(res[:nout]), list(res[nout:nout + ncar])


def _stage_bwd(st, t, params, inputs, saved, douts, dx_dtypes, hook=None):
    g, tm = st.g, min(st.tm, t)
    n = t // tm
    npar, nin, ncar = len(params), len(inputs), len(st.carry_shapes)
    flat_d = [d for ds in douts for d in ds]
    nd = len(flat_d)
    dx_idx = [i for i, dt in enumerate(dx_dtypes) if dt is not None]
    h_in, h_out, h_shape, h_sems = _hook_specs(hook)
    nh = len(h_in)

    def body(*refs):
        p_refs = refs[:npar]
        x_refs = refs[npar:npar + nin]
        s_refs = refs[npar + nin:npar + nin + ncar]
        d_refs = refs[npar + nin + ncar:npar + nin + ncar + nd]
        hi_refs = refs[npar + nin + ncar + nd:npar + nin + ncar + nd + nh]
        o = npar + nin + ncar + nd + nh
        dp_refs = refs[o:o + npar]
        dx_refs = refs[o + npar:o + npar + len(dx_idx)]
        ho_refs = refs[o + npar + len(dx_idx):o + npar + len(dx_idx) + nh]
        dc_scr = refs[o + npar + len(dx_idx) + nh:o + npar + len(dx_idx) + nh + ncar]
        gi, ni = pl.program_id(0), pl.program_id(1)
        if hook is not None:
            step = gi * n + ni
            hook.run(step == 0, step == (4 * g * n) // 5, step == g * n - 1, hi_refs, ho_refs, *refs[-2:])

        @pl.when(ni == 0)
        def _():
            for c in dc_scr:
                c[...] = jnp.zeros(c.shape, F32)

        ps = [r[...].astype(F32) for r in p_refs]
        xs = _load_pieces(x_refs, st.in_pieces)
        cs = [s[...] for s in s_refs]
        dys = []
        k = 0
        for ds, pieces in zip(douts, st.out_pieces):
            acc = _load_pieces([d_refs[k]], [pieces])
            for j in range(1, len(ds)):
                more = _load_pieces([d_refs[k + j]], [pieces])
                acc = [a + b for a, b in zip(acc, more)]
            dys += acc
            k += len(ds)
        _, vjp = jax.vjp(st.f, ps, xs, cs)
        dps, dxs, dcs = vjp((dys, [c[...] for c in dc_scr]))
        k = 0
        per_in = []
        for pieces in st.in_pieces:
            per_in.append(dxs[k:k + len(pieces)])
            k += len(pieces)
        for ref, i in zip(dx_refs, dx_idx):
            _store_pieces([ref], [st.in_pieces[i]], per_in[i])
        for c, v in zip(dc_scr, dcs):
            c[...] = v
        for ref, dp, pg in zip(dp_refs, dps, st.par_per_g):
            first = (ni == 0) if pg else ((ni == 0) & (gi == 0))

            @pl.when(first)
            def _():
                ref[...] = jnp.zeros(ref.shape, F32)

            ref[...] += dp

    in_specs = [_par_spec(p, pg, g) for p, pg in zip(params, st.par_per_g)]
    in_specs += [_row_spec(tm, sum(pc), off, n, True) for pc, off in zip(st.in_pieces, st.in_offs)]
    in_specs += [_carry_spec(s, n, True) for s in st.carry_shapes]
    for ds, pc in zip(douts, st.out_pieces):
        in_specs += [_row_spec(tm, sum(pc), 0, n, True) for _ in ds]
    out_specs = [_par_spec(p, pg, g) for p, pg in zip(params, st.par_per_g)]
    out_specs += [_row_spec(tm, sum(st.in_pieces[i]), 0, n, True) for i in dx_idx]
    out_shape = [jax.ShapeDtypeStruct(p.shape, F32) for p in params]
    out_shape += [jax.ShapeDtypeStruct((t, g * sum(st.in_pieces[i])), dx_dtypes[i]) for i in dx_idx]
    res = pl.pallas_call(
        body, name=st.name + "_bwd", grid=(g, n), in_specs=in_specs + h_in, out_specs=out_specs + h_out,
        out_shape=out_shape + h_shape,
        scratch_shapes=[pltpu.VMEM(s, F32) for s in st.carry_shapes] + h_sems,
        compiler_params=_cparams(("arbitrary", "arbitrary")),
    )(*params, *inputs, *saved, *flat_d, *(hook.arrs if hook else []))
    if hook is not None:
        hook.results = list(res[npar + len(dx_idx):])
    return list(res[:npar]), list(res[npar:npar + len(dx_idx)])


def _pick(n, cap):
    if n <= cap:
        return n
    best = LANES
    for k in range(1, n // LANES + 1):
        if (n // LANES) % k == 0 and k * LANES <= cap:
            best = k * LANES
    return best


def _mm(name, a, b, mode, out_dtype=F32, tm=1024, tn=512, b_outer=False, token=None):
    m = a.shape[1] if mode == "tn" else a.shape[0]
    k = a.shape[0] if mode == "tn" else a.shape[1]
    n = b.shape[0] if mode == "nt" else b.shape[1]
    tm, tn = _pick(m, tm), _pick(n, tn)
    if b_outer:
        grid = (n // tn, m // tm)
        ij = lambda p, q: (q, p)
    else:
        grid = (m // tm, n // tn)
        ij = lambda p, q: (p, q)
    extra = [] if token is None else [token]

    def body(*refs):
        a_ref, b_ref, o_ref = refs[0], refs[1], refs[-1]
        o_ref[...] = _raw_dot(a_ref[...], b_ref[...], mode).astype(o_ref.dtype)

    if mode == "tn":
        a_spec = pl.BlockSpec((k, tm), lambda p, q: (0, ij(p, q)[0]))
    else:
        a_spec = pl.BlockSpec((tm, k), lambda p, q: (ij(p, q)[0], 0))
    if mode == "nt":
        b_spec = pl.BlockSpec((tn, k), lambda p, q: (ij(p, q)[1], 0))
    else:
        b_spec = pl.BlockSpec((k, tn), lambda p, q: (0, ij(p, q)[1]))
    return pl.pallas_call(
        body, name=name, grid=grid,
        in_specs=[a_spec, b_spec] + [pl.BlockSpec(e.shape, lambda p, q: (0, 0)) for e in extra],
        out_specs=pl.BlockSpec((tm, tn), lambda p, q: ij(p, q)),
        out_shape=jax.ShapeDtypeStruct((m, n), out_dtype),
        compiler_params=_cparams(("arbitrary", "arbitrary")),
    )(a, b, *extra)


def _loss_stage(t, g_post, h1, ff, tgt):
    tm = min(256, t)
    n = t // tm

    def body(g_ref, h_ref, f_ref, t_ref, loss_ref, dg_ref, dh_ref, df_ref):
        ni = pl.program_id(0)
        target = t_ref[...]

        def lossf(g, h1, ff):
            e = h1 + _rms(ff, g) - target
            return 0.5 * jnp.sum(jnp.mean(e * e, axis=-1))

        l, (dg, dh, df) = jax.value_and_grad(lossf, argnums=(0, 1, 2))(g_ref[...], h_ref[...], f_ref[...])

        @pl.when(ni == 0)
        def _():
            loss_ref[...] = jnp.zeros(loss_ref.shape, F32)
            dg_ref[...] = jnp.zeros(dg_ref.shape, F32)

        loss_ref[...] += jnp.full(loss_ref.shape, l, F32)
        dg_ref[...] += dg
        dh_ref[...] = dh
        df_ref[...] = df.astype(df_ref.dtype)

    row = pl.BlockSpec((tm, D), lambda ni: (ni, 0))
    one = pl.BlockSpec((1, D), lambda ni: (0, 0))
    return pl.pallas_call(
        body, name="loss_head", grid=(n,), in_specs=[one, row, row, row],
        out_specs=[pl.BlockSpec((1, LANES), lambda ni: (0, 0)), one, row, row],
        out_shape=[jax.ShapeDtypeStruct((1, LANES), F32), jax.ShapeDtypeStruct((1, D), F32),
                   jax.ShapeDtypeStruct((t, D), F32), jax.ShapeDtypeStruct((t, D), BF)],
        compiler_params=_cparams(("arbitrary",)),
    )(g_post, h1, ff, tgt)


_ANY = pl.BlockSpec(memory_space=pl.ANY)


def _all_gather(name, blks):
    na = len(blks)

    def body(*refs):
        x_refs, out_refs = refs[:na], refs[na:2 * na]
        send_sems, recv_sems, local_sems = refs[2 * na:]
        x, y, cc = lax.axis_index("x"), lax.axis_index("y"), lax.axis_index("c")
        me, sibling = (x, y, cc), (x, y, 1 - cc)
        chips = [(1 - x, y), (x, 1 - y), (1 - x, 1 - y)]

        def copy(a, k, block, to, src=None):
            dst = out_refs[a].at[4 * block[0] + 2 * block[1] + block[2]]
            return pltpu.make_async_remote_copy(
                src_ref=dst if src is None else src, dst_ref=dst, send_sem=send_sems.at[7 * a + k],
                recv_sem=recv_sems.at[7 * a + k], device_id=to, device_id_type=MESH)

        mine, first, passed = [], [], []
        for a in range(na):
            m = pltpu.make_async_copy(x_refs[a], out_refs[a].at[4 * x + 2 * y + cc], local_sems.at[a])
            m.start()
            mine.append(m)
            cps = [copy(a, 0, me, sibling, src=x_refs[a])]
            cps += [copy(a, 1 + j, me, (*chip, cc), src=x_refs[a]) for j, chip in enumerate(chips)]
            for cp in cps:
                cp.start()
            first += cps
        for j, chip in enumerate(chips):
            for a in range(na):
                copy(a, 1 + j, (*chip, cc), me).wait_recv()
                fw = copy(a, 4 + j, (*chip, cc), sibling)
                fw.start()
                passed.append(fw)
        for a in range(na):
            copy(a, 0, sibling, me).wait_recv()
            for j, chip in enumerate(chips):
                copy(a, 4 + j, (*chip, 1 - cc), me).wait_recv()
        for cp in first + passed:
            cp.wait_send()
        for m in mine:
            m.wait()

    res = pl.pallas_call(
        body, name=name, in_specs=[_ANY] * na, out_specs=[_ANY] * na,
        out_shape=[jax.ShapeDtypeStruct((N_DEV,) + b.shape, b.dtype) for b in blks],
        scratch_shapes=[pltpu.SemaphoreType.DMA((7 * na,)), pltpu.SemaphoreType.DMA((7 * na,)),
                        pltpu.SemaphoreType.DMA((na,))],
    )(*blks)
    return list(res)


def _reduce_pair(g8s):
    na = len(g8s)

    def body(*refs):
        g_refs, recv_refs = refs[:na], refs[na:2 * na]
        ssem, rsem = refs[2 * na:]
        x, y, cc = lax.axis_index("x"), lax.axis_index("y"), lax.axis_index("c")
        chips = [(x, y), (1 - x, y), (x, 1 - y), (1 - x, 1 - y)]
        sib = (x, y, 1 - cc)
        for a in range(na):
            for k, (cx, cy) in enumerate(chips):
                pltpu.make_async_remote_copy(
                    src_ref=g_refs[a].at[4 * cx + 2 * cy + 1 - cc], dst_ref=recv_refs[a].at[k],
                    send_sem=ssem.at[a], recv_sem=rsem.at[a], device_id=sib, device_id_type=MESH).start()
        for a in range(na):
            pltpu.make_async_remote_copy(src_ref=recv_refs[a], dst_ref=recv_refs[a], send_sem=ssem.at[a],
                                         recv_sem=rsem.at[a], device_id=sib, device_id_type=MESH).wait()

    res = pl.pallas_call(
        body, name="reduce_pair", in_specs=[_ANY] * na, out_specs=[_ANY] * na,
        out_shape=[jax.ShapeDtypeStruct((4,) + g.shape[1:], g.dtype) for g in g8s],
        scratch_shapes=[pltpu.SemaphoreType.DMA((na,)), pltpu.SemaphoreType.DMA((na,))],
    )(*g8s)
    return list(res)


_HBM = pl.BlockSpec(memory_space=pltpu.HBM)
_SEM = pl.BlockSpec(memory_space=pltpu.SEMAPHORE)
_EFFECT = pltpu.SideEffectType.DATAFLOW_SIDE_EFFECTING


def _chip_swap_copies(s_refs, land_refs, ssem, rsem):
    x, y, c = lax.axis_index("x"), lax.axis_index("y"), lax.axis_index("c")
    targets = [(1 - x, y, c), (x, 1 - y, c), (1 - x, 1 - y, c)]
    return [pltpu.make_async_remote_copy(src_ref=s.at[k], dst_ref=d.at[k], send_sem=ssem.at[3 * a + k],
                                         recv_sem=rsem.at[3 * a + k], device_id=targets[k], device_id_type=MESH)
            for a, (s, d) in enumerate(zip(s_refs, land_refs)) for k in range(3)]


def _chip_swap_start(sends):
    na = len(sends)

    def body(*refs):
        cps = _chip_swap_copies(refs[:na], refs[na:2 * na], refs[2 * na], refs[2 * na + 1])
        for cp in cps:
            cp.start()
        token = refs[-1]
        token[...] = jnp.zeros(token.shape, token.dtype)

    bufs = [pltpu.HBM(s.shape, s.dtype) for s in sends]
    res = pl.pallas_call(
        body, name="chip_swap_start",
        out_shape=[pltpu.SemaphoreType.DMA((3 * na,)), pltpu.SemaphoreType.DMA((3 * na,))] + bufs + bufs
        + [jax.ShapeDtypeStruct((8, LANES), F32)],
        in_specs=[_HBM] * (2 * na), out_specs=[_SEM, _SEM] + [_HBM] * (2 * na) + [pl.BlockSpec(memory_space=pltpu.VMEM)],
        input_output_aliases={i: 2 + i for i in range(2 * na)},
        compiler_params=pltpu.CompilerParams(has_side_effects=_EFFECT),
    )(*[pltpu.with_memory_space_constraint(s, pltpu.HBM) for s in sends],
      *[pltpu.with_memory_space_constraint(lax.empty(s.shape, s.dtype), pltpu.HBM) for s in sends])
    return res[0], res[1], list(res[2:2 + na]), list(res[2 + na:2 + 2 * na]), res[-1]


def _chip_swap_wait(ssem, rsem, srcs, lands, after):
    na = len(srcs)

    def body(*refs):
        cps = _chip_swap_copies(refs[:na], refs[na:2 * na], refs[2 * na], refs[2 * na + 1])
        for cp in cps:
            cp.wait_send()
            cp.wait_recv()

    bufs = [pltpu.HBM(s.shape, s.dtype) for s in srcs]
    res = pl.pallas_call(
        body, name="chip_swap_wait", out_shape=bufs + bufs,
        in_specs=[_HBM] * (2 * na) + [_SEM, _SEM, _ANY], out_specs=[_HBM] * (2 * na),
        input_output_aliases={i: i for i in range(2 * na)},
        compiler_params=pltpu.CompilerParams(has_side_effects=_EFFECT),
    )(*srcs, *lands, ssem, rsem, after)
    return list(res[na:])


def _pick_rows(r, c):
    if r * c * 4 <= TILE_BYTES or r % 16:
        return r
    best = 16
    for tr in range(16, r, 16):
        if r % tr == 0 and tr * c * 4 <= TILE_BYTES:
            best = tr
    return best


def _pair_sum(name, idx4, g8, recv4):
    _, r, c = g8.shape
    tr = _pick_rows(r, c)

    def body(idx_ref, a_ref, b_ref, o0_ref, o3_ref):
        k = pl.program_id(1)
        s = a_ref[...].astype(F32) + b_ref[...].astype(F32)

        @pl.when(k == 0)
        def _():
            o0_ref[...] = s

        @pl.when(k > 0)
        def _():
            o3_ref[...] = s.astype(BF)

    spec = pltpu.PrefetchScalarGridSpec(
        num_scalar_prefetch=1, grid=(r // tr, 4),
        in_specs=[pl.BlockSpec((None, tr, c), lambda i, k, idx: (idx[k], i, 0)),
                  pl.BlockSpec((None, tr, c), lambda i, k, idx: (k, i, 0))],
        out_specs=[pl.BlockSpec((tr, c), lambda i, k, idx: (i, 0)),
                   pl.BlockSpec((None, tr, c), lambda i, k, idx: (jnp.maximum(k - 1, 0), i, 0))])
    return pl.pallas_call(
        body, name=name, grid_spec=spec,
        out_shape=[jax.ShapeDtypeStruct((r, c), F32), jax.ShapeDtypeStruct((3, r, c), BF)],
        compiler_params=_cparams(("arbitrary", "arbitrary")),
    )(idx4, g8, recv4)


def _adamw(w, g, m, v):
    m = ADAM_B1 * m + (1.0 - ADAM_B1) * g
    v = ADAM_B2 * v + (1.0 - ADAM_B2) * jnp.square(g)
    m_hat = m / (1.0 - ADAM_B1 ** ADAM_STEP)
    v_hat = v / (1.0 - ADAM_B2 ** ADAM_STEP)
    delta = -ADAM_LR * (m_hat / (jnp.sqrt(v_hat) + ADAM_EPS) + ADAM_WD * w)
    return delta, m, v


def _sum_partials(name, idx1, own, recv):
    _, r, c = own.shape
    tr = _pick_rows(r, c)
    nj = recv.shape[0]

    def body(idx_ref, p_ref, r_ref, g_out):
        g = p_ref[...].astype(F32)
        for k in range(nj):
            g = g + r_ref[k].astype(F32)
        g_out[...] = g

    row = pl.BlockSpec((tr, c), lambda i, idx: (i, 0))
    spec = pltpu.PrefetchScalarGridSpec(
        num_scalar_prefetch=1, grid=(r // tr,),
        in_specs=[pl.BlockSpec((None, tr, c), lambda i, idx: (idx[0], i, 0)),
                  pl.BlockSpec((nj, tr, c), lambda i, idx: (0, i, 0))],
        out_specs=row)
    return pl.pallas_call(body, name=name, grid_spec=spec, out_shape=jax.ShapeDtypeStruct((r, c), F32),
                          compiler_params=_cparams(("arbitrary",)))(idx1, own, recv)


def _adam_sharded(name, idx1, own, recv, w, m, v):
    r, c = w.shape
    tr = _pick_rows(r, c)
    nj = 0 if recv is None else recv.shape[0]
    if recv is None:
        recv = jnp.zeros((1, 8, LANES), BF)

    def body(idx_ref, p_ref, r_ref, w_ref, m_ref, v_ref, g_out, d_out, m_out, v_out):
        g = p_ref[...].astype(F32)
        for k in range(nj):
            g = g + r_ref[k].astype(F32)
        d, mn, vn = _adamw(w_ref[...], g, m_ref[...], v_ref[...])
        g_out[...] = g
        d_out[...] = d
        m_out[...] = mn
        v_out[...] = vn

    row = pl.BlockSpec((tr, c), lambda i, idx: (i, 0))
    if nj:
        recv_spec = pl.BlockSpec((nj, tr, c), lambda i, idx: (0, i, 0))
    else:
        recv_spec = pl.BlockSpec(recv.shape, lambda i, idx: (0, 0, 0))
    spec = pltpu.PrefetchScalarGridSpec(
        num_scalar_prefetch=1, grid=(r // tr,),
        in_specs=[pl.BlockSpec((None, tr, c), lambda i, idx: (idx[0], i, 0)), recv_spec, row, row, row],
        out_specs=[row] * 4)
    return pl.pallas_call(
        body, name=name, grid_spec=spec, out_shape=[jax.ShapeDtypeStruct((r, c), F32)] * 4,
        compiler_params=_cparams(("arbitrary",)),
    )(idx1, own, recv, w, m, v)


def _repl_rows():
    rows, r = {}, 0
    for name, cols in REPL:
        rows[name] = r
        r += REPL_ROWS.get(name, 1) * ((cols + D - 1) // D)
    return rows


def _pack_replicated(grads):
    rows = _repl_rows()
    names = [n for n, _ in REPL]

    def body(*refs):
        o_ref = refs[-1]
        o_ref[...] = jnp.zeros(o_ref.shape, F32)
        for name, ref in zip(names, refs[:-1]):
            r0 = rows[name]
            nr, nc = ref.shape
            if nc <= D:
                o_ref[r0:r0 + nr, 0:nc] = ref[...]
            else:
                for j in range((nc + D - 1) // D):
                    lo, hi = j * D, min(nc, (j + 1) * D)
                    o_ref[r0 + j:r0 + j + 1, 0:hi - lo] = ref[:, lo:hi]

    return pl.pallas_call(body, name="pack_replicated", out_shape=jax.ShapeDtypeStruct((REPL_TOTAL, D), F32),
                          compiler_params=_cparams())(*[grads[n] for n in names])


def _adam_replicated(g8, ws, ms, vs):
    rows = _repl_rows()
    names = [n for n, _ in REPL]
    np_ = len(names)

    def body(*refs):
        g_ref = refs[0]
        w_refs, m_refs, v_refs = refs[1:1 + np_], refs[1 + np_:1 + 2 * np_], refs[1 + 2 * np_:1 + 3 * np_]
        outs = refs[1 + 3 * np_:1 + 7 * np_]
        scr = refs[-1]
        g = g_ref[0]
        for k in range(1, N_DEV):
            g = g + g_ref[k]
        scr[...] = g
        for i, name in enumerate(names):
            r0 = rows[name]
            nr, nc = w_refs[i].shape
            if nc <= D:
                gi = scr[r0:r0 + nr, 0:nc]
            else:
                parts = []
                for j in range((nc + D - 1) // D):
                    lo, hi = j * D, min(nc, (j + 1) * D)
                    parts.append(scr[r0 + j:r0 + j + 1, 0:hi - lo])
                gi = jnp.concatenate(parts, axis=1)
            d, mn, vn = _adamw(w_refs[i][...], gi, m_refs[i][...], v_refs[i][...])
            outs[i][...] = gi
            outs[np_ + i][...] = d
            outs[2 * np_ + i][...] = mn
            outs[3 * np_ + i][...] = vn

    shp = [jax.ShapeDtypeStruct(w.shape, F32) for w in ws]
    res = pl.pallas_call(body, name="adam_replicated", out_shape=shp * 4,
                         scratch_shapes=[pltpu.VMEM((REPL_TOTAL, D), F32)], compiler_params=_cparams(),
                         )(g8, *ws, *ms, *vs)
    return [dict(zip(names, res[k * np_:(k + 1) * np_])) for k in range(4)]


_WEIGHTS = ("attn_pre_norm", "w_in", "hgrn_lb", "hgrn_gnorm", "w_branch_a", "rwkv_mu", "rwkv_w0", "rwkv_w2",
            "rwkv_a0", "rwkv_a2", "rwkv_g2", "rwkv_k_k", "rwkv_k_a", "rwkv_r_k", "rwkv_ln_w", "rwkv_ln_b",
            "w_branch_b", "w_out", "attn_post_norm", "ffn_pre_norm", "w_up", "conv_w", "conv_b", "w_down",
            "ffn_post_norm")
_BIG = ("w_in", "w_up", "w_down", "w_branch_a", "w_branch_b", "w_out")


def _stages():
    one = [D]
    hw = HG_K * HG_PER_STEP
    rw = LANES * RW_PAIRS_PER_STEP
    return dict(
        pre1=_Stage("pre1", _f_pre1, 1, 256, [False], [one], [0], [], [one], [BF]),
        mixers=_Stage("mixers", _f_mixers, 1, RW_CHUNK, [False] * 13, [[D] * 7 + [LANES, LANES]], [0],
                      [(hw, HG_K), (1, RW_COLS), (rw, LANES)], [one, one], [BF, BF]),
        merge=_Stage("merge", _f_merge, 4, 512, [], [[256]] * 4, [29, 33, 0, 0], [], [[256]], [BF]),
        post1=_Stage("post1", _f_post1, 1, 256, [False, False], [one, one], [0, 0], [], [one, one], [F32, BF]),
        conv=_Stage("conv", _f_conv, 1, 128, [False, False], [[DFF, DFF]], [0], [(1, 2 * DFF), (1, 2 * DFF)],
                    [[DFF]], [BF]),
    )


def _cols_to_blocks(w, per):
    return w.reshape(w.shape[0], N_DEV, per).transpose(1, 0, 2)


def _blocks_to_cols(g):
    return g.transpose(1, 0, 2).reshape(g.shape[1], N_DEV * g.shape[2])


def kernel(x, attn_pre_norm, w_in, hgrn_lb, hgrn_gnorm, w_branch_a, rwkv_mu, rwkv_w0, rwkv_w2, rwkv_a0, rwkv_a2, rwkv_g2, rwkv_k_k, rwkv_k_a, rwkv_r_k, rwkv_ln_w, rwkv_ln_b, w_branch_b, w_out, attn_post_norm, ffn_pre_norm, w_up, conv_w, conv_b, w_down, ffn_post_norm, loss_target, m_attn_pre_norm, m_w_in, m_hgrn_lb, m_hgrn_gnorm, m_w_branch_a, m_rwkv_mu, m_rwkv_w0, m_rwkv_w2, m_rwkv_a0, m_rwkv_a2, m_rwkv_g2, m_rwkv_k_k, m_rwkv_k_a, m_rwkv_r_k, m_rwkv_ln_w, m_rwkv_ln_b, m_w_branch_b, m_w_out, m_attn_post_norm, m_ffn_pre_norm, m_w_up, m_conv_w, m_conv_b, m_w_down, m_ffn_post_norm, v_attn_pre_norm, v_w_in, v_hgrn_lb, v_hgrn_gnorm, v_w_branch_a, v_rwkv_mu, v_rwkv_w0, v_rwkv_w2, v_rwkv_a0, v_rwkv_a2, v_rwkv_g2, v_rwkv_k_k, v_rwkv_k_a, v_rwkv_r_k, v_rwkv_ln_w, v_rwkv_ln_b, v_w_branch_b, v_w_out, v_attn_post_norm, v_ffn_pre_norm, v_w_up, v_conv_w, v_conv_b, v_w_down, v_ffn_post_norm):
    w = dict(attn_pre_norm=attn_pre_norm, w_in=w_in, hgrn_lb=hgrn_lb, hgrn_gnorm=hgrn_gnorm, w_branch_a=w_branch_a, rwkv_mu=rwkv_mu, rwkv_w0=rwkv_w0, rwkv_w2=rwkv_w2, rwkv_a0=rwkv_a0, rwkv_a2=rwkv_a2, rwkv_g2=rwkv_g2, rwkv_k_k=rwkv_k_k, rwkv_k_a=rwkv_k_a, rwkv_r_k=rwkv_r_k, rwkv_ln_w=rwkv_ln_w, rwkv_ln_b=rwkv_ln_b, w_branch_b=w_branch_b, w_out=w_out, attn_post_norm=attn_post_norm, ffn_pre_norm=ffn_pre_norm, w_up=w_up, conv_w=conv_w, conv_b=conv_b, w_down=w_down, ffn_post_norm=ffn_post_norm)
    mo = dict(attn_pre_norm=m_attn_pre_norm, w_in=m_w_in, hgrn_lb=m_hgrn_lb, hgrn_gnorm=m_hgrn_gnorm, w_branch_a=m_w_branch_a, rwkv_mu=m_rwkv_mu, rwkv_w0=m_rwkv_w0, rwkv_w2=m_rwkv_w2, rwkv_a0=m_rwkv_a0, rwkv_a2=m_rwkv_a2, rwkv_g2=m_rwkv_g2, rwkv_k_k=m_rwkv_k_k, rwkv_k_a=m_rwkv_k_a, rwkv_r_k=m_rwkv_r_k, rwkv_ln_w=m_rwkv_ln_w, rwkv_ln_b=m_rwkv_ln_b, w_branch_b=m_w_branch_b, w_out=m_w_out, attn_post_norm=m_attn_post_norm, ffn_pre_norm=m_ffn_pre_norm, w_up=m_w_up, conv_w=m_conv_w, conv_b=m_conv_b, w_down=m_w_down, ffn_post_norm=m_ffn_post_norm)
    vo = dict(attn_pre_norm=v_attn_pre_norm, w_in=v_w_in, hgrn_lb=v_hgrn_lb, hgrn_gnorm=v_hgrn_gnorm, w_branch_a=v_w_branch_a, rwkv_mu=v_rwkv_mu, rwkv_w0=v_rwkv_w0, rwkv_w2=v_rwkv_w2, rwkv_a0=v_rwkv_a0, rwkv_a2=v_rwkv_a2, rwkv_g2=v_rwkv_g2, rwkv_k_k=v_rwkv_k_k, rwkv_k_a=v_rwkv_k_a, rwkv_r_k=v_rwkv_r_k, rwkv_ln_w=v_rwkv_ln_w, rwkv_ln_b=v_rwkv_ln_b, w_branch_b=v_w_branch_b, w_out=v_w_out, attn_post_norm=v_attn_post_norm, ffn_pre_norm=v_ffn_pre_norm, w_up=v_w_up, conv_w=v_conv_w, conv_b=v_conv_b, w_down=v_w_down, ffn_post_norm=v_ffn_post_norm)

    t = x.shape[1]
    x2 = x.reshape(t, D)
    tgt = loss_target.reshape(t, D)
    st = _stages()

    me = 4 * lax.axis_index("x") + 2 * lax.axis_index("y") + lax.axis_index("c")
    small = jnp.concatenate([rwkv_w2[0], rwkv_a2[0], rwkv_g2[0]], axis=0).astype(BF)
    g_in, g_small = _all_gather("gather_weights", [w_in[0].T.astype(BF), small])
    fw_in_t = g_in.reshape(IN_COLS, D)
    z64 = jnp.zeros((64, D), BF)
    w2p = jnp.concatenate([_blocks_to_cols(g_small[:, 0:64]), z64], axis=0)
    a2p = jnp.concatenate([z64, _blocks_to_cols(g_small[:, 64:128])], axis=0)
    g2f = _blocks_to_cols(g_small[:, 128:256])
    conv_bits = lax.bitcast_convert_type(conv_w[0], BF).reshape(3, 2 * 704)
    late = [w_up[0].T.astype(BF)] + [w[k][0].astype(BF) for k in _BIG[2:]] + [conv_bits]
    late_gather = _Exchange("gather2", late)
    r_k = rwkv_r_k.reshape(1, D)

    (xn,), _ = _stage_fwd(st["pre1"], t, [attn_pre_norm], [x2])
    z = _mm("in_proj", xn, fw_in_t, "nt", F32, tm=512, tn=4736, b_outer=True)
    mix_par = [hgrn_lb, hgrn_gnorm, rwkv_mu, rwkv_w0, w2p, rwkv_a0, a2p, g2f, rwkv_k_k, rwkv_k_a,
               rwkv_ln_w, rwkv_ln_b, r_k]
    mix_in = [z]
    (o_a, o_b), mix_saved = _stage_fwd(st["mixers"], t, mix_par, mix_in, hook=late_gather)
    gl = [lax.dynamic_update_slice(g, own[None], (me, 0, 0)) for g, own in zip(late_gather.results, late)]
    fw_up_t = gl[0].reshape(2 * DFF, D)
    fw_down = gl[1].reshape(DFF, D)
    fw_a, fw_b, fw_out = (g.reshape(D, D) for g in gl[2:5])
    conv_full = _blocks_to_cols(lax.bitcast_convert_type(gl[5].reshape(N_DEV, 3, 704, 2), F32))
    y_a = _mm("branch_a", o_a, fw_a, "nn")
    y_b = _mm("branch_b", o_b, fw_b, "nn")
    (merged,), _ = _stage_fwd(st["merge"], t, [], [z, z, y_a, y_b])
    mix = _mm("out_proj", merged, fw_out, "nn")
    (h1, xn2), _ = _stage_fwd(st["post1"], t, [attn_post_norm, ffn_pre_norm], [x2, mix])
    hu = _mm("up_proj", xn2, fw_up_t, "nt", F32, tm=1024, tn=1408)
    conv_par = [conv_full, conv_b]
    (act,), conv_saved = _stage_fwd(st["conv"], t, conv_par, [hu])
    ff = _mm("down_proj", act, fw_down, "nn")

    loss_acc, d_ffn_post, dh1, dff = _loss_stage(t, ffn_post_norm, h1, ff, tgt)
    dact = _mm("d_act", dff, fw_down, "nt", F32, tm=1024, tn=1408)
    dw_down = _mm("dw_down", act, dff, "tn", BF, tm=1408, tn=512)
    (dcw, dcb), (dhu,) = _stage_bwd(st["conv"], t, conv_par, [hu], conv_saved, [[dact]], [BF])
    dxn2 = _mm("d_xn2", dhu, fw_up_t, "nn", F32, tm=1024, tn=256)
    dw_up_t = _mm("dw_up", dhu, xn2, "tn", BF, tm=1408, tn=1024)
    (d_post, d_pre2), (dx_a, dmix) = _stage_bwd(st["post1"], t, [attn_post_norm, ffn_pre_norm], [x2, mix], [],
                                                 [[dh1], [dxn2]], [F32, BF])
    dmerged = _mm("d_merged", dmix, fw_out, "nt")
    dw_out = _mm("dw_out", merged, dmix, "tn", BF)
    _, (dga, dgb, dy_a, dy_b) = _stage_bwd(st["merge"], t, [], [z, z, y_a, y_b], [], [[dmerged]], [BF, BF, BF, BF])
    do_a = _mm("d_oa", dy_a, fw_a, "nt")
    dw_a = _mm("dw_a", o_a, dy_a, "tn", BF)
    do_b = _mm("d_ob", dy_b, fw_b, "nt")
    dw_b = _mm("dw_b", o_b, dy_b, "tn", BF)
    early = [dw_up_t.reshape(N_DEV, 704, D), dw_down.reshape(N_DEV, 352, D), dw_a.reshape(N_DEV, 128, D),
             dw_b.reshape(N_DEV, 128, D), dw_out.reshape(N_DEV, 128, D), _cols_to_blocks(dcw.astype(BF), 704)]
    early_scatter = _Exchange("scatter", early)
    mix_dp, dz_hr = _stage_bwd(st["mixers"], t, mix_par, mix_in, mix_saved, [[do_a], [do_b]], [BF],
                               hook=early_scatter)
    d_lb, d_gn, d_mu, d_w0, d_w2p, d_a0, d_a2p, d_g2, d_kk, d_ka, d_lnw, d_lnb, d_rk = mix_dp
    dz = jnp.concatenate(dz_hr + [dga, dgb], axis=1)
    dw_in_t = _mm("dw_in", dz, xn, "tn", BF, tm=256, tn=1024)

    ax, ay, ac = lax.axis_index("x"), lax.axis_index("y"), lax.axis_index("c")
    idx4 = jnp.stack([4 * cx + 2 * cy + ac for cx, cy in ((ax, ay), (1 - ax, ay), (ax, 1 - ay), (1 - ax, 1 - ay))])
    idx4 = idx4.astype(jnp.int32)
    idx_me, idx_0 = idx4[0:1], jnp.zeros((1,), jnp.int32)
    d_small = jnp.concatenate([d_w2p[:64], d_a2p[64:], d_g2], axis=0).astype(BF)
    g8s = [dw_in_t.reshape(N_DEV, 1184, D), _cols_to_blocks(d_small, LANES)]
    recv4s = _reduce_pair(g8s)
    sums = [_pair_sum("pair_sum_" + n, idx4, g, r) for n, g, r in zip(("w_in", "small"), g8s, recv4s)]
    swap_ssem, swap_rsem, swap_srcs, swap_lands, token = _chip_swap_start([s[1] for s in sums])
    dxn = _mm("d_xn", dz, fw_in_t, "nn", F32, tm=512, tn=512, b_outer=True, token=token)
    (d_pre1,), (dx_b,) = _stage_bwd(st["pre1"], t, [attn_pre_norm], [x2], [], [[dxn]], [F32])
    grad_x = (dx_a + dx_b).reshape(x.shape)
    loss = lax.psum(loss_acc[0, 0], ("x", "y", "c"))

    rg = dict(attn_pre_norm=d_pre1, hgrn_lb=d_lb, hgrn_gnorm=d_gn, rwkv_mu=d_mu, rwkv_w0=d_w0, rwkv_a0=d_a0,
              rwkv_k_k=d_kk, rwkv_k_a=d_ka, rwkv_r_k=d_rk, rwkv_ln_w=d_lnw, rwkv_ln_b=d_lnb, attn_post_norm=d_post,
              ffn_pre_norm=d_pre2, conv_b=dcb, ffn_post_norm=d_ffn_post)
    (g8,) = _all_gather("gather_small_grads", [_pack_replicated(rg)])
    rnames = [n for n, _ in REPL]
    flat = lambda src: [src[n].reshape(1, D) if n == "rwkv_r_k" else src[n] for n in rnames]
    rp_out = _adam_replicated(g8, flat(w), flat(mo), flat(vo))
    recv3s = _chip_swap_wait(swap_ssem, swap_rsem, swap_srcs, swap_lands, rp_out[0]["attn_pre_norm"])
    for kind in range(4):
        rp_out[kind]["rwkv_r_k"] = rp_out[kind]["rwkv_r_k"].reshape(rwkv_r_k.shape)

    def small_of(src):
        return jnp.concatenate([src["rwkv_w2"][0], src["rwkv_a2"][0], src["rwkv_g2"][0]], axis=0)

    sh_out = [dict() for _ in range(4)]
    g_in = _sum_partials("sum_w_in", idx_0, sums[0][0][None], recv3s[0]).T
    res = _adam_sharded("adam_w_in", idx_0, g_in[None], None, *[src["w_in"][0] for src in (w, mo, vo)])
    res_s = _adam_sharded("adam_small", idx_0, sums[1][0][None], recv3s[1], *[small_of(src) for src in (w, mo, vo)])
    for kind in range(4):
        sh_out[kind]["w_in"] = res[kind][None]
        sh_out[kind]["rwkv_w2"] = res_s[kind][0:64][None]
        sh_out[kind]["rwkv_a2"] = res_s[kind][64:128][None]
        sh_out[kind]["rwkv_g2"] = res_s[kind][128:256][None]
    for n, own, recv in zip(_BIG[1:] + ("conv_w",), early, early_scatter.results):
        if n == "w_up":
            g_up = _sum_partials("sum_w_up", idx_me, own, recv).T
            res = _adam_sharded("adam_" + n, idx_0, g_up[None], None, *[src[n][0] for src in (w, mo, vo)])
        else:
            res = _adam_sharded("adam_" + n, idx_me, own, recv, *[src[n][0] for src in (w, mo, vo)])
        for kind in range(4):
            sh_out[kind][n] = res[kind][None]

    outs = [loss, grad_x]
    for kind in range(4):
        for name in _WEIGHTS:
            outs.append(sh_out[kind][name] if name in sh_out[kind] else rp_out[kind][name])
    return tuple(outs)
```

```python
import functools

import jax
import jax.numpy as jnp
from jax import lax
from jax.experimental import pallas as pl
from jax.experimental.pallas import tpu as pltpu

F32 = jnp.float32
BF = jnp.bfloat16
MESH = pl.DeviceIdType.MESH

D = 1024
HG_HEADS = 8
HG_K = 128
HG_CHUNK = 32
HG_SCALE = HG_K ** -0.5
HG_PER_STEP = 8
RW_HEADS = 16
RW_N = 64
RW_CHUNK = 64
RW_PAIRS_PER_STEP = 8
DFF = 2816
IN_COLS = 9472
RW_COLS = 3328
EPS = 1e-6
GN_EPS = 1e-5 * RW_N
ADAM_LR = 0.001
ADAM_B1 = 0.9
ADAM_B2 = 0.999
ADAM_EPS = 1e-08
ADAM_WD = 0.01
ADAM_STEP = 10
N_DEV = 8
LANES = 128
VMEM_LIMIT = 56 * 1024 * 1024
TILE_BYTES = 1280 * 1024

REPL = (("attn_pre_norm", 1024), ("hgrn_lb", 1024), ("hgrn_gnorm", 1024), ("rwkv_mu", 3328), ("rwkv_w0", 1024),
        ("rwkv_a0", 1024), ("rwkv_k_k", 1024), ("rwkv_k_a", 1024), ("rwkv_r_k", 1024), ("rwkv_ln_w", 1024),
        ("rwkv_ln_b", 1024), ("attn_post_norm", 1024), ("ffn_pre_norm", 1024), ("conv_b", 5632), ("ffn_post_norm", 1024))
REPL_ROWS = {"hgrn_lb": 2}
REPL_TOTAL = 32


def _cparams(sem=None, **kw):
    return pltpu.CompilerParams(dimension_semantics=sem, vmem_limit_bytes=VMEM_LIMIT, **kw)


_DN = {"nn": ((1,), (0,)), "nt": ((1,), (1,)), "tn": ((0,), (0,))}


def _raw_dot(a, b, mode):
    return lax.dot_general(a.astype(BF), b.astype(BF), (_DN[mode], ((), ())), preferred_element_type=F32)


@functools.partial(jax.custom_vjp, nondiff_argnums=(2,))
def _dot(a, b, mode):
    return _raw_dot(a, b, mode)


def _dot_fwd(a, b, mode):
    return _raw_dot(a, b, mode), (a, b)


def _dot_bwd(mode, res, g):
    a, b = res
    if mode == "nn":
        return _dot(g, b, "nt"), _dot(a, g, "tn")
    if mode == "nt":
        return _dot(g, b, "nn"), _dot(g, a, "tn")
    return _dot(b, g, "nt"), _dot(a, g, "nn")


_dot.defvjp(_dot_fwd, _dot_bwd)


def _bf_pieces(x, n):
    out, r = [], x
    for i in range(n):
        p = r.astype(BF)
        out.append(p)
        if i + 1 < n:
            r = r - p.astype(F32)
    return out


def _raw_split_dot(x, e, mode, n, x_left):
    eb = e.astype(BF)
    acc = None
    for p in _bf_pieces(x, n):
        ops = (p, eb) if x_left else (eb, p)
        t = lax.dot_general(*ops, (_DN[mode], ((), ())), preferred_element_type=F32)
        acc = t if acc is None else acc + t
    return acc


def _raw_headsum(x):
    t = x.shape[0]
    i = lax.broadcasted_iota(jnp.int32, (LANES, LANES), 0)
    j = lax.broadcasted_iota(jnp.int32, (LANES, LANES), 1)
    same = jnp.where((i >= RW_N) == (j >= RW_N), 1.0, 0.0).astype(F32)
    groups = x.shape[1] // LANES
    rows = jnp.concatenate([x[:, q * LANES:(q + 1) * LANES] for q in range(groups)], axis=0)
    s = _raw_split_dot(rows, same, "nn", 2, True)
    return jnp.concatenate([s[q * t:(q + 1) * t] for q in range(groups)], axis=1)


@jax.custom_vjp
def _headsum(x):
    return _raw_headsum(x)


def _headsum_fwd(x):
    return _raw_headsum(x), None


def _headsum_bwd(_, g):
    return (_raw_headsum(g),)


_headsum.defvjp(_headsum_fwd, _headsum_bwd)


@functools.partial(jax.custom_vjp, nondiff_argnums=(2,))
def _tdot(tri, x, n):
    return _raw_split_dot(x, tri, "nn", n, False)


def _tdot_fwd(tri, x, n):
    return _raw_split_dot(x, tri, "nn", n, False), tri


def _tdot_bwd(n, tri, g):
    return jnp.zeros_like(tri), _raw_split_dot(g, tri, "tn", n, False)


_tdot.defvjp(_tdot_fwd, _tdot_bwd)


def _row(x, i):
    r = lax.broadcasted_iota(jnp.int32, x.shape, 0)
    return jnp.sum(jnp.where(r == i, x, 0.0), axis=0, keepdims=True)


def _shift_down(x, prev):
    t = x.shape[0]

    @jax.custom_vjp
    def sh(x, prev):
        r = lax.broadcasted_iota(jnp.int32, x.shape, 0)
        return jnp.where(r == 0, prev, pltpu.roll(x, 1, 0))

    def fwd(x, prev):
        return sh(x, prev), None

    def bwd(_, g):
        r = lax.broadcasted_iota(jnp.int32, g.shape, 0)
        dx = jnp.where(r == t - 1, 0.0, pltpu.roll(g, t - 1, 0))
        return dx, jnp.sum(jnp.where(r == 0, g, 0.0), axis=0, keepdims=True)

    sh.defvjp(fwd, bwd)
    return sh(x, prev)


def _sigmoid(x):
    return jax.nn.sigmoid(x)


def _silu(x):
    return x * jax.nn.sigmoid(x)


def _softplus(x):
    return jnp.maximum(x, 0.0) + jnp.log(1.0 + jnp.exp(-jnp.abs(x)))


def _rms(x, g):
    return (x * lax.rsqrt(jnp.mean(x * x, axis=-1, keepdims=True) + EPS)) * g


def _tril(c):
    r = lax.broadcasted_iota(jnp.int32, (c, c), 0)
    cc = lax.broadcasted_iota(jnp.int32, (c, c), 1)
    return cc <= r


def _f_pre1(ps, xs, cs):
    return [_rms(xs[0], ps[0])], []


def _f_pre1_residual(ps, xs, cs):
    return [_rms(xs[0], ps[0]), xs[0]], []


def _f_hgrn(ps, xs, cs):
    lbraw, gn = ps
    hq, hf, hi, hg = xs
    hd = range(HG_PER_STEP)
    st = [cs[0][p * HG_K:(p + 1) * HG_K] for p in hd]
    l0, l1 = _row(lbraw, 0), _row(lbraw, 1)
    m = jnp.maximum(l0, l1)
    e0, e1 = jnp.exp(l0 - m), jnp.exp(l1 - m)
    lb = e0 / (e0 + e1)
    q = _silu(hq) * HG_SCALE
    f = lb + (1.0 - lb) * _sigmoid(hf)
    kh = 1.0 - f
    gl = jnp.log(f)
    c = HG_CHUNK
    low = _tril(c)
    tri = jnp.where(low, 1.0, 0.0).astype(F32)
    outs = []
    for i in range(hq.shape[0] // c):
        rows = slice(i * c, (i + 1) * c)
        b = _tdot(tri, gl[rows], 3)
        bref = _row(b, c // 2 - 1)
        blast = _row(b, c - 1)
        qi = q[rows] * jnp.exp(b - bref)
        ki = kh[rows] * jnp.exp(bref - b)
        qd = q[rows] * jnp.exp(b)
        kd = kh[rows] * jnp.exp(blast - b)
        dec = jnp.exp(blast)
        sl = [slice(p * HG_K, (p + 1) * HG_K) for p in hd]
        sc = [jnp.where(low, _dot(qi[:, sl[p]], ki[:, sl[p]], "nt"), 0.0) for p in hd]
        o = [_dot(sc[p], hi[rows, sl[p]], "nn") + _dot(qd[:, sl[p]], st[p], "nt") for p in hd]
        u = [_dot(hi[rows, sl[p]], kd[:, sl[p]], "tn") for p in hd]
        st = [dec[:, sl[p]] * st[p] + u[p] for p in hd]
        outs.append(jnp.concatenate(o, axis=1) if len(o) > 1 else o[0])
    o = outs[0] if len(outs) == 1 else jnp.concatenate(outs, axis=0)
    on = []
    for p in hd:
        op = o[:, p * HG_K:(p + 1) * HG_K]
        on.append(op * lax.rsqrt(jnp.mean(op * op, axis=-1, keepdims=True) + EPS))
    o = jnp.concatenate(on, axis=1) if len(on) > 1 else on[0]
    o = o * gn
    return [o * _silu(hg)], [jnp.concatenate(st, axis=0) if len(st) > 1 else st[0]]


_RW_OFFS = (0, 1024, 2048, 3072, 3200, 3328)


def _f_rwpre(ps, xs, cs):
    mu, w0, w2p, a0, a2p, g2, k_k, k_a = ps
    (prev,) = cs
    t = xs[0].shape[0]
    zs = []
    for i, z in enumerate(xs):
        lo, hi = _RW_OFFS[i], _RW_OFFS[i + 1]
        zs.append(z + mu[:, lo:hi] * (_shift_down(z, prev[:, lo:hi]) - z))
    rr, kr, vr, wa, gz = zs
    w_log = -_softplus(-(w0 + _dot(jnp.tanh(wa), w2p, "nn"))) - 0.5
    lw = -jnp.exp(w_log)
    a = _sigmoid(a0 + _dot(wa, a2p, "nn"))
    g = _dot(_sigmoid(gz), g2, "nn")
    kkr = kr * k_k
    kk = kkr / jnp.maximum(jnp.sqrt(_headsum(kkr * kkr)), 1e-12)
    k2 = kr * (1.0 + (a - 1.0) * k_a)
    newprev = jnp.concatenate([_row(z, t - 1) for z in xs], axis=1)
    return [rr, lw, k2, vr, -kk, kk * a, g], [newprev]


def _raw_inverses(ls):
    n = ls[0].shape[0]
    r = lax.broadcasted_iota(jnp.int32, (n, n), 0)
    c = lax.broadcasted_iota(jnp.int32, (n, n), 1)
    eye = jnp.where(r == c, 1.0, 0.0).astype(F32)
    tinv = [eye + l for l in ls]
    pw = ls
    for _ in range(5):
        pw = [_raw_dot(p, p, "nn") for p in pw]
        tinv = [t + _raw_dot(t, p, "nn") for t, p in zip(tinv, pw)]
    return tinv


@jax.custom_vjp
def _unit_lower_inverses(ls):
    return _raw_inverses(ls)


def _inverses_fwd(ls):
    tinv = _raw_inverses(ls)
    return tinv, tinv


def _inverses_bwd(tinv, gs):
    return ([_raw_dot(_raw_dot(t, g, "tn"), t, "nt") for t, g in zip(tinv, gs)],)


_unit_lower_inverses.defvjp(_inverses_fwd, _inverses_bwd)


def _f_rwscan(ps, xs, cs):
    npair = RW_PAIRS_PER_STEP
    pr = range(npair)
    r, lw, k, v, av, bv = [[x[:, p * LANES:(p + 1) * LANES] for p in pr] for x in xs]
    sv = [cs[0][p * LANES:(p + 1) * LANES] for p in pr]
    c = RW_CHUNK
    n = 2 * c
    tri = jnp.where(_tril(c), 1.0, 0.0).astype(F32)
    cl = [_tdot(tri, lw[p], 3) for p in pr]
    cl_last = [_row(cl[p], c - 1) for p in pr]
    lane = lax.broadcasted_iota(jnp.int32, (c, LANES), 1)
    h0 = lane < RW_N

    def stack(x):
        return jnp.concatenate([jnp.where(h0, x, 0.0), jnp.where(h0, 0.0, x)], axis=0)

    am = [stack(av[p] * jnp.exp(cl[p] - lw[p])) for p in pr]
    bm = [stack(bv[p] * jnp.exp(-cl[p])) for p in pr]
    km = [stack(k[p] * jnp.exp(-cl[p])) for p in pr]
    rm = [stack(r[p] * jnp.exp(cl[p])) for p in pr]
    vm = [stack(v[p]) for p in pr]
    rn = lax.broadcasted_iota(jnp.int32, (n, n), 0)
    cn = lax.broadcasted_iota(jnp.int32, (n, n), 1)
    blk = (rn >= c) == (cn >= c)
    strict = blk & (cn < rn)
    incl = blk & (cn <= rn)
    lab = [jnp.where(strict, _dot(am[p], bm[p], "nt"), 0.0) for p in pr]
    lak = [jnp.where(strict, _dot(am[p], km[p], "nt"), 0.0) for p in pr]
    wrb = [jnp.where(incl, _dot(rm[p], bm[p], "nt"), 0.0) for p in pr]
    wrk = [jnp.where(incl, _dot(rm[p], km[p], "nt"), 0.0) for p in pr]
    tinv = _unit_lower_inverses(lab)
    rhs = [_dot(am[p], sv[p], "nt") + _dot(lak[p], vm[p], "nn") for p in pr]
    um = [_dot(tinv[p], rhs[p], "nn") for p in pr]
    ym = [_dot(rm[p], sv[p], "nt") + _dot(wrb[p], um[p], "nn") + _dot(wrk[p], vm[p], "nn") for p in pr]
    sn = [(sv[p] + _dot(um[p], bm[p], "tn") + _dot(vm[p], km[p], "tn")) * jnp.exp(cl_last[p]) for p in pr]
    ys = [ym[p][:c] + ym[p][c:] for p in pr]
    return [jnp.concatenate(ys, axis=1)], [jnp.concatenate(sn, axis=0)]


def _f_mixers(ps, xs, cs):
    oa, st = _f_hgrn(ps[:2], xs[:4], cs[:1])
    (r, lw, k, v, av, bv, g), prev = _f_rwpre(ps[2:10], xs[4:], cs[1:2])
    y, sv = _f_rwscan([], [r, lw, k, v, av, bv], cs[2:])
    ob, _ = _f_rwpost(ps[10:], y + [r, k, v, g], [])
    return oa + ob, st + prev + sv


def _f_rwpost(ps, xs, cs):
    ln_w, ln_b, r_k = ps
    y, r, k, v, g = xs
    inv_n = 1.0 / RW_N
    yc = y - _headsum(y) * inv_n
    var = _headsum(yc * yc) * inv_n
    yn = yc * lax.rsqrt(var + GN_EPS)
    yn = yn * ln_w + ln_b
    bonus = _headsum(r * k * r_k) * v
    return [(yn + bonus) * g], []


def _f_merge(ps, xs, cs):
    ga, gb, ya, yb = xs
    return [_sigmoid(ga) * ya + _sigmoid(gb) * yb], []


def _f_post1(ps, xs, cs):
    x, mix = xs
    h1 = x + _rms(mix, ps[0])
    return [h1, _rms(h1, ps[1])], []


def _f_conv(ps, xs, cs):
    cw, cb = ps
    p1, p2 = cs
    w0, w1, w2 = _row(cw, 0), _row(cw, 1), _row(cw, 2)
    t = xs[0].shape[0]
    hc = []
    for i, x in enumerate(xs):
        sl = slice(i * DFF, (i + 1) * DFF)
        s1 = _shift_down(x, p1[:, sl])
        s2 = _shift_down(s1, p2[:, sl])
        hc.append(cb[:, sl] + w0[:, sl] * s2 + w1[:, sl] * s1 + w2[:, sl] * x)
    n1 = jnp.concatenate([_row(x, t - 1) for x in xs], axis=1)
    n2 = jnp.concatenate([_row(x, t - 2) for x in xs], axis=1)
    return [_silu(hc[0]) * hc[1]], [n1, n2]


class _Stage:
    def __init__(self, name, f, g, tm, par_per_g, in_pieces, in_offs, carry_shapes, out_pieces, out_dtypes):
        self.name, self.f, self.g, self.tm = name, f, g, tm
        self.par_per_g, self.in_pieces, self.in_offs = par_per_g, in_pieces, in_offs
        self.carry_shapes, self.out_pieces, self.out_dtypes = carry_shapes, out_pieces, out_dtypes


def _par_spec(arr, per_g, g):
    r, c = arr.shape
    if per_g:
        return pl.BlockSpec((r, c // g), lambda gi, ni: (0, gi))
    return pl.BlockSpec((r, c), lambda gi, ni: (0, 0))


def _row_spec(tm, width, off, n, rev):
    if rev:
        return pl.BlockSpec((tm, width), lambda gi, ni: (n - 1 - ni, off + gi))
    return pl.BlockSpec((tm, width), lambda gi, ni: (ni, off + gi))


def _carry_spec(shape, n, rev):
    if rev:
        return pl.BlockSpec((None, None) + shape, lambda gi, ni: (gi, n - 1 - ni, 0, 0))
    return pl.BlockSpec((None, None) + shape, lambda gi, ni: (gi, ni, 0, 0))


def _load_pieces(refs, pieces_list):
    out = []
    for ref, pieces in zip(refs, pieces_list):
        o = 0
        for w in pieces:
            out.append(ref[:, o:o + w].astype(F32))
            o += w
    return out


def _store_pieces(refs, pieces_list, vals):
    k = 0
    for ref, pieces in zip(refs, pieces_list):
        o = 0
        for w in pieces:
            ref[:, o:o + w] = vals[k].astype(ref.dtype)
            k += 1
            o += w


_ANY = pl.BlockSpec(memory_space=pl.ANY)


class _Exchange:
    def __init__(self, kind, arrs):
        self.kind, self.arrs, self.results = kind, list(arrs), None
        if kind == "scatter":
            self.out_shape = [jax.ShapeDtypeStruct((N_DEV - 1,) + a.shape[1:], a.dtype) for a in self.arrs]
        else:
            self.out_shape = [jax.ShapeDtypeStruct((N_DEV,) + a.shape, a.dtype) for a in self.arrs]
        self.nsem = (N_DEV - 1) * len(self.arrs)

    def copies(self, in_refs, out_refs, ssem, rsem):
        x, y, c = lax.axis_index("x"), lax.axis_index("y"), lax.axis_index("c")
        me = 4 * x + 2 * y + c
        cps = []
        for a, (i_ref, o_ref) in enumerate(zip(in_refs, out_refs)):
            for j in range(1, N_DEV):
                px = 1 - x if j & 4 else x
                py = 1 - y if j & 2 else y
                pc = 1 - c if j & 1 else c
                if self.kind == "gather":
                    src, dst = i_ref, o_ref.at[me]
                else:
                    src, dst = i_ref.at[4 * px + 2 * py + pc], o_ref.at[j - 1]
                s = (N_DEV - 1) * a + j - 1
                cps.append(pltpu.make_async_remote_copy(src_ref=src, dst_ref=dst, send_sem=ssem.at[s],
                                                        recv_sem=rsem.at[s], device_id=(px, py, pc),
                                                        device_id_type=MESH))
        return cps

    def run(self, first, mid, last, in_refs, out_refs, ssem, rsem):
        if self.kind == "gather2":
            return self.run_two_level(first, mid, last, in_refs, out_refs, ssem, rsem)

        @pl.when(first)
        def _():
            for cp in self.copies(in_refs, out_refs, ssem, rsem):
                cp.start()

        @pl.when(last)
        def _():
            for cp in self.copies(in_refs, out_refs, ssem, rsem):
                cp.wait()

    def run_two_level(self, first, mid, last, in_refs, out_refs, ssem, rsem):
        x, y, c = lax.axis_index("x"), lax.axis_index("y"), lax.axis_index("c")
        me, sibling = (x, y, c), (x, y, 1 - c)
        chips = [(1 - x, y), (x, 1 - y), (1 - x, 1 - y)]
        arrs = range(len(in_refs))

        def copy(a, k, block, to, src=None):
            dst = out_refs[a].at[4 * block[0] + 2 * block[1] + block[2]]
            return pltpu.make_async_remote_copy(
                src_ref=dst if src is None else src, dst_ref=dst, send_sem=ssem.at[7 * a + k],
                recv_sem=rsem.at[7 * a + k], device_id=to, device_id_type=MESH)

        def firsts(a):
            return [copy(a, 0, me, sibling, src=in_refs[a])] + [
                copy(a, 1 + j, me, (*chip, c), src=in_refs[a]) for j, chip in enumerate(chips)]

        def passed(a):
            return [copy(a, 4 + j, (*chip, c), sibling) for j, chip in enumerate(chips)]

        @pl.when(first)
        def _():
            for a in arrs:
                for cp in firsts(a):
                    cp.start()

        @pl.when(mid)
        def _():
            for j, chip in enumerate(chips):
                for a in arrs:
                    copy(a, 1 + j, (*chip, c), me).wait_recv()
                    passed(a)[j].start()

        @pl.when(last)
        def _():
            for a in arrs:
                copy(a, 0, sibling, me).wait_recv()
                for j, chip in enumerate(chips):
                    copy(a, 4 + j, (*chip, 1 - c), me).wait_recv()
                for cp in firsts(a) + passed(a):
                    cp.wait_send()


def _hook_specs(hook):
    if hook is None:
        return [], [], [], []
    na = len(hook.arrs)
    sems = [pltpu.SemaphoreType.DMA((hook.nsem,)), pltpu.SemaphoreType.DMA((hook.nsem,))]
    return [_ANY] * na, [_ANY] * na, hook.out_shape, sems


def _stage_fwd(st, t, params, inputs, hook=None):
    g, tm = st.g, min(st.tm, t)
    n = t // tm
    npar, nin, ncar, nout = len(params), len(inputs), len(st.carry_shapes), len(st.out_pieces)
    h_in, h_out, h_shape, h_sems = _hook_specs(hook)
    nh = len(h_in)

    def body(*refs):
        p_refs = refs[:npar]
        x_refs = refs[npar:npar + nin]
        hi_refs = refs[npar + nin:npar + nin + nh]
        o = npar + nin + nh
        o_refs = refs[o:o + nout]
        s_refs = refs[o + nout:o + nout + ncar]
        ho_refs = refs[o + nout + ncar:o + nout + ncar + nh]
        c_scr = refs[o + nout + ncar + nh:o + nout + ncar + nh + ncar]
        gi, ni = pl.program_id(0), pl.program_id(1)
        if hook is not None:
            step = gi * n + ni
            hook.run(step == 0, step == (4 * g * n) // 5, step == g * n - 1, hi_refs, ho_refs, *refs[-2:])

        @pl.when(ni == 0)
        def _():
            for c in c_scr:
                c[...] = jnp.zeros(c.shape, F32)

        ps = [r[...].astype(F32) for r in p_refs]
        xs = _load_pieces(x_refs, st.in_pieces)
        cs = [c[...] for c in c_scr]
        for s, c in zip(s_refs, cs):
            s[...] = c
        outs, ncs = st.f(ps, xs, cs)
        _store_pieces(o_refs, st.out_pieces, outs)
        for c, v in zip(c_scr, ncs):
            c[...] = v

    in_specs = [_par_spec(p, pg, g) for p, pg in zip(params, st.par_per_g)]
    in_specs += [_row_spec(tm, sum(pc), off, n, False) for pc, off in zip(st.in_pieces, st.in_offs)]
    out_specs = [_row_spec(tm, sum(pc), 0, n, False) for pc in st.out_pieces]
    out_specs += [_carry_spec(s, n, False) for s in st.carry_shapes]
    out_shape = [jax.ShapeDtypeStruct((t, g * sum(pc)), dt) for pc, dt in zip(st.out_pieces, st.out_dtypes)]
    out_shape += [jax.ShapeDtypeStruct((g, n) + s, F32) for s in st.carry_shapes]
    res = pl.pallas_call(
        body, name=st.name + "_fwd", grid=(g, n), in_specs=in_specs + h_in, out_specs=out_specs + h_out,
        out_shape=out_shape + h_shape,
        scratch_shapes=[pltpu.VMEM(s, F32) for s in st.carry_shapes] + h_sems,
        compiler_params=_cparams(("arbitrary", "arbitrary")),
    )(*params, *inputs, *(hook.arrs if hook else []))
    if hook is not None:
        hook.results = list(res[nout + ncar:])
    return list(res[:nout]), list(res[nout:nout + ncar])


def _stage_bwd(st, t, params, inputs, saved, douts, dx_dtypes, hook=None):
    g, tm = st.g, min(st.tm, t)
    n = t // tm
    npar, nin, ncar = len(params), len(inputs), len(st.carry_shapes)
    flat_d = [d for ds in douts for d in ds]
    nd = len(flat_d)
    dx_idx = [i for i, dt in enumerate(dx_dtypes) if dt is not None]
    h_in, h_out, h_shape, h_sems = _hook_specs(hook)
    nh = len(h_in)

    def body(*refs):
        p_refs = refs[:npar]
        x_refs = refs[npar:npar + nin]
        s_refs = refs[npar + nin:npar + nin + ncar]
        d_refs = refs[npar + nin + ncar:npar + nin + ncar + nd]
        hi_refs = refs[npar + nin + ncar + nd:npar + nin + ncar + nd + nh]
        o = npar + nin + ncar + nd + nh
        dp_refs = refs[o:o + npar]
        dx_refs = refs[o + npar:o + npar + len(dx_idx)]
        ho_refs = refs[o + npar + len(dx_idx):o + npar + len(dx_idx) + nh]
        dc_scr = refs[o + npar + len(dx_idx) + nh:o + npar + len(dx_idx) + nh + ncar]
        gi, ni = pl.program_id(0), pl.program_id(1)
        if hook is not None:
            step = gi * n + ni
            hook.run(step == 0, step == (4 * g * n) // 5, step == g * n - 1, hi_refs, ho_refs, *refs[-2:])

        @pl.when(ni == 0)
        def _():
            for c in dc_scr:
                c[...] = jnp.zeros(c.shape, F32)

        ps = [r[...].astype(F32) for r in p_refs]
        xs = _load_pieces(x_refs, st.in_pieces)
        cs = [s[...] for s in s_refs]
        dys = []
        k = 0
        for ds, pieces in zip(douts, st.out_pieces):
            acc = _load_pieces([d_refs[k]], [pieces])
            for j in range(1, len(ds)):
                more = _load_pieces([d_refs[k + j]], [pieces])
                acc = [a + b for a, b in zip(acc, more)]
            dys += acc
            k += len(ds)
        _, vjp = jax.vjp(st.f, ps, xs, cs)
        dps, dxs, dcs = vjp((dys, [c[...] for c in dc_scr]))
        k = 0
        per_in = []
        for pieces in st.in_pieces:
            per_in.append(dxs[k:k + len(pieces)])
            k += len(pieces)
        for ref, i in zip(dx_refs, dx_idx):
            _store_pieces([ref], [st.in_pieces[i]], per_in[i])
        for c, v in zip(dc_scr, dcs):
            c[...] = v
        for ref, dp, pg in zip(dp_refs, dps, st.par_per_g):
            first = (ni == 0) if pg else ((ni == 0) & (gi == 0))

            @pl.when(first)
            def _():
                ref[...] = jnp.zeros(ref.shape, F32)

            ref[...] += dp

    in_specs = [_par_spec(p, pg, g) for p, pg in zip(params, st.par_per_g)]
    in_specs += [_row_spec(tm, sum(pc), off, n, True) for pc, off in zip(st.in_pieces, st.in_offs)]
    in_specs += [_carry_spec(s, n, True) for s in st.carry_shapes]
    for ds, pc in zip(douts, st.out_pieces):
        in_specs += [_row_spec(tm, sum(pc), 0, n, True) for _ in ds]
    out_specs = [_par_spec(p, pg, g) for p, pg in zip(params, st.par_per_g)]
    out_specs += [_row_spec(tm, sum(st.in_pieces[i]), 0, n, True) for i in dx_idx]
    out_shape = [jax.ShapeDtypeStruct(p.shape, F32) for p in params]
    out_shape += [jax.ShapeDtypeStruct((t, g * sum(st.in_pieces[i])), dx_dtypes[i]) for i in dx_idx]
    res = pl.pallas_call(
        body, name=st.name + "_bwd", grid=(g, n), in_specs=in_specs + h_in, out_specs=out_specs + h_out,
        out_shape=out_shape + h_shape,
        scratch_shapes=[pltpu.VMEM(s, F32) for s in st.carry_shapes] + h_sems,
        compiler_params=_cparams(("arbitrary", "arbitrary")),
    )(*params, *inputs, *saved, *flat_d, *(hook.arrs if hook else []))
    if hook is not None:
        hook.results = list(res[npar + len(dx_idx):])
    return list(res[:npar]), list(res[npar:npar + len(dx_idx)])


def _pick(n, cap):
    if n <= cap:
        return n
    best = LANES
    for k in range(1, n // LANES + 1):
        if (n // LANES) % k == 0 and k * LANES <= cap:
            best = k * LANES
    return best


def _mm(name, a, b, mode, out_dtype=F32, tm=1024, tn=512, b_outer=False, token=None):
    m = a.shape[1] if mode == "tn" else a.shape[0]
    k = a.shape[0] if mode == "tn" else a.shape[1]
    n = b.shape[0] if mode == "nt" else b.shape[1]
    tm, tn = _pick(m, tm), _pick(n, tn)
    if b_outer:
        grid = (n // tn, m // tm)
        ij = lambda p, q: (q, p)
    else:
        grid = (m // tm, n // tn)
        ij = lambda p, q: (p, q)
    extra = [] if token is None else [token]

    def body(*refs):
        a_ref, b_ref, o_ref = refs[0], refs[1], refs[-1]
        o_ref[...] = _raw_dot(a_ref[...], b_ref[...], mode).astype(o_ref.dtype)

    if mode == "tn":
        a_spec = pl.BlockSpec((k, tm), lambda p, q: (0, ij(p, q)[0]))
    else:
        a_spec = pl.BlockSpec((tm, k), lambda p, q: (ij(p, q)[0], 0))
    if mode == "nt":
        b_spec = pl.BlockSpec((tn, k), lambda p, q: (ij(p, q)[1], 0))
    else:
        b_spec = pl.BlockSpec((k, tn), lambda p, q: (0, ij(p, q)[1]))
    return pl.pallas_call(
        body, name=name, grid=grid,
        in_specs=[a_spec, b_spec] + [pl.BlockSpec(e.shape, lambda p, q: (0, 0)) for e in extra],
        out_specs=pl.BlockSpec((tm, tn), lambda p, q: ij(p, q)),
        out_shape=jax.ShapeDtypeStruct((m, n), out_dtype),
        compiler_params=_cparams(("arbitrary", "arbitrary")),
    )(a, b, *extra)


def _loss_stage(t, g_post, h1, ff, tgt):
    tm = min(256, t)
    n = t // tm

    def body(g_ref, h_ref, f_ref, t_ref, loss_ref, dg_ref, dh_ref, df_ref):
        ni = pl.program_id(0)
        target = t_ref[...]

        def lossf(g, h1, ff):
            e = h1 + _rms(ff, g) - target
            return 0.5 * jnp.sum(jnp.mean(e * e, axis=-1))

        l, (dg, dh, df) = jax.value_and_grad(lossf, argnums=(0, 1, 2))(g_ref[...], h_ref[...], f_ref[...])

        @pl.when(ni == 0)
        def _():
            loss_ref[...] = jnp.zeros(loss_ref.shape, F32)
            dg_ref[...] = jnp.zeros(dg_ref.shape, F32)

        loss_ref[...] += jnp.full(loss_ref.shape, l, F32)
        dg_ref[...] += dg
        dh_ref[...] = dh
        df_ref[...] = df.astype(df_ref.dtype)

    row = pl.BlockSpec((tm, D), lambda ni: (ni, 0))
    one = pl.BlockSpec((1, D), lambda ni: (0, 0))
    return pl.pallas_call(
        body, name="loss_head", grid=(n,), in_specs=[one, row, row, row],
        out_specs=[pl.BlockSpec((1, LANES), lambda ni: (0, 0)), one, row, row],
        out_shape=[jax.ShapeDtypeStruct((1, LANES), F32), jax.ShapeDtypeStruct((1, D), F32),
                   jax.ShapeDtypeStruct((t, D), F32), jax.ShapeDtypeStruct((t, D), BF)],
        compiler_params=_cparams(("arbitrary",)),
    )(g_post, h1, ff, tgt)


_ANY = pl.BlockSpec(memory_space=pl.ANY)


def _all_gather(name, blks):
    na = len(blks)

    def body(*refs):
        x_refs, out_refs = refs[:na], refs[na:2 * na]
        send_sems, recv_sems, local_sems = refs[2 * na:]
        x, y, cc = lax.axis_index("x"), lax.axis_index("y"), lax.axis_index("c")
        me, sibling = (x, y, cc), (x, y, 1 - cc)
        chips = [(1 - x, y), (x, 1 - y), (1 - x, 1 - y)]

        def copy(a, k, block, to, src=None):
            dst = out_refs[a].at[4 * block[0] + 2 * block[1] + block[2]]
            return pltpu.make_async_remote_copy(
                src_ref=dst if src is None else src, dst_ref=dst, send_sem=send_sems.at[7 * a + k],
                recv_sem=recv_sems.at[7 * a + k], device_id=to, device_id_type=MESH)

        mine, first, passed = [], [], []
        for a in range(na):
            m = pltpu.make_async_copy(x_refs[a], out_refs[a].at[4 * x + 2 * y + cc], local_sems.at[a])
            m.start()
            mine.append(m)
            cps = [copy(a, 0, me, sibling, src=x_refs[a])]
            cps += [copy(a, 1 + j, me, (*chip, cc), src=x_refs[a]) for j, chip in enumerate(chips)]
            for cp in cps:
                cp.start()
            first += cps
        for j, chip in enumerate(chips):
            for a in range(na):
                copy(a, 1 + j, (*chip, cc), me).wait_recv()
                fw = copy(a, 4 + j, (*chip, cc), sibling)
                fw.start()
                passed.append(fw)
        for a in range(na):
            copy(a, 0, sibling, me).wait_recv()
            for j, chip in enumerate(chips):
                copy(a, 4 + j, (*chip, 1 - cc), me).wait_recv()
        for cp in first + passed:
            cp.wait_send()
        for m in mine:
            m.wait()

    res = pl.pallas_call(
        body, name=name, in_specs=[_ANY] * na, out_specs=[_ANY] * na,
        out_shape=[jax.ShapeDtypeStruct((N_DEV,) + b.shape, b.dtype) for b in blks],
        scratch_shapes=[pltpu.SemaphoreType.DMA((7 * na,)), pltpu.SemaphoreType.DMA((7 * na,)),
                        pltpu.SemaphoreType.DMA((na,))],
    )(*blks)
    return list(res)


def _reduce_pair(g8s):
    na = len(g8s)

    def body(*refs):
        g_refs, recv_refs = refs[:na], refs[na:2 * na]
        ssem, rsem = refs[2 * na:]
        x, y, cc = lax.axis_index("x"), lax.axis_index("y"), lax.axis_index("c")
        chips = [(x, y), (1 - x, y), (x, 1 - y), (1 - x, 1 - y)]
        sib = (x, y, 1 - cc)
        for a in range(na):
            for k, (cx, cy) in enumerate(chips):
                pltpu.make_async_remote_copy(
                    src_ref=g_refs[a].at[4 * cx + 2 * cy + 1 - cc], dst_ref=recv_refs[a].at[k],
                    send_sem=ssem.at[a], recv_sem=rsem.at[a], device_id=sib, device_id_type=MESH).start()
        for a in range(na):
            pltpu.make_async_remote_copy(src_ref=recv_refs[a], dst_ref=recv_refs[a], send_sem=ssem.at[a],
                                         recv_sem=rsem.at[a], device_id=sib, device_id_type=MESH).wait()

    res = pl.pallas_call(
        body, name="reduce_pair", in_specs=[_ANY] * na, out_specs=[_ANY] * na,
        out_shape=[jax.ShapeDtypeStruct((4,) + g.shape[1:], g.dtype) for g in g8s],
        scratch_shapes=[pltpu.SemaphoreType.DMA((na,)), pltpu.SemaphoreType.DMA((na,))],
    )(*g8s)
    return list(res)


_HBM = pl.BlockSpec(memory_space=pltpu.HBM)
_SEM = pl.BlockSpec(memory_space=pltpu.SEMAPHORE)
_EFFECT = pltpu.SideEffectType.DATAFLOW_SIDE_EFFECTING


def _chip_swap_copies(s_refs, land_refs, ssem, rsem):
    x, y, c = lax.axis_index("x"), lax.axis_index("y"), lax.axis_index("c")
    targets = [(1 - x, y, c), (x, 1 - y, c), (1 - x, 1 - y, c)]
    return [pltpu.make_async_remote_copy(src_ref=s.at[k], dst_ref=d.at[k], send_sem=ssem.at[3 * a + k],
                                         recv_sem=rsem.at[3 * a + k], device_id=targets[k], device_id_type=MESH)
            for a, (s, d) in enumerate(zip(s_refs, land_refs)) for k in range(3)]


def _chip_swap_start(sends):
    na = len(sends)

    def body(*refs):
        cps = _chip_swap_copies(refs[:na], refs[na:2 * na], refs[2 * na], refs[2 * na + 1])
        for cp in cps:
            cp.start()
        token = refs[-1]
        token[...] = jnp.zeros(token.shape, token.dtype)

    bufs = [pltpu.HBM(s.shape, s.dtype) for s in sends]
    res = pl.pallas_call(
        body, name="chip_swap_start",
        out_shape=[pltpu.SemaphoreType.DMA((3 * na,)), pltpu.SemaphoreType.DMA((3 * na,))] + bufs + bufs
        + [jax.ShapeDtypeStruct((8, LANES), F32)],
        in_specs=[_HBM] * (2 * na), out_specs=[_SEM, _SEM] + [_HBM] * (2 * na) + [pl.BlockSpec(memory_space=pltpu.VMEM)],
        input_output_aliases={i: 2 + i for i in range(2 * na)},
        compiler_params=pltpu.CompilerParams(has_side_effects=_EFFECT),
    )(*[pltpu.with_memory_space_constraint(s, pltpu.HBM) for s in sends],
      *[pltpu.with_memory_space_constraint(lax.empty(s.shape, s.dtype), pltpu.HBM) for s in sends])
    return res[0], res[1], list(res[2:2 + na]), list(res[2 + na:2 + 2 * na]), res[-1]


def _chip_swap_wait(ssem, rsem, srcs, lands, after):
    na = len(srcs)

    def body(*refs):
        cps = _chip_swap_copies(refs[:na], refs[na:2 * na], refs[2 * na], refs[2 * na + 1])
        for cp in cps:
            cp.wait_send()
            cp.wait_recv()

    bufs = [pltpu.HBM(s.shape, s.dtype) for s in srcs]
    res = pl.pallas_call(
        body, name="chip_swap_wait", out_shape=bufs + bufs,
        in_specs=[_HBM] * (2 * na) + [_SEM, _SEM, _ANY], out_specs=[_HBM] * (2 * na),
        input_output_aliases={i: i for i in range(2 * na)},
        compiler_params=pltpu.CompilerParams(has_side_effects=_EFFECT),
    )(*srcs, *lands, ssem, rsem, after)
    return list(res[na:])


def _pick_rows(r, c, budget=TILE_BYTES):
    if r * c * 4 <= budget or r % 16:
        return r
    best = 16
    for tr in range(16, r, 16):
        if r % tr == 0 and tr * c * 4 <= budget:
            best = tr
    return best


def _pair_sum(name, idx4, g8, recv4):
    _, r, c = g8.shape
    tr = _pick_rows(r, c, 2 * TILE_BYTES)

    def body(idx_ref, a_ref, b_ref, o0_ref, o3_ref):
        k = pl.program_id(1)
        s = a_ref[...].astype(F32) + b_ref[...].astype(F32)

        @pl.when(k == 0)
        def _():
            o0_ref[...] = s

        @pl.when(k > 0)
        def _():
            o3_ref[...] = s.astype(BF)

    spec = pltpu.PrefetchScalarGridSpec(
        num_scalar_prefetch=1, grid=(r // tr, 4),
        in_specs=[pl.BlockSpec((None, tr, c), lambda i, k, idx: (idx[k], i, 0)),
                  pl.BlockSpec((None, tr, c), lambda i, k, idx: (k, i, 0))],
        out_specs=[pl.BlockSpec((tr, c), lambda i, k, idx: (i, 0)),
                   pl.BlockSpec((None, tr, c), lambda i, k, idx: (jnp.maximum(k - 1, 0), i, 0))])
    return pl.pallas_call(
        body, name=name, grid_spec=spec,
        out_shape=[jax.ShapeDtypeStruct((r, c), F32), jax.ShapeDtypeStruct((3, r, c), BF)],
        compiler_params=_cparams(("arbitrary", "arbitrary")),
    )(idx4, g8, recv4)


def _adamw(w, g, m, v):
    m = ADAM_B1 * m + (1.0 - ADAM_B1) * g
    v = ADAM_B2 * v + (1.0 - ADAM_B2) * jnp.square(g)
    m_hat = m / (1.0 - ADAM_B1 ** ADAM_STEP)
    v_hat = v / (1.0 - ADAM_B2 ** ADAM_STEP)
    delta = -ADAM_LR * (m_hat / (jnp.sqrt(v_hat) + ADAM_EPS) + ADAM_WD * w)
    return delta, m, v


def _sum_partials(name, idx1, own, recv):
    _, r, c = own.shape
    tr = _pick_rows(r, c, 2 * TILE_BYTES)
    nj = recv.shape[0]

    def body(idx_ref, p_ref, r_ref, g_out):
        g = p_ref[...].astype(F32)
        for k in range(nj):
            g = g + r_ref[k].astype(F32)
        g_out[...] = g

    row = pl.BlockSpec((tr, c), lambda i, idx: (i, 0))
    spec = pltpu.PrefetchScalarGridSpec(
        num_scalar_prefetch=1, grid=(r // tr,),
        in_specs=[pl.BlockSpec((None, tr, c), lambda i, idx: (idx[0], i, 0)),
                  pl.BlockSpec((nj, tr, c), lambda i, idx: (0, i, 0))],
        out_specs=row)
    return pl.pallas_call(body, name=name, grid_spec=spec, out_shape=jax.ShapeDtypeStruct((r, c), F32),
                          compiler_params=_cparams(("arbitrary",)))(idx1, own, recv)


def _adam_sharded(name, idx1, own, recv, w, m, v):
    r, c = w.shape
    tr = _pick_rows(r, c)
    nj = 0 if recv is None else recv.shape[0]
    if recv is None:
        recv = jnp.zeros((1, 8, LANES), BF)

    def body(idx_ref, p_ref, r_ref, w_ref, m_ref, v_ref, g_out, d_out, m_out, v_out):
        g = p_ref[...].astype(F32)
        for k in range(nj):
            g = g + r_ref[k].astype(F32)
        d, mn, vn = _adamw(w_ref[...], g, m_ref[...], v_ref[...])
        g_out[...] = g
        d_out[...] = d
        m_out[...] = mn
        v_out[...] = vn

    row = pl.BlockSpec((tr, c), lambda i, idx: (i, 0))
    if nj:
        recv_spec = pl.BlockSpec((nj, tr, c), lambda i, idx: (0, i, 0))
    else:
        recv_spec = pl.BlockSpec(recv.shape, lambda i, idx: (0, 0, 0))
    spec = pltpu.PrefetchScalarGridSpec(
        num_scalar_prefetch=1, grid=(r // tr,),
        in_specs=[pl.BlockSpec((None, tr, c), lambda i, idx: (idx[0], i, 0)), recv_spec, row, row, row],
        out_specs=[row] * 4)
    return pl.pallas_call(
        body, name=name, grid_spec=spec, out_shape=[jax.ShapeDtypeStruct((r, c), F32)] * 4,
        compiler_params=_cparams(("arbitrary",)),
    )(idx1, own, recv, w, m, v)


def _repl_rows():
    rows, r = {}, 0
    for name, cols in REPL:
        rows[name] = r
        r += REPL_ROWS.get(name, 1) * ((cols + D - 1) // D)
    return rows


def _pack_replicated(grads):
    rows = _repl_rows()
    names = [n for n, _ in REPL]

    def body(*refs):
        o_ref = refs[-1]
        o_ref[...] = jnp.zeros(o_ref.shape, F32)
        for name, ref in zip(names, refs[:-1]):
            r0 = rows[name]
            nr, nc = ref.shape
            if nc <= D:
                o_ref[r0:r0 + nr, 0:nc] = ref[...]
            else:
                for j in range((nc + D - 1) // D):
                    lo, hi = j * D, min(nc, (j + 1) * D)
                    o_ref[r0 + j:r0 + j + 1, 0:hi - lo] = ref[:, lo:hi]

    return pl.pallas_call(body, name="pack_replicated", out_shape=jax.ShapeDtypeStruct((REPL_TOTAL, D), F32),
                          compiler_params=_cparams())(*[grads[n] for n in names])


def _adam_replicated(g8, ws, ms, vs):
    rows = _repl_rows()
    names = [n for n, _ in REPL]
    np_ = len(names)

    def body(*refs):
        g_ref = refs[0]
        w_refs, m_refs, v_refs = refs[1:1 + np_], refs[1 + np_:1 + 2 * np_], refs[1 + 2 * np_:1 + 3 * np_]
        outs = refs[1 + 3 * np_:1 + 7 * np_]
        scr = refs[-1]
        g = g_ref[0]
        for k in range(1, N_DEV):
            g = g + g_ref[k]
        scr[...] = g
        for i, name in enumerate(names):
            r0 = rows[name]
            nr, nc = w_refs[i].shape
            if nc <= D:
                gi = scr[r0:r0 + nr, 0:nc]
            else:
                parts = []
                for j in range((nc + D - 1) // D):
                    lo, hi = j * D, min(nc, (j + 1) * D)
                    parts.append(scr[r0 + j:r0 + j + 1, 0:hi - lo])
                gi = jnp.concatenate(parts, axis=1)
            d, mn, vn = _adamw(w_refs[i][...], gi, m_refs[i][...], v_refs[i][...])
            outs[i][...] = gi
            outs[np_ + i][...] = d
            outs[2 * np_ + i][...] = mn
            outs[3 * np_ + i][...] = vn

    shp = [jax.ShapeDtypeStruct(w.shape, F32) for w in ws]
    res = pl.pallas_call(body, name="adam_replicated", out_shape=shp * 4,
                         scratch_shapes=[pltpu.VMEM((REPL_TOTAL, D), F32)], compiler_params=_cparams(),
                         )(g8, *ws, *ms, *vs)
    return [dict(zip(names, res[k * np_:(k + 1) * np_])) for k in range(4)]


_WEIGHTS = ("attn_pre_norm", "w_in", "hgrn_lb", "hgrn_gnorm", "w_branch_a", "rwkv_mu", "rwkv_w0", "rwkv_w2",
            "rwkv_a0", "rwkv_a2", "rwkv_g2", "rwkv_k_k", "rwkv_k_a", "rwkv_r_k", "rwkv_ln_w", "rwkv_ln_b",
            "w_branch_b", "w_out", "attn_post_norm", "ffn_pre_norm", "w_up", "conv_w", "conv_b", "w_down",
            "ffn_post_norm")
_BIG = ("w_in", "w_up", "w_down", "w_branch_a", "w_branch_b", "w_out")


def _stages():
    one = [D]
    hw = HG_K * HG_PER_STEP
    rw = LANES * RW_PAIRS_PER_STEP
    return dict(
        pre1=_Stage("pre1", _f_pre1, 1, 256, [False], [one], [0], [], [one], [BF]),
        pre1_res=_Stage("pre1", _f_pre1_residual, 1, 256, [False], [one], [0], [], [one, one], [BF, F32]),
        mixers=_Stage("mixers", _f_mixers, 1, RW_CHUNK, [False] * 13, [[D] * 7 + [LANES, LANES]], [0],
                      [(hw, HG_K), (1, RW_COLS), (rw, LANES)], [one, one], [BF, BF]),
        merge=_Stage("merge", _f_merge, 4, 512, [], [[256]] * 4, [29, 33, 0, 0], [], [[256]], [BF]),
        post1=_Stage("post1", _f_post1, 1, 256, [False, False], [one, one], [0, 0], [], [one, one], [F32, BF]),
        conv=_Stage("conv", _f_conv, 1, 128, [False, False], [[DFF, DFF]], [0], [(1, 2 * DFF), (1, 2 * DFF)],
                    [[DFF]], [BF]),
    )


def _cols_to_blocks(w, per):
    return w.reshape(w.shape[0], N_DEV, per).transpose(1, 0, 2)


def _blocks_to_cols(g):
    return g.transpose(1, 0, 2).reshape(g.shape[1], N_DEV * g.shape[2])


def kernel(x, attn_pre_norm, w_in, hgrn_lb, hgrn_gnorm, w_branch_a, rwkv_mu, rwkv_w0, rwkv_w2, rwkv_a0, rwkv_a2, rwkv_g2, rwkv_k_k, rwkv_k_a, rwkv_r_k, rwkv_ln_w, rwkv_ln_b, w_branch_b, w_out, attn_post_norm, ffn_pre_norm, w_up, conv_w, conv_b, w_down, ffn_post_norm, loss_target, m_attn_pre_norm, m_w_in, m_hgrn_lb, m_hgrn_gnorm, m_w_branch_a, m_rwkv_mu, m_rwkv_w0, m_rwkv_w2, m_rwkv_a0, m_rwkv_a2, m_rwkv_g2, m_rwkv_k_k, m_rwkv_k_a, m_rwkv_r_k, m_rwkv_ln_w, m_rwkv_ln_b, m_w_branch_b, m_w_out, m_attn_post_norm, m_ffn_pre_norm, m_w_up, m_conv_w, m_conv_b, m_w_down, m_ffn_post_norm, v_attn_pre_norm, v_w_in, v_hgrn_lb, v_hgrn_gnorm, v_w_branch_a, v_rwkv_mu, v_rwkv_w0, v_rwkv_w2, v_rwkv_a0, v_rwkv_a2, v_rwkv_g2, v_rwkv_k_k, v_rwkv_k_a, v_rwkv_r_k, v_rwkv_ln_w, v_rwkv_ln_b, v_w_branch_b, v_w_out, v_attn_post_norm, v_ffn_pre_norm, v_w_up, v_conv_w, v_conv_b, v_w_down, v_ffn_post_norm):
    w = dict(attn_pre_norm=attn_pre_norm, w_in=w_in, hgrn_lb=hgrn_lb, hgrn_gnorm=hgrn_gnorm, w_branch_a=w_branch_a, rwkv_mu=rwkv_mu, rwkv_w0=rwkv_w0, rwkv_w2=rwkv_w2, rwkv_a0=rwkv_a0, rwkv_a2=rwkv_a2, rwkv_g2=rwkv_g2, rwkv_k_k=rwkv_k_k, rwkv_k_a=rwkv_k_a, rwkv_r_k=rwkv_r_k, rwkv_ln_w=rwkv_ln_w, rwkv_ln_b=rwkv_ln_b, w_branch_b=w_branch_b, w_out=w_out, attn_post_norm=attn_post_norm, ffn_pre_norm=ffn_pre_norm, w_up=w_up, conv_w=conv_w, conv_b=conv_b, w_down=w_down, ffn_post_norm=ffn_post_norm)
    mo = dict(attn_pre_norm=m_attn_pre_norm, w_in=m_w_in, hgrn_lb=m_hgrn_lb, hgrn_gnorm=m_hgrn_gnorm, w_branch_a=m_w_branch_a, rwkv_mu=m_rwkv_mu, rwkv_w0=m_rwkv_w0, rwkv_w2=m_rwkv_w2, rwkv_a0=m_rwkv_a0, rwkv_a2=m_rwkv_a2, rwkv_g2=m_rwkv_g2, rwkv_k_k=m_rwkv_k_k, rwkv_k_a=m_rwkv_k_a, rwkv_r_k=m_rwkv_r_k, rwkv_ln_w=m_rwkv_ln_w, rwkv_ln_b=m_rwkv_ln_b, w_branch_b=m_w_branch_b, w_out=m_w_out, attn_post_norm=m_attn_post_norm, ffn_pre_norm=m_ffn_pre_norm, w_up=m_w_up, conv_w=m_conv_w, conv_b=m_conv_b, w_down=m_w_down, ffn_post_norm=m_ffn_post_norm)
    vo = dict(attn_pre_norm=v_attn_pre_norm, w_in=v_w_in, hgrn_lb=v_hgrn_lb, hgrn_gnorm=v_hgrn_gnorm, w_branch_a=v_w_branch_a, rwkv_mu=v_rwkv_mu, rwkv_w0=v_rwkv_w0, rwkv_w2=v_rwkv_w2, rwkv_a0=v_rwkv_a0, rwkv_a2=v_rwkv_a2, rwkv_g2=v_rwkv_g2, rwkv_k_k=v_rwkv_k_k, rwkv_k_a=v_rwkv_k_a, rwkv_r_k=v_rwkv_r_k, rwkv_ln_w=v_rwkv_ln_w, rwkv_ln_b=v_rwkv_ln_b, w_branch_b=v_w_branch_b, w_out=v_w_out, attn_post_norm=v_attn_post_norm, ffn_pre_norm=v_ffn_pre_norm, w_up=v_w_up, conv_w=v_conv_w, conv_b=v_conv_b, w_down=v_w_down, ffn_post_norm=v_ffn_post_norm)

    t = x.shape[1]
    x2 = x.reshape(t, D)
    tgt = loss_target.reshape(t, D)
    st = _stages()

    me = 4 * lax.axis_index("x") + 2 * lax.axis_index("y") + lax.axis_index("c")
    small = jnp.concatenate([rwkv_w2[0], rwkv_a2[0], rwkv_g2[0]], axis=0).astype(BF)
    g_in, g_small = _all_gather("gather_weights", [w_in[0].T.astype(BF), small])
    fw_in_t = g_in.reshape(IN_COLS, D)
    z64 = jnp.zeros((64, D), BF)
    w2p = jnp.concatenate([_blocks_to_cols(g_small[:, 0:64]), z64], axis=0)
    a2p = jnp.concatenate([z64, _blocks_to_cols(g_small[:, 64:128])], axis=0)
    g2f = _blocks_to_cols(g_small[:, 128:256])
    conv_bits = lax.bitcast_convert_type(conv_w[0], BF).reshape(3, 2 * 704)
    late = [w_up[0].T.astype(BF)] + [w[k][0].astype(BF) for k in _BIG[2:]] + [conv_bits]
    late_gather = _Exchange("gather2", late)
    r_k = rwkv_r_k.reshape(1, D)

    (xn,), _ = _stage_fwd(st["pre1"], t, [attn_pre_norm], [x2])
    z = _mm("in_proj", xn, fw_in_t, "nt", F32, tm=512, tn=4736, b_outer=True)
    mix_par = [hgrn_lb, hgrn_gnorm, rwkv_mu, rwkv_w0, w2p, rwkv_a0, a2p, g2f, rwkv_k_k, rwkv_k_a,
               rwkv_ln_w, rwkv_ln_b, r_k]
    mix_in = [z]
    (o_a, o_b), mix_saved = _stage_fwd(st["mixers"], t, mix_par, mix_in, hook=late_gather)
    gl = [lax.dynamic_update_slice(g, own[None], (me, 0, 0)) for g, own in zip(late_gather.results, late)]
    fw_up_t = gl[0].reshape(2 * DFF, D)
    fw_down = gl[1].reshape(DFF, D)
    fw_a, fw_b, fw_out = (g.reshape(D, D) for g in gl[2:5])
    conv_full = _blocks_to_cols(lax.bitcast_convert_type(gl[5].reshape(N_DEV, 3, 704, 2), F32))
    y_a = _mm("branch_a", o_a, fw_a, "nn")
    y_b = _mm("branch_b", o_b, fw_b, "nn")
    (merged,), _ = _stage_fwd(st["merge"], t, [], [z, z, y_a, y_b])
    mix = _mm("out_proj", merged, fw_out, "nn")
    (h1, xn2), _ = _stage_fwd(st["post1"], t, [attn_post_norm, ffn_pre_norm], [x2, mix])
    hu = _mm("up_proj", xn2, fw_up_t, "nt", F32, tm=1024, tn=1408)
    conv_par = [conv_full, conv_b]
    (act,), conv_saved = _stage_fwd(st["conv"], t, conv_par, [hu])
    ff = _mm("down_proj", act, fw_down, "nn")

    loss_acc, d_ffn_post, dh1, dff = _loss_stage(t, ffn_post_norm, h1, ff, tgt)
    dact = _mm("d_act", dff, fw_down, "nt", F32, tm=1024, tn=1408)
    dw_down = _mm("dw_down", act, dff, "tn", BF, tm=1408, tn=512)
    (dcw, dcb), (dhu,) = _stage_bwd(st["conv"], t, conv_par, [hu], conv_saved, [[dact]], [BF])
    dxn2 = _mm("d_xn2", dhu, fw_up_t, "nn", F32, tm=1024, tn=256)
    dw_up_t = _mm("dw_up", dhu, xn2, "tn", BF, tm=1408, tn=1024)
    (d_post, d_pre2), (dx_a, dmix) = _stage_bwd(st["post1"], t, [attn_post_norm, ffn_pre_norm], [x2, mix], [],
                                                 [[dh1], [dxn2]], [F32, BF])
    dmerged = _mm("d_merged", dmix, fw_out, "nt")
    dw_out = _mm("dw_out", merged, dmix, "tn", BF)
    _, (dga, dgb, dy_a, dy_b) = _stage_bwd(st["merge"], t, [], [z, z, y_a, y_b], [], [[dmerged]], [BF, BF, BF, BF])
    do_a = _mm("d_oa", dy_a, fw_a, "nt")
    dw_a = _mm("dw_a", o_a, dy_a, "tn", BF)
    do_b = _mm("d_ob", dy_b, fw_b, "nt")
    dw_b = _mm("dw_b", o_b, dy_b, "tn", BF)
    early = [dw_up_t.reshape(N_DEV, 704, D), dw_down.reshape(N_DEV, 352, D), dw_a.reshape(N_DEV, 128, D),
             dw_b.reshape(N_DEV, 128, D), dw_out.reshape(N_DEV, 128, D), _cols_to_blocks(dcw.astype(BF), 704)]
    early_scatter = _Exchange("scatter", early)
    mix_dp, dz_hr = _stage_bwd(st["mixers"], t, mix_par, mix_in, mix_saved, [[do_a], [do_b]], [BF],
                               hook=early_scatter)
    d_lb, d_gn, d_mu, d_w0, d_w2p, d_a0, d_a2p, d_g2, d_kk, d_ka, d_lnw, d_lnb, d_rk = mix_dp
    dz = jnp.concatenate(dz_hr + [dga, dgb], axis=1)
    dw_in_t = _mm("dw_in", dz, xn, "tn", BF, tm=256, tn=1024)

    ax, ay, ac = lax.axis_index("x"), lax.axis_index("y"), lax.axis_index("c")
    idx4 = jnp.stack([4 * cx + 2 * cy + ac for cx, cy in ((ax, ay), (1 - ax, ay), (ax, 1 - ay), (1 - ax, 1 - ay))])
    idx4 = idx4.astype(jnp.int32)
    idx_me, idx_0 = idx4[0:1], jnp.zeros((1,), jnp.int32)
    d_small = jnp.concatenate([d_w2p[:64], d_a2p[64:], d_g2], axis=0).astype(BF)
    g8s = [dw_in_t.reshape(N_DEV, 1184, D), _cols_to_blocks(d_small, LANES)]
    recv4s = _reduce_pair(g8s)
    sums = [_pair_sum("pair_sum_" + n, idx4, g, r) for n, g, r in zip(("w_in", "small"), g8s, recv4s)]
    swap_ssem, swap_rsem, swap_srcs, swap_lands, token = _chip_swap_start([s[1] for s in sums])
    dxn = _mm("d_xn", dz, fw_in_t, "nn", F32, tm=512, tn=512, b_outer=True, token=token)
    (d_pre1,), (dx,) = _stage_bwd(st["pre1_res"], t, [attn_pre_norm], [x2], [], [[dxn], [dx_a]], [F32])
    grad_x = dx.reshape(x.shape)
    loss = lax.psum(loss_acc[0, 0], ("x", "y", "c"))

    rg = dict(attn_pre_norm=d_pre1, hgrn_lb=d_lb, hgrn_gnorm=d_gn, rwkv_mu=d_mu, rwkv_w0=d_w0, rwkv_a0=d_a0,
              rwkv_k_k=d_kk, rwkv_k_a=d_ka, rwkv_r_k=d_rk, rwkv_ln_w=d_lnw, rwkv_ln_b=d_lnb, attn_post_norm=d_post,
              ffn_pre_norm=d_pre2, conv_b=dcb, ffn_post_norm=d_ffn_post)
    (g8,) = _all_gather("gather_small_grads", [_pack_replicated(rg)])
    rnames = [n for n, _ in REPL]
    flat = lambda src: [src[n].reshape(1, D) if n == "rwkv_r_k" else src[n] for n in rnames]
    rp_out = _adam_replicated(g8, flat(w), flat(mo), flat(vo))
    recv3s = _chip_swap_wait(swap_ssem, swap_rsem, swap_srcs, swap_lands, rp_out[0]["attn_pre_norm"])
    for kind in range(4):
        rp_out[kind]["rwkv_r_k"] = rp_out[kind]["rwkv_r_k"].reshape(rwkv_r_k.shape)

    def small_of(src):
        return jnp.concatenate([src["rwkv_w2"][0], src["rwkv_a2"][0], src["rwkv_g2"][0]], axis=0)

    sh_out = [dict() for _ in range(4)]
    g_in = _sum_partials("sum_w_in", idx_0, sums[0][0][None], recv3s[0]).T
    res = _adam_sharded("adam_w_in", idx_0, g_in[None], None, *[src["w_in"][0] for src in (w, mo, vo)])
    res_s = _adam_sharded("adam_small", idx_0, sums[1][0][None], recv3s[1], *[small_of(src) for src in (w, mo, vo)])
    for kind in range(4):
        sh_out[kind]["w_in"] = res[kind][None]
        sh_out[kind]["rwkv_w2"] = res_s[kind][0:64][None]
        sh_out[kind]["rwkv_a2"] = res_s[kind][64:128][None]
        sh_out[kind]["rwkv_g2"] = res_s[kind][128:256][None]
    for n, own, recv in zip(_BIG[1:] + ("conv_w",), early, early_scatter.results):
        if n == "w_up":
            g_up = _sum_partials("sum_w_up", idx_me, own, recv).T
            res = _adam_sharded("adam_" + n, idx_0, g_up[None], None, *[src[n][0] for src in (w, mo, vo)])
        else:
            res = _adam_sharded("adam_" + n, idx_me, own, recv, *[src[n][0] for src in (w, mo, vo)])
        for kind in range(4):
            sh_out[kind][n] = res[kind][None]

    outs = [loss, grad_x]
    for kind in range(4):
        for name in _WEIGHTS:
            outs.append(sh_out[kind][name] if name in sh_out[kind] else rp_out[kind][name])
    return tuple(outs)
```

```python
import functools

import jax
import jax.numpy as jnp
from jax import lax
from jax.experimental import pallas as pl
from jax.experimental.pallas import tpu as pltpu

F32 = jnp.float32
BF = jnp.bfloat16
MESH = pl.DeviceIdType.MESH

D = 1024
HG_HEADS = 8
HG_K = 128
HG_CHUNK = 32
HG_SCALE = HG_K ** -0.5
HG_PER_STEP = 8
RW_HEADS = 16
RW_N = 64
RW_CHUNK = 64
RW_PAIRS_PER_STEP = 8
DFF = 2816
IN_COLS = 9472
RW_COLS = 3328
EPS = 1e-6
GN_EPS = 1e-5 * RW_N
ADAM_LR = 0.001
ADAM_B1 = 0.9
ADAM_B2 = 0.999
ADAM_EPS = 1e-08
ADAM_WD = 0.01
ADAM_STEP = 10
N_DEV = 8
LANES = 128
VMEM_LIMIT = 56 * 1024 * 1024
TILE_BYTES = 1280 * 1024

REPL = (("attn_pre_norm", 1024), ("hgrn_lb", 1024), ("hgrn_gnorm", 1024), ("rwkv_mu", 3328), ("rwkv_w0", 1024),
        ("rwkv_a0", 1024), ("rwkv_k_k", 1024), ("rwkv_k_a", 1024), ("rwkv_r_k", 1024), ("rwkv_ln_w", 1024),
        ("rwkv_ln_b", 1024), ("attn_post_norm", 1024), ("ffn_pre_norm", 1024), ("conv_b", 5632), ("ffn_post_norm", 1024))
REPL_ROWS = {"hgrn_lb": 2}
REPL_TOTAL = 32


def _cparams(sem=None, **kw):
    return pltpu.CompilerParams(dimension_semantics=sem, vmem_limit_bytes=VMEM_LIMIT, **kw)


_DN = {"nn": ((1,), (0,)), "nt": ((1,), (1,)), "tn": ((0,), (0,))}


def _raw_dot(a, b, mode):
    return lax.dot_general(a.astype(BF), b.astype(BF), (_DN[mode], ((), ())), preferred_element_type=F32)


@functools.partial(jax.custom_vjp, nondiff_argnums=(2,))
def _dot(a, b, mode):
    return _raw_dot(a, b, mode)


def _dot_fwd(a, b, mode):
    return _raw_dot(a, b, mode), (a, b)


def _dot_bwd(mode, res, g):
    a, b = res
    if mode == "nn":
        return _dot(g, b, "nt"), _dot(a, g, "tn")
    if mode == "nt":
        return _dot(g, b, "nn"), _dot(g, a, "tn")
    return _dot(b, g, "nt"), _dot(a, g, "nn")


_dot.defvjp(_dot_fwd, _dot_bwd)


def _bf_pieces(x, n):
    out, r = [], x
    for i in range(n):
        p = r.astype(BF)
        out.append(p)
        if i + 1 < n:
            r = r - p.astype(F32)
    return out


def _raw_split_dot(x, e, mode, n, x_left):
    eb = e.astype(BF)
    acc = None
    for p in _bf_pieces(x, n):
        ops = (p, eb) if x_left else (eb, p)
        t = lax.dot_general(*ops, (_DN[mode], ((), ())), preferred_element_type=F32)
        acc = t if acc is None else acc + t
    return acc


def _raw_headsum(x):
    t = x.shape[0]
    i = lax.broadcasted_iota(jnp.int32, (LANES, LANES), 0)
    j = lax.broadcasted_iota(jnp.int32, (LANES, LANES), 1)
    same = jnp.where((i >= RW_N) == (j >= RW_N), 1.0, 0.0).astype(F32)
    groups = x.shape[1] // LANES
    rows = jnp.concatenate([x[:, q * LANES:(q + 1) * LANES] for q in range(groups)], axis=0)
    s = _raw_split_dot(rows, same, "nn", 2, True)
    return jnp.concatenate([s[q * t:(q + 1) * t] for q in range(groups)], axis=1)


@jax.custom_vjp
def _headsum(x):
    return _raw_headsum(x)


def _headsum_fwd(x):
    return _raw_headsum(x), None


def _headsum_bwd(_, g):
    return (_raw_headsum(g),)


_headsum.defvjp(_headsum_fwd, _headsum_bwd)


@functools.partial(jax.custom_vjp, nondiff_argnums=(2,))
def _tdot(tri, x, n):
    return _raw_split_dot(x, tri, "nn", n, False)


def _tdot_fwd(tri, x, n):
    return _raw_split_dot(x, tri, "nn", n, False), tri


def _tdot_bwd(n, tri, g):
    return jnp.zeros_like(tri), _raw_split_dot(g, tri, "tn", n, False)


_tdot.defvjp(_tdot_fwd, _tdot_bwd)


def _row(x, i):
    r = lax.broadcasted_iota(jnp.int32, x.shape, 0)
    return jnp.sum(jnp.where(r == i, x, 0.0), axis=0, keepdims=True)


def _shift_down(x, prev):
    t = x.shape[0]

    @jax.custom_vjp
    def sh(x, prev):
        r = lax.broadcasted_iota(jnp.int32, x.shape, 0)
        return jnp.where(r == 0, prev, pltpu.roll(x, 1, 0))

    def fwd(x, prev):
        return sh(x, prev), None

    def bwd(_, g):
        r = lax.broadcasted_iota(jnp.int32, g.shape, 0)
        dx = jnp.where(r == t - 1, 0.0, pltpu.roll(g, t - 1, 0))
        return dx, jnp.sum(jnp.where(r == 0, g, 0.0), axis=0, keepdims=True)

    sh.defvjp(fwd, bwd)
    return sh(x, prev)


def _sigmoid(x):
    return jax.nn.sigmoid(x)


def _silu(x):
    return x * jax.nn.sigmoid(x)


def _softplus(x):
    return jnp.maximum(x, 0.0) + jnp.log(1.0 + jnp.exp(-jnp.abs(x)))


def _rms(x, g):
    return (x * lax.rsqrt(jnp.mean(x * x, axis=-1, keepdims=True) + EPS)) * g


def _tril(c):
    r = lax.broadcasted_iota(jnp.int32, (c, c), 0)
    cc = lax.broadcasted_iota(jnp.int32, (c, c), 1)
    return cc <= r


def _f_pre1(ps, xs, cs):
    return [_rms(xs[0], ps[0])], []


def _f_pre1_residual(ps, xs, cs):
    return [_rms(xs[0], ps[0]), xs[0]], []


def _f_hgrn(ps, xs, cs):
    lbraw, gn = ps
    hq, hf, hi, hg = xs
    hd = range(HG_PER_STEP)
    st = [cs[0][p * HG_K:(p + 1) * HG_K] for p in hd]
    l0, l1 = _row(lbraw, 0), _row(lbraw, 1)
    m = jnp.maximum(l0, l1)
    e0, e1 = jnp.exp(l0 - m), jnp.exp(l1 - m)
    lb = e0 / (e0 + e1)
    q = _silu(hq) * HG_SCALE
    f = lb + (1.0 - lb) * _sigmoid(hf)
    kh = 1.0 - f
    gl = jnp.log(f)
    c = HG_CHUNK
    low = _tril(c)
    tri = jnp.where(low, 1.0, 0.0).astype(F32)
    outs = []
    for i in range(hq.shape[0] // c):
        rows = slice(i * c, (i + 1) * c)
        b = _tdot(tri, gl[rows], 3)
        bref = _row(b, c // 2 - 1)
        blast = _row(b, c - 1)
        qi = q[rows] * jnp.exp(b - bref)
        ki = kh[rows] * jnp.exp(bref - b)
        qd = q[rows] * jnp.exp(b)
        kd = kh[rows] * jnp.exp(blast - b)
        dec = jnp.exp(blast)
        sl = [slice(p * HG_K, (p + 1) * HG_K) for p in hd]
        sc = [jnp.where(low, _dot(qi[:, sl[p]], ki[:, sl[p]], "nt"), 0.0) for p in hd]
        o = [_dot(sc[p], hi[rows, sl[p]], "nn") + _dot(qd[:, sl[p]], st[p], "nt") for p in hd]
        u = [_dot(hi[rows, sl[p]], kd[:, sl[p]], "tn") for p in hd]
        st = [dec[:, sl[p]] * st[p] + u[p] for p in hd]
        outs.append(jnp.concatenate(o, axis=1) if len(o) > 1 else o[0])
    o = outs[0] if len(outs) == 1 else jnp.concatenate(outs, axis=0)
    on = []
    for p in hd:
        op = o[:, p * HG_K:(p + 1) * HG_K]
        on.append(op * lax.rsqrt(jnp.mean(op * op, axis=-1, keepdims=True) + EPS))
    o = jnp.concatenate(on, axis=1) if len(on) > 1 else on[0]
    o = o * gn
    return [o * _silu(hg)], [jnp.concatenate(st, axis=0) if len(st) > 1 else st[0]]


_RW_OFFS = (0, 1024, 2048, 3072, 3200, 3328)


def _f_rwpre(ps, xs, cs):
    mu, w0, w2p, a0, a2p, g2, k_k, k_a = ps
    (prev,) = cs
    t = xs[0].shape[0]
    zs = []
    for i, z in enumerate(xs):
        lo, hi = _RW_OFFS[i], _RW_OFFS[i + 1]
        zs.append(z + mu[:, lo:hi] * (_shift_down(z, prev[:, lo:hi]) - z))
    rr, kr, vr, wa, gz = zs
    w_log = -_softplus(-(w0 + _dot(jnp.tanh(wa), w2p, "nn"))) - 0.5
    lw = -jnp.exp(w_log)
    a = _sigmoid(a0 + _dot(wa, a2p, "nn"))
    g = _dot(_sigmoid(gz), g2, "nn")
    kkr = kr * k_k
    kk = kkr / jnp.maximum(jnp.sqrt(_headsum(kkr * kkr)), 1e-12)
    k2 = kr * (1.0 + (a - 1.0) * k_a)
    newprev = jnp.concatenate([_row(z, t - 1) for z in xs], axis=1)
    return [rr, lw, k2, vr, -kk, kk * a, g], [newprev]


def _raw_inverses(ls):
    n = ls[0].shape[0]
    r = lax.broadcasted_iota(jnp.int32, (n, n), 0)
    c = lax.broadcasted_iota(jnp.int32, (n, n), 1)
    eye = jnp.where(r == c, 1.0, 0.0).astype(F32)
    tinv = [eye + l for l in ls]
    pw = ls
    for _ in range(5):
        pw = [_raw_dot(p, p, "nn") for p in pw]
        tinv = [t + _raw_dot(t, p, "nn") for t, p in zip(tinv, pw)]
    return tinv


@jax.custom_vjp
def _unit_lower_inverses(ls):
    return _raw_inverses(ls)


def _inverses_fwd(ls):
    tinv = _raw_inverses(ls)
    return tinv, tinv


def _inverses_bwd(tinv, gs):
    return ([_raw_dot(_raw_dot(t, g, "tn"), t, "nt") for t, g in zip(tinv, gs)],)


_unit_lower_inverses.defvjp(_inverses_fwd, _inverses_bwd)


def _f_rwscan(ps, xs, cs):
    state = cs[0]
    ys = []
    for i in range(xs[0].shape[0] // RW_CHUNK):
        y, state = _rwkv_chunk([x[i * RW_CHUNK:(i + 1) * RW_CHUNK] for x in xs], state)
        ys.append(y)
    return [ys[0] if len(ys) == 1 else jnp.concatenate(ys, axis=0)], [state]


def _rwkv_chunk(xs, state):
    npair = RW_PAIRS_PER_STEP
    pr = range(npair)
    r, lw, k, v, av, bv = [[x[:, p * LANES:(p + 1) * LANES] for p in pr] for x in xs]
    sv = [state[p * LANES:(p + 1) * LANES] for p in pr]
    c = RW_CHUNK
    n = 2 * c
    tri = jnp.where(_tril(c), 1.0, 0.0).astype(F32)
    cl = [_tdot(tri, lw[p], 3) for p in pr]
    cl_last = [_row(cl[p], c - 1) for p in pr]
    lane = lax.broadcasted_iota(jnp.int32, (c, LANES), 1)
    h0 = lane < RW_N

    def stack(x):
        return jnp.concatenate([jnp.where(h0, x, 0.0), jnp.where(h0, 0.0, x)], axis=0)

    am = [stack(av[p] * jnp.exp(cl[p] - lw[p])) for p in pr]
    bm = [stack(bv[p] * jnp.exp(-cl[p])) for p in pr]
    km = [stack(k[p] * jnp.exp(-cl[p])) for p in pr]
    rm = [stack(r[p] * jnp.exp(cl[p])) for p in pr]
    vm = [stack(v[p]) for p in pr]
    rn = lax.broadcasted_iota(jnp.int32, (n, n), 0)
    cn = lax.broadcasted_iota(jnp.int32, (n, n), 1)
    blk = (rn >= c) == (cn >= c)
    strict = blk & (cn < rn)
    incl = blk & (cn <= rn)
    lab = [jnp.where(strict, _dot(am[p], bm[p], "nt"), 0.0) for p in pr]
    lak = [jnp.where(strict, _dot(am[p], km[p], "nt"), 0.0) for p in pr]
    wrb = [jnp.where(incl, _dot(rm[p], bm[p], "nt"), 0.0) for p in pr]
    wrk = [jnp.where(incl, _dot(rm[p], km[p], "nt"), 0.0) for p in pr]
    tinv = _unit_lower_inverses(lab)
    rhs = [_dot(am[p], sv[p], "nt") + _dot(lak[p], vm[p], "nn") for p in pr]
    um = [_dot(tinv[p], rhs[p], "nn") for p in pr]
    ym = [_dot(rm[p], sv[p], "nt") + _dot(wrb[p], um[p], "nn") + _dot(wrk[p], vm[p], "nn") for p in pr]
    sn = [(sv[p] + _dot(um[p], bm[p], "tn") + _dot(vm[p], km[p], "tn")) * jnp.exp(cl_last[p]) for p in pr]
    ys = [ym[p][:c] + ym[p][c:] for p in pr]
    return jnp.concatenate(ys, axis=1), jnp.concatenate(sn, axis=0)


def _f_mixers(ps, xs, cs):
    oa, st = _f_hgrn(ps[:2], xs[:4], cs[:1])
    (r, lw, k, v, av, bv, g), prev = _f_rwpre(ps[2:10], xs[4:], cs[1:2])
    y, sv = _f_rwscan([], [r, lw, k, v, av, bv], cs[2:])
    ob, _ = _f_rwpost(ps[10:], y + [r, k, v, g], [])
    return oa + ob, st + prev + sv


def _f_rwpost(ps, xs, cs):
    ln_w, ln_b, r_k = ps
    y, r, k, v, g = xs
    inv_n = 1.0 / RW_N
    yc = y - _headsum(y) * inv_n
    var = _headsum(yc * yc) * inv_n
    yn = yc * lax.rsqrt(var + GN_EPS)
    yn = yn * ln_w + ln_b
    bonus = _headsum(r * k * r_k) * v
    return [(yn + bonus) * g], []


def _f_merge(ps, xs, cs):
    ga, gb, ya, yb = xs
    return [_sigmoid(ga) * ya + _sigmoid(gb) * yb], []


def _f_post1(ps, xs, cs):
    x, mix = xs
    h1 = x + _rms(mix, ps[0])
    return [h1, _rms(h1, ps[1])], []


def _f_conv(ps, xs, cs):
    cw, cb = ps
    p1, p2 = cs
    w0, w1, w2 = _row(cw, 0), _row(cw, 1), _row(cw, 2)
    t = xs[0].shape[0]
    hc = []
    for i, x in enumerate(xs):
        sl = slice(i * DFF, (i + 1) * DFF)
        s1 = _shift_down(x, p1[:, sl])
        s2 = _shift_down(s1, p2[:, sl])
        hc.append(cb[:, sl] + w0[:, sl] * s2 + w1[:, sl] * s1 + w2[:, sl] * x)
    n1 = jnp.concatenate([_row(x, t - 1) for x in xs], axis=1)
    n2 = jnp.concatenate([_row(x, t - 2) for x in xs], axis=1)
    return [_silu(hc[0]) * hc[1]], [n1, n2]


class _Stage:
    def __init__(self, name, f, g, tm, par_per_g, in_pieces, in_offs, carry_shapes, out_pieces, out_dtypes):
        self.name, self.f, self.g, self.tm = name, f, g, tm
        self.par_per_g, self.in_pieces, self.in_offs = par_per_g, in_pieces, in_offs
        self.carry_shapes, self.out_pieces, self.out_dtypes = carry_shapes, out_pieces, out_dtypes


def _par_spec(arr, per_g, g):
    r, c = arr.shape
    if per_g:
        return pl.BlockSpec((r, c // g), lambda gi, ni: (0, gi))
    return pl.BlockSpec((r, c), lambda gi, ni: (0, 0))


def _row_spec(tm, width, off, n, rev):
    if rev:
        return pl.BlockSpec((tm, width), lambda gi, ni: (n - 1 - ni, off + gi))
    return pl.BlockSpec((tm, width), lambda gi, ni: (ni, off + gi))


def _carry_spec(shape, n, rev):
    if rev:
        return pl.BlockSpec((None, None) + shape, lambda gi, ni: (gi, n - 1 - ni, 0, 0))
    return pl.BlockSpec((None, None) + shape, lambda gi, ni: (gi, ni, 0, 0))


def _load_pieces(refs, pieces_list):
    out = []
    for ref, pieces in zip(refs, pieces_list):
        o = 0
        for w in pieces:
            out.append(ref[:, o:o + w].astype(F32))
            o += w
    return out


def _store_pieces(refs, pieces_list, vals):
    k = 0
    for ref, pieces in zip(refs, pieces_list):
        o = 0
        for w in pieces:
            ref[:, o:o + w] = vals[k].astype(ref.dtype)
            k += 1
            o += w


_ANY = pl.BlockSpec(memory_space=pl.ANY)


class _Exchange:
    def __init__(self, kind, arrs):
        self.kind, self.arrs, self.results = kind, list(arrs), None
        if kind == "scatter":
            self.out_shape = [jax.ShapeDtypeStruct((N_DEV - 1,) + a.shape[1:], a.dtype) for a in self.arrs]
        else:
            self.out_shape = [jax.ShapeDtypeStruct((N_DEV,) + a.shape, a.dtype) for a in self.arrs]
        self.nsem = (N_DEV - 1) * len(self.arrs)

    def copies(self, in_refs, out_refs, ssem, rsem):
        x, y, c = lax.axis_index("x"), lax.axis_index("y"), lax.axis_index("c")
        me = 4 * x + 2 * y + c
        cps = []
        for a, (i_ref, o_ref) in enumerate(zip(in_refs, out_refs)):
            for j in range(1, N_DEV):
                px = 1 - x if j & 4 else x
                py = 1 - y if j & 2 else y
                pc = 1 - c if j & 1 else c
                if self.kind == "gather":
                    src, dst = i_ref, o_ref.at[me]
                else:
                    src, dst = i_ref.at[4 * px + 2 * py + pc], o_ref.at[j - 1]
                s = (N_DEV - 1) * a + j - 1
                cps.append(pltpu.make_async_remote_copy(src_ref=src, dst_ref=dst, send_sem=ssem.at[s],
                                                        recv_sem=rsem.at[s], device_id=(px, py, pc),
                                                        device_id_type=MESH))
        return cps

    def run(self, first, mid, last, in_refs, out_refs, ssem, rsem):
        if self.kind == "gather2":
            return self.run_two_level(first, mid, last, in_refs, out_refs, ssem, rsem)

        @pl.when(first)
        def _():
            for cp in self.copies(in_refs, out_refs, ssem, rsem):
                cp.start()

        @pl.when(last)
        def _():
            for cp in self.copies(in_refs, out_refs, ssem, rsem):
                cp.wait()

    def run_two_level(self, first, mid, last, in_refs, out_refs, ssem, rsem):
        x, y, c = lax.axis_index("x"), lax.axis_index("y"), lax.axis_index("c")
        me, sibling = (x, y, c), (x, y, 1 - c)
        chips = [(1 - x, y), (x, 1 - y), (1 - x, 1 - y)]
        arrs = range(len(in_refs))

        def copy(a, k, block, to, src=None):
            dst = out_refs[a].at[4 * block[0] + 2 * block[1] + block[2]]
            return pltpu.make_async_remote_copy(
                src_ref=dst if src is None else src, dst_ref=dst, send_sem=ssem.at[7 * a + k],
                recv_sem=rsem.at[7 * a + k], device_id=to, device_id_type=MESH)

        def firsts(a):
            return [copy(a, 0, me, sibling, src=in_refs[a])] + [
                copy(a, 1 + j, me, (*chip, c), src=in_refs[a]) for j, chip in enumerate(chips)]

        def passed(a):
            return [copy(a, 4 + j, (*chip, c), sibling) for j, chip in enumerate(chips)]

        @pl.when(first)
        def _():
            for a in arrs:
                for cp in firsts(a):
                    cp.start()

        @pl.when(mid)
        def _():
            for j, chip in enumerate(chips):
                for a in arrs:
                    copy(a, 1 + j, (*chip, c), me).wait_recv()
                    passed(a)[j].start()

        @pl.when(last)
        def _():
            for a in arrs:
                copy(a, 0, sibling, me).wait_recv()
                for j, chip in enumerate(chips):
                    copy(a, 4 + j, (*chip, 1 - c), me).wait_recv()
                for cp in firsts(a) + passed(a):
                    cp.wait_send()


def _hook_specs(hook):
    if hook is None:
        return [], [], [], []
    na = len(hook.arrs)
    sems = [pltpu.SemaphoreType.DMA((hook.nsem,)), pltpu.SemaphoreType.DMA((hook.nsem,))]
    return [_ANY] * na, [_ANY] * na, hook.out_shape, sems


def _stage_fwd(st, t, params, inputs, hook=None):
    g, tm = st.g, min(st.tm, t)
    n = t // tm
    npar, nin, ncar, nout = len(params), len(inputs), len(st.carry_shapes), len(st.out_pieces)
    h_in, h_out, h_shape, h_sems = _hook_specs(hook)
    nh = len(h_in)

    def body(*refs):
        p_refs = refs[:npar]
        x_refs = refs[npar:npar + nin]
        hi_refs = refs[npar + nin:npar + nin + nh]
        o = npar + nin + nh
        o_refs = refs[o:o + nout]
        s_refs = refs[o + nout:o + nout + ncar]
        ho_refs = refs[o + nout + ncar:o + nout + ncar + nh]
        c_scr = refs[o + nout + ncar + nh:o + nout + ncar + nh + ncar]
        gi, ni = pl.program_id(0), pl.program_id(1)
        if hook is not None:
            step = gi * n + ni
            hook.run(step == 0, step == (4 * g * n) // 5, step == g * n - 1, hi_refs, ho_refs, *refs[-2:])

        @pl.when(ni == 0)
        def _():
            for c in c_scr:
                c[...] = jnp.zeros(c.shape, F32)

        ps = [r[...].astype(F32) for r in p_refs]
        xs = _load_pieces(x_refs, st.in_pieces)
        cs = [c[...] for c in c_scr]
        for s, c in zip(s_refs, cs):
            s[...] = c
        outs, ncs = st.f(ps, xs, cs)
        _store_pieces(o_refs, st.out_pieces, outs)
        for c, v in zip(c_scr, ncs):
            c[...] = v

    in_specs = [_par_spec(p, pg, g) for p, pg in zip(params, st.par_per_g)]
    in_specs += [_row_spec(tm, sum(pc), off, n, False) for pc, off in zip(st.in_pieces, st.in_offs)]
    out_specs = [_row_spec(tm, sum(pc), 0, n, False) for pc in st.out_pieces]
    out_specs += [_carry_spec(s, n, False) for s in st.carry_shapes]
    out_shape = [jax.ShapeDtypeStruct((t, g * sum(pc)), dt) for pc, dt in zip(st.out_pieces, st.out_dtypes)]
    out_shape += [jax.ShapeDtypeStruct((g, n) + s, F32) for s in st.carry_shapes]
    res = pl.pallas_call(
        body, name=st.name + "_fwd", grid=(g, n), in_specs=in_specs + h_in, out_specs=out_specs + h_out,
        out_shape=out_shape + h_shape,
        scratch_shapes=[pltpu.VMEM(s, F32) for s in st.carry_shapes] + h_sems,
        compiler_params=_cparams(("arbitrary", "arbitrary")),
    )(*params, *inputs, *(hook.arrs if hook else []))
    if hook is not None:
        hook.results = list(res[nout + ncar:])
    return list(res[:nout]), list(res[nout:nout + ncar])


def _stage_bwd(st, t, params, inputs, saved, douts, dx_dtypes, hook=None):
    g, tm = st.g, min(st.tm, t)
    n = t // tm
    npar, nin, ncar = len(params), len(inputs), len(st.carry_shapes)
    flat_d = [d for ds in douts for d in ds]
    nd = len(flat_d)
    dx_idx = [i for i, dt in enumerate(dx_dtypes) if dt is not None]
    h_in, h_out, h_shape, h_sems = _hook_specs(hook)
    nh = len(h_in)

    def body(*refs):
        p_refs = refs[:npar]
        x_refs = refs[npar:npar + nin]
        s_refs = refs[npar + nin:npar + nin + ncar]
        d_refs = refs[npar + nin + ncar:npar + nin + ncar + nd]
        hi_refs = refs[npar + nin + ncar + nd:npar + nin + ncar + nd + nh]
        o = npar + nin + ncar + nd + nh
        dp_refs = refs[o:o + npar]
        dx_refs = refs[o + npar:o + npar + len(dx_idx)]
        ho_refs = refs[o + npar + len(dx_idx):o + npar + len(dx_idx) + nh]
        dc_scr = refs[o + npar + len(dx_idx) + nh:o + npar + len(dx_idx) + nh + ncar]
        gi, ni = pl.program_id(0), pl.program_id(1)
        if hook is not None:
            step = gi * n + ni
            hook.run(step == 0, step == (4 * g * n) // 5, step == g * n - 1, hi_refs, ho_refs, *refs[-2:])

        @pl.when(ni == 0)
        def _():
            for c in dc_scr:
                c[...] = jnp.zeros(c.shape, F32)

        ps = [r[...].astype(F32) for r in p_refs]
        xs = _load_pieces(x_refs, st.in_pieces)
        cs = [s[...] for s in s_refs]
        dys = []
        k = 0
        for ds, pieces in zip(douts, st.out_pieces):
            acc = _load_pieces([d_refs[k]], [pieces])
            for j in range(1, len(ds)):
                more = _load_pieces([d_refs[k + j]], [pieces])
                acc = [a + b for a, b in zip(acc, more)]
            dys += acc
            k += len(ds)
        _, vjp = jax.vjp(st.f, ps, xs, cs)
        dps, dxs, dcs = vjp((dys, [c[...] for c in dc_scr]))
        k = 0
        per_in = []
        for pieces in st.in_pieces:
            per_in.append(dxs[k:k + len(pieces)])
            k += len(pieces)
        for ref, i in zip(dx_refs, dx_idx):
            _store_pieces([ref], [st.in_pieces[i]], per_in[i])
        for c, v in zip(dc_scr, dcs):
            c[...] = v
        for ref, dp, pg in zip(dp_refs, dps, st.par_per_g):
            first = (ni == 0) if pg else ((ni == 0) & (gi == 0))

            @pl.when(first)
            def _():
                ref[...] = jnp.zeros(ref.shape, F32)

            ref[...] += dp

    in_specs = [_par_spec(p, pg, g) for p, pg in zip(params, st.par_per_g)]
    in_specs += [_row_spec(tm, sum(pc), off, n, True) for pc, off in zip(st.in_pieces, st.in_offs)]
    in_specs += [_carry_spec(s, n, True) for s in st.carry_shapes]
    for ds, pc in zip(douts, st.out_pieces):
        in_specs += [_row_spec(tm, sum(pc), 0, n, True) for _ in ds]
    out_specs = [_par_spec(p, pg, g) for p, pg in zip(params, st.par_per_g)]
    out_specs += [_row_spec(tm, sum(st.in_pieces[i]), 0, n, True) for i in dx_idx]
    out_shape = [jax.ShapeDtypeStruct(p.shape, F32) for p in params]
    out_shape += [jax.ShapeDtypeStruct((t, g * sum(st.in_pieces[i])), dx_dtypes[i]) for i in dx_idx]
    res = pl.pallas_call(
        body, name=st.name + "_bwd", grid=(g, n), in_specs=in_specs + h_in, out_specs=out_specs + h_out,
        out_shape=out_shape + h_shape,
        scratch_shapes=[pltpu.VMEM(s, F32) for s in st.carry_shapes] + h_sems,
        compiler_params=_cparams(("arbitrary", "arbitrary")),
    )(*params, *inputs, *saved, *flat_d, *(hook.arrs if hook else []))
    if hook is not None:
        hook.results = list(res[npar + len(dx_idx):])
    return list(res[:npar]), list(res[npar:npar + len(dx_idx)])


def _pick(n, cap):
    if n <= cap:
        return n
    best = LANES
    for k in range(1, n // LANES + 1):
        if (n // LANES) % k == 0 and k * LANES <= cap:
            best = k * LANES
    return best


def _mm(name, a, b, mode, out_dtype=F32, tm=1024, tn=512, b_outer=False, token=None):
    m = a.shape[1] if mode == "tn" else a.shape[0]
    k = a.shape[0] if mode == "tn" else a.shape[1]
    n = b.shape[0] if mode == "nt" else b.shape[1]
    tm, tn = _pick(m, tm), _pick(n, tn)
    if b_outer:
        grid = (n // tn, m // tm)
        ij = lambda p, q: (q, p)
    else:
        grid = (m // tm, n // tn)
        ij = lambda p, q: (p, q)
    extra = [] if token is None else [token]

    def body(*refs):
        a_ref, b_ref, o_ref = refs[0], refs[1], refs[-1]
        o_ref[...] = _raw_dot(a_ref[...], b_ref[...], mode).astype(o_ref.dtype)

    if mode == "tn":
        a_spec = pl.BlockSpec((k, tm), lambda p, q: (0, ij(p, q)[0]))
    else:
        a_spec = pl.BlockSpec((tm, k), lambda p, q: (ij(p, q)[0], 0))
    if mode == "nt":
        b_spec = pl.BlockSpec((tn, k), lambda p, q: (ij(p, q)[1], 0))
    else:
        b_spec = pl.BlockSpec((k, tn), lambda p, q: (0, ij(p, q)[1]))
    return pl.pallas_call(
        body, name=name, grid=grid,
        in_specs=[a_spec, b_spec] + [pl.BlockSpec(e.shape, lambda p, q: (0, 0)) for e in extra],
        out_specs=pl.BlockSpec((tm, tn), lambda p, q: ij(p, q)),
        out_shape=jax.ShapeDtypeStruct((m, n), out_dtype),
        compiler_params=_cparams(("arbitrary", "arbitrary")),
    )(a, b, *extra)


def _loss_stage(t, g_post, h1, ff, tgt):
    tm = min(256, t)
    n = t // tm

    def body(g_ref, h_ref, f_ref, t_ref, loss_ref, dg_ref, dh_ref, df_ref):
        ni = pl.program_id(0)
        target = t_ref[...]

        def lossf(g, h1, ff):
            e = h1 + _rms(ff, g) - target
            return 0.5 * jnp.sum(jnp.mean(e * e, axis=-1))

        l, (dg, dh, df) = jax.value_and_grad(lossf, argnums=(0, 1, 2))(g_ref[...], h_ref[...], f_ref[...])

        @pl.when(ni == 0)
        def _():
            loss_ref[...] = jnp.zeros(loss_ref.shape, F32)
            dg_ref[...] = jnp.zeros(dg_ref.shape, F32)

        loss_ref[...] += jnp.full(loss_ref.shape, l, F32)
        dg_ref[...] += dg
        dh_ref[...] = dh
        df_ref[...] = df.astype(df_ref.dtype)

    row = pl.BlockSpec((tm, D), lambda ni: (ni, 0))
    one = pl.BlockSpec((1, D), lambda ni: (0, 0))
    return pl.pallas_call(
        body, name="loss_head", grid=(n,), in_specs=[one, row, row, row],
        out_specs=[pl.BlockSpec((1, LANES), lambda ni: (0, 0)), one, row, row],
        out_shape=[jax.ShapeDtypeStruct((1, LANES), F32), jax.ShapeDtypeStruct((1, D), F32),
                   jax.ShapeDtypeStruct((t, D), F32), jax.ShapeDtypeStruct((t, D), BF)],
        compiler_params=_cparams(("arbitrary",)),
    )(g_post, h1, ff, tgt)


_ANY = pl.BlockSpec(memory_space=pl.ANY)


def _all_gather(name, blks):
    na = len(blks)
    ns = 8

    def body(*refs):
        x_refs, out_refs = refs[:na], refs[na:2 * na]
        send_sems, recv_sems, local_sems = refs[2 * na:]
        x, y, cc = lax.axis_index("x"), lax.axis_index("y"), lax.axis_index("c")
        sibling, xn, yn = (x, y, 1 - cc), (1 - x, y, cc), (x, 1 - y, cc)

        def num(px, py, pc):
            return 4 * px + 2 * py + pc

        def copy(a, k, to, src, dst):
            return pltpu.make_async_remote_copy(src_ref=src, dst_ref=dst, send_sem=send_sems.at[ns * a + k],
                                                recv_sem=recv_sems.at[ns * a + k], device_id=to, device_id_type=MESH)

        def halves(a, blk):
            h = blks[a].shape[0] // 2
            return out_refs[a].at[blk, pl.ds(0, h)], out_refs[a].at[blk, pl.ds(h, h)]

        mine, sends = [], []
        for a in range(na):
            o = out_refs[a]
            m = pltpu.make_async_copy(x_refs[a], o.at[num(x, y, cc)], local_sems.at[a])
            m.start()
            mine.append(m)
            own = o.at[num(x, y, cc)]
            sends.append([copy(a, 0, sibling, x_refs[a], own), copy(a, 1, xn, x_refs[a], own),
                          copy(a, 2, yn, x_refs[a], own)])
            for cp in sends[a]:
                cp.start()
        for a in range(na):
            o = out_refs[a]
            bx, by, bd = num(1 - x, y, cc), num(x, 1 - y, cc), num(1 - x, 1 - y, cc)
            copy(a, 1, xn, o.at[bx], o.at[bx]).wait_recv()
            more = [copy(a, 3, yn, halves(a, bx)[0], halves(a, bx)[0]), copy(a, 5, sibling, o.at[bx], o.at[bx])]
            for cp in more:
                cp.start()
            sends[a] += more
        for a in range(na):
            o = out_refs[a]
            bx, by, bd = num(1 - x, y, cc), num(x, 1 - y, cc), num(1 - x, 1 - y, cc)
            copy(a, 2, yn, o.at[by], o.at[by]).wait_recv()
            more = [copy(a, 4, xn, halves(a, by)[1], halves(a, by)[1]), copy(a, 6, sibling, o.at[by], o.at[by])]
            for cp in more:
                cp.start()
            sends[a] += more
        for a in range(na):
            o = out_refs[a]
            bd = num(1 - x, 1 - y, cc)
            copy(a, 3, yn, halves(a, bd)[0], halves(a, bd)[0]).wait_recv()
            copy(a, 4, xn, halves(a, bd)[1], halves(a, bd)[1]).wait_recv()
            fw = copy(a, 7, sibling, o.at[bd], o.at[bd])
            fw.start()
            sends[a].append(fw)
        for a in range(na):
            o = out_refs[a]
            for k, blk in ((0, num(x, y, 1 - cc)), (5, num(1 - x, y, 1 - cc)), (6, num(x, 1 - y, 1 - cc)),
                           (7, num(1 - x, 1 - y, 1 - cc))):
                copy(a, k, sibling, o.at[blk], o.at[blk]).wait_recv()
            for cp in sends[a]:
                cp.wait_send()
        for m in mine:
            m.wait()

    res = pl.pallas_call(
        body, name=name, in_specs=[_ANY] * na, out_specs=[_ANY] * na,
        out_shape=[jax.ShapeDtypeStruct((N_DEV,) + b.shape, b.dtype) for b in blks],
        scratch_shapes=[pltpu.SemaphoreType.DMA((ns * na,)), pltpu.SemaphoreType.DMA((ns * na,)),
                        pltpu.SemaphoreType.DMA((na,))],
    )(*blks)
    return list(res)


def _reduce_pair(g8s):
    na = len(g8s)

    def body(*refs):
        g_refs, recv_refs = refs[:na], refs[na:2 * na]
        ssem, rsem = refs[2 * na:]
        x, y, cc = lax.axis_index("x"), lax.axis_index("y"), lax.axis_index("c")
        chips = [(x, y), (1 - x, y), (x, 1 - y), (1 - x, 1 - y)]
        sib = (x, y, 1 - cc)
        for a in range(na):
            for k, (cx, cy) in enumerate(chips):
                pltpu.make_async_remote_copy(
                    src_ref=g_refs[a].at[4 * cx + 2 * cy + 1 - cc], dst_ref=recv_refs[a].at[k],
                    send_sem=ssem.at[a], recv_sem=rsem.at[a], device_id=sib, device_id_type=MESH).start()
        for a in range(na):
            pltpu.make_async_remote_copy(src_ref=recv_refs[a], dst_ref=recv_refs[a], send_sem=ssem.at[a],
                                         recv_sem=rsem.at[a], device_id=sib, device_id_type=MESH).wait()

    res = pl.pallas_call(
        body, name="reduce_pair", in_specs=[_ANY] * na, out_specs=[_ANY] * na,
        out_shape=[jax.ShapeDtypeStruct((4,) + g.shape[1:], g.dtype) for g in g8s],
        scratch_shapes=[pltpu.SemaphoreType.DMA((na,)), pltpu.SemaphoreType.DMA((na,))],
    )(*g8s)
    return list(res)


_HBM = pl.BlockSpec(memory_space=pltpu.HBM)
_SEM = pl.BlockSpec(memory_space=pltpu.SEMAPHORE)
_EFFECT = pltpu.SideEffectType.DATAFLOW_SIDE_EFFECTING


def _chip_swap_copies(s_refs, land_refs, ssem, rsem):
    x, y, c = lax.axis_index("x"), lax.axis_index("y"), lax.axis_index("c")
    targets = [(1 - x, y, c), (x, 1 - y, c), (1 - x, 1 - y, c)]
    return [pltpu.make_async_remote_copy(src_ref=s.at[k], dst_ref=d.at[k], send_sem=ssem.at[3 * a + k],
                                         recv_sem=rsem.at[3 * a + k], device_id=targets[k], device_id_type=MESH)
            for a, (s, d) in enumerate(zip(s_refs, land_refs)) for k in range(3)]


def _chip_swap_start(sends):
    na = len(sends)

    def body(*refs):
        cps = _chip_swap_copies(refs[:na], refs[na:2 * na], refs[2 * na], refs[2 * na + 1])
        for cp in cps:
            cp.start()
        token = refs[-1]
        token[...] = jnp.zeros(token.shape, token.dtype)

    bufs = [pltpu.HBM(s.shape, s.dtype) for s in sends]
    res = pl.pallas_call(
        body, name="chip_swap_start",
        out_shape=[pltpu.SemaphoreType.DMA((3 * na,)), pltpu.SemaphoreType.DMA((3 * na,))] + bufs + bufs
        + [jax.ShapeDtypeStruct((8, LANES), F32)],
        in_specs=[_HBM] * (2 * na), out_specs=[_SEM, _SEM] + [_HBM] * (2 * na) + [pl.BlockSpec(memory_space=pltpu.VMEM)],
        input_output_aliases={i: 2 + i for i in range(2 * na)},
        compiler_params=pltpu.CompilerParams(has_side_effects=_EFFECT),
    )(*[pltpu.with_memory_space_constraint(s, pltpu.HBM) for s in sends],
      *[pltpu.with_memory_space_constraint(lax.empty(s.shape, s.dtype), pltpu.HBM) for s in sends])
    return res[0], res[1], list(res[2:2 + na]), list(res[2 + na:2 + 2 * na]), res[-1]


def _chip_swap_wait(ssem, rsem, srcs, lands, after):
    na = len(srcs)

    def body(*refs):
        cps = _chip_swap_copies(refs[:na], refs[na:2 * na], refs[2 * na], refs[2 * na + 1])
        for cp in cps:
            cp.wait_send()
            cp.wait_recv()

    bufs = [pltpu.HBM(s.shape, s.dtype) for s in srcs]
    res = pl.pallas_call(
        body, name="chip_swap_wait", out_shape=bufs + bufs,
        in_specs=[_HBM] * (2 * na) + [_SEM, _SEM, _ANY], out_specs=[_HBM] * (2 * na),
        input_output_aliases={i: i for i in range(2 * na)},
        compiler_params=pltpu.CompilerParams(has_side_effects=_EFFECT),
    )(*srcs, *lands, ssem, rsem, after)
    return list(res[na:])


def _pick_rows(r, c, budget=TILE_BYTES):
    if r * c * 4 <= budget or r % 16:
        return r
    best = 16
    for tr in range(16, r, 16):
        if r % tr == 0 and tr * c * 4 <= budget:
            best = tr
    return best


def _pair_sum(name, idx4, g8, recv4):
    _, r, c = g8.shape
    tr = _pick_rows(r, c, 2 * TILE_BYTES)

    def body(idx_ref, a_ref, b_ref, o0_ref, o3_ref):
        k = pl.program_id(1)
        s = a_ref[...].astype(F32) + b_ref[...].astype(F32)

        @pl.when(k == 0)
        def _():
            o0_ref[...] = s

        @pl.when(k > 0)
        def _():
            o3_ref[...] = s.astype(BF)

    spec = pltpu.PrefetchScalarGridSpec(
        num_scalar_prefetch=1, grid=(r // tr, 4),
        in_specs=[pl.BlockSpec((None, tr, c), lambda i, k, idx: (idx[k], i, 0)),
                  pl.BlockSpec((None, tr, c), lambda i, k, idx: (k, i, 0))],
        out_specs=[pl.BlockSpec((tr, c), lambda i, k, idx: (i, 0)),
                   pl.BlockSpec((None, tr, c), lambda i, k, idx: (jnp.maximum(k - 1, 0), i, 0))])
    return pl.pallas_call(
        body, name=name, grid_spec=spec,
        out_shape=[jax.ShapeDtypeStruct((r, c), F32), jax.ShapeDtypeStruct((3, r, c), BF)],
        compiler_params=_cparams(("arbitrary", "arbitrary")),
    )(idx4, g8, recv4)


def _adamw(w, g, m, v):
    m = ADAM_B1 * m + (1.0 - ADAM_B1) * g
    v = ADAM_B2 * v + (1.0 - ADAM_B2) * jnp.square(g)
    m_hat = m / (1.0 - ADAM_B1 ** ADAM_STEP)
    v_hat = v / (1.0 - ADAM_B2 ** ADAM_STEP)
    delta = -ADAM_LR * (m_hat / (jnp.sqrt(v_hat) + ADAM_EPS) + ADAM_WD * w)
    return delta, m, v


def _sum_partials(name, idx1, own, recv):
    _, r, c = own.shape
    tr = _pick_rows(r, c, 2 * TILE_BYTES)
    nj = recv.shape[0]

    def body(idx_ref, p_ref, r_ref, g_out):
        g = p_ref[...].astype(F32)
        for k in range(nj):
            g = g + r_ref[k].astype(F32)
        g_out[...] = g

    row = pl.BlockSpec((tr, c), lambda i, idx: (i, 0))
    spec = pltpu.PrefetchScalarGridSpec(
        num_scalar_prefetch=1, grid=(r // tr,),
        in_specs=[pl.BlockSpec((None, tr, c), lambda i, idx: (idx[0], i, 0)),
                  pl.BlockSpec((nj, tr, c), lambda i, idx: (0, i, 0))],
        out_specs=row)
    return pl.pallas_call(body, name=name, grid_spec=spec, out_shape=jax.ShapeDtypeStruct((r, c), F32),
                          compiler_params=_cparams(("arbitrary",)))(idx1, own, recv)


def _adam_sharded(name, idx1, own, recv, w, m, v):
    r, c = w.shape
    tr = _pick_rows(r, c)
    nj = 0 if recv is None else recv.shape[0]
    if recv is None:
        recv = jnp.zeros((1, 8, LANES), BF)

    def body(idx_ref, p_ref, r_ref, w_ref, m_ref, v_ref, g_out, d_out, m_out, v_out):
        g = p_ref[...].astype(F32)
        for k in range(nj):
            g = g + r_ref[k].astype(F32)
        d, mn, vn = _adamw(w_ref[...], g, m_ref[...], v_ref[...])
        g_out[...] = g
        d_out[...] = d
        m_out[...] = mn
        v_out[...] = vn

    row = pl.BlockSpec((tr, c), lambda i, idx: (i, 0))
    if nj:
        recv_spec = pl.BlockSpec((nj, tr, c), lambda i, idx: (0, i, 0))
    else:
        recv_spec = pl.BlockSpec(recv.shape, lambda i, idx: (0, 0, 0))
    spec = pltpu.PrefetchScalarGridSpec(
        num_scalar_prefetch=1, grid=(r // tr,),
        in_specs=[pl.BlockSpec((None, tr, c), lambda i, idx: (idx[0], i, 0)), recv_spec, row, row, row],
        out_specs=[row] * 4)
    return pl.pallas_call(
        body, name=name, grid_spec=spec, out_shape=[jax.ShapeDtypeStruct((r, c), F32)] * 4,
        compiler_params=_cparams(("arbitrary",)),
    )(idx1, own, recv, w, m, v)


def _repl_rows():
    rows, r = {}, 0
    for name, cols in REPL:
        rows[name] = r
        r += REPL_ROWS.get(name, 1) * ((cols + D - 1) // D)
    return rows


def _pack_replicated(grads):
    rows = _repl_rows()
    names = [n for n, _ in REPL]

    def body(*refs):
        o_ref = refs[-1]
        o_ref[...] = jnp.zeros(o_ref.shape, F32)
        for name, ref in zip(names, refs[:-1]):
            r0 = rows[name]
            nr, nc = ref.shape
            if nc <= D:
                o_ref[r0:r0 + nr, 0:nc] = ref[...]
            else:
                for j in range((nc + D - 1) // D):
                    lo, hi = j * D, min(nc, (j + 1) * D)
                    o_ref[r0 + j:r0 + j + 1, 0:hi - lo] = ref[:, lo:hi]

    return pl.pallas_call(body, name="pack_replicated", out_shape=jax.ShapeDtypeStruct((REPL_TOTAL, D), F32),
                          compiler_params=_cparams())(*[grads[n] for n in names])


def _adam_replicated(g8, ws, ms, vs):
    rows = _repl_rows()
    names = [n for n, _ in REPL]
    np_ = len(names)

    def body(*refs):
        g_ref = refs[0]
        w_refs, m_refs, v_refs = refs[1:1 + np_], refs[1 + np_:1 + 2 * np_], refs[1 + 2 * np_:1 + 3 * np_]
        outs = refs[1 + 3 * np_:1 + 7 * np_]
        scr = refs[-1]
        g = g_ref[0]
        for k in range(1, N_DEV):
            g = g + g_ref[k]
        scr[...] = g
        for i, name in enumerate(names):
            r0 = rows[name]
            nr, nc = w_refs[i].shape
            if nc <= D:
                gi = scr[r0:r0 + nr, 0:nc]
            else:
                parts = []
                for j in range((nc + D - 1) // D):
                    lo, hi = j * D, min(nc, (j + 1) * D)
                    parts.append(scr[r0 + j:r0 + j + 1, 0:hi - lo])
                gi = jnp.concatenate(parts, axis=1)
            d, mn, vn = _adamw(w_refs[i][...], gi, m_refs[i][...], v_refs[i][...])
            outs[i][...] = gi
            outs[np_ + i][...] = d
            outs[2 * np_ + i][...] = mn
            outs[3 * np_ + i][...] = vn

    shp = [jax.ShapeDtypeStruct(w.shape, F32) for w in ws]
    res = pl.pallas_call(body, name="adam_replicated", out_shape=shp * 4,
                         scratch_shapes=[pltpu.VMEM((REPL_TOTAL, D), F32)], compiler_params=_cparams(),
                         )(g8, *ws, *ms, *vs)
    return [dict(zip(names, res[k * np_:(k + 1) * np_])) for k in range(4)]


_WEIGHTS = ("attn_pre_norm", "w_in", "hgrn_lb", "hgrn_gnorm", "w_branch_a", "rwkv_mu", "rwkv_w0", "rwkv_w2",
            "rwkv_a0", "rwkv_a2", "rwkv_g2", "rwkv_k_k", "rwkv_k_a", "rwkv_r_k", "rwkv_ln_w", "rwkv_ln_b",
            "w_branch_b", "w_out", "attn_post_norm", "ffn_pre_norm", "w_up", "conv_w", "conv_b", "w_down",
            "ffn_post_norm")
_BIG = ("w_in", "w_up", "w_down", "w_branch_a", "w_branch_b", "w_out")


def _stages():
    one = [D]
    hw = HG_K * HG_PER_STEP
    rw = LANES * RW_PAIRS_PER_STEP
    return dict(
        pre1=_Stage("pre1", _f_pre1, 1, 256, [False], [one], [0], [], [one], [BF]),
        pre1_res=_Stage("pre1", _f_pre1_residual, 1, 256, [False], [one], [0], [], [one, one], [BF, F32]),
        mixers=_Stage("mixers", _f_mixers, 1, 2 * RW_CHUNK, [False] * 13, [[D] * 7 + [LANES, LANES]], [0],
                      [(hw, HG_K), (1, RW_COLS), (rw, LANES)], [one, one], [BF, BF]),
        merge=_Stage("merge", _f_merge, 4, 512, [], [[256]] * 4, [29, 33, 0, 0], [], [[256]], [BF]),
        post1=_Stage("post1", _f_post1, 1, 256, [False, False], [one, one], [0, 0], [], [one, one], [F32, BF]),
        conv=_Stage("conv", _f_conv, 1, 128, [False, False], [[DFF, DFF]], [0], [(1, 2 * DFF), (1, 2 * DFF)],
                    [[DFF]], [BF]),
    )


def _cols_to_blocks(w, per):
    return w.reshape(w.shape[0], N_DEV, per).transpose(1, 0, 2)


def _blocks_to_cols(g):
    return g.transpose(1, 0, 2).reshape(g.shape[1], N_DEV * g.shape[2])


def kernel(x, attn_pre_norm, w_in, hgrn_lb, hgrn_gnorm, w_branch_a, rwkv_mu, rwkv_w0, rwkv_w2, rwkv_a0, rwkv_a2, rwkv_g2, rwkv_k_k, rwkv_k_a, rwkv_r_k, rwkv_ln_w, rwkv_ln_b, w_branch_b, w_out, attn_post_norm, ffn_pre_norm, w_up, conv_w, conv_b, w_down, ffn_post_norm, loss_target, m_attn_pre_norm, m_w_in, m_hgrn_lb, m_hgrn_gnorm, m_w_branch_a, m_rwkv_mu, m_rwkv_w0, m_rwkv_w2, m_rwkv_a0, m_rwkv_a2, m_rwkv_g2, m_rwkv_k_k, m_rwkv_k_a, m_rwkv_r_k, m_rwkv_ln_w, m_rwkv_ln_b, m_w_branch_b, m_w_out, m_attn_post_norm, m_ffn_pre_norm, m_w_up, m_conv_w, m_conv_b, m_w_down, m_ffn_post_norm, v_attn_pre_norm, v_w_in, v_hgrn_lb, v_hgrn_gnorm, v_w_branch_a, v_rwkv_mu, v_rwkv_w0, v_rwkv_w2, v_rwkv_a0, v_rwkv_a2, v_rwkv_g2, v_rwkv_k_k, v_rwkv_k_a, v_rwkv_r_k, v_rwkv_ln_w, v_rwkv_ln_b, v_w_branch_b, v_w_out, v_attn_post_norm, v_ffn_pre_norm, v_w_up, v_conv_w, v_conv_b, v_w_down, v_ffn_post_norm):
    w = dict(attn_pre_norm=attn_pre_norm, w_in=w_in, hgrn_lb=hgrn_lb, hgrn_gnorm=hgrn_gnorm, w_branch_a=w_branch_a, rwkv_mu=rwkv_mu, rwkv_w0=rwkv_w0, rwkv_w2=rwkv_w2, rwkv_a0=rwkv_a0, rwkv_a2=rwkv_a2, rwkv_g2=rwkv_g2, rwkv_k_k=rwkv_k_k, rwkv_k_a=rwkv_k_a, rwkv_r_k=rwkv_r_k, rwkv_ln_w=rwkv_ln_w, rwkv_ln_b=rwkv_ln_b, w_branch_b=w_branch_b, w_out=w_out, attn_post_norm=attn_post_norm, ffn_pre_norm=ffn_pre_norm, w_up=w_up, conv_w=conv_w, conv_b=conv_b, w_down=w_down, ffn_post_norm=ffn_post_norm)
    mo = dict(attn_pre_norm=m_attn_pre_norm, w_in=m_w_in, hgrn_lb=m_hgrn_lb, hgrn_gnorm=m_hgrn_gnorm, w_branch_a=m_w_branch_a, rwkv_mu=m_rwkv_mu, rwkv_w0=m_rwkv_w0, rwkv_w2=m_rwkv_w2, rwkv_a0=m_rwkv_a0, rwkv_a2=m_rwkv_a2, rwkv_g2=m_rwkv_g2, rwkv_k_k=m_rwkv_k_k, rwkv_k_a=m_rwkv_k_a, rwkv_r_k=m_rwkv_r_k, rwkv_ln_w=m_rwkv_ln_w, rwkv_ln_b=m_rwkv_ln_b, w_branch_b=m_w_branch_b, w_out=m_w_out, attn_post_norm=m_attn_post_norm, ffn_pre_norm=m_ffn_pre_norm, w_up=m_w_up, conv_w=m_conv_w, conv_b=m_conv_b, w_down=m_w_down, ffn_post_norm=m_ffn_post_norm)
    vo = dict(attn_pre_norm=v_attn_pre_norm, w_in=v_w_in, hgrn_lb=v_hgrn_lb, hgrn_gnorm=v_hgrn_gnorm, w_branch_a=v_w_branch_a, rwkv_mu=v_rwkv_mu, rwkv_w0=v_rwkv_w0, rwkv_w2=v_rwkv_w2, rwkv_a0=v_rwkv_a0, rwkv_a2=v_rwkv_a2, rwkv_g2=v_rwkv_g2, rwkv_k_k=v_rwkv_k_k, rwkv_k_a=v_rwkv_k_a, rwkv_r_k=v_rwkv_r_k, rwkv_ln_w=v_rwkv_ln_w, rwkv_ln_b=v_rwkv_ln_b, w_branch_b=v_w_branch_b, w_out=v_w_out, attn_post_norm=v_attn_post_norm, ffn_pre_norm=v_ffn_pre_norm, w_up=v_w_up, conv_w=v_conv_w, conv_b=v_conv_b, w_down=v_w_down, ffn_post_norm=v_ffn_post_norm)

    t = x.shape[1]
    x2 = x.reshape(t, D)
    tgt = loss_target.reshape(t, D)
    st = _stages()

    me = 4 * lax.axis_index("x") + 2 * lax.axis_index("y") + lax.axis_index("c")
    small = jnp.concatenate([rwkv_w2[0], rwkv_a2[0], rwkv_g2[0]], axis=0).astype(BF)
    g_in, g_small = _all_gather("gather_weights", [w_in[0].T.astype(BF), small])
    fw_in_t = g_in.reshape(IN_COLS, D)
    z64 = jnp.zeros((64, D), BF)
    w2p = jnp.concatenate([_blocks_to_cols(g_small[:, 0:64]), z64], axis=0)
    a2p = jnp.concatenate([z64, _blocks_to_cols(g_small[:, 64:128])], axis=0)
    g2f = _blocks_to_cols(g_small[:, 128:256])
    conv_bits = lax.bitcast_convert_type(conv_w[0], BF).reshape(3, 2 * 704)
    late = [w_up[0].T.astype(BF)] + [w[k][0].astype(BF) for k in _BIG[2:]] + [conv_bits]
    late_gather = _Exchange("gather2", late)
    r_k = rwkv_r_k.reshape(1, D)

    (xn,), _ = _stage_fwd(st["pre1"], t, [attn_pre_norm], [x2])
    z = _mm("in_proj", xn, fw_in_t, "nt", F32, tm=512, tn=4736, b_outer=True)
    mix_par = [hgrn_lb, hgrn_gnorm, rwkv_mu, rwkv_w0, w2p, rwkv_a0, a2p, g2f, rwkv_k_k, rwkv_k_a,
               rwkv_ln_w, rwkv_ln_b, r_k]
    mix_in = [z]
    (o_a, o_b), mix_saved = _stage_fwd(st["mixers"], t, mix_par, mix_in, hook=late_gather)
    gl = [lax.dynamic_update_slice(g, own[None], (me, 0, 0)) for g, own in zip(late_gather.results, late)]
    fw_up_t = gl[0].reshape(2 * DFF, D)
    fw_down = gl[1].reshape(DFF, D)
    fw_a, fw_b, fw_out = (g.reshape(D, D) for g in gl[2:5])
    conv_full = _blocks_to_cols(lax.bitcast_convert_type(gl[5].reshape(N_DEV, 3, 704, 2), F32))
    y_a = _mm("branch_a", o_a, fw_a, "nn")
    y_b = _mm("branch_b", o_b, fw_b, "nn")
    (merged,), _ = _stage_fwd(st["merge"], t, [], [z, z, y_a, y_b])
    mix = _mm("out_proj", merged, fw_out, "nn")
    (h1, xn2), _ = _stage_fwd(st["post1"], t, [attn_post_norm, ffn_pre_norm], [x2, mix])
    hu = _mm("up_proj", xn2, fw_up_t, "nt", F32, tm=1024, tn=1408)
    conv_par = [conv_full, conv_b]
    (act,), conv_saved = _stage_fwd(st["conv"], t, conv_par, [hu])
    ff = _mm("down_proj", act, fw_down, "nn")

    loss_acc, d_ffn_post, dh1, dff = _loss_stage(t, ffn_post_norm, h1, ff, tgt)
    dact = _mm("d_act", dff, fw_down, "nt", F32, tm=1024, tn=1408)
    dw_down = _mm("dw_down", act, dff, "tn", BF, tm=1408, tn=512)
    (dcw, dcb), (dhu,) = _stage_bwd(st["conv"], t, conv_par, [hu], conv_saved, [[dact]], [BF])
    dxn2 = _mm("d_xn2", dhu, fw_up_t, "nn", F32, tm=1024, tn=256)
    dw_up_t = _mm("dw_up", dhu, xn2, "tn", BF, tm=1408, tn=1024)
    (d_post, d_pre2), (dx_a, dmix) = _stage_bwd(st["post1"], t, [attn_post_norm, ffn_pre_norm], [x2, mix], [],
                                                 [[dh1], [dxn2]], [F32, BF])
    dmerged = _mm("d_merged", dmix, fw_out, "nt")
    dw_out = _mm("dw_out", merged, dmix, "tn", BF)
    _, (dga, dgb, dy_a, dy_b) = _stage_bwd(st["merge"], t, [], [z, z, y_a, y_b], [], [[dmerged]], [BF, BF, BF, BF])
    do_a = _mm("d_oa", dy_a, fw_a, "nt")
    dw_a = _mm("dw_a", o_a, dy_a, "tn", BF)
    do_b = _mm("d_ob", dy_b, fw_b, "nt")
    dw_b = _mm("dw_b", o_b, dy_b, "tn", BF)
    early = [dw_up_t.reshape(N_DEV, 704, D), dw_down.reshape(N_DEV, 352, D), dw_a.reshape(N_DEV, 128, D),
             dw_b.reshape(N_DEV, 128, D), dw_out.reshape(N_DEV, 128, D), _cols_to_blocks(dcw.astype(BF), 704)]
    early_scatter = _Exchange("scatter", early)
    mix_dp, dz_hr = _stage_bwd(st["mixers"], t, mix_par, mix_in, mix_saved, [[do_a], [do_b]], [BF],
                               hook=early_scatter)
    d_lb, d_gn, d_mu, d_w0, d_w2p, d_a0, d_a2p, d_g2, d_kk, d_ka, d_lnw, d_lnb, d_rk = mix_dp
    dz = jnp.concatenate(dz_hr + [dga, dgb], axis=1)
    dw_in_t = _mm("dw_in", dz, xn, "tn", BF, tm=256, tn=1024)

    ax, ay, ac = lax.axis_index("x"), lax.axis_index("y"), lax.axis_index("c")
    idx4 = jnp.stack([4 * cx + 2 * cy + ac for cx, cy in ((ax, ay), (1 - ax, ay), (ax, 1 - ay), (1 - ax, 1 - ay))])
    idx4 = idx4.astype(jnp.int32)
    idx_me, idx_0 = idx4[0:1], jnp.zeros((1,), jnp.int32)
    d_small = jnp.concatenate([d_w2p[:64], d_a2p[64:], d_g2], axis=0).astype(BF)
    g8s = [dw_in_t.reshape(N_DEV, 1184, D), _cols_to_blocks(d_small, LANES)]
    recv4s = _reduce_pair(g8s)
    sums = [_pair_sum("pair_sum_" + n, idx4, g, r) for n, g, r in zip(("w_in", "small"), g8s, recv4s)]
    swap_ssem, swap_rsem, swap_srcs, swap_lands, token = _chip_swap_start([s[1] for s in sums])
    dxn = _mm("d_xn", dz, fw_in_t, "nn", F32, tm=512, tn=512, b_outer=True, token=token)
    (d_pre1,), (dx,) = _stage_bwd(st["pre1_res"], t, [attn_pre_norm], [x2], [], [[dxn], [dx_a]], [F32])
    grad_x = dx.reshape(x.shape)
    loss = lax.psum(loss_acc[0, 0], ("x", "y", "c"))

    rg = dict(attn_pre_norm=d_pre1, hgrn_lb=d_lb, hgrn_gnorm=d_gn, rwkv_mu=d_mu, rwkv_w0=d_w0, rwkv_a0=d_a0,
              rwkv_k_k=d_kk, rwkv_k_a=d_ka, rwkv_r_k=d_rk, rwkv_ln_w=d_lnw, rwkv_ln_b=d_lnb, attn_post_norm=d_post,
              ffn_pre_norm=d_pre2, conv_b=dcb, ffn_post_norm=d_ffn_post)
    (g8,) = _all_gather("gather_small_grads", [_pack_replicated(rg)])
    rnames = [n for n, _ in REPL]
    flat = lambda src: [src[n].reshape(1, D) if n == "rwkv_r_k" else src[n] for n in rnames]
    rp_out = _adam_replicated(g8, flat(w), flat(mo), flat(vo))
    recv3s = _chip_swap_wait(swap_ssem, swap_rsem, swap_srcs, swap_lands, rp_out[0]["attn_pre_norm"])
    for kind in range(4):
        rp_out[kind]["rwkv_r_k"] = rp_out[kind]["rwkv_r_k"].reshape(rwkv_r_k.shape)

    def small_of(src):
        return jnp.concatenate([src["rwkv_w2"][0], src["rwkv_a2"][0], src["rwkv_g2"][0]], axis=0)

    sh_out = [dict() for _ in range(4)]
    g_in = _sum_partials("sum_w_in", idx_0, sums[0][0][None], recv3s[0]).T
    res = _adam_sharded("adam_w_in", idx_0, g_in[None], None, *[src["w_in"][0] for src in (w, mo, vo)])
    res_s = _adam_sharded("adam_small", idx_0, sums[1][0][None], recv3s[1], *[small_of(src) for src in (w, mo, vo)])
    for kind in range(4):
        sh_out[kind]["w_in"] = res[kind][None]
        sh_out[kind]["rwkv_w2"] = res_s[kind][0:64][None]
        sh_out[kind]["rwkv_a2"] = res_s[kind][64:128][None]
        sh_out[kind]["rwkv_g2"] = res_s[kind][128:256][None]
    for n, own, recv in zip(_BIG[1:] + ("conv_w",), early, early_scatter.results):
        if n == "w_up":
            g_up = _sum_partials("sum_w_up", idx_me, own, recv).T
            res = _adam_sharded("adam_" + n, idx_0, g_up[None], None, *[src[n][0] for src in (w, mo, vo)])
        else:
            res = _adam_sharded("adam_" + n, idx_me, own, recv, *[src[n][0] for src in (w, mo, vo)])
        for kind in range(4):
            sh_out[kind][n] = res[kind][None]

    outs = [loss, grad_x]
    for kind in range(4):
        for name in _WEIGHTS:
            outs.append(sh_out[kind][name] if name in sh_out[kind] else rp_out[kind][name])
    return tuple(outs)
```

```python
import functools

import jax
import jax.numpy as jnp
from jax import lax
from jax.experimental import pallas as pl
from jax.experimental.pallas import tpu as pltpu

F32 = jnp.float32
BF = jnp.bfloat16
MESH = pl.DeviceIdType.MESH

D = 1024
HG_HEADS = 8
HG_K = 128
HG_CHUNK = 32
HG_SCALE = HG_K ** -0.5
HG_PER_STEP = 8
RW_HEADS = 16
RW_N = 64
RW_CHUNK = 64
RW_PAIRS_PER_STEP = 8
DFF = 2816
IN_COLS = 9472
RW_COLS = 3328
EPS = 1e-6
GN_EPS = 1e-5 * RW_N
ADAM_LR = 0.001
ADAM_B1 = 0.9
ADAM_B2 = 0.999
ADAM_EPS = 1e-08
ADAM_WD = 0.01
ADAM_STEP = 10
N_DEV = 8
LANES = 128
VMEM_LIMIT = 56 * 1024 * 1024
TILE_BYTES = 1280 * 1024

REPL = (("attn_pre_norm", 1024), ("hgrn_lb", 1024), ("hgrn_gnorm", 1024), ("rwkv_mu", 3328), ("rwkv_w0", 1024),
        ("rwkv_a0", 1024), ("rwkv_k_k", 1024), ("rwkv_k_a", 1024), ("rwkv_r_k", 1024), ("rwkv_ln_w", 1024),
        ("rwkv_ln_b", 1024), ("attn_post_norm", 1024), ("ffn_pre_norm", 1024), ("conv_b", 5632), ("ffn_post_norm", 1024))
REPL_ROWS = {"hgrn_lb": 2}
REPL_TOTAL = 32


def _cparams(sem=None, **kw):
    return pltpu.CompilerParams(dimension_semantics=sem, vmem_limit_bytes=VMEM_LIMIT, **kw)


_DN = {"nn": ((1,), (0,)), "nt": ((1,), (1,)), "tn": ((0,), (0,))}


def _raw_dot(a, b, mode):
    return lax.dot_general(a.astype(BF), b.astype(BF), (_DN[mode], ((), ())), preferred_element_type=F32)


@functools.partial(jax.custom_vjp, nondiff_argnums=(2,))
def _dot(a, b, mode):
    return _raw_dot(a, b, mode)


def _dot_fwd(a, b, mode):
    return _raw_dot(a, b, mode), (a, b)


def _dot_bwd(mode, res, g):
    a, b = res
    if mode == "nn":
        return _dot(g, b, "nt"), _dot(a, g, "tn")
    if mode == "nt":
        return _dot(g, b, "nn"), _dot(g, a, "tn")
    return _dot(b, g, "nt"), _dot(a, g, "nn")


_dot.defvjp(_dot_fwd, _dot_bwd)


def _bf_pieces(x, n):
    out, r = [], x
    for i in range(n):
        p = r.astype(BF)
        out.append(p)
        if i + 1 < n:
            r = r - p.astype(F32)
    return out


def _raw_split_dot(x, e, mode, n, x_left):
    eb = e.astype(BF)
    acc = None
    for p in _bf_pieces(x, n):
        ops = (p, eb) if x_left else (eb, p)
        t = lax.dot_general(*ops, (_DN[mode], ((), ())), preferred_element_type=F32)
        acc = t if acc is None else acc + t
    return acc


def _raw_headsum(x):
    t = x.shape[0]
    i = lax.broadcasted_iota(jnp.int32, (LANES, LANES), 0)
    j = lax.broadcasted_iota(jnp.int32, (LANES, LANES), 1)
    same = jnp.where((i >= RW_N) == (j >= RW_N), 1.0, 0.0).astype(F32)
    groups = x.shape[1] // LANES
    rows = jnp.concatenate([x[:, q * LANES:(q + 1) * LANES] for q in range(groups)], axis=0)
    s = _raw_split_dot(rows, same, "nn", 2, True)
    return jnp.concatenate([s[q * t:(q + 1) * t] for q in range(groups)], axis=1)


@jax.custom_vjp
def _headsum(x):
    return _raw_headsum(x)


def _headsum_fwd(x):
    return _raw_headsum(x), None


def _headsum_bwd(_, g):
    return (_raw_headsum(g),)


_headsum.defvjp(_headsum_fwd, _headsum_bwd)


@functools.partial(jax.custom_vjp, nondiff_argnums=(2,))
def _tdot(tri, x, n):
    return _raw_split_dot(x, tri, "nn", n, False)


def _tdot_fwd(tri, x, n):
    return _raw_split_dot(x, tri, "nn", n, False), tri


def _tdot_bwd(n, tri, g):
    return jnp.zeros_like(tri), _raw_split_dot(g, tri, "tn", n, False)


_tdot.defvjp(_tdot_fwd, _tdot_bwd)


def _row(x, i):
    r = lax.broadcasted_iota(jnp.int32, x.shape, 0)
    return jnp.sum(jnp.where(r == i, x, 0.0), axis=0, keepdims=True)


def _shift_down(x, prev):
    t = x.shape[0]

    @jax.custom_vjp
    def sh(x, prev):
        r = lax.broadcasted_iota(jnp.int32, x.shape, 0)
        return jnp.where(r == 0, prev, pltpu.roll(x, 1, 0))

    def fwd(x, prev):
        return sh(x, prev), None

    def bwd(_, g):
        r = lax.broadcasted_iota(jnp.int32, g.shape, 0)
        dx = jnp.where(r == t - 1, 0.0, pltpu.roll(g, t - 1, 0))
        return dx, jnp.sum(jnp.where(r == 0, g, 0.0), axis=0, keepdims=True)

    sh.defvjp(fwd, bwd)
    return sh(x, prev)


def _sigmoid(x):
    return jax.nn.sigmoid(x)


def _silu(x):
    return x * jax.nn.sigmoid(x)


def _softplus(x):
    return jnp.maximum(x, 0.0) + jnp.log(1.0 + jnp.exp(-jnp.abs(x)))


def _rms(x, g):
    return (x * lax.rsqrt(jnp.mean(x * x, axis=-1, keepdims=True) + EPS)) * g


def _tril(c):
    r = lax.broadcasted_iota(jnp.int32, (c, c), 0)
    cc = lax.broadcasted_iota(jnp.int32, (c, c), 1)
    return cc <= r


def _f_pre1(ps, xs, cs):
    return [_rms(xs[0], ps[0])], []


def _f_pre1_residual(ps, xs, cs):
    return [_rms(xs[0], ps[0]), xs[0]], []


def _f_hgrn(ps, xs, cs):
    lbraw, gn = ps
    hq, hf, hi, hg = xs
    hd = range(HG_PER_STEP)
    st = [cs[0][p * HG_K:(p + 1) * HG_K] for p in hd]
    l0, l1 = _row(lbraw, 0), _row(lbraw, 1)
    m = jnp.maximum(l0, l1)
    e0, e1 = jnp.exp(l0 - m), jnp.exp(l1 - m)
    lb = e0 / (e0 + e1)
    q = _silu(hq) * HG_SCALE
    f = lb + (1.0 - lb) * _sigmoid(hf)
    kh = 1.0 - f
    gl = jnp.log(f)
    c = HG_CHUNK
    low = _tril(c)
    tri = jnp.where(low, 1.0, 0.0).astype(F32)
    outs = []
    for i in range(hq.shape[0] // c):
        rows = slice(i * c, (i + 1) * c)
        b = _tdot(tri, gl[rows], 3)
        bref = _row(b, c // 2 - 1)
        blast = _row(b, c - 1)
        qi = q[rows] * jnp.exp(b - bref)
        ki = kh[rows] * jnp.exp(bref - b)
        qd = q[rows] * jnp.exp(b)
        kd = kh[rows] * jnp.exp(blast - b)
        dec = jnp.exp(blast)
        sl = [slice(p * HG_K, (p + 1) * HG_K) for p in hd]
        sc = [jnp.where(low, _dot(qi[:, sl[p]], ki[:, sl[p]], "nt"), 0.0) for p in hd]
        o = [_dot(sc[p], hi[rows, sl[p]], "nn") + _dot(qd[:, sl[p]], st[p], "nt") for p in hd]
        u = [_dot(hi[rows, sl[p]], kd[:, sl[p]], "tn") for p in hd]
        st = [dec[:, sl[p]] * st[p] + u[p] for p in hd]
        outs.append(jnp.concatenate(o, axis=1) if len(o) > 1 else o[0])
    o = outs[0] if len(outs) == 1 else jnp.concatenate(outs, axis=0)
    on = []
    for p in hd:
        op = o[:, p * HG_K:(p + 1) * HG_K]
        on.append(op * lax.rsqrt(jnp.mean(op * op, axis=-1, keepdims=True) + EPS))
    o = jnp.concatenate(on, axis=1) if len(on) > 1 else on[0]
    o = o * gn
    return [o * _silu(hg)], [jnp.concatenate(st, axis=0) if len(st) > 1 else st[0]]


_RW_OFFS = (0, 1024, 2048, 3072, 3200, 3328)


def _f_rwpre(ps, xs, cs):
    mu, w0, w2p, a0, a2p, g2, k_k, k_a = ps
    (prev,) = cs
    t = xs[0].shape[0]
    zs = []
    for i, z in enumerate(xs):
        lo, hi = _RW_OFFS[i], _RW_OFFS[i + 1]
        zs.append(z + mu[:, lo:hi] * (_shift_down(z, prev[:, lo:hi]) - z))
    rr, kr, vr, wa, gz = zs
    w_log = -_softplus(-(w0 + _dot(jnp.tanh(wa), w2p, "nn"))) - 0.5
    lw = -jnp.exp(w_log)
    a = _sigmoid(a0 + _dot(wa, a2p, "nn"))
    g = _dot(_sigmoid(gz), g2, "nn")
    kkr = kr * k_k
    kk = kkr / jnp.maximum(jnp.sqrt(_headsum(kkr * kkr)), 1e-12)
    k2 = kr * (1.0 + (a - 1.0) * k_a)
    newprev = jnp.concatenate([_row(z, t - 1) for z in xs], axis=1)
    return [rr, lw, k2, vr, -kk, kk * a, g], [newprev]


def _raw_inverses(ls):
    n = ls[0].shape[0]
    r = lax.broadcasted_iota(jnp.int32, (n, n), 0)
    c = lax.broadcasted_iota(jnp.int32, (n, n), 1)
    eye = jnp.where(r == c, 1.0, 0.0).astype(F32)
    tinv = [eye + l for l in ls]
    pw = ls
    for _ in range(5):
        pw = [_raw_dot(p, p, "nn") for p in pw]
        tinv = [t + _raw_dot(t, p, "nn") for t, p in zip(tinv, pw)]
    return tinv


@jax.custom_vjp
def _unit_lower_inverses(ls):
    return _raw_inverses(ls)


def _inverses_fwd(ls):
    tinv = _raw_inverses(ls)
    return tinv, tinv


def _inverses_bwd(tinv, gs):
    return ([_raw_dot(_raw_dot(t, g, "tn"), t, "nt") for t, g in zip(tinv, gs)],)


_unit_lower_inverses.defvjp(_inverses_fwd, _inverses_bwd)


def _f_rwscan(ps, xs, cs):
    state = cs[0]
    ys = []
    for i in range(xs[0].shape[0] // RW_CHUNK):
        y, state = _rwkv_chunk([x[i * RW_CHUNK:(i + 1) * RW_CHUNK] for x in xs], state)
        ys.append(y)
    return [ys[0] if len(ys) == 1 else jnp.concatenate(ys, axis=0)], [state]


def _rwkv_chunk(xs, state):
    npair = RW_PAIRS_PER_STEP
    pr = range(npair)
    r, lw, k, v, av, bv = [[x[:, p * LANES:(p + 1) * LANES] for p in pr] for x in xs]
    sv = [state[p * LANES:(p + 1) * LANES] for p in pr]
    c = RW_CHUNK
    n = 2 * c
    tri = jnp.where(_tril(c), 1.0, 0.0).astype(F32)
    cl = [_tdot(tri, lw[p], 3) for p in pr]
    cl_last = [_row(cl[p], c - 1) for p in pr]
    lane = lax.broadcasted_iota(jnp.int32, (c, LANES), 1)
    h0 = lane < RW_N

    def stack(x):
        return jnp.concatenate([jnp.where(h0, x, 0.0), jnp.where(h0, 0.0, x)], axis=0)

    am = [stack(av[p] * jnp.exp(cl[p] - lw[p])) for p in pr]
    bm = [stack(bv[p] * jnp.exp(-cl[p])) for p in pr]
    km = [stack(k[p] * jnp.exp(-cl[p])) for p in pr]
    rm = [stack(r[p] * jnp.exp(cl[p])) for p in pr]
    vm = [stack(v[p]) for p in pr]
    rn = lax.broadcasted_iota(jnp.int32, (n, n), 0)
    cn = lax.broadcasted_iota(jnp.int32, (n, n), 1)
    blk = (rn >= c) == (cn >= c)
    strict = blk & (cn < rn)
    incl = blk & (cn <= rn)
    lab = [jnp.where(strict, _dot(am[p], bm[p], "nt"), 0.0) for p in pr]
    lak = [jnp.where(strict, _dot(am[p], km[p], "nt"), 0.0) for p in pr]
    wrb = [jnp.where(incl, _dot(rm[p], bm[p], "nt"), 0.0) for p in pr]
    wrk = [jnp.where(incl, _dot(rm[p], km[p], "nt"), 0.0) for p in pr]
    tinv = _unit_lower_inverses(lab)
    rhs = [_dot(am[p], sv[p], "nt") + _dot(lak[p], vm[p], "nn") for p in pr]
    um = [_dot(tinv[p], rhs[p], "nn") for p in pr]
    ym = [_dot(rm[p], sv[p], "nt") + _dot(wrb[p], um[p], "nn") + _dot(wrk[p], vm[p], "nn") for p in pr]
    sn = [(sv[p] + _dot(um[p], bm[p], "tn") + _dot(vm[p], km[p], "tn")) * jnp.exp(cl_last[p]) for p in pr]
    ys = [ym[p][:c] + ym[p][c:] for p in pr]
    return jnp.concatenate(ys, axis=1), jnp.concatenate(sn, axis=0)


def _f_mixers(ps, xs, cs):
    oa, st = _f_hgrn(ps[:2], xs[:4], cs[:1])
    (r, lw, k, v, av, bv, g), prev = _f_rwpre(ps[2:10], xs[4:], cs[1:2])
    y, sv = _f_rwscan([], [r, lw, k, v, av, bv], cs[2:])
    ob, _ = _f_rwpost(ps[10:], y + [r, k, v, g], [])
    return oa + ob, st + prev + sv


def _f_rwpost(ps, xs, cs):
    ln_w, ln_b, r_k = ps
    y, r, k, v, g = xs
    inv_n = 1.0 / RW_N
    yc = y - _headsum(y) * inv_n
    var = _headsum(yc * yc) * inv_n
    yn = yc * lax.rsqrt(var + GN_EPS)
    yn = yn * ln_w + ln_b
    bonus = _headsum(r * k * r_k) * v
    return [(yn + bonus) * g], []


def _f_merge(ps, xs, cs):
    ga, gb, ya, yb = xs
    return [_sigmoid(ga) * ya + _sigmoid(gb) * yb], []


def _f_post1(ps, xs, cs):
    x, mix = xs
    h1 = x + _rms(mix, ps[0])
    return [h1, _rms(h1, ps[1])], []


def _f_conv(ps, xs, cs):
    cw, cb = ps
    p1, p2 = cs
    w0, w1, w2 = _row(cw, 0), _row(cw, 1), _row(cw, 2)
    t = xs[0].shape[0]
    hc = []
    for i, x in enumerate(xs):
        sl = slice(i * DFF, (i + 1) * DFF)
        s1 = _shift_down(x, p1[:, sl])
        s2 = _shift_down(s1, p2[:, sl])
        hc.append(cb[:, sl] + w0[:, sl] * s2 + w1[:, sl] * s1 + w2[:, sl] * x)
    n1 = jnp.concatenate([_row(x, t - 1) for x in xs], axis=1)
    n2 = jnp.concatenate([_row(x, t - 2) for x in xs], axis=1)
    return [_silu(hc[0]) * hc[1]], [n1, n2]


class _Stage:
    def __init__(self, name, f, g, tm, par_per_g, in_pieces, in_offs, carry_shapes, out_pieces, out_dtypes):
        self.name, self.f, self.g, self.tm = name, f, g, tm
        self.par_per_g, self.in_pieces, self.in_offs = par_per_g, in_pieces, in_offs
        self.carry_shapes, self.out_pieces, self.out_dtypes = carry_shapes, out_pieces, out_dtypes


def _par_spec(arr, per_g, g):
    r, c = arr.shape
    if per_g:
        return pl.BlockSpec((r, c // g), lambda gi, ni: (0, gi))
    return pl.BlockSpec((r, c), lambda gi, ni: (0, 0))


def _row_spec(tm, width, off, n, rev):
    if rev:
        return pl.BlockSpec((tm, width), lambda gi, ni: (n - 1 - ni, off + gi))
    return pl.BlockSpec((tm, width), lambda gi, ni: (ni, off + gi))


def _carry_spec(shape, n, rev):
    if rev:
        return pl.BlockSpec((None, None) + shape, lambda gi, ni: (gi, n - 1 - ni, 0, 0))
    return pl.BlockSpec((None, None) + shape, lambda gi, ni: (gi, ni, 0, 0))


def _load_pieces(refs, pieces_list):
    out = []
    for ref, pieces in zip(refs, pieces_list):
        o = 0
        for w in pieces:
            out.append(ref[:, o:o + w].astype(F32))
            o += w
    return out


def _store_pieces(refs, pieces_list, vals):
    k = 0
    for ref, pieces in zip(refs, pieces_list):
        o = 0
        for w in pieces:
            ref[:, o:o + w] = vals[k].astype(ref.dtype)
            k += 1
            o += w


_ANY = pl.BlockSpec(memory_space=pl.ANY)


class _Exchange:
    def __init__(self, kind, arrs):
        self.kind, self.arrs, self.results = kind, list(arrs), None
        if kind == "scatter":
            self.out_shape = [jax.ShapeDtypeStruct((N_DEV - 1,) + a.shape[1:], a.dtype) for a in self.arrs]
        else:
            self.out_shape = [jax.ShapeDtypeStruct((N_DEV,) + a.shape, a.dtype) for a in self.arrs]
        self.nsem = (N_DEV if kind == "gather2" else N_DEV - 1) * len(self.arrs)

    def copies(self, in_refs, out_refs, ssem, rsem):
        x, y, c = lax.axis_index("x"), lax.axis_index("y"), lax.axis_index("c")
        me = 4 * x + 2 * y + c
        cps = []
        for a, (i_ref, o_ref) in enumerate(zip(in_refs, out_refs)):
            for j in range(1, N_DEV):
                px = 1 - x if j & 4 else x
                py = 1 - y if j & 2 else y
                pc = 1 - c if j & 1 else c
                if self.kind == "gather":
                    src, dst = i_ref, o_ref.at[me]
                else:
                    src, dst = i_ref.at[4 * px + 2 * py + pc], o_ref.at[j - 1]
                s = (N_DEV - 1) * a + j - 1
                cps.append(pltpu.make_async_remote_copy(src_ref=src, dst_ref=dst, send_sem=ssem.at[s],
                                                        recv_sem=rsem.at[s], device_id=(px, py, pc),
                                                        device_id_type=MESH))
        return cps

    def run(self, step, total, in_refs, out_refs, ssem, rsem):
        if self.kind == "gather2":
            return self.run_two_level(step, total, in_refs, out_refs, ssem, rsem)

        @pl.when(step == 0)
        def _():
            for cp in self.copies(in_refs, out_refs, ssem, rsem):
                cp.start()

        @pl.when(step == total - 1)
        def _():
            for cp in self.copies(in_refs, out_refs, ssem, rsem):
                cp.wait()

    def run_two_level(self, step, total, in_refs, out_refs, ssem, rsem):
        x, y, c = lax.axis_index("x"), lax.axis_index("y"), lax.axis_index("c")
        sibling, xn, yn = (x, y, 1 - c), (1 - x, y, c), (x, 1 - y, c)
        arrs = range(len(in_refs))
        ns = N_DEV

        def num(px, py, pc):
            return 4 * px + 2 * py + pc

        def copy(a, k, to, src, dst):
            return pltpu.make_async_remote_copy(src_ref=src, dst_ref=dst, send_sem=ssem.at[ns * a + k],
                                                recv_sem=rsem.at[ns * a + k], device_id=to, device_id_type=MESH)

        def blk(a, b):
            return out_refs[a].at[b]

        def half(a, b, second):
            h = self.arrs[a].shape[0] // 2
            return out_refs[a].at[b, pl.ds(h if second else 0, h)]

        bx, by, bd = num(1 - x, y, c), num(x, 1 - y, c), num(1 - x, 1 - y, c)

        def firsts(a):
            own = blk(a, num(x, y, c))
            return [copy(a, 0, sibling, in_refs[a], own), copy(a, 1, xn, in_refs[a], own),
                    copy(a, 2, yn, in_refs[a], own)]

        def seconds(a):
            return [copy(a, 3, yn, half(a, bx, False), half(a, bx, False)), copy(a, 5, sibling, blk(a, bx), blk(a, bx)),
                    copy(a, 4, xn, half(a, by, True), half(a, by, True)), copy(a, 6, sibling, blk(a, by), blk(a, by))]

        def third(a):
            return copy(a, 7, sibling, blk(a, bd), blk(a, bd))

        @pl.when(step == 0)
        def _():
            for a in arrs:
                for cp in firsts(a):
                    cp.start()

        @pl.when(step == total // 2)
        def _():
            for a in arrs:
                copy(a, 1, xn, blk(a, bx), blk(a, bx)).wait_recv()
                copy(a, 2, yn, blk(a, by), blk(a, by)).wait_recv()
                for cp in seconds(a):
                    cp.start()

        @pl.when(step == (4 * total) // 5)
        def _():
            for a in arrs:
                copy(a, 3, yn, half(a, bd, False), half(a, bd, False)).wait_recv()
                copy(a, 4, xn, half(a, bd, True), half(a, bd, True)).wait_recv()
                third(a).start()

        @pl.when(step == total - 1)
        def _():
            for a in arrs:
                for k, b in ((0, num(x, y, 1 - c)), (5, num(1 - x, y, 1 - c)), (6, num(x, 1 - y, 1 - c)),
                             (7, num(1 - x, 1 - y, 1 - c))):
                    copy(a, k, sibling, blk(a, b), blk(a, b)).wait_recv()
                for cp in firsts(a) + seconds(a) + [third(a)]:
                    cp.wait_send()


def _hook_specs(hook):
    if hook is None:
        return [], [], [], []
    na = len(hook.arrs)
    sems = [pltpu.SemaphoreType.DMA((hook.nsem,)), pltpu.SemaphoreType.DMA((hook.nsem,))]
    return [_ANY] * na, [_ANY] * na, hook.out_shape, sems


def _stage_fwd(st, t, params, inputs, hook=None):
    g, tm = st.g, min(st.tm, t)
    n = t // tm
    npar, nin, ncar, nout = len(params), len(inputs), len(st.carry_shapes), len(st.out_pieces)
    h_in, h_out, h_shape, h_sems = _hook_specs(hook)
    nh = len(h_in)

    def body(*refs):
        p_refs = refs[:npar]
        x_refs = refs[npar:npar + nin]
        hi_refs = refs[npar + nin:npar + nin + nh]
        o = npar + nin + nh
        o_refs = refs[o:o + nout]
        s_refs = refs[o + nout:o + nout + ncar]
        ho_refs = refs[o + nout + ncar:o + nout + ncar + nh]
        c_scr = refs[o + nout + ncar + nh:o + nout + ncar + nh + ncar]
        gi, ni = pl.program_id(0), pl.program_id(1)
        if hook is not None:
            step = gi * n + ni
            hook.run(step, g * n, hi_refs, ho_refs, *refs[-2:])

        @pl.when(ni == 0)
        def _():
            for c in c_scr:
                c[...] = jnp.zeros(c.shape, F32)

        ps = [r[...].astype(F32) for r in p_refs]
        xs = _load_pieces(x_refs, st.in_pieces)
        cs = [c[...] for c in c_scr]
        for s, c in zip(s_refs, cs):
            s[...] = c
        outs, ncs = st.f(ps, xs, cs)
        _store_pieces(o_refs, st.out_pieces, outs)
        for c, v in zip(c_scr, ncs):
            c[...] = v

    in_specs = [_par_spec(p, pg, g) for p, pg in zip(params, st.par_per_g)]
    in_specs += [_row_spec(tm, sum(pc), off, n, False) for pc, off in zip(st.in_pieces, st.in_offs)]
    out_specs = [_row_spec(tm, sum(pc), 0, n, False) for pc in st.out_pieces]
    out_specs += [_carry_spec(s, n, False) for s in st.carry_shapes]
    out_shape = [jax.ShapeDtypeStruct((t, g * sum(pc)), dt) for pc, dt in zip(st.out_pieces, st.out_dtypes)]
    out_shape += [jax.ShapeDtypeStruct((g, n) + s, F32) for s in st.carry_shapes]
    res = pl.pallas_call(
        body, name=st.name + "_fwd", grid=(g, n), in_specs=in_specs + h_in, out_specs=out_specs + h_out,
        out_shape=out_shape + h_shape,
        scratch_shapes=[pltpu.VMEM(s, F32) for s in st.carry_shapes] + h_sems,
        compiler_params=_cparams(("arbitrary", "arbitrary")),
    )(*params, *inputs, *(hook.arrs if hook else []))
    if hook is not None:
        hook.results = list(res[nout + ncar:])
    return list(res[:nout]), list(res[nout:nout + ncar])


def _stage_bwd(st, t, params, inputs, saved, douts, dx_dtypes, hook=None):
    g, tm = st.g, min(st.tm, t)
    n = t // tm
    npar, nin, ncar = len(params), len(inputs), len(st.carry_shapes)
    flat_d = [d for ds in douts for d in ds]
    nd = len(flat_d)
    dx_idx = [i for i, dt in enumerate(dx_dtypes) if dt is not None]
    h_in, h_out, h_shape, h_sems = _hook_specs(hook)
    nh = len(h_in)

    def body(*refs):
        p_refs = refs[:npar]
        x_refs = refs[npar:npar + nin]
        s_refs = refs[npar + nin:npar + nin + ncar]
        d_refs = refs[npar + nin + ncar:npar + nin + ncar + nd]
        hi_refs = refs[npar + nin + ncar + nd:npar + nin + ncar + nd + nh]
        o = npar + nin + ncar + nd + nh
        dp_refs = refs[o:o + npar]
        dx_refs = refs[o + npar:o + npar + len(dx_idx)]
        ho_refs = refs[o + npar + len(dx_idx):o + npar + len(dx_idx) + nh]
        dc_scr = refs[o + npar + len(dx_idx) + nh:o + npar + len(dx_idx) + nh + ncar]
        gi, ni = pl.program_id(0), pl.program_id(1)
        if hook is not None:
            step = gi * n + ni
            hook.run(step, g * n, hi_refs, ho_refs, *refs[-2:])

        @pl.when(ni == 0)
        def _():
            for c in dc_scr:
                c[...] = jnp.zeros(c.shape, F32)

        ps = [r[...].astype(F32) for r in p_refs]
        xs = _load_pieces(x_refs, st.in_pieces)
        cs = [s[...] for s in s_refs]
        dys = []
        k = 0
        for ds, pieces in zip(douts, st.out_pieces):
            acc = _load_pieces([d_refs[k]], [pieces])
            for j in range(1, len(ds)):
                more = _load_pieces([d_refs[k + j]], [pieces])
                acc = [a + b for a, b in zip(acc, more)]
            dys += acc
            k += len(ds)
        _, vjp = jax.vjp(st.f, ps, xs, cs)
        dps, dxs, dcs = vjp((dys, [c[...] for c in dc_scr]))
        k = 0
        per_in = []
        for pieces in st.in_pieces:
            per_in.append(dxs[k:k + len(pieces)])
            k += len(pieces)
        for ref, i in zip(dx_refs, dx_idx):
            _store_pieces([ref], [st.in_pieces[i]], per_in[i])
        for c, v in zip(dc_scr, dcs):
            c[...] = v
        for ref, dp, pg in zip(dp_refs, dps, st.par_per_g):
            first = (ni == 0) if pg else ((ni == 0) & (gi == 0))

            @pl.when(first)
            def _():
                ref[...] = jnp.zeros(ref.shape, F32)

            ref[...] += dp

    in_specs = [_par_spec(p, pg, g) for p, pg in zip(params, st.par_per_g)]
    in_specs += [_row_spec(tm, sum(pc), off, n, True) for pc, off in zip(st.in_pieces, st.in_offs)]
    in_specs += [_carry_spec(s, n, True) for s in st.carry_shapes]
    for ds, pc in zip(douts, st.out_pieces):
        in_specs += [_row_spec(tm, sum(pc), 0, n, True) for _ in ds]
    out_specs = [_par_spec(p, pg, g) for p, pg in zip(params, st.par_per_g)]
    out_specs += [_row_spec(tm, sum(st.in_pieces[i]), 0, n, True) for i in dx_idx]
    out_shape = [jax.ShapeDtypeStruct(p.shape, F32) for p in params]
    out_shape += [jax.ShapeDtypeStruct((t, g * sum(st.in_pieces[i])), dx_dtypes[i]) for i in dx_idx]
    res = pl.pallas_call(
        body, name=st.name + "_bwd", grid=(g, n), in_specs=in_specs + h_in, out_specs=out_specs + h_out,
        out_shape=out_shape + h_shape,
        scratch_shapes=[pltpu.VMEM(s, F32) for s in st.carry_shapes] + h_sems,
        compiler_params=_cparams(("arbitrary", "arbitrary")),
    )(*params, *inputs, *saved, *flat_d, *(hook.arrs if hook else []))
    if hook is not None:
        hook.results = list(res[npar + len(dx_idx):])
    return list(res[:npar]), list(res[npar:npar + len(dx_idx)])


def _pick(n, cap):
    if n <= cap:
        return n
    best = LANES
    for k in range(1, n // LANES + 1):
        if (n // LANES) % k == 0 and k * LANES <= cap:
            best = k * LANES
    return best


def _mm(name, a, b, mode, out_dtype=F32, tm=1024, tn=512, b_outer=False, token=None):
    m = a.shape[1] if mode == "tn" else a.shape[0]
    k = a.shape[0] if mode == "tn" else a.shape[1]
    n = b.shape[0] if mode == "nt" else b.shape[1]
    tm, tn = _pick(m, tm), _pick(n, tn)
    if b_outer:
        grid = (n // tn, m // tm)
        ij = lambda p, q: (q, p)
    else:
        grid = (m // tm, n // tn)
        ij = lambda p, q: (p, q)
    extra = [] if token is None else [token]

    def body(*refs):
        a_ref, b_ref, o_ref = refs[0], refs[1], refs[-1]
        o_ref[...] = _raw_dot(a_ref[...], b_ref[...], mode).astype(o_ref.dtype)

    if mode == "tn":
        a_spec = pl.BlockSpec((k, tm), lambda p, q: (0, ij(p, q)[0]))
    else:
        a_spec = pl.BlockSpec((tm, k), lambda p, q: (ij(p, q)[0], 0))
    if mode == "nt":
        b_spec = pl.BlockSpec((tn, k), lambda p, q: (ij(p, q)[1], 0))
    else:
        b_spec = pl.BlockSpec((k, tn), lambda p, q: (0, ij(p, q)[1]))
    return pl.pallas_call(
        body, name=name, grid=grid,
        in_specs=[a_spec, b_spec] + [pl.BlockSpec(e.shape, lambda p, q: (0, 0)) for e in extra],
        out_specs=pl.BlockSpec((tm, tn), lambda p, q: ij(p, q)),
        out_shape=jax.ShapeDtypeStruct((m, n), out_dtype),
        compiler_params=_cparams(("arbitrary", "arbitrary")),
    )(a, b, *extra)


def _loss_stage(t, g_post, h1, ff, tgt):
    tm = min(256, t)
    n = t // tm

    def body(g_ref, h_ref, f_ref, t_ref, loss_ref, dg_ref, dh_ref, df_ref):
        ni = pl.program_id(0)
        target = t_ref[...]

        def lossf(g, h1, ff):
            e = h1 + _rms(ff, g) - target
            return 0.5 * jnp.sum(jnp.mean(e * e, axis=-1))

        l, (dg, dh, df) = jax.value_and_grad(lossf, argnums=(0, 1, 2))(g_ref[...], h_ref[...], f_ref[...])

        @pl.when(ni == 0)
        def _():
            loss_ref[...] = jnp.zeros(loss_ref.shape, F32)
            dg_ref[...] = jnp.zeros(dg_ref.shape, F32)

        loss_ref[...] += jnp.full(loss_ref.shape, l, F32)
        dg_ref[...] += dg
        dh_ref[...] = dh
        df_ref[...] = df.astype(df_ref.dtype)

    row = pl.BlockSpec((tm, D), lambda ni: (ni, 0))
    one = pl.BlockSpec((1, D), lambda ni: (0, 0))
    return pl.pallas_call(
        body, name="loss_head", grid=(n,), in_specs=[one, row, row, row],
        out_specs=[pl.BlockSpec((1, LANES), lambda ni: (0, 0)), one, row, row],
        out_shape=[jax.ShapeDtypeStruct((1, LANES), F32), jax.ShapeDtypeStruct((1, D), F32),
                   jax.ShapeDtypeStruct((t, D), F32), jax.ShapeDtypeStruct((t, D), BF)],
        compiler_params=_cparams(("arbitrary",)),
    )(g_post, h1, ff, tgt)


_ANY = pl.BlockSpec(memory_space=pl.ANY)


def _all_gather(name, blks):
    na = len(blks)
    ns = 8

    def body(*refs):
        x_refs, out_refs = refs[:na], refs[na:2 * na]
        send_sems, recv_sems, local_sems = refs[2 * na:]
        x, y, cc = lax.axis_index("x"), lax.axis_index("y"), lax.axis_index("c")
        sibling, xn, yn = (x, y, 1 - cc), (1 - x, y, cc), (x, 1 - y, cc)

        def num(px, py, pc):
            return 4 * px + 2 * py + pc

        def copy(a, k, to, src, dst):
            return pltpu.make_async_remote_copy(src_ref=src, dst_ref=dst, send_sem=send_sems.at[ns * a + k],
                                                recv_sem=recv_sems.at[ns * a + k], device_id=to, device_id_type=MESH)

        def halves(a, blk):
            h = blks[a].shape[0] // 2
            return out_refs[a].at[blk, pl.ds(0, h)], out_refs[a].at[blk, pl.ds(h, h)]

        mine, sends = [], []
        for a in range(na):
            o = out_refs[a]
            m = pltpu.make_async_copy(x_refs[a], o.at[num(x, y, cc)], local_sems.at[a])
            m.start()
            mine.append(m)
            own = o.at[num(x, y, cc)]
            sends.append([copy(a, 0, sibling, x_refs[a], own), copy(a, 1, xn, x_refs[a], own),
                          copy(a, 2, yn, x_refs[a], own)])
            for cp in sends[a]:
                cp.start()
        for a in range(na):
            o = out_refs[a]
            bx, by, bd = num(1 - x, y, cc), num(x, 1 - y, cc), num(1 - x, 1 - y, cc)
            copy(a, 1, xn, o.at[bx], o.at[bx]).wait_recv()
            more = [copy(a, 3, yn, halves(a, bx)[0], halves(a, bx)[0]), copy(a, 5, sibling, o.at[bx], o.at[bx])]
            for cp in more:
                cp.start()
            sends[a] += more
        for a in range(na):
            o = out_refs[a]
            bx, by, bd = num(1 - x, y, cc), num(x, 1 - y, cc), num(1 - x, 1 - y, cc)
            copy(a, 2, yn, o.at[by], o.at[by]).wait_recv()
            more = [copy(a, 4, xn, halves(a, by)[1], halves(a, by)[1]), copy(a, 6, sibling, o.at[by], o.at[by])]
            for cp in more:
                cp.start()
            sends[a] += more
        for a in range(na):
            o = out_refs[a]
            bd = num(1 - x, 1 - y, cc)
            copy(a, 3, yn, halves(a, bd)[0], halves(a, bd)[0]).wait_recv()
            copy(a, 4, xn, halves(a, bd)[1], halves(a, bd)[1]).wait_recv()
            fw = copy(a, 7, sibling, o.at[bd], o.at[bd])
            fw.start()
            sends[a].append(fw)
        for a in range(na):
            o = out_refs[a]
            for k, blk in ((0, num(x, y, 1 - cc)), (5, num(1 - x, y, 1 - cc)), (6, num(x, 1 - y, 1 - cc)),
                           (7, num(1 - x, 1 - y, 1 - cc))):
                copy(a, k, sibling, o.at[blk], o.at[blk]).wait_recv()
            for cp in sends[a]:
                cp.wait_send()
        for m in mine:
            m.wait()

    res = pl.pallas_call(
        body, name=name, in_specs=[_ANY] * na, out_specs=[_ANY] * na,
        out_shape=[jax.ShapeDtypeStruct((N_DEV,) + b.shape, b.dtype) for b in blks],
        scratch_shapes=[pltpu.SemaphoreType.DMA((ns * na,)), pltpu.SemaphoreType.DMA((ns * na,)),
                        pltpu.SemaphoreType.DMA((na,))],
    )(*blks)
    return list(res)


def _reduce_pair(g8s):
    na = len(g8s)

    def body(*refs):
        g_refs, recv_refs = refs[:na], refs[na:2 * na]
        ssem, rsem = refs[2 * na:]
        x, y, cc = lax.axis_index("x"), lax.axis_index("y"), lax.axis_index("c")
        chips = [(x, y), (1 - x, y), (x, 1 - y), (1 - x, 1 - y)]
        sib = (x, y, 1 - cc)
        for a in range(na):
            for k, (cx, cy) in enumerate(chips):
                pltpu.make_async_remote_copy(
                    src_ref=g_refs[a].at[4 * cx + 2 * cy + 1 - cc], dst_ref=recv_refs[a].at[k],
                    send_sem=ssem.at[a], recv_sem=rsem.at[a], device_id=sib, device_id_type=MESH).start()
        for a in range(na):
            pltpu.make_async_remote_copy(src_ref=recv_refs[a], dst_ref=recv_refs[a], send_sem=ssem.at[a],
                                         recv_sem=rsem.at[a], device_id=sib, device_id_type=MESH).wait()

    res = pl.pallas_call(
        body, name="reduce_pair", in_specs=[_ANY] * na, out_specs=[_ANY] * na,
        out_shape=[jax.ShapeDtypeStruct((4,) + g.shape[1:], g.dtype) for g in g8s],
        scratch_shapes=[pltpu.SemaphoreType.DMA((na,)), pltpu.SemaphoreType.DMA((na,))],
    )(*g8s)
    return list(res)


_HBM = pl.BlockSpec(memory_space=pltpu.HBM)
_SEM = pl.BlockSpec(memory_space=pltpu.SEMAPHORE)
_EFFECT = pltpu.SideEffectType.DATAFLOW_SIDE_EFFECTING


def _chip_swap_copies(s_refs, land_refs, ssem, rsem):
    x, y, c = lax.axis_index("x"), lax.axis_index("y"), lax.axis_index("c")
    targets = [(1 - x, y, c), (x, 1 - y, c), (1 - x, 1 - y, c)]
    return [pltpu.make_async_remote_copy(src_ref=s.at[k], dst_ref=d.at[k], send_sem=ssem.at[3 * a + k],
                                         recv_sem=rsem.at[3 * a + k], device_id=targets[k], device_id_type=MESH)
            for a, (s, d) in enumerate(zip(s_refs, land_refs)) for k in range(3)]


def _chip_swap_start(sends):
    na = len(sends)

    def body(*refs):
        cps = _chip_swap_copies(refs[:na], refs[na:2 * na], refs[2 * na], refs[2 * na + 1])
        for cp in cps:
            cp.start()
        token = refs[-1]
        token[...] = jnp.zeros(token.shape, token.dtype)

    bufs = [pltpu.HBM(s.shape, s.dtype) for s in sends]
    res = pl.pallas_call(
        body, name="chip_swap_start",
        out_shape=[pltpu.SemaphoreType.DMA((3 * na,)), pltpu.SemaphoreType.DMA((3 * na,))] + bufs + bufs
        + [jax.ShapeDtypeStruct((8, LANES), F32)],
        in_specs=[_HBM] * (2 * na), out_specs=[_SEM, _SEM] + [_HBM] * (2 * na) + [pl.BlockSpec(memory_space=pltpu.VMEM)],
        input_output_aliases={i: 2 + i for i in range(2 * na)},
        compiler_params=pltpu.CompilerParams(has_side_effects=_EFFECT),
    )(*[pltpu.with_memory_space_constraint(s, pltpu.HBM) for s in sends],
      *[pltpu.with_memory_space_constraint(lax.empty(s.shape, s.dtype), pltpu.HBM) for s in sends])
    return res[0], res[1], list(res[2:2 + na]), list(res[2 + na:2 + 2 * na]), res[-1]


def _chip_swap_wait(ssem, rsem, srcs, lands, after):
    na = len(srcs)

    def body(*refs):
        cps = _chip_swap_copies(refs[:na], refs[na:2 * na], refs[2 * na], refs[2 * na + 1])
        for cp in cps:
            cp.wait_send()
            cp.wait_recv()

    bufs = [pltpu.HBM(s.shape, s.dtype) for s in srcs]
    res = pl.pallas_call(
        body, name="chip_swap_wait", out_shape=bufs + bufs,
        in_specs=[_HBM] * (2 * na) + [_SEM, _SEM, _ANY], out_specs=[_HBM] * (2 * na),
        input_output_aliases={i: i for i in range(2 * na)},
        compiler_params=pltpu.CompilerParams(has_side_effects=_EFFECT),
    )(*srcs, *lands, ssem, rsem, after)
    return list(res[na:])


def _pick_rows(r, c, budget=TILE_BYTES):
    if r * c * 4 <= budget or r % 16:
        return r
    best = 16
    for tr in range(16, r, 16):
        if r % tr == 0 and tr * c * 4 <= budget:
            best = tr
    return best


def _pair_sum(name, idx4, g8, recv4):
    _, r, c = g8.shape
    tr = _pick_rows(r, c, 2 * TILE_BYTES)

    def body(idx_ref, a_ref, b_ref, o0_ref, o3_ref):
        k = pl.program_id(1)
        s = a_ref[...].astype(F32) + b_ref[...].astype(F32)

        @pl.when(k == 0)
        def _():
            o0_ref[...] = s

        @pl.when(k > 0)
        def _():
            o3_ref[...] = s.astype(BF)

    spec = pltpu.PrefetchScalarGridSpec(
        num_scalar_prefetch=1, grid=(r // tr, 4),
        in_specs=[pl.BlockSpec((None, tr, c), lambda i, k, idx: (idx[k], i, 0)),
                  pl.BlockSpec((None, tr, c), lambda i, k, idx: (k, i, 0))],
        out_specs=[pl.BlockSpec((tr, c), lambda i, k, idx: (i, 0)),
                   pl.BlockSpec((None, tr, c), lambda i, k, idx: (jnp.maximum(k - 1, 0), i, 0))])
    return pl.pallas_call(
        body, name=name, grid_spec=spec,
        out_shape=[jax.ShapeDtypeStruct((r, c), F32), jax.ShapeDtypeStruct((3, r, c), BF)],
        compiler_params=_cparams(("arbitrary", "arbitrary")),
    )(idx4, g8, recv4)


def _adamw(w, g, m, v):
    m = ADAM_B1 * m + (1.0 - ADAM_B1) * g
    v = ADAM_B2 * v + (1.0 - ADAM_B2) * jnp.square(g)
    m_hat = m / (1.0 - ADAM_B1 ** ADAM_STEP)
    v_hat = v / (1.0 - ADAM_B2 ** ADAM_STEP)
    delta = -ADAM_LR * (m_hat / (jnp.sqrt(v_hat) + ADAM_EPS) + ADAM_WD * w)
    return delta, m, v


def _sum_partials(name, idx1, own, recv):
    _, r, c = own.shape
    tr = _pick_rows(r, c, 2 * TILE_BYTES)
    nj = recv.shape[0]

    def body(idx_ref, p_ref, r_ref, g_out):
        g = p_ref[...].astype(F32)
        for k in range(nj):
            g = g + r_ref[k].astype(F32)
        g_out[...] = g

    row = pl.BlockSpec((tr, c), lambda i, idx: (i, 0))
    spec = pltpu.PrefetchScalarGridSpec(
        num_scalar_prefetch=1, grid=(r // tr,),
        in_specs=[pl.BlockSpec((None, tr, c), lambda i, idx: (idx[0], i, 0)),
                  pl.BlockSpec((nj, tr, c), lambda i, idx: (0, i, 0))],
        out_specs=row)
    return pl.pallas_call(body, name=name, grid_spec=spec, out_shape=jax.ShapeDtypeStruct((r, c), F32),
                          compiler_params=_cparams(("arbitrary",)))(idx1, own, recv)


def _adam_sharded(name, idx1, own, recv, w, m, v):
    r, c = w.shape
    tr = _pick_rows(r, c)
    nj = 0 if recv is None else recv.shape[0]
    if recv is None:
        recv = jnp.zeros((1, 8, LANES), BF)

    def body(idx_ref, p_ref, r_ref, w_ref, m_ref, v_ref, g_out, d_out, m_out, v_out):
        g = p_ref[...].astype(F32)
        for k in range(nj):
            g = g + r_ref[k].astype(F32)
        d, mn, vn = _adamw(w_ref[...], g, m_ref[...], v_ref[...])
        g_out[...] = g
        d_out[...] = d
        m_out[...] = mn
        v_out[...] = vn

    row = pl.BlockSpec((tr, c), lambda i, idx: (i, 0))
    if nj:
        recv_spec = pl.BlockSpec((nj, tr, c), lambda i, idx: (0, i, 0))
    else:
        recv_spec = pl.BlockSpec(recv.shape, lambda i, idx: (0, 0, 0))
    spec = pltpu.PrefetchScalarGridSpec(
        num_scalar_prefetch=1, grid=(r // tr,),
        in_specs=[pl.BlockSpec((None, tr, c), lambda i, idx: (idx[0], i, 0)), recv_spec, row, row, row],
        out_specs=[row] * 4)
    return pl.pallas_call(
        body, name=name, grid_spec=spec, out_shape=[jax.ShapeDtypeStruct((r, c), F32)] * 4,
        compiler_params=_cparams(("arbitrary",)),
    )(idx1, own, recv, w, m, v)


def _repl_rows():
    rows, r = {}, 0
    for name, cols in REPL:
        rows[name] = r
        r += REPL_ROWS.get(name, 1) * ((cols + D - 1) // D)
    return rows


def _pack_replicated(grads):
    rows = _repl_rows()
    names = [n for n, _ in REPL]

    def body(*refs):
        o_ref = refs[-1]
        o_ref[...] = jnp.zeros(o_ref.shape, F32)
        for name, ref in zip(names, refs[:-1]):
            r0 = rows[name]
            nr, nc = ref.shape
            if nc <= D:
                o_ref[r0:r0 + nr, 0:nc] = ref[...]
            else:
                for j in range((nc + D - 1) // D):
                    lo, hi = j * D, min(nc, (j + 1) * D)
                    o_ref[r0 + j:r0 + j + 1, 0:hi - lo] = ref[:, lo:hi]

    return pl.pallas_call(body, name="pack_replicated", out_shape=jax.ShapeDtypeStruct((REPL_TOTAL, D), F32),
                          compiler_params=_cparams())(*[grads[n] for n in names])


def _adam_replicated(g8, ws, ms, vs):
    rows = _repl_rows()
    names = [n for n, _ in REPL]
    np_ = len(names)

    def body(*refs):
        g_ref = refs[0]
        w_refs, m_refs, v_refs = refs[1:1 + np_], refs[1 + np_:1 + 2 * np_], refs[1 + 2 * np_:1 + 3 * np_]
        outs = refs[1 + 3 * np_:1 + 7 * np_]
        scr = refs[-1]
        g = g_ref[0]
        for k in range(1, N_DEV):
            g = g + g_ref[k]
        scr[...] = g
        for i, name in enumerate(names):
            r0 = rows[name]
            nr, nc = w_refs[i].shape
            if nc <= D:
                gi = scr[r0:r0 + nr, 0:nc]
            else:
                parts = []
                for j in range((nc + D - 1) // D):
                    lo, hi = j * D, min(nc, (j + 1) * D)
                    parts.append(scr[r0 + j:r0 + j + 1, 0:hi - lo])
                gi = jnp.concatenate(parts, axis=1)
            d, mn, vn = _adamw(w_refs[i][...], gi, m_refs[i][...], v_refs[i][...])
            outs[i][...] = gi
            outs[np_ + i][...] = d
            outs[2 * np_ + i][...] = mn
            outs[3 * np_ + i][...] = vn

    shp = [jax.ShapeDtypeStruct(w.shape, F32) for w in ws]
    res = pl.pallas_call(body, name="adam_replicated", out_shape=shp * 4,
                         scratch_shapes=[pltpu.VMEM((REPL_TOTAL, D), F32)], compiler_params=_cparams(),
                         )(g8, *ws, *ms, *vs)
    return [dict(zip(names, res[k * np_:(k + 1) * np_])) for k in range(4)]


_WEIGHTS = ("attn_pre_norm", "w_in", "hgrn_lb", "hgrn_gnorm", "w_branch_a", "rwkv_mu", "rwkv_w0", "rwkv_w2",
            "rwkv_a0", "rwkv_a2", "rwkv_g2", "rwkv_k_k", "rwkv_k_a", "rwkv_r_k", "rwkv_ln_w", "rwkv_ln_b",
            "w_branch_b", "w_out", "attn_post_norm", "ffn_pre_norm", "w_up", "conv_w", "conv_b", "w_down",
            "ffn_post_norm")
_BIG = ("w_in", "w_up", "w_down", "w_branch_a", "w_branch_b", "w_out")


def _stages():
    one = [D]
    hw = HG_K * HG_PER_STEP
    rw = LANES * RW_PAIRS_PER_STEP
    return dict(
        pre1=_Stage("pre1", _f_pre1, 1, 256, [False], [one], [0], [], [one], [BF]),
        pre1_res=_Stage("pre1", _f_pre1_residual, 1, 256, [False], [one], [0], [], [one, one], [BF, F32]),
        mixers=_Stage("mixers", _f_mixers, 1, 2 * RW_CHUNK, [False] * 13, [[D] * 7 + [LANES, LANES]], [0],
                      [(hw, HG_K), (1, RW_COLS), (rw, LANES)], [one, one], [BF, BF]),
        merge=_Stage("merge", _f_merge, 4, 512, [], [[256]] * 4, [29, 33, 0, 0], [], [[256]], [BF]),
        post1=_Stage("post1", _f_post1, 1, 256, [False, False], [one, one], [0, 0], [], [one, one], [F32, BF]),
        conv=_Stage("conv", _f_conv, 1, 128, [False, False], [[DFF, DFF]], [0], [(1, 2 * DFF), (1, 2 * DFF)],
                    [[DFF]], [BF]),
    )


def _cols_to_blocks(w, per):
    return w.reshape(w.shape[0], N_DEV, per).transpose(1, 0, 2)


def _blocks_to_cols(g):
    return g.transpose(1, 0, 2).reshape(g.shape[1], N_DEV * g.shape[2])


def kernel(x, attn_pre_norm, w_in, hgrn_lb, hgrn_gnorm, w_branch_a, rwkv_mu, rwkv_w0, rwkv_w2, rwkv_a0, rwkv_a2, rwkv_g2, rwkv_k_k, rwkv_k_a, rwkv_r_k, rwkv_ln_w, rwkv_ln_b, w_branch_b, w_out, attn_post_norm, ffn_pre_norm, w_up, conv_w, conv_b, w_down, ffn_post_norm, loss_target, m_attn_pre_norm, m_w_in, m_hgrn_lb, m_hgrn_gnorm, m_w_branch_a, m_rwkv_mu, m_rwkv_w0, m_rwkv_w2, m_rwkv_a0, m_rwkv_a2, m_rwkv_g2, m_rwkv_k_k, m_rwkv_k_a, m_rwkv_r_k, m_rwkv_ln_w, m_rwkv_ln_b, m_w_branch_b, m_w_out, m_attn_post_norm, m_ffn_pre_norm, m_w_up, m_conv_w, m_conv_b, m_w_down, m_ffn_post_norm, v_attn_pre_norm, v_w_in, v_hgrn_lb, v_hgrn_gnorm, v_w_branch_a, v_rwkv_mu, v_rwkv_w0, v_rwkv_w2, v_rwkv_a0, v_rwkv_a2, v_rwkv_g2, v_rwkv_k_k, v_rwkv_k_a, v_rwkv_r_k, v_rwkv_ln_w, v_rwkv_ln_b, v_w_branch_b, v_w_out, v_attn_post_norm, v_ffn_pre_norm, v_w_up, v_conv_w, v_conv_b, v_w_down, v_ffn_post_norm):
    w = dict(attn_pre_norm=attn_pre_norm, w_in=w_in, hgrn_lb=hgrn_lb, hgrn_gnorm=hgrn_gnorm, w_branch_a=w_branch_a, rwkv_mu=rwkv_mu, rwkv_w0=rwkv_w0, rwkv_w2=rwkv_w2, rwkv_a0=rwkv_a0, rwkv_a2=rwkv_a2, rwkv_g2=rwkv_g2, rwkv_k_k=rwkv_k_k, rwkv_k_a=rwkv_k_a, rwkv_r_k=rwkv_r_k, rwkv_ln_w=rwkv_ln_w, rwkv_ln_b=rwkv_ln_b, w_branch_b=w_branch_b, w_out=w_out, attn_post_norm=attn_post_norm, ffn_pre_norm=ffn_pre_norm, w_up=w_up, conv_w=conv_w, conv_b=conv_b, w_down=w_down, ffn_post_norm=ffn_post_norm)
    mo = dict(attn_pre_norm=m_attn_pre_norm, w_in=m_w_in, hgrn_lb=m_hgrn_lb, hgrn_gnorm=m_hgrn_gnorm, w_branch_a=m_w_branch_a, rwkv_mu=m_rwkv_mu, rwkv_w0=m_rwkv_w0, rwkv_w2=m_rwkv_w2, rwkv_a0=m_rwkv_a0, rwkv_a2=m_rwkv_a2, rwkv_g2=m_rwkv_g2, rwkv_k_k=m_rwkv_k_k, rwkv_k_a=m_rwkv_k_a, rwkv_r_k=m_rwkv_r_k, rwkv_ln_w=m_rwkv_ln_w, rwkv_ln_b=m_rwkv_ln_b, w_branch_b=m_w_branch_b, w_out=m_w_out, attn_post_norm=m_attn_post_norm, ffn_pre_norm=m_ffn_pre_norm, w_up=m_w_up, conv_w=m_conv_w, conv_b=m_conv_b, w_down=m_w_down, ffn_post_norm=m_ffn_post_norm)
    vo = dict(attn_pre_norm=v_attn_pre_norm, w_in=v_w_in, hgrn_lb=v_hgrn_lb, hgrn_gnorm=v_hgrn_gnorm, w_branch_a=v_w_branch_a, rwkv_mu=v_rwkv_mu, rwkv_w0=v_rwkv_w0, rwkv_w2=v_rwkv_w2, rwkv_a0=v_rwkv_a0, rwkv_a2=v_rwkv_a2, rwkv_g2=v_rwkv_g2, rwkv_k_k=v_rwkv_k_k, rwkv_k_a=v_rwkv_k_a, rwkv_r_k=v_rwkv_r_k, rwkv_ln_w=v_rwkv_ln_w, rwkv_ln_b=v_rwkv_ln_b, w_branch_b=v_w_branch_b, w_out=v_w_out, attn_post_norm=v_attn_post_norm, ffn_pre_norm=v_ffn_pre_norm, w_up=v_w_up, conv_w=v_conv_w, conv_b=v_conv_b, w_down=v_w_down, ffn_post_norm=v_ffn_post_norm)

    t = x.shape[1]
    x2 = x.reshape(t, D)
    tgt = loss_target.reshape(t, D)
    st = _stages()

    me = 4 * lax.axis_index("x") + 2 * lax.axis_index("y") + lax.axis_index("c")
    small = jnp.concatenate([rwkv_w2[0], rwkv_a2[0], rwkv_g2[0]], axis=0).astype(BF)
    g_in, g_small = _all_gather("gather_weights", [w_in[0].T.astype(BF), small])
    fw_in_t = g_in.reshape(IN_COLS, D)
    z64 = jnp.zeros((64, D), BF)
    w2p = jnp.concatenate([_blocks_to_cols(g_small[:, 0:64]), z64], axis=0)
    a2p = jnp.concatenate([z64, _blocks_to_cols(g_small[:, 64:128])], axis=0)
    g2f = _blocks_to_cols(g_small[:, 128:256])
    conv_bits = jnp.pad(lax.bitcast_convert_type(conv_w[0], BF).reshape(3, 2 * 704), ((0, 29), (0, 0)))
    late = [w_up[0].T.astype(BF)] + [w[k][0].astype(BF) for k in _BIG[2:]] + [conv_bits]
    late_gather = _Exchange("gather2", late)
    r_k = rwkv_r_k.reshape(1, D)

    (xn,), _ = _stage_fwd(st["pre1"], t, [attn_pre_norm], [x2])
    z = _mm("in_proj", xn, fw_in_t, "nt", F32, tm=512, tn=4736, b_outer=True)
    mix_par = [hgrn_lb, hgrn_gnorm, rwkv_mu, rwkv_w0, w2p, rwkv_a0, a2p, g2f, rwkv_k_k, rwkv_k_a,
               rwkv_ln_w, rwkv_ln_b, r_k]
    mix_in = [z]
    (o_a, o_b), mix_saved = _stage_fwd(st["mixers"], t, mix_par, mix_in, hook=late_gather)
    gl = [lax.dynamic_update_slice(g, own[None], (me, 0, 0)) for g, own in zip(late_gather.results, late)]
    fw_up_t = gl[0].reshape(2 * DFF, D)
    fw_down = gl[1].reshape(DFF, D)
    fw_a, fw_b, fw_out = (g.reshape(D, D) for g in gl[2:5])
    conv_full = _blocks_to_cols(lax.bitcast_convert_type(gl[5][:, :3].reshape(N_DEV, 3, 704, 2), F32))
    y_a = _mm("branch_a", o_a, fw_a, "nn")
    y_b = _mm("branch_b", o_b, fw_b, "nn")
    (merged,), _ = _stage_fwd(st["merge"], t, [], [z, z, y_a, y_b])
    mix = _mm("out_proj", merged, fw_out, "nn")
    (h1, xn2), _ = _stage_fwd(st["post1"], t, [attn_post_norm, ffn_pre_norm], [x2, mix])
    hu = _mm("up_proj", xn2, fw_up_t, "nt", F32, tm=1024, tn=1408)
    conv_par = [conv_full, conv_b]
    (act,), conv_saved = _stage_fwd(st["conv"], t, conv_par, [hu])
    ff = _mm("down_proj", act, fw_down, "nn")

    loss_acc, d_ffn_post, dh1, dff = _loss_stage(t, ffn_post_norm, h1, ff, tgt)
    dact = _mm("d_act", dff, fw_down, "nt", F32, tm=1024, tn=1408)
    dw_down = _mm("dw_down", act, dff, "tn", BF, tm=1408, tn=512)
    (dcw, dcb), (dhu,) = _stage_bwd(st["conv"], t, conv_par, [hu], conv_saved, [[dact]], [BF])
    dxn2 = _mm("d_xn2", dhu, fw_up_t, "nn", F32, tm=1024, tn=256)
    dw_up_t = _mm("dw_up", dhu, xn2, "tn", BF, tm=1408, tn=1024)
    (d_post, d_pre2), (dx_a, dmix) = _stage_bwd(st["post1"], t, [attn_post_norm, ffn_pre_norm], [x2, mix], [],
                                                 [[dh1], [dxn2]], [F32, BF])
    dmerged = _mm("d_merged", dmix, fw_out, "nt")
    dw_out = _mm("dw_out", merged, dmix, "tn", BF)
    _, (dga, dgb, dy_a, dy_b) = _stage_bwd(st["merge"], t, [], [z, z, y_a, y_b], [], [[dmerged]], [BF, BF, BF, BF])
    do_a = _mm("d_oa", dy_a, fw_a, "nt")
    dw_a = _mm("dw_a", o_a, dy_a, "tn", BF)
    do_b = _mm("d_ob", dy_b, fw_b, "nt")
    dw_b = _mm("dw_b", o_b, dy_b, "tn", BF)
    early = [dw_up_t.reshape(N_DEV, 704, D), dw_down.reshape(N_DEV, 352, D), dw_a.reshape(N_DEV, 128, D),
             dw_b.reshape(N_DEV, 128, D), dw_out.reshape(N_DEV, 128, D), _cols_to_blocks(dcw.astype(BF), 704)]
    early_scatter = _Exchange("scatter", early)
    mix_dp, dz_hr = _stage_bwd(st["mixers"], t, mix_par, mix_in, mix_saved, [[do_a], [do_b]], [BF],
                               hook=early_scatter)
    d_lb, d_gn, d_mu, d_w0, d_w2p, d_a0, d_a2p, d_g2, d_kk, d_ka, d_lnw, d_lnb, d_rk = mix_dp
    dz = jnp.concatenate(dz_hr + [dga, dgb], axis=1)
    dw_in_t = _mm("dw_in", dz, xn, "tn", BF, tm=256, tn=1024)

    ax, ay, ac = lax.axis_index("x"), lax.axis_index("y"), lax.axis_index("c")
    idx4 = jnp.stack([4 * cx + 2 * cy + ac for cx, cy in ((ax, ay), (1 - ax, ay), (ax, 1 - ay), (1 - ax, 1 - ay))])
    idx4 = idx4.astype(jnp.int32)
    idx_me, idx_0 = idx4[0:1], jnp.zeros((1,), jnp.int32)
    d_small = jnp.concatenate([d_w2p[:64], d_a2p[64:], d_g2], axis=0).astype(BF)
    g8s = [dw_in_t.reshape(N_DEV, 1184, D), _cols_to_blocks(d_small, LANES)]
    recv4s = _reduce_pair(g8s)
    sums = [_pair_sum("pair_sum_" + n, idx4, g, r) for n, g, r in zip(("w_in", "small"), g8s, recv4s)]
    swap_ssem, swap_rsem, swap_srcs, swap_lands, token = _chip_swap_start([s[1] for s in sums])
    dxn = _mm("d_xn", dz, fw_in_t, "nn", F32, tm=512, tn=512, b_outer=True, token=token)
    (d_pre1,), (dx,) = _stage_bwd(st["pre1_res"], t, [attn_pre_norm], [x2], [], [[dxn], [dx_a]], [F32])
    grad_x = dx.reshape(x.shape)
    loss = lax.psum(loss_acc[0, 0], ("x", "y", "c"))

    rg = dict(attn_pre_norm=d_pre1, hgrn_lb=d_lb, hgrn_gnorm=d_gn, rwkv_mu=d_mu, rwkv_w0=d_w0, rwkv_a0=d_a0,
              rwkv_k_k=d_kk, rwkv_k_a=d_ka, rwkv_r_k=d_rk, rwkv_ln_w=d_lnw, rwkv_ln_b=d_lnb, attn_post_norm=d_post,
              ffn_pre_norm=d_pre2, conv_b=dcb, ffn_post_norm=d_ffn_post)
    (g8,) = _all_gather("gather_small_grads", [_pack_replicated(rg)])
    rnames = [n for n, _ in REPL]
    flat = lambda src: [src[n].reshape(1, D) if n == "rwkv_r_k" else src[n] for n in rnames]
    rp_out = _adam_replicated(g8, flat(w), flat(mo), flat(vo))
    recv3s = _chip_swap_wait(swap_ssem, swap_rsem, swap_srcs, swap_lands, rp_out[0]["attn_pre_norm"])
    for kind in range(4):
        rp_out[kind]["rwkv_r_k"] = rp_out[kind]["rwkv_r_k"].reshape(rwkv_r_k.shape)

    def small_of(src):
        return jnp.concatenate([src["rwkv_w2"][0], src["rwkv_a2"][0], src["rwkv_g2"][0]], axis=0)

    sh_out = [dict() for _ in range(4)]
    g_in = _sum_partials("sum_w_in", idx_0, sums[0][0][None], recv3s[0]).T
    res = _adam_sharded("adam_w_in", idx_0, g_in[None], None, *[src["w_in"][0] for src in (w, mo, vo)])
    res_s = _adam_sharded("adam_small", idx_0, sums[1][0][None], recv3s[1], *[small_of(src) for src in (w, mo, vo)])
    for kind in range(4):
        sh_out[kind]["w_in"] = res[kind][None]
        sh_out[kind]["rwkv_w2"] = res_s[kind][0:64][None]
        sh_out[kind]["rwkv_a2"] = res_s[kind][64:128][None]
        sh_out[kind]["rwkv_g2"] = res_s[kind][128:256][None]
    for n, own, recv in zip(_BIG[1:] + ("conv_w",), early, early_scatter.results):
        if n == "w_up":
            g_up = _sum_partials("sum_w_up", idx_me, own, recv).T
            res = _adam_sharded("adam_" + n, idx_0, g_up[None], None, *[src[n][0] for src in (w, mo, vo)])
        else:
            res = _adam_sharded("adam_" + n, idx_me, own, recv, *[src[n][0] for src in (w, mo, vo)])
        for kind in range(4):
            sh_out[kind][n] = res[kind][None]

    outs = [loss, grad_x]
    for kind in range(4):
        for name in _WEIGHTS:
            outs.append(sh_out[kind][name] if name in sh_out[kind] else rp_out[kind][name])
    return tuple(outs)
```

```python
import functools

import jax
import jax.numpy as jnp
from jax import lax
from jax.experimental import pallas as pl
from jax.experimental.pallas import tpu as pltpu

F32 = jnp.float32
BF = jnp.bfloat16
MESH = pl.DeviceIdType.MESH

D = 1024
HG_HEADS = 8
HG_K = 128
HG_CHUNK = 32
HG_SCALE = HG_K ** -0.5
HG_PER_STEP = 8
RW_HEADS = 16
RW_N = 64
RW_CHUNK = 64
RW_PAIRS_PER_STEP = 8
DFF = 2816
IN_COLS = 9472
RW_COLS = 3328
EPS = 1e-6
GN_EPS = 1e-5 * RW_N
ADAM_LR = 0.001
ADAM_B1 = 0.9
ADAM_B2 = 0.999
ADAM_EPS = 1e-08
ADAM_WD = 0.01
ADAM_STEP = 10
N_DEV = 8
LANES = 128
VMEM_LIMIT = 56 * 1024 * 1024
TILE_BYTES = 1280 * 1024

REPL = (("attn_pre_norm", 1024), ("hgrn_lb", 1024), ("hgrn_gnorm", 1024), ("rwkv_mu", 3328), ("rwkv_w0", 1024),
        ("rwkv_a0", 1024), ("rwkv_k_k", 1024), ("rwkv_k_a", 1024), ("rwkv_r_k", 1024), ("rwkv_ln_w", 1024),
        ("rwkv_ln_b", 1024), ("attn_post_norm", 1024), ("ffn_pre_norm", 1024), ("conv_b", 5632), ("ffn_post_norm", 1024))
REPL_ROWS = {"hgrn_lb": 2}
REPL_TOTAL = 32


def _cparams(sem=None, **kw):
    return pltpu.CompilerParams(dimension_semantics=sem, vmem_limit_bytes=VMEM_LIMIT, **kw)


_DN = {"nn": ((1,), (0,)), "nt": ((1,), (1,)), "tn": ((0,), (0,))}


def _raw_dot(a, b, mode):
    return lax.dot_general(a.astype(BF), b.astype(BF), (_DN[mode], ((), ())), preferred_element_type=F32)


@functools.partial(jax.custom_vjp, nondiff_argnums=(2,))
def _dot(a, b, mode):
    return _raw_dot(a, b, mode)


def _dot_fwd(a, b, mode):
    return _raw_dot(a, b, mode), (a, b)


def _dot_bwd(mode, res, g):
    a, b = res
    if mode == "nn":
        return _dot(g, b, "nt"), _dot(a, g, "tn")
    if mode == "nt":
        return _dot(g, b, "nn"), _dot(g, a, "tn")
    return _dot(b, g, "nt"), _dot(a, g, "nn")


_dot.defvjp(_dot_fwd, _dot_bwd)


def _bf_pieces(x, n):
    out, r = [], x
    for i in range(n):
        p = r.astype(BF)
        out.append(p)
        if i + 1 < n:
            r = r - p.astype(F32)
    return out


def _raw_split_dot(x, e, mode, n, x_left):
    eb = e.astype(BF)
    acc = None
    for p in _bf_pieces(x, n):
        ops = (p, eb) if x_left else (eb, p)
        t = lax.dot_general(*ops, (_DN[mode], ((), ())), preferred_element_type=F32)
        acc = t if acc is None else acc + t
    return acc


def _raw_headsum(x):
    t = x.shape[0]
    i = lax.broadcasted_iota(jnp.int32, (LANES, LANES), 0)
    j = lax.broadcasted_iota(jnp.int32, (LANES, LANES), 1)
    same = jnp.where((i >= RW_N) == (j >= RW_N), 1.0, 0.0).astype(F32)
    groups = x.shape[1] // LANES
    rows = jnp.concatenate([x[:, q * LANES:(q + 1) * LANES] for q in range(groups)], axis=0)
    s = _raw_split_dot(rows, same, "nn", 2, True)
    return jnp.concatenate([s[q * t:(q + 1) * t] for q in range(groups)], axis=1)


@jax.custom_vjp
def _headsum(x):
    return _raw_headsum(x)


def _headsum_fwd(x):
    return _raw_headsum(x), None


def _headsum_bwd(_, g):
    return (_raw_headsum(g),)


_headsum.defvjp(_headsum_fwd, _headsum_bwd)


@functools.partial(jax.custom_vjp, nondiff_argnums=(2,))
def _tdot(tri, x, n):
    return _raw_split_dot(x, tri, "nn", n, False)


def _tdot_fwd(tri, x, n):
    return _raw_split_dot(x, tri, "nn", n, False), tri


def _tdot_bwd(n, tri, g):
    return jnp.zeros_like(tri), _raw_split_dot(g, tri, "tn", n, False)


_tdot.defvjp(_tdot_fwd, _tdot_bwd)


def _row(x, i):
    r = lax.broadcasted_iota(jnp.int32, x.shape, 0)
    return jnp.sum(jnp.where(r == i, x, 0.0), axis=0, keepdims=True)


def _shift_down(x, prev):
    t = x.shape[0]

    @jax.custom_vjp
    def sh(x, prev):
        r = lax.broadcasted_iota(jnp.int32, x.shape, 0)
        return jnp.where(r == 0, prev, pltpu.roll(x, 1, 0))

    def fwd(x, prev):
        return sh(x, prev), None

    def bwd(_, g):
        r = lax.broadcasted_iota(jnp.int32, g.shape, 0)
        dx = jnp.where(r == t - 1, 0.0, pltpu.roll(g, t - 1, 0))
        return dx, jnp.sum(jnp.where(r == 0, g, 0.0), axis=0, keepdims=True)

    sh.defvjp(fwd, bwd)
    return sh(x, prev)


def _sigmoid(x):
    return jax.nn.sigmoid(x)


def _silu(x):
    return x * jax.nn.sigmoid(x)


def _softplus(x):
    return jnp.maximum(x, 0.0) + jnp.log(1.0 + jnp.exp(-jnp.abs(x)))


def _rms(x, g):
    return (x * lax.rsqrt(jnp.mean(x * x, axis=-1, keepdims=True) + EPS)) * g


def _tril(c):
    r = lax.broadcasted_iota(jnp.int32, (c, c), 0)
    cc = lax.broadcasted_iota(jnp.int32, (c, c), 1)
    return cc <= r


def _f_pre1(ps, xs, cs):
    return [_rms(xs[0], ps[0])], []


def _f_pre1_residual(ps, xs, cs):
    return [_rms(xs[0], ps[0]), xs[0]], []


def _f_hgrn(ps, xs, cs):
    lbraw, gn = ps
    hq, hf, hi, hg = xs
    hd = range(HG_PER_STEP)
    st = [cs[0][p * HG_K:(p + 1) * HG_K] for p in hd]
    l0, l1 = _row(lbraw, 0), _row(lbraw, 1)
    m = jnp.maximum(l0, l1)
    e0, e1 = jnp.exp(l0 - m), jnp.exp(l1 - m)
    lb = e0 / (e0 + e1)
    q = _silu(hq) * HG_SCALE
    f = lb + (1.0 - lb) * _sigmoid(hf)
    kh = 1.0 - f
    gl = jnp.log(f)
    c = HG_CHUNK
    low = _tril(c)
    tri = jnp.where(low, 1.0, 0.0).astype(F32)
    outs = []
    for i in range(hq.shape[0] // c):
        rows = slice(i * c, (i + 1) * c)
        b = _tdot(tri, gl[rows], 3)
        bref = _row(b, c // 2 - 1)
        blast = _row(b, c - 1)
        qi = q[rows] * jnp.exp(b - bref)
        ki = kh[rows] * jnp.exp(bref - b)
        qd = q[rows] * jnp.exp(b)
        kd = kh[rows] * jnp.exp(blast - b)
        dec = jnp.exp(blast)
        sl = [slice(p * HG_K, (p + 1) * HG_K) for p in hd]
        sc = [jnp.where(low, _dot(qi[:, sl[p]], ki[:, sl[p]], "nt"), 0.0) for p in hd]
        o = [_dot(sc[p], hi[rows, sl[p]], "nn") + _dot(qd[:, sl[p]], st[p], "nt") for p in hd]
        u = [_dot(hi[rows, sl[p]], kd[:, sl[p]], "tn") for p in hd]
        st = [dec[:, sl[p]] * st[p] + u[p] for p in hd]
        outs.append(jnp.concatenate(o, axis=1) if len(o) > 1 else o[0])
    o = outs[0] if len(outs) == 1 else jnp.concatenate(outs, axis=0)
    on = []
    for p in hd:
        op = o[:, p * HG_K:(p + 1) * HG_K]
        on.append(op * lax.rsqrt(jnp.mean(op * op, axis=-1, keepdims=True) + EPS))
    o = jnp.concatenate(on, axis=1) if len(on) > 1 else on[0]
    o = o * gn
    return [o * _silu(hg)], [jnp.concatenate(st, axis=0) if len(st) > 1 else st[0]]


_RW_OFFS = (0, 1024, 2048, 3072, 3200, 3328)


def _f_rwpre(ps, xs, cs):
    mu, w0, w2p, a0, a2p, g2, k_k, k_a = ps
    (prev,) = cs
    t = xs[0].shape[0]
    zs = []
    for i, z in enumerate(xs):
        lo, hi = _RW_OFFS[i], _RW_OFFS[i + 1]
        zs.append(z + mu[:, lo:hi] * (_shift_down(z, prev[:, lo:hi]) - z))
    rr, kr, vr, wa, gz = zs
    w_log = -_softplus(-(w0 + _dot(jnp.tanh(wa), w2p, "nn"))) - 0.5
    lw = -jnp.exp(w_log)
    a = _sigmoid(a0 + _dot(wa, a2p, "nn"))
    g = _dot(_sigmoid(gz), g2, "nn")
    kkr = kr * k_k
    kk = kkr / jnp.maximum(jnp.sqrt(_headsum(kkr * kkr)), 1e-12)
    k2 = kr * (1.0 + (a - 1.0) * k_a)
    newprev = jnp.concatenate([_row(z, t - 1) for z in xs], axis=1)
    return [rr, lw, k2, vr, -kk, kk * a, g], [newprev]


def _raw_inverses(ls):
    n = ls[0].shape[0]
    r = lax.broadcasted_iota(jnp.int32, (n, n), 0)
    c = lax.broadcasted_iota(jnp.int32, (n, n), 1)
    eye = jnp.where(r == c, 1.0, 0.0).astype(F32)
    tinv = [eye + l for l in ls]
    pw = ls
    for _ in range(5):
        pw = [_raw_dot(p, p, "nn") for p in pw]
        tinv = [t + _raw_dot(t, p, "nn") for t, p in zip(tinv, pw)]
    return tinv


@jax.custom_vjp
def _unit_lower_inverses(ls):
    return _raw_inverses(ls)


def _inverses_fwd(ls):
    tinv = _raw_inverses(ls)
    return tinv, tinv


def _inverses_bwd(tinv, gs):
    return ([_raw_dot(_raw_dot(t, g, "tn"), t, "nt") for t, g in zip(tinv, gs)],)


_unit_lower_inverses.defvjp(_inverses_fwd, _inverses_bwd)


def _f_rwscan(ps, xs, cs):
    state = cs[0]
    ys = []
    for i in range(xs[0].shape[0] // RW_CHUNK):
        y, state = _rwkv_chunk([x[i * RW_CHUNK:(i + 1) * RW_CHUNK] for x in xs], state)
        ys.append(y)
    return [ys[0] if len(ys) == 1 else jnp.concatenate(ys, axis=0)], [state]


def _rwkv_chunk(xs, state):
    npair = RW_PAIRS_PER_STEP
    pr = range(npair)
    r, lw, k, v, av, bv = [[x[:, p * LANES:(p + 1) * LANES] for p in pr] for x in xs]
    sv = [state[p * LANES:(p + 1) * LANES] for p in pr]
    c = RW_CHUNK
    n = 2 * c
    tri = jnp.where(_tril(c), 1.0, 0.0).astype(F32)
    cl = [_tdot(tri, lw[p], 3) for p in pr]
    cl_last = [_row(cl[p], c - 1) for p in pr]
    lane = lax.broadcasted_iota(jnp.int32, (c, LANES), 1)
    h0 = lane < RW_N

    def stack(x):
        return jnp.concatenate([jnp.where(h0, x, 0.0), jnp.where(h0, 0.0, x)], axis=0)

    am = [stack(av[p] * jnp.exp(cl[p] - lw[p])) for p in pr]
    bm = [stack(bv[p] * jnp.exp(-cl[p])) for p in pr]
    km = [stack(k[p] * jnp.exp(-cl[p])) for p in pr]
    rm = [stack(r[p] * jnp.exp(cl[p])) for p in pr]
    vm = [stack(v[p]) for p in pr]
    rn = lax.broadcasted_iota(jnp.int32, (n, n), 0)
    cn = lax.broadcasted_iota(jnp.int32, (n, n), 1)
    blk = (rn >= c) == (cn >= c)
    strict = blk & (cn < rn)
    incl = blk & (cn <= rn)
    lab = [jnp.where(strict, _dot(am[p], bm[p], "nt"), 0.0) for p in pr]
    lak = [jnp.where(strict, _dot(am[p], km[p], "nt"), 0.0) for p in pr]
    wrb = [jnp.where(incl, _dot(rm[p], bm[p], "nt"), 0.0) for p in pr]
    wrk = [jnp.where(incl, _dot(rm[p], km[p], "nt"), 0.0) for p in pr]
    tinv = _unit_lower_inverses(lab)
    rhs = [_dot(am[p], sv[p], "nt") + _dot(lak[p], vm[p], "nn") for p in pr]
    um = [_dot(tinv[p], rhs[p], "nn") for p in pr]
    ym = [_dot(rm[p], sv[p], "nt") + _dot(wrb[p], um[p], "nn") + _dot(wrk[p], vm[p], "nn") for p in pr]
    sn = [(sv[p] + _dot(um[p], bm[p], "tn") + _dot(vm[p], km[p], "tn")) * jnp.exp(cl_last[p]) for p in pr]
    ys = [ym[p][:c] + ym[p][c:] for p in pr]
    return jnp.concatenate(ys, axis=1), jnp.concatenate(sn, axis=0)


def _f_mixers(ps, xs, cs):
    oa, st = _f_hgrn(ps[:2], xs[:4], cs[:1])
    (r, lw, k, v, av, bv, g), prev = _f_rwpre(ps[2:10], xs[4:], cs[1:2])
    y, sv = _f_rwscan([], [r, lw, k, v, av, bv], cs[2:])
    ob, _ = _f_rwpost(ps[10:], y + [r, k, v, g], [])
    return oa + ob, st + prev + sv


def _f_rwpost(ps, xs, cs):
    ln_w, ln_b, r_k = ps
    y, r, k, v, g = xs
    inv_n = 1.0 / RW_N
    yc = y - _headsum(y) * inv_n
    var = _headsum(yc * yc) * inv_n
    yn = yc * lax.rsqrt(var + GN_EPS)
    yn = yn * ln_w + ln_b
    bonus = _headsum(r * k * r_k) * v
    return [(yn + bonus) * g], []


def _f_merge(ps, xs, cs):
    ga, gb, ya, yb = xs
    return [_sigmoid(ga) * ya + _sigmoid(gb) * yb], []


def _f_post1(ps, xs, cs):
    x, mix = xs
    h1 = x + _rms(mix, ps[0])
    return [h1, _rms(h1, ps[1])], []


def _f_conv(ps, xs, cs):
    cw, cb = ps
    p1, p2 = cs
    w0, w1, w2 = _row(cw, 0), _row(cw, 1), _row(cw, 2)
    t = xs[0].shape[0]
    hc = []
    for i, x in enumerate(xs):
        sl = slice(i * DFF, (i + 1) * DFF)
        s1 = _shift_down(x, p1[:, sl])
        s2 = _shift_down(s1, p2[:, sl])
        hc.append(cb[:, sl] + w0[:, sl] * s2 + w1[:, sl] * s1 + w2[:, sl] * x)
    n1 = jnp.concatenate([_row(x, t - 1) for x in xs], axis=1)
    n2 = jnp.concatenate([_row(x, t - 2) for x in xs], axis=1)
    return [_silu(hc[0]) * hc[1]], [n1, n2]


class _Stage:
    def __init__(self, name, f, g, tm, par_per_g, in_pieces, in_offs, carry_shapes, out_pieces, out_dtypes):
        self.name, self.f, self.g, self.tm = name, f, g, tm
        self.par_per_g, self.in_pieces, self.in_offs = par_per_g, in_pieces, in_offs
        self.carry_shapes, self.out_pieces, self.out_dtypes = carry_shapes, out_pieces, out_dtypes


def _par_spec(arr, per_g, g):
    r, c = arr.shape
    if per_g:
        return pl.BlockSpec((r, c // g), lambda gi, ni: (0, gi))
    return pl.BlockSpec((r, c), lambda gi, ni: (0, 0))


def _row_spec(tm, width, off, n, rev):
    if rev:
        return pl.BlockSpec((tm, width), lambda gi, ni: (n - 1 - ni, off + gi))
    return pl.BlockSpec((tm, width), lambda gi, ni: (ni, off + gi))


def _carry_spec(shape, n, rev):
    if rev:
        return pl.BlockSpec((None, None) + shape, lambda gi, ni: (gi, n - 1 - ni, 0, 0))
    return pl.BlockSpec((None, None) + shape, lambda gi, ni: (gi, ni, 0, 0))


def _load_pieces(refs, pieces_list):
    out = []
    for ref, pieces in zip(refs, pieces_list):
        o = 0
        for w in pieces:
            out.append(ref[:, o:o + w].astype(F32))
            o += w
    return out


def _store_pieces(refs, pieces_list, vals):
    k = 0
    for ref, pieces in zip(refs, pieces_list):
        o = 0
        for w in pieces:
            ref[:, o:o + w] = vals[k].astype(ref.dtype)
            k += 1
            o += w


_ANY = pl.BlockSpec(memory_space=pl.ANY)


class _Exchange:
    def __init__(self, kind, arrs):
        self.kind, self.arrs, self.results = kind, list(arrs), None
        if kind == "scatter":
            self.out_shape = [jax.ShapeDtypeStruct((N_DEV - 1,) + a.shape[1:], a.dtype) for a in self.arrs]
        else:
            self.out_shape = [jax.ShapeDtypeStruct((N_DEV,) + a.shape, a.dtype) for a in self.arrs]
        self.nsem = (N_DEV if kind == "gather2" else N_DEV - 1) * len(self.arrs)

    def copies(self, in_refs, out_refs, ssem, rsem):
        x, y, c = lax.axis_index("x"), lax.axis_index("y"), lax.axis_index("c")
        me = 4 * x + 2 * y + c
        cps = []
        for a, (i_ref, o_ref) in enumerate(zip(in_refs, out_refs)):
            for j in range(1, N_DEV):
                px = 1 - x if j & 4 else x
                py = 1 - y if j & 2 else y
                pc = 1 - c if j & 1 else c
                if self.kind == "gather":
                    src, dst = i_ref, o_ref.at[me]
                else:
                    src, dst = i_ref.at[4 * px + 2 * py + pc], o_ref.at[j - 1]
                s = (N_DEV - 1) * a + j - 1
                cps.append(pltpu.make_async_remote_copy(src_ref=src, dst_ref=dst, send_sem=ssem.at[s],
                                                        recv_sem=rsem.at[s], device_id=(px, py, pc),
                                                        device_id_type=MESH))
        return cps

    def run(self, step, total, in_refs, out_refs, ssem, rsem):
        if self.kind == "gather2":
            return self.run_two_level(step, total, in_refs, out_refs, ssem, rsem)

        @pl.when(step == 0)
        def _():
            for cp in self.copies(in_refs, out_refs, ssem, rsem):
                cp.start()

        @pl.when(step == total - 1)
        def _():
            for cp in self.copies(in_refs, out_refs, ssem, rsem):
                cp.wait()

    def run_two_level(self, step, total, in_refs, out_refs, ssem, rsem):
        x, y, c = lax.axis_index("x"), lax.axis_index("y"), lax.axis_index("c")
        sibling, xn, yn = (x, y, 1 - c), (1 - x, y, c), (x, 1 - y, c)
        arrs = range(len(in_refs))
        ns = N_DEV

        def num(px, py, pc):
            return 4 * px + 2 * py + pc

        def copy(a, k, to, src, dst):
            return pltpu.make_async_remote_copy(src_ref=src, dst_ref=dst, send_sem=ssem.at[ns * a + k],
                                                recv_sem=rsem.at[ns * a + k], device_id=to, device_id_type=MESH)

        def blk(a, b):
            return out_refs[a].at[b]

        def half(a, b, second):
            h = self.arrs[a].shape[0] // 2
            return out_refs[a].at[b, pl.ds(h if second else 0, h)]

        bx, by, bd = num(1 - x, y, c), num(x, 1 - y, c), num(1 - x, 1 - y, c)

        def firsts(a):
            own = blk(a, num(x, y, c))
            return [copy(a, 0, sibling, in_refs[a], own), copy(a, 1, xn, in_refs[a], own),
                    copy(a, 2, yn, in_refs[a], own)]

        def seconds(a):
            return [copy(a, 3, yn, half(a, bx, False), half(a, bx, False)), copy(a, 5, sibling, blk(a, bx), blk(a, bx)),
                    copy(a, 4, xn, half(a, by, True), half(a, by, True)), copy(a, 6, sibling, blk(a, by), blk(a, by))]

        def third(a):
            return copy(a, 7, sibling, blk(a, bd), blk(a, bd))

        @pl.when(step == 0)
        def _():
            for a in arrs:
                for cp in firsts(a):
                    cp.start()

        @pl.when(step == total // 2)
        def _():
            for a in arrs:
                copy(a, 1, xn, blk(a, bx), blk(a, bx)).wait_recv()
                copy(a, 2, yn, blk(a, by), blk(a, by)).wait_recv()
                for cp in seconds(a):
                    cp.start()

        @pl.when(step == (4 * total) // 5)
        def _():
            for a in arrs:
                copy(a, 3, yn, half(a, bd, False), half(a, bd, False)).wait_recv()
                copy(a, 4, xn, half(a, bd, True), half(a, bd, True)).wait_recv()
                third(a).start()

        @pl.when(step == total - 1)
        def _():
            for a in arrs:
                for k, b in ((0, num(x, y, 1 - c)), (5, num(1 - x, y, 1 - c)), (6, num(x, 1 - y, 1 - c)),
                             (7, num(1 - x, 1 - y, 1 - c))):
                    copy(a, k, sibling, blk(a, b), blk(a, b)).wait_recv()
                for cp in firsts(a) + seconds(a) + [third(a)]:
                    cp.wait_send()


def _hook_specs(hook):
    if hook is None:
        return [], [], [], []
    na = len(hook.arrs)
    sems = [pltpu.SemaphoreType.DMA((hook.nsem,)), pltpu.SemaphoreType.DMA((hook.nsem,))]
    return [_ANY] * na, [_ANY] * na, hook.out_shape, sems


def _stage_fwd(st, t, params, inputs, hook=None):
    g, tm = st.g, min(st.tm, t)
    n = t // tm
    npar, nin, ncar, nout = len(params), len(inputs), len(st.carry_shapes), len(st.out_pieces)
    h_in, h_out, h_shape, h_sems = _hook_specs(hook)
    nh = len(h_in)

    def body(*refs):
        p_refs = refs[:npar]
        x_refs = refs[npar:npar + nin]
        hi_refs = refs[npar + nin:npar + nin + nh]
        o = npar + nin + nh
        o_refs = refs[o:o + nout]
        s_refs = refs[o + nout:o + nout + ncar]
        ho_refs = refs[o + nout + ncar:o + nout + ncar + nh]
        c_scr = refs[o + nout + ncar + nh:o + nout + ncar + nh + ncar]
        gi, ni = pl.program_id(0), pl.program_id(1)
        if hook is not None:
            step = gi * n + ni
            hook.run(step, g * n, hi_refs, ho_refs, *refs[-2:])

        @pl.when(ni == 0)
        def _():
            for c in c_scr:
                c[...] = jnp.zeros(c.shape, F32)

        ps = [r[...].astype(F32) for r in p_refs]
        xs = _load_pieces(x_refs, st.in_pieces)
        cs = [c[...] for c in c_scr]
        for s, c in zip(s_refs, cs):
            s[...] = c
        outs, ncs = st.f(ps, xs, cs)
        _store_pieces(o_refs, st.out_pieces, outs)
        for c, v in zip(c_scr, ncs):
            c[...] = v

    in_specs = [_par_spec(p, pg, g) for p, pg in zip(params, st.par_per_g)]
    in_specs += [_row_spec(tm, sum(pc), off, n, False) for pc, off in zip(st.in_pieces, st.in_offs)]
    out_specs = [_row_spec(tm, sum(pc), 0, n, False) for pc in st.out_pieces]
    out_specs += [_carry_spec(s, n, False) for s in st.carry_shapes]
    out_shape = [jax.ShapeDtypeStruct((t, g * sum(pc)), dt) for pc, dt in zip(st.out_pieces, st.out_dtypes)]
    out_shape += [jax.ShapeDtypeStruct((g, n) + s, F32) for s in st.carry_shapes]
    res = pl.pallas_call(
        body, name=st.name + "_fwd", grid=(g, n), in_specs=in_specs + h_in, out_specs=out_specs + h_out,
        out_shape=out_shape + h_shape,
        scratch_shapes=[pltpu.VMEM(s, F32) for s in st.carry_shapes] + h_sems,
        compiler_params=_cparams(("arbitrary", "arbitrary")),
    )(*params, *inputs, *(hook.arrs if hook else []))
    if hook is not None:
        hook.results = list(res[nout + ncar:])
    return list(res[:nout]), list(res[nout:nout + ncar])


def _stage_bwd(st, t, params, inputs, saved, douts, dx_dtypes, hook=None):
    g, tm = st.g, min(st.tm, t)
    n = t // tm
    npar, nin, ncar = len(params), len(inputs), len(st.carry_shapes)
    flat_d = [d for ds in douts for d in ds]
    nd = len(flat_d)
    dx_idx = [i for i, dt in enumerate(dx_dtypes) if dt is not None]
    h_in, h_out, h_shape, h_sems = _hook_specs(hook)
    nh = len(h_in)

    def body(*refs):
        p_refs = refs[:npar]
        x_refs = refs[npar:npar + nin]
        s_refs = refs[npar + nin:npar + nin + ncar]
        d_refs = refs[npar + nin + ncar:npar + nin + ncar + nd]
        hi_refs = refs[npar + nin + ncar + nd:npar + nin + ncar + nd + nh]
        o = npar + nin + ncar + nd + nh
        dp_refs = refs[o:o + npar]
        dx_refs = refs[o + npar:o + npar + len(dx_idx)]
        ho_refs = refs[o + npar + len(dx_idx):o + npar + len(dx_idx) + nh]
        dc_scr = refs[o + npar + len(dx_idx) + nh:o + npar + len(dx_idx) + nh + ncar]
        gi, ni = pl.program_id(0), pl.program_id(1)
        if hook is not None:
            step = gi * n + ni
            hook.run(step, g * n, hi_refs, ho_refs, *refs[-2:])

        @pl.when(ni == 0)
        def _():
            for c in dc_scr:
                c[...] = jnp.zeros(c.shape, F32)

        ps = [r[...].astype(F32) for r in p_refs]
        xs = _load_pieces(x_refs, st.in_pieces)
        cs = [s[...] for s in s_refs]
        dys = []
        k = 0
        for ds, pieces in zip(douts, st.out_pieces):
            acc = _load_pieces([d_refs[k]], [pieces])
            for j in range(1, len(ds)):
                more = _load_pieces([d_refs[k + j]], [pieces])
                acc = [a + b for a, b in zip(acc, more)]
            dys += acc
            k += len(ds)
        _, vjp = jax.vjp(st.f, ps, xs, cs)
        dps, dxs, dcs = vjp((dys, [c[...] for c in dc_scr]))
        k = 0
        per_in = []
        for pieces in st.in_pieces:
            per_in.append(dxs[k:k + len(pieces)])
            k += len(pieces)
        for ref, i in zip(dx_refs, dx_idx):
            _store_pieces([ref], [st.in_pieces[i]], per_in[i])
        for c, v in zip(dc_scr, dcs):
            c[...] = v
        for ref, dp, pg in zip(dp_refs, dps, st.par_per_g):
            first = (ni == 0) if pg else ((ni == 0) & (gi == 0))

            @pl.when(first)
            def _():
                ref[...] = jnp.zeros(ref.shape, F32)

            ref[...] += dp

    in_specs = [_par_spec(p, pg, g) for p, pg in zip(params, st.par_per_g)]
    in_specs += [_row_spec(tm, sum(pc), off, n, True) for pc, off in zip(st.in_pieces, st.in_offs)]
    in_specs += [_carry_spec(s, n, True) for s in st.carry_shapes]
    for ds, pc in zip(douts, st.out_pieces):
        in_specs += [_row_spec(tm, sum(pc), 0, n, True) for _ in ds]
    out_specs = [_par_spec(p, pg, g) for p, pg in zip(params, st.par_per_g)]
    out_specs += [_row_spec(tm, sum(st.in_pieces[i]), 0, n, True) for i in dx_idx]
    out_shape = [jax.ShapeDtypeStruct(p.shape, F32) for p in params]
    out_shape += [jax.ShapeDtypeStruct((t, g * sum(st.in_pieces[i])), dx_dtypes[i]) for i in dx_idx]
    res = pl.pallas_call(
        body, name=st.name + "_bwd", grid=(g, n), in_specs=in_specs + h_in, out_specs=out_specs + h_out,
        out_shape=out_shape + h_shape,
        scratch_shapes=[pltpu.VMEM(s, F32) for s in st.carry_shapes] + h_sems,
        compiler_params=_cparams(("arbitrary", "arbitrary")),
    )(*params, *inputs, *saved, *flat_d, *(hook.arrs if hook else []))
    if hook is not None:
        hook.results = list(res[npar + len(dx_idx):])
    return list(res[:npar]), list(res[npar:npar + len(dx_idx)])


def _pick(n, cap):
    if n <= cap:
        return n
    best = LANES
    for k in range(1, n // LANES + 1):
        if (n // LANES) % k == 0 and k * LANES <= cap:
            best = k * LANES
    return best


def _mm(name, a, b, mode, out_dtype=F32, tm=1024, tn=512, b_outer=False, token=None):
    m = a.shape[1] if mode == "tn" else a.shape[0]
    k = a.shape[0] if mode == "tn" else a.shape[1]
    n = b.shape[0] if mode == "nt" else b.shape[1]
    tm, tn = _pick(m, tm), _pick(n, tn)
    if b_outer:
        grid = (n // tn, m // tm)
        ij = lambda p, q: (q, p)
    else:
        grid = (m // tm, n // tn)
        ij = lambda p, q: (p, q)
    extra = [] if token is None else [token]

    def body(*refs):
        a_ref, b_ref, o_ref = refs[0], refs[1], refs[-1]
        o_ref[...] = _raw_dot(a_ref[...], b_ref[...], mode).astype(o_ref.dtype)

    if mode == "tn":
        a_spec = pl.BlockSpec((k, tm), lambda p, q: (0, ij(p, q)[0]))
    else:
        a_spec = pl.BlockSpec((tm, k), lambda p, q: (ij(p, q)[0], 0))
    b_mode = dict(pipeline_mode=pl.Buffered(1)) if tn == n else {}
    if mode == "nt":
        b_spec = pl.BlockSpec((tn, k), lambda p, q: (ij(p, q)[1], 0), **b_mode)
    else:
        b_spec = pl.BlockSpec((k, tn), lambda p, q: (0, ij(p, q)[1]), **b_mode)
    return pl.pallas_call(
        body, name=name, grid=grid,
        in_specs=[a_spec, b_spec] + [pl.BlockSpec(e.shape, lambda p, q: (0, 0)) for e in extra],
        out_specs=pl.BlockSpec((tm, tn), lambda p, q: ij(p, q)),
        out_shape=jax.ShapeDtypeStruct((m, n), out_dtype),
        compiler_params=_cparams(("arbitrary", "arbitrary")),
    )(a, b, *extra)


def _loss_stage(t, g_post, h1, ff, tgt):
    tm = min(256, t)
    n = t // tm

    def body(g_ref, h_ref, f_ref, t_ref, loss_ref, dg_ref, dh_ref, df_ref):
        ni = pl.program_id(0)
        target = t_ref[...]

        def lossf(g, h1, ff):
            e = h1 + _rms(ff, g) - target
            return 0.5 * jnp.sum(jnp.mean(e * e, axis=-1))

        l, (dg, dh, df) = jax.value_and_grad(lossf, argnums=(0, 1, 2))(g_ref[...], h_ref[...], f_ref[...])

        @pl.when(ni == 0)
        def _():
            loss_ref[...] = jnp.zeros(loss_ref.shape, F32)
            dg_ref[...] = jnp.zeros(dg_ref.shape, F32)

        loss_ref[...] += jnp.full(loss_ref.shape, l, F32)
        dg_ref[...] += dg
        dh_ref[...] = dh
        df_ref[...] = df.astype(df_ref.dtype)

    row = pl.BlockSpec((tm, D), lambda ni: (ni, 0))
    one = pl.BlockSpec((1, D), lambda ni: (0, 0))
    return pl.pallas_call(
        body, name="loss_head", grid=(n,), in_specs=[one, row, row, row],
        out_specs=[pl.BlockSpec((1, LANES), lambda ni: (0, 0)), one, row, row],
        out_shape=[jax.ShapeDtypeStruct((1, LANES), F32), jax.ShapeDtypeStruct((1, D), F32),
                   jax.ShapeDtypeStruct((t, D), F32), jax.ShapeDtypeStruct((t, D), BF)],
        compiler_params=_cparams(("arbitrary",)),
    )(g_post, h1, ff, tgt)


_ANY = pl.BlockSpec(memory_space=pl.ANY)


def _all_gather(name, blks):
    na = len(blks)
    ns = 8

    def body(*refs):
        x_refs, out_refs = refs[:na], refs[na:2 * na]
        send_sems, recv_sems, local_sems = refs[2 * na:]
        x, y, cc = lax.axis_index("x"), lax.axis_index("y"), lax.axis_index("c")
        sibling, xn, yn = (x, y, 1 - cc), (1 - x, y, cc), (x, 1 - y, cc)

        def num(px, py, pc):
            return 4 * px + 2 * py + pc

        def copy(a, k, to, src, dst):
            return pltpu.make_async_remote_copy(src_ref=src, dst_ref=dst, send_sem=send_sems.at[ns * a + k],
                                                recv_sem=recv_sems.at[ns * a + k], device_id=to, device_id_type=MESH)

        def halves(a, blk):
            h = blks[a].shape[0] // 2
            return out_refs[a].at[blk, pl.ds(0, h)], out_refs[a].at[blk, pl.ds(h, h)]

        mine, sends = [], []
        for a in range(na):
            o = out_refs[a]
            m = pltpu.make_async_copy(x_refs[a], o.at[num(x, y, cc)], local_sems.at[a])
            m.start()
            mine.append(m)
            own = o.at[num(x, y, cc)]
            sends.append([copy(a, 0, sibling, x_refs[a], own), copy(a, 1, xn, x_refs[a], own),
                          copy(a, 2, yn, x_refs[a], own)])
            for cp in sends[a]:
                cp.start()
        for a in range(na):
            o = out_refs[a]
            bx, by, bd = num(1 - x, y, cc), num(x, 1 - y, cc), num(1 - x, 1 - y, cc)
            copy(a, 1, xn, o.at[bx], o.at[bx]).wait_recv()
            more = [copy(a, 3, yn, halves(a, bx)[0], halves(a, bx)[0]), copy(a, 5, sibling, o.at[bx], o.at[bx])]
            for cp in more:
                cp.start()
            sends[a] += more
        for a in range(na):
            o = out_refs[a]
            bx, by, bd = num(1 - x, y, cc), num(x, 1 - y, cc), num(1 - x, 1 - y, cc)
            copy(a, 2, yn, o.at[by], o.at[by]).wait_recv()
            more = [copy(a, 4, xn, halves(a, by)[1], halves(a, by)[1]), copy(a, 6, sibling, o.at[by], o.at[by])]
            for cp in more:
                cp.start()
            sends[a] += more
        for a in range(na):
            o = out_refs[a]
            bd = num(1 - x, 1 - y, cc)
            copy(a, 3, yn, halves(a, bd)[0], halves(a, bd)[0]).wait_recv()
            copy(a, 4, xn, halves(a, bd)[1], halves(a, bd)[1]).wait_recv()
            fw = copy(a, 7, sibling, o.at[bd], o.at[bd])
            fw.start()
            sends[a].append(fw)
        for a in range(na):
            o = out_refs[a]
            for k, blk in ((0, num(x, y, 1 - cc)), (5, num(1 - x, y, 1 - cc)), (6, num(x, 1 - y, 1 - cc)),
                           (7, num(1 - x, 1 - y, 1 - cc))):
                copy(a, k, sibling, o.at[blk], o.at[blk]).wait_recv()
            for cp in sends[a]:
                cp.wait_send()
        for m in mine:
            m.wait()

    res = pl.pallas_call(
        body, name=name, in_specs=[_ANY] * na, out_specs=[_ANY] * na,
        out_shape=[jax.ShapeDtypeStruct((N_DEV,) + b.shape, b.dtype) for b in blks],
        scratch_shapes=[pltpu.SemaphoreType.DMA((ns * na,)), pltpu.SemaphoreType.DMA((ns * na,)),
                        pltpu.SemaphoreType.DMA((na,))],
    )(*blks)
    return list(res)


def _reduce_pair(g8s):
    na = len(g8s)

    def body(*refs):
        g_refs, recv_refs = refs[:na], refs[na:2 * na]
        ssem, rsem = refs[2 * na:]
        x, y, cc = lax.axis_index("x"), lax.axis_index("y"), lax.axis_index("c")
        chips = [(x, y), (1 - x, y), (x, 1 - y), (1 - x, 1 - y)]
        sib = (x, y, 1 - cc)
        for a in range(na):
            for k, (cx, cy) in enumerate(chips):
                pltpu.make_async_remote_copy(
                    src_ref=g_refs[a].at[4 * cx + 2 * cy + 1 - cc], dst_ref=recv_refs[a].at[k],
                    send_sem=ssem.at[a], recv_sem=rsem.at[a], device_id=sib, device_id_type=MESH).start()
        for a in range(na):
            pltpu.make_async_remote_copy(src_ref=recv_refs[a], dst_ref=recv_refs[a], send_sem=ssem.at[a],
                                         recv_sem=rsem.at[a], device_id=sib, device_id_type=MESH).wait()

    res = pl.pallas_call(
        body, name="reduce_pair", in_specs=[_ANY] * na, out_specs=[_ANY] * na,
        out_shape=[jax.ShapeDtypeStruct((4,) + g.shape[1:], g.dtype) for g in g8s],
        scratch_shapes=[pltpu.SemaphoreType.DMA((na,)), pltpu.SemaphoreType.DMA((na,))],
    )(*g8s)
    return list(res)


_HBM = pl.BlockSpec(memory_space=pltpu.HBM)
_SEM = pl.BlockSpec(memory_space=pltpu.SEMAPHORE)
_EFFECT = pltpu.SideEffectType.DATAFLOW_SIDE_EFFECTING


def _chip_swap_copies(s_refs, land_refs, ssem, rsem):
    x, y, c = lax.axis_index("x"), lax.axis_index("y"), lax.axis_index("c")
    targets = [(1 - x, y, c), (x, 1 - y, c), (1 - x, 1 - y, c)]
    return [pltpu.make_async_remote_copy(src_ref=s.at[k], dst_ref=d.at[k], send_sem=ssem.at[3 * a + k],
                                         recv_sem=rsem.at[3 * a + k], device_id=targets[k], device_id_type=MESH)
            for a, (s, d) in enumerate(zip(s_refs, land_refs)) for k in range(3)]


def _chip_swap_start(sends):
    na = len(sends)

    def body(*refs):
        cps = _chip_swap_copies(refs[:na], refs[na:2 * na], refs[2 * na], refs[2 * na + 1])
        for cp in cps:
            cp.start()
        token = refs[-1]
        token[...] = jnp.zeros(token.shape, token.dtype)

    bufs = [pltpu.HBM(s.shape, s.dtype) for s in sends]
    res = pl.pallas_call(
        body, name="chip_swap_start",
        out_shape=[pltpu.SemaphoreType.DMA((3 * na,)), pltpu.SemaphoreType.DMA((3 * na,))] + bufs + bufs
        + [jax.ShapeDtypeStruct((8, LANES), F32)],
        in_specs=[_HBM] * (2 * na), out_specs=[_SEM, _SEM] + [_HBM] * (2 * na) + [pl.BlockSpec(memory_space=pltpu.VMEM)],
        input_output_aliases={i: 2 + i for i in range(2 * na)},
        compiler_params=pltpu.CompilerParams(has_side_effects=_EFFECT),
    )(*[pltpu.with_memory_space_constraint(s, pltpu.HBM) for s in sends],
      *[pltpu.with_memory_space_constraint(lax.empty(s.shape, s.dtype), pltpu.HBM) for s in sends])
    return res[0], res[1], list(res[2:2 + na]), list(res[2 + na:2 + 2 * na]), res[-1]


def _chip_swap_wait(ssem, rsem, srcs, lands, after):
    na = len(srcs)

    def body(*refs):
        cps = _chip_swap_copies(refs[:na], refs[na:2 * na], refs[2 * na], refs[2 * na + 1])
        for cp in cps:
            cp.wait_send()
            cp.wait_recv()

    bufs = [pltpu.HBM(s.shape, s.dtype) for s in srcs]
    res = pl.pallas_call(
        body, name="chip_swap_wait", out_shape=bufs + bufs,
        in_specs=[_HBM] * (2 * na) + [_SEM, _SEM, _ANY], out_specs=[_HBM] * (2 * na),
        input_output_aliases={i: i for i in range(2 * na)},
        compiler_params=pltpu.CompilerParams(has_side_effects=_EFFECT),
    )(*srcs, *lands, ssem, rsem, after)
    return list(res[na:])


def _pick_rows(r, c, budget=TILE_BYTES):
    if r * c * 4 <= budget or r % 16:
        return r
    best = 16
    for tr in range(16, r, 16):
        if r % tr == 0 and tr * c * 4 <= budget:
            best = tr
    return best


def _pair_sum(name, idx4, g8, recv4):
    _, r, c = g8.shape
    tr = _pick_rows(r, c, 2 * TILE_BYTES)

    def body(idx_ref, a_ref, b_ref, o0_ref, o3_ref):
        k = pl.program_id(1)
        s = a_ref[...].astype(F32) + b_ref[...].astype(F32)

        @pl.when(k == 0)
        def _():
            o0_ref[...] = s

        @pl.when(k > 0)
        def _():
            o3_ref[...] = s.astype(BF)

    spec = pltpu.PrefetchScalarGridSpec(
        num_scalar_prefetch=1, grid=(r // tr, 4),
        in_specs=[pl.BlockSpec((None, tr, c), lambda i, k, idx: (idx[k], i, 0)),
                  pl.BlockSpec((None, tr, c), lambda i, k, idx: (k, i, 0))],
        out_specs=[pl.BlockSpec((tr, c), lambda i, k, idx: (i, 0)),
                   pl.BlockSpec((None, tr, c), lambda i, k, idx: (jnp.maximum(k - 1, 0), i, 0))])
    return pl.pallas_call(
        body, name=name, grid_spec=spec,
        out_shape=[jax.ShapeDtypeStruct((r, c), F32), jax.ShapeDtypeStruct((3, r, c), BF)],
        compiler_params=_cparams(("arbitrary", "arbitrary")),
    )(idx4, g8, recv4)


def _adamw(w, g, m, v):
    m = ADAM_B1 * m + (1.0 - ADAM_B1) * g
    v = ADAM_B2 * v + (1.0 - ADAM_B2) * jnp.square(g)
    m_hat = m / (1.0 - ADAM_B1 ** ADAM_STEP)
    v_hat = v / (1.0 - ADAM_B2 ** ADAM_STEP)
    delta = -ADAM_LR * (m_hat / (jnp.sqrt(v_hat) + ADAM_EPS) + ADAM_WD * w)
    return delta, m, v


def _sum_partials(name, idx1, own, recv):
    _, r, c = own.shape
    tr = _pick_rows(r, c, 2 * TILE_BYTES)
    nj = recv.shape[0]

    def body(idx_ref, p_ref, r_ref, g_out):
        g = p_ref[...].astype(F32)
        for k in range(nj):
            g = g + r_ref[k].astype(F32)
        g_out[...] = g

    row = pl.BlockSpec((tr, c), lambda i, idx: (i, 0))
    spec = pltpu.PrefetchScalarGridSpec(
        num_scalar_prefetch=1, grid=(r // tr,),
        in_specs=[pl.BlockSpec((None, tr, c), lambda i, idx: (idx[0], i, 0)),
                  pl.BlockSpec((nj, tr, c), lambda i, idx: (0, i, 0))],
        out_specs=row)
    return pl.pallas_call(body, name=name, grid_spec=spec, out_shape=jax.ShapeDtypeStruct((r, c), F32),
                          compiler_params=_cparams(("arbitrary",)))(idx1, own, recv)


def _adam_sharded(name, idx1, own, recv, w, m, v):
    r, c = w.shape
    tr = _pick_rows(r, c)
    nj = 0 if recv is None else recv.shape[0]
    if recv is None:
        recv = jnp.zeros((1, 8, LANES), BF)

    def body(idx_ref, p_ref, r_ref, w_ref, m_ref, v_ref, g_out, d_out, m_out, v_out):
        g = p_ref[...].astype(F32)
        for k in range(nj):
            g = g + r_ref[k].astype(F32)
        d, mn, vn = _adamw(w_ref[...], g, m_ref[...], v_ref[...])
        g_out[...] = g
        d_out[...] = d
        m_out[...] = mn
        v_out[...] = vn

    row = pl.BlockSpec((tr, c), lambda i, idx: (i, 0))
    if nj:
        recv_spec = pl.BlockSpec((nj, tr, c), lambda i, idx: (0, i, 0))
    else:
        recv_spec = pl.BlockSpec(recv.shape, lambda i, idx: (0, 0, 0))
    spec = pltpu.PrefetchScalarGridSpec(
        num_scalar_prefetch=1, grid=(r // tr,),
        in_specs=[pl.BlockSpec((None, tr, c), lambda i, idx: (idx[0], i, 0)), recv_spec, row, row, row],
        out_specs=[row] * 4)
    return pl.pallas_call(
        body, name=name, grid_spec=spec, out_shape=[jax.ShapeDtypeStruct((r, c), F32)] * 4,
        compiler_params=_cparams(("arbitrary",)),
    )(idx1, own, recv, w, m, v)


def _repl_rows():
    rows, r = {}, 0
    for name, cols in REPL:
        rows[name] = r
        r += REPL_ROWS.get(name, 1) * ((cols + D - 1) // D)
    return rows


LOSS_ROW = 24


def _pack_replicated(grads, loss_acc):
    rows = _repl_rows()
    names = [n for n, _ in REPL]

    def body(*refs):
        o_ref = refs[-1]
        o_ref[...] = jnp.zeros(o_ref.shape, F32)
        o_ref[LOSS_ROW:LOSS_ROW + 1, 0:LANES] = refs[-2][...]
        for name, ref in zip(names, refs[:-2]):
            r0 = rows[name]
            nr, nc = ref.shape
            if nc <= D:
                o_ref[r0:r0 + nr, 0:nc] = ref[...]
            else:
                for j in range((nc + D - 1) // D):
                    lo, hi = j * D, min(nc, (j + 1) * D)
                    o_ref[r0 + j:r0 + j + 1, 0:hi - lo] = ref[:, lo:hi]

    return pl.pallas_call(body, name="pack_replicated", out_shape=jax.ShapeDtypeStruct((REPL_TOTAL, D), F32),
                          compiler_params=_cparams())(*[grads[n] for n in names], loss_acc)


def _adam_replicated(g8, ws, ms, vs):
    rows = _repl_rows()
    names = [n for n, _ in REPL]
    np_ = len(names)

    def body(*refs):
        g_ref = refs[0]
        w_refs, m_refs, v_refs = refs[1:1 + np_], refs[1 + np_:1 + 2 * np_], refs[1 + 2 * np_:1 + 3 * np_]
        outs = refs[1 + 3 * np_:1 + 7 * np_]
        scr = refs[-1]
        g = g_ref[0]
        for k in range(1, N_DEV):
            g = g + g_ref[k]
        scr[...] = g
        refs[1 + 7 * np_][...] = scr[LOSS_ROW:LOSS_ROW + 1, 0:LANES]
        for i, name in enumerate(names):
            r0 = rows[name]
            nr, nc = w_refs[i].shape
            if nc <= D:
                gi = scr[r0:r0 + nr, 0:nc]
            else:
                parts = []
                for j in range((nc + D - 1) // D):
                    lo, hi = j * D, min(nc, (j + 1) * D)
                    parts.append(scr[r0 + j:r0 + j + 1, 0:hi - lo])
                gi = jnp.concatenate(parts, axis=1)
            d, mn, vn = _adamw(w_refs[i][...], gi, m_refs[i][...], v_refs[i][...])
            outs[i][...] = gi
            outs[np_ + i][...] = d
            outs[2 * np_ + i][...] = mn
            outs[3 * np_ + i][...] = vn

    shp = [jax.ShapeDtypeStruct(w.shape, F32) for w in ws]
    res = pl.pallas_call(body, name="adam_replicated", out_shape=shp * 4 + [jax.ShapeDtypeStruct((1, LANES), F32)],
                         scratch_shapes=[pltpu.VMEM((REPL_TOTAL, D), F32)], compiler_params=_cparams(),
                         )(g8, *ws, *ms, *vs)
    return [dict(zip(names, res[k * np_:(k + 1) * np_])) for k in range(4)], res[-1]


_WEIGHTS = ("attn_pre_norm", "w_in", "hgrn_lb", "hgrn_gnorm", "w_branch_a", "rwkv_mu", "rwkv_w0", "rwkv_w2",
            "rwkv_a0", "rwkv_a2", "rwkv_g2", "rwkv_k_k", "rwkv_k_a", "rwkv_r_k", "rwkv_ln_w", "rwkv_ln_b",
            "w_branch_b", "w_out", "attn_post_norm", "ffn_pre_norm", "w_up", "conv_w", "conv_b", "w_down",
            "ffn_post_norm")
_BIG = ("w_in", "w_up", "w_down", "w_branch_a", "w_branch_b", "w_out")


def _stages():
    one = [D]
    hw = HG_K * HG_PER_STEP
    rw = LANES * RW_PAIRS_PER_STEP
    return dict(
        pre1=_Stage("pre1", _f_pre1, 1, 256, [False], [one], [0], [], [one], [BF]),
        pre1_res=_Stage("pre1", _f_pre1_residual, 1, 256, [False], [one], [0], [], [one, one], [BF, F32]),
        mixers=_Stage("mixers", _f_mixers, 1, 2 * RW_CHUNK, [False] * 13, [[D] * 7 + [LANES, LANES]], [0],
                      [(hw, HG_K), (1, RW_COLS), (rw, LANES)], [one, one], [BF, BF]),
        merge=_Stage("merge", _f_merge, 4, 512, [], [[256]] * 4, [29, 33, 0, 0], [], [[256]], [BF]),
        post1=_Stage("post1", _f_post1, 1, 256, [False, False], [one, one], [0, 0], [], [one, one], [F32, BF]),
        conv=_Stage("conv", _f_conv, 1, 128, [False, False], [[DFF, DFF]], [0], [(1, 2 * DFF), (1, 2 * DFF)],
                    [[DFF]], [BF]),
    )


def _cols_to_blocks(w, per):
    return w.reshape(w.shape[0], N_DEV, per).transpose(1, 0, 2)


def _blocks_to_cols(g):
    return g.transpose(1, 0, 2).reshape(g.shape[1], N_DEV * g.shape[2])


def kernel(x, attn_pre_norm, w_in, hgrn_lb, hgrn_gnorm, w_branch_a, rwkv_mu, rwkv_w0, rwkv_w2, rwkv_a0, rwkv_a2, rwkv_g2, rwkv_k_k, rwkv_k_a, rwkv_r_k, rwkv_ln_w, rwkv_ln_b, w_branch_b, w_out, attn_post_norm, ffn_pre_norm, w_up, conv_w, conv_b, w_down, ffn_post_norm, loss_target, m_attn_pre_norm, m_w_in, m_hgrn_lb, m_hgrn_gnorm, m_w_branch_a, m_rwkv_mu, m_rwkv_w0, m_rwkv_w2, m_rwkv_a0, m_rwkv_a2, m_rwkv_g2, m_rwkv_k_k, m_rwkv_k_a, m_rwkv_r_k, m_rwkv_ln_w, m_rwkv_ln_b, m_w_branch_b, m_w_out, m_attn_post_norm, m_ffn_pre_norm, m_w_up, m_conv_w, m_conv_b, m_w_down, m_ffn_post_norm, v_attn_pre_norm, v_w_in, v_hgrn_lb, v_hgrn_gnorm, v_w_branch_a, v_rwkv_mu, v_rwkv_w0, v_rwkv_w2, v_rwkv_a0, v_rwkv_a2, v_rwkv_g2, v_rwkv_k_k, v_rwkv_k_a, v_rwkv_r_k, v_rwkv_ln_w, v_rwkv_ln_b, v_w_branch_b, v_w_out, v_attn_post_norm, v_ffn_pre_norm, v_w_up, v_conv_w, v_conv_b, v_w_down, v_ffn_post_norm):
    w = dict(attn_pre_norm=attn_pre_norm, w_in=w_in, hgrn_lb=hgrn_lb, hgrn_gnorm=hgrn_gnorm, w_branch_a=w_branch_a, rwkv_mu=rwkv_mu, rwkv_w0=rwkv_w0, rwkv_w2=rwkv_w2, rwkv_a0=rwkv_a0, rwkv_a2=rwkv_a2, rwkv_g2=rwkv_g2, rwkv_k_k=rwkv_k_k, rwkv_k_a=rwkv_k_a, rwkv_r_k=rwkv_r_k, rwkv_ln_w=rwkv_ln_w, rwkv_ln_b=rwkv_ln_b, w_branch_b=w_branch_b, w_out=w_out, attn_post_norm=attn_post_norm, ffn_pre_norm=ffn_pre_norm, w_up=w_up, conv_w=conv_w, conv_b=conv_b, w_down=w_down, ffn_post_norm=ffn_post_norm)
    mo = dict(attn_pre_norm=m_attn_pre_norm, w_in=m_w_in, hgrn_lb=m_hgrn_lb, hgrn_gnorm=m_hgrn_gnorm, w_branch_a=m_w_branch_a, rwkv_mu=m_rwkv_mu, rwkv_w0=m_rwkv_w0, rwkv_w2=m_rwkv_w2, rwkv_a0=m_rwkv_a0, rwkv_a2=m_rwkv_a2, rwkv_g2=m_rwkv_g2, rwkv_k_k=m_rwkv_k_k, rwkv_k_a=m_rwkv_k_a, rwkv_r_k=m_rwkv_r_k, rwkv_ln_w=m_rwkv_ln_w, rwkv_ln_b=m_rwkv_ln_b, w_branch_b=m_w_branch_b, w_out=m_w_out, attn_post_norm=m_attn_post_norm, ffn_pre_norm=m_ffn_pre_norm, w_up=m_w_up, conv_w=m_conv_w, conv_b=m_conv_b, w_down=m_w_down, ffn_post_norm=m_ffn_post_norm)
    vo = dict(attn_pre_norm=v_attn_pre_norm, w_in=v_w_in, hgrn_lb=v_hgrn_lb, hgrn_gnorm=v_hgrn_gnorm, w_branch_a=v_w_branch_a, rwkv_mu=v_rwkv_mu, rwkv_w0=v_rwkv_w0, rwkv_w2=v_rwkv_w2, rwkv_a0=v_rwkv_a0, rwkv_a2=v_rwkv_a2, rwkv_g2=v_rwkv_g2, rwkv_k_k=v_rwkv_k_k, rwkv_k_a=v_rwkv_k_a, rwkv_r_k=v_rwkv_r_k, rwkv_ln_w=v_rwkv_ln_w, rwkv_ln_b=v_rwkv_ln_b, w_branch_b=v_w_branch_b, w_out=v_w_out, attn_post_norm=v_attn_post_norm, ffn_pre_norm=v_ffn_pre_norm, w_up=v_w_up, conv_w=v_conv_w, conv_b=v_conv_b, w_down=v_w_down, ffn_post_norm=v_ffn_post_norm)

    t = x.shape[1]
    x2 = x.reshape(t, D)
    tgt = loss_target.reshape(t, D)
    st = _stages()

    me = 4 * lax.axis_index("x") + 2 * lax.axis_index("y") + lax.axis_index("c")
    small = jnp.concatenate([rwkv_w2[0], rwkv_a2[0], rwkv_g2[0]], axis=0).astype(BF)
    g_in, g_small = _all_gather("gather_weights", [w_in[0].T.astype(BF), small])
    fw_in_t = g_in.reshape(IN_COLS, D)
    z64 = jnp.zeros((64, D), BF)
    w2p = jnp.concatenate([_blocks_to_cols(g_small[:, 0:64]), z64], axis=0)
    a2p = jnp.concatenate([z64, _blocks_to_cols(g_small[:, 64:128])], axis=0)
    g2f = _blocks_to_cols(g_small[:, 128:256])
    conv_bits = jnp.pad(lax.bitcast_convert_type(conv_w[0], BF).reshape(3, 2 * 704), ((0, 29), (0, 0)))
    late = [w_up[0].T.astype(BF)] + [w[k][0].astype(BF) for k in _BIG[2:]] + [conv_bits]
    late_gather = _Exchange("gather2", late)
    r_k = rwkv_r_k.reshape(1, D)

    (xn,), _ = _stage_fwd(st["pre1"], t, [attn_pre_norm], [x2])
    z = _mm("in_proj", xn, fw_in_t, "nt", F32, tm=512, tn=4736, b_outer=True)
    mix_par = [hgrn_lb, hgrn_gnorm, rwkv_mu, rwkv_w0, w2p, rwkv_a0, a2p, g2f, rwkv_k_k, rwkv_k_a,
               rwkv_ln_w, rwkv_ln_b, r_k]
    mix_in = [z]
    (o_a, o_b), mix_saved = _stage_fwd(st["mixers"], t, mix_par, mix_in, hook=late_gather)
    gl = [lax.dynamic_update_slice(g, own[None], (me, 0, 0)) for g, own in zip(late_gather.results, late)]
    fw_up_t = gl[0].reshape(2 * DFF, D)
    fw_down = gl[1].reshape(DFF, D)
    fw_a, fw_b, fw_out = (g.reshape(D, D) for g in gl[2:5])
    conv_full = _blocks_to_cols(lax.bitcast_convert_type(gl[5][:, :3].reshape(N_DEV, 3, 704, 2), F32))
    y_a = _mm("branch_a", o_a, fw_a, "nn")
    y_b = _mm("branch_b", o_b, fw_b, "nn")
    (merged,), _ = _stage_fwd(st["merge"], t, [], [z, z, y_a, y_b])
    mix = _mm("out_proj", merged, fw_out, "nn")
    (h1, xn2), _ = _stage_fwd(st["post1"], t, [attn_post_norm, ffn_pre_norm], [x2, mix])
    hu = _mm("up_proj", xn2, fw_up_t, "nt", F32, tm=1024, tn=1408)
    conv_par = [conv_full, conv_b]
    (act,), conv_saved = _stage_fwd(st["conv"], t, conv_par, [hu])
    ff = _mm("down_proj", act, fw_down, "nn")

    loss_acc, d_ffn_post, dh1, dff = _loss_stage(t, ffn_post_norm, h1, ff, tgt)
    dact = _mm("d_act", dff, fw_down, "nt", F32, tm=1024, tn=1408)
    dw_down = _mm("dw_down", act, dff, "tn", BF, tm=1408, tn=512)
    (dcw, dcb), (dhu,) = _stage_bwd(st["conv"], t, conv_par, [hu], conv_saved, [[dact]], [BF])
    dxn2 = _mm("d_xn2", dhu, fw_up_t, "nn", F32, tm=1024, tn=1024)
    dw_up_t = _mm("dw_up", dhu, xn2, "tn", BF, tm=1408, tn=1024)
    (d_post, d_pre2), (dx_a, dmix) = _stage_bwd(st["post1"], t, [attn_post_norm, ffn_pre_norm], [x2, mix], [],
                                                 [[dh1], [dxn2]], [F32, BF])
    dmerged = _mm("d_merged", dmix, fw_out, "nt")
    dw_out = _mm("dw_out", merged, dmix, "tn", BF)
    _, (dga, dgb, dy_a, dy_b) = _stage_bwd(st["merge"], t, [], [z, z, y_a, y_b], [], [[dmerged]], [BF, BF, BF, BF])
    do_a = _mm("d_oa", dy_a, fw_a, "nt")
    dw_a = _mm("dw_a", o_a, dy_a, "tn", BF)
    do_b = _mm("d_ob", dy_b, fw_b, "nt")
    dw_b = _mm("dw_b", o_b, dy_b, "tn", BF)
    early = [dw_up_t.reshape(N_DEV, 704, D), dw_down.reshape(N_DEV, 352, D), dw_a.reshape(N_DEV, 128, D),
             dw_b.reshape(N_DEV, 128, D), dw_out.reshape(N_DEV, 128, D), _cols_to_blocks(dcw.astype(BF), 704)]
    early_scatter = _Exchange("scatter", early)
    mix_dp, dz_hr = _stage_bwd(st["mixers"], t, mix_par, mix_in, mix_saved, [[do_a], [do_b]], [BF],
                               hook=early_scatter)
    d_lb, d_gn, d_mu, d_w0, d_w2p, d_a0, d_a2p, d_g2, d_kk, d_ka, d_lnw, d_lnb, d_rk = mix_dp
    dz = jnp.concatenate(dz_hr + [dga, dgb], axis=1)
    dw_in_t = _mm("dw_in", dz, xn, "tn", BF, tm=256, tn=1024)

    ax, ay, ac = lax.axis_index("x"), lax.axis_index("y"), lax.axis_index("c")
    idx4 = jnp.stack([4 * cx + 2 * cy + ac for cx, cy in ((ax, ay), (1 - ax, ay), (ax, 1 - ay), (1 - ax, 1 - ay))])
    idx4 = idx4.astype(jnp.int32)
    idx_me, idx_0 = idx4[0:1], jnp.zeros((1,), jnp.int32)
    d_small = jnp.concatenate([d_w2p[:64], d_a2p[64:], d_g2], axis=0).astype(BF)
    g8s = [dw_in_t.reshape(N_DEV, 1184, D), _cols_to_blocks(d_small, LANES)]
    recv4s = _reduce_pair(g8s)
    sums = [_pair_sum("pair_sum_" + n, idx4, g, r) for n, g, r in zip(("w_in", "small"), g8s, recv4s)]
    swap_ssem, swap_rsem, swap_srcs, swap_lands, token = _chip_swap_start([s[1] for s in sums])
    dxn = _mm("d_xn", dz, fw_in_t, "nn", F32, tm=512, tn=1024, token=token)
    (d_pre1,), (dx,) = _stage_bwd(st["pre1_res"], t, [attn_pre_norm], [x2], [], [[dxn], [dx_a]], [F32])
    grad_x = dx.reshape(x.shape)

    rg = dict(attn_pre_norm=d_pre1, hgrn_lb=d_lb, hgrn_gnorm=d_gn, rwkv_mu=d_mu, rwkv_w0=d_w0, rwkv_a0=d_a0,
              rwkv_k_k=d_kk, rwkv_k_a=d_ka, rwkv_r_k=d_rk, rwkv_ln_w=d_lnw, rwkv_ln_b=d_lnb, attn_post_norm=d_post,
              ffn_pre_norm=d_pre2, conv_b=dcb, ffn_post_norm=d_ffn_post)
    (g8,) = _all_gather("gather_small_grads", [_pack_replicated(rg, loss_acc)])
    rnames = [n for n, _ in REPL]
    flat = lambda src: [src[n].reshape(1, D) if n == "rwkv_r_k" else src[n] for n in rnames]
    rp_out, loss_row = _adam_replicated(g8, flat(w), flat(mo), flat(vo))
    loss = loss_row[0, 0]
    recv3s = _chip_swap_wait(swap_ssem, swap_rsem, swap_srcs, swap_lands, rp_out[0]["attn_pre_norm"])
    for kind in range(4):
        rp_out[kind]["rwkv_r_k"] = rp_out[kind]["rwkv_r_k"].reshape(rwkv_r_k.shape)

    def small_of(src):
        return jnp.concatenate([src["rwkv_w2"][0], src["rwkv_a2"][0], src["rwkv_g2"][0]], axis=0)

    sh_out = [dict() for _ in range(4)]
    g_in = _sum_partials("sum_w_in", idx_0, sums[0][0][None], recv3s[0]).T
    res = _adam_sharded("adam_w_in", idx_0, g_in[None], None, *[src["w_in"][0] for src in (w, mo, vo)])
    res_s = _adam_sharded("adam_small", idx_0, sums[1][0][None], recv3s[1], *[small_of(src) for src in (w, mo, vo)])
    for kind in range(4):
        sh_out[kind]["w_in"] = res[kind][None]
        sh_out[kind]["rwkv_w2"] = res_s[kind][0:64][None]
        sh_out[kind]["rwkv_a2"] = res_s[kind][64:128][None]
        sh_out[kind]["rwkv_g2"] = res_s[kind][128:256][None]
    for n, own, recv in zip(_BIG[1:] + ("conv_w",), early, early_scatter.results):
        if n == "w_up":
            g_up = _sum_partials("sum_w_up", idx_me, own, recv).T
            res = _adam_sharded("adam_" + n, idx_0, g_up[None], None, *[src[n][0] for src in (w, mo, vo)])
        else:
            res = _adam_sharded("adam_" + n, idx_me, own, recv, *[src[n][0] for src in (w, mo, vo)])
        for kind in range(4):
            sh_out[kind][n] = res[kind][None]

    outs = [loss, grad_x]
    for kind in range(4):
        for name in _WEIGHTS:
            outs.append(sh_out[kind][name] if name in sh_out[kind] else rp_out[kind][name])
    return tuple(outs)
```

```python
import functools

import jax
import jax.numpy as jnp
from jax import lax
from jax.experimental import pallas as pl
from jax.experimental.pallas import tpu as pltpu

F32 = jnp.float32
BF = jnp.bfloat16
MESH = pl.DeviceIdType.MESH

D = 1024
HG_HEADS = 8
HG_K = 128
HG_CHUNK = 32
HG_SCALE = HG_K ** -0.5
HG_PER_STEP = 8
RW_HEADS = 16
RW_N = 64
RW_CHUNK = 64
RW_PAIRS_PER_STEP = 8
DFF = 2816
IN_COLS = 9472
RW_COLS = 3328
EPS = 1e-6
GN_EPS = 1e-5 * RW_N
ADAM_LR = 0.001
ADAM_B1 = 0.9
ADAM_B2 = 0.999
ADAM_EPS = 1e-08
ADAM_WD = 0.01
ADAM_STEP = 10
N_DEV = 8
LANES = 128
VMEM_LIMIT = 56 * 1024 * 1024
TILE_BYTES = 1280 * 1024

REPL = (("attn_pre_norm", 1024), ("hgrn_lb", 1024), ("hgrn_gnorm", 1024), ("rwkv_mu", 3328), ("rwkv_w0", 1024),
        ("rwkv_a0", 1024), ("rwkv_k_k", 1024), ("rwkv_k_a", 1024), ("rwkv_r_k", 1024), ("rwkv_ln_w", 1024),
        ("rwkv_ln_b", 1024), ("attn_post_norm", 1024), ("ffn_pre_norm", 1024), ("conv_b", 5632), ("ffn_post_norm", 1024))
REPL_ROWS = {"hgrn_lb": 2}
REPL_TOTAL = 32


def _cparams(sem=None, **kw):
    return pltpu.CompilerParams(dimension_semantics=sem, vmem_limit_bytes=VMEM_LIMIT, **kw)


_DN = {"nn": ((1,), (0,)), "nt": ((1,), (1,)), "tn": ((0,), (0,))}


def _raw_dot(a, b, mode):
    return lax.dot_general(a.astype(BF), b.astype(BF), (_DN[mode], ((), ())), preferred_element_type=F32)


@functools.partial(jax.custom_vjp, nondiff_argnums=(2,))
def _dot(a, b, mode):
    return _raw_dot(a, b, mode)


def _dot_fwd(a, b, mode):
    return _raw_dot(a, b, mode), (a, b)


def _dot_bwd(mode, res, g):
    a, b = res
    if mode == "nn":
        return _dot(g, b, "nt"), _dot(a, g, "tn")
    if mode == "nt":
        return _dot(g, b, "nn"), _dot(g, a, "tn")
    return _dot(b, g, "nt"), _dot(a, g, "nn")


_dot.defvjp(_dot_fwd, _dot_bwd)


def _bf_pieces(x, n):
    out, r = [], x
    for i in range(n):
        p = r.astype(BF)
        out.append(p)
        if i + 1 < n:
            r = r - p.astype(F32)
    return out


def _raw_split_dot(x, e, mode, n, x_left):
    eb = e.astype(BF)
    acc = None
    for p in _bf_pieces(x, n):
        ops = (p, eb) if x_left else (eb, p)
        t = lax.dot_general(*ops, (_DN[mode], ((), ())), preferred_element_type=F32)
        acc = t if acc is None else acc + t
    return acc


def _raw_headsum(x):
    t = x.shape[0]
    i = lax.broadcasted_iota(jnp.int32, (LANES, LANES), 0)
    j = lax.broadcasted_iota(jnp.int32, (LANES, LANES), 1)
    same = jnp.where((i >= RW_N) == (j >= RW_N), 1.0, 0.0).astype(F32)
    groups = x.shape[1] // LANES
    rows = jnp.concatenate([x[:, q * LANES:(q + 1) * LANES] for q in range(groups)], axis=0)
    s = _raw_split_dot(rows, same, "nn", 2, True)
    return jnp.concatenate([s[q * t:(q + 1) * t] for q in range(groups)], axis=1)


@jax.custom_vjp
def _headsum(x):
    return _raw_headsum(x)


def _headsum_fwd(x):
    return _raw_headsum(x), None


def _headsum_bwd(_, g):
    return (_raw_headsum(g),)


_headsum.defvjp(_headsum_fwd, _headsum_bwd)


@functools.partial(jax.custom_vjp, nondiff_argnums=(2,))
def _tdot(tri, x, n):
    return _raw_split_dot(x, tri, "nn", n, False)


def _tdot_fwd(tri, x, n):
    return _raw_split_dot(x, tri, "nn", n, False), tri


def _tdot_bwd(n, tri, g):
    return jnp.zeros_like(tri), _raw_split_dot(g, tri, "tn", n, False)


_tdot.defvjp(_tdot_fwd, _tdot_bwd)


def _row(x, i):
    r = lax.broadcasted_iota(jnp.int32, x.shape, 0)
    return jnp.sum(jnp.where(r == i, x, 0.0), axis=0, keepdims=True)


def _shift_down(x, prev):
    t = x.shape[0]

    @jax.custom_vjp
    def sh(x, prev):
        r = lax.broadcasted_iota(jnp.int32, x.shape, 0)
        return jnp.where(r == 0, prev, pltpu.roll(x, 1, 0))

    def fwd(x, prev):
        return sh(x, prev), None

    def bwd(_, g):
        r = lax.broadcasted_iota(jnp.int32, g.shape, 0)
        dx = jnp.where(r == t - 1, 0.0, pltpu.roll(g, t - 1, 0))
        return dx, jnp.sum(jnp.where(r == 0, g, 0.0), axis=0, keepdims=True)

    sh.defvjp(fwd, bwd)
    return sh(x, prev)


def _sigmoid(x):
    return jax.nn.sigmoid(x)


def _silu(x):
    return x * jax.nn.sigmoid(x)


def _softplus(x):
    return jnp.maximum(x, 0.0) + jnp.log(1.0 + jnp.exp(-jnp.abs(x)))


def _rms(x, g):
    return (x * lax.rsqrt(jnp.mean(x * x, axis=-1, keepdims=True) + EPS)) * g


def _tril(c):
    r = lax.broadcasted_iota(jnp.int32, (c, c), 0)
    cc = lax.broadcasted_iota(jnp.int32, (c, c), 1)
    return cc <= r


def _f_pre1(ps, xs, cs):
    return [_rms(xs[0], ps[0])], []


def _f_pre1_residual(ps, xs, cs):
    return [_rms(xs[0], ps[0]), xs[0]], []


def _f_hgrn(ps, xs, cs):
    lbraw, gn = ps
    hq, hf, hi, hg = xs
    hd = range(HG_PER_STEP)
    st = [cs[0][p * HG_K:(p + 1) * HG_K] for p in hd]
    l0, l1 = _row(lbraw, 0), _row(lbraw, 1)
    m = jnp.maximum(l0, l1)
    e0, e1 = jnp.exp(l0 - m), jnp.exp(l1 - m)
    lb = e0 / (e0 + e1)
    q = _silu(hq) * HG_SCALE
    f = lb + (1.0 - lb) * _sigmoid(hf)
    kh = 1.0 - f
    gl = jnp.log(f)
    c = HG_CHUNK
    low = _tril(c)
    tri = jnp.where(low, 1.0, 0.0).astype(F32)
    outs = []
    for i in range(hq.shape[0] // c):
        rows = slice(i * c, (i + 1) * c)
        b = _tdot(tri, gl[rows], 3)
        bref = _row(b, c // 2 - 1)
        blast = _row(b, c - 1)
        qi = q[rows] * jnp.exp(b - bref)
        ki = kh[rows] * jnp.exp(bref - b)
        qd = q[rows] * jnp.exp(b)
        kd = kh[rows] * jnp.exp(blast - b)
        dec = jnp.exp(blast)
        sl = [slice(p * HG_K, (p + 1) * HG_K) for p in hd]
        sc = [jnp.where(low, _dot(qi[:, sl[p]], ki[:, sl[p]], "nt"), 0.0) for p in hd]
        o = [_dot(sc[p], hi[rows, sl[p]], "nn") + _dot(qd[:, sl[p]], st[p], "nt") for p in hd]
        u = [_dot(hi[rows, sl[p]], kd[:, sl[p]], "tn") for p in hd]
        st = [dec[:, sl[p]] * st[p] + u[p] for p in hd]
        outs.append(jnp.concatenate(o, axis=1) if len(o) > 1 else o[0])
    o = outs[0] if len(outs) == 1 else jnp.concatenate(outs, axis=0)
    on = []
    for p in hd:
        op = o[:, p * HG_K:(p + 1) * HG_K]
        on.append(op * lax.rsqrt(jnp.mean(op * op, axis=-1, keepdims=True) + EPS))
    o = jnp.concatenate(on, axis=1) if len(on) > 1 else on[0]
    o = o * gn
    return [o * _silu(hg)], [jnp.concatenate(st, axis=0) if len(st) > 1 else st[0]]


_RW_OFFS = (0, 1024, 2048, 3072, 3200, 3328)


def _f_rwpre(ps, xs, cs):
    mu, w0, w2p, a0, a2p, g2, k_k, k_a = ps
    (prev,) = cs
    t = xs[0].shape[0]
    zs = []
    for i, z in enumerate(xs):
        lo, hi = _RW_OFFS[i], _RW_OFFS[i + 1]
        zs.append(z + mu[:, lo:hi] * (_shift_down(z, prev[:, lo:hi]) - z))
    rr, kr, vr, wa, gz = zs
    w_log = -_softplus(-(w0 + _dot(jnp.tanh(wa), w2p, "nn"))) - 0.5
    lw = -jnp.exp(w_log)
    a = _sigmoid(a0 + _dot(wa, a2p, "nn"))
    g = _dot(_sigmoid(gz), g2, "nn")
    kkr = kr * k_k
    kk = kkr / jnp.maximum(jnp.sqrt(_headsum(kkr * kkr)), 1e-12)
    k2 = kr * (1.0 + (a - 1.0) * k_a)
    newprev = jnp.concatenate([_row(z, t - 1) for z in xs], axis=1)
    return [rr, lw, k2, vr, -kk, kk * a, g], [newprev]


def _raw_inverses(ls):
    n = ls[0].shape[0]
    r = lax.broadcasted_iota(jnp.int32, (n, n), 0)
    c = lax.broadcasted_iota(jnp.int32, (n, n), 1)
    eye = jnp.where(r == c, 1.0, 0.0).astype(F32)
    tinv = [eye + l for l in ls]
    pw = ls
    for _ in range(5):
        pw = [_raw_dot(p, p, "nn") for p in pw]
        tinv = [t + _raw_dot(t, p, "nn") for t, p in zip(tinv, pw)]
    return tinv


@jax.custom_vjp
def _unit_lower_inverses(ls):
    return _raw_inverses(ls)


def _inverses_fwd(ls):
    tinv = _raw_inverses(ls)
    return tinv, tinv


def _inverses_bwd(tinv, gs):
    return ([_raw_dot(_raw_dot(t, g, "tn"), t, "nt") for t, g in zip(tinv, gs)],)


_unit_lower_inverses.defvjp(_inverses_fwd, _inverses_bwd)


@jax.custom_vjp
def _known_inverses(ls, tinv):
    return tinv


def _known_fwd(ls, tinv):
    return tinv, tinv


def _known_bwd(tinv, gs):
    return [_raw_dot(_raw_dot(t, g, "tn"), t, "nt") for t, g in zip(tinv, gs)], [jnp.zeros_like(t) for t in tinv]


_known_inverses.defvjp(_known_fwd, _known_bwd)


def _f_rwscan(ps, xs, cs, kept=None):
    state = cs[0]
    ys, keep = [], []
    n = 2 * RW_CHUNK
    for i in range(xs[0].shape[0] // RW_CHUNK):
        known = None
        if kept is not None:
            lo = i * RW_PAIRS_PER_STEP * n
            known = [kept[lo + p * n:lo + (p + 1) * n] for p in range(RW_PAIRS_PER_STEP)]
        y, state, tinv = _rwkv_chunk([x[i * RW_CHUNK:(i + 1) * RW_CHUNK] for x in xs], state, known)
        ys.append(y)
        keep += tinv
    return [ys[0] if len(ys) == 1 else jnp.concatenate(ys, axis=0)], [state], jnp.concatenate(keep, axis=0)


def _rwkv_chunk(xs, state, known=None):
    npair = RW_PAIRS_PER_STEP
    pr = range(npair)
    r, lw, k, v, av, bv = [[x[:, p * LANES:(p + 1) * LANES] for p in pr] for x in xs]
    sv = [state[p * LANES:(p + 1) * LANES] for p in pr]
    c = RW_CHUNK
    n = 2 * c
    tri = jnp.where(_tril(c), 1.0, 0.0).astype(F32)
    cl = [_tdot(tri, lw[p], 3) for p in pr]
    cl_last = [_row(cl[p], c - 1) for p in pr]
    lane = lax.broadcasted_iota(jnp.int32, (c, LANES), 1)
    h0 = lane < RW_N

    def stack(x):
        return jnp.concatenate([jnp.where(h0, x, 0.0), jnp.where(h0, 0.0, x)], axis=0)

    am = [stack(av[p] * jnp.exp(cl[p] - lw[p])) for p in pr]
    bm = [stack(bv[p] * jnp.exp(-cl[p])) for p in pr]
    km = [stack(k[p] * jnp.exp(-cl[p])) for p in pr]
    rm = [stack(r[p] * jnp.exp(cl[p])) for p in pr]
    vm = [stack(v[p]) for p in pr]
    rn = lax.broadcasted_iota(jnp.int32, (n, n), 0)
    cn = lax.broadcasted_iota(jnp.int32, (n, n), 1)
    blk = (rn >= c) == (cn >= c)
    strict = blk & (cn < rn)
    incl = blk & (cn <= rn)
    lab = [jnp.where(strict, _dot(am[p], bm[p], "nt"), 0.0) for p in pr]
    lak = [jnp.where(strict, _dot(am[p], km[p], "nt"), 0.0) for p in pr]
    wrb = [jnp.where(incl, _dot(rm[p], bm[p], "nt"), 0.0) for p in pr]
    wrk = [jnp.where(incl, _dot(rm[p], km[p], "nt"), 0.0) for p in pr]
    tinv = _unit_lower_inverses(lab) if known is None else _known_inverses(lab, known)
    rhs = [_dot(am[p], sv[p], "nt") + _dot(lak[p], vm[p], "nn") for p in pr]
    um = [_dot(tinv[p], rhs[p], "nn") for p in pr]
    ym = [_dot(rm[p], sv[p], "nt") + _dot(wrb[p], um[p], "nn") + _dot(wrk[p], vm[p], "nn") for p in pr]
    sn = [(sv[p] + _dot(um[p], bm[p], "tn") + _dot(vm[p], km[p], "tn")) * jnp.exp(cl_last[p]) for p in pr]
    ys = [ym[p][:c] + ym[p][c:] for p in pr]
    return jnp.concatenate(ys, axis=1), jnp.concatenate(sn, axis=0), tinv


def _f_mixers(ps, xs, cs):
    return _mixers(ps, xs, cs, None)


def _f_mixers_kept(ps, xs, cs, kept):
    return _mixers(ps, xs, cs, kept[0])[:2]


def _mixers(ps, xs, cs, kept):
    oa, st = _f_hgrn(ps[:2], xs[:4], cs[:1])
    (r, lw, k, v, av, bv, g), prev = _f_rwpre(ps[2:10], xs[4:], cs[1:2])
    y, sv, keep = _f_rwscan([], [r, lw, k, v, av, bv], cs[2:], kept)
    ob, _ = _f_rwpost(ps[10:], y + [r, k, v, g], [])
    return oa + ob, st + prev + sv, [keep]


def _f_rwpost(ps, xs, cs):
    ln_w, ln_b, r_k = ps
    y, r, k, v, g = xs
    inv_n = 1.0 / RW_N
    yc = y - _headsum(y) * inv_n
    var = _headsum(yc * yc) * inv_n
    yn = yc * lax.rsqrt(var + GN_EPS)
    yn = yn * ln_w + ln_b
    bonus = _headsum(r * k * r_k) * v
    return [(yn + bonus) * g], []


def _f_merge(ps, xs, cs):
    ga, gb, ya, yb = xs
    return [_sigmoid(ga) * ya + _sigmoid(gb) * yb], []


def _f_post1(ps, xs, cs):
    x, mix = xs
    h1 = x + _rms(mix, ps[0])
    return [h1, _rms(h1, ps[1])], []


def _f_conv(ps, xs, cs):
    cw, cb = ps
    p1, p2 = cs
    w0, w1, w2 = _row(cw, 0), _row(cw, 1), _row(cw, 2)
    t = xs[0].shape[0]
    hc = []
    for i, x in enumerate(xs):
        sl = slice(i * DFF, (i + 1) * DFF)
        s1 = _shift_down(x, p1[:, sl])
        s2 = _shift_down(s1, p2[:, sl])
        hc.append(cb[:, sl] + w0[:, sl] * s2 + w1[:, sl] * s1 + w2[:, sl] * x)
    n1 = jnp.concatenate([_row(x, t - 1) for x in xs], axis=1)
    n2 = jnp.concatenate([_row(x, t - 2) for x in xs], axis=1)
    return [_silu(hc[0]) * hc[1]], [n1, n2]


class _Stage:
    def __init__(self, name, f, g, tm, par_per_g, in_pieces, in_offs, carry_shapes, out_pieces, out_dtypes,
                 kept_shapes=(), f_kept=None):
        self.name, self.f, self.g, self.tm = name, f, g, tm
        self.par_per_g, self.in_pieces, self.in_offs = par_per_g, in_pieces, in_offs
        self.carry_shapes, self.out_pieces, self.out_dtypes = carry_shapes, out_pieces, out_dtypes
        self.kept_shapes, self.f_kept = list(kept_shapes), f_kept


def _par_spec(arr, per_g, g):
    r, c = arr.shape
    if per_g:
        return pl.BlockSpec((r, c // g), lambda gi, ni: (0, gi))
    return pl.BlockSpec((r, c), lambda gi, ni: (0, 0))


def _row_spec(tm, width, off, n, rev):
    if rev:
        return pl.BlockSpec((tm, width), lambda gi, ni: (n - 1 - ni, off + gi))
    return pl.BlockSpec((tm, width), lambda gi, ni: (ni, off + gi))


def _carry_spec(shape, n, rev):
    if rev:
        return pl.BlockSpec((None, None) + shape, lambda gi, ni: (gi, n - 1 - ni, 0, 0))
    return pl.BlockSpec((None, None) + shape, lambda gi, ni: (gi, ni, 0, 0))


def _load_pieces(refs, pieces_list):
    out = []
    for ref, pieces in zip(refs, pieces_list):
        o = 0
        for w in pieces:
            out.append(ref[:, o:o + w].astype(F32))
            o += w
    return out


def _store_pieces(refs, pieces_list, vals):
    k = 0
    for ref, pieces in zip(refs, pieces_list):
        o = 0
        for w in pieces:
            ref[:, o:o + w] = vals[k].astype(ref.dtype)
            k += 1
            o += w


_ANY = pl.BlockSpec(memory_space=pl.ANY)


class _Exchange:
    def __init__(self, kind, arrs):
        self.kind, self.arrs, self.results = kind, list(arrs), None
        if kind == "scatter":
            self.out_shape = [jax.ShapeDtypeStruct((N_DEV - 1,) + a.shape[1:], a.dtype) for a in self.arrs]
        else:
            self.out_shape = [jax.ShapeDtypeStruct((N_DEV,) + a.shape, a.dtype) for a in self.arrs]
        self.nsem = (N_DEV if kind == "gather2" else N_DEV - 1) * len(self.arrs)

    def copies(self, in_refs, out_refs, ssem, rsem):
        x, y, c = lax.axis_index("x"), lax.axis_index("y"), lax.axis_index("c")
        me = 4 * x + 2 * y + c
        cps = []
        for a, (i_ref, o_ref) in enumerate(zip(in_refs, out_refs)):
            for j in range(1, N_DEV):
                px = 1 - x if j & 4 else x
                py = 1 - y if j & 2 else y
                pc = 1 - c if j & 1 else c
                if self.kind == "gather":
                    src, dst = i_ref, o_ref.at[me]
                else:
                    src, dst = i_ref.at[4 * px + 2 * py + pc], o_ref.at[j - 1]
                s = (N_DEV - 1) * a + j - 1
                cps.append(pltpu.make_async_remote_copy(src_ref=src, dst_ref=dst, send_sem=ssem.at[s],
                                                        recv_sem=rsem.at[s], device_id=(px, py, pc),
                                                        device_id_type=MESH))
        return cps

    def run(self, step, total, in_refs, out_refs, ssem, rsem):
        if self.kind == "gather2":
            return self.run_two_level(step, total, in_refs, out_refs, ssem, rsem)

        @pl.when(step == 0)
        def _():
            for cp in self.copies(in_refs, out_refs, ssem, rsem):
                cp.start()

        @pl.when(step == total - 1)
        def _():
            for cp in self.copies(in_refs, out_refs, ssem, rsem):
                cp.wait()

    def run_two_level(self, step, total, in_refs, out_refs, ssem, rsem):
        x, y, c = lax.axis_index("x"), lax.axis_index("y"), lax.axis_index("c")
        sibling, xn, yn = (x, y, 1 - c), (1 - x, y, c), (x, 1 - y, c)
        arrs = range(len(in_refs))
        ns = N_DEV

        def num(px, py, pc):
            return 4 * px + 2 * py + pc

        def copy(a, k, to, src, dst):
            return pltpu.make_async_remote_copy(src_ref=src, dst_ref=dst, send_sem=ssem.at[ns * a + k],
                                                recv_sem=rsem.at[ns * a + k], device_id=to, device_id_type=MESH)

        def blk(a, b):
            return out_refs[a].at[b]

        def half(a, b, second):
            h = self.arrs[a].shape[0] // 2
            return out_refs[a].at[b, pl.ds(h if second else 0, h)]

        bx, by, bd = num(1 - x, y, c), num(x, 1 - y, c), num(1 - x, 1 - y, c)

        def firsts(a):
            own = blk(a, num(x, y, c))
            return [copy(a, 0, sibling, in_refs[a], own), copy(a, 1, xn, in_refs[a], own),
                    copy(a, 2, yn, in_refs[a], own)]

        def seconds(a):
            return [copy(a, 3, yn, half(a, bx, False), half(a, bx, False)), copy(a, 5, sibling, blk(a, bx), blk(a, bx)),
                    copy(a, 4, xn, half(a, by, True), half(a, by, True)), copy(a, 6, sibling, blk(a, by), blk(a, by))]

        def third(a):
            return copy(a, 7, sibling, blk(a, bd), blk(a, bd))

        @pl.when(step == 0)
        def _():
            for a in arrs:
                for cp in firsts(a):
                    cp.start()

        @pl.when(step == total // 2)
        def _():
            for a in arrs:
                copy(a, 1, xn, blk(a, bx), blk(a, bx)).wait_recv()
                copy(a, 2, yn, blk(a, by), blk(a, by)).wait_recv()
                for cp in seconds(a):
                    cp.start()

        @pl.when(step == (4 * total) // 5)
        def _():
            for a in arrs:
                copy(a, 3, yn, half(a, bd, False), half(a, bd, False)).wait_recv()
                copy(a, 4, xn, half(a, bd, True), half(a, bd, True)).wait_recv()
                third(a).start()

        @pl.when(step == total - 1)
        def _():
            for a in arrs:
                for k, b in ((0, num(x, y, 1 - c)), (5, num(1 - x, y, 1 - c)), (6, num(x, 1 - y, 1 - c)),
                             (7, num(1 - x, 1 - y, 1 - c))):
                    copy(a, k, sibling, blk(a, b), blk(a, b)).wait_recv()
                for cp in firsts(a) + seconds(a) + [third(a)]:
                    cp.wait_send()


def _hook_specs(hook):
    if hook is None:
        return [], [], [], []
    na = len(hook.arrs)
    sems = [pltpu.SemaphoreType.DMA((hook.nsem,)), pltpu.SemaphoreType.DMA((hook.nsem,))]
    return [_ANY] * na, [_ANY] * na, hook.out_shape, sems


def _stage_fwd(st, t, params, inputs, hook=None):
    g, tm = st.g, min(st.tm, t)
    n = t // tm
    npar, nin, ncar, nout = len(params), len(inputs), len(st.carry_shapes), len(st.out_pieces)
    nk = len(st.kept_shapes)
    h_in, h_out, h_shape, h_sems = _hook_specs(hook)
    nh = len(h_in)

    def body(*refs):
        p_refs = refs[:npar]
        x_refs = refs[npar:npar + nin]
        hi_refs = refs[npar + nin:npar + nin + nh]
        o = npar + nin + nh
        o_refs = refs[o:o + nout]
        s_refs = refs[o + nout:o + nout + ncar]
        k_refs = refs[o + nout + ncar:o + nout + ncar + nk]
        o += nout + ncar + nk
        ho_refs = refs[o:o + nh]
        c_scr = refs[o + nh:o + nh + ncar]
        gi, ni = pl.program_id(0), pl.program_id(1)
        if hook is not None:
            step = gi * n + ni
            hook.run(step, g * n, hi_refs, ho_refs, *refs[-2:])

        @pl.when(ni == 0)
        def _():
            for c in c_scr:
                c[...] = jnp.zeros(c.shape, F32)

        ps = [r[...].astype(F32) for r in p_refs]
        xs = _load_pieces(x_refs, st.in_pieces)
        cs = [c[...] for c in c_scr]
        for s, c in zip(s_refs, cs):
            s[...] = c
        res = st.f(ps, xs, cs)
        outs, ncs = res[0], res[1]
        _store_pieces(o_refs, st.out_pieces, outs)
        for c, v in zip(c_scr, ncs):
            c[...] = v
        for kr, kv in zip(k_refs, res[2] if nk else []):
            kr[...] = kv.astype(kr.dtype)

    in_specs = [_par_spec(p, pg, g) for p, pg in zip(params, st.par_per_g)]
    in_specs += [_row_spec(tm, sum(pc), off, n, False) for pc, off in zip(st.in_pieces, st.in_offs)]
    out_specs = [_row_spec(tm, sum(pc), 0, n, False) for pc in st.out_pieces]
    out_specs += [_carry_spec(s, n, False) for s in st.carry_shapes]
    out_specs += [pl.BlockSpec(s, lambda gi, ni: (ni, 0)) for s in st.kept_shapes]
    out_shape = [jax.ShapeDtypeStruct((t, g * sum(pc)), dt) for pc, dt in zip(st.out_pieces, st.out_dtypes)]
    out_shape += [jax.ShapeDtypeStruct((g, n) + s, F32) for s in st.carry_shapes]
    out_shape += [jax.ShapeDtypeStruct((n * s[0], s[1]), BF) for s in st.kept_shapes]
    res = pl.pallas_call(
        body, name=st.name + "_fwd", grid=(g, n), in_specs=in_specs + h_in, out_specs=out_specs + h_out,
        out_shape=out_shape + h_shape,
        scratch_shapes=[pltpu.VMEM(s, F32) for s in st.carry_shapes] + h_sems,
        compiler_params=_cparams(("arbitrary", "arbitrary")),
    )(*params, *inputs, *(hook.arrs if hook else []))
    if hook is not None:
        hook.results = list(res[nout + ncar + nk:])
    return list(res[:nout]), list(res[nout:nout + ncar + nk])


def _stage_bwd(st, t, params, inputs, saved, douts, dx_dtypes, hook=None):
    g, tm = st.g, min(st.tm, t)
    n = t // tm
    npar, nin, ncar = len(params), len(inputs), len(st.carry_shapes)
    nk = len(st.kept_shapes)
    flat_d = [d for ds in douts for d in ds]
    nd = len(flat_d)
    dx_idx = [i for i, dt in enumerate(dx_dtypes) if dt is not None]
    h_in, h_out, h_shape, h_sems = _hook_specs(hook)
    nh = len(h_in)

    def body(*refs):
        p_refs = refs[:npar]
        x_refs = refs[npar:npar + nin]
        s_refs = refs[npar + nin:npar + nin + ncar]
        k_refs = refs[npar + nin + ncar:npar + nin + ncar + nk]
        o = npar + nin + ncar + nk
        d_refs = refs[o:o + nd]
        hi_refs = refs[o + nd:o + nd + nh]
        o += nd + nh
        dp_refs = refs[o:o + npar]
        dx_refs = refs[o + npar:o + npar + len(dx_idx)]
        ho_refs = refs[o + npar + len(dx_idx):o + npar + len(dx_idx) + nh]
        dc_scr = refs[o + npar + len(dx_idx) + nh:o + npar + len(dx_idx) + nh + ncar]
        gi, ni = pl.program_id(0), pl.program_id(1)
        if hook is not None:
            step = gi * n + ni
            hook.run(step, g * n, hi_refs, ho_refs, *refs[-2:])

        @pl.when(ni == 0)
        def _():
            for c in dc_scr:
                c[...] = jnp.zeros(c.shape, F32)

        ps = [r[...].astype(F32) for r in p_refs]
        xs = _load_pieces(x_refs, st.in_pieces)
        cs = [s[...] for s in s_refs]
        dys = []
        k = 0
        for ds, pieces in zip(douts, st.out_pieces):
            acc = _load_pieces([d_refs[k]], [pieces])
            for j in range(1, len(ds)):
                more = _load_pieces([d_refs[k + j]], [pieces])
                acc = [a + b for a, b in zip(acc, more)]
            dys += acc
            k += len(ds)
        if nk:
            kept = [r[...].astype(F32) for r in k_refs]
            _, vjp = jax.vjp(lambda p, x, c: st.f_kept(p, x, c, kept), ps, xs, cs)
        else:
            _, vjp = jax.vjp(st.f, ps, xs, cs)
        dps, dxs, dcs = vjp((dys, [c[...] for c in dc_scr]))
        k = 0
        per_in = []
        for pieces in st.in_pieces:
            per_in.append(dxs[k:k + len(pieces)])
            k += len(pieces)
        for ref, i in zip(dx_refs, dx_idx):
            _store_pieces([ref], [st.in_pieces[i]], per_in[i])
        for c, v in zip(dc_scr, dcs):
            c[...] = v
        for ref, dp, pg in zip(dp_refs, dps, st.par_per_g):
            first = (ni == 0) if pg else ((ni == 0) & (gi == 0))

            @pl.when(first)
            def _():
                ref[...] = jnp.zeros(ref.shape, F32)

            ref[...] += dp

    in_specs = [_par_spec(p, pg, g) for p, pg in zip(params, st.par_per_g)]
    in_specs += [_row_spec(tm, sum(pc), off, n, True) for pc, off in zip(st.in_pieces, st.in_offs)]
    in_specs += [_carry_spec(s, n, True) for s in st.carry_shapes]
    in_specs += [pl.BlockSpec(s, lambda gi, ni: (n - 1 - ni, 0)) for s in st.kept_shapes]
    for ds, pc in zip(douts, st.out_pieces):
        in_specs += [_row_spec(tm, sum(pc), 0, n, True) for _ in ds]
    out_specs = [_par_spec(p, pg, g) for p, pg in zip(params, st.par_per_g)]
    out_specs += [_row_spec(tm, sum(st.in_pieces[i]), 0, n, True) for i in dx_idx]
    out_shape = [jax.ShapeDtypeStruct(p.shape, F32) for p in params]
    out_shape += [jax.ShapeDtypeStruct((t, g * sum(st.in_pieces[i])), dx_dtypes[i]) for i in dx_idx]
    res = pl.pallas_call(
        body, name=st.name + "_bwd", grid=(g, n), in_specs=in_specs + h_in, out_specs=out_specs + h_out,
        out_shape=out_shape + h_shape,
        scratch_shapes=[pltpu.VMEM(s, F32) for s in st.carry_shapes] + h_sems,
        compiler_params=_cparams(("arbitrary", "arbitrary")),
    )(*params, *inputs, *saved, *flat_d, *(hook.arrs if hook else []))
    if hook is not None:
        hook.results = list(res[npar + len(dx_idx):])
    return list(res[:npar]), list(res[npar:npar + len(dx_idx)])


def _pick(n, cap):
    if n <= cap:
        return n
    best = LANES
    for k in range(1, n // LANES + 1):
        if (n // LANES) % k == 0 and k * LANES <= cap:
            best = k * LANES
    return best


def _mm(name, a, b, mode, out_dtype=F32, tm=1024, tn=512, b_outer=False, token=None):
    m = a.shape[1] if mode == "tn" else a.shape[0]
    k = a.shape[0] if mode == "tn" else a.shape[1]
    n = b.shape[0] if mode == "nt" else b.shape[1]
    tm, tn = _pick(m, tm), _pick(n, tn)
    if b_outer:
        grid = (n // tn, m // tm)
        ij = lambda p, q: (q, p)
    else:
        grid = (m // tm, n // tn)
        ij = lambda p, q: (p, q)
    extra = [] if token is None else [token]

    def body(*refs):
        a_ref, b_ref, o_ref = refs[0], refs[1], refs[-1]
        o_ref[...] = _raw_dot(a_ref[...], b_ref[...], mode).astype(o_ref.dtype)

    if mode == "tn":
        a_spec = pl.BlockSpec((k, tm), lambda p, q: (0, ij(p, q)[0]))
    else:
        a_spec = pl.BlockSpec((tm, k), lambda p, q: (ij(p, q)[0], 0))
    b_mode = dict(pipeline_mode=pl.Buffered(1)) if tn == n else {}
    if mode == "nt":
        b_spec = pl.BlockSpec((tn, k), lambda p, q: (ij(p, q)[1], 0), **b_mode)
    else:
        b_spec = pl.BlockSpec((k, tn), lambda p, q: (0, ij(p, q)[1]), **b_mode)
    return pl.pallas_call(
        body, name=name, grid=grid,
        in_specs=[a_spec, b_spec] + [pl.BlockSpec(e.shape, lambda p, q: (0, 0)) for e in extra],
        out_specs=pl.BlockSpec((tm, tn), lambda p, q: ij(p, q)),
        out_shape=jax.ShapeDtypeStruct((m, n), out_dtype),
        compiler_params=_cparams(("arbitrary", "arbitrary")),
    )(a, b, *extra)


def _loss_stage(t, g_post, h1, ff, tgt):
    tm = min(256, t)
    n = t // tm

    def body(g_ref, h_ref, f_ref, t_ref, loss_ref, dg_ref, dh_ref, df_ref):
        ni = pl.program_id(0)
        target = t_ref[...]

        def lossf(g, h1, ff):
            e = h1 + _rms(ff, g) - target
            return 0.5 * jnp.sum(jnp.mean(e * e, axis=-1))

        l, (dg, dh, df) = jax.value_and_grad(lossf, argnums=(0, 1, 2))(g_ref[...], h_ref[...], f_ref[...])

        @pl.when(ni == 0)
        def _():
            loss_ref[...] = jnp.zeros(loss_ref.shape, F32)
            dg_ref[...] = jnp.zeros(dg_ref.shape, F32)

        loss_ref[...] += jnp.full(loss_ref.shape, l, F32)
        dg_ref[...] += dg
        dh_ref[...] = dh
        df_ref[...] = df.astype(df_ref.dtype)

    row = pl.BlockSpec((tm, D), lambda ni: (ni, 0))
    one = pl.BlockSpec((1, D), lambda ni: (0, 0))
    return pl.pallas_call(
        body, name="loss_head", grid=(n,), in_specs=[one, row, row, row],
        out_specs=[pl.BlockSpec((1, LANES), lambda ni: (0, 0)), one, row, row],
        out_shape=[jax.ShapeDtypeStruct((1, LANES), F32), jax.ShapeDtypeStruct((1, D), F32),
                   jax.ShapeDtypeStruct((t, D), F32), jax.ShapeDtypeStruct((t, D), BF)],
        compiler_params=_cparams(("arbitrary",)),
    )(g_post, h1, ff, tgt)


_ANY = pl.BlockSpec(memory_space=pl.ANY)


def _all_gather(name, blks):
    na = len(blks)
    ns = 8

    def body(*refs):
        x_refs, out_refs = refs[:na], refs[na:2 * na]
        send_sems, recv_sems, local_sems = refs[2 * na:]
        x, y, cc = lax.axis_index("x"), lax.axis_index("y"), lax.axis_index("c")
        sibling, xn, yn = (x, y, 1 - cc), (1 - x, y, cc), (x, 1 - y, cc)

        def num(px, py, pc):
            return 4 * px + 2 * py + pc

        def copy(a, k, to, src, dst):
            return pltpu.make_async_remote_copy(src_ref=src, dst_ref=dst, send_sem=send_sems.at[ns * a + k],
                                                recv_sem=recv_sems.at[ns * a + k], device_id=to, device_id_type=MESH)

        def halves(a, blk):
            h = blks[a].shape[0] // 2
            return out_refs[a].at[blk, pl.ds(0, h)], out_refs[a].at[blk, pl.ds(h, h)]

        mine, sends = [], []
        for a in range(na):
            o = out_refs[a]
            m = pltpu.make_async_copy(x_refs[a], o.at[num(x, y, cc)], local_sems.at[a])
            m.start()
            mine.append(m)
            own = o.at[num(x, y, cc)]
            sends.append([copy(a, 0, sibling, x_refs[a], own), copy(a, 1, xn, x_refs[a], own),
                          copy(a, 2, yn, x_refs[a], own)])
            for cp in sends[a]:
                cp.start()
        for a in range(na):
            o = out_refs[a]
            bx, by, bd = num(1 - x, y, cc), num(x, 1 - y, cc), num(1 - x, 1 - y, cc)
            copy(a, 1, xn, o.at[bx], o.at[bx]).wait_recv()
            more = [copy(a, 3, yn, halves(a, bx)[0], halves(a, bx)[0]), copy(a, 5, sibling, o.at[bx], o.at[bx])]
            for cp in more:
                cp.start()
            sends[a] += more
        for a in range(na):
            o = out_refs[a]
            bx, by, bd = num(1 - x, y, cc), num(x, 1 - y, cc), num(1 - x, 1 - y, cc)
            copy(a, 2, yn, o.at[by], o.at[by]).wait_recv()
            more = [copy(a, 4, xn, halves(a, by)[1], halves(a, by)[1]), copy(a, 6, sibling, o.at[by], o.at[by])]
            for cp in more:
                cp.start()
            sends[a] += more
        for a in range(na):
            o = out_refs[a]
            bd = num(1 - x, 1 - y, cc)
            copy(a, 3, yn, halves(a, bd)[0], halves(a, bd)[0]).wait_recv()
            copy(a, 4, xn, halves(a, bd)[1], halves(a, bd)[1]).wait_recv()
            fw = copy(a, 7, sibling, o.at[bd], o.at[bd])
            fw.start()
            sends[a].append(fw)
        for a in range(na):
            o = out_refs[a]
            for k, blk in ((0, num(x, y, 1 - cc)), (5, num(1 - x, y, 1 - cc)), (6, num(x, 1 - y, 1 - cc)),
                           (7, num(1 - x, 1 - y, 1 - cc))):
                copy(a, k, sibling, o.at[blk], o.at[blk]).wait_recv()
            for cp in sends[a]:
                cp.wait_send()
        for m in mine:
            m.wait()

    res = pl.pallas_call(
        body, name=name, in_specs=[_ANY] * na, out_specs=[_ANY] * na,
        out_shape=[jax.ShapeDtypeStruct((N_DEV,) + b.shape, b.dtype) for b in blks],
        scratch_shapes=[pltpu.SemaphoreType.DMA((ns * na,)), pltpu.SemaphoreType.DMA((ns * na,)),
                        pltpu.SemaphoreType.DMA((na,))],
    )(*blks)
    return list(res)


def _reduce_pair(g8s):
    na = len(g8s)

    def body(*refs):
        g_refs, recv_refs = refs[:na], refs[na:2 * na]
        ssem, rsem = refs[2 * na:]
        x, y, cc = lax.axis_index("x"), lax.axis_index("y"), lax.axis_index("c")
        chips = [(x, y), (1 - x, y), (x, 1 - y), (1 - x, 1 - y)]
        sib = (x, y, 1 - cc)
        for a in range(na):
            for k, (cx, cy) in enumerate(chips):
                pltpu.make_async_remote_copy(
                    src_ref=g_refs[a].at[4 * cx + 2 * cy + 1 - cc], dst_ref=recv_refs[a].at[k],
                    send_sem=ssem.at[a], recv_sem=rsem.at[a], device_id=sib, device_id_type=MESH).start()
        for a in range(na):
            pltpu.make_async_remote_copy(src_ref=recv_refs[a], dst_ref=recv_refs[a], send_sem=ssem.at[a],
                                         recv_sem=rsem.at[a], device_id=sib, device_id_type=MESH).wait()

    res = pl.pallas_call(
        body, name="reduce_pair", in_specs=[_ANY] * na, out_specs=[_ANY] * na,
        out_shape=[jax.ShapeDtypeStruct((4,) + g.shape[1:], g.dtype) for g in g8s],
        scratch_shapes=[pltpu.SemaphoreType.DMA((na,)), pltpu.SemaphoreType.DMA((na,))],
    )(*g8s)
    return list(res)


_HBM = pl.BlockSpec(memory_space=pltpu.HBM)
_SEM = pl.BlockSpec(memory_space=pltpu.SEMAPHORE)
_EFFECT = pltpu.SideEffectType.DATAFLOW_SIDE_EFFECTING


def _chip_swap_copies(s_refs, land_refs, ssem, rsem):
    x, y, c = lax.axis_index("x"), lax.axis_index("y"), lax.axis_index("c")
    targets = [(1 - x, y, c), (x, 1 - y, c), (1 - x, 1 - y, c)]
    return [pltpu.make_async_remote_copy(src_ref=s.at[k], dst_ref=d.at[k], send_sem=ssem.at[3 * a + k],
                                         recv_sem=rsem.at[3 * a + k], device_id=targets[k], device_id_type=MESH)
            for a, (s, d) in enumerate(zip(s_refs, land_refs)) for k in range(3)]


def _chip_swap_start(sends):
    na = len(sends)

    def body(*refs):
        cps = _chip_swap_copies(refs[:na], refs[na:2 * na], refs[2 * na], refs[2 * na + 1])
        for cp in cps:
            cp.start()
        token = refs[-1]
        token[...] = jnp.zeros(token.shape, token.dtype)

    bufs = [pltpu.HBM(s.shape, s.dtype) for s in sends]
    res = pl.pallas_call(
        body, name="chip_swap_start",
        out_shape=[pltpu.SemaphoreType.DMA((3 * na,)), pltpu.SemaphoreType.DMA((3 * na,))] + bufs + bufs
        + [jax.ShapeDtypeStruct((8, LANES), F32)],
        in_specs=[_HBM] * (2 * na), out_specs=[_SEM, _SEM] + [_HBM] * (2 * na) + [pl.BlockSpec(memory_space=pltpu.VMEM)],
        input_output_aliases={i: 2 + i for i in range(2 * na)},
        compiler_params=pltpu.CompilerParams(has_side_effects=_EFFECT),
    )(*[pltpu.with_memory_space_constraint(s, pltpu.HBM) for s in sends],
      *[pltpu.with_memory_space_constraint(lax.empty(s.shape, s.dtype), pltpu.HBM) for s in sends])
    return res[0], res[1], list(res[2:2 + na]), list(res[2 + na:2 + 2 * na]), res[-1]


def _chip_swap_wait(ssem, rsem, srcs, lands, after):
    na = len(srcs)

    def body(*refs):
        cps = _chip_swap_copies(refs[:na], refs[na:2 * na], refs[2 * na], refs[2 * na + 1])
        for cp in cps:
            cp.wait_send()
            cp.wait_recv()

    bufs = [pltpu.HBM(s.shape, s.dtype) for s in srcs]
    res = pl.pallas_call(
        body, name="chip_swap_wait", out_shape=bufs + bufs,
        in_specs=[_HBM] * (2 * na) + [_SEM, _SEM, _ANY], out_specs=[_HBM] * (2 * na),
        input_output_aliases={i: i for i in range(2 * na)},
        compiler_params=pltpu.CompilerParams(has_side_effects=_EFFECT),
    )(*srcs, *lands, ssem, rsem, after)
    return list(res[na:])


def _pick_rows(r, c, budget=TILE_BYTES):
    if r * c * 4 <= budget or r % 16:
        return r
    best = 16
    for tr in range(16, r, 16):
        if r % tr == 0 and tr * c * 4 <= budget:
            best = tr
    return best


def _pair_sum(name, idx4, g8, recv4):
    _, r, c = g8.shape
    tr = _pick_rows(r, c, 2 * TILE_BYTES)

    def body(idx_ref, a_ref, b_ref, o0_ref, o3_ref):
        k = pl.program_id(1)
        s = a_ref[...].astype(F32) + b_ref[...].astype(F32)

        @pl.when(k == 0)
        def _():
            o0_ref[...] = s

        @pl.when(k > 0)
        def _():
            o3_ref[...] = s.astype(BF)

    spec = pltpu.PrefetchScalarGridSpec(
        num_scalar_prefetch=1, grid=(r // tr, 4),
        in_specs=[pl.BlockSpec((None, tr, c), lambda i, k, idx: (idx[k], i, 0)),
                  pl.BlockSpec((None, tr, c), lambda i, k, idx: (k, i, 0))],
        out_specs=[pl.BlockSpec((tr, c), lambda i, k, idx: (i, 0)),
                   pl.BlockSpec((None, tr, c), lambda i, k, idx: (jnp.maximum(k - 1, 0), i, 0))])
    return pl.pallas_call(
        body, name=name, grid_spec=spec,
        out_shape=[jax.ShapeDtypeStruct((r, c), F32), jax.ShapeDtypeStruct((3, r, c), BF)],
        compiler_params=_cparams(("arbitrary", "arbitrary")),
    )(idx4, g8, recv4)


def _adamw(w, g, m, v):
    m = ADAM_B1 * m + (1.0 - ADAM_B1) * g
    v = ADAM_B2 * v + (1.0 - ADAM_B2) * jnp.square(g)
    m_hat = m / (1.0 - ADAM_B1 ** ADAM_STEP)
    v_hat = v / (1.0 - ADAM_B2 ** ADAM_STEP)
    delta = -ADAM_LR * (m_hat / (jnp.sqrt(v_hat) + ADAM_EPS) + ADAM_WD * w)
    return delta, m, v


def _sum_partials(name, idx1, own, recv):
    _, r, c = own.shape
    tr = _pick_rows(r, c, 2 * TILE_BYTES)
    nj = recv.shape[0]

    def body(idx_ref, p_ref, r_ref, g_out):
        g = p_ref[...].astype(F32)
        for k in range(nj):
            g = g + r_ref[k].astype(F32)
        g_out[...] = g

    row = pl.BlockSpec((tr, c), lambda i, idx: (i, 0))
    spec = pltpu.PrefetchScalarGridSpec(
        num_scalar_prefetch=1, grid=(r // tr,),
        in_specs=[pl.BlockSpec((None, tr, c), lambda i, idx: (idx[0], i, 0)),
                  pl.BlockSpec((nj, tr, c), lambda i, idx: (0, i, 0))],
        out_specs=row)
    return pl.pallas_call(body, name=name, grid_spec=spec, out_shape=jax.ShapeDtypeStruct((r, c), F32),
                          compiler_params=_cparams(("arbitrary",)))(idx1, own, recv)


def _adam_sharded(name, idx1, own, recv, w, m, v):
    r, c = w.shape
    tr = _pick_rows(r, c)
    nj = 0 if recv is None else recv.shape[0]
    if recv is None:
        recv = jnp.zeros((1, 8, LANES), BF)

    def body(idx_ref, p_ref, r_ref, w_ref, m_ref, v_ref, g_out, d_out, m_out, v_out):
        g = p_ref[...].astype(F32)
        for k in range(nj):
            g = g + r_ref[k].astype(F32)
        d, mn, vn = _adamw(w_ref[...], g, m_ref[...], v_ref[...])
        g_out[...] = g
        d_out[...] = d
        m_out[...] = mn
        v_out[...] = vn

    row = pl.BlockSpec((tr, c), lambda i, idx: (i, 0))
    if nj:
        recv_spec = pl.BlockSpec((nj, tr, c), lambda i, idx: (0, i, 0))
    else:
        recv_spec = pl.BlockSpec(recv.shape, lambda i, idx: (0, 0, 0))
    spec = pltpu.PrefetchScalarGridSpec(
        num_scalar_prefetch=1, grid=(r // tr,),
        in_specs=[pl.BlockSpec((None, tr, c), lambda i, idx: (idx[0], i, 0)), recv_spec, row, row, row],
        out_specs=[row] * 4)
    return pl.pallas_call(
        body, name=name, grid_spec=spec, out_shape=[jax.ShapeDtypeStruct((r, c), F32)] * 4,
        compiler_params=_cparams(("arbitrary",)),
    )(idx1, own, recv, w, m, v)


def _repl_rows():
    rows, r = {}, 0
    for name, cols in REPL:
        rows[name] = r
        r += REPL_ROWS.get(name, 1) * ((cols + D - 1) // D)
    return rows


LOSS_ROW = 24


def _pack_replicated(grads, loss_acc):
    rows = _repl_rows()
    names = [n for n, _ in REPL]

    def body(*refs):
        o_ref = refs[-1]
        o_ref[...] = jnp.zeros(o_ref.shape, F32)
        o_ref[LOSS_ROW:LOSS_ROW + 1, 0:LANES] = refs[-2][...]
        for name, ref in zip(names, refs[:-2]):
            r0 = rows[name]
            nr, nc = ref.shape
            if nc <= D:
                o_ref[r0:r0 + nr, 0:nc] = ref[...]
            else:
                for j in range((nc + D - 1) // D):
                    lo, hi = j * D, min(nc, (j + 1) * D)
                    o_ref[r0 + j:r0 + j + 1, 0:hi - lo] = ref[:, lo:hi]

    return pl.pallas_call(body, name="pack_replicated", out_shape=jax.ShapeDtypeStruct((REPL_TOTAL, D), F32),
                          compiler_params=_cparams())(*[grads[n] for n in names], loss_acc)


def _adam_replicated(g8, ws, ms, vs):
    rows = _repl_rows()
    names = [n for n, _ in REPL]
    np_ = len(names)

    def body(*refs):
        g_ref = refs[0]
        w_refs, m_refs, v_refs = refs[1:1 + np_], refs[1 + np_:1 + 2 * np_], refs[1 + 2 * np_:1 + 3 * np_]
        outs = refs[1 + 3 * np_:1 + 7 * np_]
        scr = refs[-1]
        g = g_ref[0]
        for k in range(1, N_DEV):
            g = g + g_ref[k]
        scr[...] = g
        refs[1 + 7 * np_][...] = scr[LOSS_ROW:LOSS_ROW + 1, 0:LANES]
        for i, name in enumerate(names):
            r0 = rows[name]
            nr, nc = w_refs[i].shape
            if nc <= D:
                gi = scr[r0:r0 + nr, 0:nc]
            else:
                parts = []
                for j in range((nc + D - 1) // D):
                    lo, hi = j * D, min(nc, (j + 1) * D)
                    parts.append(scr[r0 + j:r0 + j + 1, 0:hi - lo])
                gi = jnp.concatenate(parts, axis=1)
            d, mn, vn = _adamw(w_refs[i][...], gi, m_refs[i][...], v_refs[i][...])
            outs[i][...] = gi
            outs[np_ + i][...] = d
            outs[2 * np_ + i][...] = mn
            outs[3 * np_ + i][...] = vn

    shp = [jax.ShapeDtypeStruct(w.shape, F32) for w in ws]
    res = pl.pallas_call(body, name="adam_replicated", out_shape=shp * 4 + [jax.ShapeDtypeStruct((1, LANES), F32)],
                         scratch_shapes=[pltpu.VMEM((REPL_TOTAL, D), F32)], compiler_params=_cparams(),
                         )(g8, *ws, *ms, *vs)
    return [dict(zip(names, res[k * np_:(k + 1) * np_])) for k in range(4)], res[-1]


_WEIGHTS = ("attn_pre_norm", "w_in", "hgrn_lb", "hgrn_gnorm", "w_branch_a", "rwkv_mu", "rwkv_w0", "rwkv_w2",
            "rwkv_a0", "rwkv_a2", "rwkv_g2", "rwkv_k_k", "rwkv_k_a", "rwkv_r_k", "rwkv_ln_w", "rwkv_ln_b",
            "w_branch_b", "w_out", "attn_post_norm", "ffn_pre_norm", "w_up", "conv_w", "conv_b", "w_down",
            "ffn_post_norm")
_BIG = ("w_in", "w_up", "w_down", "w_branch_a", "w_branch_b", "w_out")


def _stages():
    one = [D]
    hw = HG_K * HG_PER_STEP
    rw = LANES * RW_PAIRS_PER_STEP
    return dict(
        pre1=_Stage("pre1", _f_pre1, 1, 256, [False], [one], [0], [], [one], [BF]),
        pre1_res=_Stage("pre1", _f_pre1_residual, 1, 256, [False], [one], [0], [], [one, one], [BF, F32]),
        mixers=_Stage("mixers", _f_mixers, 1, 2 * RW_CHUNK, [False] * 13, [[D] * 7 + [LANES, LANES]], [0],
                      [(hw, HG_K), (1, RW_COLS), (rw, LANES)], [one, one], [BF, BF],
                      kept_shapes=[(2 * RW_PAIRS_PER_STEP * 2 * RW_CHUNK, LANES)], f_kept=_f_mixers_kept),
        merge=_Stage("merge", _f_merge, 4, 512, [], [[256]] * 4, [29, 33, 0, 0], [], [[256]], [BF]),
        post1=_Stage("post1", _f_post1, 1, 256, [False, False], [one, one], [0, 0], [], [one, one], [F32, BF]),
        conv=_Stage("conv", _f_conv, 1, 128, [False, False], [[DFF, DFF]], [0], [(1, 2 * DFF), (1, 2 * DFF)],
                    [[DFF]], [BF]),
    )


def _cols_to_blocks(w, per):
    return w.reshape(w.shape[0], N_DEV, per).transpose(1, 0, 2)


def _blocks_to_cols(g):
    return g.transpose(1, 0, 2).reshape(g.shape[1], N_DEV * g.shape[2])


def kernel(x, attn_pre_norm, w_in, hgrn_lb, hgrn_gnorm, w_branch_a, rwkv_mu, rwkv_w0, rwkv_w2, rwkv_a0, rwkv_a2, rwkv_g2, rwkv_k_k, rwkv_k_a, rwkv_r_k, rwkv_ln_w, rwkv_ln_b, w_branch_b, w_out, attn_post_norm, ffn_pre_norm, w_up, conv_w, conv_b, w_down, ffn_post_norm, loss_target, m_attn_pre_norm, m_w_in, m_hgrn_lb, m_hgrn_gnorm, m_w_branch_a, m_rwkv_mu, m_rwkv_w0, m_rwkv_w2, m_rwkv_a0, m_rwkv_a2, m_rwkv_g2, m_rwkv_k_k, m_rwkv_k_a, m_rwkv_r_k, m_rwkv_ln_w, m_rwkv_ln_b, m_w_branch_b, m_w_out, m_attn_post_norm, m_ffn_pre_norm, m_w_up, m_conv_w, m_conv_b, m_w_down, m_ffn_post_norm, v_attn_pre_norm, v_w_in, v_hgrn_lb, v_hgrn_gnorm, v_w_branch_a, v_rwkv_mu, v_rwkv_w0, v_rwkv_w2, v_rwkv_a0, v_rwkv_a2, v_rwkv_g2, v_rwkv_k_k, v_rwkv_k_a, v_rwkv_r_k, v_rwkv_ln_w, v_rwkv_ln_b, v_w_branch_b, v_w_out, v_attn_post_norm, v_ffn_pre_norm, v_w_up, v_conv_w, v_conv_b, v_w_down, v_ffn_post_norm):
    w = dict(attn_pre_norm=attn_pre_norm, w_in=w_in, hgrn_lb=hgrn_lb, hgrn_gnorm=hgrn_gnorm, w_branch_a=w_branch_a, rwkv_mu=rwkv_mu, rwkv_w0=rwkv_w0, rwkv_w2=rwkv_w2, rwkv_a0=rwkv_a0, rwkv_a2=rwkv_a2, rwkv_g2=rwkv_g2, rwkv_k_k=rwkv_k_k, rwkv_k_a=rwkv_k_a, rwkv_r_k=rwkv_r_k, rwkv_ln_w=rwkv_ln_w, rwkv_ln_b=rwkv_ln_b, w_branch_b=w_branch_b, w_out=w_out, attn_post_norm=attn_post_norm, ffn_pre_norm=ffn_pre_norm, w_up=w_up, conv_w=conv_w, conv_b=conv_b, w_down=w_down, ffn_post_norm=ffn_post_norm)
    mo = dict(attn_pre_norm=m_attn_pre_norm, w_in=m_w_in, hgrn_lb=m_hgrn_lb, hgrn_gnorm=m_hgrn_gnorm, w_branch_a=m_w_branch_a, rwkv_mu=m_rwkv_mu, rwkv_w0=m_rwkv_w0, rwkv_w2=m_rwkv_w2, rwkv_a0=m_rwkv_a0, rwkv_a2=m_rwkv_a2, rwkv_g2=m_rwkv_g2, rwkv_k_k=m_rwkv_k_k, rwkv_k_a=m_rwkv_k_a, rwkv_r_k=m_rwkv_r_k, rwkv_ln_w=m_rwkv_ln_w, rwkv_ln_b=m_rwkv_ln_b, w_branch_b=m_w_branch_b, w_out=m_w_out, attn_post_norm=m_attn_post_norm, ffn_pre_norm=m_ffn_pre_norm, w_up=m_w_up, conv_w=m_conv_w, conv_b=m_conv_b, w_down=m_w_down, ffn_post_norm=m_ffn_post_norm)
    vo = dict(attn_pre_norm=v_attn_pre_norm, w_in=v_w_in, hgrn_lb=v_hgrn_lb, hgrn_gnorm=v_hgrn_gnorm, w_branch_a=v_w_branch_a, rwkv_mu=v_rwkv_mu, rwkv_w0=v_rwkv_w0, rwkv_w2=v_rwkv_w2, rwkv_a0=v_rwkv_a0, rwkv_a2=v_rwkv_a2, rwkv_g2=v_rwkv_g2, rwkv_k_k=v_rwkv_k_k, rwkv_k_a=v_rwkv_k_a, rwkv_r_k=v_rwkv_r_k, rwkv_ln_w=v_rwkv_ln_w, rwkv_ln_b=v_rwkv_ln_b, w_branch_b=v_w_branch_b, w_out=v_w_out, attn_post_norm=v_attn_post_norm, ffn_pre_norm=v_ffn_pre_norm, w_up=v_w_up, conv_w=v_conv_w, conv_b=v_conv_b, w_down=v_w_down, ffn_post_norm=v_ffn_post_norm)

    t = x.shape[1]
    x2 = x.reshape(t, D)
    tgt = loss_target.reshape(t, D)
    st = _stages()

    me = 4 * lax.axis_index("x") + 2 * lax.axis_index("y") + lax.axis_index("c")
    small = jnp.concatenate([rwkv_w2[0], rwkv_a2[0], rwkv_g2[0]], axis=0).astype(BF)
    g_in, g_small = _all_gather("gather_weights", [w_in[0].T.astype(BF), small])
    fw_in_t = g_in.reshape(IN_COLS, D)
    z64 = jnp.zeros((64, D), BF)
    w2p = jnp.concatenate([_blocks_to_cols(g_small[:, 0:64]), z64], axis=0)
    a2p = jnp.concatenate([z64, _blocks_to_cols(g_small[:, 64:128])], axis=0)
    g2f = _blocks_to_cols(g_small[:, 128:256])
    conv_bits = jnp.pad(lax.bitcast_convert_type(conv_w[0], BF).reshape(3, 2 * 704), ((0, 29), (0, 0)))
    late = [w_up[0].T.astype(BF)] + [w[k][0].astype(BF) for k in _BIG[2:]] + [conv_bits]
    late_gather = _Exchange("gather2", late)
    r_k = rwkv_r_k.reshape(1, D)

    (xn,), _ = _stage_fwd(st["pre1"], t, [attn_pre_norm], [x2])
    z = _mm("in_proj", xn, fw_in_t, "nt", F32, tm=512, tn=4736, b_outer=True)
    mix_par = [hgrn_lb, hgrn_gnorm, rwkv_mu, rwkv_w0, w2p, rwkv_a0, a2p, g2f, rwkv_k_k, rwkv_k_a,
               rwkv_ln_w, rwkv_ln_b, r_k]
    mix_in = [z]
    (o_a, o_b), mix_saved = _stage_fwd(st["mixers"], t, mix_par, mix_in, hook=late_gather)
    gl = [lax.dynamic_update_slice(g, own[None], (me, 0, 0)) for g, own in zip(late_gather.results, late)]
    fw_up_t = gl[0].reshape(2 * DFF, D)
    fw_down = gl[1].reshape(DFF, D)
    fw_a, fw_b, fw_out = (g.reshape(D, D) for g in gl[2:5])
    conv_full = _blocks_to_cols(lax.bitcast_convert_type(gl[5][:, :3].reshape(N_DEV, 3, 704, 2), F32))
    y_a = _mm("branch_a", o_a, fw_a, "nn")
    y_b = _mm("branch_b", o_b, fw_b, "nn")
    (merged,), _ = _stage_fwd(st["merge"], t, [], [z, z, y_a, y_b])
    mix = _mm("out_proj", merged, fw_out, "nn")
    (h1, xn2), _ = _stage_fwd(st["post1"], t, [attn_post_norm, ffn_pre_norm], [x2, mix])
    hu = _mm("up_proj", xn2, fw_up_t, "nt", F32, tm=1024, tn=1408)
    conv_par = [conv_full, conv_b]
    (act,), conv_saved = _stage_fwd(st["conv"], t, conv_par, [hu])
    ff = _mm("down_proj", act, fw_down, "nn")

    loss_acc, d_ffn_post, dh1, dff = _loss_stage(t, ffn_post_norm, h1, ff, tgt)
    dact = _mm("d_act", dff, fw_down, "nt", BF, tm=1024, tn=1408)
    dw_down = _mm("dw_down", act, dff, "tn", BF, tm=1408, tn=512)
    (dcw, dcb), (dhu,) = _stage_bwd(st["conv"], t, conv_par, [hu], conv_saved, [[dact]], [BF])
    dxn2 = _mm("d_xn2", dhu, fw_up_t, "nn", F32, tm=1024, tn=1024)
    dw_up_t = _mm("dw_up", dhu, xn2, "tn", BF, tm=1408, tn=1024)
    (d_post, d_pre2), (dx_a, dmix) = _stage_bwd(st["post1"], t, [attn_post_norm, ffn_pre_norm], [x2, mix], [],
                                                 [[dh1], [dxn2]], [F32, BF])
    dmerged = _mm("d_merged", dmix, fw_out, "nt", BF)
    dw_out = _mm("dw_out", merged, dmix, "tn", BF)
    _, (dga, dgb, dy_a, dy_b) = _stage_bwd(st["merge"], t, [], [z, z, y_a, y_b], [], [[dmerged]], [BF, BF, BF, BF])
    do_a = _mm("d_oa", dy_a, fw_a, "nt", BF)
    dw_a = _mm("dw_a", o_a, dy_a, "tn", BF)
    do_b = _mm("d_ob", dy_b, fw_b, "nt", BF)
    dw_b = _mm("dw_b", o_b, dy_b, "tn", BF)
    early = [dw_up_t.reshape(N_DEV, 704, D), dw_down.reshape(N_DEV, 352, D), dw_a.reshape(N_DEV, 128, D),
             dw_b.reshape(N_DEV, 128, D), dw_out.reshape(N_DEV, 128, D), _cols_to_blocks(dcw.astype(BF), 704)]
    early_scatter = _Exchange("scatter", early)
    mix_dp, dz_hr = _stage_bwd(st["mixers"], t, mix_par, mix_in, mix_saved, [[do_a], [do_b]], [BF],
                               hook=early_scatter)
    d_lb, d_gn, d_mu, d_w0, d_w2p, d_a0, d_a2p, d_g2, d_kk, d_ka, d_lnw, d_lnb, d_rk = mix_dp
    dz = jnp.concatenate(dz_hr + [dga, dgb], axis=1)
    dw_in_t = _mm("dw_in", dz, xn, "tn", BF, tm=256, tn=1024)

    ax, ay, ac = lax.axis_index("x"), lax.axis_index("y"), lax.axis_index("c")
    idx4 = jnp.stack([4 * cx + 2 * cy + ac for cx, cy in ((ax, ay), (1 - ax, ay), (ax, 1 - ay), (1 - ax, 1 - ay))])
    idx4 = idx4.astype(jnp.int32)
    idx_me, idx_0 = idx4[0:1], jnp.zeros((1,), jnp.int32)
    d_small = jnp.concatenate([d_w2p[:64], d_a2p[64:], d_g2], axis=0).astype(BF)
    g8s = [dw_in_t.reshape(N_DEV, 1184, D), _cols_to_blocks(d_small, LANES)]
    recv4s = _reduce_pair(g8s)
    sums = [_pair_sum("pair_sum_" + n, idx4, g, r) for n, g, r in zip(("w_in", "small"), g8s, recv4s)]
    swap_ssem, swap_rsem, swap_srcs, swap_lands, token = _chip_swap_start([s[1] for s in sums])
    dxn = _mm("d_xn", dz, fw_in_t, "nn", BF, tm=512, tn=1024, token=token)
    (d_pre1,), (dx,) = _stage_bwd(st["pre1_res"], t, [attn_pre_norm], [x2], [], [[dxn], [dx_a]], [F32])
    grad_x = dx.reshape(x.shape)

    rg = dict(attn_pre_norm=d_pre1, hgrn_lb=d_lb, hgrn_gnorm=d_gn, rwkv_mu=d_mu, rwkv_w0=d_w0, rwkv_a0=d_a0,
              rwkv_k_k=d_kk, rwkv_k_a=d_ka, rwkv_r_k=d_rk, rwkv_ln_w=d_lnw, rwkv_ln_b=d_lnb, attn_post_norm=d_post,
              ffn_pre_norm=d_pre2, conv_b=dcb, ffn_post_norm=d_ffn_post)
    (g8,) = _all_gather("gather_small_grads", [_pack_replicated(rg, loss_acc)])
    rnames = [n for n, _ in REPL]
    flat = lambda src: [src[n].reshape(1, D) if n == "rwkv_r_k" else src[n] for n in rnames]
    rp_out, loss_row = _adam_replicated(g8, flat(w), flat(mo), flat(vo))
    loss = loss_row[0, 0]
    recv3s = _chip_swap_wait(swap_ssem, swap_rsem, swap_srcs, swap_lands, rp_out[0]["attn_pre_norm"])
    for kind in range(4):
        rp_out[kind]["rwkv_r_k"] = rp_out[kind]["rwkv_r_k"].reshape(rwkv_r_k.shape)

    def small_of(src):
        return jnp.concatenate([src["rwkv_w2"][0], src["rwkv_a2"][0], src["rwkv_g2"][0]], axis=0)

    sh_out = [dict() for _ in range(4)]
    g_in = _sum_partials("sum_w_in", idx_0, sums[0][0][None], recv3s[0]).T
    res = _adam_sharded("adam_w_in", idx_0, g_in[None], None, *[src["w_in"][0] for src in (w, mo, vo)])
    res_s = _adam_sharded("adam_small", idx_0, sums[1][0][None], recv3s[1], *[small_of(src) for src in (w, mo, vo)])
    for kind in range(4):
        sh_out[kind]["w_in"] = res[kind][None]
        sh_out[kind]["rwkv_w2"] = res_s[kind][0:64][None]
        sh_out[kind]["rwkv_a2"] = res_s[kind][64:128][None]
        sh_out[kind]["rwkv_g2"] = res_s[kind][128:256][None]
    for n, own, recv in zip(_BIG[1:] + ("conv_w",), early, early_scatter.results):
        if n == "w_up":
            g_up = _sum_partials("sum_w_up", idx_me, own, recv).T
            res = _adam_sharded("adam_" + n, idx_0, g_up[None], None, *[src[n][0] for src in (w, mo, vo)])
        else:
            res = _adam_sharded("adam_" + n, idx_me, own, recv, *[src[n][0] for src in (w, mo, vo)])
        for kind in range(4):
            sh_out[kind][n] = res[kind][None]

    outs = [loss, grad_x]
    for kind in range(4):
        for name in _WEIGHTS:
            outs.append(sh_out[kind][name] if name in sh_out[kind] else rp_out[kind][name])
    return tuple(outs)
```

```python
import functools

import jax
import jax.numpy as jnp
from jax import lax
from jax.experimental import pallas as pl
from jax.experimental.pallas import tpu as pltpu

F32 = jnp.float32
BF = jnp.bfloat16
MESH = pl.DeviceIdType.MESH

D = 1024
HG_HEADS = 8
HG_K = 128
HG_CHUNK = 32
HG_SCALE = HG_K ** -0.5
HG_PER_STEP = 8
RW_HEADS = 16
RW_N = 64
RW_CHUNK = 64
RW_PAIRS_PER_STEP = 8
DFF = 2816
IN_COLS = 9472
RW_COLS = 3328
EPS = 1e-6
GN_EPS = 1e-5 * RW_N
ADAM_LR = 0.001
ADAM_B1 = 0.9
ADAM_B2 = 0.999
ADAM_EPS = 1e-08
ADAM_WD = 0.01
ADAM_STEP = 10
N_DEV = 8
LANES = 128
VMEM_LIMIT = 56 * 1024 * 1024
TILE_BYTES = 1280 * 1024

REPL = (("attn_pre_norm", 1024), ("hgrn_lb", 1024), ("hgrn_gnorm", 1024), ("rwkv_mu", 3328), ("rwkv_w0", 1024),
        ("rwkv_a0", 1024), ("rwkv_k_k", 1024), ("rwkv_k_a", 1024), ("rwkv_r_k", 1024), ("rwkv_ln_w", 1024),
        ("rwkv_ln_b", 1024), ("attn_post_norm", 1024), ("ffn_pre_norm", 1024), ("conv_b", 5632), ("ffn_post_norm", 1024))
REPL_ROWS = {"hgrn_lb": 2}
REPL_TOTAL = 32


def _cparams(sem=None, **kw):
    return pltpu.CompilerParams(dimension_semantics=sem, vmem_limit_bytes=VMEM_LIMIT, **kw)


_DN = {"nn": ((1,), (0,)), "nt": ((1,), (1,)), "tn": ((0,), (0,))}


def _raw_dot(a, b, mode):
    return lax.dot_general(a.astype(BF), b.astype(BF), (_DN[mode], ((), ())), preferred_element_type=F32)


@functools.partial(jax.custom_vjp, nondiff_argnums=(2,))
def _dot(a, b, mode):
    return _raw_dot(a, b, mode)


def _dot_fwd(a, b, mode):
    return _raw_dot(a, b, mode), (a, b)


def _dot_bwd(mode, res, g):
    a, b = res
    if mode == "nn":
        return _dot(g, b, "nt"), _dot(a, g, "tn")
    if mode == "nt":
        return _dot(g, b, "nn"), _dot(g, a, "tn")
    return _dot(b, g, "nt"), _dot(a, g, "nn")


_dot.defvjp(_dot_fwd, _dot_bwd)


def _bf_pieces(x, n):
    out, r = [], x
    for i in range(n):
        p = r.astype(BF)
        out.append(p)
        if i + 1 < n:
            r = r - p.astype(F32)
    return out


def _raw_split_dot(x, e, mode, n, x_left):
    eb = e.astype(BF)
    acc = None
    for p in _bf_pieces(x, n):
        ops = (p, eb) if x_left else (eb, p)
        t = lax.dot_general(*ops, (_DN[mode], ((), ())), preferred_element_type=F32)
        acc = t if acc is None else acc + t
    return acc


def _raw_headsum(x):
    t = x.shape[0]
    i = lax.broadcasted_iota(jnp.int32, (LANES, LANES), 0)
    j = lax.broadcasted_iota(jnp.int32, (LANES, LANES), 1)
    same = jnp.where((i >= RW_N) == (j >= RW_N), 1.0, 0.0).astype(F32)
    groups = x.shape[1] // LANES
    rows = jnp.concatenate([x[:, q * LANES:(q + 1) * LANES] for q in range(groups)], axis=0)
    s = _raw_split_dot(rows, same, "nn", 2, True)
    return jnp.concatenate([s[q * t:(q + 1) * t] for q in range(groups)], axis=1)


@jax.custom_vjp
def _headsum(x):
    return _raw_headsum(x)


def _headsum_fwd(x):
    return _raw_headsum(x), None


def _headsum_bwd(_, g):
    return (_raw_headsum(g),)


_headsum.defvjp(_headsum_fwd, _headsum_bwd)


@functools.partial(jax.custom_vjp, nondiff_argnums=(2,))
def _tdot(tri, x, n):
    return _raw_split_dot(x, tri, "nn", n, False)


def _tdot_fwd(tri, x, n):
    return _raw_split_dot(x, tri, "nn", n, False), tri


def _tdot_bwd(n, tri, g):
    return jnp.zeros_like(tri), _raw_split_dot(g, tri, "tn", n, False)


_tdot.defvjp(_tdot_fwd, _tdot_bwd)


def _row(x, i):
    r = lax.broadcasted_iota(jnp.int32, x.shape, 0)
    return jnp.sum(jnp.where(r == i, x, 0.0), axis=0, keepdims=True)


def _shift_down(x, prev):
    t = x.shape[0]

    @jax.custom_vjp
    def sh(x, prev):
        r = lax.broadcasted_iota(jnp.int32, x.shape, 0)
        return jnp.where(r == 0, prev, pltpu.roll(x, 1, 0))

    def fwd(x, prev):
        return sh(x, prev), None

    def bwd(_, g):
        r = lax.broadcasted_iota(jnp.int32, g.shape, 0)
        dx = jnp.where(r == t - 1, 0.0, pltpu.roll(g, t - 1, 0))
        return dx, jnp.sum(jnp.where(r == 0, g, 0.0), axis=0, keepdims=True)

    sh.defvjp(fwd, bwd)
    return sh(x, prev)


def _sigmoid(x):
    return jax.nn.sigmoid(x)


def _silu(x):
    return x * jax.nn.sigmoid(x)


def _softplus(x):
    return jnp.maximum(x, 0.0) + jnp.log(1.0 + jnp.exp(-jnp.abs(x)))


def _rms(x, g):
    return (x * lax.rsqrt(jnp.mean(x * x, axis=-1, keepdims=True) + EPS)) * g


def _tril(c):
    r = lax.broadcasted_iota(jnp.int32, (c, c), 0)
    cc = lax.broadcasted_iota(jnp.int32, (c, c), 1)
    return cc <= r


def _f_pre1(ps, xs, cs):
    return [_rms(xs[0], ps[0])], []


def _f_pre1_residual(ps, xs, cs):
    return [_rms(xs[0], ps[0]), xs[0]], []


def _f_hgrn(ps, xs, cs):
    lbraw, gn = ps
    hq, hf, hi, hg = xs
    hd = range(HG_PER_STEP)
    st = [cs[0][p * HG_K:(p + 1) * HG_K] for p in hd]
    l0, l1 = _row(lbraw, 0), _row(lbraw, 1)
    m = jnp.maximum(l0, l1)
    e0, e1 = jnp.exp(l0 - m), jnp.exp(l1 - m)
    lb = e0 / (e0 + e1)
    q = _silu(hq) * HG_SCALE
    f = lb + (1.0 - lb) * _sigmoid(hf)
    kh = 1.0 - f
    gl = jnp.log(f)
    c = HG_CHUNK
    low = _tril(c)
    tri = jnp.where(low, 1.0, 0.0).astype(F32)
    outs = []
    for i in range(hq.shape[0] // c):
        rows = slice(i * c, (i + 1) * c)
        b = _tdot(tri, gl[rows], 3)
        bref = _row(b, c // 2 - 1)
        blast = _row(b, c - 1)
        qi = q[rows] * jnp.exp(b - bref)
        ki = kh[rows] * jnp.exp(bref - b)
        qd = q[rows] * jnp.exp(b)
        kd = kh[rows] * jnp.exp(blast - b)
        dec = jnp.exp(blast)
        sl = [slice(p * HG_K, (p + 1) * HG_K) for p in hd]
        sc = [jnp.where(low, _dot(qi[:, sl[p]], ki[:, sl[p]], "nt"), 0.0) for p in hd]
        o = [_dot(sc[p], hi[rows, sl[p]], "nn") + _dot(qd[:, sl[p]], st[p], "nt") for p in hd]
        u = [_dot(hi[rows, sl[p]], kd[:, sl[p]], "tn") for p in hd]
        st = [dec[:, sl[p]] * st[p] + u[p] for p in hd]
        outs.append(jnp.concatenate(o, axis=1) if len(o) > 1 else o[0])
    o = outs[0] if len(outs) == 1 else jnp.concatenate(outs, axis=0)
    on = []
    for p in hd:
        op = o[:, p * HG_K:(p + 1) * HG_K]
        on.append(op * lax.rsqrt(jnp.mean(op * op, axis=-1, keepdims=True) + EPS))
    o = jnp.concatenate(on, axis=1) if len(on) > 1 else on[0]
    o = o * gn
    return [o * _silu(hg)], [jnp.concatenate(st, axis=0) if len(st) > 1 else st[0]]


_RW_OFFS = (0, 1024, 2048, 3072, 3200, 3328)


def _f_rwpre(ps, xs, cs):
    mu, w0, w2p, a0, a2p, g2, k_k, k_a = ps
    (prev,) = cs
    t = xs[0].shape[0]
    zs = []
    for i, z in enumerate(xs):
        lo, hi = _RW_OFFS[i], _RW_OFFS[i + 1]
        zs.append(z + mu[:, lo:hi] * (_shift_down(z, prev[:, lo:hi]) - z))
    rr, kr, vr, wa, gz = zs
    w_log = -_softplus(-(w0 + _dot(jnp.tanh(wa), w2p, "nn"))) - 0.5
    lw = -jnp.exp(w_log)
    a = _sigmoid(a0 + _dot(wa, a2p, "nn"))
    g = _dot(_sigmoid(gz), g2, "nn")
    kkr = kr * k_k
    kk = kkr / jnp.maximum(jnp.sqrt(_headsum(kkr * kkr)), 1e-12)
    k2 = kr * (1.0 + (a - 1.0) * k_a)
    newprev = jnp.concatenate([_row(z, t - 1) for z in xs], axis=1)
    return [rr, lw, k2, vr, -kk, kk * a, g], [newprev]


def _raw_inverses(ls):
    n = ls[0].shape[0]
    r = lax.broadcasted_iota(jnp.int32, (n, n), 0)
    c = lax.broadcasted_iota(jnp.int32, (n, n), 1)
    eye = jnp.where(r == c, 1.0, 0.0).astype(F32)
    tinv = [eye + l for l in ls]
    pw = ls
    for _ in range(5):
        pw = [_raw_dot(p, p, "nn") for p in pw]
        tinv = [t + _raw_dot(t, p, "nn") for t, p in zip(tinv, pw)]
    return tinv


@jax.custom_vjp
def _unit_lower_inverses(ls):
    return _raw_inverses(ls)


def _inverses_fwd(ls):
    tinv = _raw_inverses(ls)
    return tinv, tinv


def _inverses_bwd(tinv, gs):
    return ([_raw_dot(_raw_dot(t, g, "tn"), t, "nt") for t, g in zip(tinv, gs)],)


_unit_lower_inverses.defvjp(_inverses_fwd, _inverses_bwd)


@jax.custom_vjp
def _known_inverses(ls, tinv):
    return tinv


def _known_fwd(ls, tinv):
    return tinv, tinv


def _known_bwd(tinv, gs):
    return [_raw_dot(_raw_dot(t, g, "tn"), t, "nt") for t, g in zip(tinv, gs)], [jnp.zeros_like(t) for t in tinv]


_known_inverses.defvjp(_known_fwd, _known_bwd)


def _f_rwscan(ps, xs, cs, kept=None):
    state = cs[0]
    ys, keep = [], []
    n = 2 * RW_CHUNK
    for i in range(xs[0].shape[0] // RW_CHUNK):
        known = None
        if kept is not None:
            lo = i * RW_PAIRS_PER_STEP * n
            known = [kept[lo + p * n:lo + (p + 1) * n] for p in range(RW_PAIRS_PER_STEP)]
        y, state, tinv = _rwkv_chunk([x[i * RW_CHUNK:(i + 1) * RW_CHUNK] for x in xs], state, known)
        ys.append(y)
        keep += tinv
    return [ys[0] if len(ys) == 1 else jnp.concatenate(ys, axis=0)], [state], jnp.concatenate(keep, axis=0)


def _rwkv_chunk(xs, state, known=None):
    npair = RW_PAIRS_PER_STEP
    pr = range(npair)
    r, lw, k, v, av, bv = [[x[:, p * LANES:(p + 1) * LANES] for p in pr] for x in xs]
    sv = [state[p * LANES:(p + 1) * LANES] for p in pr]
    c = RW_CHUNK
    n = 2 * c
    tri = jnp.where(_tril(c), 1.0, 0.0).astype(F32)
    cl = [_tdot(tri, lw[p], 3) for p in pr]
    cl_last = [_row(cl[p], c - 1) for p in pr]
    lane = lax.broadcasted_iota(jnp.int32, (c, LANES), 1)
    h0 = lane < RW_N

    def stack(x):
        return jnp.concatenate([jnp.where(h0, x, 0.0), jnp.where(h0, 0.0, x)], axis=0)

    am = [stack(av[p] * jnp.exp(cl[p] - lw[p])) for p in pr]
    bm = [stack(bv[p] * jnp.exp(-cl[p])) for p in pr]
    km = [stack(k[p] * jnp.exp(-cl[p])) for p in pr]
    rm = [stack(r[p] * jnp.exp(cl[p])) for p in pr]
    vm = [stack(v[p]) for p in pr]
    rn = lax.broadcasted_iota(jnp.int32, (n, n), 0)
    cn = lax.broadcasted_iota(jnp.int32, (n, n), 1)
    blk = (rn >= c) == (cn >= c)
    strict = blk & (cn < rn)
    incl = blk & (cn <= rn)
    lab = [jnp.where(strict, _dot(am[p], bm[p], "nt"), 0.0) for p in pr]
    lak = [jnp.where(strict, _dot(am[p], km[p], "nt"), 0.0) for p in pr]
    wrb = [jnp.where(incl, _dot(rm[p], bm[p], "nt"), 0.0) for p in pr]
    wrk = [jnp.where(incl, _dot(rm[p], km[p], "nt"), 0.0) for p in pr]
    tinv = _unit_lower_inverses(lab) if known is None else _known_inverses(lab, known)
    rhs = [_dot(am[p], sv[p], "nt") + _dot(lak[p], vm[p], "nn") for p in pr]
    um = [_dot(tinv[p], rhs[p], "nn") for p in pr]
    ym = [_dot(rm[p], sv[p], "nt") + _dot(wrb[p], um[p], "nn") + _dot(wrk[p], vm[p], "nn") for p in pr]
    sn = [(sv[p] + _dot(um[p], bm[p], "tn") + _dot(vm[p], km[p], "tn")) * jnp.exp(cl_last[p]) for p in pr]
    ys = [ym[p][:c] + ym[p][c:] for p in pr]
    return jnp.concatenate(ys, axis=1), jnp.concatenate(sn, axis=0), tinv


def _f_mixers(ps, xs, cs):
    return _mixers(ps, xs, cs, None)


def _f_mixers_kept(ps, xs, cs, kept):
    return _mixers(ps, xs, cs, kept[0])[:2]


def _mixers(ps, xs, cs, kept):
    oa, st = _f_hgrn(ps[:2], xs[:4], cs[:1])
    (r, lw, k, v, av, bv, g), prev = _f_rwpre(ps[2:10], xs[4:], cs[1:2])
    y, sv, keep = _f_rwscan([], [r, lw, k, v, av, bv], cs[2:], kept)
    ob, _ = _f_rwpost(ps[10:], y + [r, k, v, g], [])
    return oa + ob, st + prev + sv, [keep]


def _f_rwpost(ps, xs, cs):
    ln_w, ln_b, r_k = ps
    y, r, k, v, g = xs
    inv_n = 1.0 / RW_N
    yc = y - _headsum(y) * inv_n
    var = _headsum(yc * yc) * inv_n
    yn = yc * lax.rsqrt(var + GN_EPS)
    yn = yn * ln_w + ln_b
    bonus = _headsum(r * k * r_k) * v
    return [(yn + bonus) * g], []


def _f_merge(ps, xs, cs):
    ga, gb, ya, yb = xs
    return [_sigmoid(ga) * ya + _sigmoid(gb) * yb], []


def _f_post1(ps, xs, cs):
    x, mix = xs
    h1 = x + _rms(mix, ps[0])
    return [h1, _rms(h1, ps[1])], []


def _f_conv(ps, xs, cs):
    cw, cb = ps
    p1, p2 = cs
    w0, w1, w2 = _row(cw, 0), _row(cw, 1), _row(cw, 2)
    t = xs[0].shape[0]
    hc = []
    for i, x in enumerate(xs):
        sl = slice(i * DFF, (i + 1) * DFF)
        s1 = _shift_down(x, p1[:, sl])
        s2 = _shift_down(s1, p2[:, sl])
        hc.append(cb[:, sl] + w0[:, sl] * s2 + w1[:, sl] * s1 + w2[:, sl] * x)
    n1 = jnp.concatenate([_row(x, t - 1) for x in xs], axis=1)
    n2 = jnp.concatenate([_row(x, t - 2) for x in xs], axis=1)
    return [_silu(hc[0]) * hc[1]], [n1, n2]


class _Stage:
    def __init__(self, name, f, g, tm, par_per_g, in_pieces, in_offs, carry_shapes, out_pieces, out_dtypes,
                 kept_shapes=(), f_kept=None):
        self.name, self.f, self.g, self.tm = name, f, g, tm
        self.par_per_g, self.in_pieces, self.in_offs = par_per_g, in_pieces, in_offs
        self.carry_shapes, self.out_pieces, self.out_dtypes = carry_shapes, out_pieces, out_dtypes
        self.kept_shapes, self.f_kept = list(kept_shapes), f_kept


def _par_spec(arr, per_g, g):
    r, c = arr.shape
    if per_g:
        return pl.BlockSpec((r, c // g), lambda gi, ni: (0, gi))
    return pl.BlockSpec((r, c), lambda gi, ni: (0, 0))


def _row_spec(tm, width, off, n, rev):
    if rev:
        return pl.BlockSpec((tm, width), lambda gi, ni: (n - 1 - ni, off + gi))
    return pl.BlockSpec((tm, width), lambda gi, ni: (ni, off + gi))


def _carry_spec(shape, n, rev):
    if rev:
        return pl.BlockSpec((None, None) + shape, lambda gi, ni: (gi, n - 1 - ni, 0, 0))
    return pl.BlockSpec((None, None) + shape, lambda gi, ni: (gi, ni, 0, 0))


def _load_pieces(refs, pieces_list):
    out = []
    for ref, pieces in zip(refs, pieces_list):
        o = 0
        for w in pieces:
            out.append(ref[:, o:o + w].astype(F32))
            o += w
    return out


def _store_pieces(refs, pieces_list, vals):
    k = 0
    for ref, pieces in zip(refs, pieces_list):
        o = 0
        for w in pieces:
            ref[:, o:o + w] = vals[k].astype(ref.dtype)
            k += 1
            o += w


_ANY = pl.BlockSpec(memory_space=pl.ANY)


class _Exchange:
    def __init__(self, kind, arrs):
        self.kind, self.arrs, self.results = kind, list(arrs), None
        if kind == "scatter":
            self.out_shape = [jax.ShapeDtypeStruct((N_DEV - 1,) + a.shape[1:], a.dtype) for a in self.arrs]
        else:
            self.out_shape = [jax.ShapeDtypeStruct((N_DEV,) + a.shape, a.dtype) for a in self.arrs]
        self.nsem = (N_DEV if kind == "gather2" else N_DEV - 1) * len(self.arrs)

    def copies(self, in_refs, out_refs, ssem, rsem):
        x, y, c = lax.axis_index("x"), lax.axis_index("y"), lax.axis_index("c")
        me = 4 * x + 2 * y + c
        cps = []
        for a, (i_ref, o_ref) in enumerate(zip(in_refs, out_refs)):
            for j in range(1, N_DEV):
                px = 1 - x if j & 4 else x
                py = 1 - y if j & 2 else y
                pc = 1 - c if j & 1 else c
                if self.kind == "gather":
                    src, dst = i_ref, o_ref.at[me]
                else:
                    src, dst = i_ref.at[4 * px + 2 * py + pc], o_ref.at[j - 1]
                s = (N_DEV - 1) * a + j - 1
                cps.append(pltpu.make_async_remote_copy(src_ref=src, dst_ref=dst, send_sem=ssem.at[s],
                                                        recv_sem=rsem.at[s], device_id=(px, py, pc),
                                                        device_id_type=MESH))
        return cps

    def run(self, step, total, in_refs, out_refs, ssem, rsem):
        if self.kind == "gather2":
            return self.run_two_level(step, total, in_refs, out_refs, ssem, rsem)

        @pl.when(step == 0)
        def _():
            for cp in self.copies(in_refs, out_refs, ssem, rsem):
                cp.start()

        @pl.when(step == total - 1)
        def _():
            for cp in self.copies(in_refs, out_refs, ssem, rsem):
                cp.wait()

    def run_two_level(self, step, total, in_refs, out_refs, ssem, rsem):
        x, y, c = lax.axis_index("x"), lax.axis_index("y"), lax.axis_index("c")
        sibling, xn, yn = (x, y, 1 - c), (1 - x, y, c), (x, 1 - y, c)
        arrs = range(len(in_refs))
        ns = N_DEV

        def num(px, py, pc):
            return 4 * px + 2 * py + pc

        def copy(a, k, to, src, dst):
            return pltpu.make_async_remote_copy(src_ref=src, dst_ref=dst, send_sem=ssem.at[ns * a + k],
                                                recv_sem=rsem.at[ns * a + k], device_id=to, device_id_type=MESH)

        def blk(a, b):
            return out_refs[a].at[b]

        def half(a, b, second):
            h = self.arrs[a].shape[0] // 2
            return out_refs[a].at[b, pl.ds(h if second else 0, h)]

        bx, by, bd = num(1 - x, y, c), num(x, 1 - y, c), num(1 - x, 1 - y, c)

        def firsts(a):
            own = blk(a, num(x, y, c))
            return [copy(a, 0, sibling, in_refs[a], own), copy(a, 1, xn, in_refs[a], own),
                    copy(a, 2, yn, in_refs[a], own)]

        def seconds(a):
            return [copy(a, 3, yn, half(a, bx, False), half(a, bx, False)), copy(a, 5, sibling, blk(a, bx), blk(a, bx)),
                    copy(a, 4, xn, half(a, by, True), half(a, by, True)), copy(a, 6, sibling, blk(a, by), blk(a, by))]

        def third(a):
            return copy(a, 7, sibling, blk(a, bd), blk(a, bd))

        @pl.when(step == 0)
        def _():
            for a in arrs:
                for cp in firsts(a):
                    cp.start()

        @pl.when(step == total // 2)
        def _():
            for a in arrs:
                copy(a, 1, xn, blk(a, bx), blk(a, bx)).wait_recv()
                copy(a, 2, yn, blk(a, by), blk(a, by)).wait_recv()
                for cp in seconds(a):
                    cp.start()

        @pl.when(step == (4 * total) // 5)
        def _():
            for a in arrs:
                copy(a, 3, yn, half(a, bd, False), half(a, bd, False)).wait_recv()
                copy(a, 4, xn, half(a, bd, True), half(a, bd, True)).wait_recv()
                third(a).start()

        @pl.when(step == total - 1)
        def _():
            for a in arrs:
                for k, b in ((0, num(x, y, 1 - c)), (5, num(1 - x, y, 1 - c)), (6, num(x, 1 - y, 1 - c)),
                             (7, num(1 - x, 1 - y, 1 - c))):
                    copy(a, k, sibling, blk(a, b), blk(a, b)).wait_recv()
                for cp in firsts(a) + seconds(a) + [third(a)]:
                    cp.wait_send()


def _hook_specs(hook):
    if hook is None:
        return [], [], [], []
    na = len(hook.arrs)
    sems = [pltpu.SemaphoreType.DMA((hook.nsem,)), pltpu.SemaphoreType.DMA((hook.nsem,))]
    return [_ANY] * na, [_ANY] * na, hook.out_shape, sems


def _stage_fwd(st, t, params, inputs, hook=None):
    g, tm = st.g, min(st.tm, t)
    n = t // tm
    npar, nin, ncar, nout = len(params), len(inputs), len(st.carry_shapes), len(st.out_pieces)
    nk = len(st.kept_shapes)
    h_in, h_out, h_shape, h_sems = _hook_specs(hook)
    nh = len(h_in)

    def body(*refs):
        p_refs = refs[:npar]
        x_refs = refs[npar:npar + nin]
        hi_refs = refs[npar + nin:npar + nin + nh]
        o = npar + nin + nh
        o_refs = refs[o:o + nout]
        s_refs = refs[o + nout:o + nout + ncar]
        k_refs = refs[o + nout + ncar:o + nout + ncar + nk]
        o += nout + ncar + nk
        ho_refs = refs[o:o + nh]
        c_scr = refs[o + nh:o + nh + ncar]
        gi, ni = pl.program_id(0), pl.program_id(1)
        if hook is not None:
            step = gi * n + ni
            hook.run(step, g * n, hi_refs, ho_refs, *refs[-2:])

        @pl.when(ni == 0)
        def _():
            for c in c_scr:
                c[...] = jnp.zeros(c.shape, F32)

        ps = [r[...].astype(F32) for r in p_refs]
        xs = _load_pieces(x_refs, st.in_pieces)
        cs = [c[...] for c in c_scr]
        for s, c in zip(s_refs, cs):
            s[...] = c
        res = st.f(ps, xs, cs)
        outs, ncs = res[0], res[1]
        _store_pieces(o_refs, st.out_pieces, outs)
        for c, v in zip(c_scr, ncs):
            c[...] = v
        for kr, kv in zip(k_refs, res[2] if nk else []):
            kr[...] = kv.astype(kr.dtype)

    in_specs = [_par_spec(p, pg, g) for p, pg in zip(params, st.par_per_g)]
    in_specs += [_row_spec(tm, sum(pc), off, n, False) for pc, off in zip(st.in_pieces, st.in_offs)]
    out_specs = [_row_spec(tm, sum(pc), 0, n, False) for pc in st.out_pieces]
    out_specs += [_carry_spec(s, n, False) for s in st.carry_shapes]
    out_specs += [pl.BlockSpec(s, lambda gi, ni: (ni, 0)) for s in st.kept_shapes]
    out_shape = [jax.ShapeDtypeStruct((t, g * sum(pc)), dt) for pc, dt in zip(st.out_pieces, st.out_dtypes)]
    out_shape += [jax.ShapeDtypeStruct((g, n) + s, F32) for s in st.carry_shapes]
    out_shape += [jax.ShapeDtypeStruct((n * s[0], s[1]), BF) for s in st.kept_shapes]
    res = pl.pallas_call(
        body, name=st.name + "_fwd", grid=(g, n), in_specs=in_specs + h_in, out_specs=out_specs + h_out,
        out_shape=out_shape + h_shape,
        scratch_shapes=[pltpu.VMEM(s, F32) for s in st.carry_shapes] + h_sems,
        compiler_params=_cparams(("arbitrary", "arbitrary")),
    )(*params, *inputs, *(hook.arrs if hook else []))
    if hook is not None:
        hook.results = list(res[nout + ncar + nk:])
    return list(res[:nout]), list(res[nout:nout + ncar + nk])


def _stage_bwd(st, t, params, inputs, saved, douts, dx_dtypes, hook=None):
    g, tm = st.g, min(st.tm, t)
    n = t // tm
    npar, nin, ncar = len(params), len(inputs), len(st.carry_shapes)
    nk = len(st.kept_shapes)
    flat_d = [d for ds in douts for d in ds]
    nd = len(flat_d)
    dx_idx = [i for i, dt in enumerate(dx_dtypes) if dt is not None]
    h_in, h_out, h_shape, h_sems = _hook_specs(hook)
    nh = len(h_in)

    def body(*refs):
        p_refs = refs[:npar]
        x_refs = refs[npar:npar + nin]
        s_refs = refs[npar + nin:npar + nin + ncar]
        k_refs = refs[npar + nin + ncar:npar + nin + ncar + nk]
        o = npar + nin + ncar + nk
        d_refs = refs[o:o + nd]
        hi_refs = refs[o + nd:o + nd + nh]
        o += nd + nh
        dp_refs = refs[o:o + npar]
        dx_refs = refs[o + npar:o + npar + len(dx_idx)]
        ho_refs = refs[o + npar + len(dx_idx):o + npar + len(dx_idx) + nh]
        dc_scr = refs[o + npar + len(dx_idx) + nh:o + npar + len(dx_idx) + nh + ncar]
        gi, ni = pl.program_id(0), pl.program_id(1)
        if hook is not None:
            step = gi * n + ni
            hook.run(step, g * n, hi_refs, ho_refs, *refs[-2:])

        @pl.when(ni == 0)
        def _():
            for c in dc_scr:
                c[...] = jnp.zeros(c.shape, F32)

        ps = [r[...].astype(F32) for r in p_refs]
        xs = _load_pieces(x_refs, st.in_pieces)
        cs = [s[...] for s in s_refs]
        dys = []
        k = 0
        for ds, pieces in zip(douts, st.out_pieces):
            acc = _load_pieces([d_refs[k]], [pieces])
            for j in range(1, len(ds)):
                more = _load_pieces([d_refs[k + j]], [pieces])
                acc = [a + b for a, b in zip(acc, more)]
            dys += acc
            k += len(ds)
        if nk:
            kept = [r[...].astype(F32) for r in k_refs]
            _, vjp = jax.vjp(lambda p, x, c: st.f_kept(p, x, c, kept), ps, xs, cs)
        else:
            _, vjp = jax.vjp(st.f, ps, xs, cs)
        dps, dxs, dcs = vjp((dys, [c[...] for c in dc_scr]))
        k = 0
        per_in = []
        for pieces in st.in_pieces:
            per_in.append(dxs[k:k + len(pieces)])
            k += len(pieces)
        for ref, i in zip(dx_refs, dx_idx):
            _store_pieces([ref], [st.in_pieces[i]], per_in[i])
        for c, v in zip(dc_scr, dcs):
            c[...] = v
        for ref, dp, pg in zip(dp_refs, dps, st.par_per_g):
            first = (ni == 0) if pg else ((ni == 0) & (gi == 0))

            @pl.when(first)
            def _():
                ref[...] = jnp.zeros(ref.shape, F32)

            ref[...] += dp

    in_specs = [_par_spec(p, pg, g) for p, pg in zip(params, st.par_per_g)]
    in_specs += [_row_spec(tm, sum(pc), off, n, True) for pc, off in zip(st.in_pieces, st.in_offs)]
    in_specs += [_carry_spec(s, n, True) for s in st.carry_shapes]
    in_specs += [pl.BlockSpec(s, lambda gi, ni: (n - 1 - ni, 0)) for s in st.kept_shapes]
    for ds, pc in zip(douts, st.out_pieces):
        in_specs += [_row_spec(tm, sum(pc), 0, n, True) for _ in ds]
    out_specs = [_par_spec(p, pg, g) for p, pg in zip(params, st.par_per_g)]
    out_specs += [_row_spec(tm, sum(st.in_pieces[i]), 0, n, True) for i in dx_idx]
    out_shape = [jax.ShapeDtypeStruct(p.shape, F32) for p in params]
    out_shape += [jax.ShapeDtypeStruct((t, g * sum(st.in_pieces[i])), dx_dtypes[i]) for i in dx_idx]
    res = pl.pallas_call(
        body, name=st.name + "_bwd", grid=(g, n), in_specs=in_specs + h_in, out_specs=out_specs + h_out,
        out_shape=out_shape + h_shape,
        scratch_shapes=[pltpu.VMEM(s, F32) for s in st.carry_shapes] + h_sems,
        compiler_params=_cparams(("arbitrary", "arbitrary")),
    )(*params, *inputs, *saved, *flat_d, *(hook.arrs if hook else []))
    if hook is not None:
        hook.results = list(res[npar + len(dx_idx):])
    return list(res[:npar]), list(res[npar:npar + len(dx_idx)])


def _pick(n, cap):
    if n <= cap:
        return n
    best = LANES
    for k in range(1, n // LANES + 1):
        if (n // LANES) % k == 0 and k * LANES <= cap:
            best = k * LANES
    return best


def _mm(name, a, b, mode, out_dtype=F32, tm=1024, tn=512, b_outer=False, token=None):
    m = a.shape[1] if mode == "tn" else a.shape[0]
    k = a.shape[0] if mode == "tn" else a.shape[1]
    n = b.shape[0] if mode == "nt" else b.shape[1]
    tm, tn = _pick(m, tm), _pick(n, tn)
    if b_outer:
        grid = (n // tn, m // tm)
        ij = lambda p, q: (q, p)
    else:
        grid = (m // tm, n // tn)
        ij = lambda p, q: (p, q)
    extra = [] if token is None else [token]

    def body(*refs):
        a_ref, b_ref, o_ref = refs[0], refs[1], refs[-1]
        o_ref[...] = _raw_dot(a_ref[...], b_ref[...], mode).astype(o_ref.dtype)

    if mode == "tn":
        a_spec = pl.BlockSpec((k, tm), lambda p, q: (0, ij(p, q)[0]))
    else:
        a_spec = pl.BlockSpec((tm, k), lambda p, q: (ij(p, q)[0], 0))
    b_mode = dict(pipeline_mode=pl.Buffered(1)) if tn == n else {}
    if mode == "nt":
        b_spec = pl.BlockSpec((tn, k), lambda p, q: (ij(p, q)[1], 0), **b_mode)
    else:
        b_spec = pl.BlockSpec((k, tn), lambda p, q: (0, ij(p, q)[1]), **b_mode)
    return pl.pallas_call(
        body, name=name, grid=grid,
        in_specs=[a_spec, b_spec] + [pl.BlockSpec(e.shape, lambda p, q: (0, 0)) for e in extra],
        out_specs=pl.BlockSpec((tm, tn), lambda p, q: ij(p, q)),
        out_shape=jax.ShapeDtypeStruct((m, n), out_dtype),
        compiler_params=_cparams(("arbitrary", "arbitrary")),
    )(a, b, *extra)


def _mm_cols_nn(name, pieces, b, out_dtype, tm, token=None):
    m, n = pieces[0].shape[0], b.shape[1]
    tm = _pick(m, tm)
    offs = [sum(p.shape[1] for p in pieces[:i]) for i in range(len(pieces))]
    extra = [] if token is None else [token]
    na = len(pieces)

    def body(*refs):
        b_ref, o_ref = refs[na], refs[-1]
        acc = None
        for a_ref, off in zip(refs[:na], offs):
            t = _raw_dot(a_ref[...], b_ref[off:off + a_ref.shape[1], :], "nn")
            acc = t if acc is None else acc + t
        o_ref[...] = acc.astype(o_ref.dtype)

    return pl.pallas_call(
        body, name=name, grid=(m // tm,),
        in_specs=[pl.BlockSpec((tm, p.shape[1]), lambda i: (i, 0)) for p in pieces]
        + [pl.BlockSpec(b.shape, lambda i: (0, 0), pipeline_mode=pl.Buffered(1))]
        + [pl.BlockSpec(e.shape, lambda i: (0, 0)) for e in extra],
        out_specs=pl.BlockSpec((tm, n), lambda i: (i, 0)), out_shape=jax.ShapeDtypeStruct((m, n), out_dtype),
        compiler_params=_cparams(("arbitrary",)),
    )(*pieces, b, *extra)


def _mm_cols_tn(name, pieces, b, out_dtype, tm):
    k, n = b.shape
    counts = [p.shape[1] // tm for p in pieces]
    starts = [sum(counts[:i]) for i in range(len(pieces))]
    na = len(pieces)

    def body(*refs):
        b_ref, o_ref = refs[na], refs[-1]
        i = pl.program_id(0)
        for a_ref, s, c in zip(refs[:na], starts, counts):
            @pl.when((i >= s) & (i < s + c))
            def _():
                o_ref[...] = _raw_dot(a_ref[...], b_ref[...], "tn").astype(o_ref.dtype)

    def spec(s, c):
        return pl.BlockSpec((k, tm), lambda i: (0, jnp.clip(i - s, 0, c - 1)))

    return pl.pallas_call(
        body, name=name, grid=(sum(counts),),
        in_specs=[spec(s, c) for s, c in zip(starts, counts)]
        + [pl.BlockSpec(b.shape, lambda i: (0, 0), pipeline_mode=pl.Buffered(1))],
        out_specs=pl.BlockSpec((tm, n), lambda i: (i, 0)),
        out_shape=jax.ShapeDtypeStruct((sum(counts) * tm, n), out_dtype),
        compiler_params=_cparams(("arbitrary",)),
    )(*pieces, b)


def _loss_stage(t, g_post, h1, ff, tgt):
    tm = min(256, t)
    n = t // tm

    def body(g_ref, h_ref, f_ref, t_ref, loss_ref, dg_ref, dh_ref, df_ref):
        ni = pl.program_id(0)
        target = t_ref[...]

        def lossf(g, h1, ff):
            e = h1 + _rms(ff, g) - target
            return 0.5 * jnp.sum(jnp.mean(e * e, axis=-1))

        l, (dg, dh, df) = jax.value_and_grad(lossf, argnums=(0, 1, 2))(g_ref[...], h_ref[...], f_ref[...])

        @pl.when(ni == 0)
        def _():
            loss_ref[...] = jnp.zeros(loss_ref.shape, F32)
            dg_ref[...] = jnp.zeros(dg_ref.shape, F32)

        loss_ref[...] += jnp.full(loss_ref.shape, l, F32)
        dg_ref[...] += dg
        dh_ref[...] = dh
        df_ref[...] = df.astype(df_ref.dtype)

    row = pl.BlockSpec((tm, D), lambda ni: (ni, 0))
    one = pl.BlockSpec((1, D), lambda ni: (0, 0))
    return pl.pallas_call(
        body, name="loss_head", grid=(n,), in_specs=[one, row, row, row],
        out_specs=[pl.BlockSpec((1, LANES), lambda ni: (0, 0)), one, row, row],
        out_shape=[jax.ShapeDtypeStruct((1, LANES), F32), jax.ShapeDtypeStruct((1, D), F32),
                   jax.ShapeDtypeStruct((t, D), F32), jax.ShapeDtypeStruct((t, D), BF)],
        compiler_params=_cparams(("arbitrary",)),
    )(g_post, h1, ff, tgt)


_ANY = pl.BlockSpec(memory_space=pl.ANY)


def _all_gather(name, blks):
    na = len(blks)
    ns = 8

    def body(*refs):
        x_refs, out_refs = refs[:na], refs[na:2 * na]
        send_sems, recv_sems, local_sems = refs[2 * na:]
        x, y, cc = lax.axis_index("x"), lax.axis_index("y"), lax.axis_index("c")
        sibling, xn, yn = (x, y, 1 - cc), (1 - x, y, cc), (x, 1 - y, cc)

        def num(px, py, pc):
            return 4 * px + 2 * py + pc

        def copy(a, k, to, src, dst):
            return pltpu.make_async_remote_copy(src_ref=src, dst_ref=dst, send_sem=send_sems.at[ns * a + k],
                                                recv_sem=recv_sems.at[ns * a + k], device_id=to, device_id_type=MESH)

        def halves(a, blk):
            h = blks[a].shape[0] // 2
            return out_refs[a].at[blk, pl.ds(0, h)], out_refs[a].at[blk, pl.ds(h, h)]

        mine, sends = [], []
        for a in range(na):
            o = out_refs[a]
            m = pltpu.make_async_copy(x_refs[a], o.at[num(x, y, cc)], local_sems.at[a])
            m.start()
            mine.append(m)
            own = o.at[num(x, y, cc)]
            sends.append([copy(a, 0, sibling, x_refs[a], own), copy(a, 1, xn, x_refs[a], own),
                          copy(a, 2, yn, x_refs[a], own)])
            for cp in sends[a]:
                cp.start()
        for a in range(na):
            o = out_refs[a]
            bx, by, bd = num(1 - x, y, cc), num(x, 1 - y, cc), num(1 - x, 1 - y, cc)
            copy(a, 1, xn, o.at[bx], o.at[bx]).wait_recv()
            more = [copy(a, 3, yn, halves(a, bx)[0], halves(a, bx)[0]), copy(a, 5, sibling, o.at[bx], o.at[bx])]
            for cp in more:
                cp.start()
            sends[a] += more
        for a in range(na):
            o = out_refs[a]
            bx, by, bd = num(1 - x, y, cc), num(x, 1 - y, cc), num(1 - x, 1 - y, cc)
            copy(a, 2, yn, o.at[by], o.at[by]).wait_recv()
            more = [copy(a, 4, xn, halves(a, by)[1], halves(a, by)[1]), copy(a, 6, sibling, o.at[by], o.at[by])]
            for cp in more:
                cp.start()
            sends[a] += more
        for a in range(na):
            o = out_refs[a]
            bd = num(1 - x, 1 - y, cc)
            copy(a, 3, yn, halves(a, bd)[0], halves(a, bd)[0]).wait_recv()
            copy(a, 4, xn, halves(a, bd)[1], halves(a, bd)[1]).wait_recv()
            fw = copy(a, 7, sibling, o.at[bd], o.at[bd])
            fw.start()
            sends[a].append(fw)
        for a in range(na):
            o = out_refs[a]
            for k, blk in ((0, num(x, y, 1 - cc)), (5, num(1 - x, y, 1 - cc)), (6, num(x, 1 - y, 1 - cc)),
                           (7, num(1 - x, 1 - y, 1 - cc))):
                copy(a, k, sibling, o.at[blk], o.at[blk]).wait_recv()
            for cp in sends[a]:
                cp.wait_send()
        for m in mine:
            m.wait()

    res = pl.pallas_call(
        body, name=name, in_specs=[_ANY] * na, out_specs=[_ANY] * na,
        out_shape=[jax.ShapeDtypeStruct((N_DEV,) + b.shape, b.dtype) for b in blks],
        scratch_shapes=[pltpu.SemaphoreType.DMA((ns * na,)), pltpu.SemaphoreType.DMA((ns * na,)),
                        pltpu.SemaphoreType.DMA((na,))],
    )(*blks)
    return list(res)


def _reduce_pair(g8s):
    na = len(g8s)

    def body(*refs):
        g_refs, recv_refs = refs[:na], refs[na:2 * na]
        ssem, rsem = refs[2 * na:]
        x, y, cc = lax.axis_index("x"), lax.axis_index("y"), lax.axis_index("c")
        chips = [(x, y), (1 - x, y), (x, 1 - y), (1 - x, 1 - y)]
        sib = (x, y, 1 - cc)
        for a in range(na):
            for k, (cx, cy) in enumerate(chips):
                pltpu.make_async_remote_copy(
                    src_ref=g_refs[a].at[4 * cx + 2 * cy + 1 - cc], dst_ref=recv_refs[a].at[k],
                    send_sem=ssem.at[a], recv_sem=rsem.at[a], device_id=sib, device_id_type=MESH).start()
        for a in range(na):
            pltpu.make_async_remote_copy(src_ref=recv_refs[a], dst_ref=recv_refs[a], send_sem=ssem.at[a],
                                         recv_sem=rsem.at[a], device_id=sib, device_id_type=MESH).wait()

    res = pl.pallas_call(
        body, name="reduce_pair", in_specs=[_ANY] * na, out_specs=[_ANY] * na,
        out_shape=[jax.ShapeDtypeStruct((4,) + g.shape[1:], g.dtype) for g in g8s],
        scratch_shapes=[pltpu.SemaphoreType.DMA((na,)), pltpu.SemaphoreType.DMA((na,))],
    )(*g8s)
    return list(res)


_HBM = pl.BlockSpec(memory_space=pltpu.HBM)
_SEM = pl.BlockSpec(memory_space=pltpu.SEMAPHORE)
_EFFECT = pltpu.SideEffectType.DATAFLOW_SIDE_EFFECTING


def _chip_swap_copies(s_refs, land_refs, ssem, rsem):
    x, y, c = lax.axis_index("x"), lax.axis_index("y"), lax.axis_index("c")
    targets = [(1 - x, y, c), (x, 1 - y, c), (1 - x, 1 - y, c)]
    return [pltpu.make_async_remote_copy(src_ref=s.at[k], dst_ref=d.at[k], send_sem=ssem.at[3 * a + k],
                                         recv_sem=rsem.at[3 * a + k], device_id=targets[k], device_id_type=MESH)
            for a, (s, d) in enumerate(zip(s_refs, land_refs)) for k in range(3)]


def _chip_swap_start(sends):
    na = len(sends)

    def body(*refs):
        cps = _chip_swap_copies(refs[:na], refs[na:2 * na], refs[2 * na], refs[2 * na + 1])
        for cp in cps:
            cp.start()
        token = refs[-1]
        token[...] = jnp.zeros(token.shape, token.dtype)

    bufs = [pltpu.HBM(s.shape, s.dtype) for s in sends]
    res = pl.pallas_call(
        body, name="chip_swap_start",
        out_shape=[pltpu.SemaphoreType.DMA((3 * na,)), pltpu.SemaphoreType.DMA((3 * na,))] + bufs + bufs
        + [jax.ShapeDtypeStruct((8, LANES), F32)],
        in_specs=[_HBM] * (2 * na), out_specs=[_SEM, _SEM] + [_HBM] * (2 * na) + [pl.BlockSpec(memory_space=pltpu.VMEM)],
        input_output_aliases={i: 2 + i for i in range(2 * na)},
        compiler_params=pltpu.CompilerParams(has_side_effects=_EFFECT),
    )(*[pltpu.with_memory_space_constraint(s, pltpu.HBM) for s in sends],
      *[pltpu.with_memory_space_constraint(lax.empty(s.shape, s.dtype), pltpu.HBM) for s in sends])
    return res[0], res[1], list(res[2:2 + na]), list(res[2 + na:2 + 2 * na]), res[-1]


def _chip_swap_wait(ssem, rsem, srcs, lands, after):
    na = len(srcs)

    def body(*refs):
        cps = _chip_swap_copies(refs[:na], refs[na:2 * na], refs[2 * na], refs[2 * na + 1])
        for cp in cps:
            cp.wait_send()
            cp.wait_recv()

    bufs = [pltpu.HBM(s.shape, s.dtype) for s in srcs]
    res = pl.pallas_call(
        body, name="chip_swap_wait", out_shape=bufs + bufs,
        in_specs=[_HBM] * (2 * na) + [_SEM, _SEM, _ANY], out_specs=[_HBM] * (2 * na),
        input_output_aliases={i: i for i in range(2 * na)},
        compiler_params=pltpu.CompilerParams(has_side_effects=_EFFECT),
    )(*srcs, *lands, ssem, rsem, after)
    return list(res[na:])


def _pick_rows(r, c, budget=TILE_BYTES):
    if r * c * 4 <= budget or r % 16:
        return r
    best = 16
    for tr in range(16, r, 16):
        if r % tr == 0 and tr * c * 4 <= budget:
            best = tr
    return best


def _pair_sum(name, idx4, g8, recv4):
    _, r, c = g8.shape
    tr = _pick_rows(r, c, 2 * TILE_BYTES)

    def body(idx_ref, a_ref, b_ref, o0_ref, o3_ref):
        k = pl.program_id(1)
        s = a_ref[...].astype(F32) + b_ref[...].astype(F32)

        @pl.when(k == 0)
        def _():
            o0_ref[...] = s

        @pl.when(k > 0)
        def _():
            o3_ref[...] = s.astype(BF)

    spec = pltpu.PrefetchScalarGridSpec(
        num_scalar_prefetch=1, grid=(r // tr, 4),
        in_specs=[pl.BlockSpec((None, tr, c), lambda i, k, idx: (idx[k], i, 0)),
                  pl.BlockSpec((None, tr, c), lambda i, k, idx: (k, i, 0))],
        out_specs=[pl.BlockSpec((tr, c), lambda i, k, idx: (i, 0)),
                   pl.BlockSpec((None, tr, c), lambda i, k, idx: (jnp.maximum(k - 1, 0), i, 0))])
    return pl.pallas_call(
        body, name=name, grid_spec=spec,
        out_shape=[jax.ShapeDtypeStruct((r, c), F32), jax.ShapeDtypeStruct((3, r, c), BF)],
        compiler_params=_cparams(("arbitrary", "arbitrary")),
    )(idx4, g8, recv4)


def _adamw(w, g, m, v):
    m = ADAM_B1 * m + (1.0 - ADAM_B1) * g
    v = ADAM_B2 * v + (1.0 - ADAM_B2) * jnp.square(g)
    m_hat = m / (1.0 - ADAM_B1 ** ADAM_STEP)
    v_hat = v / (1.0 - ADAM_B2 ** ADAM_STEP)
    delta = -ADAM_LR * (m_hat / (jnp.sqrt(v_hat) + ADAM_EPS) + ADAM_WD * w)
    return delta, m, v


def _sum_partials(name, idx1, own, recv):
    _, r, c = own.shape
    tr = _pick_rows(r, c, 2 * TILE_BYTES)
    nj = recv.shape[0]

    def body(idx_ref, p_ref, r_ref, g_out):
        g = p_ref[...].astype(F32)
        for k in range(nj):
            g = g + r_ref[k].astype(F32)
        g_out[...] = g

    row = pl.BlockSpec((tr, c), lambda i, idx: (i, 0))
    spec = pltpu.PrefetchScalarGridSpec(
        num_scalar_prefetch=1, grid=(r // tr,),
        in_specs=[pl.BlockSpec((None, tr, c), lambda i, idx: (idx[0], i, 0)),
                  pl.BlockSpec((nj, tr, c), lambda i, idx: (0, i, 0))],
        out_specs=row)
    return pl.pallas_call(body, name=name, grid_spec=spec, out_shape=jax.ShapeDtypeStruct((r, c), F32),
                          compiler_params=_cparams(("arbitrary",)))(idx1, own, recv)


def _adam_sharded(name, idx1, own, recv, w, m, v):
    r, c = w.shape
    tr = _pick_rows(r, c)
    nj = 0 if recv is None else recv.shape[0]
    if recv is None:
        recv = jnp.zeros((1, 8, LANES), BF)

    def body(idx_ref, p_ref, r_ref, w_ref, m_ref, v_ref, g_out, d_out, m_out, v_out):
        g = p_ref[...].astype(F32)
        for k in range(nj):
            g = g + r_ref[k].astype(F32)
        d, mn, vn = _adamw(w_ref[...], g, m_ref[...], v_ref[...])
        g_out[...] = g
        d_out[...] = d
        m_out[...] = mn
        v_out[...] = vn

    row = pl.BlockSpec((tr, c), lambda i, idx: (i, 0))
    if nj:
        recv_spec = pl.BlockSpec((nj, tr, c), lambda i, idx: (0, i, 0))
    else:
        recv_spec = pl.BlockSpec(recv.shape, lambda i, idx: (0, 0, 0))
    spec = pltpu.PrefetchScalarGridSpec(
        num_scalar_prefetch=1, grid=(r // tr,),
        in_specs=[pl.BlockSpec((None, tr, c), lambda i, idx: (idx[0], i, 0)), recv_spec, row, row, row],
        out_specs=[row] * 4)
    return pl.pallas_call(
        body, name=name, grid_spec=spec, out_shape=[jax.ShapeDtypeStruct((r, c), F32)] * 4,
        compiler_params=_cparams(("arbitrary",)),
    )(idx1, own, recv, w, m, v)


def _repl_rows():
    rows, r = {}, 0
    for name, cols in REPL:
        rows[name] = r
        r += REPL_ROWS.get(name, 1) * ((cols + D - 1) // D)
    return rows


LOSS_ROW = 24


def _pack_replicated(grads, loss_acc):
    rows = _repl_rows()
    names = [n for n, _ in REPL]

    def body(*refs):
        o_ref = refs[-1]
        o_ref[...] = jnp.zeros(o_ref.shape, F32)
        o_ref[LOSS_ROW:LOSS_ROW + 1, 0:LANES] = refs[-2][...]
        for name, ref in zip(names, refs[:-2]):
            r0 = rows[name]
            nr, nc = ref.shape
            if nc <= D:
                o_ref[r0:r0 + nr, 0:nc] = ref[...]
            else:
                for j in range((nc + D - 1) // D):
                    lo, hi = j * D, min(nc, (j + 1) * D)
                    o_ref[r0 + j:r0 + j + 1, 0:hi - lo] = ref[:, lo:hi]

    return pl.pallas_call(body, name="pack_replicated", out_shape=jax.ShapeDtypeStruct((REPL_TOTAL, D), F32),
                          compiler_params=_cparams())(*[grads[n] for n in names], loss_acc)


def _adam_replicated(g8, ws, ms, vs):
    rows = _repl_rows()
    names = [n for n, _ in REPL]
    np_ = len(names)

    def body(*refs):
        g_ref = refs[0]
        w_refs, m_refs, v_refs = refs[1:1 + np_], refs[1 + np_:1 + 2 * np_], refs[1 + 2 * np_:1 + 3 * np_]
        outs = refs[1 + 3 * np_:1 + 7 * np_]
        scr = refs[-1]
        g = g_ref[0]
        for k in range(1, N_DEV):
            g = g + g_ref[k]
        scr[...] = g
        refs[1 + 7 * np_][...] = scr[LOSS_ROW:LOSS_ROW + 1, 0:LANES]
        for i, name in enumerate(names):
            r0 = rows[name]
            nr, nc = w_refs[i].shape
            if nc <= D:
                gi = scr[r0:r0 + nr, 0:nc]
            else:
                parts = []
                for j in range((nc + D - 1) // D):
                    lo, hi = j * D, min(nc, (j + 1) * D)
                    parts.append(scr[r0 + j:r0 + j + 1, 0:hi - lo])
                gi = jnp.concatenate(parts, axis=1)
            d, mn, vn = _adamw(w_refs[i][...], gi, m_refs[i][...], v_refs[i][...])
            outs[i][...] = gi
            outs[np_ + i][...] = d
            outs[2 * np_ + i][...] = mn
            outs[3 * np_ + i][...] = vn

    shp = [jax.ShapeDtypeStruct(w.shape, F32) for w in ws]
    res = pl.pallas_call(body, name="adam_replicated", out_shape=shp * 4 + [jax.ShapeDtypeStruct((1, LANES), F32)],
                         scratch_shapes=[pltpu.VMEM((REPL_TOTAL, D), F32)], compiler_params=_cparams(),
                         )(g8, *ws, *ms, *vs)
    return [dict(zip(names, res[k * np_:(k + 1) * np_])) for k in range(4)], res[-1]


_WEIGHTS = ("attn_pre_norm", "w_in", "hgrn_lb", "hgrn_gnorm", "w_branch_a", "rwkv_mu", "rwkv_w0", "rwkv_w2",
            "rwkv_a0", "rwkv_a2", "rwkv_g2", "rwkv_k_k", "rwkv_k_a", "rwkv_r_k", "rwkv_ln_w", "rwkv_ln_b",
            "w_branch_b", "w_out", "attn_post_norm", "ffn_pre_norm", "w_up", "conv_w", "conv_b", "w_down",
            "ffn_post_norm")
_BIG = ("w_in", "w_up", "w_down", "w_branch_a", "w_branch_b", "w_out")


def _stages():
    one = [D]
    hw = HG_K * HG_PER_STEP
    rw = LANES * RW_PAIRS_PER_STEP
    return dict(
        pre1=_Stage("pre1", _f_pre1, 1, 256, [False], [one], [0], [], [one], [BF]),
        pre1_res=_Stage("pre1", _f_pre1_residual, 1, 256, [False], [one], [0], [], [one, one], [BF, F32]),
        mixers=_Stage("mixers", _f_mixers, 1, 2 * RW_CHUNK, [False] * 13, [[D] * 7 + [LANES, LANES]], [0],
                      [(hw, HG_K), (1, RW_COLS), (rw, LANES)], [one, one], [BF, BF],
                      kept_shapes=[(2 * RW_PAIRS_PER_STEP * 2 * RW_CHUNK, LANES)], f_kept=_f_mixers_kept),
        merge=_Stage("merge", _f_merge, 4, 512, [], [[256]] * 4, [29, 33, 0, 0], [], [[256]], [BF]),
        post1=_Stage("post1", _f_post1, 1, 256, [False, False], [one, one], [0, 0], [], [one, one], [F32, BF]),
        conv=_Stage("conv", _f_conv, 1, 128, [False, False], [[DFF, DFF]], [0], [(1, 2 * DFF), (1, 2 * DFF)],
                    [[DFF]], [BF]),
    )


def _cols_to_blocks(w, per):
    return w.reshape(w.shape[0], N_DEV, per).transpose(1, 0, 2)


def _blocks_to_cols(g):
    return g.transpose(1, 0, 2).reshape(g.shape[1], N_DEV * g.shape[2])


def kernel(x, attn_pre_norm, w_in, hgrn_lb, hgrn_gnorm, w_branch_a, rwkv_mu, rwkv_w0, rwkv_w2, rwkv_a0, rwkv_a2, rwkv_g2, rwkv_k_k, rwkv_k_a, rwkv_r_k, rwkv_ln_w, rwkv_ln_b, w_branch_b, w_out, attn_post_norm, ffn_pre_norm, w_up, conv_w, conv_b, w_down, ffn_post_norm, loss_target, m_attn_pre_norm, m_w_in, m_hgrn_lb, m_hgrn_gnorm, m_w_branch_a, m_rwkv_mu, m_rwkv_w0, m_rwkv_w2, m_rwkv_a0, m_rwkv_a2, m_rwkv_g2, m_rwkv_k_k, m_rwkv_k_a, m_rwkv_r_k, m_rwkv_ln_w, m_rwkv_ln_b, m_w_branch_b, m_w_out, m_attn_post_norm, m_ffn_pre_norm, m_w_up, m_conv_w, m_conv_b, m_w_down, m_ffn_post_norm, v_attn_pre_norm, v_w_in, v_hgrn_lb, v_hgrn_gnorm, v_w_branch_a, v_rwkv_mu, v_rwkv_w0, v_rwkv_w2, v_rwkv_a0, v_rwkv_a2, v_rwkv_g2, v_rwkv_k_k, v_rwkv_k_a, v_rwkv_r_k, v_rwkv_ln_w, v_rwkv_ln_b, v_w_branch_b, v_w_out, v_attn_post_norm, v_ffn_pre_norm, v_w_up, v_conv_w, v_conv_b, v_w_down, v_ffn_post_norm):
    w = dict(attn_pre_norm=attn_pre_norm, w_in=w_in, hgrn_lb=hgrn_lb, hgrn_gnorm=hgrn_gnorm, w_branch_a=w_branch_a, rwkv_mu=rwkv_mu, rwkv_w0=rwkv_w0, rwkv_w2=rwkv_w2, rwkv_a0=rwkv_a0, rwkv_a2=rwkv_a2, rwkv_g2=rwkv_g2, rwkv_k_k=rwkv_k_k, rwkv_k_a=rwkv_k_a, rwkv_r_k=rwkv_r_k, rwkv_ln_w=rwkv_ln_w, rwkv_ln_b=rwkv_ln_b, w_branch_b=w_branch_b, w_out=w_out, attn_post_norm=attn_post_norm, ffn_pre_norm=ffn_pre_norm, w_up=w_up, conv_w=conv_w, conv_b=conv_b, w_down=w_down, ffn_post_norm=ffn_post_norm)
    mo = dict(attn_pre_norm=m_attn_pre_norm, w_in=m_w_in, hgrn_lb=m_hgrn_lb, hgrn_gnorm=m_hgrn_gnorm, w_branch_a=m_w_branch_a, rwkv_mu=m_rwkv_mu, rwkv_w0=m_rwkv_w0, rwkv_w2=m_rwkv_w2, rwkv_a0=m_rwkv_a0, rwkv_a2=m_rwkv_a2, rwkv_g2=m_rwkv_g2, rwkv_k_k=m_rwkv_k_k, rwkv_k_a=m_rwkv_k_a, rwkv_r_k=m_rwkv_r_k, rwkv_ln_w=m_rwkv_ln_w, rwkv_ln_b=m_rwkv_ln_b, w_branch_b=m_w_branch_b, w_out=m_w_out, attn_post_norm=m_attn_post_norm, ffn_pre_norm=m_ffn_pre_norm, w_up=m_w_up, conv_w=m_conv_w, conv_b=m_conv_b, w_down=m_w_down, ffn_post_norm=m_ffn_post_norm)
    vo = dict(attn_pre_norm=v_attn_pre_norm, w_in=v_w_in, hgrn_lb=v_hgrn_lb, hgrn_gnorm=v_hgrn_gnorm, w_branch_a=v_w_branch_a, rwkv_mu=v_rwkv_mu, rwkv_w0=v_rwkv_w0, rwkv_w2=v_rwkv_w2, rwkv_a0=v_rwkv_a0, rwkv_a2=v_rwkv_a2, rwkv_g2=v_rwkv_g2, rwkv_k_k=v_rwkv_k_k, rwkv_k_a=v_rwkv_k_a, rwkv_r_k=v_rwkv_r_k, rwkv_ln_w=v_rwkv_ln_w, rwkv_ln_b=v_rwkv_ln_b, w_branch_b=v_w_branch_b, w_out=v_w_out, attn_post_norm=v_attn_post_norm, ffn_pre_norm=v_ffn_pre_norm, w_up=v_w_up, conv_w=v_conv_w, conv_b=v_conv_b, w_down=v_w_down, ffn_post_norm=v_ffn_post_norm)

    t = x.shape[1]
    x2 = x.reshape(t, D)
    tgt = loss_target.reshape(t, D)
    st = _stages()

    me = 4 * lax.axis_index("x") + 2 * lax.axis_index("y") + lax.axis_index("c")
    small = jnp.concatenate([rwkv_w2[0], rwkv_a2[0], rwkv_g2[0]], axis=0).astype(BF)
    g_in, g_small = _all_gather("gather_weights", [w_in[0].T.astype(BF), small])
    fw_in_t = g_in.reshape(IN_COLS, D)
    z64 = jnp.zeros((64, D), BF)
    w2p = jnp.concatenate([_blocks_to_cols(g_small[:, 0:64]), z64], axis=0)
    a2p = jnp.concatenate([z64, _blocks_to_cols(g_small[:, 64:128])], axis=0)
    g2f = _blocks_to_cols(g_small[:, 128:256])
    conv_bits = jnp.pad(lax.bitcast_convert_type(conv_w[0], BF).reshape(3, 2 * 704), ((0, 29), (0, 0)))
    late = [w_up[0].T.astype(BF)] + [w[k][0].astype(BF) for k in _BIG[2:]] + [conv_bits]
    late_gather = _Exchange("gather2", late)
    r_k = rwkv_r_k.reshape(1, D)

    (xn,), _ = _stage_fwd(st["pre1"], t, [attn_pre_norm], [x2])
    z = _mm("in_proj", xn, fw_in_t, "nt", F32, tm=512, tn=4736, b_outer=True)
    mix_par = [hgrn_lb, hgrn_gnorm, rwkv_mu, rwkv_w0, w2p, rwkv_a0, a2p, g2f, rwkv_k_k, rwkv_k_a,
               rwkv_ln_w, rwkv_ln_b, r_k]
    mix_in = [z]
    (o_a, o_b), mix_saved = _stage_fwd(st["mixers"], t, mix_par, mix_in, hook=late_gather)
    gl = [lax.dynamic_update_slice(g, own[None], (me, 0, 0)) for g, own in zip(late_gather.results, late)]
    fw_up_t = gl[0].reshape(2 * DFF, D)
    fw_down = gl[1].reshape(DFF, D)
    fw_a, fw_b, fw_out = (g.reshape(D, D) for g in gl[2:5])
    conv_full = _blocks_to_cols(lax.bitcast_convert_type(gl[5][:, :3].reshape(N_DEV, 3, 704, 2), F32))
    y_a = _mm("branch_a", o_a, fw_a, "nn")
    y_b = _mm("branch_b", o_b, fw_b, "nn")
    (merged,), _ = _stage_fwd(st["merge"], t, [], [z, z, y_a, y_b])
    mix = _mm("out_proj", merged, fw_out, "nn")
    (h1, xn2), _ = _stage_fwd(st["post1"], t, [attn_post_norm, ffn_pre_norm], [x2, mix])
    hu = _mm("up_proj", xn2, fw_up_t, "nt", F32, tm=1024, tn=1408)
    conv_par = [conv_full, conv_b]
    (act,), conv_saved = _stage_fwd(st["conv"], t, conv_par, [hu])
    ff = _mm("down_proj", act, fw_down, "nn")

    loss_acc, d_ffn_post, dh1, dff = _loss_stage(t, ffn_post_norm, h1, ff, tgt)
    dact = _mm("d_act", dff, fw_down, "nt", BF, tm=1024, tn=1408)
    dw_down = _mm("dw_down", act, dff, "tn", BF, tm=1408, tn=512)
    (dcw, dcb), (dhu,) = _stage_bwd(st["conv"], t, conv_par, [hu], conv_saved, [[dact]], [BF])
    dxn2 = _mm("d_xn2", dhu, fw_up_t, "nn", F32, tm=1024, tn=1024)
    dw_up_t = _mm("dw_up", dhu, xn2, "tn", BF, tm=1408, tn=1024)
    (d_post, d_pre2), (dx_a, dmix) = _stage_bwd(st["post1"], t, [attn_post_norm, ffn_pre_norm], [x2, mix], [],
                                                 [[dh1], [dxn2]], [F32, BF])
    dmerged = _mm("d_merged", dmix, fw_out, "nt", BF)
    dw_out = _mm("dw_out", merged, dmix, "tn", BF)
    _, (dga, dgb, dy_a, dy_b) = _stage_bwd(st["merge"], t, [], [z, z, y_a, y_b], [], [[dmerged]], [BF, BF, BF, BF])
    do_a = _mm("d_oa", dy_a, fw_a, "nt", BF)
    dw_a = _mm("dw_a", o_a, dy_a, "tn", BF)
    do_b = _mm("d_ob", dy_b, fw_b, "nt", BF)
    dw_b = _mm("dw_b", o_b, dy_b, "tn", BF)
    early = [dw_up_t.reshape(N_DEV, 704, D), dw_down.reshape(N_DEV, 352, D), dw_a.reshape(N_DEV, 128, D),
             dw_b.reshape(N_DEV, 128, D), dw_out.reshape(N_DEV, 128, D), _cols_to_blocks(dcw.astype(BF), 704)]
    early_scatter = _Exchange("scatter", early)
    mix_dp, dz_hr = _stage_bwd(st["mixers"], t, mix_par, mix_in, mix_saved, [[do_a], [do_b]], [BF],
                               hook=early_scatter)
    d_lb, d_gn, d_mu, d_w0, d_w2p, d_a0, d_a2p, d_g2, d_kk, d_ka, d_lnw, d_lnb, d_rk = mix_dp
    dz = dz_hr + [dga, dgb]
    dw_in_t = _mm_cols_tn("dw_in", dz, xn, BF, 256)

    ax, ay, ac = lax.axis_index("x"), lax.axis_index("y"), lax.axis_index("c")
    idx4 = jnp.stack([4 * cx + 2 * cy + ac for cx, cy in ((ax, ay), (1 - ax, ay), (ax, 1 - ay), (1 - ax, 1 - ay))])
    idx4 = idx4.astype(jnp.int32)
    idx_me, idx_0 = idx4[0:1], jnp.zeros((1,), jnp.int32)
    d_small = jnp.concatenate([d_w2p[:64], d_a2p[64:], d_g2], axis=0).astype(BF)
    g8s = [dw_in_t.reshape(N_DEV, 1184, D), _cols_to_blocks(d_small, LANES)]
    recv4s = _reduce_pair(g8s)
    sums = [_pair_sum("pair_sum_" + n, idx4, g, r) for n, g, r in zip(("w_in", "small"), g8s, recv4s)]
    swap_ssem, swap_rsem, swap_srcs, swap_lands, token = _chip_swap_start([s[1] for s in sums])
    dxn = _mm_cols_nn("d_xn", dz, fw_in_t, BF, 512, token=token)
    (d_pre1,), (dx,) = _stage_bwd(st["pre1_res"], t, [attn_pre_norm], [x2], [], [[dxn], [dx_a]], [F32])
    grad_x = dx.reshape(x.shape)

    rg = dict(attn_pre_norm=d_pre1, hgrn_lb=d_lb, hgrn_gnorm=d_gn, rwkv_mu=d_mu, rwkv_w0=d_w0, rwkv_a0=d_a0,
              rwkv_k_k=d_kk, rwkv_k_a=d_ka, rwkv_r_k=d_rk, rwkv_ln_w=d_lnw, rwkv_ln_b=d_lnb, attn_post_norm=d_post,
              ffn_pre_norm=d_pre2, conv_b=dcb, ffn_post_norm=d_ffn_post)
    (g8,) = _all_gather("gather_small_grads", [_pack_replicated(rg, loss_acc)])
    rnames = [n for n, _ in REPL]
    flat = lambda src: [src[n].reshape(1, D) if n == "rwkv_r_k" else src[n] for n in rnames]
    rp_out, loss_row = _adam_replicated(g8, flat(w), flat(mo), flat(vo))
    loss = loss_row[0, 0]
    recv3s = _chip_swap_wait(swap_ssem, swap_rsem, swap_srcs, swap_lands, rp_out[0]["attn_pre_norm"])
    for kind in range(4):
        rp_out[kind]["rwkv_r_k"] = rp_out[kind]["rwkv_r_k"].reshape(rwkv_r_k.shape)

    def small_of(src):
        return jnp.concatenate([src["rwkv_w2"][0], src["rwkv_a2"][0], src["rwkv_g2"][0]], axis=0)

    sh_out = [dict() for _ in range(4)]
    g_in = _sum_partials("sum_w_in", idx_0, sums[0][0][None], recv3s[0]).T
    res = _adam_sharded("adam_w_in", idx_0, g_in[None], None, *[src["w_in"][0] for src in (w, mo, vo)])
    res_s = _adam_sharded("adam_small", idx_0, sums[1][0][None], recv3s[1], *[small_of(src) for src in (w, mo, vo)])
    for kind in range(4):
        sh_out[kind]["w_in"] = res[kind][None]
        sh_out[kind]["rwkv_w2"] = res_s[kind][0:64][None]
        sh_out[kind]["rwkv_a2"] = res_s[kind][64:128][None]
        sh_out[kind]["rwkv_g2"] = res_s[kind][128:256][None]
    for n, own, recv in zip(_BIG[1:] + ("conv_w",), early, early_scatter.results):
        if n == "w_up":
            g_up = _sum_partials("sum_w_up", idx_me, own, recv).T
            res = _adam_sharded("adam_" + n, idx_0, g_up[None], None, *[src[n][0] for src in (w, mo, vo)])
        else:
            res = _adam_sharded("adam_" + n, idx_me, own, recv, *[src[n][0] for src in (w, mo, vo)])
        for kind in range(4):
            sh_out[kind][n] = res[kind][None]

    outs = [loss, grad_x]
    for kind in range(4):
        for name in _WEIGHTS:
            outs.append(sh_out[kind][name] if name in sh_out[kind] else rp_out[kind][name])
    return tuple(outs)
```

```python
import functools

import jax
import jax.numpy as jnp
from jax import lax
from jax.experimental import pallas as pl
from jax.experimental.pallas import tpu as pltpu

F32 = jnp.float32
BF = jnp.bfloat16
MESH = pl.DeviceIdType.MESH

D = 1024
HG_HEADS = 8
HG_K = 128
HG_CHUNK = 32
HG_SCALE = HG_K ** -0.5
HG_PER_STEP = 8
RW_HEADS = 16
RW_N = 64
RW_CHUNK = 64
RW_PAIRS_PER_STEP = 8
DFF = 2816
IN_COLS = 9472
RW_COLS = 3328
EPS = 1e-6
GN_EPS = 1e-5 * RW_N
ADAM_LR = 0.001
ADAM_B1 = 0.9
ADAM_B2 = 0.999
ADAM_EPS = 1e-08
ADAM_WD = 0.01
ADAM_STEP = 10
N_DEV = 8
LANES = 128
VMEM_LIMIT = 56 * 1024 * 1024
TILE_BYTES = 1280 * 1024

REPL = (("attn_pre_norm", 1024), ("hgrn_lb", 1024), ("hgrn_gnorm", 1024), ("rwkv_mu", 3328), ("rwkv_w0", 1024),
        ("rwkv_a0", 1024), ("rwkv_k_k", 1024), ("rwkv_k_a", 1024), ("rwkv_r_k", 1024), ("rwkv_ln_w", 1024),
        ("rwkv_ln_b", 1024), ("attn_post_norm", 1024), ("ffn_pre_norm", 1024), ("conv_b", 5632), ("ffn_post_norm", 1024))
REPL_ROWS = {"hgrn_lb": 2}
REPL_TOTAL = 32


def _cparams(sem=None, **kw):
    return pltpu.CompilerParams(dimension_semantics=sem, vmem_limit_bytes=VMEM_LIMIT, **kw)


_DN = {"nn": ((1,), (0,)), "nt": ((1,), (1,)), "tn": ((0,), (0,))}


def _raw_dot(a, b, mode):
    return lax.dot_general(a.astype(BF), b.astype(BF), (_DN[mode], ((), ())), preferred_element_type=F32)


@functools.partial(jax.custom_vjp, nondiff_argnums=(2,))
def _dot(a, b, mode):
    return _raw_dot(a, b, mode)


def _dot_fwd(a, b, mode):
    return _raw_dot(a, b, mode), (a, b)


def _dot_bwd(mode, res, g):
    a, b = res
    if mode == "nn":
        return _dot(g, b, "nt"), _dot(a, g, "tn")
    if mode == "nt":
        return _dot(g, b, "nn"), _dot(g, a, "tn")
    return _dot(b, g, "nt"), _dot(a, g, "nn")


_dot.defvjp(_dot_fwd, _dot_bwd)


def _bf_pieces(x, n):
    out, r = [], x
    for i in range(n):
        p = r.astype(BF)
        out.append(p)
        if i + 1 < n:
            r = r - p.astype(F32)
    return out


def _raw_split_dot(x, e, mode, n, x_left):
    eb = e.astype(BF)
    acc = None
    for p in _bf_pieces(x, n):
        ops = (p, eb) if x_left else (eb, p)
        t = lax.dot_general(*ops, (_DN[mode], ((), ())), preferred_element_type=F32)
        acc = t if acc is None else acc + t
    return acc


def _raw_headsum(x):
    t = x.shape[0]
    i = lax.broadcasted_iota(jnp.int32, (LANES, LANES), 0)
    j = lax.broadcasted_iota(jnp.int32, (LANES, LANES), 1)
    same = jnp.where((i >= RW_N) == (j >= RW_N), 1.0, 0.0).astype(F32)
    groups = x.shape[1] // LANES
    rows = jnp.concatenate([x[:, q * LANES:(q + 1) * LANES] for q in range(groups)], axis=0)
    s = _raw_split_dot(rows, same, "nn", 2, True)
    return jnp.concatenate([s[q * t:(q + 1) * t] for q in range(groups)], axis=1)


@jax.custom_vjp
def _headsum(x):
    return _raw_headsum(x)


def _headsum_fwd(x):
    return _raw_headsum(x), None


def _headsum_bwd(_, g):
    return (_raw_headsum(g),)


_headsum.defvjp(_headsum_fwd, _headsum_bwd)


@functools.partial(jax.custom_vjp, nondiff_argnums=(2,))
def _tdot(tri, x, n):
    return _raw_split_dot(x, tri, "nn", n, False)


def _tdot_fwd(tri, x, n):
    return _raw_split_dot(x, tri, "nn", n, False), tri


def _tdot_bwd(n, tri, g):
    return jnp.zeros_like(tri), _raw_split_dot(g, tri, "tn", n, False)


_tdot.defvjp(_tdot_fwd, _tdot_bwd)


def _row(x, i):
    r = lax.broadcasted_iota(jnp.int32, x.shape, 0)
    return jnp.sum(jnp.where(r == i, x, 0.0), axis=0, keepdims=True)


def _shift_down(x, prev):
    t = x.shape[0]

    @jax.custom_vjp
    def sh(x, prev):
        r = lax.broadcasted_iota(jnp.int32, x.shape, 0)
        return jnp.where(r == 0, prev, pltpu.roll(x, 1, 0))

    def fwd(x, prev):
        return sh(x, prev), None

    def bwd(_, g):
        r = lax.broadcasted_iota(jnp.int32, g.shape, 0)
        dx = jnp.where(r == t - 1, 0.0, pltpu.roll(g, t - 1, 0))
        return dx, jnp.sum(jnp.where(r == 0, g, 0.0), axis=0, keepdims=True)

    sh.defvjp(fwd, bwd)
    return sh(x, prev)


def _sigmoid(x):
    return jax.nn.sigmoid(x)


def _silu(x):
    return x * jax.nn.sigmoid(x)


def _softplus(x):
    return jnp.maximum(x, 0.0) + jnp.log(1.0 + jnp.exp(-jnp.abs(x)))


def _rms(x, g):
    return (x * lax.rsqrt(jnp.mean(x * x, axis=-1, keepdims=True) + EPS)) * g


def _tril(c):
    r = lax.broadcasted_iota(jnp.int32, (c, c), 0)
    cc = lax.broadcasted_iota(jnp.int32, (c, c), 1)
    return cc <= r


def _f_pre1(ps, xs, cs):
    return [_rms(xs[0], ps[0])], []


def _f_pre1_residual(ps, xs, cs):
    return [_rms(xs[0], ps[0]), xs[0]], []


def _f_hgrn(ps, xs, cs):
    lbraw, gn = ps
    hq, hf, hi, hg = xs
    hd = range(HG_PER_STEP)
    st = [cs[0][p * HG_K:(p + 1) * HG_K] for p in hd]
    l0, l1 = _row(lbraw, 0), _row(lbraw, 1)
    m = jnp.maximum(l0, l1)
    e0, e1 = jnp.exp(l0 - m), jnp.exp(l1 - m)
    lb = e0 / (e0 + e1)
    q = _silu(hq) * HG_SCALE
    f = lb + (1.0 - lb) * _sigmoid(hf)
    kh = 1.0 - f
    gl = jnp.log(f)
    c = HG_CHUNK
    low = _tril(c)
    tri = jnp.where(low, 1.0, 0.0).astype(F32)
    outs = []
    for i in range(hq.shape[0] // c):
        rows = slice(i * c, (i + 1) * c)
        b = _tdot(tri, gl[rows], 3)
        bref = _row(b, c // 2 - 1)
        blast = _row(b, c - 1)
        qi = q[rows] * jnp.exp(b - bref)
        ki = kh[rows] * jnp.exp(bref - b)
        qd = q[rows] * jnp.exp(b)
        kd = kh[rows] * jnp.exp(blast - b)
        dec = jnp.exp(blast)
        sl = [slice(p * HG_K, (p + 1) * HG_K) for p in hd]
        sc = [jnp.where(low, _dot(qi[:, sl[p]], ki[:, sl[p]], "nt"), 0.0) for p in hd]
        o = [_dot(sc[p], hi[rows, sl[p]], "nn") + _dot(qd[:, sl[p]], st[p], "nt") for p in hd]
        u = [_dot(hi[rows, sl[p]], kd[:, sl[p]], "tn") for p in hd]
        st = [dec[:, sl[p]] * st[p] + u[p] for p in hd]
        outs.append(jnp.concatenate(o, axis=1) if len(o) > 1 else o[0])
    o = outs[0] if len(outs) == 1 else jnp.concatenate(outs, axis=0)
    on = []
    for p in hd:
        op = o[:, p * HG_K:(p + 1) * HG_K]
        on.append(op * lax.rsqrt(jnp.mean(op * op, axis=-1, keepdims=True) + EPS))
    o = jnp.concatenate(on, axis=1) if len(on) > 1 else on[0]
    o = o * gn
    return [o * _silu(hg)], [jnp.concatenate(st, axis=0) if len(st) > 1 else st[0]]


_RW_OFFS = (0, 1024, 2048, 3072, 3200, 3328)


def _f_rwpre(ps, xs, cs):
    mu, w0, w2p, a0, a2p, g2, k_k, k_a = ps
    (prev,) = cs
    t = xs[0].shape[0]
    zs = []
    for i, z in enumerate(xs):
        lo, hi = _RW_OFFS[i], _RW_OFFS[i + 1]
        zs.append(z + mu[:, lo:hi] * (_shift_down(z, prev[:, lo:hi]) - z))
    rr, kr, vr, wa, gz = zs
    w_log = -_softplus(-(w0 + _dot(jnp.tanh(wa), w2p, "nn"))) - 0.5
    lw = -jnp.exp(w_log)
    a = _sigmoid(a0 + _dot(wa, a2p, "nn"))
    g = _dot(_sigmoid(gz), g2, "nn")
    kkr = kr * k_k
    kk = kkr / jnp.maximum(jnp.sqrt(_headsum(kkr * kkr)), 1e-12)
    k2 = kr * (1.0 + (a - 1.0) * k_a)
    newprev = jnp.concatenate([_row(z, t - 1) for z in xs], axis=1)
    return [rr, lw, k2, vr, -kk, kk * a, g], [newprev]


def _raw_inverses(ls):
    n = ls[0].shape[0]
    r = lax.broadcasted_iota(jnp.int32, (n, n), 0)
    c = lax.broadcasted_iota(jnp.int32, (n, n), 1)
    eye = jnp.where(r == c, 1.0, 0.0).astype(F32)
    tinv = [eye + l for l in ls]
    pw = ls
    for _ in range(5):
        pw = [_raw_dot(p, p, "nn") for p in pw]
        tinv = [t + _raw_dot(t, p, "nn") for t, p in zip(tinv, pw)]
    return tinv


@jax.custom_vjp
def _unit_lower_inverses(ls):
    return _raw_inverses(ls)


def _inverses_fwd(ls):
    tinv = _raw_inverses(ls)
    return tinv, tinv


def _inverses_bwd(tinv, gs):
    return ([_raw_dot(_raw_dot(t, g, "tn"), t, "nt") for t, g in zip(tinv, gs)],)


_unit_lower_inverses.defvjp(_inverses_fwd, _inverses_bwd)


@jax.custom_vjp
def _known_inverses(ls, tinv):
    return tinv


def _known_fwd(ls, tinv):
    return tinv, tinv


def _known_bwd(tinv, gs):
    return [_raw_dot(_raw_dot(t, g, "tn"), t, "nt") for t, g in zip(tinv, gs)], [jnp.zeros_like(t) for t in tinv]


_known_inverses.defvjp(_known_fwd, _known_bwd)


@jax.custom_vjp
def _use_kept(computed, kept):
    return kept


def _use_kept_fwd(computed, kept):
    return kept, None


def _use_kept_bwd(_, g):
    return g, jax.tree.map(jnp.zeros_like, g)


_use_kept.defvjp(_use_kept_fwd, _use_kept_bwd)

RW_KEPT = 5


def _f_rwscan(ps, xs, cs, kept=None):
    state = cs[0]
    ys, keep = [], []
    n = 2 * RW_CHUNK
    per_chunk = RW_KEPT * RW_PAIRS_PER_STEP * n
    for i in range(xs[0].shape[0] // RW_CHUNK):
        known = None
        if kept is not None:
            known = [[kept[i * per_chunk + (q * RW_PAIRS_PER_STEP + p) * n:
                           i * per_chunk + (q * RW_PAIRS_PER_STEP + p + 1) * n] for p in range(RW_PAIRS_PER_STEP)]
                     for q in range(RW_KEPT)]
        y, state, mats = _rwkv_chunk([x[i * RW_CHUNK:(i + 1) * RW_CHUNK] for x in xs], state, known)
        ys.append(y)
        keep += [m for group in mats for m in group]
    return [ys[0] if len(ys) == 1 else jnp.concatenate(ys, axis=0)], [state], jnp.concatenate(keep, axis=0)


def _rwkv_chunk(xs, state, known=None):
    npair = RW_PAIRS_PER_STEP
    pr = range(npair)
    r, lw, k, v, av, bv = [[x[:, p * LANES:(p + 1) * LANES] for p in pr] for x in xs]
    sv = [state[p * LANES:(p + 1) * LANES] for p in pr]
    c = RW_CHUNK
    n = 2 * c
    tri = jnp.where(_tril(c), 1.0, 0.0).astype(F32)
    cl = [_tdot(tri, lw[p], 3) for p in pr]
    cl_last = [_row(cl[p], c - 1) for p in pr]
    lane = lax.broadcasted_iota(jnp.int32, (c, LANES), 1)
    h0 = lane < RW_N

    def stack(x):
        return jnp.concatenate([jnp.where(h0, x, 0.0), jnp.where(h0, 0.0, x)], axis=0)

    am = [stack(av[p] * jnp.exp(cl[p] - lw[p])) for p in pr]
    bm = [stack(bv[p] * jnp.exp(-cl[p])) for p in pr]
    km = [stack(k[p] * jnp.exp(-cl[p])) for p in pr]
    rm = [stack(r[p] * jnp.exp(cl[p])) for p in pr]
    vm = [stack(v[p]) for p in pr]
    rn = lax.broadcasted_iota(jnp.int32, (n, n), 0)
    cn = lax.broadcasted_iota(jnp.int32, (n, n), 1)
    blk = (rn >= c) == (cn >= c)
    strict = blk & (cn < rn)
    incl = blk & (cn <= rn)
    lab = [jnp.where(strict, _dot(am[p], bm[p], "nt"), 0.0) for p in pr]
    lak = [jnp.where(strict, _dot(am[p], km[p], "nt"), 0.0) for p in pr]
    wrb = [jnp.where(incl, _dot(rm[p], bm[p], "nt"), 0.0) for p in pr]
    wrk = [jnp.where(incl, _dot(rm[p], km[p], "nt"), 0.0) for p in pr]
    if known is None:
        tinv = _unit_lower_inverses(lab)
    else:
        tinv = _known_inverses(lab, known[0])
        lak, wrb, wrk = _use_kept(lak, known[1]), _use_kept(wrb, known[2]), _use_kept(wrk, known[3])
    rhs = [_dot(am[p], sv[p], "nt") + _dot(lak[p], vm[p], "nn") for p in pr]
    um = [_dot(tinv[p], rhs[p], "nn") for p in pr]
    if known is not None:
        um = _use_kept(um, known[4])
    ym = [_dot(rm[p], sv[p], "nt") + _dot(wrb[p], um[p], "nn") + _dot(wrk[p], vm[p], "nn") for p in pr]
    sn = [(sv[p] + _dot(um[p], bm[p], "tn") + _dot(vm[p], km[p], "tn")) * jnp.exp(cl_last[p]) for p in pr]
    ys = [ym[p][:c] + ym[p][c:] for p in pr]
    return jnp.concatenate(ys, axis=1), jnp.concatenate(sn, axis=0), [tinv, lak, wrb, wrk, um]


def _f_mixers(ps, xs, cs):
    return _mixers(ps, xs, cs, None)


def _f_mixers_kept(ps, xs, cs, kept):
    return _mixers(ps, xs, cs, kept[0])[:2]


def _mixers(ps, xs, cs, kept):
    oa, st = _f_hgrn(ps[:2], xs[:4], cs[:1])
    (r, lw, k, v, av, bv, g), prev = _f_rwpre(ps[2:10], xs[4:], cs[1:2])
    y, sv, keep = _f_rwscan([], [r, lw, k, v, av, bv], cs[2:], kept)
    ob, _ = _f_rwpost(ps[10:], y + [r, k, v, g], [])
    return oa + ob, st + prev + sv, [keep]


def _f_rwpost(ps, xs, cs):
    ln_w, ln_b, r_k = ps
    y, r, k, v, g = xs
    inv_n = 1.0 / RW_N
    yc = y - _headsum(y) * inv_n
    var = _headsum(yc * yc) * inv_n
    yn = yc * lax.rsqrt(var + GN_EPS)
    yn = yn * ln_w + ln_b
    bonus = _headsum(r * k * r_k) * v
    return [(yn + bonus) * g], []


def _f_merge(ps, xs, cs):
    ga, gb, ya, yb = xs
    return [_sigmoid(ga) * ya + _sigmoid(gb) * yb], []


def _f_post1(ps, xs, cs):
    x, mix = xs
    h1 = x + _rms(mix, ps[0])
    return [h1, _rms(h1, ps[1])], []


def _f_conv(ps, xs, cs):
    cw, cb = ps
    p1, p2 = cs
    w0, w1, w2 = _row(cw, 0), _row(cw, 1), _row(cw, 2)
    t = xs[0].shape[0]
    hc = []
    for i, x in enumerate(xs):
        sl = slice(i * DFF, (i + 1) * DFF)
        s1 = _shift_down(x, p1[:, sl])
        s2 = _shift_down(s1, p2[:, sl])
        hc.append(cb[:, sl] + w0[:, sl] * s2 + w1[:, sl] * s1 + w2[:, sl] * x)
    n1 = jnp.concatenate([_row(x, t - 1) for x in xs], axis=1)
    n2 = jnp.concatenate([_row(x, t - 2) for x in xs], axis=1)
    return [_silu(hc[0]) * hc[1]], [n1, n2]


class _Stage:
    def __init__(self, name, f, g, tm, par_per_g, in_pieces, in_offs, carry_shapes, out_pieces, out_dtypes,
                 kept_shapes=(), f_kept=None):
        self.name, self.f, self.g, self.tm = name, f, g, tm
        self.par_per_g, self.in_pieces, self.in_offs = par_per_g, in_pieces, in_offs
        self.carry_shapes, self.out_pieces, self.out_dtypes = carry_shapes, out_pieces, out_dtypes
        self.kept_shapes, self.f_kept = list(kept_shapes), f_kept


def _par_spec(arr, per_g, g):
    r, c = arr.shape
    if per_g:
        return pl.BlockSpec((r, c // g), lambda gi, ni: (0, gi))
    return pl.BlockSpec((r, c), lambda gi, ni: (0, 0))


def _row_spec(tm, width, off, n, rev):
    if rev:
        return pl.BlockSpec((tm, width), lambda gi, ni: (n - 1 - ni, off + gi))
    return pl.BlockSpec((tm, width), lambda gi, ni: (ni, off + gi))


def _carry_spec(shape, n, rev):
    if rev:
        return pl.BlockSpec((None, None) + shape, lambda gi, ni: (gi, n - 1 - ni, 0, 0))
    return pl.BlockSpec((None, None) + shape, lambda gi, ni: (gi, ni, 0, 0))


def _load_pieces(refs, pieces_list):
    out = []
    for ref, pieces in zip(refs, pieces_list):
        o = 0
        for w in pieces:
            out.append(ref[:, o:o + w].astype(F32))
            o += w
    return out


def _store_pieces(refs, pieces_list, vals):
    k = 0
    for ref, pieces in zip(refs, pieces_list):
        o = 0
        for w in pieces:
            ref[:, o:o + w] = vals[k].astype(ref.dtype)
            k += 1
            o += w


_ANY = pl.BlockSpec(memory_space=pl.ANY)


class _Exchange:
    def __init__(self, kind, arrs):
        self.kind, self.arrs, self.results = kind, list(arrs), None
        if kind == "scatter":
            self.out_shape = [jax.ShapeDtypeStruct((N_DEV - 1,) + a.shape[1:], a.dtype) for a in self.arrs]
        else:
            self.out_shape = [jax.ShapeDtypeStruct((N_DEV,) + a.shape, a.dtype) for a in self.arrs]
        self.nsem = (N_DEV if kind == "gather2" else N_DEV - 1) * len(self.arrs)

    def copies(self, in_refs, out_refs, ssem, rsem):
        x, y, c = lax.axis_index("x"), lax.axis_index("y"), lax.axis_index("c")
        me = 4 * x + 2 * y + c
        cps = []
        for a, (i_ref, o_ref) in enumerate(zip(in_refs, out_refs)):
            for j in range(1, N_DEV):
                px = 1 - x if j & 4 else x
                py = 1 - y if j & 2 else y
                pc = 1 - c if j & 1 else c
                if self.kind == "gather":
                    src, dst = i_ref, o_ref.at[me]
                else:
                    src, dst = i_ref.at[4 * px + 2 * py + pc], o_ref.at[j - 1]
                s = (N_DEV - 1) * a + j - 1
                cps.append(pltpu.make_async_remote_copy(src_ref=src, dst_ref=dst, send_sem=ssem.at[s],
                                                        recv_sem=rsem.at[s], device_id=(px, py, pc),
                                                        device_id_type=MESH))
        return cps

    def run(self, step, total, in_refs, out_refs, ssem, rsem):
        if self.kind == "gather2":
            return self.run_two_level(step, total, in_refs, out_refs, ssem, rsem)

        @pl.when(step == 0)
        def _():
            for cp in self.copies(in_refs, out_refs, ssem, rsem):
                cp.start()

        @pl.when(step == total - 1)
        def _():
            for cp in self.copies(in_refs, out_refs, ssem, rsem):
                cp.wait()

    def run_two_level(self, step, total, in_refs, out_refs, ssem, rsem):
        x, y, c = lax.axis_index("x"), lax.axis_index("y"), lax.axis_index("c")
        sibling, xn, yn = (x, y, 1 - c), (1 - x, y, c), (x, 1 - y, c)
        arrs = range(len(in_refs))
        ns = N_DEV

        def num(px, py, pc):
            return 4 * px + 2 * py + pc

        def copy(a, k, to, src, dst):
            return pltpu.make_async_remote_copy(src_ref=src, dst_ref=dst, send_sem=ssem.at[ns * a + k],
                                                recv_sem=rsem.at[ns * a + k], device_id=to, device_id_type=MESH)

        def blk(a, b):
            return out_refs[a].at[b]

        def half(a, b, second):
            h = self.arrs[a].shape[0] // 2
            return out_refs[a].at[b, pl.ds(h if second else 0, h)]

        bx, by, bd = num(1 - x, y, c), num(x, 1 - y, c), num(1 - x, 1 - y, c)

        def firsts(a):
            own = blk(a, num(x, y, c))
            return [copy(a, 0, sibling, in_refs[a], own), copy(a, 1, xn, in_refs[a], own),
                    copy(a, 2, yn, in_refs[a], own)]

        def seconds(a):
            return [copy(a, 3, yn, half(a, bx, False), half(a, bx, False)), copy(a, 5, sibling, blk(a, bx), blk(a, bx)),
                    copy(a, 4, xn, half(a, by, True), half(a, by, True)), copy(a, 6, sibling, blk(a, by), blk(a, by))]

        def third(a):
            return copy(a, 7, sibling, blk(a, bd), blk(a, bd))

        @pl.when(step == 0)
        def _():
            for a in arrs:
                for cp in firsts(a):
                    cp.start()

        @pl.when(step == total // 2)
        def _():
            for a in arrs:
                copy(a, 1, xn, blk(a, bx), blk(a, bx)).wait_recv()
                copy(a, 2, yn, blk(a, by), blk(a, by)).wait_recv()
                for cp in seconds(a):
                    cp.start()

        @pl.when(step == (4 * total) // 5)
        def _():
            for a in arrs:
                copy(a, 3, yn, half(a, bd, False), half(a, bd, False)).wait_recv()
                copy(a, 4, xn, half(a, bd, True), half(a, bd, True)).wait_recv()
                third(a).start()

        @pl.when(step == total - 1)
        def _():
            for a in arrs:
                for k, b in ((0, num(x, y, 1 - c)), (5, num(1 - x, y, 1 - c)), (6, num(x, 1 - y, 1 - c)),
                             (7, num(1 - x, 1 - y, 1 - c))):
                    copy(a, k, sibling, blk(a, b), blk(a, b)).wait_recv()
                for cp in firsts(a) + seconds(a) + [third(a)]:
                    cp.wait_send()


def _hook_specs(hook):
    if hook is None:
        return [], [], [], []
    na = len(hook.arrs)
    sems = [pltpu.SemaphoreType.DMA((hook.nsem,)), pltpu.SemaphoreType.DMA((hook.nsem,))]
    return [_ANY] * na, [_ANY] * na, hook.out_shape, sems


def _stage_fwd(st, t, params, inputs, hook=None):
    g, tm = st.g, min(st.tm, t)
    n = t // tm
    npar, nin, ncar, nout = len(params), len(inputs), len(st.carry_shapes), len(st.out_pieces)
    nk = len(st.kept_shapes)
    h_in, h_out, h_shape, h_sems = _hook_specs(hook)
    nh = len(h_in)

    def body(*refs):
        p_refs = refs[:npar]
        x_refs = refs[npar:npar + nin]
        hi_refs = refs[npar + nin:npar + nin + nh]
        o = npar + nin + nh
        o_refs = refs[o:o + nout]
        s_refs = refs[o + nout:o + nout + ncar]
        k_refs = refs[o + nout + ncar:o + nout + ncar + nk]
        o += nout + ncar + nk
        ho_refs = refs[o:o + nh]
        c_scr = refs[o + nh:o + nh + ncar]
        gi, ni = pl.program_id(0), pl.program_id(1)
        if hook is not None:
            step = gi * n + ni
            hook.run(step, g * n, hi_refs, ho_refs, *refs[-2:])

        @pl.when(ni == 0)
        def _():
            for c in c_scr:
                c[...] = jnp.zeros(c.shape, F32)

        ps = [r[...].astype(F32) for r in p_refs]
        xs = _load_pieces(x_refs, st.in_pieces)
        cs = [c[...] for c in c_scr]
        for s, c in zip(s_refs, cs):
            s[...] = c
        res = st.f(ps, xs, cs)
        outs, ncs = res[0], res[1]
        _store_pieces(o_refs, st.out_pieces, outs)
        for c, v in zip(c_scr, ncs):
            c[...] = v
        for kr, kv in zip(k_refs, res[2] if nk else []):
            kr[...] = kv.astype(kr.dtype)

    in_specs = [_par_spec(p, pg, g) for p, pg in zip(params, st.par_per_g)]
    in_specs += [_row_spec(tm, sum(pc), off, n, False) for pc, off in zip(st.in_pieces, st.in_offs)]
    out_specs = [_row_spec(tm, sum(pc), 0, n, False) for pc in st.out_pieces]
    out_specs += [_carry_spec(s, n, False) for s in st.carry_shapes]
    out_specs += [pl.BlockSpec(s, lambda gi, ni: (ni, 0)) for s in st.kept_shapes]
    out_shape = [jax.ShapeDtypeStruct((t, g * sum(pc)), dt) for pc, dt in zip(st.out_pieces, st.out_dtypes)]
    out_shape += [jax.ShapeDtypeStruct((g, n) + s, F32) for s in st.carry_shapes]
    out_shape += [jax.ShapeDtypeStruct((n * s[0], s[1]), BF) for s in st.kept_shapes]
    res = pl.pallas_call(
        body, name=st.name + "_fwd", grid=(g, n), in_specs=in_specs + h_in, out_specs=out_specs + h_out,
        out_shape=out_shape + h_shape,
        scratch_shapes=[pltpu.VMEM(s, F32) for s in st.carry_shapes] + h_sems,
        compiler_params=_cparams(("arbitrary", "arbitrary")),
    )(*params, *inputs, *(hook.arrs if hook else []))
    if hook is not None:
        hook.results = list(res[nout + ncar + nk:])
    return list(res[:nout]), list(res[nout:nout + ncar + nk])


def _stage_bwd(st, t, params, inputs, saved, douts, dx_dtypes, hook=None):
    g, tm = st.g, min(st.tm, t)
    n = t // tm
    npar, nin, ncar = len(params), len(inputs), len(st.carry_shapes)
    nk = len(st.kept_shapes)
    flat_d = [d for ds in douts for d in ds]
    nd = len(flat_d)
    dx_idx = [i for i, dt in enumerate(dx_dtypes) if dt is not None]
    h_in, h_out, h_shape, h_sems = _hook_specs(hook)
    nh = len(h_in)

    def body(*refs):
        p_refs = refs[:npar]
        x_refs = refs[npar:npar + nin]
        s_refs = refs[npar + nin:npar + nin + ncar]
        k_refs = refs[npar + nin + ncar:npar + nin + ncar + nk]
        o = npar + nin + ncar + nk
        d_refs = refs[o:o + nd]
        hi_refs = refs[o + nd:o + nd + nh]
        o += nd + nh
        dp_refs = refs[o:o + npar]
        dx_refs = refs[o + npar:o + npar + len(dx_idx)]
        ho_refs = refs[o + npar + len(dx_idx):o + npar + len(dx_idx) + nh]
        dc_scr = refs[o + npar + len(dx_idx) + nh:o + npar + len(dx_idx) + nh + ncar]
        gi, ni = pl.program_id(0), pl.program_id(1)
        if hook is not None:
            step = gi * n + ni
            hook.run(step, g * n, hi_refs, ho_refs, *refs[-2:])

        @pl.when(ni == 0)
        def _():
            for c in dc_scr:
                c[...] = jnp.zeros(c.shape, F32)

        ps = [r[...].astype(F32) for r in p_refs]
        xs = _load_pieces(x_refs, st.in_pieces)
        cs = [s[...] for s in s_refs]
        dys = []
        k = 0
        for ds, pieces in zip(douts, st.out_pieces):
            acc = _load_pieces([d_refs[k]], [pieces])
            for j in range(1, len(ds)):
                more = _load_pieces([d_refs[k + j]], [pieces])
                acc = [a + b for a, b in zip(acc, more)]
            dys += acc
            k += len(ds)
        if nk:
            kept = [r[...].astype(F32) for r in k_refs]
            _, vjp = jax.vjp(lambda p, x, c: st.f_kept(p, x, c, kept), ps, xs, cs)
        else:
            _, vjp = jax.vjp(st.f, ps, xs, cs)
        dps, dxs, dcs = vjp((dys, [c[...] for c in dc_scr]))
        k = 0
        per_in = []
        for pieces in st.in_pieces:
            per_in.append(dxs[k:k + len(pieces)])
            k += len(pieces)
        for ref, i in zip(dx_refs, dx_idx):
            _store_pieces([ref], [st.in_pieces[i]], per_in[i])
        for c, v in zip(dc_scr, dcs):
            c[...] = v
        for ref, dp, pg in zip(dp_refs, dps, st.par_per_g):
            first = (ni == 0) if pg else ((ni == 0) & (gi == 0))

            @pl.when(first)
            def _():
                ref[...] = jnp.zeros(ref.shape, F32)

            ref[...] += dp

    in_specs = [_par_spec(p, pg, g) for p, pg in zip(params, st.par_per_g)]
    in_specs += [_row_spec(tm, sum(pc), off, n, True) for pc, off in zip(st.in_pieces, st.in_offs)]
    in_specs += [_carry_spec(s, n, True) for s in st.carry_shapes]
    in_specs += [pl.BlockSpec(s, lambda gi, ni: (n - 1 - ni, 0)) for s in st.kept_shapes]
    for ds, pc in zip(douts, st.out_pieces):
        in_specs += [_row_spec(tm, sum(pc), 0, n, True) for _ in ds]
    out_specs = [_par_spec(p, pg, g) for p, pg in zip(params, st.par_per_g)]
    out_specs += [_row_spec(tm, sum(st.in_pieces[i]), 0, n, True) for i in dx_idx]
    out_shape = [jax.ShapeDtypeStruct(p.shape, F32) for p in params]
    out_shape += [jax.ShapeDtypeStruct((t, g * sum(st.in_pieces[i])), dx_dtypes[i]) for i in dx_idx]
    res = pl.pallas_call(
        body, name=st.name + "_bwd", grid=(g, n), in_specs=in_specs + h_in, out_specs=out_specs + h_out,
        out_shape=out_shape + h_shape,
        scratch_shapes=[pltpu.VMEM(s, F32) for s in st.carry_shapes] + h_sems,
        compiler_params=_cparams(("arbitrary", "arbitrary")),
    )(*params, *inputs, *saved, *flat_d, *(hook.arrs if hook else []))
    if hook is not None:
        hook.results = list(res[npar + len(dx_idx):])
    return list(res[:npar]), list(res[npar:npar + len(dx_idx)])


def _pick(n, cap):
    if n <= cap:
        return n
    best = LANES
    for k in range(1, n // LANES + 1):
        if (n // LANES) % k == 0 and k * LANES <= cap:
            best = k * LANES
    return best


def _mm(name, a, b, mode, out_dtype=F32, tm=1024, tn=512, b_outer=False, token=None):
    m = a.shape[1] if mode == "tn" else a.shape[0]
    k = a.shape[0] if mode == "tn" else a.shape[1]
    n = b.shape[0] if mode == "nt" else b.shape[1]
    tm, tn = _pick(m, tm), _pick(n, tn)
    if b_outer:
        grid = (n // tn, m // tm)
        ij = lambda p, q: (q, p)
    else:
        grid = (m // tm, n // tn)
        ij = lambda p, q: (p, q)
    extra = [] if token is None else [token]

    def body(*refs):
        a_ref, b_ref, o_ref = refs[0], refs[1], refs[-1]
        o_ref[...] = _raw_dot(a_ref[...], b_ref[...], mode).astype(o_ref.dtype)

    if mode == "tn":
        a_spec = pl.BlockSpec((k, tm), lambda p, q: (0, ij(p, q)[0]))
    else:
        a_spec = pl.BlockSpec((tm, k), lambda p, q: (ij(p, q)[0], 0))
    b_mode = dict(pipeline_mode=pl.Buffered(1)) if tn == n else {}
    if mode == "nt":
        b_spec = pl.BlockSpec((tn, k), lambda p, q: (ij(p, q)[1], 0), **b_mode)
    else:
        b_spec = pl.BlockSpec((k, tn), lambda p, q: (0, ij(p, q)[1]), **b_mode)
    return pl.pallas_call(
        body, name=name, grid=grid,
        in_specs=[a_spec, b_spec] + [pl.BlockSpec(e.shape, lambda p, q: (0, 0)) for e in extra],
        out_specs=pl.BlockSpec((tm, tn), lambda p, q: ij(p, q)),
        out_shape=jax.ShapeDtypeStruct((m, n), out_dtype),
        compiler_params=_cparams(("arbitrary", "arbitrary")),
    )(a, b, *extra)


def _mm_cols_nn(name, pieces, b, out_dtype, tm, token=None):
    m, n = pieces[0].shape[0], b.shape[1]
    tm = _pick(m, tm)
    offs = [sum(p.shape[1] for p in pieces[:i]) for i in range(len(pieces))]
    extra = [] if token is None else [token]
    na = len(pieces)

    def body(*refs):
        b_ref, o_ref = refs[na], refs[-1]
        acc = None
        for a_ref, off in zip(refs[:na], offs):
            t = _raw_dot(a_ref[...], b_ref[off:off + a_ref.shape[1], :], "nn")
            acc = t if acc is None else acc + t
        o_ref[...] = acc.astype(o_ref.dtype)

    return pl.pallas_call(
        body, name=name, grid=(m // tm,),
        in_specs=[pl.BlockSpec((tm, p.shape[1]), lambda i: (i, 0)) for p in pieces]
        + [pl.BlockSpec(b.shape, lambda i: (0, 0), pipeline_mode=pl.Buffered(1))]
        + [pl.BlockSpec(e.shape, lambda i: (0, 0)) for e in extra],
        out_specs=pl.BlockSpec((tm, n), lambda i: (i, 0)), out_shape=jax.ShapeDtypeStruct((m, n), out_dtype),
        compiler_params=_cparams(("arbitrary",)),
    )(*pieces, b, *extra)


def _mm_cols_tn(name, pieces, b, out_dtype, tm):
    k, n = b.shape
    counts = [p.shape[1] // tm for p in pieces]
    starts = [sum(counts[:i]) for i in range(len(pieces))]
    na = len(pieces)

    def body(*refs):
        b_ref, o_ref = refs[na], refs[-1]
        i = pl.program_id(0)
        for a_ref, s, c in zip(refs[:na], starts, counts):
            @pl.when((i >= s) & (i < s + c))
            def _():
                o_ref[...] = _raw_dot(a_ref[...], b_ref[...], "tn").astype(o_ref.dtype)

    def spec(s, c):
        return pl.BlockSpec((k, tm), lambda i: (0, jnp.clip(i - s, 0, c - 1)))

    return pl.pallas_call(
        body, name=name, grid=(sum(counts),),
        in_specs=[spec(s, c) for s, c in zip(starts, counts)]
        + [pl.BlockSpec(b.shape, lambda i: (0, 0), pipeline_mode=pl.Buffered(1))],
        out_specs=pl.BlockSpec((tm, n), lambda i: (i, 0)),
        out_shape=jax.ShapeDtypeStruct((sum(counts) * tm, n), out_dtype),
        compiler_params=_cparams(("arbitrary",)),
    )(*pieces, b)


def _loss_stage(t, g_post, h1, ff, tgt):
    tm = min(256, t)
    n = t // tm

    def body(g_ref, h_ref, f_ref, t_ref, loss_ref, dg_ref, dh_ref, df_ref):
        ni = pl.program_id(0)
        target = t_ref[...]

        def lossf(g, h1, ff):
            e = h1 + _rms(ff, g) - target
            return 0.5 * jnp.sum(jnp.mean(e * e, axis=-1))

        l, (dg, dh, df) = jax.value_and_grad(lossf, argnums=(0, 1, 2))(g_ref[...], h_ref[...], f_ref[...])

        @pl.when(ni == 0)
        def _():
            loss_ref[...] = jnp.zeros(loss_ref.shape, F32)
            dg_ref[...] = jnp.zeros(dg_ref.shape, F32)

        loss_ref[...] += jnp.full(loss_ref.shape, l, F32)
        dg_ref[...] += dg
        dh_ref[...] = dh
        df_ref[...] = df.astype(df_ref.dtype)

    row = pl.BlockSpec((tm, D), lambda ni: (ni, 0))
    one = pl.BlockSpec((1, D), lambda ni: (0, 0))
    return pl.pallas_call(
        body, name="loss_head", grid=(n,), in_specs=[one, row, row, row],
        out_specs=[pl.BlockSpec((1, LANES), lambda ni: (0, 0)), one, row, row],
        out_shape=[jax.ShapeDtypeStruct((1, LANES), F32), jax.ShapeDtypeStruct((1, D), F32),
                   jax.ShapeDtypeStruct((t, D), F32), jax.ShapeDtypeStruct((t, D), BF)],
        compiler_params=_cparams(("arbitrary",)),
    )(g_post, h1, ff, tgt)


_ANY = pl.BlockSpec(memory_space=pl.ANY)


def _all_gather(name, blks):
    na = len(blks)
    ns = 8

    def body(*refs):
        x_refs, out_refs = refs[:na], refs[na:2 * na]
        send_sems, recv_sems, local_sems = refs[2 * na:]
        x, y, cc = lax.axis_index("x"), lax.axis_index("y"), lax.axis_index("c")
        sibling, xn, yn = (x, y, 1 - cc), (1 - x, y, cc), (x, 1 - y, cc)

        def num(px, py, pc):
            return 4 * px + 2 * py + pc

        def copy(a, k, to, src, dst):
            return pltpu.make_async_remote_copy(src_ref=src, dst_ref=dst, send_sem=send_sems.at[ns * a + k],
                                                recv_sem=recv_sems.at[ns * a + k], device_id=to, device_id_type=MESH)

        def halves(a, blk):
            h = blks[a].shape[0] // 2
            return out_refs[a].at[blk, pl.ds(0, h)], out_refs[a].at[blk, pl.ds(h, h)]

        mine, sends = [], []
        for a in range(na):
            o = out_refs[a]
            m = pltpu.make_async_copy(x_refs[a], o.at[num(x, y, cc)], local_sems.at[a])
            m.start()
            mine.append(m)
            own = o.at[num(x, y, cc)]
            sends.append([copy(a, 0, sibling, x_refs[a], own), copy(a, 1, xn, x_refs[a], own),
                          copy(a, 2, yn, x_refs[a], own)])
            for cp in sends[a]:
                cp.start()
        for a in range(na):
            o = out_refs[a]
            bx, by, bd = num(1 - x, y, cc), num(x, 1 - y, cc), num(1 - x, 1 - y, cc)
            copy(a, 1, xn, o.at[bx], o.at[bx]).wait_recv()
            more = [copy(a, 3, yn, halves(a, bx)[0], halves(a, bx)[0]), copy(a, 5, sibling, o.at[bx], o.at[bx])]
            for cp in more:
                cp.start()
            sends[a] += more
        for a in range(na):
            o = out_refs[a]
            bx, by, bd = num(1 - x, y, cc), num(x, 1 - y, cc), num(1 - x, 1 - y, cc)
            copy(a, 2, yn, o.at[by], o.at[by]).wait_recv()
            more = [copy(a, 4, xn, halves(a, by)[1], halves(a, by)[1]), copy(a, 6, sibling, o.at[by], o.at[by])]
            for cp in more:
                cp.start()
            sends[a] += more
        for a in range(na):
            o = out_refs[a]
            bd = num(1 - x, 1 - y, cc)
            copy(a, 3, yn, halves(a, bd)[0], halves(a, bd)[0]).wait_recv()
            copy(a, 4, xn, halves(a, bd)[1], halves(a, bd)[1]).wait_recv()
            fw = copy(a, 7, sibling, o.at[bd], o.at[bd])
            fw.start()
            sends[a].append(fw)
        for a in range(na):
            o = out_refs[a]
            for k, blk in ((0, num(x, y, 1 - cc)), (5, num(1 - x, y, 1 - cc)), (6, num(x, 1 - y, 1 - cc)),
                           (7, num(1 - x, 1 - y, 1 - cc))):
                copy(a, k, sibling, o.at[blk], o.at[blk]).wait_recv()
            for cp in sends[a]:
                cp.wait_send()
        for m in mine:
            m.wait()

    res = pl.pallas_call(
        body, name=name, in_specs=[_ANY] * na, out_specs=[_ANY] * na,
        out_shape=[jax.ShapeDtypeStruct((N_DEV,) + b.shape, b.dtype) for b in blks],
        scratch_shapes=[pltpu.SemaphoreType.DMA((ns * na,)), pltpu.SemaphoreType.DMA((ns * na,)),
                        pltpu.SemaphoreType.DMA((na,))],
    )(*blks)
    return list(res)


def _reduce_pair(g8s):
    na = len(g8s)

    def body(*refs):
        g_refs, recv_refs = refs[:na], refs[na:2 * na]
        ssem, rsem = refs[2 * na:]
        x, y, cc = lax.axis_index("x"), lax.axis_index("y"), lax.axis_index("c")
        chips = [(x, y), (1 - x, y), (x, 1 - y), (1 - x, 1 - y)]
        sib = (x, y, 1 - cc)
        for a in range(na):
            for k, (cx, cy) in enumerate(chips):
                pltpu.make_async_remote_copy(
                    src_ref=g_refs[a].at[4 * cx + 2 * cy + 1 - cc], dst_ref=recv_refs[a].at[k],
                    send_sem=ssem.at[a], recv_sem=rsem.at[a], device_id=sib, device_id_type=MESH).start()
        for a in range(na):
            pltpu.make_async_remote_copy(src_ref=recv_refs[a], dst_ref=recv_refs[a], send_sem=ssem.at[a],
                                         recv_sem=rsem.at[a], device_id=sib, device_id_type=MESH).wait()

    res = pl.pallas_call(
        body, name="reduce_pair", in_specs=[_ANY] * na, out_specs=[_ANY] * na,
        out_shape=[jax.ShapeDtypeStruct((4,) + g.shape[1:], g.dtype) for g in g8s],
        scratch_shapes=[pltpu.SemaphoreType.DMA((na,)), pltpu.SemaphoreType.DMA((na,))],
    )(*g8s)
    return list(res)


_HBM = pl.BlockSpec(memory_space=pltpu.HBM)
_SEM = pl.BlockSpec(memory_space=pltpu.SEMAPHORE)
_EFFECT = pltpu.SideEffectType.DATAFLOW_SIDE_EFFECTING


def _chip_swap_copies(s_refs, land_refs, ssem, rsem):
    x, y, c = lax.axis_index("x"), lax.axis_index("y"), lax.axis_index("c")
    targets = [(1 - x, y, c), (x, 1 - y, c), (1 - x, 1 - y, c)]
    return [pltpu.make_async_remote_copy(src_ref=s.at[k], dst_ref=d.at[k], send_sem=ssem.at[3 * a + k],
                                         recv_sem=rsem.at[3 * a + k], device_id=targets[k], device_id_type=MESH)
            for a, (s, d) in enumerate(zip(s_refs, land_refs)) for k in range(3)]


def _chip_swap_start(sends):
    na = len(sends)

    def body(*refs):
        cps = _chip_swap_copies(refs[:na], refs[na:2 * na], refs[2 * na], refs[2 * na + 1])
        for cp in cps:
            cp.start()
        token = refs[-1]
        token[...] = jnp.zeros(token.shape, token.dtype)

    bufs = [pltpu.HBM(s.shape, s.dtype) for s in sends]
    res = pl.pallas_call(
        body, name="chip_swap_start",
        out_shape=[pltpu.SemaphoreType.DMA((3 * na,)), pltpu.SemaphoreType.DMA((3 * na,))] + bufs + bufs
        + [jax.ShapeDtypeStruct((8, LANES), F32)],
        in_specs=[_HBM] * (2 * na), out_specs=[_SEM, _SEM] + [_HBM] * (2 * na) + [pl.BlockSpec(memory_space=pltpu.VMEM)],
        input_output_aliases={i: 2 + i for i in range(2 * na)},
        compiler_params=pltpu.CompilerParams(has_side_effects=_EFFECT),
    )(*[pltpu.with_memory_space_constraint(s, pltpu.HBM) for s in sends],
      *[pltpu.with_memory_space_constraint(lax.empty(s.shape, s.dtype), pltpu.HBM) for s in sends])
    return res[0], res[1], list(res[2:2 + na]), list(res[2 + na:2 + 2 * na]), res[-1]


def _chip_swap_wait(ssem, rsem, srcs, lands, after):
    na = len(srcs)

    def body(*refs):
        cps = _chip_swap_copies(refs[:na], refs[na:2 * na], refs[2 * na], refs[2 * na + 1])
        for cp in cps:
            cp.wait_send()
            cp.wait_recv()

    bufs = [pltpu.HBM(s.shape, s.dtype) for s in srcs]
    res = pl.pallas_call(
        body, name="chip_swap_wait", out_shape=bufs + bufs,
        in_specs=[_HBM] * (2 * na) + [_SEM, _SEM, _ANY], out_specs=[_HBM] * (2 * na),
        input_output_aliases={i: i for i in range(2 * na)},
        compiler_params=pltpu.CompilerParams(has_side_effects=_EFFECT),
    )(*srcs, *lands, ssem, rsem, after)
    return list(res[na:])


def _pick_rows(r, c, budget=TILE_BYTES):
    if r * c * 4 <= budget or r % 16:
        return r
    best = 16
    for tr in range(16, r, 16):
        if r % tr == 0 and tr * c * 4 <= budget:
            best = tr
    return best


def _pair_sum(name, idx4, g8, recv4):
    _, r, c = g8.shape
    tr = _pick_rows(r, c, 2 * TILE_BYTES)

    def body(idx_ref, a_ref, b_ref, o0_ref, o3_ref):
        k = pl.program_id(1)
        s = a_ref[...].astype(F32) + b_ref[...].astype(F32)

        @pl.when(k == 0)
        def _():
            o0_ref[...] = s

        @pl.when(k > 0)
        def _():
            o3_ref[...] = s.astype(BF)

    spec = pltpu.PrefetchScalarGridSpec(
        num_scalar_prefetch=1, grid=(r // tr, 4),
        in_specs=[pl.BlockSpec((None, tr, c), lambda i, k, idx: (idx[k], i, 0)),
                  pl.BlockSpec((None, tr, c), lambda i, k, idx: (k, i, 0))],
        out_specs=[pl.BlockSpec((tr, c), lambda i, k, idx: (i, 0)),
                   pl.BlockSpec((None, tr, c), lambda i, k, idx: (jnp.maximum(k - 1, 0), i, 0))])
    return pl.pallas_call(
        body, name=name, grid_spec=spec,
        out_shape=[jax.ShapeDtypeStruct((r, c), F32), jax.ShapeDtypeStruct((3, r, c), BF)],
        compiler_params=_cparams(("arbitrary", "arbitrary")),
    )(idx4, g8, recv4)


def _adamw(w, g, m, v):
    m = ADAM_B1 * m + (1.0 - ADAM_B1) * g
    v = ADAM_B2 * v + (1.0 - ADAM_B2) * jnp.square(g)
    m_hat = m / (1.0 - ADAM_B1 ** ADAM_STEP)
    v_hat = v / (1.0 - ADAM_B2 ** ADAM_STEP)
    delta = -ADAM_LR * (m_hat / (jnp.sqrt(v_hat) + ADAM_EPS) + ADAM_WD * w)
    return delta, m, v


def _sum_partials(name, idx1, own, recv):
    _, r, c = own.shape
    tr = _pick_rows(r, c, 2 * TILE_BYTES)
    nj = recv.shape[0]

    def body(idx_ref, p_ref, r_ref, g_out):
        g = p_ref[...].astype(F32)
        for k in range(nj):
            g = g + r_ref[k].astype(F32)
        g_out[...] = g

    row = pl.BlockSpec((tr, c), lambda i, idx: (i, 0))
    spec = pltpu.PrefetchScalarGridSpec(
        num_scalar_prefetch=1, grid=(r // tr,),
        in_specs=[pl.BlockSpec((None, tr, c), lambda i, idx: (idx[0], i, 0)),
                  pl.BlockSpec((nj, tr, c), lambda i, idx: (0, i, 0))],
        out_specs=row)
    return pl.pallas_call(body, name=name, grid_spec=spec, out_shape=jax.ShapeDtypeStruct((r, c), F32),
                          compiler_params=_cparams(("arbitrary",)))(idx1, own, recv)


def _adam_sharded(name, idx1, own, recv, w, m, v):
    r, c = w.shape
    tr = _pick_rows(r, c)
    nj = 0 if recv is None else recv.shape[0]
    if recv is None:
        recv = jnp.zeros((1, 8, LANES), BF)

    def body(idx_ref, p_ref, r_ref, w_ref, m_ref, v_ref, g_out, d_out, m_out, v_out):
        g = p_ref[...].astype(F32)
        for k in range(nj):
            g = g + r_ref[k].astype(F32)
        d, mn, vn = _adamw(w_ref[...], g, m_ref[...], v_ref[...])
        g_out[...] = g
        d_out[...] = d
        m_out[...] = mn
        v_out[...] = vn

    row = pl.BlockSpec((tr, c), lambda i, idx: (i, 0))
    if nj:
        recv_spec = pl.BlockSpec((nj, tr, c), lambda i, idx: (0, i, 0))
    else:
        recv_spec = pl.BlockSpec(recv.shape, lambda i, idx: (0, 0, 0))
    spec = pltpu.PrefetchScalarGridSpec(
        num_scalar_prefetch=1, grid=(r // tr,),
        in_specs=[pl.BlockSpec((None, tr, c), lambda i, idx: (idx[0], i, 0)), recv_spec, row, row, row],
        out_specs=[row] * 4)
    return pl.pallas_call(
        body, name=name, grid_spec=spec, out_shape=[jax.ShapeDtypeStruct((r, c), F32)] * 4,
        compiler_params=_cparams(("arbitrary",)),
    )(idx1, own, recv, w, m, v)


def _repl_rows():
    rows, r = {}, 0
    for name, cols in REPL:
        rows[name] = r
        r += REPL_ROWS.get(name, 1) * ((cols + D - 1) // D)
    return rows


LOSS_ROW = 24


def _pack_replicated(grads, loss_acc):
    rows = _repl_rows()
    names = [n for n, _ in REPL]

    def body(*refs):
        o_ref = refs[-1]
        o_ref[...] = jnp.zeros(o_ref.shape, F32)
        o_ref[LOSS_ROW:LOSS_ROW + 1, 0:LANES] = refs[-2][...]
        for name, ref in zip(names, refs[:-2]):
            r0 = rows[name]
            nr, nc = ref.shape
            if nc <= D:
                o_ref[r0:r0 + nr, 0:nc] = ref[...]
            else:
                for j in range((nc + D - 1) // D):
                    lo, hi = j * D, min(nc, (j + 1) * D)
                    o_ref[r0 + j:r0 + j + 1, 0:hi - lo] = ref[:, lo:hi]

    return pl.pallas_call(body, name="pack_replicated", out_shape=jax.ShapeDtypeStruct((REPL_TOTAL, D), F32),
                          compiler_params=_cparams())(*[grads[n] for n in names], loss_acc)


def _adam_replicated(g8, ws, ms, vs):
    rows = _repl_rows()
    names = [n for n, _ in REPL]
    np_ = len(names)

    def body(*refs):
        g_ref = refs[0]
        w_refs, m_refs, v_refs = refs[1:1 + np_], refs[1 + np_:1 + 2 * np_], refs[1 + 2 * np_:1 + 3 * np_]
        outs = refs[1 + 3 * np_:1 + 7 * np_]
        scr = refs[-1]
        g = g_ref[0]
        for k in range(1, N_DEV):
            g = g + g_ref[k]
        scr[...] = g
        refs[1 + 7 * np_][...] = scr[LOSS_ROW:LOSS_ROW + 1, 0:LANES]
        for i, name in enumerate(names):
            r0 = rows[name]
            nr, nc = w_refs[i].shape
            if nc <= D:
                gi = scr[r0:r0 + nr, 0:nc]
            else:
                parts = []
                for j in range((nc + D - 1) // D):
                    lo, hi = j * D, min(nc, (j + 1) * D)
                    parts.append(scr[r0 + j:r0 + j + 1, 0:hi - lo])
                gi = jnp.concatenate(parts, axis=1)
            d, mn, vn = _adamw(w_refs[i][...], gi, m_refs[i][...], v_refs[i][...])
            outs[i][...] = gi
            outs[np_ + i][...] = d
            outs[2 * np_ + i][...] = mn
            outs[3 * np_ + i][...] = vn

    shp = [jax.ShapeDtypeStruct(w.shape, F32) for w in ws]
    res = pl.pallas_call(body, name="adam_replicated", out_shape=shp * 4 + [jax.ShapeDtypeStruct((1, LANES), F32)],
                         scratch_shapes=[pltpu.VMEM((REPL_TOTAL, D), F32)], compiler_params=_cparams(),
                         )(g8, *ws, *ms, *vs)
    return [dict(zip(names, res[k * np_:(k + 1) * np_])) for k in range(4)], res[-1]


_WEIGHTS = ("attn_pre_norm", "w_in", "hgrn_lb", "hgrn_gnorm", "w_branch_a", "rwkv_mu", "rwkv_w0", "rwkv_w2",
            "rwkv_a0", "rwkv_a2", "rwkv_g2", "rwkv_k_k", "rwkv_k_a", "rwkv_r_k", "rwkv_ln_w", "rwkv_ln_b",
            "w_branch_b", "w_out", "attn_post_norm", "ffn_pre_norm", "w_up", "conv_w", "conv_b", "w_down",
            "ffn_post_norm")
_BIG = ("w_in", "w_up", "w_down", "w_branch_a", "w_branch_b", "w_out")


def _stages():
    one = [D]
    hw = HG_K * HG_PER_STEP
    rw = LANES * RW_PAIRS_PER_STEP
    return dict(
        pre1=_Stage("pre1", _f_pre1, 1, 256, [False], [one], [0], [], [one], [BF]),
        pre1_res=_Stage("pre1", _f_pre1_residual, 1, 256, [False], [one], [0], [], [one, one], [BF, F32]),
        mixers=_Stage("mixers", _f_mixers, 1, 2 * RW_CHUNK, [False] * 13, [[D] * 7 + [LANES, LANES]], [0],
                      [(hw, HG_K), (1, RW_COLS), (rw, LANES)], [one, one], [BF, BF],
                      kept_shapes=[(2 * RW_KEPT * RW_PAIRS_PER_STEP * 2 * RW_CHUNK, LANES)], f_kept=_f_mixers_kept),
        merge=_Stage("merge", _f_merge, 4, 512, [], [[256]] * 4, [29, 33, 0, 0], [], [[256]], [BF]),
        post1=_Stage("post1", _f_post1, 1, 256, [False, False], [one, one], [0, 0], [], [one, one], [F32, BF]),
        conv=_Stage("conv", _f_conv, 1, 128, [False, False], [[DFF, DFF]], [0], [(1, 2 * DFF), (1, 2 * DFF)],
                    [[DFF]], [BF]),
    )


def _cols_to_blocks(w, per):
    return w.reshape(w.shape[0], N_DEV, per).transpose(1, 0, 2)


def _blocks_to_cols(g):
    return g.transpose(1, 0, 2).reshape(g.shape[1], N_DEV * g.shape[2])


def kernel(x, attn_pre_norm, w_in, hgrn_lb, hgrn_gnorm, w_branch_a, rwkv_mu, rwkv_w0, rwkv_w2, rwkv_a0, rwkv_a2, rwkv_g2, rwkv_k_k, rwkv_k_a, rwkv_r_k, rwkv_ln_w, rwkv_ln_b, w_branch_b, w_out, attn_post_norm, ffn_pre_norm, w_up, conv_w, conv_b, w_down, ffn_post_norm, loss_target, m_attn_pre_norm, m_w_in, m_hgrn_lb, m_hgrn_gnorm, m_w_branch_a, m_rwkv_mu, m_rwkv_w0, m_rwkv_w2, m_rwkv_a0, m_rwkv_a2, m_rwkv_g2, m_rwkv_k_k, m_rwkv_k_a, m_rwkv_r_k, m_rwkv_ln_w, m_rwkv_ln_b, m_w_branch_b, m_w_out, m_attn_post_norm, m_ffn_pre_norm, m_w_up, m_conv_w, m_conv_b, m_w_down, m_ffn_post_norm, v_attn_pre_norm, v_w_in, v_hgrn_lb, v_hgrn_gnorm, v_w_branch_a, v_rwkv_mu, v_rwkv_w0, v_rwkv_w2, v_rwkv_a0, v_rwkv_a2, v_rwkv_g2, v_rwkv_k_k, v_rwkv_k_a, v_rwkv_r_k, v_rwkv_ln_w, v_rwkv_ln_b, v_w_branch_b, v_w_out, v_attn_post_norm, v_ffn_pre_norm, v_w_up, v_conv_w, v_conv_b, v_w_down, v_ffn_post_norm):
    w = dict(attn_pre_norm=attn_pre_norm, w_in=w_in, hgrn_lb=hgrn_lb, hgrn_gnorm=hgrn_gnorm, w_branch_a=w_branch_a, rwkv_mu=rwkv_mu, rwkv_w0=rwkv_w0, rwkv_w2=rwkv_w2, rwkv_a0=rwkv_a0, rwkv_a2=rwkv_a2, rwkv_g2=rwkv_g2, rwkv_k_k=rwkv_k_k, rwkv_k_a=rwkv_k_a, rwkv_r_k=rwkv_r_k, rwkv_ln_w=rwkv_ln_w, rwkv_ln_b=rwkv_ln_b, w_branch_b=w_branch_b, w_out=w_out, attn_post_norm=attn_post_norm, ffn_pre_norm=ffn_pre_norm, w_up=w_up, conv_w=conv_w, conv_b=conv_b, w_down=w_down, ffn_post_norm=ffn_post_norm)
    mo = dict(attn_pre_norm=m_attn_pre_norm, w_in=m_w_in, hgrn_lb=m_hgrn_lb, hgrn_gnorm=m_hgrn_gnorm, w_branch_a=m_w_branch_a, rwkv_mu=m_rwkv_mu, rwkv_w0=m_rwkv_w0, rwkv_w2=m_rwkv_w2, rwkv_a0=m_rwkv_a0, rwkv_a2=m_rwkv_a2, rwkv_g2=m_rwkv_g2, rwkv_k_k=m_rwkv_k_k, rwkv_k_a=m_rwkv_k_a, rwkv_r_k=m_rwkv_r_k, rwkv_ln_w=m_rwkv_ln_w, rwkv_ln_b=m_rwkv_ln_b, w_branch_b=m_w_branch_b, w_out=m_w_out, attn_post_norm=m_attn_post_norm, ffn_pre_norm=m_ffn_pre_norm, w_up=m_w_up, conv_w=m_conv_w, conv_b=m_conv_b, w_down=m_w_down, ffn_post_norm=m_ffn_post_norm)
    vo = dict(attn_pre_norm=v_attn_pre_norm, w_in=v_w_in, hgrn_lb=v_hgrn_lb, hgrn_gnorm=v_hgrn_gnorm, w_branch_a=v_w_branch_a, rwkv_mu=v_rwkv_mu, rwkv_w0=v_rwkv_w0, rwkv_w2=v_rwkv_w2, rwkv_a0=v_rwkv_a0, rwkv_a2=v_rwkv_a2, rwkv_g2=v_rwkv_g2, rwkv_k_k=v_rwkv_k_k, rwkv_k_a=v_rwkv_k_a, rwkv_r_k=v_rwkv_r_k, rwkv_ln_w=v_rwkv_ln_w, rwkv_ln_b=v_rwkv_ln_b, w_branch_b=v_w_branch_b, w_out=v_w_out, attn_post_norm=v_attn_post_norm, ffn_pre_norm=v_ffn_pre_norm, w_up=v_w_up, conv_w=v_conv_w, conv_b=v_conv_b, w_down=v_w_down, ffn_post_norm=v_ffn_post_norm)

    t = x.shape[1]
    x2 = x.reshape(t, D)
    tgt = loss_target.reshape(t, D)
    st = _stages()

    me = 4 * lax.axis_index("x") + 2 * lax.axis_index("y") + lax.axis_index("c")
    small = jnp.concatenate([rwkv_w2[0], rwkv_a2[0], rwkv_g2[0]], axis=0).astype(BF)
    g_in, g_small = _all_gather("gather_weights", [w_in[0].T.astype(BF), small])
    fw_in_t = g_in.reshape(IN_COLS, D)
    z64 = jnp.zeros((64, D), BF)
    w2p = jnp.concatenate([_blocks_to_cols(g_small[:, 0:64]), z64], axis=0)
    a2p = jnp.concatenate([z64, _blocks_to_cols(g_small[:, 64:128])], axis=0)
    g2f = _blocks_to_cols(g_small[:, 128:256])
    conv_bits = jnp.pad(lax.bitcast_convert_type(conv_w[0], BF).reshape(3, 2 * 704), ((0, 29), (0, 0)))
    late = [w_up[0].T.astype(BF)] + [w[k][0].astype(BF) for k in _BIG[2:]] + [conv_bits]
    late_gather = _Exchange("gather2", late)
    r_k = rwkv_r_k.reshape(1, D)

    (xn,), _ = _stage_fwd(st["pre1"], t, [attn_pre_norm], [x2])
    z = _mm("in_proj", xn, fw_in_t, "nt", F32, tm=512, tn=4736, b_outer=True)
    mix_par = [hgrn_lb, hgrn_gnorm, rwkv_mu, rwkv_w0, w2p, rwkv_a0, a2p, g2f, rwkv_k_k, rwkv_k_a,
               rwkv_ln_w, rwkv_ln_b, r_k]
    mix_in = [z]
    (o_a, o_b), mix_saved = _stage_fwd(st["mixers"], t, mix_par, mix_in, hook=late_gather)
    gl = [lax.dynamic_update_slice(g, own[None], (me, 0, 0)) for g, own in zip(late_gather.results, late)]
    fw_up_t = gl[0].reshape(2 * DFF, D)
    fw_down = gl[1].reshape(DFF, D)
    fw_a, fw_b, fw_out = (g.reshape(D, D) for g in gl[2:5])
    conv_full = _blocks_to_cols(lax.bitcast_convert_type(gl[5][:, :3].reshape(N_DEV, 3, 704, 2), F32))
    y_a = _mm("branch_a", o_a, fw_a, "nn")
    y_b = _mm("branch_b", o_b, fw_b, "nn")
    (merged,), _ = _stage_fwd(st["merge"], t, [], [z, z, y_a, y_b])
    mix = _mm("out_proj", merged, fw_out, "nn")
    (h1, xn2), _ = _stage_fwd(st["post1"], t, [attn_post_norm, ffn_pre_norm], [x2, mix])
    hu = _mm("up_proj", xn2, fw_up_t, "nt", F32, tm=1024, tn=1408)
    conv_par = [conv_full, conv_b]
    (act,), conv_saved = _stage_fwd(st["conv"], t, conv_par, [hu])
    ff = _mm("down_proj", act, fw_down, "nn")

    loss_acc, d_ffn_post, dh1, dff = _loss_stage(t, ffn_post_norm, h1, ff, tgt)
    dact = _mm("d_act", dff, fw_down, "nt", BF, tm=1024, tn=1408)
    dw_down = _mm("dw_down", act, dff, "tn", BF, tm=1408, tn=512)
    (dcw, dcb), (dhu,) = _stage_bwd(st["conv"], t, conv_par, [hu], conv_saved, [[dact]], [BF])
    dxn2 = _mm("d_xn2", dhu, fw_up_t, "nn", F32, tm=1024, tn=1024)
    dw_up_t = _mm("dw_up", dhu, xn2, "tn", BF, tm=1408, tn=1024)
    (d_post, d_pre2), (dx_a, dmix) = _stage_bwd(st["post1"], t, [attn_post_norm, ffn_pre_norm], [x2, mix], [],
                                                 [[dh1], [dxn2]], [F32, BF])
    dmerged = _mm("d_merged", dmix, fw_out, "nt", BF)
    dw_out = _mm("dw_out", merged, dmix, "tn", BF)
    _, (dga, dgb, dy_a, dy_b) = _stage_bwd(st["merge"], t, [], [z, z, y_a, y_b], [], [[dmerged]], [BF, BF, BF, BF])
    do_a = _mm("d_oa", dy_a, fw_a, "nt", BF)
    dw_a = _mm("dw_a", o_a, dy_a, "tn", BF)
    do_b = _mm("d_ob", dy_b, fw_b, "nt", BF)
    dw_b = _mm("dw_b", o_b, dy_b, "tn", BF)
    early = [dw_up_t.reshape(N_DEV, 704, D), dw_down.reshape(N_DEV, 352, D), dw_a.reshape(N_DEV, 128, D),
             dw_b.reshape(N_DEV, 128, D), dw_out.reshape(N_DEV, 128, D), _cols_to_blocks(dcw.astype(BF), 704)]
    early_scatter = _Exchange("scatter", early)
    mix_dp, dz_hr = _stage_bwd(st["mixers"], t, mix_par, mix_in, mix_saved, [[do_a], [do_b]], [BF],
                               hook=early_scatter)
    d_lb, d_gn, d_mu, d_w0, d_w2p, d_a0, d_a2p, d_g2, d_kk, d_ka, d_lnw, d_lnb, d_rk = mix_dp
    dz = dz_hr + [dga, dgb]
    dw_in_t = _mm_cols_tn("dw_in", dz, xn, BF, 256)

    ax, ay, ac = lax.axis_index("x"), lax.axis_index("y"), lax.axis_index("c")
    idx4 = jnp.stack([4 * cx + 2 * cy + ac for cx, cy in ((ax, ay), (1 - ax, ay), (ax, 1 - ay), (1 - ax, 1 - ay))])
    idx4 = idx4.astype(jnp.int32)
    idx_me, idx_0 = idx4[0:1], jnp.zeros((1,), jnp.int32)
    d_small = jnp.concatenate([d_w2p[:64], d_a2p[64:], d_g2], axis=0).astype(BF)
    g8s = [dw_in_t.reshape(N_DEV, 1184, D), _cols_to_blocks(d_small, LANES)]
    recv4s = _reduce_pair(g8s)
    sums = [_pair_sum("pair_sum_" + n, idx4, g, r) for n, g, r in zip(("w_in", "small"), g8s, recv4s)]
    swap_ssem, swap_rsem, swap_srcs, swap_lands, token = _chip_swap_start([s[1] for s in sums])
    dxn = _mm_cols_nn("d_xn", dz, fw_in_t, BF, 512, token=token)
    (d_pre1,), (dx,) = _stage_bwd(st["pre1_res"], t, [attn_pre_norm], [x2], [], [[dxn], [dx_a]], [F32])
    grad_x = dx.reshape(x.shape)

    rg = dict(attn_pre_norm=d_pre1, hgrn_lb=d_lb, hgrn_gnorm=d_gn, rwkv_mu=d_mu, rwkv_w0=d_w0, rwkv_a0=d_a0,
              rwkv_k_k=d_kk, rwkv_k_a=d_ka, rwkv_r_k=d_rk, rwkv_ln_w=d_lnw, rwkv_ln_b=d_lnb, attn_post_norm=d_post,
              ffn_pre_norm=d_pre2, conv_b=dcb, ffn_post_norm=d_ffn_post)
    (g8,) = _all_gather("gather_small_grads", [_pack_replicated(rg, loss_acc)])
    rnames = [n for n, _ in REPL]
    flat = lambda src: [src[n].reshape(1, D) if n == "rwkv_r_k" else src[n] for n in rnames]
    rp_out, loss_row = _adam_replicated(g8, flat(w), flat(mo), flat(vo))
    loss = loss_row[0, 0]
    recv3s = _chip_swap_wait(swap_ssem, swap_rsem, swap_srcs, swap_lands, rp_out[0]["attn_pre_norm"])
    for kind in range(4):
        rp_out[kind]["rwkv_r_k"] = rp_out[kind]["rwkv_r_k"].reshape(rwkv_r_k.shape)

    def small_of(src):
        return jnp.concatenate([src["rwkv_w2"][0], src["rwkv_a2"][0], src["rwkv_g2"][0]], axis=0)

    sh_out = [dict() for _ in range(4)]
    g_in = _sum_partials("sum_w_in", idx_0, sums[0][0][None], recv3s[0]).T
    res = _adam_sharded("adam_w_in", idx_0, g_in[None], None, *[src["w_in"][0] for src in (w, mo, vo)])
    res_s = _adam_sharded("adam_small", idx_0, sums[1][0][None], recv3s[1], *[small_of(src) for src in (w, mo, vo)])
    for kind in range(4):
        sh_out[kind]["w_in"] = res[kind][None]
        sh_out[kind]["rwkv_w2"] = res_s[kind][0:64][None]
        sh_out[kind]["rwkv_a2"] = res_s[kind][64:128][None]
        sh_out[kind]["rwkv_g2"] = res_s[kind][128:256][None]
    for n, own, recv in zip(_BIG[1:] + ("conv_w",), early, early_scatter.results):
        if n == "w_up":
            g_up = _sum_partials("sum_w_up", idx_me, own, recv).T
            res = _adam_sharded("adam_" + n, idx_0, g_up[None], None, *[src[n][0] for src in (w, mo, vo)])
        else:
            res = _adam_sharded("adam_" + n, idx_me, own, recv, *[src[n][0] for src in (w, mo, vo)])
        for kind in range(4):
            sh_out[kind][n] = res[kind][None]

    outs = [loss, grad_x]
    for kind in range(4):
        for name in _WEIGHTS:
            outs.append(sh_out[kind][name] if name in sh_out[kind] else rp_out[kind][name])
    return tuple(outs)
```

```python
import functools

import jax
import jax.numpy as jnp
from jax import lax
from jax.experimental import pallas as pl
from jax.experimental.pallas import tpu as pltpu

F32 = jnp.float32
BF = jnp.bfloat16
MESH = pl.DeviceIdType.MESH

D = 1024
HG_HEADS = 8
HG_K = 128
HG_CHUNK = 32
HG_SCALE = HG_K ** -0.5
HG_PER_STEP = 8
RW_HEADS = 16
RW_N = 64
RW_CHUNK = 64
RW_PAIRS_PER_STEP = 8
DFF = 2816
IN_COLS = 9472
RW_COLS = 3328
EPS = 1e-6
GN_EPS = 1e-5 * RW_N
ADAM_LR = 0.001
ADAM_B1 = 0.9
ADAM_B2 = 0.999
ADAM_EPS = 1e-08
ADAM_WD = 0.01
ADAM_STEP = 10
N_DEV = 8
LANES = 128
VMEM_LIMIT = 56 * 1024 * 1024
TILE_BYTES = 1280 * 1024

REPL = (("attn_pre_norm", 1024), ("hgrn_lb", 1024), ("hgrn_gnorm", 1024), ("rwkv_mu", 3328), ("rwkv_w0", 1024),
        ("rwkv_a0", 1024), ("rwkv_k_k", 1024), ("rwkv_k_a", 1024), ("rwkv_r_k", 1024), ("rwkv_ln_w", 1024),
        ("rwkv_ln_b", 1024), ("attn_post_norm", 1024), ("ffn_pre_norm", 1024), ("conv_b", 5632), ("ffn_post_norm", 1024))
REPL_ROWS = {"hgrn_lb": 2}
REPL_TOTAL = 32


def _cparams(sem=None, **kw):
    return pltpu.CompilerParams(dimension_semantics=sem, vmem_limit_bytes=VMEM_LIMIT, **kw)


_DN = {"nn": ((1,), (0,)), "nt": ((1,), (1,)), "tn": ((0,), (0,))}


def _raw_dot(a, b, mode):
    return lax.dot_general(a.astype(BF), b.astype(BF), (_DN[mode], ((), ())), preferred_element_type=F32)


@functools.partial(jax.custom_vjp, nondiff_argnums=(2,))
def _dot(a, b, mode):
    return _raw_dot(a, b, mode)


def _dot_fwd(a, b, mode):
    return _raw_dot(a, b, mode), (a, b)


def _dot_bwd(mode, res, g):
    a, b = res
    if mode == "nn":
        return _dot(g, b, "nt"), _dot(a, g, "tn")
    if mode == "nt":
        return _dot(g, b, "nn"), _dot(g, a, "tn")
    return _dot(b, g, "nt"), _dot(a, g, "nn")


_dot.defvjp(_dot_fwd, _dot_bwd)


def _bf_pieces(x, n):
    out, r = [], x
    for i in range(n):
        p = r.astype(BF)
        out.append(p)
        if i + 1 < n:
            r = r - p.astype(F32)
    return out


def _raw_split_dot(x, e, mode, n, x_left):
    eb = e.astype(BF)
    acc = None
    for p in _bf_pieces(x, n):
        ops = (p, eb) if x_left else (eb, p)
        t = lax.dot_general(*ops, (_DN[mode], ((), ())), preferred_element_type=F32)
        acc = t if acc is None else acc + t
    return acc


def _raw_headsum(x):
    t = x.shape[0]
    i = lax.broadcasted_iota(jnp.int32, (LANES, LANES), 0)
    j = lax.broadcasted_iota(jnp.int32, (LANES, LANES), 1)
    same = jnp.where((i >= RW_N) == (j >= RW_N), 1.0, 0.0).astype(F32)
    groups = x.shape[1] // LANES
    rows = jnp.concatenate([x[:, q * LANES:(q + 1) * LANES] for q in range(groups)], axis=0)
    s = _raw_split_dot(rows, same, "nn", 2, True)
    return jnp.concatenate([s[q * t:(q + 1) * t] for q in range(groups)], axis=1)


@jax.custom_vjp
def _headsum(x):
    return _raw_headsum(x)


def _headsum_fwd(x):
    return _raw_headsum(x), None


def _headsum_bwd(_, g):
    return (_raw_headsum(g),)


_headsum.defvjp(_headsum_fwd, _headsum_bwd)


@functools.partial(jax.custom_vjp, nondiff_argnums=(2,))
def _tdot(tri, x, n):
    return _raw_split_dot(x, tri, "nn", n, False)


def _tdot_fwd(tri, x, n):
    return _raw_split_dot(x, tri, "nn", n, False), tri


def _tdot_bwd(n, tri, g):
    return jnp.zeros_like(tri), _raw_split_dot(g, tri, "tn", n, False)


_tdot.defvjp(_tdot_fwd, _tdot_bwd)


def _row(x, i):
    r = lax.broadcasted_iota(jnp.int32, x.shape, 0)
    return jnp.sum(jnp.where(r == i, x, 0.0), axis=0, keepdims=True)


def _shift_down(x, prev):
    t = x.shape[0]

    @jax.custom_vjp
    def sh(x, prev):
        r = lax.broadcasted_iota(jnp.int32, x.shape, 0)
        return jnp.where(r == 0, prev, pltpu.roll(x, 1, 0))

    def fwd(x, prev):
        return sh(x, prev), None

    def bwd(_, g):
        r = lax.broadcasted_iota(jnp.int32, g.shape, 0)
        dx = jnp.where(r == t - 1, 0.0, pltpu.roll(g, t - 1, 0))
        return dx, jnp.sum(jnp.where(r == 0, g, 0.0), axis=0, keepdims=True)

    sh.defvjp(fwd, bwd)
    return sh(x, prev)


def _sigmoid(x):
    return jax.nn.sigmoid(x)


def _silu(x):
    return x * jax.nn.sigmoid(x)


def _softplus(x):
    return jnp.maximum(x, 0.0) + jnp.log(1.0 + jnp.exp(-jnp.abs(x)))


def _rms(x, g):
    return (x * lax.rsqrt(jnp.mean(x * x, axis=-1, keepdims=True) + EPS)) * g


def _tril(c):
    r = lax.broadcasted_iota(jnp.int32, (c, c), 0)
    cc = lax.broadcasted_iota(jnp.int32, (c, c), 1)
    return cc <= r


def _f_pre1(ps, xs, cs):
    return [_rms(xs[0], ps[0])], []


def _f_pre1_residual(ps, xs, cs):
    return [_rms(xs[0], ps[0]), xs[0]], []


def _f_hgrn(ps, xs, cs):
    lbraw, gn = ps
    hq, hf, hi, hg = xs
    hd = range(HG_PER_STEP)
    st = [cs[0][p * HG_K:(p + 1) * HG_K] for p in hd]
    l0, l1 = _row(lbraw, 0), _row(lbraw, 1)
    m = jnp.maximum(l0, l1)
    e0, e1 = jnp.exp(l0 - m), jnp.exp(l1 - m)
    lb = e0 / (e0 + e1)
    q = _silu(hq) * HG_SCALE
    f = lb + (1.0 - lb) * _sigmoid(hf)
    kh = 1.0 - f
    gl = jnp.log(f)
    c = HG_CHUNK
    low = _tril(c)
    tri = jnp.where(low, 1.0, 0.0).astype(F32)
    outs = []
    for i in range(hq.shape[0] // c):
        rows = slice(i * c, (i + 1) * c)
        b = _tdot(tri, gl[rows], 3)
        bref = _row(b, c // 2 - 1)
        blast = _row(b, c - 1)
        qi = q[rows] * jnp.exp(b - bref)
        ki = kh[rows] * jnp.exp(bref - b)
        qd = q[rows] * jnp.exp(b)
        kd = kh[rows] * jnp.exp(blast - b)
        dec = jnp.exp(blast)
        sl = [slice(p * HG_K, (p + 1) * HG_K) for p in hd]
        sc = [jnp.where(low, _dot(qi[:, sl[p]], ki[:, sl[p]], "nt"), 0.0) for p in hd]
        o = [_dot(sc[p], hi[rows, sl[p]], "nn") + _dot(qd[:, sl[p]], st[p], "nt") for p in hd]
        u = [_dot(hi[rows, sl[p]], kd[:, sl[p]], "tn") for p in hd]
        st = [dec[:, sl[p]] * st[p] + u[p] for p in hd]
        outs.append(jnp.concatenate(o, axis=1) if len(o) > 1 else o[0])
    o = outs[0] if len(outs) == 1 else jnp.concatenate(outs, axis=0)
    on = []
    for p in hd:
        op = o[:, p * HG_K:(p + 1) * HG_K]
        on.append(op * lax.rsqrt(jnp.mean(op * op, axis=-1, keepdims=True) + EPS))
    o = jnp.concatenate(on, axis=1) if len(on) > 1 else on[0]
    o = o * gn
    return [o * _silu(hg)], [jnp.concatenate(st, axis=0) if len(st) > 1 else st[0]]


_RW_OFFS = (0, 1024, 2048, 3072, 3200, 3328)


def _f_rwpre(ps, xs, cs):
    mu, w0, w2p, a0, a2p, g2, k_k, k_a = ps
    (prev,) = cs
    t = xs[0].shape[0]
    zs = []
    for i, z in enumerate(xs):
        lo, hi = _RW_OFFS[i], _RW_OFFS[i + 1]
        zs.append(z + mu[:, lo:hi] * (_shift_down(z, prev[:, lo:hi]) - z))
    rr, kr, vr, wa, gz = zs
    w_log = -_softplus(-(w0 + _dot(jnp.tanh(wa), w2p, "nn"))) - 0.5
    lw = -jnp.exp(w_log)
    a = _sigmoid(a0 + _dot(wa, a2p, "nn"))
    g = _dot(_sigmoid(gz), g2, "nn")
    kkr = kr * k_k
    kk = kkr / jnp.maximum(jnp.sqrt(_headsum(kkr * kkr)), 1e-12)
    k2 = kr * (1.0 + (a - 1.0) * k_a)
    newprev = jnp.concatenate([_row(z, t - 1) for z in xs], axis=1)
    return [rr, lw, k2, vr, -kk, kk * a, g], [newprev]


def _raw_inverses(ls):
    n = ls[0].shape[0]
    r = lax.broadcasted_iota(jnp.int32, (n, n), 0)
    c = lax.broadcasted_iota(jnp.int32, (n, n), 1)
    eye = jnp.where(r == c, 1.0, 0.0).astype(F32)
    tinv = [eye + l for l in ls]
    pw = ls
    for _ in range(5):
        pw = [_raw_dot(p, p, "nn") for p in pw]
        tinv = [t + _raw_dot(t, p, "nn") for t, p in zip(tinv, pw)]
    return tinv


@jax.custom_vjp
def _unit_lower_inverses(ls):
    return _raw_inverses(ls)


def _inverses_fwd(ls):
    tinv = _raw_inverses(ls)
    return tinv, tinv


def _inverses_bwd(tinv, gs):
    return ([_raw_dot(_raw_dot(t, g, "tn"), t, "nt") for t, g in zip(tinv, gs)],)


_unit_lower_inverses.defvjp(_inverses_fwd, _inverses_bwd)


@jax.custom_vjp
def _known_inverses(ls, tinv):
    return tinv


def _known_fwd(ls, tinv):
    return tinv, tinv


def _known_bwd(tinv, gs):
    return [_raw_dot(_raw_dot(t, g, "tn"), t, "nt") for t, g in zip(tinv, gs)], [jnp.zeros_like(t) for t in tinv]


_known_inverses.defvjp(_known_fwd, _known_bwd)


@jax.custom_vjp
def _use_kept(computed, kept):
    return kept


def _use_kept_fwd(computed, kept):
    return kept, None


def _use_kept_bwd(_, g):
    return g, jax.tree.map(jnp.zeros_like, g)


_use_kept.defvjp(_use_kept_fwd, _use_kept_bwd)

RW_KEPT = 5


def _f_rwscan(ps, xs, cs, kept=None):
    state = cs[0]
    ys, keep = [], []
    n = 2 * RW_CHUNK
    per_chunk = RW_KEPT * RW_PAIRS_PER_STEP * n
    for i in range(xs[0].shape[0] // RW_CHUNK):
        known = None
        if kept is not None:
            known = [[kept[i * per_chunk + (q * RW_PAIRS_PER_STEP + p) * n:
                           i * per_chunk + (q * RW_PAIRS_PER_STEP + p + 1) * n] for p in range(RW_PAIRS_PER_STEP)]
                     for q in range(RW_KEPT)]
        y, state, mats = _rwkv_chunk([x[i * RW_CHUNK:(i + 1) * RW_CHUNK] for x in xs], state, known)
        ys.append(y)
        keep += [m for group in mats for m in group]
    return [ys[0] if len(ys) == 1 else jnp.concatenate(ys, axis=0)], [state], jnp.concatenate(keep, axis=0)


def _rwkv_chunk(xs, state, known=None):
    npair = RW_PAIRS_PER_STEP
    pr = range(npair)
    r, lw, k, v, av, bv = [[x[:, p * LANES:(p + 1) * LANES] for p in pr] for x in xs]
    sv = [state[p * LANES:(p + 1) * LANES] for p in pr]
    c = RW_CHUNK
    n = 2 * c
    tri = jnp.where(_tril(c), 1.0, 0.0).astype(F32)
    cl = [_tdot(tri, lw[p], 3) for p in pr]
    cl_last = [_row(cl[p], c - 1) for p in pr]
    lane = lax.broadcasted_iota(jnp.int32, (c, LANES), 1)
    h0 = lane < RW_N

    def stack(x):
        return jnp.concatenate([jnp.where(h0, x, 0.0), jnp.where(h0, 0.0, x)], axis=0)

    am = [stack(av[p] * jnp.exp(cl[p] - lw[p])) for p in pr]
    bm = [stack(bv[p] * jnp.exp(-cl[p])) for p in pr]
    km = [stack(k[p] * jnp.exp(-cl[p])) for p in pr]
    rm = [stack(r[p] * jnp.exp(cl[p])) for p in pr]
    vm = [stack(v[p]) for p in pr]
    rn = lax.broadcasted_iota(jnp.int32, (n, n), 0)
    cn = lax.broadcasted_iota(jnp.int32, (n, n), 1)
    blk = (rn >= c) == (cn >= c)
    strict = blk & (cn < rn)
    incl = blk & (cn <= rn)
    lab = [jnp.where(strict, _dot(am[p], bm[p], "nt"), 0.0) for p in pr]
    lak = [jnp.where(strict, _dot(am[p], km[p], "nt"), 0.0) for p in pr]
    wrb = [jnp.where(incl, _dot(rm[p], bm[p], "nt"), 0.0) for p in pr]
    wrk = [jnp.where(incl, _dot(rm[p], km[p], "nt"), 0.0) for p in pr]
    if known is None:
        tinv = _unit_lower_inverses(lab)
    else:
        tinv = _known_inverses(lab, known[0])
        lak, wrb, wrk = _use_kept(lak, known[1]), _use_kept(wrb, known[2]), _use_kept(wrk, known[3])
    rhs = [_dot(am[p], sv[p], "nt") + _dot(lak[p], vm[p], "nn") for p in pr]
    um = [_dot(tinv[p], rhs[p], "nn") for p in pr]
    if known is not None:
        um = _use_kept(um, known[4])
    ym = [_dot(rm[p], sv[p], "nt") + _dot(wrb[p], um[p], "nn") + _dot(wrk[p], vm[p], "nn") for p in pr]
    sn = [(sv[p] + _dot(um[p], bm[p], "tn") + _dot(vm[p], km[p], "tn")) * jnp.exp(cl_last[p]) for p in pr]
    ys = [ym[p][:c] + ym[p][c:] for p in pr]
    return jnp.concatenate(ys, axis=1), jnp.concatenate(sn, axis=0), [tinv, lak, wrb, wrk, um]


def _f_mixers(ps, xs, cs):
    return _mixers(ps, xs, cs, None)


def _f_mixers_kept(ps, xs, cs, kept):
    return _mixers(ps, xs, cs, kept[0])[:2]


def _mixers(ps, xs, cs, kept):
    oa, st = _f_hgrn(ps[:2], xs[:4], cs[:1])
    (r, lw, k, v, av, bv, g), prev = _f_rwpre(ps[2:10], xs[4:], cs[1:2])
    y, sv, keep = _f_rwscan([], [r, lw, k, v, av, bv], cs[2:], kept)
    ob, _ = _f_rwpost(ps[10:], y + [r, k, v, g], [])
    return oa + ob, st + prev + sv, [keep]


def _f_rwpost(ps, xs, cs):
    ln_w, ln_b, r_k = ps
    y, r, k, v, g = xs
    inv_n = 1.0 / RW_N
    yc = y - _headsum(y) * inv_n
    var = _headsum(yc * yc) * inv_n
    yn = yc * lax.rsqrt(var + GN_EPS)
    yn = yn * ln_w + ln_b
    bonus = _headsum(r * k * r_k) * v
    return [(yn + bonus) * g], []


def _f_merge(ps, xs, cs):
    ga, gb, ya, yb = xs
    return [_sigmoid(ga) * ya + _sigmoid(gb) * yb], []


def _f_post1(ps, xs, cs):
    x, mix = xs
    h1 = x + _rms(mix, ps[0])
    return [h1, _rms(h1, ps[1])], []


def _f_conv(ps, xs, cs):
    cw, cb = ps
    p1, p2 = cs
    w0, w1, w2 = _row(cw, 0), _row(cw, 1), _row(cw, 2)
    t = xs[0].shape[0]
    hc = []
    for i, x in enumerate(xs):
        sl = slice(i * DFF, (i + 1) * DFF)
        s1 = _shift_down(x, p1[:, sl])
        s2 = _shift_down(s1, p2[:, sl])
        hc.append(cb[:, sl] + w0[:, sl] * s2 + w1[:, sl] * s1 + w2[:, sl] * x)
    n1 = jnp.concatenate([_row(x, t - 1) for x in xs], axis=1)
    n2 = jnp.concatenate([_row(x, t - 2) for x in xs], axis=1)
    return [_silu(hc[0]) * hc[1]], [n1, n2]


class _Stage:
    def __init__(self, name, f, g, tm, par_per_g, in_pieces, in_offs, carry_shapes, out_pieces, out_dtypes,
                 kept_shapes=(), f_kept=None):
        self.name, self.f, self.g, self.tm = name, f, g, tm
        self.par_per_g, self.in_pieces, self.in_offs = par_per_g, in_pieces, in_offs
        self.carry_shapes, self.out_pieces, self.out_dtypes = carry_shapes, out_pieces, out_dtypes
        self.kept_shapes, self.f_kept = list(kept_shapes), f_kept


def _par_spec(arr, per_g, g):
    r, c = arr.shape
    if per_g:
        return pl.BlockSpec((r, c // g), lambda gi, ni: (0, gi))
    return pl.BlockSpec((r, c), lambda gi, ni: (0, 0))


def _row_spec(tm, width, off, n, rev):
    if rev:
        return pl.BlockSpec((tm, width), lambda gi, ni: (n - 1 - ni, off + gi))
    return pl.BlockSpec((tm, width), lambda gi, ni: (ni, off + gi))


def _carry_spec(shape, n, rev):
    if rev:
        return pl.BlockSpec((None, None) + shape, lambda gi, ni: (gi, n - 1 - ni, 0, 0))
    return pl.BlockSpec((None, None) + shape, lambda gi, ni: (gi, ni, 0, 0))


def _load_pieces(refs, pieces_list):
    out = []
    for ref, pieces in zip(refs, pieces_list):
        o = 0
        for w in pieces:
            out.append(ref[:, o:o + w].astype(F32))
            o += w
    return out


def _store_pieces(refs, pieces_list, vals):
    k = 0
    for ref, pieces in zip(refs, pieces_list):
        o = 0
        for w in pieces:
            ref[:, o:o + w] = vals[k].astype(ref.dtype)
            k += 1
            o += w


_ANY = pl.BlockSpec(memory_space=pl.ANY)


class _Exchange:
    def __init__(self, kind, arrs):
        self.kind, self.arrs, self.results = kind, list(arrs), None
        if kind == "scatter":
            self.out_shape = [jax.ShapeDtypeStruct((N_DEV - 1,) + a.shape[1:], a.dtype) for a in self.arrs]
        else:
            self.out_shape = [jax.ShapeDtypeStruct((N_DEV,) + a.shape, a.dtype) for a in self.arrs]
        self.nsem = (N_DEV if kind == "gather2" else N_DEV - 1) * len(self.arrs)

    def copies(self, in_refs, out_refs, ssem, rsem):
        x, y, c = lax.axis_index("x"), lax.axis_index("y"), lax.axis_index("c")
        me = 4 * x + 2 * y + c
        cps = []
        for a, (i_ref, o_ref) in enumerate(zip(in_refs, out_refs)):
            for j in range(1, N_DEV):
                px = 1 - x if j & 4 else x
                py = 1 - y if j & 2 else y
                pc = 1 - c if j & 1 else c
                if self.kind == "gather":
                    src, dst = i_ref, o_ref.at[me]
                else:
                    src, dst = i_ref.at[4 * px + 2 * py + pc], o_ref.at[j - 1]
                s = (N_DEV - 1) * a + j - 1
                cps.append(pltpu.make_async_remote_copy(src_ref=src, dst_ref=dst, send_sem=ssem.at[s],
                                                        recv_sem=rsem.at[s], device_id=(px, py, pc),
                                                        device_id_type=MESH))
        return cps

    def run(self, step, total, in_refs, out_refs, ssem, rsem):
        if self.kind == "gather2":
            return self.run_two_level(step, total, in_refs, out_refs, ssem, rsem)

        @pl.when(step == 0)
        def _():
            for cp in self.copies(in_refs, out_refs, ssem, rsem):
                cp.start()

        @pl.when(step == total - 1)
        def _():
            for cp in self.copies(in_refs, out_refs, ssem, rsem):
                cp.wait()

    def run_two_level(self, step, total, in_refs, out_refs, ssem, rsem):
        x, y, c = lax.axis_index("x"), lax.axis_index("y"), lax.axis_index("c")
        sibling, xn, yn = (x, y, 1 - c), (1 - x, y, c), (x, 1 - y, c)
        arrs = range(len(in_refs))
        ns = N_DEV

        def num(px, py, pc):
            return 4 * px + 2 * py + pc

        def copy(a, k, to, src, dst):
            return pltpu.make_async_remote_copy(src_ref=src, dst_ref=dst, send_sem=ssem.at[ns * a + k],
                                                recv_sem=rsem.at[ns * a + k], device_id=to, device_id_type=MESH)

        def blk(a, b):
            return out_refs[a].at[b]

        def half(a, b, second):
            h = self.arrs[a].shape[0] // 2
            return out_refs[a].at[b, pl.ds(h if second else 0, h)]

        bx, by, bd = num(1 - x, y, c), num(x, 1 - y, c), num(1 - x, 1 - y, c)

        def firsts(a):
            own = blk(a, num(x, y, c))
            return [copy(a, 0, sibling, in_refs[a], own), copy(a, 1, xn, in_refs[a], own),
                    copy(a, 2, yn, in_refs[a], own)]

        def seconds(a):
            return [copy(a, 3, yn, half(a, bx, False), half(a, bx, False)), copy(a, 5, sibling, blk(a, bx), blk(a, bx)),
                    copy(a, 4, xn, half(a, by, True), half(a, by, True)), copy(a, 6, sibling, blk(a, by), blk(a, by))]

        def third(a):
            return copy(a, 7, sibling, blk(a, bd), blk(a, bd))

        @pl.when(step == 0)
        def _():
            for a in arrs:
                for cp in firsts(a):
                    cp.start()

        @pl.when(step == total // 2)
        def _():
            for a in arrs:
                copy(a, 1, xn, blk(a, bx), blk(a, bx)).wait_recv()
                copy(a, 2, yn, blk(a, by), blk(a, by)).wait_recv()
                for cp in seconds(a):
                    cp.start()

        @pl.when(step == (4 * total) // 5)
        def _():
            for a in arrs:
                copy(a, 3, yn, half(a, bd, False), half(a, bd, False)).wait_recv()
                copy(a, 4, xn, half(a, bd, True), half(a, bd, True)).wait_recv()
                third(a).start()

        @pl.when(step == total - 1)
        def _():
            for a in arrs:
                for k, b in ((0, num(x, y, 1 - c)), (5, num(1 - x, y, 1 - c)), (6, num(x, 1 - y, 1 - c)),
                             (7, num(1 - x, 1 - y, 1 - c))):
                    copy(a, k, sibling, blk(a, b), blk(a, b)).wait_recv()
                for cp in firsts(a) + seconds(a) + [third(a)]:
                    cp.wait_send()


def _hook_specs(hook):
    if hook is None:
        return [], [], [], []
    na = len(hook.arrs)
    sems = [pltpu.SemaphoreType.DMA((hook.nsem,)), pltpu.SemaphoreType.DMA((hook.nsem,))]
    return [_ANY] * na, [_ANY] * na, hook.out_shape, sems


def _stage_fwd(st, t, params, inputs, hook=None):
    g, tm = st.g, min(st.tm, t)
    n = t // tm
    npar, nin, ncar, nout = len(params), len(inputs), len(st.carry_shapes), len(st.out_pieces)
    nk = len(st.kept_shapes)
    h_in, h_out, h_shape, h_sems = _hook_specs(hook)
    nh = len(h_in)

    def body(*refs):
        p_refs = refs[:npar]
        x_refs = refs[npar:npar + nin]
        hi_refs = refs[npar + nin:npar + nin + nh]
        o = npar + nin + nh
        o_refs = refs[o:o + nout]
        s_refs = refs[o + nout:o + nout + ncar]
        k_refs = refs[o + nout + ncar:o + nout + ncar + nk]
        o += nout + ncar + nk
        ho_refs = refs[o:o + nh]
        c_scr = refs[o + nh:o + nh + ncar]
        gi, ni = pl.program_id(0), pl.program_id(1)
        if hook is not None:
            step = gi * n + ni
            hook.run(step, g * n, hi_refs, ho_refs, *refs[-2:])

        @pl.when(ni == 0)
        def _():
            for c in c_scr:
                c[...] = jnp.zeros(c.shape, F32)

        ps = [r[...].astype(F32) for r in p_refs]
        xs = _load_pieces(x_refs, st.in_pieces)
        cs = [c[...] for c in c_scr]
        for s, c in zip(s_refs, cs):
            s[...] = c
        res = st.f(ps, xs, cs)
        outs, ncs = res[0], res[1]
        _store_pieces(o_refs, st.out_pieces, outs)
        for c, v in zip(c_scr, ncs):
            c[...] = v
        for kr, kv in zip(k_refs, res[2] if nk else []):
            kr[...] = kv.astype(kr.dtype)

    in_specs = [_par_spec(p, pg, g) for p, pg in zip(params, st.par_per_g)]
    in_specs += [_row_spec(tm, sum(pc), off, n, False) for pc, off in zip(st.in_pieces, st.in_offs)]
    out_specs = [_row_spec(tm, sum(pc), 0, n, False) for pc in st.out_pieces]
    out_specs += [_carry_spec(s, n, False) for s in st.carry_shapes]
    out_specs += [pl.BlockSpec(s, lambda gi, ni: (ni, 0)) for s in st.kept_shapes]
    out_shape = [jax.ShapeDtypeStruct((t, g * sum(pc)), dt) for pc, dt in zip(st.out_pieces, st.out_dtypes)]
    out_shape += [jax.ShapeDtypeStruct((g, n) + s, F32) for s in st.carry_shapes]
    out_shape += [jax.ShapeDtypeStruct((n * s[0], s[1]), BF) for s in st.kept_shapes]
    res = pl.pallas_call(
        body, name=st.name + "_fwd", grid=(g, n), in_specs=in_specs + h_in, out_specs=out_specs + h_out,
        out_shape=out_shape + h_shape,
        scratch_shapes=[pltpu.VMEM(s, F32) for s in st.carry_shapes] + h_sems,
        compiler_params=_cparams(("arbitrary", "arbitrary")),
    )(*params, *inputs, *(hook.arrs if hook else []))
    if hook is not None:
        hook.results = list(res[nout + ncar + nk:])
    return list(res[:nout]), list(res[nout:nout + ncar + nk])


def _stage_bwd(st, t, params, inputs, saved, douts, dx_dtypes, hook=None):
    g, tm = st.g, min(st.tm, t)
    n = t // tm
    npar, nin, ncar = len(params), len(inputs), len(st.carry_shapes)
    nk = len(st.kept_shapes)
    flat_d = [d for ds in douts for d in ds]
    nd = len(flat_d)
    dx_idx = [i for i, dt in enumerate(dx_dtypes) if dt is not None]
    h_in, h_out, h_shape, h_sems = _hook_specs(hook)
    nh = len(h_in)

    def body(*refs):
        p_refs = refs[:npar]
        x_refs = refs[npar:npar + nin]
        s_refs = refs[npar + nin:npar + nin + ncar]
        k_refs = refs[npar + nin + ncar:npar + nin + ncar + nk]
        o = npar + nin + ncar + nk
        d_refs = refs[o:o + nd]
        hi_refs = refs[o + nd:o + nd + nh]
        o += nd + nh
        dp_refs = refs[o:o + npar]
        dx_refs = refs[o + npar:o + npar + len(dx_idx)]
        ho_refs = refs[o + npar + len(dx_idx):o + npar + len(dx_idx) + nh]
        dc_scr = refs[o + npar + len(dx_idx) + nh:o + npar + len(dx_idx) + nh + ncar]
        gi, ni = pl.program_id(0), pl.program_id(1)
        if hook is not None:
            step = gi * n + ni
            hook.run(step, g * n, hi_refs, ho_refs, *refs[-2:])

        @pl.when(ni == 0)
        def _():
            for c in dc_scr:
                c[...] = jnp.zeros(c.shape, F32)

        ps = [r[...].astype(F32) for r in p_refs]
        xs = _load_pieces(x_refs, st.in_pieces)
        cs = [s[...] for s in s_refs]
        dys = []
        k = 0
        for ds, pieces in zip(douts, st.out_pieces):
            acc = _load_pieces([d_refs[k]], [pieces])
            for j in range(1, len(ds)):
                more = _load_pieces([d_refs[k + j]], [pieces])
                acc = [a + b for a, b in zip(acc, more)]
            dys += acc
            k += len(ds)
        if nk:
            kept = [r[...].astype(F32) for r in k_refs]
            _, vjp = jax.vjp(lambda p, x, c: st.f_kept(p, x, c, kept), ps, xs, cs)
        else:
            _, vjp = jax.vjp(st.f, ps, xs, cs)
        dps, dxs, dcs = vjp((dys, [c[...] for c in dc_scr]))
        k = 0
        per_in = []
        for pieces in st.in_pieces:
            per_in.append(dxs[k:k + len(pieces)])
            k += len(pieces)
        for ref, i in zip(dx_refs, dx_idx):
            _store_pieces([ref], [st.in_pieces[i]], per_in[i])
        for c, v in zip(dc_scr, dcs):
            c[...] = v
        for ref, dp, pg in zip(dp_refs, dps, st.par_per_g):
            first = (ni == 0) if pg else ((ni == 0) & (gi == 0))

            @pl.when(first)
            def _():
                ref[...] = jnp.zeros(ref.shape, F32)

            ref[...] += dp

    in_specs = [_par_spec(p, pg, g) for p, pg in zip(params, st.par_per_g)]
    in_specs += [_row_spec(tm, sum(pc), off, n, True) for pc, off in zip(st.in_pieces, st.in_offs)]
    in_specs += [_carry_spec(s, n, True) for s in st.carry_shapes]
    in_specs += [pl.BlockSpec(s, lambda gi, ni: (n - 1 - ni, 0)) for s in st.kept_shapes]
    for ds, pc in zip(douts, st.out_pieces):
        in_specs += [_row_spec(tm, sum(pc), 0, n, True) for _ in ds]
    out_specs = [_par_spec(p, pg, g) for p, pg in zip(params, st.par_per_g)]
    out_specs += [_row_spec(tm, sum(st.in_pieces[i]), 0, n, True) for i in dx_idx]
    out_shape = [jax.ShapeDtypeStruct(p.shape, F32) for p in params]
    out_shape += [jax.ShapeDtypeStruct((t, g * sum(st.in_pieces[i])), dx_dtypes[i]) for i in dx_idx]
    res = pl.pallas_call(
        body, name=st.name + "_bwd", grid=(g, n), in_specs=in_specs + h_in, out_specs=out_specs + h_out,
        out_shape=out_shape + h_shape,
        scratch_shapes=[pltpu.VMEM(s, F32) for s in st.carry_shapes] + h_sems,
        compiler_params=_cparams(("arbitrary", "arbitrary")),
    )(*params, *inputs, *saved, *flat_d, *(hook.arrs if hook else []))
    if hook is not None:
        hook.results = list(res[npar + len(dx_idx):])
    return list(res[:npar]), list(res[npar:npar + len(dx_idx)])


def _pick(n, cap):
    if n <= cap:
        return n
    best = LANES
    for k in range(1, n // LANES + 1):
        if (n // LANES) % k == 0 and k * LANES <= cap:
            best = k * LANES
    return best


def _mm(name, a, b, mode, out_dtype=F32, tm=1024, tn=512, b_outer=False, token=None):
    m = a.shape[1] if mode == "tn" else a.shape[0]
    k = a.shape[0] if mode == "tn" else a.shape[1]
    n = b.shape[0] if mode == "nt" else b.shape[1]
    tm, tn = _pick(m, tm), _pick(n, tn)
    if b_outer:
        grid = (n // tn, m // tm)
        ij = lambda p, q: (q, p)
    else:
        grid = (m // tm, n // tn)
        ij = lambda p, q: (p, q)
    extra = [] if token is None else [token]

    def body(*refs):
        a_ref, b_ref, o_ref = refs[0], refs[1], refs[-1]
        o_ref[...] = _raw_dot(a_ref[...], b_ref[...], mode).astype(o_ref.dtype)

    if mode == "tn":
        a_spec = pl.BlockSpec((k, tm), lambda p, q: (0, ij(p, q)[0]))
    else:
        a_spec = pl.BlockSpec((tm, k), lambda p, q: (ij(p, q)[0], 0))
    b_mode = dict(pipeline_mode=pl.Buffered(1)) if tn == n else {}
    if mode == "nt":
        b_spec = pl.BlockSpec((tn, k), lambda p, q: (ij(p, q)[1], 0), **b_mode)
    else:
        b_spec = pl.BlockSpec((k, tn), lambda p, q: (0, ij(p, q)[1]), **b_mode)
    return pl.pallas_call(
        body, name=name, grid=grid,
        in_specs=[a_spec, b_spec] + [pl.BlockSpec(e.shape, lambda p, q: (0, 0)) for e in extra],
        out_specs=pl.BlockSpec((tm, tn), lambda p, q: ij(p, q)),
        out_shape=jax.ShapeDtypeStruct((m, n), out_dtype),
        compiler_params=_cparams(("arbitrary", "arbitrary")),
    )(a, b, *extra)


def _mm_cols_nn(name, pieces, b, out_dtype, tm, token=None):
    m, n = pieces[0].shape[0], b.shape[1]
    tm = _pick(m, tm)
    offs = [sum(p.shape[1] for p in pieces[:i]) for i in range(len(pieces))]
    extra = [] if token is None else [token]
    na = len(pieces)

    def body(*refs):
        b_ref, o_ref = refs[na], refs[-1]
        acc = None
        for a_ref, off in zip(refs[:na], offs):
            t = _raw_dot(a_ref[...], b_ref[off:off + a_ref.shape[1], :], "nn")
            acc = t if acc is None else acc + t
        o_ref[...] = acc.astype(o_ref.dtype)

    return pl.pallas_call(
        body, name=name, grid=(m // tm,),
        in_specs=[pl.BlockSpec((tm, p.shape[1]), lambda i: (i, 0)) for p in pieces]
        + [pl.BlockSpec(b.shape, lambda i: (0, 0), pipeline_mode=pl.Buffered(1))]
        + [pl.BlockSpec(e.shape, lambda i: (0, 0)) for e in extra],
        out_specs=pl.BlockSpec((tm, n), lambda i: (i, 0)), out_shape=jax.ShapeDtypeStruct((m, n), out_dtype),
        compiler_params=_cparams(("arbitrary",)),
    )(*pieces, b, *extra)


def _mm_cols_tn(name, pieces, b, out_dtype, tm):
    k, n = b.shape
    counts = [p.shape[1] // tm for p in pieces]
    starts = [sum(counts[:i]) for i in range(len(pieces))]
    na = len(pieces)

    def body(*refs):
        b_ref, o_ref = refs[na], refs[-1]
        i = pl.program_id(0)
        for a_ref, s, c in zip(refs[:na], starts, counts):
            @pl.when((i >= s) & (i < s + c))
            def _():
                o_ref[...] = _raw_dot(a_ref[...], b_ref[...], "tn").astype(o_ref.dtype)

    def spec(s, c):
        return pl.BlockSpec((k, tm), lambda i: (0, jnp.clip(i - s, 0, c - 1)))

    return pl.pallas_call(
        body, name=name, grid=(sum(counts),),
        in_specs=[spec(s, c) for s, c in zip(starts, counts)]
        + [pl.BlockSpec(b.shape, lambda i: (0, 0), pipeline_mode=pl.Buffered(1))],
        out_specs=pl.BlockSpec((tm, n), lambda i: (i, 0)),
        out_shape=jax.ShapeDtypeStruct((sum(counts) * tm, n), out_dtype),
        compiler_params=_cparams(("arbitrary",)),
    )(*pieces, b)


def _loss_stage(t, g_post, h1, ff, tgt):
    tm = min(256, t)
    n = t // tm

    def body(g_ref, h_ref, f_ref, t_ref, loss_ref, dg_ref, dh_ref, df_ref):
        ni = pl.program_id(0)
        target = t_ref[...]

        def lossf(g, h1, ff):
            e = h1 + _rms(ff, g) - target
            return 0.5 * jnp.sum(jnp.mean(e * e, axis=-1))

        l, (dg, dh, df) = jax.value_and_grad(lossf, argnums=(0, 1, 2))(g_ref[...], h_ref[...], f_ref[...])

        @pl.when(ni == 0)
        def _():
            loss_ref[...] = jnp.zeros(loss_ref.shape, F32)
            dg_ref[...] = jnp.zeros(dg_ref.shape, F32)

        loss_ref[...] += jnp.full(loss_ref.shape, l, F32)
        dg_ref[...] += dg
        dh_ref[...] = dh
        df_ref[...] = df.astype(df_ref.dtype)

    row = pl.BlockSpec((tm, D), lambda ni: (ni, 0))
    one = pl.BlockSpec((1, D), lambda ni: (0, 0))
    return pl.pallas_call(
        body, name="loss_head", grid=(n,), in_specs=[one, row, row, row],
        out_specs=[pl.BlockSpec((1, LANES), lambda ni: (0, 0)), one, row, row],
        out_shape=[jax.ShapeDtypeStruct((1, LANES), F32), jax.ShapeDtypeStruct((1, D), F32),
                   jax.ShapeDtypeStruct((t, D), F32), jax.ShapeDtypeStruct((t, D), BF)],
        compiler_params=_cparams(("arbitrary",)),
    )(g_post, h1, ff, tgt)


_ANY = pl.BlockSpec(memory_space=pl.ANY)


def _all_gather(name, blks):
    na = len(blks)
    ns = 8

    def body(*refs):
        x_refs, out_refs = refs[:na], refs[na:2 * na]
        send_sems, recv_sems, local_sems = refs[2 * na:]
        x, y, cc = lax.axis_index("x"), lax.axis_index("y"), lax.axis_index("c")
        sibling, xn, yn = (x, y, 1 - cc), (1 - x, y, cc), (x, 1 - y, cc)

        def num(px, py, pc):
            return 4 * px + 2 * py + pc

        def copy(a, k, to, src, dst):
            return pltpu.make_async_remote_copy(src_ref=src, dst_ref=dst, send_sem=send_sems.at[ns * a + k],
                                                recv_sem=recv_sems.at[ns * a + k], device_id=to, device_id_type=MESH)

        def halves(a, blk):
            h = blks[a].shape[0] // 2
            return out_refs[a].at[blk, pl.ds(0, h)], out_refs[a].at[blk, pl.ds(h, h)]

        mine, sends = [], []
        for a in range(na):
            o = out_refs[a]
            m = pltpu.make_async_copy(x_refs[a], o.at[num(x, y, cc)], local_sems.at[a])
            m.start()
            mine.append(m)
            own = o.at[num(x, y, cc)]
            sends.append([copy(a, 0, sibling, x_refs[a], own), copy(a, 1, xn, x_refs[a], own),
                          copy(a, 2, yn, x_refs[a], own)])
            for cp in sends[a]:
                cp.start()
        for a in range(na):
            o = out_refs[a]
            bx, by, bd = num(1 - x, y, cc), num(x, 1 - y, cc), num(1 - x, 1 - y, cc)
            copy(a, 1, xn, o.at[bx], o.at[bx]).wait_recv()
            more = [copy(a, 3, yn, halves(a, bx)[0], halves(a, bx)[0]), copy(a, 5, sibling, o.at[bx], o.at[bx])]
            for cp in more:
                cp.start()
            sends[a] += more
        for a in range(na):
            o = out_refs[a]
            bx, by, bd = num(1 - x, y, cc), num(x, 1 - y, cc), num(1 - x, 1 - y, cc)
            copy(a, 2, yn, o.at[by], o.at[by]).wait_recv()
            more = [copy(a, 4, xn, halves(a, by)[1], halves(a, by)[1]), copy(a, 6, sibling, o.at[by], o.at[by])]
            for cp in more:
                cp.start()
            sends[a] += more
        for a in range(na):
            o = out_refs[a]
            bd = num(1 - x, 1 - y, cc)
            copy(a, 3, yn, halves(a, bd)[0], halves(a, bd)[0]).wait_recv()
            copy(a, 4, xn, halves(a, bd)[1], halves(a, bd)[1]).wait_recv()
            fw = copy(a, 7, sibling, o.at[bd], o.at[bd])
            fw.start()
            sends[a].append(fw)
        for a in range(na):
            o = out_refs[a]
            for k, blk in ((0, num(x, y, 1 - cc)), (5, num(1 - x, y, 1 - cc)), (6, num(x, 1 - y, 1 - cc)),
                           (7, num(1 - x, 1 - y, 1 - cc))):
                copy(a, k, sibling, o.at[blk], o.at[blk]).wait_recv()
            for cp in sends[a]:
                cp.wait_send()
        for m in mine:
            m.wait()

    res = pl.pallas_call(
        body, name=name, in_specs=[_ANY] * na, out_specs=[_ANY] * na,
        out_shape=[jax.ShapeDtypeStruct((N_DEV,) + b.shape, b.dtype) for b in blks],
        scratch_shapes=[pltpu.SemaphoreType.DMA((ns * na,)), pltpu.SemaphoreType.DMA((ns * na,)),
                        pltpu.SemaphoreType.DMA((na,))],
    )(*blks)
    return list(res)


def _all_gather_small(name, blk):
    def body(x_ref, out_ref, ssem, rsem, lsem):
        x, y, c = lax.axis_index("x"), lax.axis_index("y"), lax.axis_index("c")
        me = 4 * x + 2 * y + c
        mine = pltpu.make_async_copy(x_ref, out_ref.at[me], lsem)
        mine.start()
        cps = []
        for j in range(1, N_DEV):
            px = 1 - x if j & 4 else x
            py = 1 - y if j & 2 else y
            pc = 1 - c if j & 1 else c
            cps.append(pltpu.make_async_remote_copy(src_ref=x_ref, dst_ref=out_ref.at[me], send_sem=ssem.at[j - 1],
                                                    recv_sem=rsem.at[j - 1], device_id=(px, py, pc),
                                                    device_id_type=MESH))
        for cp in cps:
            cp.start()
        for cp in cps:
            cp.wait()
        mine.wait()

    return pl.pallas_call(
        body, name=name, in_specs=[_ANY], out_specs=_ANY,
        out_shape=jax.ShapeDtypeStruct((N_DEV,) + blk.shape, blk.dtype),
        scratch_shapes=[pltpu.SemaphoreType.DMA((N_DEV - 1,)), pltpu.SemaphoreType.DMA((N_DEV - 1,)),
                        pltpu.SemaphoreType.DMA],
    )(blk)


def _reduce_pair(g8s):
    na = len(g8s)

    def body(*refs):
        g_refs, recv_refs = refs[:na], refs[na:2 * na]
        ssem, rsem = refs[2 * na:]
        x, y, cc = lax.axis_index("x"), lax.axis_index("y"), lax.axis_index("c")
        chips = [(x, y), (1 - x, y), (x, 1 - y), (1 - x, 1 - y)]
        sib = (x, y, 1 - cc)
        for a in range(na):
            for k, (cx, cy) in enumerate(chips):
                pltpu.make_async_remote_copy(
                    src_ref=g_refs[a].at[4 * cx + 2 * cy + 1 - cc], dst_ref=recv_refs[a].at[k],
                    send_sem=ssem.at[a], recv_sem=rsem.at[a], device_id=sib, device_id_type=MESH).start()
        for a in range(na):
            pltpu.make_async_remote_copy(src_ref=recv_refs[a], dst_ref=recv_refs[a], send_sem=ssem.at[a],
                                         recv_sem=rsem.at[a], device_id=sib, device_id_type=MESH).wait()

    res = pl.pallas_call(
        body, name="reduce_pair", in_specs=[_ANY] * na, out_specs=[_ANY] * na,
        out_shape=[jax.ShapeDtypeStruct((4,) + g.shape[1:], g.dtype) for g in g8s],
        scratch_shapes=[pltpu.SemaphoreType.DMA((na,)), pltpu.SemaphoreType.DMA((na,))],
    )(*g8s)
    return list(res)


_HBM = pl.BlockSpec(memory_space=pltpu.HBM)
_SEM = pl.BlockSpec(memory_space=pltpu.SEMAPHORE)
_EFFECT = pltpu.SideEffectType.DATAFLOW_SIDE_EFFECTING


def _chip_swap_copies(s_refs, land_refs, ssem, rsem):
    x, y, c = lax.axis_index("x"), lax.axis_index("y"), lax.axis_index("c")
    targets = [(1 - x, y, c), (x, 1 - y, c), (1 - x, 1 - y, c)]
    return [pltpu.make_async_remote_copy(src_ref=s.at[k], dst_ref=d.at[k], send_sem=ssem.at[3 * a + k],
                                         recv_sem=rsem.at[3 * a + k], device_id=targets[k], device_id_type=MESH)
            for a, (s, d) in enumerate(zip(s_refs, land_refs)) for k in range(3)]


def _chip_swap_start(sends):
    na = len(sends)

    def body(*refs):
        cps = _chip_swap_copies(refs[:na], refs[na:2 * na], refs[2 * na], refs[2 * na + 1])
        for cp in cps:
            cp.start()
        token = refs[-1]
        token[...] = jnp.zeros(token.shape, token.dtype)

    bufs = [pltpu.HBM(s.shape, s.dtype) for s in sends]
    res = pl.pallas_call(
        body, name="chip_swap_start",
        out_shape=[pltpu.SemaphoreType.DMA((3 * na,)), pltpu.SemaphoreType.DMA((3 * na,))] + bufs + bufs
        + [jax.ShapeDtypeStruct((8, LANES), F32)],
        in_specs=[_HBM] * (2 * na), out_specs=[_SEM, _SEM] + [_HBM] * (2 * na) + [pl.BlockSpec(memory_space=pltpu.VMEM)],
        input_output_aliases={i: 2 + i for i in range(2 * na)},
        compiler_params=pltpu.CompilerParams(has_side_effects=_EFFECT),
    )(*[pltpu.with_memory_space_constraint(s, pltpu.HBM) for s in sends],
      *[pltpu.with_memory_space_constraint(lax.empty(s.shape, s.dtype), pltpu.HBM) for s in sends])
    return res[0], res[1], list(res[2:2 + na]), list(res[2 + na:2 + 2 * na]), res[-1]


def _chip_swap_wait(ssem, rsem, srcs, lands, after):
    na = len(srcs)

    def body(*refs):
        cps = _chip_swap_copies(refs[:na], refs[na:2 * na], refs[2 * na], refs[2 * na + 1])
        for cp in cps:
            cp.wait_send()
            cp.wait_recv()

    bufs = [pltpu.HBM(s.shape, s.dtype) for s in srcs]
    res = pl.pallas_call(
        body, name="chip_swap_wait", out_shape=bufs + bufs,
        in_specs=[_HBM] * (2 * na) + [_SEM, _SEM, _ANY], out_specs=[_HBM] * (2 * na),
        input_output_aliases={i: i for i in range(2 * na)},
        compiler_params=pltpu.CompilerParams(has_side_effects=_EFFECT),
    )(*srcs, *lands, ssem, rsem, after)
    return list(res[na:])


def _pick_rows(r, c, budget=TILE_BYTES):
    if r * c * 4 <= budget or r % 16:
        return r
    best = 16
    for tr in range(16, r, 16):
        if r % tr == 0 and tr * c * 4 <= budget:
            best = tr
    return best


def _pair_sum(name, idx4, g8, recv4):
    _, r, c = g8.shape
    tr = _pick_rows(r, c, 2 * TILE_BYTES)

    def body(idx_ref, a_ref, b_ref, o0_ref, o3_ref):
        k = pl.program_id(1)
        s = a_ref[...].astype(F32) + b_ref[...].astype(F32)

        @pl.when(k == 0)
        def _():
            o0_ref[...] = s

        @pl.when(k > 0)
        def _():
            o3_ref[...] = s.astype(BF)

    spec = pltpu.PrefetchScalarGridSpec(
        num_scalar_prefetch=1, grid=(r // tr, 4),
        in_specs=[pl.BlockSpec((None, tr, c), lambda i, k, idx: (idx[k], i, 0)),
                  pl.BlockSpec((None, tr, c), lambda i, k, idx: (k, i, 0))],
        out_specs=[pl.BlockSpec((tr, c), lambda i, k, idx: (i, 0)),
                   pl.BlockSpec((None, tr, c), lambda i, k, idx: (jnp.maximum(k - 1, 0), i, 0))])
    return pl.pallas_call(
        body, name=name, grid_spec=spec,
        out_shape=[jax.ShapeDtypeStruct((r, c), F32), jax.ShapeDtypeStruct((3, r, c), BF)],
        compiler_params=_cparams(("arbitrary", "arbitrary")),
    )(idx4, g8, recv4)


def _adamw(w, g, m, v):
    m = ADAM_B1 * m + (1.0 - ADAM_B1) * g
    v = ADAM_B2 * v + (1.0 - ADAM_B2) * jnp.square(g)
    m_hat = m / (1.0 - ADAM_B1 ** ADAM_STEP)
    v_hat = v / (1.0 - ADAM_B2 ** ADAM_STEP)
    delta = -ADAM_LR * (m_hat / (jnp.sqrt(v_hat) + ADAM_EPS) + ADAM_WD * w)
    return delta, m, v


def _sum_partials(name, idx1, own, recv):
    _, r, c = own.shape
    tr = _pick_rows(r, c, 2 * TILE_BYTES)
    nj = recv.shape[0]

    def body(idx_ref, p_ref, r_ref, g_out):
        g = p_ref[...].astype(F32)
        for k in range(nj):
            g = g + r_ref[k].astype(F32)
        g_out[...] = g

    row = pl.BlockSpec((tr, c), lambda i, idx: (i, 0))
    spec = pltpu.PrefetchScalarGridSpec(
        num_scalar_prefetch=1, grid=(r // tr,),
        in_specs=[pl.BlockSpec((None, tr, c), lambda i, idx: (idx[0], i, 0)),
                  pl.BlockSpec((nj, tr, c), lambda i, idx: (0, i, 0))],
        out_specs=row)
    return pl.pallas_call(body, name=name, grid_spec=spec, out_shape=jax.ShapeDtypeStruct((r, c), F32),
                          compiler_params=_cparams(("arbitrary",)))(idx1, own, recv)


def _adam_sharded(name, idx1, own, recv, w, m, v):
    r, c = w.shape
    tr = _pick_rows(r, c)
    nj = 0 if recv is None else recv.shape[0]
    if recv is None:
        recv = jnp.zeros((1, 8, LANES), BF)

    def body(idx_ref, p_ref, r_ref, w_ref, m_ref, v_ref, g_out, d_out, m_out, v_out):
        g = p_ref[...].astype(F32)
        for k in range(nj):
            g = g + r_ref[k].astype(F32)
        d, mn, vn = _adamw(w_ref[...], g, m_ref[...], v_ref[...])
        g_out[...] = g
        d_out[...] = d
        m_out[...] = mn
        v_out[...] = vn

    row = pl.BlockSpec((tr, c), lambda i, idx: (i, 0))
    if nj:
        recv_spec = pl.BlockSpec((nj, tr, c), lambda i, idx: (0, i, 0))
    else:
        recv_spec = pl.BlockSpec(recv.shape, lambda i, idx: (0, 0, 0))
    spec = pltpu.PrefetchScalarGridSpec(
        num_scalar_prefetch=1, grid=(r // tr,),
        in_specs=[pl.BlockSpec((None, tr, c), lambda i, idx: (idx[0], i, 0)), recv_spec, row, row, row],
        out_specs=[row] * 4)
    return pl.pallas_call(
        body, name=name, grid_spec=spec, out_shape=[jax.ShapeDtypeStruct((r, c), F32)] * 4,
        compiler_params=_cparams(("arbitrary",)),
    )(idx1, own, recv, w, m, v)


def _repl_rows():
    rows, r = {}, 0
    for name, cols in REPL:
        rows[name] = r
        r += REPL_ROWS.get(name, 1) * ((cols + D - 1) // D)
    return rows


LOSS_ROW = 24


def _pack_replicated(grads, loss_acc):
    rows = _repl_rows()
    names = [n for n, _ in REPL]

    def body(*refs):
        o_ref = refs[-1]
        o_ref[...] = jnp.zeros(o_ref.shape, F32)
        o_ref[LOSS_ROW:LOSS_ROW + 1, 0:LANES] = refs[-2][...]
        for name, ref in zip(names, refs[:-2]):
            r0 = rows[name]
            nr, nc = ref.shape
            if nc <= D:
                o_ref[r0:r0 + nr, 0:nc] = ref[...]
            else:
                for j in range((nc + D - 1) // D):
                    lo, hi = j * D, min(nc, (j + 1) * D)
                    o_ref[r0 + j:r0 + j + 1, 0:hi - lo] = ref[:, lo:hi]

    return pl.pallas_call(body, name="pack_replicated", out_shape=jax.ShapeDtypeStruct((REPL_TOTAL, D), F32),
                          compiler_params=_cparams())(*[grads[n] for n in names], loss_acc)


def _adam_replicated(g8, ws, ms, vs):
    rows = _repl_rows()
    names = [n for n, _ in REPL]
    np_ = len(names)

    def body(*refs):
        g_ref = refs[0]
        w_refs, m_refs, v_refs = refs[1:1 + np_], refs[1 + np_:1 + 2 * np_], refs[1 + 2 * np_:1 + 3 * np_]
        outs = refs[1 + 3 * np_:1 + 7 * np_]
        scr = refs[-1]
        g = g_ref[0]
        for k in range(1, N_DEV):
            g = g + g_ref[k]
        scr[...] = g
        refs[1 + 7 * np_][...] = scr[LOSS_ROW:LOSS_ROW + 1, 0:LANES]
        for i, name in enumerate(names):
            r0 = rows[name]
            nr, nc = w_refs[i].shape
            if nc <= D:
                gi = scr[r0:r0 + nr, 0:nc]
            else:
                parts = []
                for j in range((nc + D - 1) // D):
                    lo, hi = j * D, min(nc, (j + 1) * D)
                    parts.append(scr[r0 + j:r0 + j + 1, 0:hi - lo])
                gi = jnp.concatenate(parts, axis=1)
            d, mn, vn = _adamw(w_refs[i][...], gi, m_refs[i][...], v_refs[i][...])
            outs[i][...] = gi
            outs[np_ + i][...] = d
            outs[2 * np_ + i][...] = mn
            outs[3 * np_ + i][...] = vn

    shp = [jax.ShapeDtypeStruct(w.shape, F32) for w in ws]
    res = pl.pallas_call(body, name="adam_replicated", out_shape=shp * 4 + [jax.ShapeDtypeStruct((1, LANES), F32)],
                         scratch_shapes=[pltpu.VMEM((REPL_TOTAL, D), F32)], compiler_params=_cparams(),
                         )(g8, *ws, *ms, *vs)
    return [dict(zip(names, res[k * np_:(k + 1) * np_])) for k in range(4)], res[-1]


_WEIGHTS = ("attn_pre_norm", "w_in", "hgrn_lb", "hgrn_gnorm", "w_branch_a", "rwkv_mu", "rwkv_w0", "rwkv_w2",
            "rwkv_a0", "rwkv_a2", "rwkv_g2", "rwkv_k_k", "rwkv_k_a", "rwkv_r_k", "rwkv_ln_w", "rwkv_ln_b",
            "w_branch_b", "w_out", "attn_post_norm", "ffn_pre_norm", "w_up", "conv_w", "conv_b", "w_down",
            "ffn_post_norm")
_BIG = ("w_in", "w_up", "w_down", "w_branch_a", "w_branch_b", "w_out")


def _stages():
    one = [D]
    hw = HG_K * HG_PER_STEP
    rw = LANES * RW_PAIRS_PER_STEP
    return dict(
        pre1=_Stage("pre1", _f_pre1, 1, 256, [False], [one], [0], [], [one], [BF]),
        pre1_res=_Stage("pre1", _f_pre1_residual, 1, 256, [False], [one], [0], [], [one, one], [BF, F32]),
        mixers=_Stage("mixers", _f_mixers, 1, 2 * RW_CHUNK, [False] * 13, [[D] * 7 + [LANES, LANES]], [0],
                      [(hw, HG_K), (1, RW_COLS), (rw, LANES)], [one, one], [BF, BF],
                      kept_shapes=[(2 * RW_KEPT * RW_PAIRS_PER_STEP * 2 * RW_CHUNK, LANES)], f_kept=_f_mixers_kept),
        merge=_Stage("merge", _f_merge, 4, 512, [], [[256]] * 4, [29, 33, 0, 0], [], [[256]], [BF]),
        post1=_Stage("post1", _f_post1, 1, 256, [False, False], [one, one], [0, 0], [], [one, one], [F32, BF]),
        conv=_Stage("conv", _f_conv, 1, 128, [False, False], [[DFF, DFF]], [0], [(1, 2 * DFF), (1, 2 * DFF)],
                    [[DFF]], [BF]),
    )


def _cols_to_blocks(w, per):
    return w.reshape(w.shape[0], N_DEV, per).transpose(1, 0, 2)


def _blocks_to_cols(g):
    return g.transpose(1, 0, 2).reshape(g.shape[1], N_DEV * g.shape[2])


def kernel(x, attn_pre_norm, w_in, hgrn_lb, hgrn_gnorm, w_branch_a, rwkv_mu, rwkv_w0, rwkv_w2, rwkv_a0, rwkv_a2, rwkv_g2, rwkv_k_k, rwkv_k_a, rwkv_r_k, rwkv_ln_w, rwkv_ln_b, w_branch_b, w_out, attn_post_norm, ffn_pre_norm, w_up, conv_w, conv_b, w_down, ffn_post_norm, loss_target, m_attn_pre_norm, m_w_in, m_hgrn_lb, m_hgrn_gnorm, m_w_branch_a, m_rwkv_mu, m_rwkv_w0, m_rwkv_w2, m_rwkv_a0, m_rwkv_a2, m_rwkv_g2, m_rwkv_k_k, m_rwkv_k_a, m_rwkv_r_k, m_rwkv_ln_w, m_rwkv_ln_b, m_w_branch_b, m_w_out, m_attn_post_norm, m_ffn_pre_norm, m_w_up, m_conv_w, m_conv_b, m_w_down, m_ffn_post_norm, v_attn_pre_norm, v_w_in, v_hgrn_lb, v_hgrn_gnorm, v_w_branch_a, v_rwkv_mu, v_rwkv_w0, v_rwkv_w2, v_rwkv_a0, v_rwkv_a2, v_rwkv_g2, v_rwkv_k_k, v_rwkv_k_a, v_rwkv_r_k, v_rwkv_ln_w, v_rwkv_ln_b, v_w_branch_b, v_w_out, v_attn_post_norm, v_ffn_pre_norm, v_w_up, v_conv_w, v_conv_b, v_w_down, v_ffn_post_norm):
    w = dict(attn_pre_norm=attn_pre_norm, w_in=w_in, hgrn_lb=hgrn_lb, hgrn_gnorm=hgrn_gnorm, w_branch_a=w_branch_a, rwkv_mu=rwkv_mu, rwkv_w0=rwkv_w0, rwkv_w2=rwkv_w2, rwkv_a0=rwkv_a0, rwkv_a2=rwkv_a2, rwkv_g2=rwkv_g2, rwkv_k_k=rwkv_k_k, rwkv_k_a=rwkv_k_a, rwkv_r_k=rwkv_r_k, rwkv_ln_w=rwkv_ln_w, rwkv_ln_b=rwkv_ln_b, w_branch_b=w_branch_b, w_out=w_out, attn_post_norm=attn_post_norm, ffn_pre_norm=ffn_pre_norm, w_up=w_up, conv_w=conv_w, conv_b=conv_b, w_down=w_down, ffn_post_norm=ffn_post_norm)
    mo = dict(attn_pre_norm=m_attn_pre_norm, w_in=m_w_in, hgrn_lb=m_hgrn_lb, hgrn_gnorm=m_hgrn_gnorm, w_branch_a=m_w_branch_a, rwkv_mu=m_rwkv_mu, rwkv_w0=m_rwkv_w0, rwkv_w2=m_rwkv_w2, rwkv_a0=m_rwkv_a0, rwkv_a2=m_rwkv_a2, rwkv_g2=m_rwkv_g2, rwkv_k_k=m_rwkv_k_k, rwkv_k_a=m_rwkv_k_a, rwkv_r_k=m_rwkv_r_k, rwkv_ln_w=m_rwkv_ln_w, rwkv_ln_b=m_rwkv_ln_b, w_branch_b=m_w_branch_b, w_out=m_w_out, attn_post_norm=m_attn_post_norm, ffn_pre_norm=m_ffn_pre_norm, w_up=m_w_up, conv_w=m_conv_w, conv_b=m_conv_b, w_down=m_w_down, ffn_post_norm=m_ffn_post_norm)
    vo = dict(attn_pre_norm=v_attn_pre_norm, w_in=v_w_in, hgrn_lb=v_hgrn_lb, hgrn_gnorm=v_hgrn_gnorm, w_branch_a=v_w_branch_a, rwkv_mu=v_rwkv_mu, rwkv_w0=v_rwkv_w0, rwkv_w2=v_rwkv_w2, rwkv_a0=v_rwkv_a0, rwkv_a2=v_rwkv_a2, rwkv_g2=v_rwkv_g2, rwkv_k_k=v_rwkv_k_k, rwkv_k_a=v_rwkv_k_a, rwkv_r_k=v_rwkv_r_k, rwkv_ln_w=v_rwkv_ln_w, rwkv_ln_b=v_rwkv_ln_b, w_branch_b=v_w_branch_b, w_out=v_w_out, attn_post_norm=v_attn_post_norm, ffn_pre_norm=v_ffn_pre_norm, w_up=v_w_up, conv_w=v_conv_w, conv_b=v_conv_b, w_down=v_w_down, ffn_post_norm=v_ffn_post_norm)

    t = x.shape[1]
    x2 = x.reshape(t, D)
    tgt = loss_target.reshape(t, D)
    st = _stages()

    me = 4 * lax.axis_index("x") + 2 * lax.axis_index("y") + lax.axis_index("c")
    small = jnp.concatenate([rwkv_w2[0], rwkv_a2[0], rwkv_g2[0]], axis=0).astype(BF)
    g_in, g_small = _all_gather("gather_weights", [w_in[0].T.astype(BF), small])
    fw_in_t = g_in.reshape(IN_COLS, D)
    z64 = jnp.zeros((64, D), BF)
    w2p = jnp.concatenate([_blocks_to_cols(g_small[:, 0:64]), z64], axis=0)
    a2p = jnp.concatenate([z64, _blocks_to_cols(g_small[:, 64:128])], axis=0)
    g2f = _blocks_to_cols(g_small[:, 128:256])
    conv_bits = jnp.pad(lax.bitcast_convert_type(conv_w[0], BF).reshape(3, 2 * 704), ((0, 29), (0, 0)))
    late = [w_up[0].T.astype(BF)] + [w[k][0].astype(BF) for k in _BIG[2:]] + [conv_bits]
    late_gather = _Exchange("gather2", late)
    r_k = rwkv_r_k.reshape(1, D)

    (xn,), _ = _stage_fwd(st["pre1"], t, [attn_pre_norm], [x2])
    z = _mm("in_proj", xn, fw_in_t, "nt", F32, tm=512, tn=4736, b_outer=True)
    mix_par = [hgrn_lb, hgrn_gnorm, rwkv_mu, rwkv_w0, w2p, rwkv_a0, a2p, g2f, rwkv_k_k, rwkv_k_a,
               rwkv_ln_w, rwkv_ln_b, r_k]
    mix_in = [z]
    (o_a, o_b), mix_saved = _stage_fwd(st["mixers"], t, mix_par, mix_in, hook=late_gather)
    gl = [lax.dynamic_update_slice(g, own[None], (me, 0, 0)) for g, own in zip(late_gather.results, late)]
    fw_up_t = gl[0].reshape(2 * DFF, D)
    fw_down = gl[1].reshape(DFF, D)
    fw_a, fw_b, fw_out = (g.reshape(D, D) for g in gl[2:5])
    conv_full = _blocks_to_cols(lax.bitcast_convert_type(gl[5][:, :3].reshape(N_DEV, 3, 704, 2), F32))
    y_a = _mm("branch_a", o_a, fw_a, "nn", BF)
    y_b = _mm("branch_b", o_b, fw_b, "nn", BF)
    (merged,), _ = _stage_fwd(st["merge"], t, [], [z, z, y_a, y_b])
    mix = _mm("out_proj", merged, fw_out, "nn")
    (h1, xn2), _ = _stage_fwd(st["post1"], t, [attn_post_norm, ffn_pre_norm], [x2, mix])
    hu = _mm("up_proj", xn2, fw_up_t, "nt", F32, tm=1024, tn=1408)
    conv_par = [conv_full, conv_b]
    (act,), conv_saved = _stage_fwd(st["conv"], t, conv_par, [hu])
    ff = _mm("down_proj", act, fw_down, "nn")

    loss_acc, d_ffn_post, dh1, dff = _loss_stage(t, ffn_post_norm, h1, ff, tgt)
    dact = _mm("d_act", dff, fw_down, "nt", BF, tm=1024, tn=1408)
    dw_down = _mm("dw_down", act, dff, "tn", BF, tm=1408, tn=512)
    (dcw, dcb), (dhu,) = _stage_bwd(st["conv"], t, conv_par, [hu], conv_saved, [[dact]], [BF])
    dxn2 = _mm("d_xn2", dhu, fw_up_t, "nn", F32, tm=1024, tn=1024)
    dw_up_t = _mm("dw_up", dhu, xn2, "tn", BF, tm=1408, tn=1024)
    (d_post, d_pre2), (dx_a, dmix) = _stage_bwd(st["post1"], t, [attn_post_norm, ffn_pre_norm], [x2, mix], [],
                                                 [[dh1], [dxn2]], [F32, BF])
    dmerged = _mm("d_merged", dmix, fw_out, "nt", BF)
    dw_out = _mm("dw_out", merged, dmix, "tn", BF)
    _, (dga, dgb, dy_a, dy_b) = _stage_bwd(st["merge"], t, [], [z, z, y_a, y_b], [], [[dmerged]], [BF, BF, BF, BF])
    do_a = _mm("d_oa", dy_a, fw_a, "nt", BF)
    dw_a = _mm("dw_a", o_a, dy_a, "tn", BF)
    do_b = _mm("d_ob", dy_b, fw_b, "nt", BF)
    dw_b = _mm("dw_b", o_b, dy_b, "tn", BF)
    early = [dw_up_t.reshape(N_DEV, 704, D), dw_down.reshape(N_DEV, 352, D), dw_a.reshape(N_DEV, 128, D),
             dw_b.reshape(N_DEV, 128, D), dw_out.reshape(N_DEV, 128, D), _cols_to_blocks(dcw.astype(BF), 704)]
    early_scatter = _Exchange("scatter", early)
    mix_dp, dz_hr = _stage_bwd(st["mixers"], t, mix_par, mix_in, mix_saved, [[do_a], [do_b]], [BF],
                               hook=early_scatter)
    d_lb, d_gn, d_mu, d_w0, d_w2p, d_a0, d_a2p, d_g2, d_kk, d_ka, d_lnw, d_lnb, d_rk = mix_dp
    dz = dz_hr + [dga, dgb]
    dw_in_t = _mm_cols_tn("dw_in", dz, xn, BF, 256)

    ax, ay, ac = lax.axis_index("x"), lax.axis_index("y"), lax.axis_index("c")
    idx4 = jnp.stack([4 * cx + 2 * cy + ac for cx, cy in ((ax, ay), (1 - ax, ay), (ax, 1 - ay), (1 - ax, 1 - ay))])
    idx4 = idx4.astype(jnp.int32)
    idx_me, idx_0 = idx4[0:1], jnp.zeros((1,), jnp.int32)
    d_small = jnp.concatenate([d_w2p[:64], d_a2p[64:], d_g2], axis=0).astype(BF)
    g8s = [dw_in_t.reshape(N_DEV, 1184, D), _cols_to_blocks(d_small, LANES)]
    recv4s = _reduce_pair(g8s)
    sums = [_pair_sum("pair_sum_" + n, idx4, g, r) for n, g, r in zip(("w_in", "small"), g8s, recv4s)]
    swap_ssem, swap_rsem, swap_srcs, swap_lands, token = _chip_swap_start([s[1] for s in sums])
    dxn = _mm_cols_nn("d_xn", dz, fw_in_t, BF, 512, token=token)
    (d_pre1,), (dx,) = _stage_bwd(st["pre1_res"], t, [attn_pre_norm], [x2], [], [[dxn], [dx_a]], [F32])
    grad_x = dx.reshape(x.shape)

    rg = dict(attn_pre_norm=d_pre1, hgrn_lb=d_lb, hgrn_gnorm=d_gn, rwkv_mu=d_mu, rwkv_w0=d_w0, rwkv_a0=d_a0,
              rwkv_k_k=d_kk, rwkv_k_a=d_ka, rwkv_r_k=d_rk, rwkv_ln_w=d_lnw, rwkv_ln_b=d_lnb, attn_post_norm=d_post,
              ffn_pre_norm=d_pre2, conv_b=dcb, ffn_post_norm=d_ffn_post)
    g8 = _all_gather_small("gather_small_grads", _pack_replicated(rg, loss_acc))
    rnames = [n for n, _ in REPL]
    flat = lambda src: [src[n].reshape(1, D) if n == "rwkv_r_k" else src[n] for n in rnames]
    rp_out, loss_row = _adam_replicated(g8, flat(w), flat(mo), flat(vo))
    loss = loss_row[0, 0]
    recv3s = _chip_swap_wait(swap_ssem, swap_rsem, swap_srcs, swap_lands, rp_out[0]["attn_pre_norm"])
    for kind in range(4):
        rp_out[kind]["rwkv_r_k"] = rp_out[kind]["rwkv_r_k"].reshape(rwkv_r_k.shape)

    def small_of(src):
        return jnp.concatenate([src["rwkv_w2"][0], src["rwkv_a2"][0], src["rwkv_g2"][0]], axis=0)

    sh_out = [dict() for _ in range(4)]
    g_in = _sum_partials("sum_w_in", idx_0, sums[0][0][None], recv3s[0]).T
    res = _adam_sharded("adam_w_in", idx_0, g_in[None], None, *[src["w_in"][0] for src in (w, mo, vo)])
    res_s = _adam_sharded("adam_small", idx_0, sums[1][0][None], recv3s[1], *[small_of(src) for src in (w, mo, vo)])
    for kind in range(4):
        sh_out[kind]["w_in"] = res[kind][None]
        sh_out[kind]["rwkv_w2"] = res_s[kind][0:64][None]
        sh_out[kind]["rwkv_a2"] = res_s[kind][64:128][None]
        sh_out[kind]["rwkv_g2"] = res_s[kind][128:256][None]
    for n, own, recv in zip(_BIG[1:] + ("conv_w",), early, early_scatter.results):
        if n == "w_up":
            g_up = _sum_partials("sum_w_up", idx_me, own, recv).T
            res = _adam_sharded("adam_" + n, idx_0, g_up[None], None, *[src[n][0] for src in (w, mo, vo)])
        else:
            res = _adam_sharded("adam_" + n, idx_me, own, recv, *[src[n][0] for src in (w, mo, vo)])
        for kind in range(4):
            sh_out[kind][n] = res[kind][None]

    outs = [loss, grad_x]
    for kind in range(4):
        for name in _WEIGHTS:
            outs.append(sh_out[kind][name] if name in sh_out[kind] else rp_out[kind][name])
    return tuple(outs)
```

```python
import functools

import jax
import jax.numpy as jnp
from jax import lax
from jax.experimental import pallas as pl
from jax.experimental.pallas import tpu as pltpu

F32 = jnp.float32
BF = jnp.bfloat16
MESH = pl.DeviceIdType.MESH

D = 1024
HG_HEADS = 8
HG_K = 128
HG_CHUNK = 32
HG_SCALE = HG_K ** -0.5
HG_PER_STEP = 8
RW_HEADS = 16
RW_N = 64
RW_CHUNK = 64
RW_PAIRS_PER_STEP = 8
DFF = 2816
IN_COLS = 9472
RW_COLS = 3328
EPS = 1e-6
GN_EPS = 1e-5 * RW_N
ADAM_LR = 0.001
ADAM_B1 = 0.9
ADAM_B2 = 0.999
ADAM_EPS = 1e-08
ADAM_WD = 0.01
ADAM_STEP = 10
N_DEV = 8
LANES = 128
VMEM_LIMIT = 56 * 1024 * 1024
TILE_BYTES = 1280 * 1024

REPL = (("attn_pre_norm", 1024), ("hgrn_lb", 1024), ("hgrn_gnorm", 1024), ("rwkv_mu", 3328), ("rwkv_w0", 1024),
        ("rwkv_a0", 1024), ("rwkv_k_k", 1024), ("rwkv_k_a", 1024), ("rwkv_r_k", 1024), ("rwkv_ln_w", 1024),
        ("rwkv_ln_b", 1024), ("attn_post_norm", 1024), ("ffn_pre_norm", 1024), ("conv_b", 5632), ("ffn_post_norm", 1024))
REPL_ROWS = {"hgrn_lb": 2}
REPL_TOTAL = 32


def _cparams(sem=None, **kw):
    return pltpu.CompilerParams(dimension_semantics=sem, vmem_limit_bytes=VMEM_LIMIT, **kw)


_DN = {"nn": ((1,), (0,)), "nt": ((1,), (1,)), "tn": ((0,), (0,))}


def _raw_dot(a, b, mode):
    return lax.dot_general(a.astype(BF), b.astype(BF), (_DN[mode], ((), ())), preferred_element_type=F32)


@functools.partial(jax.custom_vjp, nondiff_argnums=(2,))
def _dot(a, b, mode):
    return _raw_dot(a, b, mode)


def _dot_fwd(a, b, mode):
    return _raw_dot(a, b, mode), (a, b)


def _dot_bwd(mode, res, g):
    a, b = res
    if mode == "nn":
        return _dot(g, b, "nt"), _dot(a, g, "tn")
    if mode == "nt":
        return _dot(g, b, "nn"), _dot(g, a, "tn")
    return _dot(b, g, "nt"), _dot(a, g, "nn")


_dot.defvjp(_dot_fwd, _dot_bwd)


def _bf_pieces(x, n):
    out, r = [], x
    for i in range(n):
        p = r.astype(BF)
        out.append(p)
        if i + 1 < n:
            r = r - p.astype(F32)
    return out


def _raw_split_dot(x, e, mode, n, x_left):
    eb = e.astype(BF)
    acc = None
    for p in _bf_pieces(x, n):
        ops = (p, eb) if x_left else (eb, p)
        t = lax.dot_general(*ops, (_DN[mode], ((), ())), preferred_element_type=F32)
        acc = t if acc is None else acc + t
    return acc


def _raw_headsum(x):
    t = x.shape[0]
    i = lax.broadcasted_iota(jnp.int32, (LANES, LANES), 0)
    j = lax.broadcasted_iota(jnp.int32, (LANES, LANES), 1)
    same = jnp.where((i >= RW_N) == (j >= RW_N), 1.0, 0.0).astype(F32)
    groups = x.shape[1] // LANES
    rows = jnp.concatenate([x[:, q * LANES:(q + 1) * LANES] for q in range(groups)], axis=0)
    s = _raw_split_dot(rows, same, "nn", 2, True)
    return jnp.concatenate([s[q * t:(q + 1) * t] for q in range(groups)], axis=1)


@jax.custom_vjp
def _headsum(x):
    return _raw_headsum(x)


def _headsum_fwd(x):
    return _raw_headsum(x), None


def _headsum_bwd(_, g):
    return (_raw_headsum(g),)


_headsum.defvjp(_headsum_fwd, _headsum_bwd)


@functools.partial(jax.custom_vjp, nondiff_argnums=(2,))
def _tdot(tri, x, n):
    return _raw_split_dot(x, tri, "nn", n, False)


def _tdot_fwd(tri, x, n):
    return _raw_split_dot(x, tri, "nn", n, False), tri


def _tdot_bwd(n, tri, g):
    return jnp.zeros_like(tri), _raw_split_dot(g, tri, "tn", n, False)


_tdot.defvjp(_tdot_fwd, _tdot_bwd)


def _row(x, i):
    r = lax.broadcasted_iota(jnp.int32, x.shape, 0)
    return jnp.sum(jnp.where(r == i, x, 0.0), axis=0, keepdims=True)


def _shift_down(x, prev):
    t = x.shape[0]

    @jax.custom_vjp
    def sh(x, prev):
        r = lax.broadcasted_iota(jnp.int32, x.shape, 0)
        return jnp.where(r == 0, prev, pltpu.roll(x, 1, 0))

    def fwd(x, prev):
        return sh(x, prev), None

    def bwd(_, g):
        r = lax.broadcasted_iota(jnp.int32, g.shape, 0)
        dx = jnp.where(r == t - 1, 0.0, pltpu.roll(g, t - 1, 0))
        return dx, jnp.sum(jnp.where(r == 0, g, 0.0), axis=0, keepdims=True)

    sh.defvjp(fwd, bwd)
    return sh(x, prev)


def _sigmoid(x):
    return jax.nn.sigmoid(x)


def _silu(x):
    return x * jax.nn.sigmoid(x)


def _softplus(x):
    return jnp.maximum(x, 0.0) + jnp.log(1.0 + jnp.exp(-jnp.abs(x)))


def _rms(x, g):
    return (x * lax.rsqrt(jnp.mean(x * x, axis=-1, keepdims=True) + EPS)) * g


def _tril(c):
    r = lax.broadcasted_iota(jnp.int32, (c, c), 0)
    cc = lax.broadcasted_iota(jnp.int32, (c, c), 1)
    return cc <= r


def _f_pre1(ps, xs, cs):
    return [_rms(xs[0], ps[0])], []


def _f_pre1_residual(ps, xs, cs):
    return [_rms(xs[0], ps[0]), xs[0]], []


def _f_hgrn(ps, xs, cs):
    lbraw, gn = ps
    hq, hf, hi, hg = xs
    hd = range(HG_PER_STEP)
    st = [cs[0][p * HG_K:(p + 1) * HG_K] for p in hd]
    l0, l1 = _row(lbraw, 0), _row(lbraw, 1)
    m = jnp.maximum(l0, l1)
    e0, e1 = jnp.exp(l0 - m), jnp.exp(l1 - m)
    lb = e0 / (e0 + e1)
    q = _silu(hq) * HG_SCALE
    f = lb + (1.0 - lb) * _sigmoid(hf)
    kh = 1.0 - f
    gl = jnp.log(f)
    c = HG_CHUNK
    low = _tril(c)
    tri = jnp.where(low, 1.0, 0.0).astype(F32)
    outs = []
    for i in range(hq.shape[0] // c):
        rows = slice(i * c, (i + 1) * c)
        b = _tdot(tri, gl[rows], 3)
        bref = _row(b, c // 2 - 1)
        blast = _row(b, c - 1)
        qi = q[rows] * jnp.exp(b - bref)
        ki = kh[rows] * jnp.exp(bref - b)
        qd = q[rows] * jnp.exp(b)
        kd = kh[rows] * jnp.exp(blast - b)
        dec = jnp.exp(blast)
        sl = [slice(p * HG_K, (p + 1) * HG_K) for p in hd]
        sc = [jnp.where(low, _dot(qi[:, sl[p]], ki[:, sl[p]], "nt"), 0.0) for p in hd]
        o = [_dot(sc[p], hi[rows, sl[p]], "nn") + _dot(qd[:, sl[p]], st[p], "nt") for p in hd]
        u = [_dot(hi[rows, sl[p]], kd[:, sl[p]], "tn") for p in hd]
        st = [dec[:, sl[p]] * st[p] + u[p] for p in hd]
        outs.append(jnp.concatenate(o, axis=1) if len(o) > 1 else o[0])
    o = outs[0] if len(outs) == 1 else jnp.concatenate(outs, axis=0)
    on = []
    for p in hd:
        op = o[:, p * HG_K:(p + 1) * HG_K]
        on.append(op * lax.rsqrt(jnp.mean(op * op, axis=-1, keepdims=True) + EPS))
    o = jnp.concatenate(on, axis=1) if len(on) > 1 else on[0]
    o = o * gn
    return [o * _silu(hg)], [jnp.concatenate(st, axis=0) if len(st) > 1 else st[0]]


_RW_OFFS = (0, 1024, 2048, 3072, 3200, 3328)


def _f_rwpre(ps, xs, cs):
    mu, w0, w2p, a0, a2p, g2, k_k, k_a = ps
    (prev,) = cs
    t = xs[0].shape[0]
    zs = []
    for i, z in enumerate(xs):
        lo, hi = _RW_OFFS[i], _RW_OFFS[i + 1]
        zs.append(z + mu[:, lo:hi] * (_shift_down(z, prev[:, lo:hi]) - z))
    rr, kr, vr, wa, gz = zs
    w_log = -_softplus(-(w0 + _dot(jnp.tanh(wa), w2p, "nn"))) - 0.5
    lw = -jnp.exp(w_log)
    a = _sigmoid(a0 + _dot(wa, a2p, "nn"))
    g = _dot(_sigmoid(gz), g2, "nn")
    kkr = kr * k_k
    kk = kkr / jnp.maximum(jnp.sqrt(_headsum(kkr * kkr)), 1e-12)
    k2 = kr * (1.0 + (a - 1.0) * k_a)
    newprev = jnp.concatenate([_row(z, t - 1) for z in xs], axis=1)
    return [rr, lw, k2, vr, -kk, kk * a, g], [newprev]


def _raw_inverses(ls):
    n = ls[0].shape[0]
    r = lax.broadcasted_iota(jnp.int32, (n, n), 0)
    c = lax.broadcasted_iota(jnp.int32, (n, n), 1)
    eye = jnp.where(r == c, 1.0, 0.0).astype(F32)
    tinv = [eye + l for l in ls]
    pw = ls
    for _ in range(5):
        pw = [_raw_dot(p, p, "nn") for p in pw]
        tinv = [t + _raw_dot(t, p, "nn") for t, p in zip(tinv, pw)]
    return tinv


@jax.custom_vjp
def _unit_lower_inverses(ls):
    return _raw_inverses(ls)


def _inverses_fwd(ls):
    tinv = _raw_inverses(ls)
    return tinv, tinv


def _inverses_bwd(tinv, gs):
    return ([_raw_dot(_raw_dot(t, g, "tn"), t, "nt") for t, g in zip(tinv, gs)],)


_unit_lower_inverses.defvjp(_inverses_fwd, _inverses_bwd)


@jax.custom_vjp
def _known_inverses(ls, tinv):
    return tinv


def _known_fwd(ls, tinv):
    return tinv, tinv


def _known_bwd(tinv, gs):
    return [_raw_dot(_raw_dot(t, g, "tn"), t, "nt") for t, g in zip(tinv, gs)], [jnp.zeros_like(t) for t in tinv]


_known_inverses.defvjp(_known_fwd, _known_bwd)


@jax.custom_vjp
def _use_kept(computed, kept):
    return kept


def _use_kept_fwd(computed, kept):
    return kept, None


def _use_kept_bwd(_, g):
    return g, jax.tree.map(jnp.zeros_like, g)


_use_kept.defvjp(_use_kept_fwd, _use_kept_bwd)

RW_KEPT = 5


def _f_rwscan(ps, xs, cs, kept=None):
    state = cs[0]
    ys, keep = [], []
    n = 2 * RW_CHUNK
    per_chunk = RW_KEPT * RW_PAIRS_PER_STEP * n
    for i in range(xs[0].shape[0] // RW_CHUNK):
        known = None
        if kept is not None:
            known = [[kept[i * per_chunk + (q * RW_PAIRS_PER_STEP + p) * n:
                           i * per_chunk + (q * RW_PAIRS_PER_STEP + p + 1) * n] for p in range(RW_PAIRS_PER_STEP)]
                     for q in range(RW_KEPT)]
        y, state, mats = _rwkv_chunk([x[i * RW_CHUNK:(i + 1) * RW_CHUNK] for x in xs], state, known)
        ys.append(y)
        keep += [m for group in mats for m in group]
    return [ys[0] if len(ys) == 1 else jnp.concatenate(ys, axis=0)], [state], jnp.concatenate(keep, axis=0)


def _rwkv_chunk(xs, state, known=None):
    npair = RW_PAIRS_PER_STEP
    pr = range(npair)
    r, lw, k, v, av, bv = [[x[:, p * LANES:(p + 1) * LANES] for p in pr] for x in xs]
    sv = [state[p * LANES:(p + 1) * LANES] for p in pr]
    c = RW_CHUNK
    n = 2 * c
    tri = jnp.where(_tril(c), 1.0, 0.0).astype(F32)
    cl = [_tdot(tri, lw[p], 3) for p in pr]
    cl_last = [_row(cl[p], c - 1) for p in pr]
    lane = lax.broadcasted_iota(jnp.int32, (c, LANES), 1)
    h0 = lane < RW_N

    def stack(x):
        return jnp.concatenate([jnp.where(h0, x, 0.0), jnp.where(h0, 0.0, x)], axis=0)

    am = [stack(av[p] * jnp.exp(cl[p] - lw[p])) for p in pr]
    bm = [stack(bv[p] * jnp.exp(-cl[p])) for p in pr]
    km = [stack(k[p] * jnp.exp(-cl[p])) for p in pr]
    rm = [stack(r[p] * jnp.exp(cl[p])) for p in pr]
    vm = [stack(v[p]) for p in pr]
    rn = lax.broadcasted_iota(jnp.int32, (n, n), 0)
    cn = lax.broadcasted_iota(jnp.int32, (n, n), 1)
    blk = (rn >= c) == (cn >= c)
    strict = blk & (cn < rn)
    incl = blk & (cn <= rn)
    lab = [jnp.where(strict, _dot(am[p], bm[p], "nt"), 0.0) for p in pr]
    lak = [jnp.where(strict, _dot(am[p], km[p], "nt"), 0.0) for p in pr]
    wrb = [jnp.where(incl, _dot(rm[p], bm[p], "nt"), 0.0) for p in pr]
    wrk = [jnp.where(incl, _dot(rm[p], km[p], "nt"), 0.0) for p in pr]
    if known is None:
        tinv = _unit_lower_inverses(lab)
    else:
        tinv = _known_inverses(lab, known[0])
        lak, wrb, wrk = _use_kept(lak, known[1]), _use_kept(wrb, known[2]), _use_kept(wrk, known[3])
    rhs = [_dot(am[p], sv[p], "nt") + _dot(lak[p], vm[p], "nn") for p in pr]
    um = [_dot(tinv[p], rhs[p], "nn") for p in pr]
    if known is not None:
        um = _use_kept(um, known[4])
    ym = [_dot(rm[p], sv[p], "nt") + _dot(wrb[p], um[p], "nn") + _dot(wrk[p], vm[p], "nn") for p in pr]
    sn = [(sv[p] + _dot(um[p], bm[p], "tn") + _dot(vm[p], km[p], "tn")) * jnp.exp(cl_last[p]) for p in pr]
    ys = [ym[p][:c] + ym[p][c:] for p in pr]
    return jnp.concatenate(ys, axis=1), jnp.concatenate(sn, axis=0), [tinv, lak, wrb, wrk, um]


def _f_mixers(ps, xs, cs):
    return _mixers(ps, xs, cs, None)


def _f_mixers_kept(ps, xs, cs, kept):
    return _mixers(ps, xs, cs, kept[0])[:2]


def _mixers(ps, xs, cs, kept):
    oa, st = _f_hgrn(ps[:2], xs[:4], cs[:1])
    (r, lw, k, v, av, bv, g), prev = _f_rwpre(ps[2:10], xs[4:], cs[1:2])
    y, sv, keep = _f_rwscan([], [r, lw, k, v, av, bv], cs[2:], kept)
    ob, _ = _f_rwpost(ps[10:], y + [r, k, v, g], [])
    return oa + ob, st + prev + sv, [keep]


def _f_rwpost(ps, xs, cs):
    ln_w, ln_b, r_k = ps
    y, r, k, v, g = xs
    inv_n = 1.0 / RW_N
    yc = y - _headsum(y) * inv_n
    var = _headsum(yc * yc) * inv_n
    yn = yc * lax.rsqrt(var + GN_EPS)
    yn = yn * ln_w + ln_b
    bonus = _headsum(r * k * r_k) * v
    return [(yn + bonus) * g], []


def _f_merge(ps, xs, cs):
    ga, gb, ya, yb = xs
    return [_sigmoid(ga) * ya + _sigmoid(gb) * yb], []


def _f_post1(ps, xs, cs):
    x, mix = xs
    h1 = x + _rms(mix, ps[0])
    return [h1, _rms(h1, ps[1])], []


def _f_conv(ps, xs, cs):
    cw, cb = ps
    p1, p2 = cs
    w0, w1, w2 = _row(cw, 0), _row(cw, 1), _row(cw, 2)
    t = xs[0].shape[0]
    hc = []
    for i, x in enumerate(xs):
        sl = slice(i * DFF, (i + 1) * DFF)
        s1 = _shift_down(x, p1[:, sl])
        s2 = _shift_down(s1, p2[:, sl])
        hc.append(cb[:, sl] + w0[:, sl] * s2 + w1[:, sl] * s1 + w2[:, sl] * x)
    n1 = jnp.concatenate([_row(x, t - 1) for x in xs], axis=1)
    n2 = jnp.concatenate([_row(x, t - 2) for x in xs], axis=1)
    return [_silu(hc[0]) * hc[1]], [n1, n2]


class _Stage:
    def __init__(self, name, f, g, tm, par_per_g, in_pieces, in_offs, carry_shapes, out_pieces, out_dtypes,
                 kept_shapes=(), f_kept=None):
        self.name, self.f, self.g, self.tm = name, f, g, tm
        self.par_per_g, self.in_pieces, self.in_offs = par_per_g, in_pieces, in_offs
        self.carry_shapes, self.out_pieces, self.out_dtypes = carry_shapes, out_pieces, out_dtypes
        self.kept_shapes, self.f_kept = list(kept_shapes), f_kept


def _par_spec(arr, per_g, g):
    r, c = arr.shape
    if per_g:
        return pl.BlockSpec((r, c // g), lambda gi, ni: (0, gi))
    return pl.BlockSpec((r, c), lambda gi, ni: (0, 0))


def _row_spec(tm, width, off, n, rev):
    if rev:
        return pl.BlockSpec((tm, width), lambda gi, ni: (n - 1 - ni, off + gi))
    return pl.BlockSpec((tm, width), lambda gi, ni: (ni, off + gi))


def _carry_spec(shape, n, rev):
    if rev:
        return pl.BlockSpec((None, None) + shape, lambda gi, ni: (gi, n - 1 - ni, 0, 0))
    return pl.BlockSpec((None, None) + shape, lambda gi, ni: (gi, ni, 0, 0))


def _load_pieces(refs, pieces_list):
    out = []
    for ref, pieces in zip(refs, pieces_list):
        o = 0
        for w in pieces:
            out.append(ref[:, o:o + w].astype(F32))
            o += w
    return out


def _store_pieces(refs, pieces_list, vals):
    k = 0
    for ref, pieces in zip(refs, pieces_list):
        o = 0
        for w in pieces:
            ref[:, o:o + w] = vals[k].astype(ref.dtype)
            k += 1
            o += w


_ANY = pl.BlockSpec(memory_space=pl.ANY)


class _Exchange:
    def __init__(self, kind, arrs):
        self.kind, self.arrs, self.results = kind, list(arrs), None
        if kind == "scatter":
            self.out_shape = [jax.ShapeDtypeStruct((N_DEV - 1,) + a.shape[1:], a.dtype) for a in self.arrs]
        else:
            self.out_shape = [jax.ShapeDtypeStruct((N_DEV,) + a.shape, a.dtype) for a in self.arrs]
        self.nsem = (N_DEV if kind == "gather2" else N_DEV - 1) * len(self.arrs)

    def copies(self, in_refs, out_refs, ssem, rsem):
        x, y, c = lax.axis_index("x"), lax.axis_index("y"), lax.axis_index("c")
        me = 4 * x + 2 * y + c
        cps = []
        for a, (i_ref, o_ref) in enumerate(zip(in_refs, out_refs)):
            for j in range(1, N_DEV):
                px = 1 - x if j & 4 else x
                py = 1 - y if j & 2 else y
                pc = 1 - c if j & 1 else c
                if self.kind == "gather":
                    src, dst = i_ref, o_ref.at[me]
                else:
                    src, dst = i_ref.at[4 * px + 2 * py + pc], o_ref.at[j - 1]
                s = (N_DEV - 1) * a + j - 1
                cps.append(pltpu.make_async_remote_copy(src_ref=src, dst_ref=dst, send_sem=ssem.at[s],
                                                        recv_sem=rsem.at[s], device_id=(px, py, pc),
                                                        device_id_type=MESH))
        return cps

    def run(self, step, total, in_refs, out_refs, ssem, rsem):
        if self.kind == "gather2":
            return self.run_two_level(step, total, in_refs, out_refs, ssem, rsem)

        @pl.when(step == 0)
        def _():
            for cp in self.copies(in_refs, out_refs, ssem, rsem):
                cp.start()

        @pl.when(step == total - 1)
        def _():
            for cp in self.copies(in_refs, out_refs, ssem, rsem):
                cp.wait()

    def run_two_level(self, step, total, in_refs, out_refs, ssem, rsem):
        x, y, c = lax.axis_index("x"), lax.axis_index("y"), lax.axis_index("c")
        sibling, xn, yn = (x, y, 1 - c), (1 - x, y, c), (x, 1 - y, c)
        arrs = range(len(in_refs))
        ns = N_DEV

        def num(px, py, pc):
            return 4 * px + 2 * py + pc

        def copy(a, k, to, src, dst):
            return pltpu.make_async_remote_copy(src_ref=src, dst_ref=dst, send_sem=ssem.at[ns * a + k],
                                                recv_sem=rsem.at[ns * a + k], device_id=to, device_id_type=MESH)

        def blk(a, b):
            return out_refs[a].at[b]

        def half(a, b, second):
            h = self.arrs[a].shape[0] // 2
            return out_refs[a].at[b, pl.ds(h if second else 0, h)]

        bx, by, bd = num(1 - x, y, c), num(x, 1 - y, c), num(1 - x, 1 - y, c)

        def firsts(a):
            own = blk(a, num(x, y, c))
            return [copy(a, 0, sibling, in_refs[a], own), copy(a, 1, xn, in_refs[a], own),
                    copy(a, 2, yn, in_refs[a], own)]

        def seconds(a):
            return [copy(a, 3, yn, half(a, bx, False), half(a, bx, False)), copy(a, 5, sibling, blk(a, bx), blk(a, bx)),
                    copy(a, 4, xn, half(a, by, True), half(a, by, True)), copy(a, 6, sibling, blk(a, by), blk(a, by))]

        def third(a):
            return copy(a, 7, sibling, blk(a, bd), blk(a, bd))

        @pl.when(step == 0)
        def _():
            for a in arrs:
                for cp in firsts(a):
                    cp.start()

        @pl.when(step == total // 2)
        def _():
            for a in arrs:
                copy(a, 1, xn, blk(a, bx), blk(a, bx)).wait_recv()
                copy(a, 2, yn, blk(a, by), blk(a, by)).wait_recv()
                for cp in seconds(a):
                    cp.start()

        @pl.when(step == (4 * total) // 5)
        def _():
            for a in arrs:
                copy(a, 3, yn, half(a, bd, False), half(a, bd, False)).wait_recv()
                copy(a, 4, xn, half(a, bd, True), half(a, bd, True)).wait_recv()
                third(a).start()

        @pl.when(step == total - 1)
        def _():
            for a in arrs:
                for k, b in ((0, num(x, y, 1 - c)), (5, num(1 - x, y, 1 - c)), (6, num(x, 1 - y, 1 - c)),
                             (7, num(1 - x, 1 - y, 1 - c))):
                    copy(a, k, sibling, blk(a, b), blk(a, b)).wait_recv()
                for cp in firsts(a) + seconds(a) + [third(a)]:
                    cp.wait_send()


def _hook_specs(hook):
    if hook is None:
        return [], [], [], []
    na = len(hook.arrs)
    sems = [pltpu.SemaphoreType.DMA((hook.nsem,)), pltpu.SemaphoreType.DMA((hook.nsem,))]
    return [_ANY] * na, [_ANY] * na, hook.out_shape, sems


def _stage_fwd(st, t, params, inputs, hook=None):
    g, tm = st.g, min(st.tm, t)
    n = t // tm
    npar, nin, ncar, nout = len(params), len(inputs), len(st.carry_shapes), len(st.out_pieces)
    nk = len(st.kept_shapes)
    h_in, h_out, h_shape, h_sems = _hook_specs(hook)
    nh = len(h_in)

    def body(*refs):
        p_refs = refs[:npar]
        x_refs = refs[npar:npar + nin]
        hi_refs = refs[npar + nin:npar + nin + nh]
        o = npar + nin + nh
        o_refs = refs[o:o + nout]
        s_refs = refs[o + nout:o + nout + ncar]
        k_refs = refs[o + nout + ncar:o + nout + ncar + nk]
        o += nout + ncar + nk
        ho_refs = refs[o:o + nh]
        c_scr = refs[o + nh:o + nh + ncar]
        gi, ni = pl.program_id(0), pl.program_id(1)
        if hook is not None:
            step = gi * n + ni
            hook.run(step, g * n, hi_refs, ho_refs, *refs[-2:])

        @pl.when(ni == 0)
        def _():
            for c in c_scr:
                c[...] = jnp.zeros(c.shape, F32)

        ps = [r[...].astype(F32) for r in p_refs]
        xs = _load_pieces(x_refs, st.in_pieces)
        cs = [c[...] for c in c_scr]
        for s, c in zip(s_refs, cs):
            s[...] = c
        res = st.f(ps, xs, cs)
        outs, ncs = res[0], res[1]
        _store_pieces(o_refs, st.out_pieces, outs)
        for c, v in zip(c_scr, ncs):
            c[...] = v
        for kr, kv in zip(k_refs, res[2] if nk else []):
            kr[...] = kv.astype(kr.dtype)

    in_specs = [_par_spec(p, pg, g) for p, pg in zip(params, st.par_per_g)]
    in_specs += [_row_spec(tm, sum(pc), off, n, False) for pc, off in zip(st.in_pieces, st.in_offs)]
    out_specs = [_row_spec(tm, sum(pc), 0, n, False) for pc in st.out_pieces]
    out_specs += [_carry_spec(s, n, False) for s in st.carry_shapes]
    out_specs += [pl.BlockSpec(s, lambda gi, ni: (ni, 0)) for s in st.kept_shapes]
    out_shape = [jax.ShapeDtypeStruct((t, g * sum(pc)), dt) for pc, dt in zip(st.out_pieces, st.out_dtypes)]
    out_shape += [jax.ShapeDtypeStruct((g, n) + s, F32) for s in st.carry_shapes]
    out_shape += [jax.ShapeDtypeStruct((n * s[0], s[1]), BF) for s in st.kept_shapes]
    res = pl.pallas_call(
        body, name=st.name + "_fwd", grid=(g, n), in_specs=in_specs + h_in, out_specs=out_specs + h_out,
        out_shape=out_shape + h_shape,
        scratch_shapes=[pltpu.VMEM(s, F32) for s in st.carry_shapes] + h_sems,
        compiler_params=_cparams(("arbitrary", "arbitrary")),
    )(*params, *inputs, *(hook.arrs if hook else []))
    if hook is not None:
        hook.results = list(res[nout + ncar + nk:])
    return list(res[:nout]), list(res[nout:nout + ncar + nk])


def _stage_bwd(st, t, params, inputs, saved, douts, dx_dtypes, hook=None):
    g, tm = st.g, min(st.tm, t)
    n = t // tm
    npar, nin, ncar = len(params), len(inputs), len(st.carry_shapes)
    nk = len(st.kept_shapes)
    flat_d = [d for ds in douts for d in ds]
    nd = len(flat_d)
    dx_idx = [i for i, dt in enumerate(dx_dtypes) if dt is not None]
    h_in, h_out, h_shape, h_sems = _hook_specs(hook)
    nh = len(h_in)

    def body(*refs):
        p_refs = refs[:npar]
        x_refs = refs[npar:npar + nin]
        s_refs = refs[npar + nin:npar + nin + ncar]
        k_refs = refs[npar + nin + ncar:npar + nin + ncar + nk]
        o = npar + nin + ncar + nk
        d_refs = refs[o:o + nd]
        hi_refs = refs[o + nd:o + nd + nh]
        o += nd + nh
        dp_refs = refs[o:o + npar]
        dx_refs = refs[o + npar:o + npar + len(dx_idx)]
        ho_refs = refs[o + npar + len(dx_idx):o + npar + len(dx_idx) + nh]
        dc_scr = refs[o + npar + len(dx_idx) + nh:o + npar + len(dx_idx) + nh + ncar]
        gi, ni = pl.program_id(0), pl.program_id(1)
        if hook is not None:
            step = gi * n + ni
            hook.run(step, g * n, hi_refs, ho_refs, *refs[-2:])

        @pl.when(ni == 0)
        def _():
            for c in dc_scr:
                c[...] = jnp.zeros(c.shape, F32)

        ps = [r[...].astype(F32) for r in p_refs]
        xs = _load_pieces(x_refs, st.in_pieces)
        cs = [s[...] for s in s_refs]
        dys = []
        k = 0
        for ds, pieces in zip(douts, st.out_pieces):
            acc = _load_pieces([d_refs[k]], [pieces])
            for j in range(1, len(ds)):
                more = _load_pieces([d_refs[k + j]], [pieces])
                acc = [a + b for a, b in zip(acc, more)]
            dys += acc
            k += len(ds)
        if nk:
            kept = [r[...].astype(F32) for r in k_refs]
            _, vjp = jax.vjp(lambda p, x, c: st.f_kept(p, x, c, kept), ps, xs, cs)
        else:
            _, vjp = jax.vjp(st.f, ps, xs, cs)
        dps, dxs, dcs = vjp((dys, [c[...] for c in dc_scr]))
        k = 0
        per_in = []
        for pieces in st.in_pieces:
            per_in.append(dxs[k:k + len(pieces)])
            k += len(pieces)
        for ref, i in zip(dx_refs, dx_idx):
            _store_pieces([ref], [st.in_pieces[i]], per_in[i])
        for c, v in zip(dc_scr, dcs):
            c[...] = v
        for ref, dp, pg in zip(dp_refs, dps, st.par_per_g):
            first = (ni == 0) if pg else ((ni == 0) & (gi == 0))

            @pl.when(first)
            def _():
                ref[...] = jnp.zeros(ref.shape, F32)

            ref[...] += dp

    in_specs = [_par_spec(p, pg, g) for p, pg in zip(params, st.par_per_g)]
    in_specs += [_row_spec(tm, sum(pc), off, n, True) for pc, off in zip(st.in_pieces, st.in_offs)]
    in_specs += [_carry_spec(s, n, True) for s in st.carry_shapes]
    in_specs += [pl.BlockSpec(s, lambda gi, ni: (n - 1 - ni, 0)) for s in st.kept_shapes]
    for ds, pc in zip(douts, st.out_pieces):
        in_specs += [_row_spec(tm, sum(pc), 0, n, True) for _ in ds]
    out_specs = [_par_spec(p, pg, g) for p, pg in zip(params, st.par_per_g)]
    out_specs += [_row_spec(tm, sum(st.in_pieces[i]), 0, n, True) for i in dx_idx]
    out_shape = [jax.ShapeDtypeStruct(p.shape, F32) for p in params]
    out_shape += [jax.ShapeDtypeStruct((t, g * sum(st.in_pieces[i])), dx_dtypes[i]) for i in dx_idx]
    res = pl.pallas_call(
        body, name=st.name + "_bwd", grid=(g, n), in_specs=in_specs + h_in, out_specs=out_specs + h_out,
        out_shape=out_shape + h_shape,
        scratch_shapes=[pltpu.VMEM(s, F32) for s in st.carry_shapes] + h_sems,
        compiler_params=_cparams(("arbitrary", "arbitrary")),
    )(*params, *inputs, *saved, *flat_d, *(hook.arrs if hook else []))
    if hook is not None:
        hook.results = list(res[npar + len(dx_idx):])
    return list(res[:npar]), list(res[npar:npar + len(dx_idx)])


def _pick(n, cap):
    if n <= cap:
        return n
    best = LANES
    for k in range(1, n // LANES + 1):
        if (n // LANES) % k == 0 and k * LANES <= cap:
            best = k * LANES
    return best


def _mm(name, a, b, mode, out_dtype=F32, tm=1024, tn=512, b_outer=False, token=None):
    m = a.shape[1] if mode == "tn" else a.shape[0]
    k = a.shape[0] if mode == "tn" else a.shape[1]
    n = b.shape[0] if mode == "nt" else b.shape[1]
    tm, tn = _pick(m, tm), _pick(n, tn)
    if b_outer:
        grid = (n // tn, m // tm)
        ij = lambda p, q: (q, p)
    else:
        grid = (m // tm, n // tn)
        ij = lambda p, q: (p, q)
    extra = [] if token is None else [token]

    def body(*refs):
        a_ref, b_ref, o_ref = refs[0], refs[1], refs[-1]
        o_ref[...] = _raw_dot(a_ref[...], b_ref[...], mode).astype(o_ref.dtype)

    if mode == "tn":
        a_spec = pl.BlockSpec((k, tm), lambda p, q: (0, ij(p, q)[0]))
    else:
        a_spec = pl.BlockSpec((tm, k), lambda p, q: (ij(p, q)[0], 0))
    b_mode = dict(pipeline_mode=pl.Buffered(1)) if tn == n else {}
    if mode == "nt":
        b_spec = pl.BlockSpec((tn, k), lambda p, q: (ij(p, q)[1], 0), **b_mode)
    else:
        b_spec = pl.BlockSpec((k, tn), lambda p, q: (0, ij(p, q)[1]), **b_mode)
    return pl.pallas_call(
        body, name=name, grid=grid,
        in_specs=[a_spec, b_spec] + [pl.BlockSpec(e.shape, lambda p, q: (0, 0)) for e in extra],
        out_specs=pl.BlockSpec((tm, tn), lambda p, q: ij(p, q)),
        out_shape=jax.ShapeDtypeStruct((m, n), out_dtype),
        compiler_params=_cparams(("arbitrary", "arbitrary")),
    )(a, b, *extra)


def _mm_cols_nn(name, pieces, b, out_dtype, tm, token=None):
    m, n = pieces[0].shape[0], b.shape[1]
    tm = _pick(m, tm)
    offs = [sum(p.shape[1] for p in pieces[:i]) for i in range(len(pieces))]
    extra = [] if token is None else [token]
    na = len(pieces)

    def body(*refs):
        b_ref, o_ref = refs[na], refs[-1]
        acc = None
        for a_ref, off in zip(refs[:na], offs):
            t = _raw_dot(a_ref[...], b_ref[off:off + a_ref.shape[1], :], "nn")
            acc = t if acc is None else acc + t
        o_ref[...] = acc.astype(o_ref.dtype)

    return pl.pallas_call(
        body, name=name, grid=(m // tm,),
        in_specs=[pl.BlockSpec((tm, p.shape[1]), lambda i: (i, 0)) for p in pieces]
        + [pl.BlockSpec(b.shape, lambda i: (0, 0), pipeline_mode=pl.Buffered(1))]
        + [pl.BlockSpec(e.shape, lambda i: (0, 0)) for e in extra],
        out_specs=pl.BlockSpec((tm, n), lambda i: (i, 0)), out_shape=jax.ShapeDtypeStruct((m, n), out_dtype),
        compiler_params=_cparams(("arbitrary",)),
    )(*pieces, b, *extra)


def _mm_cols_tn(name, pieces, b, out_dtype, tm):
    k, n = b.shape
    counts = [p.shape[1] // tm for p in pieces]
    starts = [sum(counts[:i]) for i in range(len(pieces))]
    na = len(pieces)

    def body(*refs):
        b_ref, o_ref = refs[na], refs[-1]
        i = pl.program_id(0)
        for a_ref, s, c in zip(refs[:na], starts, counts):
            @pl.when((i >= s) & (i < s + c))
            def _():
                o_ref[...] = _raw_dot(a_ref[...], b_ref[...], "tn").astype(o_ref.dtype)

    def spec(s, c):
        return pl.BlockSpec((k, tm), lambda i: (0, jnp.clip(i - s, 0, c - 1)))

    return pl.pallas_call(
        body, name=name, grid=(sum(counts),),
        in_specs=[spec(s, c) for s, c in zip(starts, counts)]
        + [pl.BlockSpec(b.shape, lambda i: (0, 0), pipeline_mode=pl.Buffered(1))],
        out_specs=pl.BlockSpec((tm, n), lambda i: (i, 0)),
        out_shape=jax.ShapeDtypeStruct((sum(counts) * tm, n), out_dtype),
        compiler_params=_cparams(("arbitrary",)),
    )(*pieces, b)


def _norm_in_proj(x, g, w_t, tm, tn):
    t, k = x.shape
    n = w_t.shape[0]
    tm, tn = _pick(t, tm), _pick(n, tn)

    def body(x_ref, g_ref, w_ref, xn_ref, z_ref):
        xn = _rms(x_ref[...], g_ref[...]).astype(BF)
        xn_ref[...] = xn
        z_ref[...] = _raw_dot(xn, w_ref[...], "nt")

    xns, z = pl.pallas_call(
        body, name="in_proj", grid=(n // tn, t // tm),
        in_specs=[pl.BlockSpec((tm, k), lambda j, i: (i, 0)), pl.BlockSpec((1, k), lambda j, i: (0, 0)),
                  pl.BlockSpec((tn, k), lambda j, i: (j, 0))],
        out_specs=[pl.BlockSpec((None, tm, k), lambda j, i: (j, i, 0)), pl.BlockSpec((tm, tn), lambda j, i: (i, j))],
        out_shape=[jax.ShapeDtypeStruct((n // tn, t, k), BF), jax.ShapeDtypeStruct((t, n), F32)],
        compiler_params=_cparams(("arbitrary", "arbitrary")),
    )(x, g, w_t)
    return xns[0], z


def _merge_out_post(z, y_a, y_b, w_out, x, g_post, g_pre2, tm):
    t = x.shape[0]
    tm = _pick(t, tm)
    w = 256
    npc = D // w
    ga0, gb0 = (IN_COLS - 2 * D) // w, (IN_COLS - D) // w

    def body(*refs):
        ga_refs, gb_refs = refs[:npc], refs[npc:2 * npc]
        ya_ref, yb_ref, w_ref, x_ref, gp_ref, g2_ref, m_ref, mix_ref, h_ref, xn_ref = refs[2 * npc:]
        parts = []
        for p in range(npc):
            cols = slice(p * w, (p + 1) * w)
            parts.append(_sigmoid(ga_refs[p][...]) * ya_ref[:, cols].astype(F32)
                         + _sigmoid(gb_refs[p][...]) * yb_ref[:, cols].astype(F32))
        merged = jnp.concatenate(parts, axis=1).astype(BF)
        m_ref[...] = merged
        mix = _raw_dot(merged, w_ref[...], "nn")
        mix_ref[...] = mix
        h1 = x_ref[...] + _rms(mix, gp_ref[...])
        h_ref[...] = h1
        xn_ref[...] = _rms(h1, g2_ref[...]).astype(BF)

    row = pl.BlockSpec((tm, D), lambda i: (i, 0))
    one = pl.BlockSpec((1, D), lambda i: (0, 0))

    def gate(b0):
        return [pl.BlockSpec((tm, w), functools.partial(lambda i, b: (i, b), b=b0 + p)) for p in range(npc)]

    return pl.pallas_call(
        body, name="merge_out_post", grid=(t // tm,),
        in_specs=gate(ga0) + gate(gb0) + [row, row, pl.BlockSpec((D, D), lambda i: (0, 0), pipeline_mode=pl.Buffered(1)),
                                          row, one, one],
        out_specs=[row, row, row, row],
        out_shape=[jax.ShapeDtypeStruct((t, D), BF), jax.ShapeDtypeStruct((t, D), F32),
                   jax.ShapeDtypeStruct((t, D), F32), jax.ShapeDtypeStruct((t, D), BF)],
        compiler_params=_cparams(("arbitrary",)),
    )(*([z] * (2 * npc)), y_a, y_b, w_out, x, g_post, g_pre2)


def _down_loss(act, w_down, g_post, h1, tgt, tm):
    t, k = act.shape
    tm = _pick(t, tm)

    def body(a_ref, w_ref, g_ref, h_ref, t_ref, loss_ref, dg_ref, dh_ref, df_ref):
        ni = pl.program_id(0)
        ff = _raw_dot(a_ref[...], w_ref[...], "nn")
        target = t_ref[...]

        def lossf(g, h1, ff):
            e = h1 + _rms(ff, g) - target
            return 0.5 * jnp.sum(jnp.mean(e * e, axis=-1))

        l, (dg, dh, df) = jax.value_and_grad(lossf, argnums=(0, 1, 2))(g_ref[...], h_ref[...], ff)

        @pl.when(ni == 0)
        def _():
            loss_ref[...] = jnp.zeros(loss_ref.shape, F32)
            dg_ref[...] = jnp.zeros(dg_ref.shape, F32)

        loss_ref[...] += jnp.full(loss_ref.shape, l, F32)
        dg_ref[...] += dg
        dh_ref[...] = dh
        df_ref[...] = df.astype(df_ref.dtype)

    row = pl.BlockSpec((tm, D), lambda ni: (ni, 0))
    one = pl.BlockSpec((1, D), lambda ni: (0, 0))
    return pl.pallas_call(
        body, name="down_loss", grid=(t // tm,),
        in_specs=[pl.BlockSpec((tm, k), lambda ni: (ni, 0)),
                  pl.BlockSpec((k, D), lambda ni: (0, 0), pipeline_mode=pl.Buffered(1)), one, row, row],
        out_specs=[pl.BlockSpec((1, LANES), lambda ni: (0, 0)), one, row, row],
        out_shape=[jax.ShapeDtypeStruct((1, LANES), F32), jax.ShapeDtypeStruct((1, D), F32),
                   jax.ShapeDtypeStruct((t, D), F32), jax.ShapeDtypeStruct((t, D), BF)],
        compiler_params=_cparams(("arbitrary",)),
    )(act, w_down, g_post, h1, tgt)


_ANY = pl.BlockSpec(memory_space=pl.ANY)


def _all_gather(name, blks):
    na = len(blks)
    ns = 8

    def body(*refs):
        x_refs, out_refs = refs[:na], refs[na:2 * na]
        send_sems, recv_sems, local_sems = refs[2 * na:]
        x, y, cc = lax.axis_index("x"), lax.axis_index("y"), lax.axis_index("c")
        sibling, xn, yn = (x, y, 1 - cc), (1 - x, y, cc), (x, 1 - y, cc)

        def num(px, py, pc):
            return 4 * px + 2 * py + pc

        def copy(a, k, to, src, dst):
            return pltpu.make_async_remote_copy(src_ref=src, dst_ref=dst, send_sem=send_sems.at[ns * a + k],
                                                recv_sem=recv_sems.at[ns * a + k], device_id=to, device_id_type=MESH)

        def halves(a, blk):
            h = blks[a].shape[0] // 2
            return out_refs[a].at[blk, pl.ds(0, h)], out_refs[a].at[blk, pl.ds(h, h)]

        mine, sends = [], []
        for a in range(na):
            o = out_refs[a]
            m = pltpu.make_async_copy(x_refs[a], o.at[num(x, y, cc)], local_sems.at[a])
            m.start()
            mine.append(m)
            own = o.at[num(x, y, cc)]
            sends.append([copy(a, 0, sibling, x_refs[a], own), copy(a, 1, xn, x_refs[a], own),
                          copy(a, 2, yn, x_refs[a], own)])
            for cp in sends[a]:
                cp.start()
        for a in range(na):
            o = out_refs[a]
            bx, by, bd = num(1 - x, y, cc), num(x, 1 - y, cc), num(1 - x, 1 - y, cc)
            copy(a, 1, xn, o.at[bx], o.at[bx]).wait_recv()
            more = [copy(a, 3, yn, halves(a, bx)[0], halves(a, bx)[0]), copy(a, 5, sibling, o.at[bx], o.at[bx])]
            for cp in more:
                cp.start()
            sends[a] += more
        for a in range(na):
            o = out_refs[a]
            bx, by, bd = num(1 - x, y, cc), num(x, 1 - y, cc), num(1 - x, 1 - y, cc)
            copy(a, 2, yn, o.at[by], o.at[by]).wait_recv()
            more = [copy(a, 4, xn, halves(a, by)[1], halves(a, by)[1]), copy(a, 6, sibling, o.at[by], o.at[by])]
            for cp in more:
                cp.start()
            sends[a] += more
        for a in range(na):
            o = out_refs[a]
            bd = num(1 - x, 1 - y, cc)
            copy(a, 3, yn, halves(a, bd)[0], halves(a, bd)[0]).wait_recv()
            copy(a, 4, xn, halves(a, bd)[1], halves(a, bd)[1]).wait_recv()
            fw = copy(a, 7, sibling, o.at[bd], o.at[bd])
            fw.start()
            sends[a].append(fw)
        for a in range(na):
            o = out_refs[a]
            for k, blk in ((0, num(x, y, 1 - cc)), (5, num(1 - x, y, 1 - cc)), (6, num(x, 1 - y, 1 - cc)),
                           (7, num(1 - x, 1 - y, 1 - cc))):
                copy(a, k, sibling, o.at[blk], o.at[blk]).wait_recv()
            for cp in sends[a]:
                cp.wait_send()
        for m in mine:
            m.wait()

    res = pl.pallas_call(
        body, name=name, in_specs=[_ANY] * na, out_specs=[_ANY] * na,
        out_shape=[jax.ShapeDtypeStruct((N_DEV,) + b.shape, b.dtype) for b in blks],
        scratch_shapes=[pltpu.SemaphoreType.DMA((ns * na,)), pltpu.SemaphoreType.DMA((ns * na,)),
                        pltpu.SemaphoreType.DMA((na,))],
    )(*blks)
    return list(res)


def _all_gather_small(name, blk):
    def body(x_ref, out_ref, ssem, rsem, lsem):
        x, y, c = lax.axis_index("x"), lax.axis_index("y"), lax.axis_index("c")
        me = 4 * x + 2 * y + c
        mine = pltpu.make_async_copy(x_ref, out_ref.at[me], lsem)
        mine.start()
        cps = []
        for j in range(1, N_DEV):
            px = 1 - x if j & 4 else x
            py = 1 - y if j & 2 else y
            pc = 1 - c if j & 1 else c
            cps.append(pltpu.make_async_remote_copy(src_ref=x_ref, dst_ref=out_ref.at[me], send_sem=ssem.at[j - 1],
                                                    recv_sem=rsem.at[j - 1], device_id=(px, py, pc),
                                                    device_id_type=MESH))
        for cp in cps:
            cp.start()
        for cp in cps:
            cp.wait()
        mine.wait()

    return pl.pallas_call(
        body, name=name, in_specs=[_ANY], out_specs=_ANY,
        out_shape=jax.ShapeDtypeStruct((N_DEV,) + blk.shape, blk.dtype),
        scratch_shapes=[pltpu.SemaphoreType.DMA((N_DEV - 1,)), pltpu.SemaphoreType.DMA((N_DEV - 1,)),
                        pltpu.SemaphoreType.DMA],
    )(blk)


def _reduce_pair(g8s):
    na = len(g8s)

    def body(*refs):
        g_refs, recv_refs = refs[:na], refs[na:2 * na]
        ssem, rsem = refs[2 * na:]
        x, y, cc = lax.axis_index("x"), lax.axis_index("y"), lax.axis_index("c")
        chips = [(x, y), (1 - x, y), (x, 1 - y), (1 - x, 1 - y)]
        sib = (x, y, 1 - cc)
        for a in range(na):
            for k, (cx, cy) in enumerate(chips):
                pltpu.make_async_remote_copy(
                    src_ref=g_refs[a].at[4 * cx + 2 * cy + 1 - cc], dst_ref=recv_refs[a].at[k],
                    send_sem=ssem.at[a], recv_sem=rsem.at[a], device_id=sib, device_id_type=MESH).start()
        for a in range(na):
            pltpu.make_async_remote_copy(src_ref=recv_refs[a], dst_ref=recv_refs[a], send_sem=ssem.at[a],
                                         recv_sem=rsem.at[a], device_id=sib, device_id_type=MESH).wait()

    res = pl.pallas_call(
        body, name="reduce_pair", in_specs=[_ANY] * na, out_specs=[_ANY] * na,
        out_shape=[jax.ShapeDtypeStruct((4,) + g.shape[1:], g.dtype) for g in g8s],
        scratch_shapes=[pltpu.SemaphoreType.DMA((na,)), pltpu.SemaphoreType.DMA((na,))],
    )(*g8s)
    return list(res)


_HBM = pl.BlockSpec(memory_space=pltpu.HBM)
_SEM = pl.BlockSpec(memory_space=pltpu.SEMAPHORE)
_EFFECT = pltpu.SideEffectType.DATAFLOW_SIDE_EFFECTING


def _chip_swap_copies(s_refs, land_refs, ssem, rsem):
    x, y, c = lax.axis_index("x"), lax.axis_index("y"), lax.axis_index("c")
    targets = [(1 - x, y, c), (x, 1 - y, c), (1 - x, 1 - y, c)]
    return [pltpu.make_async_remote_copy(src_ref=s.at[k], dst_ref=d.at[k], send_sem=ssem.at[3 * a + k],
                                         recv_sem=rsem.at[3 * a + k], device_id=targets[k], device_id_type=MESH)
            for a, (s, d) in enumerate(zip(s_refs, land_refs)) for k in range(3)]


def _chip_swap_start(sends):
    na = len(sends)

    def body(*refs):
        cps = _chip_swap_copies(refs[:na], refs[na:2 * na], refs[2 * na], refs[2 * na + 1])
        for cp in cps:
            cp.start()
        token = refs[-1]
        token[...] = jnp.zeros(token.shape, token.dtype)

    bufs = [pltpu.HBM(s.shape, s.dtype) for s in sends]
    res = pl.pallas_call(
        body, name="chip_swap_start",
        out_shape=[pltpu.SemaphoreType.DMA((3 * na,)), pltpu.SemaphoreType.DMA((3 * na,))] + bufs + bufs
        + [jax.ShapeDtypeStruct((8, LANES), F32)],
        in_specs=[_HBM] * (2 * na), out_specs=[_SEM, _SEM] + [_HBM] * (2 * na) + [pl.BlockSpec(memory_space=pltpu.VMEM)],
        input_output_aliases={i: 2 + i for i in range(2 * na)},
        compiler_params=pltpu.CompilerParams(has_side_effects=_EFFECT),
    )(*[pltpu.with_memory_space_constraint(s, pltpu.HBM) for s in sends],
      *[pltpu.with_memory_space_constraint(lax.empty(s.shape, s.dtype), pltpu.HBM) for s in sends])
    return res[0], res[1], list(res[2:2 + na]), list(res[2 + na:2 + 2 * na]), res[-1]


def _chip_swap_wait(ssem, rsem, srcs, lands, after):
    na = len(srcs)

    def body(*refs):
        cps = _chip_swap_copies(refs[:na], refs[na:2 * na], refs[2 * na], refs[2 * na + 1])
        for cp in cps:
            cp.wait_send()
            cp.wait_recv()

    bufs = [pltpu.HBM(s.shape, s.dtype) for s in srcs]
    res = pl.pallas_call(
        body, name="chip_swap_wait", out_shape=bufs + bufs,
        in_specs=[_HBM] * (2 * na) + [_SEM, _SEM, _ANY], out_specs=[_HBM] * (2 * na),
        input_output_aliases={i: i for i in range(2 * na)},
        compiler_params=pltpu.CompilerParams(has_side_effects=_EFFECT),
    )(*srcs, *lands, ssem, rsem, after)
    return list(res[na:])


def _pick_rows(r, c, budget=TILE_BYTES):
    if r * c * 4 <= budget or r % 16:
        return r
    best = 16
    for tr in range(16, r, 16):
        if r % tr == 0 and tr * c * 4 <= budget:
            best = tr
    return best


def _pair_sum(name, idx4, g8, recv4):
    _, r, c = g8.shape
    tr = _pick_rows(r, c, 2 * TILE_BYTES)

    def body(idx_ref, a_ref, b_ref, o0_ref, o3_ref):
        k = pl.program_id(1)
        s = a_ref[...].astype(F32) + b_ref[...].astype(F32)

        @pl.when(k == 0)
        def _():
            o0_ref[...] = s

        @pl.when(k > 0)
        def _():
            o3_ref[...] = s.astype(BF)

    spec = pltpu.PrefetchScalarGridSpec(
        num_scalar_prefetch=1, grid=(r // tr, 4),
        in_specs=[pl.BlockSpec((None, tr, c), lambda i, k, idx: (idx[k], i, 0)),
                  pl.BlockSpec((None, tr, c), lambda i, k, idx: (k, i, 0))],
        out_specs=[pl.BlockSpec((tr, c), lambda i, k, idx: (i, 0)),
                   pl.BlockSpec((None, tr, c), lambda i, k, idx: (jnp.maximum(k - 1, 0), i, 0))])
    return pl.pallas_call(
        body, name=name, grid_spec=spec,
        out_shape=[jax.ShapeDtypeStruct((r, c), F32), jax.ShapeDtypeStruct((3, r, c), BF)],
        compiler_params=_cparams(("arbitrary", "arbitrary")),
    )(idx4, g8, recv4)


def _adamw(w, g, m, v):
    m = ADAM_B1 * m + (1.0 - ADAM_B1) * g
    v = ADAM_B2 * v + (1.0 - ADAM_B2) * jnp.square(g)
    m_hat = m / (1.0 - ADAM_B1 ** ADAM_STEP)
    v_hat = v / (1.0 - ADAM_B2 ** ADAM_STEP)
    delta = -ADAM_LR * (m_hat / (jnp.sqrt(v_hat) + ADAM_EPS) + ADAM_WD * w)
    return delta, m, v


def _sum_partials(name, idx1, own, recv):
    _, r, c = own.shape
    tr = _pick_rows(r, c, 2 * TILE_BYTES)
    nj = recv.shape[0]

    def body(idx_ref, p_ref, r_ref, g_out):
        g = p_ref[...].astype(F32)
        for k in range(nj):
            g = g + r_ref[k].astype(F32)
        g_out[...] = g

    row = pl.BlockSpec((tr, c), lambda i, idx: (i, 0))
    spec = pltpu.PrefetchScalarGridSpec(
        num_scalar_prefetch=1, grid=(r // tr,),
        in_specs=[pl.BlockSpec((None, tr, c), lambda i, idx: (idx[0], i, 0)),
                  pl.BlockSpec((nj, tr, c), lambda i, idx: (0, i, 0))],
        out_specs=row)
    return pl.pallas_call(body, name=name, grid_spec=spec, out_shape=jax.ShapeDtypeStruct((r, c), F32),
                          compiler_params=_cparams(("arbitrary",)))(idx1, own, recv)


def _adam_sharded(name, idx1, own, recv, w, m, v):
    r, c = w.shape
    tr = _pick_rows(r, c)
    nj = 0 if recv is None else recv.shape[0]
    if recv is None:
        recv = jnp.zeros((1, 8, LANES), BF)

    def body(idx_ref, p_ref, r_ref, w_ref, m_ref, v_ref, g_out, d_out, m_out, v_out):
        g = p_ref[...].astype(F32)
        for k in range(nj):
            g = g + r_ref[k].astype(F32)
        d, mn, vn = _adamw(w_ref[...], g, m_ref[...], v_ref[...])
        g_out[...] = g
        d_out[...] = d
        m_out[...] = mn
        v_out[...] = vn

    row = pl.BlockSpec((tr, c), lambda i, idx: (i, 0))
    if nj:
        recv_spec = pl.BlockSpec((nj, tr, c), lambda i, idx: (0, i, 0))
    else:
        recv_spec = pl.BlockSpec(recv.shape, lambda i, idx: (0, 0, 0))
    spec = pltpu.PrefetchScalarGridSpec(
        num_scalar_prefetch=1, grid=(r // tr,),
        in_specs=[pl.BlockSpec((None, tr, c), lambda i, idx: (idx[0], i, 0)), recv_spec, row, row, row],
        out_specs=[row] * 4)
    return pl.pallas_call(
        body, name=name, grid_spec=spec, out_shape=[jax.ShapeDtypeStruct((r, c), F32)] * 4,
        compiler_params=_cparams(("arbitrary",)),
    )(idx1, own, recv, w, m, v)


def _repl_rows():
    rows, r = {}, 0
    for name, cols in REPL:
        rows[name] = r
        r += REPL_ROWS.get(name, 1) * ((cols + D - 1) // D)
    return rows


LOSS_ROW = 24


def _pack_replicated(grads, loss_acc):
    rows = _repl_rows()
    names = [n for n, _ in REPL]

    def body(*refs):
        o_ref = refs[-1]
        o_ref[...] = jnp.zeros(o_ref.shape, F32)
        o_ref[LOSS_ROW:LOSS_ROW + 1, 0:LANES] = refs[-2][...]
        for name, ref in zip(names, refs[:-2]):
            r0 = rows[name]
            nr, nc = ref.shape
            if nc <= D:
                o_ref[r0:r0 + nr, 0:nc] = ref[...]
            else:
                for j in range((nc + D - 1) // D):
                    lo, hi = j * D, min(nc, (j + 1) * D)
                    o_ref[r0 + j:r0 + j + 1, 0:hi - lo] = ref[:, lo:hi]

    return pl.pallas_call(body, name="pack_replicated", out_shape=jax.ShapeDtypeStruct((REPL_TOTAL, D), F32),
                          compiler_params=_cparams())(*[grads[n] for n in names], loss_acc)


def _adam_replicated(g8, ws, ms, vs):
    rows = _repl_rows()
    names = [n for n, _ in REPL]
    np_ = len(names)

    def body(*refs):
        g_ref = refs[0]
        w_refs, m_refs, v_refs = refs[1:1 + np_], refs[1 + np_:1 + 2 * np_], refs[1 + 2 * np_:1 + 3 * np_]
        outs = refs[1 + 3 * np_:1 + 7 * np_]
        scr = refs[-1]
        g = g_ref[0]
        for k in range(1, N_DEV):
            g = g + g_ref[k]
        scr[...] = g
        refs[1 + 7 * np_][...] = scr[LOSS_ROW:LOSS_ROW + 1, 0:LANES]
        for i, name in enumerate(names):
            r0 = rows[name]
            nr, nc = w_refs[i].shape
            if nc <= D:
                gi = scr[r0:r0 + nr, 0:nc]
            else:
                parts = []
                for j in range((nc + D - 1) // D):
                    lo, hi = j * D, min(nc, (j + 1) * D)
                    parts.append(scr[r0 + j:r0 + j + 1, 0:hi - lo])
                gi = jnp.concatenate(parts, axis=1)
            d, mn, vn = _adamw(w_refs[i][...], gi, m_refs[i][...], v_refs[i][...])
            outs[i][...] = gi
            outs[np_ + i][...] = d
            outs[2 * np_ + i][...] = mn
            outs[3 * np_ + i][...] = vn

    shp = [jax.ShapeDtypeStruct(w.shape, F32) for w in ws]
    res = pl.pallas_call(body, name="adam_replicated", out_shape=shp * 4 + [jax.ShapeDtypeStruct((1, LANES), F32)],
                         scratch_shapes=[pltpu.VMEM((REPL_TOTAL, D), F32)], compiler_params=_cparams(),
                         )(g8, *ws, *ms, *vs)
    return [dict(zip(names, res[k * np_:(k + 1) * np_])) for k in range(4)], res[-1]


_WEIGHTS = ("attn_pre_norm", "w_in", "hgrn_lb", "hgrn_gnorm", "w_branch_a", "rwkv_mu", "rwkv_w0", "rwkv_w2",
            "rwkv_a0", "rwkv_a2", "rwkv_g2", "rwkv_k_k", "rwkv_k_a", "rwkv_r_k", "rwkv_ln_w", "rwkv_ln_b",
            "w_branch_b", "w_out", "attn_post_norm", "ffn_pre_norm", "w_up", "conv_w", "conv_b", "w_down",
            "ffn_post_norm")
_BIG = ("w_in", "w_up", "w_down", "w_branch_a", "w_branch_b", "w_out")


def _stages():
    one = [D]
    hw = HG_K * HG_PER_STEP
    rw = LANES * RW_PAIRS_PER_STEP
    return dict(
        pre1=_Stage("pre1", _f_pre1, 1, 256, [False], [one], [0], [], [one], [BF]),
        pre1_res=_Stage("pre1", _f_pre1_residual, 1, 256, [False], [one], [0], [], [one, one], [BF, F32]),
        mixers=_Stage("mixers", _f_mixers, 1, 2 * RW_CHUNK, [False] * 13, [[D] * 7 + [LANES, LANES]], [0],
                      [(hw, HG_K), (1, RW_COLS), (rw, LANES)], [one, one], [BF, BF],
                      kept_shapes=[(2 * RW_KEPT * RW_PAIRS_PER_STEP * 2 * RW_CHUNK, LANES)], f_kept=_f_mixers_kept),
        merge=_Stage("merge", _f_merge, 4, 512, [], [[256]] * 4, [29, 33, 0, 0], [], [[256]], [BF]),
        post1=_Stage("post1", _f_post1, 1, 256, [False, False], [one, one], [0, 0], [], [one, one], [F32, BF]),
        conv=_Stage("conv", _f_conv, 1, 128, [False, False], [[DFF, DFF]], [0], [(1, 2 * DFF), (1, 2 * DFF)],
                    [[DFF]], [BF]),
    )


def _cols_to_blocks(w, per):
    return w.reshape(w.shape[0], N_DEV, per).transpose(1, 0, 2)


def _blocks_to_cols(g):
    return g.transpose(1, 0, 2).reshape(g.shape[1], N_DEV * g.shape[2])


def kernel(x, attn_pre_norm, w_in, hgrn_lb, hgrn_gnorm, w_branch_a, rwkv_mu, rwkv_w0, rwkv_w2, rwkv_a0, rwkv_a2, rwkv_g2, rwkv_k_k, rwkv_k_a, rwkv_r_k, rwkv_ln_w, rwkv_ln_b, w_branch_b, w_out, attn_post_norm, ffn_pre_norm, w_up, conv_w, conv_b, w_down, ffn_post_norm, loss_target, m_attn_pre_norm, m_w_in, m_hgrn_lb, m_hgrn_gnorm, m_w_branch_a, m_rwkv_mu, m_rwkv_w0, m_rwkv_w2, m_rwkv_a0, m_rwkv_a2, m_rwkv_g2, m_rwkv_k_k, m_rwkv_k_a, m_rwkv_r_k, m_rwkv_ln_w, m_rwkv_ln_b, m_w_branch_b, m_w_out, m_attn_post_norm, m_ffn_pre_norm, m_w_up, m_conv_w, m_conv_b, m_w_down, m_ffn_post_norm, v_attn_pre_norm, v_w_in, v_hgrn_lb, v_hgrn_gnorm, v_w_branch_a, v_rwkv_mu, v_rwkv_w0, v_rwkv_w2, v_rwkv_a0, v_rwkv_a2, v_rwkv_g2, v_rwkv_k_k, v_rwkv_k_a, v_rwkv_r_k, v_rwkv_ln_w, v_rwkv_ln_b, v_w_branch_b, v_w_out, v_attn_post_norm, v_ffn_pre_norm, v_w_up, v_conv_w, v_conv_b, v_w_down, v_ffn_post_norm):
    w = dict(attn_pre_norm=attn_pre_norm, w_in=w_in, hgrn_lb=hgrn_lb, hgrn_gnorm=hgrn_gnorm, w_branch_a=w_branch_a, rwkv_mu=rwkv_mu, rwkv_w0=rwkv_w0, rwkv_w2=rwkv_w2, rwkv_a0=rwkv_a0, rwkv_a2=rwkv_a2, rwkv_g2=rwkv_g2, rwkv_k_k=rwkv_k_k, rwkv_k_a=rwkv_k_a, rwkv_r_k=rwkv_r_k, rwkv_ln_w=rwkv_ln_w, rwkv_ln_b=rwkv_ln_b, w_branch_b=w_branch_b, w_out=w_out, attn_post_norm=attn_post_norm, ffn_pre_norm=ffn_pre_norm, w_up=w_up, conv_w=conv_w, conv_b=conv_b, w_down=w_down, ffn_post_norm=ffn_post_norm)
    mo = dict(attn_pre_norm=m_attn_pre_norm, w_in=m_w_in, hgrn_lb=m_hgrn_lb, hgrn_gnorm=m_hgrn_gnorm, w_branch_a=m_w_branch_a, rwkv_mu=m_rwkv_mu, rwkv_w0=m_rwkv_w0, rwkv_w2=m_rwkv_w2, rwkv_a0=m_rwkv_a0, rwkv_a2=m_rwkv_a2, rwkv_g2=m_rwkv_g2, rwkv_k_k=m_rwkv_k_k, rwkv_k_a=m_rwkv_k_a, rwkv_r_k=m_rwkv_r_k, rwkv_ln_w=m_rwkv_ln_w, rwkv_ln_b=m_rwkv_ln_b, w_branch_b=m_w_branch_b, w_out=m_w_out, attn_post_norm=m_attn_post_norm, ffn_pre_norm=m_ffn_pre_norm, w_up=m_w_up, conv_w=m_conv_w, conv_b=m_conv_b, w_down=m_w_down, ffn_post_norm=m_ffn_post_norm)
    vo = dict(attn_pre_norm=v_attn_pre_norm, w_in=v_w_in, hgrn_lb=v_hgrn_lb, hgrn_gnorm=v_hgrn_gnorm, w_branch_a=v_w_branch_a, rwkv_mu=v_rwkv_mu, rwkv_w0=v_rwkv_w0, rwkv_w2=v_rwkv_w2, rwkv_a0=v_rwkv_a0, rwkv_a2=v_rwkv_a2, rwkv_g2=v_rwkv_g2, rwkv_k_k=v_rwkv_k_k, rwkv_k_a=v_rwkv_k_a, rwkv_r_k=v_rwkv_r_k, rwkv_ln_w=v_rwkv_ln_w, rwkv_ln_b=v_rwkv_ln_b, w_branch_b=v_w_branch_b, w_out=v_w_out, attn_post_norm=v_attn_post_norm, ffn_pre_norm=v_ffn_pre_norm, w_up=v_w_up, conv_w=v_conv_w, conv_b=v_conv_b, w_down=v_w_down, ffn_post_norm=v_ffn_post_norm)

    t = x.shape[1]
    x2 = x.reshape(t, D)
    tgt = loss_target.reshape(t, D)
    st = _stages()

    me = 4 * lax.axis_index("x") + 2 * lax.axis_index("y") + lax.axis_index("c")
    small = jnp.concatenate([rwkv_w2[0], rwkv_a2[0], rwkv_g2[0]], axis=0).astype(BF)
    g_in, g_small = _all_gather("gather_weights", [w_in[0].T.astype(BF), small])
    fw_in_t = g_in.reshape(IN_COLS, D)
    z64 = jnp.zeros((64, D), BF)
    w2p = jnp.concatenate([_blocks_to_cols(g_small[:, 0:64]), z64], axis=0)
    a2p = jnp.concatenate([z64, _blocks_to_cols(g_small[:, 64:128])], axis=0)
    g2f = _blocks_to_cols(g_small[:, 128:256])
    conv_bits = jnp.pad(lax.bitcast_convert_type(conv_w[0], BF).reshape(3, 2 * 704), ((0, 29), (0, 0)))
    late = [w_up[0].T.astype(BF)] + [w[k][0].astype(BF) for k in _BIG[2:]] + [conv_bits]
    late_gather = _Exchange("gather2", late)
    r_k = rwkv_r_k.reshape(1, D)

    xn, z = _norm_in_proj(x2, attn_pre_norm, fw_in_t, 512, 4736)
    mix_par = [hgrn_lb, hgrn_gnorm, rwkv_mu, rwkv_w0, w2p, rwkv_a0, a2p, g2f, rwkv_k_k, rwkv_k_a,
               rwkv_ln_w, rwkv_ln_b, r_k]
    mix_in = [z]
    (o_a, o_b), mix_saved = _stage_fwd(st["mixers"], t, mix_par, mix_in, hook=late_gather)
    gl = [lax.dynamic_update_slice(g, own[None], (me, 0, 0)) for g, own in zip(late_gather.results, late)]
    fw_up_t = gl[0].reshape(2 * DFF, D)
    fw_down = gl[1].reshape(DFF, D)
    fw_a, fw_b, fw_out = (g.reshape(D, D) for g in gl[2:5])
    conv_full = _blocks_to_cols(lax.bitcast_convert_type(gl[5][:, :3].reshape(N_DEV, 3, 704, 2), F32))
    y_a = _mm("branch_a", o_a, fw_a, "nn", BF)
    y_b = _mm("branch_b", o_b, fw_b, "nn", BF)
    merged, mix, h1, xn2 = _merge_out_post(z, y_a, y_b, fw_out, x2, attn_post_norm, ffn_pre_norm, 512)
    hu = _mm("up_proj", xn2, fw_up_t, "nt", F32, tm=1024, tn=1408)
    conv_par = [conv_full, conv_b]
    (act,), conv_saved = _stage_fwd(st["conv"], t, conv_par, [hu])

    loss_acc, d_ffn_post, dh1, dff = _down_loss(act, fw_down, ffn_post_norm, h1, tgt, 512)
    dact = _mm("d_act", dff, fw_down, "nt", BF, tm=1024, tn=1408)
    dw_down = _mm("dw_down", act, dff, "tn", BF, tm=1408, tn=512)
    (dcw, dcb), (dhu,) = _stage_bwd(st["conv"], t, conv_par, [hu], conv_saved, [[dact]], [BF])
    dxn2 = _mm("d_xn2", dhu, fw_up_t, "nn", F32, tm=1024, tn=1024)
    dw_up_t = _mm("dw_up", dhu, xn2, "tn", BF, tm=1408, tn=1024)
    (d_post, d_pre2), (dx_a, dmix) = _stage_bwd(st["post1"], t, [attn_post_norm, ffn_pre_norm], [x2, mix], [],
                                                 [[dh1], [dxn2]], [F32, BF])
    dmerged = _mm("d_merged", dmix, fw_out, "nt", BF)
    dw_out = _mm("dw_out", merged, dmix, "tn", BF)
    _, (dga, dgb, dy_a, dy_b) = _stage_bwd(st["merge"], t, [], [z, z, y_a, y_b], [], [[dmerged]], [BF, BF, BF, BF])
    do_a = _mm("d_oa", dy_a, fw_a, "nt", BF)
    dw_a = _mm("dw_a", o_a, dy_a, "tn", BF)
    do_b = _mm("d_ob", dy_b, fw_b, "nt", BF)
    dw_b = _mm("dw_b", o_b, dy_b, "tn", BF)
    early = [dw_up_t.reshape(N_DEV, 704, D), dw_down.reshape(N_DEV, 352, D), dw_a.reshape(N_DEV, 128, D),
             dw_b.reshape(N_DEV, 128, D), dw_out.reshape(N_DEV, 128, D), _cols_to_blocks(dcw.astype(BF), 704)]
    early_scatter = _Exchange("scatter", early)
    mix_dp, dz_hr = _stage_bwd(st["mixers"], t, mix_par, mix_in, mix_saved, [[do_a], [do_b]], [BF],
                               hook=early_scatter)
    d_lb, d_gn, d_mu, d_w0, d_w2p, d_a0, d_a2p, d_g2, d_kk, d_ka, d_lnw, d_lnb, d_rk = mix_dp
    dz = dz_hr + [dga, dgb]
    dw_in_t = _mm_cols_tn("dw_in", dz, xn, BF, 256)

    ax, ay, ac = lax.axis_index("x"), lax.axis_index("y"), lax.axis_index("c")
    idx4 = jnp.stack([4 * cx + 2 * cy + ac for cx, cy in ((ax, ay), (1 - ax, ay), (ax, 1 - ay), (1 - ax, 1 - ay))])
    idx4 = idx4.astype(jnp.int32)
    idx_me, idx_0 = idx4[0:1], jnp.zeros((1,), jnp.int32)
    d_small = jnp.concatenate([d_w2p[:64], d_a2p[64:], d_g2], axis=0).astype(BF)
    g8s = [dw_in_t.reshape(N_DEV, 1184, D), _cols_to_blocks(d_small, LANES)]
    recv4s = _reduce_pair(g8s)
    sums = [_pair_sum("pair_sum_" + n, idx4, g, r) for n, g, r in zip(("w_in", "small"), g8s, recv4s)]
    swap_ssem, swap_rsem, swap_srcs, swap_lands, token = _chip_swap_start([s[1] for s in sums])
    dxn = _mm_cols_nn("d_xn", dz, fw_in_t, BF, 512, token=token)
    (d_pre1,), (dx,) = _stage_bwd(st["pre1_res"], t, [attn_pre_norm], [x2], [], [[dxn], [dx_a]], [F32])
    grad_x = dx.reshape(x.shape)

    rg = dict(attn_pre_norm=d_pre1, hgrn_lb=d_lb, hgrn_gnorm=d_gn, rwkv_mu=d_mu, rwkv_w0=d_w0, rwkv_a0=d_a0,
              rwkv_k_k=d_kk, rwkv_k_a=d_ka, rwkv_r_k=d_rk, rwkv_ln_w=d_lnw, rwkv_ln_b=d_lnb, attn_post_norm=d_post,
              ffn_pre_norm=d_pre2, conv_b=dcb, ffn_post_norm=d_ffn_post)
    g8 = _all_gather_small("gather_small_grads", _pack_replicated(rg, loss_acc))
    rnames = [n for n, _ in REPL]
    flat = lambda src: [src[n].reshape(1, D) if n == "rwkv_r_k" else src[n] for n in rnames]
    rp_out, loss_row = _adam_replicated(g8, flat(w), flat(mo), flat(vo))
    loss = loss_row[0, 0]
    recv3s = _chip_swap_wait(swap_ssem, swap_rsem, swap_srcs, swap_lands, rp_out[0]["attn_pre_norm"])
    for kind in range(4):
        rp_out[kind]["rwkv_r_k"] = rp_out[kind]["rwkv_r_k"].reshape(rwkv_r_k.shape)

    def small_of(src):
        return jnp.concatenate([src["rwkv_w2"][0], src["rwkv_a2"][0], src["rwkv_g2"][0]], axis=0)

    sh_out = [dict() for _ in range(4)]
    g_in = _sum_partials("sum_w_in", idx_0, sums[0][0][None], recv3s[0]).T
    res = _adam_sharded("adam_w_in", idx_0, g_in[None], None, *[src["w_in"][0] for src in (w, mo, vo)])
    res_s = _adam_sharded("adam_small", idx_0, sums[1][0][None], recv3s[1], *[small_of(src) for src in (w, mo, vo)])
    for kind in range(4):
        sh_out[kind]["w_in"] = res[kind][None]
        sh_out[kind]["rwkv_w2"] = res_s[kind][0:64][None]
        sh_out[kind]["rwkv_a2"] = res_s[kind][64:128][None]
        sh_out[kind]["rwkv_g2"] = res_s[kind][128:256][None]
    for n, own, recv in zip(_BIG[1:] + ("conv_w",), early, early_scatter.results):
        if n == "w_up":
            g_up = _sum_partials("sum_w_up", idx_me, own, recv).T
            res = _adam_sharded("adam_" + n, idx_0, g_up[None], None, *[src[n][0] for src in (w, mo, vo)])
        else:
            res = _adam_sharded("adam_" + n, idx_me, own, recv, *[src[n][0] for src in (w, mo, vo)])
        for kind in range(4):
            sh_out[kind][n] = res[kind][None]

    outs = [loss, grad_x]
    for kind in range(4):
        for name in _WEIGHTS:
            outs.append(sh_out[kind][name] if name in sh_out[kind] else rp_out[kind][name])
    return tuple(outs)
```

```python
import functools

import jax
import jax.numpy as jnp
from jax import lax
from jax.experimental import pallas as pl
from jax.experimental.pallas import tpu as pltpu

F32 = jnp.float32
BF = jnp.bfloat16
MESH = pl.DeviceIdType.MESH

D = 1024
HG_HEADS = 8
HG_K = 128
HG_CHUNK = 32
HG_SCALE = HG_K ** -0.5
HG_PER_STEP = 8
RW_HEADS = 16
RW_N = 64
RW_CHUNK = 64
RW_PAIRS_PER_STEP = 8
DFF = 2816
IN_COLS = 9472
RW_COLS = 3328
EPS = 1e-6
GN_EPS = 1e-5 * RW_N
ADAM_LR = 0.001
ADAM_B1 = 0.9
ADAM_B2 = 0.999
ADAM_EPS = 1e-08
ADAM_WD = 0.01
ADAM_STEP = 10
N_DEV = 8
LANES = 128
VMEM_LIMIT = 56 * 1024 * 1024
TILE_BYTES = 1280 * 1024

REPL = (("attn_pre_norm", 1024), ("hgrn_lb", 1024), ("hgrn_gnorm", 1024), ("rwkv_mu", 3328), ("rwkv_w0", 1024),
        ("rwkv_a0", 1024), ("rwkv_k_k", 1024), ("rwkv_k_a", 1024), ("rwkv_r_k", 1024), ("rwkv_ln_w", 1024),
        ("rwkv_ln_b", 1024), ("attn_post_norm", 1024), ("ffn_pre_norm", 1024), ("conv_b", 5632), ("ffn_post_norm", 1024))
REPL_ROWS = {"hgrn_lb": 2}
REPL_TOTAL = 32


def _cparams(sem=None, **kw):
    return pltpu.CompilerParams(dimension_semantics=sem, vmem_limit_bytes=VMEM_LIMIT, **kw)


_DN = {"nn": ((1,), (0,)), "nt": ((1,), (1,)), "tn": ((0,), (0,))}


def _raw_dot(a, b, mode):
    return lax.dot_general(a.astype(BF), b.astype(BF), (_DN[mode], ((), ())), preferred_element_type=F32)


@functools.partial(jax.custom_vjp, nondiff_argnums=(2,))
def _dot(a, b, mode):
    return _raw_dot(a, b, mode)


def _dot_fwd(a, b, mode):
    return _raw_dot(a, b, mode), (a, b)


def _dot_bwd(mode, res, g):
    a, b = res
    if mode == "nn":
        return _dot(g, b, "nt"), _dot(a, g, "tn")
    if mode == "nt":
        return _dot(g, b, "nn"), _dot(g, a, "tn")
    return _dot(b, g, "nt"), _dot(a, g, "nn")


_dot.defvjp(_dot_fwd, _dot_bwd)


def _bf_pieces(x, n):
    out, r = [], x
    for i in range(n):
        p = r.astype(BF)
        out.append(p)
        if i + 1 < n:
            r = r - p.astype(F32)
    return out


def _raw_split_dot(x, e, mode, n, x_left):
    eb = e.astype(BF)
    acc = None
    for p in _bf_pieces(x, n):
        ops = (p, eb) if x_left else (eb, p)
        t = lax.dot_general(*ops, (_DN[mode], ((), ())), preferred_element_type=F32)
        acc = t if acc is None else acc + t
    return acc


def _raw_headsum(x):
    t = x.shape[0]
    i = lax.broadcasted_iota(jnp.int32, (LANES, LANES), 0)
    j = lax.broadcasted_iota(jnp.int32, (LANES, LANES), 1)
    same = jnp.where((i >= RW_N) == (j >= RW_N), 1.0, 0.0).astype(F32)
    groups = x.shape[1] // LANES
    rows = jnp.concatenate([x[:, q * LANES:(q + 1) * LANES] for q in range(groups)], axis=0)
    s = _raw_split_dot(rows, same, "nn", 2, True)
    return jnp.concatenate([s[q * t:(q + 1) * t] for q in range(groups)], axis=1)


@jax.custom_vjp
def _headsum(x):
    return _raw_headsum(x)


def _headsum_fwd(x):
    return _raw_headsum(x), None


def _headsum_bwd(_, g):
    return (_raw_headsum(g),)


_headsum.defvjp(_headsum_fwd, _headsum_bwd)


@functools.partial(jax.custom_vjp, nondiff_argnums=(2,))
def _tdot(tri, x, n):
    return _raw_split_dot(x, tri, "nn", n, False)


def _tdot_fwd(tri, x, n):
    return _raw_split_dot(x, tri, "nn", n, False), tri


def _tdot_bwd(n, tri, g):
    return jnp.zeros_like(tri), _raw_split_dot(g, tri, "tn", n, False)


_tdot.defvjp(_tdot_fwd, _tdot_bwd)


def _row(x, i):
    r = lax.broadcasted_iota(jnp.int32, x.shape, 0)
    return jnp.sum(jnp.where(r == i, x, 0.0), axis=0, keepdims=True)


def _shift_down(x, prev):
    t = x.shape[0]

    @jax.custom_vjp
    def sh(x, prev):
        r = lax.broadcasted_iota(jnp.int32, x.shape, 0)
        return jnp.where(r == 0, prev, pltpu.roll(x, 1, 0))

    def fwd(x, prev):
        return sh(x, prev), None

    def bwd(_, g):
        r = lax.broadcasted_iota(jnp.int32, g.shape, 0)
        dx = jnp.where(r == t - 1, 0.0, pltpu.roll(g, t - 1, 0))
        return dx, jnp.sum(jnp.where(r == 0, g, 0.0), axis=0, keepdims=True)

    sh.defvjp(fwd, bwd)
    return sh(x, prev)


def _sigmoid(x):
    return jax.nn.sigmoid(x)


def _silu(x):
    return x * jax.nn.sigmoid(x)


def _softplus(x):
    return jnp.maximum(x, 0.0) + jnp.log(1.0 + jnp.exp(-jnp.abs(x)))


def _rms(x, g):
    return (x * lax.rsqrt(jnp.mean(x * x, axis=-1, keepdims=True) + EPS)) * g


def _tril(c):
    r = lax.broadcasted_iota(jnp.int32, (c, c), 0)
    cc = lax.broadcasted_iota(jnp.int32, (c, c), 1)
    return cc <= r


def _f_pre1_residual(ps, xs, cs):
    return [_rms(xs[0], ps[0]), xs[0]], []


def _f_hgrn(ps, xs, cs):
    lbraw, gn = ps
    hq, hf, hi, hg = xs
    hd = range(HG_PER_STEP)
    st = [cs[0][p * HG_K:(p + 1) * HG_K] for p in hd]
    l0, l1 = _row(lbraw, 0), _row(lbraw, 1)
    m = jnp.maximum(l0, l1)
    e0, e1 = jnp.exp(l0 - m), jnp.exp(l1 - m)
    lb = e0 / (e0 + e1)
    q = _silu(hq) * HG_SCALE
    f = lb + (1.0 - lb) * _sigmoid(hf)
    kh = 1.0 - f
    gl = jnp.log(f)
    c = HG_CHUNK
    low = _tril(c)
    tri = jnp.where(low, 1.0, 0.0).astype(F32)
    outs = []
    for i in range(hq.shape[0] // c):
        rows = slice(i * c, (i + 1) * c)
        b = _tdot(tri, gl[rows], 3)
        bref = _row(b, c // 2 - 1)
        blast = _row(b, c - 1)
        qi = q[rows] * jnp.exp(b - bref)
        ki = kh[rows] * jnp.exp(bref - b)
        qd = q[rows] * jnp.exp(b)
        kd = kh[rows] * jnp.exp(blast - b)
        dec = jnp.exp(blast)
        sl = [slice(p * HG_K, (p + 1) * HG_K) for p in hd]
        sc = [jnp.where(low, _dot(qi[:, sl[p]], ki[:, sl[p]], "nt"), 0.0) for p in hd]
        o = [_dot(sc[p], hi[rows, sl[p]], "nn") + _dot(qd[:, sl[p]], st[p], "nt") for p in hd]
        u = [_dot(hi[rows, sl[p]], kd[:, sl[p]], "tn") for p in hd]
        st = [dec[:, sl[p]] * st[p] + u[p] for p in hd]
        outs.append(jnp.concatenate(o, axis=1) if len(o) > 1 else o[0])
    o = outs[0] if len(outs) == 1 else jnp.concatenate(outs, axis=0)
    on = []
    for p in hd:
        op = o[:, p * HG_K:(p + 1) * HG_K]
        on.append(op * lax.rsqrt(jnp.mean(op * op, axis=-1, keepdims=True) + EPS))
    o = jnp.concatenate(on, axis=1) if len(on) > 1 else on[0]
    o = o * gn
    return [o * _silu(hg)], [jnp.concatenate(st, axis=0) if len(st) > 1 else st[0]]


_RW_OFFS = (0, 1024, 2048, 3072, 3200, 3328)


def _f_rwpre(ps, xs, cs):
    mu, w0, w2p, a0, a2p, g2, k_k, k_a = ps
    (prev,) = cs
    t = xs[0].shape[0]
    zs = []
    for i, z in enumerate(xs):
        lo, hi = _RW_OFFS[i], _RW_OFFS[i + 1]
        zs.append(z + mu[:, lo:hi] * (_shift_down(z, prev[:, lo:hi]) - z))
    rr, kr, vr, wa, gz = zs
    w_log = -_softplus(-(w0 + _dot(jnp.tanh(wa), w2p, "nn"))) - 0.5
    lw = -jnp.exp(w_log)
    a = _sigmoid(a0 + _dot(wa, a2p, "nn"))
    g = _dot(_sigmoid(gz), g2, "nn")
    kkr = kr * k_k
    kk = kkr / jnp.maximum(jnp.sqrt(_headsum(kkr * kkr)), 1e-12)
    k2 = kr * (1.0 + (a - 1.0) * k_a)
    newprev = jnp.concatenate([_row(z, t - 1) for z in xs], axis=1)
    return [rr, lw, k2, vr, -kk, kk * a, g], [newprev]


def _raw_inverses(ls):
    n = ls[0].shape[0]
    r = lax.broadcasted_iota(jnp.int32, (n, n), 0)
    c = lax.broadcasted_iota(jnp.int32, (n, n), 1)
    eye = jnp.where(r == c, 1.0, 0.0).astype(F32)
    tinv = [eye + l for l in ls]
    pw = ls
    for _ in range(5):
        pw = [_raw_dot(p, p, "nn") for p in pw]
        tinv = [t + _raw_dot(t, p, "nn") for t, p in zip(tinv, pw)]
    return tinv


@jax.custom_vjp
def _unit_lower_inverses(ls):
    return _raw_inverses(ls)


def _inverses_fwd(ls):
    tinv = _raw_inverses(ls)
    return tinv, tinv


def _inverses_bwd(tinv, gs):
    return ([_raw_dot(_raw_dot(t, g, "tn"), t, "nt") for t, g in zip(tinv, gs)],)


_unit_lower_inverses.defvjp(_inverses_fwd, _inverses_bwd)


@jax.custom_vjp
def _known_inverses(ls, tinv):
    return tinv


def _known_fwd(ls, tinv):
    return tinv, tinv


def _known_bwd(tinv, gs):
    return [_raw_dot(_raw_dot(t, g, "tn"), t, "nt") for t, g in zip(tinv, gs)], [jnp.zeros_like(t) for t in tinv]


_known_inverses.defvjp(_known_fwd, _known_bwd)


@jax.custom_vjp
def _use_kept(computed, kept):
    return kept


def _use_kept_fwd(computed, kept):
    return kept, None


def _use_kept_bwd(_, g):
    return g, jax.tree.map(jnp.zeros_like, g)


_use_kept.defvjp(_use_kept_fwd, _use_kept_bwd)

RW_KEPT = 5


def _f_rwscan(ps, xs, cs, kept=None):
    state = cs[0]
    ys, keep = [], []
    n = 2 * RW_CHUNK
    per_chunk = RW_KEPT * RW_PAIRS_PER_STEP * n
    for i in range(xs[0].shape[0] // RW_CHUNK):
        known = None
        if kept is not None:
            known = [[kept[i * per_chunk + (q * RW_PAIRS_PER_STEP + p) * n:
                           i * per_chunk + (q * RW_PAIRS_PER_STEP + p + 1) * n] for p in range(RW_PAIRS_PER_STEP)]
                     for q in range(RW_KEPT)]
        y, state, mats = _rwkv_chunk([x[i * RW_CHUNK:(i + 1) * RW_CHUNK] for x in xs], state, known)
        ys.append(y)
        keep += [m for group in mats for m in group]
    return [ys[0] if len(ys) == 1 else jnp.concatenate(ys, axis=0)], [state], jnp.concatenate(keep, axis=0)


def _rwkv_chunk(xs, state, known=None):
    npair = RW_PAIRS_PER_STEP
    pr = range(npair)
    r, lw, k, v, av, bv = [[x[:, p * LANES:(p + 1) * LANES] for p in pr] for x in xs]
    sv = [state[p * LANES:(p + 1) * LANES] for p in pr]
    c = RW_CHUNK
    n = 2 * c
    tri = jnp.where(_tril(c), 1.0, 0.0).astype(F32)
    cl = [_tdot(tri, lw[p], 3) for p in pr]
    cl_last = [_row(cl[p], c - 1) for p in pr]
    lane = lax.broadcasted_iota(jnp.int32, (c, LANES), 1)
    h0 = lane < RW_N

    def stack(x):
        return jnp.concatenate([jnp.where(h0, x, 0.0), jnp.where(h0, 0.0, x)], axis=0)

    am = [stack(av[p] * jnp.exp(cl[p] - lw[p])) for p in pr]
    bm = [stack(bv[p] * jnp.exp(-cl[p])) for p in pr]
    km = [stack(k[p] * jnp.exp(-cl[p])) for p in pr]
    rm = [stack(r[p] * jnp.exp(cl[p])) for p in pr]
    vm = [stack(v[p]) for p in pr]
    rn = lax.broadcasted_iota(jnp.int32, (n, n), 0)
    cn = lax.broadcasted_iota(jnp.int32, (n, n), 1)
    blk = (rn >= c) == (cn >= c)
    strict = blk & (cn < rn)
    incl = blk & (cn <= rn)
    lab = [jnp.where(strict, _dot(am[p], bm[p], "nt"), 0.0) for p in pr]
    lak = [jnp.where(strict, _dot(am[p], km[p], "nt"), 0.0) for p in pr]
    wrb = [jnp.where(incl, _dot(rm[p], bm[p], "nt"), 0.0) for p in pr]
    wrk = [jnp.where(incl, _dot(rm[p], km[p], "nt"), 0.0) for p in pr]
    if known is None:
        tinv = _unit_lower_inverses(lab)
    else:
        tinv = _known_inverses(lab, known[0])
        lak, wrb, wrk = _use_kept(lak, known[1]), _use_kept(wrb, known[2]), _use_kept(wrk, known[3])
    rhs = [_dot(am[p], sv[p], "nt") + _dot(lak[p], vm[p], "nn") for p in pr]
    um = [_dot(tinv[p], rhs[p], "nn") for p in pr]
    if known is not None:
        um = _use_kept(um, known[4])
    ym = [_dot(rm[p], sv[p], "nt") + _dot(wrb[p], um[p], "nn") + _dot(wrk[p], vm[p], "nn") for p in pr]
    sn = [(sv[p] + _dot(um[p], bm[p], "tn") + _dot(vm[p], km[p], "tn")) * jnp.exp(cl_last[p]) for p in pr]
    ys = [ym[p][:c] + ym[p][c:] for p in pr]
    return jnp.concatenate(ys, axis=1), jnp.concatenate(sn, axis=0), [tinv, lak, wrb, wrk, um]


def _f_mixers(ps, xs, cs):
    return _mixers(ps, xs, cs, None)


def _f_mixers_kept(ps, xs, cs, kept):
    return _mixers(ps, xs, cs, kept[0])[:2]


def _mixers(ps, xs, cs, kept):
    oa, st = _f_hgrn(ps[:2], xs[:4], cs[:1])
    (r, lw, k, v, av, bv, g), prev = _f_rwpre(ps[2:10], xs[4:], cs[1:2])
    y, sv, keep = _f_rwscan([], [r, lw, k, v, av, bv], cs[2:], kept)
    ob, _ = _f_rwpost(ps[10:], y + [r, k, v, g], [])
    return oa + ob, st + prev + sv, [keep]


def _f_rwpost(ps, xs, cs):
    ln_w, ln_b, r_k = ps
    y, r, k, v, g = xs
    inv_n = 1.0 / RW_N
    yc = y - _headsum(y) * inv_n
    var = _headsum(yc * yc) * inv_n
    yn = yc * lax.rsqrt(var + GN_EPS)
    yn = yn * ln_w + ln_b
    bonus = _headsum(r * k * r_k) * v
    return [(yn + bonus) * g], []


def _f_merge(ps, xs, cs):
    ga, gb, ya, yb = xs
    return [_sigmoid(ga) * ya + _sigmoid(gb) * yb], []


def _f_post1(ps, xs, cs):
    x, mix = xs
    h1 = x + _rms(mix, ps[0])
    return [h1, _rms(h1, ps[1])], []


def _f_conv(ps, xs, cs):
    cw, cb = ps
    p1, p2 = cs
    w0, w1, w2 = _row(cw, 0), _row(cw, 1), _row(cw, 2)
    t = xs[0].shape[0]
    hc = []
    for i, x in enumerate(xs):
        sl = slice(i * DFF, (i + 1) * DFF)
        s1 = _shift_down(x, p1[:, sl])
        s2 = _shift_down(s1, p2[:, sl])
        hc.append(cb[:, sl] + w0[:, sl] * s2 + w1[:, sl] * s1 + w2[:, sl] * x)
    n1 = jnp.concatenate([_row(x, t - 1) for x in xs], axis=1)
    n2 = jnp.concatenate([_row(x, t - 2) for x in xs], axis=1)
    return [_silu(hc[0]) * hc[1]], [n1, n2]


class _Stage:
    def __init__(self, name, f, g, tm, par_per_g, in_pieces, in_offs, carry_shapes, out_pieces, out_dtypes,
                 kept_shapes=(), f_kept=None):
        self.name, self.f, self.g, self.tm = name, f, g, tm
        self.par_per_g, self.in_pieces, self.in_offs = par_per_g, in_pieces, in_offs
        self.carry_shapes, self.out_pieces, self.out_dtypes = carry_shapes, out_pieces, out_dtypes
        self.kept_shapes, self.f_kept = list(kept_shapes), f_kept


def _par_spec(arr, per_g, g):
    r, c = arr.shape
    if per_g:
        return pl.BlockSpec((r, c // g), lambda gi, ni: (0, gi))
    return pl.BlockSpec((r, c), lambda gi, ni: (0, 0))


def _row_spec(tm, width, off, n, rev):
    if rev:
        return pl.BlockSpec((tm, width), lambda gi, ni: (n - 1 - ni, off + gi))
    return pl.BlockSpec((tm, width), lambda gi, ni: (ni, off + gi))


def _carry_spec(shape, n, rev):
    if rev:
        return pl.BlockSpec((None, None) + shape, lambda gi, ni: (gi, n - 1 - ni, 0, 0))
    return pl.BlockSpec((None, None) + shape, lambda gi, ni: (gi, ni, 0, 0))


def _load_pieces(refs, pieces_list):
    out = []
    for ref, pieces in zip(refs, pieces_list):
        o = 0
        for w in pieces:
            out.append(ref[:, o:o + w].astype(F32))
            o += w
    return out


def _store_pieces(refs, pieces_list, vals):
    k = 0
    for ref, pieces in zip(refs, pieces_list):
        o = 0
        for w in pieces:
            ref[:, o:o + w] = vals[k].astype(ref.dtype)
            k += 1
            o += w


_ANY = pl.BlockSpec(memory_space=pl.ANY)


class _Exchange:
    def __init__(self, kind, arrs):
        self.kind, self.arrs, self.results = kind, list(arrs), None
        if kind == "scatter":
            self.out_shape = [jax.ShapeDtypeStruct((N_DEV - 1,) + a.shape[1:], a.dtype) for a in self.arrs]
        else:
            self.out_shape = [jax.ShapeDtypeStruct((N_DEV,) + a.shape, a.dtype) for a in self.arrs]
        self.nsem = (N_DEV if kind == "gather2" else N_DEV - 1) * len(self.arrs)

    def copies(self, in_refs, out_refs, ssem, rsem):
        x, y, c = lax.axis_index("x"), lax.axis_index("y"), lax.axis_index("c")
        me = 4 * x + 2 * y + c
        cps = []
        for a, (i_ref, o_ref) in enumerate(zip(in_refs, out_refs)):
            for j in range(1, N_DEV):
                px = 1 - x if j & 4 else x
                py = 1 - y if j & 2 else y
                pc = 1 - c if j & 1 else c
                if self.kind == "gather":
                    src, dst = i_ref, o_ref.at[me]
                else:
                    src, dst = i_ref.at[4 * px + 2 * py + pc], o_ref.at[j - 1]
                s = (N_DEV - 1) * a + j - 1
                cps.append(pltpu.make_async_remote_copy(src_ref=src, dst_ref=dst, send_sem=ssem.at[s],
                                                        recv_sem=rsem.at[s], device_id=(px, py, pc),
                                                        device_id_type=MESH))
        return cps

    def run(self, step, total, in_refs, out_refs, ssem, rsem):
        if self.kind == "gather2":
            return self.run_two_level(step, total, in_refs, out_refs, ssem, rsem)

        @pl.when(step == 0)
        def _():
            for cp in self.copies(in_refs, out_refs, ssem, rsem):
                cp.start()

        @pl.when(step == total - 1)
        def _():
            for cp in self.copies(in_refs, out_refs, ssem, rsem):
                cp.wait()

    def run_two_level(self, step, total, in_refs, out_refs, ssem, rsem):
        x, y, c = lax.axis_index("x"), lax.axis_index("y"), lax.axis_index("c")
        sibling, xn, yn = (x, y, 1 - c), (1 - x, y, c), (x, 1 - y, c)
        arrs = range(len(in_refs))
        ns = N_DEV

        def num(px, py, pc):
            return 4 * px + 2 * py + pc

        def copy(a, k, to, src, dst):
            return pltpu.make_async_remote_copy(src_ref=src, dst_ref=dst, send_sem=ssem.at[ns * a + k],
                                                recv_sem=rsem.at[ns * a + k], device_id=to, device_id_type=MESH)

        def blk(a, b):
            return out_refs[a].at[b]

        def half(a, b, second):
            h = self.arrs[a].shape[0] // 2
            return out_refs[a].at[b, pl.ds(h if second else 0, h)]

        bx, by, bd = num(1 - x, y, c), num(x, 1 - y, c), num(1 - x, 1 - y, c)

        def firsts(a):
            own = blk(a, num(x, y, c))
            return [copy(a, 0, sibling, in_refs[a], own), copy(a, 1, xn, in_refs[a], own),
                    copy(a, 2, yn, in_refs[a], own)]

        def seconds(a):
            return [copy(a, 3, yn, half(a, bx, False), half(a, bx, False)), copy(a, 5, sibling, blk(a, bx), blk(a, bx)),
                    copy(a, 4, xn, half(a, by, True), half(a, by, True)), copy(a, 6, sibling, blk(a, by), blk(a, by))]

        def third(a):
            return copy(a, 7, sibling, blk(a, bd), blk(a, bd))

        @pl.when(step == 0)
        def _():
            for a in arrs:
                for cp in firsts(a):
                    cp.start()

        @pl.when(step == total // 2)
        def _():
            for a in arrs:
                copy(a, 1, xn, blk(a, bx), blk(a, bx)).wait_recv()
                copy(a, 2, yn, blk(a, by), blk(a, by)).wait_recv()
                for cp in seconds(a):
                    cp.start()

        @pl.when(step == (4 * total) // 5)
        def _():
            for a in arrs:
                copy(a, 3, yn, half(a, bd, False), half(a, bd, False)).wait_recv()
                copy(a, 4, xn, half(a, bd, True), half(a, bd, True)).wait_recv()
                third(a).start()

        @pl.when(step == total - 1)
        def _():
            for a in arrs:
                for k, b in ((0, num(x, y, 1 - c)), (5, num(1 - x, y, 1 - c)), (6, num(x, 1 - y, 1 - c)),
                             (7, num(1 - x, 1 - y, 1 - c))):
                    copy(a, k, sibling, blk(a, b), blk(a, b)).wait_recv()
                for cp in firsts(a) + seconds(a) + [third(a)]:
                    cp.wait_send()


def _hook_specs(hook):
    if hook is None:
        return [], [], [], []
    na = len(hook.arrs)
    sems = [pltpu.SemaphoreType.DMA((hook.nsem,)), pltpu.SemaphoreType.DMA((hook.nsem,))]
    return [_ANY] * na, [_ANY] * na, hook.out_shape, sems


def _stage_fwd(st, t, params, inputs, hook=None):
    g, tm = st.g, min(st.tm, t)
    n = t // tm
    npar, nin, ncar, nout = len(params), len(inputs), len(st.carry_shapes), len(st.out_pieces)
    nk = len(st.kept_shapes)
    h_in, h_out, h_shape, h_sems = _hook_specs(hook)
    nh = len(h_in)

    def body(*refs):
        p_refs = refs[:npar]
        x_refs = refs[npar:npar + nin]
        hi_refs = refs[npar + nin:npar + nin + nh]
        o = npar + nin + nh
        o_refs = refs[o:o + nout]
        s_refs = refs[o + nout:o + nout + ncar]
        k_refs = refs[o + nout + ncar:o + nout + ncar + nk]
        o += nout + ncar + nk
        ho_refs = refs[o:o + nh]
        c_scr = refs[o + nh:o + nh + ncar]
        gi, ni = pl.program_id(0), pl.program_id(1)
        if hook is not None:
            step = gi * n + ni
            hook.run(step, g * n, hi_refs, ho_refs, *refs[-2:])

        @pl.when(ni == 0)
        def _():
            for c in c_scr:
                c[...] = jnp.zeros(c.shape, F32)

        ps = [r[...].astype(F32) for r in p_refs]
        xs = _load_pieces(x_refs, st.in_pieces)
        cs = [c[...] for c in c_scr]
        for s, c in zip(s_refs, cs):
            s[...] = c
        res = st.f(ps, xs, cs)
        outs, ncs = res[0], res[1]
        _store_pieces(o_refs, st.out_pieces, outs)
        for c, v in zip(c_scr, ncs):
            c[...] = v
        for kr, kv in zip(k_refs, res[2] if nk else []):
            kr[...] = kv.astype(kr.dtype)

    in_specs = [_par_spec(p, pg, g) for p, pg in zip(params, st.par_per_g)]
    in_specs += [_row_spec(tm, sum(pc), off, n, False) for pc, off in zip(st.in_pieces, st.in_offs)]
    out_specs = [_row_spec(tm, sum(pc), 0, n, False) for pc in st.out_pieces]
    out_specs += [_carry_spec(s, n, False) for s in st.carry_shapes]
    out_specs += [pl.BlockSpec(s, lambda gi, ni: (ni, 0)) for s in st.kept_shapes]
    out_shape = [jax.ShapeDtypeStruct((t, g * sum(pc)), dt) for pc, dt in zip(st.out_pieces, st.out_dtypes)]
    out_shape += [jax.ShapeDtypeStruct((g, n) + s, F32) for s in st.carry_shapes]
    out_shape += [jax.ShapeDtypeStruct((n * s[0], s[1]), BF) for s in st.kept_shapes]
    res = pl.pallas_call(
        body, name=st.name + "_fwd", grid=(g, n), in_specs=in_specs + h_in, out_specs=out_specs + h_out,
        out_shape=out_shape + h_shape,
        scratch_shapes=[pltpu.VMEM(s, F32) for s in st.carry_shapes] + h_sems,
        compiler_params=_cparams(("arbitrary", "arbitrary")),
    )(*params, *inputs, *(hook.arrs if hook else []))
    if hook is not None:
        hook.results = list(res[nout + ncar + nk:])
    return list(res[:nout]), list(res[nout:nout + ncar + nk])


def _stage_bwd(st, t, params, inputs, saved, douts, dx_dtypes, hook=None):
    g, tm = st.g, min(st.tm, t)
    n = t // tm
    npar, nin, ncar = len(params), len(inputs), len(st.carry_shapes)
    nk = len(st.kept_shapes)
    flat_d = [d for ds in douts for d in ds]
    nd = len(flat_d)
    dx_idx = [i for i, dt in enumerate(dx_dtypes) if dt is not None]
    h_in, h_out, h_shape, h_sems = _hook_specs(hook)
    nh = len(h_in)

    def body(*refs):
        p_refs = refs[:npar]
        x_refs = refs[npar:npar + nin]
        s_refs = refs[npar + nin:npar + nin + ncar]
        k_refs = refs[npar + nin + ncar:npar + nin + ncar + nk]
        o = npar + nin + ncar + nk
        d_refs = refs[o:o + nd]
        hi_refs = refs[o + nd:o + nd + nh]
        o += nd + nh
        dp_refs = refs[o:o + npar]
        dx_refs = refs[o + npar:o + npar + len(dx_idx)]
        ho_refs = refs[o + npar + len(dx_idx):o + npar + len(dx_idx) + nh]
        dc_scr = refs[o + npar + len(dx_idx) + nh:o + npar + len(dx_idx) + nh + ncar]
        gi, ni = pl.program_id(0), pl.program_id(1)
        if hook is not None:
            step = gi * n + ni
            hook.run(step, g * n, hi_refs, ho_refs, *refs[-2:])

        @pl.when(ni == 0)
        def _():
            for c in dc_scr:
                c[...] = jnp.zeros(c.shape, F32)

        ps = [r[...].astype(F32) for r in p_refs]
        xs = _load_pieces(x_refs, st.in_pieces)
        cs = [s[...] for s in s_refs]
        dys = []
        k = 0
        for ds, pieces in zip(douts, st.out_pieces):
            acc = _load_pieces([d_refs[k]], [pieces])
            for j in range(1, len(ds)):
                more = _load_pieces([d_refs[k + j]], [pieces])
                acc = [a + b for a, b in zip(acc, more)]
            dys += acc
            k += len(ds)
        if nk:
            kept = [r[...].astype(F32) for r in k_refs]
            _, vjp = jax.vjp(lambda p, x, c: st.f_kept(p, x, c, kept), ps, xs, cs)
        else:
            _, vjp = jax.vjp(st.f, ps, xs, cs)
        dps, dxs, dcs = vjp((dys, [c[...] for c in dc_scr]))
        k = 0
        per_in = []
        for pieces in st.in_pieces:
            per_in.append(dxs[k:k + len(pieces)])
            k += len(pieces)
        for ref, i in zip(dx_refs, dx_idx):
            _store_pieces([ref], [st.in_pieces[i]], per_in[i])
        for c, v in zip(dc_scr, dcs):
            c[...] = v
        for ref, dp, pg in zip(dp_refs, dps, st.par_per_g):
            first = (ni == 0) if pg else ((ni == 0) & (gi == 0))

            @pl.when(first)
            def _():
                ref[...] = jnp.zeros(ref.shape, F32)

            ref[...] += dp

    in_specs = [_par_spec(p, pg, g) for p, pg in zip(params, st.par_per_g)]
    in_specs += [_row_spec(tm, sum(pc), off, n, True) for pc, off in zip(st.in_pieces, st.in_offs)]
    in_specs += [_carry_spec(s, n, True) for s in st.carry_shapes]
    in_specs += [pl.BlockSpec(s, lambda gi, ni: (n - 1 - ni, 0)) for s in st.kept_shapes]
    for ds, pc in zip(douts, st.out_pieces):
        in_specs += [_row_spec(tm, sum(pc), 0, n, True) for _ in ds]
    out_specs = [_par_spec(p, pg, g) for p, pg in zip(params, st.par_per_g)]
    out_specs += [_row_spec(tm, sum(st.in_pieces[i]), 0, n, True) for i in dx_idx]
    out_shape = [jax.ShapeDtypeStruct(p.shape, F32) for p in params]
    out_shape += [jax.ShapeDtypeStruct((t, g * sum(st.in_pieces[i])), dx_dtypes[i]) for i in dx_idx]
    res = pl.pallas_call(
        body, name=st.name + "_bwd", grid=(g, n), in_specs=in_specs + h_in, out_specs=out_specs + h_out,
        out_shape=out_shape + h_shape,
        scratch_shapes=[pltpu.VMEM(s, F32) for s in st.carry_shapes] + h_sems,
        compiler_params=_cparams(("arbitrary", "arbitrary")),
    )(*params, *inputs, *saved, *flat_d, *(hook.arrs if hook else []))
    if hook is not None:
        hook.results = list(res[npar + len(dx_idx):])
    return list(res[:npar]), list(res[npar:npar + len(dx_idx)])


def _pick(n, cap):
    if n <= cap:
        return n
    best = LANES
    for k in range(1, n // LANES + 1):
        if (n // LANES) % k == 0 and k * LANES <= cap:
            best = k * LANES
    return best


def _mm(name, a, b, mode, out_dtype=F32, tm=1024, tn=512, b_outer=False):
    m = a.shape[1] if mode == "tn" else a.shape[0]
    k = a.shape[0] if mode == "tn" else a.shape[1]
    n = b.shape[0] if mode == "nt" else b.shape[1]
    tm, tn = _pick(m, tm), _pick(n, tn)
    if b_outer:
        grid = (n // tn, m // tm)
        ij = lambda p, q: (q, p)
    else:
        grid = (m // tm, n // tn)
        ij = lambda p, q: (p, q)

    def body(a_ref, b_ref, o_ref):
        o_ref[...] = _raw_dot(a_ref[...], b_ref[...], mode).astype(o_ref.dtype)

    if mode == "tn":
        a_spec = pl.BlockSpec((k, tm), lambda p, q: (0, ij(p, q)[0]))
    else:
        a_spec = pl.BlockSpec((tm, k), lambda p, q: (ij(p, q)[0], 0))
    b_mode = dict(pipeline_mode=pl.Buffered(1)) if tn == n else {}
    if mode == "nt":
        b_spec = pl.BlockSpec((tn, k), lambda p, q: (ij(p, q)[1], 0), **b_mode)
    else:
        b_spec = pl.BlockSpec((k, tn), lambda p, q: (0, ij(p, q)[1]), **b_mode)
    return pl.pallas_call(
        body, name=name, grid=grid, in_specs=[a_spec, b_spec],
        out_specs=pl.BlockSpec((tm, tn), lambda p, q: ij(p, q)),
        out_shape=jax.ShapeDtypeStruct((m, n), out_dtype),
        compiler_params=_cparams(("arbitrary", "arbitrary")),
    )(a, b)


def _mm_cols_tn(name, pieces, b, out_dtype, tm):
    k, n = b.shape
    counts = [p.shape[1] // tm for p in pieces]
    starts = [sum(counts[:i]) for i in range(len(pieces))]
    na = len(pieces)

    def body(*refs):
        b_ref, o_ref = refs[na], refs[-1]
        i = pl.program_id(0)
        for a_ref, s, c in zip(refs[:na], starts, counts):
            @pl.when((i >= s) & (i < s + c))
            def _():
                o_ref[...] = _raw_dot(a_ref[...], b_ref[...], "tn").astype(o_ref.dtype)

    def spec(s, c):
        return pl.BlockSpec((k, tm), lambda i: (0, jnp.clip(i - s, 0, c - 1)))

    return pl.pallas_call(
        body, name=name, grid=(sum(counts),),
        in_specs=[spec(s, c) for s, c in zip(starts, counts)]
        + [pl.BlockSpec(b.shape, lambda i: (0, 0), pipeline_mode=pl.Buffered(1))],
        out_specs=pl.BlockSpec((tm, n), lambda i: (i, 0)),
        out_shape=jax.ShapeDtypeStruct((sum(counts) * tm, n), out_dtype),
        compiler_params=_cparams(("arbitrary",)),
    )(*pieces, b)


def _norm_in_proj(x, g, w_t, tm, tn):
    t, k = x.shape
    n = w_t.shape[0]
    tm, tn = _pick(t, tm), _pick(n, tn)

    def body(x_ref, g_ref, w_ref, xn_ref, z_ref):
        xn = _rms(x_ref[...], g_ref[...]).astype(BF)
        xn_ref[...] = xn
        z_ref[...] = _raw_dot(xn, w_ref[...], "nt")

    xns, z = pl.pallas_call(
        body, name="in_proj", grid=(n // tn, t // tm),
        in_specs=[pl.BlockSpec((tm, k), lambda j, i: (i, 0)), pl.BlockSpec((1, k), lambda j, i: (0, 0)),
                  pl.BlockSpec((tn, k), lambda j, i: (j, 0))],
        out_specs=[pl.BlockSpec((None, tm, k), lambda j, i: (j, i, 0)), pl.BlockSpec((tm, tn), lambda j, i: (i, j))],
        out_shape=[jax.ShapeDtypeStruct((n // tn, t, k), BF), jax.ShapeDtypeStruct((t, n), F32)],
        compiler_params=_cparams(("arbitrary", "arbitrary")),
    )(x, g, w_t)
    return xns[0], z


def _merge_out_post(z, y_a, y_b, w_out, x, g_post, g_pre2, tm):
    t = x.shape[0]
    tm = _pick(t, tm)
    w = 256
    npc = D // w
    ga0, gb0 = (IN_COLS - 2 * D) // w, (IN_COLS - D) // w

    def body(*refs):
        ga_refs, gb_refs = refs[:npc], refs[npc:2 * npc]
        ya_ref, yb_ref, w_ref, x_ref, gp_ref, g2_ref, m_ref, mix_ref, h_ref, xn_ref = refs[2 * npc:]
        parts = []
        for p in range(npc):
            cols = slice(p * w, (p + 1) * w)
            parts.append(_sigmoid(ga_refs[p][...]) * ya_ref[:, cols].astype(F32)
                         + _sigmoid(gb_refs[p][...]) * yb_ref[:, cols].astype(F32))
        merged = jnp.concatenate(parts, axis=1).astype(BF)
        m_ref[...] = merged
        mix = _raw_dot(merged, w_ref[...], "nn")
        mix_ref[...] = mix
        h1 = x_ref[...] + _rms(mix, gp_ref[...])
        h_ref[...] = h1
        xn_ref[...] = _rms(h1, g2_ref[...]).astype(BF)

    row = pl.BlockSpec((tm, D), lambda i: (i, 0))
    one = pl.BlockSpec((1, D), lambda i: (0, 0))

    def gate(b0):
        return [pl.BlockSpec((tm, w), functools.partial(lambda i, b: (i, b), b=b0 + p)) for p in range(npc)]

    return pl.pallas_call(
        body, name="merge_out_post", grid=(t // tm,),
        in_specs=gate(ga0) + gate(gb0) + [row, row, pl.BlockSpec((D, D), lambda i: (0, 0), pipeline_mode=pl.Buffered(1)),
                                          row, one, one],
        out_specs=[row, row, row, row],
        out_shape=[jax.ShapeDtypeStruct((t, D), BF), jax.ShapeDtypeStruct((t, D), F32),
                   jax.ShapeDtypeStruct((t, D), F32), jax.ShapeDtypeStruct((t, D), BF)],
        compiler_params=_cparams(("arbitrary",)),
    )(*([z] * (2 * npc)), y_a, y_b, w_out, x, g_post, g_pre2)


def _accumulate(ni, refs, vals):
    @pl.when(ni == 0)
    def _():
        for r in refs:
            r[...] = jnp.zeros(r.shape, F32)

    for r, v in zip(refs, vals):
        r[...] += v


def _dmerged_merge_bwd(dmix, w_out, z, y_a, y_b, tm):
    t = dmix.shape[0]
    tm = _pick(t, tm)
    w = 256
    npc = D // w
    ga0, gb0 = (IN_COLS - 2 * D) // w, (IN_COLS - D) // w

    def body(*refs):
        dm_ref, w_ref = refs[0], refs[1]
        ga_refs, gb_refs = refs[2:2 + npc], refs[2 + npc:2 + 2 * npc]
        ya_ref, yb_ref, dga_ref, dgb_ref, dya_ref, dyb_ref = refs[2 + 2 * npc:]
        dmerged = _raw_dot(dm_ref[...], w_ref[...], "nt")
        for p in range(npc):
            cols = slice(p * w, (p + 1) * w)
            xs = [ga_refs[p][...], gb_refs[p][...], ya_ref[:, cols].astype(F32), yb_ref[:, cols].astype(F32)]
            _, vjp = jax.vjp(lambda *a: _f_merge([], list(a), [])[0][0], *xs)
            dga, dgb, dya, dyb = vjp(dmerged[:, cols])
            dga_ref[:, cols] = dga.astype(BF)
            dgb_ref[:, cols] = dgb.astype(BF)
            dya_ref[:, cols] = dya.astype(BF)
            dyb_ref[:, cols] = dyb.astype(BF)

    row = pl.BlockSpec((tm, D), lambda i: (i, 0))

    def gate(b0):
        return [pl.BlockSpec((tm, w), functools.partial(lambda i, b: (i, b), b=b0 + p)) for p in range(npc)]

    return pl.pallas_call(
        body, name="merge_bwd", grid=(t // tm,),
        in_specs=[row, pl.BlockSpec((D, D), lambda i: (0, 0), pipeline_mode=pl.Buffered(1))] + gate(ga0) + gate(gb0)
        + [row, row],
        out_specs=[row] * 4, out_shape=[jax.ShapeDtypeStruct((t, D), BF)] * 4,
        compiler_params=_cparams(("arbitrary",)),
    )(dmix, w_out, *([z] * (2 * npc)), y_a, y_b)


def _dxn2_post1_bwd(dhu, w_up_t, x, mix, dh1, g_post, g_pre2, tm):
    t, k = dhu.shape
    tm = _pick(t, tm)

    def body(a_ref, w_ref, x_ref, m_ref, dh_ref, gp_ref, g2_ref, dgp_ref, dg2_ref, dx_ref, dm_ref):
        dxn2 = _raw_dot(a_ref[...], w_ref[...], "nn")
        _, vjp = jax.vjp(lambda gp, g2, xx, mm: _f_post1([gp, g2], [xx, mm], [])[0],
                         gp_ref[...], g2_ref[...], x_ref[...], m_ref[...])
        dgp, dg2, dx, dm = vjp([dh_ref[...], dxn2])
        _accumulate(pl.program_id(0), [dgp_ref, dg2_ref], [dgp, dg2])
        dx_ref[...] = dx
        dm_ref[...] = dm.astype(BF)

    row = pl.BlockSpec((tm, D), lambda i: (i, 0))
    one = pl.BlockSpec((1, D), lambda i: (0, 0))
    return pl.pallas_call(
        body, name="post1_bwd", grid=(t // tm,),
        in_specs=[pl.BlockSpec((tm, k), lambda i: (i, 0)),
                  pl.BlockSpec((k, D), lambda i: (0, 0), pipeline_mode=pl.Buffered(1)), row, row, row, one, one],
        out_specs=[one, one, row, row],
        out_shape=[jax.ShapeDtypeStruct((1, D), F32), jax.ShapeDtypeStruct((1, D), F32),
                   jax.ShapeDtypeStruct((t, D), F32), jax.ShapeDtypeStruct((t, D), BF)],
        compiler_params=_cparams(("arbitrary",)),
    )(dhu, w_up_t, x, mix, dh1, g_post, g_pre2)


def _dxn_pre1_bwd(pieces, w_t, x, dx_res, g, tm, token):
    t = x.shape[0]
    tm = _pick(t, tm)
    offs = [sum(p.shape[1] for p in pieces[:i]) for i in range(len(pieces))]
    na = len(pieces)

    def body(*refs):
        w_ref, x_ref, r_ref, g_ref = refs[na:na + 4]
        dg_ref, dx_ref = refs[-2:]
        dxn = None
        for a_ref, off in zip(refs[:na], offs):
            part = _raw_dot(a_ref[...], w_ref[off:off + a_ref.shape[1], :], "nn")
            dxn = part if dxn is None else dxn + part
        _, vjp = jax.vjp(lambda gg, xx: _f_pre1_residual([gg], [xx], [])[0], g_ref[...], x_ref[...])
        dg, dx = vjp([dxn, r_ref[...]])
        _accumulate(pl.program_id(0), [dg_ref], [dg])
        dx_ref[...] = dx

    row = pl.BlockSpec((tm, D), lambda i: (i, 0))
    one = pl.BlockSpec((1, D), lambda i: (0, 0))
    return pl.pallas_call(
        body, name="pre1_bwd", grid=(t // tm,),
        in_specs=[pl.BlockSpec((tm, p.shape[1]), lambda i: (i, 0)) for p in pieces]
        + [pl.BlockSpec(w_t.shape, lambda i: (0, 0), pipeline_mode=pl.Buffered(1)), row, row, one,
           pl.BlockSpec(token.shape, lambda i: (0, 0))],
        out_specs=[one, row],
        out_shape=[jax.ShapeDtypeStruct((1, D), F32), jax.ShapeDtypeStruct((t, D), F32)],
        compiler_params=_cparams(("arbitrary",)),
    )(*pieces, w_t, x, dx_res, g, token)


def _down_loss(act, w_down, g_post, h1, tgt, tm):
    t, k = act.shape
    tm = _pick(t, tm)

    def body(a_ref, w_ref, g_ref, h_ref, t_ref, loss_ref, dg_ref, dh_ref, df_ref):
        ni = pl.program_id(0)
        ff = _raw_dot(a_ref[...], w_ref[...], "nn")
        target = t_ref[...]

        def lossf(g, h1, ff):
            e = h1 + _rms(ff, g) - target
            return 0.5 * jnp.sum(jnp.mean(e * e, axis=-1))

        l, (dg, dh, df) = jax.value_and_grad(lossf, argnums=(0, 1, 2))(g_ref[...], h_ref[...], ff)

        @pl.when(ni == 0)
        def _():
            loss_ref[...] = jnp.zeros(loss_ref.shape, F32)
            dg_ref[...] = jnp.zeros(dg_ref.shape, F32)

        loss_ref[...] += jnp.full(loss_ref.shape, l, F32)
        dg_ref[...] += dg
        dh_ref[...] = dh
        df_ref[...] = df.astype(df_ref.dtype)

    row = pl.BlockSpec((tm, D), lambda ni: (ni, 0))
    one = pl.BlockSpec((1, D), lambda ni: (0, 0))
    return pl.pallas_call(
        body, name="down_loss", grid=(t // tm,),
        in_specs=[pl.BlockSpec((tm, k), lambda ni: (ni, 0)),
                  pl.BlockSpec((k, D), lambda ni: (0, 0), pipeline_mode=pl.Buffered(1)), one, row, row],
        out_specs=[pl.BlockSpec((1, LANES), lambda ni: (0, 0)), one, row, row],
        out_shape=[jax.ShapeDtypeStruct((1, LANES), F32), jax.ShapeDtypeStruct((1, D), F32),
                   jax.ShapeDtypeStruct((t, D), F32), jax.ShapeDtypeStruct((t, D), BF)],
        compiler_params=_cparams(("arbitrary",)),
    )(act, w_down, g_post, h1, tgt)


_ANY = pl.BlockSpec(memory_space=pl.ANY)


def _all_gather(name, blks):
    na = len(blks)
    ns = 8

    def body(*refs):
        x_refs, out_refs = refs[:na], refs[na:2 * na]
        send_sems, recv_sems, local_sems = refs[2 * na:]
        x, y, cc = lax.axis_index("x"), lax.axis_index("y"), lax.axis_index("c")
        sibling, xn, yn = (x, y, 1 - cc), (1 - x, y, cc), (x, 1 - y, cc)

        def num(px, py, pc):
            return 4 * px + 2 * py + pc

        def copy(a, k, to, src, dst):
            return pltpu.make_async_remote_copy(src_ref=src, dst_ref=dst, send_sem=send_sems.at[ns * a + k],
                                                recv_sem=recv_sems.at[ns * a + k], device_id=to, device_id_type=MESH)

        def halves(a, blk):
            h = blks[a].shape[0] // 2
            return out_refs[a].at[blk, pl.ds(0, h)], out_refs[a].at[blk, pl.ds(h, h)]

        mine, sends = [], []
        for a in range(na):
            o = out_refs[a]
            m = pltpu.make_async_copy(x_refs[a], o.at[num(x, y, cc)], local_sems.at[a])
            m.start()
            mine.append(m)
            own = o.at[num(x, y, cc)]
            sends.append([copy(a, 0, sibling, x_refs[a], own), copy(a, 1, xn, x_refs[a], own),
                          copy(a, 2, yn, x_refs[a], own)])
            for cp in sends[a]:
                cp.start()
        for a in range(na):
            o = out_refs[a]
            bx, by, bd = num(1 - x, y, cc), num(x, 1 - y, cc), num(1 - x, 1 - y, cc)
            copy(a, 1, xn, o.at[bx], o.at[bx]).wait_recv()
            more = [copy(a, 3, yn, halves(a, bx)[0], halves(a, bx)[0]), copy(a, 5, sibling, o.at[bx], o.at[bx])]
            for cp in more:
                cp.start()
            sends[a] += more
        for a in range(na):
            o = out_refs[a]
            bx, by, bd = num(1 - x, y, cc), num(x, 1 - y, cc), num(1 - x, 1 - y, cc)
            copy(a, 2, yn, o.at[by], o.at[by]).wait_recv()
            more = [copy(a, 4, xn, halves(a, by)[1], halves(a, by)[1]), copy(a, 6, sibling, o.at[by], o.at[by])]
            for cp in more:
                cp.start()
            sends[a] += more
        for a in range(na):
            o = out_refs[a]
            bd = num(1 - x, 1 - y, cc)
            copy(a, 3, yn, halves(a, bd)[0], halves(a, bd)[0]).wait_recv()
            copy(a, 4, xn, halves(a, bd)[1], halves(a, bd)[1]).wait_recv()
            fw = copy(a, 7, sibling, o.at[bd], o.at[bd])
            fw.start()
            sends[a].append(fw)
        for a in range(na):
            o = out_refs[a]
            for k, blk in ((0, num(x, y, 1 - cc)), (5, num(1 - x, y, 1 - cc)), (6, num(x, 1 - y, 1 - cc)),
                           (7, num(1 - x, 1 - y, 1 - cc))):
                copy(a, k, sibling, o.at[blk], o.at[blk]).wait_recv()
            for cp in sends[a]:
                cp.wait_send()
        for m in mine:
            m.wait()

    res = pl.pallas_call(
        body, name=name, in_specs=[_ANY] * na, out_specs=[_ANY] * na,
        out_shape=[jax.ShapeDtypeStruct((N_DEV,) + b.shape, b.dtype) for b in blks],
        scratch_shapes=[pltpu.SemaphoreType.DMA((ns * na,)), pltpu.SemaphoreType.DMA((ns * na,)),
                        pltpu.SemaphoreType.DMA((na,))],
    )(*blks)
    return list(res)


def _all_gather_small(name, blk):
    def body(x_ref, out_ref, ssem, rsem, lsem):
        x, y, c = lax.axis_index("x"), lax.axis_index("y"), lax.axis_index("c")
        me = 4 * x + 2 * y + c
        mine = pltpu.make_async_copy(x_ref, out_ref.at[me], lsem)
        mine.start()
        cps = []
        for j in range(1, N_DEV):
            px = 1 - x if j & 4 else x
            py = 1 - y if j & 2 else y
            pc = 1 - c if j & 1 else c
            cps.append(pltpu.make_async_remote_copy(src_ref=x_ref, dst_ref=out_ref.at[me], send_sem=ssem.at[j - 1],
                                                    recv_sem=rsem.at[j - 1], device_id=(px, py, pc),
                                                    device_id_type=MESH))
        for cp in cps:
            cp.start()
        for cp in cps:
            cp.wait()
        mine.wait()

    return pl.pallas_call(
        body, name=name, in_specs=[_ANY], out_specs=_ANY,
        out_shape=jax.ShapeDtypeStruct((N_DEV,) + blk.shape, blk.dtype),
        scratch_shapes=[pltpu.SemaphoreType.DMA((N_DEV - 1,)), pltpu.SemaphoreType.DMA((N_DEV - 1,)),
                        pltpu.SemaphoreType.DMA],
    )(blk)


def _reduce_pair(g8s):
    na = len(g8s)

    def body(*refs):
        g_refs, recv_refs = refs[:na], refs[na:2 * na]
        ssem, rsem = refs[2 * na:]
        x, y, cc = lax.axis_index("x"), lax.axis_index("y"), lax.axis_index("c")
        chips = [(x, y), (1 - x, y), (x, 1 - y), (1 - x, 1 - y)]
        sib = (x, y, 1 - cc)
        for a in range(na):
            for k, (cx, cy) in enumerate(chips):
                pltpu.make_async_remote_copy(
                    src_ref=g_refs[a].at[4 * cx + 2 * cy + 1 - cc], dst_ref=recv_refs[a].at[k],
                    send_sem=ssem.at[a], recv_sem=rsem.at[a], device_id=sib, device_id_type=MESH).start()
        for a in range(na):
            pltpu.make_async_remote_copy(src_ref=recv_refs[a], dst_ref=recv_refs[a], send_sem=ssem.at[a],
                                         recv_sem=rsem.at[a], device_id=sib, device_id_type=MESH).wait()

    res = pl.pallas_call(
        body, name="reduce_pair", in_specs=[_ANY] * na, out_specs=[_ANY] * na,
        out_shape=[jax.ShapeDtypeStruct((4,) + g.shape[1:], g.dtype) for g in g8s],
        scratch_shapes=[pltpu.SemaphoreType.DMA((na,)), pltpu.SemaphoreType.DMA((na,))],
    )(*g8s)
    return list(res)


_HBM = pl.BlockSpec(memory_space=pltpu.HBM)
_SEM = pl.BlockSpec(memory_space=pltpu.SEMAPHORE)
_EFFECT = pltpu.SideEffectType.DATAFLOW_SIDE_EFFECTING


def _chip_swap_copies(s_refs, land_refs, ssem, rsem):
    x, y, c = lax.axis_index("x"), lax.axis_index("y"), lax.axis_index("c")
    targets = [(1 - x, y, c), (x, 1 - y, c), (1 - x, 1 - y, c)]
    return [pltpu.make_async_remote_copy(src_ref=s.at[k], dst_ref=d.at[k], send_sem=ssem.at[3 * a + k],
                                         recv_sem=rsem.at[3 * a + k], device_id=targets[k], device_id_type=MESH)
            for a, (s, d) in enumerate(zip(s_refs, land_refs)) for k in range(3)]


def _chip_swap_start(sends):
    na = len(sends)

    def body(*refs):
        cps = _chip_swap_copies(refs[:na], refs[na:2 * na], refs[2 * na], refs[2 * na + 1])
        for cp in cps:
            cp.start()
        token = refs[-1]
        token[...] = jnp.zeros(token.shape, token.dtype)

    bufs = [pltpu.HBM(s.shape, s.dtype) for s in sends]
    res = pl.pallas_call(
        body, name="chip_swap_start",
        out_shape=[pltpu.SemaphoreType.DMA((3 * na,)), pltpu.SemaphoreType.DMA((3 * na,))] + bufs + bufs
        + [jax.ShapeDtypeStruct((8, LANES), F32)],
        in_specs=[_HBM] * (2 * na), out_specs=[_SEM, _SEM] + [_HBM] * (2 * na) + [pl.BlockSpec(memory_space=pltpu.VMEM)],
        input_output_aliases={i: 2 + i for i in range(2 * na)},
        compiler_params=pltpu.CompilerParams(has_side_effects=_EFFECT),
    )(*[pltpu.with_memory_space_constraint(s, pltpu.HBM) for s in sends],
      *[pltpu.with_memory_space_constraint(lax.empty(s.shape, s.dtype), pltpu.HBM) for s in sends])
    return res[0], res[1], list(res[2:2 + na]), list(res[2 + na:2 + 2 * na]), res[-1]


def _chip_swap_wait(ssem, rsem, srcs, lands, after):
    na = len(srcs)

    def body(*refs):
        cps = _chip_swap_copies(refs[:na], refs[na:2 * na], refs[2 * na], refs[2 * na + 1])
        for cp in cps:
            cp.wait_send()
            cp.wait_recv()

    bufs = [pltpu.HBM(s.shape, s.dtype) for s in srcs]
    res = pl.pallas_call(
        body, name="chip_swap_wait", out_shape=bufs + bufs,
        in_specs=[_HBM] * (2 * na) + [_SEM, _SEM, _ANY], out_specs=[_HBM] * (2 * na),
        input_output_aliases={i: i for i in range(2 * na)},
        compiler_params=pltpu.CompilerParams(has_side_effects=_EFFECT),
    )(*srcs, *lands, ssem, rsem, after)
    return list(res[na:])


def _pick_rows(r, c, budget=TILE_BYTES):
    if r * c * 4 <= budget or r % 16:
        return r
    best = 16
    for tr in range(16, r, 16):
        if r % tr == 0 and tr * c * 4 <= budget:
            best = tr
    return best


def _pair_sum(name, idx4, g8, recv4):
    _, r, c = g8.shape
    tr = _pick_rows(r, c, 2 * TILE_BYTES)

    def body(idx_ref, a_ref, b_ref, o0_ref, o3_ref):
        k = pl.program_id(1)
        s = a_ref[...].astype(F32) + b_ref[...].astype(F32)

        @pl.when(k == 0)
        def _():
            o0_ref[...] = s

        @pl.when(k > 0)
        def _():
            o3_ref[...] = s.astype(BF)

    spec = pltpu.PrefetchScalarGridSpec(
        num_scalar_prefetch=1, grid=(r // tr, 4),
        in_specs=[pl.BlockSpec((None, tr, c), lambda i, k, idx: (idx[k], i, 0)),
                  pl.BlockSpec((None, tr, c), lambda i, k, idx: (k, i, 0))],
        out_specs=[pl.BlockSpec((tr, c), lambda i, k, idx: (i, 0)),
                   pl.BlockSpec((None, tr, c), lambda i, k, idx: (jnp.maximum(k - 1, 0), i, 0))])
    return pl.pallas_call(
        body, name=name, grid_spec=spec,
        out_shape=[jax.ShapeDtypeStruct((r, c), F32), jax.ShapeDtypeStruct((3, r, c), BF)],
        compiler_params=_cparams(("arbitrary", "arbitrary")),
    )(idx4, g8, recv4)


def _adamw(w, g, m, v):
    m = ADAM_B1 * m + (1.0 - ADAM_B1) * g
    v = ADAM_B2 * v + (1.0 - ADAM_B2) * jnp.square(g)
    m_hat = m / (1.0 - ADAM_B1 ** ADAM_STEP)
    v_hat = v / (1.0 - ADAM_B2 ** ADAM_STEP)
    delta = -ADAM_LR * (m_hat / (jnp.sqrt(v_hat) + ADAM_EPS) + ADAM_WD * w)
    return delta, m, v


def _sum_partials(name, idx1, own, recv):
    _, r, c = own.shape
    tr = _pick_rows(r, c, 2 * TILE_BYTES)
    nj = recv.shape[0]

    def body(idx_ref, p_ref, r_ref, g_out):
        g = p_ref[...].astype(F32)
        for k in range(nj):
            g = g + r_ref[k].astype(F32)
        g_out[...] = g

    row = pl.BlockSpec((tr, c), lambda i, idx: (i, 0))
    spec = pltpu.PrefetchScalarGridSpec(
        num_scalar_prefetch=1, grid=(r // tr,),
        in_specs=[pl.BlockSpec((None, tr, c), lambda i, idx: (idx[0], i, 0)),
                  pl.BlockSpec((nj, tr, c), lambda i, idx: (0, i, 0))],
        out_specs=row)
    return pl.pallas_call(body, name=name, grid_spec=spec, out_shape=jax.ShapeDtypeStruct((r, c), F32),
                          compiler_params=_cparams(("arbitrary",)))(idx1, own, recv)


def _adam_sharded(name, idx1, own, recv, w, m, v):
    r, c = w.shape
    tr = _pick_rows(r, c)
    nj = 0 if recv is None else recv.shape[0]
    if recv is None:
        recv = jnp.zeros((1, 8, LANES), BF)

    def body(idx_ref, p_ref, r_ref, w_ref, m_ref, v_ref, g_out, d_out, m_out, v_out):
        g = p_ref[...].astype(F32)
        for k in range(nj):
            g = g + r_ref[k].astype(F32)
        d, mn, vn = _adamw(w_ref[...], g, m_ref[...], v_ref[...])
        g_out[...] = g
        d_out[...] = d
        m_out[...] = mn
        v_out[...] = vn

    row = pl.BlockSpec((tr, c), lambda i, idx: (i, 0))
    if nj:
        recv_spec = pl.BlockSpec((nj, tr, c), lambda i, idx: (0, i, 0))
    else:
        recv_spec = pl.BlockSpec(recv.shape, lambda i, idx: (0, 0, 0))
    spec = pltpu.PrefetchScalarGridSpec(
        num_scalar_prefetch=1, grid=(r // tr,),
        in_specs=[pl.BlockSpec((None, tr, c), lambda i, idx: (idx[0], i, 0)), recv_spec, row, row, row],
        out_specs=[row] * 4)
    return pl.pallas_call(
        body, name=name, grid_spec=spec, out_shape=[jax.ShapeDtypeStruct((r, c), F32)] * 4,
        compiler_params=_cparams(("arbitrary",)),
    )(idx1, own, recv, w, m, v)


def _repl_rows():
    rows, r = {}, 0
    for name, cols in REPL:
        rows[name] = r
        r += REPL_ROWS.get(name, 1) * ((cols + D - 1) // D)
    return rows


LOSS_ROW = 24


def _pack_replicated(grads, loss_acc):
    rows = _repl_rows()
    names = [n for n, _ in REPL]

    def body(*refs):
        o_ref = refs[-1]
        o_ref[...] = jnp.zeros(o_ref.shape, F32)
        o_ref[LOSS_ROW:LOSS_ROW + 1, 0:LANES] = refs[-2][...]
        for name, ref in zip(names, refs[:-2]):
            r0 = rows[name]
            nr, nc = ref.shape
            if nc <= D:
                o_ref[r0:r0 + nr, 0:nc] = ref[...]
            else:
                for j in range((nc + D - 1) // D):
                    lo, hi = j * D, min(nc, (j + 1) * D)
                    o_ref[r0 + j:r0 + j + 1, 0:hi - lo] = ref[:, lo:hi]

    return pl.pallas_call(body, name="pack_replicated", out_shape=jax.ShapeDtypeStruct((REPL_TOTAL, D), F32),
                          compiler_params=_cparams())(*[grads[n] for n in names], loss_acc)


def _adam_replicated(g8, ws, ms, vs):
    rows = _repl_rows()
    names = [n for n, _ in REPL]
    np_ = len(names)

    def body(*refs):
        g_ref = refs[0]
        w_refs, m_refs, v_refs = refs[1:1 + np_], refs[1 + np_:1 + 2 * np_], refs[1 + 2 * np_:1 + 3 * np_]
        outs = refs[1 + 3 * np_:1 + 7 * np_]
        scr = refs[-1]
        g = g_ref[0]
        for k in range(1, N_DEV):
            g = g + g_ref[k]
        scr[...] = g
        refs[1 + 7 * np_][...] = scr[LOSS_ROW:LOSS_ROW + 1, 0:LANES]
        for i, name in enumerate(names):
            r0 = rows[name]
            nr, nc = w_refs[i].shape
            if nc <= D:
                gi = scr[r0:r0 + nr, 0:nc]
            else:
                parts = []
                for j in range((nc + D - 1) // D):
                    lo, hi = j * D, min(nc, (j + 1) * D)
                    parts.append(scr[r0 + j:r0 + j + 1, 0:hi - lo])
                gi = jnp.concatenate(parts, axis=1)
            d, mn, vn = _adamw(w_refs[i][...], gi, m_refs[i][...], v_refs[i][...])
            outs[i][...] = gi
            outs[np_ + i][...] = d
            outs[2 * np_ + i][...] = mn
            outs[3 * np_ + i][...] = vn

    shp = [jax.ShapeDtypeStruct(w.shape, F32) for w in ws]
    res = pl.pallas_call(body, name="adam_replicated", out_shape=shp * 4 + [jax.ShapeDtypeStruct((1, LANES), F32)],
                         scratch_shapes=[pltpu.VMEM((REPL_TOTAL, D), F32)], compiler_params=_cparams(),
                         )(g8, *ws, *ms, *vs)
    return [dict(zip(names, res[k * np_:(k + 1) * np_])) for k in range(4)], res[-1]


_WEIGHTS = ("attn_pre_norm", "w_in", "hgrn_lb", "hgrn_gnorm", "w_branch_a", "rwkv_mu", "rwkv_w0", "rwkv_w2",
            "rwkv_a0", "rwkv_a2", "rwkv_g2", "rwkv_k_k", "rwkv_k_a", "rwkv_r_k", "rwkv_ln_w", "rwkv_ln_b",
            "w_branch_b", "w_out", "attn_post_norm", "ffn_pre_norm", "w_up", "conv_w", "conv_b", "w_down",
            "ffn_post_norm")
_BIG = ("w_in", "w_up", "w_down", "w_branch_a", "w_branch_b", "w_out")


def _stages():
    one = [D]
    hw = HG_K * HG_PER_STEP
    rw = LANES * RW_PAIRS_PER_STEP
    return dict(
        mixers=_Stage("mixers", _f_mixers, 1, 2 * RW_CHUNK, [False] * 13, [[D] * 7 + [LANES, LANES]], [0],
                      [(hw, HG_K), (1, RW_COLS), (rw, LANES)], [one, one], [BF, BF],
                      kept_shapes=[(2 * RW_KEPT * RW_PAIRS_PER_STEP * 2 * RW_CHUNK, LANES)], f_kept=_f_mixers_kept),
        conv=_Stage("conv", _f_conv, 1, 128, [False, False], [[DFF, DFF]], [0], [(1, 2 * DFF), (1, 2 * DFF)],
                    [[DFF]], [BF]),
    )


def _cols_to_blocks(w, per):
    return w.reshape(w.shape[0], N_DEV, per).transpose(1, 0, 2)


def _blocks_to_cols(g):
    return g.transpose(1, 0, 2).reshape(g.shape[1], N_DEV * g.shape[2])


def kernel(x, attn_pre_norm, w_in, hgrn_lb, hgrn_gnorm, w_branch_a, rwkv_mu, rwkv_w0, rwkv_w2, rwkv_a0, rwkv_a2, rwkv_g2, rwkv_k_k, rwkv_k_a, rwkv_r_k, rwkv_ln_w, rwkv_ln_b, w_branch_b, w_out, attn_post_norm, ffn_pre_norm, w_up, conv_w, conv_b, w_down, ffn_post_norm, loss_target, m_attn_pre_norm, m_w_in, m_hgrn_lb, m_hgrn_gnorm, m_w_branch_a, m_rwkv_mu, m_rwkv_w0, m_rwkv_w2, m_rwkv_a0, m_rwkv_a2, m_rwkv_g2, m_rwkv_k_k, m_rwkv_k_a, m_rwkv_r_k, m_rwkv_ln_w, m_rwkv_ln_b, m_w_branch_b, m_w_out, m_attn_post_norm, m_ffn_pre_norm, m_w_up, m_conv_w, m_conv_b, m_w_down, m_ffn_post_norm, v_attn_pre_norm, v_w_in, v_hgrn_lb, v_hgrn_gnorm, v_w_branch_a, v_rwkv_mu, v_rwkv_w0, v_rwkv_w2, v_rwkv_a0, v_rwkv_a2, v_rwkv_g2, v_rwkv_k_k, v_rwkv_k_a, v_rwkv_r_k, v_rwkv_ln_w, v_rwkv_ln_b, v_w_branch_b, v_w_out, v_attn_post_norm, v_ffn_pre_norm, v_w_up, v_conv_w, v_conv_b, v_w_down, v_ffn_post_norm):
    w = dict(attn_pre_norm=attn_pre_norm, w_in=w_in, hgrn_lb=hgrn_lb, hgrn_gnorm=hgrn_gnorm, w_branch_a=w_branch_a, rwkv_mu=rwkv_mu, rwkv_w0=rwkv_w0, rwkv_w2=rwkv_w2, rwkv_a0=rwkv_a0, rwkv_a2=rwkv_a2, rwkv_g2=rwkv_g2, rwkv_k_k=rwkv_k_k, rwkv_k_a=rwkv_k_a, rwkv_r_k=rwkv_r_k, rwkv_ln_w=rwkv_ln_w, rwkv_ln_b=rwkv_ln_b, w_branch_b=w_branch_b, w_out=w_out, attn_post_norm=attn_post_norm, ffn_pre_norm=ffn_pre_norm, w_up=w_up, conv_w=conv_w, conv_b=conv_b, w_down=w_down, ffn_post_norm=ffn_post_norm)
    mo = dict(attn_pre_norm=m_attn_pre_norm, w_in=m_w_in, hgrn_lb=m_hgrn_lb, hgrn_gnorm=m_hgrn_gnorm, w_branch_a=m_w_branch_a, rwkv_mu=m_rwkv_mu, rwkv_w0=m_rwkv_w0, rwkv_w2=m_rwkv_w2, rwkv_a0=m_rwkv_a0, rwkv_a2=m_rwkv_a2, rwkv_g2=m_rwkv_g2, rwkv_k_k=m_rwkv_k_k, rwkv_k_a=m_rwkv_k_a, rwkv_r_k=m_rwkv_r_k, rwkv_ln_w=m_rwkv_ln_w, rwkv_ln_b=m_rwkv_ln_b, w_branch_b=m_w_branch_b, w_out=m_w_out, attn_post_norm=m_attn_post_norm, ffn_pre_norm=m_ffn_pre_norm, w_up=m_w_up, conv_w=m_conv_w, conv_b=m_conv_b, w_down=m_w_down, ffn_post_norm=m_ffn_post_norm)
    vo = dict(attn_pre_norm=v_attn_pre_norm, w_in=v_w_in, hgrn_lb=v_hgrn_lb, hgrn_gnorm=v_hgrn_gnorm, w_branch_a=v_w_branch_a, rwkv_mu=v_rwkv_mu, rwkv_w0=v_rwkv_w0, rwkv_w2=v_rwkv_w2, rwkv_a0=v_rwkv_a0, rwkv_a2=v_rwkv_a2, rwkv_g2=v_rwkv_g2, rwkv_k_k=v_rwkv_k_k, rwkv_k_a=v_rwkv_k_a, rwkv_r_k=v_rwkv_r_k, rwkv_ln_w=v_rwkv_ln_w, rwkv_ln_b=v_rwkv_ln_b, w_branch_b=v_w_branch_b, w_out=v_w_out, attn_post_norm=v_attn_post_norm, ffn_pre_norm=v_ffn_pre_norm, w_up=v_w_up, conv_w=v_conv_w, conv_b=v_conv_b, w_down=v_w_down, ffn_post_norm=v_ffn_post_norm)

    t = x.shape[1]
    x2 = x.reshape(t, D)
    tgt = loss_target.reshape(t, D)
    st = _stages()

    me = 4 * lax.axis_index("x") + 2 * lax.axis_index("y") + lax.axis_index("c")
    small = jnp.concatenate([rwkv_w2[0], rwkv_a2[0], rwkv_g2[0]], axis=0).astype(BF)
    g_in, g_small = _all_gather("gather_weights", [w_in[0].T.astype(BF), small])
    fw_in_t = g_in.reshape(IN_COLS, D)
    z64 = jnp.zeros((64, D), BF)
    w2p = jnp.concatenate([_blocks_to_cols(g_small[:, 0:64]), z64], axis=0)
    a2p = jnp.concatenate([z64, _blocks_to_cols(g_small[:, 64:128])], axis=0)
    g2f = _blocks_to_cols(g_small[:, 128:256])
    conv_bits = jnp.pad(lax.bitcast_convert_type(conv_w[0], BF).reshape(3, 2 * 704), ((0, 29), (0, 0)))
    late = [w_up[0].T.astype(BF)] + [w[k][0].astype(BF) for k in _BIG[2:]] + [conv_bits]
    late_gather = _Exchange("gather2", late)
    r_k = rwkv_r_k.reshape(1, D)

    xn, z = _norm_in_proj(x2, attn_pre_norm, fw_in_t, 512, 4736)
    mix_par = [hgrn_lb, hgrn_gnorm, rwkv_mu, rwkv_w0, w2p, rwkv_a0, a2p, g2f, rwkv_k_k, rwkv_k_a,
               rwkv_ln_w, rwkv_ln_b, r_k]
    mix_in = [z]
    (o_a, o_b), mix_saved = _stage_fwd(st["mixers"], t, mix_par, mix_in, hook=late_gather)
    gl = [lax.dynamic_update_slice(g, own[None], (me, 0, 0)) for g, own in zip(late_gather.results, late)]
    fw_up_t = gl[0].reshape(2 * DFF, D)
    fw_down = gl[1].reshape(DFF, D)
    fw_a, fw_b, fw_out = (g.reshape(D, D) for g in gl[2:5])
    conv_full = _blocks_to_cols(lax.bitcast_convert_type(gl[5][:, :3].reshape(N_DEV, 3, 704, 2), F32))
    y_a = _mm("branch_a", o_a, fw_a, "nn", BF)
    y_b = _mm("branch_b", o_b, fw_b, "nn", BF)
    merged, mix, h1, xn2 = _merge_out_post(z, y_a, y_b, fw_out, x2, attn_post_norm, ffn_pre_norm, 512)
    hu = _mm("up_proj", xn2, fw_up_t, "nt", F32, tm=1024, tn=1408)
    conv_par = [conv_full, conv_b]
    (act,), conv_saved = _stage_fwd(st["conv"], t, conv_par, [hu])

    loss_acc, d_ffn_post, dh1, dff = _down_loss(act, fw_down, ffn_post_norm, h1, tgt, 512)
    dact = _mm("d_act", dff, fw_down, "nt", BF, tm=1024, tn=1408)
    dw_down = _mm("dw_down", act, dff, "tn", BF, tm=1408, tn=512)
    (dcw, dcb), (dhu,) = _stage_bwd(st["conv"], t, conv_par, [hu], conv_saved, [[dact]], [BF])
    dw_up_t = _mm("dw_up", dhu, xn2, "tn", BF, tm=1408, tn=1024)
    d_post, d_pre2, dx_a, dmix = _dxn2_post1_bwd(dhu, fw_up_t, x2, mix, dh1, attn_post_norm, ffn_pre_norm, 512)
    dw_out = _mm("dw_out", merged, dmix, "tn", BF)
    dga, dgb, dy_a, dy_b = _dmerged_merge_bwd(dmix, fw_out, z, y_a, y_b, 512)
    do_a = _mm("d_oa", dy_a, fw_a, "nt", BF)
    dw_a = _mm("dw_a", o_a, dy_a, "tn", BF)
    do_b = _mm("d_ob", dy_b, fw_b, "nt", BF)
    dw_b = _mm("dw_b", o_b, dy_b, "tn", BF)
    early = [dw_up_t.reshape(N_DEV, 704, D), dw_down.reshape(N_DEV, 352, D), dw_a.reshape(N_DEV, 128, D),
             dw_b.reshape(N_DEV, 128, D), dw_out.reshape(N_DEV, 128, D), _cols_to_blocks(dcw.astype(BF), 704)]
    early_scatter = _Exchange("scatter", early)
    mix_dp, dz_hr = _stage_bwd(st["mixers"], t, mix_par, mix_in, mix_saved, [[do_a], [do_b]], [BF],
                               hook=early_scatter)
    d_lb, d_gn, d_mu, d_w0, d_w2p, d_a0, d_a2p, d_g2, d_kk, d_ka, d_lnw, d_lnb, d_rk = mix_dp
    dz = dz_hr + [dga, dgb]
    dw_in_t = _mm_cols_tn("dw_in", dz, xn, BF, 256)

    ax, ay, ac = lax.axis_index("x"), lax.axis_index("y"), lax.axis_index("c")
    idx4 = jnp.stack([4 * cx + 2 * cy + ac for cx, cy in ((ax, ay), (1 - ax, ay), (ax, 1 - ay), (1 - ax, 1 - ay))])
    idx4 = idx4.astype(jnp.int32)
    idx_me, idx_0 = idx4[0:1], jnp.zeros((1,), jnp.int32)
    d_small = jnp.concatenate([d_w2p[:64], d_a2p[64:], d_g2], axis=0).astype(BF)
    g8s = [dw_in_t.reshape(N_DEV, 1184, D), _cols_to_blocks(d_small, LANES)]
    recv4s = _reduce_pair(g8s)
    sums = [_pair_sum("pair_sum_" + n, idx4, g, r) for n, g, r in zip(("w_in", "small"), g8s, recv4s)]
    swap_ssem, swap_rsem, swap_srcs, swap_lands, token = _chip_swap_start([s[1] for s in sums])
    d_pre1, dx = _dxn_pre1_bwd(dz, fw_in_t, x2, dx_a, attn_pre_norm, 256, token)
    grad_x = dx.reshape(x.shape)

    rg = dict(attn_pre_norm=d_pre1, hgrn_lb=d_lb, hgrn_gnorm=d_gn, rwkv_mu=d_mu, rwkv_w0=d_w0, rwkv_a0=d_a0,
              rwkv_k_k=d_kk, rwkv_k_a=d_ka, rwkv_r_k=d_rk, rwkv_ln_w=d_lnw, rwkv_ln_b=d_lnb, attn_post_norm=d_post,
              ffn_pre_norm=d_pre2, conv_b=dcb, ffn_post_norm=d_ffn_post)
    g8 = _all_gather_small("gather_small_grads", _pack_replicated(rg, loss_acc))
    rnames = [n for n, _ in REPL]
    flat = lambda src: [src[n].reshape(1, D) if n == "rwkv_r_k" else src[n] for n in rnames]
    rp_out, loss_row = _adam_replicated(g8, flat(w), flat(mo), flat(vo))
    loss = loss_row[0, 0]
    recv3s = _chip_swap_wait(swap_ssem, swap_rsem, swap_srcs, swap_lands, rp_out[0]["attn_pre_norm"])
    for kind in range(4):
        rp_out[kind]["rwkv_r_k"] = rp_out[kind]["rwkv_r_k"].reshape(rwkv_r_k.shape)

    def small_of(src):
        return jnp.concatenate([src["rwkv_w2"][0], src["rwkv_a2"][0], src["rwkv_g2"][0]], axis=0)

    sh_out = [dict() for _ in range(4)]
    g_in = _sum_partials("sum_w_in", idx_0, sums[0][0][None], recv3s[0]).T
    res = _adam_sharded("adam_w_in", idx_0, g_in[None], None, *[src["w_in"][0] for src in (w, mo, vo)])
    res_s = _adam_sharded("adam_small", idx_0, sums[1][0][None], recv3s[1], *[small_of(src) for src in (w, mo, vo)])
    for kind in range(4):
        sh_out[kind]["w_in"] = res[kind][None]
        sh_out[kind]["rwkv_w2"] = res_s[kind][0:64][None]
        sh_out[kind]["rwkv_a2"] = res_s[kind][64:128][None]
        sh_out[kind]["rwkv_g2"] = res_s[kind][128:256][None]
    for n, own, recv in zip(_BIG[1:] + ("conv_w",), early, early_scatter.results):
        if n == "w_up":
            g_up = _sum_partials("sum_w_up", idx_me, own, recv).T
            res = _adam_sharded("adam_" + n, idx_0, g_up[None], None, *[src[n][0] for src in (w, mo, vo)])
        else:
            res = _adam_sharded("adam_" + n, idx_me, own, recv, *[src[n][0] for src in (w, mo, vo)])
        for kind in range(4):
            sh_out[kind][n] = res[kind][None]

    outs = [loss, grad_x]
    for kind in range(4):
        for name in _WEIGHTS:
            outs.append(sh_out[kind][name] if name in sh_out[kind] else rp_out[kind][name])
    return tuple(outs)
```

```python
import functools

import jax
import jax.numpy as jnp
from jax import lax
from jax.experimental import pallas as pl
from jax.experimental.pallas import tpu as pltpu

F32 = jnp.float32
BF = jnp.bfloat16
MESH = pl.DeviceIdType.MESH

D = 1024
HG_HEADS = 8
HG_K = 128
HG_CHUNK = 32
HG_SCALE = HG_K ** -0.5
HG_PER_STEP = 8
RW_HEADS = 16
RW_N = 64
RW_CHUNK = 64
RW_PAIRS_PER_STEP = 8
DFF = 2816
IN_COLS = 9472
RW_COLS = 3328
EPS = 1e-6
GN_EPS = 1e-5 * RW_N
ADAM_LR = 0.001
ADAM_B1 = 0.9
ADAM_B2 = 0.999
ADAM_EPS = 1e-08
ADAM_WD = 0.01
ADAM_STEP = 10
N_DEV = 8
LANES = 128
VMEM_LIMIT = 56 * 1024 * 1024
TILE_BYTES = 1280 * 1024

REPL = (("attn_pre_norm", 1024), ("hgrn_lb", 1024), ("hgrn_gnorm", 1024), ("rwkv_mu", 3328), ("rwkv_w0", 1024),
        ("rwkv_a0", 1024), ("rwkv_k_k", 1024), ("rwkv_k_a", 1024), ("rwkv_r_k", 1024), ("rwkv_ln_w", 1024),
        ("rwkv_ln_b", 1024), ("attn_post_norm", 1024), ("ffn_pre_norm", 1024), ("conv_b", 5632), ("ffn_post_norm", 1024))
REPL_ROWS = {"hgrn_lb": 2}
REPL_TOTAL = 32


def _cparams(sem=None, **kw):
    return pltpu.CompilerParams(dimension_semantics=sem, vmem_limit_bytes=VMEM_LIMIT, **kw)


_DN = {"nn": ((1,), (0,)), "nt": ((1,), (1,)), "tn": ((0,), (0,))}


def _raw_dot(a, b, mode):
    return lax.dot_general(a.astype(BF), b.astype(BF), (_DN[mode], ((), ())), preferred_element_type=F32)


@functools.partial(jax.custom_vjp, nondiff_argnums=(2,))
def _dot(a, b, mode):
    return _raw_dot(a, b, mode)


def _dot_fwd(a, b, mode):
    return _raw_dot(a, b, mode), (a, b)


def _dot_bwd(mode, res, g):
    a, b = res
    if mode == "nn":
        return _dot(g, b, "nt"), _dot(a, g, "tn")
    if mode == "nt":
        return _dot(g, b, "nn"), _dot(g, a, "tn")
    return _dot(b, g, "nt"), _dot(a, g, "nn")


_dot.defvjp(_dot_fwd, _dot_bwd)


def _bf_pieces(x, n):
    out, r = [], x
    for i in range(n):
        p = r.astype(BF)
        out.append(p)
        if i + 1 < n:
            r = r - p.astype(F32)
    return out


def _raw_split_dot(x, e, mode, n, x_left):
    eb = e.astype(BF)
    acc = None
    for p in _bf_pieces(x, n):
        ops = (p, eb) if x_left else (eb, p)
        t = lax.dot_general(*ops, (_DN[mode], ((), ())), preferred_element_type=F32)
        acc = t if acc is None else acc + t
    return acc


def _raw_headsum(x):
    t = x.shape[0]
    i = lax.broadcasted_iota(jnp.int32, (LANES, LANES), 0)
    j = lax.broadcasted_iota(jnp.int32, (LANES, LANES), 1)
    same = jnp.where((i >= RW_N) == (j >= RW_N), 1.0, 0.0).astype(F32)
    groups = x.shape[1] // LANES
    rows = jnp.concatenate([x[:, q * LANES:(q + 1) * LANES] for q in range(groups)], axis=0)
    s = _raw_split_dot(rows, same, "nn", 2, True)
    return jnp.concatenate([s[q * t:(q + 1) * t] for q in range(groups)], axis=1)


@jax.custom_vjp
def _headsum(x):
    return _raw_headsum(x)


def _headsum_fwd(x):
    return _raw_headsum(x), None


def _headsum_bwd(_, g):
    return (_raw_headsum(g),)


_headsum.defvjp(_headsum_fwd, _headsum_bwd)


@functools.partial(jax.custom_vjp, nondiff_argnums=(2,))
def _tdot(tri, x, n):
    return _raw_split_dot(x, tri, "nn", n, False)


def _tdot_fwd(tri, x, n):
    return _raw_split_dot(x, tri, "nn", n, False), tri


def _tdot_bwd(n, tri, g):
    return jnp.zeros_like(tri), _raw_split_dot(g, tri, "tn", n, False)


_tdot.defvjp(_tdot_fwd, _tdot_bwd)


def _row(x, i):
    r = lax.broadcasted_iota(jnp.int32, x.shape, 0)
    return jnp.sum(jnp.where(r == i, x, 0.0), axis=0, keepdims=True)


def _shift_down(x, prev):
    t = x.shape[0]

    @jax.custom_vjp
    def sh(x, prev):
        r = lax.broadcasted_iota(jnp.int32, x.shape, 0)
        return jnp.where(r == 0, prev, pltpu.roll(x, 1, 0))

    def fwd(x, prev):
        return sh(x, prev), None

    def bwd(_, g):
        r = lax.broadcasted_iota(jnp.int32, g.shape, 0)
        dx = jnp.where(r == t - 1, 0.0, pltpu.roll(g, t - 1, 0))
        return dx, jnp.sum(jnp.where(r == 0, g, 0.0), axis=0, keepdims=True)

    sh.defvjp(fwd, bwd)
    return sh(x, prev)


def _sigmoid(x):
    return jax.nn.sigmoid(x)


def _silu(x):
    return x * jax.nn.sigmoid(x)


def _softplus(x):
    return jnp.maximum(x, 0.0) + jnp.log(1.0 + jnp.exp(-jnp.abs(x)))


def _rms(x, g):
    return (x * lax.rsqrt(jnp.mean(x * x, axis=-1, keepdims=True) + EPS)) * g


def _tril(c):
    r = lax.broadcasted_iota(jnp.int32, (c, c), 0)
    cc = lax.broadcasted_iota(jnp.int32, (c, c), 1)
    return cc <= r


def _f_pre1_residual(ps, xs, cs):
    return [_rms(xs[0], ps[0]), xs[0]], []


def _f_hgrn(ps, xs, cs):
    lbraw, gn = ps
    hq, hf, hi, hg = xs
    hd = range(HG_PER_STEP)
    st = [cs[0][p * HG_K:(p + 1) * HG_K] for p in hd]
    l0, l1 = _row(lbraw, 0), _row(lbraw, 1)
    m = jnp.maximum(l0, l1)
    e0, e1 = jnp.exp(l0 - m), jnp.exp(l1 - m)
    lb = e0 / (e0 + e1)
    q = _silu(hq) * HG_SCALE
    f = lb + (1.0 - lb) * _sigmoid(hf)
    kh = 1.0 - f
    gl = jnp.log(f)
    c = HG_CHUNK
    low = _tril(c)
    tri = jnp.where(low, 1.0, 0.0).astype(F32)
    outs = []
    for i in range(hq.shape[0] // c):
        rows = slice(i * c, (i + 1) * c)
        b = _tdot(tri, gl[rows], 3)
        bref = _row(b, c // 2 - 1)
        blast = _row(b, c - 1)
        qi = q[rows] * jnp.exp(b - bref)
        ki = kh[rows] * jnp.exp(bref - b)
        qd = q[rows] * jnp.exp(b)
        kd = kh[rows] * jnp.exp(blast - b)
        dec = jnp.exp(blast)
        sl = [slice(p * HG_K, (p + 1) * HG_K) for p in hd]
        sc = [jnp.where(low, _dot(qi[:, sl[p]], ki[:, sl[p]], "nt"), 0.0) for p in hd]
        o = [_dot(sc[p], hi[rows, sl[p]], "nn") + _dot(qd[:, sl[p]], st[p], "nt") for p in hd]
        u = [_dot(hi[rows, sl[p]], kd[:, sl[p]], "tn") for p in hd]
        st = [dec[:, sl[p]] * st[p] + u[p] for p in hd]
        outs.append(jnp.concatenate(o, axis=1) if len(o) > 1 else o[0])
    o = outs[0] if len(outs) == 1 else jnp.concatenate(outs, axis=0)
    on = []
    for p in hd:
        op = o[:, p * HG_K:(p + 1) * HG_K]
        on.append(op * lax.rsqrt(jnp.mean(op * op, axis=-1, keepdims=True) + EPS))
    o = jnp.concatenate(on, axis=1) if len(on) > 1 else on[0]
    o = o * gn
    return [o * _silu(hg)], [jnp.concatenate(st, axis=0) if len(st) > 1 else st[0]]


_RW_OFFS = (0, 1024, 2048, 3072, 3200, 3328)


def _f_rwpre(ps, xs, cs):
    mu, w0, w2p, a0, a2p, g2, k_k, k_a = ps
    (prev,) = cs
    t = xs[0].shape[0]
    zs = []
    for i, z in enumerate(xs):
        lo, hi = _RW_OFFS[i], _RW_OFFS[i + 1]
        zs.append(z + mu[:, lo:hi] * (_shift_down(z, prev[:, lo:hi]) - z))
    rr, kr, vr, wa, gz = zs
    w_log = -_softplus(-(w0 + _dot(jnp.tanh(wa), w2p, "nn"))) - 0.5
    lw = -jnp.exp(w_log)
    a = _sigmoid(a0 + _dot(wa, a2p, "nn"))
    g = _dot(_sigmoid(gz), g2, "nn")
    kkr = kr * k_k
    kk = kkr / jnp.maximum(jnp.sqrt(_headsum(kkr * kkr)), 1e-12)
    k2 = kr * (1.0 + (a - 1.0) * k_a)
    newprev = jnp.concatenate([_row(z, t - 1) for z in xs], axis=1)
    return [rr, lw, k2, vr, -kk, kk * a, g], [newprev]


def _raw_inverses(ls):
    n = ls[0].shape[0]
    r = lax.broadcasted_iota(jnp.int32, (n, n), 0)
    c = lax.broadcasted_iota(jnp.int32, (n, n), 1)
    eye = jnp.where(r == c, 1.0, 0.0).astype(F32)
    tinv = [eye + l for l in ls]
    pw = ls
    for _ in range(5):
        pw = [_raw_dot(p, p, "nn") for p in pw]
        tinv = [t + _raw_dot(t, p, "nn") for t, p in zip(tinv, pw)]
    return tinv


@jax.custom_vjp
def _unit_lower_inverses(ls):
    return _raw_inverses(ls)


def _inverses_fwd(ls):
    tinv = _raw_inverses(ls)
    return tinv, tinv


def _inverses_bwd(tinv, gs):
    return ([_raw_dot(_raw_dot(t, g, "tn"), t, "nt") for t, g in zip(tinv, gs)],)


_unit_lower_inverses.defvjp(_inverses_fwd, _inverses_bwd)


@jax.custom_vjp
def _known_inverses(ls, tinv):
    return tinv


def _known_fwd(ls, tinv):
    return tinv, tinv


def _known_bwd(tinv, gs):
    return [_raw_dot(_raw_dot(t, g, "tn"), t, "nt") for t, g in zip(tinv, gs)], [jnp.zeros_like(t) for t in tinv]


_known_inverses.defvjp(_known_fwd, _known_bwd)


@jax.custom_vjp
def _use_kept(computed, kept):
    return kept


def _use_kept_fwd(computed, kept):
    return kept, None


def _use_kept_bwd(_, g):
    return g, jax.tree.map(jnp.zeros_like, g)


_use_kept.defvjp(_use_kept_fwd, _use_kept_bwd)

RW_KEPT = 5


def _f_rwscan(ps, xs, cs, kept=None):
    state = cs[0]
    ys, keep = [], []
    n = 2 * RW_CHUNK
    per_chunk = RW_KEPT * RW_PAIRS_PER_STEP * n
    for i in range(xs[0].shape[0] // RW_CHUNK):
        known = None
        if kept is not None:
            known = [[kept[i * per_chunk + (q * RW_PAIRS_PER_STEP + p) * n:
                           i * per_chunk + (q * RW_PAIRS_PER_STEP + p + 1) * n] for p in range(RW_PAIRS_PER_STEP)]
                     for q in range(RW_KEPT)]
        y, state, mats = _rwkv_chunk([x[i * RW_CHUNK:(i + 1) * RW_CHUNK] for x in xs], state, known)
        ys.append(y)
        keep += [m for group in mats for m in group]
    return [ys[0] if len(ys) == 1 else jnp.concatenate(ys, axis=0)], [state], jnp.concatenate(keep, axis=0)


def _rwkv_chunk(xs, state, known=None):
    npair = RW_PAIRS_PER_STEP
    pr = range(npair)
    r, lw, k, v, av, bv = [[x[:, p * LANES:(p + 1) * LANES] for p in pr] for x in xs]
    sv = [state[p * LANES:(p + 1) * LANES] for p in pr]
    c = RW_CHUNK
    n = 2 * c
    tri = jnp.where(_tril(c), 1.0, 0.0).astype(F32)
    cl = [_tdot(tri, lw[p], 3) for p in pr]
    cl_last = [_row(cl[p], c - 1) for p in pr]
    lane = lax.broadcasted_iota(jnp.int32, (c, LANES), 1)
    h0 = lane < RW_N

    def stack(x):
        return jnp.concatenate([jnp.where(h0, x, 0.0), jnp.where(h0, 0.0, x)], axis=0)

    am = [stack(av[p] * jnp.exp(cl[p] - lw[p])) for p in pr]
    bm = [stack(bv[p] * jnp.exp(-cl[p])) for p in pr]
    km = [stack(k[p] * jnp.exp(-cl[p])) for p in pr]
    rm = [stack(r[p] * jnp.exp(cl[p])) for p in pr]
    vm = [stack(v[p]) for p in pr]
    rn = lax.broadcasted_iota(jnp.int32, (n, n), 0)
    cn = lax.broadcasted_iota(jnp.int32, (n, n), 1)
    blk = (rn >= c) == (cn >= c)
    strict = blk & (cn < rn)
    incl = blk & (cn <= rn)
    lab = [jnp.where(strict, _dot(am[p], bm[p], "nt"), 0.0) for p in pr]
    lak = [jnp.where(strict, _dot(am[p], km[p], "nt"), 0.0) for p in pr]
    wrb = [jnp.where(incl, _dot(rm[p], bm[p], "nt"), 0.0) for p in pr]
    wrk = [jnp.where(incl, _dot(rm[p], km[p], "nt"), 0.0) for p in pr]
    if known is None:
        tinv = _unit_lower_inverses(lab)
    else:
        tinv = _known_inverses(lab, known[0])
        lak, wrb, wrk = _use_kept(lak, known[1]), _use_kept(wrb, known[2]), _use_kept(wrk, known[3])
    rhs = [_dot(am[p], sv[p], "nt") + _dot(lak[p], vm[p], "nn") for p in pr]
    um = [_dot(tinv[p], rhs[p], "nn") for p in pr]
    if known is not None:
        um = _use_kept(um, known[4])
    ym = [_dot(rm[p], sv[p], "nt") + _dot(wrb[p], um[p], "nn") + _dot(wrk[p], vm[p], "nn") for p in pr]
    sn = [(sv[p] + _dot(um[p], bm[p], "tn") + _dot(vm[p], km[p], "tn")) * jnp.exp(cl_last[p]) for p in pr]
    ys = [ym[p][:c] + ym[p][c:] for p in pr]
    return jnp.concatenate(ys, axis=1), jnp.concatenate(sn, axis=0), [tinv, lak, wrb, wrk, um]


def _f_mixers(ps, xs, cs):
    return _mixers(ps, xs, cs, None)


def _f_mixers_kept(ps, xs, cs, kept):
    return _mixers(ps, xs, cs, kept[0])[:2]


def _mixers(ps, xs, cs, kept):
    oa, st = _f_hgrn(ps[:2], xs[:4], cs[:1])
    (r, lw, k, v, av, bv, g), prev = _f_rwpre(ps[2:10], xs[4:], cs[1:2])
    y, sv, keep = _f_rwscan([], [r, lw, k, v, av, bv], cs[2:], kept)
    ob, _ = _f_rwpost(ps[10:], y + [r, k, v, g], [])
    return oa + ob, st + prev + sv, [keep]


def _f_rwpost(ps, xs, cs):
    ln_w, ln_b, r_k = ps
    y, r, k, v, g = xs
    inv_n = 1.0 / RW_N
    yc = y - _headsum(y) * inv_n
    var = _headsum(yc * yc) * inv_n
    yn = yc * lax.rsqrt(var + GN_EPS)
    yn = yn * ln_w + ln_b
    bonus = _headsum(r * k * r_k) * v
    return [(yn + bonus) * g], []


def _f_merge(ps, xs, cs):
    ga, gb, ya, yb = xs
    return [_sigmoid(ga) * ya + _sigmoid(gb) * yb], []


def _f_post1(ps, xs, cs):
    x, mix = xs
    h1 = x + _rms(mix, ps[0])
    return [h1, _rms(h1, ps[1])], []


def _f_conv(ps, xs, cs):
    cw, cb = ps
    p1, p2 = cs
    w0, w1, w2 = _row(cw, 0), _row(cw, 1), _row(cw, 2)
    t = xs[0].shape[0]
    hc = []
    for i, x in enumerate(xs):
        sl = slice(i * DFF, (i + 1) * DFF)
        s1 = _shift_down(x, p1[:, sl])
        s2 = _shift_down(s1, p2[:, sl])
        hc.append(cb[:, sl] + w0[:, sl] * s2 + w1[:, sl] * s1 + w2[:, sl] * x)
    n1 = jnp.concatenate([_row(x, t - 1) for x in xs], axis=1)
    n2 = jnp.concatenate([_row(x, t - 2) for x in xs], axis=1)
    return [_silu(hc[0]) * hc[1]], [n1, n2]


class _Stage:
    def __init__(self, name, f, g, tm, par_per_g, in_pieces, in_offs, carry_shapes, out_pieces, out_dtypes,
                 kept_shapes=(), f_kept=None):
        self.name, self.f, self.g, self.tm = name, f, g, tm
        self.par_per_g, self.in_pieces, self.in_offs = par_per_g, in_pieces, in_offs
        self.carry_shapes, self.out_pieces, self.out_dtypes = carry_shapes, out_pieces, out_dtypes
        self.kept_shapes, self.f_kept = list(kept_shapes), f_kept


def _par_spec(arr, per_g, g):
    r, c = arr.shape
    if per_g:
        return pl.BlockSpec((r, c // g), lambda gi, ni: (0, gi))
    return pl.BlockSpec((r, c), lambda gi, ni: (0, 0))


def _row_spec(tm, width, off, n, rev):
    if rev:
        return pl.BlockSpec((tm, width), lambda gi, ni: (n - 1 - ni, off + gi))
    return pl.BlockSpec((tm, width), lambda gi, ni: (ni, off + gi))


def _carry_spec(shape, n, rev):
    if rev:
        return pl.BlockSpec((None, None) + shape, lambda gi, ni: (gi, n - 1 - ni, 0, 0))
    return pl.BlockSpec((None, None) + shape, lambda gi, ni: (gi, ni, 0, 0))


def _load_pieces(refs, pieces_list):
    out = []
    for ref, pieces in zip(refs, pieces_list):
        o = 0
        for w in pieces:
            out.append(ref[:, o:o + w].astype(F32))
            o += w
    return out


def _store_pieces(refs, pieces_list, vals):
    k = 0
    for ref, pieces in zip(refs, pieces_list):
        o = 0
        for w in pieces:
            ref[:, o:o + w] = vals[k].astype(ref.dtype)
            k += 1
            o += w


_ANY = pl.BlockSpec(memory_space=pl.ANY)


class _Exchange:
    def __init__(self, kind, arrs):
        self.kind, self.arrs, self.results = kind, list(arrs), None
        if kind == "scatter":
            self.out_shape = [jax.ShapeDtypeStruct((N_DEV - 1,) + a.shape[1:], a.dtype) for a in self.arrs]
        else:
            self.out_shape = [jax.ShapeDtypeStruct((N_DEV,) + a.shape, a.dtype) for a in self.arrs]
        self.nsem = (N_DEV if kind == "gather2" else N_DEV - 1) * len(self.arrs)

    def copies(self, in_refs, out_refs, ssem, rsem):
        x, y, c = lax.axis_index("x"), lax.axis_index("y"), lax.axis_index("c")
        me = 4 * x + 2 * y + c
        cps = []
        for a, (i_ref, o_ref) in enumerate(zip(in_refs, out_refs)):
            for j in range(1, N_DEV):
                px = 1 - x if j & 4 else x
                py = 1 - y if j & 2 else y
                pc = 1 - c if j & 1 else c
                if self.kind == "gather":
                    src, dst = i_ref, o_ref.at[me]
                else:
                    src, dst = i_ref.at[4 * px + 2 * py + pc], o_ref.at[j - 1]
                s = (N_DEV - 1) * a + j - 1
                cps.append(pltpu.make_async_remote_copy(src_ref=src, dst_ref=dst, send_sem=ssem.at[s],
                                                        recv_sem=rsem.at[s], device_id=(px, py, pc),
                                                        device_id_type=MESH))
        return cps

    def run(self, step, total, in_refs, out_refs, ssem, rsem):
        if self.kind == "gather2":
            return self.run_two_level(step, total, in_refs, out_refs, ssem, rsem)

        @pl.when(step == 0)
        def _():
            for cp in self.copies(in_refs, out_refs, ssem, rsem):
                cp.start()

        @pl.when(step == total - 1)
        def _():
            for cp in self.copies(in_refs, out_refs, ssem, rsem):
                cp.wait()

    def run_two_level(self, step, total, in_refs, out_refs, ssem, rsem):
        x, y, c = lax.axis_index("x"), lax.axis_index("y"), lax.axis_index("c")
        sibling, xn, yn = (x, y, 1 - c), (1 - x, y, c), (x, 1 - y, c)
        arrs = range(len(in_refs))
        ns = N_DEV

        def num(px, py, pc):
            return 4 * px + 2 * py + pc

        def copy(a, k, to, src, dst):
            return pltpu.make_async_remote_copy(src_ref=src, dst_ref=dst, send_sem=ssem.at[ns * a + k],
                                                recv_sem=rsem.at[ns * a + k], device_id=to, device_id_type=MESH)

        def blk(a, b):
            return out_refs[a].at[b]

        def half(a, b, second):
            h = self.arrs[a].shape[0] // 2
            return out_refs[a].at[b, pl.ds(h if second else 0, h)]

        bx, by, bd = num(1 - x, y, c), num(x, 1 - y, c), num(1 - x, 1 - y, c)

        def firsts(a):
            own = blk(a, num(x, y, c))
            return [copy(a, 0, sibling, in_refs[a], own), copy(a, 1, xn, in_refs[a], own),
                    copy(a, 2, yn, in_refs[a], own)]

        def seconds(a):
            return [copy(a, 3, yn, half(a, bx, False), half(a, bx, False)), copy(a, 5, sibling, blk(a, bx), blk(a, bx)),
                    copy(a, 4, xn, half(a, by, True), half(a, by, True)), copy(a, 6, sibling, blk(a, by), blk(a, by))]

        def third(a):
            return copy(a, 7, sibling, blk(a, bd), blk(a, bd))

        @pl.when(step == 0)
        def _():
            for a in arrs:
                for cp in firsts(a):
                    cp.start()

        @pl.when(step == total // 2)
        def _():
            for a in arrs:
                copy(a, 1, xn, blk(a, bx), blk(a, bx)).wait_recv()
                copy(a, 2, yn, blk(a, by), blk(a, by)).wait_recv()
                for cp in seconds(a):
                    cp.start()

        @pl.when(step == (4 * total) // 5)
        def _():
            for a in arrs:
                copy(a, 3, yn, half(a, bd, False), half(a, bd, False)).wait_recv()
                copy(a, 4, xn, half(a, bd, True), half(a, bd, True)).wait_recv()
                third(a).start()

        @pl.when(step == total - 1)
        def _():
            for a in arrs:
                for k, b in ((0, num(x, y, 1 - c)), (5, num(1 - x, y, 1 - c)), (6, num(x, 1 - y, 1 - c)),
                             (7, num(1 - x, 1 - y, 1 - c))):
                    copy(a, k, sibling, blk(a, b), blk(a, b)).wait_recv()
                for cp in firsts(a) + seconds(a) + [third(a)]:
                    cp.wait_send()


def _hook_specs(hook):
    if hook is None:
        return [], [], [], []
    na = len(hook.arrs)
    sems = [pltpu.SemaphoreType.DMA((hook.nsem,)), pltpu.SemaphoreType.DMA((hook.nsem,))]
    return [_ANY] * na, [_ANY] * na, hook.out_shape, sems


def _stage_fwd(st, t, params, inputs, hook=None):
    g, tm = st.g, min(st.tm, t)
    n = t // tm
    npar, nin, ncar, nout = len(params), len(inputs), len(st.carry_shapes), len(st.out_pieces)
    nk = len(st.kept_shapes)
    h_in, h_out, h_shape, h_sems = _hook_specs(hook)
    nh = len(h_in)

    def body(*refs):
        p_refs = refs[:npar]
        x_refs = refs[npar:npar + nin]
        hi_refs = refs[npar + nin:npar + nin + nh]
        o = npar + nin + nh
        o_refs = refs[o:o + nout]
        s_refs = refs[o + nout:o + nout + ncar]
        k_refs = refs[o + nout + ncar:o + nout + ncar + nk]
        o += nout + ncar + nk
        ho_refs = refs[o:o + nh]
        c_scr = refs[o + nh:o + nh + ncar]
        gi, ni = pl.program_id(0), pl.program_id(1)
        if hook is not None:
            step = gi * n + ni
            hook.run(step, g * n, hi_refs, ho_refs, *refs[-2:])

        @pl.when(ni == 0)
        def _():
            for c in c_scr:
                c[...] = jnp.zeros(c.shape, F32)

        ps = [r[...].astype(F32) for r in p_refs]
        xs = _load_pieces(x_refs, st.in_pieces)
        cs = [c[...] for c in c_scr]
        for s, c in zip(s_refs, cs):
            s[...] = c
        res = st.f(ps, xs, cs)
        outs, ncs = res[0], res[1]
        _store_pieces(o_refs, st.out_pieces, outs)
        for c, v in zip(c_scr, ncs):
            c[...] = v
        for kr, kv in zip(k_refs, res[2] if nk else []):
            kr[...] = kv.astype(kr.dtype)

    in_specs = [_par_spec(p, pg, g) for p, pg in zip(params, st.par_per_g)]
    in_specs += [_row_spec(tm, sum(pc), off, n, False) for pc, off in zip(st.in_pieces, st.in_offs)]
    out_specs = [_row_spec(tm, sum(pc), 0, n, False) for pc in st.out_pieces]
    out_specs += [_carry_spec(s, n, False) for s in st.carry_shapes]
    out_specs += [pl.BlockSpec(s, lambda gi, ni: (ni, 0)) for s in st.kept_shapes]
    out_shape = [jax.ShapeDtypeStruct((t, g * sum(pc)), dt) for pc, dt in zip(st.out_pieces, st.out_dtypes)]
    out_shape += [jax.ShapeDtypeStruct((g, n) + s, F32) for s in st.carry_shapes]
    out_shape += [jax.ShapeDtypeStruct((n * s[0], s[1]), BF) for s in st.kept_shapes]
    res = pl.pallas_call(
        body, name=st.name + "_fwd", grid=(g, n), in_specs=in_specs + h_in, out_specs=out_specs + h_out,
        out_shape=out_shape + h_shape,
        scratch_shapes=[pltpu.VMEM(s, F32) for s in st.carry_shapes] + h_sems,
        compiler_params=_cparams(("arbitrary", "arbitrary")),
    )(*params, *inputs, *(hook.arrs if hook else []))
    if hook is not None:
        hook.results = list(res[nout + ncar + nk:])
    return list(res[:nout]), list(res[nout:nout + ncar + nk])


def _stage_bwd(st, t, params, inputs, saved, douts, dx_dtypes, hook=None):
    g, tm = st.g, min(st.tm, t)
    n = t // tm
    npar, nin, ncar = len(params), len(inputs), len(st.carry_shapes)
    nk = len(st.kept_shapes)
    flat_d = [d for ds in douts for d in ds]
    nd = len(flat_d)
    dx_idx = [i for i, dt in enumerate(dx_dtypes) if dt is not None]
    h_in, h_out, h_shape, h_sems = _hook_specs(hook)
    nh = len(h_in)

    def body(*refs):
        p_refs = refs[:npar]
        x_refs = refs[npar:npar + nin]
        s_refs = refs[npar + nin:npar + nin + ncar]
        k_refs = refs[npar + nin + ncar:npar + nin + ncar + nk]
        o = npar + nin + ncar + nk
        d_refs = refs[o:o + nd]
        hi_refs = refs[o + nd:o + nd + nh]
        o += nd + nh
        dp_refs = refs[o:o + npar]
        dx_refs = refs[o + npar:o + npar + len(dx_idx)]
        ho_refs = refs[o + npar + len(dx_idx):o + npar + len(dx_idx) + nh]
        dc_scr = refs[o + npar + len(dx_idx) + nh:o + npar + len(dx_idx) + nh + ncar]
        gi, ni = pl.program_id(0), pl.program_id(1)
        if hook is not None:
            step = gi * n + ni
            hook.run(step, g * n, hi_refs, ho_refs, *refs[-2:])

        @pl.when(ni == 0)
        def _():
            for c in dc_scr:
                c[...] = jnp.zeros(c.shape, F32)

        ps = [r[...].astype(F32) for r in p_refs]
        xs = _load_pieces(x_refs, st.in_pieces)
        cs = [s[...] for s in s_refs]
        dys = []
        k = 0
        for ds, pieces in zip(douts, st.out_pieces):
            acc = _load_pieces([d_refs[k]], [pieces])
            for j in range(1, len(ds)):
                more = _load_pieces([d_refs[k + j]], [pieces])
                acc = [a + b for a, b in zip(acc, more)]
            dys += acc
            k += len(ds)
        if nk:
            kept = [r[...].astype(F32) for r in k_refs]
            _, vjp = jax.vjp(lambda p, x, c: st.f_kept(p, x, c, kept), ps, xs, cs)
        else:
            _, vjp = jax.vjp(st.f, ps, xs, cs)
        dps, dxs, dcs = vjp((dys, [c[...] for c in dc_scr]))
        k = 0
        per_in = []
        for pieces in st.in_pieces:
            per_in.append(dxs[k:k + len(pieces)])
            k += len(pieces)
        for ref, i in zip(dx_refs, dx_idx):
            _store_pieces([ref], [st.in_pieces[i]], per_in[i])
        for c, v in zip(dc_scr, dcs):
            c[...] = v
        for ref, dp, pg in zip(dp_refs, dps, st.par_per_g):
            first = (ni == 0) if pg else ((ni == 0) & (gi == 0))

            @pl.when(first)
            def _():
                ref[...] = jnp.zeros(ref.shape, F32)

            ref[...] += dp

    in_specs = [_par_spec(p, pg, g) for p, pg in zip(params, st.par_per_g)]
    in_specs += [_row_spec(tm, sum(pc), off, n, True) for pc, off in zip(st.in_pieces, st.in_offs)]
    in_specs += [_carry_spec(s, n, True) for s in st.carry_shapes]
    in_specs += [pl.BlockSpec(s, lambda gi, ni: (n - 1 - ni, 0)) for s in st.kept_shapes]
    for ds, pc in zip(douts, st.out_pieces):
        in_specs += [_row_spec(tm, sum(pc), 0, n, True) for _ in ds]
    out_specs = [_par_spec(p, pg, g) for p, pg in zip(params, st.par_per_g)]
    out_specs += [_row_spec(tm, sum(st.in_pieces[i]), 0, n, True) for i in dx_idx]
    out_shape = [jax.ShapeDtypeStruct(p.shape, F32) for p in params]
    out_shape += [jax.ShapeDtypeStruct((t, g * sum(st.in_pieces[i])), dx_dtypes[i]) for i in dx_idx]
    res = pl.pallas_call(
        body, name=st.name + "_bwd", grid=(g, n), in_specs=in_specs + h_in, out_specs=out_specs + h_out,
        out_shape=out_shape + h_shape,
        scratch_shapes=[pltpu.VMEM(s, F32) for s in st.carry_shapes] + h_sems,
        compiler_params=_cparams(("arbitrary", "arbitrary")),
    )(*params, *inputs, *saved, *flat_d, *(hook.arrs if hook else []))
    if hook is not None:
        hook.results = list(res[npar + len(dx_idx):])
    return list(res[:npar]), list(res[npar:npar + len(dx_idx)])


def _pick(n, cap):
    if n <= cap:
        return n
    best = LANES
    for k in range(1, n // LANES + 1):
        if (n // LANES) % k == 0 and k * LANES <= cap:
            best = k * LANES
    return best


def _mm(name, a, b, mode, out_dtype=F32, tm=1024, tn=512, b_outer=False):
    m = a.shape[1] if mode == "tn" else a.shape[0]
    k = a.shape[0] if mode == "tn" else a.shape[1]
    n = b.shape[0] if mode == "nt" else b.shape[1]
    tm, tn = _pick(m, tm), _pick(n, tn)
    if b_outer:
        grid = (n // tn, m // tm)
        ij = lambda p, q: (q, p)
    else:
        grid = (m // tm, n // tn)
        ij = lambda p, q: (p, q)

    def body(a_ref, b_ref, o_ref):
        o_ref[...] = _raw_dot(a_ref[...], b_ref[...], mode).astype(o_ref.dtype)

    if mode == "tn":
        a_spec = pl.BlockSpec((k, tm), lambda p, q: (0, ij(p, q)[0]))
    else:
        a_spec = pl.BlockSpec((tm, k), lambda p, q: (ij(p, q)[0], 0))
    b_mode = dict(pipeline_mode=pl.Buffered(1)) if tn == n else {}
    if mode == "nt":
        b_spec = pl.BlockSpec((tn, k), lambda p, q: (ij(p, q)[1], 0), **b_mode)
    else:
        b_spec = pl.BlockSpec((k, tn), lambda p, q: (0, ij(p, q)[1]), **b_mode)
    return pl.pallas_call(
        body, name=name, grid=grid, in_specs=[a_spec, b_spec],
        out_specs=pl.BlockSpec((tm, tn), lambda p, q: ij(p, q)),
        out_shape=jax.ShapeDtypeStruct((m, n), out_dtype),
        compiler_params=_cparams(("arbitrary", "arbitrary")),
    )(a, b)


def _mm_multi(name, pairs, mode, out_dtype, tm=1024, tn=512):
    a0, b0 = pairs[0]
    m = a0.shape[1] if mode == "tn" else a0.shape[0]
    k = a0.shape[0] if mode == "tn" else a0.shape[1]
    n = b0.shape[0] if mode == "nt" else b0.shape[1]
    tm, tn = _pick(m, tm), _pick(n, tn)
    npair = len(pairs)

    def body(*refs):
        for p in range(npair):
            refs[2 * npair + p][...] = _raw_dot(refs[2 * p][...], refs[2 * p + 1][...], mode).astype(out_dtype)

    a_spec = pl.BlockSpec((k, tm), lambda i, j: (0, i)) if mode == "tn" else pl.BlockSpec((tm, k), lambda i, j: (i, 0))
    b_spec = pl.BlockSpec((tn, k), lambda i, j: (j, 0)) if mode == "nt" else pl.BlockSpec((k, tn), lambda i, j: (0, j))
    return pl.pallas_call(
        body, name=name, grid=(m // tm, n // tn), in_specs=[a_spec, b_spec] * npair,
        out_specs=[pl.BlockSpec((tm, tn), lambda i, j: (i, j))] * npair,
        out_shape=[jax.ShapeDtypeStruct((m, n), out_dtype)] * npair,
        compiler_params=_cparams(("arbitrary", "arbitrary")),
    )(*[x for pair in pairs for x in pair])


def _mm_cols_tn(name, pieces, b, out_dtype, tm):
    k, n = b.shape
    counts = [p.shape[1] // tm for p in pieces]
    starts = [sum(counts[:i]) for i in range(len(pieces))]
    na = len(pieces)

    def body(*refs):
        b_ref, o_ref = refs[na], refs[-1]
        i = pl.program_id(0)
        for a_ref, s, c in zip(refs[:na], starts, counts):
            @pl.when((i >= s) & (i < s + c))
            def _():
                o_ref[...] = _raw_dot(a_ref[...], b_ref[...], "tn").astype(o_ref.dtype)

    def spec(s, c):
        return pl.BlockSpec((k, tm), lambda i: (0, jnp.clip(i - s, 0, c - 1)))

    return pl.pallas_call(
        body, name=name, grid=(sum(counts),),
        in_specs=[spec(s, c) for s, c in zip(starts, counts)]
        + [pl.BlockSpec(b.shape, lambda i: (0, 0), pipeline_mode=pl.Buffered(1))],
        out_specs=pl.BlockSpec((tm, n), lambda i: (i, 0)),
        out_shape=jax.ShapeDtypeStruct((sum(counts) * tm, n), out_dtype),
        compiler_params=_cparams(("arbitrary",)),
    )(*pieces, b)


def _norm_in_proj(x, g, w_t, tm, tn):
    t, k = x.shape
    n = w_t.shape[0]
    tm, tn = _pick(t, tm), _pick(n, tn)

    def body(x_ref, g_ref, w_ref, xn_ref, z_ref):
        xn = _rms(x_ref[...], g_ref[...]).astype(BF)
        xn_ref[...] = xn
        z_ref[...] = _raw_dot(xn, w_ref[...], "nt")

    xns, z = pl.pallas_call(
        body, name="in_proj", grid=(n // tn, t // tm),
        in_specs=[pl.BlockSpec((tm, k), lambda j, i: (i, 0)), pl.BlockSpec((1, k), lambda j, i: (0, 0)),
                  pl.BlockSpec((tn, k), lambda j, i: (j, 0))],
        out_specs=[pl.BlockSpec((None, tm, k), lambda j, i: (j, i, 0)), pl.BlockSpec((tm, tn), lambda j, i: (i, j))],
        out_shape=[jax.ShapeDtypeStruct((n // tn, t, k), BF), jax.ShapeDtypeStruct((t, n), F32)],
        compiler_params=_cparams(("arbitrary", "arbitrary")),
    )(x, g, w_t)
    return xns[0], z


def _merge_out_post(z, y_a, y_b, w_out, x, g_post, g_pre2, tm):
    t = x.shape[0]
    tm = _pick(t, tm)
    w = 256
    npc = D // w
    ga0, gb0 = (IN_COLS - 2 * D) // w, (IN_COLS - D) // w

    def body(*refs):
        ga_refs, gb_refs = refs[:npc], refs[npc:2 * npc]
        ya_ref, yb_ref, w_ref, x_ref, gp_ref, g2_ref, m_ref, mix_ref, h_ref, xn_ref = refs[2 * npc:]
        parts = []
        for p in range(npc):
            cols = slice(p * w, (p + 1) * w)
            parts.append(_sigmoid(ga_refs[p][...]) * ya_ref[:, cols].astype(F32)
                         + _sigmoid(gb_refs[p][...]) * yb_ref[:, cols].astype(F32))
        merged = jnp.concatenate(parts, axis=1).astype(BF)
        m_ref[...] = merged
        mix = _raw_dot(merged, w_ref[...], "nn")
        mix_ref[...] = mix
        h1 = x_ref[...] + _rms(mix, gp_ref[...])
        h_ref[...] = h1
        xn_ref[...] = _rms(h1, g2_ref[...]).astype(BF)

    row = pl.BlockSpec((tm, D), lambda i: (i, 0))
    one = pl.BlockSpec((1, D), lambda i: (0, 0))

    def gate(b0):
        return [pl.BlockSpec((tm, w), functools.partial(lambda i, b: (i, b), b=b0 + p)) for p in range(npc)]

    return pl.pallas_call(
        body, name="merge_out_post", grid=(t // tm,),
        in_specs=gate(ga0) + gate(gb0) + [row, row, pl.BlockSpec((D, D), lambda i: (0, 0), pipeline_mode=pl.Buffered(1)),
                                          row, one, one],
        out_specs=[row, row, row, row],
        out_shape=[jax.ShapeDtypeStruct((t, D), BF), jax.ShapeDtypeStruct((t, D), F32),
                   jax.ShapeDtypeStruct((t, D), F32), jax.ShapeDtypeStruct((t, D), BF)],
        compiler_params=_cparams(("arbitrary",)),
    )(*([z] * (2 * npc)), y_a, y_b, w_out, x, g_post, g_pre2)


def _accumulate(ni, refs, vals):
    @pl.when(ni == 0)
    def _():
        for r in refs:
            r[...] = jnp.zeros(r.shape, F32)

    for r, v in zip(refs, vals):
        r[...] += v


def _dmerged_merge_bwd(dmix, w_out, z, y_a, y_b, tm):
    t = dmix.shape[0]
    tm = _pick(t, tm)
    w = 256
    npc = D // w
    ga0, gb0 = (IN_COLS - 2 * D) // w, (IN_COLS - D) // w

    def body(*refs):
        dm_ref, w_ref = refs[0], refs[1]
        ga_refs, gb_refs = refs[2:2 + npc], refs[2 + npc:2 + 2 * npc]
        ya_ref, yb_ref, dga_ref, dgb_ref, dya_ref, dyb_ref = refs[2 + 2 * npc:]
        dmerged = _raw_dot(dm_ref[...], w_ref[...], "nt")
        for p in range(npc):
            cols = slice(p * w, (p + 1) * w)
            xs = [ga_refs[p][...], gb_refs[p][...], ya_ref[:, cols].astype(F32), yb_ref[:, cols].astype(F32)]
            _, vjp = jax.vjp(lambda *a: _f_merge([], list(a), [])[0][0], *xs)
            dga, dgb, dya, dyb = vjp(dmerged[:, cols])
            dga_ref[:, cols] = dga.astype(BF)
            dgb_ref[:, cols] = dgb.astype(BF)
            dya_ref[:, cols] = dya.astype(BF)
            dyb_ref[:, cols] = dyb.astype(BF)

    row = pl.BlockSpec((tm, D), lambda i: (i, 0))

    def gate(b0):
        return [pl.BlockSpec((tm, w), functools.partial(lambda i, b: (i, b), b=b0 + p)) for p in range(npc)]

    return pl.pallas_call(
        body, name="merge_bwd", grid=(t // tm,),
        in_specs=[row, pl.BlockSpec((D, D), lambda i: (0, 0), pipeline_mode=pl.Buffered(1))] + gate(ga0) + gate(gb0)
        + [row, row],
        out_specs=[row] * 4, out_shape=[jax.ShapeDtypeStruct((t, D), BF)] * 4,
        compiler_params=_cparams(("arbitrary",)),
    )(dmix, w_out, *([z] * (2 * npc)), y_a, y_b)


def _dxn2_post1_bwd(dhu, w_up_t, x, mix, dh1, g_post, g_pre2, tm):
    t, k = dhu.shape
    tm = _pick(t, tm)

    def body(a_ref, w_ref, x_ref, m_ref, dh_ref, gp_ref, g2_ref, dgp_ref, dg2_ref, dx_ref, dm_ref):
        dxn2 = _raw_dot(a_ref[...], w_ref[...], "nn")
        _, vjp = jax.vjp(lambda gp, g2, xx, mm: _f_post1([gp, g2], [xx, mm], [])[0],
                         gp_ref[...], g2_ref[...], x_ref[...], m_ref[...])
        dgp, dg2, dx, dm = vjp([dh_ref[...], dxn2])
        _accumulate(pl.program_id(0), [dgp_ref, dg2_ref], [dgp, dg2])
        dx_ref[...] = dx
        dm_ref[...] = dm.astype(BF)

    row = pl.BlockSpec((tm, D), lambda i: (i, 0))
    one = pl.BlockSpec((1, D), lambda i: (0, 0))
    return pl.pallas_call(
        body, name="post1_bwd", grid=(t // tm,),
        in_specs=[pl.BlockSpec((tm, k), lambda i: (i, 0)),
                  pl.BlockSpec((k, D), lambda i: (0, 0), pipeline_mode=pl.Buffered(1)), row, row, row, one, one],
        out_specs=[one, one, row, row],
        out_shape=[jax.ShapeDtypeStruct((1, D), F32), jax.ShapeDtypeStruct((1, D), F32),
                   jax.ShapeDtypeStruct((t, D), F32), jax.ShapeDtypeStruct((t, D), BF)],
        compiler_params=_cparams(("arbitrary",)),
    )(dhu, w_up_t, x, mix, dh1, g_post, g_pre2)


def _dxn_pre1_bwd(pieces, w_t, x, dx_res, g, tm, token):
    t = x.shape[0]
    tm = _pick(t, tm)
    offs = [sum(p.shape[1] for p in pieces[:i]) for i in range(len(pieces))]
    na = len(pieces)

    def body(*refs):
        w_ref, x_ref, r_ref, g_ref = refs[na:na + 4]
        dg_ref, dx_ref = refs[-2:]
        dxn = None
        for a_ref, off in zip(refs[:na], offs):
            part = _raw_dot(a_ref[...], w_ref[off:off + a_ref.shape[1], :], "nn")
            dxn = part if dxn is None else dxn + part
        _, vjp = jax.vjp(lambda gg, xx: _f_pre1_residual([gg], [xx], [])[0], g_ref[...], x_ref[...])
        dg, dx = vjp([dxn, r_ref[...]])
        _accumulate(pl.program_id(0), [dg_ref], [dg])
        dx_ref[...] = dx

    row = pl.BlockSpec((tm, D), lambda i: (i, 0))
    one = pl.BlockSpec((1, D), lambda i: (0, 0))
    return pl.pallas_call(
        body, name="pre1_bwd", grid=(t // tm,),
        in_specs=[pl.BlockSpec((tm, p.shape[1]), lambda i: (i, 0)) for p in pieces]
        + [pl.BlockSpec(w_t.shape, lambda i: (0, 0), pipeline_mode=pl.Buffered(1)), row, row, one,
           pl.BlockSpec(token.shape, lambda i: (0, 0))],
        out_specs=[one, row],
        out_shape=[jax.ShapeDtypeStruct((1, D), F32), jax.ShapeDtypeStruct((t, D), F32)],
        compiler_params=_cparams(("arbitrary",)),
    )(*pieces, w_t, x, dx_res, g, token)


def _down_loss(act, w_down, g_post, h1, tgt, tm):
    t, k = act.shape
    tm = _pick(t, tm)

    def body(a_ref, w_ref, g_ref, h_ref, t_ref, loss_ref, dg_ref, dh_ref, df_ref):
        ni = pl.program_id(0)
        ff = _raw_dot(a_ref[...], w_ref[...], "nn")
        target = t_ref[...]

        def lossf(g, h1, ff):
            e = h1 + _rms(ff, g) - target
            return 0.5 * jnp.sum(jnp.mean(e * e, axis=-1))

        l, (dg, dh, df) = jax.value_and_grad(lossf, argnums=(0, 1, 2))(g_ref[...], h_ref[...], ff)

        @pl.when(ni == 0)
        def _():
            loss_ref[...] = jnp.zeros(loss_ref.shape, F32)
            dg_ref[...] = jnp.zeros(dg_ref.shape, F32)

        loss_ref[...] += jnp.full(loss_ref.shape, l, F32)
        dg_ref[...] += dg
        dh_ref[...] = dh
        df_ref[...] = df.astype(df_ref.dtype)

    row = pl.BlockSpec((tm, D), lambda ni: (ni, 0))
    one = pl.BlockSpec((1, D), lambda ni: (0, 0))
    return pl.pallas_call(
        body, name="down_loss", grid=(t // tm,),
        in_specs=[pl.BlockSpec((tm, k), lambda ni: (ni, 0)),
                  pl.BlockSpec((k, D), lambda ni: (0, 0), pipeline_mode=pl.Buffered(1)), one, row, row],
        out_specs=[pl.BlockSpec((1, LANES), lambda ni: (0, 0)), one, row, row],
        out_shape=[jax.ShapeDtypeStruct((1, LANES), F32), jax.ShapeDtypeStruct((1, D), F32),
                   jax.ShapeDtypeStruct((t, D), F32), jax.ShapeDtypeStruct((t, D), BF)],
        compiler_params=_cparams(("arbitrary",)),
    )(act, w_down, g_post, h1, tgt)


_ANY = pl.BlockSpec(memory_space=pl.ANY)


def _all_gather(name, blks):
    na = len(blks)
    ns = 8

    def body(*refs):
        x_refs, out_refs = refs[:na], refs[na:2 * na]
        send_sems, recv_sems, local_sems = refs[2 * na:]
        x, y, cc = lax.axis_index("x"), lax.axis_index("y"), lax.axis_index("c")
        sibling, xn, yn = (x, y, 1 - cc), (1 - x, y, cc), (x, 1 - y, cc)

        def num(px, py, pc):
            return 4 * px + 2 * py + pc

        def copy(a, k, to, src, dst):
            return pltpu.make_async_remote_copy(src_ref=src, dst_ref=dst, send_sem=send_sems.at[ns * a + k],
                                                recv_sem=recv_sems.at[ns * a + k], device_id=to, device_id_type=MESH)

        def halves(a, blk):
            h = blks[a].shape[0] // 2
            return out_refs[a].at[blk, pl.ds(0, h)], out_refs[a].at[blk, pl.ds(h, h)]

        mine, sends = [], []
        for a in range(na):
            o = out_refs[a]
            m = pltpu.make_async_copy(x_refs[a], o.at[num(x, y, cc)], local_sems.at[a])
            m.start()
            mine.append(m)
            own = o.at[num(x, y, cc)]
            sends.append([copy(a, 0, sibling, x_refs[a], own), copy(a, 1, xn, x_refs[a], own),
                          copy(a, 2, yn, x_refs[a], own)])
            for cp in sends[a]:
                cp.start()
        for a in range(na):
            o = out_refs[a]
            bx, by, bd = num(1 - x, y, cc), num(x, 1 - y, cc), num(1 - x, 1 - y, cc)
            copy(a, 1, xn, o.at[bx], o.at[bx]).wait_recv()
            more = [copy(a, 3, yn, halves(a, bx)[0], halves(a, bx)[0]), copy(a, 5, sibling, o.at[bx], o.at[bx])]
            for cp in more:
                cp.start()
            sends[a] += more
        for a in range(na):
            o = out_refs[a]
            bx, by, bd = num(1 - x, y, cc), num(x, 1 - y, cc), num(1 - x, 1 - y, cc)
            copy(a, 2, yn, o.at[by], o.at[by]).wait_recv()
            more = [copy(a, 4, xn, halves(a, by)[1], halves(a, by)[1]), copy(a, 6, sibling, o.at[by], o.at[by])]
            for cp in more:
                cp.start()
            sends[a] += more
        for a in range(na):
            o = out_refs[a]
            bd = num(1 - x, 1 - y, cc)
            copy(a, 3, yn, halves(a, bd)[0], halves(a, bd)[0]).wait_recv()
            copy(a, 4, xn, halves(a, bd)[1], halves(a, bd)[1]).wait_recv()
            fw = copy(a, 7, sibling, o.at[bd], o.at[bd])
            fw.start()
            sends[a].append(fw)
        for a in range(na):
            o = out_refs[a]
            for k, blk in ((0, num(x, y, 1 - cc)), (5, num(1 - x, y, 1 - cc)), (6, num(x, 1 - y, 1 - cc)),
                           (7, num(1 - x, 1 - y, 1 - cc))):
                copy(a, k, sibling, o.at[blk], o.at[blk]).wait_recv()
            for cp in sends[a]:
                cp.wait_send()
        for m in mine:
            m.wait()

    res = pl.pallas_call(
        body, name=name, in_specs=[_ANY] * na, out_specs=[_ANY] * na,
        out_shape=[jax.ShapeDtypeStruct((N_DEV,) + b.shape, b.dtype) for b in blks],
        scratch_shapes=[pltpu.SemaphoreType.DMA((ns * na,)), pltpu.SemaphoreType.DMA((ns * na,)),
                        pltpu.SemaphoreType.DMA((na,))],
    )(*blks)
    return list(res)


def _all_gather_small(name, blk):
    def body(x_ref, out_ref, ssem, rsem, lsem):
        x, y, c = lax.axis_index("x"), lax.axis_index("y"), lax.axis_index("c")
        me = 4 * x + 2 * y + c
        mine = pltpu.make_async_copy(x_ref, out_ref.at[me], lsem)
        mine.start()
        cps = []
        for j in range(1, N_DEV):
            px = 1 - x if j & 4 else x
            py = 1 - y if j & 2 else y
            pc = 1 - c if j & 1 else c
            cps.append(pltpu.make_async_remote_copy(src_ref=x_ref, dst_ref=out_ref.at[me], send_sem=ssem.at[j - 1],
                                                    recv_sem=rsem.at[j - 1], device_id=(px, py, pc),
                                                    device_id_type=MESH))
        for cp in cps:
            cp.start()
        for cp in cps:
            cp.wait()
        mine.wait()

    return pl.pallas_call(
        body, name=name, in_specs=[_ANY], out_specs=_ANY,
        out_shape=jax.ShapeDtypeStruct((N_DEV,) + blk.shape, blk.dtype),
        scratch_shapes=[pltpu.SemaphoreType.DMA((N_DEV - 1,)), pltpu.SemaphoreType.DMA((N_DEV - 1,)),
                        pltpu.SemaphoreType.DMA],
    )(blk)


def _reduce_pair(g8s):
    na = len(g8s)

    def body(*refs):
        g_refs, recv_refs = refs[:na], refs[na:2 * na]
        ssem, rsem = refs[2 * na:]
        x, y, cc = lax.axis_index("x"), lax.axis_index("y"), lax.axis_index("c")
        chips = [(x, y), (1 - x, y), (x, 1 - y), (1 - x, 1 - y)]
        sib = (x, y, 1 - cc)
        for a in range(na):
            for k, (cx, cy) in enumerate(chips):
                pltpu.make_async_remote_copy(
                    src_ref=g_refs[a].at[4 * cx + 2 * cy + 1 - cc], dst_ref=recv_refs[a].at[k],
                    send_sem=ssem.at[a], recv_sem=rsem.at[a], device_id=sib, device_id_type=MESH).start()
        for a in range(na):
            pltpu.make_async_remote_copy(src_ref=recv_refs[a], dst_ref=recv_refs[a], send_sem=ssem.at[a],
                                         recv_sem=rsem.at[a], device_id=sib, device_id_type=MESH).wait()

    res = pl.pallas_call(
        body, name="reduce_pair", in_specs=[_ANY] * na, out_specs=[_ANY] * na,
        out_shape=[jax.ShapeDtypeStruct((4,) + g.shape[1:], g.dtype) for g in g8s],
        scratch_shapes=[pltpu.SemaphoreType.DMA((na,)), pltpu.SemaphoreType.DMA((na,))],
    )(*g8s)
    return list(res)


_HBM = pl.BlockSpec(memory_space=pltpu.HBM)
_SEM = pl.BlockSpec(memory_space=pltpu.SEMAPHORE)
_EFFECT = pltpu.SideEffectType.DATAFLOW_SIDE_EFFECTING


def _chip_swap_copies(s_refs, land_refs, ssem, rsem):
    x, y, c = lax.axis_index("x"), lax.axis_index("y"), lax.axis_index("c")
    targets = [(1 - x, y, c), (x, 1 - y, c), (1 - x, 1 - y, c)]
    return [pltpu.make_async_remote_copy(src_ref=s.at[k], dst_ref=d.at[k], send_sem=ssem.at[3 * a + k],
                                         recv_sem=rsem.at[3 * a + k], device_id=targets[k], device_id_type=MESH)
            for a, (s, d) in enumerate(zip(s_refs, land_refs)) for k in range(3)]


def _chip_swap_start(sends):
    na = len(sends)

    def body(*refs):
        cps = _chip_swap_copies(refs[:na], refs[na:2 * na], refs[2 * na], refs[2 * na + 1])
        for cp in cps:
            cp.start()
        token = refs[-1]
        token[...] = jnp.zeros(token.shape, token.dtype)

    bufs = [pltpu.HBM(s.shape, s.dtype) for s in sends]
    res = pl.pallas_call(
        body, name="chip_swap_start",
        out_shape=[pltpu.SemaphoreType.DMA((3 * na,)), pltpu.SemaphoreType.DMA((3 * na,))] + bufs + bufs
        + [jax.ShapeDtypeStruct((8, LANES), F32)],
        in_specs=[_HBM] * (2 * na), out_specs=[_SEM, _SEM] + [_HBM] * (2 * na) + [pl.BlockSpec(memory_space=pltpu.VMEM)],
        input_output_aliases={i: 2 + i for i in range(2 * na)},
        compiler_params=pltpu.CompilerParams(has_side_effects=_EFFECT),
    )(*[pltpu.with_memory_space_constraint(s, pltpu.HBM) for s in sends],
      *[pltpu.with_memory_space_constraint(lax.empty(s.shape, s.dtype), pltpu.HBM) for s in sends])
    return res[0], res[1], list(res[2:2 + na]), list(res[2 + na:2 + 2 * na]), res[-1]


def _chip_swap_wait(ssem, rsem, srcs, lands, after):
    na = len(srcs)

    def body(*refs):
        cps = _chip_swap_copies(refs[:na], refs[na:2 * na], refs[2 * na], refs[2 * na + 1])
        for cp in cps:
            cp.wait_send()
            cp.wait_recv()

    bufs = [pltpu.HBM(s.shape, s.dtype) for s in srcs]
    res = pl.pallas_call(
        body, name="chip_swap_wait", out_shape=bufs + bufs,
        in_specs=[_HBM] * (2 * na) + [_SEM, _SEM, _ANY], out_specs=[_HBM] * (2 * na),
        input_output_aliases={i: i for i in range(2 * na)},
        compiler_params=pltpu.CompilerParams(has_side_effects=_EFFECT),
    )(*srcs, *lands, ssem, rsem, after)
    return list(res[na:])


def _pick_rows(r, c, budget=TILE_BYTES):
    if r * c * 4 <= budget or r % 16:
        return r
    best = 16
    for tr in range(16, r, 16):
        if r % tr == 0 and tr * c * 4 <= budget:
            best = tr
    return best


def _pair_sum(name, idx4, g8, recv4):
    _, r, c = g8.shape
    tr = _pick_rows(r, c, 2 * TILE_BYTES)

    def body(idx_ref, a_ref, b_ref, o0_ref, o3_ref):
        k = pl.program_id(1)
        s = a_ref[...].astype(F32) + b_ref[...].astype(F32)

        @pl.when(k == 0)
        def _():
            o0_ref[...] = s

        @pl.when(k > 0)
        def _():
            o3_ref[...] = s.astype(BF)

    spec = pltpu.PrefetchScalarGridSpec(
        num_scalar_prefetch=1, grid=(r // tr, 4),
        in_specs=[pl.BlockSpec((None, tr, c), lambda i, k, idx: (idx[k], i, 0)),
                  pl.BlockSpec((None, tr, c), lambda i, k, idx: (k, i, 0))],
        out_specs=[pl.BlockSpec((tr, c), lambda i, k, idx: (i, 0)),
                   pl.BlockSpec((None, tr, c), lambda i, k, idx: (jnp.maximum(k - 1, 0), i, 0))])
    return pl.pallas_call(
        body, name=name, grid_spec=spec,
        out_shape=[jax.ShapeDtypeStruct((r, c), F32), jax.ShapeDtypeStruct((3, r, c), BF)],
        compiler_params=_cparams(("arbitrary", "arbitrary")),
    )(idx4, g8, recv4)


def _adamw(w, g, m, v):
    m = ADAM_B1 * m + (1.0 - ADAM_B1) * g
    v = ADAM_B2 * v + (1.0 - ADAM_B2) * jnp.square(g)
    m_hat = m / (1.0 - ADAM_B1 ** ADAM_STEP)
    v_hat = v / (1.0 - ADAM_B2 ** ADAM_STEP)
    delta = -ADAM_LR * (m_hat / (jnp.sqrt(v_hat) + ADAM_EPS) + ADAM_WD * w)
    return delta, m, v


def _adam_sharded(name, idx1, own, recv, w, m, v):
    r, c = w.shape
    tr = _pick_rows(r, c, 2 * TILE_BYTES)
    nj = recv.shape[0]

    def body(idx_ref, p_ref, r_ref, w_ref, m_ref, v_ref, g_out, d_out, m_out, v_out):
        g = p_ref[...].astype(F32)
        for k in range(nj):
            g = g + r_ref[k].astype(F32)
        d, mn, vn = _adamw(w_ref[...], g, m_ref[...], v_ref[...])
        g_out[...] = g
        d_out[...] = d
        m_out[...] = mn
        v_out[...] = vn

    row = pl.BlockSpec((tr, c), lambda i, idx: (i, 0))
    spec = pltpu.PrefetchScalarGridSpec(
        num_scalar_prefetch=1, grid=(r // tr,),
        in_specs=[pl.BlockSpec((None, tr, c), lambda i, idx: (idx[0], i, 0)),
                  pl.BlockSpec((nj, tr, c), lambda i, idx: (0, i, 0)), row, row, row],
        out_specs=[row] * 4)
    return pl.pallas_call(
        body, name=name, grid_spec=spec, out_shape=[jax.ShapeDtypeStruct((r, c), F32)] * 4,
        compiler_params=_cparams(("arbitrary",)),
    )(idx1, own, recv, w, m, v)


def _repl_rows():
    rows, r = {}, 0
    for name, cols in REPL:
        rows[name] = r
        r += REPL_ROWS.get(name, 1) * ((cols + D - 1) // D)
    return rows


LOSS_ROW = 24


def _pack_replicated(grads, loss_acc):
    rows = _repl_rows()
    names = [n for n, _ in REPL]

    def body(*refs):
        o_ref = refs[-1]
        o_ref[...] = jnp.zeros(o_ref.shape, F32)
        o_ref[LOSS_ROW:LOSS_ROW + 1, 0:LANES] = refs[-2][...]
        for name, ref in zip(names, refs[:-2]):
            r0 = rows[name]
            nr, nc = ref.shape
            if nc <= D:
                o_ref[r0:r0 + nr, 0:nc] = ref[...]
            else:
                for j in range((nc + D - 1) // D):
                    lo, hi = j * D, min(nc, (j + 1) * D)
                    o_ref[r0 + j:r0 + j + 1, 0:hi - lo] = ref[:, lo:hi]

    return pl.pallas_call(body, name="pack_replicated", out_shape=jax.ShapeDtypeStruct((REPL_TOTAL, D), F32),
                          compiler_params=_cparams())(*[grads[n] for n in names], loss_acc)


def _adam_replicated(g8, ws, ms, vs):
    rows = _repl_rows()
    names = [n for n, _ in REPL]
    np_ = len(names)

    def body(*refs):
        g_ref = refs[0]
        w_refs, m_refs, v_refs = refs[1:1 + np_], refs[1 + np_:1 + 2 * np_], refs[1 + 2 * np_:1 + 3 * np_]
        outs = refs[1 + 3 * np_:1 + 7 * np_]
        scr = refs[-1]
        g = g_ref[0]
        for k in range(1, N_DEV):
            g = g + g_ref[k]
        scr[...] = g
        refs[1 + 7 * np_][...] = scr[LOSS_ROW:LOSS_ROW + 1, 0:LANES]
        for i, name in enumerate(names):
            r0 = rows[name]
            nr, nc = w_refs[i].shape
            if nc <= D:
                gi = scr[r0:r0 + nr, 0:nc]
            else:
                parts = []
                for j in range((nc + D - 1) // D):
                    lo, hi = j * D, min(nc, (j + 1) * D)
                    parts.append(scr[r0 + j:r0 + j + 1, 0:hi - lo])
                gi = jnp.concatenate(parts, axis=1)
            d, mn, vn = _adamw(w_refs[i][...], gi, m_refs[i][...], v_refs[i][...])
            outs[i][...] = gi
            outs[np_ + i][...] = d
            outs[2 * np_ + i][...] = mn
            outs[3 * np_ + i][...] = vn

    shp = [jax.ShapeDtypeStruct(w.shape, F32) for w in ws]
    res = pl.pallas_call(body, name="adam_replicated", out_shape=shp * 4 + [jax.ShapeDtypeStruct((1, LANES), F32)],
                         scratch_shapes=[pltpu.VMEM((REPL_TOTAL, D), F32)], compiler_params=_cparams(),
                         )(g8, *ws, *ms, *vs)
    return [dict(zip(names, res[k * np_:(k + 1) * np_])) for k in range(4)], res[-1]


_WEIGHTS = ("attn_pre_norm", "w_in", "hgrn_lb", "hgrn_gnorm", "w_branch_a", "rwkv_mu", "rwkv_w0", "rwkv_w2",
            "rwkv_a0", "rwkv_a2", "rwkv_g2", "rwkv_k_k", "rwkv_k_a", "rwkv_r_k", "rwkv_ln_w", "rwkv_ln_b",
            "w_branch_b", "w_out", "attn_post_norm", "ffn_pre_norm", "w_up", "conv_w", "conv_b", "w_down",
            "ffn_post_norm")
_BIG = ("w_in", "w_up", "w_down", "w_branch_a", "w_branch_b", "w_out")


def _stages():
    one = [D]
    hw = HG_K * HG_PER_STEP
    rw = LANES * RW_PAIRS_PER_STEP
    return dict(
        mixers=_Stage("mixers", _f_mixers, 1, 2 * RW_CHUNK, [False] * 13, [[D] * 7 + [LANES, LANES]], [0],
                      [(hw, HG_K), (1, RW_COLS), (rw, LANES)], [one, one], [BF, BF],
                      kept_shapes=[(2 * RW_KEPT * RW_PAIRS_PER_STEP * 2 * RW_CHUNK, LANES)], f_kept=_f_mixers_kept),
        conv=_Stage("conv", _f_conv, 1, 128, [False, False], [[DFF, DFF]], [0], [(1, 2 * DFF), (1, 2 * DFF)],
                    [[DFF]], [BF]),
    )


def _cols_to_blocks(w, per):
    return w.reshape(w.shape[0], N_DEV, per).transpose(1, 0, 2)


def _blocks_to_cols(g):
    return g.transpose(1, 0, 2).reshape(g.shape[1], N_DEV * g.shape[2])


def kernel(x, attn_pre_norm, w_in, hgrn_lb, hgrn_gnorm, w_branch_a, rwkv_mu, rwkv_w0, rwkv_w2, rwkv_a0, rwkv_a2, rwkv_g2, rwkv_k_k, rwkv_k_a, rwkv_r_k, rwkv_ln_w, rwkv_ln_b, w_branch_b, w_out, attn_post_norm, ffn_pre_norm, w_up, conv_w, conv_b, w_down, ffn_post_norm, loss_target, m_attn_pre_norm, m_w_in, m_hgrn_lb, m_hgrn_gnorm, m_w_branch_a, m_rwkv_mu, m_rwkv_w0, m_rwkv_w2, m_rwkv_a0, m_rwkv_a2, m_rwkv_g2, m_rwkv_k_k, m_rwkv_k_a, m_rwkv_r_k, m_rwkv_ln_w, m_rwkv_ln_b, m_w_branch_b, m_w_out, m_attn_post_norm, m_ffn_pre_norm, m_w_up, m_conv_w, m_conv_b, m_w_down, m_ffn_post_norm, v_attn_pre_norm, v_w_in, v_hgrn_lb, v_hgrn_gnorm, v_w_branch_a, v_rwkv_mu, v_rwkv_w0, v_rwkv_w2, v_rwkv_a0, v_rwkv_a2, v_rwkv_g2, v_rwkv_k_k, v_rwkv_k_a, v_rwkv_r_k, v_rwkv_ln_w, v_rwkv_ln_b, v_w_branch_b, v_w_out, v_attn_post_norm, v_ffn_pre_norm, v_w_up, v_conv_w, v_conv_b, v_w_down, v_ffn_post_norm):
    w = dict(attn_pre_norm=attn_pre_norm, w_in=w_in, hgrn_lb=hgrn_lb, hgrn_gnorm=hgrn_gnorm, w_branch_a=w_branch_a, rwkv_mu=rwkv_mu, rwkv_w0=rwkv_w0, rwkv_w2=rwkv_w2, rwkv_a0=rwkv_a0, rwkv_a2=rwkv_a2, rwkv_g2=rwkv_g2, rwkv_k_k=rwkv_k_k, rwkv_k_a=rwkv_k_a, rwkv_r_k=rwkv_r_k, rwkv_ln_w=rwkv_ln_w, rwkv_ln_b=rwkv_ln_b, w_branch_b=w_branch_b, w_out=w_out, attn_post_norm=attn_post_norm, ffn_pre_norm=ffn_pre_norm, w_up=w_up, conv_w=conv_w, conv_b=conv_b, w_down=w_down, ffn_post_norm=ffn_post_norm)
    mo = dict(attn_pre_norm=m_attn_pre_norm, w_in=m_w_in, hgrn_lb=m_hgrn_lb, hgrn_gnorm=m_hgrn_gnorm, w_branch_a=m_w_branch_a, rwkv_mu=m_rwkv_mu, rwkv_w0=m_rwkv_w0, rwkv_w2=m_rwkv_w2, rwkv_a0=m_rwkv_a0, rwkv_a2=m_rwkv_a2, rwkv_g2=m_rwkv_g2, rwkv_k_k=m_rwkv_k_k, rwkv_k_a=m_rwkv_k_a, rwkv_r_k=m_rwkv_r_k, rwkv_ln_w=m_rwkv_ln_w, rwkv_ln_b=m_rwkv_ln_b, w_branch_b=m_w_branch_b, w_out=m_w_out, attn_post_norm=m_attn_post_norm, ffn_pre_norm=m_ffn_pre_norm, w_up=m_w_up, conv_w=m_conv_w, conv_b=m_conv_b, w_down=m_w_down, ffn_post_norm=m_ffn_post_norm)
    vo = dict(attn_pre_norm=v_attn_pre_norm, w_in=v_w_in, hgrn_lb=v_hgrn_lb, hgrn_gnorm=v_hgrn_gnorm, w_branch_a=v_w_branch_a, rwkv_mu=v_rwkv_mu, rwkv_w0=v_rwkv_w0, rwkv_w2=v_rwkv_w2, rwkv_a0=v_rwkv_a0, rwkv_a2=v_rwkv_a2, rwkv_g2=v_rwkv_g2, rwkv_k_k=v_rwkv_k_k, rwkv_k_a=v_rwkv_k_a, rwkv_r_k=v_rwkv_r_k, rwkv_ln_w=v_rwkv_ln_w, rwkv_ln_b=v_rwkv_ln_b, w_branch_b=v_w_branch_b, w_out=v_w_out, attn_post_norm=v_attn_post_norm, ffn_pre_norm=v_ffn_pre_norm, w_up=v_w_up, conv_w=v_conv_w, conv_b=v_conv_b, w_down=v_w_down, ffn_post_norm=v_ffn_post_norm)

    t = x.shape[1]
    x2 = x.reshape(t, D)
    tgt = loss_target.reshape(t, D)
    st = _stages()

    me = 4 * lax.axis_index("x") + 2 * lax.axis_index("y") + lax.axis_index("c")
    small = jnp.concatenate([rwkv_w2[0], rwkv_a2[0], rwkv_g2[0]], axis=0).astype(BF)
    g_in, g_small = _all_gather("gather_weights", [w_in[0].T.astype(BF), small])
    fw_in_t = g_in.reshape(IN_COLS, D)
    z64 = jnp.zeros((64, D), BF)
    w2p = jnp.concatenate([_blocks_to_cols(g_small[:, 0:64]), z64], axis=0)
    a2p = jnp.concatenate([z64, _blocks_to_cols(g_small[:, 64:128])], axis=0)
    g2f = _blocks_to_cols(g_small[:, 128:256])
    conv_bits = jnp.pad(lax.bitcast_convert_type(conv_w[0], BF).reshape(3, 2 * 704), ((0, 29), (0, 0)))
    late = [w_up[0].T.astype(BF)] + [w[k][0].astype(BF) for k in _BIG[2:]] + [conv_bits]
    late_gather = _Exchange("gather2", late)
    r_k = rwkv_r_k.reshape(1, D)

    xn, z = _norm_in_proj(x2, attn_pre_norm, fw_in_t, 512, 4736)
    mix_par = [hgrn_lb, hgrn_gnorm, rwkv_mu, rwkv_w0, w2p, rwkv_a0, a2p, g2f, rwkv_k_k, rwkv_k_a,
               rwkv_ln_w, rwkv_ln_b, r_k]
    mix_in = [z]
    (o_a, o_b), mix_saved = _stage_fwd(st["mixers"], t, mix_par, mix_in, hook=late_gather)
    gl = [lax.dynamic_update_slice(g, own[None], (me, 0, 0)) for g, own in zip(late_gather.results, late)]
    fw_up_t = gl[0].reshape(2 * DFF, D)
    fw_down = gl[1].reshape(DFF, D)
    fw_a, fw_b, fw_out = (g.reshape(D, D) for g in gl[2:5])
    conv_full = _blocks_to_cols(lax.bitcast_convert_type(gl[5][:, :3].reshape(N_DEV, 3, 704, 2), F32))
    y_a, y_b = _mm_multi("branches", [(o_a, fw_a), (o_b, fw_b)], "nn", BF)
    merged, mix, h1, xn2 = _merge_out_post(z, y_a, y_b, fw_out, x2, attn_post_norm, ffn_pre_norm, 512)
    hu = _mm("up_proj", xn2, fw_up_t, "nt", F32, tm=1024, tn=1408)
    conv_par = [conv_full, conv_b]
    (act,), conv_saved = _stage_fwd(st["conv"], t, conv_par, [hu])

    loss_acc, d_ffn_post, dh1, dff = _down_loss(act, fw_down, ffn_post_norm, h1, tgt, 512)
    dact = _mm("d_act", dff, fw_down, "nt", BF, tm=1024, tn=1408)
    dw_down = _mm("dw_down", act, dff, "tn", BF, tm=1408, tn=512)
    (dcw, dcb), (dhu,) = _stage_bwd(st["conv"], t, conv_par, [hu], conv_saved, [[dact]], [BF])
    dw_up_t = _mm("dw_up", dhu, xn2, "tn", BF, tm=1408, tn=1024)
    d_post, d_pre2, dx_a, dmix = _dxn2_post1_bwd(dhu, fw_up_t, x2, mix, dh1, attn_post_norm, ffn_pre_norm, 512)
    dga, dgb, dy_a, dy_b = _dmerged_merge_bwd(dmix, fw_out, z, y_a, y_b, 512)
    do_a, do_b = _mm_multi("d_branches", [(dy_a, fw_a), (dy_b, fw_b)], "nt", BF)
    dw_a, dw_b, dw_out = _mm_multi("dw_branches", [(o_a, dy_a), (o_b, dy_b), (merged, dmix)], "tn", BF)
    early = [dw_up_t.reshape(N_DEV, 704, D), dw_down.reshape(N_DEV, 352, D), dw_a.reshape(N_DEV, 128, D),
             dw_b.reshape(N_DEV, 128, D), dw_out.reshape(N_DEV, 128, D), _cols_to_blocks(dcw.astype(BF), 704)]
    early_scatter = _Exchange("scatter", early)
    mix_dp, dz_hr = _stage_bwd(st["mixers"], t, mix_par, mix_in, mix_saved, [[do_a], [do_b]], [BF],
                               hook=early_scatter)
    d_lb, d_gn, d_mu, d_w0, d_w2p, d_a0, d_a2p, d_g2, d_kk, d_ka, d_lnw, d_lnb, d_rk = mix_dp
    dz = dz_hr + [dga, dgb]
    dw_in_t = _mm_cols_tn("dw_in", dz, xn, BF, 256)

    ax, ay, ac = lax.axis_index("x"), lax.axis_index("y"), lax.axis_index("c")
    idx4 = jnp.stack([4 * cx + 2 * cy + ac for cx, cy in ((ax, ay), (1 - ax, ay), (ax, 1 - ay), (1 - ax, 1 - ay))])
    idx4 = idx4.astype(jnp.int32)
    idx_me, idx_0 = idx4[0:1], jnp.zeros((1,), jnp.int32)
    d_small = jnp.concatenate([d_w2p[:64], d_a2p[64:], d_g2], axis=0).astype(BF)
    g8s = [dw_in_t.reshape(N_DEV, 1184, D), _cols_to_blocks(d_small, LANES)]
    recv4s = _reduce_pair(g8s)
    sums = [_pair_sum("pair_sum_" + n, idx4, g, r) for n, g, r in zip(("w_in", "small"), g8s, recv4s)]
    swap_ssem, swap_rsem, swap_srcs, swap_lands, token = _chip_swap_start([s[1] for s in sums])
    d_pre1, dx = _dxn_pre1_bwd(dz, fw_in_t, x2, dx_a, attn_pre_norm, 256, token)
    grad_x = dx.reshape(x.shape)

    rg = dict(attn_pre_norm=d_pre1, hgrn_lb=d_lb, hgrn_gnorm=d_gn, rwkv_mu=d_mu, rwkv_w0=d_w0, rwkv_a0=d_a0,
              rwkv_k_k=d_kk, rwkv_k_a=d_ka, rwkv_r_k=d_rk, rwkv_ln_w=d_lnw, rwkv_ln_b=d_lnb, attn_post_norm=d_post,
              ffn_pre_norm=d_pre2, conv_b=dcb, ffn_post_norm=d_ffn_post)
    g8 = _all_gather_small("gather_small_grads", _pack_replicated(rg, loss_acc))
    rnames = [n for n, _ in REPL]
    flat = lambda src: [src[n].reshape(1, D) if n == "rwkv_r_k" else src[n] for n in rnames]
    rp_out, loss_row = _adam_replicated(g8, flat(w), flat(mo), flat(vo))
    loss = loss_row[0, 0]
    recv3s = _chip_swap_wait(swap_ssem, swap_rsem, swap_srcs, swap_lands, rp_out[0]["attn_pre_norm"])
    for kind in range(4):
        rp_out[kind]["rwkv_r_k"] = rp_out[kind]["rwkv_r_k"].reshape(rwkv_r_k.shape)

    def small_of(src):
        return jnp.concatenate([src["rwkv_w2"][0], src["rwkv_a2"][0], src["rwkv_g2"][0]], axis=0)

    sh_out = [dict() for _ in range(4)]
    res = _adam_sharded("adam_w_in", idx_0, sums[0][0][None], recv3s[0], *[src["w_in"][0].T for src in (w, mo, vo)])
    res_s = _adam_sharded("adam_small", idx_0, sums[1][0][None], recv3s[1], *[small_of(src) for src in (w, mo, vo)])
    for kind in range(4):
        sh_out[kind]["w_in"] = res[kind].T[None]
        sh_out[kind]["rwkv_w2"] = res_s[kind][0:64][None]
        sh_out[kind]["rwkv_a2"] = res_s[kind][64:128][None]
        sh_out[kind]["rwkv_g2"] = res_s[kind][128:256][None]
    for n, own, recv in zip(_BIG[1:] + ("conv_w",), early, early_scatter.results):
        tr = (lambda a: a.T) if n == "w_up" else (lambda a: a)
        res = _adam_sharded("adam_" + n, idx_me, own, recv, *[tr(src[n][0]) for src in (w, mo, vo)])
        for kind in range(4):
            sh_out[kind][n] = tr(res[kind])[None]

    outs = [loss, grad_x]
    for kind in range(4):
        for name in _WEIGHTS:
            outs.append(sh_out[kind][name] if name in sh_out[kind] else rp_out[kind][name])
    return tuple(outs)
```

```python
import functools

import jax
import jax.numpy as jnp
from jax import lax
from jax.experimental import pallas as pl
from jax.experimental.pallas import tpu as pltpu

F32 = jnp.float32
BF = jnp.bfloat16
MESH = pl.DeviceIdType.MESH

D = 1024
HG_HEADS = 8
HG_K = 128
HG_CHUNK = 32
HG_SCALE = HG_K ** -0.5
HG_PER_STEP = 8
RW_HEADS = 16
RW_N = 64
RW_CHUNK = 64
RW_PAIRS_PER_STEP = 8
DFF = 2816
IN_COLS = 9472
RW_COLS = 3328
EPS = 1e-6
GN_EPS = 1e-5 * RW_N
ADAM_LR = 0.001
ADAM_B1 = 0.9
ADAM_B2 = 0.999
ADAM_EPS = 1e-08
ADAM_WD = 0.01
ADAM_STEP = 10
N_DEV = 8
LANES = 128
VMEM_LIMIT = 56 * 1024 * 1024
TILE_BYTES = 1280 * 1024

REPL = (("attn_pre_norm", 1024), ("hgrn_lb", 1024), ("hgrn_gnorm", 1024), ("rwkv_mu", 3328), ("rwkv_w0", 1024),
        ("rwkv_a0", 1024), ("rwkv_k_k", 1024), ("rwkv_k_a", 1024), ("rwkv_r_k", 1024), ("rwkv_ln_w", 1024),
        ("rwkv_ln_b", 1024), ("attn_post_norm", 1024), ("ffn_pre_norm", 1024), ("conv_b", 5632), ("ffn_post_norm", 1024))
REPL_ROWS = {"hgrn_lb": 2}
REPL_TOTAL = 32


def _cparams(sem=None, **kw):
    return pltpu.CompilerParams(dimension_semantics=sem, vmem_limit_bytes=VMEM_LIMIT, **kw)


_DN = {"nn": ((1,), (0,)), "nt": ((1,), (1,)), "tn": ((0,), (0,))}


def _raw_dot(a, b, mode):
    return lax.dot_general(a.astype(BF), b.astype(BF), (_DN[mode], ((), ())), preferred_element_type=F32)


@functools.partial(jax.custom_vjp, nondiff_argnums=(2,))
def _dot(a, b, mode):
    return _raw_dot(a, b, mode)


def _dot_fwd(a, b, mode):
    return _raw_dot(a, b, mode), (a, b)


def _dot_bwd(mode, res, g):
    a, b = res
    if mode == "nn":
        return _dot(g, b, "nt"), _dot(a, g, "tn")
    if mode == "nt":
        return _dot(g, b, "nn"), _dot(g, a, "tn")
    return _dot(b, g, "nt"), _dot(a, g, "nn")


_dot.defvjp(_dot_fwd, _dot_bwd)


def _bf_pieces(x, n):
    out, r = [], x
    for i in range(n):
        p = r.astype(BF)
        out.append(p)
        if i + 1 < n:
            r = r - p.astype(F32)
    return out


def _raw_split_dot(x, e, mode, n, x_left):
    eb = e.astype(BF)
    acc = None
    for p in _bf_pieces(x, n):
        ops = (p, eb) if x_left else (eb, p)
        t = lax.dot_general(*ops, (_DN[mode], ((), ())), preferred_element_type=F32)
        acc = t if acc is None else acc + t
    return acc


def _raw_headsum(x):
    t = x.shape[0]
    i = lax.broadcasted_iota(jnp.int32, (LANES, LANES), 0)
    j = lax.broadcasted_iota(jnp.int32, (LANES, LANES), 1)
    same = jnp.where((i >= RW_N) == (j >= RW_N), 1.0, 0.0).astype(F32)
    groups = x.shape[1] // LANES
    rows = jnp.concatenate([x[:, q * LANES:(q + 1) * LANES] for q in range(groups)], axis=0)
    s = _raw_split_dot(rows, same, "nn", 2, True)
    return jnp.concatenate([s[q * t:(q + 1) * t] for q in range(groups)], axis=1)


@jax.custom_vjp
def _headsum(x):
    return _raw_headsum(x)


def _headsum_fwd(x):
    return _raw_headsum(x), None


def _headsum_bwd(_, g):
    return (_raw_headsum(g),)


_headsum.defvjp(_headsum_fwd, _headsum_bwd)


@functools.partial(jax.custom_vjp, nondiff_argnums=(2,))
def _tdot(tri, x, n):
    return _raw_split_dot(x, tri, "nn", n, False)


def _tdot_fwd(tri, x, n):
    return _raw_split_dot(x, tri, "nn", n, False), tri


def _tdot_bwd(n, tri, g):
    return jnp.zeros_like(tri), _raw_split_dot(g, tri, "tn", n, False)


_tdot.defvjp(_tdot_fwd, _tdot_bwd)


def _row(x, i):
    r = lax.broadcasted_iota(jnp.int32, x.shape, 0)
    return jnp.sum(jnp.where(r == i, x, 0.0), axis=0, keepdims=True)


def _shift_down(x, prev):
    t = x.shape[0]

    @jax.custom_vjp
    def sh(x, prev):
        r = lax.broadcasted_iota(jnp.int32, x.shape, 0)
        return jnp.where(r == 0, prev, pltpu.roll(x, 1, 0))

    def fwd(x, prev):
        return sh(x, prev), None

    def bwd(_, g):
        r = lax.broadcasted_iota(jnp.int32, g.shape, 0)
        dx = jnp.where(r == t - 1, 0.0, pltpu.roll(g, t - 1, 0))
        return dx, jnp.sum(jnp.where(r == 0, g, 0.0), axis=0, keepdims=True)

    sh.defvjp(fwd, bwd)
    return sh(x, prev)


def _sigmoid(x):
    return jax.nn.sigmoid(x)


def _silu(x):
    return x * jax.nn.sigmoid(x)


def _softplus(x):
    return jnp.maximum(x, 0.0) + jnp.log(1.0 + jnp.exp(-jnp.abs(x)))


def _rms(x, g):
    return (x * lax.rsqrt(jnp.mean(x * x, axis=-1, keepdims=True) + EPS)) * g


def _tril(c):
    r = lax.broadcasted_iota(jnp.int32, (c, c), 0)
    cc = lax.broadcasted_iota(jnp.int32, (c, c), 1)
    return cc <= r


def _f_pre1_residual(ps, xs, cs):
    return [_rms(xs[0], ps[0]), xs[0]], []


def _f_hgrn(ps, xs, cs):
    lbraw, gn = ps
    hq, hf, hi, hg = xs
    hd = range(HG_PER_STEP)
    st = [cs[0][p * HG_K:(p + 1) * HG_K] for p in hd]
    l0, l1 = _row(lbraw, 0), _row(lbraw, 1)
    m = jnp.maximum(l0, l1)
    e0, e1 = jnp.exp(l0 - m), jnp.exp(l1 - m)
    lb = e0 / (e0 + e1)
    q = _silu(hq) * HG_SCALE
    f = lb + (1.0 - lb) * _sigmoid(hf)
    kh = 1.0 - f
    gl = jnp.log(f)
    c = HG_CHUNK
    low = _tril(c)
    tri = jnp.where(low, 1.0, 0.0).astype(F32)
    outs = []
    for i in range(hq.shape[0] // c):
        rows = slice(i * c, (i + 1) * c)
        b = _tdot(tri, gl[rows], 3)
        bref = _row(b, c // 2 - 1)
        blast = _row(b, c - 1)
        qi = q[rows] * jnp.exp(b - bref)
        ki = kh[rows] * jnp.exp(bref - b)
        qd = q[rows] * jnp.exp(b)
        kd = kh[rows] * jnp.exp(blast - b)
        dec = jnp.exp(blast)
        sl = [slice(p * HG_K, (p + 1) * HG_K) for p in hd]
        sc = [jnp.where(low, _dot(qi[:, sl[p]], ki[:, sl[p]], "nt"), 0.0) for p in hd]
        o = [_dot(sc[p], hi[rows, sl[p]], "nn") + _dot(qd[:, sl[p]], st[p], "nt") for p in hd]
        u = [_dot(hi[rows, sl[p]], kd[:, sl[p]], "tn") for p in hd]
        st = [dec[:, sl[p]] * st[p] + u[p] for p in hd]
        outs.append(jnp.concatenate(o, axis=1) if len(o) > 1 else o[0])
    o = outs[0] if len(outs) == 1 else jnp.concatenate(outs, axis=0)
    on = []
    for p in hd:
        op = o[:, p * HG_K:(p + 1) * HG_K]
        on.append(op * lax.rsqrt(jnp.mean(op * op, axis=-1, keepdims=True) + EPS))
    o = jnp.concatenate(on, axis=1) if len(on) > 1 else on[0]
    o = o * gn
    return [o * _silu(hg)], [jnp.concatenate(st, axis=0) if len(st) > 1 else st[0]]


_RW_OFFS = (0, 1024, 2048, 3072, 3200, 3328)


def _f_rwpre(ps, xs, cs):
    mu, w0, w2p, a0, a2p, g2, k_k, k_a = ps
    (prev,) = cs
    t = xs[0].shape[0]
    zs = []
    for i, z in enumerate(xs):
        lo, hi = _RW_OFFS[i], _RW_OFFS[i + 1]
        zs.append(z + mu[:, lo:hi] * (_shift_down(z, prev[:, lo:hi]) - z))
    rr, kr, vr, wa, gz = zs
    w_log = -_softplus(-(w0 + _dot(jnp.tanh(wa), w2p, "nn"))) - 0.5
    lw = -jnp.exp(w_log)
    a = _sigmoid(a0 + _dot(wa, a2p, "nn"))
    g = _dot(_sigmoid(gz), g2, "nn")
    kkr = kr * k_k
    kk = kkr / jnp.maximum(jnp.sqrt(_headsum(kkr * kkr)), 1e-12)
    k2 = kr * (1.0 + (a - 1.0) * k_a)
    newprev = jnp.concatenate([_row(z, t - 1) for z in xs], axis=1)
    return [rr, lw, k2, vr, -kk, kk * a, g], [newprev]


def _raw_inverses(ls):
    n = ls[0].shape[0]
    r = lax.broadcasted_iota(jnp.int32, (n, n), 0)
    c = lax.broadcasted_iota(jnp.int32, (n, n), 1)
    eye = jnp.where(r == c, 1.0, 0.0).astype(F32)
    tinv = [eye + l for l in ls]
    pw = ls
    for _ in range(5):
        pw = [_raw_dot(p, p, "nn") for p in pw]
        tinv = [t + _raw_dot(t, p, "nn") for t, p in zip(tinv, pw)]
    return tinv


@jax.custom_vjp
def _unit_lower_inverses(ls):
    return _raw_inverses(ls)


def _inverses_fwd(ls):
    tinv = _raw_inverses(ls)
    return tinv, tinv


def _inverses_bwd(tinv, gs):
    return ([_raw_dot(_raw_dot(t, g, "tn"), t, "nt") for t, g in zip(tinv, gs)],)


_unit_lower_inverses.defvjp(_inverses_fwd, _inverses_bwd)


@jax.custom_vjp
def _known_inverses(ls, tinv):
    return tinv


def _known_fwd(ls, tinv):
    return tinv, tinv


def _known_bwd(tinv, gs):
    return [_raw_dot(_raw_dot(t, g, "tn"), t, "nt") for t, g in zip(tinv, gs)], [jnp.zeros_like(t) for t in tinv]


_known_inverses.defvjp(_known_fwd, _known_bwd)


@jax.custom_vjp
def _use_kept(computed, kept):
    return kept


def _use_kept_fwd(computed, kept):
    return kept, None


def _use_kept_bwd(_, g):
    return g, jax.tree.map(jnp.zeros_like, g)


_use_kept.defvjp(_use_kept_fwd, _use_kept_bwd)

RW_KEPT = 5


def _f_rwscan(ps, xs, cs, kept=None):
    state = cs[0]
    ys, keep = [], []
    n = 2 * RW_CHUNK
    per_chunk = RW_KEPT * RW_PAIRS_PER_STEP * n
    for i in range(xs[0].shape[0] // RW_CHUNK):
        known = None
        if kept is not None:
            known = [[kept[i * per_chunk + (q * RW_PAIRS_PER_STEP + p) * n:
                           i * per_chunk + (q * RW_PAIRS_PER_STEP + p + 1) * n] for p in range(RW_PAIRS_PER_STEP)]
                     for q in range(RW_KEPT)]
        y, state, mats = _rwkv_chunk([x[i * RW_CHUNK:(i + 1) * RW_CHUNK] for x in xs], state, known)
        ys.append(y)
        keep += [m for group in mats for m in group]
    return [ys[0] if len(ys) == 1 else jnp.concatenate(ys, axis=0)], [state], jnp.concatenate(keep, axis=0)


def _rwkv_chunk(xs, state, known=None):
    npair = RW_PAIRS_PER_STEP
    pr = range(npair)
    r, lw, k, v, av, bv = [[x[:, p * LANES:(p + 1) * LANES] for p in pr] for x in xs]
    sv = [state[p * LANES:(p + 1) * LANES] for p in pr]
    c = RW_CHUNK
    n = 2 * c
    tri = jnp.where(_tril(c), 1.0, 0.0).astype(F32)
    cl = [_tdot(tri, lw[p], 3) for p in pr]
    cl_last = [_row(cl[p], c - 1) for p in pr]
    lane = lax.broadcasted_iota(jnp.int32, (c, LANES), 1)
    h0 = lane < RW_N

    def stack(x):
        return jnp.concatenate([jnp.where(h0, x, 0.0), jnp.where(h0, 0.0, x)], axis=0)

    am = [stack(av[p] * jnp.exp(cl[p] - lw[p])) for p in pr]
    bm = [stack(bv[p] * jnp.exp(-cl[p])) for p in pr]
    km = [stack(k[p] * jnp.exp(-cl[p])) for p in pr]
    rm = [stack(r[p] * jnp.exp(cl[p])) for p in pr]
    vm = [stack(v[p]) for p in pr]
    rn = lax.broadcasted_iota(jnp.int32, (n, n), 0)
    cn = lax.broadcasted_iota(jnp.int32, (n, n), 1)
    blk = (rn >= c) == (cn >= c)
    strict = blk & (cn < rn)
    incl = blk & (cn <= rn)
    lab = [jnp.where(strict, _dot(am[p], bm[p], "nt"), 0.0) for p in pr]
    lak = [jnp.where(strict, _dot(am[p], km[p], "nt"), 0.0) for p in pr]
    wrb = [jnp.where(incl, _dot(rm[p], bm[p], "nt"), 0.0) for p in pr]
    wrk = [jnp.where(incl, _dot(rm[p], km[p], "nt"), 0.0) for p in pr]
    if known is None:
        tinv = _unit_lower_inverses(lab)
    else:
        tinv = _known_inverses(lab, known[0])
        lak, wrb, wrk = _use_kept(lak, known[1]), _use_kept(wrb, known[2]), _use_kept(wrk, known[3])
    rhs = [_dot(am[p], sv[p], "nt") + _dot(lak[p], vm[p], "nn") for p in pr]
    um = [_dot(tinv[p], rhs[p], "nn") for p in pr]
    if known is not None:
        um = _use_kept(um, known[4])
    ym = [_dot(rm[p], sv[p], "nt") + _dot(wrb[p], um[p], "nn") + _dot(wrk[p], vm[p], "nn") for p in pr]
    sn = [(sv[p] + _dot(um[p], bm[p], "tn") + _dot(vm[p], km[p], "tn")) * jnp.exp(cl_last[p]) for p in pr]
    ys = [ym[p][:c] + ym[p][c:] for p in pr]
    return jnp.concatenate(ys, axis=1), jnp.concatenate(sn, axis=0), [tinv, lak, wrb, wrk, um]


def _f_mixers(ps, xs, cs):
    return _mixers(ps, xs, cs, None)


def _f_mixers_kept(ps, xs, cs, kept):
    return _mixers(ps, xs, cs, kept[0])[:2]


def _mixers(ps, xs, cs, kept):
    oa, st = _f_hgrn(ps[:2], xs[:4], cs[:1])
    (r, lw, k, v, av, bv, g), prev = _f_rwpre(ps[2:10], xs[4:], cs[1:2])
    y, sv, keep = _f_rwscan([], [r, lw, k, v, av, bv], cs[2:], kept)
    ob, _ = _f_rwpost(ps[10:], y + [r, k, v, g], [])
    return oa + ob, st + prev + sv, [keep]


def _f_rwpost(ps, xs, cs):
    ln_w, ln_b, r_k = ps
    y, r, k, v, g = xs
    inv_n = 1.0 / RW_N
    yc = y - _headsum(y) * inv_n
    var = _headsum(yc * yc) * inv_n
    yn = yc * lax.rsqrt(var + GN_EPS)
    yn = yn * ln_w + ln_b
    bonus = _headsum(r * k * r_k) * v
    return [(yn + bonus) * g], []


def _f_merge(ps, xs, cs):
    ga, gb, ya, yb = xs
    return [_sigmoid(ga) * ya + _sigmoid(gb) * yb], []


def _f_post1(ps, xs, cs):
    x, mix = xs
    h1 = x + _rms(mix, ps[0])
    return [h1, _rms(h1, ps[1])], []


def _f_conv(ps, xs, cs):
    cw, cb = ps
    p1, p2 = cs
    w0, w1, w2 = _row(cw, 0), _row(cw, 1), _row(cw, 2)
    t = xs[0].shape[0]
    hc = []
    for i, x in enumerate(xs):
        sl = slice(i * DFF, (i + 1) * DFF)
        s1 = _shift_down(x, p1[:, sl])
        s2 = _shift_down(s1, p2[:, sl])
        hc.append(cb[:, sl] + w0[:, sl] * s2 + w1[:, sl] * s1 + w2[:, sl] * x)
    n1 = jnp.concatenate([_row(x, t - 1) for x in xs], axis=1)
    n2 = jnp.concatenate([_row(x, t - 2) for x in xs], axis=1)
    return [_silu(hc[0]) * hc[1]], [n1, n2]


class _Stage:
    def __init__(self, name, f, g, tm, par_per_g, in_pieces, in_offs, carry_shapes, out_pieces, out_dtypes,
                 kept_shapes=(), f_kept=None):
        self.name, self.f, self.g, self.tm = name, f, g, tm
        self.par_per_g, self.in_pieces, self.in_offs = par_per_g, in_pieces, in_offs
        self.carry_shapes, self.out_pieces, self.out_dtypes = carry_shapes, out_pieces, out_dtypes
        self.kept_shapes, self.f_kept = list(kept_shapes), f_kept


def _par_spec(arr, per_g, g):
    r, c = arr.shape
    if per_g:
        return pl.BlockSpec((r, c // g), lambda gi, ni: (0, gi))
    return pl.BlockSpec((r, c), lambda gi, ni: (0, 0))


def _row_spec(tm, width, off, n, rev):
    if rev:
        return pl.BlockSpec((tm, width), lambda gi, ni: (n - 1 - ni, off + gi))
    return pl.BlockSpec((tm, width), lambda gi, ni: (ni, off + gi))


def _carry_spec(shape, n, rev):
    if rev:
        return pl.BlockSpec((None, None) + shape, lambda gi, ni: (gi, n - 1 - ni, 0, 0))
    return pl.BlockSpec((None, None) + shape, lambda gi, ni: (gi, ni, 0, 0))


def _load_pieces(refs, pieces_list):
    out = []
    for ref, pieces in zip(refs, pieces_list):
        o = 0
        for w in pieces:
            out.append(ref[:, o:o + w].astype(F32))
            o += w
    return out


def _store_pieces(refs, pieces_list, vals):
    k = 0
    for ref, pieces in zip(refs, pieces_list):
        o = 0
        for w in pieces:
            ref[:, o:o + w] = vals[k].astype(ref.dtype)
            k += 1
            o += w


_ANY = pl.BlockSpec(memory_space=pl.ANY)


class _Exchange:
    def __init__(self, kind, arrs):
        self.kind, self.arrs, self.results = kind, list(arrs), None
        if kind == "scatter":
            self.out_shape = [jax.ShapeDtypeStruct((N_DEV - 1,) + a.shape[1:], a.dtype) for a in self.arrs]
        else:
            self.out_shape = [jax.ShapeDtypeStruct((N_DEV,) + a.shape, a.dtype) for a in self.arrs]
        self.nsem = (N_DEV if kind == "gather2" else N_DEV - 1) * len(self.arrs)

    def copies(self, in_refs, out_refs, ssem, rsem):
        x, y, c = lax.axis_index("x"), lax.axis_index("y"), lax.axis_index("c")
        me = 4 * x + 2 * y + c
        cps = []
        for a, (i_ref, o_ref) in enumerate(zip(in_refs, out_refs)):
            for j in range(1, N_DEV):
                px = 1 - x if j & 4 else x
                py = 1 - y if j & 2 else y
                pc = 1 - c if j & 1 else c
                if self.kind == "gather":
                    src, dst = i_ref, o_ref.at[me]
                else:
                    src, dst = i_ref.at[4 * px + 2 * py + pc], o_ref.at[j - 1]
                s = (N_DEV - 1) * a + j - 1
                cps.append(pltpu.make_async_remote_copy(src_ref=src, dst_ref=dst, send_sem=ssem.at[s],
                                                        recv_sem=rsem.at[s], device_id=(px, py, pc),
                                                        device_id_type=MESH))
        return cps

    def run(self, step, total, in_refs, out_refs, ssem, rsem):
        if self.kind == "gather2":
            return self.run_two_level(step, total, in_refs, out_refs, ssem, rsem)

        @pl.when(step == 0)
        def _():
            for cp in self.copies(in_refs, out_refs, ssem, rsem):
                cp.start()

        @pl.when(step == total - 1)
        def _():
            for cp in self.copies(in_refs, out_refs, ssem, rsem):
                cp.wait()

    def run_two_level(self, step, total, in_refs, out_refs, ssem, rsem):
        x, y, c = lax.axis_index("x"), lax.axis_index("y"), lax.axis_index("c")
        sibling, xn, yn = (x, y, 1 - c), (1 - x, y, c), (x, 1 - y, c)
        arrs = range(len(in_refs))
        ns = N_DEV

        def num(px, py, pc):
            return 4 * px + 2 * py + pc

        def copy(a, k, to, src, dst):
            return pltpu.make_async_remote_copy(src_ref=src, dst_ref=dst, send_sem=ssem.at[ns * a + k],
                                                recv_sem=rsem.at[ns * a + k], device_id=to, device_id_type=MESH)

        def blk(a, b):
            return out_refs[a].at[b]

        def half(a, b, second):
            h = self.arrs[a].shape[0] // 2
            return out_refs[a].at[b, pl.ds(h if second else 0, h)]

        bx, by, bd = num(1 - x, y, c), num(x, 1 - y, c), num(1 - x, 1 - y, c)

        def firsts(a):
            own = blk(a, num(x, y, c))
            return [copy(a, 0, sibling, in_refs[a], own), copy(a, 1, xn, in_refs[a], own),
                    copy(a, 2, yn, in_refs[a], own)]

        def seconds(a):
            return [copy(a, 3, yn, half(a, bx, False), half(a, bx, False)), copy(a, 5, sibling, blk(a, bx), blk(a, bx)),
                    copy(a, 4, xn, half(a, by, True), half(a, by, True)), copy(a, 6, sibling, blk(a, by), blk(a, by))]

        def third(a):
            return copy(a, 7, sibling, blk(a, bd), blk(a, bd))

        @pl.when(step == 0)
        def _():
            for a in arrs:
                for cp in firsts(a):
                    cp.start()

        @pl.when(step == total // 2)
        def _():
            for a in arrs:
                copy(a, 1, xn, blk(a, bx), blk(a, bx)).wait_recv()
                copy(a, 2, yn, blk(a, by), blk(a, by)).wait_recv()
                for cp in seconds(a):
                    cp.start()

        @pl.when(step == (4 * total) // 5)
        def _():
            for a in arrs:
                copy(a, 3, yn, half(a, bd, False), half(a, bd, False)).wait_recv()
                copy(a, 4, xn, half(a, bd, True), half(a, bd, True)).wait_recv()
                third(a).start()

        @pl.when(step == total - 1)
        def _():
            for a in arrs:
                for k, b in ((0, num(x, y, 1 - c)), (5, num(1 - x, y, 1 - c)), (6, num(x, 1 - y, 1 - c)),
                             (7, num(1 - x, 1 - y, 1 - c))):
                    copy(a, k, sibling, blk(a, b), blk(a, b)).wait_recv()
                for cp in firsts(a) + seconds(a) + [third(a)]:
                    cp.wait_send()


def _hook_specs(hook):
    if hook is None:
        return [], [], [], []
    na = len(hook.arrs)
    sems = [pltpu.SemaphoreType.DMA((hook.nsem,)), pltpu.SemaphoreType.DMA((hook.nsem,))]
    return [_ANY] * na, [_ANY] * na, hook.out_shape, sems


def _stage_fwd(st, t, params, inputs, hook=None):
    g, tm = st.g, min(st.tm, t)
    n = t // tm
    npar, nin, ncar, nout = len(params), len(inputs), len(st.carry_shapes), len(st.out_pieces)
    nk = len(st.kept_shapes)
    h_in, h_out, h_shape, h_sems = _hook_specs(hook)
    nh = len(h_in)

    def body(*refs):
        p_refs = refs[:npar]
        x_refs = refs[npar:npar + nin]
        hi_refs = refs[npar + nin:npar + nin + nh]
        o = npar + nin + nh
        o_refs = refs[o:o + nout]
        s_refs = refs[o + nout:o + nout + ncar]
        k_refs = refs[o + nout + ncar:o + nout + ncar + nk]
        o += nout + ncar + nk
        ho_refs = refs[o:o + nh]
        c_scr = refs[o + nh:o + nh + ncar]
        gi, ni = pl.program_id(0), pl.program_id(1)
        if hook is not None:
            step = gi * n + ni
            hook.run(step, g * n, hi_refs, ho_refs, *refs[-2:])

        @pl.when(ni == 0)
        def _():
            for c in c_scr:
                c[...] = jnp.zeros(c.shape, F32)

        ps = [r[...].astype(F32) for r in p_refs]
        xs = _load_pieces(x_refs, st.in_pieces)
        cs = [c[...] for c in c_scr]
        for s, c in zip(s_refs, cs):
            s[...] = c
        res = st.f(ps, xs, cs)
        outs, ncs = res[0], res[1]
        _store_pieces(o_refs, st.out_pieces, outs)
        for c, v in zip(c_scr, ncs):
            c[...] = v
        for kr, kv in zip(k_refs, res[2] if nk else []):
            kr[...] = kv.astype(kr.dtype)

    in_specs = [_par_spec(p, pg, g) for p, pg in zip(params, st.par_per_g)]
    in_specs += [_row_spec(tm, sum(pc), off, n, False) for pc, off in zip(st.in_pieces, st.in_offs)]
    out_specs = [_row_spec(tm, sum(pc), 0, n, False) for pc in st.out_pieces]
    out_specs += [_carry_spec(s, n, False) for s in st.carry_shapes]
    out_specs += [pl.BlockSpec(s, lambda gi, ni: (ni, 0)) for s in st.kept_shapes]
    out_shape = [jax.ShapeDtypeStruct((t, g * sum(pc)), dt) for pc, dt in zip(st.out_pieces, st.out_dtypes)]
    out_shape += [jax.ShapeDtypeStruct((g, n) + s, F32) for s in st.carry_shapes]
    out_shape += [jax.ShapeDtypeStruct((n * s[0], s[1]), BF) for s in st.kept_shapes]
    res = pl.pallas_call(
        body, name=st.name + "_fwd", grid=(g, n), in_specs=in_specs + h_in, out_specs=out_specs + h_out,
        out_shape=out_shape + h_shape,
        scratch_shapes=[pltpu.VMEM(s, F32) for s in st.carry_shapes] + h_sems,
        compiler_params=_cparams(("arbitrary", "arbitrary")),
    )(*params, *inputs, *(hook.arrs if hook else []))
    if hook is not None:
        hook.results = list(res[nout + ncar + nk:])
    return list(res[:nout]), list(res[nout:nout + ncar + nk])


def _stage_bwd(st, t, params, inputs, saved, douts, dx_dtypes, hook=None):
    g, tm = st.g, min(st.tm, t)
    n = t // tm
    npar, nin, ncar = len(params), len(inputs), len(st.carry_shapes)
    nk = len(st.kept_shapes)
    flat_d = [d for ds in douts for d in ds]
    nd = len(flat_d)
    dx_idx = [i for i, dt in enumerate(dx_dtypes) if dt is not None]
    h_in, h_out, h_shape, h_sems = _hook_specs(hook)
    nh = len(h_in)

    def body(*refs):
        p_refs = refs[:npar]
        x_refs = refs[npar:npar + nin]
        s_refs = refs[npar + nin:npar + nin + ncar]
        k_refs = refs[npar + nin + ncar:npar + nin + ncar + nk]
        o = npar + nin + ncar + nk
        d_refs = refs[o:o + nd]
        hi_refs = refs[o + nd:o + nd + nh]
        o += nd + nh
        dp_refs = refs[o:o + npar]
        dx_refs = refs[o + npar:o + npar + len(dx_idx)]
        ho_refs = refs[o + npar + len(dx_idx):o + npar + len(dx_idx) + nh]
        dc_scr = refs[o + npar + len(dx_idx) + nh:o + npar + len(dx_idx) + nh + ncar]
        gi, ni = pl.program_id(0), pl.program_id(1)
        if hook is not None:
            step = gi * n + ni
            hook.run(step, g * n, hi_refs, ho_refs, *refs[-2:])

        @pl.when(ni == 0)
        def _():
            for c in dc_scr:
                c[...] = jnp.zeros(c.shape, F32)

        ps = [r[...].astype(F32) for r in p_refs]
        xs = _load_pieces(x_refs, st.in_pieces)
        cs = [s[...] for s in s_refs]
        dys = []
        k = 0
        for ds, pieces in zip(douts, st.out_pieces):
            acc = _load_pieces([d_refs[k]], [pieces])
            for j in range(1, len(ds)):
                more = _load_pieces([d_refs[k + j]], [pieces])
                acc = [a + b for a, b in zip(acc, more)]
            dys += acc
            k += len(ds)
        if nk:
            kept = [r[...].astype(F32) for r in k_refs]
            _, vjp = jax.vjp(lambda p, x, c: st.f_kept(p, x, c, kept), ps, xs, cs)
        else:
            _, vjp = jax.vjp(st.f, ps, xs, cs)
        dps, dxs, dcs = vjp((dys, [c[...] for c in dc_scr]))
        k = 0
        per_in = []
        for pieces in st.in_pieces:
            per_in.append(dxs[k:k + len(pieces)])
            k += len(pieces)
        for ref, i in zip(dx_refs, dx_idx):
            _store_pieces([ref], [st.in_pieces[i]], per_in[i])
        for c, v in zip(dc_scr, dcs):
            c[...] = v
        for ref, dp, pg in zip(dp_refs, dps, st.par_per_g):
            first = (ni == 0) if pg else ((ni == 0) & (gi == 0))

            @pl.when(first)
            def _():
                ref[...] = jnp.zeros(ref.shape, F32)

            ref[...] += dp

    in_specs = [_par_spec(p, pg, g) for p, pg in zip(params, st.par_per_g)]
    in_specs += [_row_spec(tm, sum(pc), off, n, True) for pc, off in zip(st.in_pieces, st.in_offs)]
    in_specs += [_carry_spec(s, n, True) for s in st.carry_shapes]
    in_specs += [pl.BlockSpec(s, lambda gi, ni: (n - 1 - ni, 0)) for s in st.kept_shapes]
    for ds, pc in zip(douts, st.out_pieces):
        in_specs += [_row_spec(tm, sum(pc), 0, n, True) for _ in ds]
    out_specs = [_par_spec(p, pg, g) for p, pg in zip(params, st.par_per_g)]
    out_specs += [_row_spec(tm, sum(st.in_pieces[i]), 0, n, True) for i in dx_idx]
    out_shape = [jax.ShapeDtypeStruct(p.shape, F32) for p in params]
    out_shape += [jax.ShapeDtypeStruct((t, g * sum(st.in_pieces[i])), dx_dtypes[i]) for i in dx_idx]
    res = pl.pallas_call(
        body, name=st.name + "_bwd", grid=(g, n), in_specs=in_specs + h_in, out_specs=out_specs + h_out,
        out_shape=out_shape + h_shape,
        scratch_shapes=[pltpu.VMEM(s, F32) for s in st.carry_shapes] + h_sems,
        compiler_params=_cparams(("arbitrary", "arbitrary")),
    )(*params, *inputs, *saved, *flat_d, *(hook.arrs if hook else []))
    if hook is not None:
        hook.results = list(res[npar + len(dx_idx):])
    return list(res[:npar]), list(res[npar:npar + len(dx_idx)])


def _pick(n, cap):
    if n <= cap:
        return n
    best = LANES
    for k in range(1, n // LANES + 1):
        if (n // LANES) % k == 0 and k * LANES <= cap:
            best = k * LANES
    return best


def _mm(name, a, b, mode, out_dtype=F32, tm=1024, tn=512, b_outer=False):
    m = a.shape[1] if mode == "tn" else a.shape[0]
    k = a.shape[0] if mode == "tn" else a.shape[1]
    n = b.shape[0] if mode == "nt" else b.shape[1]
    tm, tn = _pick(m, tm), _pick(n, tn)
    if b_outer:
        grid = (n // tn, m // tm)
        ij = lambda p, q: (q, p)
    else:
        grid = (m // tm, n // tn)
        ij = lambda p, q: (p, q)

    def body(a_ref, b_ref, o_ref):
        o_ref[...] = _raw_dot(a_ref[...], b_ref[...], mode).astype(o_ref.dtype)

    if mode == "tn":
        a_spec = pl.BlockSpec((k, tm), lambda p, q: (0, ij(p, q)[0]))
    else:
        a_spec = pl.BlockSpec((tm, k), lambda p, q: (ij(p, q)[0], 0))
    b_mode = dict(pipeline_mode=pl.Buffered(1)) if tn == n else {}
    if mode == "nt":
        b_spec = pl.BlockSpec((tn, k), lambda p, q: (ij(p, q)[1], 0), **b_mode)
    else:
        b_spec = pl.BlockSpec((k, tn), lambda p, q: (0, ij(p, q)[1]), **b_mode)
    return pl.pallas_call(
        body, name=name, grid=grid, in_specs=[a_spec, b_spec],
        out_specs=pl.BlockSpec((tm, tn), lambda p, q: ij(p, q)),
        out_shape=jax.ShapeDtypeStruct((m, n), out_dtype),
        compiler_params=_cparams(("arbitrary", "arbitrary")),
    )(a, b)


def _mm_multi(name, pairs, mode, out_dtype, tm=1024, tn=512):
    a0, b0 = pairs[0]
    m = a0.shape[1] if mode == "tn" else a0.shape[0]
    k = a0.shape[0] if mode == "tn" else a0.shape[1]
    n = b0.shape[0] if mode == "nt" else b0.shape[1]
    tm, tn = _pick(m, tm), _pick(n, tn)
    npair = len(pairs)

    def body(*refs):
        for p in range(npair):
            refs[2 * npair + p][...] = _raw_dot(refs[2 * p][...], refs[2 * p + 1][...], mode).astype(out_dtype)

    a_spec = pl.BlockSpec((k, tm), lambda i, j: (0, i)) if mode == "tn" else pl.BlockSpec((tm, k), lambda i, j: (i, 0))
    b_spec = pl.BlockSpec((tn, k), lambda i, j: (j, 0)) if mode == "nt" else pl.BlockSpec((k, tn), lambda i, j: (0, j))
    return pl.pallas_call(
        body, name=name, grid=(m // tm, n // tn), in_specs=[a_spec, b_spec] * npair,
        out_specs=[pl.BlockSpec((tm, tn), lambda i, j: (i, j))] * npair,
        out_shape=[jax.ShapeDtypeStruct((m, n), out_dtype)] * npair,
        compiler_params=_cparams(("arbitrary", "arbitrary")),
    )(*[x for pair in pairs for x in pair])


def _mm_cols_tn(name, pieces, b, out_dtype, tm):
    k, n = b.shape
    counts = [p.shape[1] // tm for p in pieces]
    starts = [sum(counts[:i]) for i in range(len(pieces))]
    na = len(pieces)

    def body(*refs):
        b_ref, o_ref = refs[na], refs[-1]
        i = pl.program_id(0)
        for a_ref, s, c in zip(refs[:na], starts, counts):
            @pl.when((i >= s) & (i < s + c))
            def _():
                o_ref[...] = _raw_dot(a_ref[...], b_ref[...], "tn").astype(o_ref.dtype)

    def spec(s, c):
        return pl.BlockSpec((k, tm), lambda i: (0, jnp.clip(i - s, 0, c - 1)))

    return pl.pallas_call(
        body, name=name, grid=(sum(counts),),
        in_specs=[spec(s, c) for s, c in zip(starts, counts)]
        + [pl.BlockSpec(b.shape, lambda i: (0, 0), pipeline_mode=pl.Buffered(1))],
        out_specs=pl.BlockSpec((tm, n), lambda i: (i, 0)),
        out_shape=jax.ShapeDtypeStruct((sum(counts) * tm, n), out_dtype),
        compiler_params=_cparams(("arbitrary",)),
    )(*pieces, b)


def _norm_in_proj(x, g, w_t, tm, tn):
    t, k = x.shape
    n = w_t.shape[0]
    tm, tn = _pick(t, tm), _pick(n, tn)

    def body(x_ref, g_ref, w_ref, xn_ref, z_ref):
        xn = _rms(x_ref[...], g_ref[...]).astype(BF)
        xn_ref[...] = xn
        z_ref[...] = _raw_dot(xn, w_ref[...], "nt")

    xns, z = pl.pallas_call(
        body, name="in_proj", grid=(n // tn, t // tm),
        in_specs=[pl.BlockSpec((tm, k), lambda j, i: (i, 0)), pl.BlockSpec((1, k), lambda j, i: (0, 0)),
                  pl.BlockSpec((tn, k), lambda j, i: (j, 0))],
        out_specs=[pl.BlockSpec((None, tm, k), lambda j, i: (j, i, 0)), pl.BlockSpec((tm, tn), lambda j, i: (i, j))],
        out_shape=[jax.ShapeDtypeStruct((n // tn, t, k), BF), jax.ShapeDtypeStruct((t, n), F32)],
        compiler_params=_cparams(("arbitrary", "arbitrary")),
    )(x, g, w_t)
    return xns[0], z


def _merge_out_post(z, y_a, y_b, w_out, x, g_post, g_pre2, tm):
    t = x.shape[0]
    tm = _pick(t, tm)
    w = 256
    npc = D // w
    ga0, gb0 = (IN_COLS - 2 * D) // w, (IN_COLS - D) // w

    def body(*refs):
        ga_refs, gb_refs = refs[:npc], refs[npc:2 * npc]
        ya_ref, yb_ref, w_ref, x_ref, gp_ref, g2_ref, m_ref, mix_ref, h_ref, xn_ref = refs[2 * npc:]
        parts = []
        for p in range(npc):
            cols = slice(p * w, (p + 1) * w)
            parts.append(_sigmoid(ga_refs[p][...]) * ya_ref[:, cols].astype(F32)
                         + _sigmoid(gb_refs[p][...]) * yb_ref[:, cols].astype(F32))
        merged = jnp.concatenate(parts, axis=1).astype(BF)
        m_ref[...] = merged
        mix = _raw_dot(merged, w_ref[...], "nn")
        mix_ref[...] = mix
        h1 = x_ref[...] + _rms(mix, gp_ref[...])
        h_ref[...] = h1
        xn_ref[...] = _rms(h1, g2_ref[...]).astype(BF)

    row = pl.BlockSpec((tm, D), lambda i: (i, 0))
    one = pl.BlockSpec((1, D), lambda i: (0, 0))

    def gate(b0):
        return [pl.BlockSpec((tm, w), functools.partial(lambda i, b: (i, b), b=b0 + p)) for p in range(npc)]

    return pl.pallas_call(
        body, name="merge_out_post", grid=(t // tm,),
        in_specs=gate(ga0) + gate(gb0) + [row, row, pl.BlockSpec((D, D), lambda i: (0, 0), pipeline_mode=pl.Buffered(1)),
                                          row, one, one],
        out_specs=[row, row, row, row],
        out_shape=[jax.ShapeDtypeStruct((t, D), BF), jax.ShapeDtypeStruct((t, D), F32),
                   jax.ShapeDtypeStruct((t, D), F32), jax.ShapeDtypeStruct((t, D), BF)],
        compiler_params=_cparams(("arbitrary",)),
    )(*([z] * (2 * npc)), y_a, y_b, w_out, x, g_post, g_pre2)


def _accumulate(ni, refs, vals):
    @pl.when(ni == 0)
    def _():
        for r in refs:
            r[...] = jnp.zeros(r.shape, F32)

    for r, v in zip(refs, vals):
        r[...] += v


def _dmerged_merge_bwd(dmix, w_out, z, y_a, y_b, tm):
    t = dmix.shape[0]
    tm = _pick(t, tm)
    w = 256
    npc = D // w
    ga0, gb0 = (IN_COLS - 2 * D) // w, (IN_COLS - D) // w

    def body(*refs):
        dm_ref, w_ref = refs[0], refs[1]
        ga_refs, gb_refs = refs[2:2 + npc], refs[2 + npc:2 + 2 * npc]
        ya_ref, yb_ref, dga_ref, dgb_ref, dya_ref, dyb_ref = refs[2 + 2 * npc:]
        dmerged = _raw_dot(dm_ref[...], w_ref[...], "nt")
        for p in range(npc):
            cols = slice(p * w, (p + 1) * w)
            xs = [ga_refs[p][...], gb_refs[p][...], ya_ref[:, cols].astype(F32), yb_ref[:, cols].astype(F32)]
            _, vjp = jax.vjp(lambda *a: _f_merge([], list(a), [])[0][0], *xs)
            dga, dgb, dya, dyb = vjp(dmerged[:, cols])
            dga_ref[:, cols] = dga.astype(BF)
            dgb_ref[:, cols] = dgb.astype(BF)
            dya_ref[:, cols] = dya.astype(BF)
            dyb_ref[:, cols] = dyb.astype(BF)

    row = pl.BlockSpec((tm, D), lambda i: (i, 0))

    def gate(b0):
        return [pl.BlockSpec((tm, w), functools.partial(lambda i, b: (i, b), b=b0 + p)) for p in range(npc)]

    return pl.pallas_call(
        body, name="merge_bwd", grid=(t // tm,),
        in_specs=[row, pl.BlockSpec((D, D), lambda i: (0, 0), pipeline_mode=pl.Buffered(1))] + gate(ga0) + gate(gb0)
        + [row, row],
        out_specs=[row] * 4, out_shape=[jax.ShapeDtypeStruct((t, D), BF)] * 4,
        compiler_params=_cparams(("arbitrary",)),
    )(dmix, w_out, *([z] * (2 * npc)), y_a, y_b)


def _dxn2_post1_bwd(dhu, w_up_t, x, mix, dh1, g_post, g_pre2, tm):
    t, k = dhu.shape
    tm = _pick(t, tm)

    def body(a_ref, w_ref, x_ref, m_ref, dh_ref, gp_ref, g2_ref, dgp_ref, dg2_ref, dx_ref, dm_ref):
        dxn2 = _raw_dot(a_ref[...], w_ref[...], "nn")
        _, vjp = jax.vjp(lambda gp, g2, xx, mm: _f_post1([gp, g2], [xx, mm], [])[0],
                         gp_ref[...], g2_ref[...], x_ref[...], m_ref[...])
        dgp, dg2, dx, dm = vjp([dh_ref[...], dxn2])
        _accumulate(pl.program_id(0), [dgp_ref, dg2_ref], [dgp, dg2])
        dx_ref[...] = dx
        dm_ref[...] = dm.astype(BF)

    row = pl.BlockSpec((tm, D), lambda i: (i, 0))
    one = pl.BlockSpec((1, D), lambda i: (0, 0))
    return pl.pallas_call(
        body, name="post1_bwd", grid=(t // tm,),
        in_specs=[pl.BlockSpec((tm, k), lambda i: (i, 0)),
                  pl.BlockSpec((k, D), lambda i: (0, 0), pipeline_mode=pl.Buffered(1)), row, row, row, one, one],
        out_specs=[one, one, row, row],
        out_shape=[jax.ShapeDtypeStruct((1, D), F32), jax.ShapeDtypeStruct((1, D), F32),
                   jax.ShapeDtypeStruct((t, D), F32), jax.ShapeDtypeStruct((t, D), BF)],
        compiler_params=_cparams(("arbitrary",)),
    )(dhu, w_up_t, x, mix, dh1, g_post, g_pre2)


def _dxn_pre1_bwd(pieces, w_t, x, dx_res, g, tm, token):
    t = x.shape[0]
    tm = _pick(t, tm)
    offs = [sum(p.shape[1] for p in pieces[:i]) for i in range(len(pieces))]
    na = len(pieces)

    def body(*refs):
        w_ref, x_ref, r_ref, g_ref = refs[na:na + 4]
        dg_ref, dx_ref = refs[-2:]
        dxn = None
        for a_ref, off in zip(refs[:na], offs):
            part = _raw_dot(a_ref[...], w_ref[off:off + a_ref.shape[1], :], "nn")
            dxn = part if dxn is None else dxn + part
        _, vjp = jax.vjp(lambda gg, xx: _f_pre1_residual([gg], [xx], [])[0], g_ref[...], x_ref[...])
        dg, dx = vjp([dxn, r_ref[...]])
        _accumulate(pl.program_id(0), [dg_ref], [dg])
        dx_ref[...] = dx

    row = pl.BlockSpec((tm, D), lambda i: (i, 0))
    one = pl.BlockSpec((1, D), lambda i: (0, 0))
    return pl.pallas_call(
        body, name="pre1_bwd", grid=(t // tm,),
        in_specs=[pl.BlockSpec((tm, p.shape[1]), lambda i: (i, 0)) for p in pieces]
        + [pl.BlockSpec(w_t.shape, lambda i: (0, 0), pipeline_mode=pl.Buffered(1)), row, row, one,
           pl.BlockSpec(token.shape, lambda i: (0, 0))],
        out_specs=[one, row],
        out_shape=[jax.ShapeDtypeStruct((1, D), F32), jax.ShapeDtypeStruct((t, D), F32)],
        compiler_params=_cparams(("arbitrary",)),
    )(*pieces, w_t, x, dx_res, g, token)


def _down_loss(act, w_down, g_post, h1, tgt, tm):
    t, k = act.shape
    tm = _pick(t, tm)

    def body(a_ref, w_ref, g_ref, h_ref, t_ref, loss_ref, dg_ref, dh_ref, df_ref):
        ni = pl.program_id(0)
        ff = _raw_dot(a_ref[...], w_ref[...], "nn")
        target = t_ref[...]

        def lossf(g, h1, ff):
            e = h1 + _rms(ff, g) - target
            return 0.5 * jnp.sum(jnp.mean(e * e, axis=-1))

        l, (dg, dh, df) = jax.value_and_grad(lossf, argnums=(0, 1, 2))(g_ref[...], h_ref[...], ff)

        @pl.when(ni == 0)
        def _():
            loss_ref[...] = jnp.zeros(loss_ref.shape, F32)
            dg_ref[...] = jnp.zeros(dg_ref.shape, F32)

        loss_ref[...] += jnp.full(loss_ref.shape, l, F32)
        dg_ref[...] += dg
        dh_ref[...] = dh
        df_ref[...] = df.astype(df_ref.dtype)

    row = pl.BlockSpec((tm, D), lambda ni: (ni, 0))
    one = pl.BlockSpec((1, D), lambda ni: (0, 0))
    return pl.pallas_call(
        body, name="down_loss", grid=(t // tm,),
        in_specs=[pl.BlockSpec((tm, k), lambda ni: (ni, 0)),
                  pl.BlockSpec((k, D), lambda ni: (0, 0), pipeline_mode=pl.Buffered(1)), one, row, row],
        out_specs=[pl.BlockSpec((1, LANES), lambda ni: (0, 0)), one, row, row],
        out_shape=[jax.ShapeDtypeStruct((1, LANES), F32), jax.ShapeDtypeStruct((1, D), F32),
                   jax.ShapeDtypeStruct((t, D), F32), jax.ShapeDtypeStruct((t, D), BF)],
        compiler_params=_cparams(("arbitrary",)),
    )(act, w_down, g_post, h1, tgt)


_ANY = pl.BlockSpec(memory_space=pl.ANY)


def _all_gather(name, blks):
    na = len(blks)
    ns = 8

    def body(*refs):
        x_refs, out_refs = refs[:na], refs[na:2 * na]
        send_sems, recv_sems, local_sems = refs[2 * na:]
        x, y, cc = lax.axis_index("x"), lax.axis_index("y"), lax.axis_index("c")
        sibling, xn, yn = (x, y, 1 - cc), (1 - x, y, cc), (x, 1 - y, cc)

        def num(px, py, pc):
            return 4 * px + 2 * py + pc

        def copy(a, k, to, src, dst):
            return pltpu.make_async_remote_copy(src_ref=src, dst_ref=dst, send_sem=send_sems.at[ns * a + k],
                                                recv_sem=recv_sems.at[ns * a + k], device_id=to, device_id_type=MESH)

        def halves(a, blk):
            h = blks[a].shape[0] // 2
            return out_refs[a].at[blk, pl.ds(0, h)], out_refs[a].at[blk, pl.ds(h, h)]

        mine, sends = [], []
        for a in range(na):
            o = out_refs[a]
            m = pltpu.make_async_copy(x_refs[a], o.at[num(x, y, cc)], local_sems.at[a])
            m.start()
            mine.append(m)
            own = o.at[num(x, y, cc)]
            sends.append([copy(a, 0, sibling, x_refs[a], own), copy(a, 1, xn, x_refs[a], own),
                          copy(a, 2, yn, x_refs[a], own)])
            for cp in sends[a]:
                cp.start()
        for a in range(na):
            o = out_refs[a]
            bx, by, bd = num(1 - x, y, cc), num(x, 1 - y, cc), num(1 - x, 1 - y, cc)
            copy(a, 1, xn, o.at[bx], o.at[bx]).wait_recv()
            more = [copy(a, 3, yn, halves(a, bx)[0], halves(a, bx)[0]), copy(a, 5, sibling, o.at[bx], o.at[bx])]
            for cp in more:
                cp.start()
            sends[a] += more
        for a in range(na):
            o = out_refs[a]
            bx, by, bd = num(1 - x, y, cc), num(x, 1 - y, cc), num(1 - x, 1 - y, cc)
            copy(a, 2, yn, o.at[by], o.at[by]).wait_recv()
            more = [copy(a, 4, xn, halves(a, by)[1], halves(a, by)[1]), copy(a, 6, sibling, o.at[by], o.at[by])]
            for cp in more:
                cp.start()
            sends[a] += more
        for a in range(na):
            o = out_refs[a]
            bd = num(1 - x, 1 - y, cc)
            copy(a, 3, yn, halves(a, bd)[0], halves(a, bd)[0]).wait_recv()
            copy(a, 4, xn, halves(a, bd)[1], halves(a, bd)[1]).wait_recv()
            fw = copy(a, 7, sibling, o.at[bd], o.at[bd])
            fw.start()
            sends[a].append(fw)
        for a in range(na):
            o = out_refs[a]
            for k, blk in ((0, num(x, y, 1 - cc)), (5, num(1 - x, y, 1 - cc)), (6, num(x, 1 - y, 1 - cc)),
                           (7, num(1 - x, 1 - y, 1 - cc))):
                copy(a, k, sibling, o.at[blk], o.at[blk]).wait_recv()
            for cp in sends[a]:
                cp.wait_send()
        for m in mine:
            m.wait()

    res = pl.pallas_call(
        body, name=name, in_specs=[_ANY] * na, out_specs=[_ANY] * na,
        out_shape=[jax.ShapeDtypeStruct((N_DEV,) + b.shape, b.dtype) for b in blks],
        scratch_shapes=[pltpu.SemaphoreType.DMA((ns * na,)), pltpu.SemaphoreType.DMA((ns * na,)),
                        pltpu.SemaphoreType.DMA((na,))],
    )(*blks)
    return list(res)


def _all_gather_small(name, blk):
    def body(x_ref, out_ref, ssem, rsem, lsem):
        x, y, c = lax.axis_index("x"), lax.axis_index("y"), lax.axis_index("c")
        me = 4 * x + 2 * y + c
        mine = pltpu.make_async_copy(x_ref, out_ref.at[me], lsem)
        mine.start()
        cps = []
        for j in range(1, N_DEV):
            px = 1 - x if j & 4 else x
            py = 1 - y if j & 2 else y
            pc = 1 - c if j & 1 else c
            cps.append(pltpu.make_async_remote_copy(src_ref=x_ref, dst_ref=out_ref.at[me], send_sem=ssem.at[j - 1],
                                                    recv_sem=rsem.at[j - 1], device_id=(px, py, pc),
                                                    device_id_type=MESH))
        for cp in cps:
            cp.start()
        for cp in cps:
            cp.wait()
        mine.wait()

    return pl.pallas_call(
        body, name=name, in_specs=[_ANY], out_specs=_ANY,
        out_shape=jax.ShapeDtypeStruct((N_DEV,) + blk.shape, blk.dtype),
        scratch_shapes=[pltpu.SemaphoreType.DMA((N_DEV - 1,)), pltpu.SemaphoreType.DMA((N_DEV - 1,)),
                        pltpu.SemaphoreType.DMA],
    )(blk)


def _reduce_pair(g8s):
    na = len(g8s)

    def body(*refs):
        g_refs, recv_refs = refs[:na], refs[na:2 * na]
        ssem, rsem = refs[2 * na:]
        x, y, cc = lax.axis_index("x"), lax.axis_index("y"), lax.axis_index("c")
        chips = [(x, y), (1 - x, y), (x, 1 - y), (1 - x, 1 - y)]
        sib = (x, y, 1 - cc)
        for a in range(na):
            for k, (cx, cy) in enumerate(chips):
                pltpu.make_async_remote_copy(
                    src_ref=g_refs[a].at[4 * cx + 2 * cy + 1 - cc], dst_ref=recv_refs[a].at[k],
                    send_sem=ssem.at[a], recv_sem=rsem.at[a], device_id=sib, device_id_type=MESH).start()
        for a in range(na):
            pltpu.make_async_remote_copy(src_ref=recv_refs[a], dst_ref=recv_refs[a], send_sem=ssem.at[a],
                                         recv_sem=rsem.at[a], device_id=sib, device_id_type=MESH).wait()

    res = pl.pallas_call(
        body, name="reduce_pair", in_specs=[_ANY] * na, out_specs=[_ANY] * na,
        out_shape=[jax.ShapeDtypeStruct((4,) + g.shape[1:], g.dtype) for g in g8s],
        scratch_shapes=[pltpu.SemaphoreType.DMA((na,)), pltpu.SemaphoreType.DMA((na,))],
    )(*g8s)
    return list(res)


_HBM = pl.BlockSpec(memory_space=pltpu.HBM)
_SEM = pl.BlockSpec(memory_space=pltpu.SEMAPHORE)
_EFFECT = pltpu.SideEffectType.DATAFLOW_SIDE_EFFECTING


def _chip_swap_copies(s_refs, land_refs, ssem, rsem):
    x, y, c = lax.axis_index("x"), lax.axis_index("y"), lax.axis_index("c")
    targets = [(1 - x, y, c), (x, 1 - y, c), (1 - x, 1 - y, c)]
    return [pltpu.make_async_remote_copy(src_ref=s.at[k], dst_ref=d.at[k], send_sem=ssem.at[3 * a + k],
                                         recv_sem=rsem.at[3 * a + k], device_id=targets[k], device_id_type=MESH)
            for a, (s, d) in enumerate(zip(s_refs, land_refs)) for k in range(3)]


def _chip_swap_start(sends):
    na = len(sends)

    def body(*refs):
        cps = _chip_swap_copies(refs[:na], refs[na:2 * na], refs[2 * na], refs[2 * na + 1])
        for cp in cps:
            cp.start()
        token = refs[-1]
        token[...] = jnp.zeros(token.shape, token.dtype)

    bufs = [pltpu.HBM(s.shape, s.dtype) for s in sends]
    res = pl.pallas_call(
        body, name="chip_swap_start",
        out_shape=[pltpu.SemaphoreType.DMA((3 * na,)), pltpu.SemaphoreType.DMA((3 * na,))] + bufs + bufs
        + [jax.ShapeDtypeStruct((8, LANES), F32)],
        in_specs=[_HBM] * (2 * na), out_specs=[_SEM, _SEM] + [_HBM] * (2 * na) + [pl.BlockSpec(memory_space=pltpu.VMEM)],
        input_output_aliases={i: 2 + i for i in range(2 * na)},
        compiler_params=pltpu.CompilerParams(has_side_effects=_EFFECT),
    )(*[pltpu.with_memory_space_constraint(s, pltpu.HBM) for s in sends],
      *[pltpu.with_memory_space_constraint(lax.empty(s.shape, s.dtype), pltpu.HBM) for s in sends])
    return res[0], res[1], list(res[2:2 + na]), list(res[2 + na:2 + 2 * na]), res[-1]


def _chip_swap_wait(ssem, rsem, srcs, lands, after):
    na = len(srcs)

    def body(*refs):
        cps = _chip_swap_copies(refs[:na], refs[na:2 * na], refs[2 * na], refs[2 * na + 1])
        for cp in cps:
            cp.wait_send()
            cp.wait_recv()

    bufs = [pltpu.HBM(s.shape, s.dtype) for s in srcs]
    res = pl.pallas_call(
        body, name="chip_swap_wait", out_shape=bufs + bufs,
        in_specs=[_HBM] * (2 * na) + [_SEM, _SEM, _ANY], out_specs=[_HBM] * (2 * na),
        input_output_aliases={i: i for i in range(2 * na)},
        compiler_params=pltpu.CompilerParams(has_side_effects=_EFFECT),
    )(*srcs, *lands, ssem, rsem, after)
    return list(res[na:])


def _pick_rows(r, c, budget=TILE_BYTES):
    if r * c * 4 <= budget or r % 16:
        return r
    best = 16
    for tr in range(16, r, 16):
        if r % tr == 0 and tr * c * 4 <= budget:
            best = tr
    return best


def _pair_sum(name, idx4, g8, recv4):
    _, r, c = g8.shape
    tr = _pick_rows(r, c, 2 * TILE_BYTES)

    def body(idx_ref, a_ref, b_ref, o0_ref, o3_ref):
        k = pl.program_id(1)
        s = a_ref[...].astype(F32) + b_ref[...].astype(F32)

        @pl.when(k == 0)
        def _():
            o0_ref[...] = s

        @pl.when(k > 0)
        def _():
            o3_ref[...] = s.astype(BF)

    spec = pltpu.PrefetchScalarGridSpec(
        num_scalar_prefetch=1, grid=(r // tr, 4),
        in_specs=[pl.BlockSpec((None, tr, c), lambda i, k, idx: (idx[k], i, 0)),
                  pl.BlockSpec((None, tr, c), lambda i, k, idx: (k, i, 0))],
        out_specs=[pl.BlockSpec((tr, c), lambda i, k, idx: (i, 0)),
                   pl.BlockSpec((None, tr, c), lambda i, k, idx: (jnp.maximum(k - 1, 0), i, 0))])
    return pl.pallas_call(
        body, name=name, grid_spec=spec,
        out_shape=[jax.ShapeDtypeStruct((r, c), F32), jax.ShapeDtypeStruct((3, r, c), BF)],
        compiler_params=_cparams(("arbitrary", "arbitrary")),
    )(idx4, g8, recv4)


def _adamw(w, g, m, v):
    m = ADAM_B1 * m + (1.0 - ADAM_B1) * g
    v = ADAM_B2 * v + (1.0 - ADAM_B2) * jnp.square(g)
    m_hat = m / (1.0 - ADAM_B1 ** ADAM_STEP)
    v_hat = v / (1.0 - ADAM_B2 ** ADAM_STEP)
    delta = -ADAM_LR * (m_hat / (jnp.sqrt(v_hat) + ADAM_EPS) + ADAM_WD * w)
    return delta, m, v


def _adam_sharded(name, idx1, own, recv, w, m, v, after=None):
    r, c = w.shape
    tr = _pick_rows(r, c, 2 * TILE_BYTES)
    nj = recv.shape[0]
    extra = [] if after is None else [after]

    def body(idx_ref, p_ref, r_ref, w_ref, m_ref, v_ref, *rest):
        g_out, d_out, m_out, v_out = rest[-4:]
        g = p_ref[...].astype(F32)
        for k in range(nj):
            g = g + r_ref[k].astype(F32)
        d, mn, vn = _adamw(w_ref[...], g, m_ref[...], v_ref[...])
        g_out[...] = g
        d_out[...] = d
        m_out[...] = mn
        v_out[...] = vn

    row = pl.BlockSpec((tr, c), lambda i, idx: (i, 0))
    spec = pltpu.PrefetchScalarGridSpec(
        num_scalar_prefetch=1, grid=(r // tr,),
        in_specs=[pl.BlockSpec((None, tr, c), lambda i, idx: (idx[0], i, 0)),
                  pl.BlockSpec((nj, tr, c), lambda i, idx: (0, i, 0)), row, row, row]
        + [pl.BlockSpec(e.shape, lambda i, idx: (0, 0)) for e in extra],
        out_specs=[row] * 4)
    return pl.pallas_call(
        body, name=name, grid_spec=spec, out_shape=[jax.ShapeDtypeStruct((r, c), F32)] * 4,
        compiler_params=_cparams(("arbitrary",)),
    )(idx1, own, recv, w, m, v, *extra)


def _repl_rows():
    rows, r = {}, 0
    for name, cols in REPL:
        rows[name] = r
        r += REPL_ROWS.get(name, 1) * ((cols + D - 1) // D)
    return rows


LOSS_ROW = 24


def _pack_replicated(grads, loss_acc, after):
    rows = _repl_rows()
    names = [n for n, _ in REPL]

    def body(*refs):
        o_ref = refs[-1]
        o_ref[...] = jnp.zeros(o_ref.shape, F32)
        o_ref[LOSS_ROW:LOSS_ROW + 1, 0:LANES] = refs[len(names)][...]
        for name, ref in zip(names, refs[:len(names)]):
            r0 = rows[name]
            nr, nc = ref.shape
            if nc <= D:
                o_ref[r0:r0 + nr, 0:nc] = ref[...]
            else:
                for j in range((nc + D - 1) // D):
                    lo, hi = j * D, min(nc, (j + 1) * D)
                    o_ref[r0 + j:r0 + j + 1, 0:hi - lo] = ref[:, lo:hi]

    return pl.pallas_call(body, name="pack_replicated", out_shape=jax.ShapeDtypeStruct((REPL_TOTAL, D), F32),
                          in_specs=[pl.BlockSpec(memory_space=pltpu.VMEM)] * (len(names) + 1) + [_ANY] * len(after),
                          compiler_params=_cparams())(*[grads[n] for n in names], loss_acc, *after)


def _adam_replicated(g8, ws, ms, vs):
    rows = _repl_rows()
    names = [n for n, _ in REPL]
    np_ = len(names)

    def body(*refs):
        g_ref = refs[0]
        w_refs, m_refs, v_refs = refs[1:1 + np_], refs[1 + np_:1 + 2 * np_], refs[1 + 2 * np_:1 + 3 * np_]
        outs = refs[1 + 3 * np_:1 + 7 * np_]
        scr = refs[-1]
        g = g_ref[0]
        for k in range(1, N_DEV):
            g = g + g_ref[k]
        scr[...] = g
        refs[1 + 7 * np_][...] = scr[LOSS_ROW:LOSS_ROW + 1, 0:LANES]
        for i, name in enumerate(names):
            r0 = rows[name]
            nr, nc = w_refs[i].shape
            if nc <= D:
                gi = scr[r0:r0 + nr, 0:nc]
            else:
                parts = []
                for j in range((nc + D - 1) // D):
                    lo, hi = j * D, min(nc, (j + 1) * D)
                    parts.append(scr[r0 + j:r0 + j + 1, 0:hi - lo])
                gi = jnp.concatenate(parts, axis=1)
            d, mn, vn = _adamw(w_refs[i][...], gi, m_refs[i][...], v_refs[i][...])
            outs[i][...] = gi
            outs[np_ + i][...] = d
            outs[2 * np_ + i][...] = mn
            outs[3 * np_ + i][...] = vn

    shp = [jax.ShapeDtypeStruct(w.shape, F32) for w in ws]
    res = pl.pallas_call(body, name="adam_replicated", out_shape=shp * 4 + [jax.ShapeDtypeStruct((1, LANES), F32)],
                         scratch_shapes=[pltpu.VMEM((REPL_TOTAL, D), F32)], compiler_params=_cparams(),
                         )(g8, *ws, *ms, *vs)
    return [dict(zip(names, res[k * np_:(k + 1) * np_])) for k in range(4)], res[-1]


_WEIGHTS = ("attn_pre_norm", "w_in", "hgrn_lb", "hgrn_gnorm", "w_branch_a", "rwkv_mu", "rwkv_w0", "rwkv_w2",
            "rwkv_a0", "rwkv_a2", "rwkv_g2", "rwkv_k_k", "rwkv_k_a", "rwkv_r_k", "rwkv_ln_w", "rwkv_ln_b",
            "w_branch_b", "w_out", "attn_post_norm", "ffn_pre_norm", "w_up", "conv_w", "conv_b", "w_down",
            "ffn_post_norm")
_BIG = ("w_in", "w_up", "w_down", "w_branch_a", "w_branch_b", "w_out")


def _stages():
    one = [D]
    hw = HG_K * HG_PER_STEP
    rw = LANES * RW_PAIRS_PER_STEP
    return dict(
        mixers=_Stage("mixers", _f_mixers, 1, 2 * RW_CHUNK, [False] * 13, [[D] * 7 + [LANES, LANES]], [0],
                      [(hw, HG_K), (1, RW_COLS), (rw, LANES)], [one, one], [BF, BF],
                      kept_shapes=[(2 * RW_KEPT * RW_PAIRS_PER_STEP * 2 * RW_CHUNK, LANES)], f_kept=_f_mixers_kept),
        conv=_Stage("conv", _f_conv, 1, 128, [False, False], [[DFF, DFF]], [0], [(1, 2 * DFF), (1, 2 * DFF)],
                    [[DFF]], [BF]),
    )


def _cols_to_blocks(w, per):
    return w.reshape(w.shape[0], N_DEV, per).transpose(1, 0, 2)


def _blocks_to_cols(g):
    return g.transpose(1, 0, 2).reshape(g.shape[1], N_DEV * g.shape[2])


def kernel(x, attn_pre_norm, w_in, hgrn_lb, hgrn_gnorm, w_branch_a, rwkv_mu, rwkv_w0, rwkv_w2, rwkv_a0, rwkv_a2, rwkv_g2, rwkv_k_k, rwkv_k_a, rwkv_r_k, rwkv_ln_w, rwkv_ln_b, w_branch_b, w_out, attn_post_norm, ffn_pre_norm, w_up, conv_w, conv_b, w_down, ffn_post_norm, loss_target, m_attn_pre_norm, m_w_in, m_hgrn_lb, m_hgrn_gnorm, m_w_branch_a, m_rwkv_mu, m_rwkv_w0, m_rwkv_w2, m_rwkv_a0, m_rwkv_a2, m_rwkv_g2, m_rwkv_k_k, m_rwkv_k_a, m_rwkv_r_k, m_rwkv_ln_w, m_rwkv_ln_b, m_w_branch_b, m_w_out, m_attn_post_norm, m_ffn_pre_norm, m_w_up, m_conv_w, m_conv_b, m_w_down, m_ffn_post_norm, v_attn_pre_norm, v_w_in, v_hgrn_lb, v_hgrn_gnorm, v_w_branch_a, v_rwkv_mu, v_rwkv_w0, v_rwkv_w2, v_rwkv_a0, v_rwkv_a2, v_rwkv_g2, v_rwkv_k_k, v_rwkv_k_a, v_rwkv_r_k, v_rwkv_ln_w, v_rwkv_ln_b, v_w_branch_b, v_w_out, v_attn_post_norm, v_ffn_pre_norm, v_w_up, v_conv_w, v_conv_b, v_w_down, v_ffn_post_norm):
    w = dict(attn_pre_norm=attn_pre_norm, w_in=w_in, hgrn_lb=hgrn_lb, hgrn_gnorm=hgrn_gnorm, w_branch_a=w_branch_a, rwkv_mu=rwkv_mu, rwkv_w0=rwkv_w0, rwkv_w2=rwkv_w2, rwkv_a0=rwkv_a0, rwkv_a2=rwkv_a2, rwkv_g2=rwkv_g2, rwkv_k_k=rwkv_k_k, rwkv_k_a=rwkv_k_a, rwkv_r_k=rwkv_r_k, rwkv_ln_w=rwkv_ln_w, rwkv_ln_b=rwkv_ln_b, w_branch_b=w_branch_b, w_out=w_out, attn_post_norm=attn_post_norm, ffn_pre_norm=ffn_pre_norm, w_up=w_up, conv_w=conv_w, conv_b=conv_b, w_down=w_down, ffn_post_norm=ffn_post_norm)
    mo = dict(attn_pre_norm=m_attn_pre_norm, w_in=m_w_in, hgrn_lb=m_hgrn_lb, hgrn_gnorm=m_hgrn_gnorm, w_branch_a=m_w_branch_a, rwkv_mu=m_rwkv_mu, rwkv_w0=m_rwkv_w0, rwkv_w2=m_rwkv_w2, rwkv_a0=m_rwkv_a0, rwkv_a2=m_rwkv_a2, rwkv_g2=m_rwkv_g2, rwkv_k_k=m_rwkv_k_k, rwkv_k_a=m_rwkv_k_a, rwkv_r_k=m_rwkv_r_k, rwkv_ln_w=m_rwkv_ln_w, rwkv_ln_b=m_rwkv_ln_b, w_branch_b=m_w_branch_b, w_out=m_w_out, attn_post_norm=m_attn_post_norm, ffn_pre_norm=m_ffn_pre_norm, w_up=m_w_up, conv_w=m_conv_w, conv_b=m_conv_b, w_down=m_w_down, ffn_post_norm=m_ffn_post_norm)
    vo = dict(attn_pre_norm=v_attn_pre_norm, w_in=v_w_in, hgrn_lb=v_hgrn_lb, hgrn_gnorm=v_hgrn_gnorm, w_branch_a=v_w_branch_a, rwkv_mu=v_rwkv_mu, rwkv_w0=v_rwkv_w0, rwkv_w2=v_rwkv_w2, rwkv_a0=v_rwkv_a0, rwkv_a2=v_rwkv_a2, rwkv_g2=v_rwkv_g2, rwkv_k_k=v_rwkv_k_k, rwkv_k_a=v_rwkv_k_a, rwkv_r_k=v_rwkv_r_k, rwkv_ln_w=v_rwkv_ln_w, rwkv_ln_b=v_rwkv_ln_b, w_branch_b=v_w_branch_b, w_out=v_w_out, attn_post_norm=v_attn_post_norm, ffn_pre_norm=v_ffn_pre_norm, w_up=v_w_up, conv_w=v_conv_w, conv_b=v_conv_b, w_down=v_w_down, ffn_post_norm=v_ffn_post_norm)

    t = x.shape[1]
    x2 = x.reshape(t, D)
    tgt = loss_target.reshape(t, D)
    st = _stages()

    me = 4 * lax.axis_index("x") + 2 * lax.axis_index("y") + lax.axis_index("c")
    small = jnp.concatenate([rwkv_w2[0], rwkv_a2[0], rwkv_g2[0]], axis=0).astype(BF)
    g_in, g_small = _all_gather("gather_weights", [w_in[0].T.astype(BF), small])
    fw_in_t = g_in.reshape(IN_COLS, D)
    z64 = jnp.zeros((64, D), BF)
    w2p = jnp.concatenate([_blocks_to_cols(g_small[:, 0:64]), z64], axis=0)
    a2p = jnp.concatenate([z64, _blocks_to_cols(g_small[:, 64:128])], axis=0)
    g2f = _blocks_to_cols(g_small[:, 128:256])
    conv_bits = jnp.pad(lax.bitcast_convert_type(conv_w[0], BF).reshape(3, 2 * 704), ((0, 29), (0, 0)))
    late = [w_up[0].T.astype(BF)] + [w[k][0].astype(BF) for k in _BIG[2:]] + [conv_bits]
    late_gather = _Exchange("gather2", late)
    r_k = rwkv_r_k.reshape(1, D)

    xn, z = _norm_in_proj(x2, attn_pre_norm, fw_in_t, 512, 4736)
    mix_par = [hgrn_lb, hgrn_gnorm, rwkv_mu, rwkv_w0, w2p, rwkv_a0, a2p, g2f, rwkv_k_k, rwkv_k_a,
               rwkv_ln_w, rwkv_ln_b, r_k]
    mix_in = [z]
    (o_a, o_b), mix_saved = _stage_fwd(st["mixers"], t, mix_par, mix_in, hook=late_gather)
    gl = [lax.dynamic_update_slice(g, own[None], (me, 0, 0)) for g, own in zip(late_gather.results, late)]
    fw_up_t = gl[0].reshape(2 * DFF, D)
    fw_down = gl[1].reshape(DFF, D)
    fw_a, fw_b, fw_out = (g.reshape(D, D) for g in gl[2:5])
    conv_full = _blocks_to_cols(lax.bitcast_convert_type(gl[5][:, :3].reshape(N_DEV, 3, 704, 2), F32))
    y_a, y_b = _mm_multi("branches", [(o_a, fw_a), (o_b, fw_b)], "nn", BF)
    merged, mix, h1, xn2 = _merge_out_post(z, y_a, y_b, fw_out, x2, attn_post_norm, ffn_pre_norm, 512)
    hu = _mm("up_proj", xn2, fw_up_t, "nt", F32, tm=1024, tn=1408)
    conv_par = [conv_full, conv_b]
    (act,), conv_saved = _stage_fwd(st["conv"], t, conv_par, [hu])

    loss_acc, d_ffn_post, dh1, dff = _down_loss(act, fw_down, ffn_post_norm, h1, tgt, 512)
    dact = _mm("d_act", dff, fw_down, "nt", BF, tm=1024, tn=1408)
    dw_down = _mm("dw_down", act, dff, "tn", BF, tm=1408, tn=512)
    (dcw, dcb), (dhu,) = _stage_bwd(st["conv"], t, conv_par, [hu], conv_saved, [[dact]], [BF])
    dw_up_t = _mm("dw_up", dhu, xn2, "tn", BF, tm=1408, tn=1024)
    d_post, d_pre2, dx_a, dmix = _dxn2_post1_bwd(dhu, fw_up_t, x2, mix, dh1, attn_post_norm, ffn_pre_norm, 512)
    dga, dgb, dy_a, dy_b = _dmerged_merge_bwd(dmix, fw_out, z, y_a, y_b, 512)
    do_a, do_b = _mm_multi("d_branches", [(dy_a, fw_a), (dy_b, fw_b)], "nt", BF)
    dw_a, dw_b, dw_out = _mm_multi("dw_branches", [(o_a, dy_a), (o_b, dy_b), (merged, dmix)], "tn", BF)
    early = [dw_up_t.reshape(N_DEV, 704, D), dw_down.reshape(N_DEV, 352, D), dw_a.reshape(N_DEV, 128, D),
             dw_b.reshape(N_DEV, 128, D), dw_out.reshape(N_DEV, 128, D), _cols_to_blocks(dcw.astype(BF), 704)]
    early_scatter = _Exchange("scatter", early)
    mix_dp, dz_hr = _stage_bwd(st["mixers"], t, mix_par, mix_in, mix_saved, [[do_a], [do_b]], [BF],
                               hook=early_scatter)
    d_lb, d_gn, d_mu, d_w0, d_w2p, d_a0, d_a2p, d_g2, d_kk, d_ka, d_lnw, d_lnb, d_rk = mix_dp
    dz = dz_hr + [dga, dgb]
    dw_in_t = _mm_cols_tn("dw_in", dz, xn, BF, 256)

    ax, ay, ac = lax.axis_index("x"), lax.axis_index("y"), lax.axis_index("c")
    idx4 = jnp.stack([4 * cx + 2 * cy + ac for cx, cy in ((ax, ay), (1 - ax, ay), (ax, 1 - ay), (1 - ax, 1 - ay))])
    idx4 = idx4.astype(jnp.int32)
    idx_me, idx_0 = idx4[0:1], jnp.zeros((1,), jnp.int32)
    d_small = jnp.concatenate([d_w2p[:64], d_a2p[64:], d_g2], axis=0).astype(BF)
    g8s = [dw_in_t.reshape(N_DEV, 1184, D), _cols_to_blocks(d_small, LANES)]
    recv4s = _reduce_pair(g8s)
    sums = [_pair_sum("pair_sum_" + n, idx4, g, r) for n, g, r in zip(("w_in", "small"), g8s, recv4s)]
    swap_ssem, swap_rsem, swap_srcs, swap_lands, token = _chip_swap_start([s[1] for s in sums])
    d_pre1, dx = _dxn_pre1_bwd(dz, fw_in_t, x2, dx_a, attn_pre_norm, 256, token)
    grad_x = dx.reshape(x.shape)

    sh_out = [dict() for _ in range(4)]
    done = []
    for n, own, recv in zip(_BIG[1:] + ("conv_w",), early, early_scatter.results):
        tr = (lambda a: a.T) if n == "w_up" else (lambda a: a)
        res = _adam_sharded("adam_" + n, idx_me, own, recv, *[tr(src[n][0]) for src in (w, mo, vo)], after=token)
        done.append(res[0])
        for kind in range(4):
            sh_out[kind][n] = tr(res[kind])[None]

    rg = dict(attn_pre_norm=d_pre1, hgrn_lb=d_lb, hgrn_gnorm=d_gn, rwkv_mu=d_mu, rwkv_w0=d_w0, rwkv_a0=d_a0,
              rwkv_k_k=d_kk, rwkv_k_a=d_ka, rwkv_r_k=d_rk, rwkv_ln_w=d_lnw, rwkv_ln_b=d_lnb, attn_post_norm=d_post,
              ffn_pre_norm=d_pre2, conv_b=dcb, ffn_post_norm=d_ffn_post)
    g8 = _all_gather_small("gather_small_grads", _pack_replicated(rg, loss_acc, done))
    rnames = [n for n, _ in REPL]
    flat = lambda src: [src[n].reshape(1, D) if n == "rwkv_r_k" else src[n] for n in rnames]
    rp_out, loss_row = _adam_replicated(g8, flat(w), flat(mo), flat(vo))
    loss = loss_row[0, 0]
    recv3s = _chip_swap_wait(swap_ssem, swap_rsem, swap_srcs, swap_lands, rp_out[0]["attn_pre_norm"])
    for kind in range(4):
        rp_out[kind]["rwkv_r_k"] = rp_out[kind]["rwkv_r_k"].reshape(rwkv_r_k.shape)

    def small_of(src):
        return jnp.concatenate([src["rwkv_w2"][0], src["rwkv_a2"][0], src["rwkv_g2"][0]], axis=0)

    res = _adam_sharded("adam_w_in", idx_0, sums[0][0][None], recv3s[0], *[src["w_in"][0].T for src in (w, mo, vo)])
    res_s = _adam_sharded("adam_small", idx_0, sums[1][0][None], recv3s[1], *[small_of(src) for src in (w, mo, vo)])
    for kind in range(4):
        sh_out[kind]["w_in"] = res[kind].T[None]
        sh_out[kind]["rwkv_w2"] = res_s[kind][0:64][None]
        sh_out[kind]["rwkv_a2"] = res_s[kind][64:128][None]
        sh_out[kind]["rwkv_g2"] = res_s[kind][128:256][None]

    outs = [loss, grad_x]
    for kind in range(4):
        for name in _WEIGHTS:
            outs.append(sh_out[kind][name] if name in sh_out[kind] else rp_out[kind][name])
    return tuple(outs)
```

```python
import functools

import jax
import jax.numpy as jnp
from jax import lax
from jax.experimental import pallas as pl
from jax.experimental.pallas import tpu as pltpu

F32 = jnp.float32
BF = jnp.bfloat16
MESH = pl.DeviceIdType.MESH

D = 1024
HG_HEADS = 8
HG_K = 128
HG_CHUNK = 32
HG_SCALE = HG_K ** -0.5
HG_PER_STEP = 8
RW_HEADS = 16
RW_N = 64
RW_CHUNK = 64
RW_PAIRS_PER_STEP = 8
DFF = 2816
IN_COLS = 9472
RW_COLS = 3328
EPS = 1e-6
GN_EPS = 1e-5 * RW_N
ADAM_LR = 0.001
ADAM_B1 = 0.9
ADAM_B2 = 0.999
ADAM_EPS = 1e-08
ADAM_WD = 0.01
ADAM_STEP = 10
N_DEV = 8
LANES = 128
VMEM_LIMIT = 56 * 1024 * 1024
TILE_BYTES = 1280 * 1024

REPL = (("attn_pre_norm", 1024), ("hgrn_lb", 1024), ("hgrn_gnorm", 1024), ("rwkv_mu", 3328), ("rwkv_w0", 1024),
        ("rwkv_a0", 1024), ("rwkv_k_k", 1024), ("rwkv_k_a", 1024), ("rwkv_r_k", 1024), ("rwkv_ln_w", 1024),
        ("rwkv_ln_b", 1024), ("attn_post_norm", 1024), ("ffn_pre_norm", 1024), ("conv_b", 5632), ("ffn_post_norm", 1024))
REPL_ROWS = {"hgrn_lb": 2}
REPL_TOTAL = 32


def _cparams(sem=None, **kw):
    return pltpu.CompilerParams(dimension_semantics=sem, vmem_limit_bytes=VMEM_LIMIT, **kw)


_DN = {"nn": ((1,), (0,)), "nt": ((1,), (1,)), "tn": ((0,), (0,))}


def _raw_dot(a, b, mode):
    return lax.dot_general(a.astype(BF), b.astype(BF), (_DN[mode], ((), ())), preferred_element_type=F32)


@functools.partial(jax.custom_vjp, nondiff_argnums=(2,))
def _dot(a, b, mode):
    return _raw_dot(a, b, mode)


def _dot_fwd(a, b, mode):
    return _raw_dot(a, b, mode), (a, b)


def _dot_bwd(mode, res, g):
    a, b = res
    if mode == "nn":
        return _dot(g, b, "nt"), _dot(a, g, "tn")
    if mode == "nt":
        return _dot(g, b, "nn"), _dot(g, a, "tn")
    return _dot(b, g, "nt"), _dot(a, g, "nn")


_dot.defvjp(_dot_fwd, _dot_bwd)


def _bf_pieces(x, n):
    out, r = [], x
    for i in range(n):
        p = r.astype(BF)
        out.append(p)
        if i + 1 < n:
            r = r - p.astype(F32)
    return out


def _raw_split_dot(x, e, mode, n, x_left):
    eb = e.astype(BF)
    acc = None
    for p in _bf_pieces(x, n):
        ops = (p, eb) if x_left else (eb, p)
        t = lax.dot_general(*ops, (_DN[mode], ((), ())), preferred_element_type=F32)
        acc = t if acc is None else acc + t
    return acc


def _raw_headsum(x):
    t = x.shape[0]
    i = lax.broadcasted_iota(jnp.int32, (LANES, LANES), 0)
    j = lax.broadcasted_iota(jnp.int32, (LANES, LANES), 1)
    same = jnp.where((i >= RW_N) == (j >= RW_N), 1.0, 0.0).astype(F32)
    groups = x.shape[1] // LANES
    rows = jnp.concatenate([x[:, q * LANES:(q + 1) * LANES] for q in range(groups)], axis=0)
    s = _raw_split_dot(rows, same, "nn", 2, True)
    return jnp.concatenate([s[q * t:(q + 1) * t] for q in range(groups)], axis=1)


@jax.custom_vjp
def _headsum(x):
    return _raw_headsum(x)


def _headsum_fwd(x):
    return _raw_headsum(x), None


def _headsum_bwd(_, g):
    return (_raw_headsum(g),)


_headsum.defvjp(_headsum_fwd, _headsum_bwd)


@functools.partial(jax.custom_vjp, nondiff_argnums=(2,))
def _tdot(tri, x, n):
    return _raw_split_dot(x, tri, "nn", n, False)


def _tdot_fwd(tri, x, n):
    return _raw_split_dot(x, tri, "nn", n, False), tri


def _tdot_bwd(n, tri, g):
    return jnp.zeros_like(tri), _raw_split_dot(g, tri, "tn", n, False)


_tdot.defvjp(_tdot_fwd, _tdot_bwd)


def _row(x, i):
    r = lax.broadcasted_iota(jnp.int32, x.shape, 0)
    return jnp.sum(jnp.where(r == i, x, 0.0), axis=0, keepdims=True)


def _shift_down(x, prev):
    t = x.shape[0]

    @jax.custom_vjp
    def sh(x, prev):
        r = lax.broadcasted_iota(jnp.int32, x.shape, 0)
        return jnp.where(r == 0, prev, pltpu.roll(x, 1, 0))

    def fwd(x, prev):
        return sh(x, prev), None

    def bwd(_, g):
        r = lax.broadcasted_iota(jnp.int32, g.shape, 0)
        dx = jnp.where(r == t - 1, 0.0, pltpu.roll(g, t - 1, 0))
        return dx, jnp.sum(jnp.where(r == 0, g, 0.0), axis=0, keepdims=True)

    sh.defvjp(fwd, bwd)
    return sh(x, prev)


def _sigmoid(x):
    return jax.nn.sigmoid(x)


def _silu(x):
    return x * jax.nn.sigmoid(x)


def _softplus(x):
    return jnp.maximum(x, 0.0) + jnp.log(1.0 + jnp.exp(-jnp.abs(x)))


def _rms(x, g):
    return (x * lax.rsqrt(jnp.mean(x * x, axis=-1, keepdims=True) + EPS)) * g


def _tril(c):
    r = lax.broadcasted_iota(jnp.int32, (c, c), 0)
    cc = lax.broadcasted_iota(jnp.int32, (c, c), 1)
    return cc <= r


def _f_pre1_residual(ps, xs, cs):
    return [_rms(xs[0], ps[0]), xs[0]], []


def _f_hgrn(ps, xs, cs):
    lbraw, gn = ps
    hq, hf, hi, hg = xs
    hd = range(HG_PER_STEP)
    st = [cs[0][p * HG_K:(p + 1) * HG_K] for p in hd]
    l0, l1 = _row(lbraw, 0), _row(lbraw, 1)
    m = jnp.maximum(l0, l1)
    e0, e1 = jnp.exp(l0 - m), jnp.exp(l1 - m)
    lb = e0 / (e0 + e1)
    q = _silu(hq) * HG_SCALE
    f = lb + (1.0 - lb) * _sigmoid(hf)
    kh = 1.0 - f
    gl = jnp.log(f)
    c = HG_CHUNK
    low = _tril(c)
    tri = jnp.where(low, 1.0, 0.0).astype(F32)
    outs = []
    for i in range(hq.shape[0] // c):
        rows = slice(i * c, (i + 1) * c)
        b = _tdot(tri, gl[rows], 3)
        bref = _row(b, c // 2 - 1)
        blast = _row(b, c - 1)
        qi = q[rows] * jnp.exp(b - bref)
        ki = kh[rows] * jnp.exp(bref - b)
        qd = q[rows] * jnp.exp(b)
        kd = kh[rows] * jnp.exp(blast - b)
        dec = jnp.exp(blast)
        sl = [slice(p * HG_K, (p + 1) * HG_K) for p in hd]
        sc = [jnp.where(low, _dot(qi[:, sl[p]], ki[:, sl[p]], "nt"), 0.0) for p in hd]
        o = [_dot(sc[p], hi[rows, sl[p]], "nn") + _dot(qd[:, sl[p]], st[p], "nt") for p in hd]
        u = [_dot(hi[rows, sl[p]], kd[:, sl[p]], "tn") for p in hd]
        st = [dec[:, sl[p]] * st[p] + u[p] for p in hd]
        outs.append(jnp.concatenate(o, axis=1) if len(o) > 1 else o[0])
    o = outs[0] if len(outs) == 1 else jnp.concatenate(outs, axis=0)
    on = []
    for p in hd:
        op = o[:, p * HG_K:(p + 1) * HG_K]
        on.append(op * lax.rsqrt(jnp.mean(op * op, axis=-1, keepdims=True) + EPS))
    o = jnp.concatenate(on, axis=1) if len(on) > 1 else on[0]
    o = o * gn
    return [o * _silu(hg)], [jnp.concatenate(st, axis=0) if len(st) > 1 else st[0]]


_RW_OFFS = (0, 1024, 2048, 3072, 3200, 3328)


def _f_rwpre(ps, xs, cs):
    mu, w0, w2p, a0, a2p, g2, k_k, k_a = ps
    (prev,) = cs
    t = xs[0].shape[0]
    zs = []
    for i, z in enumerate(xs):
        lo, hi = _RW_OFFS[i], _RW_OFFS[i + 1]
        zs.append(z + mu[:, lo:hi] * (_shift_down(z, prev[:, lo:hi]) - z))
    rr, kr, vr, wa, gz = zs
    w_log = -_softplus(-(w0 + _dot(jnp.tanh(wa), w2p, "nn"))) - 0.5
    lw = -jnp.exp(w_log)
    a = _sigmoid(a0 + _dot(wa, a2p, "nn"))
    g = _dot(_sigmoid(gz), g2, "nn")
    kkr = kr * k_k
    kk = kkr / jnp.maximum(jnp.sqrt(_headsum(kkr * kkr)), 1e-12)
    k2 = kr * (1.0 + (a - 1.0) * k_a)
    newprev = jnp.concatenate([_row(z, t - 1) for z in xs], axis=1)
    return [rr, lw, k2, vr, -kk, kk * a, g], [newprev]


def _raw_inverses(ls):
    n = ls[0].shape[0]
    r = lax.broadcasted_iota(jnp.int32, (n, n), 0)
    c = lax.broadcasted_iota(jnp.int32, (n, n), 1)
    eye = jnp.where(r == c, 1.0, 0.0).astype(F32)
    tinv = [eye + l for l in ls]
    pw = ls
    for _ in range(5):
        pw = [_raw_dot(p, p, "nn") for p in pw]
        tinv = [t + _raw_dot(t, p, "nn") for t, p in zip(tinv, pw)]
    return tinv


@jax.custom_vjp
def _unit_lower_inverses(ls):
    return _raw_inverses(ls)


def _inverses_fwd(ls):
    tinv = _raw_inverses(ls)
    return tinv, tinv


def _inverses_bwd(tinv, gs):
    return ([_raw_dot(_raw_dot(t, g, "tn"), t, "nt") for t, g in zip(tinv, gs)],)


_unit_lower_inverses.defvjp(_inverses_fwd, _inverses_bwd)


@jax.custom_vjp
def _known_inverses(ls, tinv):
    return tinv


def _known_fwd(ls, tinv):
    return tinv, tinv


def _known_bwd(tinv, gs):
    return [_raw_dot(_raw_dot(t, g, "tn"), t, "nt") for t, g in zip(tinv, gs)], [jnp.zeros_like(t) for t in tinv]


_known_inverses.defvjp(_known_fwd, _known_bwd)


@jax.custom_vjp
def _use_kept(computed, kept):
    return kept


def _use_kept_fwd(computed, kept):
    return kept, None


def _use_kept_bwd(_, g):
    return g, jax.tree.map(jnp.zeros_like, g)


_use_kept.defvjp(_use_kept_fwd, _use_kept_bwd)

RW_KEPT = 5


def _f_rwscan(ps, xs, cs, kept=None):
    state = cs[0]
    ys, keep = [], []
    n = 2 * RW_CHUNK
    per_chunk = RW_KEPT * RW_PAIRS_PER_STEP * n
    for i in range(xs[0].shape[0] // RW_CHUNK):
        known = None
        if kept is not None:
            known = [[kept[i * per_chunk + (q * RW_PAIRS_PER_STEP + p) * n:
                           i * per_chunk + (q * RW_PAIRS_PER_STEP + p + 1) * n] for p in range(RW_PAIRS_PER_STEP)]
                     for q in range(RW_KEPT)]
        y, state, mats = _rwkv_chunk([x[i * RW_CHUNK:(i + 1) * RW_CHUNK] for x in xs], state, known)
        ys.append(y)
        keep += [m for group in mats for m in group]
    return [ys[0] if len(ys) == 1 else jnp.concatenate(ys, axis=0)], [state], jnp.concatenate(keep, axis=0)


def _rwkv_chunk(xs, state, known=None):
    npair = RW_PAIRS_PER_STEP
    pr = range(npair)
    r, lw, k, v, av, bv = [[x[:, p * LANES:(p + 1) * LANES] for p in pr] for x in xs]
    sv = [state[p * LANES:(p + 1) * LANES] for p in pr]
    c = RW_CHUNK
    n = 2 * c
    tri = jnp.where(_tril(c), 1.0, 0.0).astype(F32)
    cl = [_tdot(tri, lw[p], 3) for p in pr]
    cl_last = [_row(cl[p], c - 1) for p in pr]
    lane = lax.broadcasted_iota(jnp.int32, (c, LANES), 1)
    h0 = lane < RW_N

    def stack(x):
        return jnp.concatenate([jnp.where(h0, x, 0.0), jnp.where(h0, 0.0, x)], axis=0)

    am = [stack(av[p] * jnp.exp(cl[p] - lw[p])) for p in pr]
    bm = [stack(bv[p] * jnp.exp(-cl[p])) for p in pr]
    km = [stack(k[p] * jnp.exp(-cl[p])) for p in pr]
    rm = [stack(r[p] * jnp.exp(cl[p])) for p in pr]
    vm = [stack(v[p]) for p in pr]
    rn = lax.broadcasted_iota(jnp.int32, (n, n), 0)
    cn = lax.broadcasted_iota(jnp.int32, (n, n), 1)
    blk = (rn >= c) == (cn >= c)
    strict = blk & (cn < rn)
    incl = blk & (cn <= rn)
    lab = [jnp.where(strict, _dot(am[p], bm[p], "nt"), 0.0) for p in pr]
    lak = [jnp.where(strict, _dot(am[p], km[p], "nt"), 0.0) for p in pr]
    wrb = [jnp.where(incl, _dot(rm[p], bm[p], "nt"), 0.0) for p in pr]
    wrk = [jnp.where(incl, _dot(rm[p], km[p], "nt"), 0.0) for p in pr]
    if known is None:
        tinv = _unit_lower_inverses(lab)
    else:
        tinv = _known_inverses(lab, known[0])
        lak, wrb, wrk = _use_kept(lak, known[1]), _use_kept(wrb, known[2]), _use_kept(wrk, known[3])
    rhs = [_dot(am[p], sv[p], "nt") + _dot(lak[p], vm[p], "nn") for p in pr]
    um = [_dot(tinv[p], rhs[p], "nn") for p in pr]
    if known is not None:
        um = _use_kept(um, known[4])
    ym = [_dot(rm[p], sv[p], "nt") + _dot(wrb[p], um[p], "nn") + _dot(wrk[p], vm[p], "nn") for p in pr]
    sn = [(sv[p] + _dot(um[p], bm[p], "tn") + _dot(vm[p], km[p], "tn")) * jnp.exp(cl_last[p]) for p in pr]
    ys = [ym[p][:c] + ym[p][c:] for p in pr]
    return jnp.concatenate(ys, axis=1), jnp.concatenate(sn, axis=0), [tinv, lak, wrb, wrk, um]


def _f_mixers(ps, xs, cs):
    return _mixers(ps, xs, cs, None)


def _f_mixers_kept(ps, xs, cs, kept):
    return _mixers(ps, xs, cs, kept[0])[:2]


def _mixers(ps, xs, cs, kept):
    oa, st = _f_hgrn(ps[:2], xs[:4], cs[:1])
    (r, lw, k, v, av, bv, g), prev = _f_rwpre(ps[2:10], xs[4:], cs[1:2])
    y, sv, keep = _f_rwscan([], [r, lw, k, v, av, bv], cs[2:], kept)
    ob, _ = _f_rwpost(ps[10:], y + [r, k, v, g], [])
    return oa + ob, st + prev + sv, [keep]


def _f_rwpost(ps, xs, cs):
    ln_w, ln_b, r_k = ps
    y, r, k, v, g = xs
    inv_n = 1.0 / RW_N
    yc = y - _headsum(y) * inv_n
    var = _headsum(yc * yc) * inv_n
    yn = yc * lax.rsqrt(var + GN_EPS)
    yn = yn * ln_w + ln_b
    bonus = _headsum(r * k * r_k) * v
    return [(yn + bonus) * g], []


def _f_merge(ps, xs, cs):
    ga, gb, ya, yb = xs
    return [_sigmoid(ga) * ya + _sigmoid(gb) * yb], []


def _f_post1(ps, xs, cs):
    x, mix = xs
    h1 = x + _rms(mix, ps[0])
    return [h1, _rms(h1, ps[1])], []


def _f_conv(ps, xs, cs):
    cw, cb = ps
    p1, p2 = cs
    w0, w1, w2 = _row(cw, 0), _row(cw, 1), _row(cw, 2)
    t = xs[0].shape[0]
    hc = []
    for i, x in enumerate(xs):
        sl = slice(i * DFF, (i + 1) * DFF)
        s1 = _shift_down(x, p1[:, sl])
        s2 = _shift_down(s1, p2[:, sl])
        hc.append(cb[:, sl] + w0[:, sl] * s2 + w1[:, sl] * s1 + w2[:, sl] * x)
    n1 = jnp.concatenate([_row(x, t - 1) for x in xs], axis=1)
    n2 = jnp.concatenate([_row(x, t - 2) for x in xs], axis=1)
    return [_silu(hc[0]) * hc[1]], [n1, n2]


class _Stage:
    def __init__(self, name, f, g, tm, par_per_g, in_pieces, in_offs, carry_shapes, out_pieces, out_dtypes,
                 kept_shapes=(), f_kept=None):
        self.name, self.f, self.g, self.tm = name, f, g, tm
        self.par_per_g, self.in_pieces, self.in_offs = par_per_g, in_pieces, in_offs
        self.carry_shapes, self.out_pieces, self.out_dtypes = carry_shapes, out_pieces, out_dtypes
        self.kept_shapes, self.f_kept = list(kept_shapes), f_kept


def _par_spec(arr, per_g, g):
    r, c = arr.shape
    if per_g:
        return pl.BlockSpec((r, c // g), lambda gi, ni: (0, gi))
    return pl.BlockSpec((r, c), lambda gi, ni: (0, 0))


def _row_spec(tm, width, off, n, rev):
    if rev:
        return pl.BlockSpec((tm, width), lambda gi, ni: (n - 1 - ni, off + gi))
    return pl.BlockSpec((tm, width), lambda gi, ni: (ni, off + gi))


def _carry_spec(shape, n, rev):
    if rev:
        return pl.BlockSpec((None, None) + shape, lambda gi, ni: (gi, n - 1 - ni, 0, 0))
    return pl.BlockSpec((None, None) + shape, lambda gi, ni: (gi, ni, 0, 0))


def _load_pieces(refs, pieces_list):
    out = []
    for ref, pieces in zip(refs, pieces_list):
        o = 0
        for w in pieces:
            out.append(ref[:, o:o + w].astype(F32))
            o += w
    return out


def _store_pieces(refs, pieces_list, vals):
    k = 0
    for ref, pieces in zip(refs, pieces_list):
        o = 0
        for w in pieces:
            ref[:, o:o + w] = vals[k].astype(ref.dtype)
            k += 1
            o += w


_ANY = pl.BlockSpec(memory_space=pl.ANY)


class _Exchange:
    def __init__(self, kind, arrs):
        self.kind, self.arrs, self.results = kind, list(arrs), None
        if kind == "scatter":
            self.out_shape = [jax.ShapeDtypeStruct((N_DEV - 1,) + a.shape[1:], a.dtype) for a in self.arrs]
        else:
            self.out_shape = [jax.ShapeDtypeStruct((N_DEV,) + a.shape, a.dtype) for a in self.arrs]
        self.nsem = (N_DEV if kind == "gather2" else N_DEV - 1) * len(self.arrs)

    def copies(self, in_refs, out_refs, ssem, rsem):
        x, y, c = lax.axis_index("x"), lax.axis_index("y"), lax.axis_index("c")
        me = 4 * x + 2 * y + c
        cps = []
        for a, (i_ref, o_ref) in enumerate(zip(in_refs, out_refs)):
            for j in range(1, N_DEV):
                px = 1 - x if j & 4 else x
                py = 1 - y if j & 2 else y
                pc = 1 - c if j & 1 else c
                if self.kind == "gather":
                    src, dst = i_ref, o_ref.at[me]
                else:
                    src, dst = i_ref.at[4 * px + 2 * py + pc], o_ref.at[j - 1]
                s = (N_DEV - 1) * a + j - 1
                cps.append(pltpu.make_async_remote_copy(src_ref=src, dst_ref=dst, send_sem=ssem.at[s],
                                                        recv_sem=rsem.at[s], device_id=(px, py, pc),
                                                        device_id_type=MESH))
        return cps

    def run(self, step, total, in_refs, out_refs, ssem, rsem):
        if self.kind == "gather2":
            return self.run_two_level(step, total, in_refs, out_refs, ssem, rsem)

        @pl.when(step == 0)
        def _():
            for cp in self.copies(in_refs, out_refs, ssem, rsem):
                cp.start()

        @pl.when(step == total - 1)
        def _():
            for cp in self.copies(in_refs, out_refs, ssem, rsem):
                cp.wait()

    def run_two_level(self, step, total, in_refs, out_refs, ssem, rsem):
        x, y, c = lax.axis_index("x"), lax.axis_index("y"), lax.axis_index("c")
        sibling, xn, yn = (x, y, 1 - c), (1 - x, y, c), (x, 1 - y, c)
        arrs = range(len(in_refs))
        ns = N_DEV

        def num(px, py, pc):
            return 4 * px + 2 * py + pc

        def copy(a, k, to, src, dst):
            return pltpu.make_async_remote_copy(src_ref=src, dst_ref=dst, send_sem=ssem.at[ns * a + k],
                                                recv_sem=rsem.at[ns * a + k], device_id=to, device_id_type=MESH)

        def blk(a, b):
            return out_refs[a].at[b]

        def half(a, b, second):
            h = self.arrs[a].shape[0] // 2
            return out_refs[a].at[b, pl.ds(h if second else 0, h)]

        bx, by, bd = num(1 - x, y, c), num(x, 1 - y, c), num(1 - x, 1 - y, c)

        def firsts(a):
            own = blk(a, num(x, y, c))
            return [copy(a, 0, sibling, in_refs[a], own), copy(a, 1, xn, in_refs[a], own),
                    copy(a, 2, yn, in_refs[a], own)]

        def seconds(a):
            return [copy(a, 3, yn, half(a, bx, False), half(a, bx, False)), copy(a, 5, sibling, blk(a, bx), blk(a, bx)),
                    copy(a, 4, xn, half(a, by, True), half(a, by, True)), copy(a, 6, sibling, blk(a, by), blk(a, by))]

        def third(a):
            return copy(a, 7, sibling, blk(a, bd), blk(a, bd))

        @pl.when(step == 0)
        def _():
            for a in arrs:
                for cp in firsts(a):
                    cp.start()

        @pl.when(step == total // 2)
        def _():
            for a in arrs:
                copy(a, 1, xn, blk(a, bx), blk(a, bx)).wait_recv()
                copy(a, 2, yn, blk(a, by), blk(a, by)).wait_recv()
                for cp in seconds(a):
                    cp.start()

        @pl.when(step == (4 * total) // 5)
        def _():
            for a in arrs:
                copy(a, 3, yn, half(a, bd, False), half(a, bd, False)).wait_recv()
                copy(a, 4, xn, half(a, bd, True), half(a, bd, True)).wait_recv()
                third(a).start()

        @pl.when(step == total - 1)
        def _():
            for a in arrs:
                for k, b in ((0, num(x, y, 1 - c)), (5, num(1 - x, y, 1 - c)), (6, num(x, 1 - y, 1 - c)),
                             (7, num(1 - x, 1 - y, 1 - c))):
                    copy(a, k, sibling, blk(a, b), blk(a, b)).wait_recv()
                for cp in firsts(a) + seconds(a) + [third(a)]:
                    cp.wait_send()


def _hook_specs(hook):
    if hook is None:
        return [], [], [], []
    na = len(hook.arrs)
    sems = [pltpu.SemaphoreType.DMA((hook.nsem,)), pltpu.SemaphoreType.DMA((hook.nsem,))]
    return [_ANY] * na, [_ANY] * na, hook.out_shape, sems


def _stage_fwd(st, t, params, inputs, hook=None):
    g, tm = st.g, min(st.tm, t)
    n = t // tm
    npar, nin, ncar, nout = len(params), len(inputs), len(st.carry_shapes), len(st.out_pieces)
    nk = len(st.kept_shapes)
    h_in, h_out, h_shape, h_sems = _hook_specs(hook)
    nh = len(h_in)

    def body(*refs):
        p_refs = refs[:npar]
        x_refs = refs[npar:npar + nin]
        hi_refs = refs[npar + nin:npar + nin + nh]
        o = npar + nin + nh
        o_refs = refs[o:o + nout]
        s_refs = refs[o + nout:o + nout + ncar]
        k_refs = refs[o + nout + ncar:o + nout + ncar + nk]
        o += nout + ncar + nk
        ho_refs = refs[o:o + nh]
        c_scr = refs[o + nh:o + nh + ncar]
        gi, ni = pl.program_id(0), pl.program_id(1)
        if hook is not None:
            step = gi * n + ni
            hook.run(step, g * n, hi_refs, ho_refs, *refs[-2:])

        @pl.when(ni == 0)
        def _():
            for c in c_scr:
                c[...] = jnp.zeros(c.shape, F32)

        ps = [r[...].astype(F32) for r in p_refs]
        xs = _load_pieces(x_refs, st.in_pieces)
        cs = [c[...] for c in c_scr]
        for s, c in zip(s_refs, cs):
            s[...] = c
        res = st.f(ps, xs, cs)
        outs, ncs = res[0], res[1]
        _store_pieces(o_refs, st.out_pieces, outs)
        for c, v in zip(c_scr, ncs):
            c[...] = v
        for kr, kv in zip(k_refs, res[2] if nk else []):
            kr[...] = kv.astype(kr.dtype)

    in_specs = [_par_spec(p, pg, g) for p, pg in zip(params, st.par_per_g)]
    in_specs += [_row_spec(tm, sum(pc), off, n, False) for pc, off in zip(st.in_pieces, st.in_offs)]
    out_specs = [_row_spec(tm, sum(pc), 0, n, False) for pc in st.out_pieces]
    out_specs += [_carry_spec(s, n, False) for s in st.carry_shapes]
    out_specs += [pl.BlockSpec(s, lambda gi, ni: (ni, 0)) for s in st.kept_shapes]
    out_shape = [jax.ShapeDtypeStruct((t, g * sum(pc)), dt) for pc, dt in zip(st.out_pieces, st.out_dtypes)]
    out_shape += [jax.ShapeDtypeStruct((g, n) + s, F32) for s in st.carry_shapes]
    out_shape += [jax.ShapeDtypeStruct((n * s[0], s[1]), BF) for s in st.kept_shapes]
    res = pl.pallas_call(
        body, name=st.name + "_fwd", grid=(g, n), in_specs=in_specs + h_in, out_specs=out_specs + h_out,
        out_shape=out_shape + h_shape,
        scratch_shapes=[pltpu.VMEM(s, F32) for s in st.carry_shapes] + h_sems,
        compiler_params=_cparams(("arbitrary", "arbitrary")),
    )(*params, *inputs, *(hook.arrs if hook else []))
    if hook is not None:
        hook.results = list(res[nout + ncar + nk:])
    return list(res[:nout]), list(res[nout:nout + ncar + nk])


def _stage_bwd(st, t, params, inputs, saved, douts, dx_dtypes, hook=None):
    g, tm = st.g, min(st.tm, t)
    n = t // tm
    npar, nin, ncar = len(params), len(inputs), len(st.carry_shapes)
    nk = len(st.kept_shapes)
    flat_d = [d for ds in douts for d in ds]
    nd = len(flat_d)
    dx_idx = [i for i, dt in enumerate(dx_dtypes) if dt is not None]
    h_in, h_out, h_shape, h_sems = _hook_specs(hook)
    nh = len(h_in)

    def body(*refs):
        p_refs = refs[:npar]
        x_refs = refs[npar:npar + nin]
        s_refs = refs[npar + nin:npar + nin + ncar]
        k_refs = refs[npar + nin + ncar:npar + nin + ncar + nk]
        o = npar + nin + ncar + nk
        d_refs = refs[o:o + nd]
        hi_refs = refs[o + nd:o + nd + nh]
        o += nd + nh
        dp_refs = refs[o:o + npar]
        dx_refs = refs[o + npar:o + npar + len(dx_idx)]
        ho_refs = refs[o + npar + len(dx_idx):o + npar + len(dx_idx) + nh]
        dc_scr = refs[o + npar + len(dx_idx) + nh:o + npar + len(dx_idx) + nh + ncar]
        gi, ni = pl.program_id(0), pl.program_id(1)
        if hook is not None:
            step = gi * n + ni
            hook.run(step, g * n, hi_refs, ho_refs, *refs[-2:])

        @pl.when(ni == 0)
        def _():
            for c in dc_scr:
                c[...] = jnp.zeros(c.shape, F32)

        ps = [r[...].astype(F32) for r in p_refs]
        xs = _load_pieces(x_refs, st.in_pieces)
        cs = [s[...] for s in s_refs]
        dys = []
        k = 0
        for ds, pieces in zip(douts, st.out_pieces):
            acc = _load_pieces([d_refs[k]], [pieces])
            for j in range(1, len(ds)):
                more = _load_pieces([d_refs[k + j]], [pieces])
                acc = [a + b for a, b in zip(acc, more)]
            dys += acc
            k += len(ds)
        if nk:
            kept = [r[...].astype(F32) for r in k_refs]
            _, vjp = jax.vjp(lambda p, x, c: st.f_kept(p, x, c, kept), ps, xs, cs)
        else:
            _, vjp = jax.vjp(st.f, ps, xs, cs)
        dps, dxs, dcs = vjp((dys, [c[...] for c in dc_scr]))
        k = 0
        per_in = []
        for pieces in st.in_pieces:
            per_in.append(dxs[k:k + len(pieces)])
            k += len(pieces)
        for ref, i in zip(dx_refs, dx_idx):
            _store_pieces([ref], [st.in_pieces[i]], per_in[i])
        for c, v in zip(dc_scr, dcs):
            c[...] = v
        for ref, dp, pg in zip(dp_refs, dps, st.par_per_g):
            first = (ni == 0) if pg else ((ni == 0) & (gi == 0))

            @pl.when(first)
            def _():
                ref[...] = jnp.zeros(ref.shape, F32)

            ref[...] += dp

    in_specs = [_par_spec(p, pg, g) for p, pg in zip(params, st.par_per_g)]
    in_specs += [_row_spec(tm, sum(pc), off, n, True) for pc, off in zip(st.in_pieces, st.in_offs)]
    in_specs += [_carry_spec(s, n, True) for s in st.carry_shapes]
    in_specs += [pl.BlockSpec(s, lambda gi, ni: (n - 1 - ni, 0)) for s in st.kept_shapes]
    for ds, pc in zip(douts, st.out_pieces):
        in_specs += [_row_spec(tm, sum(pc), 0, n, True) for _ in ds]
    out_specs = [_par_spec(p, pg, g) for p, pg in zip(params, st.par_per_g)]
    out_specs += [_row_spec(tm, sum(st.in_pieces[i]), 0, n, True) for i in dx_idx]
    out_shape = [jax.ShapeDtypeStruct(p.shape, F32) for p in params]
    out_shape += [jax.ShapeDtypeStruct((t, g * sum(st.in_pieces[i])), dx_dtypes[i]) for i in dx_idx]
    res = pl.pallas_call(
        body, name=st.name + "_bwd", grid=(g, n), in_specs=in_specs + h_in, out_specs=out_specs + h_out,
        out_shape=out_shape + h_shape,
        scratch_shapes=[pltpu.VMEM(s, F32) for s in st.carry_shapes] + h_sems,
        compiler_params=_cparams(("arbitrary", "arbitrary")),
    )(*params, *inputs, *saved, *flat_d, *(hook.arrs if hook else []))
    if hook is not None:
        hook.results = list(res[npar + len(dx_idx):])
    return list(res[:npar]), list(res[npar:npar + len(dx_idx)])


def _pick(n, cap):
    if n <= cap:
        return n
    best = LANES
    for k in range(1, n // LANES + 1):
        if (n // LANES) % k == 0 and k * LANES <= cap:
            best = k * LANES
    return best


def _mm(name, a, b, mode, out_dtype=F32, tm=1024, tn=512, b_outer=False):
    m = a.shape[1] if mode == "tn" else a.shape[0]
    k = a.shape[0] if mode == "tn" else a.shape[1]
    n = b.shape[0] if mode == "nt" else b.shape[1]
    tm, tn = _pick(m, tm), _pick(n, tn)
    if b_outer:
        grid = (n // tn, m // tm)
        ij = lambda p, q: (q, p)
    else:
        grid = (m // tm, n // tn)
        ij = lambda p, q: (p, q)

    def body(a_ref, b_ref, o_ref):
        o_ref[...] = _raw_dot(a_ref[...], b_ref[...], mode).astype(o_ref.dtype)

    if mode == "tn":
        a_spec = pl.BlockSpec((k, tm), lambda p, q: (0, ij(p, q)[0]))
    else:
        a_spec = pl.BlockSpec((tm, k), lambda p, q: (ij(p, q)[0], 0))
    b_mode = dict(pipeline_mode=pl.Buffered(1)) if tn == n else {}
    if mode == "nt":
        b_spec = pl.BlockSpec((tn, k), lambda p, q: (ij(p, q)[1], 0), **b_mode)
    else:
        b_spec = pl.BlockSpec((k, tn), lambda p, q: (0, ij(p, q)[1]), **b_mode)
    return pl.pallas_call(
        body, name=name, grid=grid, in_specs=[a_spec, b_spec],
        out_specs=pl.BlockSpec((tm, tn), lambda p, q: ij(p, q)),
        out_shape=jax.ShapeDtypeStruct((m, n), out_dtype),
        compiler_params=_cparams(("arbitrary", "arbitrary")),
    )(a, b)


def _mm_multi(name, pairs, mode, out_dtype, tm=1024, tn=512):
    a0, b0 = pairs[0]
    m = a0.shape[1] if mode == "tn" else a0.shape[0]
    k = a0.shape[0] if mode == "tn" else a0.shape[1]
    n = b0.shape[0] if mode == "nt" else b0.shape[1]
    tm, tn = _pick(m, tm), _pick(n, tn)
    npair = len(pairs)

    def body(*refs):
        for p in range(npair):
            refs[2 * npair + p][...] = _raw_dot(refs[2 * p][...], refs[2 * p + 1][...], mode).astype(out_dtype)

    a_spec = pl.BlockSpec((k, tm), lambda i, j: (0, i)) if mode == "tn" else pl.BlockSpec((tm, k), lambda i, j: (i, 0))
    b_spec = pl.BlockSpec((tn, k), lambda i, j: (j, 0)) if mode == "nt" else pl.BlockSpec((k, tn), lambda i, j: (0, j))
    return pl.pallas_call(
        body, name=name, grid=(m // tm, n // tn), in_specs=[a_spec, b_spec] * npair,
        out_specs=[pl.BlockSpec((tm, tn), lambda i, j: (i, j))] * npair,
        out_shape=[jax.ShapeDtypeStruct((m, n), out_dtype)] * npair,
        compiler_params=_cparams(("arbitrary", "arbitrary")),
    )(*[x for pair in pairs for x in pair])


def _mm_cols_tn(name, pieces, b, out_dtype, tm):
    k, n = b.shape
    counts = [p.shape[1] // tm for p in pieces]
    starts = [sum(counts[:i]) for i in range(len(pieces))]
    na = len(pieces)

    def body(*refs):
        b_ref, o_ref = refs[na], refs[-1]
        i = pl.program_id(0)
        for a_ref, s, c in zip(refs[:na], starts, counts):
            @pl.when((i >= s) & (i < s + c))
            def _():
                o_ref[...] = _raw_dot(a_ref[...], b_ref[...], "tn").astype(o_ref.dtype)

    def spec(s, c):
        return pl.BlockSpec((k, tm), lambda i: (0, jnp.clip(i - s, 0, c - 1)))

    return pl.pallas_call(
        body, name=name, grid=(sum(counts),),
        in_specs=[spec(s, c) for s, c in zip(starts, counts)]
        + [pl.BlockSpec(b.shape, lambda i: (0, 0), pipeline_mode=pl.Buffered(1))],
        out_specs=pl.BlockSpec((tm, n), lambda i: (i, 0)),
        out_shape=jax.ShapeDtypeStruct((sum(counts) * tm, n), out_dtype),
        compiler_params=_cparams(("arbitrary",)),
    )(*pieces, b)


def _norm_in_proj(x, g, w_t, tm, tn):
    t, k = x.shape
    n = w_t.shape[0]
    tm, tn = _pick(t, tm), _pick(n, tn)

    def body(x_ref, g_ref, w_ref, xn_ref, z_ref):
        xn = _rms(x_ref[...], g_ref[...]).astype(BF)
        xn_ref[...] = xn
        z_ref[...] = _raw_dot(xn, w_ref[...], "nt")

    xns, z = pl.pallas_call(
        body, name="in_proj", grid=(n // tn, t // tm),
        in_specs=[pl.BlockSpec((tm, k), lambda j, i: (i, 0)), pl.BlockSpec((1, k), lambda j, i: (0, 0)),
                  pl.BlockSpec((tn, k), lambda j, i: (j, 0))],
        out_specs=[pl.BlockSpec((None, tm, k), lambda j, i: (j, i, 0)), pl.BlockSpec((tm, tn), lambda j, i: (i, j))],
        out_shape=[jax.ShapeDtypeStruct((n // tn, t, k), BF), jax.ShapeDtypeStruct((t, n), F32)],
        compiler_params=_cparams(("arbitrary", "arbitrary")),
    )(x, g, w_t)
    return xns[0], z


def _merge_out_post(z, o_a, o_b, w_a, w_b, w_out, x, g_post, g_pre2, tm):
    t = x.shape[0]
    tm = _pick(t, tm)
    w = 256
    npc = D // w
    ga0, gb0 = (IN_COLS - 2 * D) // w, (IN_COLS - D) // w

    def body(*refs):
        ga_refs, gb_refs = refs[:npc], refs[npc:2 * npc]
        oa_ref, ob_ref, wa_ref, wb_ref, w_ref, x_ref, gp_ref, g2_ref = refs[2 * npc:2 * npc + 8]
        ya_ref, yb_ref, m_ref, mix_ref, h_ref, xn_ref = refs[2 * npc + 8:]
        ya = _raw_dot(oa_ref[...], wa_ref[...], "nn").astype(BF)
        yb = _raw_dot(ob_ref[...], wb_ref[...], "nn").astype(BF)
        ya_ref[...] = ya
        yb_ref[...] = yb
        parts = []
        for p in range(npc):
            cols = slice(p * w, (p + 1) * w)
            parts.append(_sigmoid(ga_refs[p][...]) * ya[:, cols].astype(F32)
                         + _sigmoid(gb_refs[p][...]) * yb[:, cols].astype(F32))
        merged = jnp.concatenate(parts, axis=1).astype(BF)
        m_ref[...] = merged
        mix = _raw_dot(merged, w_ref[...], "nn")
        mix_ref[...] = mix
        h1 = x_ref[...] + _rms(mix, gp_ref[...])
        h_ref[...] = h1
        xn_ref[...] = _rms(h1, g2_ref[...]).astype(BF)

    row = pl.BlockSpec((tm, D), lambda i: (i, 0))
    one = pl.BlockSpec((1, D), lambda i: (0, 0))

    def gate(b0):
        return [pl.BlockSpec((tm, w), functools.partial(lambda i, b: (i, b), b=b0 + p)) for p in range(npc)]

    wgt = pl.BlockSpec((D, D), lambda i: (0, 0), pipeline_mode=pl.Buffered(1))
    return pl.pallas_call(
        body, name="merge_out_post", grid=(t // tm,),
        in_specs=gate(ga0) + gate(gb0) + [row, row, wgt, wgt, wgt, row, one, one],
        out_specs=[row] * 6,
        out_shape=[jax.ShapeDtypeStruct((t, D), BF), jax.ShapeDtypeStruct((t, D), BF), jax.ShapeDtypeStruct((t, D), BF),
                   jax.ShapeDtypeStruct((t, D), F32), jax.ShapeDtypeStruct((t, D), F32),
                   jax.ShapeDtypeStruct((t, D), BF)],
        compiler_params=_cparams(("arbitrary",)),
    )(*([z] * (2 * npc)), o_a, o_b, w_a, w_b, w_out, x, g_post, g_pre2)


def _accumulate(ni, refs, vals):
    @pl.when(ni == 0)
    def _():
        for r in refs:
            r[...] = jnp.zeros(r.shape, F32)

    for r, v in zip(refs, vals):
        r[...] += v


def _dmerged_merge_bwd(dmix, w_out, w_a, w_b, z, y_a, y_b, tm):
    t = dmix.shape[0]
    tm = _pick(t, tm)
    w = 256
    npc = D // w
    ga0, gb0 = (IN_COLS - 2 * D) // w, (IN_COLS - D) // w

    def body(*refs):
        dm_ref, w_ref, wa_ref, wb_ref = refs[:4]
        ga_refs, gb_refs = refs[4:4 + npc], refs[4 + npc:4 + 2 * npc]
        ya_ref, yb_ref, dga_ref, dgb_ref, dya_ref, dyb_ref, doa_ref, dob_ref = refs[4 + 2 * npc:]
        dmerged = _raw_dot(dm_ref[...], w_ref[...], "nt")
        dyas, dybs = [], []
        for p in range(npc):
            cols = slice(p * w, (p + 1) * w)
            xs = [ga_refs[p][...], gb_refs[p][...], ya_ref[:, cols].astype(F32), yb_ref[:, cols].astype(F32)]
            _, vjp = jax.vjp(lambda *a: _f_merge([], list(a), [])[0][0], *xs)
            dga, dgb, dya, dyb = vjp(dmerged[:, cols])
            dga_ref[:, cols] = dga.astype(BF)
            dgb_ref[:, cols] = dgb.astype(BF)
            dyas.append(dya.astype(BF))
            dybs.append(dyb.astype(BF))
        dya, dyb = jnp.concatenate(dyas, axis=1), jnp.concatenate(dybs, axis=1)
        dya_ref[...] = dya
        dyb_ref[...] = dyb
        doa_ref[...] = _raw_dot(dya, wa_ref[...], "nt").astype(BF)
        dob_ref[...] = _raw_dot(dyb, wb_ref[...], "nt").astype(BF)

    row = pl.BlockSpec((tm, D), lambda i: (i, 0))
    wgt = pl.BlockSpec((D, D), lambda i: (0, 0), pipeline_mode=pl.Buffered(1))

    def gate(b0):
        return [pl.BlockSpec((tm, w), functools.partial(lambda i, b: (i, b), b=b0 + p)) for p in range(npc)]

    return pl.pallas_call(
        body, name="merge_bwd", grid=(t // tm,),
        in_specs=[row, wgt, wgt, wgt] + gate(ga0) + gate(gb0) + [row, row],
        out_specs=[row] * 6, out_shape=[jax.ShapeDtypeStruct((t, D), BF)] * 6,
        compiler_params=_cparams(("arbitrary",)),
    )(dmix, w_out, w_a, w_b, *([z] * (2 * npc)), y_a, y_b)


def _dxn2_post1_bwd(dhu, w_up_t, x, mix, dh1, g_post, g_pre2, tm):
    t, k = dhu.shape
    tm = _pick(t, tm)

    def body(a_ref, w_ref, x_ref, m_ref, dh_ref, gp_ref, g2_ref, dgp_ref, dg2_ref, dx_ref, dm_ref):
        dxn2 = _raw_dot(a_ref[...], w_ref[...], "nn")
        _, vjp = jax.vjp(lambda gp, g2, xx, mm: _f_post1([gp, g2], [xx, mm], [])[0],
                         gp_ref[...], g2_ref[...], x_ref[...], m_ref[...])
        dgp, dg2, dx, dm = vjp([dh_ref[...], dxn2])
        _accumulate(pl.program_id(0), [dgp_ref, dg2_ref], [dgp, dg2])
        dx_ref[...] = dx
        dm_ref[...] = dm.astype(BF)

    row = pl.BlockSpec((tm, D), lambda i: (i, 0))
    one = pl.BlockSpec((1, D), lambda i: (0, 0))
    return pl.pallas_call(
        body, name="post1_bwd", grid=(t // tm,),
        in_specs=[pl.BlockSpec((tm, k), lambda i: (i, 0)),
                  pl.BlockSpec((k, D), lambda i: (0, 0), pipeline_mode=pl.Buffered(1)), row, row, row, one, one],
        out_specs=[one, one, row, row],
        out_shape=[jax.ShapeDtypeStruct((1, D), F32), jax.ShapeDtypeStruct((1, D), F32),
                   jax.ShapeDtypeStruct((t, D), F32), jax.ShapeDtypeStruct((t, D), BF)],
        compiler_params=_cparams(("arbitrary",)),
    )(dhu, w_up_t, x, mix, dh1, g_post, g_pre2)


def _dxn_pre1_bwd(pieces, w_t, x, dx_res, g, tm, token):
    t = x.shape[0]
    tm = _pick(t, tm)
    offs = [sum(p.shape[1] for p in pieces[:i]) for i in range(len(pieces))]
    na = len(pieces)

    def body(*refs):
        w_ref, x_ref, r_ref, g_ref = refs[na:na + 4]
        dg_ref, dx_ref = refs[-2:]
        dxn = None
        for a_ref, off in zip(refs[:na], offs):
            part = _raw_dot(a_ref[...], w_ref[off:off + a_ref.shape[1], :], "nn")
            dxn = part if dxn is None else dxn + part
        _, vjp = jax.vjp(lambda gg, xx: _f_pre1_residual([gg], [xx], [])[0], g_ref[...], x_ref[...])
        dg, dx = vjp([dxn, r_ref[...]])
        _accumulate(pl.program_id(0), [dg_ref], [dg])
        dx_ref[...] = dx

    row = pl.BlockSpec((tm, D), lambda i: (i, 0))
    one = pl.BlockSpec((1, D), lambda i: (0, 0))
    return pl.pallas_call(
        body, name="pre1_bwd", grid=(t // tm,),
        in_specs=[pl.BlockSpec((tm, p.shape[1]), lambda i: (i, 0)) for p in pieces]
        + [pl.BlockSpec(w_t.shape, lambda i: (0, 0), pipeline_mode=pl.Buffered(1)), row, row, one,
           pl.BlockSpec(token.shape, lambda i: (0, 0))],
        out_specs=[one, row],
        out_shape=[jax.ShapeDtypeStruct((1, D), F32), jax.ShapeDtypeStruct((t, D), F32)],
        compiler_params=_cparams(("arbitrary",)),
    )(*pieces, w_t, x, dx_res, g, token)


def _down_loss(act, w_down, g_post, h1, tgt, tm):
    t, k = act.shape
    tm = _pick(t, tm)

    def body(a_ref, w_ref, g_ref, h_ref, t_ref, loss_ref, dg_ref, dh_ref, df_ref):
        ni = pl.program_id(0)
        ff = _raw_dot(a_ref[...], w_ref[...], "nn")
        target = t_ref[...]

        def lossf(g, h1, ff):
            e = h1 + _rms(ff, g) - target
            return 0.5 * jnp.sum(jnp.mean(e * e, axis=-1))

        l, (dg, dh, df) = jax.value_and_grad(lossf, argnums=(0, 1, 2))(g_ref[...], h_ref[...], ff)

        @pl.when(ni == 0)
        def _():
            loss_ref[...] = jnp.zeros(loss_ref.shape, F32)
            dg_ref[...] = jnp.zeros(dg_ref.shape, F32)

        loss_ref[...] += jnp.full(loss_ref.shape, l, F32)
        dg_ref[...] += dg
        dh_ref[...] = dh
        df_ref[...] = df.astype(df_ref.dtype)

    row = pl.BlockSpec((tm, D), lambda ni: (ni, 0))
    one = pl.BlockSpec((1, D), lambda ni: (0, 0))
    return pl.pallas_call(
        body, name="down_loss", grid=(t // tm,),
        in_specs=[pl.BlockSpec((tm, k), lambda ni: (ni, 0)),
                  pl.BlockSpec((k, D), lambda ni: (0, 0), pipeline_mode=pl.Buffered(1)), one, row, row],
        out_specs=[pl.BlockSpec((1, LANES), lambda ni: (0, 0)), one, row, row],
        out_shape=[jax.ShapeDtypeStruct((1, LANES), F32), jax.ShapeDtypeStruct((1, D), F32),
                   jax.ShapeDtypeStruct((t, D), F32), jax.ShapeDtypeStruct((t, D), BF)],
        compiler_params=_cparams(("arbitrary",)),
    )(act, w_down, g_post, h1, tgt)


_ANY = pl.BlockSpec(memory_space=pl.ANY)


def _all_gather(name, blks):
    na = len(blks)
    ns = 8

    def body(*refs):
        x_refs, out_refs = refs[:na], refs[na:2 * na]
        send_sems, recv_sems, local_sems = refs[2 * na:]
        x, y, cc = lax.axis_index("x"), lax.axis_index("y"), lax.axis_index("c")
        sibling, xn, yn = (x, y, 1 - cc), (1 - x, y, cc), (x, 1 - y, cc)

        def num(px, py, pc):
            return 4 * px + 2 * py + pc

        def copy(a, k, to, src, dst):
            return pltpu.make_async_remote_copy(src_ref=src, dst_ref=dst, send_sem=send_sems.at[ns * a + k],
                                                recv_sem=recv_sems.at[ns * a + k], device_id=to, device_id_type=MESH)

        def halves(a, blk):
            h = blks[a].shape[0] // 2
            return out_refs[a].at[blk, pl.ds(0, h)], out_refs[a].at[blk, pl.ds(h, h)]

        mine, sends = [], []
        for a in range(na):
            o = out_refs[a]
            m = pltpu.make_async_copy(x_refs[a], o.at[num(x, y, cc)], local_sems.at[a])
            m.start()
            mine.append(m)
            own = o.at[num(x, y, cc)]
            sends.append([copy(a, 0, sibling, x_refs[a], own), copy(a, 1, xn, x_refs[a], own),
                          copy(a, 2, yn, x_refs[a], own)])
            for cp in sends[a]:
                cp.start()
        for a in range(na):
            o = out_refs[a]
            bx, by, bd = num(1 - x, y, cc), num(x, 1 - y, cc), num(1 - x, 1 - y, cc)
            copy(a, 1, xn, o.at[bx], o.at[bx]).wait_recv()
            more = [copy(a, 3, yn, halves(a, bx)[0], halves(a, bx)[0]), copy(a, 5, sibling, o.at[bx], o.at[bx])]
            for cp in more:
                cp.start()
            sends[a] += more
        for a in range(na):
            o = out_refs[a]
            bx, by, bd = num(1 - x, y, cc), num(x, 1 - y, cc), num(1 - x, 1 - y, cc)
            copy(a, 2, yn, o.at[by], o.at[by]).wait_recv()
            more = [copy(a, 4, xn, halves(a, by)[1], halves(a, by)[1]), copy(a, 6, sibling, o.at[by], o.at[by])]
            for cp in more:
                cp.start()
            sends[a] += more
        for a in range(na):
            o = out_refs[a]
            bd = num(1 - x, 1 - y, cc)
            copy(a, 3, yn, halves(a, bd)[0], halves(a, bd)[0]).wait_recv()
            copy(a, 4, xn, halves(a, bd)[1], halves(a, bd)[1]).wait_recv()
            fw = copy(a, 7, sibling, o.at[bd], o.at[bd])
            fw.start()
            sends[a].append(fw)
        for a in range(na):
            o = out_refs[a]
            for k, blk in ((0, num(x, y, 1 - cc)), (5, num(1 - x, y, 1 - cc)), (6, num(x, 1 - y, 1 - cc)),
                           (7, num(1 - x, 1 - y, 1 - cc))):
                copy(a, k, sibling, o.at[blk], o.at[blk]).wait_recv()
            for cp in sends[a]:
                cp.wait_send()
        for m in mine:
            m.wait()

    res = pl.pallas_call(
        body, name=name, in_specs=[_ANY] * na, out_specs=[_ANY] * na,
        out_shape=[jax.ShapeDtypeStruct((N_DEV,) + b.shape, b.dtype) for b in blks],
        scratch_shapes=[pltpu.SemaphoreType.DMA((ns * na,)), pltpu.SemaphoreType.DMA((ns * na,)),
                        pltpu.SemaphoreType.DMA((na,))],
    )(*blks)
    return list(res)


def _all_gather_small(name, blk):
    def body(x_ref, out_ref, ssem, rsem, lsem):
        x, y, c = lax.axis_index("x"), lax.axis_index("y"), lax.axis_index("c")
        me = 4 * x + 2 * y + c
        mine = pltpu.make_async_copy(x_ref, out_ref.at[me], lsem)
        mine.start()
        cps = []
        for j in range(1, N_DEV):
            px = 1 - x if j & 4 else x
            py = 1 - y if j & 2 else y
            pc = 1 - c if j & 1 else c
            cps.append(pltpu.make_async_remote_copy(src_ref=x_ref, dst_ref=out_ref.at[me], send_sem=ssem.at[j - 1],
                                                    recv_sem=rsem.at[j - 1], device_id=(px, py, pc),
                                                    device_id_type=MESH))
        for cp in cps:
            cp.start()
        for cp in cps:
            cp.wait()
        mine.wait()

    return pl.pallas_call(
        body, name=name, in_specs=[_ANY], out_specs=_ANY,
        out_shape=jax.ShapeDtypeStruct((N_DEV,) + blk.shape, blk.dtype),
        scratch_shapes=[pltpu.SemaphoreType.DMA((N_DEV - 1,)), pltpu.SemaphoreType.DMA((N_DEV - 1,)),
                        pltpu.SemaphoreType.DMA],
    )(blk)


def _reduce_pair(g8s):
    na = len(g8s)

    def body(*refs):
        g_refs, recv_refs = refs[:na], refs[na:2 * na]
        ssem, rsem = refs[2 * na:]
        x, y, cc = lax.axis_index("x"), lax.axis_index("y"), lax.axis_index("c")
        chips = [(x, y), (1 - x, y), (x, 1 - y), (1 - x, 1 - y)]
        sib = (x, y, 1 - cc)
        for a in range(na):
            for k, (cx, cy) in enumerate(chips):
                pltpu.make_async_remote_copy(
                    src_ref=g_refs[a].at[4 * cx + 2 * cy + 1 - cc], dst_ref=recv_refs[a].at[k],
                    send_sem=ssem.at[a], recv_sem=rsem.at[a], device_id=sib, device_id_type=MESH).start()
        for a in range(na):
            pltpu.make_async_remote_copy(src_ref=recv_refs[a], dst_ref=recv_refs[a], send_sem=ssem.at[a],
                                         recv_sem=rsem.at[a], device_id=sib, device_id_type=MESH).wait()

    res = pl.pallas_call(
        body, name="reduce_pair", in_specs=[_ANY] * na, out_specs=[_ANY] * na,
        out_shape=[jax.ShapeDtypeStruct((4,) + g.shape[1:], g.dtype) for g in g8s],
        scratch_shapes=[pltpu.SemaphoreType.DMA((na,)), pltpu.SemaphoreType.DMA((na,))],
    )(*g8s)
    return list(res)


_HBM = pl.BlockSpec(memory_space=pltpu.HBM)
_SEM = pl.BlockSpec(memory_space=pltpu.SEMAPHORE)
_EFFECT = pltpu.SideEffectType.DATAFLOW_SIDE_EFFECTING


def _chip_swap_copies(s_refs, land_refs, ssem, rsem):
    x, y, c = lax.axis_index("x"), lax.axis_index("y"), lax.axis_index("c")
    targets = [(1 - x, y, c), (x, 1 - y, c), (1 - x, 1 - y, c)]
    return [pltpu.make_async_remote_copy(src_ref=s.at[k], dst_ref=d.at[k], send_sem=ssem.at[3 * a + k],
                                         recv_sem=rsem.at[3 * a + k], device_id=targets[k], device_id_type=MESH)
            for a, (s, d) in enumerate(zip(s_refs, land_refs)) for k in range(3)]


def _chip_swap_start(sends):
    na = len(sends)

    def body(*refs):
        cps = _chip_swap_copies(refs[:na], refs[na:2 * na], refs[2 * na], refs[2 * na + 1])
        for cp in cps:
            cp.start()
        token = refs[-1]
        token[...] = jnp.zeros(token.shape, token.dtype)

    bufs = [pltpu.HBM(s.shape, s.dtype) for s in sends]
    res = pl.pallas_call(
        body, name="chip_swap_start",
        out_shape=[pltpu.SemaphoreType.DMA((3 * na,)), pltpu.SemaphoreType.DMA((3 * na,))] + bufs + bufs
        + [jax.ShapeDtypeStruct((8, LANES), F32)],
        in_specs=[_HBM] * (2 * na), out_specs=[_SEM, _SEM] + [_HBM] * (2 * na) + [pl.BlockSpec(memory_space=pltpu.VMEM)],
        input_output_aliases={i: 2 + i for i in range(2 * na)},
        compiler_params=pltpu.CompilerParams(has_side_effects=_EFFECT),
    )(*[pltpu.with_memory_space_constraint(s, pltpu.HBM) for s in sends],
      *[pltpu.with_memory_space_constraint(lax.empty(s.shape, s.dtype), pltpu.HBM) for s in sends])
    return res[0], res[1], list(res[2:2 + na]), list(res[2 + na:2 + 2 * na]), res[-1]


def _chip_swap_wait(ssem, rsem, srcs, lands, after):
    na = len(srcs)

    def body(*refs):
        cps = _chip_swap_copies(refs[:na], refs[na:2 * na], refs[2 * na], refs[2 * na + 1])
        for cp in cps:
            cp.wait_send()
            cp.wait_recv()

    bufs = [pltpu.HBM(s.shape, s.dtype) for s in srcs]
    res = pl.pallas_call(
        body, name="chip_swap_wait", out_shape=bufs + bufs,
        in_specs=[_HBM] * (2 * na) + [_SEM, _SEM, _ANY], out_specs=[_HBM] * (2 * na),
        input_output_aliases={i: i for i in range(2 * na)},
        compiler_params=pltpu.CompilerParams(has_side_effects=_EFFECT),
    )(*srcs, *lands, ssem, rsem, after)
    return list(res[na:])


def _pick_rows(r, c, budget=TILE_BYTES):
    if r * c * 4 <= budget or r % 16:
        return r
    best = 16
    for tr in range(16, r, 16):
        if r % tr == 0 and tr * c * 4 <= budget:
            best = tr
    return best


def _pair_sum(name, idx4, g8, recv4):
    _, r, c = g8.shape
    tr = _pick_rows(r, c, 2 * TILE_BYTES)

    def body(idx_ref, a_ref, b_ref, o0_ref, o3_ref):
        k = pl.program_id(1)
        s = a_ref[...].astype(F32) + b_ref[...].astype(F32)

        @pl.when(k == 0)
        def _():
            o0_ref[...] = s

        @pl.when(k > 0)
        def _():
            o3_ref[...] = s.astype(BF)

    spec = pltpu.PrefetchScalarGridSpec(
        num_scalar_prefetch=1, grid=(r // tr, 4),
        in_specs=[pl.BlockSpec((None, tr, c), lambda i, k, idx: (idx[k], i, 0)),
                  pl.BlockSpec((None, tr, c), lambda i, k, idx: (k, i, 0))],
        out_specs=[pl.BlockSpec((tr, c), lambda i, k, idx: (i, 0)),
                   pl.BlockSpec((None, tr, c), lambda i, k, idx: (jnp.maximum(k - 1, 0), i, 0))])
    return pl.pallas_call(
        body, name=name, grid_spec=spec,
        out_shape=[jax.ShapeDtypeStruct((r, c), F32), jax.ShapeDtypeStruct((3, r, c), BF)],
        compiler_params=_cparams(("arbitrary", "arbitrary")),
    )(idx4, g8, recv4)


def _adamw(w, g, m, v):
    m = ADAM_B1 * m + (1.0 - ADAM_B1) * g
    v = ADAM_B2 * v + (1.0 - ADAM_B2) * jnp.square(g)
    m_hat = m / (1.0 - ADAM_B1 ** ADAM_STEP)
    v_hat = v / (1.0 - ADAM_B2 ** ADAM_STEP)
    delta = -ADAM_LR * (m_hat / (jnp.sqrt(v_hat) + ADAM_EPS) + ADAM_WD * w)
    return delta, m, v


def _adam_sharded(name, idx1, own, recv, w, m, v, after=None):
    r, c = w.shape
    tr = _pick_rows(r, c, 2 * TILE_BYTES)
    nj = recv.shape[0]
    extra = [] if after is None else [after]

    def body(idx_ref, p_ref, r_ref, w_ref, m_ref, v_ref, *rest):
        g_out, d_out, m_out, v_out = rest[-4:]
        g = p_ref[...].astype(F32)
        for k in range(nj):
            g = g + r_ref[k].astype(F32)
        d, mn, vn = _adamw(w_ref[...], g, m_ref[...], v_ref[...])
        g_out[...] = g
        d_out[...] = d
        m_out[...] = mn
        v_out[...] = vn

    row = pl.BlockSpec((tr, c), lambda i, idx: (i, 0))
    spec = pltpu.PrefetchScalarGridSpec(
        num_scalar_prefetch=1, grid=(r // tr,),
        in_specs=[pl.BlockSpec((None, tr, c), lambda i, idx: (idx[0], i, 0)),
                  pl.BlockSpec((nj, tr, c), lambda i, idx: (0, i, 0)), row, row, row]
        + [pl.BlockSpec(e.shape, lambda i, idx: (0, 0)) for e in extra],
        out_specs=[row] * 4)
    return pl.pallas_call(
        body, name=name, grid_spec=spec, out_shape=[jax.ShapeDtypeStruct((r, c), F32)] * 4,
        compiler_params=_cparams(("arbitrary",)),
    )(idx1, own, recv, w, m, v, *extra)


def _repl_rows():
    rows, r = {}, 0
    for name, cols in REPL:
        rows[name] = r
        r += REPL_ROWS.get(name, 1) * ((cols + D - 1) // D)
    return rows


LOSS_ROW = 24


def _pack_replicated(grads, loss_acc, after):
    rows = _repl_rows()
    names = [n for n, _ in REPL]

    def body(*refs):
        o_ref = refs[-1]
        o_ref[...] = jnp.zeros(o_ref.shape, F32)
        o_ref[LOSS_ROW:LOSS_ROW + 1, 0:LANES] = refs[len(names)][...]
        for name, ref in zip(names, refs[:len(names)]):
            r0 = rows[name]
            nr, nc = ref.shape
            if nc <= D:
                o_ref[r0:r0 + nr, 0:nc] = ref[...]
            else:
                for j in range((nc + D - 1) // D):
                    lo, hi = j * D, min(nc, (j + 1) * D)
                    o_ref[r0 + j:r0 + j + 1, 0:hi - lo] = ref[:, lo:hi]

    return pl.pallas_call(body, name="pack_replicated", out_shape=jax.ShapeDtypeStruct((REPL_TOTAL, D), F32),
                          in_specs=[pl.BlockSpec(memory_space=pltpu.VMEM)] * (len(names) + 1) + [_ANY] * len(after),
                          compiler_params=_cparams())(*[grads[n] for n in names], loss_acc, *after)


def _adam_replicated(g8, ws, ms, vs):
    rows = _repl_rows()
    names = [n for n, _ in REPL]
    np_ = len(names)

    def body(*refs):
        g_ref = refs[0]
        w_refs, m_refs, v_refs = refs[1:1 + np_], refs[1 + np_:1 + 2 * np_], refs[1 + 2 * np_:1 + 3 * np_]
        outs = refs[1 + 3 * np_:1 + 7 * np_]
        scr = refs[-1]
        g = g_ref[0]
        for k in range(1, N_DEV):
            g = g + g_ref[k]
        scr[...] = g
        refs[1 + 7 * np_][...] = scr[LOSS_ROW:LOSS_ROW + 1, 0:LANES]
        for i, name in enumerate(names):
            r0 = rows[name]
            nr, nc = w_refs[i].shape
            if nc <= D:
                gi = scr[r0:r0 + nr, 0:nc]
            else:
                parts = []
                for j in range((nc + D - 1) // D):
                    lo, hi = j * D, min(nc, (j + 1) * D)
                    parts.append(scr[r0 + j:r0 + j + 1, 0:hi - lo])
                gi = jnp.concatenate(parts, axis=1)
            d, mn, vn = _adamw(w_refs[i][...], gi, m_refs[i][...], v_refs[i][...])
            outs[i][...] = gi
            outs[np_ + i][...] = d
            outs[2 * np_ + i][...] = mn
            outs[3 * np_ + i][...] = vn

    shp = [jax.ShapeDtypeStruct(w.shape, F32) for w in ws]
    res = pl.pallas_call(body, name="adam_replicated", out_shape=shp * 4 + [jax.ShapeDtypeStruct((1, LANES), F32)],
                         scratch_shapes=[pltpu.VMEM((REPL_TOTAL, D), F32)], compiler_params=_cparams(),
                         )(g8, *ws, *ms, *vs)
    return [dict(zip(names, res[k * np_:(k + 1) * np_])) for k in range(4)], res[-1]


_WEIGHTS = ("attn_pre_norm", "w_in", "hgrn_lb", "hgrn_gnorm", "w_branch_a", "rwkv_mu", "rwkv_w0", "rwkv_w2",
            "rwkv_a0", "rwkv_a2", "rwkv_g2", "rwkv_k_k", "rwkv_k_a", "rwkv_r_k", "rwkv_ln_w", "rwkv_ln_b",
            "w_branch_b", "w_out", "attn_post_norm", "ffn_pre_norm", "w_up", "conv_w", "conv_b", "w_down",
            "ffn_post_norm")
_BIG = ("w_in", "w_up", "w_down", "w_branch_a", "w_branch_b", "w_out")


def _stages():
    one = [D]
    hw = HG_K * HG_PER_STEP
    rw = LANES * RW_PAIRS_PER_STEP
    return dict(
        mixers=_Stage("mixers", _f_mixers, 1, 2 * RW_CHUNK, [False] * 13, [[D] * 7 + [LANES, LANES]], [0],
                      [(hw, HG_K), (1, RW_COLS), (rw, LANES)], [one, one], [BF, BF],
                      kept_shapes=[(2 * RW_KEPT * RW_PAIRS_PER_STEP * 2 * RW_CHUNK, LANES)], f_kept=_f_mixers_kept),
        conv=_Stage("conv", _f_conv, 1, 128, [False, False], [[DFF, DFF]], [0], [(1, 2 * DFF), (1, 2 * DFF)],
                    [[DFF]], [BF]),
    )


def _cols_to_blocks(w, per):
    return w.reshape(w.shape[0], N_DEV, per).transpose(1, 0, 2)


def _blocks_to_cols(g):
    return g.transpose(1, 0, 2).reshape(g.shape[1], N_DEV * g.shape[2])


def kernel(x, attn_pre_norm, w_in, hgrn_lb, hgrn_gnorm, w_branch_a, rwkv_mu, rwkv_w0, rwkv_w2, rwkv_a0, rwkv_a2, rwkv_g2, rwkv_k_k, rwkv_k_a, rwkv_r_k, rwkv_ln_w, rwkv_ln_b, w_branch_b, w_out, attn_post_norm, ffn_pre_norm, w_up, conv_w, conv_b, w_down, ffn_post_norm, loss_target, m_attn_pre_norm, m_w_in, m_hgrn_lb, m_hgrn_gnorm, m_w_branch_a, m_rwkv_mu, m_rwkv_w0, m_rwkv_w2, m_rwkv_a0, m_rwkv_a2, m_rwkv_g2, m_rwkv_k_k, m_rwkv_k_a, m_rwkv_r_k, m_rwkv_ln_w, m_rwkv_ln_b, m_w_branch_b, m_w_out, m_attn_post_norm, m_ffn_pre_norm, m_w_up, m_conv_w, m_conv_b, m_w_down, m_ffn_post_norm, v_attn_pre_norm, v_w_in, v_hgrn_lb, v_hgrn_gnorm, v_w_branch_a, v_rwkv_mu, v_rwkv_w0, v_rwkv_w2, v_rwkv_a0, v_rwkv_a2, v_rwkv_g2, v_rwkv_k_k, v_rwkv_k_a, v_rwkv_r_k, v_rwkv_ln_w, v_rwkv_ln_b, v_w_branch_b, v_w_out, v_attn_post_norm, v_ffn_pre_norm, v_w_up, v_conv_w, v_conv_b, v_w_down, v_ffn_post_norm):
    w = dict(attn_pre_norm=attn_pre_norm, w_in=w_in, hgrn_lb=hgrn_lb, hgrn_gnorm=hgrn_gnorm, w_branch_a=w_branch_a, rwkv_mu=rwkv_mu, rwkv_w0=rwkv_w0, rwkv_w2=rwkv_w2, rwkv_a0=rwkv_a0, rwkv_a2=rwkv_a2, rwkv_g2=rwkv_g2, rwkv_k_k=rwkv_k_k, rwkv_k_a=rwkv_k_a, rwkv_r_k=rwkv_r_k, rwkv_ln_w=rwkv_ln_w, rwkv_ln_b=rwkv_ln_b, w_branch_b=w_branch_b, w_out=w_out, attn_post_norm=attn_post_norm, ffn_pre_norm=ffn_pre_norm, w_up=w_up, conv_w=conv_w, conv_b=conv_b, w_down=w_down, ffn_post_norm=ffn_post_norm)
    mo = dict(attn_pre_norm=m_attn_pre_norm, w_in=m_w_in, hgrn_lb=m_hgrn_lb, hgrn_gnorm=m_hgrn_gnorm, w_branch_a=m_w_branch_a, rwkv_mu=m_rwkv_mu, rwkv_w0=m_rwkv_w0, rwkv_w2=m_rwkv_w2, rwkv_a0=m_rwkv_a0, rwkv_a2=m_rwkv_a2, rwkv_g2=m_rwkv_g2, rwkv_k_k=m_rwkv_k_k, rwkv_k_a=m_rwkv_k_a, rwkv_r_k=m_rwkv_r_k, rwkv_ln_w=m_rwkv_ln_w, rwkv_ln_b=m_rwkv_ln_b, w_branch_b=m_w_branch_b, w_out=m_w_out, attn_post_norm=m_attn_post_norm, ffn_pre_norm=m_ffn_pre_norm, w_up=m_w_up, conv_w=m_conv_w, conv_b=m_conv_b, w_down=m_w_down, ffn_post_norm=m_ffn_post_norm)
    vo = dict(attn_pre_norm=v_attn_pre_norm, w_in=v_w_in, hgrn_lb=v_hgrn_lb, hgrn_gnorm=v_hgrn_gnorm, w_branch_a=v_w_branch_a, rwkv_mu=v_rwkv_mu, rwkv_w0=v_rwkv_w0, rwkv_w2=v_rwkv_w2, rwkv_a0=v_rwkv_a0, rwkv_a2=v_rwkv_a2, rwkv_g2=v_rwkv_g2, rwkv_k_k=v_rwkv_k_k, rwkv_k_a=v_rwkv_k_a, rwkv_r_k=v_rwkv_r_k, rwkv_ln_w=v_rwkv_ln_w, rwkv_ln_b=v_rwkv_ln_b, w_branch_b=v_w_branch_b, w_out=v_w_out, attn_post_norm=v_attn_post_norm, ffn_pre_norm=v_ffn_pre_norm, w_up=v_w_up, conv_w=v_conv_w, conv_b=v_conv_b, w_down=v_w_down, ffn_post_norm=v_ffn_post_norm)

    t = x.shape[1]
    x2 = x.reshape(t, D)
    tgt = loss_target.reshape(t, D)
    st = _stages()

    me = 4 * lax.axis_index("x") + 2 * lax.axis_index("y") + lax.axis_index("c")
    small = jnp.concatenate([rwkv_w2[0], rwkv_a2[0], rwkv_g2[0]], axis=0).astype(BF)
    g_in, g_small = _all_gather("gather_weights", [w_in[0].T.astype(BF), small])
    fw_in_t = g_in.reshape(IN_COLS, D)
    z64 = jnp.zeros((64, D), BF)
    w2p = jnp.concatenate([_blocks_to_cols(g_small[:, 0:64]), z64], axis=0)
    a2p = jnp.concatenate([z64, _blocks_to_cols(g_small[:, 64:128])], axis=0)
    g2f = _blocks_to_cols(g_small[:, 128:256])
    conv_bits = jnp.pad(lax.bitcast_convert_type(conv_w[0], BF).reshape(3, 2 * 704), ((0, 29), (0, 0)))
    late = [w_up[0].T.astype(BF)] + [w[k][0].astype(BF) for k in _BIG[2:]] + [conv_bits]
    late_gather = _Exchange("gather2", late)
    r_k = rwkv_r_k.reshape(1, D)

    xn, z = _norm_in_proj(x2, attn_pre_norm, fw_in_t, 512, 4736)
    mix_par = [hgrn_lb, hgrn_gnorm, rwkv_mu, rwkv_w0, w2p, rwkv_a0, a2p, g2f, rwkv_k_k, rwkv_k_a,
               rwkv_ln_w, rwkv_ln_b, r_k]
    mix_in = [z]
    (o_a, o_b), mix_saved = _stage_fwd(st["mixers"], t, mix_par, mix_in, hook=late_gather)
    gl = [lax.dynamic_update_slice(g, own[None], (me, 0, 0)) for g, own in zip(late_gather.results, late)]
    fw_up_t = gl[0].reshape(2 * DFF, D)
    fw_down = gl[1].reshape(DFF, D)
    fw_a, fw_b, fw_out = (g.reshape(D, D) for g in gl[2:5])
    conv_full = _blocks_to_cols(lax.bitcast_convert_type(gl[5][:, :3].reshape(N_DEV, 3, 704, 2), F32))
    y_a, y_b, merged, mix, h1, xn2 = _merge_out_post(z, o_a, o_b, fw_a, fw_b, fw_out, x2, attn_post_norm,
                                                     ffn_pre_norm, 512)
    hu = _mm("up_proj", xn2, fw_up_t, "nt", F32, tm=1024, tn=1408)
    conv_par = [conv_full, conv_b]
    (act,), conv_saved = _stage_fwd(st["conv"], t, conv_par, [hu])

    loss_acc, d_ffn_post, dh1, dff = _down_loss(act, fw_down, ffn_post_norm, h1, tgt, 512)
    dact = _mm("d_act", dff, fw_down, "nt", BF, tm=1024, tn=1408)
    dw_down = _mm("dw_down", act, dff, "tn", BF, tm=1408, tn=512)
    (dcw, dcb), (dhu,) = _stage_bwd(st["conv"], t, conv_par, [hu], conv_saved, [[dact]], [BF])
    dw_up_t = _mm("dw_up", dhu, xn2, "tn", BF, tm=1408, tn=1024)
    d_post, d_pre2, dx_a, dmix = _dxn2_post1_bwd(dhu, fw_up_t, x2, mix, dh1, attn_post_norm, ffn_pre_norm, 512)
    dga, dgb, dy_a, dy_b, do_a, do_b = _dmerged_merge_bwd(dmix, fw_out, fw_a, fw_b, z, y_a, y_b, 512)
    dw_a, dw_b, dw_out = _mm_multi("dw_branches", [(o_a, dy_a), (o_b, dy_b), (merged, dmix)], "tn", BF)
    early = [dw_up_t.reshape(N_DEV, 704, D), dw_down.reshape(N_DEV, 352, D), dw_a.reshape(N_DEV, 128, D),
             dw_b.reshape(N_DEV, 128, D), dw_out.reshape(N_DEV, 128, D), _cols_to_blocks(dcw.astype(BF), 704)]
    early_scatter = _Exchange("scatter", early)
    mix_dp, dz_hr = _stage_bwd(st["mixers"], t, mix_par, mix_in, mix_saved, [[do_a], [do_b]], [BF],
                               hook=early_scatter)
    d_lb, d_gn, d_mu, d_w0, d_w2p, d_a0, d_a2p, d_g2, d_kk, d_ka, d_lnw, d_lnb, d_rk = mix_dp
    dz = dz_hr + [dga, dgb]
    dw_in_t = _mm_cols_tn("dw_in", dz, xn, BF, 256)

    ax, ay, ac = lax.axis_index("x"), lax.axis_index("y"), lax.axis_index("c")
    idx4 = jnp.stack([4 * cx + 2 * cy + ac for cx, cy in ((ax, ay), (1 - ax, ay), (ax, 1 - ay), (1 - ax, 1 - ay))])
    idx4 = idx4.astype(jnp.int32)
    idx_me, idx_0 = idx4[0:1], jnp.zeros((1,), jnp.int32)
    d_small = jnp.concatenate([d_w2p[:64], d_a2p[64:], d_g2], axis=0).astype(BF)
    g8s = [dw_in_t.reshape(N_DEV, 1184, D), _cols_to_blocks(d_small, LANES)]
    recv4s = _reduce_pair(g8s)
    sums = [_pair_sum("pair_sum_" + n, idx4, g, r) for n, g, r in zip(("w_in", "small"), g8s, recv4s)]
    swap_ssem, swap_rsem, swap_srcs, swap_lands, token = _chip_swap_start([s[1] for s in sums])
    d_pre1, dx = _dxn_pre1_bwd(dz, fw_in_t, x2, dx_a, attn_pre_norm, 256, token)
    grad_x = dx.reshape(x.shape)

    sh_out = [dict() for _ in range(4)]
    done = []
    for n, own, recv in zip(_BIG[1:] + ("conv_w",), early, early_scatter.results):
        tr = (lambda a: a.T) if n == "w_up" else (lambda a: a)
        res = _adam_sharded("adam_" + n, idx_me, own, recv, *[tr(src[n][0]) for src in (w, mo, vo)], after=token)
        done.append(res[0])
        for kind in range(4):
            sh_out[kind][n] = tr(res[kind])[None]

    rg = dict(attn_pre_norm=d_pre1, hgrn_lb=d_lb, hgrn_gnorm=d_gn, rwkv_mu=d_mu, rwkv_w0=d_w0, rwkv_a0=d_a0,
              rwkv_k_k=d_kk, rwkv_k_a=d_ka, rwkv_r_k=d_rk, rwkv_ln_w=d_lnw, rwkv_ln_b=d_lnb, attn_post_norm=d_post,
              ffn_pre_norm=d_pre2, conv_b=dcb, ffn_post_norm=d_ffn_post)
    g8 = _all_gather_small("gather_small_grads", _pack_replicated(rg, loss_acc, done))
    rnames = [n for n, _ in REPL]
    flat = lambda src: [src[n].reshape(1, D) if n == "rwkv_r_k" else src[n] for n in rnames]
    rp_out, loss_row = _adam_replicated(g8, flat(w), flat(mo), flat(vo))
    loss = loss_row[0, 0]
    recv3s = _chip_swap_wait(swap_ssem, swap_rsem, swap_srcs, swap_lands, rp_out[0]["attn_pre_norm"])
    for kind in range(4):
        rp_out[kind]["rwkv_r_k"] = rp_out[kind]["rwkv_r_k"].reshape(rwkv_r_k.shape)

    def small_of(src):
        return jnp.concatenate([src["rwkv_w2"][0], src["rwkv_a2"][0], src["rwkv_g2"][0]], axis=0)

    res = _adam_sharded("adam_w_in", idx_0, sums[0][0][None], recv3s[0], *[src["w_in"][0].T for src in (w, mo, vo)])
    res_s = _adam_sharded("adam_small", idx_0, sums[1][0][None], recv3s[1], *[small_of(src) for src in (w, mo, vo)])
    for kind in range(4):
        sh_out[kind]["w_in"] = res[kind].T[None]
        sh_out[kind]["rwkv_w2"] = res_s[kind][0:64][None]
        sh_out[kind]["rwkv_a2"] = res_s[kind][64:128][None]
        sh_out[kind]["rwkv_g2"] = res_s[kind][128:256][None]

    outs = [loss, grad_x]
    for kind in range(4):
        for name in _WEIGHTS:
            outs.append(sh_out[kind][name] if name in sh_out[kind] else rp_out[kind][name])
    return tuple(outs)
```

```python
import functools

import jax
import jax.numpy as jnp
from jax import lax
from jax.experimental import pallas as pl
from jax.experimental.pallas import tpu as pltpu

F32 = jnp.float32
BF = jnp.bfloat16
MESH = pl.DeviceIdType.MESH

D = 1024
HG_HEADS = 8
HG_K = 128
HG_CHUNK = 32
HG_SCALE = HG_K ** -0.5
HG_PER_STEP = 8
RW_HEADS = 16
RW_N = 64
RW_CHUNK = 64
RW_PAIRS_PER_STEP = 8
DFF = 2816
IN_COLS = 9472
RW_COLS = 3328
EPS = 1e-6
GN_EPS = 1e-5 * RW_N
ADAM_LR = 0.001
ADAM_B1 = 0.9
ADAM_B2 = 0.999
ADAM_EPS = 1e-08
ADAM_WD = 0.01
ADAM_STEP = 10
N_DEV = 8
LANES = 128
VMEM_LIMIT = 56 * 1024 * 1024
TILE_BYTES = 1280 * 1024

REPL = (("attn_pre_norm", 1024), ("hgrn_lb", 1024), ("hgrn_gnorm", 1024), ("rwkv_mu", 3328), ("rwkv_w0", 1024),
        ("rwkv_a0", 1024), ("rwkv_k_k", 1024), ("rwkv_k_a", 1024), ("rwkv_r_k", 1024), ("rwkv_ln_w", 1024),
        ("rwkv_ln_b", 1024), ("attn_post_norm", 1024), ("ffn_pre_norm", 1024), ("conv_b", 5632), ("ffn_post_norm", 1024))
REPL_ROWS = {"hgrn_lb": 2}
REPL_TOTAL = 32


def _cparams(sem=None, **kw):
    return pltpu.CompilerParams(dimension_semantics=sem, vmem_limit_bytes=VMEM_LIMIT, **kw)


_DN = {"nn": ((1,), (0,)), "nt": ((1,), (1,)), "tn": ((0,), (0,))}


def _raw_dot(a, b, mode):
    return lax.dot_general(a.astype(BF), b.astype(BF), (_DN[mode], ((), ())), preferred_element_type=F32)


@functools.partial(jax.custom_vjp, nondiff_argnums=(2,))
def _dot(a, b, mode):
    return _raw_dot(a, b, mode)


def _dot_fwd(a, b, mode):
    return _raw_dot(a, b, mode), (a, b)


def _dot_bwd(mode, res, g):
    a, b = res
    if mode == "nn":
        return _dot(g, b, "nt"), _dot(a, g, "tn")
    if mode == "nt":
        return _dot(g, b, "nn"), _dot(g, a, "tn")
    return _dot(b, g, "nt"), _dot(a, g, "nn")


_dot.defvjp(_dot_fwd, _dot_bwd)


def _bf_pieces(x, n):
    out, r = [], x
    for i in range(n):
        p = r.astype(BF)
        out.append(p)
        if i + 1 < n:
            r = r - p.astype(F32)
    return out


def _raw_split_dot(x, e, mode, n, x_left):
    eb = e.astype(BF)
    acc = None
    for p in _bf_pieces(x, n):
        ops = (p, eb) if x_left else (eb, p)
        t = lax.dot_general(*ops, (_DN[mode], ((), ())), preferred_element_type=F32)
        acc = t if acc is None else acc + t
    return acc


def _raw_headsum(x):
    t = x.shape[0]
    i = lax.broadcasted_iota(jnp.int32, (LANES, LANES), 0)
    j = lax.broadcasted_iota(jnp.int32, (LANES, LANES), 1)
    same = jnp.where((i >= RW_N) == (j >= RW_N), 1.0, 0.0).astype(F32)
    groups = x.shape[1] // LANES
    rows = jnp.concatenate([x[:, q * LANES:(q + 1) * LANES] for q in range(groups)], axis=0)
    s = _raw_split_dot(rows, same, "nn", 2, True)
    return jnp.concatenate([s[q * t:(q + 1) * t] for q in range(groups)], axis=1)


@jax.custom_vjp
def _headsum(x):
    return _raw_headsum(x)


def _headsum_fwd(x):
    return _raw_headsum(x), None


def _headsum_bwd(_, g):
    return (_raw_headsum(g),)


_headsum.defvjp(_headsum_fwd, _headsum_bwd)


@functools.partial(jax.custom_vjp, nondiff_argnums=(2,))
def _tdot(tri, x, n):
    return _raw_split_dot(x, tri, "nn", n, False)


def _tdot_fwd(tri, x, n):
    return _raw_split_dot(x, tri, "nn", n, False), tri


def _tdot_bwd(n, tri, g):
    return jnp.zeros_like(tri), _raw_split_dot(g, tri, "tn", n, False)


_tdot.defvjp(_tdot_fwd, _tdot_bwd)


def _row(x, i):
    r = lax.broadcasted_iota(jnp.int32, x.shape, 0)
    return jnp.sum(jnp.where(r == i, x, 0.0), axis=0, keepdims=True)


def _shift_down(x, prev):
    t = x.shape[0]

    @jax.custom_vjp
    def sh(x, prev):
        r = lax.broadcasted_iota(jnp.int32, x.shape, 0)
        return jnp.where(r == 0, prev, pltpu.roll(x, 1, 0))

    def fwd(x, prev):
        return sh(x, prev), None

    def bwd(_, g):
        r = lax.broadcasted_iota(jnp.int32, g.shape, 0)
        dx = jnp.where(r == t - 1, 0.0, pltpu.roll(g, t - 1, 0))
        return dx, jnp.sum(jnp.where(r == 0, g, 0.0), axis=0, keepdims=True)

    sh.defvjp(fwd, bwd)
    return sh(x, prev)


def _sigmoid(x):
    return jax.nn.sigmoid(x)


def _silu(x):
    return x * jax.nn.sigmoid(x)


def _softplus(x):
    return jnp.maximum(x, 0.0) + jnp.log(1.0 + jnp.exp(-jnp.abs(x)))


def _rms(x, g):
    return (x * lax.rsqrt(jnp.mean(x * x, axis=-1, keepdims=True) + EPS)) * g


def _tril(c):
    r = lax.broadcasted_iota(jnp.int32, (c, c), 0)
    cc = lax.broadcasted_iota(jnp.int32, (c, c), 1)
    return cc <= r


def _f_pre1_residual(ps, xs, cs):
    return [_rms(xs[0], ps[0]), xs[0]], []


def _f_hgrn(ps, xs, cs):
    lbraw, gn = ps
    hq, hf, hi, hg = xs
    hd = range(HG_PER_STEP)
    st = [cs[0][p * HG_K:(p + 1) * HG_K] for p in hd]
    l0, l1 = _row(lbraw, 0), _row(lbraw, 1)
    m = jnp.maximum(l0, l1)
    e0, e1 = jnp.exp(l0 - m), jnp.exp(l1 - m)
    lb = e0 / (e0 + e1)
    q = _silu(hq) * HG_SCALE
    f = lb + (1.0 - lb) * _sigmoid(hf)
    kh = 1.0 - f
    gl = jnp.log(f)
    c = HG_CHUNK
    low = _tril(c)
    tri = jnp.where(low, 1.0, 0.0).astype(F32)
    outs = []
    for i in range(hq.shape[0] // c):
        rows = slice(i * c, (i + 1) * c)
        b = _tdot(tri, gl[rows], 3)
        bref = _row(b, c // 2 - 1)
        blast = _row(b, c - 1)
        qi = q[rows] * jnp.exp(b - bref)
        ki = kh[rows] * jnp.exp(bref - b)
        qd = q[rows] * jnp.exp(b)
        kd = kh[rows] * jnp.exp(blast - b)
        dec = jnp.exp(blast)
        sl = [slice(p * HG_K, (p + 1) * HG_K) for p in hd]
        sc = [jnp.where(low, _dot(qi[:, sl[p]], ki[:, sl[p]], "nt"), 0.0) for p in hd]
        o = [_dot(sc[p], hi[rows, sl[p]], "nn") + _dot(qd[:, sl[p]], st[p], "nt") for p in hd]
        u = [_dot(hi[rows, sl[p]], kd[:, sl[p]], "tn") for p in hd]
        st = [dec[:, sl[p]] * st[p] + u[p] for p in hd]
        outs.append(jnp.concatenate(o, axis=1) if len(o) > 1 else o[0])
    o = outs[0] if len(outs) == 1 else jnp.concatenate(outs, axis=0)
    on = []
    for p in hd:
        op = o[:, p * HG_K:(p + 1) * HG_K]
        on.append(op * lax.rsqrt(jnp.mean(op * op, axis=-1, keepdims=True) + EPS))
    o = jnp.concatenate(on, axis=1) if len(on) > 1 else on[0]
    o = o * gn
    return [o * _silu(hg)], [jnp.concatenate(st, axis=0) if len(st) > 1 else st[0]]


_RW_OFFS = (0, 1024, 2048, 3072, 3200, 3328)


def _f_rwpre(ps, xs, cs):
    mu, w0, w2p, a0, a2p, g2, k_k, k_a = ps
    (prev,) = cs
    t = xs[0].shape[0]
    zs = []
    for i, z in enumerate(xs):
        lo, hi = _RW_OFFS[i], _RW_OFFS[i + 1]
        zs.append(z + mu[:, lo:hi] * (_shift_down(z, prev[:, lo:hi]) - z))
    rr, kr, vr, wa, gz = zs
    w_log = -_softplus(-(w0 + _dot(jnp.tanh(wa), w2p, "nn"))) - 0.5
    lw = -jnp.exp(w_log)
    a = _sigmoid(a0 + _dot(wa, a2p, "nn"))
    g = _dot(_sigmoid(gz), g2, "nn")
    kkr = kr * k_k
    kk = kkr / jnp.maximum(jnp.sqrt(_headsum(kkr * kkr)), 1e-12)
    k2 = kr * (1.0 + (a - 1.0) * k_a)
    newprev = jnp.concatenate([_row(z, t - 1) for z in xs], axis=1)
    return [rr, lw, k2, vr, -kk, kk * a, g], [newprev]


def _raw_inverses(ls):
    n = ls[0].shape[0]
    r = lax.broadcasted_iota(jnp.int32, (n, n), 0)
    c = lax.broadcasted_iota(jnp.int32, (n, n), 1)
    eye = jnp.where(r == c, 1.0, 0.0).astype(F32)
    tinv = [eye + l for l in ls]
    pw = ls
    for _ in range(5):
        pw = [_raw_dot(p, p, "nn") for p in pw]
        tinv = [t + _raw_dot(t, p, "nn") for t, p in zip(tinv, pw)]
    return tinv


@jax.custom_vjp
def _unit_lower_inverses(ls):
    return _raw_inverses(ls)


def _inverses_fwd(ls):
    tinv = _raw_inverses(ls)
    return tinv, tinv


def _inverses_bwd(tinv, gs):
    return ([_raw_dot(_raw_dot(t, g, "tn"), t, "nt") for t, g in zip(tinv, gs)],)


_unit_lower_inverses.defvjp(_inverses_fwd, _inverses_bwd)


@jax.custom_vjp
def _known_inverses(ls, tinv):
    return tinv


def _known_fwd(ls, tinv):
    return tinv, tinv


def _known_bwd(tinv, gs):
    return [_raw_dot(_raw_dot(t, g, "tn"), t, "nt") for t, g in zip(tinv, gs)], [jnp.zeros_like(t) for t in tinv]


_known_inverses.defvjp(_known_fwd, _known_bwd)


@jax.custom_vjp
def _use_kept(computed, kept):
    return kept


def _use_kept_fwd(computed, kept):
    return kept, None


def _use_kept_bwd(_, g):
    return g, jax.tree.map(jnp.zeros_like, g)


_use_kept.defvjp(_use_kept_fwd, _use_kept_bwd)

RW_KEPT = 5


def _f_rwscan(ps, xs, cs, kept=None):
    state = cs[0]
    ys, keep = [], []
    n = 2 * RW_CHUNK
    per_chunk = RW_KEPT * RW_PAIRS_PER_STEP * n
    for i in range(xs[0].shape[0] // RW_CHUNK):
        known = None
        if kept is not None:
            known = [[kept[i * per_chunk + (q * RW_PAIRS_PER_STEP + p) * n:
                           i * per_chunk + (q * RW_PAIRS_PER_STEP + p + 1) * n] for p in range(RW_PAIRS_PER_STEP)]
                     for q in range(RW_KEPT)]
        y, state, mats = _rwkv_chunk([x[i * RW_CHUNK:(i + 1) * RW_CHUNK] for x in xs], state, known)
        ys.append(y)
        keep += [m for group in mats for m in group]
    return [ys[0] if len(ys) == 1 else jnp.concatenate(ys, axis=0)], [state], jnp.concatenate(keep, axis=0)


def _rwkv_chunk(xs, state, known=None):
    npair = RW_PAIRS_PER_STEP
    pr = range(npair)
    r, lw, k, v, av, bv = [[x[:, p * LANES:(p + 1) * LANES] for p in pr] for x in xs]
    sv = [state[p * LANES:(p + 1) * LANES] for p in pr]
    c = RW_CHUNK
    n = 2 * c
    tri = jnp.where(_tril(c), 1.0, 0.0).astype(F32)
    cl = [_tdot(tri, lw[p], 3) for p in pr]
    cl_last = [_row(cl[p], c - 1) for p in pr]
    lane = lax.broadcasted_iota(jnp.int32, (c, LANES), 1)
    h0 = lane < RW_N

    def stack(x):
        return jnp.concatenate([jnp.where(h0, x, 0.0), jnp.where(h0, 0.0, x)], axis=0)

    am = [stack(av[p] * jnp.exp(cl[p] - lw[p])) for p in pr]
    bm = [stack(bv[p] * jnp.exp(-cl[p])) for p in pr]
    km = [stack(k[p] * jnp.exp(-cl[p])) for p in pr]
    rm = [stack(r[p] * jnp.exp(cl[p])) for p in pr]
    vm = [stack(v[p]) for p in pr]
    rn = lax.broadcasted_iota(jnp.int32, (n, n), 0)
    cn = lax.broadcasted_iota(jnp.int32, (n, n), 1)
    blk = (rn >= c) == (cn >= c)
    strict = blk & (cn < rn)
    incl = blk & (cn <= rn)
    lab = [jnp.where(strict, _dot(am[p], bm[p], "nt"), 0.0) for p in pr]
    lak = [jnp.where(strict, _dot(am[p], km[p], "nt"), 0.0) for p in pr]
    wrb = [jnp.where(incl, _dot(rm[p], bm[p], "nt"), 0.0) for p in pr]
    wrk = [jnp.where(incl, _dot(rm[p], km[p], "nt"), 0.0) for p in pr]
    if known is None:
        tinv = _unit_lower_inverses(lab)
    else:
        tinv = _known_inverses(lab, known[0])
        lak, wrb, wrk = _use_kept(lak, known[1]), _use_kept(wrb, known[2]), _use_kept(wrk, known[3])
    rhs = [_dot(am[p], sv[p], "nt") + _dot(lak[p], vm[p], "nn") for p in pr]
    um = [_dot(tinv[p], rhs[p], "nn") for p in pr]
    if known is not None:
        um = _use_kept(um, known[4])
    ym = [_dot(rm[p], sv[p], "nt") + _dot(wrb[p], um[p], "nn") + _dot(wrk[p], vm[p], "nn") for p in pr]
    sn = [(sv[p] + _dot(um[p], bm[p], "tn") + _dot(vm[p], km[p], "tn")) * jnp.exp(cl_last[p]) for p in pr]
    ys = [ym[p][:c] + ym[p][c:] for p in pr]
    return jnp.concatenate(ys, axis=1), jnp.concatenate(sn, axis=0), [tinv, lak, wrb, wrk, um]


def _f_mixers(ps, xs, cs):
    return _mixers(ps, xs, cs, None)


def _f_mixers_kept(ps, xs, cs, kept):
    return _mixers(ps, xs, cs, kept[0])[:2]


def _mixers(ps, xs, cs, kept):
    oa, st = _f_hgrn(ps[:2], xs[:4], cs[:1])
    (r, lw, k, v, av, bv, g), prev = _f_rwpre(ps[2:10], xs[4:], cs[1:2])
    y, sv, keep = _f_rwscan([], [r, lw, k, v, av, bv], cs[2:], kept)
    ob, _ = _f_rwpost(ps[10:], y + [r, k, v, g], [])
    return oa + ob, st + prev + sv, [keep]


def _f_rwpost(ps, xs, cs):
    ln_w, ln_b, r_k = ps
    y, r, k, v, g = xs
    inv_n = 1.0 / RW_N
    yc = y - _headsum(y) * inv_n
    var = _headsum(yc * yc) * inv_n
    yn = yc * lax.rsqrt(var + GN_EPS)
    yn = yn * ln_w + ln_b
    bonus = _headsum(r * k * r_k) * v
    return [(yn + bonus) * g], []


def _f_merge(ps, xs, cs):
    ga, gb, ya, yb = xs
    return [_sigmoid(ga) * ya + _sigmoid(gb) * yb], []


def _f_post1(ps, xs, cs):
    x, mix = xs
    h1 = x + _rms(mix, ps[0])
    return [h1, _rms(h1, ps[1])], []


def _f_conv(ps, xs, cs):
    cw, cb = ps
    p1, p2 = cs
    w0, w1, w2 = _row(cw, 0), _row(cw, 1), _row(cw, 2)
    t = xs[0].shape[0]
    hc = []
    for i, x in enumerate(xs):
        sl = slice(i * DFF, (i + 1) * DFF)
        s1 = _shift_down(x, p1[:, sl])
        s2 = _shift_down(s1, p2[:, sl])
        hc.append(cb[:, sl] + w0[:, sl] * s2 + w1[:, sl] * s1 + w2[:, sl] * x)
    n1 = jnp.concatenate([_row(x, t - 1) for x in xs], axis=1)
    n2 = jnp.concatenate([_row(x, t - 2) for x in xs], axis=1)
    return [_silu(hc[0]) * hc[1]], [n1, n2]


class _Stage:
    def __init__(self, name, f, g, tm, par_per_g, in_pieces, in_offs, carry_shapes, out_pieces, out_dtypes,
                 kept_shapes=(), f_kept=None):
        self.name, self.f, self.g, self.tm = name, f, g, tm
        self.par_per_g, self.in_pieces, self.in_offs = par_per_g, in_pieces, in_offs
        self.carry_shapes, self.out_pieces, self.out_dtypes = carry_shapes, out_pieces, out_dtypes
        self.kept_shapes, self.f_kept = list(kept_shapes), f_kept


def _par_spec(arr, per_g, g):
    r, c = arr.shape
    if per_g:
        return pl.BlockSpec((r, c // g), lambda gi, ni: (0, gi))
    return pl.BlockSpec((r, c), lambda gi, ni: (0, 0))


def _row_spec(tm, width, off, n, rev):
    if rev:
        return pl.BlockSpec((tm, width), lambda gi, ni: (n - 1 - ni, off + gi))
    return pl.BlockSpec((tm, width), lambda gi, ni: (ni, off + gi))


def _carry_spec(shape, n, rev):
    if rev:
        return pl.BlockSpec((None, None) + shape, lambda gi, ni: (gi, n - 1 - ni, 0, 0))
    return pl.BlockSpec((None, None) + shape, lambda gi, ni: (gi, ni, 0, 0))


def _load_pieces(refs, pieces_list):
    out = []
    for ref, pieces in zip(refs, pieces_list):
        o = 0
        for w in pieces:
            out.append(ref[:, o:o + w].astype(F32))
            o += w
    return out


def _store_pieces(refs, pieces_list, vals):
    k = 0
    for ref, pieces in zip(refs, pieces_list):
        o = 0
        for w in pieces:
            ref[:, o:o + w] = vals[k].astype(ref.dtype)
            k += 1
            o += w


_ANY = pl.BlockSpec(memory_space=pl.ANY)


class _Exchange:
    def __init__(self, kind, arrs):
        self.kind, self.arrs, self.results = kind, list(arrs), None
        if kind == "scatter":
            self.out_shape = [jax.ShapeDtypeStruct((N_DEV - 1,) + a.shape[1:], a.dtype) for a in self.arrs]
        else:
            self.out_shape = [jax.ShapeDtypeStruct((N_DEV,) + a.shape, a.dtype) for a in self.arrs]
        self.nsem = (N_DEV if kind == "gather2" else N_DEV - 1) * len(self.arrs)

    def copies(self, in_refs, out_refs, ssem, rsem):
        x, y, c = lax.axis_index("x"), lax.axis_index("y"), lax.axis_index("c")
        me = 4 * x + 2 * y + c
        cps = []
        for a, (i_ref, o_ref) in enumerate(zip(in_refs, out_refs)):
            for j in range(1, N_DEV):
                px = 1 - x if j & 4 else x
                py = 1 - y if j & 2 else y
                pc = 1 - c if j & 1 else c
                if self.kind == "gather":
                    src, dst = i_ref, o_ref.at[me]
                else:
                    src, dst = i_ref.at[4 * px + 2 * py + pc], o_ref.at[j - 1]
                s = (N_DEV - 1) * a + j - 1
                cps.append(pltpu.make_async_remote_copy(src_ref=src, dst_ref=dst, send_sem=ssem.at[s],
                                                        recv_sem=rsem.at[s], device_id=(px, py, pc),
                                                        device_id_type=MESH))
        return cps

    def run(self, step, total, in_refs, out_refs, ssem, rsem):
        if self.kind == "gather2":
            return self.run_two_level(step, total, in_refs, out_refs, ssem, rsem)

        @pl.when(step == 0)
        def _():
            for cp in self.copies(in_refs, out_refs, ssem, rsem):
                cp.start()

        @pl.when(step == total - 1)
        def _():
            for cp in self.copies(in_refs, out_refs, ssem, rsem):
                cp.wait()

    def run_two_level(self, step, total, in_refs, out_refs, ssem, rsem):
        x, y, c = lax.axis_index("x"), lax.axis_index("y"), lax.axis_index("c")
        sibling, xn, yn = (x, y, 1 - c), (1 - x, y, c), (x, 1 - y, c)
        arrs = range(len(in_refs))
        ns = N_DEV

        def num(px, py, pc):
            return 4 * px + 2 * py + pc

        def copy(a, k, to, src, dst):
            return pltpu.make_async_remote_copy(src_ref=src, dst_ref=dst, send_sem=ssem.at[ns * a + k],
                                                recv_sem=rsem.at[ns * a + k], device_id=to, device_id_type=MESH)

        def blk(a, b):
            return out_refs[a].at[b]

        def half(a, b, second):
            h = self.arrs[a].shape[0] // 2
            return out_refs[a].at[b, pl.ds(h if second else 0, h)]

        bx, by, bd = num(1 - x, y, c), num(x, 1 - y, c), num(1 - x, 1 - y, c)

        def firsts(a):
            own = blk(a, num(x, y, c))
            return [copy(a, 0, sibling, in_refs[a], own), copy(a, 1, xn, in_refs[a], own),
                    copy(a, 2, yn, in_refs[a], own)]

        def seconds(a):
            return [copy(a, 3, yn, half(a, bx, False), half(a, bx, False)), copy(a, 5, sibling, blk(a, bx), blk(a, bx)),
                    copy(a, 4, xn, half(a, by, True), half(a, by, True)), copy(a, 6, sibling, blk(a, by), blk(a, by))]

        def third(a):
            return copy(a, 7, sibling, blk(a, bd), blk(a, bd))

        @pl.when(step == 0)
        def _():
            for a in arrs:
                for cp in firsts(a):
                    cp.start()

        @pl.when(step == total // 2)
        def _():
            for a in arrs:
                copy(a, 1, xn, blk(a, bx), blk(a, bx)).wait_recv()
                copy(a, 2, yn, blk(a, by), blk(a, by)).wait_recv()
                for cp in seconds(a):
                    cp.start()

        @pl.when(step == (4 * total) // 5)
        def _():
            for a in arrs:
                copy(a, 3, yn, half(a, bd, False), half(a, bd, False)).wait_recv()
                copy(a, 4, xn, half(a, bd, True), half(a, bd, True)).wait_recv()
                third(a).start()

        @pl.when(step == total - 1)
        def _():
            for a in arrs:
                for k, b in ((0, num(x, y, 1 - c)), (5, num(1 - x, y, 1 - c)), (6, num(x, 1 - y, 1 - c)),
                             (7, num(1 - x, 1 - y, 1 - c))):
                    copy(a, k, sibling, blk(a, b), blk(a, b)).wait_recv()
                for cp in firsts(a) + seconds(a) + [third(a)]:
                    cp.wait_send()


def _hook_specs(hook):
    if hook is None:
        return [], [], [], []
    na = len(hook.arrs)
    sems = [pltpu.SemaphoreType.DMA((hook.nsem,)), pltpu.SemaphoreType.DMA((hook.nsem,))]
    return [_ANY] * na, [_ANY] * na, hook.out_shape, sems


def _stage_fwd(st, t, params, inputs, hook=None):
    g, tm = st.g, min(st.tm, t)
    n = t // tm
    npar, nin, ncar, nout = len(params), len(inputs), len(st.carry_shapes), len(st.out_pieces)
    nk = len(st.kept_shapes)
    h_in, h_out, h_shape, h_sems = _hook_specs(hook)
    nh = len(h_in)

    def body(*refs):
        p_refs = refs[:npar]
        x_refs = refs[npar:npar + nin]
        hi_refs = refs[npar + nin:npar + nin + nh]
        o = npar + nin + nh
        o_refs = refs[o:o + nout]
        s_refs = refs[o + nout:o + nout + ncar]
        k_refs = refs[o + nout + ncar:o + nout + ncar + nk]
        o += nout + ncar + nk
        ho_refs = refs[o:o + nh]
        c_scr = refs[o + nh:o + nh + ncar]
        gi, ni = pl.program_id(0), pl.program_id(1)
        if hook is not None:
            step = gi * n + ni
            hook.run(step, g * n, hi_refs, ho_refs, *refs[-2:])

        @pl.when(ni == 0)
        def _():
            for c in c_scr:
                c[...] = jnp.zeros(c.shape, F32)

        ps = [r[...].astype(F32) for r in p_refs]
        xs = _load_pieces(x_refs, st.in_pieces)
        cs = [c[...] for c in c_scr]
        for s, c in zip(s_refs, cs):
            s[...] = c
        res = st.f(ps, xs, cs)
        outs, ncs = res[0], res[1]
        _store_pieces(o_refs, st.out_pieces, outs)
        for c, v in zip(c_scr, ncs):
            c[...] = v
        for kr, kv in zip(k_refs, res[2] if nk else []):
            kr[...] = kv.astype(kr.dtype)

    in_specs = [_par_spec(p, pg, g) for p, pg in zip(params, st.par_per_g)]
    in_specs += [_row_spec(tm, sum(pc), off, n, False) for pc, off in zip(st.in_pieces, st.in_offs)]
    out_specs = [_row_spec(tm, sum(pc), 0, n, False) for pc in st.out_pieces]
    out_specs += [_carry_spec(s, n, False) for s in st.carry_shapes]
    out_specs += [pl.BlockSpec(s, lambda gi, ni: (ni, 0)) for s in st.kept_shapes]
    out_shape = [jax.ShapeDtypeStruct((t, g * sum(pc)), dt) for pc, dt in zip(st.out_pieces, st.out_dtypes)]
    out_shape += [jax.ShapeDtypeStruct((g, n) + s, F32) for s in st.carry_shapes]
    out_shape += [jax.ShapeDtypeStruct((n * s[0], s[1]), BF) for s in st.kept_shapes]
    res = pl.pallas_call(
        body, name=st.name + "_fwd", grid=(g, n), in_specs=in_specs + h_in, out_specs=out_specs + h_out,
        out_shape=out_shape + h_shape,
        scratch_shapes=[pltpu.VMEM(s, F32) for s in st.carry_shapes] + h_sems,
        compiler_params=_cparams(("arbitrary", "arbitrary")),
    )(*params, *inputs, *(hook.arrs if hook else []))
    if hook is not None:
        hook.results = list(res[nout + ncar + nk:])
    return list(res[:nout]), list(res[nout:nout + ncar + nk])


def _stage_bwd(st, t, params, inputs, saved, douts, dx_dtypes, hook=None):
    g, tm = st.g, min(st.tm, t)
    n = t // tm
    npar, nin, ncar = len(params), len(inputs), len(st.carry_shapes)
    nk = len(st.kept_shapes)
    flat_d = [d for ds in douts for d in ds]
    nd = len(flat_d)
    dx_idx = [i for i, dt in enumerate(dx_dtypes) if dt is not None]
    h_in, h_out, h_shape, h_sems = _hook_specs(hook)
    nh = len(h_in)

    def body(*refs):
        p_refs = refs[:npar]
        x_refs = refs[npar:npar + nin]
        s_refs = refs[npar + nin:npar + nin + ncar]
        k_refs = refs[npar + nin + ncar:npar + nin + ncar + nk]
        o = npar + nin + ncar + nk
        d_refs = refs[o:o + nd]
        hi_refs = refs[o + nd:o + nd + nh]
        o += nd + nh
        dp_refs = refs[o:o + npar]
        dx_refs = refs[o + npar:o + npar + len(dx_idx)]
        ho_refs = refs[o + npar + len(dx_idx):o + npar + len(dx_idx) + nh]
        dc_scr = refs[o + npar + len(dx_idx) + nh:o + npar + len(dx_idx) + nh + ncar]
        gi, ni = pl.program_id(0), pl.program_id(1)
        if hook is not None:
            step = gi * n + ni
            hook.run(step, g * n, hi_refs, ho_refs, *refs[-2:])

        @pl.when(ni == 0)
        def _():
            for c in dc_scr:
                c[...] = jnp.zeros(c.shape, F32)

        ps = [r[...].astype(F32) for r in p_refs]
        xs = _load_pieces(x_refs, st.in_pieces)
        cs = [s[...] for s in s_refs]
        dys = []
        k = 0
        for ds, pieces in zip(douts, st.out_pieces):
            acc = _load_pieces([d_refs[k]], [pieces])
            for j in range(1, len(ds)):
                more = _load_pieces([d_refs[k + j]], [pieces])
                acc = [a + b for a, b in zip(acc, more)]
            dys += acc
            k += len(ds)
        if nk:
            kept = [r[...].astype(F32) for r in k_refs]
            _, vjp = jax.vjp(lambda p, x, c: st.f_kept(p, x, c, kept), ps, xs, cs)
        else:
            _, vjp = jax.vjp(st.f, ps, xs, cs)
        dps, dxs, dcs = vjp((dys, [c[...] for c in dc_scr]))
        k = 0
        per_in = []
        for pieces in st.in_pieces:
            per_in.append(dxs[k:k + len(pieces)])
            k += len(pieces)
        for ref, i in zip(dx_refs, dx_idx):
            _store_pieces([ref], [st.in_pieces[i]], per_in[i])
        for c, v in zip(dc_scr, dcs):
            c[...] = v
        for ref, dp, pg in zip(dp_refs, dps, st.par_per_g):
            first = (ni == 0) if pg else ((ni == 0) & (gi == 0))

            @pl.when(first)
            def _():
                ref[...] = jnp.zeros(ref.shape, F32)

            ref[...] += dp

    in_specs = [_par_spec(p, pg, g) for p, pg in zip(params, st.par_per_g)]
    in_specs += [_row_spec(tm, sum(pc), off, n, True) for pc, off in zip(st.in_pieces, st.in_offs)]
    in_specs += [_carry_spec(s, n, True) for s in st.carry_shapes]
    in_specs += [pl.BlockSpec(s, lambda gi, ni: (n - 1 - ni, 0)) for s in st.kept_shapes]
    for ds, pc in zip(douts, st.out_pieces):
        in_specs += [_row_spec(tm, sum(pc), 0, n, True) for _ in ds]
    out_specs = [_par_spec(p, pg, g) for p, pg in zip(params, st.par_per_g)]
    out_specs += [_row_spec(tm, sum(st.in_pieces[i]), 0, n, True) for i in dx_idx]
    out_shape = [jax.ShapeDtypeStruct(p.shape, F32) for p in params]
    out_shape += [jax.ShapeDtypeStruct((t, g * sum(st.in_pieces[i])), dx_dtypes[i]) for i in dx_idx]
    res = pl.pallas_call(
        body, name=st.name + "_bwd", grid=(g, n), in_specs=in_specs + h_in, out_specs=out_specs + h_out,
        out_shape=out_shape + h_shape,
        scratch_shapes=[pltpu.VMEM(s, F32) for s in st.carry_shapes] + h_sems,
        compiler_params=_cparams(("arbitrary", "arbitrary")),
    )(*params, *inputs, *saved, *flat_d, *(hook.arrs if hook else []))
    if hook is not None:
        hook.results = list(res[npar + len(dx_idx):])
    return list(res[:npar]), list(res[npar:npar + len(dx_idx)])


def _pick(n, cap):
    if n <= cap:
        return n
    best = LANES
    for k in range(1, n // LANES + 1):
        if (n // LANES) % k == 0 and k * LANES <= cap:
            best = k * LANES
    return best


def _mm(name, a, b, mode, out_dtype=F32, tm=1024, tn=512, b_outer=False):
    m = a.shape[1] if mode == "tn" else a.shape[0]
    k = a.shape[0] if mode == "tn" else a.shape[1]
    n = b.shape[0] if mode == "nt" else b.shape[1]
    tm, tn = _pick(m, tm), _pick(n, tn)
    if b_outer:
        grid = (n // tn, m // tm)
        ij = lambda p, q: (q, p)
    else:
        grid = (m // tm, n // tn)
        ij = lambda p, q: (p, q)

    def body(a_ref, b_ref, o_ref):
        o_ref[...] = _raw_dot(a_ref[...], b_ref[...], mode).astype(o_ref.dtype)

    if mode == "tn":
        a_spec = pl.BlockSpec((k, tm), lambda p, q: (0, ij(p, q)[0]))
    else:
        a_spec = pl.BlockSpec((tm, k), lambda p, q: (ij(p, q)[0], 0))
    b_mode = dict(pipeline_mode=pl.Buffered(1)) if tn == n else {}
    if mode == "nt":
        b_spec = pl.BlockSpec((tn, k), lambda p, q: (ij(p, q)[1], 0), **b_mode)
    else:
        b_spec = pl.BlockSpec((k, tn), lambda p, q: (0, ij(p, q)[1]), **b_mode)
    return pl.pallas_call(
        body, name=name, grid=grid, in_specs=[a_spec, b_spec],
        out_specs=pl.BlockSpec((tm, tn), lambda p, q: ij(p, q)),
        out_shape=jax.ShapeDtypeStruct((m, n), out_dtype),
        compiler_params=_cparams(("arbitrary", "arbitrary")),
    )(a, b)


def _mm_multi(name, pairs, mode, out_dtype, tm=1024, tn=512):
    a0, b0 = pairs[0]
    m = a0.shape[1] if mode == "tn" else a0.shape[0]
    k = a0.shape[0] if mode == "tn" else a0.shape[1]
    n = b0.shape[0] if mode == "nt" else b0.shape[1]
    tm, tn = _pick(m, tm), _pick(n, tn)
    npair = len(pairs)

    def body(*refs):
        for p in range(npair):
            refs[2 * npair + p][...] = _raw_dot(refs[2 * p][...], refs[2 * p + 1][...], mode).astype(out_dtype)

    a_spec = pl.BlockSpec((k, tm), lambda i, j: (0, i)) if mode == "tn" else pl.BlockSpec((tm, k), lambda i, j: (i, 0))
    b_spec = pl.BlockSpec((tn, k), lambda i, j: (j, 0)) if mode == "nt" else pl.BlockSpec((k, tn), lambda i, j: (0, j))
    return pl.pallas_call(
        body, name=name, grid=(m // tm, n // tn), in_specs=[a_spec, b_spec] * npair,
        out_specs=[pl.BlockSpec((tm, tn), lambda i, j: (i, j))] * npair,
        out_shape=[jax.ShapeDtypeStruct((m, n), out_dtype)] * npair,
        compiler_params=_cparams(("arbitrary", "arbitrary")),
    )(*[x for pair in pairs for x in pair])


def _mm_cols_tn(name, pieces, b, out_dtype, tm):
    k, n = b.shape
    counts = [p.shape[1] // tm for p in pieces]
    starts = [sum(counts[:i]) for i in range(len(pieces))]
    na = len(pieces)

    def body(*refs):
        b_ref, o_ref = refs[na], refs[-1]
        i = pl.program_id(0)
        for a_ref, s, c in zip(refs[:na], starts, counts):
            @pl.when((i >= s) & (i < s + c))
            def _():
                o_ref[...] = _raw_dot(a_ref[...], b_ref[...], "tn").astype(o_ref.dtype)

    def spec(s, c):
        return pl.BlockSpec((k, tm), lambda i: (0, jnp.clip(i - s, 0, c - 1)))

    return pl.pallas_call(
        body, name=name, grid=(sum(counts),),
        in_specs=[spec(s, c) for s, c in zip(starts, counts)]
        + [pl.BlockSpec(b.shape, lambda i: (0, 0), pipeline_mode=pl.Buffered(1))],
        out_specs=pl.BlockSpec((tm, n), lambda i: (i, 0)),
        out_shape=jax.ShapeDtypeStruct((sum(counts) * tm, n), out_dtype),
        compiler_params=_cparams(("arbitrary",)),
    )(*pieces, b)


def _norm_in_proj(x, g, w_t, tm, tn):
    t, k = x.shape
    n = w_t.shape[0]
    tm, tn = _pick(t, tm), _pick(n, tn)

    def body(x_ref, g_ref, w_ref, xn_ref, z_ref):
        xn = _rms(x_ref[...], g_ref[...]).astype(BF)
        xn_ref[...] = xn
        z_ref[...] = _raw_dot(xn, w_ref[...], "nt")

    xns, z = pl.pallas_call(
        body, name="in_proj", grid=(n // tn, t // tm),
        in_specs=[pl.BlockSpec((tm, k), lambda j, i: (i, 0)), pl.BlockSpec((1, k), lambda j, i: (0, 0)),
                  pl.BlockSpec((tn, k), lambda j, i: (j, 0))],
        out_specs=[pl.BlockSpec((None, tm, k), lambda j, i: (j, i, 0)), pl.BlockSpec((tm, tn), lambda j, i: (i, j))],
        out_shape=[jax.ShapeDtypeStruct((n // tn, t, k), BF), jax.ShapeDtypeStruct((t, n), F32)],
        compiler_params=_cparams(("arbitrary", "arbitrary")),
    )(x, g, w_t)
    return xns[0], z


def _merge_out_post(z, o_a, o_b, w_a, w_b, w_out, x, g_post, g_pre2, tm):
    t = x.shape[0]
    tm = _pick(t, tm)
    w = 256
    npc = D // w
    ga0, gb0 = (IN_COLS - 2 * D) // w, (IN_COLS - D) // w

    def body(*refs):
        ga_refs, gb_refs = refs[:npc], refs[npc:2 * npc]
        oa_ref, ob_ref, wa_ref, wb_ref, w_ref, x_ref, gp_ref, g2_ref = refs[2 * npc:2 * npc + 8]
        ya_ref, yb_ref, m_ref, mix_ref, h_ref, xn_ref = refs[2 * npc + 8:]
        ya = _raw_dot(oa_ref[...], wa_ref[...], "nn").astype(BF)
        yb = _raw_dot(ob_ref[...], wb_ref[...], "nn").astype(BF)
        ya_ref[...] = ya
        yb_ref[...] = yb
        parts = []
        for p in range(npc):
            cols = slice(p * w, (p + 1) * w)
            parts.append(_sigmoid(ga_refs[p][...]) * ya[:, cols].astype(F32)
                         + _sigmoid(gb_refs[p][...]) * yb[:, cols].astype(F32))
        merged = jnp.concatenate(parts, axis=1).astype(BF)
        m_ref[...] = merged
        mix = _raw_dot(merged, w_ref[...], "nn")
        mix_ref[...] = mix
        h1 = x_ref[...] + _rms(mix, gp_ref[...])
        h_ref[...] = h1
        xn_ref[...] = _rms(h1, g2_ref[...]).astype(BF)

    row = pl.BlockSpec((tm, D), lambda i: (i, 0))
    one = pl.BlockSpec((1, D), lambda i: (0, 0))

    def gate(b0):
        return [pl.BlockSpec((tm, w), functools.partial(lambda i, b: (i, b), b=b0 + p)) for p in range(npc)]

    wgt = pl.BlockSpec((D, D), lambda i: (0, 0), pipeline_mode=pl.Buffered(1))
    return pl.pallas_call(
        body, name="merge_out_post", grid=(t // tm,),
        in_specs=gate(ga0) + gate(gb0) + [row, row, wgt, wgt, wgt, row, one, one],
        out_specs=[row] * 6,
        out_shape=[jax.ShapeDtypeStruct((t, D), BF), jax.ShapeDtypeStruct((t, D), BF), jax.ShapeDtypeStruct((t, D), BF),
                   jax.ShapeDtypeStruct((t, D), F32), jax.ShapeDtypeStruct((t, D), F32),
                   jax.ShapeDtypeStruct((t, D), BF)],
        compiler_params=_cparams(("arbitrary",)),
    )(*([z] * (2 * npc)), o_a, o_b, w_a, w_b, w_out, x, g_post, g_pre2)


def _accumulate(ni, refs, vals):
    @pl.when(ni == 0)
    def _():
        for r in refs:
            r[...] = jnp.zeros(r.shape, F32)

    for r, v in zip(refs, vals):
        r[...] += v


def _dmerged_merge_bwd(dmix, w_out, w_a, w_b, z, y_a, y_b, tm):
    t = dmix.shape[0]
    tm = _pick(t, tm)
    w = 256
    npc = D // w
    ga0, gb0 = (IN_COLS - 2 * D) // w, (IN_COLS - D) // w

    def body(*refs):
        dm_ref, w_ref, wa_ref, wb_ref = refs[:4]
        ga_refs, gb_refs = refs[4:4 + npc], refs[4 + npc:4 + 2 * npc]
        ya_ref, yb_ref, dga_ref, dgb_ref, dya_ref, dyb_ref, doa_ref, dob_ref = refs[4 + 2 * npc:]
        dmerged = _raw_dot(dm_ref[...], w_ref[...], "nt")
        dyas, dybs = [], []
        for p in range(npc):
            cols = slice(p * w, (p + 1) * w)
            xs = [ga_refs[p][...], gb_refs[p][...], ya_ref[:, cols].astype(F32), yb_ref[:, cols].astype(F32)]
            _, vjp = jax.vjp(lambda *a: _f_merge([], list(a), [])[0][0], *xs)
            dga, dgb, dya, dyb = vjp(dmerged[:, cols])
            dga_ref[:, cols] = dga.astype(BF)
            dgb_ref[:, cols] = dgb.astype(BF)
            dyas.append(dya.astype(BF))
            dybs.append(dyb.astype(BF))
        dya, dyb = jnp.concatenate(dyas, axis=1), jnp.concatenate(dybs, axis=1)
        dya_ref[...] = dya
        dyb_ref[...] = dyb
        doa_ref[...] = _raw_dot(dya, wa_ref[...], "nt").astype(BF)
        dob_ref[...] = _raw_dot(dyb, wb_ref[...], "nt").astype(BF)

    row = pl.BlockSpec((tm, D), lambda i: (i, 0))
    wgt = pl.BlockSpec((D, D), lambda i: (0, 0), pipeline_mode=pl.Buffered(1))

    def gate(b0):
        return [pl.BlockSpec((tm, w), functools.partial(lambda i, b: (i, b), b=b0 + p)) for p in range(npc)]

    return pl.pallas_call(
        body, name="merge_bwd", grid=(t // tm,),
        in_specs=[row, wgt, wgt, wgt] + gate(ga0) + gate(gb0) + [row, row],
        out_specs=[row] * 6, out_shape=[jax.ShapeDtypeStruct((t, D), BF)] * 6,
        compiler_params=_cparams(("arbitrary",)),
    )(dmix, w_out, w_a, w_b, *([z] * (2 * npc)), y_a, y_b)


def _dxn2_post1_bwd(pieces, w_up_t, x, mix, dh1, g_post, g_pre2, tm):
    t = x.shape[0]
    tm = _pick(t, tm)
    k = w_up_t.shape[0]
    offs = [sum(p.shape[1] for p in pieces[:i]) for i in range(len(pieces))]
    na = len(pieces)

    def body(*refs):
        w_ref, x_ref, m_ref, dh_ref, gp_ref, g2_ref, dgp_ref, dg2_ref, dx_ref, dm_ref = refs[na:]
        dxn2 = None
        for a_ref, off in zip(refs[:na], offs):
            part = _raw_dot(a_ref[...], w_ref[off:off + a_ref.shape[1], :], "nn")
            dxn2 = part if dxn2 is None else dxn2 + part
        _, vjp = jax.vjp(lambda gp, g2, xx, mm: _f_post1([gp, g2], [xx, mm], [])[0],
                         gp_ref[...], g2_ref[...], x_ref[...], m_ref[...])
        dgp, dg2, dx, dm = vjp([dh_ref[...], dxn2])
        _accumulate(pl.program_id(0), [dgp_ref, dg2_ref], [dgp, dg2])
        dx_ref[...] = dx
        dm_ref[...] = dm.astype(BF)

    row = pl.BlockSpec((tm, D), lambda i: (i, 0))
    one = pl.BlockSpec((1, D), lambda i: (0, 0))
    return pl.pallas_call(
        body, name="post1_bwd", grid=(t // tm,),
        in_specs=[pl.BlockSpec((tm, p.shape[1]), lambda i: (i, 0)) for p in pieces]
        + [pl.BlockSpec((k, D), lambda i: (0, 0), pipeline_mode=pl.Buffered(1)), row, row, row, one, one],
        out_specs=[one, one, row, row],
        out_shape=[jax.ShapeDtypeStruct((1, D), F32), jax.ShapeDtypeStruct((1, D), F32),
                   jax.ShapeDtypeStruct((t, D), F32), jax.ShapeDtypeStruct((t, D), BF)],
        compiler_params=_cparams(("arbitrary",)),
    )(*pieces, w_up_t, x, mix, dh1, g_post, g_pre2)


def _conv_taps(h, cw, cb, p2, p1):
    s1 = _shift_down(h, p1)
    s2 = _shift_down(s1, p2)
    return cb + _row(cw, 0) * s2 + _row(cw, 1) * s1 + _row(cw, 2) * h


def _up_conv(xn2, w_up_t, conv_w, conv_b, tm):
    t = xn2.shape[0]
    tm = _pick(t, tm)
    tn = _pick(DFF, 1408)
    nj = DFF // tn

    def body(x_ref, wg_ref, wv_ref, cwg_ref, cwv_ref, cbg_ref, cbv_ref, hg_ref, hv_ref, act_ref, prev):
        @pl.when(pl.program_id(1) == 0)
        def _():
            prev[...] = jnp.zeros(prev.shape, F32)

        x = x_ref[...]
        hg = _raw_dot(x, wg_ref[...], "nt")
        hv = _raw_dot(x, wv_ref[...], "nt")
        hg_ref[...] = hg
        hv_ref[...] = hv
        pv = prev[...]
        cg = _conv_taps(hg, cwg_ref[...], cbg_ref[...], _row(pv, 0), _row(pv, 1))
        cv = _conv_taps(hv, cwv_ref[...], cbv_ref[...], _row(pv, 2), _row(pv, 3))
        act_ref[...] = (_silu(cg) * cv).astype(BF)
        prev[...] = jnp.concatenate([_row(hg, tm - 2), _row(hg, tm - 1), _row(hv, tm - 2), _row(hv, tm - 1),
                                     jnp.zeros((4, tn), F32)], axis=0)

    def cols(rows, off):
        return pl.BlockSpec((rows, tn), lambda j, i: (0, j + off))

    tile = pl.BlockSpec((tm, tn), lambda j, i: (i, j))
    return pl.pallas_call(
        body, name="up_conv", grid=(nj, t // tm),
        in_specs=[pl.BlockSpec((tm, D), lambda j, i: (i, 0)), pl.BlockSpec((tn, D), lambda j, i: (j, 0)),
                  pl.BlockSpec((tn, D), lambda j, i: (j + nj, 0)), cols(3, 0), cols(3, nj), cols(1, 0), cols(1, nj)],
        out_specs=[tile, tile, tile],
        out_shape=[jax.ShapeDtypeStruct((t, DFF), F32), jax.ShapeDtypeStruct((t, DFF), F32),
                   jax.ShapeDtypeStruct((t, DFF), BF)],
        scratch_shapes=[pltpu.VMEM((8, tn), F32)],
        compiler_params=_cparams(("arbitrary", "arbitrary")),
    )(xn2, w_up_t, w_up_t, conv_w, conv_w, conv_b, conv_b)


def _dxn_pre1_bwd(pieces, w_t, x, dx_res, g, tm, token):
    t = x.shape[0]
    tm = _pick(t, tm)
    offs = [sum(p.shape[1] for p in pieces[:i]) for i in range(len(pieces))]
    na = len(pieces)

    def body(*refs):
        w_ref, x_ref, r_ref, g_ref = refs[na:na + 4]
        dg_ref, dx_ref = refs[-2:]
        dxn = None
        for a_ref, off in zip(refs[:na], offs):
            part = _raw_dot(a_ref[...], w_ref[off:off + a_ref.shape[1], :], "nn")
            dxn = part if dxn is None else dxn + part
        _, vjp = jax.vjp(lambda gg, xx: _f_pre1_residual([gg], [xx], [])[0], g_ref[...], x_ref[...])
        dg, dx = vjp([dxn, r_ref[...]])
        _accumulate(pl.program_id(0), [dg_ref], [dg])
        dx_ref[...] = dx

    row = pl.BlockSpec((tm, D), lambda i: (i, 0))
    one = pl.BlockSpec((1, D), lambda i: (0, 0))
    return pl.pallas_call(
        body, name="pre1_bwd", grid=(t // tm,),
        in_specs=[pl.BlockSpec((tm, p.shape[1]), lambda i: (i, 0)) for p in pieces]
        + [pl.BlockSpec(w_t.shape, lambda i: (0, 0), pipeline_mode=pl.Buffered(1)), row, row, one,
           pl.BlockSpec(token.shape, lambda i: (0, 0))],
        out_specs=[one, row],
        out_shape=[jax.ShapeDtypeStruct((1, D), F32), jax.ShapeDtypeStruct((t, D), F32)],
        compiler_params=_cparams(("arbitrary",)),
    )(*pieces, w_t, x, dx_res, g, token)


def _down_loss(act, w_down, g_post, h1, tgt, tm):
    t, k = act.shape
    tm = _pick(t, tm)

    def body(a_ref, w_ref, g_ref, h_ref, t_ref, loss_ref, dg_ref, dh_ref, df_ref):
        ni = pl.program_id(0)
        ff = _raw_dot(a_ref[...], w_ref[...], "nn")
        target = t_ref[...]

        def lossf(g, h1, ff):
            e = h1 + _rms(ff, g) - target
            return 0.5 * jnp.sum(jnp.mean(e * e, axis=-1))

        l, (dg, dh, df) = jax.value_and_grad(lossf, argnums=(0, 1, 2))(g_ref[...], h_ref[...], ff)

        @pl.when(ni == 0)
        def _():
            loss_ref[...] = jnp.zeros(loss_ref.shape, F32)
            dg_ref[...] = jnp.zeros(dg_ref.shape, F32)

        loss_ref[...] += jnp.full(loss_ref.shape, l, F32)
        dg_ref[...] += dg
        dh_ref[...] = dh
        df_ref[...] = df.astype(df_ref.dtype)

    row = pl.BlockSpec((tm, D), lambda ni: (ni, 0))
    one = pl.BlockSpec((1, D), lambda ni: (0, 0))
    return pl.pallas_call(
        body, name="down_loss", grid=(t // tm,),
        in_specs=[pl.BlockSpec((tm, k), lambda ni: (ni, 0)),
                  pl.BlockSpec((k, D), lambda ni: (0, 0), pipeline_mode=pl.Buffered(1)), one, row, row],
        out_specs=[pl.BlockSpec((1, LANES), lambda ni: (0, 0)), one, row, row],
        out_shape=[jax.ShapeDtypeStruct((1, LANES), F32), jax.ShapeDtypeStruct((1, D), F32),
                   jax.ShapeDtypeStruct((t, D), F32), jax.ShapeDtypeStruct((t, D), BF)],
        compiler_params=_cparams(("arbitrary",)),
    )(act, w_down, g_post, h1, tgt)


_ANY = pl.BlockSpec(memory_space=pl.ANY)


def _all_gather(name, blks):
    na = len(blks)
    ns = 8

    def body(*refs):
        x_refs, out_refs = refs[:na], refs[na:2 * na]
        send_sems, recv_sems, local_sems = refs[2 * na:]
        x, y, cc = lax.axis_index("x"), lax.axis_index("y"), lax.axis_index("c")
        sibling, xn, yn = (x, y, 1 - cc), (1 - x, y, cc), (x, 1 - y, cc)

        def num(px, py, pc):
            return 4 * px + 2 * py + pc

        def copy(a, k, to, src, dst):
            return pltpu.make_async_remote_copy(src_ref=src, dst_ref=dst, send_sem=send_sems.at[ns * a + k],
                                                recv_sem=recv_sems.at[ns * a + k], device_id=to, device_id_type=MESH)

        def halves(a, blk):
            h = blks[a].shape[0] // 2
            return out_refs[a].at[blk, pl.ds(0, h)], out_refs[a].at[blk, pl.ds(h, h)]

        mine, sends = [], []
        for a in range(na):
            o = out_refs[a]
            m = pltpu.make_async_copy(x_refs[a], o.at[num(x, y, cc)], local_sems.at[a])
            m.start()
            mine.append(m)
            own = o.at[num(x, y, cc)]
            sends.append([copy(a, 0, sibling, x_refs[a], own), copy(a, 1, xn, x_refs[a], own),
                          copy(a, 2, yn, x_refs[a], own)])
            for cp in sends[a]:
                cp.start()
        for a in range(na):
            o = out_refs[a]
            bx, by, bd = num(1 - x, y, cc), num(x, 1 - y, cc), num(1 - x, 1 - y, cc)
            copy(a, 1, xn, o.at[bx], o.at[bx]).wait_recv()
            more = [copy(a, 3, yn, halves(a, bx)[0], halves(a, bx)[0]), copy(a, 5, sibling, o.at[bx], o.at[bx])]
            for cp in more:
                cp.start()
            sends[a] += more
        for a in range(na):
            o = out_refs[a]
            bx, by, bd = num(1 - x, y, cc), num(x, 1 - y, cc), num(1 - x, 1 - y, cc)
            copy(a, 2, yn, o.at[by], o.at[by]).wait_recv()
            more = [copy(a, 4, xn, halves(a, by)[1], halves(a, by)[1]), copy(a, 6, sibling, o.at[by], o.at[by])]
            for cp in more:
                cp.start()
            sends[a] += more
        for a in range(na):
            o = out_refs[a]
            bd = num(1 - x, 1 - y, cc)
            copy(a, 3, yn, halves(a, bd)[0], halves(a, bd)[0]).wait_recv()
            copy(a, 4, xn, halves(a, bd)[1], halves(a, bd)[1]).wait_recv()
            fw = copy(a, 7, sibling, o.at[bd], o.at[bd])
            fw.start()
            sends[a].append(fw)
        for a in range(na):
            o = out_refs[a]
            for k, blk in ((0, num(x, y, 1 - cc)), (5, num(1 - x, y, 1 - cc)), (6, num(x, 1 - y, 1 - cc)),
                           (7, num(1 - x, 1 - y, 1 - cc))):
                copy(a, k, sibling, o.at[blk], o.at[blk]).wait_recv()
            for cp in sends[a]:
                cp.wait_send()
        for m in mine:
            m.wait()

    res = pl.pallas_call(
        body, name=name, in_specs=[_ANY] * na, out_specs=[_ANY] * na,
        out_shape=[jax.ShapeDtypeStruct((N_DEV,) + b.shape, b.dtype) for b in blks],
        scratch_shapes=[pltpu.SemaphoreType.DMA((ns * na,)), pltpu.SemaphoreType.DMA((ns * na,)),
                        pltpu.SemaphoreType.DMA((na,))],
    )(*blks)
    return list(res)


def _all_gather_small(name, blk):
    def body(x_ref, out_ref, ssem, rsem, lsem):
        x, y, c = lax.axis_index("x"), lax.axis_index("y"), lax.axis_index("c")
        me = 4 * x + 2 * y + c
        mine = pltpu.make_async_copy(x_ref, out_ref.at[me], lsem)
        mine.start()
        cps = []
        for j in range(1, N_DEV):
            px = 1 - x if j & 4 else x
            py = 1 - y if j & 2 else y
            pc = 1 - c if j & 1 else c
            cps.append(pltpu.make_async_remote_copy(src_ref=x_ref, dst_ref=out_ref.at[me], send_sem=ssem.at[j - 1],
                                                    recv_sem=rsem.at[j - 1], device_id=(px, py, pc),
                                                    device_id_type=MESH))
        for cp in cps:
            cp.start()
        for cp in cps:
            cp.wait()
        mine.wait()

    return pl.pallas_call(
        body, name=name, in_specs=[_ANY], out_specs=_ANY,
        out_shape=jax.ShapeDtypeStruct((N_DEV,) + blk.shape, blk.dtype),
        scratch_shapes=[pltpu.SemaphoreType.DMA((N_DEV - 1,)), pltpu.SemaphoreType.DMA((N_DEV - 1,)),
                        pltpu.SemaphoreType.DMA],
    )(blk)


def _reduce_pair(g8s):
    na = len(g8s)

    def body(*refs):
        g_refs, recv_refs = refs[:na], refs[na:2 * na]
        ssem, rsem = refs[2 * na:]
        x, y, cc = lax.axis_index("x"), lax.axis_index("y"), lax.axis_index("c")
        chips = [(x, y), (1 - x, y), (x, 1 - y), (1 - x, 1 - y)]
        sib = (x, y, 1 - cc)
        for a in range(na):
            for k, (cx, cy) in enumerate(chips):
                pltpu.make_async_remote_copy(
                    src_ref=g_refs[a].at[4 * cx + 2 * cy + 1 - cc], dst_ref=recv_refs[a].at[k],
                    send_sem=ssem.at[a], recv_sem=rsem.at[a], device_id=sib, device_id_type=MESH).start()
        for a in range(na):
            pltpu.make_async_remote_copy(src_ref=recv_refs[a], dst_ref=recv_refs[a], send_sem=ssem.at[a],
                                         recv_sem=rsem.at[a], device_id=sib, device_id_type=MESH).wait()

    res = pl.pallas_call(
        body, name="reduce_pair", in_specs=[_ANY] * na, out_specs=[_ANY] * na,
        out_shape=[jax.ShapeDtypeStruct((4,) + g.shape[1:], g.dtype) for g in g8s],
        scratch_shapes=[pltpu.SemaphoreType.DMA((na,)), pltpu.SemaphoreType.DMA((na,))],
    )(*g8s)
    return list(res)


_HBM = pl.BlockSpec(memory_space=pltpu.HBM)
_SEM = pl.BlockSpec(memory_space=pltpu.SEMAPHORE)
_EFFECT = pltpu.SideEffectType.DATAFLOW_SIDE_EFFECTING


def _chip_swap_copies(s_refs, land_refs, ssem, rsem):
    x, y, c = lax.axis_index("x"), lax.axis_index("y"), lax.axis_index("c")
    targets = [(1 - x, y, c), (x, 1 - y, c), (1 - x, 1 - y, c)]
    return [pltpu.make_async_remote_copy(src_ref=s.at[k], dst_ref=d.at[k], send_sem=ssem.at[3 * a + k],
                                         recv_sem=rsem.at[3 * a + k], device_id=targets[k], device_id_type=MESH)
            for a, (s, d) in enumerate(zip(s_refs, land_refs)) for k in range(3)]


def _chip_swap_start(sends):
    na = len(sends)

    def body(*refs):
        cps = _chip_swap_copies(refs[:na], refs[na:2 * na], refs[2 * na], refs[2 * na + 1])
        for cp in cps:
            cp.start()
        token = refs[-1]
        token[...] = jnp.zeros(token.shape, token.dtype)

    bufs = [pltpu.HBM(s.shape, s.dtype) for s in sends]
    res = pl.pallas_call(
        body, name="chip_swap_start",
        out_shape=[pltpu.SemaphoreType.DMA((3 * na,)), pltpu.SemaphoreType.DMA((3 * na,))] + bufs + bufs
        + [jax.ShapeDtypeStruct((8, LANES), F32)],
        in_specs=[_HBM] * (2 * na), out_specs=[_SEM, _SEM] + [_HBM] * (2 * na) + [pl.BlockSpec(memory_space=pltpu.VMEM)],
        input_output_aliases={i: 2 + i for i in range(2 * na)},
        compiler_params=pltpu.CompilerParams(has_side_effects=_EFFECT),
    )(*[pltpu.with_memory_space_constraint(s, pltpu.HBM) for s in sends],
      *[pltpu.with_memory_space_constraint(lax.empty(s.shape, s.dtype), pltpu.HBM) for s in sends])
    return res[0], res[1], list(res[2:2 + na]), list(res[2 + na:2 + 2 * na]), res[-1]


def _chip_swap_wait(ssem, rsem, srcs, lands, after):
    na = len(srcs)

    def body(*refs):
        cps = _chip_swap_copies(refs[:na], refs[na:2 * na], refs[2 * na], refs[2 * na + 1])
        for cp in cps:
            cp.wait_send()
            cp.wait_recv()

    bufs = [pltpu.HBM(s.shape, s.dtype) for s in srcs]
    res = pl.pallas_call(
        body, name="chip_swap_wait", out_shape=bufs + bufs,
        in_specs=[_HBM] * (2 * na) + [_SEM, _SEM, _ANY], out_specs=[_HBM] * (2 * na),
        input_output_aliases={i: i for i in range(2 * na)},
        compiler_params=pltpu.CompilerParams(has_side_effects=_EFFECT),
    )(*srcs, *lands, ssem, rsem, after)
    return list(res[na:])


def _pick_rows(r, c, budget=TILE_BYTES):
    if r * c * 4 <= budget or r % 16:
        return r
    best = 16
    for tr in range(16, r, 16):
        if r % tr == 0 and tr * c * 4 <= budget:
            best = tr
    return best


def _pair_sum(name, idx4, g8, recv4):
    _, r, c = g8.shape
    tr = _pick_rows(r, c, 2 * TILE_BYTES)

    def body(idx_ref, a_ref, b_ref, o0_ref, o3_ref):
        k = pl.program_id(1)
        s = a_ref[...].astype(F32) + b_ref[...].astype(F32)

        @pl.when(k == 0)
        def _():
            o0_ref[...] = s

        @pl.when(k > 0)
        def _():
            o3_ref[...] = s.astype(BF)

    spec = pltpu.PrefetchScalarGridSpec(
        num_scalar_prefetch=1, grid=(r // tr, 4),
        in_specs=[pl.BlockSpec((None, tr, c), lambda i, k, idx: (idx[k], i, 0)),
                  pl.BlockSpec((None, tr, c), lambda i, k, idx: (k, i, 0))],
        out_specs=[pl.BlockSpec((tr, c), lambda i, k, idx: (i, 0)),
                   pl.BlockSpec((None, tr, c), lambda i, k, idx: (jnp.maximum(k - 1, 0), i, 0))])
    return pl.pallas_call(
        body, name=name, grid_spec=spec,
        out_shape=[jax.ShapeDtypeStruct((r, c), F32), jax.ShapeDtypeStruct((3, r, c), BF)],
        compiler_params=_cparams(("arbitrary", "arbitrary")),
    )(idx4, g8, recv4)


def _adamw(w, g, m, v):
    m = ADAM_B1 * m + (1.0 - ADAM_B1) * g
    v = ADAM_B2 * v + (1.0 - ADAM_B2) * jnp.square(g)
    m_hat = m / (1.0 - ADAM_B1 ** ADAM_STEP)
    v_hat = v / (1.0 - ADAM_B2 ** ADAM_STEP)
    delta = -ADAM_LR * (m_hat / (jnp.sqrt(v_hat) + ADAM_EPS) + ADAM_WD * w)
    return delta, m, v


def _adam_sharded(name, idx1, own, recv, w, m, v, after=None):
    r, c = w.shape
    tr = _pick_rows(r, c, 2 * TILE_BYTES)
    nj = recv.shape[0]
    extra = [] if after is None else [after]

    def body(idx_ref, p_ref, r_ref, w_ref, m_ref, v_ref, *rest):
        g_out, d_out, m_out, v_out = rest[-4:]
        g = p_ref[...].astype(F32)
        for k in range(nj):
            g = g + r_ref[k].astype(F32)
        d, mn, vn = _adamw(w_ref[...], g, m_ref[...], v_ref[...])
        g_out[...] = g
        d_out[...] = d
        m_out[...] = mn
        v_out[...] = vn

    row = pl.BlockSpec((tr, c), lambda i, idx: (i, 0))
    spec = pltpu.PrefetchScalarGridSpec(
        num_scalar_prefetch=1, grid=(r // tr,),
        in_specs=[pl.BlockSpec((None, tr, c), lambda i, idx: (idx[0], i, 0)),
                  pl.BlockSpec((nj, tr, c), lambda i, idx: (0, i, 0)), row, row, row]
        + [pl.BlockSpec(e.shape, lambda i, idx: (0, 0)) for e in extra],
        out_specs=[row] * 4)
    return pl.pallas_call(
        body, name=name, grid_spec=spec, out_shape=[jax.ShapeDtypeStruct((r, c), F32)] * 4,
        compiler_params=_cparams(("arbitrary",)),
    )(idx1, own, recv, w, m, v, *extra)


def _repl_rows():
    rows, r = {}, 0
    for name, cols in REPL:
        rows[name] = r
        r += REPL_ROWS.get(name, 1) * ((cols + D - 1) // D)
    return rows


LOSS_ROW = 24


def _pack_replicated(grads, loss_acc, after):
    rows = _repl_rows()
    names = [n for n, _ in REPL]

    def body(*refs):
        o_ref = refs[-1]
        o_ref[...] = jnp.zeros(o_ref.shape, F32)
        o_ref[LOSS_ROW:LOSS_ROW + 1, 0:LANES] = refs[len(names)][...]
        for name, ref in zip(names, refs[:len(names)]):
            r0 = rows[name]
            nr, nc = ref.shape
            if nc <= D:
                o_ref[r0:r0 + nr, 0:nc] = ref[...]
            else:
                for j in range((nc + D - 1) // D):
                    lo, hi = j * D, min(nc, (j + 1) * D)
                    o_ref[r0 + j:r0 + j + 1, 0:hi - lo] = ref[:, lo:hi]

    return pl.pallas_call(body, name="pack_replicated", out_shape=jax.ShapeDtypeStruct((REPL_TOTAL, D), F32),
                          in_specs=[pl.BlockSpec(memory_space=pltpu.VMEM)] * (len(names) + 1) + [_ANY] * len(after),
                          compiler_params=_cparams())(*[grads[n] for n in names], loss_acc, *after)


def _adam_replicated(g8, ws, ms, vs):
    rows = _repl_rows()
    names = [n for n, _ in REPL]
    np_ = len(names)

    def body(*refs):
        g_ref = refs[0]
        w_refs, m_refs, v_refs = refs[1:1 + np_], refs[1 + np_:1 + 2 * np_], refs[1 + 2 * np_:1 + 3 * np_]
        outs = refs[1 + 3 * np_:1 + 7 * np_]
        scr = refs[-1]
        g = g_ref[0]
        for k in range(1, N_DEV):
            g = g + g_ref[k]
        scr[...] = g
        refs[1 + 7 * np_][...] = scr[LOSS_ROW:LOSS_ROW + 1, 0:LANES]
        for i, name in enumerate(names):
            r0 = rows[name]
            nr, nc = w_refs[i].shape
            if nc <= D:
                gi = scr[r0:r0 + nr, 0:nc]
            else:
                parts = []
                for j in range((nc + D - 1) // D):
                    lo, hi = j * D, min(nc, (j + 1) * D)
                    parts.append(scr[r0 + j:r0 + j + 1, 0:hi - lo])
                gi = jnp.concatenate(parts, axis=1)
            d, mn, vn = _adamw(w_refs[i][...], gi, m_refs[i][...], v_refs[i][...])
            outs[i][...] = gi
            outs[np_ + i][...] = d
            outs[2 * np_ + i][...] = mn
            outs[3 * np_ + i][...] = vn

    shp = [jax.ShapeDtypeStruct(w.shape, F32) for w in ws]
    res = pl.pallas_call(body, name="adam_replicated", out_shape=shp * 4 + [jax.ShapeDtypeStruct((1, LANES), F32)],
                         scratch_shapes=[pltpu.VMEM((REPL_TOTAL, D), F32)], compiler_params=_cparams(),
                         )(g8, *ws, *ms, *vs)
    return [dict(zip(names, res[k * np_:(k + 1) * np_])) for k in range(4)], res[-1]


_WEIGHTS = ("attn_pre_norm", "w_in", "hgrn_lb", "hgrn_gnorm", "w_branch_a", "rwkv_mu", "rwkv_w0", "rwkv_w2",
            "rwkv_a0", "rwkv_a2", "rwkv_g2", "rwkv_k_k", "rwkv_k_a", "rwkv_r_k", "rwkv_ln_w", "rwkv_ln_b",
            "w_branch_b", "w_out", "attn_post_norm", "ffn_pre_norm", "w_up", "conv_w", "conv_b", "w_down",
            "ffn_post_norm")
_BIG = ("w_in", "w_up", "w_down", "w_branch_a", "w_branch_b", "w_out")


def _stages():
    one = [D]
    hw = HG_K * HG_PER_STEP
    rw = LANES * RW_PAIRS_PER_STEP
    return dict(
        mixers=_Stage("mixers", _f_mixers, 1, 2 * RW_CHUNK, [False] * 13, [[D] * 7 + [LANES, LANES]], [0],
                      [(hw, HG_K), (1, RW_COLS), (rw, LANES)], [one, one], [BF, BF],
                      kept_shapes=[(2 * RW_KEPT * RW_PAIRS_PER_STEP * 2 * RW_CHUNK, LANES)], f_kept=_f_mixers_kept),
        conv=_Stage("conv", _f_conv, 1, 128, [False, False], [[DFF], [DFF]], [0, 0], [(1, 2 * DFF), (1, 2 * DFF)],
                    [[DFF]], [BF]),
    )


def _cols_to_blocks(w, per):
    return w.reshape(w.shape[0], N_DEV, per).transpose(1, 0, 2)


def _blocks_to_cols(g):
    return g.transpose(1, 0, 2).reshape(g.shape[1], N_DEV * g.shape[2])


def kernel(x, attn_pre_norm, w_in, hgrn_lb, hgrn_gnorm, w_branch_a, rwkv_mu, rwkv_w0, rwkv_w2, rwkv_a0, rwkv_a2, rwkv_g2, rwkv_k_k, rwkv_k_a, rwkv_r_k, rwkv_ln_w, rwkv_ln_b, w_branch_b, w_out, attn_post_norm, ffn_pre_norm, w_up, conv_w, conv_b, w_down, ffn_post_norm, loss_target, m_attn_pre_norm, m_w_in, m_hgrn_lb, m_hgrn_gnorm, m_w_branch_a, m_rwkv_mu, m_rwkv_w0, m_rwkv_w2, m_rwkv_a0, m_rwkv_a2, m_rwkv_g2, m_rwkv_k_k, m_rwkv_k_a, m_rwkv_r_k, m_rwkv_ln_w, m_rwkv_ln_b, m_w_branch_b, m_w_out, m_attn_post_norm, m_ffn_pre_norm, m_w_up, m_conv_w, m_conv_b, m_w_down, m_ffn_post_norm, v_attn_pre_norm, v_w_in, v_hgrn_lb, v_hgrn_gnorm, v_w_branch_a, v_rwkv_mu, v_rwkv_w0, v_rwkv_w2, v_rwkv_a0, v_rwkv_a2, v_rwkv_g2, v_rwkv_k_k, v_rwkv_k_a, v_rwkv_r_k, v_rwkv_ln_w, v_rwkv_ln_b, v_w_branch_b, v_w_out, v_attn_post_norm, v_ffn_pre_norm, v_w_up, v_conv_w, v_conv_b, v_w_down, v_ffn_post_norm):
    w = dict(attn_pre_norm=attn_pre_norm, w_in=w_in, hgrn_lb=hgrn_lb, hgrn_gnorm=hgrn_gnorm, w_branch_a=w_branch_a, rwkv_mu=rwkv_mu, rwkv_w0=rwkv_w0, rwkv_w2=rwkv_w2, rwkv_a0=rwkv_a0, rwkv_a2=rwkv_a2, rwkv_g2=rwkv_g2, rwkv_k_k=rwkv_k_k, rwkv_k_a=rwkv_k_a, rwkv_r_k=rwkv_r_k, rwkv_ln_w=rwkv_ln_w, rwkv_ln_b=rwkv_ln_b, w_branch_b=w_branch_b, w_out=w_out, attn_post_norm=attn_post_norm, ffn_pre_norm=ffn_pre_norm, w_up=w_up, conv_w=conv_w, conv_b=conv_b, w_down=w_down, ffn_post_norm=ffn_post_norm)
    mo = dict(attn_pre_norm=m_attn_pre_norm, w_in=m_w_in, hgrn_lb=m_hgrn_lb, hgrn_gnorm=m_hgrn_gnorm, w_branch_a=m_w_branch_a, rwkv_mu=m_rwkv_mu, rwkv_w0=m_rwkv_w0, rwkv_w2=m_rwkv_w2, rwkv_a0=m_rwkv_a0, rwkv_a2=m_rwkv_a2, rwkv_g2=m_rwkv_g2, rwkv_k_k=m_rwkv_k_k, rwkv_k_a=m_rwkv_k_a, rwkv_r_k=m_rwkv_r_k, rwkv_ln_w=m_rwkv_ln_w, rwkv_ln_b=m_rwkv_ln_b, w_branch_b=m_w_branch_b, w_out=m_w_out, attn_post_norm=m_attn_post_norm, ffn_pre_norm=m_ffn_pre_norm, w_up=m_w_up, conv_w=m_conv_w, conv_b=m_conv_b, w_down=m_w_down, ffn_post_norm=m_ffn_post_norm)
    vo = dict(attn_pre_norm=v_attn_pre_norm, w_in=v_w_in, hgrn_lb=v_hgrn_lb, hgrn_gnorm=v_hgrn_gnorm, w_branch_a=v_w_branch_a, rwkv_mu=v_rwkv_mu, rwkv_w0=v_rwkv_w0, rwkv_w2=v_rwkv_w2, rwkv_a0=v_rwkv_a0, rwkv_a2=v_rwkv_a2, rwkv_g2=v_rwkv_g2, rwkv_k_k=v_rwkv_k_k, rwkv_k_a=v_rwkv_k_a, rwkv_r_k=v_rwkv_r_k, rwkv_ln_w=v_rwkv_ln_w, rwkv_ln_b=v_rwkv_ln_b, w_branch_b=v_w_branch_b, w_out=v_w_out, attn_post_norm=v_attn_post_norm, ffn_pre_norm=v_ffn_pre_norm, w_up=v_w_up, conv_w=v_conv_w, conv_b=v_conv_b, w_down=v_w_down, ffn_post_norm=v_ffn_post_norm)

    t = x.shape[1]
    x2 = x.reshape(t, D)
    tgt = loss_target.reshape(t, D)
    st = _stages()

    me = 4 * lax.axis_index("x") + 2 * lax.axis_index("y") + lax.axis_index("c")
    small = jnp.concatenate([rwkv_w2[0], rwkv_a2[0], rwkv_g2[0]], axis=0).astype(BF)
    g_in, g_small = _all_gather("gather_weights", [w_in[0].T.astype(BF), small])
    fw_in_t = g_in.reshape(IN_COLS, D)
    z64 = jnp.zeros((64, D), BF)
    w2p = jnp.concatenate([_blocks_to_cols(g_small[:, 0:64]), z64], axis=0)
    a2p = jnp.concatenate([z64, _blocks_to_cols(g_small[:, 64:128])], axis=0)
    g2f = _blocks_to_cols(g_small[:, 128:256])
    conv_bits = jnp.pad(lax.bitcast_convert_type(conv_w[0], BF).reshape(3, 2 * 704), ((0, 29), (0, 0)))
    late = [w_up[0].T.astype(BF)] + [w[k][0].astype(BF) for k in _BIG[2:]] + [conv_bits]
    late_gather = _Exchange("gather2", late)
    r_k = rwkv_r_k.reshape(1, D)

    xn, z = _norm_in_proj(x2, attn_pre_norm, fw_in_t, 512, 4736)
    mix_par = [hgrn_lb, hgrn_gnorm, rwkv_mu, rwkv_w0, w2p, rwkv_a0, a2p, g2f, rwkv_k_k, rwkv_k_a,
               rwkv_ln_w, rwkv_ln_b, r_k]
    mix_in = [z]
    (o_a, o_b), mix_saved = _stage_fwd(st["mixers"], t, mix_par, mix_in, hook=late_gather)
    gl = [lax.dynamic_update_slice(g, own[None], (me, 0, 0)) for g, own in zip(late_gather.results, late)]
    fw_up_t = gl[0].reshape(2 * DFF, D)
    fw_down = gl[1].reshape(DFF, D)
    fw_a, fw_b, fw_out = (g.reshape(D, D) for g in gl[2:5])
    conv_full = _blocks_to_cols(lax.bitcast_convert_type(gl[5][:, :3].reshape(N_DEV, 3, 704, 2), F32))
    y_a, y_b, merged, mix, h1, xn2 = _merge_out_post(z, o_a, o_b, fw_a, fw_b, fw_out, x2, attn_post_norm,
                                                     ffn_pre_norm, 512)
    conv_par = [conv_full, conv_b]
    hu_g, hu_v, act = _up_conv(xn2, fw_up_t, conv_full, conv_b, 512)
    conv_tm = min(st["conv"].tm, t)

    def rows_before(k):
        sel = jnp.concatenate([hu_g[conv_tm - k::conv_tm], hu_v[conv_tm - k::conv_tm]], axis=1)
        return jnp.concatenate([jnp.zeros((1, 2 * DFF), F32), sel[:-1]], axis=0).reshape(1, -1, 1, 2 * DFF)

    conv_saved = [rows_before(1), rows_before(2)]

    loss_acc, d_ffn_post, dh1, dff = _down_loss(act, fw_down, ffn_post_norm, h1, tgt, 512)
    dact = _mm("d_act", dff, fw_down, "nt", BF, tm=1024, tn=1408)
    dw_down = _mm("dw_down", act, dff, "tn", BF, tm=1408, tn=512)
    (dcw, dcb), dhu = _stage_bwd(st["conv"], t, conv_par, [hu_g, hu_v], conv_saved, [[dact]], [BF, BF])
    dw_up_t = _mm_cols_tn("dw_up", dhu, xn2, BF, 1408)
    d_post, d_pre2, dx_a, dmix = _dxn2_post1_bwd(dhu, fw_up_t, x2, mix, dh1, attn_post_norm, ffn_pre_norm, 512)
    dga, dgb, dy_a, dy_b, do_a, do_b = _dmerged_merge_bwd(dmix, fw_out, fw_a, fw_b, z, y_a, y_b, 512)
    dw_a, dw_b, dw_out = _mm_multi("dw_branches", [(o_a, dy_a), (o_b, dy_b), (merged, dmix)], "tn", BF)
    early = [dw_up_t.reshape(N_DEV, 704, D), dw_down.reshape(N_DEV, 352, D), dw_a.reshape(N_DEV, 128, D),
             dw_b.reshape(N_DEV, 128, D), dw_out.reshape(N_DEV, 128, D), _cols_to_blocks(dcw.astype(BF), 704)]
    early_scatter = _Exchange("scatter", early)
    mix_dp, dz_hr = _stage_bwd(st["mixers"], t, mix_par, mix_in, mix_saved, [[do_a], [do_b]], [BF],
                               hook=early_scatter)
    d_lb, d_gn, d_mu, d_w0, d_w2p, d_a0, d_a2p, d_g2, d_kk, d_ka, d_lnw, d_lnb, d_rk = mix_dp
    dz = dz_hr + [dga, dgb]
    dw_in_t = _mm_cols_tn("dw_in", dz, xn, BF, 256)

    ax, ay, ac = lax.axis_index("x"), lax.axis_index("y"), lax.axis_index("c")
    idx4 = jnp.stack([4 * cx + 2 * cy + ac for cx, cy in ((ax, ay), (1 - ax, ay), (ax, 1 - ay), (1 - ax, 1 - ay))])
    idx4 = idx4.astype(jnp.int32)
    idx_me, idx_0 = idx4[0:1], jnp.zeros((1,), jnp.int32)
    d_small = jnp.concatenate([d_w2p[:64], d_a2p[64:], d_g2], axis=0).astype(BF)
    g8s = [dw_in_t.reshape(N_DEV, 1184, D), _cols_to_blocks(d_small, LANES)]
    recv4s = _reduce_pair(g8s)
    sums = [_pair_sum("pair_sum_" + n, idx4, g, r) for n, g, r in zip(("w_in", "small"), g8s, recv4s)]
    swap_ssem, swap_rsem, swap_srcs, swap_lands, token = _chip_swap_start([s[1] for s in sums])
    d_pre1, dx = _dxn_pre1_bwd(dz, fw_in_t, x2, dx_a, attn_pre_norm, 256, token)
    grad_x = dx.reshape(x.shape)

    sh_out = [dict() for _ in range(4)]
    done = []
    for n, own, recv in zip(_BIG[1:] + ("conv_w",), early, early_scatter.results):
        tr = (lambda a: a.T) if n == "w_up" else (lambda a: a)
        res = _adam_sharded("adam_" + n, idx_me, own, recv, *[tr(src[n][0]) for src in (w, mo, vo)], after=token)
        done.append(res[0])
        for kind in range(4):
            sh_out[kind][n] = tr(res[kind])[None]

    rg = dict(attn_pre_norm=d_pre1, hgrn_lb=d_lb, hgrn_gnorm=d_gn, rwkv_mu=d_mu, rwkv_w0=d_w0, rwkv_a0=d_a0,
              rwkv_k_k=d_kk, rwkv_k_a=d_ka, rwkv_r_k=d_rk, rwkv_ln_w=d_lnw, rwkv_ln_b=d_lnb, attn_post_norm=d_post,
              ffn_pre_norm=d_pre2, conv_b=dcb, ffn_post_norm=d_ffn_post)
    g8 = _all_gather_small("gather_small_grads", _pack_replicated(rg, loss_acc, done))
    rnames = [n for n, _ in REPL]
    flat = lambda src: [src[n].reshape(1, D) if n == "rwkv_r_k" else src[n] for n in rnames]
    rp_out, loss_row = _adam_replicated(g8, flat(w), flat(mo), flat(vo))
    loss = loss_row[0, 0]
    recv3s = _chip_swap_wait(swap_ssem, swap_rsem, swap_srcs, swap_lands, rp_out[0]["attn_pre_norm"])
    for kind in range(4):
        rp_out[kind]["rwkv_r_k"] = rp_out[kind]["rwkv_r_k"].reshape(rwkv_r_k.shape)

    def small_of(src):
        return jnp.concatenate([src["rwkv_w2"][0], src["rwkv_a2"][0], src["rwkv_g2"][0]], axis=0)

    res = _adam_sharded("adam_w_in", idx_0, sums[0][0][None], recv3s[0], *[src["w_in"][0].T for src in (w, mo, vo)])
    res_s = _adam_sharded("adam_small", idx_0, sums[1][0][None], recv3s[1], *[small_of(src) for src in (w, mo, vo)])
    for kind in range(4):
        sh_out[kind]["w_in"] = res[kind].T[None]
        sh_out[kind]["rwkv_w2"] = res_s[kind][0:64][None]
        sh_out[kind]["rwkv_a2"] = res_s[kind][64:128][None]
        sh_out[kind]["rwkv_g2"] = res_s[kind][128:256][None]

    outs = [loss, grad_x]
    for kind in range(4):
        for name in _WEIGHTS:
            outs.append(sh_out[kind][name] if name in sh_out[kind] else rp_out[kind][name])
    return tuple(outs)
```

```python
import functools

import jax
import jax.numpy as jnp
from jax import lax
from jax.experimental import pallas as pl
from jax.experimental.pallas import tpu as pltpu

F32 = jnp.float32
BF = jnp.bfloat16
MESH = pl.DeviceIdType.MESH

D = 1024
HG_HEADS = 8
HG_K = 128
HG_CHUNK = 32
HG_SCALE = HG_K ** -0.5
HG_PER_STEP = 8
RW_HEADS = 16
RW_N = 64
RW_CHUNK = 64
RW_PAIRS_PER_STEP = 8
DFF = 2816
IN_COLS = 9472
RW_COLS = 3328
EPS = 1e-6
GN_EPS = 1e-5 * RW_N
ADAM_LR = 0.001
ADAM_B1 = 0.9
ADAM_B2 = 0.999
ADAM_EPS = 1e-08
ADAM_WD = 0.01
ADAM_STEP = 10
N_DEV = 8
LANES = 128
SUBLANES = 8
VMEM_LIMIT = 56 * 1024 * 1024
TILE_BYTES = 1280 * 1024

REPL = (("attn_pre_norm", 1024), ("hgrn_lb", 1024), ("hgrn_gnorm", 1024), ("rwkv_mu", 3328), ("rwkv_w0", 1024),
        ("rwkv_a0", 1024), ("rwkv_k_k", 1024), ("rwkv_k_a", 1024), ("rwkv_r_k", 1024), ("rwkv_ln_w", 1024),
        ("rwkv_ln_b", 1024), ("attn_post_norm", 1024), ("ffn_pre_norm", 1024), ("conv_b", 5632), ("ffn_post_norm", 1024))
REPL_ROWS = {"hgrn_lb": 2}
REPL_TOTAL = 32


def _cparams(sem=None, **kw):
    return pltpu.CompilerParams(dimension_semantics=sem, vmem_limit_bytes=VMEM_LIMIT, **kw)


_DN = {"nn": ((1,), (0,)), "nt": ((1,), (1,)), "tn": ((0,), (0,))}


def _raw_dot(a, b, mode):
    return lax.dot_general(a.astype(BF), b.astype(BF), (_DN[mode], ((), ())), preferred_element_type=F32)


@functools.partial(jax.custom_vjp, nondiff_argnums=(2,))
def _dot(a, b, mode):
    return _raw_dot(a, b, mode)


def _dot_fwd(a, b, mode):
    return _raw_dot(a, b, mode), (a, b)


def _dot_bwd(mode, res, g):
    a, b = res
    if mode == "nn":
        return _dot(g, b, "nt"), _dot(a, g, "tn")
    if mode == "nt":
        return _dot(g, b, "nn"), _dot(g, a, "tn")
    return _dot(b, g, "nt"), _dot(a, g, "nn")


_dot.defvjp(_dot_fwd, _dot_bwd)


def _bf_pieces(x, n):
    out, r = [], x
    for i in range(n):
        p = r.astype(BF)
        out.append(p)
        if i + 1 < n:
            r = r - p.astype(F32)
    return out


def _raw_split_dot(x, e, mode, n, x_left):
    eb = e.astype(BF)
    acc = None
    for p in _bf_pieces(x, n):
        ops = (p, eb) if x_left else (eb, p)
        t = lax.dot_general(*ops, (_DN[mode], ((), ())), preferred_element_type=F32)
        acc = t if acc is None else acc + t
    return acc


def _raw_headsum(x):
    t = x.shape[0]
    i = lax.broadcasted_iota(jnp.int32, (LANES, LANES), 0)
    j = lax.broadcasted_iota(jnp.int32, (LANES, LANES), 1)
    same = jnp.where((i >= RW_N) == (j >= RW_N), 1.0, 0.0).astype(F32)
    groups = x.shape[1] // LANES
    rows = jnp.concatenate([x[:, q * LANES:(q + 1) * LANES] for q in range(groups)], axis=0)
    s = _raw_split_dot(rows, same, "nn", 2, True)
    return jnp.concatenate([s[q * t:(q + 1) * t] for q in range(groups)], axis=1)


@jax.custom_vjp
def _headsum(x):
    return _raw_headsum(x)


def _headsum_fwd(x):
    return _raw_headsum(x), None


def _headsum_bwd(_, g):
    return (_raw_headsum(g),)


_headsum.defvjp(_headsum_fwd, _headsum_bwd)


@functools.partial(jax.custom_vjp, nondiff_argnums=(2,))
def _tdot(tri, x, n):
    return _raw_split_dot(x, tri, "nn", n, False)


def _tdot_fwd(tri, x, n):
    return _raw_split_dot(x, tri, "nn", n, False), tri


def _tdot_bwd(n, tri, g):
    return jnp.zeros_like(tri), _raw_split_dot(g, tri, "tn", n, False)


_tdot.defvjp(_tdot_fwd, _tdot_bwd)


def _row(x, i):
    r = lax.broadcasted_iota(jnp.int32, x.shape, 0)
    return jnp.sum(jnp.where(r == i, x, 0.0), axis=0, keepdims=True)


def _shift_down(x, prev):
    t = x.shape[0]

    @jax.custom_vjp
    def sh(x, prev):
        r = lax.broadcasted_iota(jnp.int32, x.shape, 0)
        return jnp.where(r == 0, prev, pltpu.roll(x, 1, 0))

    def fwd(x, prev):
        return sh(x, prev), None

    def bwd(_, g):
        r = lax.broadcasted_iota(jnp.int32, g.shape, 0)
        dx = jnp.where(r == t - 1, 0.0, pltpu.roll(g, t - 1, 0))
        return dx, jnp.sum(jnp.where(r == 0, g, 0.0), axis=0, keepdims=True)

    sh.defvjp(fwd, bwd)
    return sh(x, prev)


def _sigmoid(x):
    return jax.nn.sigmoid(x)


def _silu(x):
    return x * jax.nn.sigmoid(x)


def _softplus(x):
    return jnp.maximum(x, 0.0) + jnp.log(1.0 + jnp.exp(-jnp.abs(x)))


def _rms(x, g):
    return (x * lax.rsqrt(jnp.mean(x * x, axis=-1, keepdims=True) + EPS)) * g


def _tril(c):
    r = lax.broadcasted_iota(jnp.int32, (c, c), 0)
    cc = lax.broadcasted_iota(jnp.int32, (c, c), 1)
    return cc <= r


def _f_pre1_residual(ps, xs, cs):
    return [_rms(xs[0], ps[0]), xs[0]], []


def _f_hgrn(ps, xs, cs):
    lbraw, gn = ps
    hq, hf, hi, hg = xs
    hd = range(HG_PER_STEP)
    st = [cs[0][p * HG_K:(p + 1) * HG_K] for p in hd]
    l0, l1 = _row(lbraw, 0), _row(lbraw, 1)
    m = jnp.maximum(l0, l1)
    e0, e1 = jnp.exp(l0 - m), jnp.exp(l1 - m)
    lb = e0 / (e0 + e1)
    q = _silu(hq) * HG_SCALE
    f = lb + (1.0 - lb) * _sigmoid(hf)
    kh = 1.0 - f
    gl = jnp.log(f)
    c = HG_CHUNK
    low = _tril(c)
    tri = jnp.where(low, 1.0, 0.0).astype(F32)
    outs = []
    for i in range(hq.shape[0] // c):
        rows = slice(i * c, (i + 1) * c)
        b = _tdot(tri, gl[rows], 3)
        bref = _row(b, c // 2 - 1)
        blast = _row(b, c - 1)
        qi = q[rows] * jnp.exp(b - bref)
        ki = kh[rows] * jnp.exp(bref - b)
        qd = q[rows] * jnp.exp(b)
        kd = kh[rows] * jnp.exp(blast - b)
        dec = jnp.exp(blast)
        sl = [slice(p * HG_K, (p + 1) * HG_K) for p in hd]
        sc = [jnp.where(low, _dot(qi[:, sl[p]], ki[:, sl[p]], "nt"), 0.0) for p in hd]
        o = [_dot(sc[p], hi[rows, sl[p]], "nn") + _dot(qd[:, sl[p]], st[p], "nt") for p in hd]
        u = [_dot(hi[rows, sl[p]], kd[:, sl[p]], "tn") for p in hd]
        st = [dec[:, sl[p]] * st[p] + u[p] for p in hd]
        outs.append(jnp.concatenate(o, axis=1) if len(o) > 1 else o[0])
    o = outs[0] if len(outs) == 1 else jnp.concatenate(outs, axis=0)
    on = []
    for p in hd:
        op = o[:, p * HG_K:(p + 1) * HG_K]
        on.append(op * lax.rsqrt(jnp.mean(op * op, axis=-1, keepdims=True) + EPS))
    o = jnp.concatenate(on, axis=1) if len(on) > 1 else on[0]
    o = o * gn
    return [o * _silu(hg)], [jnp.concatenate(st, axis=0) if len(st) > 1 else st[0]]


_RW_OFFS = (0, 1024, 2048, 3072, 3200, 3328)


def _f_rwpre(ps, xs, cs):
    mu, w0, w2p, a0, a2p, g2, k_k, k_a = ps
    (prev,) = cs
    t = xs[0].shape[0]
    zs = []
    for i, z in enumerate(xs):
        lo, hi = _RW_OFFS[i], _RW_OFFS[i + 1]
        zs.append(z + mu[:, lo:hi] * (_shift_down(z, prev[:, lo:hi]) - z))
    rr, kr, vr, wa, gz = zs
    w_log = -_softplus(-(w0 + _dot(jnp.tanh(wa), w2p, "nn"))) - 0.5
    lw = -jnp.exp(w_log)
    a = _sigmoid(a0 + _dot(wa, a2p, "nn"))
    g = _dot(_sigmoid(gz), g2, "nn")
    kkr = kr * k_k
    kk = kkr / jnp.maximum(jnp.sqrt(_headsum(kkr * kkr)), 1e-12)
    k2 = kr * (1.0 + (a - 1.0) * k_a)
    newprev = jnp.concatenate([_row(z, t - 1) for z in xs], axis=1)
    return [rr, lw, k2, vr, -kk, kk * a, g], [newprev]


def _raw_inverses(ls):
    n = ls[0].shape[0]
    r = lax.broadcasted_iota(jnp.int32, (n, n), 0)
    c = lax.broadcasted_iota(jnp.int32, (n, n), 1)
    eye = jnp.where(r == c, 1.0, 0.0).astype(F32)
    tinv = [eye + l for l in ls]
    pw = ls
    for _ in range(5):
        pw = [_raw_dot(p, p, "nn") for p in pw]
        tinv = [t + _raw_dot(t, p, "nn") for t, p in zip(tinv, pw)]
    return tinv


@jax.custom_vjp
def _unit_lower_inverses(ls):
    return _raw_inverses(ls)


def _inverses_fwd(ls):
    tinv = _raw_inverses(ls)
    return tinv, tinv


def _inverses_bwd(tinv, gs):
    return ([_raw_dot(_raw_dot(t, g, "tn"), t, "nt") for t, g in zip(tinv, gs)],)


_unit_lower_inverses.defvjp(_inverses_fwd, _inverses_bwd)


@jax.custom_vjp
def _known_inverses(ls, tinv):
    return tinv


def _known_fwd(ls, tinv):
    return tinv, tinv


def _known_bwd(tinv, gs):
    return [_raw_dot(_raw_dot(t, g, "tn"), t, "nt") for t, g in zip(tinv, gs)], [jnp.zeros_like(t) for t in tinv]


_known_inverses.defvjp(_known_fwd, _known_bwd)


@jax.custom_vjp
def _use_kept(computed, kept):
    return kept


def _use_kept_fwd(computed, kept):
    return kept, None


def _use_kept_bwd(_, g):
    return g, jax.tree.map(jnp.zeros_like, g)


_use_kept.defvjp(_use_kept_fwd, _use_kept_bwd)

RW_KEPT = 5


def _f_rwscan(ps, xs, cs, kept=None):
    state = cs[0]
    ys, keep = [], []
    n = 2 * RW_CHUNK
    per_chunk = RW_KEPT * RW_PAIRS_PER_STEP * n
    for i in range(xs[0].shape[0] // RW_CHUNK):
        known = None
        if kept is not None:
            known = [[kept[i * per_chunk + (q * RW_PAIRS_PER_STEP + p) * n:
                           i * per_chunk + (q * RW_PAIRS_PER_STEP + p + 1) * n] for p in range(RW_PAIRS_PER_STEP)]
                     for q in range(RW_KEPT)]
        y, state, mats = _rwkv_chunk([x[i * RW_CHUNK:(i + 1) * RW_CHUNK] for x in xs], state, known)
        ys.append(y)
        keep += [m for group in mats for m in group]
    return [ys[0] if len(ys) == 1 else jnp.concatenate(ys, axis=0)], [state], jnp.concatenate(keep, axis=0)


def _rwkv_chunk(xs, state, known=None):
    npair = RW_PAIRS_PER_STEP
    pr = range(npair)
    r, lw, k, v, av, bv = [[x[:, p * LANES:(p + 1) * LANES] for p in pr] for x in xs]
    sv = [state[p * LANES:(p + 1) * LANES] for p in pr]
    c = RW_CHUNK
    n = 2 * c
    tri = jnp.where(_tril(c), 1.0, 0.0).astype(F32)
    cl = [_tdot(tri, lw[p], 3) for p in pr]
    cl_last = [_row(cl[p], c - 1) for p in pr]
    lane = lax.broadcasted_iota(jnp.int32, (c, LANES), 1)
    h0 = lane < RW_N

    def stack(x):
        return jnp.concatenate([jnp.where(h0, x, 0.0), jnp.where(h0, 0.0, x)], axis=0)

    am = [stack(av[p] * jnp.exp(cl[p] - lw[p])) for p in pr]
    bm = [stack(bv[p] * jnp.exp(-cl[p])) for p in pr]
    km = [stack(k[p] * jnp.exp(-cl[p])) for p in pr]
    rm = [stack(r[p] * jnp.exp(cl[p])) for p in pr]
    vm = [stack(v[p]) for p in pr]
    rn = lax.broadcasted_iota(jnp.int32, (n, n), 0)
    cn = lax.broadcasted_iota(jnp.int32, (n, n), 1)
    blk = (rn >= c) == (cn >= c)
    strict = blk & (cn < rn)
    incl = blk & (cn <= rn)
    lab = [jnp.where(strict, _dot(am[p], bm[p], "nt"), 0.0) for p in pr]
    lak = [jnp.where(strict, _dot(am[p], km[p], "nt"), 0.0) for p in pr]
    wrb = [jnp.where(incl, _dot(rm[p], bm[p], "nt"), 0.0) for p in pr]
    wrk = [jnp.where(incl, _dot(rm[p], km[p], "nt"), 0.0) for p in pr]
    if known is None:
        tinv = _unit_lower_inverses(lab)
    else:
        tinv = _known_inverses(lab, known[0])
        lak, wrb, wrk = _use_kept(lak, known[1]), _use_kept(wrb, known[2]), _use_kept(wrk, known[3])
    rhs = [_dot(am[p], sv[p], "nt") + _dot(lak[p], vm[p], "nn") for p in pr]
    um = [_dot(tinv[p], rhs[p], "nn") for p in pr]
    if known is not None:
        um = _use_kept(um, known[4])
    ym = [_dot(rm[p], sv[p], "nt") + _dot(wrb[p], um[p], "nn") + _dot(wrk[p], vm[p], "nn") for p in pr]
    sn = [(sv[p] + _dot(um[p], bm[p], "tn") + _dot(vm[p], km[p], "tn")) * jnp.exp(cl_last[p]) for p in pr]
    ys = [ym[p][:c] + ym[p][c:] for p in pr]
    return jnp.concatenate(ys, axis=1), jnp.concatenate(sn, axis=0), [tinv, lak, wrb, wrk, um]


def _f_mixers(ps, xs, cs):
    return _mixers(ps, xs, cs, None)


def _f_mixers_kept(ps, xs, cs, kept):
    return _mixers(ps, xs, cs, kept[0])[:2]


def _mixers(ps, xs, cs, kept):
    oa, st = _f_hgrn(ps[:2], xs[:4], cs[:1])
    (r, lw, k, v, av, bv, g), prev = _f_rwpre(ps[2:10], xs[4:], cs[1:2])
    y, sv, keep = _f_rwscan([], [r, lw, k, v, av, bv], cs[2:], kept)
    ob, _ = _f_rwpost(ps[10:], y + [r, k, v, g], [])
    return oa + ob, st + prev + sv, [keep]


def _f_rwpost(ps, xs, cs):
    ln_w, ln_b, r_k = ps
    y, r, k, v, g = xs
    inv_n = 1.0 / RW_N
    yc = y - _headsum(y) * inv_n
    var = _headsum(yc * yc) * inv_n
    yn = yc * lax.rsqrt(var + GN_EPS)
    yn = yn * ln_w + ln_b
    bonus = _headsum(r * k * r_k) * v
    return [(yn + bonus) * g], []


def _f_merge(ps, xs, cs):
    ga, gb, ya, yb = xs
    return [_sigmoid(ga) * ya + _sigmoid(gb) * yb], []


def _f_post1(ps, xs, cs):
    x, mix = xs
    h1 = x + _rms(mix, ps[0])
    return [h1, _rms(h1, ps[1])], []


def _f_conv(ps, xs, cs):
    cw, cb = ps
    p1, p2 = cs
    w0, w1, w2 = _row(cw, 0), _row(cw, 1), _row(cw, 2)
    t = xs[0].shape[0]
    hc = []
    for i, x in enumerate(xs):
        sl = slice(i * DFF, (i + 1) * DFF)
        s1 = _shift_down(x, p1[:, sl])
        s2 = _shift_down(s1, p2[:, sl])
        hc.append(cb[:, sl] + w0[:, sl] * s2 + w1[:, sl] * s1 + w2[:, sl] * x)
    n1 = jnp.concatenate([_row(x, t - 1) for x in xs], axis=1)
    n2 = jnp.concatenate([_row(x, t - 2) for x in xs], axis=1)
    return [_silu(hc[0]) * hc[1]], [n1, n2]


class _Stage:
    def __init__(self, name, f, g, tm, par_per_g, in_pieces, in_offs, carry_shapes, out_pieces, out_dtypes,
                 kept_shapes=(), f_kept=None):
        self.name, self.f, self.g, self.tm = name, f, g, tm
        self.par_per_g, self.in_pieces, self.in_offs = par_per_g, in_pieces, in_offs
        self.carry_shapes, self.out_pieces, self.out_dtypes = carry_shapes, out_pieces, out_dtypes
        self.kept_shapes, self.f_kept = list(kept_shapes), f_kept


def _par_spec(arr, per_g, g):
    r, c = arr.shape
    if per_g:
        return pl.BlockSpec((r, c // g), lambda gi, ni: (0, gi))
    return pl.BlockSpec((r, c), lambda gi, ni: (0, 0))


def _row_spec(tm, width, off, n, rev):
    if rev:
        return pl.BlockSpec((tm, width), lambda gi, ni: (n - 1 - ni, off + gi))
    return pl.BlockSpec((tm, width), lambda gi, ni: (ni, off + gi))


def _carry_spec(shape, n, rev):
    if rev:
        return pl.BlockSpec((None, None) + shape, lambda gi, ni: (gi, n - 1 - ni, 0, 0))
    return pl.BlockSpec((None, None) + shape, lambda gi, ni: (gi, ni, 0, 0))


def _load_pieces(refs, pieces_list):
    out = []
    for ref, pieces in zip(refs, pieces_list):
        o = 0
        for w in pieces:
            out.append(ref[:, o:o + w].astype(F32))
            o += w
    return out


def _store_pieces(refs, pieces_list, vals):
    k = 0
    for ref, pieces in zip(refs, pieces_list):
        o = 0
        for w in pieces:
            ref[:, o:o + w] = vals[k].astype(ref.dtype)
            k += 1
            o += w


_ANY = pl.BlockSpec(memory_space=pl.ANY)


class _Exchange:
    def __init__(self, kind, arrs):
        self.kind, self.arrs, self.results = kind, list(arrs), None
        if kind == "scatter":
            self.out_shape = [jax.ShapeDtypeStruct((N_DEV - 1,) + a.shape[1:], a.dtype) for a in self.arrs]
        else:
            self.out_shape = [jax.ShapeDtypeStruct((N_DEV,) + a.shape, a.dtype) for a in self.arrs]
        self.nsem = (N_DEV if kind == "gather2" else N_DEV - 1) * len(self.arrs)

    def copies(self, in_refs, out_refs, ssem, rsem):
        x, y, c = lax.axis_index("x"), lax.axis_index("y"), lax.axis_index("c")
        me = 4 * x + 2 * y + c
        cps = []
        for a, (i_ref, o_ref) in enumerate(zip(in_refs, out_refs)):
            for j in range(1, N_DEV):
                px = 1 - x if j & 4 else x
                py = 1 - y if j & 2 else y
                pc = 1 - c if j & 1 else c
                if self.kind == "gather":
                    src, dst = i_ref, o_ref.at[me]
                else:
                    src, dst = i_ref.at[4 * px + 2 * py + pc], o_ref.at[j - 1]
                s = (N_DEV - 1) * a + j - 1
                cps.append(pltpu.make_async_remote_copy(src_ref=src, dst_ref=dst, send_sem=ssem.at[s],
                                                        recv_sem=rsem.at[s], device_id=(px, py, pc),
                                                        device_id_type=MESH))
        return cps

    def run(self, step, total, in_refs, out_refs, ssem, rsem):
        if self.kind == "gather2":
            return self.run_two_level(step, total, in_refs, out_refs, ssem, rsem)

        @pl.when(step == 0)
        def _():
            for cp in self.copies(in_refs, out_refs, ssem, rsem):
                cp.start()

        @pl.when(step == total - 1)
        def _():
            for cp in self.copies(in_refs, out_refs, ssem, rsem):
                cp.wait()

    def run_two_level(self, step, total, in_refs, out_refs, ssem, rsem):
        x, y, c = lax.axis_index("x"), lax.axis_index("y"), lax.axis_index("c")
        sibling, xn, yn = (x, y, 1 - c), (1 - x, y, c), (x, 1 - y, c)
        arrs = range(len(in_refs))
        ns = N_DEV

        def num(px, py, pc):
            return 4 * px + 2 * py + pc

        def copy(a, k, to, src, dst):
            return pltpu.make_async_remote_copy(src_ref=src, dst_ref=dst, send_sem=ssem.at[ns * a + k],
                                                recv_sem=rsem.at[ns * a + k], device_id=to, device_id_type=MESH)

        def blk(a, b):
            return out_refs[a].at[b]

        def half(a, b, second):
            h = self.arrs[a].shape[0] // 2
            return out_refs[a].at[b, pl.ds(h if second else 0, h)]

        bx, by, bd = num(1 - x, y, c), num(x, 1 - y, c), num(1 - x, 1 - y, c)

        def firsts(a):
            own = blk(a, num(x, y, c))
            return [copy(a, 0, sibling, in_refs[a], own), copy(a, 1, xn, in_refs[a], own),
                    copy(a, 2, yn, in_refs[a], own)]

        def seconds(a):
            return [copy(a, 3, yn, half(a, bx, False), half(a, bx, False)), copy(a, 5, sibling, blk(a, bx), blk(a, bx)),
                    copy(a, 4, xn, half(a, by, True), half(a, by, True)), copy(a, 6, sibling, blk(a, by), blk(a, by))]

        def third(a):
            return copy(a, 7, sibling, blk(a, bd), blk(a, bd))

        @pl.when(step == 0)
        def _():
            for a in arrs:
                for cp in firsts(a):
                    cp.start()

        @pl.when(step == total // 2)
        def _():
            for a in arrs:
                copy(a, 1, xn, blk(a, bx), blk(a, bx)).wait_recv()
                copy(a, 2, yn, blk(a, by), blk(a, by)).wait_recv()
                for cp in seconds(a):
                    cp.start()

        @pl.when(step == (4 * total) // 5)
        def _():
            for a in arrs:
                copy(a, 3, yn, half(a, bd, False), half(a, bd, False)).wait_recv()
                copy(a, 4, xn, half(a, bd, True), half(a, bd, True)).wait_recv()
                third(a).start()

        @pl.when(step == total - 1)
        def _():
            for a in arrs:
                for k, b in ((0, num(x, y, 1 - c)), (5, num(1 - x, y, 1 - c)), (6, num(x, 1 - y, 1 - c)),
                             (7, num(1 - x, 1 - y, 1 - c))):
                    copy(a, k, sibling, blk(a, b), blk(a, b)).wait_recv()
                for cp in firsts(a) + seconds(a) + [third(a)]:
                    cp.wait_send()


def _hook_specs(hook):
    if hook is None:
        return [], [], [], []
    na = len(hook.arrs)
    sems = [pltpu.SemaphoreType.DMA((hook.nsem,)), pltpu.SemaphoreType.DMA((hook.nsem,))]
    return [_ANY] * na, [_ANY] * na, hook.out_shape, sems


def _stage_fwd(st, t, params, inputs, hook=None):
    g, tm = st.g, min(st.tm, t)
    n = t // tm
    npar, nin, ncar, nout = len(params), len(inputs), len(st.carry_shapes), len(st.out_pieces)
    nk = len(st.kept_shapes)
    h_in, h_out, h_shape, h_sems = _hook_specs(hook)
    nh = len(h_in)

    def body(*refs):
        p_refs = refs[:npar]
        x_refs = refs[npar:npar + nin]
        hi_refs = refs[npar + nin:npar + nin + nh]
        o = npar + nin + nh
        o_refs = refs[o:o + nout]
        s_refs = refs[o + nout:o + nout + ncar]
        k_refs = refs[o + nout + ncar:o + nout + ncar + nk]
        o += nout + ncar + nk
        ho_refs = refs[o:o + nh]
        c_scr = refs[o + nh:o + nh + ncar]
        gi, ni = pl.program_id(0), pl.program_id(1)
        if hook is not None:
            step = gi * n + ni
            hook.run(step, g * n, hi_refs, ho_refs, *refs[-2:])

        @pl.when(ni == 0)
        def _():
            for c in c_scr:
                c[...] = jnp.zeros(c.shape, F32)

        ps = [r[...].astype(F32) for r in p_refs]
        xs = _load_pieces(x_refs, st.in_pieces)
        cs = [c[...] for c in c_scr]
        for s, c in zip(s_refs, cs):
            s[...] = c
        res = st.f(ps, xs, cs)
        outs, ncs = res[0], res[1]
        _store_pieces(o_refs, st.out_pieces, outs)
        for c, v in zip(c_scr, ncs):
            c[...] = v
        for kr, kv in zip(k_refs, res[2] if nk else []):
            kr[...] = kv.astype(kr.dtype)

    in_specs = [_par_spec(p, pg, g) for p, pg in zip(params, st.par_per_g)]
    in_specs += [_row_spec(tm, sum(pc), off, n, False) for pc, off in zip(st.in_pieces, st.in_offs)]
    out_specs = [_row_spec(tm, sum(pc), 0, n, False) for pc in st.out_pieces]
    out_specs += [_carry_spec(s, n, False) for s in st.carry_shapes]
    out_specs += [pl.BlockSpec(s, lambda gi, ni: (ni, 0)) for s in st.kept_shapes]
    out_shape = [jax.ShapeDtypeStruct((t, g * sum(pc)), dt) for pc, dt in zip(st.out_pieces, st.out_dtypes)]
    out_shape += [jax.ShapeDtypeStruct((g, n) + s, F32) for s in st.carry_shapes]
    out_shape += [jax.ShapeDtypeStruct((n * s[0], s[1]), BF) for s in st.kept_shapes]
    res = pl.pallas_call(
        body, name=st.name + "_fwd", grid=(g, n), in_specs=in_specs + h_in, out_specs=out_specs + h_out,
        out_shape=out_shape + h_shape,
        scratch_shapes=[pltpu.VMEM(s, F32) for s in st.carry_shapes] + h_sems,
        compiler_params=_cparams(("arbitrary", "arbitrary")),
    )(*params, *inputs, *(hook.arrs if hook else []))
    if hook is not None:
        hook.results = list(res[nout + ncar + nk:])
    return list(res[:nout]), list(res[nout:nout + ncar + nk])


def _stage_bwd(st, t, params, inputs, saved, douts, dx_dtypes, hook=None):
    g, tm = st.g, min(st.tm, t)
    n = t // tm
    npar, nin, ncar = len(params), len(inputs), len(st.carry_shapes)
    nk = len(st.kept_shapes)
    flat_d = [d for ds in douts for d in ds]
    nd = len(flat_d)
    dx_idx = [i for i, dt in enumerate(dx_dtypes) if dt is not None]
    h_in, h_out, h_shape, h_sems = _hook_specs(hook)
    nh = len(h_in)

    def body(*refs):
        p_refs = refs[:npar]
        x_refs = refs[npar:npar + nin]
        s_refs = refs[npar + nin:npar + nin + ncar]
        k_refs = refs[npar + nin + ncar:npar + nin + ncar + nk]
        o = npar + nin + ncar + nk
        d_refs = refs[o:o + nd]
        hi_refs = refs[o + nd:o + nd + nh]
        o += nd + nh
        dp_refs = refs[o:o + npar]
        dx_refs = refs[o + npar:o + npar + len(dx_idx)]
        ho_refs = refs[o + npar + len(dx_idx):o + npar + len(dx_idx) + nh]
        dc_scr = refs[o + npar + len(dx_idx) + nh:o + npar + len(dx_idx) + nh + ncar]
        gi, ni = pl.program_id(0), pl.program_id(1)
        if hook is not None:
            step = gi * n + ni
            hook.run(step, g * n, hi_refs, ho_refs, *refs[-2:])

        @pl.when(ni == 0)
        def _():
            for c in dc_scr:
                c[...] = jnp.zeros(c.shape, F32)

        ps = [r[...].astype(F32) for r in p_refs]
        xs = _load_pieces(x_refs, st.in_pieces)
        cs = [s[...] for s in s_refs]
        dys = []
        k = 0
        for ds, pieces in zip(douts, st.out_pieces):
            acc = _load_pieces([d_refs[k]], [pieces])
            for j in range(1, len(ds)):
                more = _load_pieces([d_refs[k + j]], [pieces])
                acc = [a + b for a, b in zip(acc, more)]
            dys += acc
            k += len(ds)
        if nk:
            kept = [r[...].astype(F32) for r in k_refs]
            _, vjp = jax.vjp(lambda p, x, c: st.f_kept(p, x, c, kept), ps, xs, cs)
        else:
            _, vjp = jax.vjp(st.f, ps, xs, cs)
        dps, dxs, dcs = vjp((dys, [c[...] for c in dc_scr]))
        k = 0
        per_in = []
        for pieces in st.in_pieces:
            per_in.append(dxs[k:k + len(pieces)])
            k += len(pieces)
        for ref, i in zip(dx_refs, dx_idx):
            _store_pieces([ref], [st.in_pieces[i]], per_in[i])
        for c, v in zip(dc_scr, dcs):
            c[...] = v
        for ref, dp, pg in zip(dp_refs, dps, st.par_per_g):
            first = (ni == 0) if pg else ((ni == 0) & (gi == 0))

            @pl.when(first)
            def _():
                ref[...] = jnp.zeros(ref.shape, F32)

            ref[...] += dp

    in_specs = [_par_spec(p, pg, g) for p, pg in zip(params, st.par_per_g)]
    in_specs += [_row_spec(tm, sum(pc), off, n, True) for pc, off in zip(st.in_pieces, st.in_offs)]
    in_specs += [_carry_spec(s, n, True) for s in st.carry_shapes]
    in_specs += [pl.BlockSpec(s, lambda gi, ni: (n - 1 - ni, 0)) for s in st.kept_shapes]
    for ds, pc in zip(douts, st.out_pieces):
        in_specs += [_row_spec(tm, sum(pc), 0, n, True) for _ in ds]
    out_specs = [_par_spec(p, pg, g) for p, pg in zip(params, st.par_per_g)]
    out_specs += [_row_spec(tm, sum(st.in_pieces[i]), 0, n, True) for i in dx_idx]
    out_shape = [jax.ShapeDtypeStruct(p.shape, F32) for p in params]
    out_shape += [jax.ShapeDtypeStruct((t, g * sum(st.in_pieces[i])), dx_dtypes[i]) for i in dx_idx]
    res = pl.pallas_call(
        body, name=st.name + "_bwd", grid=(g, n), in_specs=in_specs + h_in, out_specs=out_specs + h_out,
        out_shape=out_shape + h_shape,
        scratch_shapes=[pltpu.VMEM(s, F32) for s in st.carry_shapes] + h_sems,
        compiler_params=_cparams(("arbitrary", "arbitrary")),
    )(*params, *inputs, *saved, *flat_d, *(hook.arrs if hook else []))
    if hook is not None:
        hook.results = list(res[npar + len(dx_idx):])
    return list(res[:npar]), list(res[npar:npar + len(dx_idx)])


def _pick(n, cap):
    if n <= cap:
        return n
    best = LANES
    for k in range(1, n // LANES + 1):
        if (n // LANES) % k == 0 and k * LANES <= cap:
            best = k * LANES
    return best


def _mm(name, a, b, mode, out_dtype=F32, tm=1024, tn=512, b_outer=False):
    m = a.shape[1] if mode == "tn" else a.shape[0]
    k = a.shape[0] if mode == "tn" else a.shape[1]
    n = b.shape[0] if mode == "nt" else b.shape[1]
    tm, tn = _pick(m, tm), _pick(n, tn)
    if b_outer:
        grid = (n // tn, m // tm)
        ij = lambda p, q: (q, p)
    else:
        grid = (m // tm, n // tn)
        ij = lambda p, q: (p, q)

    def body(a_ref, b_ref, o_ref):
        o_ref[...] = _raw_dot(a_ref[...], b_ref[...], mode).astype(o_ref.dtype)

    if mode == "tn":
        a_spec = pl.BlockSpec((k, tm), lambda p, q: (0, ij(p, q)[0]))
    else:
        a_spec = pl.BlockSpec((tm, k), lambda p, q: (ij(p, q)[0], 0))
    b_mode = dict(pipeline_mode=pl.Buffered(1)) if tn == n else {}
    if mode == "nt":
        b_spec = pl.BlockSpec((tn, k), lambda p, q: (ij(p, q)[1], 0), **b_mode)
    else:
        b_spec = pl.BlockSpec((k, tn), lambda p, q: (0, ij(p, q)[1]), **b_mode)
    return pl.pallas_call(
        body, name=name, grid=grid, in_specs=[a_spec, b_spec],
        out_specs=pl.BlockSpec((tm, tn), lambda p, q: ij(p, q)),
        out_shape=jax.ShapeDtypeStruct((m, n), out_dtype),
        compiler_params=_cparams(("arbitrary", "arbitrary")),
    )(a, b)


def _mm_multi(name, pairs, mode, out_dtype, tm=1024, tn=512):
    a0, b0 = pairs[0]
    m = a0.shape[1] if mode == "tn" else a0.shape[0]
    k = a0.shape[0] if mode == "tn" else a0.shape[1]
    n = b0.shape[0] if mode == "nt" else b0.shape[1]
    tm, tn = _pick(m, tm), _pick(n, tn)
    npair = len(pairs)

    def body(*refs):
        for p in range(npair):
            refs[2 * npair + p][...] = _raw_dot(refs[2 * p][...], refs[2 * p + 1][...], mode).astype(out_dtype)

    a_spec = pl.BlockSpec((k, tm), lambda i, j: (0, i)) if mode == "tn" else pl.BlockSpec((tm, k), lambda i, j: (i, 0))
    b_spec = pl.BlockSpec((tn, k), lambda i, j: (j, 0)) if mode == "nt" else pl.BlockSpec((k, tn), lambda i, j: (0, j))
    return pl.pallas_call(
        body, name=name, grid=(m // tm, n // tn), in_specs=[a_spec, b_spec] * npair,
        out_specs=[pl.BlockSpec((tm, tn), lambda i, j: (i, j))] * npair,
        out_shape=[jax.ShapeDtypeStruct((m, n), out_dtype)] * npair,
        compiler_params=_cparams(("arbitrary", "arbitrary")),
    )(*[x for pair in pairs for x in pair])


def _mm_cols_tn(name, pieces, b, out_dtype, tm):
    k, n = b.shape
    counts = [p.shape[1] // tm for p in pieces]
    starts = [sum(counts[:i]) for i in range(len(pieces))]
    na = len(pieces)

    def body(*refs):
        b_ref, o_ref = refs[na], refs[-1]
        i = pl.program_id(0)
        for a_ref, s, c in zip(refs[:na], starts, counts):
            @pl.when((i >= s) & (i < s + c))
            def _():
                o_ref[...] = _raw_dot(a_ref[...], b_ref[...], "tn").astype(o_ref.dtype)

    def spec(s, c):
        return pl.BlockSpec((k, tm), lambda i: (0, jnp.clip(i - s, 0, c - 1)))

    return pl.pallas_call(
        body, name=name, grid=(sum(counts),),
        in_specs=[spec(s, c) for s, c in zip(starts, counts)]
        + [pl.BlockSpec(b.shape, lambda i: (0, 0), pipeline_mode=pl.Buffered(1))],
        out_specs=pl.BlockSpec((tm, n), lambda i: (i, 0)),
        out_shape=jax.ShapeDtypeStruct((sum(counts) * tm, n), out_dtype),
        compiler_params=_cparams(("arbitrary",)),
    )(*pieces, b)


def _norm_in_proj(x, g, w_t, tm, tn):
    t, k = x.shape
    n = w_t.shape[0]
    tm, tn = _pick(t, tm), _pick(n, tn)

    def body(x_ref, g_ref, w_ref, xn_ref, z_ref):
        xn = _rms(x_ref[...], g_ref[...]).astype(BF)
        xn_ref[...] = xn
        z_ref[...] = _raw_dot(xn, w_ref[...], "nt")

    xns, z = pl.pallas_call(
        body, name="in_proj", grid=(n // tn, t // tm),
        in_specs=[pl.BlockSpec((tm, k), lambda j, i: (i, 0)), pl.BlockSpec((1, k), lambda j, i: (0, 0)),
                  pl.BlockSpec((tn, k), lambda j, i: (j, 0))],
        out_specs=[pl.BlockSpec((None, tm, k), lambda j, i: (j, i, 0)), pl.BlockSpec((tm, tn), lambda j, i: (i, j))],
        out_shape=[jax.ShapeDtypeStruct((n // tn, t, k), BF), jax.ShapeDtypeStruct((t, n), F32)],
        compiler_params=_cparams(("arbitrary", "arbitrary")),
    )(x, g, w_t)
    return xns[0], z


def _merge_out_post(z, o_a, o_b, w_a, w_b, w_out, x, g_post, g_pre2, tm):
    t = x.shape[0]
    tm = _pick(t, tm)
    w = 256
    npc = D // w
    ga0, gb0 = (IN_COLS - 2 * D) // w, (IN_COLS - D) // w

    def body(*refs):
        ga_refs, gb_refs = refs[:npc], refs[npc:2 * npc]
        oa_ref, ob_ref, wa_ref, wb_ref, w_ref, x_ref, gp_ref, g2_ref = refs[2 * npc:2 * npc + 8]
        ya_ref, yb_ref, m_ref, mix_ref, h_ref, xn_ref = refs[2 * npc + 8:]
        ya = _raw_dot(oa_ref[...], wa_ref[...], "nn").astype(BF)
        yb = _raw_dot(ob_ref[...], wb_ref[...], "nn").astype(BF)
        ya_ref[...] = ya
        yb_ref[...] = yb
        parts = []
        for p in range(npc):
            cols = slice(p * w, (p + 1) * w)
            parts.append(_sigmoid(ga_refs[p][...]) * ya[:, cols].astype(F32)
                         + _sigmoid(gb_refs[p][...]) * yb[:, cols].astype(F32))
        merged = jnp.concatenate(parts, axis=1).astype(BF)
        m_ref[...] = merged
        mix = _raw_dot(merged, w_ref[...], "nn")
        mix_ref[...] = mix
        h1 = x_ref[...] + _rms(mix, gp_ref[...])
        h_ref[...] = h1
        xn_ref[...] = _rms(h1, g2_ref[...]).astype(BF)

    row = pl.BlockSpec((tm, D), lambda i: (i, 0))
    one = pl.BlockSpec((1, D), lambda i: (0, 0))

    def gate(b0):
        return [pl.BlockSpec((tm, w), functools.partial(lambda i, b: (i, b), b=b0 + p)) for p in range(npc)]

    wgt = pl.BlockSpec((D, D), lambda i: (0, 0), pipeline_mode=pl.Buffered(1))
    return pl.pallas_call(
        body, name="merge_out_post", grid=(t // tm,),
        in_specs=gate(ga0) + gate(gb0) + [row, row, wgt, wgt, wgt, row, one, one],
        out_specs=[row] * 6,
        out_shape=[jax.ShapeDtypeStruct((t, D), BF), jax.ShapeDtypeStruct((t, D), BF), jax.ShapeDtypeStruct((t, D), BF),
                   jax.ShapeDtypeStruct((t, D), F32), jax.ShapeDtypeStruct((t, D), F32),
                   jax.ShapeDtypeStruct((t, D), BF)],
        compiler_params=_cparams(("arbitrary",)),
    )(*([z] * (2 * npc)), o_a, o_b, w_a, w_b, w_out, x, g_post, g_pre2)


def _accumulate(ni, refs, vals):
    @pl.when(ni == 0)
    def _():
        for r in refs:
            r[...] = jnp.zeros(r.shape, F32)

    for r, v in zip(refs, vals):
        r[...] += v


def _dmerged_merge_bwd(dmix, w_out, w_a, w_b, z, y_a, y_b, tm):
    t = dmix.shape[0]
    tm = _pick(t, tm)
    w = 256
    npc = D // w
    ga0, gb0 = (IN_COLS - 2 * D) // w, (IN_COLS - D) // w

    def body(*refs):
        dm_ref, w_ref, wa_ref, wb_ref = refs[:4]
        ga_refs, gb_refs = refs[4:4 + npc], refs[4 + npc:4 + 2 * npc]
        ya_ref, yb_ref, dga_ref, dgb_ref, dya_ref, dyb_ref, doa_ref, dob_ref = refs[4 + 2 * npc:]
        dmerged = _raw_dot(dm_ref[...], w_ref[...], "nt")
        dyas, dybs = [], []
        for p in range(npc):
            cols = slice(p * w, (p + 1) * w)
            xs = [ga_refs[p][...], gb_refs[p][...], ya_ref[:, cols].astype(F32), yb_ref[:, cols].astype(F32)]
            _, vjp = jax.vjp(lambda *a: _f_merge([], list(a), [])[0][0], *xs)
            dga, dgb, dya, dyb = vjp(dmerged[:, cols])
            dga_ref[:, cols] = dga.astype(BF)
            dgb_ref[:, cols] = dgb.astype(BF)
            dyas.append(dya.astype(BF))
            dybs.append(dyb.astype(BF))
        dya, dyb = jnp.concatenate(dyas, axis=1), jnp.concatenate(dybs, axis=1)
        dya_ref[...] = dya
        dyb_ref[...] = dyb
        doa_ref[...] = _raw_dot(dya, wa_ref[...], "nt").astype(BF)
        dob_ref[...] = _raw_dot(dyb, wb_ref[...], "nt").astype(BF)

    row = pl.BlockSpec((tm, D), lambda i: (i, 0))
    wgt = pl.BlockSpec((D, D), lambda i: (0, 0), pipeline_mode=pl.Buffered(1))

    def gate(b0):
        return [pl.BlockSpec((tm, w), functools.partial(lambda i, b: (i, b), b=b0 + p)) for p in range(npc)]

    return pl.pallas_call(
        body, name="merge_bwd", grid=(t // tm,),
        in_specs=[row, wgt, wgt, wgt] + gate(ga0) + gate(gb0) + [row, row],
        out_specs=[row] * 6, out_shape=[jax.ShapeDtypeStruct((t, D), BF)] * 6,
        compiler_params=_cparams(("arbitrary",)),
    )(dmix, w_out, w_a, w_b, *([z] * (2 * npc)), y_a, y_b)


def _dxn2_post1_bwd(pieces, w_up_t, x, mix, dh1, g_post, g_pre2, tm):
    t = x.shape[0]
    tm = _pick(t, tm)
    k = w_up_t.shape[0]
    offs = [sum(p.shape[1] for p in pieces[:i]) for i in range(len(pieces))]
    na = len(pieces)

    def body(*refs):
        w_ref, x_ref, m_ref, dh_ref, gp_ref, g2_ref, dgp_ref, dg2_ref, dx_ref, dm_ref = refs[na:]
        dxn2 = None
        for a_ref, off in zip(refs[:na], offs):
            part = _raw_dot(a_ref[...], w_ref[off:off + a_ref.shape[1], :], "nn")
            dxn2 = part if dxn2 is None else dxn2 + part
        _, vjp = jax.vjp(lambda gp, g2, xx, mm: _f_post1([gp, g2], [xx, mm], [])[0],
                         gp_ref[...], g2_ref[...], x_ref[...], m_ref[...])
        dgp, dg2, dx, dm = vjp([dh_ref[...], dxn2])
        _accumulate(pl.program_id(0), [dgp_ref, dg2_ref], [dgp, dg2])
        dx_ref[...] = dx
        dm_ref[...] = dm.astype(BF)

    row = pl.BlockSpec((tm, D), lambda i: (i, 0))
    one = pl.BlockSpec((1, D), lambda i: (0, 0))
    return pl.pallas_call(
        body, name="post1_bwd", grid=(t // tm,),
        in_specs=[pl.BlockSpec((tm, p.shape[1]), lambda i: (i, 0)) for p in pieces]
        + [pl.BlockSpec((k, D), lambda i: (0, 0), pipeline_mode=pl.Buffered(1)), row, row, row, one, one],
        out_specs=[one, one, row, row],
        out_shape=[jax.ShapeDtypeStruct((1, D), F32), jax.ShapeDtypeStruct((1, D), F32),
                   jax.ShapeDtypeStruct((t, D), F32), jax.ShapeDtypeStruct((t, D), BF)],
        compiler_params=_cparams(("arbitrary",)),
    )(*pieces, w_up_t, x, mix, dh1, g_post, g_pre2)


def _conv_taps(h, cw, cb, p2, p1):
    s1 = _shift_down(h, p1)
    s2 = _shift_down(s1, p2)
    return cb + _row(cw, 0) * s2 + _row(cw, 1) * s1 + _row(cw, 2) * h


def _up_conv(xn2, w_up_t, conv_w, conv_b, tm, tc):
    t = xn2.shape[0]
    tm = _pick(t, tm)
    tn = _pick(DFF, 1408)
    nj = DFF // tn
    sub = tm // tc
    n = t // tc
    last = t // tm - 1

    def body(x_ref, wg_ref, wv_ref, cwg_ref, cwv_ref, cbg_ref, cbv_ref, hg_ref, hv_ref, act_ref, c1_ref, c2_ref, prev):
        j, i = pl.program_id(0), pl.program_id(1)

        @pl.when(i == 0)
        def _():
            prev[...] = jnp.zeros(prev.shape, F32)

        x = x_ref[...]
        hg = _raw_dot(x, wg_ref[...], "nt")
        hv = _raw_dot(x, wv_ref[...], "nt")
        hg_ref[...] = hg
        hv_ref[...] = hv
        pg, pv = prev[0:SUBLANES], prev[SUBLANES:2 * SUBLANES]
        cg = _conv_taps(hg, cwg_ref[...], cbg_ref[...], _row(pg, SUBLANES - 2), _row(pg, SUBLANES - 1))
        cv = _conv_taps(hv, cwv_ref[...], cbv_ref[...], _row(pv, SUBLANES - 2), _row(pv, SUBLANES - 1))
        act_ref[...] = (_silu(cg) * cv).astype(BF)
        prev[0:SUBLANES] = hg[tm - SUBLANES:tm]
        prev[SUBLANES:2 * SUBLANES] = hv[tm - SUBLANES:tm]

        def keep(h, off):
            cols = slice(off, off + tn)

            @pl.when(i == 0)
            def _():
                c1_ref[0, :, cols] = jnp.zeros((1, tn), F32)
                c2_ref[0, :, cols] = jnp.zeros((1, tn), F32)

            for s in range(sub):
                def put(s=s):
                    tail = h[(s + 1) * tc - SUBLANES:(s + 1) * tc]
                    c1_ref[i * sub + s + 1, :, cols] = _row(tail, SUBLANES - 1)
                    c2_ref[i * sub + s + 1, :, cols] = _row(tail, SUBLANES - 2)

                if s < sub - 1:
                    put()
                else:
                    pl.when(i < last)(put)

        for col in range(nj):
            @pl.when(j == col)
            def _(col=col):
                keep(hg, col * tn)
                keep(hv, DFF + col * tn)

    def cols(rows, off):
        return pl.BlockSpec((rows, tn), lambda j, i: (0, j + off))

    tile = pl.BlockSpec((tm, tn), lambda j, i: (i, j))
    before = pl.BlockSpec((n, 1, 2 * DFF), lambda j, i: (0, 0, 0))
    return pl.pallas_call(
        body, name="up_conv", grid=(nj, t // tm),
        in_specs=[pl.BlockSpec((tm, D), lambda j, i: (i, 0)), pl.BlockSpec((tn, D), lambda j, i: (j, 0)),
                  pl.BlockSpec((tn, D), lambda j, i: (j + nj, 0)), cols(3, 0), cols(3, nj), cols(1, 0), cols(1, nj)],
        out_specs=[tile, tile, tile, before, before],
        out_shape=[jax.ShapeDtypeStruct((t, DFF), F32), jax.ShapeDtypeStruct((t, DFF), F32),
                   jax.ShapeDtypeStruct((t, DFF), BF), jax.ShapeDtypeStruct((n, 1, 2 * DFF), F32),
                   jax.ShapeDtypeStruct((n, 1, 2 * DFF), F32)],
        scratch_shapes=[pltpu.VMEM((2 * SUBLANES, tn), F32)],
        compiler_params=_cparams(("arbitrary", "arbitrary")),
    )(xn2, w_up_t, w_up_t, conv_w, conv_w, conv_b, conv_b)


def _dxn_pre1_bwd(pieces, w_t, x, dx_res, g, tm, token):
    t = x.shape[0]
    tm = _pick(t, tm)
    offs = [sum(p.shape[1] for p in pieces[:i]) for i in range(len(pieces))]
    na = len(pieces)

    def body(*refs):
        w_ref, x_ref, r_ref, g_ref = refs[na:na + 4]
        dg_ref, dx_ref = refs[-2:]
        dxn = None
        for a_ref, off in zip(refs[:na], offs):
            part = _raw_dot(a_ref[...], w_ref[off:off + a_ref.shape[1], :], "nn")
            dxn = part if dxn is None else dxn + part
        _, vjp = jax.vjp(lambda gg, xx: _f_pre1_residual([gg], [xx], [])[0], g_ref[...], x_ref[...])
        dg, dx = vjp([dxn, r_ref[...]])
        _accumulate(pl.program_id(0), [dg_ref], [dg])
        dx_ref[...] = dx

    row = pl.BlockSpec((tm, D), lambda i: (i, 0))
    one = pl.BlockSpec((1, D), lambda i: (0, 0))
    return pl.pallas_call(
        body, name="pre1_bwd", grid=(t // tm,),
        in_specs=[pl.BlockSpec((tm, p.shape[1]), lambda i: (i, 0)) for p in pieces]
        + [pl.BlockSpec(w_t.shape, lambda i: (0, 0), pipeline_mode=pl.Buffered(1)), row, row, one,
           pl.BlockSpec(token.shape, lambda i: (0, 0))],
        out_specs=[one, row],
        out_shape=[jax.ShapeDtypeStruct((1, D), F32), jax.ShapeDtypeStruct((t, D), F32)],
        compiler_params=_cparams(("arbitrary",)),
    )(*pieces, w_t, x, dx_res, g, token)


def _down_loss(act, w_down, g_post, h1, tgt, tm):
    t, k = act.shape
    tm = _pick(t, tm)

    def body(a_ref, w_ref, g_ref, h_ref, t_ref, loss_ref, dg_ref, dh_ref, df_ref):
        ni = pl.program_id(0)
        ff = _raw_dot(a_ref[...], w_ref[...], "nn")
        target = t_ref[...]

        def lossf(g, h1, ff):
            e = h1 + _rms(ff, g) - target
            return 0.5 * jnp.sum(jnp.mean(e * e, axis=-1))

        l, (dg, dh, df) = jax.value_and_grad(lossf, argnums=(0, 1, 2))(g_ref[...], h_ref[...], ff)

        @pl.when(ni == 0)
        def _():
            loss_ref[...] = jnp.zeros(loss_ref.shape, F32)
            dg_ref[...] = jnp.zeros(dg_ref.shape, F32)

        loss_ref[...] += jnp.full(loss_ref.shape, l, F32)
        dg_ref[...] += dg
        dh_ref[...] = dh
        df_ref[...] = df.astype(df_ref.dtype)

    row = pl.BlockSpec((tm, D), lambda ni: (ni, 0))
    one = pl.BlockSpec((1, D), lambda ni: (0, 0))
    return pl.pallas_call(
        body, name="down_loss", grid=(t // tm,),
        in_specs=[pl.BlockSpec((tm, k), lambda ni: (ni, 0)),
                  pl.BlockSpec((k, D), lambda ni: (0, 0), pipeline_mode=pl.Buffered(1)), one, row, row],
        out_specs=[pl.BlockSpec((1, LANES), lambda ni: (0, 0)), one, row, row],
        out_shape=[jax.ShapeDtypeStruct((1, LANES), F32), jax.ShapeDtypeStruct((1, D), F32),
                   jax.ShapeDtypeStruct((t, D), F32), jax.ShapeDtypeStruct((t, D), BF)],
        compiler_params=_cparams(("arbitrary",)),
    )(act, w_down, g_post, h1, tgt)


_ANY = pl.BlockSpec(memory_space=pl.ANY)


def _all_gather(name, blks):
    na = len(blks)
    ns = 8

    def body(*refs):
        x_refs, out_refs = refs[:na], refs[na:2 * na]
        send_sems, recv_sems, local_sems = refs[2 * na:]
        x, y, cc = lax.axis_index("x"), lax.axis_index("y"), lax.axis_index("c")
        sibling, xn, yn = (x, y, 1 - cc), (1 - x, y, cc), (x, 1 - y, cc)

        def num(px, py, pc):
            return 4 * px + 2 * py + pc

        def copy(a, k, to, src, dst):
            return pltpu.make_async_remote_copy(src_ref=src, dst_ref=dst, send_sem=send_sems.at[ns * a + k],
                                                recv_sem=recv_sems.at[ns * a + k], device_id=to, device_id_type=MESH)

        def halves(a, blk):
            h = blks[a].shape[0] // 2
            return out_refs[a].at[blk, pl.ds(0, h)], out_refs[a].at[blk, pl.ds(h, h)]

        mine, sends = [], []
        for a in range(na):
            o = out_refs[a]
            m = pltpu.make_async_copy(x_refs[a], o.at[num(x, y, cc)], local_sems.at[a])
            m.start()
            mine.append(m)
            own = o.at[num(x, y, cc)]
            sends.append([copy(a, 0, sibling, x_refs[a], own), copy(a, 1, xn, x_refs[a], own),
                          copy(a, 2, yn, x_refs[a], own)])
            for cp in sends[a]:
                cp.start()
        for a in range(na):
            o = out_refs[a]
            bx, by, bd = num(1 - x, y, cc), num(x, 1 - y, cc), num(1 - x, 1 - y, cc)
            copy(a, 1, xn, o.at[bx], o.at[bx]).wait_recv()
            more = [copy(a, 3, yn, halves(a, bx)[0], halves(a, bx)[0]), copy(a, 5, sibling, o.at[bx], o.at[bx])]
            for cp in more:
                cp.start()
            sends[a] += more
        for a in range(na):
            o = out_refs[a]
            bx, by, bd = num(1 - x, y, cc), num(x, 1 - y, cc), num(1 - x, 1 - y, cc)
            copy(a, 2, yn, o.at[by], o.at[by]).wait_recv()
            more = [copy(a, 4, xn, halves(a, by)[1], halves(a, by)[1]), copy(a, 6, sibling, o.at[by], o.at[by])]
            for cp in more:
                cp.start()
            sends[a] += more
        for a in range(na):
            o = out_refs[a]
            bd = num(1 - x, 1 - y, cc)
            copy(a, 3, yn, halves(a, bd)[0], halves(a, bd)[0]).wait_recv()
            copy(a, 4, xn, halves(a, bd)[1], halves(a, bd)[1]).wait_recv()
            fw = copy(a, 7, sibling, o.at[bd], o.at[bd])
            fw.start()
            sends[a].append(fw)
        for a in range(na):
            o = out_refs[a]
            for k, blk in ((0, num(x, y, 1 - cc)), (5, num(1 - x, y, 1 - cc)), (6, num(x, 1 - y, 1 - cc)),
                           (7, num(1 - x, 1 - y, 1 - cc))):
                copy(a, k, sibling, o.at[blk], o.at[blk]).wait_recv()
            for cp in sends[a]:
                cp.wait_send()
        for m in mine:
            m.wait()

    res = pl.pallas_call(
        body, name=name, in_specs=[_ANY] * na, out_specs=[_ANY] * na,
        out_shape=[jax.ShapeDtypeStruct((N_DEV,) + b.shape, b.dtype) for b in blks],
        scratch_shapes=[pltpu.SemaphoreType.DMA((ns * na,)), pltpu.SemaphoreType.DMA((ns * na,)),
                        pltpu.SemaphoreType.DMA((na,))],
    )(*blks)
    return list(res)


def _all_gather_small(name, blk):
    def body(x_ref, out_ref, ssem, rsem, lsem):
        x, y, c = lax.axis_index("x"), lax.axis_index("y"), lax.axis_index("c")
        me = 4 * x + 2 * y + c
        mine = pltpu.make_async_copy(x_ref, out_ref.at[me], lsem)
        mine.start()
        cps = []
        for j in range(1, N_DEV):
            px = 1 - x if j & 4 else x
            py = 1 - y if j & 2 else y
            pc = 1 - c if j & 1 else c
            cps.append(pltpu.make_async_remote_copy(src_ref=x_ref, dst_ref=out_ref.at[me], send_sem=ssem.at[j - 1],
                                                    recv_sem=rsem.at[j - 1], device_id=(px, py, pc),
                                                    device_id_type=MESH))
        for cp in cps:
            cp.start()
        for cp in cps:
            cp.wait()
        mine.wait()

    return pl.pallas_call(
        body, name=name, in_specs=[_ANY], out_specs=_ANY,
        out_shape=jax.ShapeDtypeStruct((N_DEV,) + blk.shape, blk.dtype),
        scratch_shapes=[pltpu.SemaphoreType.DMA((N_DEV - 1,)), pltpu.SemaphoreType.DMA((N_DEV - 1,)),
                        pltpu.SemaphoreType.DMA],
    )(blk)


def _reduce_pair(g8s):
    na = len(g8s)

    def body(*refs):
        g_refs, recv_refs = refs[:na], refs[na:2 * na]
        ssem, rsem = refs[2 * na:]
        x, y, cc = lax.axis_index("x"), lax.axis_index("y"), lax.axis_index("c")
        chips = [(x, y), (1 - x, y), (x, 1 - y), (1 - x, 1 - y)]
        sib = (x, y, 1 - cc)
        for a in range(na):
            for k, (cx, cy) in enumerate(chips):
                pltpu.make_async_remote_copy(
                    src_ref=g_refs[a].at[4 * cx + 2 * cy + 1 - cc], dst_ref=recv_refs[a].at[k],
                    send_sem=ssem.at[a], recv_sem=rsem.at[a], device_id=sib, device_id_type=MESH).start()
        for a in range(na):
            pltpu.make_async_remote_copy(src_ref=recv_refs[a], dst_ref=recv_refs[a], send_sem=ssem.at[a],
                                         recv_sem=rsem.at[a], device_id=sib, device_id_type=MESH).wait()

    res = pl.pallas_call(
        body, name="reduce_pair", in_specs=[_ANY] * na, out_specs=[_ANY] * na,
        out_shape=[jax.ShapeDtypeStruct((4,) + g.shape[1:], g.dtype) for g in g8s],
        scratch_shapes=[pltpu.SemaphoreType.DMA((na,)), pltpu.SemaphoreType.DMA((na,))],
    )(*g8s)
    return list(res)


_HBM = pl.BlockSpec(memory_space=pltpu.HBM)
_SEM = pl.BlockSpec(memory_space=pltpu.SEMAPHORE)
_EFFECT = pltpu.SideEffectType.DATAFLOW_SIDE_EFFECTING


def _chip_swap_copies(s_refs, land_refs, ssem, rsem):
    x, y, c = lax.axis_index("x"), lax.axis_index("y"), lax.axis_index("c")
    targets = [(1 - x, y, c), (x, 1 - y, c), (1 - x, 1 - y, c)]
    return [pltpu.make_async_remote_copy(src_ref=s.at[k], dst_ref=d.at[k], send_sem=ssem.at[3 * a + k],
                                         recv_sem=rsem.at[3 * a + k], device_id=targets[k], device_id_type=MESH)
            for a, (s, d) in enumerate(zip(s_refs, land_refs)) for k in range(3)]


def _chip_swap_start(sends):
    na = len(sends)

    def body(*refs):
        cps = _chip_swap_copies(refs[:na], refs[na:2 * na], refs[2 * na], refs[2 * na + 1])
        for cp in cps:
            cp.start()
        token = refs[-1]
        token[...] = jnp.zeros(token.shape, token.dtype)

    bufs = [pltpu.HBM(s.shape, s.dtype) for s in sends]
    res = pl.pallas_call(
        body, name="chip_swap_start",
        out_shape=[pltpu.SemaphoreType.DMA((3 * na,)), pltpu.SemaphoreType.DMA((3 * na,))] + bufs + bufs
        + [jax.ShapeDtypeStruct((8, LANES), F32)],
        in_specs=[_HBM] * (2 * na), out_specs=[_SEM, _SEM] + [_HBM] * (2 * na) + [pl.BlockSpec(memory_space=pltpu.VMEM)],
        input_output_aliases={i: 2 + i for i in range(2 * na)},
        compiler_params=pltpu.CompilerParams(has_side_effects=_EFFECT),
    )(*[pltpu.with_memory_space_constraint(s, pltpu.HBM) for s in sends],
      *[pltpu.with_memory_space_constraint(lax.empty(s.shape, s.dtype), pltpu.HBM) for s in sends])
    return res[0], res[1], list(res[2:2 + na]), list(res[2 + na:2 + 2 * na]), res[-1]


def _chip_swap_wait(ssem, rsem, srcs, lands, after):
    na = len(srcs)

    def body(*refs):
        cps = _chip_swap_copies(refs[:na], refs[na:2 * na], refs[2 * na], refs[2 * na + 1])
        for cp in cps:
            cp.wait_send()
            cp.wait_recv()

    bufs = [pltpu.HBM(s.shape, s.dtype) for s in srcs]
    res = pl.pallas_call(
        body, name="chip_swap_wait", out_shape=bufs + bufs,
        in_specs=[_HBM] * (2 * na) + [_SEM, _SEM, _ANY], out_specs=[_HBM] * (2 * na),
        input_output_aliases={i: i for i in range(2 * na)},
        compiler_params=pltpu.CompilerParams(has_side_effects=_EFFECT),
    )(*srcs, *lands, ssem, rsem, after)
    return list(res[na:])


def _pick_rows(r, c, budget=TILE_BYTES):
    if r * c * 4 <= budget or r % 16:
        return r
    best = 16
    for tr in range(16, r, 16):
        if r % tr == 0 and tr * c * 4 <= budget:
            best = tr
    return best


def _pair_sum(name, idx4, g8, recv4):
    _, r, c = g8.shape
    tr = _pick_rows(r, c, 2 * TILE_BYTES)

    def body(idx_ref, a_ref, b_ref, o0_ref, o3_ref):
        k = pl.program_id(1)
        s = a_ref[...].astype(F32) + b_ref[...].astype(F32)

        @pl.when(k == 0)
        def _():
            o0_ref[...] = s

        @pl.when(k > 0)
        def _():
            o3_ref[...] = s.astype(BF)

    spec = pltpu.PrefetchScalarGridSpec(
        num_scalar_prefetch=1, grid=(r // tr, 4),
        in_specs=[pl.BlockSpec((None, tr, c), lambda i, k, idx: (idx[k], i, 0)),
                  pl.BlockSpec((None, tr, c), lambda i, k, idx: (k, i, 0))],
        out_specs=[pl.BlockSpec((tr, c), lambda i, k, idx: (i, 0)),
                   pl.BlockSpec((None, tr, c), lambda i, k, idx: (jnp.maximum(k - 1, 0), i, 0))])
    return pl.pallas_call(
        body, name=name, grid_spec=spec,
        out_shape=[jax.ShapeDtypeStruct((r, c), F32), jax.ShapeDtypeStruct((3, r, c), BF)],
        compiler_params=_cparams(("arbitrary", "arbitrary")),
    )(idx4, g8, recv4)


def _adamw(w, g, m, v):
    m = ADAM_B1 * m + (1.0 - ADAM_B1) * g
    v = ADAM_B2 * v + (1.0 - ADAM_B2) * jnp.square(g)
    m_hat = m / (1.0 - ADAM_B1 ** ADAM_STEP)
    v_hat = v / (1.0 - ADAM_B2 ** ADAM_STEP)
    delta = -ADAM_LR * (m_hat / (jnp.sqrt(v_hat) + ADAM_EPS) + ADAM_WD * w)
    return delta, m, v


def _adam_sharded(name, idx1, own, recv, w, m, v, after=None):
    r, c = w.shape
    tr = _pick_rows(r, c, 2 * TILE_BYTES)
    nj = recv.shape[0]
    extra = [] if after is None else [after]

    def body(idx_ref, p_ref, r_ref, w_ref, m_ref, v_ref, *rest):
        g_out, d_out, m_out, v_out = rest[-4:]
        g = p_ref[...].astype(F32)
        for k in range(nj):
            g = g + r_ref[k].astype(F32)
        d, mn, vn = _adamw(w_ref[...], g, m_ref[...], v_ref[...])
        g_out[...] = g
        d_out[...] = d
        m_out[...] = mn
        v_out[...] = vn

    row = pl.BlockSpec((tr, c), lambda i, idx: (i, 0))
    spec = pltpu.PrefetchScalarGridSpec(
        num_scalar_prefetch=1, grid=(r // tr,),
        in_specs=[pl.BlockSpec((None, tr, c), lambda i, idx: (idx[0], i, 0)),
                  pl.BlockSpec((nj, tr, c), lambda i, idx: (0, i, 0)), row, row, row]
        + [pl.BlockSpec(e.shape, lambda i, idx: (0, 0)) for e in extra],
        out_specs=[row] * 4)
    return pl.pallas_call(
        body, name=name, grid_spec=spec, out_shape=[jax.ShapeDtypeStruct((r, c), F32)] * 4,
        compiler_params=_cparams(("arbitrary",)),
    )(idx1, own, recv, w, m, v, *extra)


def _repl_rows():
    rows, r = {}, 0
    for name, cols in REPL:
        rows[name] = r
        r += REPL_ROWS.get(name, 1) * ((cols + D - 1) // D)
    return rows


LOSS_ROW = 24


def _pack_replicated(grads, loss_acc, after):
    rows = _repl_rows()
    names = [n for n, _ in REPL]

    def body(*refs):
        o_ref = refs[-1]
        o_ref[...] = jnp.zeros(o_ref.shape, F32)
        o_ref[LOSS_ROW:LOSS_ROW + 1, 0:LANES] = refs[len(names)][...]
        for name, ref in zip(names, refs[:len(names)]):
            r0 = rows[name]
            nr, nc = ref.shape
            if nc <= D:
                o_ref[r0:r0 + nr, 0:nc] = ref[...]
            else:
                for j in range((nc + D - 1) // D):
                    lo, hi = j * D, min(nc, (j + 1) * D)
                    o_ref[r0 + j:r0 + j + 1, 0:hi - lo] = ref[:, lo:hi]

    return pl.pallas_call(body, name="pack_replicated", out_shape=jax.ShapeDtypeStruct((REPL_TOTAL, D), F32),
                          in_specs=[pl.BlockSpec(memory_space=pltpu.VMEM)] * (len(names) + 1) + [_ANY] * len(after),
                          compiler_params=_cparams())(*[grads[n] for n in names], loss_acc, *after)


def _adam_replicated(g8, ws, ms, vs):
    rows = _repl_rows()
    names = [n for n, _ in REPL]
    np_ = len(names)

    def body(*refs):
        g_ref = refs[0]
        w_refs, m_refs, v_refs = refs[1:1 + np_], refs[1 + np_:1 + 2 * np_], refs[1 + 2 * np_:1 + 3 * np_]
        outs = refs[1 + 3 * np_:1 + 7 * np_]
        scr = refs[-1]
        g = g_ref[0]
        for k in range(1, N_DEV):
            g = g + g_ref[k]
        scr[...] = g
        refs[1 + 7 * np_][...] = scr[LOSS_ROW:LOSS_ROW + 1, 0:LANES]
        for i, name in enumerate(names):
            r0 = rows[name]
            nr, nc = w_refs[i].shape
            if nc <= D:
                gi = scr[r0:r0 + nr, 0:nc]
            else:
                parts = []
                for j in range((nc + D - 1) // D):
                    lo, hi = j * D, min(nc, (j + 1) * D)
                    parts.append(scr[r0 + j:r0 + j + 1, 0:hi - lo])
                gi = jnp.concatenate(parts, axis=1)
            d, mn, vn = _adamw(w_refs[i][...], gi, m_refs[i][...], v_refs[i][...])
            outs[i][...] = gi
            outs[np_ + i][...] = d
            outs[2 * np_ + i][...] = mn
            outs[3 * np_ + i][...] = vn

    shp = [jax.ShapeDtypeStruct(w.shape, F32) for w in ws]
    res = pl.pallas_call(body, name="adam_replicated", out_shape=shp * 4 + [jax.ShapeDtypeStruct((1, LANES), F32)],
                         scratch_shapes=[pltpu.VMEM((REPL_TOTAL, D), F32)], compiler_params=_cparams(),
                         )(g8, *ws, *ms, *vs)
    return [dict(zip(names, res[k * np_:(k + 1) * np_])) for k in range(4)], res[-1]


_WEIGHTS = ("attn_pre_norm", "w_in", "hgrn_lb", "hgrn_gnorm", "w_branch_a", "rwkv_mu", "rwkv_w0", "rwkv_w2",
            "rwkv_a0", "rwkv_a2", "rwkv_g2", "rwkv_k_k", "rwkv_k_a", "rwkv_r_k", "rwkv_ln_w", "rwkv_ln_b",
            "w_branch_b", "w_out", "attn_post_norm", "ffn_pre_norm", "w_up", "conv_w", "conv_b", "w_down",
            "ffn_post_norm")
_BIG = ("w_in", "w_up", "w_down", "w_branch_a", "w_branch_b", "w_out")


def _stages():
    one = [D]
    hw = HG_K * HG_PER_STEP
    rw = LANES * RW_PAIRS_PER_STEP
    return dict(
        mixers=_Stage("mixers", _f_mixers, 1, 2 * RW_CHUNK, [False] * 13, [[D] * 7 + [LANES, LANES]], [0],
                      [(hw, HG_K), (1, RW_COLS), (rw, LANES)], [one, one], [BF, BF],
                      kept_shapes=[(2 * RW_KEPT * RW_PAIRS_PER_STEP * 2 * RW_CHUNK, LANES)], f_kept=_f_mixers_kept),
        conv=_Stage("conv", _f_conv, 1, 128, [False, False], [[DFF], [DFF]], [0, 0], [(1, 2 * DFF), (1, 2 * DFF)],
                    [[DFF]], [BF]),
    )


def _cols_to_blocks(w, per):
    return w.reshape(w.shape[0], N_DEV, per).transpose(1, 0, 2)


def _blocks_to_cols(g):
    return g.transpose(1, 0, 2).reshape(g.shape[1], N_DEV * g.shape[2])


def kernel(x, attn_pre_norm, w_in, hgrn_lb, hgrn_gnorm, w_branch_a, rwkv_mu, rwkv_w0, rwkv_w2, rwkv_a0, rwkv_a2, rwkv_g2, rwkv_k_k, rwkv_k_a, rwkv_r_k, rwkv_ln_w, rwkv_ln_b, w_branch_b, w_out, attn_post_norm, ffn_pre_norm, w_up, conv_w, conv_b, w_down, ffn_post_norm, loss_target, m_attn_pre_norm, m_w_in, m_hgrn_lb, m_hgrn_gnorm, m_w_branch_a, m_rwkv_mu, m_rwkv_w0, m_rwkv_w2, m_rwkv_a0, m_rwkv_a2, m_rwkv_g2, m_rwkv_k_k, m_rwkv_k_a, m_rwkv_r_k, m_rwkv_ln_w, m_rwkv_ln_b, m_w_branch_b, m_w_out, m_attn_post_norm, m_ffn_pre_norm, m_w_up, m_conv_w, m_conv_b, m_w_down, m_ffn_post_norm, v_attn_pre_norm, v_w_in, v_hgrn_lb, v_hgrn_gnorm, v_w_branch_a, v_rwkv_mu, v_rwkv_w0, v_rwkv_w2, v_rwkv_a0, v_rwkv_a2, v_rwkv_g2, v_rwkv_k_k, v_rwkv_k_a, v_rwkv_r_k, v_rwkv_ln_w, v_rwkv_ln_b, v_w_branch_b, v_w_out, v_attn_post_norm, v_ffn_pre_norm, v_w_up, v_conv_w, v_conv_b, v_w_down, v_ffn_post_norm):
    w = dict(attn_pre_norm=attn_pre_norm, w_in=w_in, hgrn_lb=hgrn_lb, hgrn_gnorm=hgrn_gnorm, w_branch_a=w_branch_a, rwkv_mu=rwkv_mu, rwkv_w0=rwkv_w0, rwkv_w2=rwkv_w2, rwkv_a0=rwkv_a0, rwkv_a2=rwkv_a2, rwkv_g2=rwkv_g2, rwkv_k_k=rwkv_k_k, rwkv_k_a=rwkv_k_a, rwkv_r_k=rwkv_r_k, rwkv_ln_w=rwkv_ln_w, rwkv_ln_b=rwkv_ln_b, w_branch_b=w_branch_b, w_out=w_out, attn_post_norm=attn_post_norm, ffn_pre_norm=ffn_pre_norm, w_up=w_up, conv_w=conv_w, conv_b=conv_b, w_down=w_down, ffn_post_norm=ffn_post_norm)
    mo = dict(attn_pre_norm=m_attn_pre_norm, w_in=m_w_in, hgrn_lb=m_hgrn_lb, hgrn_gnorm=m_hgrn_gnorm, w_branch_a=m_w_branch_a, rwkv_mu=m_rwkv_mu, rwkv_w0=m_rwkv_w0, rwkv_w2=m_rwkv_w2, rwkv_a0=m_rwkv_a0, rwkv_a2=m_rwkv_a2, rwkv_g2=m_rwkv_g2, rwkv_k_k=m_rwkv_k_k, rwkv_k_a=m_rwkv_k_a, rwkv_r_k=m_rwkv_r_k, rwkv_ln_w=m_rwkv_ln_w, rwkv_ln_b=m_rwkv_ln_b, w_branch_b=m_w_branch_b, w_out=m_w_out, attn_post_norm=m_attn_post_norm, ffn_pre_norm=m_ffn_pre_norm, w_up=m_w_up, conv_w=m_conv_w, conv_b=m_conv_b, w_down=m_w_down, ffn_post_norm=m_ffn_post_norm)
    vo = dict(attn_pre_norm=v_attn_pre_norm, w_in=v_w_in, hgrn_lb=v_hgrn_lb, hgrn_gnorm=v_hgrn_gnorm, w_branch_a=v_w_branch_a, rwkv_mu=v_rwkv_mu, rwkv_w0=v_rwkv_w0, rwkv_w2=v_rwkv_w2, rwkv_a0=v_rwkv_a0, rwkv_a2=v_rwkv_a2, rwkv_g2=v_rwkv_g2, rwkv_k_k=v_rwkv_k_k, rwkv_k_a=v_rwkv_k_a, rwkv_r_k=v_rwkv_r_k, rwkv_ln_w=v_rwkv_ln_w, rwkv_ln_b=v_rwkv_ln_b, w_branch_b=v_w_branch_b, w_out=v_w_out, attn_post_norm=v_attn_post_norm, ffn_pre_norm=v_ffn_pre_norm, w_up=v_w_up, conv_w=v_conv_w, conv_b=v_conv_b, w_down=v_w_down, ffn_post_norm=v_ffn_post_norm)

    t = x.shape[1]
    x2 = x.reshape(t, D)
    tgt = loss_target.reshape(t, D)
    st = _stages()

    me = 4 * lax.axis_index("x") + 2 * lax.axis_index("y") + lax.axis_index("c")
    small = jnp.concatenate([rwkv_w2[0], rwkv_a2[0], rwkv_g2[0]], axis=0).astype(BF)
    g_in, g_small = _all_gather("gather_weights", [w_in[0].T.astype(BF), small])
    fw_in_t = g_in.reshape(IN_COLS, D)
    z64 = jnp.zeros((64, D), BF)
    w2p = jnp.concatenate([_blocks_to_cols(g_small[:, 0:64]), z64], axis=0)
    a2p = jnp.concatenate([z64, _blocks_to_cols(g_small[:, 64:128])], axis=0)
    g2f = _blocks_to_cols(g_small[:, 128:256])
    conv_bits = jnp.pad(lax.bitcast_convert_type(conv_w[0], BF).reshape(3, 2 * 704), ((0, 29), (0, 0)))
    late = [w_up[0].T.astype(BF)] + [w[k][0].astype(BF) for k in _BIG[2:]] + [conv_bits]
    late_gather = _Exchange("gather2", late)
    r_k = rwkv_r_k.reshape(1, D)

    xn, z = _norm_in_proj(x2, attn_pre_norm, fw_in_t, 512, 4736)
    mix_par = [hgrn_lb, hgrn_gnorm, rwkv_mu, rwkv_w0, w2p, rwkv_a0, a2p, g2f, rwkv_k_k, rwkv_k_a,
               rwkv_ln_w, rwkv_ln_b, r_k]
    mix_in = [z]
    (o_a, o_b), mix_saved = _stage_fwd(st["mixers"], t, mix_par, mix_in, hook=late_gather)
    gl = [lax.dynamic_update_slice(g, own[None], (me, 0, 0)) for g, own in zip(late_gather.results, late)]
    fw_up_t = gl[0].reshape(2 * DFF, D)
    fw_down = gl[1].reshape(DFF, D)
    fw_a, fw_b, fw_out = (g.reshape(D, D) for g in gl[2:5])
    conv_full = _blocks_to_cols(lax.bitcast_convert_type(gl[5][:, :3].reshape(N_DEV, 3, 704, 2), F32))
    y_a, y_b, merged, mix, h1, xn2 = _merge_out_post(z, o_a, o_b, fw_a, fw_b, fw_out, x2, attn_post_norm,
                                                     ffn_pre_norm, 512)
    conv_par = [conv_full, conv_b]
    hu_g, hu_v, act, before1, before2 = _up_conv(xn2, fw_up_t, conv_full, conv_b, 512, min(st["conv"].tm, t))
    conv_saved = [before1[None], before2[None]]

    loss_acc, d_ffn_post, dh1, dff = _down_loss(act, fw_down, ffn_post_norm, h1, tgt, 512)
    dact = _mm("d_act", dff, fw_down, "nt", BF, tm=1024, tn=1408)
    dw_down = _mm("dw_down", act, dff, "tn", BF, tm=1408, tn=512)
    (dcw, dcb), dhu = _stage_bwd(st["conv"], t, conv_par, [hu_g, hu_v], conv_saved, [[dact]], [BF, BF])
    dw_up_t = _mm_cols_tn("dw_up", dhu, xn2, BF, 1408)
    d_post, d_pre2, dx_a, dmix = _dxn2_post1_bwd(dhu, fw_up_t, x2, mix, dh1, attn_post_norm, ffn_pre_norm, 512)
    dga, dgb, dy_a, dy_b, do_a, do_b = _dmerged_merge_bwd(dmix, fw_out, fw_a, fw_b, z, y_a, y_b, 512)
    dw_a, dw_b, dw_out = _mm_multi("dw_branches", [(o_a, dy_a), (o_b, dy_b), (merged, dmix)], "tn", BF)
    early = [dw_up_t.reshape(N_DEV, 704, D), dw_down.reshape(N_DEV, 352, D), dw_a.reshape(N_DEV, 128, D),
             dw_b.reshape(N_DEV, 128, D), dw_out.reshape(N_DEV, 128, D), _cols_to_blocks(dcw.astype(BF), 704)]
    early_scatter = _Exchange("scatter", early)
    mix_dp, dz_hr = _stage_bwd(st["mixers"], t, mix_par, mix_in, mix_saved, [[do_a], [do_b]], [BF],
                               hook=early_scatter)
    d_lb, d_gn, d_mu, d_w0, d_w2p, d_a0, d_a2p, d_g2, d_kk, d_ka, d_lnw, d_lnb, d_rk = mix_dp
    dz = dz_hr + [dga, dgb]
    dw_in_t = _mm_cols_tn("dw_in", dz, xn, BF, 256)

    ax, ay, ac = lax.axis_index("x"), lax.axis_index("y"), lax.axis_index("c")
    idx4 = jnp.stack([4 * cx + 2 * cy + ac for cx, cy in ((ax, ay), (1 - ax, ay), (ax, 1 - ay), (1 - ax, 1 - ay))])
    idx4 = idx4.astype(jnp.int32)
    idx_me, idx_0 = idx4[0:1], jnp.zeros((1,), jnp.int32)
    d_small = jnp.concatenate([d_w2p[:64], d_a2p[64:], d_g2], axis=0).astype(BF)
    g8s = [dw_in_t.reshape(N_DEV, 1184, D), _cols_to_blocks(d_small, LANES)]
    recv4s = _reduce_pair(g8s)
    sums = [_pair_sum("pair_sum_" + n, idx4, g, r) for n, g, r in zip(("w_in", "small"), g8s, recv4s)]
    swap_ssem, swap_rsem, swap_srcs, swap_lands, token = _chip_swap_start([s[1] for s in sums])
    d_pre1, dx = _dxn_pre1_bwd(dz, fw_in_t, x2, dx_a, attn_pre_norm, 256, token)
    grad_x = dx.reshape(x.shape)

    sh_out = [dict() for _ in range(4)]
    done = []
    for n, own, recv in zip(_BIG[1:] + ("conv_w",), early, early_scatter.results):
        tr = (lambda a: a.T) if n == "w_up" else (lambda a: a)
        res = _adam_sharded("adam_" + n, idx_me, own, recv, *[tr(src[n][0]) for src in (w, mo, vo)], after=token)
        done.append(res[0])
        for kind in range(4):
            sh_out[kind][n] = tr(res[kind])[None]

    rg = dict(attn_pre_norm=d_pre1, hgrn_lb=d_lb, hgrn_gnorm=d_gn, rwkv_mu=d_mu, rwkv_w0=d_w0, rwkv_a0=d_a0,
              rwkv_k_k=d_kk, rwkv_k_a=d_ka, rwkv_r_k=d_rk, rwkv_ln_w=d_lnw, rwkv_ln_b=d_lnb, attn_post_norm=d_post,
              ffn_pre_norm=d_pre2, conv_b=dcb, ffn_post_norm=d_ffn_post)
    g8 = _all_gather_small("gather_small_grads", _pack_replicated(rg, loss_acc, done))
    rnames = [n for n, _ in REPL]
    flat = lambda src: [src[n].reshape(1, D) if n == "rwkv_r_k" else src[n] for n in rnames]
    rp_out, loss_row = _adam_replicated(g8, flat(w), flat(mo), flat(vo))
    loss = loss_row[0, 0]
    recv3s = _chip_swap_wait(swap_ssem, swap_rsem, swap_srcs, swap_lands, rp_out[0]["attn_pre_norm"])
    for kind in range(4):
        rp_out[kind]["rwkv_r_k"] = rp_out[kind]["rwkv_r_k"].reshape(rwkv_r_k.shape)

    def small_of(src):
        return jnp.concatenate([src["rwkv_w2"][0], src["rwkv_a2"][0], src["rwkv_g2"][0]], axis=0)

    res = _adam_sharded("adam_w_in", idx_0, sums[0][0][None], recv3s[0], *[src["w_in"][0].T for src in (w, mo, vo)])
    res_s = _adam_sharded("adam_small", idx_0, sums[1][0][None], recv3s[1], *[small_of(src) for src in (w, mo, vo)])
    for kind in range(4):
        sh_out[kind]["w_in"] = res[kind].T[None]
        sh_out[kind]["rwkv_w2"] = res_s[kind][0:64][None]
        sh_out[kind]["rwkv_a2"] = res_s[kind][64:128][None]
        sh_out[kind]["rwkv_g2"] = res_s[kind][128:256][None]

    outs = [loss, grad_x]
    for kind in range(4):
        for name in _WEIGHTS:
            outs.append(sh_out[kind][name] if name in sh_out[kind] else rp_out[kind][name])
    return tuple(outs)
```

```python
import functools

import jax
import jax.numpy as jnp
from jax import lax
from jax.experimental import pallas as pl
from jax.experimental.pallas import tpu as pltpu

F32 = jnp.float32
BF = jnp.bfloat16
MESH = pl.DeviceIdType.MESH

D = 1024
HG_HEADS = 8
HG_K = 128
HG_CHUNK = 32
HG_SCALE = HG_K ** -0.5
HG_PER_STEP = 8
RW_HEADS = 16
RW_N = 64
RW_CHUNK = 64
RW_PAIRS_PER_STEP = 8
DFF = 2816
IN_COLS = 9472
RW_COLS = 3328
EPS = 1e-6
GN_EPS = 1e-5 * RW_N
ADAM_LR = 0.001
ADAM_B1 = 0.9
ADAM_B2 = 0.999
ADAM_EPS = 1e-08
ADAM_WD = 0.01
ADAM_STEP = 10
N_DEV = 8
LANES = 128
SUBLANES = 8
VMEM_LIMIT = 56 * 1024 * 1024
TILE_BYTES = 1280 * 1024

REPL = (("attn_pre_norm", 1024), ("hgrn_lb", 1024), ("hgrn_gnorm", 1024), ("rwkv_mu", 3328), ("rwkv_w0", 1024),
        ("rwkv_a0", 1024), ("rwkv_k_k", 1024), ("rwkv_k_a", 1024), ("rwkv_r_k", 1024), ("rwkv_ln_w", 1024),
        ("rwkv_ln_b", 1024), ("attn_post_norm", 1024), ("ffn_pre_norm", 1024), ("conv_b", 5632), ("ffn_post_norm", 1024))
REPL_ROWS = {"hgrn_lb": 2}
REPL_TOTAL = 32


def _cparams(sem=None, **kw):
    return pltpu.CompilerParams(dimension_semantics=sem, vmem_limit_bytes=VMEM_LIMIT, **kw)


_DN = {"nn": ((1,), (0,)), "nt": ((1,), (1,)), "tn": ((0,), (0,))}


def _raw_dot(a, b, mode):
    return lax.dot_general(a.astype(BF), b.astype(BF), (_DN[mode], ((), ())), preferred_element_type=F32)


@functools.partial(jax.custom_vjp, nondiff_argnums=(2,))
def _dot(a, b, mode):
    return _raw_dot(a, b, mode)


def _dot_fwd(a, b, mode):
    return _raw_dot(a, b, mode), (a, b)


def _dot_bwd(mode, res, g):
    a, b = res
    if mode == "nn":
        return _dot(g, b, "nt"), _dot(a, g, "tn")
    if mode == "nt":
        return _dot(g, b, "nn"), _dot(g, a, "tn")
    return _dot(b, g, "nt"), _dot(a, g, "nn")


_dot.defvjp(_dot_fwd, _dot_bwd)


def _bf_pieces(x, n):
    out, r = [], x
    for i in range(n):
        p = r.astype(BF)
        out.append(p)
        if i + 1 < n:
            r = r - p.astype(F32)
    return out


def _raw_split_dot(x, e, mode, n, x_left):
    eb = e.astype(BF)
    acc = None
    for p in _bf_pieces(x, n):
        ops = (p, eb) if x_left else (eb, p)
        t = lax.dot_general(*ops, (_DN[mode], ((), ())), preferred_element_type=F32)
        acc = t if acc is None else acc + t
    return acc


def _raw_headsum(x):
    t = x.shape[0]
    i = lax.broadcasted_iota(jnp.int32, (LANES, LANES), 0)
    j = lax.broadcasted_iota(jnp.int32, (LANES, LANES), 1)
    same = jnp.where((i >= RW_N) == (j >= RW_N), 1.0, 0.0).astype(F32)
    groups = x.shape[1] // LANES
    rows = jnp.concatenate([x[:, q * LANES:(q + 1) * LANES] for q in range(groups)], axis=0)
    s = _raw_split_dot(rows, same, "nn", 2, True)
    return jnp.concatenate([s[q * t:(q + 1) * t] for q in range(groups)], axis=1)


@jax.custom_vjp
def _headsum(x):
    return _raw_headsum(x)


def _headsum_fwd(x):
    return _raw_headsum(x), None


def _headsum_bwd(_, g):
    return (_raw_headsum(g),)


_headsum.defvjp(_headsum_fwd, _headsum_bwd)


@functools.partial(jax.custom_vjp, nondiff_argnums=(2,))
def _tdot(tri, x, n):
    return _raw_split_dot(x, tri, "nn", n, False)


def _tdot_fwd(tri, x, n):
    return _raw_split_dot(x, tri, "nn", n, False), tri


def _tdot_bwd(n, tri, g):
    return jnp.zeros_like(tri), _raw_split_dot(g, tri, "tn", n, False)


_tdot.defvjp(_tdot_fwd, _tdot_bwd)


def _row(x, i):
    r = lax.broadcasted_iota(jnp.int32, x.shape, 0)
    return jnp.sum(jnp.where(r == i, x, 0.0), axis=0, keepdims=True)


def _shift_down(x, prev):
    t = x.shape[0]

    @jax.custom_vjp
    def sh(x, prev):
        r = lax.broadcasted_iota(jnp.int32, x.shape, 0)
        return jnp.where(r == 0, prev, pltpu.roll(x, 1, 0))

    def fwd(x, prev):
        return sh(x, prev), None

    def bwd(_, g):
        r = lax.broadcasted_iota(jnp.int32, g.shape, 0)
        dx = jnp.where(r == t - 1, 0.0, pltpu.roll(g, t - 1, 0))
        return dx, jnp.sum(jnp.where(r == 0, g, 0.0), axis=0, keepdims=True)

    sh.defvjp(fwd, bwd)
    return sh(x, prev)


def _sigmoid(x):
    return jax.nn.sigmoid(x)


def _silu(x):
    return x * jax.nn.sigmoid(x)


def _softplus(x):
    return jnp.maximum(x, 0.0) + jnp.log(1.0 + jnp.exp(-jnp.abs(x)))


def _rms(x, g):
    return (x * lax.rsqrt(jnp.mean(x * x, axis=-1, keepdims=True) + EPS)) * g


def _tril(c):
    r = lax.broadcasted_iota(jnp.int32, (c, c), 0)
    cc = lax.broadcasted_iota(jnp.int32, (c, c), 1)
    return cc <= r


def _f_pre1_residual(ps, xs, cs):
    return [_rms(xs[0], ps[0]), xs[0]], []


def _f_hgrn(ps, xs, cs):
    lbraw, gn = ps
    hq, hf, hi, hg = xs
    hd = range(HG_PER_STEP)
    st = [cs[0][p * HG_K:(p + 1) * HG_K] for p in hd]
    l0, l1 = _row(lbraw, 0), _row(lbraw, 1)
    m = jnp.maximum(l0, l1)
    e0, e1 = jnp.exp(l0 - m), jnp.exp(l1 - m)
    lb = e0 / (e0 + e1)
    q = _silu(hq) * HG_SCALE
    f = lb + (1.0 - lb) * _sigmoid(hf)
    kh = 1.0 - f
    gl = jnp.log(f)
    c = HG_CHUNK
    low = _tril(c)
    tri = jnp.where(low, 1.0, 0.0).astype(F32)
    outs = []
    for i in range(hq.shape[0] // c):
        rows = slice(i * c, (i + 1) * c)
        b = _tdot(tri, gl[rows], 3)
        bref = _row(b, c // 2 - 1)
        blast = _row(b, c - 1)
        qi = q[rows] * jnp.exp(b - bref)
        ki = kh[rows] * jnp.exp(bref - b)
        qd = q[rows] * jnp.exp(b)
        kd = kh[rows] * jnp.exp(blast - b)
        dec = jnp.exp(blast)
        sl = [slice(p * HG_K, (p + 1) * HG_K) for p in hd]
        sc = [jnp.where(low, _dot(qi[:, sl[p]], ki[:, sl[p]], "nt"), 0.0) for p in hd]
        o = [_dot(sc[p], hi[rows, sl[p]], "nn") + _dot(qd[:, sl[p]], st[p], "nt") for p in hd]
        u = [_dot(hi[rows, sl[p]], kd[:, sl[p]], "tn") for p in hd]
        st = [dec[:, sl[p]] * st[p] + u[p] for p in hd]
        outs.append(jnp.concatenate(o, axis=1) if len(o) > 1 else o[0])
    o = outs[0] if len(outs) == 1 else jnp.concatenate(outs, axis=0)
    on = []
    for p in hd:
        op = o[:, p * HG_K:(p + 1) * HG_K]
        on.append(op * lax.rsqrt(jnp.mean(op * op, axis=-1, keepdims=True) + EPS))
    o = jnp.concatenate(on, axis=1) if len(on) > 1 else on[0]
    o = o * gn
    return [o * _silu(hg)], [jnp.concatenate(st, axis=0) if len(st) > 1 else st[0]]


_RW_OFFS = (0, 1024, 2048, 3072, 3200, 3328)


def _f_rwpre(ps, xs, cs):
    mu, w0, w2p, a0, a2p, g2, k_k, k_a = ps
    (prev,) = cs
    t = xs[0].shape[0]
    zs = []
    for i, z in enumerate(xs):
        lo, hi = _RW_OFFS[i], _RW_OFFS[i + 1]
        zs.append(z + mu[:, lo:hi] * (_shift_down(z, prev[:, lo:hi]) - z))
    rr, kr, vr, wa, gz = zs
    w_log = -_softplus(-(w0 + _dot(jnp.tanh(wa), w2p, "nn"))) - 0.5
    lw = -jnp.exp(w_log)
    a = _sigmoid(a0 + _dot(wa, a2p, "nn"))
    g = _dot(_sigmoid(gz), g2, "nn")
    kkr = kr * k_k
    kk = kkr / jnp.maximum(jnp.sqrt(_headsum(kkr * kkr)), 1e-12)
    k2 = kr * (1.0 + (a - 1.0) * k_a)
    newprev = jnp.concatenate([_row(z, t - 1) for z in xs], axis=1)
    return [rr, lw, k2, vr, -kk, kk * a, g], [newprev]


def _raw_inverses(ls):
    n = ls[0].shape[0]
    r = lax.broadcasted_iota(jnp.int32, (n, n), 0)
    c = lax.broadcasted_iota(jnp.int32, (n, n), 1)
    eye = jnp.where(r == c, 1.0, 0.0).astype(F32)
    tinv = [eye + l for l in ls]
    pw = ls
    for _ in range(5):
        pw = [_raw_dot(p, p, "nn") for p in pw]
        tinv = [t + _raw_dot(t, p, "nn") for t, p in zip(tinv, pw)]
    return tinv


@jax.custom_vjp
def _unit_lower_inverses(ls):
    return _raw_inverses(ls)


def _inverses_fwd(ls):
    tinv = _raw_inverses(ls)
    return tinv, tinv


def _inverses_bwd(tinv, gs):
    return ([_raw_dot(_raw_dot(t, g, "tn"), t, "nt") for t, g in zip(tinv, gs)],)


_unit_lower_inverses.defvjp(_inverses_fwd, _inverses_bwd)


@jax.custom_vjp
def _known_inverses(ls, tinv):
    return tinv


def _known_fwd(ls, tinv):
    return tinv, tinv


def _known_bwd(tinv, gs):
    return [_raw_dot(_raw_dot(t, g, "tn"), t, "nt") for t, g in zip(tinv, gs)], [jnp.zeros_like(t) for t in tinv]


_known_inverses.defvjp(_known_fwd, _known_bwd)


@jax.custom_vjp
def _use_kept(computed, kept):
    return kept


def _use_kept_fwd(computed, kept):
    return kept, None


def _use_kept_bwd(_, g):
    return g, jax.tree.map(jnp.zeros_like, g)


_use_kept.defvjp(_use_kept_fwd, _use_kept_bwd)

RW_KEPT = 5


def _f_rwscan(ps, xs, cs, kept=None):
    state = cs[0]
    ys, keep = [], []
    n = 2 * RW_CHUNK
    per_chunk = RW_KEPT * RW_PAIRS_PER_STEP * n
    for i in range(xs[0].shape[0] // RW_CHUNK):
        known = None
        if kept is not None:
            known = [[kept[i * per_chunk + (q * RW_PAIRS_PER_STEP + p) * n:
                           i * per_chunk + (q * RW_PAIRS_PER_STEP + p + 1) * n] for p in range(RW_PAIRS_PER_STEP)]
                     for q in range(RW_KEPT)]
        y, state, mats = _rwkv_chunk([x[i * RW_CHUNK:(i + 1) * RW_CHUNK] for x in xs], state, known)
        ys.append(y)
        keep += [m for group in mats for m in group]
    return [ys[0] if len(ys) == 1 else jnp.concatenate(ys, axis=0)], [state], jnp.concatenate(keep, axis=0)


def _rwkv_chunk(xs, state, known=None):
    npair = RW_PAIRS_PER_STEP
    pr = range(npair)
    r, lw, k, v, av, bv = [[x[:, p * LANES:(p + 1) * LANES] for p in pr] for x in xs]
    sv = [state[p * LANES:(p + 1) * LANES] for p in pr]
    c = RW_CHUNK
    n = 2 * c
    tri = jnp.where(_tril(c), 1.0, 0.0).astype(F32)
    cl = [_tdot(tri, lw[p], 3) for p in pr]
    cl_last = [_row(cl[p], c - 1) for p in pr]
    lane = lax.broadcasted_iota(jnp.int32, (c, LANES), 1)
    h0 = lane < RW_N

    def stack(x):
        return jnp.concatenate([jnp.where(h0, x, 0.0), jnp.where(h0, 0.0, x)], axis=0)

    am = [stack(av[p] * jnp.exp(cl[p] - lw[p])) for p in pr]
    bm = [stack(bv[p] * jnp.exp(-cl[p])) for p in pr]
    km = [stack(k[p] * jnp.exp(-cl[p])) for p in pr]
    rm = [stack(r[p] * jnp.exp(cl[p])) for p in pr]
    vm = [stack(v[p]) for p in pr]
    rn = lax.broadcasted_iota(jnp.int32, (n, n), 0)
    cn = lax.broadcasted_iota(jnp.int32, (n, n), 1)
    blk = (rn >= c) == (cn >= c)
    strict = blk & (cn < rn)
    incl = blk & (cn <= rn)
    lab = [jnp.where(strict, _dot(am[p], bm[p], "nt"), 0.0) for p in pr]
    lak = [jnp.where(strict, _dot(am[p], km[p], "nt"), 0.0) for p in pr]
    wrb = [jnp.where(incl, _dot(rm[p], bm[p], "nt"), 0.0) for p in pr]
    wrk = [jnp.where(incl, _dot(rm[p], km[p], "nt"), 0.0) for p in pr]
    if known is None:
        tinv = _unit_lower_inverses(lab)
    else:
        tinv = _known_inverses(lab, known[0])
        lak, wrb, wrk = _use_kept(lak, known[1]), _use_kept(wrb, known[2]), _use_kept(wrk, known[3])
    rhs = [_dot(am[p], sv[p], "nt") + _dot(lak[p], vm[p], "nn") for p in pr]
    um = [_dot(tinv[p], rhs[p], "nn") for p in pr]
    if known is not None:
        um = _use_kept(um, known[4])
    ym = [_dot(rm[p], sv[p], "nt") + _dot(wrb[p], um[p], "nn") + _dot(wrk[p], vm[p], "nn") for p in pr]
    sn = [(sv[p] + _dot(um[p], bm[p], "tn") + _dot(vm[p], km[p], "tn")) * jnp.exp(cl_last[p]) for p in pr]
    ys = [ym[p][:c] + ym[p][c:] for p in pr]
    return jnp.concatenate(ys, axis=1), jnp.concatenate(sn, axis=0), [tinv, lak, wrb, wrk, um]


def _f_mixers(ps, xs, cs):
    return _mixers(ps, xs, cs, None)


def _f_mixers_kept(ps, xs, cs, kept):
    return _mixers(ps, xs, cs, kept[0])[:2]


def _mixers(ps, xs, cs, kept):
    oa, st = _f_hgrn(ps[:2], xs[:4], cs[:1])
    (r, lw, k, v, av, bv, g), prev = _f_rwpre(ps[2:10], xs[4:], cs[1:2])
    y, sv, keep = _f_rwscan([], [r, lw, k, v, av, bv], cs[2:], kept)
    ob, _ = _f_rwpost(ps[10:], y + [r, k, v, g], [])
    return oa + ob, st + prev + sv, [keep]


def _f_rwpost(ps, xs, cs):
    ln_w, ln_b, r_k = ps
    y, r, k, v, g = xs
    inv_n = 1.0 / RW_N
    yc = y - _headsum(y) * inv_n
    var = _headsum(yc * yc) * inv_n
    yn = yc * lax.rsqrt(var + GN_EPS)
    yn = yn * ln_w + ln_b
    bonus = _headsum(r * k * r_k) * v
    return [(yn + bonus) * g], []


def _f_merge(ps, xs, cs):
    ga, gb, ya, yb = xs
    return [_sigmoid(ga) * ya + _sigmoid(gb) * yb], []


def _f_post1(ps, xs, cs):
    x, mix = xs
    h1 = x + _rms(mix, ps[0])
    return [h1, _rms(h1, ps[1])], []


def _f_conv(ps, xs, cs):
    cw, cb = ps
    p1, p2 = cs
    w0, w1, w2 = _row(cw, 0), _row(cw, 1), _row(cw, 2)
    t = xs[0].shape[0]
    hc = []
    for i, x in enumerate(xs):
        sl = slice(i * DFF, (i + 1) * DFF)
        s1 = _shift_down(x, p1[:, sl])
        s2 = _shift_down(s1, p2[:, sl])
        hc.append(cb[:, sl] + w0[:, sl] * s2 + w1[:, sl] * s1 + w2[:, sl] * x)
    n1 = jnp.concatenate([_row(x, t - 1) for x in xs], axis=1)
    n2 = jnp.concatenate([_row(x, t - 2) for x in xs], axis=1)
    return [_silu(hc[0]) * hc[1]], [n1, n2]


class _Stage:
    def __init__(self, name, f, g, tm, par_per_g, in_pieces, in_offs, carry_shapes, out_pieces, out_dtypes,
                 kept_shapes=(), f_kept=None):
        self.name, self.f, self.g, self.tm = name, f, g, tm
        self.par_per_g, self.in_pieces, self.in_offs = par_per_g, in_pieces, in_offs
        self.carry_shapes, self.out_pieces, self.out_dtypes = carry_shapes, out_pieces, out_dtypes
        self.kept_shapes, self.f_kept = list(kept_shapes), f_kept


def _par_spec(arr, per_g, g):
    r, c = arr.shape
    if per_g:
        return pl.BlockSpec((r, c // g), lambda gi, ni: (0, gi))
    return pl.BlockSpec((r, c), lambda gi, ni: (0, 0))


def _row_spec(tm, width, off, n, rev):
    if rev:
        return pl.BlockSpec((tm, width), lambda gi, ni: (n - 1 - ni, off + gi))
    return pl.BlockSpec((tm, width), lambda gi, ni: (ni, off + gi))


def _carry_spec(shape, n, rev):
    if rev:
        return pl.BlockSpec((None, None) + shape, lambda gi, ni: (gi, n - 1 - ni, 0, 0))
    return pl.BlockSpec((None, None) + shape, lambda gi, ni: (gi, ni, 0, 0))


def _load_pieces(refs, pieces_list):
    out = []
    for ref, pieces in zip(refs, pieces_list):
        o = 0
        for w in pieces:
            out.append(ref[:, o:o + w].astype(F32))
            o += w
    return out


def _store_pieces(refs, pieces_list, vals):
    k = 0
    for ref, pieces in zip(refs, pieces_list):
        o = 0
        for w in pieces:
            ref[:, o:o + w] = vals[k].astype(ref.dtype)
            k += 1
            o += w


_ANY = pl.BlockSpec(memory_space=pl.ANY)


class _Exchange:
    def __init__(self, kind, arrs):
        self.kind, self.arrs, self.results = kind, list(arrs), None
        if kind == "scatter":
            self.out_shape = [jax.ShapeDtypeStruct((N_DEV - 1,) + a.shape[1:], a.dtype) for a in self.arrs]
        else:
            self.out_shape = [jax.ShapeDtypeStruct((N_DEV,) + a.shape, a.dtype) for a in self.arrs]
        self.nsem = (N_DEV if kind == "gather2" else N_DEV - 1) * len(self.arrs)

    def copies(self, in_refs, out_refs, ssem, rsem):
        x, y, c = lax.axis_index("x"), lax.axis_index("y"), lax.axis_index("c")
        me = 4 * x + 2 * y + c
        cps = []
        for a, (i_ref, o_ref) in enumerate(zip(in_refs, out_refs)):
            for j in range(1, N_DEV):
                px = 1 - x if j & 4 else x
                py = 1 - y if j & 2 else y
                pc = 1 - c if j & 1 else c
                if self.kind == "gather":
                    src, dst = i_ref, o_ref.at[me]
                else:
                    src, dst = i_ref.at[4 * px + 2 * py + pc], o_ref.at[j - 1]
                s = (N_DEV - 1) * a + j - 1
                cps.append(pltpu.make_async_remote_copy(src_ref=src, dst_ref=dst, send_sem=ssem.at[s],
                                                        recv_sem=rsem.at[s], device_id=(px, py, pc),
                                                        device_id_type=MESH))
        return cps

    def run(self, step, total, in_refs, out_refs, ssem, rsem):
        if self.kind == "gather2":
            return self.run_two_level(step, total, in_refs, out_refs, ssem, rsem)

        @pl.when(step == 0)
        def _():
            for cp in self.copies(in_refs, out_refs, ssem, rsem):
                cp.start()

        @pl.when(step == total - 1)
        def _():
            for cp in self.copies(in_refs, out_refs, ssem, rsem):
                cp.wait()

    def run_two_level(self, step, total, in_refs, out_refs, ssem, rsem):
        x, y, c = lax.axis_index("x"), lax.axis_index("y"), lax.axis_index("c")
        sibling, xn, yn = (x, y, 1 - c), (1 - x, y, c), (x, 1 - y, c)
        arrs = range(len(in_refs))
        ns = N_DEV

        def num(px, py, pc):
            return 4 * px + 2 * py + pc

        def copy(a, k, to, src, dst):
            return pltpu.make_async_remote_copy(src_ref=src, dst_ref=dst, send_sem=ssem.at[ns * a + k],
                                                recv_sem=rsem.at[ns * a + k], device_id=to, device_id_type=MESH)

        def blk(a, b):
            return out_refs[a].at[b]

        def half(a, b, second):
            h = self.arrs[a].shape[0] // 2
            return out_refs[a].at[b, pl.ds(h if second else 0, h)]

        bx, by, bd = num(1 - x, y, c), num(x, 1 - y, c), num(1 - x, 1 - y, c)

        def firsts(a):
            own = blk(a, num(x, y, c))
            return [copy(a, 0, sibling, in_refs[a], own), copy(a, 1, xn, in_refs[a], own),
                    copy(a, 2, yn, in_refs[a], own)]

        def seconds(a):
            return [copy(a, 3, yn, half(a, bx, False), half(a, bx, False)), copy(a, 5, sibling, blk(a, bx), blk(a, bx)),
                    copy(a, 4, xn, half(a, by, True), half(a, by, True)), copy(a, 6, sibling, blk(a, by), blk(a, by))]

        def third(a):
            return copy(a, 7, sibling, blk(a, bd), blk(a, bd))

        @pl.when(step == 0)
        def _():
            for a in arrs:
                for cp in firsts(a):
                    cp.start()

        @pl.when(step == total // 2)
        def _():
            for a in arrs:
                copy(a, 1, xn, blk(a, bx), blk(a, bx)).wait_recv()
                copy(a, 2, yn, blk(a, by), blk(a, by)).wait_recv()
                for cp in seconds(a):
                    cp.start()

        @pl.when(step == (4 * total) // 5)
        def _():
            for a in arrs:
                copy(a, 3, yn, half(a, bd, False), half(a, bd, False)).wait_recv()
                copy(a, 4, xn, half(a, bd, True), half(a, bd, True)).wait_recv()
                third(a).start()

        @pl.when(step == total - 1)
        def _():
            for a in arrs:
                for k, b in ((0, num(x, y, 1 - c)), (5, num(1 - x, y, 1 - c)), (6, num(x, 1 - y, 1 - c)),
                             (7, num(1 - x, 1 - y, 1 - c))):
                    copy(a, k, sibling, blk(a, b), blk(a, b)).wait_recv()
                for cp in firsts(a) + seconds(a) + [third(a)]:
                    cp.wait_send()


def _hook_specs(hook):
    if hook is None:
        return [], [], [], []
    na = len(hook.arrs)
    sems = [pltpu.SemaphoreType.DMA((hook.nsem,)), pltpu.SemaphoreType.DMA((hook.nsem,))]
    return [_ANY] * na, [_ANY] * na, hook.out_shape, sems


def _stage_fwd(st, t, params, inputs, hook=None):
    g, tm = st.g, min(st.tm, t)
    n = t // tm
    npar, nin, ncar, nout = len(params), len(inputs), len(st.carry_shapes), len(st.out_pieces)
    nk = len(st.kept_shapes)
    h_in, h_out, h_shape, h_sems = _hook_specs(hook)
    nh = len(h_in)

    def body(*refs):
        p_refs = refs[:npar]
        x_refs = refs[npar:npar + nin]
        hi_refs = refs[npar + nin:npar + nin + nh]
        o = npar + nin + nh
        o_refs = refs[o:o + nout]
        s_refs = refs[o + nout:o + nout + ncar]
        k_refs = refs[o + nout + ncar:o + nout + ncar + nk]
        o += nout + ncar + nk
        ho_refs = refs[o:o + nh]
        c_scr = refs[o + nh:o + nh + ncar]
        gi, ni = pl.program_id(0), pl.program_id(1)
        if hook is not None:
            step = gi * n + ni
            hook.run(step, g * n, hi_refs, ho_refs, *refs[-2:])

        @pl.when(ni == 0)
        def _():
            for c in c_scr:
                c[...] = jnp.zeros(c.shape, F32)

        ps = [r[...].astype(F32) for r in p_refs]
        xs = _load_pieces(x_refs, st.in_pieces)
        cs = [c[...] for c in c_scr]
        for s, c in zip(s_refs, cs):
            s[...] = c
        res = st.f(ps, xs, cs)
        outs, ncs = res[0], res[1]
        _store_pieces(o_refs, st.out_pieces, outs)
        for c, v in zip(c_scr, ncs):
            c[...] = v
        for kr, kv in zip(k_refs, res[2] if nk else []):
            kr[...] = kv.astype(kr.dtype)

    in_specs = [_par_spec(p, pg, g) for p, pg in zip(params, st.par_per_g)]
    in_specs += [_row_spec(tm, sum(pc), off, n, False) for pc, off in zip(st.in_pieces, st.in_offs)]
    out_specs = [_row_spec(tm, sum(pc), 0, n, False) for pc in st.out_pieces]
    out_specs += [_carry_spec(s, n, False) for s in st.carry_shapes]
    out_specs += [pl.BlockSpec(s, lambda gi, ni: (ni, 0)) for s in st.kept_shapes]
    out_shape = [jax.ShapeDtypeStruct((t, g * sum(pc)), dt) for pc, dt in zip(st.out_pieces, st.out_dtypes)]
    out_shape += [jax.ShapeDtypeStruct((g, n) + s, F32) for s in st.carry_shapes]
    out_shape += [jax.ShapeDtypeStruct((n * s[0], s[1]), BF) for s in st.kept_shapes]
    res = pl.pallas_call(
        body, name=st.name + "_fwd", grid=(g, n), in_specs=in_specs + h_in, out_specs=out_specs + h_out,
        out_shape=out_shape + h_shape,
        scratch_shapes=[pltpu.VMEM(s, F32) for s in st.carry_shapes] + h_sems,
        compiler_params=_cparams(("arbitrary", "arbitrary")),
    )(*params, *inputs, *(hook.arrs if hook else []))
    if hook is not None:
        hook.results = list(res[nout + ncar + nk:])
    return list(res[:nout]), list(res[nout:nout + ncar + nk])


def _stage_bwd(st, t, params, inputs, saved, douts, dx_dtypes, hook=None, dout_dot=None):
    g, tm = st.g, min(st.tm, t)
    n = t // tm
    npar, nin, ncar = len(params), len(inputs), len(st.carry_shapes)
    nk = len(st.kept_shapes)
    flat_d = list(dout_dot) if dout_dot is not None else [d for ds in douts for d in ds]
    nd = len(flat_d)
    dx_idx = [i for i, dt in enumerate(dx_dtypes) if dt is not None]
    h_in, h_out, h_shape, h_sems = _hook_specs(hook)
    nh = len(h_in)

    def body(*refs):
        p_refs = refs[:npar]
        x_refs = refs[npar:npar + nin]
        s_refs = refs[npar + nin:npar + nin + ncar]
        k_refs = refs[npar + nin + ncar:npar + nin + ncar + nk]
        o = npar + nin + ncar + nk
        d_refs = refs[o:o + nd]
        hi_refs = refs[o + nd:o + nd + nh]
        o += nd + nh
        dp_refs = refs[o:o + npar]
        dx_refs = refs[o + npar:o + npar + len(dx_idx)]
        ho_refs = refs[o + npar + len(dx_idx):o + npar + len(dx_idx) + nh]
        dc_scr = refs[o + npar + len(dx_idx) + nh:o + npar + len(dx_idx) + nh + ncar]
        gi, ni = pl.program_id(0), pl.program_id(1)
        if hook is not None:
            step = gi * n + ni
            hook.run(step, g * n, hi_refs, ho_refs, *refs[-2:])

        @pl.when(ni == 0)
        def _():
            for c in dc_scr:
                c[...] = jnp.zeros(c.shape, F32)

        ps = [r[...].astype(F32) for r in p_refs]
        xs = _load_pieces(x_refs, st.in_pieces)
        cs = [s[...] for s in s_refs]
        dys = [_raw_dot(d_refs[0][...], d_refs[1][...], "nt")] if dout_dot is not None else []
        k = 0
        for ds, pieces in zip(douts, st.out_pieces):
            acc = _load_pieces([d_refs[k]], [pieces])
            for j in range(1, len(ds)):
                more = _load_pieces([d_refs[k + j]], [pieces])
                acc = [a + b for a, b in zip(acc, more)]
            dys += acc
            k += len(ds)
        if nk:
            kept = [r[...].astype(F32) for r in k_refs]
            _, vjp = jax.vjp(lambda p, x, c: st.f_kept(p, x, c, kept), ps, xs, cs)
        else:
            _, vjp = jax.vjp(st.f, ps, xs, cs)
        dps, dxs, dcs = vjp((dys, [c[...] for c in dc_scr]))
        k = 0
        per_in = []
        for pieces in st.in_pieces:
            per_in.append(dxs[k:k + len(pieces)])
            k += len(pieces)
        for ref, i in zip(dx_refs, dx_idx):
            _store_pieces([ref], [st.in_pieces[i]], per_in[i])
        for c, v in zip(dc_scr, dcs):
            c[...] = v
        for ref, dp, pg in zip(dp_refs, dps, st.par_per_g):
            first = (ni == 0) if pg else ((ni == 0) & (gi == 0))

            @pl.when(first)
            def _():
                ref[...] = jnp.zeros(ref.shape, F32)

            ref[...] += dp

    in_specs = [_par_spec(p, pg, g) for p, pg in zip(params, st.par_per_g)]
    in_specs += [_row_spec(tm, sum(pc), off, n, True) for pc, off in zip(st.in_pieces, st.in_offs)]
    in_specs += [_carry_spec(s, n, True) for s in st.carry_shapes]
    in_specs += [pl.BlockSpec(s, lambda gi, ni: (n - 1 - ni, 0)) for s in st.kept_shapes]
    for ds, pc in zip(douts, st.out_pieces):
        in_specs += [_row_spec(tm, sum(pc), 0, n, True) for _ in ds]
    if dout_dot is not None:
        a, w = dout_dot
        in_specs += [pl.BlockSpec((tm, a.shape[1]), lambda gi, ni: (n - 1 - ni, 0)),
                     pl.BlockSpec(w.shape, lambda gi, ni: (0, 0), pipeline_mode=pl.Buffered(1))]
    out_specs = [_par_spec(p, pg, g) for p, pg in zip(params, st.par_per_g)]
    out_specs += [_row_spec(tm, sum(st.in_pieces[i]), 0, n, True) for i in dx_idx]
    out_shape = [jax.ShapeDtypeStruct(p.shape, F32) for p in params]
    out_shape += [jax.ShapeDtypeStruct((t, g * sum(st.in_pieces[i])), dx_dtypes[i]) for i in dx_idx]
    res = pl.pallas_call(
        body, name=st.name + "_bwd", grid=(g, n), in_specs=in_specs + h_in, out_specs=out_specs + h_out,
        out_shape=out_shape + h_shape,
        scratch_shapes=[pltpu.VMEM(s, F32) for s in st.carry_shapes] + h_sems,
        compiler_params=_cparams(("arbitrary", "arbitrary")),
    )(*params, *inputs, *saved, *flat_d, *(hook.arrs if hook else []))
    if hook is not None:
        hook.results = list(res[npar + len(dx_idx):])
    return list(res[:npar]), list(res[npar:npar + len(dx_idx)])


def _pick(n, cap):
    if n <= cap:
        return n
    best = LANES
    for k in range(1, n // LANES + 1):
        if (n // LANES) % k == 0 and k * LANES <= cap:
            best = k * LANES
    return best


def _mm(name, a, b, mode, out_dtype=F32, tm=1024, tn=512, b_outer=False):
    m = a.shape[1] if mode == "tn" else a.shape[0]
    k = a.shape[0] if mode == "tn" else a.shape[1]
    n = b.shape[0] if mode == "nt" else b.shape[1]
    tm, tn = _pick(m, tm), _pick(n, tn)
    if b_outer:
        grid = (n // tn, m // tm)
        ij = lambda p, q: (q, p)
    else:
        grid = (m // tm, n // tn)
        ij = lambda p, q: (p, q)

    def body(a_ref, b_ref, o_ref):
        o_ref[...] = _raw_dot(a_ref[...], b_ref[...], mode).astype(o_ref.dtype)

    if mode == "tn":
        a_spec = pl.BlockSpec((k, tm), lambda p, q: (0, ij(p, q)[0]))
    else:
        a_spec = pl.BlockSpec((tm, k), lambda p, q: (ij(p, q)[0], 0))
    b_mode = dict(pipeline_mode=pl.Buffered(1)) if tn == n else {}
    if mode == "nt":
        b_spec = pl.BlockSpec((tn, k), lambda p, q: (ij(p, q)[1], 0), **b_mode)
    else:
        b_spec = pl.BlockSpec((k, tn), lambda p, q: (0, ij(p, q)[1]), **b_mode)
    return pl.pallas_call(
        body, name=name, grid=grid, in_specs=[a_spec, b_spec],
        out_specs=pl.BlockSpec((tm, tn), lambda p, q: ij(p, q)),
        out_shape=jax.ShapeDtypeStruct((m, n), out_dtype),
        compiler_params=_cparams(("arbitrary", "arbitrary")),
    )(a, b)


def _mm_multi(name, pairs, mode, out_dtype, tm=1024, tn=512):
    a0, b0 = pairs[0]
    m = a0.shape[1] if mode == "tn" else a0.shape[0]
    k = a0.shape[0] if mode == "tn" else a0.shape[1]
    n = b0.shape[0] if mode == "nt" else b0.shape[1]
    tm, tn = _pick(m, tm), _pick(n, tn)
    npair = len(pairs)

    def body(*refs):
        for p in range(npair):
            refs[2 * npair + p][...] = _raw_dot(refs[2 * p][...], refs[2 * p + 1][...], mode).astype(out_dtype)

    a_spec = pl.BlockSpec((k, tm), lambda i, j: (0, i)) if mode == "tn" else pl.BlockSpec((tm, k), lambda i, j: (i, 0))
    b_spec = pl.BlockSpec((tn, k), lambda i, j: (j, 0)) if mode == "nt" else pl.BlockSpec((k, tn), lambda i, j: (0, j))
    return pl.pallas_call(
        body, name=name, grid=(m // tm, n // tn), in_specs=[a_spec, b_spec] * npair,
        out_specs=[pl.BlockSpec((tm, tn), lambda i, j: (i, j))] * npair,
        out_shape=[jax.ShapeDtypeStruct((m, n), out_dtype)] * npair,
        compiler_params=_cparams(("arbitrary", "arbitrary")),
    )(*[x for pair in pairs for x in pair])


def _mm_cols_tn(name, pieces, b, out_dtype, tm):
    k, n = b.shape
    counts = [p.shape[1] // tm for p in pieces]
    starts = [sum(counts[:i]) for i in range(len(pieces))]
    na = len(pieces)

    def body(*refs):
        b_ref, o_ref = refs[na], refs[-1]
        i = pl.program_id(0)
        for a_ref, s, c in zip(refs[:na], starts, counts):
            @pl.when((i >= s) & (i < s + c))
            def _():
                o_ref[...] = _raw_dot(a_ref[...], b_ref[...], "tn").astype(o_ref.dtype)

    def spec(s, c):
        return pl.BlockSpec((k, tm), lambda i: (0, jnp.clip(i - s, 0, c - 1)))

    return pl.pallas_call(
        body, name=name, grid=(sum(counts),),
        in_specs=[spec(s, c) for s, c in zip(starts, counts)]
        + [pl.BlockSpec(b.shape, lambda i: (0, 0), pipeline_mode=pl.Buffered(1))],
        out_specs=pl.BlockSpec((tm, n), lambda i: (i, 0)),
        out_shape=jax.ShapeDtypeStruct((sum(counts) * tm, n), out_dtype),
        compiler_params=_cparams(("arbitrary",)),
    )(*pieces, b)


def _norm_in_proj(x, g, w_t, tm, tn):
    t, k = x.shape
    n = w_t.shape[0]
    tm, tn = _pick(t, tm), _pick(n, tn)

    def body(x_ref, g_ref, w_ref, xn_ref, z_ref):
        xn = _rms(x_ref[...], g_ref[...]).astype(BF)
        xn_ref[...] = xn
        z_ref[...] = _raw_dot(xn, w_ref[...], "nt")

    xns, z = pl.pallas_call(
        body, name="in_proj", grid=(n // tn, t // tm),
        in_specs=[pl.BlockSpec((tm, k), lambda j, i: (i, 0)), pl.BlockSpec((1, k), lambda j, i: (0, 0)),
                  pl.BlockSpec((tn, k), lambda j, i: (j, 0))],
        out_specs=[pl.BlockSpec((None, tm, k), lambda j, i: (j, i, 0)), pl.BlockSpec((tm, tn), lambda j, i: (i, j))],
        out_shape=[jax.ShapeDtypeStruct((n // tn, t, k), BF), jax.ShapeDtypeStruct((t, n), F32)],
        compiler_params=_cparams(("arbitrary", "arbitrary")),
    )(x, g, w_t)
    return xns[0], z


def _merge_out_post(z, o_a, o_b, w_a, w_b, w_out, x, g_post, g_pre2, tm):
    t = x.shape[0]
    tm = _pick(t, tm)
    w = 256
    npc = D // w
    ga0, gb0 = (IN_COLS - 2 * D) // w, (IN_COLS - D) // w

    def body(*refs):
        ga_refs, gb_refs = refs[:npc], refs[npc:2 * npc]
        oa_ref, ob_ref, wa_ref, wb_ref, w_ref, x_ref, gp_ref, g2_ref = refs[2 * npc:2 * npc + 8]
        ya_ref, yb_ref, m_ref, mix_ref, h_ref, xn_ref = refs[2 * npc + 8:]
        ya = _raw_dot(oa_ref[...], wa_ref[...], "nn").astype(BF)
        yb = _raw_dot(ob_ref[...], wb_ref[...], "nn").astype(BF)
        ya_ref[...] = ya
        yb_ref[...] = yb
        parts = []
        for p in range(npc):
            cols = slice(p * w, (p + 1) * w)
            parts.append(_sigmoid(ga_refs[p][...]) * ya[:, cols].astype(F32)
                         + _sigmoid(gb_refs[p][...]) * yb[:, cols].astype(F32))
        merged = jnp.concatenate(parts, axis=1).astype(BF)
        m_ref[...] = merged
        mix = _raw_dot(merged, w_ref[...], "nn")
        mix_ref[...] = mix
        h1 = x_ref[...] + _rms(mix, gp_ref[...])
        h_ref[...] = h1
        xn_ref[...] = _rms(h1, g2_ref[...]).astype(BF)

    row = pl.BlockSpec((tm, D), lambda i: (i, 0))
    one = pl.BlockSpec((1, D), lambda i: (0, 0))

    def gate(b0):
        return [pl.BlockSpec((tm, w), functools.partial(lambda i, b: (i, b), b=b0 + p)) for p in range(npc)]

    wgt = pl.BlockSpec((D, D), lambda i: (0, 0), pipeline_mode=pl.Buffered(1))
    return pl.pallas_call(
        body, name="merge_out_post", grid=(t // tm,),
        in_specs=gate(ga0) + gate(gb0) + [row, row, wgt, wgt, wgt, row, one, one],
        out_specs=[row] * 6,
        out_shape=[jax.ShapeDtypeStruct((t, D), BF), jax.ShapeDtypeStruct((t, D), BF), jax.ShapeDtypeStruct((t, D), BF),
                   jax.ShapeDtypeStruct((t, D), F32), jax.ShapeDtypeStruct((t, D), F32),
                   jax.ShapeDtypeStruct((t, D), BF)],
        compiler_params=_cparams(("arbitrary",)),
    )(*([z] * (2 * npc)), o_a, o_b, w_a, w_b, w_out, x, g_post, g_pre2)


def _accumulate(ni, refs, vals):
    @pl.when(ni == 0)
    def _():
        for r in refs:
            r[...] = jnp.zeros(r.shape, F32)

    for r, v in zip(refs, vals):
        r[...] += v


def _dmerged_merge_bwd(dmix, w_out, w_a, w_b, z, y_a, y_b, tm):
    t = dmix.shape[0]
    tm = _pick(t, tm)
    w = 256
    npc = D // w
    ga0, gb0 = (IN_COLS - 2 * D) // w, (IN_COLS - D) // w

    def body(*refs):
        dm_ref, w_ref, wa_ref, wb_ref = refs[:4]
        ga_refs, gb_refs = refs[4:4 + npc], refs[4 + npc:4 + 2 * npc]
        ya_ref, yb_ref, dga_ref, dgb_ref, dya_ref, dyb_ref, doa_ref, dob_ref = refs[4 + 2 * npc:]
        dmerged = _raw_dot(dm_ref[...], w_ref[...], "nt")
        dyas, dybs = [], []
        for p in range(npc):
            cols = slice(p * w, (p + 1) * w)
            xs = [ga_refs[p][...], gb_refs[p][...], ya_ref[:, cols].astype(F32), yb_ref[:, cols].astype(F32)]
            _, vjp = jax.vjp(lambda *a: _f_merge([], list(a), [])[0][0], *xs)
            dga, dgb, dya, dyb = vjp(dmerged[:, cols])
            dga_ref[:, cols] = dga.astype(BF)
            dgb_ref[:, cols] = dgb.astype(BF)
            dyas.append(dya.astype(BF))
            dybs.append(dyb.astype(BF))
        dya, dyb = jnp.concatenate(dyas, axis=1), jnp.concatenate(dybs, axis=1)
        dya_ref[...] = dya
        dyb_ref[...] = dyb
        doa_ref[...] = _raw_dot(dya, wa_ref[...], "nt").astype(BF)
        dob_ref[...] = _raw_dot(dyb, wb_ref[...], "nt").astype(BF)

    row = pl.BlockSpec((tm, D), lambda i: (i, 0))
    wgt = pl.BlockSpec((D, D), lambda i: (0, 0), pipeline_mode=pl.Buffered(1))

    def gate(b0):
        return [pl.BlockSpec((tm, w), functools.partial(lambda i, b: (i, b), b=b0 + p)) for p in range(npc)]

    return pl.pallas_call(
        body, name="merge_bwd", grid=(t // tm,),
        in_specs=[row, wgt, wgt, wgt] + gate(ga0) + gate(gb0) + [row, row],
        out_specs=[row] * 6, out_shape=[jax.ShapeDtypeStruct((t, D), BF)] * 6,
        compiler_params=_cparams(("arbitrary",)),
    )(dmix, w_out, w_a, w_b, *([z] * (2 * npc)), y_a, y_b)


def _dxn2_post1_bwd(pieces, w_up_t, x, mix, dh1, g_post, g_pre2, tm):
    t = x.shape[0]
    tm = _pick(t, tm)
    k = w_up_t.shape[0]
    offs = [sum(p.shape[1] for p in pieces[:i]) for i in range(len(pieces))]
    na = len(pieces)

    def body(*refs):
        w_ref, x_ref, m_ref, dh_ref, gp_ref, g2_ref, dgp_ref, dg2_ref, dx_ref, dm_ref = refs[na:]
        dxn2 = None
        for a_ref, off in zip(refs[:na], offs):
            part = _raw_dot(a_ref[...], w_ref[off:off + a_ref.shape[1], :], "nn")
            dxn2 = part if dxn2 is None else dxn2 + part
        _, vjp = jax.vjp(lambda gp, g2, xx, mm: _f_post1([gp, g2], [xx, mm], [])[0],
                         gp_ref[...], g2_ref[...], x_ref[...], m_ref[...])
        dgp, dg2, dx, dm = vjp([dh_ref[...], dxn2])
        _accumulate(pl.program_id(0), [dgp_ref, dg2_ref], [dgp, dg2])
        dx_ref[...] = dx
        dm_ref[...] = dm.astype(BF)

    row = pl.BlockSpec((tm, D), lambda i: (i, 0))
    one = pl.BlockSpec((1, D), lambda i: (0, 0))
    return pl.pallas_call(
        body, name="post1_bwd", grid=(t // tm,),
        in_specs=[pl.BlockSpec((tm, p.shape[1]), lambda i: (i, 0)) for p in pieces]
        + [pl.BlockSpec((k, D), lambda i: (0, 0), pipeline_mode=pl.Buffered(1)), row, row, row, one, one],
        out_specs=[one, one, row, row],
        out_shape=[jax.ShapeDtypeStruct((1, D), F32), jax.ShapeDtypeStruct((1, D), F32),
                   jax.ShapeDtypeStruct((t, D), F32), jax.ShapeDtypeStruct((t, D), BF)],
        compiler_params=_cparams(("arbitrary",)),
    )(*pieces, w_up_t, x, mix, dh1, g_post, g_pre2)


def _conv_taps(h, cw, cb, p2, p1):
    s1 = _shift_down(h, p1)
    s2 = _shift_down(s1, p2)
    return cb + _row(cw, 0) * s2 + _row(cw, 1) * s1 + _row(cw, 2) * h


def _up_conv(xn2, w_up_t, conv_w, conv_b, tm, tc):
    t = xn2.shape[0]
    tm = _pick(t, tm)
    tn = _pick(DFF, 1408)
    nj = DFF // tn
    sub = tm // tc
    n = t // tc
    last = t // tm - 1

    def body(x_ref, wg_ref, wv_ref, cwg_ref, cwv_ref, cbg_ref, cbv_ref, hg_ref, hv_ref, act_ref, c1_ref, c2_ref, prev):
        j, i = pl.program_id(0), pl.program_id(1)

        @pl.when(i == 0)
        def _():
            prev[...] = jnp.zeros(prev.shape, F32)

        x = x_ref[...]
        hg = _raw_dot(x, wg_ref[...], "nt")
        hv = _raw_dot(x, wv_ref[...], "nt")
        hg_ref[...] = hg
        hv_ref[...] = hv
        pg, pv = prev[0:SUBLANES], prev[SUBLANES:2 * SUBLANES]
        cg = _conv_taps(hg, cwg_ref[...], cbg_ref[...], _row(pg, SUBLANES - 2), _row(pg, SUBLANES - 1))
        cv = _conv_taps(hv, cwv_ref[...], cbv_ref[...], _row(pv, SUBLANES - 2), _row(pv, SUBLANES - 1))
        act_ref[...] = (_silu(cg) * cv).astype(BF)
        prev[0:SUBLANES] = hg[tm - SUBLANES:tm]
        prev[SUBLANES:2 * SUBLANES] = hv[tm - SUBLANES:tm]

        def keep(h, off):
            cols = slice(off, off + tn)

            @pl.when(i == 0)
            def _():
                c1_ref[0, :, cols] = jnp.zeros((1, tn), F32)
                c2_ref[0, :, cols] = jnp.zeros((1, tn), F32)

            for s in range(sub):
                def put(s=s):
                    tail = h[(s + 1) * tc - SUBLANES:(s + 1) * tc]
                    c1_ref[i * sub + s + 1, :, cols] = _row(tail, SUBLANES - 1)
                    c2_ref[i * sub + s + 1, :, cols] = _row(tail, SUBLANES - 2)

                if s < sub - 1:
                    put()
                else:
                    pl.when(i < last)(put)

        for col in range(nj):
            @pl.when(j == col)
            def _(col=col):
                keep(hg, col * tn)
                keep(hv, DFF + col * tn)

    def cols(rows, off):
        return pl.BlockSpec((rows, tn), lambda j, i: (0, j + off))

    tile = pl.BlockSpec((tm, tn), lambda j, i: (i, j))
    before = pl.BlockSpec((n, 1, 2 * DFF), lambda j, i: (0, 0, 0))
    return pl.pallas_call(
        body, name="up_conv", grid=(nj, t // tm),
        in_specs=[pl.BlockSpec((tm, D), lambda j, i: (i, 0)), pl.BlockSpec((tn, D), lambda j, i: (j, 0)),
                  pl.BlockSpec((tn, D), lambda j, i: (j + nj, 0)), cols(3, 0), cols(3, nj), cols(1, 0), cols(1, nj)],
        out_specs=[tile, tile, tile, before, before],
        out_shape=[jax.ShapeDtypeStruct((t, DFF), F32), jax.ShapeDtypeStruct((t, DFF), F32),
                   jax.ShapeDtypeStruct((t, DFF), BF), jax.ShapeDtypeStruct((n, 1, 2 * DFF), F32),
                   jax.ShapeDtypeStruct((n, 1, 2 * DFF), F32)],
        scratch_shapes=[pltpu.VMEM((2 * SUBLANES, tn), F32)],
        compiler_params=_cparams(("arbitrary", "arbitrary")),
    )(xn2, w_up_t, w_up_t, conv_w, conv_w, conv_b, conv_b)


def _dxn_pre1_bwd(pieces, w_t, x, dx_res, g, tm, token):
    t = x.shape[0]
    tm = _pick(t, tm)
    offs = [sum(p.shape[1] for p in pieces[:i]) for i in range(len(pieces))]
    na = len(pieces)

    def body(*refs):
        w_ref, x_ref, r_ref, g_ref = refs[na:na + 4]
        dg_ref, dx_ref = refs[-2:]
        dxn = None
        for a_ref, off in zip(refs[:na], offs):
            part = _raw_dot(a_ref[...], w_ref[off:off + a_ref.shape[1], :], "nn")
            dxn = part if dxn is None else dxn + part
        _, vjp = jax.vjp(lambda gg, xx: _f_pre1_residual([gg], [xx], [])[0], g_ref[...], x_ref[...])
        dg, dx = vjp([dxn, r_ref[...]])
        _accumulate(pl.program_id(0), [dg_ref], [dg])
        dx_ref[...] = dx

    row = pl.BlockSpec((tm, D), lambda i: (i, 0))
    one = pl.BlockSpec((1, D), lambda i: (0, 0))
    return pl.pallas_call(
        body, name="pre1_bwd", grid=(t // tm,),
        in_specs=[pl.BlockSpec((tm, p.shape[1]), lambda i: (i, 0)) for p in pieces]
        + [pl.BlockSpec(w_t.shape, lambda i: (0, 0), pipeline_mode=pl.Buffered(1)), row, row, one,
           pl.BlockSpec(token.shape, lambda i: (0, 0))],
        out_specs=[one, row],
        out_shape=[jax.ShapeDtypeStruct((1, D), F32), jax.ShapeDtypeStruct((t, D), F32)],
        compiler_params=_cparams(("arbitrary",)),
    )(*pieces, w_t, x, dx_res, g, token)


def _down_loss(act, w_down, g_post, h1, tgt, tm):
    t, k = act.shape
    tm = _pick(t, tm)

    def body(a_ref, w_ref, g_ref, h_ref, t_ref, loss_ref, dg_ref, dh_ref, df_ref):
        ni = pl.program_id(0)
        ff = _raw_dot(a_ref[...], w_ref[...], "nn")
        target = t_ref[...]

        def lossf(g, h1, ff):
            e = h1 + _rms(ff, g) - target
            return 0.5 * jnp.sum(jnp.mean(e * e, axis=-1))

        l, (dg, dh, df) = jax.value_and_grad(lossf, argnums=(0, 1, 2))(g_ref[...], h_ref[...], ff)

        @pl.when(ni == 0)
        def _():
            loss_ref[...] = jnp.zeros(loss_ref.shape, F32)
            dg_ref[...] = jnp.zeros(dg_ref.shape, F32)

        loss_ref[...] += jnp.full(loss_ref.shape, l, F32)
        dg_ref[...] += dg
        dh_ref[...] = dh
        df_ref[...] = df.astype(df_ref.dtype)

    row = pl.BlockSpec((tm, D), lambda ni: (ni, 0))
    one = pl.BlockSpec((1, D), lambda ni: (0, 0))
    return pl.pallas_call(
        body, name="down_loss", grid=(t // tm,),
        in_specs=[pl.BlockSpec((tm, k), lambda ni: (ni, 0)),
                  pl.BlockSpec((k, D), lambda ni: (0, 0), pipeline_mode=pl.Buffered(1)), one, row, row],
        out_specs=[pl.BlockSpec((1, LANES), lambda ni: (0, 0)), one, row, row],
        out_shape=[jax.ShapeDtypeStruct((1, LANES), F32), jax.ShapeDtypeStruct((1, D), F32),
                   jax.ShapeDtypeStruct((t, D), F32), jax.ShapeDtypeStruct((t, D), BF)],
        compiler_params=_cparams(("arbitrary",)),
    )(act, w_down, g_post, h1, tgt)


_ANY = pl.BlockSpec(memory_space=pl.ANY)


def _all_gather(name, blks):
    na = len(blks)
    ns = 8

    def body(*refs):
        x_refs, out_refs = refs[:na], refs[na:2 * na]
        send_sems, recv_sems, local_sems = refs[2 * na:]
        x, y, cc = lax.axis_index("x"), lax.axis_index("y"), lax.axis_index("c")
        sibling, xn, yn = (x, y, 1 - cc), (1 - x, y, cc), (x, 1 - y, cc)

        def num(px, py, pc):
            return 4 * px + 2 * py + pc

        def copy(a, k, to, src, dst):
            return pltpu.make_async_remote_copy(src_ref=src, dst_ref=dst, send_sem=send_sems.at[ns * a + k],
                                                recv_sem=recv_sems.at[ns * a + k], device_id=to, device_id_type=MESH)

        def halves(a, blk):
            h = blks[a].shape[0] // 2
            return out_refs[a].at[blk, pl.ds(0, h)], out_refs[a].at[blk, pl.ds(h, h)]

        mine, sends = [], []
        for a in range(na):
            o = out_refs[a]
            m = pltpu.make_async_copy(x_refs[a], o.at[num(x, y, cc)], local_sems.at[a])
            m.start()
            mine.append(m)
            own = o.at[num(x, y, cc)]
            sends.append([copy(a, 0, sibling, x_refs[a], own), copy(a, 1, xn, x_refs[a], own),
                          copy(a, 2, yn, x_refs[a], own)])
            for cp in sends[a]:
                cp.start()
        for a in range(na):
            o = out_refs[a]
            bx, by, bd = num(1 - x, y, cc), num(x, 1 - y, cc), num(1 - x, 1 - y, cc)
            copy(a, 1, xn, o.at[bx], o.at[bx]).wait_recv()
            more = [copy(a, 3, yn, halves(a, bx)[0], halves(a, bx)[0]), copy(a, 5, sibling, o.at[bx], o.at[bx])]
            for cp in more:
                cp.start()
            sends[a] += more
        for a in range(na):
            o = out_refs[a]
            bx, by, bd = num(1 - x, y, cc), num(x, 1 - y, cc), num(1 - x, 1 - y, cc)
            copy(a, 2, yn, o.at[by], o.at[by]).wait_recv()
            more = [copy(a, 4, xn, halves(a, by)[1], halves(a, by)[1]), copy(a, 6, sibling, o.at[by], o.at[by])]
            for cp in more:
                cp.start()
            sends[a] += more
        for a in range(na):
            o = out_refs[a]
            bd = num(1 - x, 1 - y, cc)
            copy(a, 3, yn, halves(a, bd)[0], halves(a, bd)[0]).wait_recv()
            copy(a, 4, xn, halves(a, bd)[1], halves(a, bd)[1]).wait_recv()
            fw = copy(a, 7, sibling, o.at[bd], o.at[bd])
            fw.start()
            sends[a].append(fw)
        for a in range(na):
            o = out_refs[a]
            for k, blk in ((0, num(x, y, 1 - cc)), (5, num(1 - x, y, 1 - cc)), (6, num(x, 1 - y, 1 - cc)),
                           (7, num(1 - x, 1 - y, 1 - cc))):
                copy(a, k, sibling, o.at[blk], o.at[blk]).wait_recv()
            for cp in sends[a]:
                cp.wait_send()
        for m in mine:
            m.wait()

    res = pl.pallas_call(
        body, name=name, in_specs=[_ANY] * na, out_specs=[_ANY] * na,
        out_shape=[jax.ShapeDtypeStruct((N_DEV,) + b.shape, b.dtype) for b in blks],
        scratch_shapes=[pltpu.SemaphoreType.DMA((ns * na,)), pltpu.SemaphoreType.DMA((ns * na,)),
                        pltpu.SemaphoreType.DMA((na,))],
    )(*blks)
    return list(res)


def _all_gather_small(name, blk):
    def body(x_ref, out_ref, ssem, rsem, lsem):
        x, y, c = lax.axis_index("x"), lax.axis_index("y"), lax.axis_index("c")
        me = 4 * x + 2 * y + c
        mine = pltpu.make_async_copy(x_ref, out_ref.at[me], lsem)
        mine.start()
        cps = []
        for j in range(1, N_DEV):
            px = 1 - x if j & 4 else x
            py = 1 - y if j & 2 else y
            pc = 1 - c if j & 1 else c
            cps.append(pltpu.make_async_remote_copy(src_ref=x_ref, dst_ref=out_ref.at[me], send_sem=ssem.at[j - 1],
                                                    recv_sem=rsem.at[j - 1], device_id=(px, py, pc),
                                                    device_id_type=MESH))
        for cp in cps:
            cp.start()
        for cp in cps:
            cp.wait()
        mine.wait()

    return pl.pallas_call(
        body, name=name, in_specs=[_ANY], out_specs=_ANY,
        out_shape=jax.ShapeDtypeStruct((N_DEV,) + blk.shape, blk.dtype),
        scratch_shapes=[pltpu.SemaphoreType.DMA((N_DEV - 1,)), pltpu.SemaphoreType.DMA((N_DEV - 1,)),
                        pltpu.SemaphoreType.DMA],
    )(blk)


def _reduce_pair(g8s):
    na = len(g8s)

    def body(*refs):
        g_refs, recv_refs = refs[:na], refs[na:2 * na]
        ssem, rsem = refs[2 * na:]
        x, y, cc = lax.axis_index("x"), lax.axis_index("y"), lax.axis_index("c")
        chips = [(x, y), (1 - x, y), (x, 1 - y), (1 - x, 1 - y)]
        sib = (x, y, 1 - cc)
        for a in range(na):
            for k, (cx, cy) in enumerate(chips):
                pltpu.make_async_remote_copy(
                    src_ref=g_refs[a].at[4 * cx + 2 * cy + 1 - cc], dst_ref=recv_refs[a].at[k],
                    send_sem=ssem.at[a], recv_sem=rsem.at[a], device_id=sib, device_id_type=MESH).start()
        for a in range(na):
            pltpu.make_async_remote_copy(src_ref=recv_refs[a], dst_ref=recv_refs[a], send_sem=ssem.at[a],
                                         recv_sem=rsem.at[a], device_id=sib, device_id_type=MESH).wait()

    res = pl.pallas_call(
        body, name="reduce_pair", in_specs=[_ANY] * na, out_specs=[_ANY] * na,
        out_shape=[jax.ShapeDtypeStruct((4,) + g.shape[1:], g.dtype) for g in g8s],
        scratch_shapes=[pltpu.SemaphoreType.DMA((na,)), pltpu.SemaphoreType.DMA((na,))],
    )(*g8s)
    return list(res)


_HBM = pl.BlockSpec(memory_space=pltpu.HBM)
_SEM = pl.BlockSpec(memory_space=pltpu.SEMAPHORE)
_EFFECT = pltpu.SideEffectType.DATAFLOW_SIDE_EFFECTING


def _chip_swap_copies(s_refs, land_refs, ssem, rsem):
    x, y, c = lax.axis_index("x"), lax.axis_index("y"), lax.axis_index("c")
    targets = [(1 - x, y, c), (x, 1 - y, c), (1 - x, 1 - y, c)]
    return [pltpu.make_async_remote_copy(src_ref=s.at[k], dst_ref=d.at[k], send_sem=ssem.at[3 * a + k],
                                         recv_sem=rsem.at[3 * a + k], device_id=targets[k], device_id_type=MESH)
            for a, (s, d) in enumerate(zip(s_refs, land_refs)) for k in range(3)]


def _chip_swap_start(sends):
    na = len(sends)

    def body(*refs):
        cps = _chip_swap_copies(refs[:na], refs[na:2 * na], refs[2 * na], refs[2 * na + 1])
        for cp in cps:
            cp.start()
        token = refs[-1]
        token[...] = jnp.zeros(token.shape, token.dtype)

    bufs = [pltpu.HBM(s.shape, s.dtype) for s in sends]
    res = pl.pallas_call(
        body, name="chip_swap_start",
        out_shape=[pltpu.SemaphoreType.DMA((3 * na,)), pltpu.SemaphoreType.DMA((3 * na,))] + bufs + bufs
        + [jax.ShapeDtypeStruct((8, LANES), F32)],
        in_specs=[_HBM] * (2 * na), out_specs=[_SEM, _SEM] + [_HBM] * (2 * na) + [pl.BlockSpec(memory_space=pltpu.VMEM)],
        input_output_aliases={i: 2 + i for i in range(2 * na)},
        compiler_params=pltpu.CompilerParams(has_side_effects=_EFFECT),
    )(*[pltpu.with_memory_space_constraint(s, pltpu.HBM) for s in sends],
      *[pltpu.with_memory_space_constraint(lax.empty(s.shape, s.dtype), pltpu.HBM) for s in sends])
    return res[0], res[1], list(res[2:2 + na]), list(res[2 + na:2 + 2 * na]), res[-1]


def _chip_swap_wait(ssem, rsem, srcs, lands, after):
    na = len(srcs)

    def body(*refs):
        cps = _chip_swap_copies(refs[:na], refs[na:2 * na], refs[2 * na], refs[2 * na + 1])
        for cp in cps:
            cp.wait_send()
            cp.wait_recv()

    bufs = [pltpu.HBM(s.shape, s.dtype) for s in srcs]
    res = pl.pallas_call(
        body, name="chip_swap_wait", out_shape=bufs + bufs,
        in_specs=[_HBM] * (2 * na) + [_SEM, _SEM, _ANY], out_specs=[_HBM] * (2 * na),
        input_output_aliases={i: i for i in range(2 * na)},
        compiler_params=pltpu.CompilerParams(has_side_effects=_EFFECT),
    )(*srcs, *lands, ssem, rsem, after)
    return list(res[na:])


def _pick_rows(r, c, budget=TILE_BYTES):
    if r * c * 4 <= budget or r % 16:
        return r
    best = 16
    for tr in range(16, r, 16):
        if r % tr == 0 and tr * c * 4 <= budget:
            best = tr
    return best


def _pair_sum(name, idx4, g8, recv4):
    _, r, c = g8.shape
    tr = _pick_rows(r, c, 2 * TILE_BYTES)

    def body(idx_ref, a_ref, b_ref, o0_ref, o3_ref):
        k = pl.program_id(1)
        s = a_ref[...].astype(F32) + b_ref[...].astype(F32)

        @pl.when(k == 0)
        def _():
            o0_ref[...] = s

        @pl.when(k > 0)
        def _():
            o3_ref[...] = s.astype(BF)

    spec = pltpu.PrefetchScalarGridSpec(
        num_scalar_prefetch=1, grid=(r // tr, 4),
        in_specs=[pl.BlockSpec((None, tr, c), lambda i, k, idx: (idx[k], i, 0)),
                  pl.BlockSpec((None, tr, c), lambda i, k, idx: (k, i, 0))],
        out_specs=[pl.BlockSpec((tr, c), lambda i, k, idx: (i, 0)),
                   pl.BlockSpec((None, tr, c), lambda i, k, idx: (jnp.maximum(k - 1, 0), i, 0))])
    return pl.pallas_call(
        body, name=name, grid_spec=spec,
        out_shape=[jax.ShapeDtypeStruct((r, c), F32), jax.ShapeDtypeStruct((3, r, c), BF)],
        compiler_params=_cparams(("arbitrary", "arbitrary")),
    )(idx4, g8, recv4)


def _adamw(w, g, m, v):
    m = ADAM_B1 * m + (1.0 - ADAM_B1) * g
    v = ADAM_B2 * v + (1.0 - ADAM_B2) * jnp.square(g)
    m_hat = m / (1.0 - ADAM_B1 ** ADAM_STEP)
    v_hat = v / (1.0 - ADAM_B2 ** ADAM_STEP)
    delta = -ADAM_LR * (m_hat / (jnp.sqrt(v_hat) + ADAM_EPS) + ADAM_WD * w)
    return delta, m, v


def _adam_sharded(name, idx1, own, recv, w, m, v, after=None):
    r, c = w.shape
    tr = _pick_rows(r, c, 2 * TILE_BYTES)
    nj = recv.shape[0]
    extra = [] if after is None else [after]

    def body(idx_ref, p_ref, r_ref, w_ref, m_ref, v_ref, *rest):
        g_out, d_out, m_out, v_out = rest[-4:]
        g = p_ref[...].astype(F32)
        for k in range(nj):
            g = g + r_ref[k].astype(F32)
        d, mn, vn = _adamw(w_ref[...], g, m_ref[...], v_ref[...])
        g_out[...] = g
        d_out[...] = d
        m_out[...] = mn
        v_out[...] = vn

    row = pl.BlockSpec((tr, c), lambda i, idx: (i, 0))
    spec = pltpu.PrefetchScalarGridSpec(
        num_scalar_prefetch=1, grid=(r // tr,),
        in_specs=[pl.BlockSpec((None, tr, c), lambda i, idx: (idx[0], i, 0)),
                  pl.BlockSpec((nj, tr, c), lambda i, idx: (0, i, 0)), row, row, row]
        + [pl.BlockSpec(e.shape, lambda i, idx: (0, 0)) for e in extra],
        out_specs=[row] * 4)
    return pl.pallas_call(
        body, name=name, grid_spec=spec, out_shape=[jax.ShapeDtypeStruct((r, c), F32)] * 4,
        compiler_params=_cparams(("arbitrary",)),
    )(idx1, own, recv, w, m, v, *extra)


def _repl_rows():
    rows, r = {}, 0
    for name, cols in REPL:
        rows[name] = r
        r += REPL_ROWS.get(name, 1) * ((cols + D - 1) // D)
    return rows


LOSS_ROW = 24


def _pack_replicated(grads, loss_acc, after):
    rows = _repl_rows()
    names = [n for n, _ in REPL]

    def body(*refs):
        o_ref = refs[-1]
        o_ref[...] = jnp.zeros(o_ref.shape, F32)
        o_ref[LOSS_ROW:LOSS_ROW + 1, 0:LANES] = refs[len(names)][...]
        for name, ref in zip(names, refs[:len(names)]):
            r0 = rows[name]
            nr, nc = ref.shape
            if nc <= D:
                o_ref[r0:r0 + nr, 0:nc] = ref[...]
            else:
                for j in range((nc + D - 1) // D):
                    lo, hi = j * D, min(nc, (j + 1) * D)
                    o_ref[r0 + j:r0 + j + 1, 0:hi - lo] = ref[:, lo:hi]

    return pl.pallas_call(body, name="pack_replicated", out_shape=jax.ShapeDtypeStruct((REPL_TOTAL, D), F32),
                          in_specs=[pl.BlockSpec(memory_space=pltpu.VMEM)] * (len(names) + 1) + [_ANY] * len(after),
                          compiler_params=_cparams())(*[grads[n] for n in names], loss_acc, *after)


def _adam_replicated(g8, ws, ms, vs):
    rows = _repl_rows()
    names = [n for n, _ in REPL]
    np_ = len(names)

    def body(*refs):
        g_ref = refs[0]
        w_refs, m_refs, v_refs = refs[1:1 + np_], refs[1 + np_:1 + 2 * np_], refs[1 + 2 * np_:1 + 3 * np_]
        outs = refs[1 + 3 * np_:1 + 7 * np_]
        scr = refs[-1]
        g = g_ref[0]
        for k in range(1, N_DEV):
            g = g + g_ref[k]
        scr[...] = g
        refs[1 + 7 * np_][...] = scr[LOSS_ROW:LOSS_ROW + 1, 0:LANES]
        for i, name in enumerate(names):
            r0 = rows[name]
            nr, nc = w_refs[i].shape
            if nc <= D:
                gi = scr[r0:r0 + nr, 0:nc]
            else:
                parts = []
                for j in range((nc + D - 1) // D):
                    lo, hi = j * D, min(nc, (j + 1) * D)
                    parts.append(scr[r0 + j:r0 + j + 1, 0:hi - lo])
                gi = jnp.concatenate(parts, axis=1)
            d, mn, vn = _adamw(w_refs[i][...], gi, m_refs[i][...], v_refs[i][...])
            outs[i][...] = gi
            outs[np_ + i][...] = d
            outs[2 * np_ + i][...] = mn
            outs[3 * np_ + i][...] = vn

    shp = [jax.ShapeDtypeStruct(w.shape, F32) for w in ws]
    res = pl.pallas_call(body, name="adam_replicated", out_shape=shp * 4 + [jax.ShapeDtypeStruct((1, LANES), F32)],
                         scratch_shapes=[pltpu.VMEM((REPL_TOTAL, D), F32)], compiler_params=_cparams(),
                         )(g8, *ws, *ms, *vs)
    return [dict(zip(names, res[k * np_:(k + 1) * np_])) for k in range(4)], res[-1]


_WEIGHTS = ("attn_pre_norm", "w_in", "hgrn_lb", "hgrn_gnorm", "w_branch_a", "rwkv_mu", "rwkv_w0", "rwkv_w2",
            "rwkv_a0", "rwkv_a2", "rwkv_g2", "rwkv_k_k", "rwkv_k_a", "rwkv_r_k", "rwkv_ln_w", "rwkv_ln_b",
            "w_branch_b", "w_out", "attn_post_norm", "ffn_pre_norm", "w_up", "conv_w", "conv_b", "w_down",
            "ffn_post_norm")
_BIG = ("w_in", "w_up", "w_down", "w_branch_a", "w_branch_b", "w_out")


def _stages():
    one = [D]
    hw = HG_K * HG_PER_STEP
    rw = LANES * RW_PAIRS_PER_STEP
    return dict(
        mixers=_Stage("mixers", _f_mixers, 1, 2 * RW_CHUNK, [False] * 13, [[D] * 7 + [LANES, LANES]], [0],
                      [(hw, HG_K), (1, RW_COLS), (rw, LANES)], [one, one], [BF, BF],
                      kept_shapes=[(2 * RW_KEPT * RW_PAIRS_PER_STEP * 2 * RW_CHUNK, LANES)], f_kept=_f_mixers_kept),
        conv=_Stage("conv", _f_conv, 1, 128, [False, False], [[DFF], [DFF]], [0, 0], [(1, 2 * DFF), (1, 2 * DFF)],
                    [[DFF]], [BF]),
    )


def _cols_to_blocks(w, per):
    return w.reshape(w.shape[0], N_DEV, per).transpose(1, 0, 2)


def _blocks_to_cols(g):
    return g.transpose(1, 0, 2).reshape(g.shape[1], N_DEV * g.shape[2])


def kernel(x, attn_pre_norm, w_in, hgrn_lb, hgrn_gnorm, w_branch_a, rwkv_mu, rwkv_w0, rwkv_w2, rwkv_a0, rwkv_a2, rwkv_g2, rwkv_k_k, rwkv_k_a, rwkv_r_k, rwkv_ln_w, rwkv_ln_b, w_branch_b, w_out, attn_post_norm, ffn_pre_norm, w_up, conv_w, conv_b, w_down, ffn_post_norm, loss_target, m_attn_pre_norm, m_w_in, m_hgrn_lb, m_hgrn_gnorm, m_w_branch_a, m_rwkv_mu, m_rwkv_w0, m_rwkv_w2, m_rwkv_a0, m_rwkv_a2, m_rwkv_g2, m_rwkv_k_k, m_rwkv_k_a, m_rwkv_r_k, m_rwkv_ln_w, m_rwkv_ln_b, m_w_branch_b, m_w_out, m_attn_post_norm, m_ffn_pre_norm, m_w_up, m_conv_w, m_conv_b, m_w_down, m_ffn_post_norm, v_attn_pre_norm, v_w_in, v_hgrn_lb, v_hgrn_gnorm, v_w_branch_a, v_rwkv_mu, v_rwkv_w0, v_rwkv_w2, v_rwkv_a0, v_rwkv_a2, v_rwkv_g2, v_rwkv_k_k, v_rwkv_k_a, v_rwkv_r_k, v_rwkv_ln_w, v_rwkv_ln_b, v_w_branch_b, v_w_out, v_attn_post_norm, v_ffn_pre_norm, v_w_up, v_conv_w, v_conv_b, v_w_down, v_ffn_post_norm):
    w = dict(attn_pre_norm=attn_pre_norm, w_in=w_in, hgrn_lb=hgrn_lb, hgrn_gnorm=hgrn_gnorm, w_branch_a=w_branch_a, rwkv_mu=rwkv_mu, rwkv_w0=rwkv_w0, rwkv_w2=rwkv_w2, rwkv_a0=rwkv_a0, rwkv_a2=rwkv_a2, rwkv_g2=rwkv_g2, rwkv_k_k=rwkv_k_k, rwkv_k_a=rwkv_k_a, rwkv_r_k=rwkv_r_k, rwkv_ln_w=rwkv_ln_w, rwkv_ln_b=rwkv_ln_b, w_branch_b=w_branch_b, w_out=w_out, attn_post_norm=attn_post_norm, ffn_pre_norm=ffn_pre_norm, w_up=w_up, conv_w=conv_w, conv_b=conv_b, w_down=w_down, ffn_post_norm=ffn_post_norm)
    mo = dict(attn_pre_norm=m_attn_pre_norm, w_in=m_w_in, hgrn_lb=m_hgrn_lb, hgrn_gnorm=m_hgrn_gnorm, w_branch_a=m_w_branch_a, rwkv_mu=m_rwkv_mu, rwkv_w0=m_rwkv_w0, rwkv_w2=m_rwkv_w2, rwkv_a0=m_rwkv_a0, rwkv_a2=m_rwkv_a2, rwkv_g2=m_rwkv_g2, rwkv_k_k=m_rwkv_k_k, rwkv_k_a=m_rwkv_k_a, rwkv_r_k=m_rwkv_r_k, rwkv_ln_w=m_rwkv_ln_w, rwkv_ln_b=m_rwkv_ln_b, w_branch_b=m_w_branch_b, w_out=m_w_out, attn_post_norm=m_attn_post_norm, ffn_pre_norm=m_ffn_pre_norm, w_up=m_w_up, conv_w=m_conv_w, conv_b=m_conv_b, w_down=m_w_down, ffn_post_norm=m_ffn_post_norm)
    vo = dict(attn_pre_norm=v_attn_pre_norm, w_in=v_w_in, hgrn_lb=v_hgrn_lb, hgrn_gnorm=v_hgrn_gnorm, w_branch_a=v_w_branch_a, rwkv_mu=v_rwkv_mu, rwkv_w0=v_rwkv_w0, rwkv_w2=v_rwkv_w2, rwkv_a0=v_rwkv_a0, rwkv_a2=v_rwkv_a2, rwkv_g2=v_rwkv_g2, rwkv_k_k=v_rwkv_k_k, rwkv_k_a=v_rwkv_k_a, rwkv_r_k=v_rwkv_r_k, rwkv_ln_w=v_rwkv_ln_w, rwkv_ln_b=v_rwkv_ln_b, w_branch_b=v_w_branch_b, w_out=v_w_out, attn_post_norm=v_attn_post_norm, ffn_pre_norm=v_ffn_pre_norm, w_up=v_w_up, conv_w=v_conv_w, conv_b=v_conv_b, w_down=v_w_down, ffn_post_norm=v_ffn_post_norm)

    t = x.shape[1]
    x2 = x.reshape(t, D)
    tgt = loss_target.reshape(t, D)
    st = _stages()

    me = 4 * lax.axis_index("x") + 2 * lax.axis_index("y") + lax.axis_index("c")
    small = jnp.concatenate([rwkv_w2[0], rwkv_a2[0], rwkv_g2[0]], axis=0).astype(BF)
    g_in, g_small = _all_gather("gather_weights", [w_in[0].T.astype(BF), small])
    fw_in_t = g_in.reshape(IN_COLS, D)
    z64 = jnp.zeros((64, D), BF)
    w2p = jnp.concatenate([_blocks_to_cols(g_small[:, 0:64]), z64], axis=0)
    a2p = jnp.concatenate([z64, _blocks_to_cols(g_small[:, 64:128])], axis=0)
    g2f = _blocks_to_cols(g_small[:, 128:256])
    conv_bits = jnp.pad(lax.bitcast_convert_type(conv_w[0], BF).reshape(3, 2 * 704), ((0, 29), (0, 0)))
    late = [w_up[0].T.astype(BF)] + [w[k][0].astype(BF) for k in _BIG[2:]] + [conv_bits]
    late_gather = _Exchange("gather2", late)
    r_k = rwkv_r_k.reshape(1, D)

    xn, z = _norm_in_proj(x2, attn_pre_norm, fw_in_t, 512, 4736)
    mix_par = [hgrn_lb, hgrn_gnorm, rwkv_mu, rwkv_w0, w2p, rwkv_a0, a2p, g2f, rwkv_k_k, rwkv_k_a,
               rwkv_ln_w, rwkv_ln_b, r_k]
    mix_in = [z]
    (o_a, o_b), mix_saved = _stage_fwd(st["mixers"], t, mix_par, mix_in, hook=late_gather)
    gl = [lax.dynamic_update_slice(g, own[None], (me, 0, 0)) for g, own in zip(late_gather.results, late)]
    fw_up_t = gl[0].reshape(2 * DFF, D)
    fw_down = gl[1].reshape(DFF, D)
    fw_a, fw_b, fw_out = (g.reshape(D, D) for g in gl[2:5])
    conv_full = _blocks_to_cols(lax.bitcast_convert_type(gl[5][:, :3].reshape(N_DEV, 3, 704, 2), F32))
    y_a, y_b, merged, mix, h1, xn2 = _merge_out_post(z, o_a, o_b, fw_a, fw_b, fw_out, x2, attn_post_norm,
                                                     ffn_pre_norm, 512)
    conv_par = [conv_full, conv_b]
    hu_g, hu_v, act, before1, before2 = _up_conv(xn2, fw_up_t, conv_full, conv_b, 512, min(st["conv"].tm, t))
    conv_saved = [before1[None], before2[None]]

    loss_acc, d_ffn_post, dh1, dff = _down_loss(act, fw_down, ffn_post_norm, h1, tgt, 512)
    dw_down = _mm("dw_down", act, dff, "tn", BF, tm=1408, tn=512)
    (dcw, dcb), dhu = _stage_bwd(st["conv"], t, conv_par, [hu_g, hu_v], conv_saved, [], [BF, BF],
                                 dout_dot=(dff, fw_down))
    dw_up_t = _mm_cols_tn("dw_up", dhu, xn2, BF, 1408)
    d_post, d_pre2, dx_a, dmix = _dxn2_post1_bwd(dhu, fw_up_t, x2, mix, dh1, attn_post_norm, ffn_pre_norm, 512)
    dga, dgb, dy_a, dy_b, do_a, do_b = _dmerged_merge_bwd(dmix, fw_out, fw_a, fw_b, z, y_a, y_b, 512)
    dw_a, dw_b, dw_out = _mm_multi("dw_branches", [(o_a, dy_a), (o_b, dy_b), (merged, dmix)], "tn", BF)
    early = [dw_up_t.reshape(N_DEV, 704, D), dw_down.reshape(N_DEV, 352, D), dw_a.reshape(N_DEV, 128, D),
             dw_b.reshape(N_DEV, 128, D), dw_out.reshape(N_DEV, 128, D), _cols_to_blocks(dcw.astype(BF), 704)]
    early_scatter = _Exchange("scatter", early)
    mix_dp, dz_hr = _stage_bwd(st["mixers"], t, mix_par, mix_in, mix_saved, [[do_a], [do_b]], [BF],
                               hook=early_scatter)
    d_lb, d_gn, d_mu, d_w0, d_w2p, d_a0, d_a2p, d_g2, d_kk, d_ka, d_lnw, d_lnb, d_rk = mix_dp
    dz = dz_hr + [dga, dgb]
    dw_in_t = _mm_cols_tn("dw_in", dz, xn, BF, 256)

    ax, ay, ac = lax.axis_index("x"), lax.axis_index("y"), lax.axis_index("c")
    idx4 = jnp.stack([4 * cx + 2 * cy + ac for cx, cy in ((ax, ay), (1 - ax, ay), (ax, 1 - ay), (1 - ax, 1 - ay))])
    idx4 = idx4.astype(jnp.int32)
    idx_me, idx_0 = idx4[0:1], jnp.zeros((1,), jnp.int32)
    d_small = jnp.concatenate([d_w2p[:64], d_a2p[64:], d_g2], axis=0).astype(BF)
    g8s = [dw_in_t.reshape(N_DEV, 1184, D), _cols_to_blocks(d_small, LANES)]
    recv4s = _reduce_pair(g8s)
    sums = [_pair_sum("pair_sum_" + n, idx4, g, r) for n, g, r in zip(("w_in", "small"), g8s, recv4s)]
    swap_ssem, swap_rsem, swap_srcs, swap_lands, token = _chip_swap_start([s[1] for s in sums])
    d_pre1, dx = _dxn_pre1_bwd(dz, fw_in_t, x2, dx_a, attn_pre_norm, 256, token)
    grad_x = dx.reshape(x.shape)

    sh_out = [dict() for _ in range(4)]
    done = []
    for n, own, recv in zip(_BIG[1:] + ("conv_w",), early, early_scatter.results):
        tr = (lambda a: a.T) if n == "w_up" else (lambda a: a)
        res = _adam_sharded("adam_" + n, idx_me, own, recv, *[tr(src[n][0]) for src in (w, mo, vo)], after=token)
        done.append(res[0])
        for kind in range(4):
            sh_out[kind][n] = tr(res[kind])[None]

    rg = dict(attn_pre_norm=d_pre1, hgrn_lb=d_lb, hgrn_gnorm=d_gn, rwkv_mu=d_mu, rwkv_w0=d_w0, rwkv_a0=d_a0,
              rwkv_k_k=d_kk, rwkv_k_a=d_ka, rwkv_r_k=d_rk, rwkv_ln_w=d_lnw, rwkv_ln_b=d_lnb, attn_post_norm=d_post,
              ffn_pre_norm=d_pre2, conv_b=dcb, ffn_post_norm=d_ffn_post)
    g8 = _all_gather_small("gather_small_grads", _pack_replicated(rg, loss_acc, done))
    rnames = [n for n, _ in REPL]
    flat = lambda src: [src[n].reshape(1, D) if n == "rwkv_r_k" else src[n] for n in rnames]
    rp_out, loss_row = _adam_replicated(g8, flat(w), flat(mo), flat(vo))
    loss = loss_row[0, 0]
    recv3s = _chip_swap_wait(swap_ssem, swap_rsem, swap_srcs, swap_lands, rp_out[0]["attn_pre_norm"])
    for kind in range(4):
        rp_out[kind]["rwkv_r_k"] = rp_out[kind]["rwkv_r_k"].reshape(rwkv_r_k.shape)

    def small_of(src):
        return jnp.concatenate([src["rwkv_w2"][0], src["rwkv_a2"][0], src["rwkv_g2"][0]], axis=0)

    res = _adam_sharded("adam_w_in", idx_0, sums[0][0][None], recv3s[0], *[src["w_in"][0].T for src in (w, mo, vo)])
    res_s = _adam_sharded("adam_small", idx_0, sums[1][0][None], recv3s[1], *[small_of(src) for src in (w, mo, vo)])
    for kind in range(4):
        sh_out[kind]["w_in"] = res[kind].T[None]
        sh_out[kind]["rwkv_w2"] = res_s[kind][0:64][None]
        sh_out[kind]["rwkv_a2"] = res_s[kind][64:128][None]
        sh_out[kind]["rwkv_g2"] = res_s[kind][128:256][None]

    outs = [loss, grad_x]
    for kind in range(4):
        for name in _WEIGHTS:
            outs.append(sh_out[kind][name] if name in sh_out[kind] else rp_out[kind][name])
    return tuple(outs)
```

```python
import functools

import jax
import jax.numpy as jnp
from jax import lax
from jax.experimental import pallas as pl
from jax.experimental.pallas import tpu as pltpu

F32 = jnp.float32
BF = jnp.bfloat16
MESH = pl.DeviceIdType.MESH

D = 1024
HG_HEADS = 8
HG_K = 128
HG_CHUNK = 32
HG_SCALE = HG_K ** -0.5
HG_PER_STEP = 8
RW_HEADS = 16
RW_N = 64
RW_CHUNK = 64
RW_PAIRS_PER_STEP = 8
DFF = 2816
IN_COLS = 9472
RW_COLS = 3328
EPS = 1e-6
GN_EPS = 1e-5 * RW_N
ADAM_LR = 0.001
ADAM_B1 = 0.9
ADAM_B2 = 0.999
ADAM_EPS = 1e-08
ADAM_WD = 0.01
ADAM_STEP = 10
N_DEV = 8
LANES = 128
SUBLANES = 8
VMEM_LIMIT = 56 * 1024 * 1024
TILE_BYTES = 1280 * 1024

REPL = (("attn_pre_norm", 1024), ("hgrn_lb", 1024), ("hgrn_gnorm", 1024), ("rwkv_mu", 3328), ("rwkv_w0", 1024),
        ("rwkv_a0", 1024), ("rwkv_k_k", 1024), ("rwkv_k_a", 1024), ("rwkv_r_k", 1024), ("rwkv_ln_w", 1024),
        ("rwkv_ln_b", 1024), ("attn_post_norm", 1024), ("ffn_pre_norm", 1024), ("conv_b", 5632), ("ffn_post_norm", 1024))
REPL_ROWS = {"hgrn_lb": 2}
REPL_TOTAL = 32


def _cparams(sem=None, **kw):
    return pltpu.CompilerParams(dimension_semantics=sem, vmem_limit_bytes=VMEM_LIMIT, **kw)


_DN = {"nn": ((1,), (0,)), "nt": ((1,), (1,)), "tn": ((0,), (0,))}


def _raw_dot(a, b, mode):
    return lax.dot_general(a.astype(BF), b.astype(BF), (_DN[mode], ((), ())), preferred_element_type=F32)


@functools.partial(jax.custom_vjp, nondiff_argnums=(2,))
def _dot(a, b, mode):
    return _raw_dot(a, b, mode)


def _dot_fwd(a, b, mode):
    return _raw_dot(a, b, mode), (a, b)


def _dot_bwd(mode, res, g):
    a, b = res
    if mode == "nn":
        return _dot(g, b, "nt"), _dot(a, g, "tn")
    if mode == "nt":
        return _dot(g, b, "nn"), _dot(g, a, "tn")
    return _dot(b, g, "nt"), _dot(a, g, "nn")


_dot.defvjp(_dot_fwd, _dot_bwd)


def _bf_pieces(x, n):
    out, r = [], x
    for i in range(n):
        p = r.astype(BF)
        out.append(p)
        if i + 1 < n:
            r = r - p.astype(F32)
    return out


def _raw_split_dot(x, e, mode, n, x_left):
    eb = e.astype(BF)
    acc = None
    for p in _bf_pieces(x, n):
        ops = (p, eb) if x_left else (eb, p)
        t = lax.dot_general(*ops, (_DN[mode], ((), ())), preferred_element_type=F32)
        acc = t if acc is None else acc + t
    return acc


def _raw_headsum(x):
    t = x.shape[0]
    i = lax.broadcasted_iota(jnp.int32, (LANES, LANES), 0)
    j = lax.broadcasted_iota(jnp.int32, (LANES, LANES), 1)
    same = jnp.where((i >= RW_N) == (j >= RW_N), 1.0, 0.0).astype(F32)
    groups = x.shape[1] // LANES
    rows = jnp.concatenate([x[:, q * LANES:(q + 1) * LANES] for q in range(groups)], axis=0)
    s = _raw_split_dot(rows, same, "nn", 2, True)
    return jnp.concatenate([s[q * t:(q + 1) * t] for q in range(groups)], axis=1)


@jax.custom_vjp
def _headsum(x):
    return _raw_headsum(x)


def _headsum_fwd(x):
    return _raw_headsum(x), None


def _headsum_bwd(_, g):
    return (_raw_headsum(g),)


_headsum.defvjp(_headsum_fwd, _headsum_bwd)


@functools.partial(jax.custom_vjp, nondiff_argnums=(2,))
def _tdot(tri, x, n):
    return _raw_split_dot(x, tri, "nn", n, False)


def _tdot_fwd(tri, x, n):
    return _raw_split_dot(x, tri, "nn", n, False), tri


def _tdot_bwd(n, tri, g):
    return jnp.zeros_like(tri), _raw_split_dot(g, tri, "tn", n, False)


_tdot.defvjp(_tdot_fwd, _tdot_bwd)


def _row(x, i):
    r = lax.broadcasted_iota(jnp.int32, x.shape, 0)
    return jnp.sum(jnp.where(r == i, x, 0.0), axis=0, keepdims=True)


def _shift_down(x, prev):
    t = x.shape[0]

    @jax.custom_vjp
    def sh(x, prev):
        r = lax.broadcasted_iota(jnp.int32, x.shape, 0)
        return jnp.where(r == 0, prev, pltpu.roll(x, 1, 0))

    def fwd(x, prev):
        return sh(x, prev), None

    def bwd(_, g):
        r = lax.broadcasted_iota(jnp.int32, g.shape, 0)
        dx = jnp.where(r == t - 1, 0.0, pltpu.roll(g, t - 1, 0))
        return dx, jnp.sum(jnp.where(r == 0, g, 0.0), axis=0, keepdims=True)

    sh.defvjp(fwd, bwd)
    return sh(x, prev)


def _sigmoid(x):
    return jax.nn.sigmoid(x)


def _silu(x):
    return x * jax.nn.sigmoid(x)


def _softplus(x):
    return jnp.maximum(x, 0.0) + jnp.log(1.0 + jnp.exp(-jnp.abs(x)))


def _rms(x, g):
    return (x * lax.rsqrt(jnp.mean(x * x, axis=-1, keepdims=True) + EPS)) * g


def _tril(c):
    r = lax.broadcasted_iota(jnp.int32, (c, c), 0)
    cc = lax.broadcasted_iota(jnp.int32, (c, c), 1)
    return cc <= r


def _f_pre1_residual(ps, xs, cs):
    return [_rms(xs[0], ps[0]), xs[0]], []


def _f_hgrn(ps, xs, cs):
    lbraw, gn = ps
    hq, hf, hi, hg = xs
    hd = range(HG_PER_STEP)
    st = [cs[0][p * HG_K:(p + 1) * HG_K] for p in hd]
    l0, l1 = _row(lbraw, 0), _row(lbraw, 1)
    m = jnp.maximum(l0, l1)
    e0, e1 = jnp.exp(l0 - m), jnp.exp(l1 - m)
    lb = e0 / (e0 + e1)
    q = _silu(hq) * HG_SCALE
    f = lb + (1.0 - lb) * _sigmoid(hf)
    kh = 1.0 - f
    gl = jnp.log(f)
    c = HG_CHUNK
    low = _tril(c)
    tri = jnp.where(low, 1.0, 0.0).astype(F32)
    outs = []
    for i in range(hq.shape[0] // c):
        rows = slice(i * c, (i + 1) * c)
        b = _tdot(tri, gl[rows], 3)
        bref = _row(b, c // 2 - 1)
        blast = _row(b, c - 1)
        qi = q[rows] * jnp.exp(b - bref)
        ki = kh[rows] * jnp.exp(bref - b)
        qd = q[rows] * jnp.exp(b)
        kd = kh[rows] * jnp.exp(blast - b)
        dec = jnp.exp(blast)
        sl = [slice(p * HG_K, (p + 1) * HG_K) for p in hd]
        sc = [jnp.where(low, _dot(qi[:, sl[p]], ki[:, sl[p]], "nt"), 0.0) for p in hd]
        o = [_dot(sc[p], hi[rows, sl[p]], "nn") + _dot(qd[:, sl[p]], st[p], "nt") for p in hd]
        u = [_dot(hi[rows, sl[p]], kd[:, sl[p]], "tn") for p in hd]
        st = [dec[:, sl[p]] * st[p] + u[p] for p in hd]
        outs.append(jnp.concatenate(o, axis=1) if len(o) > 1 else o[0])
    o = outs[0] if len(outs) == 1 else jnp.concatenate(outs, axis=0)
    on = []
    for p in hd:
        op = o[:, p * HG_K:(p + 1) * HG_K]
        on.append(op * lax.rsqrt(jnp.mean(op * op, axis=-1, keepdims=True) + EPS))
    o = jnp.concatenate(on, axis=1) if len(on) > 1 else on[0]
    o = o * gn
    return [o * _silu(hg)], [jnp.concatenate(st, axis=0) if len(st) > 1 else st[0]]


_RW_OFFS = (0, 1024, 2048, 3072, 3200, 3328)


def _f_rwpre(ps, xs, cs):
    mu, w0, w2p, a0, a2p, g2, k_k, k_a = ps
    (prev,) = cs
    t = xs[0].shape[0]
    zs = []
    for i, z in enumerate(xs):
        lo, hi = _RW_OFFS[i], _RW_OFFS[i + 1]
        zs.append(z + mu[:, lo:hi] * (_shift_down(z, prev[:, lo:hi]) - z))
    rr, kr, vr, wa, gz = zs
    w_log = -_softplus(-(w0 + _dot(jnp.tanh(wa), w2p, "nn"))) - 0.5
    lw = -jnp.exp(w_log)
    a = _sigmoid(a0 + _dot(wa, a2p, "nn"))
    g = _dot(_sigmoid(gz), g2, "nn")
    kkr = kr * k_k
    kk = kkr / jnp.maximum(jnp.sqrt(_headsum(kkr * kkr)), 1e-12)
    k2 = kr * (1.0 + (a - 1.0) * k_a)
    newprev = jnp.concatenate([_row(z, t - 1) for z in xs], axis=1)
    return [rr, lw, k2, vr, -kk, kk * a, g], [newprev]


def _raw_inverses(ls):
    n = ls[0].shape[0]
    r = lax.broadcasted_iota(jnp.int32, (n, n), 0)
    c = lax.broadcasted_iota(jnp.int32, (n, n), 1)
    eye = jnp.where(r == c, 1.0, 0.0).astype(F32)
    tinv = [eye + l for l in ls]
    pw = ls
    for _ in range(5):
        pw = [_raw_dot(p, p, "nn") for p in pw]
        tinv = [t + _raw_dot(t, p, "nn") for t, p in zip(tinv, pw)]
    return tinv


@jax.custom_vjp
def _unit_lower_inverses(ls):
    return _raw_inverses(ls)


def _inverses_fwd(ls):
    tinv = _raw_inverses(ls)
    return tinv, tinv


def _inverses_bwd(tinv, gs):
    return ([_raw_dot(_raw_dot(t, g, "tn"), t, "nt") for t, g in zip(tinv, gs)],)


_unit_lower_inverses.defvjp(_inverses_fwd, _inverses_bwd)


@jax.custom_vjp
def _known_inverses(ls, tinv):
    return tinv


def _known_fwd(ls, tinv):
    return tinv, tinv


def _known_bwd(tinv, gs):
    return [_raw_dot(_raw_dot(t, g, "tn"), t, "nt") for t, g in zip(tinv, gs)], [jnp.zeros_like(t) for t in tinv]


_known_inverses.defvjp(_known_fwd, _known_bwd)


@jax.custom_vjp
def _use_kept(computed, kept):
    return kept


def _use_kept_fwd(computed, kept):
    return kept, None


def _use_kept_bwd(_, g):
    return g, jax.tree.map(jnp.zeros_like, g)


_use_kept.defvjp(_use_kept_fwd, _use_kept_bwd)

RW_KEPT = 5


def _f_rwscan(ps, xs, cs, kept=None):
    state = cs[0]
    ys, keep = [], []
    n = 2 * RW_CHUNK
    per_chunk = RW_KEPT * RW_PAIRS_PER_STEP * n
    for i in range(xs[0].shape[0] // RW_CHUNK):
        known = None
        if kept is not None:
            known = [[kept[i * per_chunk + (q * RW_PAIRS_PER_STEP + p) * n:
                           i * per_chunk + (q * RW_PAIRS_PER_STEP + p + 1) * n] for p in range(RW_PAIRS_PER_STEP)]
                     for q in range(RW_KEPT)]
        y, state, mats = _rwkv_chunk([x[i * RW_CHUNK:(i + 1) * RW_CHUNK] for x in xs], state, known)
        ys.append(y)
        keep += [m for group in mats for m in group]
    return [ys[0] if len(ys) == 1 else jnp.concatenate(ys, axis=0)], [state], jnp.concatenate(keep, axis=0)


def _rwkv_chunk(xs, state, known=None):
    npair = RW_PAIRS_PER_STEP
    pr = range(npair)
    r, lw, k, v, av, bv = [[x[:, p * LANES:(p + 1) * LANES] for p in pr] for x in xs]
    sv = [state[p * LANES:(p + 1) * LANES] for p in pr]
    c = RW_CHUNK
    n = 2 * c
    tri = jnp.where(_tril(c), 1.0, 0.0).astype(F32)
    cl = [_tdot(tri, lw[p], 3) for p in pr]
    cl_last = [_row(cl[p], c - 1) for p in pr]
    lane = lax.broadcasted_iota(jnp.int32, (c, LANES), 1)
    h0 = lane < RW_N

    def stack(x):
        return jnp.concatenate([jnp.where(h0, x, 0.0), jnp.where(h0, 0.0, x)], axis=0)

    am = [stack(av[p] * jnp.exp(cl[p] - lw[p])) for p in pr]
    bm = [stack(bv[p] * jnp.exp(-cl[p])) for p in pr]
    km = [stack(k[p] * jnp.exp(-cl[p])) for p in pr]
    rm = [stack(r[p] * jnp.exp(cl[p])) for p in pr]
    vm = [stack(v[p]) for p in pr]
    rn = lax.broadcasted_iota(jnp.int32, (n, n), 0)
    cn = lax.broadcasted_iota(jnp.int32, (n, n), 1)
    blk = (rn >= c) == (cn >= c)
    strict = blk & (cn < rn)
    incl = blk & (cn <= rn)
    lab = [jnp.where(strict, _dot(am[p], bm[p], "nt"), 0.0) for p in pr]
    lak = [jnp.where(strict, _dot(am[p], km[p], "nt"), 0.0) for p in pr]
    wrb = [jnp.where(incl, _dot(rm[p], bm[p], "nt"), 0.0) for p in pr]
    wrk = [jnp.where(incl, _dot(rm[p], km[p], "nt"), 0.0) for p in pr]
    if known is None:
        tinv = _unit_lower_inverses(lab)
    else:
        tinv = _known_inverses(lab, known[0])
        lak, wrb, wrk = _use_kept(lak, known[1]), _use_kept(wrb, known[2]), _use_kept(wrk, known[3])
    rhs = [_dot(am[p], sv[p], "nt") + _dot(lak[p], vm[p], "nn") for p in pr]
    um = [_dot(tinv[p], rhs[p], "nn") for p in pr]
    if known is not None:
        um = _use_kept(um, known[4])
    ym = [_dot(rm[p], sv[p], "nt") + _dot(wrb[p], um[p], "nn") + _dot(wrk[p], vm[p], "nn") for p in pr]
    sn = [(sv[p] + _dot(um[p], bm[p], "tn") + _dot(vm[p], km[p], "tn")) * jnp.exp(cl_last[p]) for p in pr]
    ys = [ym[p][:c] + ym[p][c:] for p in pr]
    return jnp.concatenate(ys, axis=1), jnp.concatenate(sn, axis=0), [tinv, lak, wrb, wrk, um]


def _f_mixers(ps, xs, cs):
    return _mixers(ps, xs, cs, None)


def _f_mixers_kept(ps, xs, cs, kept):
    return _mixers(ps, xs, cs, kept[0])[:2]


def _mixers(ps, xs, cs, kept):
    oa, st = _f_hgrn(ps[:2], xs[:4], cs[:1])
    (r, lw, k, v, av, bv, g), prev = _f_rwpre(ps[2:10], xs[4:], cs[1:2])
    y, sv, keep = _f_rwscan([], [r, lw, k, v, av, bv], cs[2:], kept)
    ob, _ = _f_rwpost(ps[10:], y + [r, k, v, g], [])
    return oa + ob, st + prev + sv, [keep]


def _f_rwpost(ps, xs, cs):
    ln_w, ln_b, r_k = ps
    y, r, k, v, g = xs
    inv_n = 1.0 / RW_N
    yc = y - _headsum(y) * inv_n
    var = _headsum(yc * yc) * inv_n
    yn = yc * lax.rsqrt(var + GN_EPS)
    yn = yn * ln_w + ln_b
    bonus = _headsum(r * k * r_k) * v
    return [(yn + bonus) * g], []


def _f_merge(ps, xs, cs):
    ga, gb, ya, yb = xs
    return [_sigmoid(ga) * ya + _sigmoid(gb) * yb], []


def _f_post1(ps, xs, cs):
    x, mix = xs
    h1 = x + _rms(mix, ps[0])
    return [h1, _rms(h1, ps[1])], []


def _f_conv(ps, xs, cs):
    cw, cb = ps
    p1, p2 = cs
    w0, w1, w2 = _row(cw, 0), _row(cw, 1), _row(cw, 2)
    t = xs[0].shape[0]
    hc = []
    for i, x in enumerate(xs):
        sl = slice(i * DFF, (i + 1) * DFF)
        s1 = _shift_down(x, p1[:, sl])
        s2 = _shift_down(s1, p2[:, sl])
        hc.append(cb[:, sl] + w0[:, sl] * s2 + w1[:, sl] * s1 + w2[:, sl] * x)
    n1 = jnp.concatenate([_row(x, t - 1) for x in xs], axis=1)
    n2 = jnp.concatenate([_row(x, t - 2) for x in xs], axis=1)
    return [_silu(hc[0]) * hc[1]], [n1, n2]


class _Stage:
    def __init__(self, name, f, g, tm, par_per_g, in_pieces, in_offs, carry_shapes, out_pieces, out_dtypes,
                 kept_shapes=(), f_kept=None):
        self.name, self.f, self.g, self.tm = name, f, g, tm
        self.par_per_g, self.in_pieces, self.in_offs = par_per_g, in_pieces, in_offs
        self.carry_shapes, self.out_pieces, self.out_dtypes = carry_shapes, out_pieces, out_dtypes
        self.kept_shapes, self.f_kept = list(kept_shapes), f_kept


def _par_spec(arr, per_g, g):
    r, c = arr.shape
    if per_g:
        return pl.BlockSpec((r, c // g), lambda gi, ni: (0, gi))
    return pl.BlockSpec((r, c), lambda gi, ni: (0, 0))


def _row_spec(tm, width, off, n, rev):
    if rev:
        return pl.BlockSpec((tm, width), lambda gi, ni: (n - 1 - ni, off + gi))
    return pl.BlockSpec((tm, width), lambda gi, ni: (ni, off + gi))


def _carry_spec(shape, n, rev):
    if rev:
        return pl.BlockSpec((None, None) + shape, lambda gi, ni: (gi, n - 1 - ni, 0, 0))
    return pl.BlockSpec((None, None) + shape, lambda gi, ni: (gi, ni, 0, 0))


def _load_pieces(refs, pieces_list):
    out = []
    for ref, pieces in zip(refs, pieces_list):
        o = 0
        for w in pieces:
            out.append(ref[:, o:o + w].astype(F32))
            o += w
    return out


def _store_pieces(refs, pieces_list, vals):
    k = 0
    for ref, pieces in zip(refs, pieces_list):
        o = 0
        for w in pieces:
            ref[:, o:o + w] = vals[k].astype(ref.dtype)
            k += 1
            o += w


_ANY = pl.BlockSpec(memory_space=pl.ANY)


class _Exchange:
    def __init__(self, kind, arrs):
        self.kind, self.arrs, self.results = kind, list(arrs), None
        if kind == "scatter":
            self.out_shape = [jax.ShapeDtypeStruct((N_DEV - 1,) + a.shape[1:], a.dtype) for a in self.arrs]
        else:
            self.out_shape = [jax.ShapeDtypeStruct((N_DEV,) + a.shape, a.dtype) for a in self.arrs]
        self.nsem = (N_DEV if kind == "gather2" else N_DEV - 1) * len(self.arrs)

    def copies(self, in_refs, out_refs, ssem, rsem):
        x, y, c = lax.axis_index("x"), lax.axis_index("y"), lax.axis_index("c")
        me = 4 * x + 2 * y + c
        cps = []
        for a, (i_ref, o_ref) in enumerate(zip(in_refs, out_refs)):
            for j in range(1, N_DEV):
                px = 1 - x if j & 4 else x
                py = 1 - y if j & 2 else y
                pc = 1 - c if j & 1 else c
                if self.kind == "gather":
                    src, dst = i_ref, o_ref.at[me]
                else:
                    src, dst = i_ref.at[4 * px + 2 * py + pc], o_ref.at[j - 1]
                s = (N_DEV - 1) * a + j - 1
                cps.append(pltpu.make_async_remote_copy(src_ref=src, dst_ref=dst, send_sem=ssem.at[s],
                                                        recv_sem=rsem.at[s], device_id=(px, py, pc),
                                                        device_id_type=MESH))
        return cps

    def run(self, step, total, in_refs, out_refs, ssem, rsem):
        if self.kind == "gather2":
            return self.run_two_level(step, total, in_refs, out_refs, ssem, rsem)

        @pl.when(step == 0)
        def _():
            for cp in self.copies(in_refs, out_refs, ssem, rsem):
                cp.start()

        @pl.when(step == total - 1)
        def _():
            for cp in self.copies(in_refs, out_refs, ssem, rsem):
                cp.wait()

    def run_two_level(self, step, total, in_refs, out_refs, ssem, rsem):
        x, y, c = lax.axis_index("x"), lax.axis_index("y"), lax.axis_index("c")
        sibling, xn, yn = (x, y, 1 - c), (1 - x, y, c), (x, 1 - y, c)
        arrs = range(len(in_refs))
        ns = N_DEV

        def num(px, py, pc):
            return 4 * px + 2 * py + pc

        def copy(a, k, to, src, dst):
            return pltpu.make_async_remote_copy(src_ref=src, dst_ref=dst, send_sem=ssem.at[ns * a + k],
                                                recv_sem=rsem.at[ns * a + k], device_id=to, device_id_type=MESH)

        def blk(a, b):
            return out_refs[a].at[b]

        def half(a, b, second):
            h = self.arrs[a].shape[0] // 2
            return out_refs[a].at[b, pl.ds(h if second else 0, h)]

        bx, by, bd = num(1 - x, y, c), num(x, 1 - y, c), num(1 - x, 1 - y, c)

        def firsts(a):
            own = blk(a, num(x, y, c))
            return [copy(a, 0, sibling, in_refs[a], own), copy(a, 1, xn, in_refs[a], own),
                    copy(a, 2, yn, in_refs[a], own)]

        def seconds(a):
            return [copy(a, 3, yn, half(a, bx, False), half(a, bx, False)), copy(a, 5, sibling, blk(a, bx), blk(a, bx)),
                    copy(a, 4, xn, half(a, by, True), half(a, by, True)), copy(a, 6, sibling, blk(a, by), blk(a, by))]

        def third(a):
            return copy(a, 7, sibling, blk(a, bd), blk(a, bd))

        @pl.when(step == 0)
        def _():
            for a in arrs:
                for cp in firsts(a):
                    cp.start()

        @pl.when(step == total // 2)
        def _():
            for a in arrs:
                copy(a, 1, xn, blk(a, bx), blk(a, bx)).wait_recv()
                copy(a, 2, yn, blk(a, by), blk(a, by)).wait_recv()
                for cp in seconds(a):
                    cp.start()

        @pl.when(step == (4 * total) // 5)
        def _():
            for a in arrs:
                copy(a, 3, yn, half(a, bd, False), half(a, bd, False)).wait_recv()
                copy(a, 4, xn, half(a, bd, True), half(a, bd, True)).wait_recv()
                third(a).start()

        @pl.when(step == total - 1)
        def _():
            for a in arrs:
                for k, b in ((0, num(x, y, 1 - c)), (5, num(1 - x, y, 1 - c)), (6, num(x, 1 - y, 1 - c)),
                             (7, num(1 - x, 1 - y, 1 - c))):
                    copy(a, k, sibling, blk(a, b), blk(a, b)).wait_recv()
                for cp in firsts(a) + seconds(a) + [third(a)]:
                    cp.wait_send()


def _hook_specs(hook):
    if hook is None:
        return [], [], [], []
    na = len(hook.arrs)
    sems = [pltpu.SemaphoreType.DMA((hook.nsem,)), pltpu.SemaphoreType.DMA((hook.nsem,))]
    return [_ANY] * na, [_ANY] * na, hook.out_shape, sems


def _stage_fwd(st, t, params, inputs, hook=None):
    g, tm = st.g, min(st.tm, t)
    n = t // tm
    npar, nin, ncar, nout = len(params), len(inputs), len(st.carry_shapes), len(st.out_pieces)
    nk = len(st.kept_shapes)
    h_in, h_out, h_shape, h_sems = _hook_specs(hook)
    nh = len(h_in)

    def body(*refs):
        p_refs = refs[:npar]
        x_refs = refs[npar:npar + nin]
        hi_refs = refs[npar + nin:npar + nin + nh]
        o = npar + nin + nh
        o_refs = refs[o:o + nout]
        s_refs = refs[o + nout:o + nout + ncar]
        k_refs = refs[o + nout + ncar:o + nout + ncar + nk]
        o += nout + ncar + nk
        ho_refs = refs[o:o + nh]
        c_scr = refs[o + nh:o + nh + ncar]
        gi, ni = pl.program_id(0), pl.program_id(1)
        if hook is not None:
            step = gi * n + ni
            hook.run(step, g * n, hi_refs, ho_refs, *refs[-2:])

        @pl.when(ni == 0)
        def _():
            for c in c_scr:
                c[...] = jnp.zeros(c.shape, F32)

        ps = [r[...].astype(F32) for r in p_refs]
        xs = _load_pieces(x_refs, st.in_pieces)
        cs = [c[...] for c in c_scr]
        for s, c in zip(s_refs, cs):
            s[...] = c
        res = st.f(ps, xs, cs)
        outs, ncs = res[0], res[1]
        _store_pieces(o_refs, st.out_pieces, outs)
        for c, v in zip(c_scr, ncs):
            c[...] = v
        for kr, kv in zip(k_refs, res[2] if nk else []):
            kr[...] = kv.astype(kr.dtype)

    in_specs = [_par_spec(p, pg, g) for p, pg in zip(params, st.par_per_g)]
    in_specs += [_row_spec(tm, sum(pc), off, n, False) for pc, off in zip(st.in_pieces, st.in_offs)]
    out_specs = [_row_spec(tm, sum(pc), 0, n, False) for pc in st.out_pieces]
    out_specs += [_carry_spec(s, n, False) for s in st.carry_shapes]
    out_specs += [pl.BlockSpec(s, lambda gi, ni: (ni, 0)) for s in st.kept_shapes]
    out_shape = [jax.ShapeDtypeStruct((t, g * sum(pc)), dt) for pc, dt in zip(st.out_pieces, st.out_dtypes)]
    out_shape += [jax.ShapeDtypeStruct((g, n) + s, F32) for s in st.carry_shapes]
    out_shape += [jax.ShapeDtypeStruct((n * s[0], s[1]), BF) for s in st.kept_shapes]
    res = pl.pallas_call(
        body, name=st.name + "_fwd", grid=(g, n), in_specs=in_specs + h_in, out_specs=out_specs + h_out,
        out_shape=out_shape + h_shape,
        scratch_shapes=[pltpu.VMEM(s, F32) for s in st.carry_shapes] + h_sems,
        compiler_params=_cparams(("arbitrary", "arbitrary")),
    )(*params, *inputs, *(hook.arrs if hook else []))
    if hook is not None:
        hook.results = list(res[nout + ncar + nk:])
    return list(res[:nout]), list(res[nout:nout + ncar + nk])


def _stage_bwd(st, t, params, inputs, saved, douts, dx_dtypes, hook=None, dout_dot=None):
    g, tm = st.g, min(st.tm, t)
    n = t // tm
    npar, nin, ncar = len(params), len(inputs), len(st.carry_shapes)
    nk = len(st.kept_shapes)
    flat_d = list(dout_dot) if dout_dot is not None else [d for ds in douts for d in ds]
    nd = len(flat_d)
    dx_idx = [i for i, dt in enumerate(dx_dtypes) if dt is not None]
    h_in, h_out, h_shape, h_sems = _hook_specs(hook)
    nh = len(h_in)

    def body(*refs):
        p_refs = refs[:npar]
        x_refs = refs[npar:npar + nin]
        s_refs = refs[npar + nin:npar + nin + ncar]
        k_refs = refs[npar + nin + ncar:npar + nin + ncar + nk]
        o = npar + nin + ncar + nk
        d_refs = refs[o:o + nd]
        hi_refs = refs[o + nd:o + nd + nh]
        o += nd + nh
        dp_refs = refs[o:o + npar]
        dx_refs = refs[o + npar:o + npar + len(dx_idx)]
        ho_refs = refs[o + npar + len(dx_idx):o + npar + len(dx_idx) + nh]
        dc_scr = refs[o + npar + len(dx_idx) + nh:o + npar + len(dx_idx) + nh + ncar]
        gi, ni = pl.program_id(0), pl.program_id(1)
        if hook is not None:
            step = gi * n + ni
            hook.run(step, g * n, hi_refs, ho_refs, *refs[-2:])

        @pl.when(ni == 0)
        def _():
            for c in dc_scr:
                c[...] = jnp.zeros(c.shape, F32)

        ps = [r[...].astype(F32) for r in p_refs]
        xs = _load_pieces(x_refs, st.in_pieces)
        cs = [s[...] for s in s_refs]
        dys = [_raw_dot(d_refs[0][...], d_refs[1][...], "nt")] if dout_dot is not None else []
        k = 0
        for ds, pieces in zip(douts, st.out_pieces):
            acc = _load_pieces([d_refs[k]], [pieces])
            for j in range(1, len(ds)):
                more = _load_pieces([d_refs[k + j]], [pieces])
                acc = [a + b for a, b in zip(acc, more)]
            dys += acc
            k += len(ds)
        if nk:
            kept = [r[...].astype(F32) for r in k_refs]
            _, vjp = jax.vjp(lambda p, x, c: st.f_kept(p, x, c, kept), ps, xs, cs)
        else:
            _, vjp = jax.vjp(st.f, ps, xs, cs)
        dps, dxs, dcs = vjp((dys, [c[...] for c in dc_scr]))
        k = 0
        per_in = []
        for pieces in st.in_pieces:
            per_in.append(dxs[k:k + len(pieces)])
            k += len(pieces)
        for ref, i in zip(dx_refs, dx_idx):
            _store_pieces([ref], [st.in_pieces[i]], per_in[i])
        for c, v in zip(dc_scr, dcs):
            c[...] = v
        for ref, dp, pg in zip(dp_refs, dps, st.par_per_g):
            first = (ni == 0) if pg else ((ni == 0) & (gi == 0))

            @pl.when(first)
            def _():
                ref[...] = jnp.zeros(ref.shape, F32)

            ref[...] += dp

    in_specs = [_par_spec(p, pg, g) for p, pg in zip(params, st.par_per_g)]
    in_specs += [_row_spec(tm, sum(pc), off, n, True) for pc, off in zip(st.in_pieces, st.in_offs)]
    in_specs += [_carry_spec(s, n, True) for s in st.carry_shapes]
    in_specs += [pl.BlockSpec(s, lambda gi, ni: (n - 1 - ni, 0)) for s in st.kept_shapes]
    for ds, pc in zip(douts, st.out_pieces):
        in_specs += [_row_spec(tm, sum(pc), 0, n, True) for _ in ds]
    if dout_dot is not None:
        a, w = dout_dot
        in_specs += [pl.BlockSpec((tm, a.shape[1]), lambda gi, ni: (n - 1 - ni, 0)),
                     pl.BlockSpec(w.shape, lambda gi, ni: (0, 0), pipeline_mode=pl.Buffered(1))]
    out_specs = [_par_spec(p, pg, g) for p, pg in zip(params, st.par_per_g)]
    out_specs += [_row_spec(tm, sum(st.in_pieces[i]), 0, n, True) for i in dx_idx]
    out_shape = [jax.ShapeDtypeStruct(p.shape, F32) for p in params]
    out_shape += [jax.ShapeDtypeStruct((t, g * sum(st.in_pieces[i])), dx_dtypes[i]) for i in dx_idx]
    res = pl.pallas_call(
        body, name=st.name + "_bwd", grid=(g, n), in_specs=in_specs + h_in, out_specs=out_specs + h_out,
        out_shape=out_shape + h_shape,
        scratch_shapes=[pltpu.VMEM(s, F32) for s in st.carry_shapes] + h_sems,
        compiler_params=_cparams(("arbitrary", "arbitrary")),
    )(*params, *inputs, *saved, *flat_d, *(hook.arrs if hook else []))
    if hook is not None:
        hook.results = list(res[npar + len(dx_idx):])
    return list(res[:npar]), list(res[npar:npar + len(dx_idx)])


def _pick(n, cap):
    if n <= cap:
        return n
    best = LANES
    for k in range(1, n // LANES + 1):
        if (n // LANES) % k == 0 and k * LANES <= cap:
            best = k * LANES
    return best


def _mm(name, a, b, mode, out_dtype=F32, tm=1024, tn=512, b_outer=False):
    m = a.shape[1] if mode == "tn" else a.shape[0]
    k = a.shape[0] if mode == "tn" else a.shape[1]
    n = b.shape[0] if mode == "nt" else b.shape[1]
    tm, tn = _pick(m, tm), _pick(n, tn)
    if b_outer:
        grid = (n // tn, m // tm)
        ij = lambda p, q: (q, p)
    else:
        grid = (m // tm, n // tn)
        ij = lambda p, q: (p, q)

    def body(a_ref, b_ref, o_ref):
        o_ref[...] = _raw_dot(a_ref[...], b_ref[...], mode).astype(o_ref.dtype)

    if mode == "tn":
        a_spec = pl.BlockSpec((k, tm), lambda p, q: (0, ij(p, q)[0]))
    else:
        a_spec = pl.BlockSpec((tm, k), lambda p, q: (ij(p, q)[0], 0))
    b_mode = dict(pipeline_mode=pl.Buffered(1)) if tn == n else {}
    if mode == "nt":
        b_spec = pl.BlockSpec((tn, k), lambda p, q: (ij(p, q)[1], 0), **b_mode)
    else:
        b_spec = pl.BlockSpec((k, tn), lambda p, q: (0, ij(p, q)[1]), **b_mode)
    return pl.pallas_call(
        body, name=name, grid=grid, in_specs=[a_spec, b_spec],
        out_specs=pl.BlockSpec((tm, tn), lambda p, q: ij(p, q)),
        out_shape=jax.ShapeDtypeStruct((m, n), out_dtype),
        compiler_params=_cparams(("arbitrary", "arbitrary")),
    )(a, b)


def _mm_multi(name, pairs, mode, out_dtype, tm=1024, tn=512):
    a0, b0 = pairs[0]
    m = a0.shape[1] if mode == "tn" else a0.shape[0]
    k = a0.shape[0] if mode == "tn" else a0.shape[1]
    n = b0.shape[0] if mode == "nt" else b0.shape[1]
    tm, tn = _pick(m, tm), _pick(n, tn)
    npair = len(pairs)

    def body(*refs):
        for p in range(npair):
            refs[2 * npair + p][...] = _raw_dot(refs[2 * p][...], refs[2 * p + 1][...], mode).astype(out_dtype)

    a_spec = pl.BlockSpec((k, tm), lambda i, j: (0, i)) if mode == "tn" else pl.BlockSpec((tm, k), lambda i, j: (i, 0))
    b_spec = pl.BlockSpec((tn, k), lambda i, j: (j, 0)) if mode == "nt" else pl.BlockSpec((k, tn), lambda i, j: (0, j))
    return pl.pallas_call(
        body, name=name, grid=(m // tm, n // tn), in_specs=[a_spec, b_spec] * npair,
        out_specs=[pl.BlockSpec((tm, tn), lambda i, j: (i, j))] * npair,
        out_shape=[jax.ShapeDtypeStruct((m, n), out_dtype)] * npair,
        compiler_params=_cparams(("arbitrary", "arbitrary")),
    )(*[x for pair in pairs for x in pair])


def _mm_cols_tn(name, pieces, b, out_dtype, tm):
    k, n = b.shape
    counts = [p.shape[1] // tm for p in pieces]
    starts = [sum(counts[:i]) for i in range(len(pieces))]
    na = len(pieces)

    def body(*refs):
        b_ref, o_ref = refs[na], refs[-1]
        i = pl.program_id(0)
        for a_ref, s, c in zip(refs[:na], starts, counts):
            @pl.when((i >= s) & (i < s + c))
            def _():
                o_ref[...] = _raw_dot(a_ref[...], b_ref[...], "tn").astype(o_ref.dtype)

    def spec(s, c):
        return pl.BlockSpec((k, tm), lambda i: (0, jnp.clip(i - s, 0, c - 1)))

    return pl.pallas_call(
        body, name=name, grid=(sum(counts),),
        in_specs=[spec(s, c) for s, c in zip(starts, counts)]
        + [pl.BlockSpec(b.shape, lambda i: (0, 0), pipeline_mode=pl.Buffered(1))],
        out_specs=pl.BlockSpec((tm, n), lambda i: (i, 0)),
        out_shape=jax.ShapeDtypeStruct((sum(counts) * tm, n), out_dtype),
        compiler_params=_cparams(("arbitrary",)),
    )(*pieces, b)


def _norm_in_proj(x, g, w_t, tm, tn):
    t, k = x.shape
    n = w_t.shape[0]
    tm, tn = _pick(t, tm), _pick(n, tn)

    def body(x_ref, g_ref, w_ref, xn_ref, z_ref):
        xn = _rms(x_ref[...], g_ref[...]).astype(BF)
        xn_ref[...] = xn
        z_ref[...] = _raw_dot(xn, w_ref[...], "nt")

    xns, z = pl.pallas_call(
        body, name="in_proj", grid=(n // tn, t // tm),
        in_specs=[pl.BlockSpec((tm, k), lambda j, i: (i, 0)), pl.BlockSpec((1, k), lambda j, i: (0, 0)),
                  pl.BlockSpec((tn, k), lambda j, i: (j, 0))],
        out_specs=[pl.BlockSpec((None, tm, k), lambda j, i: (j, i, 0)), pl.BlockSpec((tm, tn), lambda j, i: (i, j))],
        out_shape=[jax.ShapeDtypeStruct((n // tn, t, k), BF), jax.ShapeDtypeStruct((t, n), F32)],
        compiler_params=_cparams(("arbitrary", "arbitrary")),
    )(x, g, w_t)
    return xns[0], z


def _merge_out_post(z, o_a, o_b, w_a, w_b, w_out, x, g_post, g_pre2, tm):
    t = x.shape[0]
    tm = _pick(t, tm)
    w = 256
    npc = D // w
    ga0, gb0 = (IN_COLS - 2 * D) // w, (IN_COLS - D) // w

    def body(*refs):
        ga_refs, gb_refs = refs[:npc], refs[npc:2 * npc]
        oa_ref, ob_ref, wa_ref, wb_ref, w_ref, x_ref, gp_ref, g2_ref = refs[2 * npc:2 * npc + 8]
        ya_ref, yb_ref, m_ref, mix_ref, h_ref, xn_ref = refs[2 * npc + 8:]
        ya = _raw_dot(oa_ref[...], wa_ref[...], "nn").astype(BF)
        yb = _raw_dot(ob_ref[...], wb_ref[...], "nn").astype(BF)
        ya_ref[...] = ya
        yb_ref[...] = yb
        parts = []
        for p in range(npc):
            cols = slice(p * w, (p + 1) * w)
            parts.append(_sigmoid(ga_refs[p][...]) * ya[:, cols].astype(F32)
                         + _sigmoid(gb_refs[p][...]) * yb[:, cols].astype(F32))
        merged = jnp.concatenate(parts, axis=1).astype(BF)
        m_ref[...] = merged
        mix = _raw_dot(merged, w_ref[...], "nn")
        mix_ref[...] = mix
        h1 = x_ref[...] + _rms(mix, gp_ref[...])
        h_ref[...] = h1
        xn_ref[...] = _rms(h1, g2_ref[...]).astype(BF)

    row = pl.BlockSpec((tm, D), lambda i: (i, 0))
    one = pl.BlockSpec((1, D), lambda i: (0, 0))

    def gate(b0):
        return [pl.BlockSpec((tm, w), functools.partial(lambda i, b: (i, b), b=b0 + p)) for p in range(npc)]

    wgt = pl.BlockSpec((D, D), lambda i: (0, 0), pipeline_mode=pl.Buffered(1))
    return pl.pallas_call(
        body, name="merge_out_post", grid=(t // tm,),
        in_specs=gate(ga0) + gate(gb0) + [row, row, wgt, wgt, wgt, row, one, one],
        out_specs=[row] * 6,
        out_shape=[jax.ShapeDtypeStruct((t, D), BF), jax.ShapeDtypeStruct((t, D), BF), jax.ShapeDtypeStruct((t, D), BF),
                   jax.ShapeDtypeStruct((t, D), F32), jax.ShapeDtypeStruct((t, D), F32),
                   jax.ShapeDtypeStruct((t, D), BF)],
        compiler_params=_cparams(("arbitrary",)),
    )(*([z] * (2 * npc)), o_a, o_b, w_a, w_b, w_out, x, g_post, g_pre2)


def _accumulate(ni, refs, vals):
    @pl.when(ni == 0)
    def _():
        for r in refs:
            r[...] = jnp.zeros(r.shape, F32)

    for r, v in zip(refs, vals):
        r[...] += v


def _dmerged_merge_bwd(dmix, w_out, w_a, w_b, z, y_a, y_b, tm):
    t = dmix.shape[0]
    tm = _pick(t, tm)
    w = 256
    npc = D // w
    ga0, gb0 = (IN_COLS - 2 * D) // w, (IN_COLS - D) // w

    def body(*refs):
        dm_ref, w_ref, wa_ref, wb_ref = refs[:4]
        ga_refs, gb_refs = refs[4:4 + npc], refs[4 + npc:4 + 2 * npc]
        ya_ref, yb_ref, dga_ref, dgb_ref, dya_ref, dyb_ref, doa_ref, dob_ref = refs[4 + 2 * npc:]
        dmerged = _raw_dot(dm_ref[...], w_ref[...], "nt")
        dyas, dybs = [], []
        for p in range(npc):
            cols = slice(p * w, (p + 1) * w)
            xs = [ga_refs[p][...], gb_refs[p][...], ya_ref[:, cols].astype(F32), yb_ref[:, cols].astype(F32)]
            _, vjp = jax.vjp(lambda *a: _f_merge([], list(a), [])[0][0], *xs)
            dga, dgb, dya, dyb = vjp(dmerged[:, cols])
            dga_ref[:, cols] = dga.astype(BF)
            dgb_ref[:, cols] = dgb.astype(BF)
            dyas.append(dya.astype(BF))
            dybs.append(dyb.astype(BF))
        dya, dyb = jnp.concatenate(dyas, axis=1), jnp.concatenate(dybs, axis=1)
        dya_ref[...] = dya
        dyb_ref[...] = dyb
        doa_ref[...] = _raw_dot(dya, wa_ref[...], "nt").astype(BF)
        dob_ref[...] = _raw_dot(dyb, wb_ref[...], "nt").astype(BF)

    row = pl.BlockSpec((tm, D), lambda i: (i, 0))
    wgt = pl.BlockSpec((D, D), lambda i: (0, 0), pipeline_mode=pl.Buffered(1))

    def gate(b0):
        return [pl.BlockSpec((tm, w), functools.partial(lambda i, b: (i, b), b=b0 + p)) for p in range(npc)]

    return pl.pallas_call(
        body, name="merge_bwd", grid=(t // tm,),
        in_specs=[row, wgt, wgt, wgt] + gate(ga0) + gate(gb0) + [row, row],
        out_specs=[row] * 6, out_shape=[jax.ShapeDtypeStruct((t, D), BF)] * 6,
        compiler_params=_cparams(("arbitrary",)),
    )(dmix, w_out, w_a, w_b, *([z] * (2 * npc)), y_a, y_b)


def _dxn2_post1_bwd(pieces, w_up_t, x, mix, dh1, g_post, g_pre2, tm):
    t = x.shape[0]
    tm = _pick(t, tm)
    k = w_up_t.shape[0]
    offs = [sum(p.shape[1] for p in pieces[:i]) for i in range(len(pieces))]
    na = len(pieces)

    def body(*refs):
        w_ref, x_ref, m_ref, dh_ref, gp_ref, g2_ref, dgp_ref, dg2_ref, dx_ref, dm_ref = refs[na:]
        dxn2 = None
        for a_ref, off in zip(refs[:na], offs):
            part = _raw_dot(a_ref[...], w_ref[off:off + a_ref.shape[1], :], "nn")
            dxn2 = part if dxn2 is None else dxn2 + part
        _, vjp = jax.vjp(lambda gp, g2, xx, mm: _f_post1([gp, g2], [xx, mm], [])[0],
                         gp_ref[...], g2_ref[...], x_ref[...], m_ref[...])
        dgp, dg2, dx, dm = vjp([dh_ref[...], dxn2])
        _accumulate(pl.program_id(0), [dgp_ref, dg2_ref], [dgp, dg2])
        dx_ref[...] = dx
        dm_ref[...] = dm.astype(BF)

    row = pl.BlockSpec((tm, D), lambda i: (i, 0))
    one = pl.BlockSpec((1, D), lambda i: (0, 0))
    return pl.pallas_call(
        body, name="post1_bwd", grid=(t // tm,),
        in_specs=[pl.BlockSpec((tm, p.shape[1]), lambda i: (i, 0)) for p in pieces]
        + [pl.BlockSpec((k, D), lambda i: (0, 0), pipeline_mode=pl.Buffered(1)), row, row, row, one, one],
        out_specs=[one, one, row, row],
        out_shape=[jax.ShapeDtypeStruct((1, D), F32), jax.ShapeDtypeStruct((1, D), F32),
                   jax.ShapeDtypeStruct((t, D), F32), jax.ShapeDtypeStruct((t, D), BF)],
        compiler_params=_cparams(("arbitrary",)),
    )(*pieces, w_up_t, x, mix, dh1, g_post, g_pre2)


def _conv_taps(h, cw, cb, p2, p1):
    s1 = _shift_down(h, p1)
    s2 = _shift_down(s1, p2)
    return cb + _row(cw, 0) * s2 + _row(cw, 1) * s1 + _row(cw, 2) * h


def _up_conv(xn2, w_up_t, conv_w, conv_b, tm, tc):
    t = xn2.shape[0]
    tm = _pick(t, tm)
    tn = _pick(DFF, 1408)
    nj = DFF // tn
    sub = tm // tc
    n = t // tc
    last = t // tm - 1

    def body(x_ref, wg_ref, wv_ref, cwg_ref, cwv_ref, cbg_ref, cbv_ref, hg_ref, hv_ref, act_ref, c1_ref, c2_ref, prev):
        j, i = pl.program_id(0), pl.program_id(1)

        @pl.when(i == 0)
        def _():
            prev[...] = jnp.zeros(prev.shape, F32)

        x = x_ref[...]
        hg = _raw_dot(x, wg_ref[...], "nt")
        hv = _raw_dot(x, wv_ref[...], "nt")
        hg_ref[...] = hg
        hv_ref[...] = hv
        pg, pv = prev[0:SUBLANES], prev[SUBLANES:2 * SUBLANES]
        cg = _conv_taps(hg, cwg_ref[...], cbg_ref[...], _row(pg, SUBLANES - 2), _row(pg, SUBLANES - 1))
        cv = _conv_taps(hv, cwv_ref[...], cbv_ref[...], _row(pv, SUBLANES - 2), _row(pv, SUBLANES - 1))
        act_ref[...] = (_silu(cg) * cv).astype(BF)
        prev[0:SUBLANES] = hg[tm - SUBLANES:tm]
        prev[SUBLANES:2 * SUBLANES] = hv[tm - SUBLANES:tm]

        def keep(h, off):
            cols = slice(off, off + tn)

            @pl.when(i == 0)
            def _():
                c1_ref[0, :, cols] = jnp.zeros((1, tn), F32)
                c2_ref[0, :, cols] = jnp.zeros((1, tn), F32)

            for s in range(sub):
                def put(s=s):
                    tail = h[(s + 1) * tc - SUBLANES:(s + 1) * tc]
                    c1_ref[i * sub + s + 1, :, cols] = _row(tail, SUBLANES - 1)
                    c2_ref[i * sub + s + 1, :, cols] = _row(tail, SUBLANES - 2)

                if s < sub - 1:
                    put()
                else:
                    pl.when(i < last)(put)

        for col in range(nj):
            @pl.when(j == col)
            def _(col=col):
                keep(hg, col * tn)
                keep(hv, DFF + col * tn)

    def cols(rows, off):
        return pl.BlockSpec((rows, tn), lambda j, i: (0, j + off))

    tile = pl.BlockSpec((tm, tn), lambda j, i: (i, j))
    before = pl.BlockSpec((n, 1, 2 * DFF), lambda j, i: (0, 0, 0))
    return pl.pallas_call(
        body, name="up_conv", grid=(nj, t // tm),
        in_specs=[pl.BlockSpec((tm, D), lambda j, i: (i, 0)), pl.BlockSpec((tn, D), lambda j, i: (j, 0)),
                  pl.BlockSpec((tn, D), lambda j, i: (j + nj, 0)), cols(3, 0), cols(3, nj), cols(1, 0), cols(1, nj)],
        out_specs=[tile, tile, tile, before, before],
        out_shape=[jax.ShapeDtypeStruct((t, DFF), F32), jax.ShapeDtypeStruct((t, DFF), F32),
                   jax.ShapeDtypeStruct((t, DFF), BF), jax.ShapeDtypeStruct((n, 1, 2 * DFF), F32),
                   jax.ShapeDtypeStruct((n, 1, 2 * DFF), F32)],
        scratch_shapes=[pltpu.VMEM((2 * SUBLANES, tn), F32)],
        compiler_params=_cparams(("arbitrary", "arbitrary")),
    )(xn2, w_up_t, w_up_t, conv_w, conv_w, conv_b, conv_b)


def _dxn_pre1_bwd(pieces, w_t, x, dx_res, g, tm, token):
    t = x.shape[0]
    tm = _pick(t, tm)
    offs = [sum(p.shape[1] for p in pieces[:i]) for i in range(len(pieces))]
    na = len(pieces)

    def body(*refs):
        w_ref, x_ref, r_ref, g_ref = refs[na:na + 4]
        dg_ref, dx_ref = refs[-2:]
        dxn = None
        for a_ref, off in zip(refs[:na], offs):
            part = _raw_dot(a_ref[...], w_ref[off:off + a_ref.shape[1], :], "nn")
            dxn = part if dxn is None else dxn + part
        _, vjp = jax.vjp(lambda gg, xx: _f_pre1_residual([gg], [xx], [])[0], g_ref[...], x_ref[...])
        dg, dx = vjp([dxn, r_ref[...]])
        _accumulate(pl.program_id(0), [dg_ref], [dg])
        dx_ref[...] = dx

    row = pl.BlockSpec((tm, D), lambda i: (i, 0))
    one = pl.BlockSpec((1, D), lambda i: (0, 0))
    return pl.pallas_call(
        body, name="pre1_bwd", grid=(t // tm,),
        in_specs=[pl.BlockSpec((tm, p.shape[1]), lambda i: (i, 0)) for p in pieces]
        + [pl.BlockSpec(w_t.shape, lambda i: (0, 0), pipeline_mode=pl.Buffered(1)), row, row, one,
           pl.BlockSpec(token.shape, lambda i: (0, 0))],
        out_specs=[one, row],
        out_shape=[jax.ShapeDtypeStruct((1, D), F32), jax.ShapeDtypeStruct((t, D), F32)],
        compiler_params=_cparams(("arbitrary",)),
    )(*pieces, w_t, x, dx_res, g, token)


def _down_loss(act, w_down, g_post, h1, tgt, tm):
    t, k = act.shape
    tm = _pick(t, tm)

    def body(a_ref, w_ref, g_ref, h_ref, t_ref, loss_ref, dg_ref, dh_ref, df_ref):
        ni = pl.program_id(0)
        ff = _raw_dot(a_ref[...], w_ref[...], "nn")
        target = t_ref[...]

        def lossf(g, h1, ff):
            e = h1 + _rms(ff, g) - target
            return 0.5 * jnp.sum(jnp.mean(e * e, axis=-1))

        l, (dg, dh, df) = jax.value_and_grad(lossf, argnums=(0, 1, 2))(g_ref[...], h_ref[...], ff)

        @pl.when(ni == 0)
        def _():
            loss_ref[...] = jnp.zeros(loss_ref.shape, F32)
            dg_ref[...] = jnp.zeros(dg_ref.shape, F32)

        loss_ref[...] += jnp.full(loss_ref.shape, l, F32)
        dg_ref[...] += dg
        dh_ref[...] = dh
        df_ref[...] = df.astype(df_ref.dtype)

    row = pl.BlockSpec((tm, D), lambda ni: (ni, 0))
    one = pl.BlockSpec((1, D), lambda ni: (0, 0))
    return pl.pallas_call(
        body, name="down_loss", grid=(t // tm,),
        in_specs=[pl.BlockSpec((tm, k), lambda ni: (ni, 0)),
                  pl.BlockSpec((k, D), lambda ni: (0, 0), pipeline_mode=pl.Buffered(1)), one, row, row],
        out_specs=[pl.BlockSpec((1, LANES), lambda ni: (0, 0)), one, row, row],
        out_shape=[jax.ShapeDtypeStruct((1, LANES), F32), jax.ShapeDtypeStruct((1, D), F32),
                   jax.ShapeDtypeStruct((t, D), F32), jax.ShapeDtypeStruct((t, D), BF)],
        compiler_params=_cparams(("arbitrary",)),
    )(act, w_down, g_post, h1, tgt)


_ANY = pl.BlockSpec(memory_space=pl.ANY)


def _all_gather(name, blks):
    na = len(blks)
    ns = 8

    def body(*refs):
        x_refs, out_refs = refs[:na], refs[na:2 * na]
        send_sems, recv_sems, local_sems = refs[2 * na:]
        x, y, cc = lax.axis_index("x"), lax.axis_index("y"), lax.axis_index("c")
        sibling, xn, yn = (x, y, 1 - cc), (1 - x, y, cc), (x, 1 - y, cc)

        def num(px, py, pc):
            return 4 * px + 2 * py + pc

        def copy(a, k, to, src, dst):
            return pltpu.make_async_remote_copy(src_ref=src, dst_ref=dst, send_sem=send_sems.at[ns * a + k],
                                                recv_sem=recv_sems.at[ns * a + k], device_id=to, device_id_type=MESH)

        def halves(a, blk):
            h = blks[a].shape[0] // 2
            return out_refs[a].at[blk, pl.ds(0, h)], out_refs[a].at[blk, pl.ds(h, h)]

        mine, sends = [], []
        for a in range(na):
            o = out_refs[a]
            m = pltpu.make_async_copy(x_refs[a], o.at[num(x, y, cc)], local_sems.at[a])
            m.start()
            mine.append(m)
            own = o.at[num(x, y, cc)]
            sends.append([copy(a, 0, sibling, x_refs[a], own), copy(a, 1, xn, x_refs[a], own),
                          copy(a, 2, yn, x_refs[a], own)])
            for cp in sends[a]:
                cp.start()
        for a in range(na):
            o = out_refs[a]
            bx, by, bd = num(1 - x, y, cc), num(x, 1 - y, cc), num(1 - x, 1 - y, cc)
            copy(a, 1, xn, o.at[bx], o.at[bx]).wait_recv()
            more = [copy(a, 3, yn, halves(a, bx)[0], halves(a, bx)[0]), copy(a, 5, sibling, o.at[bx], o.at[bx])]
            for cp in more:
                cp.start()
            sends[a] += more
        for a in range(na):
            o = out_refs[a]
            bx, by, bd = num(1 - x, y, cc), num(x, 1 - y, cc), num(1 - x, 1 - y, cc)
            copy(a, 2, yn, o.at[by], o.at[by]).wait_recv()
            more = [copy(a, 4, xn, halves(a, by)[1], halves(a, by)[1]), copy(a, 6, sibling, o.at[by], o.at[by])]
            for cp in more:
                cp.start()
            sends[a] += more
        for a in range(na):
            o = out_refs[a]
            bd = num(1 - x, 1 - y, cc)
            copy(a, 3, yn, halves(a, bd)[0], halves(a, bd)[0]).wait_recv()
            copy(a, 4, xn, halves(a, bd)[1], halves(a, bd)[1]).wait_recv()
            fw = copy(a, 7, sibling, o.at[bd], o.at[bd])
            fw.start()
            sends[a].append(fw)
        for a in range(na):
            o = out_refs[a]
            for k, blk in ((0, num(x, y, 1 - cc)), (5, num(1 - x, y, 1 - cc)), (6, num(x, 1 - y, 1 - cc)),
                           (7, num(1 - x, 1 - y, 1 - cc))):
                copy(a, k, sibling, o.at[blk], o.at[blk]).wait_recv()
            for cp in sends[a]:
                cp.wait_send()
        for m in mine:
            m.wait()

    res = pl.pallas_call(
        body, name=name, in_specs=[_ANY] * na, out_specs=[_ANY] * na,
        out_shape=[jax.ShapeDtypeStruct((N_DEV,) + b.shape, b.dtype) for b in blks],
        scratch_shapes=[pltpu.SemaphoreType.DMA((ns * na,)), pltpu.SemaphoreType.DMA((ns * na,)),
                        pltpu.SemaphoreType.DMA((na,))],
    )(*blks)
    return list(res)


def _all_gather_small(name, blk):
    def body(x_ref, out_ref, ssem, rsem, lsem):
        x, y, c = lax.axis_index("x"), lax.axis_index("y"), lax.axis_index("c")
        me = 4 * x + 2 * y + c
        mine = pltpu.make_async_copy(x_ref, out_ref.at[me], lsem)
        mine.start()
        cps = []
        for j in range(1, N_DEV):
            px = 1 - x if j & 4 else x
            py = 1 - y if j & 2 else y
            pc = 1 - c if j & 1 else c
            cps.append(pltpu.make_async_remote_copy(src_ref=x_ref, dst_ref=out_ref.at[me], send_sem=ssem.at[j - 1],
                                                    recv_sem=rsem.at[j - 1], device_id=(px, py, pc),
                                                    device_id_type=MESH))
        for cp in cps:
            cp.start()
        for cp in cps:
            cp.wait()
        mine.wait()

    return pl.pallas_call(
        body, name=name, in_specs=[_ANY], out_specs=_ANY,
        out_shape=jax.ShapeDtypeStruct((N_DEV,) + blk.shape, blk.dtype),
        scratch_shapes=[pltpu.SemaphoreType.DMA((N_DEV - 1,)), pltpu.SemaphoreType.DMA((N_DEV - 1,)),
                        pltpu.SemaphoreType.DMA],
    )(blk)


def _reduce_pair(g8s):
    na = len(g8s)

    def body(*refs):
        g_refs, recv_refs = refs[:na], refs[na:2 * na]
        ssem, rsem = refs[2 * na:]
        x, y, cc = lax.axis_index("x"), lax.axis_index("y"), lax.axis_index("c")
        chips = [(x, y), (1 - x, y), (x, 1 - y), (1 - x, 1 - y)]
        sib = (x, y, 1 - cc)
        for a in range(na):
            for k, (cx, cy) in enumerate(chips):
                pltpu.make_async_remote_copy(
                    src_ref=g_refs[a].at[4 * cx + 2 * cy + 1 - cc], dst_ref=recv_refs[a].at[k],
                    send_sem=ssem.at[a], recv_sem=rsem.at[a], device_id=sib, device_id_type=MESH).start()
        for a in range(na):
            pltpu.make_async_remote_copy(src_ref=recv_refs[a], dst_ref=recv_refs[a], send_sem=ssem.at[a],
                                         recv_sem=rsem.at[a], device_id=sib, device_id_type=MESH).wait()

    res = pl.pallas_call(
        body, name="reduce_pair", in_specs=[_ANY] * na, out_specs=[_ANY] * na,
        out_shape=[jax.ShapeDtypeStruct((4,) + g.shape[1:], g.dtype) for g in g8s],
        scratch_shapes=[pltpu.SemaphoreType.DMA((na,)), pltpu.SemaphoreType.DMA((na,))],
    )(*g8s)
    return list(res)


_HBM = pl.BlockSpec(memory_space=pltpu.HBM)
_SEM = pl.BlockSpec(memory_space=pltpu.SEMAPHORE)
_EFFECT = pltpu.SideEffectType.DATAFLOW_SIDE_EFFECTING


def _chip_swap_copies(s_refs, land_refs, ssem, rsem):
    x, y, c = lax.axis_index("x"), lax.axis_index("y"), lax.axis_index("c")
    targets = [(1 - x, y, c), (x, 1 - y, c), (1 - x, 1 - y, c)]
    return [pltpu.make_async_remote_copy(src_ref=s.at[k], dst_ref=d.at[k], send_sem=ssem.at[3 * a + k],
                                         recv_sem=rsem.at[3 * a + k], device_id=targets[k], device_id_type=MESH)
            for a, (s, d) in enumerate(zip(s_refs, land_refs)) for k in range(3)]


def _chip_swap_start(sends):
    na = len(sends)

    def body(*refs):
        cps = _chip_swap_copies(refs[:na], refs[na:2 * na], refs[2 * na], refs[2 * na + 1])
        for cp in cps:
            cp.start()
        token = refs[-1]
        token[...] = jnp.zeros(token.shape, token.dtype)

    bufs = [pltpu.HBM(s.shape, s.dtype) for s in sends]
    res = pl.pallas_call(
        body, name="chip_swap_start",
        out_shape=[pltpu.SemaphoreType.DMA((3 * na,)), pltpu.SemaphoreType.DMA((3 * na,))] + bufs + bufs
        + [jax.ShapeDtypeStruct((8, LANES), F32)],
        in_specs=[_HBM] * (2 * na), out_specs=[_SEM, _SEM] + [_HBM] * (2 * na) + [pl.BlockSpec(memory_space=pltpu.VMEM)],
        input_output_aliases={i: 2 + i for i in range(2 * na)},
        compiler_params=pltpu.CompilerParams(has_side_effects=_EFFECT),
    )(*[pltpu.with_memory_space_constraint(s, pltpu.HBM) for s in sends],
      *[pltpu.with_memory_space_constraint(lax.empty(s.shape, s.dtype), pltpu.HBM) for s in sends])
    return res[0], res[1], list(res[2:2 + na]), list(res[2 + na:2 + 2 * na]), res[-1]


def _chip_swap_wait(ssem, rsem, srcs, lands, after):
    na = len(srcs)

    def body(*refs):
        cps = _chip_swap_copies(refs[:na], refs[na:2 * na], refs[2 * na], refs[2 * na + 1])
        for cp in cps:
            cp.wait_send()
            cp.wait_recv()

    bufs = [pltpu.HBM(s.shape, s.dtype) for s in srcs]
    res = pl.pallas_call(
        body, name="chip_swap_wait", out_shape=bufs + bufs,
        in_specs=[_HBM] * (2 * na) + [_SEM, _SEM, _ANY], out_specs=[_HBM] * (2 * na),
        input_output_aliases={i: i for i in range(2 * na)},
        compiler_params=pltpu.CompilerParams(has_side_effects=_EFFECT),
    )(*srcs, *lands, ssem, rsem, after)
    return list(res[na:])


def _pick_rows(r, c, budget=TILE_BYTES):
    if r * c * 4 <= budget or r % 16:
        return r
    best = 16
    for tr in range(16, r, 16):
        if r % tr == 0 and tr * c * 4 <= budget:
            best = tr
    return best


def _pair_sum(name, idx4, g8, recv4):
    _, r, c = g8.shape
    tr = _pick_rows(r, c, 2 * TILE_BYTES)

    def body(idx_ref, a_ref, b_ref, o0_ref, o3_ref):
        k = pl.program_id(1)
        s = a_ref[...].astype(F32) + b_ref[...].astype(F32)

        @pl.when(k == 0)
        def _():
            o0_ref[...] = s

        @pl.when(k > 0)
        def _():
            o3_ref[...] = s.astype(BF)

    spec = pltpu.PrefetchScalarGridSpec(
        num_scalar_prefetch=1, grid=(r // tr, 4),
        in_specs=[pl.BlockSpec((None, tr, c), lambda i, k, idx: (idx[k], i, 0)),
                  pl.BlockSpec((None, tr, c), lambda i, k, idx: (k, i, 0))],
        out_specs=[pl.BlockSpec((tr, c), lambda i, k, idx: (i, 0)),
                   pl.BlockSpec((None, tr, c), lambda i, k, idx: (jnp.maximum(k - 1, 0), i, 0))])
    return pl.pallas_call(
        body, name=name, grid_spec=spec,
        out_shape=[jax.ShapeDtypeStruct((r, c), F32), jax.ShapeDtypeStruct((3, r, c), BF)],
        compiler_params=_cparams(("arbitrary", "arbitrary")),
    )(idx4, g8, recv4)


def _adamw(w, g, m, v):
    m = ADAM_B1 * m + (1.0 - ADAM_B1) * g
    v = ADAM_B2 * v + (1.0 - ADAM_B2) * jnp.square(g)
    m_hat = m / (1.0 - ADAM_B1 ** ADAM_STEP)
    v_hat = v / (1.0 - ADAM_B2 ** ADAM_STEP)
    delta = -ADAM_LR * (m_hat / (jnp.sqrt(v_hat) + ADAM_EPS) + ADAM_WD * w)
    return delta, m, v


def _adam_sharded(name, idx1, own, recv, w, m, v, after=None):
    r, c = w.shape
    tr = _pick_rows(r, c, 2 * TILE_BYTES)
    nj = recv.shape[0]
    extra = [] if after is None else [after]

    def body(idx_ref, p_ref, r_ref, w_ref, m_ref, v_ref, *rest):
        g_out, d_out, m_out, v_out = rest[-4:]
        g = p_ref[...].astype(F32)
        for k in range(nj):
            g = g + r_ref[k].astype(F32)
        d, mn, vn = _adamw(w_ref[...], g, m_ref[...], v_ref[...])
        g_out[...] = g
        d_out[...] = d
        m_out[...] = mn
        v_out[...] = vn

    row = pl.BlockSpec((tr, c), lambda i, idx: (i, 0))
    spec = pltpu.PrefetchScalarGridSpec(
        num_scalar_prefetch=1, grid=(r // tr,),
        in_specs=[pl.BlockSpec((None, tr, c), lambda i, idx: (idx[0], i, 0)),
                  pl.BlockSpec((nj, tr, c), lambda i, idx: (0, i, 0)), row, row, row]
        + [pl.BlockSpec(e.shape, lambda i, idx: (0, 0)) for e in extra],
        out_specs=[row] * 4)
    return pl.pallas_call(
        body, name=name, grid_spec=spec, out_shape=[jax.ShapeDtypeStruct((r, c), F32)] * 4,
        compiler_params=_cparams(("arbitrary",)),
    )(idx1, own, recv, w, m, v, *extra)


def _repl_rows():
    rows, r = {}, 0
    for name, cols in REPL:
        rows[name] = r
        r += REPL_ROWS.get(name, 1) * ((cols + D - 1) // D)
    return rows


LOSS_ROW = 24


def _pack_replicated(grads, loss_acc, after):
    rows = _repl_rows()
    names = [n for n, _ in REPL]

    def body(*refs):
        o_ref = refs[-1]
        o_ref[...] = jnp.zeros(o_ref.shape, F32)
        o_ref[LOSS_ROW:LOSS_ROW + 1, 0:LANES] = refs[len(names)][...]
        for name, ref in zip(names, refs[:len(names)]):
            r0 = rows[name]
            nr, nc = ref.shape
            if nc <= D:
                o_ref[r0:r0 + nr, 0:nc] = ref[...]
            else:
                for j in range((nc + D - 1) // D):
                    lo, hi = j * D, min(nc, (j + 1) * D)
                    o_ref[r0 + j:r0 + j + 1, 0:hi - lo] = ref[:, lo:hi]

    return pl.pallas_call(body, name="pack_replicated", out_shape=jax.ShapeDtypeStruct((REPL_TOTAL, D), F32),
                          in_specs=[pl.BlockSpec(memory_space=pltpu.VMEM)] * (len(names) + 1) + [_ANY] * len(after),
                          compiler_params=_cparams())(*[grads[n] for n in names], loss_acc, *after)


def _adam_replicated(g8, ws, ms, vs):
    rows = _repl_rows()
    names = [n for n, _ in REPL]
    np_ = len(names)

    def body(*refs):
        g_ref = refs[0]
        w_refs, m_refs, v_refs = refs[1:1 + np_], refs[1 + np_:1 + 2 * np_], refs[1 + 2 * np_:1 + 3 * np_]
        outs = refs[1 + 3 * np_:1 + 7 * np_]
        scr = refs[-1]
        g = g_ref[0]
        for k in range(1, N_DEV):
            g = g + g_ref[k]
        scr[...] = g
        refs[1 + 7 * np_][...] = scr[LOSS_ROW:LOSS_ROW + 1, 0:LANES]
        for i, name in enumerate(names):
            r0 = rows[name]
            nr, nc = w_refs[i].shape
            if nc <= D:
                gi = scr[r0:r0 + nr, 0:nc]
            else:
                parts = []
                for j in range((nc + D - 1) // D):
                    lo, hi = j * D, min(nc, (j + 1) * D)
                    parts.append(scr[r0 + j:r0 + j + 1, 0:hi - lo])
                gi = jnp.concatenate(parts, axis=1)
            d, mn, vn = _adamw(w_refs[i][...], gi, m_refs[i][...], v_refs[i][...])
            outs[i][...] = gi
            outs[np_ + i][...] = d
            outs[2 * np_ + i][...] = mn
            outs[3 * np_ + i][...] = vn

    shp = [jax.ShapeDtypeStruct(w.shape, F32) for w in ws]
    res = pl.pallas_call(body, name="adam_replicated", out_shape=shp * 4 + [jax.ShapeDtypeStruct((1, LANES), F32)],
                         scratch_shapes=[pltpu.VMEM((REPL_TOTAL, D), F32)], compiler_params=_cparams(),
                         )(g8, *ws, *ms, *vs)
    return [dict(zip(names, res[k * np_:(k + 1) * np_])) for k in range(4)], res[-1]


_WEIGHTS = ("attn_pre_norm", "w_in", "hgrn_lb", "hgrn_gnorm", "w_branch_a", "rwkv_mu", "rwkv_w0", "rwkv_w2",
            "rwkv_a0", "rwkv_a2", "rwkv_g2", "rwkv_k_k", "rwkv_k_a", "rwkv_r_k", "rwkv_ln_w", "rwkv_ln_b",
            "w_branch_b", "w_out", "attn_post_norm", "ffn_pre_norm", "w_up", "conv_w", "conv_b", "w_down",
            "ffn_post_norm")
_BIG = ("w_in", "w_up", "w_down", "w_branch_a", "w_branch_b", "w_out")


def _stages():
    one = [D]
    hw = HG_K * HG_PER_STEP
    rw = LANES * RW_PAIRS_PER_STEP
    return dict(
        mixers=_Stage("mixers", _f_mixers, 1, 2 * RW_CHUNK, [False] * 13, [[D] * 7 + [LANES, LANES]], [0],
                      [(hw, HG_K), (1, RW_COLS), (rw, LANES)], [one, one], [BF, BF],
                      kept_shapes=[(2 * RW_KEPT * RW_PAIRS_PER_STEP * 2 * RW_CHUNK, LANES)], f_kept=_f_mixers_kept),
        conv=_Stage("conv", _f_conv, 1, 256, [False, False], [[DFF], [DFF]], [0, 0], [(1, 2 * DFF), (1, 2 * DFF)],
                    [[DFF]], [BF]),
    )


def _cols_to_blocks(w, per):
    return w.reshape(w.shape[0], N_DEV, per).transpose(1, 0, 2)


def _blocks_to_cols(g):
    return g.transpose(1, 0, 2).reshape(g.shape[1], N_DEV * g.shape[2])


def kernel(x, attn_pre_norm, w_in, hgrn_lb, hgrn_gnorm, w_branch_a, rwkv_mu, rwkv_w0, rwkv_w2, rwkv_a0, rwkv_a2, rwkv_g2, rwkv_k_k, rwkv_k_a, rwkv_r_k, rwkv_ln_w, rwkv_ln_b, w_branch_b, w_out, attn_post_norm, ffn_pre_norm, w_up, conv_w, conv_b, w_down, ffn_post_norm, loss_target, m_attn_pre_norm, m_w_in, m_hgrn_lb, m_hgrn_gnorm, m_w_branch_a, m_rwkv_mu, m_rwkv_w0, m_rwkv_w2, m_rwkv_a0, m_rwkv_a2, m_rwkv_g2, m_rwkv_k_k, m_rwkv_k_a, m_rwkv_r_k, m_rwkv_ln_w, m_rwkv_ln_b, m_w_branch_b, m_w_out, m_attn_post_norm, m_ffn_pre_norm, m_w_up, m_conv_w, m_conv_b, m_w_down, m_ffn_post_norm, v_attn_pre_norm, v_w_in, v_hgrn_lb, v_hgrn_gnorm, v_w_branch_a, v_rwkv_mu, v_rwkv_w0, v_rwkv_w2, v_rwkv_a0, v_rwkv_a2, v_rwkv_g2, v_rwkv_k_k, v_rwkv_k_a, v_rwkv_r_k, v_rwkv_ln_w, v_rwkv_ln_b, v_w_branch_b, v_w_out, v_attn_post_norm, v_ffn_pre_norm, v_w_up, v_conv_w, v_conv_b, v_w_down, v_ffn_post_norm):
    w = dict(attn_pre_norm=attn_pre_norm, w_in=w_in, hgrn_lb=hgrn_lb, hgrn_gnorm=hgrn_gnorm, w_branch_a=w_branch_a, rwkv_mu=rwkv_mu, rwkv_w0=rwkv_w0, rwkv_w2=rwkv_w2, rwkv_a0=rwkv_a0, rwkv_a2=rwkv_a2, rwkv_g2=rwkv_g2, rwkv_k_k=rwkv_k_k, rwkv_k_a=rwkv_k_a, rwkv_r_k=rwkv_r_k, rwkv_ln_w=rwkv_ln_w, rwkv_ln_b=rwkv_ln_b, w_branch_b=w_branch_b, w_out=w_out, attn_post_norm=attn_post_norm, ffn_pre_norm=ffn_pre_norm, w_up=w_up, conv_w=conv_w, conv_b=conv_b, w_down=w_down, ffn_post_norm=ffn_post_norm)
    mo = dict(attn_pre_norm=m_attn_pre_norm, w_in=m_w_in, hgrn_lb=m_hgrn_lb, hgrn_gnorm=m_hgrn_gnorm, w_branch_a=m_w_branch_a, rwkv_mu=m_rwkv_mu, rwkv_w0=m_rwkv_w0, rwkv_w2=m_rwkv_w2, rwkv_a0=m_rwkv_a0, rwkv_a2=m_rwkv_a2, rwkv_g2=m_rwkv_g2, rwkv_k_k=m_rwkv_k_k, rwkv_k_a=m_rwkv_k_a, rwkv_r_k=m_rwkv_r_k, rwkv_ln_w=m_rwkv_ln_w, rwkv_ln_b=m_rwkv_ln_b, w_branch_b=m_w_branch_b, w_out=m_w_out, attn_post_norm=m_attn_post_norm, ffn_pre_norm=m_ffn_pre_norm, w_up=m_w_up, conv_w=m_conv_w, conv_b=m_conv_b, w_down=m_w_down, ffn_post_norm=m_ffn_post_norm)
    vo = dict(attn_pre_norm=v_attn_pre_norm, w_in=v_w_in, hgrn_lb=v_hgrn_lb, hgrn_gnorm=v_hgrn_gnorm, w_branch_a=v_w_branch_a, rwkv_mu=v_rwkv_mu, rwkv_w0=v_rwkv_w0, rwkv_w2=v_rwkv_w2, rwkv_a0=v_rwkv_a0, rwkv_a2=v_rwkv_a2, rwkv_g2=v_rwkv_g2, rwkv_k_k=v_rwkv_k_k, rwkv_k_a=v_rwkv_k_a, rwkv_r_k=v_rwkv_r_k, rwkv_ln_w=v_rwkv_ln_w, rwkv_ln_b=v_rwkv_ln_b, w_branch_b=v_w_branch_b, w_out=v_w_out, attn_post_norm=v_attn_post_norm, ffn_pre_norm=v_ffn_pre_norm, w_up=v_w_up, conv_w=v_conv_w, conv_b=v_conv_b, w_down=v_w_down, ffn_post_norm=v_ffn_post_norm)

    t = x.shape[1]
    x2 = x.reshape(t, D)
    tgt = loss_target.reshape(t, D)
    st = _stages()

    me = 4 * lax.axis_index("x") + 2 * lax.axis_index("y") + lax.axis_index("c")
    small = jnp.concatenate([rwkv_w2[0], rwkv_a2[0], rwkv_g2[0]], axis=0).astype(BF)
    g_in, g_small = _all_gather("gather_weights", [w_in[0].T.astype(BF), small])
    fw_in_t = g_in.reshape(IN_COLS, D)
    z64 = jnp.zeros((64, D), BF)
    w2p = jnp.concatenate([_blocks_to_cols(g_small[:, 0:64]), z64], axis=0)
    a2p = jnp.concatenate([z64, _blocks_to_cols(g_small[:, 64:128])], axis=0)
    g2f = _blocks_to_cols(g_small[:, 128:256])
    conv_bits = jnp.pad(lax.bitcast_convert_type(conv_w[0], BF).reshape(3, 2 * 704), ((0, 29), (0, 0)))
    late = [w_up[0].T.astype(BF)] + [w[k][0].astype(BF) for k in _BIG[2:]] + [conv_bits]
    late_gather = _Exchange("gather2", late)
    r_k = rwkv_r_k.reshape(1, D)

    xn, z = _norm_in_proj(x2, attn_pre_norm, fw_in_t, 512, 4736)
    mix_par = [hgrn_lb, hgrn_gnorm, rwkv_mu, rwkv_w0, w2p, rwkv_a0, a2p, g2f, rwkv_k_k, rwkv_k_a,
               rwkv_ln_w, rwkv_ln_b, r_k]
    mix_in = [z]
    (o_a, o_b), mix_saved = _stage_fwd(st["mixers"], t, mix_par, mix_in, hook=late_gather)
    gl = [lax.dynamic_update_slice(g, own[None], (me, 0, 0)) for g, own in zip(late_gather.results, late)]
    fw_up_t = gl[0].reshape(2 * DFF, D)
    fw_down = gl[1].reshape(DFF, D)
    fw_a, fw_b, fw_out = (g.reshape(D, D) for g in gl[2:5])
    conv_full = _blocks_to_cols(lax.bitcast_convert_type(gl[5][:, :3].reshape(N_DEV, 3, 704, 2), F32))
    y_a, y_b, merged, mix, h1, xn2 = _merge_out_post(z, o_a, o_b, fw_a, fw_b, fw_out, x2, attn_post_norm,
                                                     ffn_pre_norm, 512)
    conv_par = [conv_full, conv_b]
    hu_g, hu_v, act, before1, before2 = _up_conv(xn2, fw_up_t, conv_full, conv_b, 512, min(st["conv"].tm, t))
    conv_saved = [before1[None], before2[None]]

    loss_acc, d_ffn_post, dh1, dff = _down_loss(act, fw_down, ffn_post_norm, h1, tgt, 512)
    dw_down = _mm("dw_down", act, dff, "tn", BF, tm=1408, tn=512)
    (dcw, dcb), dhu = _stage_bwd(st["conv"], t, conv_par, [hu_g, hu_v], conv_saved, [], [BF, BF],
                                 dout_dot=(dff, fw_down))
    dw_up_t = _mm_cols_tn("dw_up", dhu, xn2, BF, 1408)
    d_post, d_pre2, dx_a, dmix = _dxn2_post1_bwd(dhu, fw_up_t, x2, mix, dh1, attn_post_norm, ffn_pre_norm, 512)
    dga, dgb, dy_a, dy_b, do_a, do_b = _dmerged_merge_bwd(dmix, fw_out, fw_a, fw_b, z, y_a, y_b, 512)
    dw_a, dw_b, dw_out = _mm_multi("dw_branches", [(o_a, dy_a), (o_b, dy_b), (merged, dmix)], "tn", BF)
    early = [dw_up_t.reshape(N_DEV, 704, D), dw_down.reshape(N_DEV, 352, D), dw_a.reshape(N_DEV, 128, D),
             dw_b.reshape(N_DEV, 128, D), dw_out.reshape(N_DEV, 128, D), _cols_to_blocks(dcw.astype(BF), 704)]
    early_scatter = _Exchange("scatter", early)
    mix_dp, dz_hr = _stage_bwd(st["mixers"], t, mix_par, mix_in, mix_saved, [[do_a], [do_b]], [BF],
                               hook=early_scatter)
    d_lb, d_gn, d_mu, d_w0, d_w2p, d_a0, d_a2p, d_g2, d_kk, d_ka, d_lnw, d_lnb, d_rk = mix_dp
    dz = dz_hr + [dga, dgb]
    dw_in_t = _mm_cols_tn("dw_in", dz, xn, BF, 256)

    ax, ay, ac = lax.axis_index("x"), lax.axis_index("y"), lax.axis_index("c")
    idx4 = jnp.stack([4 * cx + 2 * cy + ac for cx, cy in ((ax, ay), (1 - ax, ay), (ax, 1 - ay), (1 - ax, 1 - ay))])
    idx4 = idx4.astype(jnp.int32)
    idx_me, idx_0 = idx4[0:1], jnp.zeros((1,), jnp.int32)
    d_small = jnp.concatenate([d_w2p[:64], d_a2p[64:], d_g2], axis=0).astype(BF)
    g8s = [dw_in_t.reshape(N_DEV, 1184, D), _cols_to_blocks(d_small, LANES)]
    recv4s = _reduce_pair(g8s)
    sums = [_pair_sum("pair_sum_" + n, idx4, g, r) for n, g, r in zip(("w_in", "small"), g8s, recv4s)]
    swap_ssem, swap_rsem, swap_srcs, swap_lands, token = _chip_swap_start([s[1] for s in sums])
    d_pre1, dx = _dxn_pre1_bwd(dz, fw_in_t, x2, dx_a, attn_pre_norm, 256, token)
    grad_x = dx.reshape(x.shape)

    sh_out = [dict() for _ in range(4)]
    done = []
    for n, own, recv in zip(_BIG[1:] + ("conv_w",), early, early_scatter.results):
        tr = (lambda a: a.T) if n == "w_up" else (lambda a: a)
        res = _adam_sharded("adam_" + n, idx_me, own, recv, *[tr(src[n][0]) for src in (w, mo, vo)], after=token)
        done.append(res[0])
        for kind in range(4):
            sh_out[kind][n] = tr(res[kind])[None]

    rg = dict(attn_pre_norm=d_pre1, hgrn_lb=d_lb, hgrn_gnorm=d_gn, rwkv_mu=d_mu, rwkv_w0=d_w0, rwkv_a0=d_a0,
              rwkv_k_k=d_kk, rwkv_k_a=d_ka, rwkv_r_k=d_rk, rwkv_ln_w=d_lnw, rwkv_ln_b=d_lnb, attn_post_norm=d_post,
              ffn_pre_norm=d_pre2, conv_b=dcb, ffn_post_norm=d_ffn_post)
    g8 = _all_gather_small("gather_small_grads", _pack_replicated(rg, loss_acc, done))
    rnames = [n for n, _ in REPL]
    flat = lambda src: [src[n].reshape(1, D) if n == "rwkv_r_k" else src[n] for n in rnames]
    rp_out, loss_row = _adam_replicated(g8, flat(w), flat(mo), flat(vo))
    loss = loss_row[0, 0]
    recv3s = _chip_swap_wait(swap_ssem, swap_rsem, swap_srcs, swap_lands, rp_out[0]["attn_pre_norm"])
    for kind in range(4):
        rp_out[kind]["rwkv_r_k"] = rp_out[kind]["rwkv_r_k"].reshape(rwkv_r_k.shape)

    def small_of(src):
        return jnp.concatenate([src["rwkv_w2"][0], src["rwkv_a2"][0], src["rwkv_g2"][0]], axis=0)

    res = _adam_sharded("adam_w_in", idx_0, sums[0][0][None], recv3s[0], *[src["w_in"][0].T for src in (w, mo, vo)])
    res_s = _adam_sharded("adam_small", idx_0, sums[1][0][None], recv3s[1], *[small_of(src) for src in (w, mo, vo)])
    for kind in range(4):
        sh_out[kind]["w_in"] = res[kind].T[None]
        sh_out[kind]["rwkv_w2"] = res_s[kind][0:64][None]
        sh_out[kind]["rwkv_a2"] = res_s[kind][64:128][None]
        sh_out[kind]["rwkv_g2"] = res_s[kind][128:256][None]

    outs = [loss, grad_x]
    for kind in range(4):
        for name in _WEIGHTS:
            outs.append(sh_out[kind][name] if name in sh_out[kind] else rp_out[kind][name])
    return tuple(outs)
```

```python
import functools

import jax
import jax.numpy as jnp
from jax import lax
from jax.experimental import pallas as pl
from jax.experimental.pallas import tpu as pltpu

F32 = jnp.float32
BF = jnp.bfloat16
MESH = pl.DeviceIdType.MESH

D = 1024
HG_HEADS = 8
HG_K = 128
HG_CHUNK = 32
HG_SCALE = HG_K ** -0.5
HG_PER_STEP = 8
RW_HEADS = 16
RW_N = 64
RW_CHUNK = 64
RW_PAIRS_PER_STEP = 8
DFF = 2816
IN_COLS = 9472
RW_COLS = 3328
EPS = 1e-6
GN_EPS = 1e-5 * RW_N
ADAM_LR = 0.001
ADAM_B1 = 0.9
ADAM_B2 = 0.999
ADAM_EPS = 1e-08
ADAM_WD = 0.01
ADAM_STEP = 10
N_DEV = 8
LANES = 128
SUBLANES = 8
VMEM_LIMIT = 56 * 1024 * 1024
TILE_BYTES = 1280 * 1024

REPL = (("attn_pre_norm", 1024), ("hgrn_lb", 1024), ("hgrn_gnorm", 1024), ("rwkv_mu", 3328), ("rwkv_w0", 1024),
        ("rwkv_a0", 1024), ("rwkv_k_k", 1024), ("rwkv_k_a", 1024), ("rwkv_r_k", 1024), ("rwkv_ln_w", 1024),
        ("rwkv_ln_b", 1024), ("attn_post_norm", 1024), ("ffn_pre_norm", 1024), ("conv_b", 5632), ("ffn_post_norm", 1024))
REPL_ROWS = {"hgrn_lb": 2}
REPL_TOTAL = 32


def _cparams(sem=None, **kw):
    return pltpu.CompilerParams(dimension_semantics=sem, vmem_limit_bytes=VMEM_LIMIT, **kw)


_DN = {"nn": ((1,), (0,)), "nt": ((1,), (1,)), "tn": ((0,), (0,))}


def _raw_dot(a, b, mode):
    return lax.dot_general(a.astype(BF), b.astype(BF), (_DN[mode], ((), ())), preferred_element_type=F32)


@functools.partial(jax.custom_vjp, nondiff_argnums=(2,))
def _dot(a, b, mode):
    return _raw_dot(a, b, mode)


def _dot_fwd(a, b, mode):
    return _raw_dot(a, b, mode), (a, b)


def _dot_bwd(mode, res, g):
    a, b = res
    if mode == "nn":
        return _dot(g, b, "nt"), _dot(a, g, "tn")
    if mode == "nt":
        return _dot(g, b, "nn"), _dot(g, a, "tn")
    return _dot(b, g, "nt"), _dot(a, g, "nn")


_dot.defvjp(_dot_fwd, _dot_bwd)


def _bf_pieces(x, n):
    out, r = [], x
    for i in range(n):
        p = r.astype(BF)
        out.append(p)
        if i + 1 < n:
            r = r - p.astype(F32)
    return out


def _raw_split_dot(x, e, mode, n, x_left):
    eb = e.astype(BF)
    acc = None
    for p in _bf_pieces(x, n):
        ops = (p, eb) if x_left else (eb, p)
        t = lax.dot_general(*ops, (_DN[mode], ((), ())), preferred_element_type=F32)
        acc = t if acc is None else acc + t
    return acc


def _raw_headsum(x):
    t = x.shape[0]
    i = lax.broadcasted_iota(jnp.int32, (LANES, LANES), 0)
    j = lax.broadcasted_iota(jnp.int32, (LANES, LANES), 1)
    same = jnp.where((i >= RW_N) == (j >= RW_N), 1.0, 0.0).astype(F32)
    groups = x.shape[1] // LANES
    rows = jnp.concatenate([x[:, q * LANES:(q + 1) * LANES] for q in range(groups)], axis=0)
    s = _raw_split_dot(rows, same, "nn", 2, True)
    return jnp.concatenate([s[q * t:(q + 1) * t] for q in range(groups)], axis=1)


@jax.custom_vjp
def _headsum(x):
    return _raw_headsum(x)


def _headsum_fwd(x):
    return _raw_headsum(x), None


def _headsum_bwd(_, g):
    return (_raw_headsum(g),)


_headsum.defvjp(_headsum_fwd, _headsum_bwd)


@functools.partial(jax.custom_vjp, nondiff_argnums=(2,))
def _tdot(tri, x, n):
    return _raw_split_dot(x, tri, "nn", n, False)


def _tdot_fwd(tri, x, n):
    return _raw_split_dot(x, tri, "nn", n, False), tri


def _tdot_bwd(n, tri, g):
    return jnp.zeros_like(tri), _raw_split_dot(g, tri, "tn", n, False)


_tdot.defvjp(_tdot_fwd, _tdot_bwd)


def _row(x, i):
    r = lax.broadcasted_iota(jnp.int32, x.shape, 0)
    return jnp.sum(jnp.where(r == i, x, 0.0), axis=0, keepdims=True)


def _shift_down(x, prev):
    t = x.shape[0]

    @jax.custom_vjp
    def sh(x, prev):
        r = lax.broadcasted_iota(jnp.int32, x.shape, 0)
        return jnp.where(r == 0, prev, pltpu.roll(x, 1, 0))

    def fwd(x, prev):
        return sh(x, prev), None

    def bwd(_, g):
        r = lax.broadcasted_iota(jnp.int32, g.shape, 0)
        dx = jnp.where(r == t - 1, 0.0, pltpu.roll(g, t - 1, 0))
        return dx, jnp.sum(jnp.where(r == 0, g, 0.0), axis=0, keepdims=True)

    sh.defvjp(fwd, bwd)
    return sh(x, prev)


def _sigmoid(x):
    return jax.nn.sigmoid(x)


def _silu(x):
    return x * jax.nn.sigmoid(x)


def _softplus(x):
    return jnp.maximum(x, 0.0) + jnp.log(1.0 + jnp.exp(-jnp.abs(x)))


def _rms(x, g):
    return (x * lax.rsqrt(jnp.mean(x * x, axis=-1, keepdims=True) + EPS)) * g


def _tril(c):
    r = lax.broadcasted_iota(jnp.int32, (c, c), 0)
    cc = lax.broadcasted_iota(jnp.int32, (c, c), 1)
    return cc <= r


def _f_pre1_residual(ps, xs, cs):
    return [_rms(xs[0], ps[0]), xs[0]], []


def _f_hgrn(ps, xs, cs):
    lbraw, gn = ps
    hq, hf, hi, hg = xs
    hd = range(HG_PER_STEP)
    st = [cs[0][p * HG_K:(p + 1) * HG_K] for p in hd]
    l0, l1 = _row(lbraw, 0), _row(lbraw, 1)
    m = jnp.maximum(l0, l1)
    e0, e1 = jnp.exp(l0 - m), jnp.exp(l1 - m)
    lb = e0 / (e0 + e1)
    q = _silu(hq) * HG_SCALE
    f = lb + (1.0 - lb) * _sigmoid(hf)
    kh = 1.0 - f
    gl = jnp.log(f)
    c = HG_CHUNK
    low = _tril(c)
    tri = jnp.where(low, 1.0, 0.0).astype(F32)
    outs = []
    for i in range(hq.shape[0] // c):
        rows = slice(i * c, (i + 1) * c)
        b = _tdot(tri, gl[rows], 3)
        bref = _row(b, c // 2 - 1)
        blast = _row(b, c - 1)
        qi = q[rows] * jnp.exp(b - bref)
        ki = kh[rows] * jnp.exp(bref - b)
        qd = q[rows] * jnp.exp(b)
        kd = kh[rows] * jnp.exp(blast - b)
        dec = jnp.exp(blast)
        sl = [slice(p * HG_K, (p + 1) * HG_K) for p in hd]
        sc = [jnp.where(low, _dot(qi[:, sl[p]], ki[:, sl[p]], "nt"), 0.0) for p in hd]
        o = [_dot(sc[p], hi[rows, sl[p]], "nn") + _dot(qd[:, sl[p]], st[p], "nt") for p in hd]
        u = [_dot(hi[rows, sl[p]], kd[:, sl[p]], "tn") for p in hd]
        st = [dec[:, sl[p]] * st[p] + u[p] for p in hd]
        outs.append(jnp.concatenate(o, axis=1) if len(o) > 1 else o[0])
    o = outs[0] if len(outs) == 1 else jnp.concatenate(outs, axis=0)
    on = []
    for p in hd:
        op = o[:, p * HG_K:(p + 1) * HG_K]
        on.append(op * lax.rsqrt(jnp.mean(op * op, axis=-1, keepdims=True) + EPS))
    o = jnp.concatenate(on, axis=1) if len(on) > 1 else on[0]
    o = o * gn
    return [o * _silu(hg)], [jnp.concatenate(st, axis=0) if len(st) > 1 else st[0]]


_RW_OFFS = (0, 1024, 2048, 3072, 3200, 3328)


def _f_rwpre(ps, xs, cs):
    mu, w0, w2p, a0, a2p, g2, k_k, k_a = ps
    (prev,) = cs
    t = xs[0].shape[0]
    zs = []
    for i, z in enumerate(xs):
        lo, hi = _RW_OFFS[i], _RW_OFFS[i + 1]
        zs.append(z + mu[:, lo:hi] * (_shift_down(z, prev[:, lo:hi]) - z))
    rr, kr, vr, wa, gz = zs
    w_log = -_softplus(-(w0 + _dot(jnp.tanh(wa), w2p, "nn"))) - 0.5
    lw = -jnp.exp(w_log)
    a = _sigmoid(a0 + _dot(wa, a2p, "nn"))
    g = _dot(_sigmoid(gz), g2, "nn")
    kkr = kr * k_k
    kk = kkr / jnp.maximum(jnp.sqrt(_headsum(kkr * kkr)), 1e-12)
    k2 = kr * (1.0 + (a - 1.0) * k_a)
    newprev = jnp.concatenate([_row(z, t - 1) for z in xs], axis=1)
    return [rr, lw, k2, vr, -kk, kk * a, g], [newprev]


def _raw_inverses(ls):
    n = ls[0].shape[0]
    r = lax.broadcasted_iota(jnp.int32, (n, n), 0)
    c = lax.broadcasted_iota(jnp.int32, (n, n), 1)
    eye = jnp.where(r == c, 1.0, 0.0).astype(F32)
    tinv = [eye + l for l in ls]
    pw = ls
    for _ in range(5):
        pw = [_raw_dot(p, p, "nn") for p in pw]
        tinv = [t + _raw_dot(t, p, "nn") for t, p in zip(tinv, pw)]
    return tinv


@jax.custom_vjp
def _unit_lower_inverses(ls):
    return _raw_inverses(ls)


def _inverses_fwd(ls):
    tinv = _raw_inverses(ls)
    return tinv, tinv


def _inverses_bwd(tinv, gs):
    return ([_raw_dot(_raw_dot(t, g, "tn"), t, "nt") for t, g in zip(tinv, gs)],)


_unit_lower_inverses.defvjp(_inverses_fwd, _inverses_bwd)


@jax.custom_vjp
def _known_inverses(ls, tinv):
    return tinv


def _known_fwd(ls, tinv):
    return tinv, tinv


def _known_bwd(tinv, gs):
    return [_raw_dot(_raw_dot(t, g, "tn"), t, "nt") for t, g in zip(tinv, gs)], [jnp.zeros_like(t) for t in tinv]


_known_inverses.defvjp(_known_fwd, _known_bwd)


@jax.custom_vjp
def _use_kept(computed, kept):
    return kept


def _use_kept_fwd(computed, kept):
    return kept, None


def _use_kept_bwd(_, g):
    return g, jax.tree.map(jnp.zeros_like, g)


_use_kept.defvjp(_use_kept_fwd, _use_kept_bwd)

RW_KEPT = 5


def _f_rwscan(ps, xs, cs, kept=None):
    state = cs[0]
    ys, keep = [], []
    n = 2 * RW_CHUNK
    per_chunk = RW_KEPT * RW_PAIRS_PER_STEP * n
    for i in range(xs[0].shape[0] // RW_CHUNK):
        known = None
        if kept is not None:
            known = [[kept[i * per_chunk + (q * RW_PAIRS_PER_STEP + p) * n:
                           i * per_chunk + (q * RW_PAIRS_PER_STEP + p + 1) * n] for p in range(RW_PAIRS_PER_STEP)]
                     for q in range(RW_KEPT)]
        y, state, mats = _rwkv_chunk([x[i * RW_CHUNK:(i + 1) * RW_CHUNK] for x in xs], state, known)
        ys.append(y)
        keep += [m for group in mats for m in group]
    return [ys[0] if len(ys) == 1 else jnp.concatenate(ys, axis=0)], [state], jnp.concatenate(keep, axis=0)


def _rwkv_chunk(xs, state, known=None):
    npair = RW_PAIRS_PER_STEP
    pr = range(npair)
    r, lw, k, v, av, bv = [[x[:, p * LANES:(p + 1) * LANES] for p in pr] for x in xs]
    sv = [state[p * LANES:(p + 1) * LANES] for p in pr]
    c = RW_CHUNK
    n = 2 * c
    tri = jnp.where(_tril(c), 1.0, 0.0).astype(F32)
    cl = [_tdot(tri, lw[p], 3) for p in pr]
    cl_last = [_row(cl[p], c - 1) for p in pr]
    lane = lax.broadcasted_iota(jnp.int32, (c, LANES), 1)
    h0 = lane < RW_N

    def stack(x):
        return jnp.concatenate([jnp.where(h0, x, 0.0), jnp.where(h0, 0.0, x)], axis=0)

    am = [stack(av[p] * jnp.exp(cl[p] - lw[p])) for p in pr]
    bm = [stack(bv[p] * jnp.exp(-cl[p])) for p in pr]
    km = [stack(k[p] * jnp.exp(-cl[p])) for p in pr]
    rm = [stack(r[p] * jnp.exp(cl[p])) for p in pr]
    vm = [stack(v[p]) for p in pr]
    rn = lax.broadcasted_iota(jnp.int32, (n, n), 0)
    cn = lax.broadcasted_iota(jnp.int32, (n, n), 1)
    blk = (rn >= c) == (cn >= c)
    strict = blk & (cn < rn)
    incl = blk & (cn <= rn)
    lab = [jnp.where(strict, _dot(am[p], bm[p], "nt"), 0.0) for p in pr]
    lak = [jnp.where(strict, _dot(am[p], km[p], "nt"), 0.0) for p in pr]
    wrb = [jnp.where(incl, _dot(rm[p], bm[p], "nt"), 0.0) for p in pr]
    wrk = [jnp.where(incl, _dot(rm[p], km[p], "nt"), 0.0) for p in pr]
    if known is None:
        tinv = _unit_lower_inverses(lab)
    else:
        tinv = _known_inverses(lab, known[0])
        lak, wrb, wrk = _use_kept(lak, known[1]), _use_kept(wrb, known[2]), _use_kept(wrk, known[3])
    rhs = [_dot(am[p], sv[p], "nt") + _dot(lak[p], vm[p], "nn") for p in pr]
    um = [_dot(tinv[p], rhs[p], "nn") for p in pr]
    if known is not None:
        um = _use_kept(um, known[4])
    ym = [_dot(rm[p], sv[p], "nt") + _dot(wrb[p], um[p], "nn") + _dot(wrk[p], vm[p], "nn") for p in pr]
    sn = [(sv[p] + _dot(um[p], bm[p], "tn") + _dot(vm[p], km[p], "tn")) * jnp.exp(cl_last[p]) for p in pr]
    ys = [ym[p][:c] + ym[p][c:] for p in pr]
    return jnp.concatenate(ys, axis=1), jnp.concatenate(sn, axis=0), [tinv, lak, wrb, wrk, um]


def _f_mixers(ps, xs, cs):
    return _mixers(ps, xs, cs, None)


def _f_mixers_kept(ps, xs, cs, kept):
    return _mixers(ps, xs, cs, kept[0])[:2]


def _mixers(ps, xs, cs, kept):
    oa, st = _f_hgrn(ps[:2], xs[:4], cs[:1])
    (r, lw, k, v, av, bv, g), prev = _f_rwpre(ps[2:10], xs[4:], cs[1:2])
    y, sv, keep = _f_rwscan([], [r, lw, k, v, av, bv], cs[2:], kept)
    ob, _ = _f_rwpost(ps[10:], y + [r, k, v, g], [])
    return oa + ob, st + prev + sv, [keep]


def _f_rwpost(ps, xs, cs):
    ln_w, ln_b, r_k = ps
    y, r, k, v, g = xs
    inv_n = 1.0 / RW_N
    yc = y - _headsum(y) * inv_n
    var = _headsum(yc * yc) * inv_n
    yn = yc * lax.rsqrt(var + GN_EPS)
    yn = yn * ln_w + ln_b
    bonus = _headsum(r * k * r_k) * v
    return [(yn + bonus) * g], []


def _f_merge(ps, xs, cs):
    ga, gb, ya, yb = xs
    return [_sigmoid(ga) * ya + _sigmoid(gb) * yb], []


def _f_post1(ps, xs, cs):
    x, mix = xs
    h1 = x + _rms(mix, ps[0])
    return [h1, _rms(h1, ps[1])], []


def _f_conv(ps, xs, cs):
    cw, cb = ps
    p1, p2 = cs
    w0, w1, w2 = _row(cw, 0), _row(cw, 1), _row(cw, 2)
    t = xs[0].shape[0]
    hc = []
    for i, x in enumerate(xs):
        sl = slice(i * DFF, (i + 1) * DFF)
        s1 = _shift_down(x, p1[:, sl])
        s2 = _shift_down(s1, p2[:, sl])
        hc.append(cb[:, sl] + w0[:, sl] * s2 + w1[:, sl] * s1 + w2[:, sl] * x)
    n1 = jnp.concatenate([_row(x, t - 1) for x in xs], axis=1)
    n2 = jnp.concatenate([_row(x, t - 2) for x in xs], axis=1)
    return [_silu(hc[0]) * hc[1]], [n1, n2]


class _Stage:
    def __init__(self, name, f, g, tm, par_per_g, in_pieces, in_offs, carry_shapes, out_pieces, out_dtypes,
                 kept_shapes=(), f_kept=None):
        self.name, self.f, self.g, self.tm = name, f, g, tm
        self.par_per_g, self.in_pieces, self.in_offs = par_per_g, in_pieces, in_offs
        self.carry_shapes, self.out_pieces, self.out_dtypes = carry_shapes, out_pieces, out_dtypes
        self.kept_shapes, self.f_kept = list(kept_shapes), f_kept


def _par_spec(arr, per_g, g):
    r, c = arr.shape
    if per_g:
        return pl.BlockSpec((r, c // g), lambda gi, ni: (0, gi))
    return pl.BlockSpec((r, c), lambda gi, ni: (0, 0))


def _row_spec(tm, width, off, n, rev):
    if rev:
        return pl.BlockSpec((tm, width), lambda gi, ni: (n - 1 - ni, off + gi))
    return pl.BlockSpec((tm, width), lambda gi, ni: (ni, off + gi))


def _carry_spec(shape, n, rev):
    if rev:
        return pl.BlockSpec((None, None) + shape, lambda gi, ni: (gi, n - 1 - ni, 0, 0))
    return pl.BlockSpec((None, None) + shape, lambda gi, ni: (gi, ni, 0, 0))


def _load_pieces(refs, pieces_list):
    out = []
    for ref, pieces in zip(refs, pieces_list):
        o = 0
        for w in pieces:
            out.append(ref[:, o:o + w].astype(F32))
            o += w
    return out


def _store_pieces(refs, pieces_list, vals):
    k = 0
    for ref, pieces in zip(refs, pieces_list):
        o = 0
        for w in pieces:
            ref[:, o:o + w] = vals[k].astype(ref.dtype)
            k += 1
            o += w


_ANY = pl.BlockSpec(memory_space=pl.ANY)


class _Exchange:
    def __init__(self, kind, arrs):
        self.kind, self.arrs, self.results = kind, list(arrs), None
        if kind == "scatter":
            self.out_shape = [jax.ShapeDtypeStruct((N_DEV - 1,) + a.shape[1:], a.dtype) for a in self.arrs]
        else:
            self.out_shape = [jax.ShapeDtypeStruct((N_DEV,) + a.shape, a.dtype) for a in self.arrs]
        self.nsem = (N_DEV if kind == "gather2" else N_DEV - 1) * len(self.arrs)

    def copies(self, in_refs, out_refs, ssem, rsem):
        x, y, c = lax.axis_index("x"), lax.axis_index("y"), lax.axis_index("c")
        me = 4 * x + 2 * y + c
        cps = []
        for a, (i_ref, o_ref) in enumerate(zip(in_refs, out_refs)):
            for j in range(1, N_DEV):
                px = 1 - x if j & 4 else x
                py = 1 - y if j & 2 else y
                pc = 1 - c if j & 1 else c
                if self.kind == "gather":
                    src, dst = i_ref, o_ref.at[me]
                else:
                    src, dst = i_ref.at[4 * px + 2 * py + pc], o_ref.at[j - 1]
                s = (N_DEV - 1) * a + j - 1
                cps.append(pltpu.make_async_remote_copy(src_ref=src, dst_ref=dst, send_sem=ssem.at[s],
                                                        recv_sem=rsem.at[s], device_id=(px, py, pc),
                                                        device_id_type=MESH))
        return cps

    def run(self, step, total, in_refs, out_refs, ssem, rsem):
        if self.kind == "gather2":
            return self.run_two_level(step, total, in_refs, out_refs, ssem, rsem)

        @pl.when(step == 0)
        def _():
            for cp in self.copies(in_refs, out_refs, ssem, rsem):
                cp.start()

        @pl.when(step == total - 1)
        def _():
            for cp in self.copies(in_refs, out_refs, ssem, rsem):
                cp.wait()

    def run_two_level(self, step, total, in_refs, out_refs, ssem, rsem):
        x, y, c = lax.axis_index("x"), lax.axis_index("y"), lax.axis_index("c")
        sibling, xn, yn = (x, y, 1 - c), (1 - x, y, c), (x, 1 - y, c)
        arrs = range(len(in_refs))
        ns = N_DEV

        def num(px, py, pc):
            return 4 * px + 2 * py + pc

        def copy(a, k, to, src, dst):
            return pltpu.make_async_remote_copy(src_ref=src, dst_ref=dst, send_sem=ssem.at[ns * a + k],
                                                recv_sem=rsem.at[ns * a + k], device_id=to, device_id_type=MESH)

        def blk(a, b):
            return out_refs[a].at[b]

        def half(a, b, second):
            h = self.arrs[a].shape[0] // 2
            return out_refs[a].at[b, pl.ds(h if second else 0, h)]

        bx, by, bd = num(1 - x, y, c), num(x, 1 - y, c), num(1 - x, 1 - y, c)

        def firsts(a):
            own = blk(a, num(x, y, c))
            return [copy(a, 0, sibling, in_refs[a], own), copy(a, 1, xn, in_refs[a], own),
                    copy(a, 2, yn, in_refs[a], own)]

        def seconds(a):
            return [copy(a, 3, yn, half(a, bx, False), half(a, bx, False)), copy(a, 5, sibling, blk(a, bx), blk(a, bx)),
                    copy(a, 4, xn, half(a, by, True), half(a, by, True)), copy(a, 6, sibling, blk(a, by), blk(a, by))]

        def third(a):
            return copy(a, 7, sibling, blk(a, bd), blk(a, bd))

        @pl.when(step == 0)
        def _():
            for a in arrs:
                for cp in firsts(a):
                    cp.start()

        @pl.when(step == total // 2)
        def _():
            for a in arrs:
                copy(a, 1, xn, blk(a, bx), blk(a, bx)).wait_recv()
                copy(a, 2, yn, blk(a, by), blk(a, by)).wait_recv()
                for cp in seconds(a):
                    cp.start()

        @pl.when(step == (4 * total) // 5)
        def _():
            for a in arrs:
                copy(a, 3, yn, half(a, bd, False), half(a, bd, False)).wait_recv()
                copy(a, 4, xn, half(a, bd, True), half(a, bd, True)).wait_recv()
                third(a).start()

        @pl.when(step == total - 1)
        def _():
            for a in arrs:
                for k, b in ((0, num(x, y, 1 - c)), (5, num(1 - x, y, 1 - c)), (6, num(x, 1 - y, 1 - c)),
                             (7, num(1 - x, 1 - y, 1 - c))):
                    copy(a, k, sibling, blk(a, b), blk(a, b)).wait_recv()
                for cp in firsts(a) + seconds(a) + [third(a)]:
                    cp.wait_send()


def _hook_specs(hook):
    if hook is None:
        return [], [], [], []
    na = len(hook.arrs)
    sems = [pltpu.SemaphoreType.DMA((hook.nsem,)), pltpu.SemaphoreType.DMA((hook.nsem,))]
    return [_ANY] * na, [_ANY] * na, hook.out_shape, sems


def _stage_fwd(st, t, params, inputs, hook=None):
    g, tm = st.g, min(st.tm, t)
    n = t // tm
    npar, nin, ncar, nout = len(params), len(inputs), len(st.carry_shapes), len(st.out_pieces)
    nk = len(st.kept_shapes)
    h_in, h_out, h_shape, h_sems = _hook_specs(hook)
    nh = len(h_in)

    def body(*refs):
        p_refs = refs[:npar]
        x_refs = refs[npar:npar + nin]
        hi_refs = refs[npar + nin:npar + nin + nh]
        o = npar + nin + nh
        o_refs = refs[o:o + nout]
        s_refs = refs[o + nout:o + nout + ncar]
        k_refs = refs[o + nout + ncar:o + nout + ncar + nk]
        o += nout + ncar + nk
        ho_refs = refs[o:o + nh]
        c_scr = refs[o + nh:o + nh + ncar]
        gi, ni = pl.program_id(0), pl.program_id(1)
        if hook is not None:
            step = gi * n + ni
            hook.run(step, g * n, hi_refs, ho_refs, *refs[-2:])

        @pl.when(ni == 0)
        def _():
            for c in c_scr:
                c[...] = jnp.zeros(c.shape, F32)

        ps = [r[...].astype(F32) for r in p_refs]
        xs = _load_pieces(x_refs, st.in_pieces)
        cs = [c[...] for c in c_scr]
        for s, c in zip(s_refs, cs):
            s[...] = c
        res = st.f(ps, xs, cs)
        outs, ncs = res[0], res[1]
        _store_pieces(o_refs, st.out_pieces, outs)
        for c, v in zip(c_scr, ncs):
            c[...] = v
        for kr, kv in zip(k_refs, res[2] if nk else []):
            kr[...] = kv.astype(kr.dtype)

    in_specs = [_par_spec(p, pg, g) for p, pg in zip(params, st.par_per_g)]
    in_specs += [_row_spec(tm, sum(pc), off, n, False) for pc, off in zip(st.in_pieces, st.in_offs)]
    out_specs = [_row_spec(tm, sum(pc), 0, n, False) for pc in st.out_pieces]
    out_specs += [_carry_spec(s, n, False) for s in st.carry_shapes]
    out_specs += [pl.BlockSpec(s, lambda gi, ni: (ni, 0)) for s in st.kept_shapes]
    out_shape = [jax.ShapeDtypeStruct((t, g * sum(pc)), dt) for pc, dt in zip(st.out_pieces, st.out_dtypes)]
    out_shape += [jax.ShapeDtypeStruct((g, n) + s, F32) for s in st.carry_shapes]
    out_shape += [jax.ShapeDtypeStruct((n * s[0], s[1]), BF) for s in st.kept_shapes]
    res = pl.pallas_call(
        body, name=st.name + "_fwd", grid=(g, n), in_specs=in_specs + h_in, out_specs=out_specs + h_out,
        out_shape=out_shape + h_shape,
        scratch_shapes=[pltpu.VMEM(s, F32) for s in st.carry_shapes] + h_sems,
        compiler_params=_cparams(("arbitrary", "arbitrary")),
    )(*params, *inputs, *(hook.arrs if hook else []))
    if hook is not None:
        hook.results = list(res[nout + ncar + nk:])
    return list(res[:nout]), list(res[nout:nout + ncar + nk])


def _stage_bwd(st, t, params, inputs, saved, douts, dx_dtypes, hook=None, dout_dot=None):
    g, tm = st.g, min(st.tm, t)
    n = t // tm
    npar, nin, ncar = len(params), len(inputs), len(st.carry_shapes)
    nk = len(st.kept_shapes)
    flat_d = list(dout_dot) if dout_dot is not None else [d for ds in douts for d in ds]
    nd = len(flat_d)
    dx_idx = [i for i, dt in enumerate(dx_dtypes) if dt is not None]
    h_in, h_out, h_shape, h_sems = _hook_specs(hook)
    nh = len(h_in)

    def body(*refs):
        p_refs = refs[:npar]
        x_refs = refs[npar:npar + nin]
        s_refs = refs[npar + nin:npar + nin + ncar]
        k_refs = refs[npar + nin + ncar:npar + nin + ncar + nk]
        o = npar + nin + ncar + nk
        d_refs = refs[o:o + nd]
        hi_refs = refs[o + nd:o + nd + nh]
        o += nd + nh
        dp_refs = refs[o:o + npar]
        dx_refs = refs[o + npar:o + npar + len(dx_idx)]
        ho_refs = refs[o + npar + len(dx_idx):o + npar + len(dx_idx) + nh]
        dc_scr = refs[o + npar + len(dx_idx) + nh:o + npar + len(dx_idx) + nh + ncar]
        gi, ni = pl.program_id(0), pl.program_id(1)
        if hook is not None:
            step = gi * n + ni
            hook.run(step, g * n, hi_refs, ho_refs, *refs[-2:])

        @pl.when(ni == 0)
        def _():
            for c in dc_scr:
                c[...] = jnp.zeros(c.shape, F32)

        ps = [r[...].astype(F32) for r in p_refs]
        xs = _load_pieces(x_refs, st.in_pieces)
        cs = [s[...] for s in s_refs]
        dys = [_raw_dot(d_refs[0][...], d_refs[1][...], "nt")] if dout_dot is not None else []
        k = 0
        for ds, pieces in zip(douts, st.out_pieces):
            acc = _load_pieces([d_refs[k]], [pieces])
            for j in range(1, len(ds)):
                more = _load_pieces([d_refs[k + j]], [pieces])
                acc = [a + b for a, b in zip(acc, more)]
            dys += acc
            k += len(ds)
        if nk:
            kept = [r[...].astype(F32) for r in k_refs]
            _, vjp = jax.vjp(lambda p, x, c: st.f_kept(p, x, c, kept), ps, xs, cs)
        else:
            _, vjp = jax.vjp(st.f, ps, xs, cs)
        dps, dxs, dcs = vjp((dys, [c[...] for c in dc_scr]))
        k = 0
        per_in = []
        for pieces in st.in_pieces:
            per_in.append(dxs[k:k + len(pieces)])
            k += len(pieces)
        for ref, i in zip(dx_refs, dx_idx):
            _store_pieces([ref], [st.in_pieces[i]], per_in[i])
        for c, v in zip(dc_scr, dcs):
            c[...] = v
        for ref, dp, pg in zip(dp_refs, dps, st.par_per_g):
            first = (ni == 0) if pg else ((ni == 0) & (gi == 0))

            @pl.when(first)
            def _():
                ref[...] = jnp.zeros(ref.shape, F32)

            ref[...] += dp

    in_specs = [_par_spec(p, pg, g) for p, pg in zip(params, st.par_per_g)]
    in_specs += [_row_spec(tm, sum(pc), off, n, True) for pc, off in zip(st.in_pieces, st.in_offs)]
    in_specs += [_carry_spec(s, n, True) for s in st.carry_shapes]
    in_specs += [pl.BlockSpec(s, lambda gi, ni: (n - 1 - ni, 0)) for s in st.kept_shapes]
    for ds, pc in zip(douts, st.out_pieces):
        in_specs += [_row_spec(tm, sum(pc), 0, n, True) for _ in ds]
    if dout_dot is not None:
        a, w = dout_dot
        in_specs += [pl.BlockSpec((tm, a.shape[1]), lambda gi, ni: (n - 1 - ni, 0)),
                     pl.BlockSpec(w.shape, lambda gi, ni: (0, 0), pipeline_mode=pl.Buffered(1))]
    out_specs = [_par_spec(p, pg, g) for p, pg in zip(params, st.par_per_g)]
    out_specs += [_row_spec(tm, sum(st.in_pieces[i]), 0, n, True) for i in dx_idx]
    out_shape = [jax.ShapeDtypeStruct(p.shape, F32) for p in params]
    out_shape += [jax.ShapeDtypeStruct((t, g * sum(st.in_pieces[i])), dx_dtypes[i]) for i in dx_idx]
    res = pl.pallas_call(
        body, name=st.name + "_bwd", grid=(g, n), in_specs=in_specs + h_in, out_specs=out_specs + h_out,
        out_shape=out_shape + h_shape,
        scratch_shapes=[pltpu.VMEM(s, F32) for s in st.carry_shapes] + h_sems,
        compiler_params=_cparams(("arbitrary", "arbitrary")),
    )(*params, *inputs, *saved, *flat_d, *(hook.arrs if hook else []))
    if hook is not None:
        hook.results = list(res[npar + len(dx_idx):])
    return list(res[:npar]), list(res[npar:npar + len(dx_idx)])


def _pick(n, cap):
    if n <= cap:
        return n
    best = LANES
    for k in range(1, n // LANES + 1):
        if (n // LANES) % k == 0 and k * LANES <= cap:
            best = k * LANES
    return best


def _mm(name, a, b, mode, out_dtype=F32, tm=1024, tn=512, b_outer=False):
    m = a.shape[1] if mode == "tn" else a.shape[0]
    k = a.shape[0] if mode == "tn" else a.shape[1]
    n = b.shape[0] if mode == "nt" else b.shape[1]
    tm, tn = _pick(m, tm), _pick(n, tn)
    if b_outer:
        grid = (n // tn, m // tm)
        ij = lambda p, q: (q, p)
    else:
        grid = (m // tm, n // tn)
        ij = lambda p, q: (p, q)

    def body(a_ref, b_ref, o_ref):
        o_ref[...] = _raw_dot(a_ref[...], b_ref[...], mode).astype(o_ref.dtype)

    if mode == "tn":
        a_spec = pl.BlockSpec((k, tm), lambda p, q: (0, ij(p, q)[0]))
    else:
        a_spec = pl.BlockSpec((tm, k), lambda p, q: (ij(p, q)[0], 0))
    b_mode = dict(pipeline_mode=pl.Buffered(1)) if tn == n else {}
    if mode == "nt":
        b_spec = pl.BlockSpec((tn, k), lambda p, q: (ij(p, q)[1], 0), **b_mode)
    else:
        b_spec = pl.BlockSpec((k, tn), lambda p, q: (0, ij(p, q)[1]), **b_mode)
    return pl.pallas_call(
        body, name=name, grid=grid, in_specs=[a_spec, b_spec],
        out_specs=pl.BlockSpec((tm, tn), lambda p, q: ij(p, q)),
        out_shape=jax.ShapeDtypeStruct((m, n), out_dtype),
        compiler_params=_cparams(("arbitrary", "arbitrary")),
    )(a, b)


def _mm_multi(name, pairs, mode, out_dtype, tm=1024, tn=512):
    a0, b0 = pairs[0]
    m = a0.shape[1] if mode == "tn" else a0.shape[0]
    k = a0.shape[0] if mode == "tn" else a0.shape[1]
    n = b0.shape[0] if mode == "nt" else b0.shape[1]
    tm, tn = _pick(m, tm), _pick(n, tn)
    npair = len(pairs)

    def body(*refs):
        for p in range(npair):
            refs[2 * npair + p][...] = _raw_dot(refs[2 * p][...], refs[2 * p + 1][...], mode).astype(out_dtype)

    a_spec = pl.BlockSpec((k, tm), lambda i, j: (0, i)) if mode == "tn" else pl.BlockSpec((tm, k), lambda i, j: (i, 0))
    b_spec = pl.BlockSpec((tn, k), lambda i, j: (j, 0)) if mode == "nt" else pl.BlockSpec((k, tn), lambda i, j: (0, j))
    return pl.pallas_call(
        body, name=name, grid=(m // tm, n // tn), in_specs=[a_spec, b_spec] * npair,
        out_specs=[pl.BlockSpec((tm, tn), lambda i, j: (i, j))] * npair,
        out_shape=[jax.ShapeDtypeStruct((m, n), out_dtype)] * npair,
        compiler_params=_cparams(("arbitrary", "arbitrary")),
    )(*[x for pair in pairs for x in pair])


def _mm_cols_tn(name, pieces, b, out_dtype, tm):
    k, n = b.shape
    counts = [p.shape[1] // tm for p in pieces]
    starts = [sum(counts[:i]) for i in range(len(pieces))]
    na = len(pieces)

    def body(*refs):
        b_ref, o_ref = refs[na], refs[-1]
        i = pl.program_id(0)
        for a_ref, s, c in zip(refs[:na], starts, counts):
            @pl.when((i >= s) & (i < s + c))
            def _():
                o_ref[...] = _raw_dot(a_ref[...], b_ref[...], "tn").astype(o_ref.dtype)

    def spec(s, c):
        return pl.BlockSpec((k, tm), lambda i: (0, jnp.clip(i - s, 0, c - 1)))

    return pl.pallas_call(
        body, name=name, grid=(sum(counts),),
        in_specs=[spec(s, c) for s, c in zip(starts, counts)]
        + [pl.BlockSpec(b.shape, lambda i: (0, 0), pipeline_mode=pl.Buffered(1))],
        out_specs=pl.BlockSpec((tm, n), lambda i: (i, 0)),
        out_shape=jax.ShapeDtypeStruct((sum(counts) * tm, n), out_dtype),
        compiler_params=_cparams(("arbitrary",)),
    )(*pieces, b)


def _norm_in_proj(x, g, w_t, tm, tn):
    t, k = x.shape
    n = w_t.shape[0]
    tm, tn = _pick(t, tm), _pick(n, tn)

    def body(x_ref, g_ref, w_ref, xn_ref, z_ref):
        xn = _rms(x_ref[...], g_ref[...]).astype(BF)
        xn_ref[...] = xn
        z_ref[...] = _raw_dot(xn, w_ref[...], "nt")

    xns, z = pl.pallas_call(
        body, name="in_proj", grid=(n // tn, t // tm),
        in_specs=[pl.BlockSpec((tm, k), lambda j, i: (i, 0)), pl.BlockSpec((1, k), lambda j, i: (0, 0)),
                  pl.BlockSpec((tn, k), lambda j, i: (j, 0))],
        out_specs=[pl.BlockSpec((None, tm, k), lambda j, i: (j, i, 0)), pl.BlockSpec((tm, tn), lambda j, i: (i, j))],
        out_shape=[jax.ShapeDtypeStruct((n // tn, t, k), BF), jax.ShapeDtypeStruct((t, n), F32)],
        compiler_params=_cparams(("arbitrary", "arbitrary")),
    )(x, g, w_t)
    return xns[0], z


def _merge_out_post(z, o_a, o_b, w_a, w_b, w_out, x, g_post, g_pre2, tm):
    t = x.shape[0]
    tm = _pick(t, tm)
    w = 256
    npc = D // w
    ga0, gb0 = (IN_COLS - 2 * D) // w, (IN_COLS - D) // w

    def body(*refs):
        ga_refs, gb_refs = refs[:npc], refs[npc:2 * npc]
        oa_ref, ob_ref, wa_ref, wb_ref, w_ref, x_ref, gp_ref, g2_ref = refs[2 * npc:2 * npc + 8]
        ya_ref, yb_ref, m_ref, mix_ref, h_ref, xn_ref = refs[2 * npc + 8:]
        ya = _raw_dot(oa_ref[...], wa_ref[...], "nn").astype(BF)
        yb = _raw_dot(ob_ref[...], wb_ref[...], "nn").astype(BF)
        ya_ref[...] = ya
        yb_ref[...] = yb
        parts = []
        for p in range(npc):
            cols = slice(p * w, (p + 1) * w)
            parts.append(_sigmoid(ga_refs[p][...]) * ya[:, cols].astype(F32)
                         + _sigmoid(gb_refs[p][...]) * yb[:, cols].astype(F32))
        merged = jnp.concatenate(parts, axis=1).astype(BF)
        m_ref[...] = merged
        mix = _raw_dot(merged, w_ref[...], "nn")
        mix_ref[...] = mix
        h1 = x_ref[...] + _rms(mix, gp_ref[...])
        h_ref[...] = h1
        xn_ref[...] = _rms(h1, g2_ref[...]).astype(BF)

    row = pl.BlockSpec((tm, D), lambda i: (i, 0))
    one = pl.BlockSpec((1, D), lambda i: (0, 0))

    def gate(b0):
        return [pl.BlockSpec((tm, w), functools.partial(lambda i, b: (i, b), b=b0 + p)) for p in range(npc)]

    wgt = pl.BlockSpec((D, D), lambda i: (0, 0), pipeline_mode=pl.Buffered(1))
    return pl.pallas_call(
        body, name="merge_out_post", grid=(t // tm,),
        in_specs=gate(ga0) + gate(gb0) + [row, row, wgt, wgt, wgt, row, one, one],
        out_specs=[row] * 6,
        out_shape=[jax.ShapeDtypeStruct((t, D), BF), jax.ShapeDtypeStruct((t, D), BF), jax.ShapeDtypeStruct((t, D), BF),
                   jax.ShapeDtypeStruct((t, D), F32), jax.ShapeDtypeStruct((t, D), F32),
                   jax.ShapeDtypeStruct((t, D), BF)],
        compiler_params=_cparams(("arbitrary",)),
    )(*([z] * (2 * npc)), o_a, o_b, w_a, w_b, w_out, x, g_post, g_pre2)


def _accumulate(ni, refs, vals):
    @pl.when(ni == 0)
    def _():
        for r in refs:
            r[...] = jnp.zeros(r.shape, F32)

    for r, v in zip(refs, vals):
        r[...] += v


def _dmerged_merge_bwd(dmix, w_out, w_a, w_b, z, y_a, y_b, tm):
    t = dmix.shape[0]
    tm = _pick(t, tm)
    w = 256
    npc = D // w
    ga0, gb0 = (IN_COLS - 2 * D) // w, (IN_COLS - D) // w

    def body(*refs):
        dm_ref, w_ref, wa_ref, wb_ref = refs[:4]
        ga_refs, gb_refs = refs[4:4 + npc], refs[4 + npc:4 + 2 * npc]
        ya_ref, yb_ref, dga_ref, dgb_ref, dya_ref, dyb_ref, doa_ref, dob_ref = refs[4 + 2 * npc:]
        dmerged = _raw_dot(dm_ref[...], w_ref[...], "nt")
        dyas, dybs = [], []
        for p in range(npc):
            cols = slice(p * w, (p + 1) * w)
            xs = [ga_refs[p][...], gb_refs[p][...], ya_ref[:, cols].astype(F32), yb_ref[:, cols].astype(F32)]
            _, vjp = jax.vjp(lambda *a: _f_merge([], list(a), [])[0][0], *xs)
            dga, dgb, dya, dyb = vjp(dmerged[:, cols])
            dga_ref[:, cols] = dga.astype(BF)
            dgb_ref[:, cols] = dgb.astype(BF)
            dyas.append(dya.astype(BF))
            dybs.append(dyb.astype(BF))
        dya, dyb = jnp.concatenate(dyas, axis=1), jnp.concatenate(dybs, axis=1)
        dya_ref[...] = dya
        dyb_ref[...] = dyb
        doa_ref[...] = _raw_dot(dya, wa_ref[...], "nt").astype(BF)
        dob_ref[...] = _raw_dot(dyb, wb_ref[...], "nt").astype(BF)

    row = pl.BlockSpec((tm, D), lambda i: (i, 0))
    wgt = pl.BlockSpec((D, D), lambda i: (0, 0), pipeline_mode=pl.Buffered(1))

    def gate(b0):
        return [pl.BlockSpec((tm, w), functools.partial(lambda i, b: (i, b), b=b0 + p)) for p in range(npc)]

    return pl.pallas_call(
        body, name="merge_bwd", grid=(t // tm,),
        in_specs=[row, wgt, wgt, wgt] + gate(ga0) + gate(gb0) + [row, row],
        out_specs=[row] * 6, out_shape=[jax.ShapeDtypeStruct((t, D), BF)] * 6,
        compiler_params=_cparams(("arbitrary",)),
    )(dmix, w_out, w_a, w_b, *([z] * (2 * npc)), y_a, y_b)


def _dxn2_post1_bwd(pieces, w_up_t, x, mix, dh1, g_post, g_pre2, tm):
    t = x.shape[0]
    tm = _pick(t, tm)
    k = w_up_t.shape[0]
    offs = [sum(p.shape[1] for p in pieces[:i]) for i in range(len(pieces))]
    na = len(pieces)

    def body(*refs):
        w_ref, x_ref, m_ref, dh_ref, gp_ref, g2_ref, dgp_ref, dg2_ref, dx_ref, dm_ref = refs[na:]
        dxn2 = None
        for a_ref, off in zip(refs[:na], offs):
            part = _raw_dot(a_ref[...], w_ref[off:off + a_ref.shape[1], :], "nn")
            dxn2 = part if dxn2 is None else dxn2 + part
        _, vjp = jax.vjp(lambda gp, g2, xx, mm: _f_post1([gp, g2], [xx, mm], [])[0],
                         gp_ref[...], g2_ref[...], x_ref[...], m_ref[...])
        dgp, dg2, dx, dm = vjp([dh_ref[...], dxn2])
        _accumulate(pl.program_id(0), [dgp_ref, dg2_ref], [dgp, dg2])
        dx_ref[...] = dx
        dm_ref[...] = dm.astype(BF)

    row = pl.BlockSpec((tm, D), lambda i: (i, 0))
    one = pl.BlockSpec((1, D), lambda i: (0, 0))
    return pl.pallas_call(
        body, name="post1_bwd", grid=(t // tm,),
        in_specs=[pl.BlockSpec((tm, p.shape[1]), lambda i: (i, 0)) for p in pieces]
        + [pl.BlockSpec((k, D), lambda i: (0, 0), pipeline_mode=pl.Buffered(1)), row, row, row, one, one],
        out_specs=[one, one, row, row],
        out_shape=[jax.ShapeDtypeStruct((1, D), F32), jax.ShapeDtypeStruct((1, D), F32),
                   jax.ShapeDtypeStruct((t, D), F32), jax.ShapeDtypeStruct((t, D), BF)],
        compiler_params=_cparams(("arbitrary",)),
    )(*pieces, w_up_t, x, mix, dh1, g_post, g_pre2)


def _conv_taps(h, cw, cb, p2, p1):
    s1 = _shift_down(h, p1)
    s2 = _shift_down(s1, p2)
    return cb + _row(cw, 0) * s2 + _row(cw, 1) * s1 + _row(cw, 2) * h


def _up_conv(xn2, w_up_t, conv_w, conv_b, tm, tc):
    t = xn2.shape[0]
    tm = _pick(t, tm)
    tn = _pick(DFF, 1408)
    nj = DFF // tn
    sub = tm // tc
    n = t // tc
    last = t // tm - 1

    def body(x_ref, wg_ref, wv_ref, cwg_ref, cwv_ref, cbg_ref, cbv_ref, hg_ref, hv_ref, act_ref, c1_ref, c2_ref, prev):
        j, i = pl.program_id(0), pl.program_id(1)

        @pl.when(i == 0)
        def _():
            prev[...] = jnp.zeros(prev.shape, F32)

        x = x_ref[...]
        hg = _raw_dot(x, wg_ref[...], "nt")
        hv = _raw_dot(x, wv_ref[...], "nt")
        hg_ref[...] = hg
        hv_ref[...] = hv
        pg, pv = prev[0:SUBLANES], prev[SUBLANES:2 * SUBLANES]
        cg = _conv_taps(hg, cwg_ref[...], cbg_ref[...], _row(pg, SUBLANES - 2), _row(pg, SUBLANES - 1))
        cv = _conv_taps(hv, cwv_ref[...], cbv_ref[...], _row(pv, SUBLANES - 2), _row(pv, SUBLANES - 1))
        act_ref[...] = (_silu(cg) * cv).astype(BF)
        prev[0:SUBLANES] = hg[tm - SUBLANES:tm]
        prev[SUBLANES:2 * SUBLANES] = hv[tm - SUBLANES:tm]

        def keep(h, off):
            cols = slice(off, off + tn)

            @pl.when(i == 0)
            def _():
                c1_ref[0, :, cols] = jnp.zeros((1, tn), F32)
                c2_ref[0, :, cols] = jnp.zeros((1, tn), F32)

            for s in range(sub):
                def put(s=s):
                    tail = h[(s + 1) * tc - SUBLANES:(s + 1) * tc]
                    c1_ref[i * sub + s + 1, :, cols] = _row(tail, SUBLANES - 1)
                    c2_ref[i * sub + s + 1, :, cols] = _row(tail, SUBLANES - 2)

                if s < sub - 1:
                    put()
                else:
                    pl.when(i < last)(put)

        for col in range(nj):
            @pl.when(j == col)
            def _(col=col):
                keep(hg, col * tn)
                keep(hv, DFF + col * tn)

    def cols(rows, off):
        return pl.BlockSpec((rows, tn), lambda j, i: (0, j + off))

    tile = pl.BlockSpec((tm, tn), lambda j, i: (i, j))
    before = pl.BlockSpec((n, 1, 2 * DFF), lambda j, i: (0, 0, 0))
    return pl.pallas_call(
        body, name="up_conv", grid=(nj, t // tm),
        in_specs=[pl.BlockSpec((tm, D), lambda j, i: (i, 0)), pl.BlockSpec((tn, D), lambda j, i: (j, 0)),
                  pl.BlockSpec((tn, D), lambda j, i: (j + nj, 0)), cols(3, 0), cols(3, nj), cols(1, 0), cols(1, nj)],
        out_specs=[tile, tile, tile, before, before],
        out_shape=[jax.ShapeDtypeStruct((t, DFF), F32), jax.ShapeDtypeStruct((t, DFF), F32),
                   jax.ShapeDtypeStruct((t, DFF), BF), jax.ShapeDtypeStruct((n, 1, 2 * DFF), F32),
                   jax.ShapeDtypeStruct((n, 1, 2 * DFF), F32)],
        scratch_shapes=[pltpu.VMEM((2 * SUBLANES, tn), F32)],
        compiler_params=_cparams(("arbitrary", "arbitrary")),
    )(xn2, w_up_t, w_up_t, conv_w, conv_w, conv_b, conv_b)


def _dxn_pre1_bwd(pieces, w_t, x, dx_res, g, tm, token):
    t = x.shape[0]
    tm = _pick(t, tm)
    offs = [sum(p.shape[1] for p in pieces[:i]) for i in range(len(pieces))]
    na = len(pieces)

    def body(*refs):
        w_ref, x_ref, r_ref, g_ref = refs[na:na + 4]
        dg_ref, dx_ref = refs[-2:]
        dxn = None
        for a_ref, off in zip(refs[:na], offs):
            part = _raw_dot(a_ref[...], w_ref[off:off + a_ref.shape[1], :], "nn")
            dxn = part if dxn is None else dxn + part
        _, vjp = jax.vjp(lambda gg, xx: _f_pre1_residual([gg], [xx], [])[0], g_ref[...], x_ref[...])
        dg, dx = vjp([dxn, r_ref[...]])
        _accumulate(pl.program_id(0), [dg_ref], [dg])
        dx_ref[...] = dx

    row = pl.BlockSpec((tm, D), lambda i: (i, 0))
    one = pl.BlockSpec((1, D), lambda i: (0, 0))
    return pl.pallas_call(
        body, name="pre1_bwd", grid=(t // tm,),
        in_specs=[pl.BlockSpec((tm, p.shape[1]), lambda i: (i, 0)) for p in pieces]
        + [pl.BlockSpec(w_t.shape, lambda i: (0, 0), pipeline_mode=pl.Buffered(1)), row, row, one,
           pl.BlockSpec(token.shape, lambda i: (0, 0))],
        out_specs=[one, row],
        out_shape=[jax.ShapeDtypeStruct((1, D), F32), jax.ShapeDtypeStruct((t, D), F32)],
        compiler_params=_cparams(("arbitrary",)),
    )(*pieces, w_t, x, dx_res, g, token)


def _down_loss(act, w_down, g_post, h1, tgt, tm):
    t, k = act.shape
    tm = _pick(t, tm)

    def body(a_ref, w_ref, g_ref, h_ref, t_ref, loss_ref, dg_ref, dh_ref, df_ref):
        ni = pl.program_id(0)
        ff = _raw_dot(a_ref[...], w_ref[...], "nn")
        target = t_ref[...]

        def lossf(g, h1, ff):
            e = h1 + _rms(ff, g) - target
            return 0.5 * jnp.sum(jnp.mean(e * e, axis=-1))

        l, (dg, dh, df) = jax.value_and_grad(lossf, argnums=(0, 1, 2))(g_ref[...], h_ref[...], ff)

        @pl.when(ni == 0)
        def _():
            loss_ref[...] = jnp.zeros(loss_ref.shape, F32)
            dg_ref[...] = jnp.zeros(dg_ref.shape, F32)

        loss_ref[...] += jnp.full(loss_ref.shape, l, F32)
        dg_ref[...] += dg
        dh_ref[...] = dh
        df_ref[...] = df.astype(df_ref.dtype)

    row = pl.BlockSpec((tm, D), lambda ni: (ni, 0))
    one = pl.BlockSpec((1, D), lambda ni: (0, 0))
    return pl.pallas_call(
        body, name="down_loss", grid=(t // tm,),
        in_specs=[pl.BlockSpec((tm, k), lambda ni: (ni, 0)),
                  pl.BlockSpec((k, D), lambda ni: (0, 0), pipeline_mode=pl.Buffered(1)), one, row, row],
        out_specs=[pl.BlockSpec((1, LANES), lambda ni: (0, 0)), one, row, row],
        out_shape=[jax.ShapeDtypeStruct((1, LANES), F32), jax.ShapeDtypeStruct((1, D), F32),
                   jax.ShapeDtypeStruct((t, D), F32), jax.ShapeDtypeStruct((t, D), BF)],
        compiler_params=_cparams(("arbitrary",)),
    )(act, w_down, g_post, h1, tgt)


_ANY = pl.BlockSpec(memory_space=pl.ANY)


def _all_gather(name, blks):
    na = len(blks)
    ns = 8

    def body(*refs):
        x_refs, out_refs = refs[:na], refs[na:2 * na]
        send_sems, recv_sems, local_sems = refs[2 * na:]
        x, y, cc = lax.axis_index("x"), lax.axis_index("y"), lax.axis_index("c")
        sibling, xn, yn = (x, y, 1 - cc), (1 - x, y, cc), (x, 1 - y, cc)

        def num(px, py, pc):
            return 4 * px + 2 * py + pc

        def copy(a, k, to, src, dst):
            return pltpu.make_async_remote_copy(src_ref=src, dst_ref=dst, send_sem=send_sems.at[ns * a + k],
                                                recv_sem=recv_sems.at[ns * a + k], device_id=to, device_id_type=MESH)

        def halves(a, blk):
            h = blks[a].shape[0] // 2
            return out_refs[a].at[blk, pl.ds(0, h)], out_refs[a].at[blk, pl.ds(h, h)]

        mine, sends = [], []
        for a in range(na):
            o = out_refs[a]
            m = pltpu.make_async_copy(x_refs[a], o.at[num(x, y, cc)], local_sems.at[a])
            m.start()
            mine.append(m)
            own = o.at[num(x, y, cc)]
            sends.append([copy(a, 0, sibling, x_refs[a], own), copy(a, 1, xn, x_refs[a], own),
                          copy(a, 2, yn, x_refs[a], own)])
            for cp in sends[a]:
                cp.start()
        for a in range(na):
            o = out_refs[a]
            bx, by, bd = num(1 - x, y, cc), num(x, 1 - y, cc), num(1 - x, 1 - y, cc)
            copy(a, 1, xn, o.at[bx], o.at[bx]).wait_recv()
            more = [copy(a, 3, yn, halves(a, bx)[0], halves(a, bx)[0]), copy(a, 5, sibling, o.at[bx], o.at[bx])]
            for cp in more:
                cp.start()
            sends[a] += more
        for a in range(na):
            o = out_refs[a]
            bx, by, bd = num(1 - x, y, cc), num(x, 1 - y, cc), num(1 - x, 1 - y, cc)
            copy(a, 2, yn, o.at[by], o.at[by]).wait_recv()
            more = [copy(a, 4, xn, halves(a, by)[1], halves(a, by)[1]), copy(a, 6, sibling, o.at[by], o.at[by])]
            for cp in more:
                cp.start()
            sends[a] += more
        for a in range(na):
            o = out_refs[a]
            bd = num(1 - x, 1 - y, cc)
            copy(a, 3, yn, halves(a, bd)[0], halves(a, bd)[0]).wait_recv()
            copy(a, 4, xn, halves(a, bd)[1], halves(a, bd)[1]).wait_recv()
            fw = copy(a, 7, sibling, o.at[bd], o.at[bd])
            fw.start()
            sends[a].append(fw)
        for a in range(na):
            o = out_refs[a]
            for k, blk in ((0, num(x, y, 1 - cc)), (5, num(1 - x, y, 1 - cc)), (6, num(x, 1 - y, 1 - cc)),
                           (7, num(1 - x, 1 - y, 1 - cc))):
                copy(a, k, sibling, o.at[blk], o.at[blk]).wait_recv()
            for cp in sends[a]:
                cp.wait_send()
        for m in mine:
            m.wait()

    res = pl.pallas_call(
        body, name=name, in_specs=[_ANY] * na, out_specs=[_ANY] * na,
        out_shape=[jax.ShapeDtypeStruct((N_DEV,) + b.shape, b.dtype) for b in blks],
        scratch_shapes=[pltpu.SemaphoreType.DMA((ns * na,)), pltpu.SemaphoreType.DMA((ns * na,)),
                        pltpu.SemaphoreType.DMA((na,))],
    )(*blks)
    return list(res)


def _all_gather_small(name, blk):
    def body(x_ref, out_ref, ssem, rsem, lsem):
        x, y, c = lax.axis_index("x"), lax.axis_index("y"), lax.axis_index("c")
        me = 4 * x + 2 * y + c
        mine = pltpu.make_async_copy(x_ref, out_ref.at[me], lsem)
        mine.start()
        cps = []
        for j in range(1, N_DEV):
            px = 1 - x if j & 4 else x
            py = 1 - y if j & 2 else y
            pc = 1 - c if j & 1 else c
            cps.append(pltpu.make_async_remote_copy(src_ref=x_ref, dst_ref=out_ref.at[me], send_sem=ssem.at[j - 1],
                                                    recv_sem=rsem.at[j - 1], device_id=(px, py, pc),
                                                    device_id_type=MESH))
        for cp in cps:
            cp.start()
        for cp in cps:
            cp.wait()
        mine.wait()

    return pl.pallas_call(
        body, name=name, in_specs=[_ANY], out_specs=_ANY,
        out_shape=jax.ShapeDtypeStruct((N_DEV,) + blk.shape, blk.dtype),
        scratch_shapes=[pltpu.SemaphoreType.DMA((N_DEV - 1,)), pltpu.SemaphoreType.DMA((N_DEV - 1,)),
                        pltpu.SemaphoreType.DMA],
    )(blk)


def _reduce_pair(g8s):
    na = len(g8s)

    def body(*refs):
        g_refs, recv_refs = refs[:na], refs[na:2 * na]
        ssem, rsem = refs[2 * na:]
        x, y, cc = lax.axis_index("x"), lax.axis_index("y"), lax.axis_index("c")
        chips = [(x, y), (1 - x, y), (x, 1 - y), (1 - x, 1 - y)]
        sib = (x, y, 1 - cc)
        for a in range(na):
            for k, (cx, cy) in enumerate(chips):
                pltpu.make_async_remote_copy(
                    src_ref=g_refs[a].at[4 * cx + 2 * cy + 1 - cc], dst_ref=recv_refs[a].at[k],
                    send_sem=ssem.at[a], recv_sem=rsem.at[a], device_id=sib, device_id_type=MESH).start()
        for a in range(na):
            pltpu.make_async_remote_copy(src_ref=recv_refs[a], dst_ref=recv_refs[a], send_sem=ssem.at[a],
                                         recv_sem=rsem.at[a], device_id=sib, device_id_type=MESH).wait()

    res = pl.pallas_call(
        body, name="reduce_pair", in_specs=[_ANY] * na, out_specs=[_ANY] * na,
        out_shape=[jax.ShapeDtypeStruct((4,) + g.shape[1:], g.dtype) for g in g8s],
        scratch_shapes=[pltpu.SemaphoreType.DMA((na,)), pltpu.SemaphoreType.DMA((na,))],
    )(*g8s)
    return list(res)


_HBM = pl.BlockSpec(memory_space=pltpu.HBM)
_SEM = pl.BlockSpec(memory_space=pltpu.SEMAPHORE)
_EFFECT = pltpu.SideEffectType.DATAFLOW_SIDE_EFFECTING


def _chip_swap_copies(s_refs, land_refs, ssem, rsem):
    x, y, c = lax.axis_index("x"), lax.axis_index("y"), lax.axis_index("c")
    targets = [(1 - x, y, c), (x, 1 - y, c), (1 - x, 1 - y, c)]
    return [pltpu.make_async_remote_copy(src_ref=s.at[k], dst_ref=d.at[k], send_sem=ssem.at[3 * a + k],
                                         recv_sem=rsem.at[3 * a + k], device_id=targets[k], device_id_type=MESH)
            for a, (s, d) in enumerate(zip(s_refs, land_refs)) for k in range(3)]


def _chip_swap_start(sends):
    na = len(sends)

    def body(*refs):
        cps = _chip_swap_copies(refs[:na], refs[na:2 * na], refs[2 * na], refs[2 * na + 1])
        for cp in cps:
            cp.start()
        token = refs[-1]
        token[...] = jnp.zeros(token.shape, token.dtype)

    bufs = [pltpu.HBM(s.shape, s.dtype) for s in sends]
    res = pl.pallas_call(
        body, name="chip_swap_start",
        out_shape=[pltpu.SemaphoreType.DMA((3 * na,)), pltpu.SemaphoreType.DMA((3 * na,))] + bufs + bufs
        + [jax.ShapeDtypeStruct((8, LANES), F32)],
        in_specs=[_HBM] * (2 * na), out_specs=[_SEM, _SEM] + [_HBM] * (2 * na) + [pl.BlockSpec(memory_space=pltpu.VMEM)],
        input_output_aliases={i: 2 + i for i in range(2 * na)},
        compiler_params=pltpu.CompilerParams(has_side_effects=_EFFECT),
    )(*[pltpu.with_memory_space_constraint(s, pltpu.HBM) for s in sends],
      *[pltpu.with_memory_space_constraint(lax.empty(s.shape, s.dtype), pltpu.HBM) for s in sends])
    return res[0], res[1], list(res[2:2 + na]), list(res[2 + na:2 + 2 * na]), res[-1]


def _chip_swap_wait(ssem, rsem, srcs, lands, after):
    na = len(srcs)

    def body(*refs):
        cps = _chip_swap_copies(refs[:na], refs[na:2 * na], refs[2 * na], refs[2 * na + 1])
        for cp in cps:
            cp.wait_send()
            cp.wait_recv()

    bufs = [pltpu.HBM(s.shape, s.dtype) for s in srcs]
    res = pl.pallas_call(
        body, name="chip_swap_wait", out_shape=bufs + bufs,
        in_specs=[_HBM] * (2 * na) + [_SEM, _SEM, _ANY], out_specs=[_HBM] * (2 * na),
        input_output_aliases={i: i for i in range(2 * na)},
        compiler_params=pltpu.CompilerParams(has_side_effects=_EFFECT),
    )(*srcs, *lands, ssem, rsem, after)
    return list(res[na:])


def _pick_rows(r, c, budget=TILE_BYTES):
    if r * c * 4 <= budget or r % 16:
        return r
    best = 16
    for tr in range(16, r, 16):
        if r % tr == 0 and tr * c * 4 <= budget:
            best = tr
    return best


def _pair_sum(name, idx4, g8, recv4):
    _, r, c = g8.shape
    tr = _pick_rows(r, c, 2 * TILE_BYTES)

    def body(idx_ref, a_ref, b_ref, o0_ref, o3_ref):
        k = pl.program_id(1)
        s = a_ref[...].astype(F32) + b_ref[...].astype(F32)

        @pl.when(k == 0)
        def _():
            o0_ref[...] = s

        @pl.when(k > 0)
        def _():
            o3_ref[...] = s.astype(BF)

    spec = pltpu.PrefetchScalarGridSpec(
        num_scalar_prefetch=1, grid=(r // tr, 4),
        in_specs=[pl.BlockSpec((None, tr, c), lambda i, k, idx: (idx[k], i, 0)),
                  pl.BlockSpec((None, tr, c), lambda i, k, idx: (k, i, 0))],
        out_specs=[pl.BlockSpec((tr, c), lambda i, k, idx: (i, 0)),
                   pl.BlockSpec((None, tr, c), lambda i, k, idx: (jnp.maximum(k - 1, 0), i, 0))])
    return pl.pallas_call(
        body, name=name, grid_spec=spec,
        out_shape=[jax.ShapeDtypeStruct((r, c), F32), jax.ShapeDtypeStruct((3, r, c), BF)],
        compiler_params=_cparams(("arbitrary", "arbitrary")),
    )(idx4, g8, recv4)


def _adamw(w, g, m, v):
    m = ADAM_B1 * m + (1.0 - ADAM_B1) * g
    v = ADAM_B2 * v + (1.0 - ADAM_B2) * jnp.square(g)
    m_hat = m / (1.0 - ADAM_B1 ** ADAM_STEP)
    v_hat = v / (1.0 - ADAM_B2 ** ADAM_STEP)
    delta = -ADAM_LR * (m_hat / (jnp.sqrt(v_hat) + ADAM_EPS) + ADAM_WD * w)
    return delta, m, v


def _adam_sharded(name, idx1, own, recv, w, m, v, after=None):
    r, c = w.shape
    tr = _pick_rows(r, c, 2 * TILE_BYTES)
    nj = recv.shape[0]
    extra = [] if after is None else [after]

    def body(idx_ref, p_ref, r_ref, w_ref, m_ref, v_ref, *rest):
        g_out, d_out, m_out, v_out = rest[-4:]
        g = p_ref[...].astype(F32)
        for k in range(nj):
            g = g + r_ref[k].astype(F32)
        d, mn, vn = _adamw(w_ref[...], g, m_ref[...], v_ref[...])
        g_out[...] = g
        d_out[...] = d
        m_out[...] = mn
        v_out[...] = vn

    row = pl.BlockSpec((tr, c), lambda i, idx: (i, 0))
    spec = pltpu.PrefetchScalarGridSpec(
        num_scalar_prefetch=1, grid=(r // tr,),
        in_specs=[pl.BlockSpec((None, tr, c), lambda i, idx: (idx[0], i, 0)),
                  pl.BlockSpec((nj, tr, c), lambda i, idx: (0, i, 0)), row, row, row]
        + [pl.BlockSpec(e.shape, lambda i, idx: (0, 0)) for e in extra],
        out_specs=[row] * 4)
    return pl.pallas_call(
        body, name=name, grid_spec=spec, out_shape=[jax.ShapeDtypeStruct((r, c), F32)] * 4,
        compiler_params=_cparams(("arbitrary",)),
    )(idx1, own, recv, w, m, v, *extra)


def _repl_rows():
    rows, r = {}, 0
    for name, cols in REPL:
        rows[name] = r
        r += REPL_ROWS.get(name, 1) * ((cols + D - 1) // D)
    return rows


LOSS_ROW = 24


def _pack_replicated(grads, loss_acc, after):
    rows = _repl_rows()
    names = [n for n, _ in REPL]

    def body(*refs):
        o_ref = refs[-1]
        o_ref[...] = jnp.zeros(o_ref.shape, F32)
        o_ref[LOSS_ROW:LOSS_ROW + 1, 0:LANES] = refs[len(names)][...]
        for name, ref in zip(names, refs[:len(names)]):
            r0 = rows[name]
            nr, nc = ref.shape
            if nc <= D:
                o_ref[r0:r0 + nr, 0:nc] = ref[...]
            else:
                for j in range((nc + D - 1) // D):
                    lo, hi = j * D, min(nc, (j + 1) * D)
                    o_ref[r0 + j:r0 + j + 1, 0:hi - lo] = ref[:, lo:hi]

    return pl.pallas_call(body, name="pack_replicated", out_shape=jax.ShapeDtypeStruct((REPL_TOTAL, D), F32),
                          in_specs=[pl.BlockSpec(memory_space=pltpu.VMEM)] * (len(names) + 1) + [_ANY] * len(after),
                          compiler_params=_cparams())(*[grads[n] for n in names], loss_acc, *after)


def _adam_replicated(g8, ws, ms, vs):
    rows = _repl_rows()
    names = [n for n, _ in REPL]
    np_ = len(names)

    def body(*refs):
        g_ref = refs[0]
        w_refs, m_refs, v_refs = refs[1:1 + np_], refs[1 + np_:1 + 2 * np_], refs[1 + 2 * np_:1 + 3 * np_]
        outs = refs[1 + 3 * np_:1 + 7 * np_]
        scr = refs[-1]
        g = g_ref[0]
        for k in range(1, N_DEV):
            g = g + g_ref[k]
        scr[...] = g
        refs[1 + 7 * np_][...] = scr[LOSS_ROW:LOSS_ROW + 1, 0:LANES]
        for i, name in enumerate(names):
            r0 = rows[name]
            nr, nc = w_refs[i].shape
            if nc <= D:
                gi = scr[r0:r0 + nr, 0:nc]
            else:
                parts = []
                for j in range((nc + D - 1) // D):
                    lo, hi = j * D, min(nc, (j + 1) * D)
                    parts.append(scr[r0 + j:r0 + j + 1, 0:hi - lo])
                gi = jnp.concatenate(parts, axis=1)
            d, mn, vn = _adamw(w_refs[i][...], gi, m_refs[i][...], v_refs[i][...])
            outs[i][...] = gi
            outs[np_ + i][...] = d
            outs[2 * np_ + i][...] = mn
            outs[3 * np_ + i][...] = vn

    shp = [jax.ShapeDtypeStruct(w.shape, F32) for w in ws]
    res = pl.pallas_call(body, name="adam_replicated", out_shape=shp * 4 + [jax.ShapeDtypeStruct((1, LANES), F32)],
                         scratch_shapes=[pltpu.VMEM((REPL_TOTAL, D), F32)], compiler_params=_cparams(),
                         )(g8, *ws, *ms, *vs)
    return [dict(zip(names, res[k * np_:(k + 1) * np_])) for k in range(4)], res[-1]


_WEIGHTS = ("attn_pre_norm", "w_in", "hgrn_lb", "hgrn_gnorm", "w_branch_a", "rwkv_mu", "rwkv_w0", "rwkv_w2",
            "rwkv_a0", "rwkv_a2", "rwkv_g2", "rwkv_k_k", "rwkv_k_a", "rwkv_r_k", "rwkv_ln_w", "rwkv_ln_b",
            "w_branch_b", "w_out", "attn_post_norm", "ffn_pre_norm", "w_up", "conv_w", "conv_b", "w_down",
            "ffn_post_norm")
_BIG = ("w_in", "w_up", "w_down", "w_branch_a", "w_branch_b", "w_out")


def _stages():
    one = [D]
    hw = HG_K * HG_PER_STEP
    rw = LANES * RW_PAIRS_PER_STEP
    return dict(
        mixers=_Stage("mixers", _f_mixers, 1, 2 * RW_CHUNK, [False] * 13, [[D] * 7 + [LANES, LANES]], [0],
                      [(hw, HG_K), (1, RW_COLS), (rw, LANES)], [one, one], [BF, BF],
                      kept_shapes=[(2 * RW_KEPT * RW_PAIRS_PER_STEP * 2 * RW_CHUNK, LANES)], f_kept=_f_mixers_kept),
        conv=_Stage("conv", _f_conv, 1, 512, [False, False], [[DFF], [DFF]], [0, 0], [(1, 2 * DFF), (1, 2 * DFF)],
                    [[DFF]], [BF]),
    )


def _cols_to_blocks(w, per):
    return w.reshape(w.shape[0], N_DEV, per).transpose(1, 0, 2)


def _blocks_to_cols(g):
    return g.transpose(1, 0, 2).reshape(g.shape[1], N_DEV * g.shape[2])


def kernel(x, attn_pre_norm, w_in, hgrn_lb, hgrn_gnorm, w_branch_a, rwkv_mu, rwkv_w0, rwkv_w2, rwkv_a0, rwkv_a2, rwkv_g2, rwkv_k_k, rwkv_k_a, rwkv_r_k, rwkv_ln_w, rwkv_ln_b, w_branch_b, w_out, attn_post_norm, ffn_pre_norm, w_up, conv_w, conv_b, w_down, ffn_post_norm, loss_target, m_attn_pre_norm, m_w_in, m_hgrn_lb, m_hgrn_gnorm, m_w_branch_a, m_rwkv_mu, m_rwkv_w0, m_rwkv_w2, m_rwkv_a0, m_rwkv_a2, m_rwkv_g2, m_rwkv_k_k, m_rwkv_k_a, m_rwkv_r_k, m_rwkv_ln_w, m_rwkv_ln_b, m_w_branch_b, m_w_out, m_attn_post_norm, m_ffn_pre_norm, m_w_up, m_conv_w, m_conv_b, m_w_down, m_ffn_post_norm, v_attn_pre_norm, v_w_in, v_hgrn_lb, v_hgrn_gnorm, v_w_branch_a, v_rwkv_mu, v_rwkv_w0, v_rwkv_w2, v_rwkv_a0, v_rwkv_a2, v_rwkv_g2, v_rwkv_k_k, v_rwkv_k_a, v_rwkv_r_k, v_rwkv_ln_w, v_rwkv_ln_b, v_w_branch_b, v_w_out, v_attn_post_norm, v_ffn_pre_norm, v_w_up, v_conv_w, v_conv_b, v_w_down, v_ffn_post_norm):
    w = dict(attn_pre_norm=attn_pre_norm, w_in=w_in, hgrn_lb=hgrn_lb, hgrn_gnorm=hgrn_gnorm, w_branch_a=w_branch_a, rwkv_mu=rwkv_mu, rwkv_w0=rwkv_w0, rwkv_w2=rwkv_w2, rwkv_a0=rwkv_a0, rwkv_a2=rwkv_a2, rwkv_g2=rwkv_g2, rwkv_k_k=rwkv_k_k, rwkv_k_a=rwkv_k_a, rwkv_r_k=rwkv_r_k, rwkv_ln_w=rwkv_ln_w, rwkv_ln_b=rwkv_ln_b, w_branch_b=w_branch_b, w_out=w_out, attn_post_norm=attn_post_norm, ffn_pre_norm=ffn_pre_norm, w_up=w_up, conv_w=conv_w, conv_b=conv_b, w_down=w_down, ffn_post_norm=ffn_post_norm)
    mo = dict(attn_pre_norm=m_attn_pre_norm, w_in=m_w_in, hgrn_lb=m_hgrn_lb, hgrn_gnorm=m_hgrn_gnorm, w_branch_a=m_w_branch_a, rwkv_mu=m_rwkv_mu, rwkv_w0=m_rwkv_w0, rwkv_w2=m_rwkv_w2, rwkv_a0=m_rwkv_a0, rwkv_a2=m_rwkv_a2, rwkv_g2=m_rwkv_g2, rwkv_k_k=m_rwkv_k_k, rwkv_k_a=m_rwkv_k_a, rwkv_r_k=m_rwkv_r_k, rwkv_ln_w=m_rwkv_ln_w, rwkv_ln_b=m_rwkv_ln_b, w_branch_b=m_w_branch_b, w_out=m_w_out, attn_post_norm=m_attn_post_norm, ffn_pre_norm=m_ffn_pre_norm, w_up=m_w_up, conv_w=m_conv_w, conv_b=m_conv_b, w_down=m_w_down, ffn_post_norm=m_ffn_post_norm)
    vo = dict(attn_pre_norm=v_attn_pre_norm, w_in=v_w_in, hgrn_lb=v_hgrn_lb, hgrn_gnorm=v_hgrn_gnorm, w_branch_a=v_w_branch_a, rwkv_mu=v_rwkv_mu, rwkv_w0=v_rwkv_w0, rwkv_w2=v_rwkv_w2, rwkv_a0=v_rwkv_a0, rwkv_a2=v_rwkv_a2, rwkv_g2=v_rwkv_g2, rwkv_k_k=v_rwkv_k_k, rwkv_k_a=v_rwkv_k_a, rwkv_r_k=v_rwkv_r_k, rwkv_ln_w=v_rwkv_ln_w, rwkv_ln_b=v_rwkv_ln_b, w_branch_b=v_w_branch_b, w_out=v_w_out, attn_post_norm=v_attn_post_norm, ffn_pre_norm=v_ffn_pre_norm, w_up=v_w_up, conv_w=v_conv_w, conv_b=v_conv_b, w_down=v_w_down, ffn_post_norm=v_ffn_post_norm)

    t = x.shape[1]
    x2 = x.reshape(t, D)
    tgt = loss_target.reshape(t, D)
    st = _stages()

    me = 4 * lax.axis_index("x") + 2 * lax.axis_index("y") + lax.axis_index("c")
    small = jnp.concatenate([rwkv_w2[0], rwkv_a2[0], rwkv_g2[0]], axis=0).astype(BF)
    g_in, g_small = _all_gather("gather_weights", [w_in[0].T.astype(BF), small])
    fw_in_t = g_in.reshape(IN_COLS, D)
    z64 = jnp.zeros((64, D), BF)
    w2p = jnp.concatenate([_blocks_to_cols(g_small[:, 0:64]), z64], axis=0)
    a2p = jnp.concatenate([z64, _blocks_to_cols(g_small[:, 64:128])], axis=0)
    g2f = _blocks_to_cols(g_small[:, 128:256])
    conv_bits = jnp.pad(lax.bitcast_convert_type(conv_w[0], BF).reshape(3, 2 * 704), ((0, 29), (0, 0)))
    late = [w_up[0].T.astype(BF)] + [w[k][0].astype(BF) for k in _BIG[2:]] + [conv_bits]
    late_gather = _Exchange("gather2", late)
    r_k = rwkv_r_k.reshape(1, D)

    xn, z = _norm_in_proj(x2, attn_pre_norm, fw_in_t, 512, 4736)
    mix_par = [hgrn_lb, hgrn_gnorm, rwkv_mu, rwkv_w0, w2p, rwkv_a0, a2p, g2f, rwkv_k_k, rwkv_k_a,
               rwkv_ln_w, rwkv_ln_b, r_k]
    mix_in = [z]
    (o_a, o_b), mix_saved = _stage_fwd(st["mixers"], t, mix_par, mix_in, hook=late_gather)
    gl = [lax.dynamic_update_slice(g, own[None], (me, 0, 0)) for g, own in zip(late_gather.results, late)]
    fw_up_t = gl[0].reshape(2 * DFF, D)
    fw_down = gl[1].reshape(DFF, D)
    fw_a, fw_b, fw_out = (g.reshape(D, D) for g in gl[2:5])
    conv_full = _blocks_to_cols(lax.bitcast_convert_type(gl[5][:, :3].reshape(N_DEV, 3, 704, 2), F32))
    y_a, y_b, merged, mix, h1, xn2 = _merge_out_post(z, o_a, o_b, fw_a, fw_b, fw_out, x2, attn_post_norm,
                                                     ffn_pre_norm, 512)
    conv_par = [conv_full, conv_b]
    hu_g, hu_v, act, before1, before2 = _up_conv(xn2, fw_up_t, conv_full, conv_b, 512, min(st["conv"].tm, t))
    conv_saved = [before1[None], before2[None]]

    loss_acc, d_ffn_post, dh1, dff = _down_loss(act, fw_down, ffn_post_norm, h1, tgt, 512)
    dw_down = _mm("dw_down", act, dff, "tn", BF, tm=1408, tn=512)
    (dcw, dcb), dhu = _stage_bwd(st["conv"], t, conv_par, [hu_g, hu_v], conv_saved, [], [BF, BF],
                                 dout_dot=(dff, fw_down))
    dw_up_t = _mm_cols_tn("dw_up", dhu, xn2, BF, 1408)
    d_post, d_pre2, dx_a, dmix = _dxn2_post1_bwd(dhu, fw_up_t, x2, mix, dh1, attn_post_norm, ffn_pre_norm, 512)
    dga, dgb, dy_a, dy_b, do_a, do_b = _dmerged_merge_bwd(dmix, fw_out, fw_a, fw_b, z, y_a, y_b, 512)
    dw_a, dw_b, dw_out = _mm_multi("dw_branches", [(o_a, dy_a), (o_b, dy_b), (merged, dmix)], "tn", BF)
    early = [dw_up_t.reshape(N_DEV, 704, D), dw_down.reshape(N_DEV, 352, D), dw_a.reshape(N_DEV, 128, D),
             dw_b.reshape(N_DEV, 128, D), dw_out.reshape(N_DEV, 128, D), _cols_to_blocks(dcw.astype(BF), 704)]
    early_scatter = _Exchange("scatter", early)
    mix_dp, dz_hr = _stage_bwd(st["mixers"], t, mix_par, mix_in, mix_saved, [[do_a], [do_b]], [BF],
                               hook=early_scatter)
    d_lb, d_gn, d_mu, d_w0, d_w2p, d_a0, d_a2p, d_g2, d_kk, d_ka, d_lnw, d_lnb, d_rk = mix_dp
    dz = dz_hr + [dga, dgb]
    dw_in_t = _mm_cols_tn("dw_in", dz, xn, BF, 256)

    ax, ay, ac = lax.axis_index("x"), lax.axis_index("y"), lax.axis_index("c")
    idx4 = jnp.stack([4 * cx + 2 * cy + ac for cx, cy in ((ax, ay), (1 - ax, ay), (ax, 1 - ay), (1 - ax, 1 - ay))])
    idx4 = idx4.astype(jnp.int32)
    idx_me, idx_0 = idx4[0:1], jnp.zeros((1,), jnp.int32)
    d_small = jnp.concatenate([d_w2p[:64], d_a2p[64:], d_g2], axis=0).astype(BF)
    g8s = [dw_in_t.reshape(N_DEV, 1184, D), _cols_to_blocks(d_small, LANES)]
    recv4s = _reduce_pair(g8s)
    sums = [_pair_sum("pair_sum_" + n, idx4, g, r) for n, g, r in zip(("w_in", "small"), g8s, recv4s)]
    swap_ssem, swap_rsem, swap_srcs, swap_lands, token = _chip_swap_start([s[1] for s in sums])
    d_pre1, dx = _dxn_pre1_bwd(dz, fw_in_t, x2, dx_a, attn_pre_norm, 256, token)
    grad_x = dx.reshape(x.shape)

    sh_out = [dict() for _ in range(4)]
    done = []
    for n, own, recv in zip(_BIG[1:] + ("conv_w",), early, early_scatter.results):
        tr = (lambda a: a.T) if n == "w_up" else (lambda a: a)
        res = _adam_sharded("adam_" + n, idx_me, own, recv, *[tr(src[n][0]) for src in (w, mo, vo)], after=token)
        done.append(res[0])
        for kind in range(4):
            sh_out[kind][n] = tr(res[kind])[None]

    rg = dict(attn_pre_norm=d_pre1, hgrn_lb=d_lb, hgrn_gnorm=d_gn, rwkv_mu=d_mu, rwkv_w0=d_w0, rwkv_a0=d_a0,
              rwkv_k_k=d_kk, rwkv_k_a=d_ka, rwkv_r_k=d_rk, rwkv_ln_w=d_lnw, rwkv_ln_b=d_lnb, attn_post_norm=d_post,
              ffn_pre_norm=d_pre2, conv_b=dcb, ffn_post_norm=d_ffn_post)
    g8 = _all_gather_small("gather_small_grads", _pack_replicated(rg, loss_acc, done))
    rnames = [n for n, _ in REPL]
    flat = lambda src: [src[n].reshape(1, D) if n == "rwkv_r_k" else src[n] for n in rnames]
    rp_out, loss_row = _adam_replicated(g8, flat(w), flat(mo), flat(vo))
    loss = loss_row[0, 0]
    recv3s = _chip_swap_wait(swap_ssem, swap_rsem, swap_srcs, swap_lands, rp_out[0]["attn_pre_norm"])
    for kind in range(4):
        rp_out[kind]["rwkv_r_k"] = rp_out[kind]["rwkv_r_k"].reshape(rwkv_r_k.shape)

    def small_of(src):
        return jnp.concatenate([src["rwkv_w2"][0], src["rwkv_a2"][0], src["rwkv_g2"][0]], axis=0)

    res = _adam_sharded("adam_w_in", idx_0, sums[0][0][None], recv3s[0], *[src["w_in"][0].T for src in (w, mo, vo)])
    res_s = _adam_sharded("adam_small", idx_0, sums[1][0][None], recv3s[1], *[small_of(src) for src in (w, mo, vo)])
    for kind in range(4):
        sh_out[kind]["w_in"] = res[kind].T[None]
        sh_out[kind]["rwkv_w2"] = res_s[kind][0:64][None]
        sh_out[kind]["rwkv_a2"] = res_s[kind][64:128][None]
        sh_out[kind]["rwkv_g2"] = res_s[kind][128:256][None]

    outs = [loss, grad_x]
    for kind in range(4):
        for name in _WEIGHTS:
            outs.append(sh_out[kind][name] if name in sh_out[kind] else rp_out[kind][name])
    return tuple(outs)
```

```python
import functools

import jax
import jax.numpy as jnp
from jax import lax
from jax.experimental import pallas as pl
from jax.experimental.pallas import tpu as pltpu

F32 = jnp.float32
BF = jnp.bfloat16
MESH = pl.DeviceIdType.MESH

D = 1024
HG_HEADS = 8
HG_K = 128
HG_CHUNK = 32
HG_SCALE = HG_K ** -0.5
HG_PER_STEP = 8
RW_HEADS = 16
RW_N = 64
RW_CHUNK = 64
RW_PAIRS_PER_STEP = 8
DFF = 2816
IN_COLS = 9472
RW_COLS = 3328
EPS = 1e-6
GN_EPS = 1e-5 * RW_N
ADAM_LR = 0.001
ADAM_B1 = 0.9
ADAM_B2 = 0.999
ADAM_EPS = 1e-08
ADAM_WD = 0.01
ADAM_STEP = 10
N_DEV = 8
LANES = 128
SUBLANES = 8
VMEM_LIMIT = 56 * 1024 * 1024
TILE_BYTES = 1280 * 1024

REPL = (("attn_pre_norm", 1024), ("hgrn_lb", 1024), ("hgrn_gnorm", 1024), ("rwkv_mu", 3328), ("rwkv_w0", 1024),
        ("rwkv_a0", 1024), ("rwkv_k_k", 1024), ("rwkv_k_a", 1024), ("rwkv_r_k", 1024), ("rwkv_ln_w", 1024),
        ("rwkv_ln_b", 1024), ("attn_post_norm", 1024), ("ffn_pre_norm", 1024), ("conv_b", 5632), ("ffn_post_norm", 1024))
REPL_ROWS = {"hgrn_lb": 2}
REPL_TOTAL = 32


def _cparams(sem=None, **kw):
    return pltpu.CompilerParams(dimension_semantics=sem, vmem_limit_bytes=VMEM_LIMIT, **kw)


_DN = {"nn": ((1,), (0,)), "nt": ((1,), (1,)), "tn": ((0,), (0,))}


def _raw_dot(a, b, mode):
    return lax.dot_general(a.astype(BF), b.astype(BF), (_DN[mode], ((), ())), preferred_element_type=F32)


@functools.partial(jax.custom_vjp, nondiff_argnums=(2,))
def _dot(a, b, mode):
    return _raw_dot(a, b, mode)


def _dot_fwd(a, b, mode):
    return _raw_dot(a, b, mode), (a, b)


def _dot_bwd(mode, res, g):
    a, b = res
    if mode == "nn":
        return _dot(g, b, "nt"), _dot(a, g, "tn")
    if mode == "nt":
        return _dot(g, b, "nn"), _dot(g, a, "tn")
    return _dot(b, g, "nt"), _dot(a, g, "nn")


_dot.defvjp(_dot_fwd, _dot_bwd)


def _bf_pieces(x, n):
    out, r = [], x
    for i in range(n):
        p = r.astype(BF)
        out.append(p)
        if i + 1 < n:
            r = r - p.astype(F32)
    return out


def _raw_split_dot(x, e, mode, n, x_left):
    eb = e.astype(BF)
    acc = None
    for p in _bf_pieces(x, n):
        ops = (p, eb) if x_left else (eb, p)
        t = lax.dot_general(*ops, (_DN[mode], ((), ())), preferred_element_type=F32)
        acc = t if acc is None else acc + t
    return acc


def _raw_headsum(x):
    t = x.shape[0]
    i = lax.broadcasted_iota(jnp.int32, (LANES, LANES), 0)
    j = lax.broadcasted_iota(jnp.int32, (LANES, LANES), 1)
    same = jnp.where((i >= RW_N) == (j >= RW_N), 1.0, 0.0).astype(F32)
    groups = x.shape[1] // LANES
    rows = jnp.concatenate([x[:, q * LANES:(q + 1) * LANES] for q in range(groups)], axis=0)
    s = _raw_split_dot(rows, same, "nn", 2, True)
    return jnp.concatenate([s[q * t:(q + 1) * t] for q in range(groups)], axis=1)


@jax.custom_vjp
def _headsum(x):
    return _raw_headsum(x)


def _headsum_fwd(x):
    return _raw_headsum(x), None


def _headsum_bwd(_, g):
    return (_raw_headsum(g),)


_headsum.defvjp(_headsum_fwd, _headsum_bwd)


@functools.partial(jax.custom_vjp, nondiff_argnums=(2,))
def _tdot(tri, x, n):
    return _raw_split_dot(x, tri, "nn", n, False)


def _tdot_fwd(tri, x, n):
    return _raw_split_dot(x, tri, "nn", n, False), tri


def _tdot_bwd(n, tri, g):
    return jnp.zeros_like(tri), _raw_split_dot(g, tri, "tn", n, False)


_tdot.defvjp(_tdot_fwd, _tdot_bwd)


def _row(x, i):
    r = lax.broadcasted_iota(jnp.int32, x.shape, 0)
    return jnp.sum(jnp.where(r == i, x, 0.0), axis=0, keepdims=True)


def _shift_down(x, prev):
    t = x.shape[0]

    @jax.custom_vjp
    def sh(x, prev):
        r = lax.broadcasted_iota(jnp.int32, x.shape, 0)
        return jnp.where(r == 0, prev, pltpu.roll(x, 1, 0))

    def fwd(x, prev):
        return sh(x, prev), None

    def bwd(_, g):
        r = lax.broadcasted_iota(jnp.int32, g.shape, 0)
        dx = jnp.where(r == t - 1, 0.0, pltpu.roll(g, t - 1, 0))
        return dx, _row(g[0:min(t, SUBLANES)], 0)

    sh.defvjp(fwd, bwd)
    return sh(x, prev)


def _sigmoid(x):
    return jax.nn.sigmoid(x)


def _silu(x):
    return x * jax.nn.sigmoid(x)


def _softplus(x):
    return jnp.maximum(x, 0.0) + jnp.log(1.0 + jnp.exp(-jnp.abs(x)))


def _rms(x, g):
    return (x * lax.rsqrt(jnp.mean(x * x, axis=-1, keepdims=True) + EPS)) * g


def _tril(c):
    r = lax.broadcasted_iota(jnp.int32, (c, c), 0)
    cc = lax.broadcasted_iota(jnp.int32, (c, c), 1)
    return cc <= r


def _f_pre1_residual(ps, xs, cs):
    return [_rms(xs[0], ps[0]), xs[0]], []


def _f_hgrn(ps, xs, cs):
    lbraw, gn = ps
    hq, hf, hi, hg = xs
    hd = range(HG_PER_STEP)
    st = [cs[0][p * HG_K:(p + 1) * HG_K] for p in hd]
    l0, l1 = _row(lbraw, 0), _row(lbraw, 1)
    m = jnp.maximum(l0, l1)
    e0, e1 = jnp.exp(l0 - m), jnp.exp(l1 - m)
    lb = e0 / (e0 + e1)
    q = _silu(hq) * HG_SCALE
    f = lb + (1.0 - lb) * _sigmoid(hf)
    kh = 1.0 - f
    gl = jnp.log(f)
    c = HG_CHUNK
    low = _tril(c)
    tri = jnp.where(low, 1.0, 0.0).astype(F32)
    outs = []
    for i in range(hq.shape[0] // c):
        rows = slice(i * c, (i + 1) * c)
        b = _tdot(tri, gl[rows], 3)
        bref = _row(b, c // 2 - 1)
        blast = _row(b, c - 1)
        qi = q[rows] * jnp.exp(b - bref)
        ki = kh[rows] * jnp.exp(bref - b)
        qd = q[rows] * jnp.exp(b)
        kd = kh[rows] * jnp.exp(blast - b)
        dec = jnp.exp(blast)
        sl = [slice(p * HG_K, (p + 1) * HG_K) for p in hd]
        sc = [jnp.where(low, _dot(qi[:, sl[p]], ki[:, sl[p]], "nt"), 0.0) for p in hd]
        o = [_dot(sc[p], hi[rows, sl[p]], "nn") + _dot(qd[:, sl[p]], st[p], "nt") for p in hd]
        u = [_dot(hi[rows, sl[p]], kd[:, sl[p]], "tn") for p in hd]
        st = [dec[:, sl[p]] * st[p] + u[p] for p in hd]
        outs.append(jnp.concatenate(o, axis=1) if len(o) > 1 else o[0])
    o = outs[0] if len(outs) == 1 else jnp.concatenate(outs, axis=0)
    on = []
    for p in hd:
        op = o[:, p * HG_K:(p + 1) * HG_K]
        on.append(op * lax.rsqrt(jnp.mean(op * op, axis=-1, keepdims=True) + EPS))
    o = jnp.concatenate(on, axis=1) if len(on) > 1 else on[0]
    o = o * gn
    return [o * _silu(hg)], [jnp.concatenate(st, axis=0) if len(st) > 1 else st[0]]


_RW_OFFS = (0, 1024, 2048, 3072, 3200, 3328)


def _f_rwpre(ps, xs, cs):
    mu, w0, w2p, a0, a2p, g2, k_k, k_a = ps
    (prev,) = cs
    t = xs[0].shape[0]
    zs = []
    for i, z in enumerate(xs):
        lo, hi = _RW_OFFS[i], _RW_OFFS[i + 1]
        zs.append(z + mu[:, lo:hi] * (_shift_down(z, prev[:, lo:hi]) - z))
    rr, kr, vr, wa, gz = zs
    w_log = -_softplus(-(w0 + _dot(jnp.tanh(wa), w2p, "nn"))) - 0.5
    lw = -jnp.exp(w_log)
    a = _sigmoid(a0 + _dot(wa, a2p, "nn"))
    g = _dot(_sigmoid(gz), g2, "nn")
    kkr = kr * k_k
    kk = kkr / jnp.maximum(jnp.sqrt(_headsum(kkr * kkr)), 1e-12)
    k2 = kr * (1.0 + (a - 1.0) * k_a)
    newprev = jnp.concatenate([_row(z, t - 1) for z in xs], axis=1)
    return [rr, lw, k2, vr, -kk, kk * a, g], [newprev]


def _raw_inverses(ls):
    n = ls[0].shape[0]
    r = lax.broadcasted_iota(jnp.int32, (n, n), 0)
    c = lax.broadcasted_iota(jnp.int32, (n, n), 1)
    eye = jnp.where(r == c, 1.0, 0.0).astype(F32)
    tinv = [eye + l for l in ls]
    pw = ls
    for _ in range(5):
        pw = [_raw_dot(p, p, "nn") for p in pw]
        tinv = [t + _raw_dot(t, p, "nn") for t, p in zip(tinv, pw)]
    return tinv


@jax.custom_vjp
def _unit_lower_inverses(ls):
    return _raw_inverses(ls)


def _inverses_fwd(ls):
    tinv = _raw_inverses(ls)
    return tinv, tinv


def _inverses_bwd(tinv, gs):
    return ([_raw_dot(_raw_dot(t, g, "tn"), t, "nt") for t, g in zip(tinv, gs)],)


_unit_lower_inverses.defvjp(_inverses_fwd, _inverses_bwd)


@jax.custom_vjp
def _known_inverses(ls, tinv):
    return tinv


def _known_fwd(ls, tinv):
    return tinv, tinv


def _known_bwd(tinv, gs):
    return [_raw_dot(_raw_dot(t, g, "tn"), t, "nt") for t, g in zip(tinv, gs)], [jnp.zeros_like(t) for t in tinv]


_known_inverses.defvjp(_known_fwd, _known_bwd)


@jax.custom_vjp
def _use_kept(computed, kept):
    return kept


def _use_kept_fwd(computed, kept):
    return kept, None


def _use_kept_bwd(_, g):
    return g, jax.tree.map(jnp.zeros_like, g)


_use_kept.defvjp(_use_kept_fwd, _use_kept_bwd)

RW_KEPT = 5


def _f_rwscan(ps, xs, cs, kept=None):
    state = cs[0]
    ys, keep = [], []
    n = 2 * RW_CHUNK
    per_chunk = RW_KEPT * RW_PAIRS_PER_STEP * n
    for i in range(xs[0].shape[0] // RW_CHUNK):
        known = None
        if kept is not None:
            known = [[kept[i * per_chunk + (q * RW_PAIRS_PER_STEP + p) * n:
                           i * per_chunk + (q * RW_PAIRS_PER_STEP + p + 1) * n] for p in range(RW_PAIRS_PER_STEP)]
                     for q in range(RW_KEPT)]
        y, state, mats = _rwkv_chunk([x[i * RW_CHUNK:(i + 1) * RW_CHUNK] for x in xs], state, known)
        ys.append(y)
        keep += [m for group in mats for m in group]
    return [ys[0] if len(ys) == 1 else jnp.concatenate(ys, axis=0)], [state], jnp.concatenate(keep, axis=0)


def _rwkv_chunk(xs, state, known=None):
    npair = RW_PAIRS_PER_STEP
    pr = range(npair)
    r, lw, k, v, av, bv = [[x[:, p * LANES:(p + 1) * LANES] for p in pr] for x in xs]
    sv = [state[p * LANES:(p + 1) * LANES] for p in pr]
    c = RW_CHUNK
    n = 2 * c
    tri = jnp.where(_tril(c), 1.0, 0.0).astype(F32)
    cl = [_tdot(tri, lw[p], 3) for p in pr]
    cl_last = [_row(cl[p], c - 1) for p in pr]
    lane = lax.broadcasted_iota(jnp.int32, (c, LANES), 1)
    h0 = lane < RW_N

    def stack(x):
        return jnp.concatenate([jnp.where(h0, x, 0.0), jnp.where(h0, 0.0, x)], axis=0)

    am = [stack(av[p] * jnp.exp(cl[p] - lw[p])) for p in pr]
    bm = [stack(bv[p] * jnp.exp(-cl[p])) for p in pr]
    km = [stack(k[p] * jnp.exp(-cl[p])) for p in pr]
    rm = [stack(r[p] * jnp.exp(cl[p])) for p in pr]
    vm = [stack(v[p]) for p in pr]
    rn = lax.broadcasted_iota(jnp.int32, (n, n), 0)
    cn = lax.broadcasted_iota(jnp.int32, (n, n), 1)
    blk = (rn >= c) == (cn >= c)
    strict = blk & (cn < rn)
    incl = blk & (cn <= rn)
    lab = [jnp.where(strict, _dot(am[p], bm[p], "nt"), 0.0) for p in pr]
    lak = [jnp.where(strict, _dot(am[p], km[p], "nt"), 0.0) for p in pr]
    wrb = [jnp.where(incl, _dot(rm[p], bm[p], "nt"), 0.0) for p in pr]
    wrk = [jnp.where(incl, _dot(rm[p], km[p], "nt"), 0.0) for p in pr]
    if known is None:
        tinv = _unit_lower_inverses(lab)
    else:
        tinv = _known_inverses(lab, known[0])
        lak, wrb, wrk = _use_kept(lak, known[1]), _use_kept(wrb, known[2]), _use_kept(wrk, known[3])
    rhs = [_dot(am[p], sv[p], "nt") + _dot(lak[p], vm[p], "nn") for p in pr]
    um = [_dot(tinv[p], rhs[p], "nn") for p in pr]
    if known is not None:
        um = _use_kept(um, known[4])
    ym = [_dot(rm[p], sv[p], "nt") + _dot(wrb[p], um[p], "nn") + _dot(wrk[p], vm[p], "nn") for p in pr]
    sn = [(sv[p] + _dot(um[p], bm[p], "tn") + _dot(vm[p], km[p], "tn")) * jnp.exp(cl_last[p]) for p in pr]
    ys = [ym[p][:c] + ym[p][c:] for p in pr]
    return jnp.concatenate(ys, axis=1), jnp.concatenate(sn, axis=0), [tinv, lak, wrb, wrk, um]


def _f_mixers(ps, xs, cs):
    return _mixers(ps, xs, cs, None)


def _f_mixers_kept(ps, xs, cs, kept):
    return _mixers(ps, xs, cs, kept[0])[:2]


def _mixers(ps, xs, cs, kept):
    oa, st = _f_hgrn(ps[:2], xs[:4], cs[:1])
    (r, lw, k, v, av, bv, g), prev = _f_rwpre(ps[2:10], xs[4:], cs[1:2])
    y, sv, keep = _f_rwscan([], [r, lw, k, v, av, bv], cs[2:], kept)
    ob, _ = _f_rwpost(ps[10:], y + [r, k, v, g], [])
    return oa + ob, st + prev + sv, [keep]


def _f_rwpost(ps, xs, cs):
    ln_w, ln_b, r_k = ps
    y, r, k, v, g = xs
    inv_n = 1.0 / RW_N
    yc = y - _headsum(y) * inv_n
    var = _headsum(yc * yc) * inv_n
    yn = yc * lax.rsqrt(var + GN_EPS)
    yn = yn * ln_w + ln_b
    bonus = _headsum(r * k * r_k) * v
    return [(yn + bonus) * g], []


def _f_merge(ps, xs, cs):
    ga, gb, ya, yb = xs
    return [_sigmoid(ga) * ya + _sigmoid(gb) * yb], []


def _f_post1(ps, xs, cs):
    x, mix = xs
    h1 = x + _rms(mix, ps[0])
    return [h1, _rms(h1, ps[1])], []


def _f_conv(ps, xs, cs):
    cw, cb = ps
    p1, p2 = cs
    w0, w1, w2 = _row(cw, 0), _row(cw, 1), _row(cw, 2)
    t = xs[0].shape[0]
    hc = []
    for i, x in enumerate(xs):
        sl = slice(i * DFF, (i + 1) * DFF)
        s1 = _shift_down(x, p1[:, sl])
        s2 = _shift_down(s1, p2[:, sl])
        hc.append(cb[:, sl] + w0[:, sl] * s2 + w1[:, sl] * s1 + w2[:, sl] * x)
    tails = [x[t - SUBLANES:t] for x in xs]
    n1 = jnp.concatenate([_row(x, SUBLANES - 1) for x in tails], axis=1)
    n2 = jnp.concatenate([_row(x, SUBLANES - 2) for x in tails], axis=1)
    return [_silu(hc[0]) * hc[1]], [n1, n2]


class _Stage:
    def __init__(self, name, f, g, tm, par_per_g, in_pieces, in_offs, carry_shapes, out_pieces, out_dtypes,
                 kept_shapes=(), f_kept=None):
        self.name, self.f, self.g, self.tm = name, f, g, tm
        self.par_per_g, self.in_pieces, self.in_offs = par_per_g, in_pieces, in_offs
        self.carry_shapes, self.out_pieces, self.out_dtypes = carry_shapes, out_pieces, out_dtypes
        self.kept_shapes, self.f_kept = list(kept_shapes), f_kept


def _par_spec(arr, per_g, g):
    r, c = arr.shape
    if per_g:
        return pl.BlockSpec((r, c // g), lambda gi, ni: (0, gi))
    return pl.BlockSpec((r, c), lambda gi, ni: (0, 0))


def _row_spec(tm, width, off, n, rev):
    if rev:
        return pl.BlockSpec((tm, width), lambda gi, ni: (n - 1 - ni, off + gi))
    return pl.BlockSpec((tm, width), lambda gi, ni: (ni, off + gi))


def _carry_spec(shape, n, rev):
    if rev:
        return pl.BlockSpec((None, None) + shape, lambda gi, ni: (gi, n - 1 - ni, 0, 0))
    return pl.BlockSpec((None, None) + shape, lambda gi, ni: (gi, ni, 0, 0))


def _load_pieces(refs, pieces_list):
    out = []
    for ref, pieces in zip(refs, pieces_list):
        o = 0
        for w in pieces:
            out.append(ref[:, o:o + w].astype(F32))
            o += w
    return out


def _store_pieces(refs, pieces_list, vals):
    k = 0
    for ref, pieces in zip(refs, pieces_list):
        o = 0
        for w in pieces:
            ref[:, o:o + w] = vals[k].astype(ref.dtype)
            k += 1
            o += w


_ANY = pl.BlockSpec(memory_space=pl.ANY)


class _Exchange:
    def __init__(self, kind, arrs):
        self.kind, self.arrs, self.results = kind, list(arrs), None
        if kind == "scatter":
            self.out_shape = [jax.ShapeDtypeStruct((N_DEV - 1,) + a.shape[1:], a.dtype) for a in self.arrs]
        else:
            self.out_shape = [jax.ShapeDtypeStruct((N_DEV,) + a.shape, a.dtype) for a in self.arrs]
        self.nsem = (N_DEV if kind == "gather2" else N_DEV - 1) * len(self.arrs)

    def copies(self, in_refs, out_refs, ssem, rsem):
        x, y, c = lax.axis_index("x"), lax.axis_index("y"), lax.axis_index("c")
        me = 4 * x + 2 * y + c
        cps = []
        for a, (i_ref, o_ref) in enumerate(zip(in_refs, out_refs)):
            for j in range(1, N_DEV):
                px = 1 - x if j & 4 else x
                py = 1 - y if j & 2 else y
                pc = 1 - c if j & 1 else c
                if self.kind == "gather":
                    src, dst = i_ref, o_ref.at[me]
                else:
                    src, dst = i_ref.at[4 * px + 2 * py + pc], o_ref.at[j - 1]
                s = (N_DEV - 1) * a + j - 1
                cps.append(pltpu.make_async_remote_copy(src_ref=src, dst_ref=dst, send_sem=ssem.at[s],
                                                        recv_sem=rsem.at[s], device_id=(px, py, pc),
                                                        device_id_type=MESH))
        return cps

    def run(self, step, total, in_refs, out_refs, ssem, rsem):
        if self.kind == "gather2":
            return self.run_two_level(step, total, in_refs, out_refs, ssem, rsem)

        @pl.when(step == 0)
        def _():
            for cp in self.copies(in_refs, out_refs, ssem, rsem):
                cp.start()

        @pl.when(step == total - 1)
        def _():
            for cp in self.copies(in_refs, out_refs, ssem, rsem):
                cp.wait()

    def run_two_level(self, step, total, in_refs, out_refs, ssem, rsem):
        x, y, c = lax.axis_index("x"), lax.axis_index("y"), lax.axis_index("c")
        sibling, xn, yn = (x, y, 1 - c), (1 - x, y, c), (x, 1 - y, c)
        arrs = range(len(in_refs))
        ns = N_DEV

        def num(px, py, pc):
            return 4 * px + 2 * py + pc

        def copy(a, k, to, src, dst):
            return pltpu.make_async_remote_copy(src_ref=src, dst_ref=dst, send_sem=ssem.at[ns * a + k],
                                                recv_sem=rsem.at[ns * a + k], device_id=to, device_id_type=MESH)

        def blk(a, b):
            return out_refs[a].at[b]

        def half(a, b, second):
            h = self.arrs[a].shape[0] // 2
            return out_refs[a].at[b, pl.ds(h if second else 0, h)]

        bx, by, bd = num(1 - x, y, c), num(x, 1 - y, c), num(1 - x, 1 - y, c)

        def firsts(a):
            own = blk(a, num(x, y, c))
            return [copy(a, 0, sibling, in_refs[a], own), copy(a, 1, xn, in_refs[a], own),
                    copy(a, 2, yn, in_refs[a], own)]

        def seconds(a):
            return [copy(a, 3, yn, half(a, bx, False), half(a, bx, False)), copy(a, 5, sibling, blk(a, bx), blk(a, bx)),
                    copy(a, 4, xn, half(a, by, True), half(a, by, True)), copy(a, 6, sibling, blk(a, by), blk(a, by))]

        def third(a):
            return copy(a, 7, sibling, blk(a, bd), blk(a, bd))

        @pl.when(step == 0)
        def _():
            for a in arrs:
                for cp in firsts(a):
                    cp.start()

        @pl.when(step == total // 2)
        def _():
            for a in arrs:
                copy(a, 1, xn, blk(a, bx), blk(a, bx)).wait_recv()
                copy(a, 2, yn, blk(a, by), blk(a, by)).wait_recv()
                for cp in seconds(a):
                    cp.start()

        @pl.when(step == (4 * total) // 5)
        def _():
            for a in arrs:
                copy(a, 3, yn, half(a, bd, False), half(a, bd, False)).wait_recv()
                copy(a, 4, xn, half(a, bd, True), half(a, bd, True)).wait_recv()
                third(a).start()

        @pl.when(step == total - 1)
        def _():
            for a in arrs:
                for k, b in ((0, num(x, y, 1 - c)), (5, num(1 - x, y, 1 - c)), (6, num(x, 1 - y, 1 - c)),
                             (7, num(1 - x, 1 - y, 1 - c))):
                    copy(a, k, sibling, blk(a, b), blk(a, b)).wait_recv()
                for cp in firsts(a) + seconds(a) + [third(a)]:
                    cp.wait_send()


def _hook_specs(hook):
    if hook is None:
        return [], [], [], []
    na = len(hook.arrs)
    sems = [pltpu.SemaphoreType.DMA((hook.nsem,)), pltpu.SemaphoreType.DMA((hook.nsem,))]
    return [_ANY] * na, [_ANY] * na, hook.out_shape, sems


def _stage_fwd(st, t, params, inputs, hook=None):
    g, tm = st.g, min(st.tm, t)
    n = t // tm
    npar, nin, ncar, nout = len(params), len(inputs), len(st.carry_shapes), len(st.out_pieces)
    nk = len(st.kept_shapes)
    h_in, h_out, h_shape, h_sems = _hook_specs(hook)
    nh = len(h_in)

    def body(*refs):
        p_refs = refs[:npar]
        x_refs = refs[npar:npar + nin]
        hi_refs = refs[npar + nin:npar + nin + nh]
        o = npar + nin + nh
        o_refs = refs[o:o + nout]
        s_refs = refs[o + nout:o + nout + ncar]
        k_refs = refs[o + nout + ncar:o + nout + ncar + nk]
        o += nout + ncar + nk
        ho_refs = refs[o:o + nh]
        c_scr = refs[o + nh:o + nh + ncar]
        gi, ni = pl.program_id(0), pl.program_id(1)
        if hook is not None:
            step = gi * n + ni
            hook.run(step, g * n, hi_refs, ho_refs, *refs[-2:])

        @pl.when(ni == 0)
        def _():
            for c in c_scr:
                c[...] = jnp.zeros(c.shape, F32)

        ps = [r[...].astype(F32) for r in p_refs]
        xs = _load_pieces(x_refs, st.in_pieces)
        cs = [c[...] for c in c_scr]
        for s, c in zip(s_refs, cs):
            s[...] = c
        res = st.f(ps, xs, cs)
        outs, ncs = res[0], res[1]
        _store_pieces(o_refs, st.out_pieces, outs)
        for c, v in zip(c_scr, ncs):
            c[...] = v
        for kr, kv in zip(k_refs, res[2] if nk else []):
            kr[...] = kv.astype(kr.dtype)

    in_specs = [_par_spec(p, pg, g) for p, pg in zip(params, st.par_per_g)]
    in_specs += [_row_spec(tm, sum(pc), off, n, False) for pc, off in zip(st.in_pieces, st.in_offs)]
    out_specs = [_row_spec(tm, sum(pc), 0, n, False) for pc in st.out_pieces]
    out_specs += [_carry_spec(s, n, False) for s in st.carry_shapes]
    out_specs += [pl.BlockSpec(s, lambda gi, ni: (ni, 0)) for s in st.kept_shapes]
    out_shape = [jax.ShapeDtypeStruct((t, g * sum(pc)), dt) for pc, dt in zip(st.out_pieces, st.out_dtypes)]
    out_shape += [jax.ShapeDtypeStruct((g, n) + s, F32) for s in st.carry_shapes]
    out_shape += [jax.ShapeDtypeStruct((n * s[0], s[1]), BF) for s in st.kept_shapes]
    res = pl.pallas_call(
        body, name=st.name + "_fwd", grid=(g, n), in_specs=in_specs + h_in, out_specs=out_specs + h_out,
        out_shape=out_shape + h_shape,
        scratch_shapes=[pltpu.VMEM(s, F32) for s in st.carry_shapes] + h_sems,
        compiler_params=_cparams(("arbitrary", "arbitrary")),
    )(*params, *inputs, *(hook.arrs if hook else []))
    if hook is not None:
        hook.results = list(res[nout + ncar + nk:])
    return list(res[:nout]), list(res[nout:nout + ncar + nk])


def _stage_bwd(st, t, params, inputs, saved, douts, dx_dtypes, hook=None, dout_dot=None):
    g, tm = st.g, min(st.tm, t)
    n = t // tm
    npar, nin, ncar = len(params), len(inputs), len(st.carry_shapes)
    nk = len(st.kept_shapes)
    flat_d = list(dout_dot) if dout_dot is not None else [d for ds in douts for d in ds]
    nd = len(flat_d)
    dx_idx = [i for i, dt in enumerate(dx_dtypes) if dt is not None]
    h_in, h_out, h_shape, h_sems = _hook_specs(hook)
    nh = len(h_in)

    def body(*refs):
        p_refs = refs[:npar]
        x_refs = refs[npar:npar + nin]
        s_refs = refs[npar + nin:npar + nin + ncar]
        k_refs = refs[npar + nin + ncar:npar + nin + ncar + nk]
        o = npar + nin + ncar + nk
        d_refs = refs[o:o + nd]
        hi_refs = refs[o + nd:o + nd + nh]
        o += nd + nh
        dp_refs = refs[o:o + npar]
        dx_refs = refs[o + npar:o + npar + len(dx_idx)]
        ho_refs = refs[o + npar + len(dx_idx):o + npar + len(dx_idx) + nh]
        dc_scr = refs[o + npar + len(dx_idx) + nh:o + npar + len(dx_idx) + nh + ncar]
        gi, ni = pl.program_id(0), pl.program_id(1)
        if hook is not None:
            step = gi * n + ni
            hook.run(step, g * n, hi_refs, ho_refs, *refs[-2:])

        @pl.when(ni == 0)
        def _():
            for c in dc_scr:
                c[...] = jnp.zeros(c.shape, F32)

        ps = [r[...].astype(F32) for r in p_refs]
        xs = _load_pieces(x_refs, st.in_pieces)
        cs = [s[...] for s in s_refs]
        dys = [_raw_dot(d_refs[0][...], d_refs[1][...], "nt")] if dout_dot is not None else []
        k = 0
        for ds, pieces in zip(douts, st.out_pieces):
            acc = _load_pieces([d_refs[k]], [pieces])
            for j in range(1, len(ds)):
                more = _load_pieces([d_refs[k + j]], [pieces])
                acc = [a + b for a, b in zip(acc, more)]
            dys += acc
            k += len(ds)
        if nk:
            kept = [r[...].astype(F32) for r in k_refs]
            _, vjp = jax.vjp(lambda p, x, c: st.f_kept(p, x, c, kept), ps, xs, cs)
        else:
            _, vjp = jax.vjp(st.f, ps, xs, cs)
        dps, dxs, dcs = vjp((dys, [c[...] for c in dc_scr]))
        k = 0
        per_in = []
        for pieces in st.in_pieces:
            per_in.append(dxs[k:k + len(pieces)])
            k += len(pieces)
        for ref, i in zip(dx_refs, dx_idx):
            _store_pieces([ref], [st.in_pieces[i]], per_in[i])
        for c, v in zip(dc_scr, dcs):
            c[...] = v
        for ref, dp, pg in zip(dp_refs, dps, st.par_per_g):
            first = (ni == 0) if pg else ((ni == 0) & (gi == 0))

            @pl.when(first)
            def _():
                ref[...] = jnp.zeros(ref.shape, F32)

            ref[...] += dp

    in_specs = [_par_spec(p, pg, g) for p, pg in zip(params, st.par_per_g)]
    in_specs += [_row_spec(tm, sum(pc), off, n, True) for pc, off in zip(st.in_pieces, st.in_offs)]
    in_specs += [_carry_spec(s, n, True) for s in st.carry_shapes]
    in_specs += [pl.BlockSpec(s, lambda gi, ni: (n - 1 - ni, 0)) for s in st.kept_shapes]
    for ds, pc in zip(douts, st.out_pieces):
        in_specs += [_row_spec(tm, sum(pc), 0, n, True) for _ in ds]
    if dout_dot is not None:
        a, w = dout_dot
        in_specs += [pl.BlockSpec((tm, a.shape[1]), lambda gi, ni: (n - 1 - ni, 0)),
                     pl.BlockSpec(w.shape, lambda gi, ni: (0, 0), pipeline_mode=pl.Buffered(1))]
    out_specs = [_par_spec(p, pg, g) for p, pg in zip(params, st.par_per_g)]
    out_specs += [_row_spec(tm, sum(st.in_pieces[i]), 0, n, True) for i in dx_idx]
    out_shape = [jax.ShapeDtypeStruct(p.shape, F32) for p in params]
    out_shape += [jax.ShapeDtypeStruct((t, g * sum(st.in_pieces[i])), dx_dtypes[i]) for i in dx_idx]
    res = pl.pallas_call(
        body, name=st.name + "_bwd", grid=(g, n), in_specs=in_specs + h_in, out_specs=out_specs + h_out,
        out_shape=out_shape + h_shape,
        scratch_shapes=[pltpu.VMEM(s, F32) for s in st.carry_shapes] + h_sems,
        compiler_params=_cparams(("arbitrary", "arbitrary")),
    )(*params, *inputs, *saved, *flat_d, *(hook.arrs if hook else []))
    if hook is not None:
        hook.results = list(res[npar + len(dx_idx):])
    return list(res[:npar]), list(res[npar:npar + len(dx_idx)])


def _pick(n, cap):
    if n <= cap:
        return n
    best = LANES
    for k in range(1, n // LANES + 1):
        if (n // LANES) % k == 0 and k * LANES <= cap:
            best = k * LANES
    return best


def _mm(name, a, b, mode, out_dtype=F32, tm=1024, tn=512, b_outer=False):
    m = a.shape[1] if mode == "tn" else a.shape[0]
    k = a.shape[0] if mode == "tn" else a.shape[1]
    n = b.shape[0] if mode == "nt" else b.shape[1]
    tm, tn = _pick(m, tm), _pick(n, tn)
    if b_outer:
        grid = (n // tn, m // tm)
        ij = lambda p, q: (q, p)
    else:
        grid = (m // tm, n // tn)
        ij = lambda p, q: (p, q)

    def body(a_ref, b_ref, o_ref):
        o_ref[...] = _raw_dot(a_ref[...], b_ref[...], mode).astype(o_ref.dtype)

    if mode == "tn":
        a_spec = pl.BlockSpec((k, tm), lambda p, q: (0, ij(p, q)[0]))
    else:
        a_spec = pl.BlockSpec((tm, k), lambda p, q: (ij(p, q)[0], 0))
    b_mode = dict(pipeline_mode=pl.Buffered(1)) if tn == n else {}
    if mode == "nt":
        b_spec = pl.BlockSpec((tn, k), lambda p, q: (ij(p, q)[1], 0), **b_mode)
    else:
        b_spec = pl.BlockSpec((k, tn), lambda p, q: (0, ij(p, q)[1]), **b_mode)
    return pl.pallas_call(
        body, name=name, grid=grid, in_specs=[a_spec, b_spec],
        out_specs=pl.BlockSpec((tm, tn), lambda p, q: ij(p, q)),
        out_shape=jax.ShapeDtypeStruct((m, n), out_dtype),
        compiler_params=_cparams(("arbitrary", "arbitrary")),
    )(a, b)


def _mm_multi(name, pairs, mode, out_dtype, tm=1024, tn=512):
    a0, b0 = pairs[0]
    m = a0.shape[1] if mode == "tn" else a0.shape[0]
    k = a0.shape[0] if mode == "tn" else a0.shape[1]
    n = b0.shape[0] if mode == "nt" else b0.shape[1]
    tm, tn = _pick(m, tm), _pick(n, tn)
    npair = len(pairs)

    def body(*refs):
        for p in range(npair):
            refs[2 * npair + p][...] = _raw_dot(refs[2 * p][...], refs[2 * p + 1][...], mode).astype(out_dtype)

    a_spec = pl.BlockSpec((k, tm), lambda i, j: (0, i)) if mode == "tn" else pl.BlockSpec((tm, k), lambda i, j: (i, 0))
    b_spec = pl.BlockSpec((tn, k), lambda i, j: (j, 0)) if mode == "nt" else pl.BlockSpec((k, tn), lambda i, j: (0, j))
    return pl.pallas_call(
        body, name=name, grid=(m // tm, n // tn), in_specs=[a_spec, b_spec] * npair,
        out_specs=[pl.BlockSpec((tm, tn), lambda i, j: (i, j))] * npair,
        out_shape=[jax.ShapeDtypeStruct((m, n), out_dtype)] * npair,
        compiler_params=_cparams(("arbitrary", "arbitrary")),
    )(*[x for pair in pairs for x in pair])


def _mm_cols_tn(name, pieces, b, out_dtype, tm):
    k, n = b.shape
    counts = [p.shape[1] // tm for p in pieces]
    starts = [sum(counts[:i]) for i in range(len(pieces))]
    na = len(pieces)

    def body(*refs):
        b_ref, o_ref = refs[na], refs[-1]
        i = pl.program_id(0)
        for a_ref, s, c in zip(refs[:na], starts, counts):
            @pl.when((i >= s) & (i < s + c))
            def _():
                o_ref[...] = _raw_dot(a_ref[...], b_ref[...], "tn").astype(o_ref.dtype)

    def spec(s, c):
        return pl.BlockSpec((k, tm), lambda i: (0, jnp.clip(i - s, 0, c - 1)))

    return pl.pallas_call(
        body, name=name, grid=(sum(counts),),
        in_specs=[spec(s, c) for s, c in zip(starts, counts)]
        + [pl.BlockSpec(b.shape, lambda i: (0, 0), pipeline_mode=pl.Buffered(1))],
        out_specs=pl.BlockSpec((tm, n), lambda i: (i, 0)),
        out_shape=jax.ShapeDtypeStruct((sum(counts) * tm, n), out_dtype),
        compiler_params=_cparams(("arbitrary",)),
    )(*pieces, b)


def _norm_in_proj(x, g, w_t, tm, tn):
    t, k = x.shape
    n = w_t.shape[0]
    tm, tn = _pick(t, tm), _pick(n, tn)

    def body(x_ref, g_ref, w_ref, xn_ref, z_ref):
        xn = _rms(x_ref[...], g_ref[...]).astype(BF)
        xn_ref[...] = xn
        z_ref[...] = _raw_dot(xn, w_ref[...], "nt")

    xns, z = pl.pallas_call(
        body, name="in_proj", grid=(n // tn, t // tm),
        in_specs=[pl.BlockSpec((tm, k), lambda j, i: (i, 0)), pl.BlockSpec((1, k), lambda j, i: (0, 0)),
                  pl.BlockSpec((tn, k), lambda j, i: (j, 0))],
        out_specs=[pl.BlockSpec((None, tm, k), lambda j, i: (j, i, 0)), pl.BlockSpec((tm, tn), lambda j, i: (i, j))],
        out_shape=[jax.ShapeDtypeStruct((n // tn, t, k), BF), jax.ShapeDtypeStruct((t, n), F32)],
        compiler_params=_cparams(("arbitrary", "arbitrary")),
    )(x, g, w_t)
    return xns[0], z


def _merge_out_post(z, o_a, o_b, w_a, w_b, w_out, x, g_post, g_pre2, tm):
    t = x.shape[0]
    tm = _pick(t, tm)
    w = 256
    npc = D // w
    ga0, gb0 = (IN_COLS - 2 * D) // w, (IN_COLS - D) // w

    def body(*refs):
        ga_refs, gb_refs = refs[:npc], refs[npc:2 * npc]
        oa_ref, ob_ref, wa_ref, wb_ref, w_ref, x_ref, gp_ref, g2_ref = refs[2 * npc:2 * npc + 8]
        ya_ref, yb_ref, m_ref, mix_ref, h_ref, xn_ref = refs[2 * npc + 8:]
        ya = _raw_dot(oa_ref[...], wa_ref[...], "nn").astype(BF)
        yb = _raw_dot(ob_ref[...], wb_ref[...], "nn").astype(BF)
        ya_ref[...] = ya
        yb_ref[...] = yb
        parts = []
        for p in range(npc):
            cols = slice(p * w, (p + 1) * w)
            parts.append(_sigmoid(ga_refs[p][...]) * ya[:, cols].astype(F32)
                         + _sigmoid(gb_refs[p][...]) * yb[:, cols].astype(F32))
        merged = jnp.concatenate(parts, axis=1).astype(BF)
        m_ref[...] = merged
        mix = _raw_dot(merged, w_ref[...], "nn")
        mix_ref[...] = mix
        h1 = x_ref[...] + _rms(mix, gp_ref[...])
        h_ref[...] = h1
        xn_ref[...] = _rms(h1, g2_ref[...]).astype(BF)

    row = pl.BlockSpec((tm, D), lambda i: (i, 0))
    one = pl.BlockSpec((1, D), lambda i: (0, 0))

    def gate(b0):
        return [pl.BlockSpec((tm, w), functools.partial(lambda i, b: (i, b), b=b0 + p)) for p in range(npc)]

    wgt = pl.BlockSpec((D, D), lambda i: (0, 0), pipeline_mode=pl.Buffered(1))
    return pl.pallas_call(
        body, name="merge_out_post", grid=(t // tm,),
        in_specs=gate(ga0) + gate(gb0) + [row, row, wgt, wgt, wgt, row, one, one],
        out_specs=[row] * 6,
        out_shape=[jax.ShapeDtypeStruct((t, D), BF), jax.ShapeDtypeStruct((t, D), BF), jax.ShapeDtypeStruct((t, D), BF),
                   jax.ShapeDtypeStruct((t, D), F32), jax.ShapeDtypeStruct((t, D), F32),
                   jax.ShapeDtypeStruct((t, D), BF)],
        compiler_params=_cparams(("arbitrary",)),
    )(*([z] * (2 * npc)), o_a, o_b, w_a, w_b, w_out, x, g_post, g_pre2)


def _accumulate(ni, refs, vals):
    @pl.when(ni == 0)
    def _():
        for r in refs:
            r[...] = jnp.zeros(r.shape, F32)

    for r, v in zip(refs, vals):
        r[...] += v


def _dmerged_merge_bwd(dmix, w_out, w_a, w_b, z, y_a, y_b, tm):
    t = dmix.shape[0]
    tm = _pick(t, tm)
    w = 256
    npc = D // w
    ga0, gb0 = (IN_COLS - 2 * D) // w, (IN_COLS - D) // w

    def body(*refs):
        dm_ref, w_ref, wa_ref, wb_ref = refs[:4]
        ga_refs, gb_refs = refs[4:4 + npc], refs[4 + npc:4 + 2 * npc]
        ya_ref, yb_ref, dga_ref, dgb_ref, dya_ref, dyb_ref, doa_ref, dob_ref = refs[4 + 2 * npc:]
        dmerged = _raw_dot(dm_ref[...], w_ref[...], "nt")
        dyas, dybs = [], []
        for p in range(npc):
            cols = slice(p * w, (p + 1) * w)
            xs = [ga_refs[p][...], gb_refs[p][...], ya_ref[:, cols].astype(F32), yb_ref[:, cols].astype(F32)]
            _, vjp = jax.vjp(lambda *a: _f_merge([], list(a), [])[0][0], *xs)
            dga, dgb, dya, dyb = vjp(dmerged[:, cols])
            dga_ref[:, cols] = dga.astype(BF)
            dgb_ref[:, cols] = dgb.astype(BF)
            dyas.append(dya.astype(BF))
            dybs.append(dyb.astype(BF))
        dya, dyb = jnp.concatenate(dyas, axis=1), jnp.concatenate(dybs, axis=1)
        dya_ref[...] = dya
        dyb_ref[...] = dyb
        doa_ref[...] = _raw_dot(dya, wa_ref[...], "nt").astype(BF)
        dob_ref[...] = _raw_dot(dyb, wb_ref[...], "nt").astype(BF)

    row = pl.BlockSpec((tm, D), lambda i: (i, 0))
    wgt = pl.BlockSpec((D, D), lambda i: (0, 0), pipeline_mode=pl.Buffered(1))

    def gate(b0):
        return [pl.BlockSpec((tm, w), functools.partial(lambda i, b: (i, b), b=b0 + p)) for p in range(npc)]

    return pl.pallas_call(
        body, name="merge_bwd", grid=(t // tm,),
        in_specs=[row, wgt, wgt, wgt] + gate(ga0) + gate(gb0) + [row, row],
        out_specs=[row] * 6, out_shape=[jax.ShapeDtypeStruct((t, D), BF)] * 6,
        compiler_params=_cparams(("arbitrary",)),
    )(dmix, w_out, w_a, w_b, *([z] * (2 * npc)), y_a, y_b)


def _dxn2_post1_bwd(pieces, w_up_t, x, mix, dh1, g_post, g_pre2, tm):
    t = x.shape[0]
    tm = _pick(t, tm)
    k = w_up_t.shape[0]
    offs = [sum(p.shape[1] for p in pieces[:i]) for i in range(len(pieces))]
    na = len(pieces)

    def body(*refs):
        w_ref, x_ref, m_ref, dh_ref, gp_ref, g2_ref, dgp_ref, dg2_ref, dx_ref, dm_ref = refs[na:]
        dxn2 = None
        for a_ref, off in zip(refs[:na], offs):
            part = _raw_dot(a_ref[...], w_ref[off:off + a_ref.shape[1], :], "nn")
            dxn2 = part if dxn2 is None else dxn2 + part
        _, vjp = jax.vjp(lambda gp, g2, xx, mm: _f_post1([gp, g2], [xx, mm], [])[0],
                         gp_ref[...], g2_ref[...], x_ref[...], m_ref[...])
        dgp, dg2, dx, dm = vjp([dh_ref[...], dxn2])
        _accumulate(pl.program_id(0), [dgp_ref, dg2_ref], [dgp, dg2])
        dx_ref[...] = dx
        dm_ref[...] = dm.astype(BF)

    row = pl.BlockSpec((tm, D), lambda i: (i, 0))
    one = pl.BlockSpec((1, D), lambda i: (0, 0))
    return pl.pallas_call(
        body, name="post1_bwd", grid=(t // tm,),
        in_specs=[pl.BlockSpec((tm, p.shape[1]), lambda i: (i, 0)) for p in pieces]
        + [pl.BlockSpec((k, D), lambda i: (0, 0), pipeline_mode=pl.Buffered(1)), row, row, row, one, one],
        out_specs=[one, one, row, row],
        out_shape=[jax.ShapeDtypeStruct((1, D), F32), jax.ShapeDtypeStruct((1, D), F32),
                   jax.ShapeDtypeStruct((t, D), F32), jax.ShapeDtypeStruct((t, D), BF)],
        compiler_params=_cparams(("arbitrary",)),
    )(*pieces, w_up_t, x, mix, dh1, g_post, g_pre2)


def _conv_taps(h, cw, cb, p2, p1):
    s1 = _shift_down(h, p1)
    s2 = _shift_down(s1, p2)
    return cb + _row(cw, 0) * s2 + _row(cw, 1) * s1 + _row(cw, 2) * h


def _up_conv(xn2, w_up_t, conv_w, conv_b, tm, tc):
    t = xn2.shape[0]
    tm = _pick(t, tm)
    tn = _pick(DFF, 1408)
    nj = DFF // tn
    sub = tm // tc
    n = t // tc
    last = t // tm - 1

    def body(x_ref, wg_ref, wv_ref, cwg_ref, cwv_ref, cbg_ref, cbv_ref, hg_ref, hv_ref, act_ref, c1_ref, c2_ref, prev):
        j, i = pl.program_id(0), pl.program_id(1)

        @pl.when(i == 0)
        def _():
            prev[...] = jnp.zeros(prev.shape, F32)

        x = x_ref[...]
        hg = _raw_dot(x, wg_ref[...], "nt")
        hv = _raw_dot(x, wv_ref[...], "nt")
        hg_ref[...] = hg
        hv_ref[...] = hv
        pg, pv = prev[0:SUBLANES], prev[SUBLANES:2 * SUBLANES]
        cg = _conv_taps(hg, cwg_ref[...], cbg_ref[...], _row(pg, SUBLANES - 2), _row(pg, SUBLANES - 1))
        cv = _conv_taps(hv, cwv_ref[...], cbv_ref[...], _row(pv, SUBLANES - 2), _row(pv, SUBLANES - 1))
        act_ref[...] = (_silu(cg) * cv).astype(BF)
        prev[0:SUBLANES] = hg[tm - SUBLANES:tm]
        prev[SUBLANES:2 * SUBLANES] = hv[tm - SUBLANES:tm]

        def keep(h, off):
            cols = slice(off, off + tn)

            @pl.when(i == 0)
            def _():
                c1_ref[0, :, cols] = jnp.zeros((1, tn), F32)
                c2_ref[0, :, cols] = jnp.zeros((1, tn), F32)

            for s in range(sub):
                def put(s=s):
                    tail = h[(s + 1) * tc - SUBLANES:(s + 1) * tc]
                    c1_ref[i * sub + s + 1, :, cols] = _row(tail, SUBLANES - 1)
                    c2_ref[i * sub + s + 1, :, cols] = _row(tail, SUBLANES - 2)

                if s < sub - 1:
                    put()
                else:
                    pl.when(i < last)(put)

        for col in range(nj):
            @pl.when(j == col)
            def _(col=col):
                keep(hg, col * tn)
                keep(hv, DFF + col * tn)

    def cols(rows, off):
        return pl.BlockSpec((rows, tn), lambda j, i: (0, j + off))

    tile = pl.BlockSpec((tm, tn), lambda j, i: (i, j))
    before = pl.BlockSpec((n, 1, 2 * DFF), lambda j, i: (0, 0, 0))
    return pl.pallas_call(
        body, name="up_conv", grid=(nj, t // tm),
        in_specs=[pl.BlockSpec((tm, D), lambda j, i: (i, 0)), pl.BlockSpec((tn, D), lambda j, i: (j, 0)),
                  pl.BlockSpec((tn, D), lambda j, i: (j + nj, 0)), cols(3, 0), cols(3, nj), cols(1, 0), cols(1, nj)],
        out_specs=[tile, tile, tile, before, before],
        out_shape=[jax.ShapeDtypeStruct((t, DFF), F32), jax.ShapeDtypeStruct((t, DFF), F32),
                   jax.ShapeDtypeStruct((t, DFF), BF), jax.ShapeDtypeStruct((n, 1, 2 * DFF), F32),
                   jax.ShapeDtypeStruct((n, 1, 2 * DFF), F32)],
        scratch_shapes=[pltpu.VMEM((2 * SUBLANES, tn), F32)],
        compiler_params=_cparams(("arbitrary", "arbitrary")),
    )(xn2, w_up_t, w_up_t, conv_w, conv_w, conv_b, conv_b)


def _dxn_pre1_bwd(pieces, w_t, x, dx_res, g, tm, token):
    t = x.shape[0]
    tm = _pick(t, tm)
    offs = [sum(p.shape[1] for p in pieces[:i]) for i in range(len(pieces))]
    na = len(pieces)

    def body(*refs):
        w_ref, x_ref, r_ref, g_ref = refs[na:na + 4]
        dg_ref, dx_ref = refs[-2:]
        dxn = None
        for a_ref, off in zip(refs[:na], offs):
            part = _raw_dot(a_ref[...], w_ref[off:off + a_ref.shape[1], :], "nn")
            dxn = part if dxn is None else dxn + part
        _, vjp = jax.vjp(lambda gg, xx: _f_pre1_residual([gg], [xx], [])[0], g_ref[...], x_ref[...])
        dg, dx = vjp([dxn, r_ref[...]])
        _accumulate(pl.program_id(0), [dg_ref], [dg])
        dx_ref[...] = dx

    row = pl.BlockSpec((tm, D), lambda i: (i, 0))
    one = pl.BlockSpec((1, D), lambda i: (0, 0))
    return pl.pallas_call(
        body, name="pre1_bwd", grid=(t // tm,),
        in_specs=[pl.BlockSpec((tm, p.shape[1]), lambda i: (i, 0)) for p in pieces]
        + [pl.BlockSpec(w_t.shape, lambda i: (0, 0), pipeline_mode=pl.Buffered(1)), row, row, one,
           pl.BlockSpec(token.shape, lambda i: (0, 0))],
        out_specs=[one, row],
        out_shape=[jax.ShapeDtypeStruct((1, D), F32), jax.ShapeDtypeStruct((t, D), F32)],
        compiler_params=_cparams(("arbitrary",)),
    )(*pieces, w_t, x, dx_res, g, token)


def _down_loss(act, w_down, g_post, h1, tgt, tm):
    t, k = act.shape
    tm = _pick(t, tm)

    def body(a_ref, w_ref, g_ref, h_ref, t_ref, loss_ref, dg_ref, dh_ref, df_ref):
        ni = pl.program_id(0)
        ff = _raw_dot(a_ref[...], w_ref[...], "nn")
        target = t_ref[...]

        def lossf(g, h1, ff):
            e = h1 + _rms(ff, g) - target
            return 0.5 * jnp.sum(jnp.mean(e * e, axis=-1))

        l, (dg, dh, df) = jax.value_and_grad(lossf, argnums=(0, 1, 2))(g_ref[...], h_ref[...], ff)

        @pl.when(ni == 0)
        def _():
            loss_ref[...] = jnp.zeros(loss_ref.shape, F32)
            dg_ref[...] = jnp.zeros(dg_ref.shape, F32)

        loss_ref[...] += jnp.full(loss_ref.shape, l, F32)
        dg_ref[...] += dg
        dh_ref[...] = dh
        df_ref[...] = df.astype(df_ref.dtype)

    row = pl.BlockSpec((tm, D), lambda ni: (ni, 0))
    one = pl.BlockSpec((1, D), lambda ni: (0, 0))
    return pl.pallas_call(
        body, name="down_loss", grid=(t // tm,),
        in_specs=[pl.BlockSpec((tm, k), lambda ni: (ni, 0)),
                  pl.BlockSpec((k, D), lambda ni: (0, 0), pipeline_mode=pl.Buffered(1)), one, row, row],
        out_specs=[pl.BlockSpec((1, LANES), lambda ni: (0, 0)), one, row, row],
        out_shape=[jax.ShapeDtypeStruct((1, LANES), F32), jax.ShapeDtypeStruct((1, D), F32),
                   jax.ShapeDtypeStruct((t, D), F32), jax.ShapeDtypeStruct((t, D), BF)],
        compiler_params=_cparams(("arbitrary",)),
    )(act, w_down, g_post, h1, tgt)


_ANY = pl.BlockSpec(memory_space=pl.ANY)


def _all_gather(name, blks):
    na = len(blks)
    ns = 8

    def body(*refs):
        x_refs, out_refs = refs[:na], refs[na:2 * na]
        send_sems, recv_sems, local_sems = refs[2 * na:]
        x, y, cc = lax.axis_index("x"), lax.axis_index("y"), lax.axis_index("c")
        sibling, xn, yn = (x, y, 1 - cc), (1 - x, y, cc), (x, 1 - y, cc)

        def num(px, py, pc):
            return 4 * px + 2 * py + pc

        def copy(a, k, to, src, dst):
            return pltpu.make_async_remote_copy(src_ref=src, dst_ref=dst, send_sem=send_sems.at[ns * a + k],
                                                recv_sem=recv_sems.at[ns * a + k], device_id=to, device_id_type=MESH)

        def halves(a, blk):
            h = blks[a].shape[0] // 2
            return out_refs[a].at[blk, pl.ds(0, h)], out_refs[a].at[blk, pl.ds(h, h)]

        mine, sends = [], []
        for a in range(na):
            o = out_refs[a]
            m = pltpu.make_async_copy(x_refs[a], o.at[num(x, y, cc)], local_sems.at[a])
            m.start()
            mine.append(m)
            own = o.at[num(x, y, cc)]
            sends.append([copy(a, 0, sibling, x_refs[a], own), copy(a, 1, xn, x_refs[a], own),
                          copy(a, 2, yn, x_refs[a], own)])
            for cp in sends[a]:
                cp.start()
        for a in range(na):
            o = out_refs[a]
            bx, by, bd = num(1 - x, y, cc), num(x, 1 - y, cc), num(1 - x, 1 - y, cc)
            copy(a, 1, xn, o.at[bx], o.at[bx]).wait_recv()
            more = [copy(a, 3, yn, halves(a, bx)[0], halves(a, bx)[0]), copy(a, 5, sibling, o.at[bx], o.at[bx])]
            for cp in more:
                cp.start()
            sends[a] += more
        for a in range(na):
            o = out_refs[a]
            bx, by, bd = num(1 - x, y, cc), num(x, 1 - y, cc), num(1 - x, 1 - y, cc)
            copy(a, 2, yn, o.at[by], o.at[by]).wait_recv()
            more = [copy(a, 4, xn, halves(a, by)[1], halves(a, by)[1]), copy(a, 6, sibling, o.at[by], o.at[by])]
            for cp in more:
                cp.start()
            sends[a] += more
        for a in range(na):
            o = out_refs[a]
            bd = num(1 - x, 1 - y, cc)
            copy(a, 3, yn, halves(a, bd)[0], halves(a, bd)[0]).wait_recv()
            copy(a, 4, xn, halves(a, bd)[1], halves(a, bd)[1]).wait_recv()
            fw = copy(a, 7, sibling, o.at[bd], o.at[bd])
            fw.start()
            sends[a].append(fw)
        for a in range(na):
            o = out_refs[a]
            for k, blk in ((0, num(x, y, 1 - cc)), (5, num(1 - x, y, 1 - cc)), (6, num(x, 1 - y, 1 - cc)),
                           (7, num(1 - x, 1 - y, 1 - cc))):
                copy(a, k, sibling, o.at[blk], o.at[blk]).wait_recv()
            for cp in sends[a]:
                cp.wait_send()
        for m in mine:
            m.wait()

    res = pl.pallas_call(
        body, name=name, in_specs=[_ANY] * na, out_specs=[_ANY] * na,
        out_shape=[jax.ShapeDtypeStruct((N_DEV,) + b.shape, b.dtype) for b in blks],
        scratch_shapes=[pltpu.SemaphoreType.DMA((ns * na,)), pltpu.SemaphoreType.DMA((ns * na,)),
                        pltpu.SemaphoreType.DMA((na,))],
    )(*blks)
    return list(res)


def _all_gather_small(name, blk):
    def body(x_ref, out_ref, ssem, rsem, lsem):
        x, y, c = lax.axis_index("x"), lax.axis_index("y"), lax.axis_index("c")
        me = 4 * x + 2 * y + c
        mine = pltpu.make_async_copy(x_ref, out_ref.at[me], lsem)
        mine.start()
        cps = []
        for j in range(1, N_DEV):
            px = 1 - x if j & 4 else x
            py = 1 - y if j & 2 else y
            pc = 1 - c if j & 1 else c
            cps.append(pltpu.make_async_remote_copy(src_ref=x_ref, dst_ref=out_ref.at[me], send_sem=ssem.at[j - 1],
                                                    recv_sem=rsem.at[j - 1], device_id=(px, py, pc),
                                                    device_id_type=MESH))
        for cp in cps:
            cp.start()
        for cp in cps:
            cp.wait()
        mine.wait()

    return pl.pallas_call(
        body, name=name, in_specs=[_ANY], out_specs=_ANY,
        out_shape=jax.ShapeDtypeStruct((N_DEV,) + blk.shape, blk.dtype),
        scratch_shapes=[pltpu.SemaphoreType.DMA((N_DEV - 1,)), pltpu.SemaphoreType.DMA((N_DEV - 1,)),
                        pltpu.SemaphoreType.DMA],
    )(blk)


def _reduce_pair(g8s):
    na = len(g8s)

    def body(*refs):
        g_refs, recv_refs = refs[:na], refs[na:2 * na]
        ssem, rsem = refs[2 * na:]
        x, y, cc = lax.axis_index("x"), lax.axis_index("y"), lax.axis_index("c")
        chips = [(x, y), (1 - x, y), (x, 1 - y), (1 - x, 1 - y)]
        sib = (x, y, 1 - cc)
        for a in range(na):
            for k, (cx, cy) in enumerate(chips):
                pltpu.make_async_remote_copy(
                    src_ref=g_refs[a].at[4 * cx + 2 * cy + 1 - cc], dst_ref=recv_refs[a].at[k],
                    send_sem=ssem.at[a], recv_sem=rsem.at[a], device_id=sib, device_id_type=MESH).start()
        for a in range(na):
            pltpu.make_async_remote_copy(src_ref=recv_refs[a], dst_ref=recv_refs[a], send_sem=ssem.at[a],
                                         recv_sem=rsem.at[a], device_id=sib, device_id_type=MESH).wait()

    res = pl.pallas_call(
        body, name="reduce_pair", in_specs=[_ANY] * na, out_specs=[_ANY] * na,
        out_shape=[jax.ShapeDtypeStruct((4,) + g.shape[1:], g.dtype) for g in g8s],
        scratch_shapes=[pltpu.SemaphoreType.DMA((na,)), pltpu.SemaphoreType.DMA((na,))],
    )(*g8s)
    return list(res)


_HBM = pl.BlockSpec(memory_space=pltpu.HBM)
_SEM = pl.BlockSpec(memory_space=pltpu.SEMAPHORE)
_EFFECT = pltpu.SideEffectType.DATAFLOW_SIDE_EFFECTING


def _chip_swap_copies(s_refs, land_refs, ssem, rsem):
    x, y, c = lax.axis_index("x"), lax.axis_index("y"), lax.axis_index("c")
    targets = [(1 - x, y, c), (x, 1 - y, c), (1 - x, 1 - y, c)]
    return [pltpu.make_async_remote_copy(src_ref=s.at[k], dst_ref=d.at[k], send_sem=ssem.at[3 * a + k],
                                         recv_sem=rsem.at[3 * a + k], device_id=targets[k], device_id_type=MESH)
            for a, (s, d) in enumerate(zip(s_refs, land_refs)) for k in range(3)]


def _chip_swap_start(sends):
    na = len(sends)

    def body(*refs):
        cps = _chip_swap_copies(refs[:na], refs[na:2 * na], refs[2 * na], refs[2 * na + 1])
        for cp in cps:
            cp.start()
        token = refs[-1]
        token[...] = jnp.zeros(token.shape, token.dtype)

    bufs = [pltpu.HBM(s.shape, s.dtype) for s in sends]
    res = pl.pallas_call(
        body, name="chip_swap_start",
        out_shape=[pltpu.SemaphoreType.DMA((3 * na,)), pltpu.SemaphoreType.DMA((3 * na,))] + bufs + bufs
        + [jax.ShapeDtypeStruct((8, LANES), F32)],
        in_specs=[_HBM] * (2 * na), out_specs=[_SEM, _SEM] + [_HBM] * (2 * na) + [pl.BlockSpec(memory_space=pltpu.VMEM)],
        input_output_aliases={i: 2 + i for i in range(2 * na)},
        compiler_params=pltpu.CompilerParams(has_side_effects=_EFFECT),
    )(*[pltpu.with_memory_space_constraint(s, pltpu.HBM) for s in sends],
      *[pltpu.with_memory_space_constraint(lax.empty(s.shape, s.dtype), pltpu.HBM) for s in sends])
    return res[0], res[1], list(res[2:2 + na]), list(res[2 + na:2 + 2 * na]), res[-1]


def _chip_swap_wait(ssem, rsem, srcs, lands, after):
    na = len(srcs)

    def body(*refs):
        cps = _chip_swap_copies(refs[:na], refs[na:2 * na], refs[2 * na], refs[2 * na + 1])
        for cp in cps:
            cp.wait_send()
            cp.wait_recv()

    bufs = [pltpu.HBM(s.shape, s.dtype) for s in srcs]
    res = pl.pallas_call(
        body, name="chip_swap_wait", out_shape=bufs + bufs,
        in_specs=[_HBM] * (2 * na) + [_SEM, _SEM, _ANY], out_specs=[_HBM] * (2 * na),
        input_output_aliases={i: i for i in range(2 * na)},
        compiler_params=pltpu.CompilerParams(has_side_effects=_EFFECT),
    )(*srcs, *lands, ssem, rsem, after)
    return list(res[na:])


def _pick_rows(r, c, budget=TILE_BYTES):
    if r * c * 4 <= budget or r % 16:
        return r
    best = 16
    for tr in range(16, r, 16):
        if r % tr == 0 and tr * c * 4 <= budget:
            best = tr
    return best


def _pair_sum(name, idx4, g8, recv4):
    _, r, c = g8.shape
    tr = _pick_rows(r, c, 2 * TILE_BYTES)

    def body(idx_ref, a_ref, b_ref, o0_ref, o3_ref):
        k = pl.program_id(1)
        s = a_ref[...].astype(F32) + b_ref[...].astype(F32)

        @pl.when(k == 0)
        def _():
            o0_ref[...] = s

        @pl.when(k > 0)
        def _():
            o3_ref[...] = s.astype(BF)

    spec = pltpu.PrefetchScalarGridSpec(
        num_scalar_prefetch=1, grid=(r // tr, 4),
        in_specs=[pl.BlockSpec((None, tr, c), lambda i, k, idx: (idx[k], i, 0)),
                  pl.BlockSpec((None, tr, c), lambda i, k, idx: (k, i, 0))],
        out_specs=[pl.BlockSpec((tr, c), lambda i, k, idx: (i, 0)),
                   pl.BlockSpec((None, tr, c), lambda i, k, idx: (jnp.maximum(k - 1, 0), i, 0))])
    return pl.pallas_call(
        body, name=name, grid_spec=spec,
        out_shape=[jax.ShapeDtypeStruct((r, c), F32), jax.ShapeDtypeStruct((3, r, c), BF)],
        compiler_params=_cparams(("arbitrary", "arbitrary")),
    )(idx4, g8, recv4)


def _adamw(w, g, m, v):
    m = ADAM_B1 * m + (1.0 - ADAM_B1) * g
    v = ADAM_B2 * v + (1.0 - ADAM_B2) * jnp.square(g)
    m_hat = m / (1.0 - ADAM_B1 ** ADAM_STEP)
    v_hat = v / (1.0 - ADAM_B2 ** ADAM_STEP)
    delta = -ADAM_LR * (m_hat / (jnp.sqrt(v_hat) + ADAM_EPS) + ADAM_WD * w)
    return delta, m, v


def _adam_sharded(name, idx1, own, recv, w, m, v, after=None):
    r, c = w.shape
    tr = _pick_rows(r, c, 2 * TILE_BYTES)
    nj = recv.shape[0]
    extra = [] if after is None else [after]

    def body(idx_ref, p_ref, r_ref, w_ref, m_ref, v_ref, *rest):
        g_out, d_out, m_out, v_out = rest[-4:]
        g = p_ref[...].astype(F32)
        for k in range(nj):
            g = g + r_ref[k].astype(F32)
        d, mn, vn = _adamw(w_ref[...], g, m_ref[...], v_ref[...])
        g_out[...] = g
        d_out[...] = d
        m_out[...] = mn
        v_out[...] = vn

    row = pl.BlockSpec((tr, c), lambda i, idx: (i, 0))
    spec = pltpu.PrefetchScalarGridSpec(
        num_scalar_prefetch=1, grid=(r // tr,),
        in_specs=[pl.BlockSpec((None, tr, c), lambda i, idx: (idx[0], i, 0)),
                  pl.BlockSpec((nj, tr, c), lambda i, idx: (0, i, 0)), row, row, row]
        + [pl.BlockSpec(e.shape, lambda i, idx: (0, 0)) for e in extra],
        out_specs=[row] * 4)
    return pl.pallas_call(
        body, name=name, grid_spec=spec, out_shape=[jax.ShapeDtypeStruct((r, c), F32)] * 4,
        compiler_params=_cparams(("arbitrary",)),
    )(idx1, own, recv, w, m, v, *extra)


def _repl_rows():
    rows, r = {}, 0
    for name, cols in REPL:
        rows[name] = r
        r += REPL_ROWS.get(name, 1) * ((cols + D - 1) // D)
    return rows


LOSS_ROW = 24


def _pack_replicated(grads, loss_acc, after):
    rows = _repl_rows()
    names = [n for n, _ in REPL]

    def body(*refs):
        o_ref = refs[-1]
        o_ref[...] = jnp.zeros(o_ref.shape, F32)
        o_ref[LOSS_ROW:LOSS_ROW + 1, 0:LANES] = refs[len(names)][...]
        for name, ref in zip(names, refs[:len(names)]):
            r0 = rows[name]
            nr, nc = ref.shape
            if nc <= D:
                o_ref[r0:r0 + nr, 0:nc] = ref[...]
            else:
                for j in range((nc + D - 1) // D):
                    lo, hi = j * D, min(nc, (j + 1) * D)
                    o_ref[r0 + j:r0 + j + 1, 0:hi - lo] = ref[:, lo:hi]

    return pl.pallas_call(body, name="pack_replicated", out_shape=jax.ShapeDtypeStruct((REPL_TOTAL, D), F32),
                          in_specs=[pl.BlockSpec(memory_space=pltpu.VMEM)] * (len(names) + 1) + [_ANY] * len(after),
                          compiler_params=_cparams())(*[grads[n] for n in names], loss_acc, *after)


def _adam_replicated(g8, ws, ms, vs):
    rows = _repl_rows()
    names = [n for n, _ in REPL]
    np_ = len(names)

    def body(*refs):
        g_ref = refs[0]
        w_refs, m_refs, v_refs = refs[1:1 + np_], refs[1 + np_:1 + 2 * np_], refs[1 + 2 * np_:1 + 3 * np_]
        outs = refs[1 + 3 * np_:1 + 7 * np_]
        scr = refs[-1]
        g = g_ref[0]
        for k in range(1, N_DEV):
            g = g + g_ref[k]
        scr[...] = g
        refs[1 + 7 * np_][...] = scr[LOSS_ROW:LOSS_ROW + 1, 0:LANES]
        for i, name in enumerate(names):
            r0 = rows[name]
            nr, nc = w_refs[i].shape
            if nc <= D:
                gi = scr[r0:r0 + nr, 0:nc]
            else:
                parts = []
                for j in range((nc + D - 1) // D):
                    lo, hi = j * D, min(nc, (j + 1) * D)
                    parts.append(scr[r0 + j:r0 + j + 1, 0:hi - lo])
                gi = jnp.concatenate(parts, axis=1)
            d, mn, vn = _adamw(w_refs[i][...], gi, m_refs[i][...], v_refs[i][...])
            outs[i][...] = gi
            outs[np_ + i][...] = d
            outs[2 * np_ + i][...] = mn
            outs[3 * np_ + i][...] = vn

    shp = [jax.ShapeDtypeStruct(w.shape, F32) for w in ws]
    res = pl.pallas_call(body, name="adam_replicated", out_shape=shp * 4 + [jax.ShapeDtypeStruct((1, LANES), F32)],
                         scratch_shapes=[pltpu.VMEM((REPL_TOTAL, D), F32)], compiler_params=_cparams(),
                         )(g8, *ws, *ms, *vs)
    return [dict(zip(names, res[k * np_:(k + 1) * np_])) for k in range(4)], res[-1]


_WEIGHTS = ("attn_pre_norm", "w_in", "hgrn_lb", "hgrn_gnorm", "w_branch_a", "rwkv_mu", "rwkv_w0", "rwkv_w2",
            "rwkv_a0", "rwkv_a2", "rwkv_g2", "rwkv_k_k", "rwkv_k_a", "rwkv_r_k", "rwkv_ln_w", "rwkv_ln_b",
            "w_branch_b", "w_out", "attn_post_norm", "ffn_pre_norm", "w_up", "conv_w", "conv_b", "w_down",
            "ffn_post_norm")
_BIG = ("w_in", "w_up", "w_down", "w_branch_a", "w_branch_b", "w_out")


def _stages():
    one = [D]
    hw = HG_K * HG_PER_STEP
    rw = LANES * RW_PAIRS_PER_STEP
    return dict(
        mixers=_Stage("mixers", _f_mixers, 1, 2 * RW_CHUNK, [False] * 13, [[D] * 7 + [LANES, LANES]], [0],
                      [(hw, HG_K), (1, RW_COLS), (rw, LANES)], [one, one], [BF, BF],
                      kept_shapes=[(2 * RW_KEPT * RW_PAIRS_PER_STEP * 2 * RW_CHUNK, LANES)], f_kept=_f_mixers_kept),
        conv=_Stage("conv", _f_conv, 1, 512, [False, False], [[DFF], [DFF]], [0, 0], [(1, 2 * DFF), (1, 2 * DFF)],
                    [[DFF]], [BF]),
    )


def _cols_to_blocks(w, per):
    return w.reshape(w.shape[0], N_DEV, per).transpose(1, 0, 2)


def _blocks_to_cols(g):
    return g.transpose(1, 0, 2).reshape(g.shape[1], N_DEV * g.shape[2])


def kernel(x, attn_pre_norm, w_in, hgrn_lb, hgrn_gnorm, w_branch_a, rwkv_mu, rwkv_w0, rwkv_w2, rwkv_a0, rwkv_a2, rwkv_g2, rwkv_k_k, rwkv_k_a, rwkv_r_k, rwkv_ln_w, rwkv_ln_b, w_branch_b, w_out, attn_post_norm, ffn_pre_norm, w_up, conv_w, conv_b, w_down, ffn_post_norm, loss_target, m_attn_pre_norm, m_w_in, m_hgrn_lb, m_hgrn_gnorm, m_w_branch_a, m_rwkv_mu, m_rwkv_w0, m_rwkv_w2, m_rwkv_a0, m_rwkv_a2, m_rwkv_g2, m_rwkv_k_k, m_rwkv_k_a, m_rwkv_r_k, m_rwkv_ln_w, m_rwkv_ln_b, m_w_branch_b, m_w_out, m_attn_post_norm, m_ffn_pre_norm, m_w_up, m_conv_w, m_conv_b, m_w_down, m_ffn_post_norm, v_attn_pre_norm, v_w_in, v_hgrn_lb, v_hgrn_gnorm, v_w_branch_a, v_rwkv_mu, v_rwkv_w0, v_rwkv_w2, v_rwkv_a0, v_rwkv_a2, v_rwkv_g2, v_rwkv_k_k, v_rwkv_k_a, v_rwkv_r_k, v_rwkv_ln_w, v_rwkv_ln_b, v_w_branch_b, v_w_out, v_attn_post_norm, v_ffn_pre_norm, v_w_up, v_conv_w, v_conv_b, v_w_down, v_ffn_post_norm):
    w = dict(attn_pre_norm=attn_pre_norm, w_in=w_in, hgrn_lb=hgrn_lb, hgrn_gnorm=hgrn_gnorm, w_branch_a=w_branch_a, rwkv_mu=rwkv_mu, rwkv_w0=rwkv_w0, rwkv_w2=rwkv_w2, rwkv_a0=rwkv_a0, rwkv_a2=rwkv_a2, rwkv_g2=rwkv_g2, rwkv_k_k=rwkv_k_k, rwkv_k_a=rwkv_k_a, rwkv_r_k=rwkv_r_k, rwkv_ln_w=rwkv_ln_w, rwkv_ln_b=rwkv_ln_b, w_branch_b=w_branch_b, w_out=w_out, attn_post_norm=attn_post_norm, ffn_pre_norm=ffn_pre_norm, w_up=w_up, conv_w=conv_w, conv_b=conv_b, w_down=w_down, ffn_post_norm=ffn_post_norm)
    mo = dict(attn_pre_norm=m_attn_pre_norm, w_in=m_w_in, hgrn_lb=m_hgrn_lb, hgrn_gnorm=m_hgrn_gnorm, w_branch_a=m_w_branch_a, rwkv_mu=m_rwkv_mu, rwkv_w0=m_rwkv_w0, rwkv_w2=m_rwkv_w2, rwkv_a0=m_rwkv_a0, rwkv_a2=m_rwkv_a2, rwkv_g2=m_rwkv_g2, rwkv_k_k=m_rwkv_k_k, rwkv_k_a=m_rwkv_k_a, rwkv_r_k=m_rwkv_r_k, rwkv_ln_w=m_rwkv_ln_w, rwkv_ln_b=m_rwkv_ln_b, w_branch_b=m_w_branch_b, w_out=m_w_out, attn_post_norm=m_attn_post_norm, ffn_pre_norm=m_ffn_pre_norm, w_up=m_w_up, conv_w=m_conv_w, conv_b=m_conv_b, w_down=m_w_down, ffn_post_norm=m_ffn_post_norm)
    vo = dict(attn_pre_norm=v_attn_pre_norm, w_in=v_w_in, hgrn_lb=v_hgrn_lb, hgrn_gnorm=v_hgrn_gnorm, w_branch_a=v_w_branch_a, rwkv_mu=v_rwkv_mu, rwkv_w0=v_rwkv_w0, rwkv_w2=v_rwkv_w2, rwkv_a0=v_rwkv_a0, rwkv_a2=v_rwkv_a2, rwkv_g2=v_rwkv_g2, rwkv_k_k=v_rwkv_k_k, rwkv_k_a=v_rwkv_k_a, rwkv_r_k=v_rwkv_r_k, rwkv_ln_w=v_rwkv_ln_w, rwkv_ln_b=v_rwkv_ln_b, w_branch_b=v_w_branch_b, w_out=v_w_out, attn_post_norm=v_attn_post_norm, ffn_pre_norm=v_ffn_pre_norm, w_up=v_w_up, conv_w=v_conv_w, conv_b=v_conv_b, w_down=v_w_down, ffn_post_norm=v_ffn_post_norm)

    t = x.shape[1]
    x2 = x.reshape(t, D)
    tgt = loss_target.reshape(t, D)
    st = _stages()

    me = 4 * lax.axis_index("x") + 2 * lax.axis_index("y") + lax.axis_index("c")
    small = jnp.concatenate([rwkv_w2[0], rwkv_a2[0], rwkv_g2[0]], axis=0).astype(BF)
    g_in, g_small = _all_gather("gather_weights", [w_in[0].T.astype(BF), small])
    fw_in_t = g_in.reshape(IN_COLS, D)
    z64 = jnp.zeros((64, D), BF)
    w2p = jnp.concatenate([_blocks_to_cols(g_small[:, 0:64]), z64], axis=0)
    a2p = jnp.concatenate([z64, _blocks_to_cols(g_small[:, 64:128])], axis=0)
    g2f = _blocks_to_cols(g_small[:, 128:256])
    conv_bits = jnp.pad(lax.bitcast_convert_type(conv_w[0], BF).reshape(3, 2 * 704), ((0, 29), (0, 0)))
    late = [w_up[0].T.astype(BF)] + [w[k][0].astype(BF) for k in _BIG[2:]] + [conv_bits]
    late_gather = _Exchange("gather2", late)
    r_k = rwkv_r_k.reshape(1, D)

    xn, z = _norm_in_proj(x2, attn_pre_norm, fw_in_t, 512, 4736)
    mix_par = [hgrn_lb, hgrn_gnorm, rwkv_mu, rwkv_w0, w2p, rwkv_a0, a2p, g2f, rwkv_k_k, rwkv_k_a,
               rwkv_ln_w, rwkv_ln_b, r_k]
    mix_in = [z]
    (o_a, o_b), mix_saved = _stage_fwd(st["mixers"], t, mix_par, mix_in, hook=late_gather)
    gl = [lax.dynamic_update_slice(g, own[None], (me, 0, 0)) for g, own in zip(late_gather.results, late)]
    fw_up_t = gl[0].reshape(2 * DFF, D)
    fw_down = gl[1].reshape(DFF, D)
    fw_a, fw_b, fw_out = (g.reshape(D, D) for g in gl[2:5])
    conv_full = _blocks_to_cols(lax.bitcast_convert_type(gl[5][:, :3].reshape(N_DEV, 3, 704, 2), F32))
    y_a, y_b, merged, mix, h1, xn2 = _merge_out_post(z, o_a, o_b, fw_a, fw_b, fw_out, x2, attn_post_norm,
                                                     ffn_pre_norm, 512)
    conv_par = [conv_full, conv_b]
    hu_g, hu_v, act, before1, before2 = _up_conv(xn2, fw_up_t, conv_full, conv_b, 512, min(st["conv"].tm, t))
    conv_saved = [before1[None], before2[None]]

    loss_acc, d_ffn_post, dh1, dff = _down_loss(act, fw_down, ffn_post_norm, h1, tgt, 512)
    dw_down = _mm("dw_down", act, dff, "tn", BF, tm=1408, tn=512)
    (dcw, dcb), dhu = _stage_bwd(st["conv"], t, conv_par, [hu_g, hu_v], conv_saved, [], [BF, BF],
                                 dout_dot=(dff, fw_down))
    dw_up_t = _mm_cols_tn("dw_up", dhu, xn2, BF, 1408)
    d_post, d_pre2, dx_a, dmix = _dxn2_post1_bwd(dhu, fw_up_t, x2, mix, dh1, attn_post_norm, ffn_pre_norm, 512)
    dga, dgb, dy_a, dy_b, do_a, do_b = _dmerged_merge_bwd(dmix, fw_out, fw_a, fw_b, z, y_a, y_b, 512)
    dw_a, dw_b, dw_out = _mm_multi("dw_branches", [(o_a, dy_a), (o_b, dy_b), (merged, dmix)], "tn", BF)
    early = [dw_up_t.reshape(N_DEV, 704, D), dw_down.reshape(N_DEV, 352, D), dw_a.reshape(N_DEV, 128, D),
             dw_b.reshape(N_DEV, 128, D), dw_out.reshape(N_DEV, 128, D), _cols_to_blocks(dcw.astype(BF), 704)]
    early_scatter = _Exchange("scatter", early)
    mix_dp, dz_hr = _stage_bwd(st["mixers"], t, mix_par, mix_in, mix_saved, [[do_a], [do_b]], [BF],
                               hook=early_scatter)
    d_lb, d_gn, d_mu, d_w0, d_w2p, d_a0, d_a2p, d_g2, d_kk, d_ka, d_lnw, d_lnb, d_rk = mix_dp
    dz = dz_hr + [dga, dgb]
    dw_in_t = _mm_cols_tn("dw_in", dz, xn, BF, 256)

    ax, ay, ac = lax.axis_index("x"), lax.axis_index("y"), lax.axis_index("c")
    idx4 = jnp.stack([4 * cx + 2 * cy + ac for cx, cy in ((ax, ay), (1 - ax, ay), (ax, 1 - ay), (1 - ax, 1 - ay))])
    idx4 = idx4.astype(jnp.int32)
    idx_me, idx_0 = idx4[0:1], jnp.zeros((1,), jnp.int32)
    d_small = jnp.concatenate([d_w2p[:64], d_a2p[64:], d_g2], axis=0).astype(BF)
    g8s = [dw_in_t.reshape(N_DEV, 1184, D), _cols_to_blocks(d_small, LANES)]
    recv4s = _reduce_pair(g8s)
    sums = [_pair_sum("pair_sum_" + n, idx4, g, r) for n, g, r in zip(("w_in", "small"), g8s, recv4s)]
    swap_ssem, swap_rsem, swap_srcs, swap_lands, token = _chip_swap_start([s[1] for s in sums])
    d_pre1, dx = _dxn_pre1_bwd(dz, fw_in_t, x2, dx_a, attn_pre_norm, 256, token)
    grad_x = dx.reshape(x.shape)

    sh_out = [dict() for _ in range(4)]
    done = []
    for n, own, recv in zip(_BIG[1:] + ("conv_w",), early, early_scatter.results):
        tr = (lambda a: a.T) if n == "w_up" else (lambda a: a)
        res = _adam_sharded("adam_" + n, idx_me, own, recv, *[tr(src[n][0]) for src in (w, mo, vo)], after=token)
        done.append(res[0])
        for kind in range(4):
            sh_out[kind][n] = tr(res[kind])[None]

    rg = dict(attn_pre_norm=d_pre1, hgrn_lb=d_lb, hgrn_gnorm=d_gn, rwkv_mu=d_mu, rwkv_w0=d_w0, rwkv_a0=d_a0,
              rwkv_k_k=d_kk, rwkv_k_a=d_ka, rwkv_r_k=d_rk, rwkv_ln_w=d_lnw, rwkv_ln_b=d_lnb, attn_post_norm=d_post,
              ffn_pre_norm=d_pre2, conv_b=dcb, ffn_post_norm=d_ffn_post)
    g8 = _all_gather_small("gather_small_grads", _pack_replicated(rg, loss_acc, done))
    rnames = [n for n, _ in REPL]
    flat = lambda src: [src[n].reshape(1, D) if n == "rwkv_r_k" else src[n] for n in rnames]
    rp_out, loss_row = _adam_replicated(g8, flat(w), flat(mo), flat(vo))
    loss = loss_row[0, 0]
    recv3s = _chip_swap_wait(swap_ssem, swap_rsem, swap_srcs, swap_lands, rp_out[0]["attn_pre_norm"])
    for kind in range(4):
        rp_out[kind]["rwkv_r_k"] = rp_out[kind]["rwkv_r_k"].reshape(rwkv_r_k.shape)

    def small_of(src):
        return jnp.concatenate([src["rwkv_w2"][0], src["rwkv_a2"][0], src["rwkv_g2"][0]], axis=0)

    res = _adam_sharded("adam_w_in", idx_0, sums[0][0][None], recv3s[0], *[src["w_in"][0].T for src in (w, mo, vo)])
    res_s = _adam_sharded("adam_small", idx_0, sums[1][0][None], recv3s[1], *[small_of(src) for src in (w, mo, vo)])
    for kind in range(4):
        sh_out[kind]["w_in"] = res[kind].T[None]
        sh_out[kind]["rwkv_w2"] = res_s[kind][0:64][None]
        sh_out[kind]["rwkv_a2"] = res_s[kind][64:128][None]
        sh_out[kind]["rwkv_g2"] = res_s[kind][128:256][None]

    outs = [loss, grad_x]
    for kind in range(4):
        for name in _WEIGHTS:
            outs.append(sh_out[kind][name] if name in sh_out[kind] else rp_out[kind][name])
    return tuple(outs)
```

```python
import functools

import jax
import jax.numpy as jnp
from jax import lax
from jax.experimental import pallas as pl
from jax.experimental.pallas import tpu as pltpu

F32 = jnp.float32
BF = jnp.bfloat16
MESH = pl.DeviceIdType.MESH

D = 1024
HG_HEADS = 8
HG_K = 128
HG_CHUNK = 32
HG_SCALE = HG_K ** -0.5
HG_PER_STEP = 8
RW_HEADS = 16
RW_N = 64
RW_CHUNK = 64
RW_PAIRS_PER_STEP = 8
DFF = 2816
IN_COLS = 9472
RW_COLS = 3328
EPS = 1e-6
GN_EPS = 1e-5 * RW_N
ADAM_LR = 0.001
ADAM_B1 = 0.9
ADAM_B2 = 0.999
ADAM_EPS = 1e-08
ADAM_WD = 0.01
ADAM_STEP = 10
N_DEV = 8
LANES = 128
SUBLANES = 8
VMEM_LIMIT = 56 * 1024 * 1024
TILE_BYTES = 1280 * 1024

REPL = (("attn_pre_norm", 1024), ("hgrn_lb", 1024), ("hgrn_gnorm", 1024), ("rwkv_mu", 3328), ("rwkv_w0", 1024),
        ("rwkv_a0", 1024), ("rwkv_k_k", 1024), ("rwkv_k_a", 1024), ("rwkv_r_k", 1024), ("rwkv_ln_w", 1024),
        ("rwkv_ln_b", 1024), ("attn_post_norm", 1024), ("ffn_pre_norm", 1024), ("conv_b", 5632), ("ffn_post_norm", 1024))
REPL_ROWS = {"hgrn_lb": 2}
REPL_TOTAL = 32


def _cparams(sem=None, **kw):
    return pltpu.CompilerParams(dimension_semantics=sem, vmem_limit_bytes=VMEM_LIMIT, **kw)


_DN = {"nn": ((1,), (0,)), "nt": ((1,), (1,)), "tn": ((0,), (0,))}


def _raw_dot(a, b, mode):
    return lax.dot_general(a.astype(BF), b.astype(BF), (_DN[mode], ((), ())), preferred_element_type=F32)


@functools.partial(jax.custom_vjp, nondiff_argnums=(2,))
def _dot(a, b, mode):
    return _raw_dot(a, b, mode)


def _dot_fwd(a, b, mode):
    return _raw_dot(a, b, mode), (a, b)


def _dot_bwd(mode, res, g):
    a, b = res
    if mode == "nn":
        return _dot(g, b, "nt"), _dot(a, g, "tn")
    if mode == "nt":
        return _dot(g, b, "nn"), _dot(g, a, "tn")
    return _dot(b, g, "nt"), _dot(a, g, "nn")


_dot.defvjp(_dot_fwd, _dot_bwd)


def _bf_pieces(x, n):
    out, r = [], x
    for i in range(n):
        p = r.astype(BF)
        out.append(p)
        if i + 1 < n:
            r = r - p.astype(F32)
    return out


def _raw_split_dot(x, e, mode, n, x_left):
    eb = e.astype(BF)
    acc = None
    for p in _bf_pieces(x, n):
        ops = (p, eb) if x_left else (eb, p)
        t = lax.dot_general(*ops, (_DN[mode], ((), ())), preferred_element_type=F32)
        acc = t if acc is None else acc + t
    return acc


def _raw_headsum(x):
    t = x.shape[0]
    i = lax.broadcasted_iota(jnp.int32, (LANES, LANES), 0)
    j = lax.broadcasted_iota(jnp.int32, (LANES, LANES), 1)
    same = jnp.where((i >= RW_N) == (j >= RW_N), 1.0, 0.0).astype(F32)
    groups = x.shape[1] // LANES
    rows = jnp.concatenate([x[:, q * LANES:(q + 1) * LANES] for q in range(groups)], axis=0)
    s = _raw_split_dot(rows, same, "nn", 2, True)
    return jnp.concatenate([s[q * t:(q + 1) * t] for q in range(groups)], axis=1)


@jax.custom_vjp
def _headsum(x):
    return _raw_headsum(x)


def _headsum_fwd(x):
    return _raw_headsum(x), None


def _headsum_bwd(_, g):
    return (_raw_headsum(g),)


_headsum.defvjp(_headsum_fwd, _headsum_bwd)


@functools.partial(jax.custom_vjp, nondiff_argnums=(2,))
def _tdot(tri, x, n):
    return _raw_split_dot(x, tri, "nn", n, False)


def _tdot_fwd(tri, x, n):
    return _raw_split_dot(x, tri, "nn", n, False), tri


def _tdot_bwd(n, tri, g):
    return jnp.zeros_like(tri), _raw_split_dot(g, tri, "tn", n, False)


_tdot.defvjp(_tdot_fwd, _tdot_bwd)


def _row(x, i):
    r = lax.broadcasted_iota(jnp.int32, x.shape, 0)
    return jnp.sum(jnp.where(r == i, x, 0.0), axis=0, keepdims=True)


def _shift_down(x, prev):
    t = x.shape[0]

    @jax.custom_vjp
    def sh(x, prev):
        r = lax.broadcasted_iota(jnp.int32, x.shape, 0)
        return jnp.where(r == 0, prev, pltpu.roll(x, 1, 0))

    def fwd(x, prev):
        return sh(x, prev), None

    def bwd(_, g):
        r = lax.broadcasted_iota(jnp.int32, g.shape, 0)
        dx = jnp.where(r == t - 1, 0.0, pltpu.roll(g, t - 1, 0))
        return dx, jnp.sum(jnp.where(r == 0, g, 0.0), axis=0, keepdims=True)

    sh.defvjp(fwd, bwd)
    return sh(x, prev)


def _sigmoid(x):
    return jax.nn.sigmoid(x)


@jax.custom_vjp
def _silu(x):
    return x * jax.nn.sigmoid(x)


def _silu_fwd(x):
    s = jax.nn.sigmoid(x)
    return x * s, (x, s)


def _silu_bwd(res, g):
    x, s = res
    return (g * (s * (1.0 + x * (1.0 - s))),)


_silu.defvjp(_silu_fwd, _silu_bwd)


def _softplus(x):
    return jnp.maximum(x, 0.0) + jnp.log(1.0 + jnp.exp(-jnp.abs(x)))


def _rms(x, g):
    return (x * lax.rsqrt(jnp.mean(x * x, axis=-1, keepdims=True) + EPS)) * g


def _tril(c):
    r = lax.broadcasted_iota(jnp.int32, (c, c), 0)
    cc = lax.broadcasted_iota(jnp.int32, (c, c), 1)
    return cc <= r


def _f_pre1_residual(ps, xs, cs):
    return [_rms(xs[0], ps[0]), xs[0]], []


def _f_hgrn(ps, xs, cs):
    lbraw, gn = ps
    hq, hf, hi, hg = xs
    hd = range(HG_PER_STEP)
    st = [cs[0][p * HG_K:(p + 1) * HG_K] for p in hd]
    l0, l1 = _row(lbraw, 0), _row(lbraw, 1)
    m = jnp.maximum(l0, l1)
    e0, e1 = jnp.exp(l0 - m), jnp.exp(l1 - m)
    lb = e0 / (e0 + e1)
    q = _silu(hq) * HG_SCALE
    f = lb + (1.0 - lb) * _sigmoid(hf)
    kh = 1.0 - f
    gl = jnp.log(f)
    c = HG_CHUNK
    low = _tril(c)
    tri = jnp.where(low, 1.0, 0.0).astype(F32)
    outs = []
    for i in range(hq.shape[0] // c):
        rows = slice(i * c, (i + 1) * c)
        b = _tdot(tri, gl[rows], 3)
        bref = _row(b, c // 2 - 1)
        blast = _row(b, c - 1)
        qi = q[rows] * jnp.exp(b - bref)
        ki = kh[rows] * jnp.exp(bref - b)
        qd = q[rows] * jnp.exp(b)
        kd = kh[rows] * jnp.exp(blast - b)
        dec = jnp.exp(blast)
        sl = [slice(p * HG_K, (p + 1) * HG_K) for p in hd]
        sc = [jnp.where(low, _dot(qi[:, sl[p]], ki[:, sl[p]], "nt"), 0.0) for p in hd]
        o = [_dot(sc[p], hi[rows, sl[p]], "nn") + _dot(qd[:, sl[p]], st[p], "nt") for p in hd]
        u = [_dot(hi[rows, sl[p]], kd[:, sl[p]], "tn") for p in hd]
        st = [dec[:, sl[p]] * st[p] + u[p] for p in hd]
        outs.append(jnp.concatenate(o, axis=1) if len(o) > 1 else o[0])
    o = outs[0] if len(outs) == 1 else jnp.concatenate(outs, axis=0)
    on = []
    for p in hd:
        op = o[:, p * HG_K:(p + 1) * HG_K]
        on.append(op * lax.rsqrt(jnp.mean(op * op, axis=-1, keepdims=True) + EPS))
    o = jnp.concatenate(on, axis=1) if len(on) > 1 else on[0]
    o = o * gn
    return [o * _silu(hg)], [jnp.concatenate(st, axis=0) if len(st) > 1 else st[0]]


_RW_OFFS = (0, 1024, 2048, 3072, 3200, 3328)


def _f_rwpre(ps, xs, cs):
    mu, w0, w2p, a0, a2p, g2, k_k, k_a = ps
    (prev,) = cs
    t = xs[0].shape[0]
    zs = []
    for i, z in enumerate(xs):
        lo, hi = _RW_OFFS[i], _RW_OFFS[i + 1]
        zs.append(z + mu[:, lo:hi] * (_shift_down(z, prev[:, lo:hi]) - z))
    rr, kr, vr, wa, gz = zs
    w_log = -_softplus(-(w0 + _dot(jnp.tanh(wa), w2p, "nn"))) - 0.5
    lw = -jnp.exp(w_log)
    a = _sigmoid(a0 + _dot(wa, a2p, "nn"))
    g = _dot(_sigmoid(gz), g2, "nn")
    kkr = kr * k_k
    kk = kkr / jnp.maximum(jnp.sqrt(_headsum(kkr * kkr)), 1e-12)
    k2 = kr * (1.0 + (a - 1.0) * k_a)
    newprev = jnp.concatenate([_row(z, t - 1) for z in xs], axis=1)
    return [rr, lw, k2, vr, -kk, kk * a, g], [newprev]


def _raw_inverses(ls):
    n = ls[0].shape[0]
    r = lax.broadcasted_iota(jnp.int32, (n, n), 0)
    c = lax.broadcasted_iota(jnp.int32, (n, n), 1)
    eye = jnp.where(r == c, 1.0, 0.0).astype(F32)
    tinv = [eye + l for l in ls]
    pw = ls
    for _ in range(5):
        pw = [_raw_dot(p, p, "nn") for p in pw]
        tinv = [t + _raw_dot(t, p, "nn") for t, p in zip(tinv, pw)]
    return tinv


@jax.custom_vjp
def _unit_lower_inverses(ls):
    return _raw_inverses(ls)


def _inverses_fwd(ls):
    tinv = _raw_inverses(ls)
    return tinv, tinv


def _inverses_bwd(tinv, gs):
    return ([_raw_dot(_raw_dot(t, g, "tn"), t, "nt") for t, g in zip(tinv, gs)],)


_unit_lower_inverses.defvjp(_inverses_fwd, _inverses_bwd)


@jax.custom_vjp
def _known_inverses(ls, tinv):
    return tinv


def _known_fwd(ls, tinv):
    return tinv, tinv


def _known_bwd(tinv, gs):
    return [_raw_dot(_raw_dot(t, g, "tn"), t, "nt") for t, g in zip(tinv, gs)], [jnp.zeros_like(t) for t in tinv]


_known_inverses.defvjp(_known_fwd, _known_bwd)


@jax.custom_vjp
def _use_kept(computed, kept):
    return kept


def _use_kept_fwd(computed, kept):
    return kept, None


def _use_kept_bwd(_, g):
    return g, jax.tree.map(jnp.zeros_like, g)


_use_kept.defvjp(_use_kept_fwd, _use_kept_bwd)

RW_KEPT = 5


def _f_rwscan(ps, xs, cs, kept=None):
    state = cs[0]
    ys, keep = [], []
    n = 2 * RW_CHUNK
    per_chunk = RW_KEPT * RW_PAIRS_PER_STEP * n
    for i in range(xs[0].shape[0] // RW_CHUNK):
        known = None
        if kept is not None:
            known = [[kept[i * per_chunk + (q * RW_PAIRS_PER_STEP + p) * n:
                           i * per_chunk + (q * RW_PAIRS_PER_STEP + p + 1) * n] for p in range(RW_PAIRS_PER_STEP)]
                     for q in range(RW_KEPT)]
        y, state, mats = _rwkv_chunk([x[i * RW_CHUNK:(i + 1) * RW_CHUNK] for x in xs], state, known)
        ys.append(y)
        keep += [m for group in mats for m in group]
    return [ys[0] if len(ys) == 1 else jnp.concatenate(ys, axis=0)], [state], jnp.concatenate(keep, axis=0)


def _rwkv_chunk(xs, state, known=None):
    npair = RW_PAIRS_PER_STEP
    pr = range(npair)
    r, lw, k, v, av, bv = [[x[:, p * LANES:(p + 1) * LANES] for p in pr] for x in xs]
    sv = [state[p * LANES:(p + 1) * LANES] for p in pr]
    c = RW_CHUNK
    n = 2 * c
    tri = jnp.where(_tril(c), 1.0, 0.0).astype(F32)
    cl = [_tdot(tri, lw[p], 3) for p in pr]
    cl_last = [_row(cl[p], c - 1) for p in pr]
    lane = lax.broadcasted_iota(jnp.int32, (c, LANES), 1)
    h0 = lane < RW_N

    def stack(x):
        return jnp.concatenate([jnp.where(h0, x, 0.0), jnp.where(h0, 0.0, x)], axis=0)

    am = [stack(av[p] * jnp.exp(cl[p] - lw[p])) for p in pr]
    bm = [stack(bv[p] * jnp.exp(-cl[p])) for p in pr]
    km = [stack(k[p] * jnp.exp(-cl[p])) for p in pr]
    rm = [stack(r[p] * jnp.exp(cl[p])) for p in pr]
    vm = [stack(v[p]) for p in pr]
    rn = lax.broadcasted_iota(jnp.int32, (n, n), 0)
    cn = lax.broadcasted_iota(jnp.int32, (n, n), 1)
    blk = (rn >= c) == (cn >= c)
    strict = blk & (cn < rn)
    incl = blk & (cn <= rn)
    lab = [jnp.where(strict, _dot(am[p], bm[p], "nt"), 0.0) for p in pr]
    lak = [jnp.where(strict, _dot(am[p], km[p], "nt"), 0.0) for p in pr]
    wrb = [jnp.where(incl, _dot(rm[p], bm[p], "nt"), 0.0) for p in pr]
    wrk = [jnp.where(incl, _dot(rm[p], km[p], "nt"), 0.0) for p in pr]
    if known is None:
        tinv = _unit_lower_inverses(lab)
    else:
        tinv = _known_inverses(lab, known[0])
        lak, wrb, wrk = _use_kept(lak, known[1]), _use_kept(wrb, known[2]), _use_kept(wrk, known[3])
    rhs = [_dot(am[p], sv[p], "nt") + _dot(lak[p], vm[p], "nn") for p in pr]
    um = [_dot(tinv[p], rhs[p], "nn") for p in pr]
    if known is not None:
        um = _use_kept(um, known[4])
    ym = [_dot(rm[p], sv[p], "nt") + _dot(wrb[p], um[p], "nn") + _dot(wrk[p], vm[p], "nn") for p in pr]
    sn = [(sv[p] + _dot(um[p], bm[p], "tn") + _dot(vm[p], km[p], "tn")) * jnp.exp(cl_last[p]) for p in pr]
    ys = [ym[p][:c] + ym[p][c:] for p in pr]
    return jnp.concatenate(ys, axis=1), jnp.concatenate(sn, axis=0), [tinv, lak, wrb, wrk, um]


def _f_mixers(ps, xs, cs):
    return _mixers(ps, xs, cs, None)


def _f_mixers_kept(ps, xs, cs, kept):
    return _mixers(ps, xs, cs, kept[0])[:2]


def _mixers(ps, xs, cs, kept):
    oa, st = _f_hgrn(ps[:2], xs[:4], cs[:1])
    (r, lw, k, v, av, bv, g), prev = _f_rwpre(ps[2:10], xs[4:], cs[1:2])
    y, sv, keep = _f_rwscan([], [r, lw, k, v, av, bv], cs[2:], kept)
    ob, _ = _f_rwpost(ps[10:], y + [r, k, v, g], [])
    return oa + ob, st + prev + sv, [keep]


def _f_rwpost(ps, xs, cs):
    ln_w, ln_b, r_k = ps
    y, r, k, v, g = xs
    inv_n = 1.0 / RW_N
    yc = y - _headsum(y) * inv_n
    var = _headsum(yc * yc) * inv_n
    yn = yc * lax.rsqrt(var + GN_EPS)
    yn = yn * ln_w + ln_b
    bonus = _headsum(r * k * r_k) * v
    return [(yn + bonus) * g], []


def _f_merge(ps, xs, cs):
    ga, gb, ya, yb = xs
    return [_sigmoid(ga) * ya + _sigmoid(gb) * yb], []


def _f_post1(ps, xs, cs):
    x, mix = xs
    h1 = x + _rms(mix, ps[0])
    return [h1, _rms(h1, ps[1])], []


def _f_conv(ps, xs, cs):
    cw, cb = ps
    p1, p2 = cs
    w0, w1, w2 = _row(cw, 0), _row(cw, 1), _row(cw, 2)
    t = xs[0].shape[0]
    hc = []
    for i, x in enumerate(xs):
        sl = slice(i * DFF, (i + 1) * DFF)
        s1 = _shift_down(x, p1[:, sl])
        s2 = _shift_down(s1, p2[:, sl])
        hc.append(cb[:, sl] + w0[:, sl] * s2 + w1[:, sl] * s1 + w2[:, sl] * x)
    n1 = jnp.concatenate([_row(x, t - 1) for x in xs], axis=1)
    n2 = jnp.concatenate([_row(x, t - 2) for x in xs], axis=1)
    return [_silu(hc[0]) * hc[1]], [n1, n2]


class _Stage:
    def __init__(self, name, f, g, tm, par_per_g, in_pieces, in_offs, carry_shapes, out_pieces, out_dtypes,
                 kept_shapes=(), f_kept=None):
        self.name, self.f, self.g, self.tm = name, f, g, tm
        self.par_per_g, self.in_pieces, self.in_offs = par_per_g, in_pieces, in_offs
        self.carry_shapes, self.out_pieces, self.out_dtypes = carry_shapes, out_pieces, out_dtypes
        self.kept_shapes, self.f_kept = list(kept_shapes), f_kept


def _par_spec(arr, per_g, g):
    r, c = arr.shape
    if per_g:
        return pl.BlockSpec((r, c // g), lambda gi, ni: (0, gi))
    return pl.BlockSpec((r, c), lambda gi, ni: (0, 0))


def _row_spec(tm, width, off, n, rev):
    if rev:
        return pl.BlockSpec((tm, width), lambda gi, ni: (n - 1 - ni, off + gi))
    return pl.BlockSpec((tm, width), lambda gi, ni: (ni, off + gi))


def _carry_spec(shape, n, rev):
    if rev:
        return pl.BlockSpec((None, None) + shape, lambda gi, ni: (gi, n - 1 - ni, 0, 0))
    return pl.BlockSpec((None, None) + shape, lambda gi, ni: (gi, ni, 0, 0))


def _load_pieces(refs, pieces_list):
    out = []
    for ref, pieces in zip(refs, pieces_list):
        o = 0
        for w in pieces:
            out.append(ref[:, o:o + w].astype(F32))
            o += w
    return out


def _store_pieces(refs, pieces_list, vals):
    k = 0
    for ref, pieces in zip(refs, pieces_list):
        o = 0
        for w in pieces:
            ref[:, o:o + w] = vals[k].astype(ref.dtype)
            k += 1
            o += w


_ANY = pl.BlockSpec(memory_space=pl.ANY)


class _Exchange:
    def __init__(self, kind, arrs):
        self.kind, self.arrs, self.results = kind, list(arrs), None
        if kind == "scatter":
            self.out_shape = [jax.ShapeDtypeStruct((N_DEV - 1,) + a.shape[1:], a.dtype) for a in self.arrs]
        else:
            self.out_shape = [jax.ShapeDtypeStruct((N_DEV,) + a.shape, a.dtype) for a in self.arrs]
        self.nsem = (N_DEV if kind == "gather2" else N_DEV - 1) * len(self.arrs)

    def copies(self, in_refs, out_refs, ssem, rsem):
        x, y, c = lax.axis_index("x"), lax.axis_index("y"), lax.axis_index("c")
        me = 4 * x + 2 * y + c
        cps = []
        for a, (i_ref, o_ref) in enumerate(zip(in_refs, out_refs)):
            for j in range(1, N_DEV):
                px = 1 - x if j & 4 else x
                py = 1 - y if j & 2 else y
                pc = 1 - c if j & 1 else c
                if self.kind == "gather":
                    src, dst = i_ref, o_ref.at[me]
                else:
                    src, dst = i_ref.at[4 * px + 2 * py + pc], o_ref.at[j - 1]
                s = (N_DEV - 1) * a + j - 1
                cps.append(pltpu.make_async_remote_copy(src_ref=src, dst_ref=dst, send_sem=ssem.at[s],
                                                        recv_sem=rsem.at[s], device_id=(px, py, pc),
                                                        device_id_type=MESH))
        return cps

    def run(self, step, total, in_refs, out_refs, ssem, rsem):
        if self.kind == "gather2":
            return self.run_two_level(step, total, in_refs, out_refs, ssem, rsem)

        @pl.when(step == 0)
        def _():
            for cp in self.copies(in_refs, out_refs, ssem, rsem):
                cp.start()

        @pl.when(step == total - 1)
        def _():
            for cp in self.copies(in_refs, out_refs, ssem, rsem):
                cp.wait()

    def run_two_level(self, step, total, in_refs, out_refs, ssem, rsem):
        x, y, c = lax.axis_index("x"), lax.axis_index("y"), lax.axis_index("c")
        sibling, xn, yn = (x, y, 1 - c), (1 - x, y, c), (x, 1 - y, c)
        arrs = range(len(in_refs))
        ns = N_DEV

        def num(px, py, pc):
            return 4 * px + 2 * py + pc

        def copy(a, k, to, src, dst):
            return pltpu.make_async_remote_copy(src_ref=src, dst_ref=dst, send_sem=ssem.at[ns * a + k],
                                                recv_sem=rsem.at[ns * a + k], device_id=to, device_id_type=MESH)

        def blk(a, b):
            return out_refs[a].at[b]

        def half(a, b, second):
            h = self.arrs[a].shape[0] // 2
            return out_refs[a].at[b, pl.ds(h if second else 0, h)]

        bx, by, bd = num(1 - x, y, c), num(x, 1 - y, c), num(1 - x, 1 - y, c)

        def firsts(a):
            own = blk(a, num(x, y, c))
            return [copy(a, 0, sibling, in_refs[a], own), copy(a, 1, xn, in_refs[a], own),
                    copy(a, 2, yn, in_refs[a], own)]

        def seconds(a):
            return [copy(a, 3, yn, half(a, bx, False), half(a, bx, False)), copy(a, 5, sibling, blk(a, bx), blk(a, bx)),
                    copy(a, 4, xn, half(a, by, True), half(a, by, True)), copy(a, 6, sibling, blk(a, by), blk(a, by))]

        def third(a):
            return copy(a, 7, sibling, blk(a, bd), blk(a, bd))

        @pl.when(step == 0)
        def _():
            for a in arrs:
                for cp in firsts(a):
                    cp.start()

        @pl.when(step == total // 2)
        def _():
            for a in arrs:
                copy(a, 1, xn, blk(a, bx), blk(a, bx)).wait_recv()
                copy(a, 2, yn, blk(a, by), blk(a, by)).wait_recv()
                for cp in seconds(a):
                    cp.start()

        @pl.when(step == (4 * total) // 5)
        def _():
            for a in arrs:
                copy(a, 3, yn, half(a, bd, False), half(a, bd, False)).wait_recv()
                copy(a, 4, xn, half(a, bd, True), half(a, bd, True)).wait_recv()
                third(a).start()

        @pl.when(step == total - 1)
        def _():
            for a in arrs:
                for k, b in ((0, num(x, y, 1 - c)), (5, num(1 - x, y, 1 - c)), (6, num(x, 1 - y, 1 - c)),
                             (7, num(1 - x, 1 - y, 1 - c))):
                    copy(a, k, sibling, blk(a, b), blk(a, b)).wait_recv()
                for cp in firsts(a) + seconds(a) + [third(a)]:
                    cp.wait_send()


def _hook_specs(hook):
    if hook is None:
        return [], [], [], []
    na = len(hook.arrs)
    sems = [pltpu.SemaphoreType.DMA((hook.nsem,)), pltpu.SemaphoreType.DMA((hook.nsem,))]
    return [_ANY] * na, [_ANY] * na, hook.out_shape, sems


def _stage_fwd(st, t, params, inputs, hook=None):
    g, tm = st.g, min(st.tm, t)
    n = t // tm
    npar, nin, ncar, nout = len(params), len(inputs), len(st.carry_shapes), len(st.out_pieces)
    nk = len(st.kept_shapes)
    h_in, h_out, h_shape, h_sems = _hook_specs(hook)
    nh = len(h_in)

    def body(*refs):
        p_refs = refs[:npar]
        x_refs = refs[npar:npar + nin]
        hi_refs = refs[npar + nin:npar + nin + nh]
        o = npar + nin + nh
        o_refs = refs[o:o + nout]
        s_refs = refs[o + nout:o + nout + ncar]
        k_refs = refs[o + nout + ncar:o + nout + ncar + nk]
        o += nout + ncar + nk
        ho_refs = refs[o:o + nh]
        c_scr = refs[o + nh:o + nh + ncar]
        gi, ni = pl.program_id(0), pl.program_id(1)
        if hook is not None:
            step = gi * n + ni
            hook.run(step, g * n, hi_refs, ho_refs, *refs[-2:])

        @pl.when(ni == 0)
        def _():
            for c in c_scr:
                c[...] = jnp.zeros(c.shape, F32)

        ps = [r[...].astype(F32) for r in p_refs]
        xs = _load_pieces(x_refs, st.in_pieces)
        cs = [c[...] for c in c_scr]
        for s, c in zip(s_refs, cs):
            s[...] = c
        res = st.f(ps, xs, cs)
        outs, ncs = res[0], res[1]
        _store_pieces(o_refs, st.out_pieces, outs)
        for c, v in zip(c_scr, ncs):
            c[...] = v
        for kr, kv in zip(k_refs, res[2] if nk else []):
            kr[...] = kv.astype(kr.dtype)

    in_specs = [_par_spec(p, pg, g) for p, pg in zip(params, st.par_per_g)]
    in_specs += [_row_spec(tm, sum(pc), off, n, False) for pc, off in zip(st.in_pieces, st.in_offs)]
    out_specs = [_row_spec(tm, sum(pc), 0, n, False) for pc in st.out_pieces]
    out_specs += [_carry_spec(s, n, False) for s in st.carry_shapes]
    out_specs += [pl.BlockSpec(s, lambda gi, ni: (ni, 0)) for s in st.kept_shapes]
    out_shape = [jax.ShapeDtypeStruct((t, g * sum(pc)), dt) for pc, dt in zip(st.out_pieces, st.out_dtypes)]
    out_shape += [jax.ShapeDtypeStruct((g, n) + s, F32) for s in st.carry_shapes]
    out_shape += [jax.ShapeDtypeStruct((n * s[0], s[1]), BF) for s in st.kept_shapes]
    res = pl.pallas_call(
        body, name=st.name + "_fwd", grid=(g, n), in_specs=in_specs + h_in, out_specs=out_specs + h_out,
        out_shape=out_shape + h_shape,
        scratch_shapes=[pltpu.VMEM(s, F32) for s in st.carry_shapes] + h_sems,
        compiler_params=_cparams(("arbitrary", "arbitrary")),
    )(*params, *inputs, *(hook.arrs if hook else []))
    if hook is not None:
        hook.results = list(res[nout + ncar + nk:])
    return list(res[:nout]), list(res[nout:nout + ncar + nk])


def _stage_bwd(st, t, params, inputs, saved, douts, dx_dtypes, hook=None, dout_dot=None):
    g, tm = st.g, min(st.tm, t)
    n = t // tm
    npar, nin, ncar = len(params), len(inputs), len(st.carry_shapes)
    nk = len(st.kept_shapes)
    flat_d = list(dout_dot) if dout_dot is not None else [d for ds in douts for d in ds]
    nd = len(flat_d)
    dx_idx = [i for i, dt in enumerate(dx_dtypes) if dt is not None]
    h_in, h_out, h_shape, h_sems = _hook_specs(hook)
    nh = len(h_in)

    def body(*refs):
        p_refs = refs[:npar]
        x_refs = refs[npar:npar + nin]
        s_refs = refs[npar + nin:npar + nin + ncar]
        k_refs = refs[npar + nin + ncar:npar + nin + ncar + nk]
        o = npar + nin + ncar + nk
        d_refs = refs[o:o + nd]
        hi_refs = refs[o + nd:o + nd + nh]
        o += nd + nh
        dp_refs = refs[o:o + npar]
        dx_refs = refs[o + npar:o + npar + len(dx_idx)]
        ho_refs = refs[o + npar + len(dx_idx):o + npar + len(dx_idx) + nh]
        dc_scr = refs[o + npar + len(dx_idx) + nh:o + npar + len(dx_idx) + nh + ncar]
        gi, ni = pl.program_id(0), pl.program_id(1)
        if hook is not None:
            step = gi * n + ni
            hook.run(step, g * n, hi_refs, ho_refs, *refs[-2:])

        @pl.when(ni == 0)
        def _():
            for c in dc_scr:
                c[...] = jnp.zeros(c.shape, F32)

        ps = [r[...].astype(F32) for r in p_refs]
        xs = _load_pieces(x_refs, st.in_pieces)
        cs = [s[...] for s in s_refs]
        dys = [_raw_dot(d_refs[0][...], d_refs[1][...], "nt")] if dout_dot is not None else []
        k = 0
        for ds, pieces in zip(douts, st.out_pieces):
            acc = _load_pieces([d_refs[k]], [pieces])
            for j in range(1, len(ds)):
                more = _load_pieces([d_refs[k + j]], [pieces])
                acc = [a + b for a, b in zip(acc, more)]
            dys += acc
            k += len(ds)
        if nk:
            kept = [r[...].astype(F32) for r in k_refs]
            _, vjp = jax.vjp(lambda p, x, c: st.f_kept(p, x, c, kept), ps, xs, cs)
        else:
            _, vjp = jax.vjp(st.f, ps, xs, cs)
        dps, dxs, dcs = vjp((dys, [c[...] for c in dc_scr]))
        k = 0
        per_in = []
        for pieces in st.in_pieces:
            per_in.append(dxs[k:k + len(pieces)])
            k += len(pieces)
        for ref, i in zip(dx_refs, dx_idx):
            _store_pieces([ref], [st.in_pieces[i]], per_in[i])
        for c, v in zip(dc_scr, dcs):
            c[...] = v
        for ref, dp, pg in zip(dp_refs, dps, st.par_per_g):
            first = (ni == 0) if pg else ((ni == 0) & (gi == 0))

            @pl.when(first)
            def _():
                ref[...] = jnp.zeros(ref.shape, F32)

            ref[...] += dp

    in_specs = [_par_spec(p, pg, g) for p, pg in zip(params, st.par_per_g)]
    in_specs += [_row_spec(tm, sum(pc), off, n, True) for pc, off in zip(st.in_pieces, st.in_offs)]
    in_specs += [_carry_spec(s, n, True) for s in st.carry_shapes]
    in_specs += [pl.BlockSpec(s, lambda gi, ni: (n - 1 - ni, 0)) for s in st.kept_shapes]
    for ds, pc in zip(douts, st.out_pieces):
        in_specs += [_row_spec(tm, sum(pc), 0, n, True) for _ in ds]
    if dout_dot is not None:
        a, w = dout_dot
        in_specs += [pl.BlockSpec((tm, a.shape[1]), lambda gi, ni: (n - 1 - ni, 0)),
                     pl.BlockSpec(w.shape, lambda gi, ni: (0, 0), pipeline_mode=pl.Buffered(1))]
    out_specs = [_par_spec(p, pg, g) for p, pg in zip(params, st.par_per_g)]
    out_specs += [_row_spec(tm, sum(st.in_pieces[i]), 0, n, True) for i in dx_idx]
    out_shape = [jax.ShapeDtypeStruct(p.shape, F32) for p in params]
    out_shape += [jax.ShapeDtypeStruct((t, g * sum(st.in_pieces[i])), dx_dtypes[i]) for i in dx_idx]
    res = pl.pallas_call(
        body, name=st.name + "_bwd", grid=(g, n), in_specs=in_specs + h_in, out_specs=out_specs + h_out,
        out_shape=out_shape + h_shape,
        scratch_shapes=[pltpu.VMEM(s, F32) for s in st.carry_shapes] + h_sems,
        compiler_params=_cparams(("arbitrary", "arbitrary")),
    )(*params, *inputs, *saved, *flat_d, *(hook.arrs if hook else []))
    if hook is not None:
        hook.results = list(res[npar + len(dx_idx):])
    return list(res[:npar]), list(res[npar:npar + len(dx_idx)])


def _pick(n, cap):
    if n <= cap:
        return n
    best = LANES
    for k in range(1, n // LANES + 1):
        if (n // LANES) % k == 0 and k * LANES <= cap:
            best = k * LANES
    return best


def _mm(name, a, b, mode, out_dtype=F32, tm=1024, tn=512, b_outer=False):
    m = a.shape[1] if mode == "tn" else a.shape[0]
    k = a.shape[0] if mode == "tn" else a.shape[1]
    n = b.shape[0] if mode == "nt" else b.shape[1]
    tm, tn = _pick(m, tm), _pick(n, tn)
    if b_outer:
        grid = (n // tn, m // tm)
        ij = lambda p, q: (q, p)
    else:
        grid = (m // tm, n // tn)
        ij = lambda p, q: (p, q)

    def body(a_ref, b_ref, o_ref):
        o_ref[...] = _raw_dot(a_ref[...], b_ref[...], mode).astype(o_ref.dtype)

    if mode == "tn":
        a_spec = pl.BlockSpec((k, tm), lambda p, q: (0, ij(p, q)[0]))
    else:
        a_spec = pl.BlockSpec((tm, k), lambda p, q: (ij(p, q)[0], 0))
    b_mode = dict(pipeline_mode=pl.Buffered(1)) if tn == n else {}
    if mode == "nt":
        b_spec = pl.BlockSpec((tn, k), lambda p, q: (ij(p, q)[1], 0), **b_mode)
    else:
        b_spec = pl.BlockSpec((k, tn), lambda p, q: (0, ij(p, q)[1]), **b_mode)
    return pl.pallas_call(
        body, name=name, grid=grid, in_specs=[a_spec, b_spec],
        out_specs=pl.BlockSpec((tm, tn), lambda p, q: ij(p, q)),
        out_shape=jax.ShapeDtypeStruct((m, n), out_dtype),
        compiler_params=_cparams(("arbitrary", "arbitrary")),
    )(a, b)


def _mm_multi(name, pairs, mode, out_dtype, tm=1024, tn=512):
    a0, b0 = pairs[0]
    m = a0.shape[1] if mode == "tn" else a0.shape[0]
    k = a0.shape[0] if mode == "tn" else a0.shape[1]
    n = b0.shape[0] if mode == "nt" else b0.shape[1]
    tm, tn = _pick(m, tm), _pick(n, tn)
    npair = len(pairs)

    def body(*refs):
        for p in range(npair):
            refs[2 * npair + p][...] = _raw_dot(refs[2 * p][...], refs[2 * p + 1][...], mode).astype(out_dtype)

    a_spec = pl.BlockSpec((k, tm), lambda i, j: (0, i)) if mode == "tn" else pl.BlockSpec((tm, k), lambda i, j: (i, 0))
    b_spec = pl.BlockSpec((tn, k), lambda i, j: (j, 0)) if mode == "nt" else pl.BlockSpec((k, tn), lambda i, j: (0, j))
    return pl.pallas_call(
        body, name=name, grid=(m // tm, n // tn), in_specs=[a_spec, b_spec] * npair,
        out_specs=[pl.BlockSpec((tm, tn), lambda i, j: (i, j))] * npair,
        out_shape=[jax.ShapeDtypeStruct((m, n), out_dtype)] * npair,
        compiler_params=_cparams(("arbitrary", "arbitrary")),
    )(*[x for pair in pairs for x in pair])


def _mm_cols_tn(name, pieces, b, out_dtype, tm):
    k, n = b.shape
    counts = [p.shape[1] // tm for p in pieces]
    starts = [sum(counts[:i]) for i in range(len(pieces))]
    na = len(pieces)

    def body(*refs):
        b_ref, o_ref = refs[na], refs[-1]
        i = pl.program_id(0)
        for a_ref, s, c in zip(refs[:na], starts, counts):
            @pl.when((i >= s) & (i < s + c))
            def _():
                o_ref[...] = _raw_dot(a_ref[...], b_ref[...], "tn").astype(o_ref.dtype)

    def spec(s, c):
        return pl.BlockSpec((k, tm), lambda i: (0, jnp.clip(i - s, 0, c - 1)))

    return pl.pallas_call(
        body, name=name, grid=(sum(counts),),
        in_specs=[spec(s, c) for s, c in zip(starts, counts)]
        + [pl.BlockSpec(b.shape, lambda i: (0, 0), pipeline_mode=pl.Buffered(1))],
        out_specs=pl.BlockSpec((tm, n), lambda i: (i, 0)),
        out_shape=jax.ShapeDtypeStruct((sum(counts) * tm, n), out_dtype),
        compiler_params=_cparams(("arbitrary",)),
    )(*pieces, b)


def _norm_in_proj(x, g, w_t, tm, tn):
    t, k = x.shape
    n = w_t.shape[0]
    tm, tn = _pick(t, tm), _pick(n, tn)

    def body(x_ref, g_ref, w_ref, xn_ref, z_ref):
        xn = _rms(x_ref[...], g_ref[...]).astype(BF)
        xn_ref[...] = xn
        z_ref[...] = _raw_dot(xn, w_ref[...], "nt")

    xns, z = pl.pallas_call(
        body, name="in_proj", grid=(n // tn, t // tm),
        in_specs=[pl.BlockSpec((tm, k), lambda j, i: (i, 0)), pl.BlockSpec((1, k), lambda j, i: (0, 0)),
                  pl.BlockSpec((tn, k), lambda j, i: (j, 0))],
        out_specs=[pl.BlockSpec((None, tm, k), lambda j, i: (j, i, 0)), pl.BlockSpec((tm, tn), lambda j, i: (i, j))],
        out_shape=[jax.ShapeDtypeStruct((n // tn, t, k), BF), jax.ShapeDtypeStruct((t, n), F32)],
        compiler_params=_cparams(("arbitrary", "arbitrary")),
    )(x, g, w_t)
    return xns[0], z


def _merge_out_post(z, o_a, o_b, w_a, w_b, w_out, x, g_post, g_pre2, tm):
    t = x.shape[0]
    tm = _pick(t, tm)
    w = 256
    npc = D // w
    ga0, gb0 = (IN_COLS - 2 * D) // w, (IN_COLS - D) // w

    def body(*refs):
        ga_refs, gb_refs = refs[:npc], refs[npc:2 * npc]
        oa_ref, ob_ref, wa_ref, wb_ref, w_ref, x_ref, gp_ref, g2_ref = refs[2 * npc:2 * npc + 8]
        ya_ref, yb_ref, m_ref, mix_ref, h_ref, xn_ref = refs[2 * npc + 8:]
        ya = _raw_dot(oa_ref[...], wa_ref[...], "nn").astype(BF)
        yb = _raw_dot(ob_ref[...], wb_ref[...], "nn").astype(BF)
        ya_ref[...] = ya
        yb_ref[...] = yb
        parts = []
        for p in range(npc):
            cols = slice(p * w, (p + 1) * w)
            parts.append(_sigmoid(ga_refs[p][...]) * ya[:, cols].astype(F32)
                         + _sigmoid(gb_refs[p][...]) * yb[:, cols].astype(F32))
        merged = jnp.concatenate(parts, axis=1).astype(BF)
        m_ref[...] = merged
        mix = _raw_dot(merged, w_ref[...], "nn")
        mix_ref[...] = mix
        h1 = x_ref[...] + _rms(mix, gp_ref[...])
        h_ref[...] = h1
        xn_ref[...] = _rms(h1, g2_ref[...]).astype(BF)

    row = pl.BlockSpec((tm, D), lambda i: (i, 0))
    one = pl.BlockSpec((1, D), lambda i: (0, 0))

    def gate(b0):
        return [pl.BlockSpec((tm, w), functools.partial(lambda i, b: (i, b), b=b0 + p)) for p in range(npc)]

    wgt = pl.BlockSpec((D, D), lambda i: (0, 0), pipeline_mode=pl.Buffered(1))
    return pl.pallas_call(
        body, name="merge_out_post", grid=(t // tm,),
        in_specs=gate(ga0) + gate(gb0) + [row, row, wgt, wgt, wgt, row, one, one],
        out_specs=[row] * 6,
        out_shape=[jax.ShapeDtypeStruct((t, D), BF), jax.ShapeDtypeStruct((t, D), BF), jax.ShapeDtypeStruct((t, D), BF),
                   jax.ShapeDtypeStruct((t, D), F32), jax.ShapeDtypeStruct((t, D), F32),
                   jax.ShapeDtypeStruct((t, D), BF)],
        compiler_params=_cparams(("arbitrary",)),
    )(*([z] * (2 * npc)), o_a, o_b, w_a, w_b, w_out, x, g_post, g_pre2)


def _accumulate(ni, refs, vals):
    @pl.when(ni == 0)
    def _():
        for r in refs:
            r[...] = jnp.zeros(r.shape, F32)

    for r, v in zip(refs, vals):
        r[...] += v


def _dmerged_merge_bwd(dmix, w_out, w_a, w_b, z, y_a, y_b, tm):
    t = dmix.shape[0]
    tm = _pick(t, tm)
    w = 256
    npc = D // w
    ga0, gb0 = (IN_COLS - 2 * D) // w, (IN_COLS - D) // w

    def body(*refs):
        dm_ref, w_ref, wa_ref, wb_ref = refs[:4]
        ga_refs, gb_refs = refs[4:4 + npc], refs[4 + npc:4 + 2 * npc]
        ya_ref, yb_ref, dga_ref, dgb_ref, dya_ref, dyb_ref, doa_ref, dob_ref = refs[4 + 2 * npc:]
        dmerged = _raw_dot(dm_ref[...], w_ref[...], "nt")
        dyas, dybs = [], []
        for p in range(npc):
            cols = slice(p * w, (p + 1) * w)
            xs = [ga_refs[p][...], gb_refs[p][...], ya_ref[:, cols].astype(F32), yb_ref[:, cols].astype(F32)]
            _, vjp = jax.vjp(lambda *a: _f_merge([], list(a), [])[0][0], *xs)
            dga, dgb, dya, dyb = vjp(dmerged[:, cols])
            dga_ref[:, cols] = dga.astype(BF)
            dgb_ref[:, cols] = dgb.astype(BF)
            dyas.append(dya.astype(BF))
            dybs.append(dyb.astype(BF))
        dya, dyb = jnp.concatenate(dyas, axis=1), jnp.concatenate(dybs, axis=1)
        dya_ref[...] = dya
        dyb_ref[...] = dyb
        doa_ref[...] = _raw_dot(dya, wa_ref[...], "nt").astype(BF)
        dob_ref[...] = _raw_dot(dyb, wb_ref[...], "nt").astype(BF)

    row = pl.BlockSpec((tm, D), lambda i: (i, 0))
    wgt = pl.BlockSpec((D, D), lambda i: (0, 0), pipeline_mode=pl.Buffered(1))

    def gate(b0):
        return [pl.BlockSpec((tm, w), functools.partial(lambda i, b: (i, b), b=b0 + p)) for p in range(npc)]

    return pl.pallas_call(
        body, name="merge_bwd", grid=(t // tm,),
        in_specs=[row, wgt, wgt, wgt] + gate(ga0) + gate(gb0) + [row, row],
        out_specs=[row] * 6, out_shape=[jax.ShapeDtypeStruct((t, D), BF)] * 6,
        compiler_params=_cparams(("arbitrary",)),
    )(dmix, w_out, w_a, w_b, *([z] * (2 * npc)), y_a, y_b)


def _dxn2_post1_bwd(pieces, w_up_t, x, mix, dh1, g_post, g_pre2, tm):
    t = x.shape[0]
    tm = _pick(t, tm)
    k = w_up_t.shape[0]
    offs = [sum(p.shape[1] for p in pieces[:i]) for i in range(len(pieces))]
    na = len(pieces)

    def body(*refs):
        w_ref, x_ref, m_ref, dh_ref, gp_ref, g2_ref, dgp_ref, dg2_ref, dx_ref, dm_ref = refs[na:]
        dxn2 = None
        for a_ref, off in zip(refs[:na], offs):
            part = _raw_dot(a_ref[...], w_ref[off:off + a_ref.shape[1], :], "nn")
            dxn2 = part if dxn2 is None else dxn2 + part
        _, vjp = jax.vjp(lambda gp, g2, xx, mm: _f_post1([gp, g2], [xx, mm], [])[0],
                         gp_ref[...], g2_ref[...], x_ref[...], m_ref[...])
        dgp, dg2, dx, dm = vjp([dh_ref[...], dxn2])
        _accumulate(pl.program_id(0), [dgp_ref, dg2_ref], [dgp, dg2])
        dx_ref[...] = dx
        dm_ref[...] = dm.astype(BF)

    row = pl.BlockSpec((tm, D), lambda i: (i, 0))
    one = pl.BlockSpec((1, D), lambda i: (0, 0))
    return pl.pallas_call(
        body, name="post1_bwd", grid=(t // tm,),
        in_specs=[pl.BlockSpec((tm, p.shape[1]), lambda i: (i, 0)) for p in pieces]
        + [pl.BlockSpec((k, D), lambda i: (0, 0), pipeline_mode=pl.Buffered(1)), row, row, row, one, one],
        out_specs=[one, one, row, row],
        out_shape=[jax.ShapeDtypeStruct((1, D), F32), jax.ShapeDtypeStruct((1, D), F32),
                   jax.ShapeDtypeStruct((t, D), F32), jax.ShapeDtypeStruct((t, D), BF)],
        compiler_params=_cparams(("arbitrary",)),
    )(*pieces, w_up_t, x, mix, dh1, g_post, g_pre2)


def _conv_taps(h, cw, cb, p2, p1):
    s1 = _shift_down(h, p1)
    s2 = _shift_down(s1, p2)
    return cb + _row(cw, 0) * s2 + _row(cw, 1) * s1 + _row(cw, 2) * h


def _up_conv(xn2, w_up_t, conv_w, conv_b, tm, tc):
    t = xn2.shape[0]
    tm = _pick(t, tm)
    tn = _pick(DFF, 1408)
    nj = DFF // tn
    sub = tm // tc
    n = t // tc
    last = t // tm - 1

    def body(x_ref, wg_ref, wv_ref, cwg_ref, cwv_ref, cbg_ref, cbv_ref, hg_ref, hv_ref, act_ref, c1_ref, c2_ref, prev):
        j, i = pl.program_id(0), pl.program_id(1)

        @pl.when(i == 0)
        def _():
            prev[...] = jnp.zeros(prev.shape, F32)

        x = x_ref[...]
        hg = _raw_dot(x, wg_ref[...], "nt")
        hv = _raw_dot(x, wv_ref[...], "nt")
        hg_ref[...] = hg
        hv_ref[...] = hv
        pg, pv = prev[0:SUBLANES], prev[SUBLANES:2 * SUBLANES]
        cg = _conv_taps(hg, cwg_ref[...], cbg_ref[...], _row(pg, SUBLANES - 2), _row(pg, SUBLANES - 1))
        cv = _conv_taps(hv, cwv_ref[...], cbv_ref[...], _row(pv, SUBLANES - 2), _row(pv, SUBLANES - 1))
        act_ref[...] = (_silu(cg) * cv).astype(BF)
        prev[0:SUBLANES] = hg[tm - SUBLANES:tm]
        prev[SUBLANES:2 * SUBLANES] = hv[tm - SUBLANES:tm]

        def keep(h, off):
            cols = slice(off, off + tn)

            @pl.when(i == 0)
            def _():
                c1_ref[0, :, cols] = jnp.zeros((1, tn), F32)
                c2_ref[0, :, cols] = jnp.zeros((1, tn), F32)

            for s in range(sub):
                def put(s=s):
                    tail = h[(s + 1) * tc - SUBLANES:(s + 1) * tc]
                    c1_ref[i * sub + s + 1, :, cols] = _row(tail, SUBLANES - 1)
                    c2_ref[i * sub + s + 1, :, cols] = _row(tail, SUBLANES - 2)

                if s < sub - 1:
                    put()
                else:
                    pl.when(i < last)(put)

        for col in range(nj):
            @pl.when(j == col)
            def _(col=col):
                keep(hg, col * tn)
                keep(hv, DFF + col * tn)

    def cols(rows, off):
        return pl.BlockSpec((rows, tn), lambda j, i: (0, j + off))

    tile = pl.BlockSpec((tm, tn), lambda j, i: (i, j))
    before = pl.BlockSpec((n, 1, 2 * DFF), lambda j, i: (0, 0, 0))
    return pl.pallas_call(
        body, name="up_conv", grid=(nj, t // tm),
        in_specs=[pl.BlockSpec((tm, D), lambda j, i: (i, 0)), pl.BlockSpec((tn, D), lambda j, i: (j, 0)),
                  pl.BlockSpec((tn, D), lambda j, i: (j + nj, 0)), cols(3, 0), cols(3, nj), cols(1, 0), cols(1, nj)],
        out_specs=[tile, tile, tile, before, before],
        out_shape=[jax.ShapeDtypeStruct((t, DFF), F32), jax.ShapeDtypeStruct((t, DFF), F32),
                   jax.ShapeDtypeStruct((t, DFF), BF), jax.ShapeDtypeStruct((n, 1, 2 * DFF), F32),
                   jax.ShapeDtypeStruct((n, 1, 2 * DFF), F32)],
        scratch_shapes=[pltpu.VMEM((2 * SUBLANES, tn), F32)],
        compiler_params=_cparams(("arbitrary", "arbitrary")),
    )(xn2, w_up_t, w_up_t, conv_w, conv_w, conv_b, conv_b)


def _dxn_pre1_bwd(pieces, w_t, x, dx_res, g, tm, token):
    t = x.shape[0]
    tm = _pick(t, tm)
    offs = [sum(p.shape[1] for p in pieces[:i]) for i in range(len(pieces))]
    na = len(pieces)

    def body(*refs):
        w_ref, x_ref, r_ref, g_ref = refs[na:na + 4]
        dg_ref, dx_ref = refs[-2:]
        dxn = None
        for a_ref, off in zip(refs[:na], offs):
            part = _raw_dot(a_ref[...], w_ref[off:off + a_ref.shape[1], :], "nn")
            dxn = part if dxn is None else dxn + part
        _, vjp = jax.vjp(lambda gg, xx: _f_pre1_residual([gg], [xx], [])[0], g_ref[...], x_ref[...])
        dg, dx = vjp([dxn, r_ref[...]])
        _accumulate(pl.program_id(0), [dg_ref], [dg])
        dx_ref[...] = dx

    row = pl.BlockSpec((tm, D), lambda i: (i, 0))
    one = pl.BlockSpec((1, D), lambda i: (0, 0))
    return pl.pallas_call(
        body, name="pre1_bwd", grid=(t // tm,),
        in_specs=[pl.BlockSpec((tm, p.shape[1]), lambda i: (i, 0)) for p in pieces]
        + [pl.BlockSpec(w_t.shape, lambda i: (0, 0), pipeline_mode=pl.Buffered(1)), row, row, one,
           pl.BlockSpec(token.shape, lambda i: (0, 0))],
        out_specs=[one, row],
        out_shape=[jax.ShapeDtypeStruct((1, D), F32), jax.ShapeDtypeStruct((t, D), F32)],
        compiler_params=_cparams(("arbitrary",)),
    )(*pieces, w_t, x, dx_res, g, token)


def _down_loss(act, w_down, g_post, h1, tgt, tm):
    t, k = act.shape
    tm = _pick(t, tm)

    def body(a_ref, w_ref, g_ref, h_ref, t_ref, loss_ref, dg_ref, dh_ref, df_ref):
        ni = pl.program_id(0)
        ff = _raw_dot(a_ref[...], w_ref[...], "nn")
        target = t_ref[...]

        def lossf(g, h1, ff):
            e = h1 + _rms(ff, g) - target
            return 0.5 * jnp.sum(jnp.mean(e * e, axis=-1))

        l, (dg, dh, df) = jax.value_and_grad(lossf, argnums=(0, 1, 2))(g_ref[...], h_ref[...], ff)

        @pl.when(ni == 0)
        def _():
            loss_ref[...] = jnp.zeros(loss_ref.shape, F32)
            dg_ref[...] = jnp.zeros(dg_ref.shape, F32)

        loss_ref[...] += jnp.full(loss_ref.shape, l, F32)
        dg_ref[...] += dg
        dh_ref[...] = dh
        df_ref[...] = df.astype(df_ref.dtype)

    row = pl.BlockSpec((tm, D), lambda ni: (ni, 0))
    one = pl.BlockSpec((1, D), lambda ni: (0, 0))
    return pl.pallas_call(
        body, name="down_loss", grid=(t // tm,),
        in_specs=[pl.BlockSpec((tm, k), lambda ni: (ni, 0)),
                  pl.BlockSpec((k, D), lambda ni: (0, 0), pipeline_mode=pl.Buffered(1)), one, row, row],
        out_specs=[pl.BlockSpec((1, LANES), lambda ni: (0, 0)), one, row, row],
        out_shape=[jax.ShapeDtypeStruct((1, LANES), F32), jax.ShapeDtypeStruct((1, D), F32),
                   jax.ShapeDtypeStruct((t, D), F32), jax.ShapeDtypeStruct((t, D), BF)],
        compiler_params=_cparams(("arbitrary",)),
    )(act, w_down, g_post, h1, tgt)


_ANY = pl.BlockSpec(memory_space=pl.ANY)


def _all_gather(name, blks):
    na = len(blks)
    ns = 8

    def body(*refs):
        x_refs, out_refs = refs[:na], refs[na:2 * na]
        send_sems, recv_sems, local_sems = refs[2 * na:]
        x, y, cc = lax.axis_index("x"), lax.axis_index("y"), lax.axis_index("c")
        sibling, xn, yn = (x, y, 1 - cc), (1 - x, y, cc), (x, 1 - y, cc)

        def num(px, py, pc):
            return 4 * px + 2 * py + pc

        def copy(a, k, to, src, dst):
            return pltpu.make_async_remote_copy(src_ref=src, dst_ref=dst, send_sem=send_sems.at[ns * a + k],
                                                recv_sem=recv_sems.at[ns * a + k], device_id=to, device_id_type=MESH)

        def halves(a, blk):
            h = blks[a].shape[0] // 2
            return out_refs[a].at[blk, pl.ds(0, h)], out_refs[a].at[blk, pl.ds(h, h)]

        mine, sends = [], []
        for a in range(na):
            o = out_refs[a]
            m = pltpu.make_async_copy(x_refs[a], o.at[num(x, y, cc)], local_sems.at[a])
            m.start()
            mine.append(m)
            own = o.at[num(x, y, cc)]
            sends.append([copy(a, 0, sibling, x_refs[a], own), copy(a, 1, xn, x_refs[a], own),
                          copy(a, 2, yn, x_refs[a], own)])
            for cp in sends[a]:
                cp.start()
        for a in range(na):
            o = out_refs[a]
            bx, by, bd = num(1 - x, y, cc), num(x, 1 - y, cc), num(1 - x, 1 - y, cc)
            copy(a, 1, xn, o.at[bx], o.at[bx]).wait_recv()
            more = [copy(a, 3, yn, halves(a, bx)[0], halves(a, bx)[0]), copy(a, 5, sibling, o.at[bx], o.at[bx])]
            for cp in more:
                cp.start()
            sends[a] += more
        for a in range(na):
            o = out_refs[a]
            bx, by, bd = num(1 - x, y, cc), num(x, 1 - y, cc), num(1 - x, 1 - y, cc)
            copy(a, 2, yn, o.at[by], o.at[by]).wait_recv()
            more = [copy(a, 4, xn, halves(a, by)[1], halves(a, by)[1]), copy(a, 6, sibling, o.at[by], o.at[by])]
            for cp in more:
                cp.start()
            sends[a] += more
        for a in range(na):
            o = out_refs[a]
            bd = num(1 - x, 1 - y, cc)
            copy(a, 3, yn, halves(a, bd)[0], halves(a, bd)[0]).wait_recv()
            copy(a, 4, xn, halves(a, bd)[1], halves(a, bd)[1]).wait_recv()
            fw = copy(a, 7, sibling, o.at[bd], o.at[bd])
            fw.start()
            sends[a].append(fw)
        for a in range(na):
            o = out_refs[a]
            for k, blk in ((0, num(x, y, 1 - cc)), (5, num(1 - x, y, 1 - cc)), (6, num(x, 1 - y, 1 - cc)),
                           (7, num(1 - x, 1 - y, 1 - cc))):
                copy(a, k, sibling, o.at[blk], o.at[blk]).wait_recv()
            for cp in sends[a]:
                cp.wait_send()
        for m in mine:
            m.wait()

    res = pl.pallas_call(
        body, name=name, in_specs=[_ANY] * na, out_specs=[_ANY] * na,
        out_shape=[jax.ShapeDtypeStruct((N_DEV,) + b.shape, b.dtype) for b in blks],
        scratch_shapes=[pltpu.SemaphoreType.DMA((ns * na,)), pltpu.SemaphoreType.DMA((ns * na,)),
                        pltpu.SemaphoreType.DMA((na,))],
    )(*blks)
    return list(res)


def _all_gather_small(name, blk):
    def body(x_ref, out_ref, ssem, rsem, lsem):
        x, y, c = lax.axis_index("x"), lax.axis_index("y"), lax.axis_index("c")
        me = 4 * x + 2 * y + c
        mine = pltpu.make_async_copy(x_ref, out_ref.at[me], lsem)
        mine.start()
        cps = []
        for j in range(1, N_DEV):
            px = 1 - x if j & 4 else x
            py = 1 - y if j & 2 else y
            pc = 1 - c if j & 1 else c
            cps.append(pltpu.make_async_remote_copy(src_ref=x_ref, dst_ref=out_ref.at[me], send_sem=ssem.at[j - 1],
                                                    recv_sem=rsem.at[j - 1], device_id=(px, py, pc),
                                                    device_id_type=MESH))
        for cp in cps:
            cp.start()
        for cp in cps:
            cp.wait()
        mine.wait()

    return pl.pallas_call(
        body, name=name, in_specs=[_ANY], out_specs=_ANY,
        out_shape=jax.ShapeDtypeStruct((N_DEV,) + blk.shape, blk.dtype),
        scratch_shapes=[pltpu.SemaphoreType.DMA((N_DEV - 1,)), pltpu.SemaphoreType.DMA((N_DEV - 1,)),
                        pltpu.SemaphoreType.DMA],
    )(blk)


def _reduce_pair(g8s):
    na = len(g8s)

    def body(*refs):
        g_refs, recv_refs = refs[:na], refs[na:2 * na]
        ssem, rsem = refs[2 * na:]
        x, y, cc = lax.axis_index("x"), lax.axis_index("y"), lax.axis_index("c")
        chips = [(x, y), (1 - x, y), (x, 1 - y), (1 - x, 1 - y)]
        sib = (x, y, 1 - cc)
        for a in range(na):
            for k, (cx, cy) in enumerate(chips):
                pltpu.make_async_remote_copy(
                    src_ref=g_refs[a].at[4 * cx + 2 * cy + 1 - cc], dst_ref=recv_refs[a].at[k],
                    send_sem=ssem.at[a], recv_sem=rsem.at[a], device_id=sib, device_id_type=MESH).start()
        for a in range(na):
            pltpu.make_async_remote_copy(src_ref=recv_refs[a], dst_ref=recv_refs[a], send_sem=ssem.at[a],
                                         recv_sem=rsem.at[a], device_id=sib, device_id_type=MESH).wait()

    res = pl.pallas_call(
        body, name="reduce_pair", in_specs=[_ANY] * na, out_specs=[_ANY] * na,
        out_shape=[jax.ShapeDtypeStruct((4,) + g.shape[1:], g.dtype) for g in g8s],
        scratch_shapes=[pltpu.SemaphoreType.DMA((na,)), pltpu.SemaphoreType.DMA((na,))],
    )(*g8s)
    return list(res)


_HBM = pl.BlockSpec(memory_space=pltpu.HBM)
_SEM = pl.BlockSpec(memory_space=pltpu.SEMAPHORE)
_EFFECT = pltpu.SideEffectType.DATAFLOW_SIDE_EFFECTING


def _chip_swap_copies(s_refs, land_refs, ssem, rsem):
    x, y, c = lax.axis_index("x"), lax.axis_index("y"), lax.axis_index("c")
    targets = [(1 - x, y, c), (x, 1 - y, c), (1 - x, 1 - y, c)]
    return [pltpu.make_async_remote_copy(src_ref=s.at[k], dst_ref=d.at[k], send_sem=ssem.at[3 * a + k],
                                         recv_sem=rsem.at[3 * a + k], device_id=targets[k], device_id_type=MESH)
            for a, (s, d) in enumerate(zip(s_refs, land_refs)) for k in range(3)]


def _chip_swap_start(sends):
    na = len(sends)

    def body(*refs):
        cps = _chip_swap_copies(refs[:na], refs[na:2 * na], refs[2 * na], refs[2 * na + 1])
        for cp in cps:
            cp.start()
        token = refs[-1]
        token[...] = jnp.zeros(token.shape, token.dtype)

    bufs = [pltpu.HBM(s.shape, s.dtype) for s in sends]
    res = pl.pallas_call(
        body, name="chip_swap_start",
        out_shape=[pltpu.SemaphoreType.DMA((3 * na,)), pltpu.SemaphoreType.DMA((3 * na,))] + bufs + bufs
        + [jax.ShapeDtypeStruct((8, LANES), F32)],
        in_specs=[_HBM] * (2 * na), out_specs=[_SEM, _SEM] + [_HBM] * (2 * na) + [pl.BlockSpec(memory_space=pltpu.VMEM)],
        input_output_aliases={i: 2 + i for i in range(2 * na)},
        compiler_params=pltpu.CompilerParams(has_side_effects=_EFFECT),
    )(*[pltpu.with_memory_space_constraint(s, pltpu.HBM) for s in sends],
      *[pltpu.with_memory_space_constraint(lax.empty(s.shape, s.dtype), pltpu.HBM) for s in sends])
    return res[0], res[1], list(res[2:2 + na]), list(res[2 + na:2 + 2 * na]), res[-1]


def _chip_swap_wait(ssem, rsem, srcs, lands, after):
    na = len(srcs)

    def body(*refs):
        cps = _chip_swap_copies(refs[:na], refs[na:2 * na], refs[2 * na], refs[2 * na + 1])
        for cp in cps:
            cp.wait_send()
            cp.wait_recv()

    bufs = [pltpu.HBM(s.shape, s.dtype) for s in srcs]
    res = pl.pallas_call(
        body, name="chip_swap_wait", out_shape=bufs + bufs,
        in_specs=[_HBM] * (2 * na) + [_SEM, _SEM, _ANY], out_specs=[_HBM] * (2 * na),
        input_output_aliases={i: i for i in range(2 * na)},
        compiler_params=pltpu.CompilerParams(has_side_effects=_EFFECT),
    )(*srcs, *lands, ssem, rsem, after)
    return list(res[na:])


def _pick_rows(r, c, budget=TILE_BYTES):
    if r * c * 4 <= budget or r % 16:
        return r
    best = 16
    for tr in range(16, r, 16):
        if r % tr == 0 and tr * c * 4 <= budget:
            best = tr
    return best


def _pair_sum(name, idx4, g8, recv4):
    _, r, c = g8.shape
    tr = _pick_rows(r, c, 2 * TILE_BYTES)

    def body(idx_ref, a_ref, b_ref, o0_ref, o3_ref):
        k = pl.program_id(1)
        s = a_ref[...].astype(F32) + b_ref[...].astype(F32)

        @pl.when(k == 0)
        def _():
            o0_ref[...] = s

        @pl.when(k > 0)
        def _():
            o3_ref[...] = s.astype(BF)

    spec = pltpu.PrefetchScalarGridSpec(
        num_scalar_prefetch=1, grid=(r // tr, 4),
        in_specs=[pl.BlockSpec((None, tr, c), lambda i, k, idx: (idx[k], i, 0)),
                  pl.BlockSpec((None, tr, c), lambda i, k, idx: (k, i, 0))],
        out_specs=[pl.BlockSpec((tr, c), lambda i, k, idx: (i, 0)),
                   pl.BlockSpec((None, tr, c), lambda i, k, idx: (jnp.maximum(k - 1, 0), i, 0))])
    return pl.pallas_call(
        body, name=name, grid_spec=spec,
        out_shape=[jax.ShapeDtypeStruct((r, c), F32), jax.ShapeDtypeStruct((3, r, c), BF)],
        compiler_params=_cparams(("arbitrary", "arbitrary")),
    )(idx4, g8, recv4)


def _adamw(w, g, m, v):
    m = ADAM_B1 * m + (1.0 - ADAM_B1) * g
    v = ADAM_B2 * v + (1.0 - ADAM_B2) * jnp.square(g)
    m_hat = m / (1.0 - ADAM_B1 ** ADAM_STEP)
    v_hat = v / (1.0 - ADAM_B2 ** ADAM_STEP)
    delta = -ADAM_LR * (m_hat / (jnp.sqrt(v_hat) + ADAM_EPS) + ADAM_WD * w)
    return delta, m, v


def _adam_sharded(name, idx1, own, recv, w, m, v, after=None):
    r, c = w.shape
    tr = _pick_rows(r, c, 2 * TILE_BYTES)
    nj = recv.shape[0]
    extra = [] if after is None else [after]

    def body(idx_ref, p_ref, r_ref, w_ref, m_ref, v_ref, *rest):
        g_out, d_out, m_out, v_out = rest[-4:]
        g = p_ref[...].astype(F32)
        for k in range(nj):
            g = g + r_ref[k].astype(F32)
        d, mn, vn = _adamw(w_ref[...], g, m_ref[...], v_ref[...])
        g_out[...] = g
        d_out[...] = d
        m_out[...] = mn
        v_out[...] = vn

    row = pl.BlockSpec((tr, c), lambda i, idx: (i, 0))
    spec = pltpu.PrefetchScalarGridSpec(
        num_scalar_prefetch=1, grid=(r // tr,),
        in_specs=[pl.BlockSpec((None, tr, c), lambda i, idx: (idx[0], i, 0)),
                  pl.BlockSpec((nj, tr, c), lambda i, idx: (0, i, 0)), row, row, row]
        + [pl.BlockSpec(e.shape, lambda i, idx: (0, 0)) for e in extra],
        out_specs=[row] * 4)
    return pl.pallas_call(
        body, name=name, grid_spec=spec, out_shape=[jax.ShapeDtypeStruct((r, c), F32)] * 4,
        compiler_params=_cparams(("arbitrary",)),
    )(idx1, own, recv, w, m, v, *extra)


def _repl_rows():
    rows, r = {}, 0
    for name, cols in REPL:
        rows[name] = r
        r += REPL_ROWS.get(name, 1) * ((cols + D - 1) // D)
    return rows


LOSS_ROW = 24


def _pack_replicated(grads, loss_acc, after):
    rows = _repl_rows()
    names = [n for n, _ in REPL]

    def body(*refs):
        o_ref = refs[-1]
        o_ref[...] = jnp.zeros(o_ref.shape, F32)
        o_ref[LOSS_ROW:LOSS_ROW + 1, 0:LANES] = refs[len(names)][...]
        for name, ref in zip(names, refs[:len(names)]):
            r0 = rows[name]
            nr, nc = ref.shape
            if nc <= D:
                o_ref[r0:r0 + nr, 0:nc] = ref[...]
            else:
                for j in range((nc + D - 1) // D):
                    lo, hi = j * D, min(nc, (j + 1) * D)
                    o_ref[r0 + j:r0 + j + 1, 0:hi - lo] = ref[:, lo:hi]

    return pl.pallas_call(body, name="pack_replicated", out_shape=jax.ShapeDtypeStruct((REPL_TOTAL, D), F32),
                          in_specs=[pl.BlockSpec(memory_space=pltpu.VMEM)] * (len(names) + 1) + [_ANY] * len(after),
                          compiler_params=_cparams())(*[grads[n] for n in names], loss_acc, *after)


def _adam_replicated(g8, ws, ms, vs):
    rows = _repl_rows()
    names = [n for n, _ in REPL]
    np_ = len(names)

    def body(*refs):
        g_ref = refs[0]
        w_refs, m_refs, v_refs = refs[1:1 + np_], refs[1 + np_:1 + 2 * np_], refs[1 + 2 * np_:1 + 3 * np_]
        outs = refs[1 + 3 * np_:1 + 7 * np_]
        scr = refs[-1]
        g = g_ref[0]
        for k in range(1, N_DEV):
            g = g + g_ref[k]
        scr[...] = g
        refs[1 + 7 * np_][...] = scr[LOSS_ROW:LOSS_ROW + 1, 0:LANES]
        for i, name in enumerate(names):
            r0 = rows[name]
            nr, nc = w_refs[i].shape
            if nc <= D:
                gi = scr[r0:r0 + nr, 0:nc]
            else:
                parts = []
                for j in range((nc + D - 1) // D):
                    lo, hi = j * D, min(nc, (j + 1) * D)
                    parts.append(scr[r0 + j:r0 + j + 1, 0:hi - lo])
                gi = jnp.concatenate(parts, axis=1)
            d, mn, vn = _adamw(w_refs[i][...], gi, m_refs[i][...], v_refs[i][...])
            outs[i][...] = gi
            outs[np_ + i][...] = d
            outs[2 * np_ + i][...] = mn
            outs[3 * np_ + i][...] = vn

    shp = [jax.ShapeDtypeStruct(w.shape, F32) for w in ws]
    res = pl.pallas_call(body, name="adam_replicated", out_shape=shp * 4 + [jax.ShapeDtypeStruct((1, LANES), F32)],
                         scratch_shapes=[pltpu.VMEM((REPL_TOTAL, D), F32)], compiler_params=_cparams(),
                         )(g8, *ws, *ms, *vs)
    return [dict(zip(names, res[k * np_:(k + 1) * np_])) for k in range(4)], res[-1]


_WEIGHTS = ("attn_pre_norm", "w_in", "hgrn_lb", "hgrn_gnorm", "w_branch_a", "rwkv_mu", "rwkv_w0", "rwkv_w2",
            "rwkv_a0", "rwkv_a2", "rwkv_g2", "rwkv_k_k", "rwkv_k_a", "rwkv_r_k", "rwkv_ln_w", "rwkv_ln_b",
            "w_branch_b", "w_out", "attn_post_norm", "ffn_pre_norm", "w_up", "conv_w", "conv_b", "w_down",
            "ffn_post_norm")
_BIG = ("w_in", "w_up", "w_down", "w_branch_a", "w_branch_b", "w_out")


def _stages():
    one = [D]
    hw = HG_K * HG_PER_STEP
    rw = LANES * RW_PAIRS_PER_STEP
    return dict(
        mixers=_Stage("mixers", _f_mixers, 1, 2 * RW_CHUNK, [False] * 13, [[D] * 7 + [LANES, LANES]], [0],
                      [(hw, HG_K), (1, RW_COLS), (rw, LANES)], [one, one], [BF, BF],
                      kept_shapes=[(2 * RW_KEPT * RW_PAIRS_PER_STEP * 2 * RW_CHUNK, LANES)], f_kept=_f_mixers_kept),
        conv=_Stage("conv", _f_conv, 1, 512, [False, False], [[DFF], [DFF]], [0, 0], [(1, 2 * DFF), (1, 2 * DFF)],
                    [[DFF]], [BF]),
    )


def _cols_to_blocks(w, per):
    return w.reshape(w.shape[0], N_DEV, per).transpose(1, 0, 2)


def _blocks_to_cols(g):
    return g.transpose(1, 0, 2).reshape(g.shape[1], N_DEV * g.shape[2])


def kernel(x, attn_pre_norm, w_in, hgrn_lb, hgrn_gnorm, w_branch_a, rwkv_mu, rwkv_w0, rwkv_w2, rwkv_a0, rwkv_a2, rwkv_g2, rwkv_k_k, rwkv_k_a, rwkv_r_k, rwkv_ln_w, rwkv_ln_b, w_branch_b, w_out, attn_post_norm, ffn_pre_norm, w_up, conv_w, conv_b, w_down, ffn_post_norm, loss_target, m_attn_pre_norm, m_w_in, m_hgrn_lb, m_hgrn_gnorm, m_w_branch_a, m_rwkv_mu, m_rwkv_w0, m_rwkv_w2, m_rwkv_a0, m_rwkv_a2, m_rwkv_g2, m_rwkv_k_k, m_rwkv_k_a, m_rwkv_r_k, m_rwkv_ln_w, m_rwkv_ln_b, m_w_branch_b, m_w_out, m_attn_post_norm, m_ffn_pre_norm, m_w_up, m_conv_w, m_conv_b, m_w_down, m_ffn_post_norm, v_attn_pre_norm, v_w_in, v_hgrn_lb, v_hgrn_gnorm, v_w_branch_a, v_rwkv_mu, v_rwkv_w0, v_rwkv_w2, v_rwkv_a0, v_rwkv_a2, v_rwkv_g2, v_rwkv_k_k, v_rwkv_k_a, v_rwkv_r_k, v_rwkv_ln_w, v_rwkv_ln_b, v_w_branch_b, v_w_out, v_attn_post_norm, v_ffn_pre_norm, v_w_up, v_conv_w, v_conv_b, v_w_down, v_ffn_post_norm):
    w = dict(attn_pre_norm=attn_pre_norm, w_in=w_in, hgrn_lb=hgrn_lb, hgrn_gnorm=hgrn_gnorm, w_branch_a=w_branch_a, rwkv_mu=rwkv_mu, rwkv_w0=rwkv_w0, rwkv_w2=rwkv_w2, rwkv_a0=rwkv_a0, rwkv_a2=rwkv_a2, rwkv_g2=rwkv_g2, rwkv_k_k=rwkv_k_k, rwkv_k_a=rwkv_k_a, rwkv_r_k=rwkv_r_k, rwkv_ln_w=rwkv_ln_w, rwkv_ln_b=rwkv_ln_b, w_branch_b=w_branch_b, w_out=w_out, attn_post_norm=attn_post_norm, ffn_pre_norm=ffn_pre_norm, w_up=w_up, conv_w=conv_w, conv_b=conv_b, w_down=w_down, ffn_post_norm=ffn_post_norm)
    mo = dict(attn_pre_norm=m_attn_pre_norm, w_in=m_w_in, hgrn_lb=m_hgrn_lb, hgrn_gnorm=m_hgrn_gnorm, w_branch_a=m_w_branch_a, rwkv_mu=m_rwkv_mu, rwkv_w0=m_rwkv_w0, rwkv_w2=m_rwkv_w2, rwkv_a0=m_rwkv_a0, rwkv_a2=m_rwkv_a2, rwkv_g2=m_rwkv_g2, rwkv_k_k=m_rwkv_k_k, rwkv_k_a=m_rwkv_k_a, rwkv_r_k=m_rwkv_r_k, rwkv_ln_w=m_rwkv_ln_w, rwkv_ln_b=m_rwkv_ln_b, w_branch_b=m_w_branch_b, w_out=m_w_out, attn_post_norm=m_attn_post_norm, ffn_pre_norm=m_ffn_pre_norm, w_up=m_w_up, conv_w=m_conv_w, conv_b=m_conv_b, w_down=m_w_down, ffn_post_norm=m_ffn_post_norm)
    vo = dict(attn_pre_norm=v_attn_pre_norm, w_in=v_w_in, hgrn_lb=v_hgrn_lb, hgrn_gnorm=v_hgrn_gnorm, w_branch_a=v_w_branch_a, rwkv_mu=v_rwkv_mu, rwkv_w0=v_rwkv_w0, rwkv_w2=v_rwkv_w2, rwkv_a0=v_rwkv_a0, rwkv_a2=v_rwkv_a2, rwkv_g2=v_rwkv_g2, rwkv_k_k=v_rwkv_k_k, rwkv_k_a=v_rwkv_k_a, rwkv_r_k=v_rwkv_r_k, rwkv_ln_w=v_rwkv_ln_w, rwkv_ln_b=v_rwkv_ln_b, w_branch_b=v_w_branch_b, w_out=v_w_out, attn_post_norm=v_attn_post_norm, ffn_pre_norm=v_ffn_pre_norm, w_up=v_w_up, conv_w=v_conv_w, conv_b=v_conv_b, w_down=v_w_down, ffn_post_norm=v_ffn_post_norm)

    t = x.shape[1]
    x2 = x.reshape(t, D)
    tgt = loss_target.reshape(t, D)
    st = _stages()

    me = 4 * lax.axis_index("x") + 2 * lax.axis_index("y") + lax.axis_index("c")
    small = jnp.concatenate([rwkv_w2[0], rwkv_a2[0], rwkv_g2[0]], axis=0).astype(BF)
    g_in, g_small = _all_gather("gather_weights", [w_in[0].T.astype(BF), small])
    fw_in_t = g_in.reshape(IN_COLS, D)
    z64 = jnp.zeros((64, D), BF)
    w2p = jnp.concatenate([_blocks_to_cols(g_small[:, 0:64]), z64], axis=0)
    a2p = jnp.concatenate([z64, _blocks_to_cols(g_small[:, 64:128])], axis=0)
    g2f = _blocks_to_cols(g_small[:, 128:256])
    conv_bits = jnp.pad(lax.bitcast_convert_type(conv_w[0], BF).reshape(3, 2 * 704), ((0, 29), (0, 0)))
    late = [w_up[0].T.astype(BF)] + [w[k][0].astype(BF) for k in _BIG[2:]] + [conv_bits]
    late_gather = _Exchange("gather2", late)
    r_k = rwkv_r_k.reshape(1, D)

    xn, z = _norm_in_proj(x2, attn_pre_norm, fw_in_t, 512, 4736)
    mix_par = [hgrn_lb, hgrn_gnorm, rwkv_mu, rwkv_w0, w2p, rwkv_a0, a2p, g2f, rwkv_k_k, rwkv_k_a,
               rwkv_ln_w, rwkv_ln_b, r_k]
    mix_in = [z]
    (o_a, o_b), mix_saved = _stage_fwd(st["mixers"], t, mix_par, mix_in, hook=late_gather)
    gl = [lax.dynamic_update_slice(g, own[None], (me, 0, 0)) for g, own in zip(late_gather.results, late)]
    fw_up_t = gl[0].reshape(2 * DFF, D)
    fw_down = gl[1].reshape(DFF, D)
    fw_a, fw_b, fw_out = (g.reshape(D, D) for g in gl[2:5])
    conv_full = _blocks_to_cols(lax.bitcast_convert_type(gl[5][:, :3].reshape(N_DEV, 3, 704, 2), F32))
    y_a, y_b, merged, mix, h1, xn2 = _merge_out_post(z, o_a, o_b, fw_a, fw_b, fw_out, x2, attn_post_norm,
                                                     ffn_pre_norm, 512)
    conv_par = [conv_full, conv_b]
    hu_g, hu_v, act, before1, before2 = _up_conv(xn2, fw_up_t, conv_full, conv_b, 512, min(st["conv"].tm, t))
    conv_saved = [before1[None], before2[None]]

    loss_acc, d_ffn_post, dh1, dff = _down_loss(act, fw_down, ffn_post_norm, h1, tgt, 512)
    dw_down = _mm("dw_down", act, dff, "tn", BF, tm=1408, tn=512)
    (dcw, dcb), dhu = _stage_bwd(st["conv"], t, conv_par, [hu_g, hu_v], conv_saved, [], [BF, BF],
                                 dout_dot=(dff, fw_down))
    dw_up_t = _mm_cols_tn("dw_up", dhu, xn2, BF, 1408)
    d_post, d_pre2, dx_a, dmix = _dxn2_post1_bwd(dhu, fw_up_t, x2, mix, dh1, attn_post_norm, ffn_pre_norm, 512)
    dga, dgb, dy_a, dy_b, do_a, do_b = _dmerged_merge_bwd(dmix, fw_out, fw_a, fw_b, z, y_a, y_b, 512)
    dw_a, dw_b, dw_out = _mm_multi("dw_branches", [(o_a, dy_a), (o_b, dy_b), (merged, dmix)], "tn", BF)
    early = [dw_up_t.reshape(N_DEV, 704, D), dw_down.reshape(N_DEV, 352, D), dw_a.reshape(N_DEV, 128, D),
             dw_b.reshape(N_DEV, 128, D), dw_out.reshape(N_DEV, 128, D), _cols_to_blocks(dcw.astype(BF), 704)]
    early_scatter = _Exchange("scatter", early)
    mix_dp, dz_hr = _stage_bwd(st["mixers"], t, mix_par, mix_in, mix_saved, [[do_a], [do_b]], [BF],
                               hook=early_scatter)
    d_lb, d_gn, d_mu, d_w0, d_w2p, d_a0, d_a2p, d_g2, d_kk, d_ka, d_lnw, d_lnb, d_rk = mix_dp
    dz = dz_hr + [dga, dgb]
    dw_in_t = _mm_cols_tn("dw_in", dz, xn, BF, 256)

    ax, ay, ac = lax.axis_index("x"), lax.axis_index("y"), lax.axis_index("c")
    idx4 = jnp.stack([4 * cx + 2 * cy + ac for cx, cy in ((ax, ay), (1 - ax, ay), (ax, 1 - ay), (1 - ax, 1 - ay))])
    idx4 = idx4.astype(jnp.int32)
    idx_me, idx_0 = idx4[0:1], jnp.zeros((1,), jnp.int32)
    d_small = jnp.concatenate([d_w2p[:64], d_a2p[64:], d_g2], axis=0).astype(BF)
    g8s = [dw_in_t.reshape(N_DEV, 1184, D), _cols_to_blocks(d_small, LANES)]
    recv4s = _reduce_pair(g8s)
    sums = [_pair_sum("pair_sum_" + n, idx4, g, r) for n, g, r in zip(("w_in", "small"), g8s, recv4s)]
    swap_ssem, swap_rsem, swap_srcs, swap_lands, token = _chip_swap_start([s[1] for s in sums])
    d_pre1, dx = _dxn_pre1_bwd(dz, fw_in_t, x2, dx_a, attn_pre_norm, 256, token)
    grad_x = dx.reshape(x.shape)

    sh_out = [dict() for _ in range(4)]
    done = []
    for n, own, recv in zip(_BIG[1:] + ("conv_w",), early, early_scatter.results):
        tr = (lambda a: a.T) if n == "w_up" else (lambda a: a)
        res = _adam_sharded("adam_" + n, idx_me, own, recv, *[tr(src[n][0]) for src in (w, mo, vo)], after=token)
        done.append(res[0])
        for kind in range(4):
            sh_out[kind][n] = tr(res[kind])[None]

    rg = dict(attn_pre_norm=d_pre1, hgrn_lb=d_lb, hgrn_gnorm=d_gn, rwkv_mu=d_mu, rwkv_w0=d_w0, rwkv_a0=d_a0,
              rwkv_k_k=d_kk, rwkv_k_a=d_ka, rwkv_r_k=d_rk, rwkv_ln_w=d_lnw, rwkv_ln_b=d_lnb, attn_post_norm=d_post,
              ffn_pre_norm=d_pre2, conv_b=dcb, ffn_post_norm=d_ffn_post)
    g8 = _all_gather_small("gather_small_grads", _pack_replicated(rg, loss_acc, done))
    rnames = [n for n, _ in REPL]
    flat = lambda src: [src[n].reshape(1, D) if n == "rwkv_r_k" else src[n] for n in rnames]
    rp_out, loss_row = _adam_replicated(g8, flat(w), flat(mo), flat(vo))
    loss = loss_row[0, 0]
    recv3s = _chip_swap_wait(swap_ssem, swap_rsem, swap_srcs, swap_lands, rp_out[0]["attn_pre_norm"])
    for kind in range(4):
        rp_out[kind]["rwkv_r_k"] = rp_out[kind]["rwkv_r_k"].reshape(rwkv_r_k.shape)

    def small_of(src):
        return jnp.concatenate([src["rwkv_w2"][0], src["rwkv_a2"][0], src["rwkv_g2"][0]], axis=0)

    res = _adam_sharded("adam_w_in", idx_0, sums[0][0][None], recv3s[0], *[src["w_in"][0].T for src in (w, mo, vo)])
    res_s = _adam_sharded("adam_small", idx_0, sums[1][0][None], recv3s[1], *[small_of(src) for src in (w, mo, vo)])
    for kind in range(4):
        sh_out[kind]["w_in"] = res[kind].T[None]
        sh_out[kind]["rwkv_w2"] = res_s[kind][0:64][None]
        sh_out[kind]["rwkv_a2"] = res_s[kind][64:128][None]
        sh_out[kind]["rwkv_g2"] = res_s[kind][128:256][None]

    outs = [loss, grad_x]
    for kind in range(4):
        for name in _WEIGHTS:
            outs.append(sh_out[kind][name] if name in sh_out[kind] else rp_out[kind][name])
    return tuple(outs)
```

```python
import functools

import jax
import jax.numpy as jnp
from jax import lax
from jax.experimental import pallas as pl
from jax.experimental.pallas import tpu as pltpu

F32 = jnp.float32
BF = jnp.bfloat16
MESH = pl.DeviceIdType.MESH

D = 1024
HG_HEADS = 8
HG_K = 128
HG_CHUNK = 32
HG_SCALE = HG_K ** -0.5
HG_PER_STEP = 8
RW_HEADS = 16
RW_N = 64
RW_CHUNK = 64
RW_PAIRS_PER_STEP = 8
DFF = 2816
IN_COLS = 9472
RW_COLS = 3328
EPS = 1e-6
GN_EPS = 1e-5 * RW_N
ADAM_LR = 0.001
ADAM_B1 = 0.9
ADAM_B2 = 0.999
ADAM_EPS = 1e-08
ADAM_WD = 0.01
ADAM_STEP = 10
N_DEV = 8
LANES = 128
SUBLANES = 8
VMEM_LIMIT = 56 * 1024 * 1024
TILE_BYTES = 1280 * 1024

REPL = (("attn_pre_norm", 1024), ("hgrn_lb", 1024), ("hgrn_gnorm", 1024), ("rwkv_mu", 3328), ("rwkv_w0", 1024),
        ("rwkv_a0", 1024), ("rwkv_k_k", 1024), ("rwkv_k_a", 1024), ("rwkv_r_k", 1024), ("rwkv_ln_w", 1024),
        ("rwkv_ln_b", 1024), ("attn_post_norm", 1024), ("ffn_pre_norm", 1024), ("conv_b", 5632), ("ffn_post_norm", 1024))
REPL_ROWS = {"hgrn_lb": 2}
REPL_TOTAL = 32


def _cparams(sem=None, **kw):
    return pltpu.CompilerParams(dimension_semantics=sem, vmem_limit_bytes=VMEM_LIMIT, **kw)


_DN = {"nn": ((1,), (0,)), "nt": ((1,), (1,)), "tn": ((0,), (0,))}


def _raw_dot(a, b, mode):
    return lax.dot_general(a.astype(BF), b.astype(BF), (_DN[mode], ((), ())), preferred_element_type=F32)


@functools.partial(jax.custom_vjp, nondiff_argnums=(2,))
def _dot(a, b, mode):
    return _raw_dot(a, b, mode)


def _dot_fwd(a, b, mode):
    return _raw_dot(a, b, mode), (a, b)


def _dot_bwd(mode, res, g):
    a, b = res
    if mode == "nn":
        return _dot(g, b, "nt"), _dot(a, g, "tn")
    if mode == "nt":
        return _dot(g, b, "nn"), _dot(g, a, "tn")
    return _dot(b, g, "nt"), _dot(a, g, "nn")


_dot.defvjp(_dot_fwd, _dot_bwd)


def _bf_pieces(x, n):
    out, r = [], x
    for i in range(n):
        p = r.astype(BF)
        out.append(p)
        if i + 1 < n:
            r = r - p.astype(F32)
    return out


def _raw_split_dot(x, e, mode, n, x_left):
    eb = e.astype(BF)
    acc = None
    for p in _bf_pieces(x, n):
        ops = (p, eb) if x_left else (eb, p)
        t = lax.dot_general(*ops, (_DN[mode], ((), ())), preferred_element_type=F32)
        acc = t if acc is None else acc + t
    return acc


def _raw_headsum(x):
    t = x.shape[0]
    i = lax.broadcasted_iota(jnp.int32, (LANES, LANES), 0)
    j = lax.broadcasted_iota(jnp.int32, (LANES, LANES), 1)
    same = jnp.where((i >= RW_N) == (j >= RW_N), 1.0, 0.0).astype(F32)
    groups = x.shape[1] // LANES
    rows = jnp.concatenate([x[:, q * LANES:(q + 1) * LANES] for q in range(groups)], axis=0)
    s = _raw_split_dot(rows, same, "nn", 2, True)
    return jnp.concatenate([s[q * t:(q + 1) * t] for q in range(groups)], axis=1)


@jax.custom_vjp
def _headsum(x):
    return _raw_headsum(x)


def _headsum_fwd(x):
    return _raw_headsum(x), None


def _headsum_bwd(_, g):
    return (_raw_headsum(g),)


_headsum.defvjp(_headsum_fwd, _headsum_bwd)


@functools.partial(jax.custom_vjp, nondiff_argnums=(2,))
def _tdot(tri, x, n):
    return _raw_split_dot(x, tri, "nn", n, False)


def _tdot_fwd(tri, x, n):
    return _raw_split_dot(x, tri, "nn", n, False), tri


def _tdot_bwd(n, tri, g):
    return jnp.zeros_like(tri), _raw_split_dot(g, tri, "tn", n, False)


_tdot.defvjp(_tdot_fwd, _tdot_bwd)


def _row(x, i):
    r = lax.broadcasted_iota(jnp.int32, x.shape, 0)
    return jnp.sum(jnp.where(r == i, x, 0.0), axis=0, keepdims=True)


def _shift_down(x, prev):
    t = x.shape[0]

    @jax.custom_vjp
    def sh(x, prev):
        r = lax.broadcasted_iota(jnp.int32, x.shape, 0)
        return jnp.where(r == 0, prev, pltpu.roll(x, 1, 0))

    def fwd(x, prev):
        return sh(x, prev), None

    def bwd(_, g):
        r = lax.broadcasted_iota(jnp.int32, g.shape, 0)
        dx = jnp.where(r == t - 1, 0.0, pltpu.roll(g, t - 1, 0))
        return dx, jnp.sum(jnp.where(r == 0, g, 0.0), axis=0, keepdims=True)

    sh.defvjp(fwd, bwd)
    return sh(x, prev)


def _sigmoid(x):
    return jax.nn.sigmoid(x)


@jax.custom_vjp
def _silu(x):
    return x * jax.nn.sigmoid(x)


def _silu_fwd(x):
    s = jax.nn.sigmoid(x)
    return x * s, (x, s)


def _silu_bwd(res, g):
    x, s = res
    return (g * (s * (1.0 + x * (1.0 - s))),)


_silu.defvjp(_silu_fwd, _silu_bwd)


def _softplus(x):
    return jnp.maximum(x, 0.0) + jnp.log(1.0 + jnp.exp(-jnp.abs(x)))


def _rms(x, g):
    return (x * lax.rsqrt(jnp.mean(x * x, axis=-1, keepdims=True) + EPS)) * g


def _tril(c):
    r = lax.broadcasted_iota(jnp.int32, (c, c), 0)
    cc = lax.broadcasted_iota(jnp.int32, (c, c), 1)
    return cc <= r


def _f_pre1_residual(ps, xs, cs):
    return [_rms(xs[0], ps[0]), xs[0]], []


def _f_hgrn(ps, xs, cs):
    lbraw, gn = ps
    hq, hf, hi, hg = xs
    hd = range(HG_PER_STEP)
    st = [cs[0][p * HG_K:(p + 1) * HG_K] for p in hd]
    l0, l1 = _row(lbraw, 0), _row(lbraw, 1)
    m = jnp.maximum(l0, l1)
    e0, e1 = jnp.exp(l0 - m), jnp.exp(l1 - m)
    lb = e0 / (e0 + e1)
    q = _silu(hq) * HG_SCALE
    f = lb + (1.0 - lb) * _sigmoid(hf)
    kh = 1.0 - f
    gl = jnp.log(f)
    c = HG_CHUNK
    low = _tril(c)
    tri = jnp.where(low, 1.0, 0.0).astype(F32)
    outs = []
    for i in range(hq.shape[0] // c):
        rows = slice(i * c, (i + 1) * c)
        b = _tdot(tri, gl[rows], 3)
        bref = _row(b, c // 2 - 1)
        blast = _row(b, c - 1)
        qi = q[rows] * jnp.exp(b - bref)
        ki = kh[rows] * jnp.exp(bref - b)
        qd = q[rows] * jnp.exp(b)
        kd = kh[rows] * jnp.exp(blast - b)
        dec = jnp.exp(blast)
        sl = [slice(p * HG_K, (p + 1) * HG_K) for p in hd]
        sc = [jnp.where(low, _dot(qi[:, sl[p]], ki[:, sl[p]], "nt"), 0.0) for p in hd]
        o = [_dot(sc[p], hi[rows, sl[p]], "nn") + _dot(qd[:, sl[p]], st[p], "nt") for p in hd]
        u = [_dot(hi[rows, sl[p]], kd[:, sl[p]], "tn") for p in hd]
        st = [dec[:, sl[p]] * st[p] + u[p] for p in hd]
        outs.append(jnp.concatenate(o, axis=1) if len(o) > 1 else o[0])
    o = outs[0] if len(outs) == 1 else jnp.concatenate(outs, axis=0)
    on = []
    for p in hd:
        op = o[:, p * HG_K:(p + 1) * HG_K]
        on.append(op * lax.rsqrt(jnp.mean(op * op, axis=-1, keepdims=True) + EPS))
    o = jnp.concatenate(on, axis=1) if len(on) > 1 else on[0]
    o = o * gn
    return [o * _silu(hg)], [jnp.concatenate(st, axis=0) if len(st) > 1 else st[0]]


_RW_OFFS = (0, 1024, 2048, 3072, 3200, 3328)


def _f_rwpre(ps, xs, cs):
    mu, w0, w2p, a0, a2p, g2, k_k, k_a = ps
    (prev,) = cs
    t = xs[0].shape[0]
    zs = []
    for i, z in enumerate(xs):
        lo, hi = _RW_OFFS[i], _RW_OFFS[i + 1]
        zs.append(z + mu[:, lo:hi] * (_shift_down(z, prev[:, lo:hi]) - z))
    rr, kr, vr, wa, gz = zs
    w_log = -_softplus(-(w0 + _dot(jnp.tanh(wa), w2p, "nn"))) - 0.5
    lw = -jnp.exp(w_log)
    a = _sigmoid(a0 + _dot(wa, a2p, "nn"))
    g = _dot(_sigmoid(gz), g2, "nn")
    kkr = kr * k_k
    kk = kkr / jnp.maximum(jnp.sqrt(_headsum(kkr * kkr)), 1e-12)
    k2 = kr * (1.0 + (a - 1.0) * k_a)
    newprev = jnp.concatenate([_row(z, t - 1) for z in xs], axis=1)
    return [rr, lw, k2, vr, -kk, kk * a, g], [newprev]


def _raw_inverses(ls):
    n = ls[0].shape[0]
    r = lax.broadcasted_iota(jnp.int32, (n, n), 0)
    c = lax.broadcasted_iota(jnp.int32, (n, n), 1)
    eye = jnp.where(r == c, 1.0, 0.0).astype(F32)
    tinv = [eye + l for l in ls]
    pw = ls
    for _ in range(5):
        pw = [_raw_dot(p, p, "nn") for p in pw]
        tinv = [t + _raw_dot(t, p, "nn") for t, p in zip(tinv, pw)]
    return tinv


@jax.custom_vjp
def _unit_lower_inverses(ls):
    return _raw_inverses(ls)


def _inverses_fwd(ls):
    tinv = _raw_inverses(ls)
    return tinv, tinv


def _inverses_bwd(tinv, gs):
    return ([_raw_dot(_raw_dot(t, g, "tn"), t, "nt") for t, g in zip(tinv, gs)],)


_unit_lower_inverses.defvjp(_inverses_fwd, _inverses_bwd)


@jax.custom_vjp
def _known_inverses(ls, tinv):
    return tinv


def _known_fwd(ls, tinv):
    return tinv, tinv


def _known_bwd(tinv, gs):
    return [_raw_dot(_raw_dot(t, g, "tn"), t, "nt") for t, g in zip(tinv, gs)], [jnp.zeros_like(t) for t in tinv]


_known_inverses.defvjp(_known_fwd, _known_bwd)


@jax.custom_vjp
def _use_kept(computed, kept):
    return kept


def _use_kept_fwd(computed, kept):
    return kept, None


def _use_kept_bwd(_, g):
    return g, jax.tree.map(jnp.zeros_like, g)


_use_kept.defvjp(_use_kept_fwd, _use_kept_bwd)

RW_KEPT = 5


def _f_rwscan(ps, xs, cs, kept=None):
    state = cs[0]
    ys, keep = [], []
    n = 2 * RW_CHUNK
    per_chunk = RW_KEPT * RW_PAIRS_PER_STEP * n
    for i in range(xs[0].shape[0] // RW_CHUNK):
        known = None
        if kept is not None:
            known = [[kept[i * per_chunk + (q * RW_PAIRS_PER_STEP + p) * n:
                           i * per_chunk + (q * RW_PAIRS_PER_STEP + p + 1) * n] for p in range(RW_PAIRS_PER_STEP)]
                     for q in range(RW_KEPT)]
        y, state, mats = _rwkv_chunk([x[i * RW_CHUNK:(i + 1) * RW_CHUNK] for x in xs], state, known)
        ys.append(y)
        keep += [m for group in mats for m in group]
    return [ys[0] if len(ys) == 1 else jnp.concatenate(ys, axis=0)], [state], jnp.concatenate(keep, axis=0)


def _rwkv_chunk(xs, state, known=None):
    npair = RW_PAIRS_PER_STEP
    pr = range(npair)
    r, lw, k, v, av, bv = [[x[:, p * LANES:(p + 1) * LANES] for p in pr] for x in xs]
    sv = [state[p * LANES:(p + 1) * LANES] for p in pr]
    c = RW_CHUNK
    n = 2 * c
    tri = jnp.where(_tril(c), 1.0, 0.0).astype(F32)
    cl = [_tdot(tri, lw[p], 3) for p in pr]
    cl_last = [_row(cl[p], c - 1) for p in pr]
    lane = lax.broadcasted_iota(jnp.int32, (c, LANES), 1)
    h0 = lane < RW_N

    def stack(x):
        return jnp.concatenate([jnp.where(h0, x, 0.0), jnp.where(h0, 0.0, x)], axis=0)

    am = [stack(av[p] * jnp.exp(cl[p] - lw[p])) for p in pr]
    bm = [stack(bv[p] * jnp.exp(-cl[p])) for p in pr]
    km = [stack(k[p] * jnp.exp(-cl[p])) for p in pr]
    rm = [stack(r[p] * jnp.exp(cl[p])) for p in pr]
    vm = [stack(v[p]) for p in pr]
    rn = lax.broadcasted_iota(jnp.int32, (n, n), 0)
    cn = lax.broadcasted_iota(jnp.int32, (n, n), 1)
    blk = (rn >= c) == (cn >= c)
    strict = blk & (cn < rn)
    incl = blk & (cn <= rn)
    lab = [jnp.where(strict, _dot(am[p], bm[p], "nt"), 0.0) for p in pr]
    lak = [jnp.where(strict, _dot(am[p], km[p], "nt"), 0.0) for p in pr]
    wrb = [jnp.where(incl, _dot(rm[p], bm[p], "nt"), 0.0) for p in pr]
    wrk = [jnp.where(incl, _dot(rm[p], km[p], "nt"), 0.0) for p in pr]
    if known is None:
        tinv = _unit_lower_inverses(lab)
    else:
        tinv = _known_inverses(lab, known[0])
        lak, wrb, wrk = _use_kept(lak, known[1]), _use_kept(wrb, known[2]), _use_kept(wrk, known[3])
    rhs = [_dot(am[p], sv[p], "nt") + _dot(lak[p], vm[p], "nn") for p in pr]
    um = [_dot(tinv[p], rhs[p], "nn") for p in pr]
    if known is not None:
        um = _use_kept(um, known[4])
    ym = [_dot(rm[p], sv[p], "nt") + _dot(wrb[p], um[p], "nn") + _dot(wrk[p], vm[p], "nn") for p in pr]
    sn = [(sv[p] + _dot(um[p], bm[p], "tn") + _dot(vm[p], km[p], "tn")) * jnp.exp(cl_last[p]) for p in pr]
    ys = [ym[p][:c] + ym[p][c:] for p in pr]
    return jnp.concatenate(ys, axis=1), jnp.concatenate(sn, axis=0), [tinv, lak, wrb, wrk, um]


def _f_mixers(ps, xs, cs):
    return _mixers(ps, xs, cs, None)


def _f_mixers_kept(ps, xs, cs, kept):
    return _mixers(ps, xs, cs, kept[0])[:2]


def _mixers(ps, xs, cs, kept):
    oa, st = _f_hgrn(ps[:2], xs[:4], cs[:1])
    (r, lw, k, v, av, bv, g), prev = _f_rwpre(ps[2:10], xs[4:], cs[1:2])
    y, sv, keep = _f_rwscan([], [r, lw, k, v, av, bv], cs[2:], kept)
    ob, _ = _f_rwpost(ps[10:], y + [r, k, v, g], [])
    return oa + ob, st + prev + sv, [keep]


def _f_rwpost(ps, xs, cs):
    ln_w, ln_b, r_k = ps
    y, r, k, v, g = xs
    inv_n = 1.0 / RW_N
    yc = y - _headsum(y) * inv_n
    var = _headsum(yc * yc) * inv_n
    yn = yc * lax.rsqrt(var + GN_EPS)
    yn = yn * ln_w + ln_b
    bonus = _headsum(r * k * r_k) * v
    return [(yn + bonus) * g], []


def _f_merge(ps, xs, cs):
    ga, gb, ya, yb = xs
    return [_sigmoid(ga) * ya + _sigmoid(gb) * yb], []


def _f_post1(ps, xs, cs):
    x, mix = xs
    h1 = x + _rms(mix, ps[0])
    return [h1, _rms(h1, ps[1])], []


def _f_conv(ps, xs, cs):
    cw, cb = ps
    p1, p2 = cs
    w0, w1, w2 = _row(cw, 0), _row(cw, 1), _row(cw, 2)
    t = xs[0].shape[0]
    hc = []
    for i, x in enumerate(xs):
        sl = slice(i * DFF, (i + 1) * DFF)
        s1 = _shift_down(x, p1[:, sl])
        s2 = _shift_down(s1, p2[:, sl])
        hc.append(cb[:, sl] + w0[:, sl] * s2 + w1[:, sl] * s1 + w2[:, sl] * x)
    n1 = jnp.concatenate([_row(x, t - 1) for x in xs], axis=1)
    n2 = jnp.concatenate([_row(x, t - 2) for x in xs], axis=1)
    return [_silu(hc[0]) * hc[1]], [n1, n2]


class _Stage:
    def __init__(self, name, f, g, tm, par_per_g, in_pieces, in_offs, carry_shapes, out_pieces, out_dtypes,
                 kept_shapes=(), f_kept=None):
        self.name, self.f, self.g, self.tm = name, f, g, tm
        self.par_per_g, self.in_pieces, self.in_offs = par_per_g, in_pieces, in_offs
        self.carry_shapes, self.out_pieces, self.out_dtypes = carry_shapes, out_pieces, out_dtypes
        self.kept_shapes, self.f_kept = list(kept_shapes), f_kept


def _par_spec(arr, per_g, g):
    r, c = arr.shape
    if per_g:
        return pl.BlockSpec((r, c // g), lambda gi, ni: (0, gi))
    return pl.BlockSpec((r, c), lambda gi, ni: (0, 0))


def _row_spec(tm, width, off, n, rev):
    if rev:
        return pl.BlockSpec((tm, width), lambda gi, ni: (n - 1 - ni, off + gi))
    return pl.BlockSpec((tm, width), lambda gi, ni: (ni, off + gi))


def _carry_spec(shape, n, rev):
    if rev:
        return pl.BlockSpec((None, None) + shape, lambda gi, ni: (gi, n - 1 - ni, 0, 0))
    return pl.BlockSpec((None, None) + shape, lambda gi, ni: (gi, ni, 0, 0))


def _load_pieces(refs, pieces_list):
    out = []
    for ref, pieces in zip(refs, pieces_list):
        o = 0
        for w in pieces:
            out.append(ref[:, o:o + w].astype(F32))
            o += w
    return out


def _store_pieces(refs, pieces_list, vals):
    k = 0
    for ref, pieces in zip(refs, pieces_list):
        o = 0
        for w in pieces:
            ref[:, o:o + w] = vals[k].astype(ref.dtype)
            k += 1
            o += w


_ANY = pl.BlockSpec(memory_space=pl.ANY)


class _Exchange:
    def __init__(self, kind, arrs):
        self.kind, self.arrs, self.results = kind, list(arrs), None
        if kind == "scatter":
            self.out_shape = [jax.ShapeDtypeStruct((N_DEV - 1,) + a.shape[1:], a.dtype) for a in self.arrs]
        else:
            self.out_shape = [jax.ShapeDtypeStruct((N_DEV,) + a.shape, a.dtype) for a in self.arrs]
        self.nsem = (N_DEV if kind == "gather2" else N_DEV - 1) * len(self.arrs)

    def copies(self, in_refs, out_refs, ssem, rsem):
        x, y, c = lax.axis_index("x"), lax.axis_index("y"), lax.axis_index("c")
        me = 4 * x + 2 * y + c
        cps = []
        for a, (i_ref, o_ref) in enumerate(zip(in_refs, out_refs)):
            for j in range(1, N_DEV):
                px = 1 - x if j & 4 else x
                py = 1 - y if j & 2 else y
                pc = 1 - c if j & 1 else c
                if self.kind == "gather":
                    src, dst = i_ref, o_ref.at[me]
                else:
                    src, dst = i_ref.at[4 * px + 2 * py + pc], o_ref.at[j - 1]
                s = (N_DEV - 1) * a + j - 1
                cps.append(pltpu.make_async_remote_copy(src_ref=src, dst_ref=dst, send_sem=ssem.at[s],
                                                        recv_sem=rsem.at[s], device_id=(px, py, pc),
                                                        device_id_type=MESH))
        return cps

    def run(self, step, total, in_refs, out_refs, ssem, rsem):
        if self.kind == "gather2":
            return self.run_two_level(step, total, in_refs, out_refs, ssem, rsem)

        @pl.when(step == 0)
        def _():
            for cp in self.copies(in_refs, out_refs, ssem, rsem):
                cp.start()

        @pl.when(step == total - 1)
        def _():
            for cp in self.copies(in_refs, out_refs, ssem, rsem):
                cp.wait()

    def run_two_level(self, step, total, in_refs, out_refs, ssem, rsem):
        x, y, c = lax.axis_index("x"), lax.axis_index("y"), lax.axis_index("c")
        sibling, xn, yn = (x, y, 1 - c), (1 - x, y, c), (x, 1 - y, c)
        arrs = range(len(in_refs))
        ns = N_DEV

        def num(px, py, pc):
            return 4 * px + 2 * py + pc

        def copy(a, k, to, src, dst):
            return pltpu.make_async_remote_copy(src_ref=src, dst_ref=dst, send_sem=ssem.at[ns * a + k],
                                                recv_sem=rsem.at[ns * a + k], device_id=to, device_id_type=MESH)

        def blk(a, b):
            return out_refs[a].at[b]

        def half(a, b, second):
            h = self.arrs[a].shape[0] // 2
            return out_refs[a].at[b, pl.ds(h if second else 0, h)]

        bx, by, bd = num(1 - x, y, c), num(x, 1 - y, c), num(1 - x, 1 - y, c)

        def firsts(a):
            own = blk(a, num(x, y, c))
            return [copy(a, 0, sibling, in_refs[a], own), copy(a, 1, xn, in_refs[a], own),
                    copy(a, 2, yn, in_refs[a], own)]

        def seconds(a):
            return [copy(a, 3, yn, half(a, bx, False), half(a, bx, False)), copy(a, 5, sibling, blk(a, bx), blk(a, bx)),
                    copy(a, 4, xn, half(a, by, True), half(a, by, True)), copy(a, 6, sibling, blk(a, by), blk(a, by))]

        def third(a):
            return copy(a, 7, sibling, blk(a, bd), blk(a, bd))

        @pl.when(step == 0)
        def _():
            for a in arrs:
                for cp in firsts(a):
                    cp.start()

        @pl.when(step == total // 2)
        def _():
            for a in arrs:
                copy(a, 1, xn, blk(a, bx), blk(a, bx)).wait_recv()
                copy(a, 2, yn, blk(a, by), blk(a, by)).wait_recv()
                for cp in seconds(a):
                    cp.start()

        @pl.when(step == (4 * total) // 5)
        def _():
            for a in arrs:
                copy(a, 3, yn, half(a, bd, False), half(a, bd, False)).wait_recv()
                copy(a, 4, xn, half(a, bd, True), half(a, bd, True)).wait_recv()
                third(a).start()

        @pl.when(step == total - 1)
        def _():
            for a in arrs:
                for k, b in ((0, num(x, y, 1 - c)), (5, num(1 - x, y, 1 - c)), (6, num(x, 1 - y, 1 - c)),
                             (7, num(1 - x, 1 - y, 1 - c))):
                    copy(a, k, sibling, blk(a, b), blk(a, b)).wait_recv()
                for cp in firsts(a) + seconds(a) + [third(a)]:
                    cp.wait_send()


def _hook_specs(hook):
    if hook is None:
        return [], [], [], []
    na = len(hook.arrs)
    sems = [pltpu.SemaphoreType.DMA((hook.nsem,)), pltpu.SemaphoreType.DMA((hook.nsem,))]
    return [_ANY] * na, [_ANY] * na, hook.out_shape, sems


def _stage_fwd(st, t, params, inputs, hook=None):
    g, tm = st.g, min(st.tm, t)
    n = t // tm
    npar, nin, ncar, nout = len(params), len(inputs), len(st.carry_shapes), len(st.out_pieces)
    nk = len(st.kept_shapes)
    h_in, h_out, h_shape, h_sems = _hook_specs(hook)
    nh = len(h_in)

    def body(*refs):
        p_refs = refs[:npar]
        x_refs = refs[npar:npar + nin]
        hi_refs = refs[npar + nin:npar + nin + nh]
        o = npar + nin + nh
        o_refs = refs[o:o + nout]
        s_refs = refs[o + nout:o + nout + ncar]
        k_refs = refs[o + nout + ncar:o + nout + ncar + nk]
        o += nout + ncar + nk
        ho_refs = refs[o:o + nh]
        c_scr = refs[o + nh:o + nh + ncar]
        gi, ni = pl.program_id(0), pl.program_id(1)
        if hook is not None:
            step = gi * n + ni
            hook.run(step, g * n, hi_refs, ho_refs, *refs[-2:])

        @pl.when(ni == 0)
        def _():
            for c in c_scr:
                c[...] = jnp.zeros(c.shape, F32)

        ps = [r[...].astype(F32) for r in p_refs]
        xs = _load_pieces(x_refs, st.in_pieces)
        cs = [c[...] for c in c_scr]
        for s, c in zip(s_refs, cs):
            s[...] = c
        res = st.f(ps, xs, cs)
        outs, ncs = res[0], res[1]
        _store_pieces(o_refs, st.out_pieces, outs)
        for c, v in zip(c_scr, ncs):
            c[...] = v
        for kr, kv in zip(k_refs, res[2] if nk else []):
            kr[...] = kv.astype(kr.dtype)

    in_specs = [_par_spec(p, pg, g) for p, pg in zip(params, st.par_per_g)]
    in_specs += [_row_spec(tm, sum(pc), off, n, False) for pc, off in zip(st.in_pieces, st.in_offs)]
    out_specs = [_row_spec(tm, sum(pc), 0, n, False) for pc in st.out_pieces]
    out_specs += [_carry_spec(s, n, False) for s in st.carry_shapes]
    out_specs += [pl.BlockSpec(s, lambda gi, ni: (ni, 0)) for s in st.kept_shapes]
    out_shape = [jax.ShapeDtypeStruct((t, g * sum(pc)), dt) for pc, dt in zip(st.out_pieces, st.out_dtypes)]
    out_shape += [jax.ShapeDtypeStruct((g, n) + s, F32) for s in st.carry_shapes]
    out_shape += [jax.ShapeDtypeStruct((n * s[0], s[1]), BF) for s in st.kept_shapes]
    res = pl.pallas_call(
        body, name=st.name + "_fwd", grid=(g, n), in_specs=in_specs + h_in, out_specs=out_specs + h_out,
        out_shape=out_shape + h_shape,
        scratch_shapes=[pltpu.VMEM(s, F32) for s in st.carry_shapes] + h_sems,
        compiler_params=_cparams(("arbitrary", "arbitrary")),
    )(*params, *inputs, *(hook.arrs if hook else []))
    if hook is not None:
        hook.results = list(res[nout + ncar + nk:])
    return list(res[:nout]), list(res[nout:nout + ncar + nk])


def _stage_bwd(st, t, params, inputs, saved, douts, dx_dtypes, hook=None, dout_dot=None):
    g, tm = st.g, min(st.tm, t)
    n = t // tm
    npar, nin, ncar = len(params), len(inputs), len(st.carry_shapes)
    nk = len(st.kept_shapes)
    flat_d = list(dout_dot) if dout_dot is not None else [d for ds in douts for d in ds]
    nd = len(flat_d)
    dx_idx = [i for i, dt in enumerate(dx_dtypes) if dt is not None]
    h_in, h_out, h_shape, h_sems = _hook_specs(hook)
    nh = len(h_in)

    def body(*refs):
        p_refs = refs[:npar]
        x_refs = refs[npar:npar + nin]
        s_refs = refs[npar + nin:npar + nin + ncar]
        k_refs = refs[npar + nin + ncar:npar + nin + ncar + nk]
        o = npar + nin + ncar + nk
        d_refs = refs[o:o + nd]
        hi_refs = refs[o + nd:o + nd + nh]
        o += nd + nh
        dp_refs = refs[o:o + npar]
        dx_refs = refs[o + npar:o + npar + len(dx_idx)]
        ho_refs = refs[o + npar + len(dx_idx):o + npar + len(dx_idx) + nh]
        dc_scr = refs[o + npar + len(dx_idx) + nh:o + npar + len(dx_idx) + nh + ncar]
        gi, ni = pl.program_id(0), pl.program_id(1)
        if hook is not None:
            step = gi * n + ni
            hook.run(step, g * n, hi_refs, ho_refs, *refs[-2:])

        @pl.when(ni == 0)
        def _():
            for c in dc_scr:
                c[...] = jnp.zeros(c.shape, F32)

        ps = [r[...].astype(F32) for r in p_refs]
        xs = _load_pieces(x_refs, st.in_pieces)
        cs = [s[...] for s in s_refs]
        dys = [_raw_dot(d_refs[0][...], d_refs[1][...], "nt")] if dout_dot is not None else []
        k = 0
        for ds, pieces in zip(douts, st.out_pieces):
            acc = _load_pieces([d_refs[k]], [pieces])
            for j in range(1, len(ds)):
                more = _load_pieces([d_refs[k + j]], [pieces])
                acc = [a + b for a, b in zip(acc, more)]
            dys += acc
            k += len(ds)
        if nk:
            kept = [r[...].astype(F32) for r in k_refs]
            _, vjp = jax.vjp(lambda p, x, c: st.f_kept(p, x, c, kept), ps, xs, cs)
        else:
            _, vjp = jax.vjp(st.f, ps, xs, cs)
        dps, dxs, dcs = vjp((dys, [c[...] for c in dc_scr]))
        k = 0
        per_in = []
        for pieces in st.in_pieces:
            per_in.append(dxs[k:k + len(pieces)])
            k += len(pieces)
        for ref, i in zip(dx_refs, dx_idx):
            _store_pieces([ref], [st.in_pieces[i]], per_in[i])
        for c, v in zip(dc_scr, dcs):
            c[...] = v
        for ref, dp, pg in zip(dp_refs, dps, st.par_per_g):
            first = (ni == 0) if pg else ((ni == 0) & (gi == 0))

            @pl.when(first)
            def _():
                ref[...] = jnp.zeros(ref.shape, F32)

            ref[...] += dp

    in_specs = [_par_spec(p, pg, g) for p, pg in zip(params, st.par_per_g)]
    in_specs += [_row_spec(tm, sum(pc), off, n, True) for pc, off in zip(st.in_pieces, st.in_offs)]
    in_specs += [_carry_spec(s, n, True) for s in st.carry_shapes]
    in_specs += [pl.BlockSpec(s, lambda gi, ni: (n - 1 - ni, 0)) for s in st.kept_shapes]
    for ds, pc in zip(douts, st.out_pieces):
        in_specs += [_row_spec(tm, sum(pc), 0, n, True) for _ in ds]
    if dout_dot is not None:
        a, w = dout_dot
        in_specs += [pl.BlockSpec((tm, a.shape[1]), lambda gi, ni: (n - 1 - ni, 0)),
                     pl.BlockSpec(w.shape, lambda gi, ni: (0, 0), pipeline_mode=pl.Buffered(1))]
    out_specs = [_par_spec(p, pg, g) for p, pg in zip(params, st.par_per_g)]
    out_specs += [_row_spec(tm, sum(st.in_pieces[i]), 0, n, True) for i in dx_idx]
    out_shape = [jax.ShapeDtypeStruct(p.shape, F32) for p in params]
    out_shape += [jax.ShapeDtypeStruct((t, g * sum(st.in_pieces[i])), dx_dtypes[i]) for i in dx_idx]
    res = pl.pallas_call(
        body, name=st.name + "_bwd", grid=(g, n), in_specs=in_specs + h_in, out_specs=out_specs + h_out,
        out_shape=out_shape + h_shape,
        scratch_shapes=[pltpu.VMEM(s, F32) for s in st.carry_shapes] + h_sems,
        compiler_params=_cparams(("arbitrary", "arbitrary")),
    )(*params, *inputs, *saved, *flat_d, *(hook.arrs if hook else []))
    if hook is not None:
        hook.results = list(res[npar + len(dx_idx):])
    return list(res[:npar]), list(res[npar:npar + len(dx_idx)])


def _pick(n, cap):
    if n <= cap:
        return n
    best = LANES
    for k in range(1, n // LANES + 1):
        if (n // LANES) % k == 0 and k * LANES <= cap:
            best = k * LANES
    return best


def _mm(name, a, b, mode, out_dtype=F32, tm=1024, tn=512, b_outer=False):
    m = a.shape[1] if mode == "tn" else a.shape[0]
    k = a.shape[0] if mode == "tn" else a.shape[1]
    n = b.shape[0] if mode == "nt" else b.shape[1]
    tm, tn = _pick(m, tm), _pick(n, tn)
    if b_outer:
        grid = (n // tn, m // tm)
        ij = lambda p, q: (q, p)
    else:
        grid = (m // tm, n // tn)
        ij = lambda p, q: (p, q)

    def body(a_ref, b_ref, o_ref):
        o_ref[...] = _raw_dot(a_ref[...], b_ref[...], mode).astype(o_ref.dtype)

    if mode == "tn":
        a_spec = pl.BlockSpec((k, tm), lambda p, q: (0, ij(p, q)[0]))
    else:
        a_spec = pl.BlockSpec((tm, k), lambda p, q: (ij(p, q)[0], 0))
    b_mode = dict(pipeline_mode=pl.Buffered(1)) if tn == n else {}
    if mode == "nt":
        b_spec = pl.BlockSpec((tn, k), lambda p, q: (ij(p, q)[1], 0), **b_mode)
    else:
        b_spec = pl.BlockSpec((k, tn), lambda p, q: (0, ij(p, q)[1]), **b_mode)
    return pl.pallas_call(
        body, name=name, grid=grid, in_specs=[a_spec, b_spec],
        out_specs=pl.BlockSpec((tm, tn), lambda p, q: ij(p, q)),
        out_shape=jax.ShapeDtypeStruct((m, n), out_dtype),
        compiler_params=_cparams(("arbitrary", "arbitrary")),
    )(a, b)


def _mm_multi(name, pairs, mode, out_dtype, tm=1024, tn=512):
    a0, b0 = pairs[0]
    m = a0.shape[1] if mode == "tn" else a0.shape[0]
    k = a0.shape[0] if mode == "tn" else a0.shape[1]
    n = b0.shape[0] if mode == "nt" else b0.shape[1]
    tm, tn = _pick(m, tm), _pick(n, tn)
    npair = len(pairs)

    def body(*refs):
        for p in range(npair):
            refs[2 * npair + p][...] = _raw_dot(refs[2 * p][...], refs[2 * p + 1][...], mode).astype(out_dtype)

    a_spec = pl.BlockSpec((k, tm), lambda i, j: (0, i)) if mode == "tn" else pl.BlockSpec((tm, k), lambda i, j: (i, 0))
    b_spec = pl.BlockSpec((tn, k), lambda i, j: (j, 0)) if mode == "nt" else pl.BlockSpec((k, tn), lambda i, j: (0, j))
    return pl.pallas_call(
        body, name=name, grid=(m // tm, n // tn), in_specs=[a_spec, b_spec] * npair,
        out_specs=[pl.BlockSpec((tm, tn), lambda i, j: (i, j))] * npair,
        out_shape=[jax.ShapeDtypeStruct((m, n), out_dtype)] * npair,
        compiler_params=_cparams(("arbitrary", "arbitrary")),
    )(*[x for pair in pairs for x in pair])


def _mm_cols_tn(name, pieces, b, out_dtype, tm):
    k, n = b.shape
    counts = [p.shape[1] // tm for p in pieces]
    starts = [sum(counts[:i]) for i in range(len(pieces))]
    na = len(pieces)

    def body(*refs):
        b_ref, o_ref = refs[na], refs[-1]
        i = pl.program_id(0)
        for a_ref, s, c in zip(refs[:na], starts, counts):
            @pl.when((i >= s) & (i < s + c))
            def _():
                o_ref[...] = _raw_dot(a_ref[...], b_ref[...], "tn").astype(o_ref.dtype)

    def spec(s, c):
        return pl.BlockSpec((k, tm), lambda i: (0, jnp.clip(i - s, 0, c - 1)))

    return pl.pallas_call(
        body, name=name, grid=(sum(counts),),
        in_specs=[spec(s, c) for s, c in zip(starts, counts)]
        + [pl.BlockSpec(b.shape, lambda i: (0, 0), pipeline_mode=pl.Buffered(1))],
        out_specs=pl.BlockSpec((tm, n), lambda i: (i, 0)),
        out_shape=jax.ShapeDtypeStruct((sum(counts) * tm, n), out_dtype),
        compiler_params=_cparams(("arbitrary",)),
    )(*pieces, b)


def _norm_in_proj(x, g, w_t, tm, tn):
    t, k = x.shape
    n = w_t.shape[0]
    tm, tn = _pick(t, tm), _pick(n, tn)

    def body(x_ref, g_ref, w_ref, xn_ref, z_ref):
        xn = _rms(x_ref[...], g_ref[...]).astype(BF)
        xn_ref[...] = xn
        z_ref[...] = _raw_dot(xn, w_ref[...], "nt")

    xns, z = pl.pallas_call(
        body, name="in_proj", grid=(n // tn, t // tm),
        in_specs=[pl.BlockSpec((tm, k), lambda j, i: (i, 0)), pl.BlockSpec((1, k), lambda j, i: (0, 0)),
                  pl.BlockSpec((tn, k), lambda j, i: (j, 0))],
        out_specs=[pl.BlockSpec((None, tm, k), lambda j, i: (j, i, 0)), pl.BlockSpec((tm, tn), lambda j, i: (i, j))],
        out_shape=[jax.ShapeDtypeStruct((n // tn, t, k), BF), jax.ShapeDtypeStruct((t, n), F32)],
        compiler_params=_cparams(("arbitrary", "arbitrary")),
    )(x, g, w_t)
    return xns[0], z


def _merge_out_post(z, o_a, o_b, w_a, w_b, w_out, x, g_post, g_pre2, tm):
    t = x.shape[0]
    tm = _pick(t, tm)
    w = 256
    npc = D // w
    ga0, gb0 = (IN_COLS - 2 * D) // w, (IN_COLS - D) // w

    def body(*refs):
        ga_refs, gb_refs = refs[:npc], refs[npc:2 * npc]
        oa_ref, ob_ref, wa_ref, wb_ref, w_ref, x_ref, gp_ref, g2_ref = refs[2 * npc:2 * npc + 8]
        ya_ref, yb_ref, m_ref, mix_ref, h_ref, xn_ref = refs[2 * npc + 8:]
        ya = _raw_dot(oa_ref[...], wa_ref[...], "nn").astype(BF)
        yb = _raw_dot(ob_ref[...], wb_ref[...], "nn").astype(BF)
        ya_ref[...] = ya
        yb_ref[...] = yb
        parts = []
        for p in range(npc):
            cols = slice(p * w, (p + 1) * w)
            parts.append(_sigmoid(ga_refs[p][...]) * ya[:, cols].astype(F32)
                         + _sigmoid(gb_refs[p][...]) * yb[:, cols].astype(F32))
        merged = jnp.concatenate(parts, axis=1).astype(BF)
        m_ref[...] = merged
        mix = _raw_dot(merged, w_ref[...], "nn")
        mix_ref[...] = mix
        h1 = x_ref[...] + _rms(mix, gp_ref[...])
        h_ref[...] = h1
        xn_ref[...] = _rms(h1, g2_ref[...]).astype(BF)

    row = pl.BlockSpec((tm, D), lambda i: (i, 0))
    one = pl.BlockSpec((1, D), lambda i: (0, 0))

    def gate(b0):
        return [pl.BlockSpec((tm, w), functools.partial(lambda i, b: (i, b), b=b0 + p)) for p in range(npc)]

    wgt = pl.BlockSpec((D, D), lambda i: (0, 0), pipeline_mode=pl.Buffered(1))
    return pl.pallas_call(
        body, name="merge_out_post", grid=(t // tm,),
        in_specs=gate(ga0) + gate(gb0) + [row, row, wgt, wgt, wgt, row, one, one],
        out_specs=[row] * 6,
        out_shape=[jax.ShapeDtypeStruct((t, D), BF), jax.ShapeDtypeStruct((t, D), BF), jax.ShapeDtypeStruct((t, D), BF),
                   jax.ShapeDtypeStruct((t, D), F32), jax.ShapeDtypeStruct((t, D), F32),
                   jax.ShapeDtypeStruct((t, D), BF)],
        compiler_params=_cparams(("arbitrary",)),
    )(*([z] * (2 * npc)), o_a, o_b, w_a, w_b, w_out, x, g_post, g_pre2)


def _accumulate(ni, refs, vals):
    @pl.when(ni == 0)
    def _():
        for r in refs:
            r[...] = jnp.zeros(r.shape, F32)

    for r, v in zip(refs, vals):
        r[...] += v


def _dmerged_merge_bwd(dmix, w_out, w_a, w_b, z, y_a, y_b, tm):
    t = dmix.shape[0]
    tm = _pick(t, tm)
    w = 256
    npc = D // w
    ga0, gb0 = (IN_COLS - 2 * D) // w, (IN_COLS - D) // w

    def body(*refs):
        dm_ref, w_ref, wa_ref, wb_ref = refs[:4]
        ga_refs, gb_refs = refs[4:4 + npc], refs[4 + npc:4 + 2 * npc]
        ya_ref, yb_ref, dga_ref, dgb_ref, dya_ref, dyb_ref, doa_ref, dob_ref = refs[4 + 2 * npc:]
        dmerged = _raw_dot(dm_ref[...], w_ref[...], "nt")
        dyas, dybs = [], []
        for p in range(npc):
            cols = slice(p * w, (p + 1) * w)
            xs = [ga_refs[p][...], gb_refs[p][...], ya_ref[:, cols].astype(F32), yb_ref[:, cols].astype(F32)]
            _, vjp = jax.vjp(lambda *a: _f_merge([], list(a), [])[0][0], *xs)
            dga, dgb, dya, dyb = vjp(dmerged[:, cols])
            dga_ref[:, cols] = dga.astype(BF)
            dgb_ref[:, cols] = dgb.astype(BF)
            dyas.append(dya.astype(BF))
            dybs.append(dyb.astype(BF))
        dya, dyb = jnp.concatenate(dyas, axis=1), jnp.concatenate(dybs, axis=1)
        dya_ref[...] = dya
        dyb_ref[...] = dyb
        doa_ref[...] = _raw_dot(dya, wa_ref[...], "nt").astype(BF)
        dob_ref[...] = _raw_dot(dyb, wb_ref[...], "nt").astype(BF)

    row = pl.BlockSpec((tm, D), lambda i: (i, 0))
    wgt = pl.BlockSpec((D, D), lambda i: (0, 0), pipeline_mode=pl.Buffered(1))

    def gate(b0):
        return [pl.BlockSpec((tm, w), functools.partial(lambda i, b: (i, b), b=b0 + p)) for p in range(npc)]

    return pl.pallas_call(
        body, name="merge_bwd", grid=(t // tm,),
        in_specs=[row, wgt, wgt, wgt] + gate(ga0) + gate(gb0) + [row, row],
        out_specs=[row] * 6, out_shape=[jax.ShapeDtypeStruct((t, D), BF)] * 6,
        compiler_params=_cparams(("arbitrary",)),
    )(dmix, w_out, w_a, w_b, *([z] * (2 * npc)), y_a, y_b)


def _dxn2_post1_bwd(pieces, w_up_t, x, mix, dh1, g_post, g_pre2, tm):
    t = x.shape[0]
    tm = _pick(t, tm)
    k = w_up_t.shape[0]
    offs = [sum(p.shape[1] for p in pieces[:i]) for i in range(len(pieces))]
    na = len(pieces)

    def body(*refs):
        w_ref, x_ref, m_ref, dh_ref, gp_ref, g2_ref, dgp_ref, dg2_ref, dx_ref, dm_ref = refs[na:]
        dxn2 = None
        for a_ref, off in zip(refs[:na], offs):
            part = _raw_dot(a_ref[...], w_ref[off:off + a_ref.shape[1], :], "nn")
            dxn2 = part if dxn2 is None else dxn2 + part
        _, vjp = jax.vjp(lambda gp, g2, xx, mm: _f_post1([gp, g2], [xx, mm], [])[0],
                         gp_ref[...], g2_ref[...], x_ref[...], m_ref[...])
        dgp, dg2, dx, dm = vjp([dh_ref[...], dxn2])
        _accumulate(pl.program_id(0), [dgp_ref, dg2_ref], [dgp, dg2])
        dx_ref[...] = dx
        dm_ref[...] = dm.astype(BF)

    row = pl.BlockSpec((tm, D), lambda i: (i, 0))
    one = pl.BlockSpec((1, D), lambda i: (0, 0))
    return pl.pallas_call(
        body, name="post1_bwd", grid=(t // tm,),
        in_specs=[pl.BlockSpec((tm, p.shape[1]), lambda i: (i, 0)) for p in pieces]
        + [pl.BlockSpec((k, D), lambda i: (0, 0), pipeline_mode=pl.Buffered(1)), row, row, row, one, one],
        out_specs=[one, one, row, row],
        out_shape=[jax.ShapeDtypeStruct((1, D), F32), jax.ShapeDtypeStruct((1, D), F32),
                   jax.ShapeDtypeStruct((t, D), F32), jax.ShapeDtypeStruct((t, D), BF)],
        compiler_params=_cparams(("arbitrary",)),
    )(*pieces, w_up_t, x, mix, dh1, g_post, g_pre2)


def _conv_taps(h, cw, cb, p2, p1):
    s1 = _shift_down(h, p1)
    s2 = _shift_down(s1, p2)
    return cb + _row(cw, 0) * s2 + _row(cw, 1) * s1 + _row(cw, 2) * h


def _shift_up(x, last):
    t = x.shape[0]
    r = lax.broadcasted_iota(jnp.int32, x.shape, 0)
    return jnp.where(r == t - 1, last, pltpu.roll(x, t - 1, 0))


def _conv_bwd(dff, w_down, hu_g, hu_v, cg, cv, conv_w, tm):
    t = hu_g.shape[0]
    tm = _pick(t, tm)
    tn = _pick(DFF, 1408)
    nj, n = DFF // tn, t // tm

    def body(dff_ref, w_ref, xg_ref, xv_ref, cg_ref, cv_ref, wg_ref, wv_ref,
             dwg_ref, dwv_ref, dbg_ref, dbv_ref, dxg_ref, dxv_ref, nxt):
        @pl.when(pl.program_id(1) == 0)
        def _():
            nxt[...] = jnp.zeros(nxt.shape, F32)
            for ref in (dwg_ref, dwv_ref, dbg_ref, dbv_ref):
                ref[...] = jnp.zeros(ref.shape, F32)

        dact = _raw_dot(dff_ref[...], w_ref[...], "nt")
        c_g, c_v = cg_ref[...], cv_ref[...]
        s = jax.nn.sigmoid(c_g)
        d_v = dact * (c_g * s)
        d_g = dact * c_v * (s * (1.0 + c_g * (1.0 - s)))
        halves = ((d_g, xg_ref, wg_ref, dwg_ref, dbg_ref, dxg_ref), (d_v, xv_ref, wv_ref, dwv_ref, dbv_ref, dxv_ref))
        for k, (dc, x_ref, cw_ref, dw_ref, db_ref, dx_ref) in enumerate(halves):
            after = nxt[k * SUBLANES:(k + 1) * SUBLANES]
            u1 = _shift_up(dc, _row(after, 0))
            u2 = _shift_up(u1, _row(after, 1))
            cw = cw_ref[...]
            dx_ref[...] = (_row(cw, 2) * dc + _row(cw, 1) * u1 + _row(cw, 0) * u2).astype(dx_ref.dtype)
            x = x_ref[...]
            dw_ref[...] += jnp.concatenate([jnp.sum(u * x, axis=0, keepdims=True) for u in (u2, u1, dc)], axis=0)
            db_ref[...] += jnp.sum(dc, axis=0, keepdims=True)
            nxt[k * SUBLANES:(k + 1) * SUBLANES] = dc[0:SUBLANES]

    def cols(rows, off):
        return pl.BlockSpec((rows, tn), lambda j, i: (0, j + off))

    tile = pl.BlockSpec((tm, tn), lambda j, i: (n - 1 - i, j))
    return pl.pallas_call(
        body, name="conv_bwd", grid=(nj, n),
        in_specs=[pl.BlockSpec((tm, D), lambda j, i: (n - 1 - i, 0)), pl.BlockSpec((tn, D), lambda j, i: (j, 0)),
                  tile, tile, tile, tile, cols(3, 0), cols(3, nj)],
        out_specs=[cols(3, 0), cols(3, 0), cols(1, 0), cols(1, 0), tile, tile],
        out_shape=[jax.ShapeDtypeStruct((3, DFF), F32), jax.ShapeDtypeStruct((3, DFF), F32),
                   jax.ShapeDtypeStruct((1, DFF), F32), jax.ShapeDtypeStruct((1, DFF), F32),
                   jax.ShapeDtypeStruct((t, DFF), BF), jax.ShapeDtypeStruct((t, DFF), BF)],
        scratch_shapes=[pltpu.VMEM((2 * SUBLANES, tn), F32)],
        compiler_params=_cparams(("arbitrary", "arbitrary")),
    )(dff, w_down, hu_g, hu_v, cg, cv, conv_w, conv_w)


def _up_conv(xn2, w_up_t, conv_w, conv_b, tm):
    t = xn2.shape[0]
    tm = _pick(t, tm)
    tn = _pick(DFF, 1408)
    nj = DFF // tn

    def body(x_ref, wg_ref, wv_ref, cwg_ref, cwv_ref, cbg_ref, cbv_ref, hg_ref, hv_ref, act_ref, c1_ref, c2_ref, prev):
        j, i = pl.program_id(0), pl.program_id(1)

        @pl.when(i == 0)
        def _():
            prev[...] = jnp.zeros(prev.shape, F32)

        x = x_ref[...]
        hg = _raw_dot(x, wg_ref[...], "nt")
        hv = _raw_dot(x, wv_ref[...], "nt")
        hg_ref[...] = hg
        hv_ref[...] = hv
        pg, pv = prev[0:SUBLANES], prev[SUBLANES:2 * SUBLANES]
        cg = _conv_taps(hg, cwg_ref[...], cbg_ref[...], _row(pg, SUBLANES - 2), _row(pg, SUBLANES - 1))
        cv = _conv_taps(hv, cwv_ref[...], cbv_ref[...], _row(pv, SUBLANES - 2), _row(pv, SUBLANES - 1))
        act_ref[...] = (_silu(cg) * cv).astype(BF)
        c1_ref[...] = cg
        c2_ref[...] = cv
        prev[0:SUBLANES] = hg[tm - SUBLANES:tm]
        prev[SUBLANES:2 * SUBLANES] = hv[tm - SUBLANES:tm]

    def cols(rows, off):
        return pl.BlockSpec((rows, tn), lambda j, i: (0, j + off))

    tile = pl.BlockSpec((tm, tn), lambda j, i: (i, j))
    return pl.pallas_call(
        body, name="up_conv", grid=(nj, t // tm),
        in_specs=[pl.BlockSpec((tm, D), lambda j, i: (i, 0)), pl.BlockSpec((tn, D), lambda j, i: (j, 0)),
                  pl.BlockSpec((tn, D), lambda j, i: (j + nj, 0)), cols(3, 0), cols(3, nj), cols(1, 0), cols(1, nj)],
        out_specs=[tile, tile, tile, tile, tile],
        out_shape=[jax.ShapeDtypeStruct((t, DFF), F32), jax.ShapeDtypeStruct((t, DFF), F32),
                   jax.ShapeDtypeStruct((t, DFF), BF), jax.ShapeDtypeStruct((t, DFF), F32),
                   jax.ShapeDtypeStruct((t, DFF), F32)],
        scratch_shapes=[pltpu.VMEM((2 * SUBLANES, tn), F32)],
        compiler_params=_cparams(("arbitrary", "arbitrary")),
    )(xn2, w_up_t, w_up_t, conv_w, conv_w, conv_b, conv_b)


def _dxn_pre1_bwd(pieces, w_t, x, dx_res, g, tm, token):
    t = x.shape[0]
    tm = _pick(t, tm)
    offs = [sum(p.shape[1] for p in pieces[:i]) for i in range(len(pieces))]
    na = len(pieces)

    def body(*refs):
        w_ref, x_ref, r_ref, g_ref = refs[na:na + 4]
        dg_ref, dx_ref = refs[-2:]
        dxn = None
        for a_ref, off in zip(refs[:na], offs):
            part = _raw_dot(a_ref[...], w_ref[off:off + a_ref.shape[1], :], "nn")
            dxn = part if dxn is None else dxn + part
        _, vjp = jax.vjp(lambda gg, xx: _f_pre1_residual([gg], [xx], [])[0], g_ref[...], x_ref[...])
        dg, dx = vjp([dxn, r_ref[...]])
        _accumulate(pl.program_id(0), [dg_ref], [dg])
        dx_ref[...] = dx

    row = pl.BlockSpec((tm, D), lambda i: (i, 0))
    one = pl.BlockSpec((1, D), lambda i: (0, 0))
    return pl.pallas_call(
        body, name="pre1_bwd", grid=(t // tm,),
        in_specs=[pl.BlockSpec((tm, p.shape[1]), lambda i: (i, 0)) for p in pieces]
        + [pl.BlockSpec(w_t.shape, lambda i: (0, 0), pipeline_mode=pl.Buffered(1)), row, row, one,
           pl.BlockSpec(token.shape, lambda i: (0, 0))],
        out_specs=[one, row],
        out_shape=[jax.ShapeDtypeStruct((1, D), F32), jax.ShapeDtypeStruct((t, D), F32)],
        compiler_params=_cparams(("arbitrary",)),
    )(*pieces, w_t, x, dx_res, g, token)


def _down_loss(act, w_down, g_post, h1, tgt, tm):
    t, k = act.shape
    tm = _pick(t, tm)

    def body(a_ref, w_ref, g_ref, h_ref, t_ref, loss_ref, dg_ref, dh_ref, df_ref):
        ni = pl.program_id(0)
        ff = _raw_dot(a_ref[...], w_ref[...], "nn")
        target = t_ref[...]

        def lossf(g, h1, ff):
            e = h1 + _rms(ff, g) - target
            return 0.5 * jnp.sum(jnp.mean(e * e, axis=-1))

        l, (dg, dh, df) = jax.value_and_grad(lossf, argnums=(0, 1, 2))(g_ref[...], h_ref[...], ff)

        @pl.when(ni == 0)
        def _():
            loss_ref[...] = jnp.zeros(loss_ref.shape, F32)
            dg_ref[...] = jnp.zeros(dg_ref.shape, F32)

        loss_ref[...] += jnp.full(loss_ref.shape, l, F32)
        dg_ref[...] += dg
        dh_ref[...] = dh
        df_ref[...] = df.astype(df_ref.dtype)

    row = pl.BlockSpec((tm, D), lambda ni: (ni, 0))
    one = pl.BlockSpec((1, D), lambda ni: (0, 0))
    return pl.pallas_call(
        body, name="down_loss", grid=(t // tm,),
        in_specs=[pl.BlockSpec((tm, k), lambda ni: (ni, 0)),
                  pl.BlockSpec((k, D), lambda ni: (0, 0), pipeline_mode=pl.Buffered(1)), one, row, row],
        out_specs=[pl.BlockSpec((1, LANES), lambda ni: (0, 0)), one, row, row],
        out_shape=[jax.ShapeDtypeStruct((1, LANES), F32), jax.ShapeDtypeStruct((1, D), F32),
                   jax.ShapeDtypeStruct((t, D), F32), jax.ShapeDtypeStruct((t, D), BF)],
        compiler_params=_cparams(("arbitrary",)),
    )(act, w_down, g_post, h1, tgt)


_ANY = pl.BlockSpec(memory_space=pl.ANY)


def _all_gather(name, blks):
    na = len(blks)
    ns = 8

    def body(*refs):
        x_refs, out_refs = refs[:na], refs[na:2 * na]
        send_sems, recv_sems, local_sems = refs[2 * na:]
        x, y, cc = lax.axis_index("x"), lax.axis_index("y"), lax.axis_index("c")
        sibling, xn, yn = (x, y, 1 - cc), (1 - x, y, cc), (x, 1 - y, cc)

        def num(px, py, pc):
            return 4 * px + 2 * py + pc

        def copy(a, k, to, src, dst):
            return pltpu.make_async_remote_copy(src_ref=src, dst_ref=dst, send_sem=send_sems.at[ns * a + k],
                                                recv_sem=recv_sems.at[ns * a + k], device_id=to, device_id_type=MESH)

        def halves(a, blk):
            h = blks[a].shape[0] // 2
            return out_refs[a].at[blk, pl.ds(0, h)], out_refs[a].at[blk, pl.ds(h, h)]

        mine, sends = [], []
        for a in range(na):
            o = out_refs[a]
            m = pltpu.make_async_copy(x_refs[a], o.at[num(x, y, cc)], local_sems.at[a])
            m.start()
            mine.append(m)
            own = o.at[num(x, y, cc)]
            sends.append([copy(a, 0, sibling, x_refs[a], own), copy(a, 1, xn, x_refs[a], own),
                          copy(a, 2, yn, x_refs[a], own)])
            for cp in sends[a]:
                cp.start()
        for a in range(na):
            o = out_refs[a]
            bx, by, bd = num(1 - x, y, cc), num(x, 1 - y, cc), num(1 - x, 1 - y, cc)
            copy(a, 1, xn, o.at[bx], o.at[bx]).wait_recv()
            more = [copy(a, 3, yn, halves(a, bx)[0], halves(a, bx)[0]), copy(a, 5, sibling, o.at[bx], o.at[bx])]
            for cp in more:
                cp.start()
            sends[a] += more
        for a in range(na):
            o = out_refs[a]
            bx, by, bd = num(1 - x, y, cc), num(x, 1 - y, cc), num(1 - x, 1 - y, cc)
            copy(a, 2, yn, o.at[by], o.at[by]).wait_recv()
            more = [copy(a, 4, xn, halves(a, by)[1], halves(a, by)[1]), copy(a, 6, sibling, o.at[by], o.at[by])]
            for cp in more:
                cp.start()
            sends[a] += more
        for a in range(na):
            o = out_refs[a]
            bd = num(1 - x, 1 - y, cc)
            copy(a, 3, yn, halves(a, bd)[0], halves(a, bd)[0]).wait_recv()
            copy(a, 4, xn, halves(a, bd)[1], halves(a, bd)[1]).wait_recv()
            fw = copy(a, 7, sibling, o.at[bd], o.at[bd])
            fw.start()
            sends[a].append(fw)
        for a in range(na):
            o = out_refs[a]
            for k, blk in ((0, num(x, y, 1 - cc)), (5, num(1 - x, y, 1 - cc)), (6, num(x, 1 - y, 1 - cc)),
                           (7, num(1 - x, 1 - y, 1 - cc))):
                copy(a, k, sibling, o.at[blk], o.at[blk]).wait_recv()
            for cp in sends[a]:
                cp.wait_send()
        for m in mine:
            m.wait()

    res = pl.pallas_call(
        body, name=name, in_specs=[_ANY] * na, out_specs=[_ANY] * na,
        out_shape=[jax.ShapeDtypeStruct((N_DEV,) + b.shape, b.dtype) for b in blks],
        scratch_shapes=[pltpu.SemaphoreType.DMA((ns * na,)), pltpu.SemaphoreType.DMA((ns * na,)),
                        pltpu.SemaphoreType.DMA((na,))],
    )(*blks)
    return list(res)


def _all_gather_small(name, blk):
    def body(x_ref, out_ref, ssem, rsem, lsem):
        x, y, c = lax.axis_index("x"), lax.axis_index("y"), lax.axis_index("c")
        me = 4 * x + 2 * y + c
        mine = pltpu.make_async_copy(x_ref, out_ref.at[me], lsem)
        mine.start()
        cps = []
        for j in range(1, N_DEV):
            px = 1 - x if j & 4 else x
            py = 1 - y if j & 2 else y
            pc = 1 - c if j & 1 else c
            cps.append(pltpu.make_async_remote_copy(src_ref=x_ref, dst_ref=out_ref.at[me], send_sem=ssem.at[j - 1],
                                                    recv_sem=rsem.at[j - 1], device_id=(px, py, pc),
                                                    device_id_type=MESH))
        for cp in cps:
            cp.start()
        for cp in cps:
            cp.wait()
        mine.wait()

    return pl.pallas_call(
        body, name=name, in_specs=[_ANY], out_specs=_ANY,
        out_shape=jax.ShapeDtypeStruct((N_DEV,) + blk.shape, blk.dtype),
        scratch_shapes=[pltpu.SemaphoreType.DMA((N_DEV - 1,)), pltpu.SemaphoreType.DMA((N_DEV - 1,)),
                        pltpu.SemaphoreType.DMA],
    )(blk)


def _reduce_pair(g8s):
    na = len(g8s)

    def body(*refs):
        g_refs, recv_refs = refs[:na], refs[na:2 * na]
        ssem, rsem = refs[2 * na:]
        x, y, cc = lax.axis_index("x"), lax.axis_index("y"), lax.axis_index("c")
        chips = [(x, y), (1 - x, y), (x, 1 - y), (1 - x, 1 - y)]
        sib = (x, y, 1 - cc)
        for a in range(na):
            for k, (cx, cy) in enumerate(chips):
                pltpu.make_async_remote_copy(
                    src_ref=g_refs[a].at[4 * cx + 2 * cy + 1 - cc], dst_ref=recv_refs[a].at[k],
                    send_sem=ssem.at[a], recv_sem=rsem.at[a], device_id=sib, device_id_type=MESH).start()
        for a in range(na):
            pltpu.make_async_remote_copy(src_ref=recv_refs[a], dst_ref=recv_refs[a], send_sem=ssem.at[a],
                                         recv_sem=rsem.at[a], device_id=sib, device_id_type=MESH).wait()

    res = pl.pallas_call(
        body, name="reduce_pair", in_specs=[_ANY] * na, out_specs=[_ANY] * na,
        out_shape=[jax.ShapeDtypeStruct((4,) + g.shape[1:], g.dtype) for g in g8s],
        scratch_shapes=[pltpu.SemaphoreType.DMA((na,)), pltpu.SemaphoreType.DMA((na,))],
    )(*g8s)
    return list(res)


_HBM = pl.BlockSpec(memory_space=pltpu.HBM)
_SEM = pl.BlockSpec(memory_space=pltpu.SEMAPHORE)
_EFFECT = pltpu.SideEffectType.DATAFLOW_SIDE_EFFECTING


def _chip_swap_copies(s_refs, land_refs, ssem, rsem):
    x, y, c = lax.axis_index("x"), lax.axis_index("y"), lax.axis_index("c")
    targets = [(1 - x, y, c), (x, 1 - y, c), (1 - x, 1 - y, c)]
    return [pltpu.make_async_remote_copy(src_ref=s.at[k], dst_ref=d.at[k], send_sem=ssem.at[3 * a + k],
                                         recv_sem=rsem.at[3 * a + k], device_id=targets[k], device_id_type=MESH)
            for a, (s, d) in enumerate(zip(s_refs, land_refs)) for k in range(3)]


def _chip_swap_start(sends):
    na = len(sends)

    def body(*refs):
        cps = _chip_swap_copies(refs[:na], refs[na:2 * na], refs[2 * na], refs[2 * na + 1])
        for cp in cps:
            cp.start()
        token = refs[-1]
        token[...] = jnp.zeros(token.shape, token.dtype)

    bufs = [pltpu.HBM(s.shape, s.dtype) for s in sends]
    res = pl.pallas_call(
        body, name="chip_swap_start",
        out_shape=[pltpu.SemaphoreType.DMA((3 * na,)), pltpu.SemaphoreType.DMA((3 * na,))] + bufs + bufs
        + [jax.ShapeDtypeStruct((8, LANES), F32)],
        in_specs=[_HBM] * (2 * na), out_specs=[_SEM, _SEM] + [_HBM] * (2 * na) + [pl.BlockSpec(memory_space=pltpu.VMEM)],
        input_output_aliases={i: 2 + i for i in range(2 * na)},
        compiler_params=pltpu.CompilerParams(has_side_effects=_EFFECT),
    )(*[pltpu.with_memory_space_constraint(s, pltpu.HBM) for s in sends],
      *[pltpu.with_memory_space_constraint(lax.empty(s.shape, s.dtype), pltpu.HBM) for s in sends])
    return res[0], res[1], list(res[2:2 + na]), list(res[2 + na:2 + 2 * na]), res[-1]


def _chip_swap_wait(ssem, rsem, srcs, lands, after):
    na = len(srcs)

    def body(*refs):
        cps = _chip_swap_copies(refs[:na], refs[na:2 * na], refs[2 * na], refs[2 * na + 1])
        for cp in cps:
            cp.wait_send()
            cp.wait_recv()

    bufs = [pltpu.HBM(s.shape, s.dtype) for s in srcs]
    res = pl.pallas_call(
        body, name="chip_swap_wait", out_shape=bufs + bufs,
        in_specs=[_HBM] * (2 * na) + [_SEM, _SEM, _ANY], out_specs=[_HBM] * (2 * na),
        input_output_aliases={i: i for i in range(2 * na)},
        compiler_params=pltpu.CompilerParams(has_side_effects=_EFFECT),
    )(*srcs, *lands, ssem, rsem, after)
    return list(res[na:])


def _pick_rows(r, c, budget=TILE_BYTES):
    if r * c * 4 <= budget or r % 16:
        return r
    best = 16
    for tr in range(16, r, 16):
        if r % tr == 0 and tr * c * 4 <= budget:
            best = tr
    return best


def _pair_sum(name, idx4, g8, recv4):
    _, r, c = g8.shape
    tr = _pick_rows(r, c, 2 * TILE_BYTES)

    def body(idx_ref, a_ref, b_ref, o0_ref, o3_ref):
        k = pl.program_id(1)
        s = a_ref[...].astype(F32) + b_ref[...].astype(F32)

        @pl.when(k == 0)
        def _():
            o0_ref[...] = s

        @pl.when(k > 0)
        def _():
            o3_ref[...] = s.astype(BF)

    spec = pltpu.PrefetchScalarGridSpec(
        num_scalar_prefetch=1, grid=(r // tr, 4),
        in_specs=[pl.BlockSpec((None, tr, c), lambda i, k, idx: (idx[k], i, 0)),
                  pl.BlockSpec((None, tr, c), lambda i, k, idx: (k, i, 0))],
        out_specs=[pl.BlockSpec((tr, c), lambda i, k, idx: (i, 0)),
                   pl.BlockSpec((None, tr, c), lambda i, k, idx: (jnp.maximum(k - 1, 0), i, 0))])
    return pl.pallas_call(
        body, name=name, grid_spec=spec,
        out_shape=[jax.ShapeDtypeStruct((r, c), F32), jax.ShapeDtypeStruct((3, r, c), BF)],
        compiler_params=_cparams(("arbitrary", "arbitrary")),
    )(idx4, g8, recv4)


def _adamw(w, g, m, v):
    m = ADAM_B1 * m + (1.0 - ADAM_B1) * g
    v = ADAM_B2 * v + (1.0 - ADAM_B2) * jnp.square(g)
    m_hat = m / (1.0 - ADAM_B1 ** ADAM_STEP)
    v_hat = v / (1.0 - ADAM_B2 ** ADAM_STEP)
    delta = -ADAM_LR * (m_hat / (jnp.sqrt(v_hat) + ADAM_EPS) + ADAM_WD * w)
    return delta, m, v


def _adam_sharded(name, idx1, own, recv, w, m, v, after=None):
    r, c = w.shape
    tr = _pick_rows(r, c, 2 * TILE_BYTES)
    nj = recv.shape[0]
    extra = [] if after is None else [after]

    def body(idx_ref, p_ref, r_ref, w_ref, m_ref, v_ref, *rest):
        g_out, d_out, m_out, v_out = rest[-4:]
        g = p_ref[...].astype(F32)
        for k in range(nj):
            g = g + r_ref[k].astype(F32)
        d, mn, vn = _adamw(w_ref[...], g, m_ref[...], v_ref[...])
        g_out[...] = g
        d_out[...] = d
        m_out[...] = mn
        v_out[...] = vn

    row = pl.BlockSpec((tr, c), lambda i, idx: (i, 0))
    spec = pltpu.PrefetchScalarGridSpec(
        num_scalar_prefetch=1, grid=(r // tr,),
        in_specs=[pl.BlockSpec((None, tr, c), lambda i, idx: (idx[0], i, 0)),
                  pl.BlockSpec((nj, tr, c), lambda i, idx: (0, i, 0)), row, row, row]
        + [pl.BlockSpec(e.shape, lambda i, idx: (0, 0)) for e in extra],
        out_specs=[row] * 4)
    return pl.pallas_call(
        body, name=name, grid_spec=spec, out_shape=[jax.ShapeDtypeStruct((r, c), F32)] * 4,
        compiler_params=_cparams(("arbitrary",)),
    )(idx1, own, recv, w, m, v, *extra)


def _repl_rows():
    rows, r = {}, 0
    for name, cols in REPL:
        rows[name] = r
        r += REPL_ROWS.get(name, 1) * ((cols + D - 1) // D)
    return rows


LOSS_ROW = 24


def _pack_replicated(grads, loss_acc, after):
    rows = _repl_rows()
    names = [n for n, _ in REPL]

    def body(*refs):
        o_ref = refs[-1]
        o_ref[...] = jnp.zeros(o_ref.shape, F32)
        o_ref[LOSS_ROW:LOSS_ROW + 1, 0:LANES] = refs[len(names)][...]
        for name, ref in zip(names, refs[:len(names)]):
            r0 = rows[name]
            nr, nc = ref.shape
            if nc <= D:
                o_ref[r0:r0 + nr, 0:nc] = ref[...]
            else:
                for j in range((nc + D - 1) // D):
                    lo, hi = j * D, min(nc, (j + 1) * D)
                    o_ref[r0 + j:r0 + j + 1, 0:hi - lo] = ref[:, lo:hi]

    return pl.pallas_call(body, name="pack_replicated", out_shape=jax.ShapeDtypeStruct((REPL_TOTAL, D), F32),
                          in_specs=[pl.BlockSpec(memory_space=pltpu.VMEM)] * (len(names) + 1) + [_ANY] * len(after),
                          compiler_params=_cparams())(*[grads[n] for n in names], loss_acc, *after)


def _adam_replicated(g8, ws, ms, vs):
    rows = _repl_rows()
    names = [n for n, _ in REPL]
    np_ = len(names)

    def body(*refs):
        g_ref = refs[0]
        w_refs, m_refs, v_refs = refs[1:1 + np_], refs[1 + np_:1 + 2 * np_], refs[1 + 2 * np_:1 + 3 * np_]
        outs = refs[1 + 3 * np_:1 + 7 * np_]
        scr = refs[-1]
        g = g_ref[0]
        for k in range(1, N_DEV):
            g = g + g_ref[k]
        scr[...] = g
        refs[1 + 7 * np_][...] = scr[LOSS_ROW:LOSS_ROW + 1, 0:LANES]
        for i, name in enumerate(names):
            r0 = rows[name]
            nr, nc = w_refs[i].shape
            if nc <= D:
                gi = scr[r0:r0 + nr, 0:nc]
            else:
                parts = []
                for j in range((nc + D - 1) // D):
                    lo, hi = j * D, min(nc, (j + 1) * D)
                    parts.append(scr[r0 + j:r0 + j + 1, 0:hi - lo])
                gi = jnp.concatenate(parts, axis=1)
            d, mn, vn = _adamw(w_refs[i][...], gi, m_refs[i][...], v_refs[i][...])
            outs[i][...] = gi
            outs[np_ + i][...] = d
            outs[2 * np_ + i][...] = mn
            outs[3 * np_ + i][...] = vn

    shp = [jax.ShapeDtypeStruct(w.shape, F32) for w in ws]
    res = pl.pallas_call(body, name="adam_replicated", out_shape=shp * 4 + [jax.ShapeDtypeStruct((1, LANES), F32)],
                         scratch_shapes=[pltpu.VMEM((REPL_TOTAL, D), F32)], compiler_params=_cparams(),
                         )(g8, *ws, *ms, *vs)
    return [dict(zip(names, res[k * np_:(k + 1) * np_])) for k in range(4)], res[-1]


_WEIGHTS = ("attn_pre_norm", "w_in", "hgrn_lb", "hgrn_gnorm", "w_branch_a", "rwkv_mu", "rwkv_w0", "rwkv_w2",
            "rwkv_a0", "rwkv_a2", "rwkv_g2", "rwkv_k_k", "rwkv_k_a", "rwkv_r_k", "rwkv_ln_w", "rwkv_ln_b",
            "w_branch_b", "w_out", "attn_post_norm", "ffn_pre_norm", "w_up", "conv_w", "conv_b", "w_down",
            "ffn_post_norm")
_BIG = ("w_in", "w_up", "w_down", "w_branch_a", "w_branch_b", "w_out")


def _stages():
    one = [D]
    hw = HG_K * HG_PER_STEP
    rw = LANES * RW_PAIRS_PER_STEP
    return dict(
        mixers=_Stage("mixers", _f_mixers, 1, 2 * RW_CHUNK, [False] * 13, [[D] * 7 + [LANES, LANES]], [0],
                      [(hw, HG_K), (1, RW_COLS), (rw, LANES)], [one, one], [BF, BF],
                      kept_shapes=[(2 * RW_KEPT * RW_PAIRS_PER_STEP * 2 * RW_CHUNK, LANES)], f_kept=_f_mixers_kept),
        conv=_Stage("conv", _f_conv, 1, 512, [False, False], [[DFF], [DFF]], [0, 0], [(1, 2 * DFF), (1, 2 * DFF)],
                    [[DFF]], [BF]),
    )


def _cols_to_blocks(w, per):
    return w.reshape(w.shape[0], N_DEV, per).transpose(1, 0, 2)


def _blocks_to_cols(g):
    return g.transpose(1, 0, 2).reshape(g.shape[1], N_DEV * g.shape[2])


def kernel(x, attn_pre_norm, w_in, hgrn_lb, hgrn_gnorm, w_branch_a, rwkv_mu, rwkv_w0, rwkv_w2, rwkv_a0, rwkv_a2, rwkv_g2, rwkv_k_k, rwkv_k_a, rwkv_r_k, rwkv_ln_w, rwkv_ln_b, w_branch_b, w_out, attn_post_norm, ffn_pre_norm, w_up, conv_w, conv_b, w_down, ffn_post_norm, loss_target, m_attn_pre_norm, m_w_in, m_hgrn_lb, m_hgrn_gnorm, m_w_branch_a, m_rwkv_mu, m_rwkv_w0, m_rwkv_w2, m_rwkv_a0, m_rwkv_a2, m_rwkv_g2, m_rwkv_k_k, m_rwkv_k_a, m_rwkv_r_k, m_rwkv_ln_w, m_rwkv_ln_b, m_w_branch_b, m_w_out, m_attn_post_norm, m_ffn_pre_norm, m_w_up, m_conv_w, m_conv_b, m_w_down, m_ffn_post_norm, v_attn_pre_norm, v_w_in, v_hgrn_lb, v_hgrn_gnorm, v_w_branch_a, v_rwkv_mu, v_rwkv_w0, v_rwkv_w2, v_rwkv_a0, v_rwkv_a2, v_rwkv_g2, v_rwkv_k_k, v_rwkv_k_a, v_rwkv_r_k, v_rwkv_ln_w, v_rwkv_ln_b, v_w_branch_b, v_w_out, v_attn_post_norm, v_ffn_pre_norm, v_w_up, v_conv_w, v_conv_b, v_w_down, v_ffn_post_norm):
    w = dict(attn_pre_norm=attn_pre_norm, w_in=w_in, hgrn_lb=hgrn_lb, hgrn_gnorm=hgrn_gnorm, w_branch_a=w_branch_a, rwkv_mu=rwkv_mu, rwkv_w0=rwkv_w0, rwkv_w2=rwkv_w2, rwkv_a0=rwkv_a0, rwkv_a2=rwkv_a2, rwkv_g2=rwkv_g2, rwkv_k_k=rwkv_k_k, rwkv_k_a=rwkv_k_a, rwkv_r_k=rwkv_r_k, rwkv_ln_w=rwkv_ln_w, rwkv_ln_b=rwkv_ln_b, w_branch_b=w_branch_b, w_out=w_out, attn_post_norm=attn_post_norm, ffn_pre_norm=ffn_pre_norm, w_up=w_up, conv_w=conv_w, conv_b=conv_b, w_down=w_down, ffn_post_norm=ffn_post_norm)
    mo = dict(attn_pre_norm=m_attn_pre_norm, w_in=m_w_in, hgrn_lb=m_hgrn_lb, hgrn_gnorm=m_hgrn_gnorm, w_branch_a=m_w_branch_a, rwkv_mu=m_rwkv_mu, rwkv_w0=m_rwkv_w0, rwkv_w2=m_rwkv_w2, rwkv_a0=m_rwkv_a0, rwkv_a2=m_rwkv_a2, rwkv_g2=m_rwkv_g2, rwkv_k_k=m_rwkv_k_k, rwkv_k_a=m_rwkv_k_a, rwkv_r_k=m_rwkv_r_k, rwkv_ln_w=m_rwkv_ln_w, rwkv_ln_b=m_rwkv_ln_b, w_branch_b=m_w_branch_b, w_out=m_w_out, attn_post_norm=m_attn_post_norm, ffn_pre_norm=m_ffn_pre_norm, w_up=m_w_up, conv_w=m_conv_w, conv_b=m_conv_b, w_down=m_w_down, ffn_post_norm=m_ffn_post_norm)
    vo = dict(attn_pre_norm=v_attn_pre_norm, w_in=v_w_in, hgrn_lb=v_hgrn_lb, hgrn_gnorm=v_hgrn_gnorm, w_branch_a=v_w_branch_a, rwkv_mu=v_rwkv_mu, rwkv_w0=v_rwkv_w0, rwkv_w2=v_rwkv_w2, rwkv_a0=v_rwkv_a0, rwkv_a2=v_rwkv_a2, rwkv_g2=v_rwkv_g2, rwkv_k_k=v_rwkv_k_k, rwkv_k_a=v_rwkv_k_a, rwkv_r_k=v_rwkv_r_k, rwkv_ln_w=v_rwkv_ln_w, rwkv_ln_b=v_rwkv_ln_b, w_branch_b=v_w_branch_b, w_out=v_w_out, attn_post_norm=v_attn_post_norm, ffn_pre_norm=v_ffn_pre_norm, w_up=v_w_up, conv_w=v_conv_w, conv_b=v_conv_b, w_down=v_w_down, ffn_post_norm=v_ffn_post_norm)

    t = x.shape[1]
    x2 = x.reshape(t, D)
    tgt = loss_target.reshape(t, D)
    st = _stages()

    me = 4 * lax.axis_index("x") + 2 * lax.axis_index("y") + lax.axis_index("c")
    small = jnp.concatenate([rwkv_w2[0], rwkv_a2[0], rwkv_g2[0]], axis=0).astype(BF)
    g_in, g_small = _all_gather("gather_weights", [w_in[0].T.astype(BF), small])
    fw_in_t = g_in.reshape(IN_COLS, D)
    z64 = jnp.zeros((64, D), BF)
    w2p = jnp.concatenate([_blocks_to_cols(g_small[:, 0:64]), z64], axis=0)
    a2p = jnp.concatenate([z64, _blocks_to_cols(g_small[:, 64:128])], axis=0)
    g2f = _blocks_to_cols(g_small[:, 128:256])
    conv_bits = jnp.pad(lax.bitcast_convert_type(conv_w[0], BF).reshape(3, 2 * 704), ((0, 29), (0, 0)))
    late = [w_up[0].T.astype(BF)] + [w[k][0].astype(BF) for k in _BIG[2:]] + [conv_bits]
    late_gather = _Exchange("gather2", late)
    r_k = rwkv_r_k.reshape(1, D)

    xn, z = _norm_in_proj(x2, attn_pre_norm, fw_in_t, 512, 4736)
    mix_par = [hgrn_lb, hgrn_gnorm, rwkv_mu, rwkv_w0, w2p, rwkv_a0, a2p, g2f, rwkv_k_k, rwkv_k_a,
               rwkv_ln_w, rwkv_ln_b, r_k]
    mix_in = [z]
    (o_a, o_b), mix_saved = _stage_fwd(st["mixers"], t, mix_par, mix_in, hook=late_gather)
    gl = [lax.dynamic_update_slice(g, own[None], (me, 0, 0)) for g, own in zip(late_gather.results, late)]
    fw_up_t = gl[0].reshape(2 * DFF, D)
    fw_down = gl[1].reshape(DFF, D)
    fw_a, fw_b, fw_out = (g.reshape(D, D) for g in gl[2:5])
    conv_full = _blocks_to_cols(lax.bitcast_convert_type(gl[5][:, :3].reshape(N_DEV, 3, 704, 2), F32))
    y_a, y_b, merged, mix, h1, xn2 = _merge_out_post(z, o_a, o_b, fw_a, fw_b, fw_out, x2, attn_post_norm,
                                                     ffn_pre_norm, 512)
    hu_g, hu_v, act, conv_g, conv_v = _up_conv(xn2, fw_up_t, conv_full, conv_b, 256)

    loss_acc, d_ffn_post, dh1, dff = _down_loss(act, fw_down, ffn_post_norm, h1, tgt, 512)
    dw_down = _mm("dw_down", act, dff, "tn", BF, tm=1408, tn=512)
    dcw_g, dcw_v, dcb_g, dcb_v, dhu_g, dhu_v = _conv_bwd(dff, fw_down, hu_g, hu_v, conv_g, conv_v, conv_full, 256)
    dcw, dcb = jnp.concatenate([dcw_g, dcw_v], axis=1), jnp.concatenate([dcb_g, dcb_v], axis=1)
    dhu = [dhu_g, dhu_v]
    dw_up_t = _mm_cols_tn("dw_up", dhu, xn2, BF, 1408)
    d_post, d_pre2, dx_a, dmix = _dxn2_post1_bwd(dhu, fw_up_t, x2, mix, dh1, attn_post_norm, ffn_pre_norm, 512)
    dga, dgb, dy_a, dy_b, do_a, do_b = _dmerged_merge_bwd(dmix, fw_out, fw_a, fw_b, z, y_a, y_b, 512)
    dw_a, dw_b, dw_out = _mm_multi("dw_branches", [(o_a, dy_a), (o_b, dy_b), (merged, dmix)], "tn", BF)
    early = [dw_up_t.reshape(N_DEV, 704, D), dw_down.reshape(N_DEV, 352, D), dw_a.reshape(N_DEV, 128, D),
             dw_b.reshape(N_DEV, 128, D), dw_out.reshape(N_DEV, 128, D), _cols_to_blocks(dcw.astype(BF), 704)]
    early_scatter = _Exchange("scatter", early)
    mix_dp, dz_hr = _stage_bwd(st["mixers"], t, mix_par, mix_in, mix_saved, [[do_a], [do_b]], [BF],
                               hook=early_scatter)
    d_lb, d_gn, d_mu, d_w0, d_w2p, d_a0, d_a2p, d_g2, d_kk, d_ka, d_lnw, d_lnb, d_rk = mix_dp
    dz = dz_hr + [dga, dgb]
    dw_in_t = _mm_cols_tn("dw_in", dz, xn, BF, 256)

    ax, ay, ac = lax.axis_index("x"), lax.axis_index("y"), lax.axis_index("c")
    idx4 = jnp.stack([4 * cx + 2 * cy + ac for cx, cy in ((ax, ay), (1 - ax, ay), (ax, 1 - ay), (1 - ax, 1 - ay))])
    idx4 = idx4.astype(jnp.int32)
    idx_me, idx_0 = idx4[0:1], jnp.zeros((1,), jnp.int32)
    d_small = jnp.concatenate([d_w2p[:64], d_a2p[64:], d_g2], axis=0).astype(BF)
    g8s = [dw_in_t.reshape(N_DEV, 1184, D), _cols_to_blocks(d_small, LANES)]
    recv4s = _reduce_pair(g8s)
    sums = [_pair_sum("pair_sum_" + n, idx4, g, r) for n, g, r in zip(("w_in", "small"), g8s, recv4s)]
    swap_ssem, swap_rsem, swap_srcs, swap_lands, token = _chip_swap_start([s[1] for s in sums])
    d_pre1, dx = _dxn_pre1_bwd(dz, fw_in_t, x2, dx_a, attn_pre_norm, 256, token)
    grad_x = dx.reshape(x.shape)

    sh_out = [dict() for _ in range(4)]
    done = []
    for n, own, recv in zip(_BIG[1:] + ("conv_w",), early, early_scatter.results):
        tr = (lambda a: a.T) if n == "w_up" else (lambda a: a)
        res = _adam_sharded("adam_" + n, idx_me, own, recv, *[tr(src[n][0]) for src in (w, mo, vo)], after=token)
        done.append(res[0])
        for kind in range(4):
            sh_out[kind][n] = tr(res[kind])[None]

    rg = dict(attn_pre_norm=d_pre1, hgrn_lb=d_lb, hgrn_gnorm=d_gn, rwkv_mu=d_mu, rwkv_w0=d_w0, rwkv_a0=d_a0,
              rwkv_k_k=d_kk, rwkv_k_a=d_ka, rwkv_r_k=d_rk, rwkv_ln_w=d_lnw, rwkv_ln_b=d_lnb, attn_post_norm=d_post,
              ffn_pre_norm=d_pre2, conv_b=dcb, ffn_post_norm=d_ffn_post)
    g8 = _all_gather_small("gather_small_grads", _pack_replicated(rg, loss_acc, done))
    rnames = [n for n, _ in REPL]
    flat = lambda src: [src[n].reshape(1, D) if n == "rwkv_r_k" else src[n] for n in rnames]
    rp_out, loss_row = _adam_replicated(g8, flat(w), flat(mo), flat(vo))
    loss = loss_row[0, 0]
    recv3s = _chip_swap_wait(swap_ssem, swap_rsem, swap_srcs, swap_lands, rp_out[0]["attn_pre_norm"])
    for kind in range(4):
        rp_out[kind]["rwkv_r_k"] = rp_out[kind]["rwkv_r_k"].reshape(rwkv_r_k.shape)

    def small_of(src):
        return jnp.concatenate([src["rwkv_w2"][0], src["rwkv_a2"][0], src["rwkv_g2"][0]], axis=0)

    res = _adam_sharded("adam_w_in", idx_0, sums[0][0][None], recv3s[0], *[src["w_in"][0].T for src in (w, mo, vo)])
    res_s = _adam_sharded("adam_small", idx_0, sums[1][0][None], recv3s[1], *[small_of(src) for src in (w, mo, vo)])
    for kind in range(4):
        sh_out[kind]["w_in"] = res[kind].T[None]
        sh_out[kind]["rwkv_w2"] = res_s[kind][0:64][None]
        sh_out[kind]["rwkv_a2"] = res_s[kind][64:128][None]
        sh_out[kind]["rwkv_g2"] = res_s[kind][128:256][None]

    outs = [loss, grad_x]
    for kind in range(4):
        for name in _WEIGHTS:
            outs.append(sh_out[kind][name] if name in sh_out[kind] else rp_out[kind][name])
    return tuple(outs)
```

```python
import functools

import jax
import jax.numpy as jnp
from jax import lax
from jax.experimental import pallas as pl
from jax.experimental.pallas import tpu as pltpu

F32 = jnp.float32
BF = jnp.bfloat16
MESH = pl.DeviceIdType.MESH

D = 1024
HG_HEADS = 8
HG_K = 128
HG_CHUNK = 32
HG_SCALE = HG_K ** -0.5
HG_PER_STEP = 8
RW_HEADS = 16
RW_N = 64
RW_CHUNK = 64
RW_PAIRS_PER_STEP = 8
DFF = 2816
IN_COLS = 9472
RW_COLS = 3328
EPS = 1e-6
GN_EPS = 1e-5 * RW_N
ADAM_LR = 0.001
ADAM_B1 = 0.9
ADAM_B2 = 0.999
ADAM_EPS = 1e-08
ADAM_WD = 0.01
ADAM_STEP = 10
N_DEV = 8
LANES = 128
SUBLANES = 8
VMEM_LIMIT = 56 * 1024 * 1024
TILE_BYTES = 1280 * 1024

REPL = (("attn_pre_norm", 1024), ("hgrn_lb", 1024), ("hgrn_gnorm", 1024), ("rwkv_mu", 3328), ("rwkv_w0", 1024),
        ("rwkv_a0", 1024), ("rwkv_k_k", 1024), ("rwkv_k_a", 1024), ("rwkv_r_k", 1024), ("rwkv_ln_w", 1024),
        ("rwkv_ln_b", 1024), ("attn_post_norm", 1024), ("ffn_pre_norm", 1024), ("conv_b", 5632), ("ffn_post_norm", 1024))
REPL_ROWS = {"hgrn_lb": 2}
REPL_TOTAL = 32


def _cparams(sem=None, **kw):
    return pltpu.CompilerParams(dimension_semantics=sem, vmem_limit_bytes=VMEM_LIMIT, **kw)


_DN = {"nn": ((1,), (0,)), "nt": ((1,), (1,)), "tn": ((0,), (0,))}


def _raw_dot(a, b, mode):
    return lax.dot_general(a.astype(BF), b.astype(BF), (_DN[mode], ((), ())), preferred_element_type=F32)


@functools.partial(jax.custom_vjp, nondiff_argnums=(2,))
def _dot(a, b, mode):
    return _raw_dot(a, b, mode)


def _dot_fwd(a, b, mode):
    return _raw_dot(a, b, mode), (a, b)


def _dot_bwd(mode, res, g):
    a, b = res
    if mode == "nn":
        return _dot(g, b, "nt"), _dot(a, g, "tn")
    if mode == "nt":
        return _dot(g, b, "nn"), _dot(g, a, "tn")
    return _dot(b, g, "nt"), _dot(a, g, "nn")


_dot.defvjp(_dot_fwd, _dot_bwd)


def _bf_pieces(x, n):
    out, r = [], x
    for i in range(n):
        p = r.astype(BF)
        out.append(p)
        if i + 1 < n:
            r = r - p.astype(F32)
    return out


def _raw_split_dot(x, e, mode, n, x_left):
    eb = e.astype(BF)
    acc = None
    for p in _bf_pieces(x, n):
        ops = (p, eb) if x_left else (eb, p)
        t = lax.dot_general(*ops, (_DN[mode], ((), ())), preferred_element_type=F32)
        acc = t if acc is None else acc + t
    return acc


def _raw_headsum(x):
    t = x.shape[0]
    i = lax.broadcasted_iota(jnp.int32, (LANES, LANES), 0)
    j = lax.broadcasted_iota(jnp.int32, (LANES, LANES), 1)
    same = jnp.where((i >= RW_N) == (j >= RW_N), 1.0, 0.0).astype(F32)
    groups = x.shape[1] // LANES
    rows = jnp.concatenate([x[:, q * LANES:(q + 1) * LANES] for q in range(groups)], axis=0)
    s = _raw_split_dot(rows, same, "nn", 2, True)
    return jnp.concatenate([s[q * t:(q + 1) * t] for q in range(groups)], axis=1)


@jax.custom_vjp
def _headsum(x):
    return _raw_headsum(x)


def _headsum_fwd(x):
    return _raw_headsum(x), None


def _headsum_bwd(_, g):
    return (_raw_headsum(g),)


_headsum.defvjp(_headsum_fwd, _headsum_bwd)


@functools.partial(jax.custom_vjp, nondiff_argnums=(2,))
def _tdot(tri, x, n):
    return _raw_split_dot(x, tri, "nn", n, False)


def _tdot_fwd(tri, x, n):
    return _raw_split_dot(x, tri, "nn", n, False), tri


def _tdot_bwd(n, tri, g):
    return jnp.zeros_like(tri), _raw_split_dot(g, tri, "tn", n, False)


_tdot.defvjp(_tdot_fwd, _tdot_bwd)


def _row(x, i):
    r = lax.broadcasted_iota(jnp.int32, x.shape, 0)
    return jnp.sum(jnp.where(r == i, x, 0.0), axis=0, keepdims=True)


def _shift_down(x, prev):
    t = x.shape[0]

    @jax.custom_vjp
    def sh(x, prev):
        r = lax.broadcasted_iota(jnp.int32, x.shape, 0)
        return jnp.where(r == 0, prev, pltpu.roll(x, 1, 0))

    def fwd(x, prev):
        return sh(x, prev), None

    def bwd(_, g):
        r = lax.broadcasted_iota(jnp.int32, g.shape, 0)
        dx = jnp.where(r == t - 1, 0.0, pltpu.roll(g, t - 1, 0))
        return dx, jnp.sum(jnp.where(r == 0, g, 0.0), axis=0, keepdims=True)

    sh.defvjp(fwd, bwd)
    return sh(x, prev)


def _sigmoid(x):
    return jax.nn.sigmoid(x)


@jax.custom_vjp
def _silu(x):
    return x * jax.nn.sigmoid(x)


def _silu_fwd(x):
    s = jax.nn.sigmoid(x)
    return x * s, (x, s)


def _silu_bwd(res, g):
    x, s = res
    return (g * (s * (1.0 + x * (1.0 - s))),)


_silu.defvjp(_silu_fwd, _silu_bwd)


def _softplus(x):
    return jnp.maximum(x, 0.0) + jnp.log(1.0 + jnp.exp(-jnp.abs(x)))


def _rms(x, g):
    return (x * lax.rsqrt(jnp.mean(x * x, axis=-1, keepdims=True) + EPS)) * g


def _tril(c):
    r = lax.broadcasted_iota(jnp.int32, (c, c), 0)
    cc = lax.broadcasted_iota(jnp.int32, (c, c), 1)
    return cc <= r


def _f_pre1_residual(ps, xs, cs):
    return [_rms(xs[0], ps[0]), xs[0]], []


def _f_hgrn(ps, xs, cs):
    lbraw, gn = ps
    hq, hf, hi, hg = xs
    hd = range(HG_PER_STEP)
    st = [cs[0][p * HG_K:(p + 1) * HG_K] for p in hd]
    l0, l1 = _row(lbraw, 0), _row(lbraw, 1)
    m = jnp.maximum(l0, l1)
    e0, e1 = jnp.exp(l0 - m), jnp.exp(l1 - m)
    lb = e0 / (e0 + e1)
    q = _silu(hq) * HG_SCALE
    f = lb + (1.0 - lb) * _sigmoid(hf)
    kh = 1.0 - f
    gl = jnp.log(f)
    c = HG_CHUNK
    low = _tril(c)
    tri = jnp.where(low, 1.0, 0.0).astype(F32)
    outs = []
    for i in range(hq.shape[0] // c):
        rows = slice(i * c, (i + 1) * c)
        b = _tdot(tri, gl[rows], 3)
        bref = _row(b, c // 2 - 1)
        blast = _row(b, c - 1)
        qi = q[rows] * jnp.exp(b - bref)
        ki = kh[rows] * jnp.exp(bref - b)
        qd = q[rows] * jnp.exp(b)
        kd = kh[rows] * jnp.exp(blast - b)
        dec = jnp.exp(blast)
        sl = [slice(p * HG_K, (p + 1) * HG_K) for p in hd]
        sc = [jnp.where(low, _dot(qi[:, sl[p]], ki[:, sl[p]], "nt"), 0.0) for p in hd]
        o = [_dot(sc[p], hi[rows, sl[p]], "nn") + _dot(qd[:, sl[p]], st[p], "nt") for p in hd]
        u = [_dot(hi[rows, sl[p]], kd[:, sl[p]], "tn") for p in hd]
        st = [dec[:, sl[p]] * st[p] + u[p] for p in hd]
        outs.append(jnp.concatenate(o, axis=1) if len(o) > 1 else o[0])
    o = outs[0] if len(outs) == 1 else jnp.concatenate(outs, axis=0)
    on = []
    for p in hd:
        op = o[:, p * HG_K:(p + 1) * HG_K]
        on.append(op * lax.rsqrt(jnp.mean(op * op, axis=-1, keepdims=True) + EPS))
    o = jnp.concatenate(on, axis=1) if len(on) > 1 else on[0]
    o = o * gn
    return [o * _silu(hg)], [jnp.concatenate(st, axis=0) if len(st) > 1 else st[0]]


_RW_OFFS = (0, 1024, 2048, 3072, 3200, 3328)


def _f_rwpre(ps, xs, cs):
    mu, w0, w2p, a0, a2p, g2, k_k, k_a = ps
    (prev,) = cs
    t = xs[0].shape[0]
    zs = []
    for i, z in enumerate(xs):
        lo, hi = _RW_OFFS[i], _RW_OFFS[i + 1]
        zs.append(z + mu[:, lo:hi] * (_shift_down(z, prev[:, lo:hi]) - z))
    rr, kr, vr, wa, gz = zs
    w_log = -_softplus(-(w0 + _dot(jnp.tanh(wa), w2p, "nn"))) - 0.5
    lw = -jnp.exp(w_log)
    a = _sigmoid(a0 + _dot(wa, a2p, "nn"))
    g = _dot(_sigmoid(gz), g2, "nn")
    kkr = kr * k_k
    kk = kkr / jnp.maximum(jnp.sqrt(_headsum(kkr * kkr)), 1e-12)
    k2 = kr * (1.0 + (a - 1.0) * k_a)
    newprev = jnp.concatenate([_row(z, t - 1) for z in xs], axis=1)
    return [rr, lw, k2, vr, -kk, kk * a, g], [newprev]


def _raw_inverses(ls):
    n = ls[0].shape[0]
    r = lax.broadcasted_iota(jnp.int32, (n, n), 0)
    c = lax.broadcasted_iota(jnp.int32, (n, n), 1)
    eye = jnp.where(r == c, 1.0, 0.0).astype(F32)
    tinv = [eye + l for l in ls]
    pw = ls
    for _ in range(5):
        pw = [_raw_dot(p, p, "nn") for p in pw]
        tinv = [t + _raw_dot(t, p, "nn") for t, p in zip(tinv, pw)]
    return tinv


@jax.custom_vjp
def _unit_lower_inverses(ls):
    return _raw_inverses(ls)


def _inverses_fwd(ls):
    tinv = _raw_inverses(ls)
    return tinv, tinv


def _inverses_bwd(tinv, gs):
    return ([_raw_dot(_raw_dot(t, g, "tn"), t, "nt") for t, g in zip(tinv, gs)],)


_unit_lower_inverses.defvjp(_inverses_fwd, _inverses_bwd)


@jax.custom_vjp
def _known_inverses(ls, tinv):
    return tinv


def _known_fwd(ls, tinv):
    return tinv, tinv


def _known_bwd(tinv, gs):
    return [_raw_dot(_raw_dot(t, g, "tn"), t, "nt") for t, g in zip(tinv, gs)], [jnp.zeros_like(t) for t in tinv]


_known_inverses.defvjp(_known_fwd, _known_bwd)


@jax.custom_vjp
def _use_kept(computed, kept):
    return kept


def _use_kept_fwd(computed, kept):
    return kept, None


def _use_kept_bwd(_, g):
    return g, jax.tree.map(jnp.zeros_like, g)


_use_kept.defvjp(_use_kept_fwd, _use_kept_bwd)

RW_KEPT = 5


def _f_rwscan(ps, xs, cs, kept=None):
    state = cs[0]
    ys, keep = [], []
    n = 2 * RW_CHUNK
    per_chunk = RW_KEPT * RW_PAIRS_PER_STEP * n
    for i in range(xs[0].shape[0] // RW_CHUNK):
        known = None
        if kept is not None:
            known = [[kept[i * per_chunk + (q * RW_PAIRS_PER_STEP + p) * n:
                           i * per_chunk + (q * RW_PAIRS_PER_STEP + p + 1) * n] for p in range(RW_PAIRS_PER_STEP)]
                     for q in range(RW_KEPT)]
        y, state, mats = _rwkv_chunk([x[i * RW_CHUNK:(i + 1) * RW_CHUNK] for x in xs], state, known)
        ys.append(y)
        keep += [m for group in mats for m in group]
    return [ys[0] if len(ys) == 1 else jnp.concatenate(ys, axis=0)], [state], jnp.concatenate(keep, axis=0)


def _rwkv_chunk(xs, state, known=None):
    npair = RW_PAIRS_PER_STEP
    pr = range(npair)
    r, lw, k, v, av, bv = [[x[:, p * LANES:(p + 1) * LANES] for p in pr] for x in xs]
    sv = [state[p * LANES:(p + 1) * LANES] for p in pr]
    c = RW_CHUNK
    n = 2 * c
    tri = jnp.where(_tril(c), 1.0, 0.0).astype(F32)
    cl = [_tdot(tri, lw[p], 3) for p in pr]
    cl_last = [_row(cl[p], c - 1) for p in pr]
    lane = lax.broadcasted_iota(jnp.int32, (c, LANES), 1)
    h0 = lane < RW_N

    def stack(x):
        return jnp.concatenate([jnp.where(h0, x, 0.0), jnp.where(h0, 0.0, x)], axis=0)

    am = [stack(av[p] * jnp.exp(cl[p] - lw[p])) for p in pr]
    bm = [stack(bv[p] * jnp.exp(-cl[p])) for p in pr]
    km = [stack(k[p] * jnp.exp(-cl[p])) for p in pr]
    rm = [stack(r[p] * jnp.exp(cl[p])) for p in pr]
    vm = [stack(v[p]) for p in pr]
    rn = lax.broadcasted_iota(jnp.int32, (n, n), 0)
    cn = lax.broadcasted_iota(jnp.int32, (n, n), 1)
    blk = (rn >= c) == (cn >= c)
    strict = blk & (cn < rn)
    incl = blk & (cn <= rn)
    lab = [jnp.where(strict, _dot(am[p], bm[p], "nt"), 0.0) for p in pr]
    lak = [jnp.where(strict, _dot(am[p], km[p], "nt"), 0.0) for p in pr]
    wrb = [jnp.where(incl, _dot(rm[p], bm[p], "nt"), 0.0) for p in pr]
    wrk = [jnp.where(incl, _dot(rm[p], km[p], "nt"), 0.0) for p in pr]
    if known is None:
        tinv = _unit_lower_inverses(lab)
    else:
        tinv = _known_inverses(lab, known[0])
        lak, wrb, wrk = _use_kept(lak, known[1]), _use_kept(wrb, known[2]), _use_kept(wrk, known[3])
    rhs = [_dot(am[p], sv[p], "nt") + _dot(lak[p], vm[p], "nn") for p in pr]
    um = [_dot(tinv[p], rhs[p], "nn") for p in pr]
    if known is not None:
        um = _use_kept(um, known[4])
    ym = [_dot(rm[p], sv[p], "nt") + _dot(wrb[p], um[p], "nn") + _dot(wrk[p], vm[p], "nn") for p in pr]
    sn = [(sv[p] + _dot(um[p], bm[p], "tn") + _dot(vm[p], km[p], "tn")) * jnp.exp(cl_last[p]) for p in pr]
    ys = [ym[p][:c] + ym[p][c:] for p in pr]
    return jnp.concatenate(ys, axis=1), jnp.concatenate(sn, axis=0), [tinv, lak, wrb, wrk, um]


def _f_mixers(ps, xs, cs):
    return _mixers(ps, xs, cs, None)


def _f_mixers_kept(ps, xs, cs, kept):
    return _mixers(ps, xs, cs, kept[0])[:2]


def _mixers(ps, xs, cs, kept):
    oa, st = _f_hgrn(ps[:2], xs[:4], cs[:1])
    (r, lw, k, v, av, bv, g), prev = _f_rwpre(ps[2:10], xs[4:], cs[1:2])
    y, sv, keep = _f_rwscan([], [r, lw, k, v, av, bv], cs[2:], kept)
    ob, _ = _f_rwpost(ps[10:], y + [r, k, v, g], [])
    return oa + ob, st + prev + sv, [keep]


def _f_rwpost(ps, xs, cs):
    ln_w, ln_b, r_k = ps
    y, r, k, v, g = xs
    inv_n = 1.0 / RW_N
    yc = y - _headsum(y) * inv_n
    var = _headsum(yc * yc) * inv_n
    yn = yc * lax.rsqrt(var + GN_EPS)
    yn = yn * ln_w + ln_b
    bonus = _headsum(r * k * r_k) * v
    return [(yn + bonus) * g], []


def _f_merge(ps, xs, cs):
    ga, gb, ya, yb = xs
    return [_sigmoid(ga) * ya + _sigmoid(gb) * yb], []


def _f_post1(ps, xs, cs):
    x, mix = xs
    h1 = x + _rms(mix, ps[0])
    return [h1, _rms(h1, ps[1])], []


def _f_conv(ps, xs, cs):
    cw, cb = ps
    p1, p2 = cs
    w0, w1, w2 = _row(cw, 0), _row(cw, 1), _row(cw, 2)
    t = xs[0].shape[0]
    hc = []
    for i, x in enumerate(xs):
        sl = slice(i * DFF, (i + 1) * DFF)
        s1 = _shift_down(x, p1[:, sl])
        s2 = _shift_down(s1, p2[:, sl])
        hc.append(cb[:, sl] + w0[:, sl] * s2 + w1[:, sl] * s1 + w2[:, sl] * x)
    n1 = jnp.concatenate([_row(x, t - 1) for x in xs], axis=1)
    n2 = jnp.concatenate([_row(x, t - 2) for x in xs], axis=1)
    return [_silu(hc[0]) * hc[1]], [n1, n2]


class _Stage:
    def __init__(self, name, f, g, tm, par_per_g, in_pieces, in_offs, carry_shapes, out_pieces, out_dtypes,
                 kept_shapes=(), f_kept=None):
        self.name, self.f, self.g, self.tm = name, f, g, tm
        self.par_per_g, self.in_pieces, self.in_offs = par_per_g, in_pieces, in_offs
        self.carry_shapes, self.out_pieces, self.out_dtypes = carry_shapes, out_pieces, out_dtypes
        self.kept_shapes, self.f_kept = list(kept_shapes), f_kept


def _par_spec(arr, per_g, g):
    r, c = arr.shape
    if per_g:
        return pl.BlockSpec((r, c // g), lambda gi, ni: (0, gi))
    return pl.BlockSpec((r, c), lambda gi, ni: (0, 0))


def _row_spec(tm, width, off, n, rev):
    if rev:
        return pl.BlockSpec((tm, width), lambda gi, ni: (n - 1 - ni, off + gi))
    return pl.BlockSpec((tm, width), lambda gi, ni: (ni, off + gi))


def _carry_spec(shape, n, rev):
    if rev:
        return pl.BlockSpec((None, None) + shape, lambda gi, ni: (gi, n - 1 - ni, 0, 0))
    return pl.BlockSpec((None, None) + shape, lambda gi, ni: (gi, ni, 0, 0))


def _load_pieces(refs, pieces_list):
    out = []
    for ref, pieces in zip(refs, pieces_list):
        o = 0
        for w in pieces:
            out.append(ref[:, o:o + w].astype(F32))
            o += w
    return out


def _store_pieces(refs, pieces_list, vals):
    k = 0
    for ref, pieces in zip(refs, pieces_list):
        o = 0
        for w in pieces:
            ref[:, o:o + w] = vals[k].astype(ref.dtype)
            k += 1
            o += w


_ANY = pl.BlockSpec(memory_space=pl.ANY)


class _Exchange:
    def __init__(self, kind, arrs):
        self.kind, self.arrs, self.results = kind, list(arrs), None
        if kind == "scatter":
            self.out_shape = [jax.ShapeDtypeStruct((N_DEV - 1,) + a.shape[1:], a.dtype) for a in self.arrs]
        else:
            self.out_shape = [jax.ShapeDtypeStruct((N_DEV,) + a.shape, a.dtype) for a in self.arrs]
        self.nsem = (N_DEV if kind == "gather2" else N_DEV - 1) * len(self.arrs)

    def copies(self, in_refs, out_refs, ssem, rsem):
        x, y, c = lax.axis_index("x"), lax.axis_index("y"), lax.axis_index("c")
        me = 4 * x + 2 * y + c
        cps = []
        for a, (i_ref, o_ref) in enumerate(zip(in_refs, out_refs)):
            for j in range(1, N_DEV):
                px = 1 - x if j & 4 else x
                py = 1 - y if j & 2 else y
                pc = 1 - c if j & 1 else c
                if self.kind == "gather":
                    src, dst = i_ref, o_ref.at[me]
                else:
                    src, dst = i_ref.at[4 * px + 2 * py + pc], o_ref.at[j - 1]
                s = (N_DEV - 1) * a + j - 1
                cps.append(pltpu.make_async_remote_copy(src_ref=src, dst_ref=dst, send_sem=ssem.at[s],
                                                        recv_sem=rsem.at[s], device_id=(px, py, pc),
                                                        device_id_type=MESH))
        return cps

    def run(self, step, total, in_refs, out_refs, ssem, rsem):
        if self.kind == "gather2":
            return self.run_two_level(step, total, in_refs, out_refs, ssem, rsem)

        @pl.when(step == 0)
        def _():
            for cp in self.copies(in_refs, out_refs, ssem, rsem):
                cp.start()

        @pl.when(step == total - 1)
        def _():
            for cp in self.copies(in_refs, out_refs, ssem, rsem):
                cp.wait()

    def run_two_level(self, step, total, in_refs, out_refs, ssem, rsem):
        x, y, c = lax.axis_index("x"), lax.axis_index("y"), lax.axis_index("c")
        sibling, xn, yn = (x, y, 1 - c), (1 - x, y, c), (x, 1 - y, c)
        arrs = range(len(in_refs))
        ns = N_DEV

        def num(px, py, pc):
            return 4 * px + 2 * py + pc

        def copy(a, k, to, src, dst):
            return pltpu.make_async_remote_copy(src_ref=src, dst_ref=dst, send_sem=ssem.at[ns * a + k],
                                                recv_sem=rsem.at[ns * a + k], device_id=to, device_id_type=MESH)

        def blk(a, b):
            return out_refs[a].at[b]

        def half(a, b, second):
            h = self.arrs[a].shape[0] // 2
            return out_refs[a].at[b, pl.ds(h if second else 0, h)]

        bx, by, bd = num(1 - x, y, c), num(x, 1 - y, c), num(1 - x, 1 - y, c)

        def firsts(a):
            own = blk(a, num(x, y, c))
            return [copy(a, 0, sibling, in_refs[a], own), copy(a, 1, xn, in_refs[a], own),
                    copy(a, 2, yn, in_refs[a], own)]

        def seconds(a):
            return [copy(a, 3, yn, half(a, bx, False), half(a, bx, False)), copy(a, 5, sibling, blk(a, bx), blk(a, bx)),
                    copy(a, 4, xn, half(a, by, True), half(a, by, True)), copy(a, 6, sibling, blk(a, by), blk(a, by))]

        def third(a):
            return copy(a, 7, sibling, blk(a, bd), blk(a, bd))

        @pl.when(step == 0)
        def _():
            for a in arrs:
                for cp in firsts(a):
                    cp.start()

        @pl.when(step == total // 2)
        def _():
            for a in arrs:
                copy(a, 1, xn, blk(a, bx), blk(a, bx)).wait_recv()
                copy(a, 2, yn, blk(a, by), blk(a, by)).wait_recv()
                for cp in seconds(a):
                    cp.start()

        @pl.when(step == (4 * total) // 5)
        def _():
            for a in arrs:
                copy(a, 3, yn, half(a, bd, False), half(a, bd, False)).wait_recv()
                copy(a, 4, xn, half(a, bd, True), half(a, bd, True)).wait_recv()
                third(a).start()

        @pl.when(step == total - 1)
        def _():
            for a in arrs:
                for k, b in ((0, num(x, y, 1 - c)), (5, num(1 - x, y, 1 - c)), (6, num(x, 1 - y, 1 - c)),
                             (7, num(1 - x, 1 - y, 1 - c))):
                    copy(a, k, sibling, blk(a, b), blk(a, b)).wait_recv()
                for cp in firsts(a) + seconds(a) + [third(a)]:
                    cp.wait_send()


def _hook_specs(hook):
    if hook is None:
        return [], [], [], []
    na = len(hook.arrs)
    sems = [pltpu.SemaphoreType.DMA((hook.nsem,)), pltpu.SemaphoreType.DMA((hook.nsem,))]
    return [_ANY] * na, [_ANY] * na, hook.out_shape, sems


def _stage_fwd(st, t, params, inputs, hook=None):
    g, tm = st.g, min(st.tm, t)
    n = t // tm
    npar, nin, ncar, nout = len(params), len(inputs), len(st.carry_shapes), len(st.out_pieces)
    nk = len(st.kept_shapes)
    h_in, h_out, h_shape, h_sems = _hook_specs(hook)
    nh = len(h_in)

    def body(*refs):
        p_refs = refs[:npar]
        x_refs = refs[npar:npar + nin]
        hi_refs = refs[npar + nin:npar + nin + nh]
        o = npar + nin + nh
        o_refs = refs[o:o + nout]
        s_refs = refs[o + nout:o + nout + ncar]
        k_refs = refs[o + nout + ncar:o + nout + ncar + nk]
        o += nout + ncar + nk
        ho_refs = refs[o:o + nh]
        c_scr = refs[o + nh:o + nh + ncar]
        gi, ni = pl.program_id(0), pl.program_id(1)
        if hook is not None:
            step = gi * n + ni
            hook.run(step, g * n, hi_refs, ho_refs, *refs[-2:])

        @pl.when(ni == 0)
        def _():
            for c in c_scr:
                c[...] = jnp.zeros(c.shape, F32)

        ps = [r[...].astype(F32) for r in p_refs]
        xs = _load_pieces(x_refs, st.in_pieces)
        cs = [c[...] for c in c_scr]
        for s, c in zip(s_refs, cs):
            s[...] = c
        res = st.f(ps, xs, cs)
        outs, ncs = res[0], res[1]
        _store_pieces(o_refs, st.out_pieces, outs)
        for c, v in zip(c_scr, ncs):
            c[...] = v
        for kr, kv in zip(k_refs, res[2] if nk else []):
            kr[...] = kv.astype(kr.dtype)

    in_specs = [_par_spec(p, pg, g) for p, pg in zip(params, st.par_per_g)]
    in_specs += [_row_spec(tm, sum(pc), off, n, False) for pc, off in zip(st.in_pieces, st.in_offs)]
    out_specs = [_row_spec(tm, sum(pc), 0, n, False) for pc in st.out_pieces]
    out_specs += [_carry_spec(s, n, False) for s in st.carry_shapes]
    out_specs += [pl.BlockSpec(s, lambda gi, ni: (ni, 0)) for s in st.kept_shapes]
    out_shape = [jax.ShapeDtypeStruct((t, g * sum(pc)), dt) for pc, dt in zip(st.out_pieces, st.out_dtypes)]
    out_shape += [jax.ShapeDtypeStruct((g, n) + s, F32) for s in st.carry_shapes]
    out_shape += [jax.ShapeDtypeStruct((n * s[0], s[1]), BF) for s in st.kept_shapes]
    res = pl.pallas_call(
        body, name=st.name + "_fwd", grid=(g, n), in_specs=in_specs + h_in, out_specs=out_specs + h_out,
        out_shape=out_shape + h_shape,
        scratch_shapes=[pltpu.VMEM(s, F32) for s in st.carry_shapes] + h_sems,
        compiler_params=_cparams(("arbitrary", "arbitrary")),
    )(*params, *inputs, *(hook.arrs if hook else []))
    if hook is not None:
        hook.results = list(res[nout + ncar + nk:])
    return list(res[:nout]), list(res[nout:nout + ncar + nk])


def _stage_bwd(st, t, params, inputs, saved, douts, dx_dtypes, hook=None, dout_dot=None):
    g, tm = st.g, min(st.tm, t)
    n = t // tm
    npar, nin, ncar = len(params), len(inputs), len(st.carry_shapes)
    nk = len(st.kept_shapes)
    flat_d = list(dout_dot) if dout_dot is not None else [d for ds in douts for d in ds]
    nd = len(flat_d)
    dx_idx = [i for i, dt in enumerate(dx_dtypes) if dt is not None]
    h_in, h_out, h_shape, h_sems = _hook_specs(hook)
    nh = len(h_in)

    def body(*refs):
        p_refs = refs[:npar]
        x_refs = refs[npar:npar + nin]
        s_refs = refs[npar + nin:npar + nin + ncar]
        k_refs = refs[npar + nin + ncar:npar + nin + ncar + nk]
        o = npar + nin + ncar + nk
        d_refs = refs[o:o + nd]
        hi_refs = refs[o + nd:o + nd + nh]
        o += nd + nh
        dp_refs = refs[o:o + npar]
        dx_refs = refs[o + npar:o + npar + len(dx_idx)]
        ho_refs = refs[o + npar + len(dx_idx):o + npar + len(dx_idx) + nh]
        dc_scr = refs[o + npar + len(dx_idx) + nh:o + npar + len(dx_idx) + nh + ncar]
        gi, ni = pl.program_id(0), pl.program_id(1)
        if hook is not None:
            step = gi * n + ni
            hook.run(step, g * n, hi_refs, ho_refs, *refs[-2:])

        @pl.when(ni == 0)
        def _():
            for c in dc_scr:
                c[...] = jnp.zeros(c.shape, F32)

        ps = [r[...].astype(F32) for r in p_refs]
        xs = _load_pieces(x_refs, st.in_pieces)
        cs = [s[...] for s in s_refs]
        dys = [_raw_dot(d_refs[0][...], d_refs[1][...], "nt")] if dout_dot is not None else []
        k = 0
        for ds, pieces in zip(douts, st.out_pieces):
            acc = _load_pieces([d_refs[k]], [pieces])
            for j in range(1, len(ds)):
                more = _load_pieces([d_refs[k + j]], [pieces])
                acc = [a + b for a, b in zip(acc, more)]
            dys += acc
            k += len(ds)
        if nk:
            kept = [r[...].astype(F32) for r in k_refs]
            _, vjp = jax.vjp(lambda p, x, c: st.f_kept(p, x, c, kept), ps, xs, cs)
        else:
            _, vjp = jax.vjp(st.f, ps, xs, cs)
        dps, dxs, dcs = vjp((dys, [c[...] for c in dc_scr]))
        k = 0
        per_in = []
        for pieces in st.in_pieces:
            per_in.append(dxs[k:k + len(pieces)])
            k += len(pieces)
        for ref, i in zip(dx_refs, dx_idx):
            _store_pieces([ref], [st.in_pieces[i]], per_in[i])
        for c, v in zip(dc_scr, dcs):
            c[...] = v
        for ref, dp, pg in zip(dp_refs, dps, st.par_per_g):
            first = (ni == 0) if pg else ((ni == 0) & (gi == 0))

            @pl.when(first)
            def _():
                ref[...] = jnp.zeros(ref.shape, F32)

            ref[...] += dp

    in_specs = [_par_spec(p, pg, g) for p, pg in zip(params, st.par_per_g)]
    in_specs += [_row_spec(tm, sum(pc), off, n, True) for pc, off in zip(st.in_pieces, st.in_offs)]
    in_specs += [_carry_spec(s, n, True) for s in st.carry_shapes]
    in_specs += [pl.BlockSpec(s, lambda gi, ni: (n - 1 - ni, 0)) for s in st.kept_shapes]
    for ds, pc in zip(douts, st.out_pieces):
        in_specs += [_row_spec(tm, sum(pc), 0, n, True) for _ in ds]
    if dout_dot is not None:
        a, w = dout_dot
        in_specs += [pl.BlockSpec((tm, a.shape[1]), lambda gi, ni: (n - 1 - ni, 0)),
                     pl.BlockSpec(w.shape, lambda gi, ni: (0, 0), pipeline_mode=pl.Buffered(1))]
    out_specs = [_par_spec(p, pg, g) for p, pg in zip(params, st.par_per_g)]
    out_specs += [_row_spec(tm, sum(st.in_pieces[i]), 0, n, True) for i in dx_idx]
    out_shape = [jax.ShapeDtypeStruct(p.shape, F32) for p in params]
    out_shape += [jax.ShapeDtypeStruct((t, g * sum(st.in_pieces[i])), dx_dtypes[i]) for i in dx_idx]
    res = pl.pallas_call(
        body, name=st.name + "_bwd", grid=(g, n), in_specs=in_specs + h_in, out_specs=out_specs + h_out,
        out_shape=out_shape + h_shape,
        scratch_shapes=[pltpu.VMEM(s, F32) for s in st.carry_shapes] + h_sems,
        compiler_params=_cparams(("arbitrary", "arbitrary")),
    )(*params, *inputs, *saved, *flat_d, *(hook.arrs if hook else []))
    if hook is not None:
        hook.results = list(res[npar + len(dx_idx):])
    return list(res[:npar]), list(res[npar:npar + len(dx_idx)])


def _pick(n, cap):
    if n <= cap:
        return n
    best = LANES
    for k in range(1, n // LANES + 1):
        if (n // LANES) % k == 0 and k * LANES <= cap:
            best = k * LANES
    return best


def _mm(name, a, b, mode, out_dtype=F32, tm=1024, tn=512, b_outer=False):
    m = a.shape[1] if mode == "tn" else a.shape[0]
    k = a.shape[0] if mode == "tn" else a.shape[1]
    n = b.shape[0] if mode == "nt" else b.shape[1]
    tm, tn = _pick(m, tm), _pick(n, tn)
    if b_outer:
        grid = (n // tn, m // tm)
        ij = lambda p, q: (q, p)
    else:
        grid = (m // tm, n // tn)
        ij = lambda p, q: (p, q)

    def body(a_ref, b_ref, o_ref):
        o_ref[...] = _raw_dot(a_ref[...], b_ref[...], mode).astype(o_ref.dtype)

    if mode == "tn":
        a_spec = pl.BlockSpec((k, tm), lambda p, q: (0, ij(p, q)[0]))
    else:
        a_spec = pl.BlockSpec((tm, k), lambda p, q: (ij(p, q)[0], 0))
    b_mode = dict(pipeline_mode=pl.Buffered(1)) if tn == n else {}
    if mode == "nt":
        b_spec = pl.BlockSpec((tn, k), lambda p, q: (ij(p, q)[1], 0), **b_mode)
    else:
        b_spec = pl.BlockSpec((k, tn), lambda p, q: (0, ij(p, q)[1]), **b_mode)
    return pl.pallas_call(
        body, name=name, grid=grid, in_specs=[a_spec, b_spec],
        out_specs=pl.BlockSpec((tm, tn), lambda p, q: ij(p, q)),
        out_shape=jax.ShapeDtypeStruct((m, n), out_dtype),
        compiler_params=_cparams(("arbitrary", "arbitrary")),
    )(a, b)


def _mm_multi(name, pairs, mode, out_dtype, tm=1024, tn=512):
    a0, b0 = pairs[0]
    m = a0.shape[1] if mode == "tn" else a0.shape[0]
    k = a0.shape[0] if mode == "tn" else a0.shape[1]
    n = b0.shape[0] if mode == "nt" else b0.shape[1]
    tm, tn = _pick(m, tm), _pick(n, tn)
    npair = len(pairs)

    def body(*refs):
        for p in range(npair):
            refs[2 * npair + p][...] = _raw_dot(refs[2 * p][...], refs[2 * p + 1][...], mode).astype(out_dtype)

    a_spec = pl.BlockSpec((k, tm), lambda i, j: (0, i)) if mode == "tn" else pl.BlockSpec((tm, k), lambda i, j: (i, 0))
    b_spec = pl.BlockSpec((tn, k), lambda i, j: (j, 0)) if mode == "nt" else pl.BlockSpec((k, tn), lambda i, j: (0, j))
    return pl.pallas_call(
        body, name=name, grid=(m // tm, n // tn), in_specs=[a_spec, b_spec] * npair,
        out_specs=[pl.BlockSpec((tm, tn), lambda i, j: (i, j))] * npair,
        out_shape=[jax.ShapeDtypeStruct((m, n), out_dtype)] * npair,
        compiler_params=_cparams(("arbitrary", "arbitrary")),
    )(*[x for pair in pairs for x in pair])


def _mm_cols_tn(name, pieces, b, out_dtype, tm):
    k, n = b.shape
    counts = [p.shape[1] // tm for p in pieces]
    starts = [sum(counts[:i]) for i in range(len(pieces))]
    na = len(pieces)

    def body(*refs):
        b_ref, o_ref = refs[na], refs[-1]
        i = pl.program_id(0)
        for a_ref, s, c in zip(refs[:na], starts, counts):
            @pl.when((i >= s) & (i < s + c))
            def _():
                o_ref[...] = _raw_dot(a_ref[...], b_ref[...], "tn").astype(o_ref.dtype)

    def spec(s, c):
        return pl.BlockSpec((k, tm), lambda i: (0, jnp.clip(i - s, 0, c - 1)))

    return pl.pallas_call(
        body, name=name, grid=(sum(counts),),
        in_specs=[spec(s, c) for s, c in zip(starts, counts)]
        + [pl.BlockSpec(b.shape, lambda i: (0, 0), pipeline_mode=pl.Buffered(1))],
        out_specs=pl.BlockSpec((tm, n), lambda i: (i, 0)),
        out_shape=jax.ShapeDtypeStruct((sum(counts) * tm, n), out_dtype),
        compiler_params=_cparams(("arbitrary",)),
    )(*pieces, b)


def _norm_in_proj(x, g, w_t, tm, tn):
    t, k = x.shape
    n = w_t.shape[0]
    tm, tn = _pick(t, tm), _pick(n, tn)

    def body(x_ref, g_ref, w_ref, xn_ref, z_ref):
        xn = _rms(x_ref[...], g_ref[...]).astype(BF)
        xn_ref[...] = xn
        z_ref[...] = _raw_dot(xn, w_ref[...], "nt")

    xns, z = pl.pallas_call(
        body, name="in_proj", grid=(n // tn, t // tm),
        in_specs=[pl.BlockSpec((tm, k), lambda j, i: (i, 0)), pl.BlockSpec((1, k), lambda j, i: (0, 0)),
                  pl.BlockSpec((tn, k), lambda j, i: (j, 0))],
        out_specs=[pl.BlockSpec((None, tm, k), lambda j, i: (j, i, 0)), pl.BlockSpec((tm, tn), lambda j, i: (i, j))],
        out_shape=[jax.ShapeDtypeStruct((n // tn, t, k), BF), jax.ShapeDtypeStruct((t, n), F32)],
        compiler_params=_cparams(("arbitrary", "arbitrary")),
    )(x, g, w_t)
    return xns[0], z


def _merge_out_post(z, o_a, o_b, w_a, w_b, w_out, x, g_post, g_pre2, tm):
    t = x.shape[0]
    tm = _pick(t, tm)
    w = 256
    npc = D // w
    ga0, gb0 = (IN_COLS - 2 * D) // w, (IN_COLS - D) // w

    def body(*refs):
        ga_refs, gb_refs = refs[:npc], refs[npc:2 * npc]
        oa_ref, ob_ref, wa_ref, wb_ref, w_ref, x_ref, gp_ref, g2_ref = refs[2 * npc:2 * npc + 8]
        ya_ref, yb_ref, m_ref, mix_ref, h_ref, xn_ref = refs[2 * npc + 8:]
        ya = _raw_dot(oa_ref[...], wa_ref[...], "nn").astype(BF)
        yb = _raw_dot(ob_ref[...], wb_ref[...], "nn").astype(BF)
        ya_ref[...] = ya
        yb_ref[...] = yb
        parts = []
        for p in range(npc):
            cols = slice(p * w, (p + 1) * w)
            parts.append(_sigmoid(ga_refs[p][...]) * ya[:, cols].astype(F32)
                         + _sigmoid(gb_refs[p][...]) * yb[:, cols].astype(F32))
        merged = jnp.concatenate(parts, axis=1).astype(BF)
        m_ref[...] = merged
        mix = _raw_dot(merged, w_ref[...], "nn")
        mix_ref[...] = mix
        h1 = x_ref[...] + _rms(mix, gp_ref[...])
        h_ref[...] = h1
        xn_ref[...] = _rms(h1, g2_ref[...]).astype(BF)

    row = pl.BlockSpec((tm, D), lambda i: (i, 0))
    one = pl.BlockSpec((1, D), lambda i: (0, 0))

    def gate(b0):
        return [pl.BlockSpec((tm, w), functools.partial(lambda i, b: (i, b), b=b0 + p)) for p in range(npc)]

    wgt = pl.BlockSpec((D, D), lambda i: (0, 0), pipeline_mode=pl.Buffered(1))
    return pl.pallas_call(
        body, name="merge_out_post", grid=(t // tm,),
        in_specs=gate(ga0) + gate(gb0) + [row, row, wgt, wgt, wgt, row, one, one],
        out_specs=[row] * 6,
        out_shape=[jax.ShapeDtypeStruct((t, D), BF), jax.ShapeDtypeStruct((t, D), BF), jax.ShapeDtypeStruct((t, D), BF),
                   jax.ShapeDtypeStruct((t, D), F32), jax.ShapeDtypeStruct((t, D), F32),
                   jax.ShapeDtypeStruct((t, D), BF)],
        compiler_params=_cparams(("arbitrary",)),
    )(*([z] * (2 * npc)), o_a, o_b, w_a, w_b, w_out, x, g_post, g_pre2)


def _accumulate(ni, refs, vals):
    @pl.when(ni == 0)
    def _():
        for r in refs:
            r[...] = jnp.zeros(r.shape, F32)

    for r, v in zip(refs, vals):
        r[...] += v


def _dmerged_merge_bwd(dmix, w_out, w_a, w_b, z, y_a, y_b, tm):
    t = dmix.shape[0]
    tm = _pick(t, tm)
    w = 256
    npc = D // w
    ga0, gb0 = (IN_COLS - 2 * D) // w, (IN_COLS - D) // w

    def body(*refs):
        dm_ref, w_ref, wa_ref, wb_ref = refs[:4]
        ga_refs, gb_refs = refs[4:4 + npc], refs[4 + npc:4 + 2 * npc]
        ya_ref, yb_ref, dga_ref, dgb_ref, dya_ref, dyb_ref, doa_ref, dob_ref = refs[4 + 2 * npc:]
        dmerged = _raw_dot(dm_ref[...], w_ref[...], "nt")
        dyas, dybs = [], []
        for p in range(npc):
            cols = slice(p * w, (p + 1) * w)
            xs = [ga_refs[p][...], gb_refs[p][...], ya_ref[:, cols].astype(F32), yb_ref[:, cols].astype(F32)]
            _, vjp = jax.vjp(lambda *a: _f_merge([], list(a), [])[0][0], *xs)
            dga, dgb, dya, dyb = vjp(dmerged[:, cols])
            dga_ref[:, cols] = dga.astype(BF)
            dgb_ref[:, cols] = dgb.astype(BF)
            dyas.append(dya.astype(BF))
            dybs.append(dyb.astype(BF))
        dya, dyb = jnp.concatenate(dyas, axis=1), jnp.concatenate(dybs, axis=1)
        dya_ref[...] = dya
        dyb_ref[...] = dyb
        doa_ref[...] = _raw_dot(dya, wa_ref[...], "nt").astype(BF)
        dob_ref[...] = _raw_dot(dyb, wb_ref[...], "nt").astype(BF)

    row = pl.BlockSpec((tm, D), lambda i: (i, 0))
    wgt = pl.BlockSpec((D, D), lambda i: (0, 0), pipeline_mode=pl.Buffered(1))

    def gate(b0):
        return [pl.BlockSpec((tm, w), functools.partial(lambda i, b: (i, b), b=b0 + p)) for p in range(npc)]

    return pl.pallas_call(
        body, name="merge_bwd", grid=(t // tm,),
        in_specs=[row, wgt, wgt, wgt] + gate(ga0) + gate(gb0) + [row, row],
        out_specs=[row] * 6, out_shape=[jax.ShapeDtypeStruct((t, D), BF)] * 6,
        compiler_params=_cparams(("arbitrary",)),
    )(dmix, w_out, w_a, w_b, *([z] * (2 * npc)), y_a, y_b)


def _dxn2_post1_bwd(pieces, w_up_t, x, mix, dh1, g_post, g_pre2, tm):
    t = x.shape[0]
    tm = _pick(t, tm)
    k = w_up_t.shape[0]
    offs = [sum(p.shape[1] for p in pieces[:i]) for i in range(len(pieces))]
    na = len(pieces)

    def body(*refs):
        w_ref, x_ref, m_ref, dh_ref, gp_ref, g2_ref, dgp_ref, dg2_ref, dx_ref, dm_ref = refs[na:]
        dxn2 = None
        for a_ref, off in zip(refs[:na], offs):
            part = _raw_dot(a_ref[...], w_ref[off:off + a_ref.shape[1], :], "nn")
            dxn2 = part if dxn2 is None else dxn2 + part
        _, vjp = jax.vjp(lambda gp, g2, xx, mm: _f_post1([gp, g2], [xx, mm], [])[0],
                         gp_ref[...], g2_ref[...], x_ref[...], m_ref[...])
        dgp, dg2, dx, dm = vjp([dh_ref[...], dxn2])
        _accumulate(pl.program_id(0), [dgp_ref, dg2_ref], [dgp, dg2])
        dx_ref[...] = dx
        dm_ref[...] = dm.astype(BF)

    row = pl.BlockSpec((tm, D), lambda i: (i, 0))
    one = pl.BlockSpec((1, D), lambda i: (0, 0))
    return pl.pallas_call(
        body, name="post1_bwd", grid=(t // tm,),
        in_specs=[pl.BlockSpec((tm, p.shape[1]), lambda i: (i, 0)) for p in pieces]
        + [pl.BlockSpec((k, D), lambda i: (0, 0), pipeline_mode=pl.Buffered(1)), row, row, row, one, one],
        out_specs=[one, one, row, row],
        out_shape=[jax.ShapeDtypeStruct((1, D), F32), jax.ShapeDtypeStruct((1, D), F32),
                   jax.ShapeDtypeStruct((t, D), F32), jax.ShapeDtypeStruct((t, D), BF)],
        compiler_params=_cparams(("arbitrary",)),
    )(*pieces, w_up_t, x, mix, dh1, g_post, g_pre2)


def _conv_taps(h, cw, cb, p2, p1):
    s1 = _shift_down(h, p1)
    s2 = _shift_down(s1, p2)
    return cb + _row(cw, 0) * s2 + _row(cw, 1) * s1 + _row(cw, 2) * h


def _shift_up(x, last):
    t = x.shape[0]
    r = lax.broadcasted_iota(jnp.int32, x.shape, 0)
    return jnp.where(r == t - 1, last, pltpu.roll(x, t - 1, 0))


def _conv_bwd(dff, w_down, hu_g, hu_v, cg, cv, conv_w, tm):
    t = hu_g.shape[0]
    tm = _pick(t, tm)
    tn = _pick(DFF, 1408)
    nj, n = DFF // tn, t // tm

    def body(dff_ref, w_ref, xg_ref, xv_ref, cg_ref, cv_ref, wg_ref, wv_ref,
             dwg_ref, dwv_ref, dbg_ref, dbv_ref, dxg_ref, dxv_ref, nxt):
        @pl.when(pl.program_id(1) == 0)
        def _():
            nxt[...] = jnp.zeros(nxt.shape, F32)
            for ref in (dwg_ref, dwv_ref, dbg_ref, dbv_ref):
                ref[...] = jnp.zeros(ref.shape, F32)

        dact = _raw_dot(dff_ref[...], w_ref[...], "nt")
        c_g, c_v = cg_ref[...], cv_ref[...]
        s = jax.nn.sigmoid(c_g)
        d_v = dact * (c_g * s)
        d_g = dact * c_v * (s * (1.0 + c_g * (1.0 - s)))
        halves = ((d_g, xg_ref, wg_ref, dwg_ref, dbg_ref, dxg_ref), (d_v, xv_ref, wv_ref, dwv_ref, dbv_ref, dxv_ref))
        for k, (dc, x_ref, cw_ref, dw_ref, db_ref, dx_ref) in enumerate(halves):
            after = nxt[k * SUBLANES:(k + 1) * SUBLANES]
            u1 = _shift_up(dc, _row(after, 0))
            u2 = _shift_up(u1, _row(after, 1))
            cw = cw_ref[...]
            dx_ref[...] = (_row(cw, 2) * dc + _row(cw, 1) * u1 + _row(cw, 0) * u2).astype(dx_ref.dtype)
            x = x_ref[...]
            dw_ref[...] += jnp.concatenate([jnp.sum(u * x, axis=0, keepdims=True) for u in (u2, u1, dc)], axis=0)
            db_ref[...] += jnp.sum(dc, axis=0, keepdims=True)
            nxt[k * SUBLANES:(k + 1) * SUBLANES] = dc[0:SUBLANES]

    def cols(rows, off):
        return pl.BlockSpec((rows, tn), lambda j, i: (0, j + off))

    tile = pl.BlockSpec((tm, tn), lambda j, i: (n - 1 - i, j))
    return pl.pallas_call(
        body, name="conv_bwd", grid=(nj, n),
        in_specs=[pl.BlockSpec((tm, D), lambda j, i: (n - 1 - i, 0)), pl.BlockSpec((tn, D), lambda j, i: (j, 0)),
                  tile, tile, tile, tile, cols(3, 0), cols(3, nj)],
        out_specs=[cols(3, 0), cols(3, 0), cols(1, 0), cols(1, 0), tile, tile],
        out_shape=[jax.ShapeDtypeStruct((3, DFF), F32), jax.ShapeDtypeStruct((3, DFF), F32),
                   jax.ShapeDtypeStruct((1, DFF), F32), jax.ShapeDtypeStruct((1, DFF), F32),
                   jax.ShapeDtypeStruct((t, DFF), BF), jax.ShapeDtypeStruct((t, DFF), BF)],
        scratch_shapes=[pltpu.VMEM((2 * SUBLANES, tn), F32)],
        compiler_params=_cparams(("arbitrary", "arbitrary")),
    )(dff, w_down, hu_g, hu_v, cg, cv, conv_w, conv_w)


def _up_conv(xn2, w_up_t, conv_w, conv_b, tm):
    t = xn2.shape[0]
    tm = _pick(t, tm)
    tn = _pick(DFF, 1408)
    nj = DFF // tn

    def body(x_ref, wg_ref, wv_ref, cwg_ref, cwv_ref, cbg_ref, cbv_ref, hg_ref, hv_ref, act_ref, c1_ref, c2_ref, prev):
        j, i = pl.program_id(0), pl.program_id(1)

        @pl.when(i == 0)
        def _():
            prev[...] = jnp.zeros(prev.shape, F32)

        x = x_ref[...]
        hg = _raw_dot(x, wg_ref[...], "nt")
        hv = _raw_dot(x, wv_ref[...], "nt")
        hg_ref[...] = hg
        hv_ref[...] = hv
        pg, pv = prev[0:SUBLANES], prev[SUBLANES:2 * SUBLANES]
        cg = _conv_taps(hg, cwg_ref[...], cbg_ref[...], _row(pg, SUBLANES - 2), _row(pg, SUBLANES - 1))
        cv = _conv_taps(hv, cwv_ref[...], cbv_ref[...], _row(pv, SUBLANES - 2), _row(pv, SUBLANES - 1))
        act_ref[...] = (_silu(cg) * cv).astype(BF)
        c1_ref[...] = cg
        c2_ref[...] = cv
        prev[0:SUBLANES] = hg[tm - SUBLANES:tm]
        prev[SUBLANES:2 * SUBLANES] = hv[tm - SUBLANES:tm]

    def cols(rows, off):
        return pl.BlockSpec((rows, tn), lambda j, i: (0, j + off))

    tile = pl.BlockSpec((tm, tn), lambda j, i: (i, j))
    return pl.pallas_call(
        body, name="up_conv", grid=(nj, t // tm),
        in_specs=[pl.BlockSpec((tm, D), lambda j, i: (i, 0)), pl.BlockSpec((tn, D), lambda j, i: (j, 0)),
                  pl.BlockSpec((tn, D), lambda j, i: (j + nj, 0)), cols(3, 0), cols(3, nj), cols(1, 0), cols(1, nj)],
        out_specs=[tile, tile, tile, tile, tile],
        out_shape=[jax.ShapeDtypeStruct((t, DFF), F32), jax.ShapeDtypeStruct((t, DFF), F32),
                   jax.ShapeDtypeStruct((t, DFF), BF), jax.ShapeDtypeStruct((t, DFF), F32),
                   jax.ShapeDtypeStruct((t, DFF), F32)],
        scratch_shapes=[pltpu.VMEM((2 * SUBLANES, tn), F32)],
        compiler_params=_cparams(("arbitrary", "arbitrary")),
    )(xn2, w_up_t, w_up_t, conv_w, conv_w, conv_b, conv_b)


def _dxn_pre1_bwd(pieces, w_t, x, dx_res, g, tm, token):
    t = x.shape[0]
    tm = _pick(t, tm)
    offs = [sum(p.shape[1] for p in pieces[:i]) for i in range(len(pieces))]
    na = len(pieces)

    def body(*refs):
        w_ref, x_ref, r_ref, g_ref = refs[na:na + 4]
        dg_ref, dx_ref = refs[-2:]
        dxn = None
        for a_ref, off in zip(refs[:na], offs):
            part = _raw_dot(a_ref[...], w_ref[off:off + a_ref.shape[1], :], "nn")
            dxn = part if dxn is None else dxn + part
        _, vjp = jax.vjp(lambda gg, xx: _f_pre1_residual([gg], [xx], [])[0], g_ref[...], x_ref[...])
        dg, dx = vjp([dxn, r_ref[...]])
        _accumulate(pl.program_id(0), [dg_ref], [dg])
        dx_ref[...] = dx

    row = pl.BlockSpec((tm, D), lambda i: (i, 0))
    one = pl.BlockSpec((1, D), lambda i: (0, 0))
    return pl.pallas_call(
        body, name="pre1_bwd", grid=(t // tm,),
        in_specs=[pl.BlockSpec((tm, p.shape[1]), lambda i: (i, 0)) for p in pieces]
        + [pl.BlockSpec(w_t.shape, lambda i: (0, 0), pipeline_mode=pl.Buffered(1)), row, row, one,
           pl.BlockSpec(token.shape, lambda i: (0, 0))],
        out_specs=[one, row],
        out_shape=[jax.ShapeDtypeStruct((1, D), F32), jax.ShapeDtypeStruct((t, D), F32)],
        compiler_params=_cparams(("arbitrary",)),
    )(*pieces, w_t, x, dx_res, g, token)


def _down_loss(act, w_down, g_post, h1, tgt, tm):
    t, k = act.shape
    tm = _pick(t, tm)

    def body(a_ref, w_ref, g_ref, h_ref, t_ref, loss_ref, dg_ref, dh_ref, df_ref):
        ni = pl.program_id(0)
        ff = _raw_dot(a_ref[...], w_ref[...], "nn")
        target = t_ref[...]

        def lossf(g, h1, ff):
            e = h1 + _rms(ff, g) - target
            return 0.5 * jnp.sum(jnp.mean(e * e, axis=-1))

        l, (dg, dh, df) = jax.value_and_grad(lossf, argnums=(0, 1, 2))(g_ref[...], h_ref[...], ff)

        @pl.when(ni == 0)
        def _():
            loss_ref[...] = jnp.zeros(loss_ref.shape, F32)
            dg_ref[...] = jnp.zeros(dg_ref.shape, F32)

        loss_ref[...] += jnp.full(loss_ref.shape, l, F32)
        dg_ref[...] += dg
        dh_ref[...] = dh
        df_ref[...] = df.astype(df_ref.dtype)

    row = pl.BlockSpec((tm, D), lambda ni: (ni, 0))
    one = pl.BlockSpec((1, D), lambda ni: (0, 0))
    return pl.pallas_call(
        body, name="down_loss", grid=(t // tm,),
        in_specs=[pl.BlockSpec((tm, k), lambda ni: (ni, 0)),
                  pl.BlockSpec((k, D), lambda ni: (0, 0), pipeline_mode=pl.Buffered(1)), one, row, row],
        out_specs=[pl.BlockSpec((1, LANES), lambda ni: (0, 0)), one, row, row],
        out_shape=[jax.ShapeDtypeStruct((1, LANES), F32), jax.ShapeDtypeStruct((1, D), F32),
                   jax.ShapeDtypeStruct((t, D), F32), jax.ShapeDtypeStruct((t, D), BF)],
        compiler_params=_cparams(("arbitrary",)),
    )(act, w_down, g_post, h1, tgt)


_ANY = pl.BlockSpec(memory_space=pl.ANY)


def _all_gather(name, blks):
    na = len(blks)
    ns = 8

    def body(*refs):
        x_refs, out_refs = refs[:na], refs[na:2 * na]
        send_sems, recv_sems, local_sems = refs[2 * na:]
        x, y, cc = lax.axis_index("x"), lax.axis_index("y"), lax.axis_index("c")
        sibling, xn, yn = (x, y, 1 - cc), (1 - x, y, cc), (x, 1 - y, cc)

        def num(px, py, pc):
            return 4 * px + 2 * py + pc

        def copy(a, k, to, src, dst):
            return pltpu.make_async_remote_copy(src_ref=src, dst_ref=dst, send_sem=send_sems.at[ns * a + k],
                                                recv_sem=recv_sems.at[ns * a + k], device_id=to, device_id_type=MESH)

        def halves(a, blk):
            h = blks[a].shape[0] // 2
            return out_refs[a].at[blk, pl.ds(0, h)], out_refs[a].at[blk, pl.ds(h, h)]

        mine, sends = [], []
        for a in range(na):
            o = out_refs[a]
            m = pltpu.make_async_copy(x_refs[a], o.at[num(x, y, cc)], local_sems.at[a])
            m.start()
            mine.append(m)
            own = o.at[num(x, y, cc)]
            sends.append([copy(a, 0, sibling, x_refs[a], own), copy(a, 1, xn, x_refs[a], own),
                          copy(a, 2, yn, x_refs[a], own)])
            for cp in sends[a]:
                cp.start()
        for a in range(na):
            o = out_refs[a]
            bx, by, bd = num(1 - x, y, cc), num(x, 1 - y, cc), num(1 - x, 1 - y, cc)
            copy(a, 1, xn, o.at[bx], o.at[bx]).wait_recv()
            more = [copy(a, 3, yn, halves(a, bx)[0], halves(a, bx)[0]), copy(a, 5, sibling, o.at[bx], o.at[bx])]
            for cp in more:
                cp.start()
            sends[a] += more
        for a in range(na):
            o = out_refs[a]
            bx, by, bd = num(1 - x, y, cc), num(x, 1 - y, cc), num(1 - x, 1 - y, cc)
            copy(a, 2, yn, o.at[by], o.at[by]).wait_recv()
            more = [copy(a, 4, xn, halves(a, by)[1], halves(a, by)[1]), copy(a, 6, sibling, o.at[by], o.at[by])]
            for cp in more:
                cp.start()
            sends[a] += more
        for a in range(na):
            o = out_refs[a]
            bd = num(1 - x, 1 - y, cc)
            copy(a, 3, yn, halves(a, bd)[0], halves(a, bd)[0]).wait_recv()
            copy(a, 4, xn, halves(a, bd)[1], halves(a, bd)[1]).wait_recv()
            fw = copy(a, 7, sibling, o.at[bd], o.at[bd])
            fw.start()
            sends[a].append(fw)
        for a in range(na):
            o = out_refs[a]
            for k, blk in ((0, num(x, y, 1 - cc)), (5, num(1 - x, y, 1 - cc)), (6, num(x, 1 - y, 1 - cc)),
                           (7, num(1 - x, 1 - y, 1 - cc))):
                copy(a, k, sibling, o.at[blk], o.at[blk]).wait_recv()
            for cp in sends[a]:
                cp.wait_send()
        for m in mine:
            m.wait()

    res = pl.pallas_call(
        body, name=name, in_specs=[_ANY] * na, out_specs=[_ANY] * na,
        out_shape=[jax.ShapeDtypeStruct((N_DEV,) + b.shape, b.dtype) for b in blks],
        scratch_shapes=[pltpu.SemaphoreType.DMA((ns * na,)), pltpu.SemaphoreType.DMA((ns * na,)),
                        pltpu.SemaphoreType.DMA((na,))],
    )(*blks)
    return list(res)


def _all_gather_small(name, blk):
    def body(x_ref, out_ref, ssem, rsem, lsem):
        x, y, c = lax.axis_index("x"), lax.axis_index("y"), lax.axis_index("c")
        me = 4 * x + 2 * y + c
        mine = pltpu.make_async_copy(x_ref, out_ref.at[me], lsem)
        mine.start()
        cps = []
        for j in range(1, N_DEV):
            px = 1 - x if j & 4 else x
            py = 1 - y if j & 2 else y
            pc = 1 - c if j & 1 else c
            cps.append(pltpu.make_async_remote_copy(src_ref=x_ref, dst_ref=out_ref.at[me], send_sem=ssem.at[j - 1],
                                                    recv_sem=rsem.at[j - 1], device_id=(px, py, pc),
                                                    device_id_type=MESH))
        for cp in cps:
            cp.start()
        for cp in cps:
            cp.wait()
        mine.wait()

    return pl.pallas_call(
        body, name=name, in_specs=[_ANY], out_specs=_ANY,
        out_shape=jax.ShapeDtypeStruct((N_DEV,) + blk.shape, blk.dtype),
        scratch_shapes=[pltpu.SemaphoreType.DMA((N_DEV - 1,)), pltpu.SemaphoreType.DMA((N_DEV - 1,)),
                        pltpu.SemaphoreType.DMA],
    )(blk)


def _reduce_pair(g8s):
    na = len(g8s)

    def body(*refs):
        g_refs, recv_refs = refs[:na], refs[na:2 * na]
        ssem, rsem = refs[2 * na:]
        x, y, cc = lax.axis_index("x"), lax.axis_index("y"), lax.axis_index("c")
        chips = [(x, y), (1 - x, y), (x, 1 - y), (1 - x, 1 - y)]
        sib = (x, y, 1 - cc)
        for a in range(na):
            for k, (cx, cy) in enumerate(chips):
                pltpu.make_async_remote_copy(
                    src_ref=g_refs[a].at[4 * cx + 2 * cy + 1 - cc], dst_ref=recv_refs[a].at[k],
                    send_sem=ssem.at[a], recv_sem=rsem.at[a], device_id=sib, device_id_type=MESH).start()
        for a in range(na):
            pltpu.make_async_remote_copy(src_ref=recv_refs[a], dst_ref=recv_refs[a], send_sem=ssem.at[a],
                                         recv_sem=rsem.at[a], device_id=sib, device_id_type=MESH).wait()

    res = pl.pallas_call(
        body, name="reduce_pair", in_specs=[_ANY] * na, out_specs=[_ANY] * na,
        out_shape=[jax.ShapeDtypeStruct((4,) + g.shape[1:], g.dtype) for g in g8s],
        scratch_shapes=[pltpu.SemaphoreType.DMA((na,)), pltpu.SemaphoreType.DMA((na,))],
    )(*g8s)
    return list(res)


_HBM = pl.BlockSpec(memory_space=pltpu.HBM)
_SEM = pl.BlockSpec(memory_space=pltpu.SEMAPHORE)
_EFFECT = pltpu.SideEffectType.DATAFLOW_SIDE_EFFECTING


def _chip_swap_copies(s_refs, land_refs, ssem, rsem):
    x, y, c = lax.axis_index("x"), lax.axis_index("y"), lax.axis_index("c")
    targets = [(1 - x, y, c), (x, 1 - y, c), (1 - x, 1 - y, c)]
    return [pltpu.make_async_remote_copy(src_ref=s.at[k], dst_ref=d.at[k], send_sem=ssem.at[3 * a + k],
                                         recv_sem=rsem.at[3 * a + k], device_id=targets[k], device_id_type=MESH)
            for a, (s, d) in enumerate(zip(s_refs, land_refs)) for k in range(3)]


def _chip_swap_start(sends):
    na = len(sends)

    def body(*refs):
        cps = _chip_swap_copies(refs[:na], refs[na:2 * na], refs[2 * na], refs[2 * na + 1])
        for cp in cps:
            cp.start()
        token = refs[-1]
        token[...] = jnp.zeros(token.shape, token.dtype)

    bufs = [pltpu.HBM(s.shape, s.dtype) for s in sends]
    res = pl.pallas_call(
        body, name="chip_swap_start",
        out_shape=[pltpu.SemaphoreType.DMA((3 * na,)), pltpu.SemaphoreType.DMA((3 * na,))] + bufs + bufs
        + [jax.ShapeDtypeStruct((8, LANES), F32)],
        in_specs=[_HBM] * (2 * na), out_specs=[_SEM, _SEM] + [_HBM] * (2 * na) + [pl.BlockSpec(memory_space=pltpu.VMEM)],
        input_output_aliases={i: 2 + i for i in range(2 * na)},
        compiler_params=pltpu.CompilerParams(has_side_effects=_EFFECT),
    )(*[pltpu.with_memory_space_constraint(s, pltpu.HBM) for s in sends],
      *[pltpu.with_memory_space_constraint(lax.empty(s.shape, s.dtype), pltpu.HBM) for s in sends])
    return res[0], res[1], list(res[2:2 + na]), list(res[2 + na:2 + 2 * na]), res[-1]


def _chip_swap_wait(ssem, rsem, srcs, lands, after):
    na = len(srcs)

    def body(*refs):
        cps = _chip_swap_copies(refs[:na], refs[na:2 * na], refs[2 * na], refs[2 * na + 1])
        for cp in cps:
            cp.wait_send()
            cp.wait_recv()

    bufs = [pltpu.HBM(s.shape, s.dtype) for s in srcs]
    res = pl.pallas_call(
        body, name="chip_swap_wait", out_shape=bufs + bufs,
        in_specs=[_HBM] * (2 * na) + [_SEM, _SEM, _ANY], out_specs=[_HBM] * (2 * na),
        input_output_aliases={i: i for i in range(2 * na)},
        compiler_params=pltpu.CompilerParams(has_side_effects=_EFFECT),
    )(*srcs, *lands, ssem, rsem, after)
    return list(res[na:])


def _pick_rows(r, c, budget=TILE_BYTES):
    if r * c * 4 <= budget or r % 16:
        return r
    best = 16
    for tr in range(16, r, 16):
        if r % tr == 0 and tr * c * 4 <= budget:
            best = tr
    return best


def _pair_sum(name, idx4, g8, recv4):
    _, r, c = g8.shape
    tr = _pick_rows(r, c, 2 * TILE_BYTES)

    def body(idx_ref, a_ref, b_ref, o0_ref, o3_ref):
        k = pl.program_id(1)
        s = a_ref[...].astype(F32) + b_ref[...].astype(F32)

        @pl.when(k == 0)
        def _():
            o0_ref[...] = s

        @pl.when(k > 0)
        def _():
            o3_ref[...] = s.astype(BF)

    spec = pltpu.PrefetchScalarGridSpec(
        num_scalar_prefetch=1, grid=(r // tr, 4),
        in_specs=[pl.BlockSpec((None, tr, c), lambda i, k, idx: (idx[k], i, 0)),
                  pl.BlockSpec((None, tr, c), lambda i, k, idx: (k, i, 0))],
        out_specs=[pl.BlockSpec((tr, c), lambda i, k, idx: (i, 0)),
                   pl.BlockSpec((None, tr, c), lambda i, k, idx: (jnp.maximum(k - 1, 0), i, 0))])
    return pl.pallas_call(
        body, name=name, grid_spec=spec,
        out_shape=[jax.ShapeDtypeStruct((r, c), F32), jax.ShapeDtypeStruct((3, r, c), BF)],
        compiler_params=_cparams(("arbitrary", "arbitrary")),
    )(idx4, g8, recv4)


def _adamw(w, g, m, v):
    m = ADAM_B1 * m + (1.0 - ADAM_B1) * g
    v = ADAM_B2 * v + (1.0 - ADAM_B2) * jnp.square(g)
    m_hat = m / (1.0 - ADAM_B1 ** ADAM_STEP)
    v_hat = v / (1.0 - ADAM_B2 ** ADAM_STEP)
    delta = -ADAM_LR * (m_hat / (jnp.sqrt(v_hat) + ADAM_EPS) + ADAM_WD * w)
    return delta, m, v


def _adam_sharded(name, idx1, own, recv, w, m, v, after=None):
    r, c = w.shape
    tr = _pick_rows(r, c, 2 * TILE_BYTES)
    nj = recv.shape[0]
    extra = [] if after is None else [after]

    def body(idx_ref, p_ref, r_ref, w_ref, m_ref, v_ref, *rest):
        g_out, d_out, m_out, v_out = rest[-4:]
        g = p_ref[...].astype(F32)
        for k in range(nj):
            g = g + r_ref[k].astype(F32)
        d, mn, vn = _adamw(w_ref[...], g, m_ref[...], v_ref[...])
        g_out[...] = g
        d_out[...] = d
        m_out[...] = mn
        v_out[...] = vn

    row = pl.BlockSpec((tr, c), lambda i, idx: (i, 0))
    spec = pltpu.PrefetchScalarGridSpec(
        num_scalar_prefetch=1, grid=(r // tr,),
        in_specs=[pl.BlockSpec((None, tr, c), lambda i, idx: (idx[0], i, 0)),
                  pl.BlockSpec((nj, tr, c), lambda i, idx: (0, i, 0)), row, row, row]
        + [pl.BlockSpec(e.shape, lambda i, idx: (0, 0)) for e in extra],
        out_specs=[row] * 4)
    return pl.pallas_call(
        body, name=name, grid_spec=spec, out_shape=[jax.ShapeDtypeStruct((r, c), F32)] * 4,
        compiler_params=_cparams(("arbitrary",)),
    )(idx1, own, recv, w, m, v, *extra)


def _repl_rows():
    rows, r = {}, 0
    for name, cols in REPL:
        rows[name] = r
        r += REPL_ROWS.get(name, 1) * ((cols + D - 1) // D)
    return rows


LOSS_ROW = 24


def _pack_replicated(grads, loss_acc, after):
    rows = _repl_rows()
    names = [n for n, _ in REPL]

    def body(*refs):
        o_ref = refs[-1]
        o_ref[...] = jnp.zeros(o_ref.shape, F32)
        o_ref[LOSS_ROW:LOSS_ROW + 1, 0:LANES] = refs[len(names)][...]
        for name, ref in zip(names, refs[:len(names)]):
            r0 = rows[name]
            nr, nc = ref.shape
            if nc <= D:
                o_ref[r0:r0 + nr, 0:nc] = ref[...]
            else:
                for j in range((nc + D - 1) // D):
                    lo, hi = j * D, min(nc, (j + 1) * D)
                    o_ref[r0 + j:r0 + j + 1, 0:hi - lo] = ref[:, lo:hi]

    return pl.pallas_call(body, name="pack_replicated", out_shape=jax.ShapeDtypeStruct((REPL_TOTAL, D), F32),
                          in_specs=[pl.BlockSpec(memory_space=pltpu.VMEM)] * (len(names) + 1) + [_ANY] * len(after),
                          compiler_params=_cparams())(*[grads[n] for n in names], loss_acc, *after)


def _adam_replicated(g8, ws, ms, vs):
    rows = _repl_rows()
    names = [n for n, _ in REPL]
    np_ = len(names)

    def body(*refs):
        g_ref = refs[0]
        w_refs, m_refs, v_refs = refs[1:1 + np_], refs[1 + np_:1 + 2 * np_], refs[1 + 2 * np_:1 + 3 * np_]
        outs = refs[1 + 3 * np_:1 + 7 * np_]
        scr = refs[-1]
        g = g_ref[0]
        for k in range(1, N_DEV):
            g = g + g_ref[k]
        scr[...] = g
        refs[1 + 7 * np_][...] = scr[LOSS_ROW:LOSS_ROW + 1, 0:LANES]
        for i, name in enumerate(names):
            r0 = rows[name]
            nr, nc = w_refs[i].shape
            if nc <= D:
                gi = scr[r0:r0 + nr, 0:nc]
            else:
                parts = []
                for j in range((nc + D - 1) // D):
                    lo, hi = j * D, min(nc, (j + 1) * D)
                    parts.append(scr[r0 + j:r0 + j + 1, 0:hi - lo])
                gi = jnp.concatenate(parts, axis=1)
            d, mn, vn = _adamw(w_refs[i][...], gi, m_refs[i][...], v_refs[i][...])
            outs[i][...] = gi
            outs[np_ + i][...] = d
            outs[2 * np_ + i][...] = mn
            outs[3 * np_ + i][...] = vn

    shp = [jax.ShapeDtypeStruct(w.shape, F32) for w in ws]
    res = pl.pallas_call(body, name="adam_replicated", out_shape=shp * 4 + [jax.ShapeDtypeStruct((1, LANES), F32)],
                         scratch_shapes=[pltpu.VMEM((REPL_TOTAL, D), F32)], compiler_params=_cparams(),
                         )(g8, *ws, *ms, *vs)
    return [dict(zip(names, res[k * np_:(k + 1) * np_])) for k in range(4)], res[-1]


_WEIGHTS = ("attn_pre_norm", "w_in", "hgrn_lb", "hgrn_gnorm", "w_branch_a", "rwkv_mu", "rwkv_w0", "rwkv_w2",
            "rwkv_a0", "rwkv_a2", "rwkv_g2", "rwkv_k_k", "rwkv_k_a", "rwkv_r_k", "rwkv_ln_w", "rwkv_ln_b",
            "w_branch_b", "w_out", "attn_post_norm", "ffn_pre_norm", "w_up", "conv_w", "conv_b", "w_down",
            "ffn_post_norm")
_BIG = ("w_in", "w_up", "w_down", "w_branch_a", "w_branch_b", "w_out")


def _stages():
    one = [D]
    hw = HG_K * HG_PER_STEP
    rw = LANES * RW_PAIRS_PER_STEP
    return dict(
        mixers=_Stage("mixers", _f_mixers, 1, 2 * RW_CHUNK, [False] * 13, [[D] * 7 + [LANES, LANES]], [0],
                      [(hw, HG_K), (1, RW_COLS), (rw, LANES)], [one, one], [BF, BF],
                      kept_shapes=[(2 * RW_KEPT * RW_PAIRS_PER_STEP * 2 * RW_CHUNK, LANES)], f_kept=_f_mixers_kept),
        conv=_Stage("conv", _f_conv, 1, 512, [False, False], [[DFF], [DFF]], [0, 0], [(1, 2 * DFF), (1, 2 * DFF)],
                    [[DFF]], [BF]),
    )


def _cols_to_blocks(w, per):
    return w.reshape(w.shape[0], N_DEV, per).transpose(1, 0, 2)


def _blocks_to_cols(g):
    return g.transpose(1, 0, 2).reshape(g.shape[1], N_DEV * g.shape[2])


def kernel(x, attn_pre_norm, w_in, hgrn_lb, hgrn_gnorm, w_branch_a, rwkv_mu, rwkv_w0, rwkv_w2, rwkv_a0, rwkv_a2, rwkv_g2, rwkv_k_k, rwkv_k_a, rwkv_r_k, rwkv_ln_w, rwkv_ln_b, w_branch_b, w_out, attn_post_norm, ffn_pre_norm, w_up, conv_w, conv_b, w_down, ffn_post_norm, loss_target, m_attn_pre_norm, m_w_in, m_hgrn_lb, m_hgrn_gnorm, m_w_branch_a, m_rwkv_mu, m_rwkv_w0, m_rwkv_w2, m_rwkv_a0, m_rwkv_a2, m_rwkv_g2, m_rwkv_k_k, m_rwkv_k_a, m_rwkv_r_k, m_rwkv_ln_w, m_rwkv_ln_b, m_w_branch_b, m_w_out, m_attn_post_norm, m_ffn_pre_norm, m_w_up, m_conv_w, m_conv_b, m_w_down, m_ffn_post_norm, v_attn_pre_norm, v_w_in, v_hgrn_lb, v_hgrn_gnorm, v_w_branch_a, v_rwkv_mu, v_rwkv_w0, v_rwkv_w2, v_rwkv_a0, v_rwkv_a2, v_rwkv_g2, v_rwkv_k_k, v_rwkv_k_a, v_rwkv_r_k, v_rwkv_ln_w, v_rwkv_ln_b, v_w_branch_b, v_w_out, v_attn_post_norm, v_ffn_pre_norm, v_w_up, v_conv_w, v_conv_b, v_w_down, v_ffn_post_norm):
    w = dict(attn_pre_norm=attn_pre_norm, w_in=w_in, hgrn_lb=hgrn_lb, hgrn_gnorm=hgrn_gnorm, w_branch_a=w_branch_a, rwkv_mu=rwkv_mu, rwkv_w0=rwkv_w0, rwkv_w2=rwkv_w2, rwkv_a0=rwkv_a0, rwkv_a2=rwkv_a2, rwkv_g2=rwkv_g2, rwkv_k_k=rwkv_k_k, rwkv_k_a=rwkv_k_a, rwkv_r_k=rwkv_r_k, rwkv_ln_w=rwkv_ln_w, rwkv_ln_b=rwkv_ln_b, w_branch_b=w_branch_b, w_out=w_out, attn_post_norm=attn_post_norm, ffn_pre_norm=ffn_pre_norm, w_up=w_up, conv_w=conv_w, conv_b=conv_b, w_down=w_down, ffn_post_norm=ffn_post_norm)
    mo = dict(attn_pre_norm=m_attn_pre_norm, w_in=m_w_in, hgrn_lb=m_hgrn_lb, hgrn_gnorm=m_hgrn_gnorm, w_branch_a=m_w_branch_a, rwkv_mu=m_rwkv_mu, rwkv_w0=m_rwkv_w0, rwkv_w2=m_rwkv_w2, rwkv_a0=m_rwkv_a0, rwkv_a2=m_rwkv_a2, rwkv_g2=m_rwkv_g2, rwkv_k_k=m_rwkv_k_k, rwkv_k_a=m_rwkv_k_a, rwkv_r_k=m_rwkv_r_k, rwkv_ln_w=m_rwkv_ln_w, rwkv_ln_b=m_rwkv_ln_b, w_branch_b=m_w_branch_b, w_out=m_w_out, attn_post_norm=m_attn_post_norm, ffn_pre_norm=m_ffn_pre_norm, w_up=m_w_up, conv_w=m_conv_w, conv_b=m_conv_b, w_down=m_w_down, ffn_post_norm=m_ffn_post_norm)
    vo = dict(attn_pre_norm=v_attn_pre_norm, w_in=v_w_in, hgrn_lb=v_hgrn_lb, hgrn_gnorm=v_hgrn_gnorm, w_branch_a=v_w_branch_a, rwkv_mu=v_rwkv_mu, rwkv_w0=v_rwkv_w0, rwkv_w2=v_rwkv_w2, rwkv_a0=v_rwkv_a0, rwkv_a2=v_rwkv_a2, rwkv_g2=v_rwkv_g2, rwkv_k_k=v_rwkv_k_k, rwkv_k_a=v_rwkv_k_a, rwkv_r_k=v_rwkv_r_k, rwkv_ln_w=v_rwkv_ln_w, rwkv_ln_b=v_rwkv_ln_b, w_branch_b=v_w_branch_b, w_out=v_w_out, attn_post_norm=v_attn_post_norm, ffn_pre_norm=v_ffn_pre_norm, w_up=v_w_up, conv_w=v_conv_w, conv_b=v_conv_b, w_down=v_w_down, ffn_post_norm=v_ffn_post_norm)

    t = x.shape[1]
    x2 = x.reshape(t, D)
    tgt = loss_target.reshape(t, D)
    st = _stages()

    me = 4 * lax.axis_index("x") + 2 * lax.axis_index("y") + lax.axis_index("c")
    small = jnp.concatenate([rwkv_w2[0], rwkv_a2[0], rwkv_g2[0]], axis=0).astype(BF)
    g_in, g_small = _all_gather("gather_weights", [w_in[0].T.astype(BF), small])
    fw_in_t = g_in.reshape(IN_COLS, D)
    z64 = jnp.zeros((64, D), BF)
    w2p = jnp.concatenate([_blocks_to_cols(g_small[:, 0:64]), z64], axis=0)
    a2p = jnp.concatenate([z64, _blocks_to_cols(g_small[:, 64:128])], axis=0)
    g2f = _blocks_to_cols(g_small[:, 128:256])
    conv_bits = jnp.pad(lax.bitcast_convert_type(conv_w[0], BF).reshape(3, 2 * 704), ((0, 29), (0, 0)))
    late = [w_up[0].T.astype(BF)] + [w[k][0].astype(BF) for k in _BIG[2:]] + [conv_bits]
    late_gather = _Exchange("gather2", late)
    r_k = rwkv_r_k.reshape(1, D)

    xn, z = _norm_in_proj(x2, attn_pre_norm, fw_in_t, 512, 4736)
    mix_par = [hgrn_lb, hgrn_gnorm, rwkv_mu, rwkv_w0, w2p, rwkv_a0, a2p, g2f, rwkv_k_k, rwkv_k_a,
               rwkv_ln_w, rwkv_ln_b, r_k]
    mix_in = [z]
    (o_a, o_b), mix_saved = _stage_fwd(st["mixers"], t, mix_par, mix_in, hook=late_gather)
    gl = [lax.dynamic_update_slice(g, own[None], (me, 0, 0)) for g, own in zip(late_gather.results, late)]
    fw_up_t = gl[0].reshape(2 * DFF, D)
    fw_down = gl[1].reshape(DFF, D)
    fw_a, fw_b, fw_out = (g.reshape(D, D) for g in gl[2:5])
    conv_full = _blocks_to_cols(lax.bitcast_convert_type(gl[5][:, :3].reshape(N_DEV, 3, 704, 2), F32))
    y_a, y_b, merged, mix, h1, xn2 = _merge_out_post(z, o_a, o_b, fw_a, fw_b, fw_out, x2, attn_post_norm,
                                                     ffn_pre_norm, 512)
    hu_g, hu_v, act, conv_g, conv_v = _up_conv(xn2, fw_up_t, conv_full, conv_b, 512)

    loss_acc, d_ffn_post, dh1, dff = _down_loss(act, fw_down, ffn_post_norm, h1, tgt, 512)
    dw_down = _mm("dw_down", act, dff, "tn", BF, tm=1408, tn=512)
    dcw_g, dcw_v, dcb_g, dcb_v, dhu_g, dhu_v = _conv_bwd(dff, fw_down, hu_g, hu_v, conv_g, conv_v, conv_full, 256)
    dcw, dcb = jnp.concatenate([dcw_g, dcw_v], axis=1), jnp.concatenate([dcb_g, dcb_v], axis=1)
    dhu = [dhu_g, dhu_v]
    dw_up_t = _mm_cols_tn("dw_up", dhu, xn2, BF, 1408)
    d_post, d_pre2, dx_a, dmix = _dxn2_post1_bwd(dhu, fw_up_t, x2, mix, dh1, attn_post_norm, ffn_pre_norm, 512)
    dga, dgb, dy_a, dy_b, do_a, do_b = _dmerged_merge_bwd(dmix, fw_out, fw_a, fw_b, z, y_a, y_b, 512)
    dw_a, dw_b, dw_out = _mm_multi("dw_branches", [(o_a, dy_a), (o_b, dy_b), (merged, dmix)], "tn", BF)
    early = [dw_up_t.reshape(N_DEV, 704, D), dw_down.reshape(N_DEV, 352, D), dw_a.reshape(N_DEV, 128, D),
             dw_b.reshape(N_DEV, 128, D), dw_out.reshape(N_DEV, 128, D), _cols_to_blocks(dcw.astype(BF), 704)]
    early_scatter = _Exchange("scatter", early)
    mix_dp, dz_hr = _stage_bwd(st["mixers"], t, mix_par, mix_in, mix_saved, [[do_a], [do_b]], [BF],
                               hook=early_scatter)
    d_lb, d_gn, d_mu, d_w0, d_w2p, d_a0, d_a2p, d_g2, d_kk, d_ka, d_lnw, d_lnb, d_rk = mix_dp
    dz = dz_hr + [dga, dgb]
    dw_in_t = _mm_cols_tn("dw_in", dz, xn, BF, 256)

    ax, ay, ac = lax.axis_index("x"), lax.axis_index("y"), lax.axis_index("c")
    idx4 = jnp.stack([4 * cx + 2 * cy + ac for cx, cy in ((ax, ay), (1 - ax, ay), (ax, 1 - ay), (1 - ax, 1 - ay))])
    idx4 = idx4.astype(jnp.int32)
    idx_me, idx_0 = idx4[0:1], jnp.zeros((1,), jnp.int32)
    d_small = jnp.concatenate([d_w2p[:64], d_a2p[64:], d_g2], axis=0).astype(BF)
    g8s = [dw_in_t.reshape(N_DEV, 1184, D), _cols_to_blocks(d_small, LANES)]
    recv4s = _reduce_pair(g8s)
    sums = [_pair_sum("pair_sum_" + n, idx4, g, r) for n, g, r in zip(("w_in", "small"), g8s, recv4s)]
    swap_ssem, swap_rsem, swap_srcs, swap_lands, token = _chip_swap_start([s[1] for s in sums])
    d_pre1, dx = _dxn_pre1_bwd(dz, fw_in_t, x2, dx_a, attn_pre_norm, 256, token)
    grad_x = dx.reshape(x.shape)

    sh_out = [dict() for _ in range(4)]
    done = []
    for n, own, recv in zip(_BIG[1:] + ("conv_w",), early, early_scatter.results):
        tr = (lambda a: a.T) if n == "w_up" else (lambda a: a)
        res = _adam_sharded("adam_" + n, idx_me, own, recv, *[tr(src[n][0]) for src in (w, mo, vo)], after=token)
        done.append(res[0])
        for kind in range(4):
            sh_out[kind][n] = tr(res[kind])[None]

    rg = dict(attn_pre_norm=d_pre1, hgrn_lb=d_lb, hgrn_gnorm=d_gn, rwkv_mu=d_mu, rwkv_w0=d_w0, rwkv_a0=d_a0,
              rwkv_k_k=d_kk, rwkv_k_a=d_ka, rwkv_r_k=d_rk, rwkv_ln_w=d_lnw, rwkv_ln_b=d_lnb, attn_post_norm=d_post,
              ffn_pre_norm=d_pre2, conv_b=dcb, ffn_post_norm=d_ffn_post)
    g8 = _all_gather_small("gather_small_grads", _pack_replicated(rg, loss_acc, done))
    rnames = [n for n, _ in REPL]
    flat = lambda src: [src[n].reshape(1, D) if n == "rwkv_r_k" else src[n] for n in rnames]
    rp_out, loss_row = _adam_replicated(g8, flat(w), flat(mo), flat(vo))
    loss = loss_row[0, 0]
    recv3s = _chip_swap_wait(swap_ssem, swap_rsem, swap_srcs, swap_lands, rp_out[0]["attn_pre_norm"])
    for kind in range(4):
        rp_out[kind]["rwkv_r_k"] = rp_out[kind]["rwkv_r_k"].reshape(rwkv_r_k.shape)

    def small_of(src):
        return jnp.concatenate([src["rwkv_w2"][0], src["rwkv_a2"][0], src["rwkv_g2"][0]], axis=0)

    res = _adam_sharded("adam_w_in", idx_0, sums[0][0][None], recv3s[0], *[src["w_in"][0].T for src in (w, mo, vo)])
    res_s = _adam_sharded("adam_small", idx_0, sums[1][0][None], recv3s[1], *[small_of(src) for src in (w, mo, vo)])
    for kind in range(4):
        sh_out[kind]["w_in"] = res[kind].T[None]
        sh_out[kind]["rwkv_w2"] = res_s[kind][0:64][None]
        sh_out[kind]["rwkv_a2"] = res_s[kind][64:128][None]
        sh_out[kind]["rwkv_g2"] = res_s[kind][128:256][None]

    outs = [loss, grad_x]
    for kind in range(4):
        for name in _WEIGHTS:
            outs.append(sh_out[kind][name] if name in sh_out[kind] else rp_out[kind][name])
    return tuple(outs)
```
